```python
import jax, jax.numpy as jnp
from jax import lax
import numpy as np

D_MODEL = 1024
BATCH = 8
SEQ = 8192
DEPTH = 1

D_MIX = D_MODEL
RG_WIDTH = D_MIX // 2
RG_BLOCKS = 8
RG_BLOCK = RG_WIDTH // RG_BLOCKS
RG_C = 8.0
CONV_W = 4
GDN_HEADS = 4
GDN_DK = 128
GDN_DV = 128
GDN_QK = GDN_HEADS * GDN_DK
GDN_VW = GDN_HEADS * GDN_DV
CHUNK = 64
D_FF = 2816
N_DIR = 2
EPS = 1e-6

OFF_RG_X = 0
OFF_RG_G = OFF_RG_X + RG_WIDTH
OFF_QKV = OFF_RG_G + RG_WIDTH
OFF_Z = OFF_QKV + 2 * GDN_QK + GDN_VW
OFF_BETA = OFF_Z + GDN_VW
OFF_ALPHA = OFF_BETA + N_DIR * GDN_HEADS
D_IN_PROJ = OFF_ALPHA + N_DIR * GDN_HEADS

kernel_name = "hymba_style_rglru_gdn_macaron_encoder"


def rmsnorm(x, g):
    xf = x.astype(jnp.float32)
    y = xf * lax.rsqrt(jnp.mean(xf * xf, axis=-1, keepdims=True) + EPS)
    return (y * g.astype(jnp.float32)).astype(x.dtype)


def l2norm(t):
    return t * lax.rsqrt(jnp.sum(t * t, axis=-1, keepdims=True) + EPS)


def centred_dwconv(x, w):
    left = CONV_W // 2
    return lax.conv_general_dilated(
        x, w[:, None, :].astype(x.dtype), window_strides=(1,),
        padding=[(left, CONV_W - 1 - left)],
        dimension_numbers=("NWC", "WIO", "NWC"),
        feature_group_count=x.shape[-1])


def swiglu_ffn(x, g, w_gate, w_up, w_down):
    h = rmsnorm(x, g)
    return (jax.nn.silu(h @ w_gate) * (h @ w_up)) @ w_down


def linear_scan(a, b, reverse):
    def combine(l, r):
        return (l[0] * r[0], r[0] * l[1] + r[1])
    _, h = lax.associative_scan(combine, (a, b), reverse=reverse, axis=1)
    return h


def rg_lru_bidir(xc, wa, ba, wx, bx, lam):
    B, S, _ = xc.shape
    xb = xc.reshape(B, S, RG_BLOCKS, RG_BLOCK)
    r = jax.nn.sigmoid(jnp.einsum("bsni,dnij->dbsnj", xb, wa.astype(jnp.float32)).reshape(N_DIR, B, S, RG_WIDTH)
                       + ba.astype(jnp.float32)[:, None, None, :])
    i = jax.nn.sigmoid(jnp.einsum("bsni,dnij->dbsnj", xb, wx.astype(jnp.float32)).reshape(N_DIR, B, S, RG_WIDTH)
                       + bx.astype(jnp.float32)[:, None, None, :])
    log_a = -RG_C * r * jax.nn.softplus(-lam.astype(jnp.float32))[:, None, None, :]
    a = jnp.exp(log_a)
    b = jnp.sqrt(-jnp.expm1(2.0 * log_a)) * (i * xc[None])
    h_f = linear_scan(a[0], b[0], reverse=False)
    h_b = linear_scan(a[1], b[1], reverse=True)
    return h_f + h_b


def gdn_chunked(q, k, v, beta, g):
    B, S, H, DK = q.shape
    DV = v.shape[-1]
    N = S // CHUNK

    def chunks(t):
        t = t.reshape((B, N, CHUNK, H) + t.shape[3:])
        return jnp.moveaxis(t, 3, 1)

    q, k, v, beta, g = chunks(q), chunks(k), chunks(v), chunks(beta), chunks(g)
    g_cum = jnp.cumsum(g, axis=-1)
    idx = jnp.arange(CHUNK)
    incl = idx[:, None] >= idx[None, :]
    strict = idx[:, None] > idx[None, :]
    decay = jnp.exp(jnp.where(incl, g_cum[..., :, None] - g_cum[..., None, :], -jnp.inf))
    k_beta = k * beta[..., None]
    v_beta = v * beta[..., None]
    L = jnp.where(strict, jnp.einsum("bhnik,bhnjk->bhnij", k_beta, k) * decay, 0.0)
    eye = jnp.broadcast_to(jnp.eye(CHUNK, dtype=q.dtype), L.shape)
    T = lax.linalg.triangular_solve(L, eye, left_side=True, lower=True, unit_diagonal=True)
    u = jnp.einsum("bhnij,bhnjv->bhniv", T, v_beta)
    w = jnp.einsum("bhnij,bhnjk->bhnik", T, k_beta * jnp.exp(g_cum)[..., None])
    attn = jnp.einsum("bhnik,bhnjk->bhnij", q, k) * decay
    g_last = g_cum[..., -1:]
    q_dec = q * jnp.exp(g_cum)[..., None]
    k_dec = k * jnp.exp(g_last - g_cum)[..., None]
    c_decay = jnp.exp(g_last[..., 0])

    xs = tuple(jnp.moveaxis(t, 2, 0) for t in (w, u, q_dec, k_dec, attn, c_decay))

    def step(state, inp):
        w_n, u_n, qd_n, kd_n, a_n, cd_n = inp
        v_new = u_n - jnp.einsum("bhck,bhkv->bhcv", w_n, state)
        o_n = jnp.einsum("bhck,bhkv->bhcv", qd_n, state) + jnp.einsum("bhij,bhjv->bhiv", a_n, v_new)
        state = state * cd_n[..., None, None] + jnp.einsum("bhck,bhcv->bhkv", kd_n, v_new)
        return state, o_n

    s0 = jnp.zeros((B, H, DK, DV), q.dtype)
    _, o = lax.scan(step, s0, xs)
    return jnp.transpose(o, (1, 0, 3, 2, 4)).reshape(B, S, H, DV)


def hybrid_mixer(h, w_in, w_out, rg_conv_w, rg_conv_b, rg_gate_a_w, rg_gate_a_b,
                 rg_gate_x_w, rg_gate_x_b, rg_lambda, gdn_conv_w, gdn_a_log, gdn_dt_bias, gdn_norm):
    B, S, _ = h.shape
    f32 = jnp.float32
    p = h @ w_in
    x_rg = p[..., OFF_RG_X:OFF_RG_G]
    gate_rg = p[..., OFF_RG_G:OFF_QKV]
    qkv = p[..., OFF_QKV:OFF_Z]
    z = p[..., OFF_Z:OFF_BETA]
    beta_raw = p[..., OFF_BETA:OFF_ALPHA].reshape(B, S, N_DIR, GDN_HEADS)
    alpha_raw = p[..., OFF_ALPHA:D_IN_PROJ].reshape(B, S, N_DIR, GDN_HEADS)

    xc = (centred_dwconv(x_rg, rg_conv_w) + rg_conv_b).astype(f32)
    hr = rg_lru_bidir(xc, rg_gate_a_w, rg_gate_a_b, rg_gate_x_w, rg_gate_x_b, rg_lambda)
    y_rg = (hr * jax.nn.gelu(gate_rg.astype(f32))).astype(h.dtype)

    qkv = jax.nn.silu(centred_dwconv(qkv, gdn_conv_w)).astype(f32)
    q = l2norm(qkv[..., :GDN_QK].reshape(B, S, GDN_HEADS, GDN_DK)) * (GDN_DK ** -0.5)
    k = l2norm(qkv[..., GDN_QK:2 * GDN_QK].reshape(B, S, GDN_HEADS, GDN_DK))
    v = qkv[..., 2 * GDN_QK:].reshape(B, S, GDN_HEADS, GDN_DV)
    beta = jax.nn.sigmoid(beta_raw.astype(f32))
    g = -jnp.exp(gdn_a_log.astype(f32)) * jax.nn.softplus(alpha_raw.astype(f32) + gdn_dt_bias.astype(f32))
    flip = lambda t: jnp.flip(t, axis=1)
    o_f = gdn_chunked(q, k, v, beta[:, :, 0], g[:, :, 0])
    o_b = flip(gdn_chunked(flip(q), flip(k), flip(v), flip(beta[:, :, 1]), flip(g[:, :, 1])))
    o = rmsnorm(o_f + o_b, gdn_norm) * jax.nn.silu(z.astype(f32).reshape(B, S, GDN_HEADS, GDN_DV))
    y_gdn = o.reshape(B, S, GDN_VW).astype(h.dtype)

    return jnp.concatenate([y_rg, y_gdn], axis=-1) @ w_out


def _fwd_setup_inputs(seed: int = 0) -> dict:
    key = jax.random.key(seed)
    ks = iter(jax.random.split(key, 32))
    nrm = lambda shape, scale: jax.random.normal(next(ks), shape, jnp.float32) * scale
    gain = lambda shape: 1.0 + nrm(shape, 0.01)
    L = DEPTH
    a_c = jax.random.uniform(next(ks), (L, N_DIR, RG_WIDTH), jnp.float32, 0.9, 0.999)
    s = a_c ** (1.0 / RG_C)
    rg_lambda = jnp.log(s) - jnp.log1p(-s)
    gdn_a_log = jnp.log(jax.random.uniform(next(ks), (L, N_DIR, GDN_HEADS), jnp.float32, 1.0, 16.0))
    dt = jnp.exp(jax.random.uniform(next(ks), (L, N_DIR, GDN_HEADS), jnp.float32, np.log(1e-3), np.log(1e-1)))
    gdn_dt_bias = dt + jnp.log(-jnp.expm1(-dt))
    return {
        "x": nrm((BATCH, SEQ, D_MODEL), 1.0),
        "ffn1_norm": gain((L, D_MODEL)),
        "ffn1_w_gate": nrm((L, D_MODEL, D_FF), D_MODEL ** -0.5),
        "ffn1_w_up": nrm((L, D_MODEL, D_FF), D_MODEL ** -0.5),
        "ffn1_w_down": nrm((L, D_FF, D_MODEL), D_FF ** -0.5),
        "mix_norm": gain((L, D_MODEL)),
        "w_in": nrm((L, D_MODEL, D_IN_PROJ), D_MODEL ** -0.5),
        "w_out": nrm((L, D_MIX, D_MODEL), D_MIX ** -0.5),
        "rg_conv_w": nrm((L, CONV_W, RG_WIDTH), CONV_W ** -0.5),
        "rg_conv_b": nrm((L, RG_WIDTH), 0.01),
        "rg_gate_a_w": nrm((L, N_DIR, RG_BLOCKS, RG_BLOCK, RG_BLOCK), RG_BLOCK ** -0.5),
        "rg_gate_a_b": nrm((L, N_DIR, RG_WIDTH), 0.01),
        "rg_gate_x_w": nrm((L, N_DIR, RG_BLOCKS, RG_BLOCK, RG_BLOCK), RG_BLOCK ** -0.5),
        "rg_gate_x_b": nrm((L, N_DIR, RG_WIDTH), 0.01),
        "rg_lambda": rg_lambda,
        "gdn_conv_w": nrm((L, CONV_W, 2 * GDN_QK + GDN_VW), CONV_W ** -0.5),
        "gdn_a_log": gdn_a_log,
        "gdn_dt_bias": gdn_dt_bias,
        "gdn_norm": gain((L, GDN_DV)),
        "ffn2_norm": gain((L, D_MODEL)),
        "ffn2_w_gate": nrm((L, D_MODEL, D_FF), D_MODEL ** -0.5),
        "ffn2_w_up": nrm((L, D_MODEL, D_FF), D_MODEL ** -0.5),
        "ffn2_w_down": nrm((L, D_FF, D_MODEL), D_FF ** -0.5),
        "final_norm": gain((D_MODEL,)),
    }


def _fwd_reference(x, ffn1_norm, ffn1_w_gate, ffn1_w_up, ffn1_w_down, mix_norm, w_in, w_out,
              rg_conv_w, rg_conv_b, rg_gate_a_w, rg_gate_a_b, rg_gate_x_w, rg_gate_x_b, rg_lambda,
              gdn_conv_w, gdn_a_log, gdn_dt_bias, gdn_norm,
              ffn2_norm, ffn2_w_gate, ffn2_w_up, ffn2_w_down, final_norm):
    for l in range(DEPTH):
        x = x + 0.5 * swiglu_ffn(x, ffn1_norm[l], ffn1_w_gate[l], ffn1_w_up[l], ffn1_w_down[l])
        x = x + hybrid_mixer(rmsnorm(x, mix_norm[l]), w_in[l], w_out[l],
                             rg_conv_w[l], rg_conv_b[l], rg_gate_a_w[l], rg_gate_a_b[l],
                             rg_gate_x_w[l], rg_gate_x_b[l], rg_lambda[l],
                             gdn_conv_w[l], gdn_a_log[l], gdn_dt_bias[l], gdn_norm[l])
        x = x + 0.5 * swiglu_ffn(x, ffn2_norm[l], ffn2_w_gate[l], ffn2_w_up[l], ffn2_w_down[l])
    return rmsnorm(x, final_norm)


import jax as _jax
import jax.numpy as _jnp

TWIN_FORMAT = 'train_step'
FWD_PARAMS = ['x', 'ffn1_norm', 'ffn1_w_gate', 'ffn1_w_up', 'ffn1_w_down', 'mix_norm', 'w_in', 'w_out', 'rg_conv_w', 'rg_conv_b', 'rg_gate_a_w', 'rg_gate_a_b', 'rg_gate_x_w', 'rg_gate_x_b', 'rg_lambda', 'gdn_conv_w', 'gdn_a_log', 'gdn_dt_bias', 'gdn_norm', 'ffn2_norm', 'ffn2_w_gate', 'ffn2_w_up', 'ffn2_w_down', 'final_norm']
TWIN_WEIGHTS = ['ffn1_norm', 'ffn1_w_gate', 'ffn1_w_up', 'ffn1_w_down', 'mix_norm', 'w_in', 'w_out', 'rg_conv_w', 'rg_conv_b', 'rg_gate_a_w', 'rg_gate_a_b', 'rg_gate_x_w', 'rg_gate_x_b', 'rg_lambda', 'gdn_conv_w', 'gdn_a_log', 'gdn_dt_bias', 'gdn_norm', 'ffn2_norm', 'ffn2_w_gate', 'ffn2_w_up', 'ffn2_w_down', 'final_norm']
TWIN_DIFF_INPUT = 'x'
TWIN_INPUTS = ['x', 'ffn1_norm', 'ffn1_w_gate', 'ffn1_w_up', 'ffn1_w_down', 'mix_norm', 'w_in', 'w_out', 'rg_conv_w', 'rg_conv_b', 'rg_gate_a_w', 'rg_gate_a_b', 'rg_gate_x_w', 'rg_gate_x_b', 'rg_lambda', 'gdn_conv_w', 'gdn_a_log', 'gdn_dt_bias', 'gdn_norm', 'ffn2_norm', 'ffn2_w_gate', 'ffn2_w_up', 'ffn2_w_down', 'final_norm', 'loss_target', 'm_ffn1_norm', 'm_ffn1_w_gate', 'm_ffn1_w_up', 'm_ffn1_w_down', 'm_mix_norm', 'm_w_in', 'm_w_out', 'm_rg_conv_w', 'm_rg_conv_b', 'm_rg_gate_a_w', 'm_rg_gate_a_b', 'm_rg_gate_x_w', 'm_rg_gate_x_b', 'm_rg_lambda', 'm_gdn_conv_w', 'm_gdn_a_log', 'm_gdn_dt_bias', 'm_gdn_norm', 'm_ffn2_norm', 'm_ffn2_w_gate', 'm_ffn2_w_up', 'm_ffn2_w_down', 'm_final_norm', 'v_ffn1_norm', 'v_ffn1_w_gate', 'v_ffn1_w_up', 'v_ffn1_w_down', 'v_mix_norm', 'v_w_in', 'v_w_out', 'v_rg_conv_w', 'v_rg_conv_b', 'v_rg_gate_a_w', 'v_rg_gate_a_b', 'v_rg_gate_x_w', 'v_rg_gate_x_b', 'v_rg_lambda', 'v_gdn_conv_w', 'v_gdn_a_log', 'v_gdn_dt_bias', 'v_gdn_norm', 'v_ffn2_norm', 'v_ffn2_w_gate', 'v_ffn2_w_up', 'v_ffn2_w_down', 'v_final_norm']
TWIN_OUTPUTS = ['loss', 'grad_x', 'grad_ffn1_norm', 'grad_ffn1_w_gate', 'grad_ffn1_w_up', 'grad_ffn1_w_down', 'grad_mix_norm', 'grad_w_in', 'grad_w_out', 'grad_rg_conv_w', 'grad_rg_conv_b', 'grad_rg_gate_a_w', 'grad_rg_gate_a_b', 'grad_rg_gate_x_w', 'grad_rg_gate_x_b', 'grad_rg_lambda', 'grad_gdn_conv_w', 'grad_gdn_a_log', 'grad_gdn_dt_bias', 'grad_gdn_norm', 'grad_ffn2_norm', 'grad_ffn2_w_gate', 'grad_ffn2_w_up', 'grad_ffn2_w_down', 'grad_final_norm', 'delta_ffn1_norm', 'delta_ffn1_w_gate', 'delta_ffn1_w_up', 'delta_ffn1_w_down', 'delta_mix_norm', 'delta_w_in', 'delta_w_out', 'delta_rg_conv_w', 'delta_rg_conv_b', 'delta_rg_gate_a_w', 'delta_rg_gate_a_b', 'delta_rg_gate_x_w', 'delta_rg_gate_x_b', 'delta_rg_lambda', 'delta_gdn_conv_w', 'delta_gdn_a_log', 'delta_gdn_dt_bias', 'delta_gdn_norm', 'delta_ffn2_norm', 'delta_ffn2_w_gate', 'delta_ffn2_w_up', 'delta_ffn2_w_down', 'delta_final_norm', 'new_m_ffn1_norm', 'new_m_ffn1_w_gate', 'new_m_ffn1_w_up', 'new_m_ffn1_w_down', 'new_m_mix_norm', 'new_m_w_in', 'new_m_w_out', 'new_m_rg_conv_w', 'new_m_rg_conv_b', 'new_m_rg_gate_a_w', 'new_m_rg_gate_a_b', 'new_m_rg_gate_x_w', 'new_m_rg_gate_x_b', 'new_m_rg_lambda', 'new_m_gdn_conv_w', 'new_m_gdn_a_log', 'new_m_gdn_dt_bias', 'new_m_gdn_norm', 'new_m_ffn2_norm', 'new_m_ffn2_w_gate', 'new_m_ffn2_w_up', 'new_m_ffn2_w_down', 'new_m_final_norm', 'new_v_ffn1_norm', 'new_v_ffn1_w_gate', 'new_v_ffn1_w_up', 'new_v_ffn1_w_down', 'new_v_mix_norm', 'new_v_w_in', 'new_v_w_out', 'new_v_rg_conv_w', 'new_v_rg_conv_b', 'new_v_rg_gate_a_w', 'new_v_rg_gate_a_b', 'new_v_rg_gate_x_w', 'new_v_rg_gate_x_b', 'new_v_rg_lambda', 'new_v_gdn_conv_w', 'new_v_gdn_a_log', 'new_v_gdn_dt_bias', 'new_v_gdn_norm', 'new_v_ffn2_norm', 'new_v_ffn2_w_gate', 'new_v_ffn2_w_up', 'new_v_ffn2_w_down', 'new_v_final_norm']
TWIN_LEAF_KINDS = {'loss': 'loss', 'grad_x': 'grad_x', 'grad_ffn1_norm': 'grad_w', 'grad_ffn1_w_gate': 'grad_w', 'grad_ffn1_w_up': 'grad_w', 'grad_ffn1_w_down': 'grad_w', 'grad_mix_norm': 'grad_w', 'grad_w_in': 'grad_w', 'grad_w_out': 'grad_w', 'grad_rg_conv_w': 'grad_w', 'grad_rg_conv_b': 'grad_w', 'grad_rg_gate_a_w': 'grad_w', 'grad_rg_gate_a_b': 'grad_w', 'grad_rg_gate_x_w': 'grad_w', 'grad_rg_gate_x_b': 'grad_w', 'grad_rg_lambda': 'grad_w', 'grad_gdn_conv_w': 'grad_w', 'grad_gdn_a_log': 'grad_w', 'grad_gdn_dt_bias': 'grad_w', 'grad_gdn_norm': 'grad_w', 'grad_ffn2_norm': 'grad_w', 'grad_ffn2_w_gate': 'grad_w', 'grad_ffn2_w_up': 'grad_w', 'grad_ffn2_w_down': 'grad_w', 'grad_final_norm': 'grad_w', 'delta_ffn1_norm': 'delta_w', 'delta_ffn1_w_gate': 'delta_w', 'delta_ffn1_w_up': 'delta_w', 'delta_ffn1_w_down': 'delta_w', 'delta_mix_norm': 'delta_w', 'delta_w_in': 'delta_w', 'delta_w_out': 'delta_w', 'delta_rg_conv_w': 'delta_w', 'delta_rg_conv_b': 'delta_w', 'delta_rg_gate_a_w': 'delta_w', 'delta_rg_gate_a_b': 'delta_w', 'delta_rg_gate_x_w': 'delta_w', 'delta_rg_gate_x_b': 'delta_w', 'delta_rg_lambda': 'delta_w', 'delta_gdn_conv_w': 'delta_w', 'delta_gdn_a_log': 'delta_w', 'delta_gdn_dt_bias': 'delta_w', 'delta_gdn_norm': 'delta_w', 'delta_ffn2_norm': 'delta_w', 'delta_ffn2_w_gate': 'delta_w', 'delta_ffn2_w_up': 'delta_w', 'delta_ffn2_w_down': 'delta_w', 'delta_final_norm': 'delta_w', 'new_m_ffn1_norm': 'new_m', 'new_m_ffn1_w_gate': 'new_m', 'new_m_ffn1_w_up': 'new_m', 'new_m_ffn1_w_down': 'new_m', 'new_m_mix_norm': 'new_m', 'new_m_w_in': 'new_m', 'new_m_w_out': 'new_m', 'new_m_rg_conv_w': 'new_m', 'new_m_rg_conv_b': 'new_m', 'new_m_rg_gate_a_w': 'new_m', 'new_m_rg_gate_a_b': 'new_m', 'new_m_rg_gate_x_w': 'new_m', 'new_m_rg_gate_x_b': 'new_m', 'new_m_rg_lambda': 'new_m', 'new_m_gdn_conv_w': 'new_m', 'new_m_gdn_a_log': 'new_m', 'new_m_gdn_dt_bias': 'new_m', 'new_m_gdn_norm': 'new_m', 'new_m_ffn2_norm': 'new_m', 'new_m_ffn2_w_gate': 'new_m', 'new_m_ffn2_w_up': 'new_m', 'new_m_ffn2_w_down': 'new_m', 'new_m_final_norm': 'new_m', 'new_v_ffn1_norm': 'new_v', 'new_v_ffn1_w_gate': 'new_v', 'new_v_ffn1_w_up': 'new_v', 'new_v_ffn1_w_down': 'new_v', 'new_v_mix_norm': 'new_v', 'new_v_w_in': 'new_v', 'new_v_w_out': 'new_v', 'new_v_rg_conv_w': 'new_v', 'new_v_rg_conv_b': 'new_v', 'new_v_rg_gate_a_w': 'new_v', 'new_v_rg_gate_a_b': 'new_v', 'new_v_rg_gate_x_w': 'new_v', 'new_v_rg_gate_x_b': 'new_v', 'new_v_rg_lambda': 'new_v', 'new_v_gdn_conv_w': 'new_v', 'new_v_gdn_a_log': 'new_v', 'new_v_gdn_dt_bias': 'new_v', 'new_v_gdn_norm': 'new_v', 'new_v_ffn2_norm': 'new_v', 'new_v_ffn2_w_gate': 'new_v', 'new_v_ffn2_w_up': 'new_v', 'new_v_ffn2_w_down': 'new_v', 'new_v_final_norm': 'new_v'}


def _forward(args):
    return _fwd_reference(*[args[k] for k in FWD_PARAMS])


def _output_shape():
    def fwd():
        inp = _fwd_setup_inputs(0)
        return _fwd_reference(*[inp[k] for k in FWD_PARAMS])
    out = _jax.eval_shape(fwd)
    return out.shape, out.dtype

N_MICROBATCH = 1
ADAM_LR = 0.001
ADAM_B1 = 0.9
ADAM_B2 = 0.999
ADAM_EPS = 1e-08
ADAM_WD = 0.01
ADAM_STEP = 10
PER_EXAMPLE_BATCH_AXIS = {'x': 0, 'loss_target': 0}
SHARED_INPUTS = []
_WEIGHT_DTYPES = {'ffn1_norm': _jnp.float32, 'ffn1_w_gate': _jnp.float32, 'ffn1_w_up': _jnp.float32, 'ffn1_w_down': _jnp.float32, 'mix_norm': _jnp.float32, 'w_in': _jnp.float32, 'w_out': _jnp.float32, 'rg_conv_w': _jnp.float32, 'rg_conv_b': _jnp.float32, 'rg_gate_a_w': _jnp.float32, 'rg_gate_a_b': _jnp.float32, 'rg_gate_x_w': _jnp.float32, 'rg_gate_x_b': _jnp.float32, 'rg_lambda': _jnp.float32, 'gdn_conv_w': _jnp.float32, 'gdn_a_log': _jnp.float32, 'gdn_dt_bias': _jnp.float32, 'gdn_norm': _jnp.float32, 'ffn2_norm': _jnp.float32, 'ffn2_w_gate': _jnp.float32, 'ffn2_w_up': _jnp.float32, 'ffn2_w_down': _jnp.float32, 'final_norm': _jnp.float32}
MOMENT_SCALE = {'ffn1_norm': 1.192476e-01, 'ffn1_w_gate': 5.173071e-02, 'ffn1_w_up': 5.014269e-02, 'ffn1_w_down': 8.303613e-02, 'mix_norm': 1.936099e-01, 'w_in': 1.077782e-01, 'w_out': 1.336121e-01, 'rg_conv_w': 1.383493e-01, 'rg_conv_b': 2.089978e+00, 'rg_gate_a_w': 3.563750e-02, 'rg_gate_a_b': 2.975754e-02, 'rg_gate_x_w': 6.443664e-02, 'rg_gate_x_b': 2.947811e-02, 'rg_lambda': 4.952707e-02, 'gdn_conv_w': 8.868025e-02, 'gdn_a_log': 3.094195e-01, 'gdn_dt_bias': 2.987715e-01, 'gdn_norm': 2.556777e-01, 'ffn2_norm': 9.281403e-02, 'ffn2_w_gate': 3.797671e-02, 'ffn2_w_up': 3.677046e-02, 'ffn2_w_down': 6.084247e-02, 'final_norm': 6.383057e+01}


def _to_microbatches(a, axis):
    t = _jnp.moveaxis(a, axis, 0)
    t = t.reshape((N_MICROBATCH, t.shape[0] // N_MICROBATCH) + t.shape[1:])
    return _jnp.moveaxis(t, 1, axis + 1)


def setup_inputs(seed: int = 0) -> dict:
    inp = _fwd_setup_inputs(seed)
    key = _jax.random.fold_in(_jax.random.key(seed), 7919)
    shape, _ = _output_shape()
    out = dict(inp)
    out["loss_target"] = _jax.random.normal(_jax.random.fold_in(key, 0), shape, _jnp.float32)
    for i, name in enumerate(TWIN_WEIGHTS):
        w = inp[name].astype(_jnp.float32)
        if MOMENT_SCALE is None:
            s = _jnp.sqrt(_jnp.mean(_jnp.square(w)) + 1e-30)
        else:
            s = MOMENT_SCALE[name]
        km, kv = _jax.random.split(_jax.random.fold_in(key, i + 1))
        out[name] = w
        out["m_" + name] = s * _jax.random.normal(km, w.shape, _jnp.float32)
        out["v_" + name] = (s * s) * _jax.random.uniform(kv, w.shape, _jnp.float32, 0.5, 1.5)
    if N_MICROBATCH > 1:
        for name, axis in PER_EXAMPLE_BATCH_AXIS.items():
            out[name] = _to_microbatches(out[name], axis)
    return {'x': out['x'], 'ffn1_norm': out['ffn1_norm'], 'ffn1_w_gate': out['ffn1_w_gate'], 'ffn1_w_up': out['ffn1_w_up'], 'ffn1_w_down': out['ffn1_w_down'], 'mix_norm': out['mix_norm'], 'w_in': out['w_in'], 'w_out': out['w_out'], 'rg_conv_w': out['rg_conv_w'], 'rg_conv_b': out['rg_conv_b'], 'rg_gate_a_w': out['rg_gate_a_w'], 'rg_gate_a_b': out['rg_gate_a_b'], 'rg_gate_x_w': out['rg_gate_x_w'], 'rg_gate_x_b': out['rg_gate_x_b'], 'rg_lambda': out['rg_lambda'], 'gdn_conv_w': out['gdn_conv_w'], 'gdn_a_log': out['gdn_a_log'], 'gdn_dt_bias': out['gdn_dt_bias'], 'gdn_norm': out['gdn_norm'], 'ffn2_norm': out['ffn2_norm'], 'ffn2_w_gate': out['ffn2_w_gate'], 'ffn2_w_up': out['ffn2_w_up'], 'ffn2_w_down': out['ffn2_w_down'], 'final_norm': out['final_norm'], 'loss_target': out['loss_target'], 'm_ffn1_norm': out['m_ffn1_norm'], 'm_ffn1_w_gate': out['m_ffn1_w_gate'], 'm_ffn1_w_up': out['m_ffn1_w_up'], 'm_ffn1_w_down': out['m_ffn1_w_down'], 'm_mix_norm': out['m_mix_norm'], 'm_w_in': out['m_w_in'], 'm_w_out': out['m_w_out'], 'm_rg_conv_w': out['m_rg_conv_w'], 'm_rg_conv_b': out['m_rg_conv_b'], 'm_rg_gate_a_w': out['m_rg_gate_a_w'], 'm_rg_gate_a_b': out['m_rg_gate_a_b'], 'm_rg_gate_x_w': out['m_rg_gate_x_w'], 'm_rg_gate_x_b': out['m_rg_gate_x_b'], 'm_rg_lambda': out['m_rg_lambda'], 'm_gdn_conv_w': out['m_gdn_conv_w'], 'm_gdn_a_log': out['m_gdn_a_log'], 'm_gdn_dt_bias': out['m_gdn_dt_bias'], 'm_gdn_norm': out['m_gdn_norm'], 'm_ffn2_norm': out['m_ffn2_norm'], 'm_ffn2_w_gate': out['m_ffn2_w_gate'], 'm_ffn2_w_up': out['m_ffn2_w_up'], 'm_ffn2_w_down': out['m_ffn2_w_down'], 'm_final_norm': out['m_final_norm'], 'v_ffn1_norm': out['v_ffn1_norm'], 'v_ffn1_w_gate': out['v_ffn1_w_gate'], 'v_ffn1_w_up': out['v_ffn1_w_up'], 'v_ffn1_w_down': out['v_ffn1_w_down'], 'v_mix_norm': out['v_mix_norm'], 'v_w_in': out['v_w_in'], 'v_w_out': out['v_w_out'], 'v_rg_conv_w': out['v_rg_conv_w'], 'v_rg_conv_b': out['v_rg_conv_b'], 'v_rg_gate_a_w': out['v_rg_gate_a_w'], 'v_rg_gate_a_b': out['v_rg_gate_a_b'], 'v_rg_gate_x_w': out['v_rg_gate_x_w'], 'v_rg_gate_x_b': out['v_rg_gate_x_b'], 'v_rg_lambda': out['v_rg_lambda'], 'v_gdn_conv_w': out['v_gdn_conv_w'], 'v_gdn_a_log': out['v_gdn_a_log'], 'v_gdn_dt_bias': out['v_gdn_dt_bias'], 'v_gdn_norm': out['v_gdn_norm'], 'v_ffn2_norm': out['v_ffn2_norm'], 'v_ffn2_w_gate': out['v_ffn2_w_gate'], 'v_ffn2_w_up': out['v_ffn2_w_up'], 'v_ffn2_w_down': out['v_ffn2_w_down'], 'v_final_norm': out['v_final_norm']}


def _loss(weights, diff, rest, loss_target):
    with _jax.named_scope("forward"):
        args = {**rest, TWIN_DIFF_INPUT: diff, **{k: w.astype(_WEIGHT_DTYPES[k]) for k, w in weights.items()}}
        y = _forward(args)
    with _jax.named_scope("loss_head"):
        err = _jnp.square(y.astype(_jnp.float32) - loss_target)
        return 0.5 * _jnp.sum(_jnp.mean(err, axis=-1)) if err.ndim else 0.5 * err


def _adamw(w, g, m, v):
    m = ADAM_B1 * m + (1.0 - ADAM_B1) * g
    v = ADAM_B2 * v + (1.0 - ADAM_B2) * _jnp.square(g)
    m_hat = m / (1.0 - ADAM_B1 ** ADAM_STEP)
    v_hat = v / (1.0 - ADAM_B2 ** ADAM_STEP)
    delta = -ADAM_LR * (m_hat / (_jnp.sqrt(v_hat) + ADAM_EPS) + ADAM_WD * w)
    return delta, m, v


def reference(x, ffn1_norm, ffn1_w_gate, ffn1_w_up, ffn1_w_down, mix_norm, w_in, w_out, rg_conv_w, rg_conv_b, rg_gate_a_w, rg_gate_a_b, rg_gate_x_w, rg_gate_x_b, rg_lambda, gdn_conv_w, gdn_a_log, gdn_dt_bias, gdn_norm, ffn2_norm, ffn2_w_gate, ffn2_w_up, ffn2_w_down, final_norm, loss_target, m_ffn1_norm, m_ffn1_w_gate, m_ffn1_w_up, m_ffn1_w_down, m_mix_norm, m_w_in, m_w_out, m_rg_conv_w, m_rg_conv_b, m_rg_gate_a_w, m_rg_gate_a_b, m_rg_gate_x_w, m_rg_gate_x_b, m_rg_lambda, m_gdn_conv_w, m_gdn_a_log, m_gdn_dt_bias, m_gdn_norm, m_ffn2_norm, m_ffn2_w_gate, m_ffn2_w_up, m_ffn2_w_down, m_final_norm, v_ffn1_norm, v_ffn1_w_gate, v_ffn1_w_up, v_ffn1_w_down, v_mix_norm, v_w_in, v_w_out, v_rg_conv_w, v_rg_conv_b, v_rg_gate_a_w, v_rg_gate_a_b, v_rg_gate_x_w, v_rg_gate_x_b, v_rg_lambda, v_gdn_conv_w, v_gdn_a_log, v_gdn_dt_bias, v_gdn_norm, v_ffn2_norm, v_ffn2_w_gate, v_ffn2_w_up, v_ffn2_w_down, v_final_norm):
    given = dict(x=x, ffn1_norm=ffn1_norm, ffn1_w_gate=ffn1_w_gate, ffn1_w_up=ffn1_w_up, ffn1_w_down=ffn1_w_down, mix_norm=mix_norm, w_in=w_in, w_out=w_out, rg_conv_w=rg_conv_w, rg_conv_b=rg_conv_b, rg_gate_a_w=rg_gate_a_w, rg_gate_a_b=rg_gate_a_b, rg_gate_x_w=rg_gate_x_w, rg_gate_x_b=rg_gate_x_b, rg_lambda=rg_lambda, gdn_conv_w=gdn_conv_w, gdn_a_log=gdn_a_log, gdn_dt_bias=gdn_dt_bias, gdn_norm=gdn_norm, ffn2_norm=ffn2_norm, ffn2_w_gate=ffn2_w_gate, ffn2_w_up=ffn2_w_up, ffn2_w_down=ffn2_w_down, final_norm=final_norm, loss_target=loss_target, m_ffn1_norm=m_ffn1_norm, m_ffn1_w_gate=m_ffn1_w_gate, m_ffn1_w_up=m_ffn1_w_up, m_ffn1_w_down=m_ffn1_w_down, m_mix_norm=m_mix_norm, m_w_in=m_w_in, m_w_out=m_w_out, m_rg_conv_w=m_rg_conv_w, m_rg_conv_b=m_rg_conv_b, m_rg_gate_a_w=m_rg_gate_a_w, m_rg_gate_a_b=m_rg_gate_a_b, m_rg_gate_x_w=m_rg_gate_x_w, m_rg_gate_x_b=m_rg_gate_x_b, m_rg_lambda=m_rg_lambda, m_gdn_conv_w=m_gdn_conv_w, m_gdn_a_log=m_gdn_a_log, m_gdn_dt_bias=m_gdn_dt_bias, m_gdn_norm=m_gdn_norm, m_ffn2_norm=m_ffn2_norm, m_ffn2_w_gate=m_ffn2_w_gate, m_ffn2_w_up=m_ffn2_w_up, m_ffn2_w_down=m_ffn2_w_down, m_final_norm=m_final_norm, v_ffn1_norm=v_ffn1_norm, v_ffn1_w_gate=v_ffn1_w_gate, v_ffn1_w_up=v_ffn1_w_up, v_ffn1_w_down=v_ffn1_w_down, v_mix_norm=v_mix_norm, v_w_in=v_w_in, v_w_out=v_w_out, v_rg_conv_w=v_rg_conv_w, v_rg_conv_b=v_rg_conv_b, v_rg_gate_a_w=v_rg_gate_a_w, v_rg_gate_a_b=v_rg_gate_a_b, v_rg_gate_x_w=v_rg_gate_x_w, v_rg_gate_x_b=v_rg_gate_x_b, v_rg_lambda=v_rg_lambda, v_gdn_conv_w=v_gdn_conv_w, v_gdn_a_log=v_gdn_a_log, v_gdn_dt_bias=v_gdn_dt_bias, v_gdn_norm=v_gdn_norm, v_ffn2_norm=v_ffn2_norm, v_ffn2_w_gate=v_ffn2_w_gate, v_ffn2_w_up=v_ffn2_w_up, v_ffn2_w_down=v_ffn2_w_down, v_final_norm=v_final_norm)
    weights = {n: given[n] for n in TWIN_WEIGHTS}
    shared = {n: given[n] for n in SHARED_INPUTS}
    per_example = {n: given[n] for n in ['x']}
    grad_fn = _jax.value_and_grad(_loss, argnums=(0, 1))

    def one_microbatch(ex, loss_target):
        ex = dict(ex)
        diff = ex.pop(TWIN_DIFF_INPUT)
        return grad_fn(weights, diff, {**shared, **ex}, loss_target)

    if N_MICROBATCH == 1:
        loss, (grad_w, grad_x) = one_microbatch(per_example, given["loss_target"])
    else:
        def body(carry, xs):
            loss_sum, grad_sum = carry
            l_k, (gw_k, gx_k) = one_microbatch(xs[0], xs[1])
            with _jax.named_scope("update"):
                return (loss_sum + l_k, _jax.tree.map(_jnp.add, grad_sum, gw_k)), gx_k

        init = (_jnp.zeros((), _jnp.float32), _jax.tree.map(_jnp.zeros_like, weights))
        (loss, grad_w), grad_x = _jax.lax.scan(body, init, (per_example, given["loss_target"]))
    with _jax.named_scope("update"):
        delta_w, new_m, new_v = {}, {}, {}
        for n in TWIN_WEIGHTS:
            delta_w[n], new_m[n], new_v[n] = _adamw(weights[n], grad_w[n], given["m_" + n], given["v_" + n])
    return (loss, grad_x, *[grad_w[n] for n in TWIN_WEIGHTS], *[delta_w[n] for n in TWIN_WEIGHTS],
            *[new_m[n] for n in TWIN_WEIGHTS], *[new_v[n] for n in TWIN_WEIGHTS])
```

```python
import functools
import math

import jax
import jax.numpy as jnp
from jax import lax
from jax.experimental import pallas as pl
from jax.experimental.pallas import tpu as pltpu

F32, BF16 = jnp.float32, jnp.bfloat16

D_MODEL = 1024
D_FF = 2816
RG_W = 512
RG_BLOCKS = 8
RG_BLOCK = 64
RG_C = 8.0
CONV_W = 4
GDN_H = 4
GDN_DK = 128
CHUNK = 64
EPS = 1e-6
D_IN = 3088
D_IN_PAD = 3200
COL_BA = 3072
N_DEV = 8
HALO = 8
VMEM_LIMIT = 48 * 1024 * 1024

ADAM_LR = 0.001
ADAM_B1 = 0.9
ADAM_B2 = 0.999
ADAM_EPS = 1e-08
ADAM_WD = 0.01
ADAM_STEP = 10

HI = lax.Precision.HIGHEST


def _cp(n):
    return pltpu.CompilerParams(dimension_semantics=("arbitrary",) * n, vmem_limit_bytes=VMEM_LIMIT)


def _tile(n, pref):
    return min(n, pref)


def _sigmoid(x):
    return jax.nn.sigmoid(x)


def _softplus(x):
    return jnp.maximum(x, 0.0) + jnp.log(1.0 + jnp.exp(-jnp.abs(x)))


def _dot(a, b, ca, cb, prec=None):
    return lax.dot_general(a, b, (((ca,), (cb,)), ((), ())), preferred_element_type=F32, precision=prec)


def _fused_mm(name, M, N, K, tm, tn, tk, ops, pairs, extras, outs, epilogue):
    nm, nn, nk = M // tm, N // tn, K // tk
    assert nm * tm == M and nn * tn == N and nk * tk == K, (name, M, N, K, tm, tn, tk)
    spec_of = {
        "mk": pl.BlockSpec((tm, tk), lambda i, j, k: (i, k)),
        "km": pl.BlockSpec((tk, tm), lambda i, j, k: (k, i)),
        "kn": pl.BlockSpec((tk, tn), lambda i, j, k: (k, j)),
        "nk": pl.BlockSpec((tn, tk), lambda i, j, k: (j, k)),
    }
    in_specs = [spec_of[m] for _, m in ops]
    in_specs += [pl.BlockSpec(bs, lambda i, j, k, im=im: im(i, j)) for _, bs, im in extras]
    out_specs = [pl.BlockSpec(bs, lambda i, j, k, im=im: im(i, j)) for _, bs, im in outs]
    n_ops, n_ex, n_out = len(ops), len(extras), len(outs)
    n_acc = 1 + max(g for _, _, g in pairs)
    modes = [m for _, m in ops]

    def body(*refs):
        op_refs = refs[:n_ops]
        ex_refs = refs[n_ops:n_ops + n_ex]
        out_refs = refs[n_ops + n_ex:n_ops + n_ex + n_out]
        accs = refs[n_ops + n_ex + n_out:]
        i = pl.program_id(0)
        k = pl.program_id(2)

        @pl.when(k == 0)
        def _():
            for a in accs:
                a[...] = jnp.zeros_like(a)

        vals = [r[...].astype(BF16) for r in op_refs]
        for ia, ib, g in pairs:
            ca = 1 if modes[ia] == "mk" else 0
            cb = 0 if modes[ib] == "kn" else 1
            accs[g][...] += _dot(vals[ia], vals[ib], ca, cb)

        @pl.when(k == nk - 1)
        def _():
            epilogue(i, accs, ex_refs, out_refs)

    res = pl.pallas_call(
        body, name=name, grid=(nm, nn, nk), in_specs=in_specs, out_specs=out_specs,
        out_shape=[o for o, _, _ in outs],
        scratch_shapes=[pltpu.VMEM((tm, tn), F32)] * n_acc,
        compiler_params=_cp(3),
    )(*[a for a, _ in ops], *[a for a, _, _ in extras])
    return res


def _mn(i, j):
    return (i, j)


def _row0(i, j):
    return (0, 0)


def _rows(name, S, ts, ins, outs, body, scratch=()):
    return pl.pallas_call(
        body, name=name, grid=(S // ts,),
        in_specs=[pl.BlockSpec(bs, im) for _, bs, im in ins],
        out_specs=[pl.BlockSpec(bs, im) for _, bs, im in outs],
        out_shape=[o for o, _, _ in outs],
        scratch_shapes=list(scratch),
        compiler_params=_cp(1),
    )(*[a for a, _, _ in ins])


def _halo_ins(arr, S, ts, width, colblk):
    per = ts // HALO
    last = S // HALO - 1
    return [
        (arr, (ts, width), lambda i: (i, colblk)),
        (arr, (HALO, width), lambda i: (jnp.maximum(i * per - 1, 0), colblk)),
        (arr, (HALO, width), lambda i: (jnp.minimum((i + 1) * per, last), colblk)),
    ]


def _ext(main_ref, prev_ref, next_ref, i, n_tiles):
    prev = jnp.where(i > 0, prev_ref[...].astype(F32), 0.0)
    nxt = jnp.where(i < n_tiles - 1, next_ref[...].astype(F32), 0.0)
    return jnp.concatenate([prev, main_ref[...].astype(F32), nxt], axis=0)


def _shift(ext, off, ts):
    n = ext.shape[0]
    if off == 0:
        return ext[HALO:HALO + ts]
    return pltpu.roll(ext, (-off) % n, 0)[HALO:HALO + ts]


def _rmsnorm_fwd(name, x, g):
    S, D = x.shape
    ts = _tile(S, 512)

    def body(x_ref, g_ref, o_ref):
        xv = x_ref[...]
        r = lax.rsqrt(jnp.mean(xv * xv, axis=-1, keepdims=True) + EPS)
        o_ref[...] = (xv * r * g_ref[...]).astype(BF16)

    return _rows(name, S, ts,
                 [(x, (ts, D), lambda i: (i, 0)), (g, (1, D), lambda i: (0, 0))],
                 [(jax.ShapeDtypeStruct((S, D), BF16), (ts, D), lambda i: (i, 0))], body)[0]


def _rmsnorm_bwd_tile(dh, x, g):
    r = lax.rsqrt(jnp.mean(x * x, axis=-1, keepdims=True) + EPS)
    xhat = x * r
    dxn = dh * g
    dx = r * (dxn - xhat * jnp.mean(dxn * xhat, axis=-1, keepdims=True))
    return dx, dh * xhat


def _ffn_fwd(tag, x, h, wg, wu, wd):
    S = x.shape[0]
    tm = _tile(S, 512)
    tn = 1408

    def epi_up(i, accs, ex, out):
        a = accs[0][...]
        b = accs[1][...]
        out[0][...] = a.astype(BF16)
        out[1][...] = b.astype(BF16)
        out[2][...] = (a * _sigmoid(a) * b).astype(BF16)

    sds = jax.ShapeDtypeStruct((S, D_FF), BF16)
    a, b, f = _fused_mm(f"{tag}_up", S, D_FF, D_MODEL, tm, tn, D_MODEL,
                        [(h, "mk"), (wg, "kn"), (wu, "kn")], [(0, 1, 0), (0, 2, 1)], [],
                        [(sds, (tm, tn), _mn)] * 3, epi_up)

    def epi_down(i, accs, ex, out):
        out[0][...] = ex[0][...] + 0.5 * accs[0][...]

    xo = _fused_mm(f"{tag}_down", S, D_MODEL, D_FF, tm, D_MODEL, 1408,
                   [(f, "mk"), (wd, "kn")], [(0, 1, 0)], [(x, (tm, D_MODEL), _mn)],
                   [(jax.ShapeDtypeStruct((S, D_MODEL), F32), (tm, D_MODEL), _mn)], epi_down)[0]
    return xo, a, b, f


def _conv_taps(ext, w_ref, ts):
    acc = None
    for j in range(CONV_W):
        term = w_ref[j:j + 1, :] * _shift(ext, j - 2, ts)
        acc = term if acc is None else acc + term
    return acc


def _l2norm_heads(s, scale):
    outs = []
    for h in range(GDN_H):
        sh = s[:, h * GDN_DK:(h + 1) * GDN_DK]
        outs.append(sh * (lax.rsqrt(jnp.sum(sh * sh, axis=-1, keepdims=True) + EPS) * scale))
    return jnp.concatenate(outs, axis=-1)


def _conv_fwd(name, p, colblk, w, bias, mode):
    S = p.shape[0]
    ts = _tile(S, 512)
    n_tiles = S // ts
    C = w.shape[1]

    def body(main, prev, nxt, w_ref, b_ref, o_ref):
        i = pl.program_id(0)
        c = _conv_taps(_ext(main, prev, nxt, i, n_tiles), w_ref, ts)
        if mode == "bias":
            o_ref[...] = c + b_ref[...]
        else:
            s = c * _sigmoid(c)
            if mode == "q":
                s = _l2norm_heads(s, GDN_DK ** -0.5)
            elif mode == "k":
                s = _l2norm_heads(s, 1.0)
            o_ref[...] = s

    ins = _halo_ins(p, S, ts, C, colblk) + [(w, (CONV_W, C), lambda i: (0, 0)), (bias, (1, C), lambda i: (0, 0))]
    return _rows(name, S, ts, ins, [(jax.ShapeDtypeStruct((S, C), F32), (ts, C), lambda i: (i, 0))], body)[0]


def _rg_gate_terms(pre, xc, prm_ref, d):
    r = _sigmoid(pre[:, d * 1024:d * 1024 + RG_W] + prm_ref[2 * d:2 * d + 1, :])
    ig = _sigmoid(pre[:, d * 1024 + RG_W:(d + 1) * 1024] + prm_ref[2 * d + 1:2 * d + 2, :])
    sp = _softplus(-prm_ref[4 + d:5 + d, :])
    log_a = -RG_C * r * sp
    a = jnp.exp(log_a)
    t = jnp.tanh(log_a)
    sq = jnp.sqrt(-2.0 * t / (1.0 - t))
    return r, ig, sp, a, sq


def _rg_gates_fwd(xc, bd, prm):
    S = xc.shape[0]
    tm = _tile(S, 256)

    def epi(i, accs, ex, out):
        pre = accs[0][...]
        xv = ex[0][...]
        for d in range(2):
            r, ig, sp, a, sq = _rg_gate_terms(pre, xv, ex[1], d)
            out[2 * d][...] = a
            out[2 * d + 1][...] = sq * ig * xv

    sds = jax.ShapeDtypeStruct((S, RG_W), F32)
    blk = (tm, RG_W)
    im = lambda i, j: (i, 0)
    return _fused_mm("rg_gates_fwd", S, 4 * RG_W, RG_W, tm, 4 * RG_W, RG_W,
                     [(xc, "mk"), (bd, "kn")], [(0, 1, 0)],
                     [(xc, blk, im), (prm, (8, RG_W), _row0)], [(sds, blk, im)] * 4, epi)


def _rg_scan(name, a_f, b_f, a_b, b_b):
    S, C = a_f.shape
    ts = _tile(S, 512)
    n_tiles = S // ts

    def body(af, bf, ab, bb, hf, hb, carry):
        @pl.when(pl.program_id(0) == 0)
        def _():
            carry[...] = jnp.zeros_like(carry)

        def step(t, c):
            cf, cb = c
            cf = af[pl.ds(t, 1), :] * cf + bf[pl.ds(t, 1), :]
            hf[pl.ds(t, 1), :] = cf
            tb = ts - 1 - t
            cb = ab[pl.ds(tb, 1), :] * cb + bb[pl.ds(tb, 1), :]
            hb[pl.ds(tb, 1), :] = cb
            return cf, cb

        cf, cb = lax.fori_loop(0, ts, step, (carry[0:1, :], carry[1:2, :]), unroll=8)
        carry[0:1, :] = cf
        carry[1:2, :] = cb

    fw = lambda i: (i, 0)
    bw = lambda i: (n_tiles - 1 - i, 0)
    sds = jax.ShapeDtypeStruct((S, C), F32)
    return _rows(name, S, ts,
                 [(a_f, (ts, C), fw), (b_f, (ts, C), fw), (a_b, (ts, C), bw), (b_b, (ts, C), bw)],
                 [(sds, (ts, C), fw), (sds, (ts, C), bw)], body, scratch=[pltpu.VMEM((8, C), F32)])


def _tri_masks():
    ri = lax.broadcasted_iota(jnp.int32, (CHUNK, CHUNK), 0)
    ci = lax.broadcasted_iota(jnp.int32, (CHUNK, CHUNK), 1)
    return ri, ci


def _gdn_prep_fwd(p, prm):
    S = p.shape[0]
    ts = _tile(S, 512)

    def body(p_ref, prm_ref, o_ref):
        raw = p_ref[...]
        lane = lax.broadcasted_iota(jnp.int32, (1, 128), 1)
        g = -jnp.exp(prm_ref[0:1, :]) * _softplus(raw + prm_ref[1:2, :])
        g = jnp.where((lane >= 8) & (lane < 16), g, 0.0)
        beta = _sigmoid(raw)
        ri, ci = _tri_masks()
        lower = (ri >= ci).astype(F32)
        upper = (ri <= ci).astype(F32)
        for c in range(ts // CHUNK):
            rows = slice(c * CHUNK, (c + 1) * CHUNK)
            gch = g[rows]
            gc = jnp.where(lane < 12, _dot(lower, gch, 1, 0, HI), _dot(upper, gch, 1, 0, HI))
            o_ref[rows, :] = jnp.where(lane < 8, beta[rows], gc)

    return _rows("gdn_prep_fwd", S, ts,
                 [(p, (ts, 128), lambda i: (i, COL_BA // 128)), (prm, (8, 128), lambda i: (0, 0))],
                 [(jax.ShapeDtypeStruct((S, 128), F32), (ts, 128), lambda i: (i, 0))], body)[0]


def _tri_inv(l_mat, eye):
    x = -l_mat
    t = eye + x
    pw = x
    for _ in range(5):
        pw = _dot(pw, pw, 1, 0, HI)
        t = t + _dot(t, pw, 1, 0, HI)
    return t


def _gdn_chunk_terms(q_ref, k_ref, v_ref, bg_ref, gcr_ref, c, h, d, rev, ri, ci):
    r0 = pl.multiple_of(c * CHUNK, CHUNK)
    rows = pl.ds(r0, CHUNK)
    cols = slice(h * GDN_DK, (h + 1) * GDN_DK)
    col = d * GDN_H + h
    qh = q_ref[rows, cols]
    kh = k_ref[rows, cols]
    vh = v_ref[rows, cols]
    beta = bg_ref[rows, col:col + 1]
    gc = bg_ref[rows, 8 + col:9 + col]
    gcr = gcr_ref[c, col:col + 1, :]
    last = 0 if rev else CHUNK - 1
    gl = bg_ref[pl.ds(r0 + last, 1), 8 + col:9 + col]
    incl = (ri <= ci) if rev else (ri >= ci)
    strict = (ri < ci) if rev else (ri > ci)
    dm = jnp.where(incl, jnp.exp(jnp.where(incl, gc - gcr, 0.0)), 0.0)
    kb = kh * beta
    vb = vh * beta
    kk = _dot(kb, kh, 1, 1, HI)
    qk = _dot(qh, kh, 1, 1, HI)
    l_mat = jnp.where(strict, kk * dm, 0.0)
    eg = jnp.exp(gc)
    egl = jnp.exp(gl - gc)
    return dict(rows=rows, cols=cols, col=col, qh=qh, kh=kh, vh=vh, beta=beta, gc=gc, gl=gl, incl=incl,
                strict=strict, dm=dm, kb=kb, vb=vb, kk=kk, qk=qk, l_mat=l_mat, eg=eg, egl=egl,
                kbg=kb * eg, qd=qh * eg, kd=kh * egl, cd=jnp.exp(gl), a_mat=qk * dm)


def _gdn_fwd(name, q, k, v, bg, gcr, d):
    S = q.shape[0]
    ts = _tile(S, 512)
    n_tiles = S // ts
    ncb = ts // CHUNK
    rev = d == 1
    W = GDN_H * GDN_DK

    def body(q_ref, k_ref, v_ref, bg_ref, gcr_ref, o_ref, s_ref, t_ref, state):
        @pl.when(pl.program_id(0) == 0)
        def _():
            state[...] = jnp.zeros_like(state)

        ri, ci = _tri_masks()
        eye = (ri == ci).astype(F32)

        def chunk(cc, carry):
            c = ncb - 1 - cc if rev else cc
            for h in range(GDN_H):
                m = _gdn_chunk_terms(q_ref, k_ref, v_ref, bg_ref, gcr_ref, c, h, d, rev, ri, ci)
                t_mat = _tri_inv(m["l_mat"], eye)
                u = _dot(t_mat, m["vb"], 1, 0, HI)
                w = _dot(t_mat, m["kbg"], 1, 0, HI)
                st = state[h]
                vn = u - _dot(w, st, 1, 0, HI)
                o_ref[m["rows"], m["cols"]] = _dot(m["qd"], st, 1, 0, HI) + _dot(m["a_mat"], vn, 1, 0, HI)
                s_ref[c, h] = st
                t_ref[c, h] = t_mat
                state[h] = st * m["cd"] + _dot(m["kd"], vn, 0, 0, HI)
            return carry

        lax.fori_loop(0, ncb, chunk, 0)

    tix = (lambda i: n_tiles - 1 - i) if rev else (lambda i: i)
    ins = [(q, (ts, W), lambda i: (tix(i), 0)), (k, (ts, W), lambda i: (tix(i), 0)),
           (v, (ts, W), lambda i: (tix(i), 0)), (bg, (ts, 128), lambda i: (tix(i), 0)),
           (gcr, (ncb, 8, CHUNK), lambda i: (tix(i), 0, 0))]
    nch = S // CHUNK
    outs = [(jax.ShapeDtypeStruct((S, W), F32), (ts, W), lambda i: (tix(i), 0)),
            (jax.ShapeDtypeStruct((nch, GDN_H, GDN_DK, GDN_DK), F32), (ncb, GDN_H, GDN_DK, GDN_DK),
             lambda i: (tix(i), 0, 0, 0)),
            (jax.ShapeDtypeStruct((nch, GDN_H, CHUNK, CHUNK), F32), (ncb, GDN_H, CHUNK, CHUNK),
             lambda i: (tix(i), 0, 0, 0))]
    return _rows(name, S, ts, ins, outs, body, scratch=[pltpu.VMEM((GDN_H, GDN_DK, GDN_DK), F32)])


def _gelu(x):
    c = math.sqrt(2.0 / math.pi)
    t = jnp.tanh(c * (x + 0.044715 * x * x * x))
    return 0.5 * x * (1.0 + t), t


def _mix_out_fwd(h_f, h_b, o_f, o_b, p, gn):
    S = h_f.shape[0]
    ts = _tile(S, 512)

    def body(hf, hb, of, ob, gate, z, gn_ref, y_ref):
        ge, _ = _gelu(gate[...])
        y_ref[:, 0:RG_W] = ((hf[...] + hb[...]) * ge).astype(BF16)
        o = of[...] + ob[...]
        zv = z[...]
        sz = zv * _sigmoid(zv)
        for h in range(GDN_H):
            cols = slice(h * GDN_DK, (h + 1) * GDN_DK)
            oh = o[:, cols]
            n = oh * lax.rsqrt(jnp.mean(oh * oh, axis=-1, keepdims=True) + EPS) * gn_ref[...]
            y_ref[:, RG_W + h * GDN_DK:RG_W + (h + 1) * GDN_DK] = (n * sz[:, cols]).astype(BF16)

    blk = (ts, RG_W)
    im = lambda i: (i, 0)
    ins = [(h_f, blk, im), (h_b, blk, im), (o_f, blk, im), (o_b, blk, im),
           (p, blk, lambda i: (i, 1)), (p, blk, lambda i: (i, 5)), (gn, (1, GDN_DK), lambda i: (0, 0))]
    return _rows("mix_out_fwd", S, ts, ins,
                 [(jax.ShapeDtypeStruct((S, D_MODEL), BF16), (ts, D_MODEL), im)], body)[0]


def _loss_head(x, target, g):
    S, D = x.shape
    ts = _tile(S, 512)

    def body(x_ref, t_ref, g_ref, dx_ref, loss_ref, dg_ref):
        @pl.when(pl.program_id(0) == 0)
        def _():
            loss_ref[...] = jnp.zeros_like(loss_ref)
            dg_ref[...] = jnp.zeros_like(dg_ref)

        xv = x_ref[...]
        gv = g_ref[...]
        r = lax.rsqrt(jnp.mean(xv * xv, axis=-1, keepdims=True) + EPS)
        err = xv * r * gv - t_ref[...]
        loss_ref[...] += jnp.sum(err * err) * (0.5 / D)
        dx, dgt = _rmsnorm_bwd_tile(err * (1.0 / D), xv, gv)
        dx_ref[...] = dx
        dg_ref[...] += jnp.sum(dgt, axis=0, keepdims=True)

    im = lambda i: (i, 0)
    z = lambda i: (0, 0)
    return _rows("loss_head", S, ts,
                 [(x, (ts, D), im), (target, (ts, D), im), (g, (1, D), z)],
                 [(jax.ShapeDtypeStruct((S, D), F32), (ts, D), im),
                  (jax.ShapeDtypeStruct((8, 128), F32), (8, 128), z),
                  (jax.ShapeDtypeStruct((1, D), F32), (1, D), z)], body)


def _block_diag(w):
    n = w.shape[0]
    return jnp.einsum("nij,nm->nimj", w, jnp.eye(n, dtype=w.dtype)).reshape(n * w.shape[1], n * w.shape[2])


def _rg_bd(a_w, x_w):
    return jnp.concatenate([_block_diag(a_w[0]), _block_diag(x_w[0]), _block_diag(a_w[1]), _block_diag(x_w[1])],
                           axis=1).astype(BF16)


def _rg_prm(ba, bx, lam):
    return jnp.concatenate([ba[0:1], bx[0:1], ba[1:2], bx[1:2], lam, jnp.zeros((2, RG_W), F32)], axis=0)


def _gdn_prm(a_log, dt_bias):
    rows = jnp.zeros((8, 128), F32)
    rows = rows.at[0, 8:16].set(a_log.reshape(-1))
    return rows.at[1, 8:16].set(dt_bias.reshape(-1))


def _gc_rows(bg):
    S = bg.shape[0]
    return bg[:, 8:16].reshape(S // CHUNK, CHUNK, 8).transpose(0, 2, 1)


def _layer_fwd(x0, target, W):
    S = x0.shape[0]
    R = {}
    R["h1"] = _rmsnorm_fwd("rms1", x0, W["ffn1_norm"])
    R["x1"], R["a1"], R["b1"], R["f1"] = _ffn_fwd("ffn1", x0, R["h1"], W["ffn1_w_gate"], W["ffn1_w_up"], W["ffn1_w_down"])
    R["h2"] = _rmsnorm_fwd("rms2", R["x1"], W["mix_norm"])
    tm = _tile(S, 512)
    R["p"] = _fused_mm("in_proj", S, D_IN_PAD, D_MODEL, tm, 640, D_MODEL, [(R["h2"], "mk"), (W["w_in"], "kn")],
                       [(0, 1, 0)], [], [(jax.ShapeDtypeStruct((S, D_IN_PAD), F32), (tm, 640), _mn)],
                       lambda i, accs, ex, out: out[0].__setitem__(Ellipsis, accs[0][...]))[0]
    p = R["p"]
    R["xc"] = _conv_fwd("rg_conv_fwd", p, 0, W["rg_conv_w"], W["rg_conv_b"], "bias")
    R["bd"] = _rg_bd(W["rg_gate_a_w"], W["rg_gate_x_w"])
    R["rg_prm"] = _rg_prm(W["rg_gate_a_b"], W["rg_gate_x_b"], W["rg_lambda"])
    a_f, b_f, a_b, b_b = _rg_gates_fwd(R["xc"], R["bd"], R["rg_prm"])
    R["a_f"], R["a_b"] = a_f, a_b
    R["h_f"], R["h_b"] = _rg_scan("rg_scan_fwd", a_f, b_f, a_b, b_b)
    zero_b = jnp.zeros((1, RG_W), F32)
    cw = W["gdn_conv_w"]
    R["q"] = _conv_fwd("gdn_conv_q", p, 2, cw[:, 0:512], zero_b, "q")
    R["k"] = _conv_fwd("gdn_conv_k", p, 3, cw[:, 512:1024], zero_b, "k")
    R["v"] = _conv_fwd("gdn_conv_v", p, 4, cw[:, 1024:1536], zero_b, "v")
    R["gdn_prm"] = _gdn_prm(W["gdn_a_log"], W["gdn_dt_bias"])
    R["bg"] = _gdn_prep_fwd(p, R["gdn_prm"])
    R["gcr"] = _gc_rows(R["bg"])
    R["o_f"], R["s_f"], R["t_f"] = _gdn_fwd("gdn_fwd_f", R["q"], R["k"], R["v"], R["bg"], R["gcr"], 0)
    R["o_b"], R["s_b"], R["t_b"] = _gdn_fwd("gdn_fwd_b", R["q"], R["k"], R["v"], R["bg"], R["gcr"], 1)
    R["y"] = _mix_out_fwd(R["h_f"], R["h_b"], R["o_f"], R["o_b"], p, W["gdn_norm"])
    R["x2"] = _fused_mm("out_proj", S, D_MODEL, D_MODEL, tm, D_MODEL, D_MODEL, [(R["y"], "mk"), (W["w_out"], "kn")],
                        [(0, 1, 0)], [(R["x1"], (tm, D_MODEL), _mn)],
                        [(jax.ShapeDtypeStruct((S, D_MODEL), F32), (tm, D_MODEL), _mn)],
                        lambda i, accs, ex, out: out[0].__setitem__(Ellipsis, ex[0][...] + accs[0][...]))[0]
    R["h3"] = _rmsnorm_fwd("rms3", R["x2"], W["ffn2_norm"])
    R["x3"], R["a2"], R["b2"], R["f2"] = _ffn_fwd("ffn2", R["x2"], R["h3"], W["ffn2_w_gate"], W["ffn2_w_up"], W["ffn2_w_down"])
    R["dx3"], R["loss"], R["d_final_norm"] = _loss_head(R["x3"], target, W["final_norm"])
    return R


def _colsum_into(ref, i, val):
    @pl.when(i == 0)
    def _():
        ref[...] = val

    @pl.when(i > 0)
    def _():
        ref[...] += val


def _ffn_bwd(tag, dout, x, g, h, a, b, f, wg, wu, wd):
    S = x.shape[0]
    tm = _tile(S, 512)
    tk_s = _tile(S, 512)

    def epi_act(i, accs, ex, out):
        df = 0.5 * accs[0][...]
        av = ex[0][...].astype(F32)
        bv = ex[1][...].astype(F32)
        s = _sigmoid(av)
        out[0][...] = (df * bv * (s * (1.0 + av * (1.0 - s)))).astype(BF16)
        out[1][...] = (df * av * s).astype(BF16)

    sds = jax.ShapeDtypeStruct((S, D_FF), BF16)
    da, db = _fused_mm(f"{tag}_dact", S, D_FF, D_MODEL, tm, 1408, D_MODEL, [(dout, "mk"), (wd, "nk")], [(0, 1, 0)],
                       [(a, (tm, 1408), _mn), (b, (tm, 1408), _mn)], [(sds, (tm, 1408), _mn)] * 2, epi_act)

    def epi_dx(i, accs, ex, out):
        dx, dgt = _rmsnorm_bwd_tile(accs[0][...], ex[0][...], ex[1][...])
        out[0][...] = ex[2][...] + dx
        _colsum_into(out[1], i, jnp.sum(dgt, axis=0, keepdims=True))

    dx, dg = _fused_mm(f"{tag}_dx", S, D_MODEL, D_FF, tm, D_MODEL, 1408,
                       [(da, "mk"), (wg, "nk"), (db, "mk"), (wu, "nk")], [(0, 1, 0), (2, 3, 0)],
                       [(x, (tm, D_MODEL), _mn), (g, (1, D_MODEL), _row0), (dout, (tm, D_MODEL), _mn)],
                       [(jax.ShapeDtypeStruct((S, D_MODEL), F32), (tm, D_MODEL), _mn),
                        (jax.ShapeDtypeStruct((1, D_MODEL), F32), (1, D_MODEL), _row0)], epi_dx)

    def epi_w2(i, accs, ex, out):
        out[0][...] = accs[0][...].astype(BF16)
        out[1][...] = accs[1][...].astype(BF16)

    sdw = jax.ShapeDtypeStruct((D_MODEL, D_FF), BF16)
    dwg, dwu = _fused_mm(f"{tag}_dw_up", D_MODEL, D_FF, S, D_MODEL, 1408, tk_s,
                         [(h, "km"), (da, "kn"), (db, "kn")], [(0, 1, 0), (0, 2, 1)], [],
                         [(sdw, (D_MODEL, 1408), _mn)] * 2, epi_w2)
    dwd = _fused_mm(f"{tag}_dw_down", D_FF, D_MODEL, S, 1408, D_MODEL, tk_s, [(f, "km"), (dout, "kn")], [(0, 1, 0)], [],
                    [(jax.ShapeDtypeStruct((D_FF, D_MODEL), BF16), (1408, D_MODEL), _mn)],
                    lambda i, accs, ex, out: out[0].__setitem__(Ellipsis, (0.5 * accs[0][...]).astype(BF16)))[0]
    return dx, dg, dwg, dwu, dwd


def _mix_out_bwd(dy, h_f, h_b, o_f, o_b, p, gn):
    S = dy.shape[0]
    ts = _tile(S, 512)
    c0 = math.sqrt(2.0 / math.pi)

    def body(dy_ref, hf, hb, of, ob, gate, z, gn_ref, dhr_ref, dgate_ref, do_ref, dz_ref, dgn_ref):
        i = pl.program_id(0)
        gv = gate[...]
        ge, t = _gelu(gv)
        dy_rg = dy_ref[:, 0:RG_W]
        dhr_ref[...] = dy_rg * ge
        dgelu = 0.5 * (1.0 + t) + 0.5 * gv * (1.0 - t * t) * c0 * (1.0 + 3.0 * 0.044715 * gv * gv)
        dgate_ref[...] = (dy_rg * (hf[...] + hb[...]) * dgelu).astype(BF16)
        o = of[...] + ob[...]
        zv = z[...]
        sig = _sigmoid(zv)
        gnv = gn_ref[...]
        dgn = jnp.zeros((1, GDN_DK), F32)
        for h in range(GDN_H):
            cols = slice(h * GDN_DK, (h + 1) * GDN_DK)
            oh = o[:, cols]
            r = lax.rsqrt(jnp.mean(oh * oh, axis=-1, keepdims=True) + EPS)
            ohat = oh * r
            dyh = dy_ref[:, RG_W + h * GDN_DK:RG_W + (h + 1) * GDN_DK]
            zh = zv[:, cols]
            sh = sig[:, cols]
            dn = dyh * zh * sh
            dz_ref[:, cols] = (dyh * ohat * gnv * (sh * (1.0 + zh * (1.0 - sh)))).astype(BF16)
            dxn = dn * gnv
            do_ref[:, cols] = r * (dxn - ohat * jnp.mean(dxn * ohat, axis=-1, keepdims=True))
            dgn = dgn + jnp.sum(dn * ohat, axis=0, keepdims=True)
        _colsum_into(dgn_ref, i, dgn)

    blk = (ts, RG_W)
    im = lambda i: (i, 0)
    z0 = lambda i: (0, 0)
    ins = [(dy, (ts, D_MODEL), im), (h_f, blk, im), (h_b, blk, im), (o_f, blk, im), (o_b, blk, im),
           (p, blk, lambda i: (i, 1)), (p, blk, lambda i: (i, 5)), (gn, (1, GDN_DK), z0)]
    outs = [(jax.ShapeDtypeStruct((S, RG_W), F32), blk, im), (jax.ShapeDtypeStruct((S, RG_W), BF16), blk, im),
            (jax.ShapeDtypeStruct((S, RG_W), F32), blk, im), (jax.ShapeDtypeStruct((S, RG_W), BF16), blk, im),
            (jax.ShapeDtypeStruct((1, GDN_DK), F32), (1, GDN_DK), z0)]
    return _rows("mix_out_bwd", S, ts, ins, outs, body)


def _rg_scan_adj(name, a_up, b_up, a_dn, b_dn):
    S, C = a_up.shape
    ts = _tile(S, 512)
    n_tiles = S // ts

    def body(au, bu, ad, bd, mu_ref, lam_ref, carry):
        @pl.when(pl.program_id(0) == 0)
        def _():
            carry[...] = jnp.zeros_like(carry)

        def step(t, c):
            cu, cd = c
            mu = bu[pl.ds(t, 1), :] + cu
            mu_ref[pl.ds(t, 1), :] = mu
            cu = au[pl.ds(t, 1), :] * mu
            tb = ts - 1 - t
            lam = bd[pl.ds(tb, 1), :] + cd
            lam_ref[pl.ds(tb, 1), :] = lam
            cd = ad[pl.ds(tb, 1), :] * lam
            return cu, cd

        cu, cd = lax.fori_loop(0, ts, step, (carry[0:1, :], carry[1:2, :]), unroll=8)
        carry[0:1, :] = cu
        carry[1:2, :] = cd

    fw = lambda i: (i, 0)
    bw = lambda i: (n_tiles - 1 - i, 0)
    sds = jax.ShapeDtypeStruct((S, C), F32)
    return _rows(name, S, ts,
                 [(a_up, (ts, C), fw), (b_up, (ts, C), fw), (a_dn, (ts, C), bw), (b_dn, (ts, C), bw)],
                 [(sds, (ts, C), fw), (sds, (ts, C), bw)], body, scratch=[pltpu.VMEM((8, C), F32)])


def _halo_ex(arr, S, tm, width):
    per = tm // HALO
    last = S // HALO - 1
    return [
        (arr, (tm, width), lambda i, j: (i, 0)),
        (arr, (HALO, width), lambda i, j: (jnp.maximum(i * per - 1, 0), 0)),
        (arr, (HALO, width), lambda i, j: (jnp.minimum((i + 1) * per, last), 0)),
    ]


def _rg_gates_bwd(xc, bd, prm, lam_f, lam_b, h_f, h_b):
    S = xc.shape[0]
    tm = _tile(S, 256)
    n_tiles = S // tm

    def epi(i, accs, ex, out):
        pre = accs[0][...]
        xv = ex[0][...]
        prm_ref = ex[1]
        lams = (ex[2][...], ex[3][...])
        hprev = (_shift(_ext(ex[4], ex[5], ex[6], i, n_tiles), -1, tm),
                 _shift(_ext(ex[7], ex[8], ex[9], i, n_tiles), 1, tm))
        dxc = jnp.zeros_like(xv)
        rows = []
        dlam_rows = []
        for d in range(2):
            r, ig, sp, a, sq = _rg_gate_terms(pre, xv, prm_ref, d)
            lam = lams[d]
            da = lam * hprev[d]
            di = lam * sq * xv
            dxc = dxc + lam * sq * ig
            dsq = lam * ig * xv
            dlog_a = da * a - dsq * (a * a) / sq
            dpre_r = dlog_a * (-RG_C * sp) * r * (1.0 - r)
            dpre_i = di * ig * (1.0 - ig)
            out[0][:, d * 1024:d * 1024 + RG_W] = dpre_r.astype(BF16)
            out[0][:, d * 1024 + RG_W:(d + 1) * 1024] = dpre_i.astype(BF16)
            rows += [jnp.sum(dpre_r, axis=0, keepdims=True), jnp.sum(dpre_i, axis=0, keepdims=True)]
            dsp = jnp.sum(dlog_a * (-RG_C * r), axis=0, keepdims=True)
            dlam_rows.append(-dsp * _sigmoid(-prm_ref[4 + d:5 + d, :]))
        out[1][...] = dxc
        zero = jnp.zeros((2, RG_W), F32)
        _colsum_into(out[2], i, jnp.concatenate(rows + dlam_rows + [zero], axis=0))

    blk = (tm, RG_W)
    im = lambda i, j: (i, 0)
    extras = ([(xc, blk, im), (prm, (8, RG_W), _row0), (lam_f, blk, im), (lam_b, blk, im)]
              + _halo_ex(h_f, S, tm, RG_W) + _halo_ex(h_b, S, tm, RG_W))
    outs = [(jax.ShapeDtypeStruct((S, 4 * RG_W), BF16), (tm, 4 * RG_W), im),
            (jax.ShapeDtypeStruct((S, RG_W), F32), blk, im),
            (jax.ShapeDtypeStruct((8, RG_W), F32), (8, RG_W), _row0)]
    return _fused_mm("rg_gates_bwd", S, 4 * RG_W, RG_W, tm, 4 * RG_W, RG_W, [(xc, "mk"), (bd, "kn")], [(0, 1, 0)],
                     extras, outs, epi)


def _roll_rows(ext, off):
    if off == 0:
        return ext
    return pltpu.roll(ext, (-off) % ext.shape[0], 0)


def _conv_bwd(name, p, colblk, w, grads, mode):
    S = p.shape[0]
    ts = _tile(S, 512)
    n_tiles = S // ts
    C = w.shape[1]
    ng = len(grads)

    def body(*refs):
        p_refs = refs[0:3]
        g_refs = refs[3:3 + 3 * ng]
        w_ref = refs[3 + 3 * ng]
        dx_ref, dw_ref, db_ref = refs[4 + 3 * ng:]
        i = pl.program_id(0)
        ext_p = _ext(*p_refs, i, n_tiles)
        dn = _ext(*g_refs[0:3], i, n_tiles)
        for gi in range(1, ng):
            dn = dn + _ext(*g_refs[3 * gi:3 * gi + 3], i, n_tiles)
        if mode == "bias":
            dc = dn
        else:
            c = None
            for j in range(CONV_W):
                term = w_ref[j:j + 1, :] * _roll_rows(ext_p, j - 2)
                c = term if c is None else c + term
            sig = _sigmoid(c)
            s = c * sig
            if mode in ("q", "k"):
                scale = GDN_DK ** -0.5 if mode == "q" else 1.0
                parts = []
                for h in range(GDN_H):
                    cols = slice(h * GDN_DK, (h + 1) * GDN_DK)
                    sh = s[:, cols]
                    dnh = dn[:, cols]
                    rinv = lax.rsqrt(jnp.sum(sh * sh, axis=-1, keepdims=True) + EPS)
                    parts.append(scale * rinv * (dnh - sh * (rinv * rinv) * jnp.sum(dnh * sh, axis=-1, keepdims=True)))
                ds = jnp.concatenate(parts, axis=-1)
            else:
                ds = dn
            dc = ds * (sig * (1.0 + c * (1.0 - sig)))
        dx = None
        for j in range(CONV_W):
            term = w_ref[j:j + 1, :] * _shift(dc, 2 - j, ts)
            dx = term if dx is None else dx + term
        dx_ref[...] = dx.astype(BF16)
        dc_main = dc[HALO:HALO + ts]
        dw = jnp.concatenate([jnp.sum(dc_main * _shift(ext_p, j - 2, ts), axis=0, keepdims=True)
                              for j in range(CONV_W)], axis=0)
        _colsum_into(dw_ref, i, dw)
        _colsum_into(db_ref, i, jnp.sum(dc_main, axis=0, keepdims=True))

    ins = _halo_ins(p, S, ts, C, colblk)
    for garr in grads:
        ins += _halo_ins(garr, S, ts, C, 0)
    ins += [(w, (CONV_W, C), lambda i: (0, 0))]
    z0 = lambda i: (0, 0)
    outs = [(jax.ShapeDtypeStruct((S, C), BF16), (ts, C), lambda i: (i, 0)),
            (jax.ShapeDtypeStruct((CONV_W, C), F32), (CONV_W, C), z0),
            (jax.ShapeDtypeStruct((1, C), F32), (1, C), z0)]
    return _rows(name, S, ts, ins, outs, body)


def _gdn_bwd(name, q, k, v, bg, gcr, s_st, t_st, do, d):
    S = q.shape[0]
    ts = _tile(S, 512)
    n_tiles = S // ts
    ncb = ts // CHUNK
    rev = d == 1
    W = GDN_H * GDN_DK

    def body(q_ref, k_ref, v_ref, bg_ref, gcr_ref, s_ref, t_ref, do_ref, dq_ref, dk_ref, dv_ref, dbg_ref, dstate):
        @pl.when(pl.program_id(0) == 0)
        def _():
            dstate[...] = jnp.zeros_like(dstate)

        ri, ci = _tri_masks()
        lane = lax.broadcasted_iota(jnp.int32, (CHUNK, 128), 1)
        rowi = lax.broadcasted_iota(jnp.int32, (CHUNK, 1), 0)
        ones = jnp.ones((CHUNK, 128), F32)
        last = 0 if rev else CHUNK - 1

        def chunk(cc, carry):
            c = cc if rev else ncb - 1 - cc
            acc_bg = jnp.zeros((CHUNK, 128), F32)
            for h in range(GDN_H):
                m = _gdn_chunk_terms(q_ref, k_ref, v_ref, bg_ref, gcr_ref, c, h, d, rev, ri, ci)
                t_mat = t_ref[c, h]
                st = s_ref[c, h]
                dsn = dstate[h]
                doh = do_ref[m["rows"], m["cols"]]
                u = _dot(t_mat, m["vb"], 1, 0, HI)
                w = _dot(t_mat, m["kbg"], 1, 0, HI)
                vn = u - _dot(w, st, 1, 0, HI)
                dvn = _dot(m["a_mat"], doh, 0, 0, HI) + _dot(m["kd"], dsn, 1, 0, HI)
                dqd = _dot(doh, st, 1, 1, HI)
                d_a = _dot(doh, vn, 1, 1, HI)
                dkd = _dot(vn, dsn, 1, 1, HI)
                dcd = jnp.sum(jnp.sum(st * dsn, axis=1, keepdims=True), axis=0, keepdims=True)
                dstate[h] = _dot(m["qd"], doh, 0, 0, HI) + m["cd"] * dsn - _dot(w, dvn, 0, 0, HI)
                dw = -_dot(dvn, st, 1, 1, HI)
                d_t = _dot(dvn, m["vb"], 1, 1, HI) + _dot(dw, m["kbg"], 1, 1, HI)
                dvb = _dot(t_mat, dvn, 0, 0, HI)
                dkbg = _dot(t_mat, dw, 0, 0, HI)
                d_l = -_dot(t_mat, _dot(d_t, t_mat, 1, 1, HI), 0, 0, HI)
                d_l = jnp.where(m["strict"], d_l, 0.0)
                mm = d_l * m["dm"]
                nn = d_a * m["dm"]
                dkb = _dot(mm, m["kh"], 1, 0, HI) + dkbg * m["eg"]
                dk = (_dot(mm, m["kb"], 0, 0, HI) + _dot(nn, m["qh"], 0, 0, HI) + dkd * m["egl"] + dkb * m["beta"])
                dq = _dot(nn, m["kh"], 1, 0, HI) + dqd * m["eg"]
                e = d_l * m["l_mat"] + nn * m["qk"]
                rs = jnp.sum(e, axis=1, keepdims=True)
                cs = _dot(e, ones, 0, 0, HI)[:, 0:1]
                dkd_kd = dkd * m["kd"]
                dgc = (rs - cs + jnp.sum(dqd * m["qd"], axis=1, keepdims=True) - jnp.sum(dkd_kd, axis=1, keepdims=True)
                       + jnp.sum(dkbg * m["kbg"], axis=1, keepdims=True))
                dgl = jnp.sum(jnp.sum(dkd_kd, axis=1, keepdims=True), axis=0, keepdims=True) + dcd * m["cd"]
                dgc = dgc + jnp.where(rowi == last, dgl, 0.0)
                dbeta = jnp.sum(dkb * m["kh"], axis=1, keepdims=True) + jnp.sum(dvb * m["vh"], axis=1, keepdims=True)
                dq_ref[m["rows"], m["cols"]] = dq
                dk_ref[m["rows"], m["cols"]] = dk
                dv_ref[m["rows"], m["cols"]] = dvb * m["beta"]
                col = m["col"]
                acc_bg = acc_bg + jnp.where(lane == col, dbeta, 0.0) + jnp.where(lane == 8 + col, dgc, 0.0)
            dbg_ref[m["rows"], :] = acc_bg
            return carry

        lax.fori_loop(0, ncb, chunk, 0)

    tix = (lambda i: i) if rev else (lambda i: n_tiles - 1 - i)
    blk = (ts, W)
    im = lambda i: (tix(i), 0)
    im4 = lambda i: (tix(i), 0, 0, 0)
    ins = [(q, blk, im), (k, blk, im), (v, blk, im), (bg, (ts, 128), im), (gcr, (ncb, 8, CHUNK), lambda i: (tix(i), 0, 0)),
           (s_st, (ncb, GDN_H, GDN_DK, GDN_DK), im4), (t_st, (ncb, GDN_H, CHUNK, CHUNK), im4), (do, blk, im)]
    sds = jax.ShapeDtypeStruct((S, W), F32)
    outs = [(sds, blk, im), (sds, blk, im), (sds, blk, im), (jax.ShapeDtypeStruct((S, 128), F32), (ts, 128), im)]
    return _rows(name, S, ts, ins, outs, body, scratch=[pltpu.VMEM((GDN_H, GDN_DK, GDN_DK), F32)])


def _gdn_prep_bwd(dbg_f, dbg_b, p, prm):
    S = p.shape[0]
    ts = _tile(S, 512)

    def body(df_ref, db_ref, p_ref, prm_ref, dba_ref, dprm_ref):
        i = pl.program_id(0)
        raw = p_ref[...]
        dbg = df_ref[...] + db_ref[...]
        lane = lax.broadcasted_iota(jnp.int32, (1, 128), 1)
        is_g = (lane >= 8) & (lane < 16)
        ea = jnp.exp(prm_ref[0:1, :])
        arg = raw + prm_ref[1:2, :]
        g = jnp.where(is_g, -ea * _softplus(arg), 0.0)
        beta = _sigmoid(raw)
        dgc = jnp.where(is_g, dbg, 0.0)
        ri, ci = _tri_masks()
        lower = (ri >= ci).astype(F32)
        upper = (ri <= ci).astype(F32)
        dgs = []
        for c in range(ts // CHUNK):
            ch = dgc[c * CHUNK:(c + 1) * CHUNK]
            dgs.append(jnp.where(lane < 12, _dot(upper, ch, 1, 0, HI), _dot(lower, ch, 1, 0, HI)))
        dg = jnp.concatenate(dgs, axis=0)
        dalpha = jnp.where(is_g, dg * (-ea) * _sigmoid(arg), 0.0)
        dba_ref[...] = jnp.where(lane < 8, dbg * beta * (1.0 - beta), dalpha).astype(BF16)
        rows = jnp.concatenate([jnp.sum(dg * g, axis=0, keepdims=True), jnp.sum(dalpha, axis=0, keepdims=True),
                                jnp.zeros((6, 128), F32)], axis=0)
        _colsum_into(dprm_ref, i, rows)

    im = lambda i: (i, 0)
    z0 = lambda i: (0, 0)
    return _rows("gdn_prep_bwd", S, ts,
                 [(dbg_f, (ts, 128), im), (dbg_b, (ts, 128), im), (p, (ts, 128), lambda i: (i, COL_BA // 128)),
                  (prm, (8, 128), z0)],
                 [(jax.ShapeDtypeStruct((S, 128), BF16), (ts, 128), im), (jax.ShapeDtypeStruct((8, 128), F32), (8, 128), z0)],
                 body)


def _mm_plain(name, M, N, K, tm, tn, tk, a, am, b, bm, dtype):
    return _fused_mm(name, M, N, K, tm, tn, tk, [(a, am), (b, bm)], [(0, 1, 0)], [],
                     [(jax.ShapeDtypeStruct((M, N), dtype), (tm, tn), _mn)],
                     lambda i, accs, ex, out: out[0].__setitem__(Ellipsis, accs[0][...].astype(dtype)))[0]


def _layer_bwd(x0, W, R):
    S = x0.shape[0]
    tm = _tile(S, 512)
    tk_s = _tile(S, 512)
    G = {}
    dx2, G["ffn2_norm"], G["ffn2_w_gate"], G["ffn2_w_up"], G["ffn2_w_down"] = _ffn_bwd(
        "ffn2b", R["dx3"], R["x2"], W["ffn2_norm"], R["h3"], R["a2"], R["b2"], R["f2"],
        W["ffn2_w_gate"], W["ffn2_w_up"], W["ffn2_w_down"])
    G["w_out"] = _mm_plain("dw_out", D_MODEL, D_MODEL, S, D_MODEL, D_MODEL, tk_s, R["y"], "km", dx2, "kn", BF16)
    dy = _mm_plain("dy_mix", S, D_MODEL, D_MODEL, tm, D_MODEL, D_MODEL, dx2, "mk", W["w_out"], "nk", F32)
    p = R["p"]
    dhr, dgate, do, dz, G["gdn_norm"] = _mix_out_bwd(dy, R["h_f"], R["h_b"], R["o_f"], R["o_b"], p, W["gdn_norm"])
    lam_b, lam_f = _rg_scan_adj("rg_scan_bwd", R["a_b"], dhr, R["a_f"], dhr)
    dpre, dxc_direct, d_rgprm = _rg_gates_bwd(R["xc"], R["bd"], R["rg_prm"], lam_f, lam_b, R["h_f"], R["h_b"])
    tmg = _tile(S, 512)
    dxc = _fused_mm("rg_dxc", S, RG_W, 4 * RG_W, tmg, RG_W, 4 * RG_W, [(dpre, "mk"), (R["bd"], "nk")], [(0, 1, 0)],
                    [(dxc_direct, (tmg, RG_W), _mn)], [(jax.ShapeDtypeStruct((S, RG_W), F32), (tmg, RG_W), _mn)],
                    lambda i, accs, ex, out: out[0].__setitem__(Ellipsis, ex[0][...] + accs[0][...]))[0]
    d_bd = _mm_plain("rg_dbd", RG_W, 4 * RG_W, S, RG_W, 4 * RG_W, tk_s, R["xc"], "km", dpre, "kn", F32)
    dx_rg, G["rg_conv_w"], G["rg_conv_b"] = _conv_bwd("rg_conv_bwd", p, 0, W["rg_conv_w"], [dxc], "bias")
    blocks = jnp.einsum("nigmj,nm->gnij", d_bd.reshape(RG_BLOCKS, RG_BLOCK, 4, RG_BLOCKS, RG_BLOCK),
                        jnp.eye(RG_BLOCKS, dtype=F32))
    G["rg_gate_a_w"] = jnp.stack([blocks[0], blocks[2]])
    G["rg_gate_x_w"] = jnp.stack([blocks[1], blocks[3]])
    G["rg_gate_a_b"] = jnp.stack([d_rgprm[0], d_rgprm[2]])
    G["rg_gate_x_b"] = jnp.stack([d_rgprm[1], d_rgprm[3]])
    G["rg_lambda"] = d_rgprm[4:6]
    gf = _gdn_bwd("gdn_bwd_f", R["q"], R["k"], R["v"], R["bg"], R["gcr"], R["s_f"], R["t_f"], do, 0)
    gb = _gdn_bwd("gdn_bwd_b", R["q"], R["k"], R["v"], R["bg"], R["gcr"], R["s_b"], R["t_b"], do, 1)
    cw = W["gdn_conv_w"]
    dpq, dwq, _ = _conv_bwd("gdn_conv_q_bwd", p, 2, cw[:, 0:512], [gf[0], gb[0]], "q")
    dpk, dwk, _ = _conv_bwd("gdn_conv_k_bwd", p, 3, cw[:, 512:1024], [gf[1], gb[1]], "k")
    dpv, dwv, _ = _conv_bwd("gdn_conv_v_bwd", p, 4, cw[:, 1024:1536], [gf[2], gb[2]], "v")
    G["gdn_conv_w"] = jnp.concatenate([dwq, dwk, dwv], axis=1)
    dba, d_gprm = _gdn_prep_bwd(gf[3], gb[3], p, R["gdn_prm"])
    G["gdn_a_log"] = d_gprm[0, 8:16].reshape(2, GDN_H)
    G["gdn_dt_bias"] = d_gprm[1, 8:16].reshape(2, GDN_H)
    dp = jnp.concatenate([dx_rg, dgate, dpq, dpk, dpv, dz, dba], axis=1)
    G["w_in"] = _mm_plain("dw_in", D_MODEL, D_IN_PAD, S, D_MODEL, 640, tk_s, R["h2"], "km", dp, "kn", BF16)

    def epi_dx1(i, accs, ex, out):
        dx, dgt = _rmsnorm_bwd_tile(accs[0][...], ex[0][...], ex[1][...])
        out[0][...] = ex[2][...] + dx
        _colsum_into(out[1], i, jnp.sum(dgt, axis=0, keepdims=True))

    dx1, G["mix_norm"] = _fused_mm(
        "mix_dx", S, D_MODEL, D_IN_PAD, tm, D_MODEL, 640, [(dp, "mk"), (W["w_in"], "nk")], [(0, 1, 0)],
        [(R["x1"], (tm, D_MODEL), _mn), (W["mix_norm"], (1, D_MODEL), _row0), (dx2, (tm, D_MODEL), _mn)],
        [(jax.ShapeDtypeStruct((S, D_MODEL), F32), (tm, D_MODEL), _mn),
         (jax.ShapeDtypeStruct((1, D_MODEL), F32), (1, D_MODEL), _row0)], epi_dx1)
    dx0, G["ffn1_norm"], G["ffn1_w_gate"], G["ffn1_w_up"], G["ffn1_w_down"] = _ffn_bwd(
        "ffn1b", dx1, x0, W["ffn1_norm"], R["h1"], R["a1"], R["b1"], R["f1"],
        W["ffn1_w_gate"], W["ffn1_w_up"], W["ffn1_w_down"])
    G["final_norm"] = R["d_final_norm"]
    return dx0, G


def _mesh_pos():
    x, y, c = lax.axis_index("x"), lax.axis_index("y"), lax.axis_index("c")
    return x, y, c, 4 * x + 2 * y + c


def _peer(x, y, c, r):
    px = 1 - x if r & 4 else x
    py = 1 - y if r & 2 else y
    pc = 1 - c if r & 1 else c
    return (px, py, pc), 4 * px + 2 * py + pc


def _exchange(name, arrays, all_to_all):
    n = len(arrays)

    def body(*refs):
        ins = refs[:n]
        outs = refs[n:2 * n]
        send_sems, recv_sems, local_sems = refs[2 * n:]
        x, y, c, me = _mesh_pos()
        started = []
        for a in range(n):
            src_local = ins[a].at[me] if all_to_all else ins[a]
            loc = pltpu.make_async_copy(src_local, outs[a].at[me], local_sems.at[a])
            loc.start()
            started.append(loc)
        sends = []
        for a in range(n):
            for r in range(1, N_DEV):
                peer, peer_idx = _peer(x, y, c, r)
                src = ins[a].at[peer_idx] if all_to_all else ins[a]
                cp = pltpu.make_async_remote_copy(
                    src_ref=src, dst_ref=outs[a].at[me], send_sem=send_sems.at[a * 7 + r - 1],
                    recv_sem=recv_sems.at[a * 7 + r - 1], device_id=peer, device_id_type=pl.DeviceIdType.MESH)
                cp.start()
                sends.append(cp)
        for a in range(n):
            for r in range(1, N_DEV):
                peer, peer_idx = _peer(x, y, c, r)
                src = ins[a].at[peer_idx] if all_to_all else ins[a]
                pltpu.make_async_remote_copy(
                    src_ref=src, dst_ref=outs[a].at[peer_idx], send_sem=send_sems.at[a * 7 + r - 1],
                    recv_sem=recv_sems.at[a * 7 + r - 1], device_id=peer, device_id_type=pl.DeviceIdType.MESH).wait_recv()
        for cp in sends:
            cp.wait_send()
        for loc in started:
            loc.wait()

    any_spec = pl.BlockSpec(memory_space=pl.ANY)
    out_shape = [jax.ShapeDtypeStruct(a.shape if all_to_all else (N_DEV,) + a.shape, a.dtype) for a in arrays]
    return pl.pallas_call(
        body, name=name, in_specs=[any_spec] * n, out_specs=[any_spec] * n, out_shape=out_shape,
        scratch_shapes=[pltpu.SemaphoreType.DMA((7 * n,)), pltpu.SemaphoreType.DMA((7 * n,)),
                        pltpu.SemaphoreType.DMA((n,))],
        compiler_params=pltpu.CompilerParams(has_side_effects=True),
    )(*arrays)


def _adamw_math(w, g, m, v):
    m2 = ADAM_B1 * m + (1.0 - ADAM_B1) * g
    v2 = ADAM_B2 * v + (1.0 - ADAM_B2) * (g * g)
    m_hat = m2 / (1.0 - ADAM_B1 ** ADAM_STEP)
    v_hat = v2 / (1.0 - ADAM_B2 ** ADAM_STEP)
    delta = -ADAM_LR * (m_hat / (jnp.sqrt(v_hat) + ADAM_EPS) + ADAM_WD * w)
    return delta, m2, v2


def _adamw_slabs(name, slabs, w, m, v, tr):
    R, C = w.shape

    def body(s_ref, w_ref, m_ref, v_ref, g_ref, d_ref, m2_ref, v2_ref):
        g = s_ref[0].astype(F32)
        for s in range(1, N_DEV):
            g = g + s_ref[s].astype(F32)
        delta, m2, v2 = _adamw_math(w_ref[...], g, m_ref[...], v_ref[...])
        g_ref[...] = g
        d_ref[...] = delta
        m2_ref[...] = m2
        v2_ref[...] = v2

    im = lambda i: (i, 0)
    sds = jax.ShapeDtypeStruct((R, C), F32)
    ins = [(slabs, (N_DEV, tr, C), lambda i: (0, i, 0)), (w, (tr, C), im), (m, (tr, C), im), (v, (tr, C), im)]
    return _rows(name, R, tr, ins, [(sds, (tr, C), im)] * 4, body)


def _sum_slots(name, slots):
    _, R, C = slots.shape

    def body(s_ref, o_ref):
        g = s_ref[0]
        for s in range(1, N_DEV):
            g = g + s_ref[s]
        o_ref[...] = g

    return _rows(name, R, R, [(slots, (N_DEV, R, C), lambda i: (0, 0, 0))],
                 [(jax.ShapeDtypeStruct((R, C), F32), (R, C), lambda i: (0, 0))], body)[0]


def _adamw_packed(name, g, w, m, v):
    R, C = g.shape

    def body(g_ref, w_ref, m_ref, v_ref, d_ref, m2_ref, v2_ref):
        delta, m2, v2 = _adamw_math(w_ref[...], g_ref[...], m_ref[...], v_ref[...])
        d_ref[...] = delta
        m2_ref[...] = m2
        v2_ref[...] = v2

    im = lambda i: (0, 0)
    sds = jax.ShapeDtypeStruct((R, C), F32)
    return _rows(name, R, R, [(a, (R, C), im) for a in (g, w, m, v)], [(sds, (R, C), im)] * 3, body)


def _pack(arrays):
    rows = []
    for a in arrays:
        flat = a.reshape(-1).astype(F32)
        pad = (-flat.shape[0]) % 128
        rows.append(jnp.pad(flat, (0, pad)).reshape(-1, 128))
    out = jnp.concatenate(rows, axis=0)
    return jnp.pad(out, ((0, (-out.shape[0]) % 8), (0, 0)))


def _unpack(packed, shapes):
    lead = packed.shape[:-2]
    outs = []
    r = 0
    for shp in shapes:
        n = math.prod(shp)
        nr = -(-n // 128)
        flat = packed[..., r:r + nr, :].reshape(lead + (nr * 128,))[..., :n]
        outs.append(flat.reshape(lead + tuple(shp)))
        r += nr
    return outs


BIG = ["ffn1_w_gate", "ffn1_w_up", "ffn1_w_down", "w_in", "w_out", "ffn2_w_gate", "ffn2_w_up", "ffn2_w_down"]
COL_SHARDED = {"ffn1_w_gate", "ffn1_w_up", "w_in", "ffn2_w_gate", "ffn2_w_up"}
SMALL_SHARDED = ["rg_conv_w", "rg_gate_a_b", "rg_gate_x_b", "rg_lambda", "gdn_conv_w"]
WEIGHTS = ["ffn1_norm", "ffn1_w_gate", "ffn1_w_up", "ffn1_w_down", "mix_norm", "w_in", "w_out", "rg_conv_w", "rg_conv_b",
           "rg_gate_a_w", "rg_gate_a_b", "rg_gate_x_w", "rg_gate_x_b", "rg_lambda", "gdn_conv_w", "gdn_a_log",
           "gdn_dt_bias", "gdn_norm", "ffn2_norm", "ffn2_w_gate", "ffn2_w_up", "ffn2_w_down", "final_norm"]
SMALL = [n for n in WEIGHTS if n not in BIG]
ROW_VECTORS = {"ffn1_norm", "mix_norm", "ffn2_norm", "gdn_norm", "rg_conv_b", "final_norm"}
ROW_TILE = {"ffn1_w_gate": 256, "ffn1_w_up": 256, "ffn1_w_down": 176, "w_in": 256, "w_out": 64,
            "ffn2_w_gate": 256, "ffn2_w_up": 256, "ffn2_w_down": 176}


def _unshard_cols(g):
    return g.transpose(1, 0, 2).reshape(g.shape[1], N_DEV * g.shape[2])


def _to_slabs(name, g):
    if name in COL_SHARDED:
        r, ctot = g.shape
        return g.reshape(r, N_DEV, ctot // N_DEV).transpose(1, 0, 2)
    return g.reshape(N_DEV, g.shape[0] // N_DEV, g.shape[1])


def _step(x, target, w, m, v):
    _, _, _, me = _mesh_pos()
    small_shards = [w[n] for n in SMALL_SHARDED]
    gathered = _exchange("gather_weights", [w[n].astype(BF16) for n in BIG] + [_pack(small_shards)], False)
    W = {}
    for n, gth in zip(BIG, gathered[:len(BIG)]):
        W[n] = _unshard_cols(gth) if n in COL_SHARDED else gth.reshape(-1, gth.shape[-1])
    W["w_in"] = jnp.pad(W["w_in"], ((0, 0), (0, D_IN_PAD - D_IN)))
    for n, gth in zip(SMALL_SHARDED, _unpack(gathered[-1], [s.shape for s in small_shards])):
        W[n] = jnp.moveaxis(gth, 0, -2).reshape(gth.shape[1:-1] + (N_DEV * gth.shape[-1],))
    for n in SMALL:
        if n not in SMALL_SHARDED:
            W[n] = w[n]
    R = _layer_fwd(x, target, W)
    grad_x, G = _layer_bwd(x, W, R)
    loss = lax.psum(R["loss"][0, 0], ("x", "y", "c"))
    G["w_in"] = G["w_in"][:, :D_IN]
    received = _exchange("scatter_grads", [_to_slabs(n, G[n]) for n in BIG], True)
    out = {}
    for n, slabs in zip(BIG, received):
        out[n] = _adamw_slabs(f"adamw_{n}", slabs, w[n], m[n], v[n], ROW_TILE[n])
    full_shapes = [G[n].shape for n in SMALL]
    slots = _exchange("gather_small_grads", [_pack([G[n] for n in SMALL])], False)[0]
    reduced = dict(zip(SMALL, _unpack(_sum_slots("sum_small_grads", slots), full_shapes)))
    g_small = []
    for n in SMALL:
        g = reduced[n]
        if n in SMALL_SHARDED:
            per = g.shape[-1] // N_DEV
            g = lax.dynamic_slice_in_dim(g, me * per, per, axis=g.ndim - 1)
        g_small.append(g.reshape(w[n].shape))
    shapes = [w[n].shape for n in SMALL]
    d_p, m_p, v_p = _adamw_packed("adamw_small", _pack(g_small), _pack([w[n] for n in SMALL]),
                                  _pack([m[n] for n in SMALL]), _pack([v[n] for n in SMALL]))
    for n, g, d_, m_, v_ in zip(SMALL, g_small, _unpack(d_p, shapes), _unpack(m_p, shapes), _unpack(v_p, shapes)):
        out[n] = (g, d_, m_, v_)
    return loss, grad_x, out


def kernel(x, ffn1_norm, ffn1_w_gate, ffn1_w_up, ffn1_w_down, mix_norm, w_in, w_out, rg_conv_w, rg_conv_b, rg_gate_a_w, rg_gate_a_b, rg_gate_x_w, rg_gate_x_b, rg_lambda, gdn_conv_w, gdn_a_log, gdn_dt_bias, gdn_norm, ffn2_norm, ffn2_w_gate, ffn2_w_up, ffn2_w_down, final_norm, loss_target, m_ffn1_norm, m_ffn1_w_gate, m_ffn1_w_up, m_ffn1_w_down, m_mix_norm, m_w_in, m_w_out, m_rg_conv_w, m_rg_conv_b, m_rg_gate_a_w, m_rg_gate_a_b, m_rg_gate_x_w, m_rg_gate_x_b, m_rg_lambda, m_gdn_conv_w, m_gdn_a_log, m_gdn_dt_bias, m_gdn_norm, m_ffn2_norm, m_ffn2_w_gate, m_ffn2_w_up, m_ffn2_w_down, m_final_norm, v_ffn1_norm, v_ffn1_w_gate, v_ffn1_w_up, v_ffn1_w_down, v_mix_norm, v_w_in, v_w_out, v_rg_conv_w, v_rg_conv_b, v_rg_gate_a_w, v_rg_gate_a_b, v_rg_gate_x_w, v_rg_gate_x_b, v_rg_lambda, v_gdn_conv_w, v_gdn_a_log, v_gdn_dt_bias, v_gdn_norm, v_ffn2_norm, v_ffn2_w_gate, v_ffn2_w_up, v_ffn2_w_down, v_final_norm):
    args = dict(locals())
    orig_shapes = {n: args[n].shape for n in WEIGHTS}

    def local(prefix):
        d = {}
        for n in WEIGHTS:
            a = args[prefix + n]
            d[n] = a.reshape(1, -1) if n in ROW_VECTORS else a[0]
        return d

    loss, grad_x, out = _step(x[0], loss_target[0], local(""), local("m_"), local("v_"))
    res = [loss, grad_x[None]]
    for k in range(4):
        res += [out[n][k].reshape(orig_shapes[n]) for n in WEIGHTS]
    return tuple(res)
```

```python
import functools
import math

import jax
import jax.numpy as jnp
from jax import lax
from jax.experimental import pallas as pl
from jax.experimental.pallas import tpu as pltpu

F32, BF16 = jnp.float32, jnp.bfloat16

D_MODEL = 1024
D_FF = 2816
RG_W = 512
RG_BLOCKS = 8
RG_BLOCK = 64
RG_C = 8.0
CONV_W = 4
GDN_H = 4
GDN_DK = 128
CHUNK = 64
EPS = 1e-6
D_IN = 3088
D_IN_PAD = 3200
COL_BA = 3072
N_DEV = 8
HALO = 8
VMEM_LIMIT = 48 * 1024 * 1024

ADAM_LR = 0.001
ADAM_B1 = 0.9
ADAM_B2 = 0.999
ADAM_EPS = 1e-08
ADAM_WD = 0.01
ADAM_STEP = 10

HI = lax.Precision.HIGHEST


def _cp(n):
    return pltpu.CompilerParams(dimension_semantics=("arbitrary",) * n, vmem_limit_bytes=VMEM_LIMIT)


def _tile(n, pref):
    return min(n, pref)


def _sigmoid(x):
    return jax.nn.sigmoid(x)


def _softplus(x):
    return jnp.maximum(x, 0.0) + jnp.log(1.0 + jnp.exp(-jnp.abs(x)))


def _dot(a, b, ca, cb, prec=None):
    return lax.dot_general(a, b, (((ca,), (cb,)), ((), ())), preferred_element_type=F32, precision=prec)


def _fused_mm(name, M, N, K, tm, tn, tk, ops, pairs, extras, outs, epilogue):
    nm, nn, nk = M // tm, N // tn, K // tk
    assert nm * tm == M and nn * tn == N and nk * tk == K, (name, M, N, K, tm, tn, tk)
    spec_of = {
        "mk": pl.BlockSpec((tm, tk), lambda i, j, k: (i, k)),
        "km": pl.BlockSpec((tk, tm), lambda i, j, k: (k, i)),
        "kn": pl.BlockSpec((tk, tn), lambda i, j, k: (k, j)),
        "nk": pl.BlockSpec((tn, tk), lambda i, j, k: (j, k)),
    }
    in_specs = [spec_of[m] for _, m in ops]
    in_specs += [pl.BlockSpec(bs, lambda i, j, k, im=im: im(i, j)) for _, bs, im in extras]
    out_specs = [pl.BlockSpec(bs, lambda i, j, k, im=im: im(i, j)) for _, bs, im in outs]
    n_ops, n_ex, n_out = len(ops), len(extras), len(outs)
    n_acc = 1 + max(g for _, _, g in pairs)
    modes = [m for _, m in ops]

    def body(*refs):
        op_refs = refs[:n_ops]
        ex_refs = refs[n_ops:n_ops + n_ex]
        out_refs = refs[n_ops + n_ex:n_ops + n_ex + n_out]
        accs = refs[n_ops + n_ex + n_out:]
        i = pl.program_id(0)
        k = pl.program_id(2)

        @pl.when(k == 0)
        def _():
            for a in accs:
                a[...] = jnp.zeros_like(a)

        vals = [r[...].astype(BF16) for r in op_refs]
        for ia, ib, g in pairs:
            ca = 1 if modes[ia] == "mk" else 0
            cb = 0 if modes[ib] == "kn" else 1
            accs[g][...] += _dot(vals[ia], vals[ib], ca, cb)

        @pl.when(k == nk - 1)
        def _():
            epilogue(i, accs, ex_refs, out_refs)

    res = pl.pallas_call(
        body, name=name, grid=(nm, nn, nk), in_specs=in_specs, out_specs=out_specs,
        out_shape=[o for o, _, _ in outs],
        scratch_shapes=[pltpu.VMEM((tm, tn), F32)] * n_acc,
        compiler_params=_cp(3),
    )(*[a for a, _ in ops], *[a for a, _, _ in extras])
    return res


def _mn(i, j):
    return (i, j)


def _row0(i, j):
    return (0, 0)


def _rows(name, S, ts, ins, outs, body, scratch=()):
    return pl.pallas_call(
        body, name=name, grid=(S // ts,),
        in_specs=[pl.BlockSpec(bs, im) for _, bs, im in ins],
        out_specs=[pl.BlockSpec(bs, im) for _, bs, im in outs],
        out_shape=[o for o, _, _ in outs],
        scratch_shapes=list(scratch),
        compiler_params=_cp(1),
    )(*[a for a, _, _ in ins])


def _halo_ins(arr, S, ts, width, colblk):
    per = ts // HALO
    last = S // HALO - 1
    return [
        (arr, (ts, width), lambda i: (i, colblk)),
        (arr, (HALO, width), lambda i: (jnp.maximum(i * per - 1, 0), colblk)),
        (arr, (HALO, width), lambda i: (jnp.minimum((i + 1) * per, last), colblk)),
    ]


def _ext(main_ref, prev_ref, next_ref, i, n_tiles):
    prev = jnp.where(i > 0, prev_ref[...].astype(F32), 0.0)
    nxt = jnp.where(i < n_tiles - 1, next_ref[...].astype(F32), 0.0)
    return jnp.concatenate([prev, main_ref[...].astype(F32), nxt], axis=0)


def _shift(ext, off, ts):
    n = ext.shape[0]
    if off == 0:
        return ext[HALO:HALO + ts]
    return pltpu.roll(ext, (-off) % n, 0)[HALO:HALO + ts]


def _rmsnorm_fwd(name, x, g):
    S, D = x.shape
    ts = _tile(S, 512)

    def body(x_ref, g_ref, o_ref):
        xv = x_ref[...]
        r = lax.rsqrt(jnp.mean(xv * xv, axis=-1, keepdims=True) + EPS)
        o_ref[...] = (xv * r * g_ref[...]).astype(BF16)

    return _rows(name, S, ts,
                 [(x, (ts, D), lambda i: (i, 0)), (g, (1, D), lambda i: (0, 0))],
                 [(jax.ShapeDtypeStruct((S, D), BF16), (ts, D), lambda i: (i, 0))], body)[0]


def _rmsnorm_bwd_tile(dh, x, g):
    r = lax.rsqrt(jnp.mean(x * x, axis=-1, keepdims=True) + EPS)
    xhat = x * r
    dxn = dh * g
    dx = r * (dxn - xhat * jnp.mean(dxn * xhat, axis=-1, keepdims=True))
    return dx, dh * xhat


def _ffn_fwd(tag, x, h, wg, wu, wd):
    S = x.shape[0]
    tm = _tile(S, 512)
    tn = 1408

    def epi_up(i, accs, ex, out):
        a = accs[0][...]
        b = accs[1][...]
        out[0][...] = a.astype(BF16)
        out[1][...] = b.astype(BF16)
        out[2][...] = (a * _sigmoid(a) * b).astype(BF16)

    sds = jax.ShapeDtypeStruct((S, D_FF), BF16)
    a, b, f = _fused_mm(f"{tag}_up", S, D_FF, D_MODEL, tm, tn, D_MODEL,
                        [(h, "mk"), (wg, "kn"), (wu, "kn")], [(0, 1, 0), (0, 2, 1)], [],
                        [(sds, (tm, tn), _mn)] * 3, epi_up)

    def epi_down(i, accs, ex, out):
        out[0][...] = ex[0][...] + 0.5 * accs[0][...]

    xo = _fused_mm(f"{tag}_down", S, D_MODEL, D_FF, tm, D_MODEL, 1408,
                   [(f, "mk"), (wd, "kn")], [(0, 1, 0)], [(x, (tm, D_MODEL), _mn)],
                   [(jax.ShapeDtypeStruct((S, D_MODEL), F32), (tm, D_MODEL), _mn)], epi_down)[0]
    return xo, a, b, f


def _conv_taps(ext, w_ref, ts):
    acc = None
    for j in range(CONV_W):
        term = w_ref[j:j + 1, :] * _shift(ext, j - 2, ts)
        acc = term if acc is None else acc + term
    return acc


def _l2norm_heads(s, scale):
    outs = []
    for h in range(GDN_H):
        sh = s[:, h * GDN_DK:(h + 1) * GDN_DK]
        outs.append(sh * (lax.rsqrt(jnp.sum(sh * sh, axis=-1, keepdims=True) + EPS) * scale))
    return jnp.concatenate(outs, axis=-1)


def _conv_fwd(name, p, colblk, w, bias, mode):
    S = p.shape[0]
    ts = _tile(S, 512)
    n_tiles = S // ts
    C = w.shape[1]

    def body(main, prev, nxt, w_ref, b_ref, o_ref):
        i = pl.program_id(0)
        c = _conv_taps(_ext(main, prev, nxt, i, n_tiles), w_ref, ts)
        if mode == "bias":
            o_ref[...] = c + b_ref[...]
        else:
            s = c * _sigmoid(c)
            if mode == "q":
                s = _l2norm_heads(s, GDN_DK ** -0.5)
            elif mode == "k":
                s = _l2norm_heads(s, 1.0)
            o_ref[...] = s

    ins = _halo_ins(p, S, ts, C, colblk) + [(w, (CONV_W, C), lambda i: (0, 0)), (bias, (1, C), lambda i: (0, 0))]
    return _rows(name, S, ts, ins, [(jax.ShapeDtypeStruct((S, C), F32), (ts, C), lambda i: (i, 0))], body)[0]


def _rg_gate_terms(pre, xc, prm_ref, d):
    r = _sigmoid(pre[:, d * 1024:d * 1024 + RG_W] + prm_ref[2 * d:2 * d + 1, :])
    ig = _sigmoid(pre[:, d * 1024 + RG_W:(d + 1) * 1024] + prm_ref[2 * d + 1:2 * d + 2, :])
    sp = _softplus(-prm_ref[4 + d:5 + d, :])
    log_a = -RG_C * r * sp
    a = jnp.exp(log_a)
    t = jnp.tanh(log_a)
    sq = jnp.sqrt(-2.0 * t / (1.0 - t))
    return r, ig, sp, a, sq


def _rg_gates_fwd(xc, bd, prm):
    S = xc.shape[0]
    tm = _tile(S, 256)

    def epi(i, accs, ex, out):
        pre = accs[0][...]
        xv = ex[0][...]
        for d in range(2):
            r, ig, sp, a, sq = _rg_gate_terms(pre, xv, ex[1], d)
            out[2 * d][...] = a
            out[2 * d + 1][...] = sq * ig * xv

    sds = jax.ShapeDtypeStruct((S, RG_W), F32)
    blk = (tm, RG_W)
    im = lambda i, j: (i, 0)
    return _fused_mm("rg_gates_fwd", S, 4 * RG_W, RG_W, tm, 4 * RG_W, RG_W,
                     [(xc, "mk"), (bd, "kn")], [(0, 1, 0)],
                     [(xc, blk, im), (prm, (8, RG_W), _row0)], [(sds, blk, im)] * 4, epi)


def _rg_scan(name, a_f, b_f, a_b, b_b):
    S, C = a_f.shape
    ts = _tile(S, 512)
    n_tiles = S // ts

    def body(af, bf, ab, bb, hf, hb, carry):
        @pl.when(pl.program_id(0) == 0)
        def _():
            carry[...] = jnp.zeros_like(carry)

        def step(t, c):
            cf, cb = c
            cf = af[pl.ds(t, 1), :] * cf + bf[pl.ds(t, 1), :]
            hf[pl.ds(t, 1), :] = cf
            tb = ts - 1 - t
            cb = ab[pl.ds(tb, 1), :] * cb + bb[pl.ds(tb, 1), :]
            hb[pl.ds(tb, 1), :] = cb
            return cf, cb

        cf, cb = lax.fori_loop(0, ts, step, (carry[0:1, :], carry[1:2, :]), unroll=8)
        carry[0:1, :] = cf
        carry[1:2, :] = cb

    fw = lambda i: (i, 0)
    bw = lambda i: (n_tiles - 1 - i, 0)
    sds = jax.ShapeDtypeStruct((S, C), F32)
    return _rows(name, S, ts,
                 [(a_f, (ts, C), fw), (b_f, (ts, C), fw), (a_b, (ts, C), bw), (b_b, (ts, C), bw)],
                 [(sds, (ts, C), fw), (sds, (ts, C), bw)], body, scratch=[pltpu.VMEM((8, C), F32)])


def _tri_masks():
    ri = lax.broadcasted_iota(jnp.int32, (CHUNK, CHUNK), 0)
    ci = lax.broadcasted_iota(jnp.int32, (CHUNK, CHUNK), 1)
    return ri, ci


def _gdn_prep_fwd(p, prm):
    S = p.shape[0]
    ts = _tile(S, 512)

    def body(p_ref, prm_ref, o_ref):
        raw = p_ref[...]
        lane = lax.broadcasted_iota(jnp.int32, (1, 128), 1)
        g = -jnp.exp(prm_ref[0:1, :]) * _softplus(raw + prm_ref[1:2, :])
        g = jnp.where((lane >= 8) & (lane < 16), g, 0.0)
        beta = _sigmoid(raw)
        ri, ci = _tri_masks()
        lower = (ri >= ci).astype(F32)
        upper = (ri <= ci).astype(F32)
        for c in range(ts // CHUNK):
            rows = slice(c * CHUNK, (c + 1) * CHUNK)
            gch = g[rows]
            gc = jnp.where(lane < 12, _dot(lower, gch, 1, 0, HI), _dot(upper, gch, 1, 0, HI))
            o_ref[rows, :] = jnp.where(lane < 8, beta[rows], gc)

    return _rows("gdn_prep_fwd", S, ts,
                 [(p, (ts, 128), lambda i: (i, COL_BA // 128)), (prm, (8, 128), lambda i: (0, 0))],
                 [(jax.ShapeDtypeStruct((S, 128), F32), (ts, 128), lambda i: (i, 0))], body)[0]


def _tri_inv(l_mat, eye):
    x = -l_mat
    t = eye + x
    pw = x
    for _ in range(5):
        pw = _bdot(pw, pw, 1, 0)
        t = t + _bdot(t, pw, 1, 0)
    return t


def _bdot(a, b, ca, cb):
    return _dot(a.astype(BF16), b.astype(BF16), ca, cb)


GDN_W = GDN_H * GDN_DK
GDN_TS = 256


def _gdn_decay(bg_ref, gcr_ref, c, rows, r0, col, rev, ri, ci):
    beta = bg_ref[rows, col:col + 1]
    gc = bg_ref[rows, 8 + col:9 + col]
    last = 0 if rev else CHUNK - 1
    gl = bg_ref[pl.ds(r0 + last, 1), 8 + col:9 + col]
    out = dict(beta=beta, gc=gc, gl=gl, eg=jnp.exp(gc), egl=jnp.exp(gl - gc), cd=jnp.exp(gl))
    if gcr_ref is not None:
        incl = (ri <= ci) if rev else (ri >= ci)
        out["strict"] = (ri < ci) if rev else (ri > ci)
        out["dm"] = jnp.where(incl, jnp.exp(jnp.where(incl, gc - gcr_ref[c, col:col + 1, :], 0.0)), 0.0)
    return out


def _dir_tile(d, n_tiles, flip):
    if (d == 1) != flip:
        return lambda i: n_tiles - 1 - i
    return lambda i: i


def _gdn_local_fwd(q, k, v, bg, gcr):
    S = q.shape[0]
    ts = _tile(S, GDN_TS)
    ncb = ts // CHUNK
    nch = S // CHUNK

    def body(q_ref, k_ref, v_ref, bg_ref, gcr_ref, u0, w0, a0, t0, u1, w1, a1, t1):
        ri, ci = _tri_masks()
        eye = (ri == ci).astype(F32)
        outs = ((u0, w0, a0, t0), (u1, w1, a1, t1))

        def chunk(c, carry):
            r0 = pl.multiple_of(c * CHUNK, CHUNK)
            rows = pl.ds(r0, CHUNK)
            for h in range(GDN_H):
                cols = slice(h * GDN_DK, (h + 1) * GDN_DK)
                qh, kh, vh = q_ref[rows, cols], k_ref[rows, cols], v_ref[rows, cols]
                kk = _bdot(kh, kh, 1, 1)
                qk = _bdot(qh, kh, 1, 1)
                for d in range(2):
                    m = _gdn_decay(bg_ref, gcr_ref, c, rows, r0, d * GDN_H + h, d == 1, ri, ci)
                    u_ref, w_ref, a_ref, t_ref = outs[d]
                    t_mat = _tri_inv(jnp.where(m["strict"], m["beta"] * kk * m["dm"], 0.0), eye)
                    u_ref[rows, cols] = _bdot(t_mat, vh * m["beta"], 1, 0)
                    w_ref[rows, cols] = _bdot(t_mat, kh * (m["beta"] * m["eg"]), 1, 0).astype(BF16)
                    a_ref[c, h] = (qk * m["dm"]).astype(BF16)
                    t_ref[c, h] = t_mat.astype(BF16)
            return carry

        lax.fori_loop(0, ncb, chunk, 0)

    im = lambda i: (i, 0)
    im4 = lambda i: (i, 0, 0, 0)
    ins = [(q, (ts, GDN_W), im), (k, (ts, GDN_W), im), (v, (ts, GDN_W), im), (bg, (ts, 128), im),
           (gcr, (ncb, 8, CHUNK), lambda i: (i, 0, 0))]
    per_dir = [(jax.ShapeDtypeStruct((S, GDN_W), F32), (ts, GDN_W), im),
               (jax.ShapeDtypeStruct((S, GDN_W), BF16), (ts, GDN_W), im),
               (jax.ShapeDtypeStruct((nch, GDN_H, CHUNK, CHUNK), BF16), (ncb, GDN_H, CHUNK, CHUNK), im4),
               (jax.ShapeDtypeStruct((nch, GDN_H, CHUNK, CHUNK), BF16), (ncb, GDN_H, CHUNK, CHUNK), im4)]
    res = _rows("gdn_local_fwd", S, ts, ins, per_dir * 2, body)
    return res[0:4], res[4:8]


def _gdn_scan_fwd(q, k, bg, loc):
    S = q.shape[0]
    ts = _tile(S, GDN_TS)
    n_tiles = S // ts
    ncb = ts // CHUNK
    nch = S // CHUNK

    def body(*refs):
        ins = (refs[0:6], refs[6:12])
        outs = (refs[12:15], refs[15:18])
        state = refs[18]

        @pl.when(pl.program_id(0) == 0)
        def _():
            state[...] = jnp.zeros_like(state)

        def chunk(cc, carry):
            for d in range(2):
                q_ref, k_ref, bg_ref, u_ref, w_ref, a_ref = ins[d]
                o_ref, vn_ref, s_ref = outs[d]
                c = cc if d == 0 else ncb - 1 - cc
                r0 = pl.multiple_of(c * CHUNK, CHUNK)
                rows = pl.ds(r0, CHUNK)
                for h in range(GDN_H):
                    cols = slice(h * GDN_DK, (h + 1) * GDN_DK)
                    m = _gdn_decay(bg_ref, None, c, rows, r0, d * GDN_H + h, d == 1, None, None)
                    st = state[d * GDN_H + h]
                    stb = st.astype(BF16)
                    vn = u_ref[rows, cols] - _dot(w_ref[rows, cols], stb, 1, 0)
                    vnb = vn.astype(BF16)
                    o_ref[rows, cols] = _bdot(q_ref[rows, cols] * m["eg"], stb, 1, 0) + _dot(a_ref[c, h], vnb, 1, 0)
                    vn_ref[rows, cols] = vn
                    s_ref[c, h] = st
                    state[d * GDN_H + h] = st * m["cd"] + _bdot(k_ref[rows, cols] * m["egl"], vnb, 0, 0)
            return carry

        lax.fori_loop(0, ncb, chunk, 0)

    ins, outs = [], []
    for d in range(2):
        tix = _dir_tile(d, n_tiles, False)
        im = lambda i, tix=tix: (tix(i), 0)
        im4 = lambda i, tix=tix: (tix(i), 0, 0, 0)
        u, w, a, _ = loc[d]
        ins += [(q, (ts, GDN_W), im), (k, (ts, GDN_W), im), (bg, (ts, 128), im), (u, (ts, GDN_W), im),
                (w, (ts, GDN_W), im), (a, (ncb, GDN_H, CHUNK, CHUNK), im4)]
        outs += [(jax.ShapeDtypeStruct((S, GDN_W), F32), (ts, GDN_W), im),
                 (jax.ShapeDtypeStruct((S, GDN_W), F32), (ts, GDN_W), im),
                 (jax.ShapeDtypeStruct((nch, GDN_H, GDN_DK, GDN_DK), F32), (ncb, GDN_H, GDN_DK, GDN_DK), im4)]
    res = _rows("gdn_scan_fwd", S, ts, ins, outs, body, scratch=[pltpu.VMEM((2 * GDN_H, GDN_DK, GDN_DK), F32)])
    return res[0:3], res[3:6]


def _gelu(x):
    c = math.sqrt(2.0 / math.pi)
    t = jnp.tanh(c * (x + 0.044715 * x * x * x))
    return 0.5 * x * (1.0 + t), t


def _mix_out_fwd(h_f, h_b, o_f, o_b, p, gn):
    S = h_f.shape[0]
    ts = _tile(S, 512)

    def body(hf, hb, of, ob, gate, z, gn_ref, y_ref):
        ge, _ = _gelu(gate[...])
        y_ref[:, 0:RG_W] = ((hf[...] + hb[...]) * ge).astype(BF16)
        o = of[...] + ob[...]
        zv = z[...]
        sz = zv * _sigmoid(zv)
        for h in range(GDN_H):
            cols = slice(h * GDN_DK, (h + 1) * GDN_DK)
            oh = o[:, cols]
            n = oh * lax.rsqrt(jnp.mean(oh * oh, axis=-1, keepdims=True) + EPS) * gn_ref[...]
            y_ref[:, RG_W + h * GDN_DK:RG_W + (h + 1) * GDN_DK] = (n * sz[:, cols]).astype(BF16)

    blk = (ts, RG_W)
    im = lambda i: (i, 0)
    ins = [(h_f, blk, im), (h_b, blk, im), (o_f, blk, im), (o_b, blk, im),
           (p, blk, lambda i: (i, 1)), (p, blk, lambda i: (i, 5)), (gn, (1, GDN_DK), lambda i: (0, 0))]
    return _rows("mix_out_fwd", S, ts, ins,
                 [(jax.ShapeDtypeStruct((S, D_MODEL), BF16), (ts, D_MODEL), im)], body)[0]


def _loss_head(x, target, g):
    S, D = x.shape
    ts = _tile(S, 512)

    def body(x_ref, t_ref, g_ref, dx_ref, loss_ref, dg_ref):
        @pl.when(pl.program_id(0) == 0)
        def _():
            loss_ref[...] = jnp.zeros_like(loss_ref)
            dg_ref[...] = jnp.zeros_like(dg_ref)

        xv = x_ref[...]
        gv = g_ref[...]
        r = lax.rsqrt(jnp.mean(xv * xv, axis=-1, keepdims=True) + EPS)
        err = xv * r * gv - t_ref[...]
        loss_ref[...] += jnp.sum(err * err) * (0.5 / D)
        dx, dgt = _rmsnorm_bwd_tile(err * (1.0 / D), xv, gv)
        dx_ref[...] = dx
        dg_ref[...] += jnp.sum(dgt, axis=0, keepdims=True)

    im = lambda i: (i, 0)
    z = lambda i: (0, 0)
    return _rows("loss_head", S, ts,
                 [(x, (ts, D), im), (target, (ts, D), im), (g, (1, D), z)],
                 [(jax.ShapeDtypeStruct((S, D), F32), (ts, D), im),
                  (jax.ShapeDtypeStruct((8, 128), F32), (8, 128), z),
                  (jax.ShapeDtypeStruct((1, D), F32), (1, D), z)], body)


def _block_diag(w):
    n = w.shape[0]
    return jnp.einsum("nij,nm->nimj", w, jnp.eye(n, dtype=w.dtype)).reshape(n * w.shape[1], n * w.shape[2])


def _rg_bd(a_w, x_w):
    return jnp.concatenate([_block_diag(a_w[0]), _block_diag(x_w[0]), _block_diag(a_w[1]), _block_diag(x_w[1])],
                           axis=1).astype(BF16)


def _rg_prm(ba, bx, lam):
    return jnp.concatenate([ba[0:1], bx[0:1], ba[1:2], bx[1:2], lam, jnp.zeros((2, RG_W), F32)], axis=0)


def _gdn_prm(a_log, dt_bias):
    rows = jnp.zeros((8, 128), F32)
    rows = rows.at[0, 8:16].set(a_log.reshape(-1))
    return rows.at[1, 8:16].set(dt_bias.reshape(-1))


def _gc_rows(bg):
    S = bg.shape[0]
    return bg[:, 8:16].reshape(S // CHUNK, CHUNK, 8).transpose(0, 2, 1)


def _layer_fwd(x0, target, W):
    S = x0.shape[0]
    R = {}
    R["h1"] = _rmsnorm_fwd("rms1", x0, W["ffn1_norm"])
    R["x1"], R["a1"], R["b1"], R["f1"] = _ffn_fwd("ffn1", x0, R["h1"], W["ffn1_w_gate"], W["ffn1_w_up"], W["ffn1_w_down"])
    R["h2"] = _rmsnorm_fwd("rms2", R["x1"], W["mix_norm"])
    tm = _tile(S, 512)
    R["p"] = _fused_mm("in_proj", S, D_IN_PAD, D_MODEL, tm, 640, D_MODEL, [(R["h2"], "mk"), (W["w_in"], "kn")],
                       [(0, 1, 0)], [], [(jax.ShapeDtypeStruct((S, D_IN_PAD), F32), (tm, 640), _mn)],
                       lambda i, accs, ex, out: out[0].__setitem__(Ellipsis, accs[0][...]))[0]
    p = R["p"]
    R["xc"] = _conv_fwd("rg_conv_fwd", p, 0, W["rg_conv_w"], W["rg_conv_b"], "bias")
    R["bd"] = _rg_bd(W["rg_gate_a_w"], W["rg_gate_x_w"])
    R["rg_prm"] = _rg_prm(W["rg_gate_a_b"], W["rg_gate_x_b"], W["rg_lambda"])
    a_f, b_f, a_b, b_b = _rg_gates_fwd(R["xc"], R["bd"], R["rg_prm"])
    R["a_f"], R["a_b"] = a_f, a_b
    R["h_f"], R["h_b"] = _rg_scan("rg_scan_fwd", a_f, b_f, a_b, b_b)
    zero_b = jnp.zeros((1, RG_W), F32)
    cw = W["gdn_conv_w"]
    R["q"] = _conv_fwd("gdn_conv_q", p, 2, cw[:, 0:512], zero_b, "q")
    R["k"] = _conv_fwd("gdn_conv_k", p, 3, cw[:, 512:1024], zero_b, "k")
    R["v"] = _conv_fwd("gdn_conv_v", p, 4, cw[:, 1024:1536], zero_b, "v")
    R["gdn_prm"] = _gdn_prm(W["gdn_a_log"], W["gdn_dt_bias"])
    R["bg"] = _gdn_prep_fwd(p, R["gdn_prm"])
    R["gcr"] = _gc_rows(R["bg"])
    R["gdn_loc"] = _gdn_local_fwd(R["q"], R["k"], R["v"], R["bg"], R["gcr"])
    R["gdn_fwd"] = _gdn_scan_fwd(R["q"], R["k"], R["bg"], R["gdn_loc"])
    R["o_f"], R["o_b"] = R["gdn_fwd"][0][0], R["gdn_fwd"][1][0]
    R["y"] = _mix_out_fwd(R["h_f"], R["h_b"], R["o_f"], R["o_b"], p, W["gdn_norm"])
    R["x2"] = _fused_mm("out_proj", S, D_MODEL, D_MODEL, tm, D_MODEL, D_MODEL, [(R["y"], "mk"), (W["w_out"], "kn")],
                        [(0, 1, 0)], [(R["x1"], (tm, D_MODEL), _mn)],
                        [(jax.ShapeDtypeStruct((S, D_MODEL), F32), (tm, D_MODEL), _mn)],
                        lambda i, accs, ex, out: out[0].__setitem__(Ellipsis, ex[0][...] + accs[0][...]))[0]
    R["h3"] = _rmsnorm_fwd("rms3", R["x2"], W["ffn2_norm"])
    R["x3"], R["a2"], R["b2"], R["f2"] = _ffn_fwd("ffn2", R["x2"], R["h3"], W["ffn2_w_gate"], W["ffn2_w_up"], W["ffn2_w_down"])
    R["dx3"], R["loss"], R["d_final_norm"] = _loss_head(R["x3"], target, W["final_norm"])
    return R


def _colsum_into(ref, i, val):
    @pl.when(i == 0)
    def _():
        ref[...] = val

    @pl.when(i > 0)
    def _():
        ref[...] += val


def _ffn_bwd(tag, dout, x, g, h, a, b, f, wg, wu, wd):
    S = x.shape[0]
    tm = _tile(S, 512)
    tk_s = _tile(S, 512)

    def epi_act(i, accs, ex, out):
        df = 0.5 * accs[0][...]
        av = ex[0][...].astype(F32)
        bv = ex[1][...].astype(F32)
        s = _sigmoid(av)
        out[0][...] = (df * bv * (s * (1.0 + av * (1.0 - s)))).astype(BF16)
        out[1][...] = (df * av * s).astype(BF16)

    sds = jax.ShapeDtypeStruct((S, D_FF), BF16)
    da, db = _fused_mm(f"{tag}_dact", S, D_FF, D_MODEL, tm, 1408, D_MODEL, [(dout, "mk"), (wd, "nk")], [(0, 1, 0)],
                       [(a, (tm, 1408), _mn), (b, (tm, 1408), _mn)], [(sds, (tm, 1408), _mn)] * 2, epi_act)

    def epi_dx(i, accs, ex, out):
        dx, dgt = _rmsnorm_bwd_tile(accs[0][...], ex[0][...], ex[1][...])
        out[0][...] = ex[2][...] + dx
        _colsum_into(out[1], i, jnp.sum(dgt, axis=0, keepdims=True))

    dx, dg = _fused_mm(f"{tag}_dx", S, D_MODEL, D_FF, tm, D_MODEL, 1408,
                       [(da, "mk"), (wg, "nk"), (db, "mk"), (wu, "nk")], [(0, 1, 0), (2, 3, 0)],
                       [(x, (tm, D_MODEL), _mn), (g, (1, D_MODEL), _row0), (dout, (tm, D_MODEL), _mn)],
                       [(jax.ShapeDtypeStruct((S, D_MODEL), F32), (tm, D_MODEL), _mn),
                        (jax.ShapeDtypeStruct((1, D_MODEL), F32), (1, D_MODEL), _row0)], epi_dx)

    def epi_w2(i, accs, ex, out):
        out[0][...] = accs[0][...].astype(BF16)
        out[1][...] = accs[1][...].astype(BF16)

    sdw = jax.ShapeDtypeStruct((D_MODEL, D_FF), BF16)
    dwg, dwu = _fused_mm(f"{tag}_dw_up", D_MODEL, D_FF, S, D_MODEL, 1408, tk_s,
                         [(h, "km"), (da, "kn"), (db, "kn")], [(0, 1, 0), (0, 2, 1)], [],
                         [(sdw, (D_MODEL, 1408), _mn)] * 2, epi_w2)
    dwd = _fused_mm(f"{tag}_dw_down", D_FF, D_MODEL, S, 1408, D_MODEL, tk_s, [(f, "km"), (dout, "kn")], [(0, 1, 0)], [],
                    [(jax.ShapeDtypeStruct((D_FF, D_MODEL), BF16), (1408, D_MODEL), _mn)],
                    lambda i, accs, ex, out: out[0].__setitem__(Ellipsis, (0.5 * accs[0][...]).astype(BF16)))[0]
    return dx, dg, dwg, dwu, dwd


def _mix_out_bwd(dy, h_f, h_b, o_f, o_b, p, gn):
    S = dy.shape[0]
    ts = _tile(S, 512)
    c0 = math.sqrt(2.0 / math.pi)

    def body(dy_ref, hf, hb, of, ob, gate, z, gn_ref, dhr_ref, dgate_ref, do_ref, dz_ref, dgn_ref):
        i = pl.program_id(0)
        gv = gate[...]
        ge, t = _gelu(gv)
        dy_rg = dy_ref[:, 0:RG_W]
        dhr_ref[...] = dy_rg * ge
        dgelu = 0.5 * (1.0 + t) + 0.5 * gv * (1.0 - t * t) * c0 * (1.0 + 3.0 * 0.044715 * gv * gv)
        dgate_ref[...] = (dy_rg * (hf[...] + hb[...]) * dgelu).astype(BF16)
        o = of[...] + ob[...]
        zv = z[...]
        sig = _sigmoid(zv)
        gnv = gn_ref[...]
        dgn = jnp.zeros((1, GDN_DK), F32)
        for h in range(GDN_H):
            cols = slice(h * GDN_DK, (h + 1) * GDN_DK)
            oh = o[:, cols]
            r = lax.rsqrt(jnp.mean(oh * oh, axis=-1, keepdims=True) + EPS)
            ohat = oh * r
            dyh = dy_ref[:, RG_W + h * GDN_DK:RG_W + (h + 1) * GDN_DK]
            zh = zv[:, cols]
            sh = sig[:, cols]
            dn = dyh * zh * sh
            dz_ref[:, cols] = (dyh * ohat * gnv * (sh * (1.0 + zh * (1.0 - sh)))).astype(BF16)
            dxn = dn * gnv
            do_ref[:, cols] = r * (dxn - ohat * jnp.mean(dxn * ohat, axis=-1, keepdims=True))
            dgn = dgn + jnp.sum(dn * ohat, axis=0, keepdims=True)
        _colsum_into(dgn_ref, i, dgn)

    blk = (ts, RG_W)
    im = lambda i: (i, 0)
    z0 = lambda i: (0, 0)
    ins = [(dy, (ts, D_MODEL), im), (h_f, blk, im), (h_b, blk, im), (o_f, blk, im), (o_b, blk, im),
           (p, blk, lambda i: (i, 1)), (p, blk, lambda i: (i, 5)), (gn, (1, GDN_DK), z0)]
    outs = [(jax.ShapeDtypeStruct((S, RG_W), F32), blk, im), (jax.ShapeDtypeStruct((S, RG_W), BF16), blk, im),
            (jax.ShapeDtypeStruct((S, RG_W), F32), blk, im), (jax.ShapeDtypeStruct((S, RG_W), BF16), blk, im),
            (jax.ShapeDtypeStruct((1, GDN_DK), F32), (1, GDN_DK), z0)]
    return _rows("mix_out_bwd", S, ts, ins, outs, body)


def _rg_scan_adj(name, a_up, b_up, a_dn, b_dn):
    S, C = a_up.shape
    ts = _tile(S, 512)
    n_tiles = S // ts

    def body(au, bu, ad, bd, mu_ref, lam_ref, carry):
        @pl.when(pl.program_id(0) == 0)
        def _():
            carry[...] = jnp.zeros_like(carry)

        def step(t, c):
            cu, cd = c
            mu = bu[pl.ds(t, 1), :] + cu
            mu_ref[pl.ds(t, 1), :] = mu
            cu = au[pl.ds(t, 1), :] * mu
            tb = ts - 1 - t
            lam = bd[pl.ds(tb, 1), :] + cd
            lam_ref[pl.ds(tb, 1), :] = lam
            cd = ad[pl.ds(tb, 1), :] * lam
            return cu, cd

        cu, cd = lax.fori_loop(0, ts, step, (carry[0:1, :], carry[1:2, :]), unroll=8)
        carry[0:1, :] = cu
        carry[1:2, :] = cd

    fw = lambda i: (i, 0)
    bw = lambda i: (n_tiles - 1 - i, 0)
    sds = jax.ShapeDtypeStruct((S, C), F32)
    return _rows(name, S, ts,
                 [(a_up, (ts, C), fw), (b_up, (ts, C), fw), (a_dn, (ts, C), bw), (b_dn, (ts, C), bw)],
                 [(sds, (ts, C), fw), (sds, (ts, C), bw)], body, scratch=[pltpu.VMEM((8, C), F32)])


def _halo_ex(arr, S, tm, width):
    per = tm // HALO
    last = S // HALO - 1
    return [
        (arr, (tm, width), lambda i, j: (i, 0)),
        (arr, (HALO, width), lambda i, j: (jnp.maximum(i * per - 1, 0), 0)),
        (arr, (HALO, width), lambda i, j: (jnp.minimum((i + 1) * per, last), 0)),
    ]


def _rg_gates_bwd(xc, bd, prm, lam_f, lam_b, h_f, h_b):
    S = xc.shape[0]
    tm = _tile(S, 256)
    n_tiles = S // tm

    def epi(i, accs, ex, out):
        pre = accs[0][...]
        xv = ex[0][...]
        prm_ref = ex[1]
        lams = (ex[2][...], ex[3][...])
        hprev = (_shift(_ext(ex[4], ex[5], ex[6], i, n_tiles), -1, tm),
                 _shift(_ext(ex[7], ex[8], ex[9], i, n_tiles), 1, tm))
        dxc = jnp.zeros_like(xv)
        rows = []
        dlam_rows = []
        for d in range(2):
            r, ig, sp, a, sq = _rg_gate_terms(pre, xv, prm_ref, d)
            lam = lams[d]
            da = lam * hprev[d]
            di = lam * sq * xv
            dxc = dxc + lam * sq * ig
            dsq = lam * ig * xv
            dlog_a = da * a - dsq * (a * a) / sq
            dpre_r = dlog_a * (-RG_C * sp) * r * (1.0 - r)
            dpre_i = di * ig * (1.0 - ig)
            out[0][:, d * 1024:d * 1024 + RG_W] = dpre_r.astype(BF16)
            out[0][:, d * 1024 + RG_W:(d + 1) * 1024] = dpre_i.astype(BF16)
            rows += [jnp.sum(dpre_r, axis=0, keepdims=True), jnp.sum(dpre_i, axis=0, keepdims=True)]
            dsp = jnp.sum(dlog_a * (-RG_C * r), axis=0, keepdims=True)
            dlam_rows.append(-dsp * _sigmoid(-prm_ref[4 + d:5 + d, :]))
        out[1][...] = dxc
        zero = jnp.zeros((2, RG_W), F32)
        _colsum_into(out[2], i, jnp.concatenate(rows + dlam_rows + [zero], axis=0))

    blk = (tm, RG_W)
    im = lambda i, j: (i, 0)
    extras = ([(xc, blk, im), (prm, (8, RG_W), _row0), (lam_f, blk, im), (lam_b, blk, im)]
              + _halo_ex(h_f, S, tm, RG_W) + _halo_ex(h_b, S, tm, RG_W))
    outs = [(jax.ShapeDtypeStruct((S, 4 * RG_W), BF16), (tm, 4 * RG_W), im),
            (jax.ShapeDtypeStruct((S, RG_W), F32), blk, im),
            (jax.ShapeDtypeStruct((8, RG_W), F32), (8, RG_W), _row0)]
    return _fused_mm("rg_gates_bwd", S, 4 * RG_W, RG_W, tm, 4 * RG_W, RG_W, [(xc, "mk"), (bd, "kn")], [(0, 1, 0)],
                     extras, outs, epi)


def _roll_rows(ext, off):
    if off == 0:
        return ext
    return pltpu.roll(ext, (-off) % ext.shape[0], 0)


def _conv_bwd(name, p, colblk, w, grads, mode):
    S = p.shape[0]
    ts = _tile(S, 512)
    n_tiles = S // ts
    C = w.shape[1]
    ng = len(grads)

    def body(*refs):
        p_refs = refs[0:3]
        g_refs = refs[3:3 + 3 * ng]
        w_ref = refs[3 + 3 * ng]
        dx_ref, dw_ref, db_ref = refs[4 + 3 * ng:]
        i = pl.program_id(0)
        ext_p = _ext(*p_refs, i, n_tiles)
        dn = _ext(*g_refs[0:3], i, n_tiles)
        for gi in range(1, ng):
            dn = dn + _ext(*g_refs[3 * gi:3 * gi + 3], i, n_tiles)
        if mode == "bias":
            dc = dn
        else:
            c = None
            for j in range(CONV_W):
                term = w_ref[j:j + 1, :] * _roll_rows(ext_p, j - 2)
                c = term if c is None else c + term
            sig = _sigmoid(c)
            s = c * sig
            if mode in ("q", "k"):
                scale = GDN_DK ** -0.5 if mode == "q" else 1.0
                parts = []
                for h in range(GDN_H):
                    cols = slice(h * GDN_DK, (h + 1) * GDN_DK)
                    sh = s[:, cols]
                    dnh = dn[:, cols]
                    rinv = lax.rsqrt(jnp.sum(sh * sh, axis=-1, keepdims=True) + EPS)
                    parts.append(scale * rinv * (dnh - sh * (rinv * rinv) * jnp.sum(dnh * sh, axis=-1, keepdims=True)))
                ds = jnp.concatenate(parts, axis=-1)
            else:
                ds = dn
            dc = ds * (sig * (1.0 + c * (1.0 - sig)))
        dx = None
        for j in range(CONV_W):
            term = w_ref[j:j + 1, :] * _shift(dc, 2 - j, ts)
            dx = term if dx is None else dx + term
        dx_ref[...] = dx.astype(BF16)
        dc_main = dc[HALO:HALO + ts]
        dw = jnp.concatenate([jnp.sum(dc_main * _shift(ext_p, j - 2, ts), axis=0, keepdims=True)
                              for j in range(CONV_W)], axis=0)
        _colsum_into(dw_ref, i, dw)
        _colsum_into(db_ref, i, jnp.sum(dc_main, axis=0, keepdims=True))

    ins = _halo_ins(p, S, ts, C, colblk)
    for garr in grads:
        ins += _halo_ins(garr, S, ts, C, 0)
    ins += [(w, (CONV_W, C), lambda i: (0, 0))]
    z0 = lambda i: (0, 0)
    outs = [(jax.ShapeDtypeStruct((S, C), BF16), (ts, C), lambda i: (i, 0)),
            (jax.ShapeDtypeStruct((CONV_W, C), F32), (CONV_W, C), z0),
            (jax.ShapeDtypeStruct((1, C), F32), (1, C), z0)]
    return _rows(name, S, ts, ins, outs, body)


def _gdn_scan_bwd(q, k, bg, loc, do):
    S = q.shape[0]
    ts = _tile(S, GDN_TS)
    n_tiles = S // ts
    ncb = ts // CHUNK
    nch = S // CHUNK

    def body(*refs):
        ins = (refs[0:6], refs[6:12])
        outs = (refs[12:14], refs[14:16])
        dstate = refs[16]

        @pl.when(pl.program_id(0) == 0)
        def _():
            dstate[...] = jnp.zeros_like(dstate)

        def chunk(cc, carry):
            for d in range(2):
                q_ref, k_ref, bg_ref, w_ref, a_ref, do_ref = ins[d]
                dvn_ref, ds_ref = outs[d]
                c = ncb - 1 - cc if d == 0 else cc
                r0 = pl.multiple_of(c * CHUNK, CHUNK)
                rows = pl.ds(r0, CHUNK)
                for h in range(GDN_H):
                    cols = slice(h * GDN_DK, (h + 1) * GDN_DK)
                    m = _gdn_decay(bg_ref, None, c, rows, r0, d * GDN_H + h, d == 1, None, None)
                    dsn = dstate[d * GDN_H + h]
                    dob = do_ref[rows, cols].astype(BF16)
                    dvn = _dot(a_ref[c, h], dob, 0, 0) + _bdot(k_ref[rows, cols] * m["egl"], dsn, 1, 0)
                    dvn_ref[rows, cols] = dvn
                    ds_ref[c, h] = dsn
                    dstate[d * GDN_H + h] = (_bdot(q_ref[rows, cols] * m["eg"], dob, 0, 0) + m["cd"] * dsn
                                             - _dot(w_ref[rows, cols], dvn.astype(BF16), 0, 0))
            return carry

        lax.fori_loop(0, ncb, chunk, 0)

    ins, outs = [], []
    for d in range(2):
        tix = _dir_tile(d, n_tiles, True)
        im = lambda i, tix=tix: (tix(i), 0)
        im4 = lambda i, tix=tix: (tix(i), 0, 0, 0)
        _, w, a, _ = loc[d]
        ins += [(q, (ts, GDN_W), im), (k, (ts, GDN_W), im), (bg, (ts, 128), im), (w, (ts, GDN_W), im),
                (a, (ncb, GDN_H, CHUNK, CHUNK), im4), (do, (ts, GDN_W), im)]
        outs += [(jax.ShapeDtypeStruct((S, GDN_W), F32), (ts, GDN_W), im),
                 (jax.ShapeDtypeStruct((nch, GDN_H, GDN_DK, GDN_DK), F32), (ncb, GDN_H, GDN_DK, GDN_DK), im4)]
    res = _rows("gdn_scan_bwd", S, ts, ins, outs, body, scratch=[pltpu.VMEM((2 * GDN_H, GDN_DK, GDN_DK), F32)])
    return res[0:2], res[2:4]


def _gdn_local_bwd(q, k, v, bg, gcr, do, loc, fwd, adj):
    S = q.shape[0]
    ts = _tile(S, GDN_TS)
    ncb = ts // CHUNK

    def body(q_ref, k_ref, v_ref, bg_ref, gcr_ref, do_ref, *rest):
        per_dir = (rest[0:5], rest[5:10])
        dq_ref, dk_ref, dv_ref, dbg_ref = rest[10:14]
        ri, ci = _tri_masks()
        lane = lax.broadcasted_iota(jnp.int32, (CHUNK, 128), 1)
        rowi = lax.broadcasted_iota(jnp.int32, (CHUNK, 1), 0)
        ones = jnp.ones((CHUNK, 128), F32)

        def chunk(c, carry):
            r0 = pl.multiple_of(c * CHUNK, CHUNK)
            rows = pl.ds(r0, CHUNK)
            acc_bg = jnp.zeros((CHUNK, 128), F32)
            for h in range(GDN_H):
                cols = slice(h * GDN_DK, (h + 1) * GDN_DK)
                qh, kh, vh = q_ref[rows, cols], k_ref[rows, cols], v_ref[rows, cols]
                dob = do_ref[rows, cols].astype(BF16)
                kk = _bdot(kh, kh, 1, 1)
                qk = _bdot(qh, kh, 1, 1)
                dq = dk = dv = None
                for d in range(2):
                    col = d * GDN_H + h
                    m = _gdn_decay(bg_ref, gcr_ref, c, rows, r0, col, d == 1, ri, ci)
                    t_ref, s_ref, ds_ref, vn_ref, dvn_ref = per_dir[d]
                    beta, eg, egl, dm = m["beta"], m["eg"], m["egl"], m["dm"]
                    l_mat = jnp.where(m["strict"], beta * kk * dm, 0.0)
                    kb = kh * beta
                    kbg = kb * eg
                    t_mat = t_ref[c, h]
                    stb = s_ref[c, h].astype(BF16)
                    dsn = ds_ref[c, h]
                    vnb = vn_ref[rows, cols].astype(BF16)
                    dvnb = dvn_ref[rows, cols].astype(BF16)
                    dqd = _dot(dob, stb, 1, 1)
                    d_a = _dot(dob, vnb, 1, 1)
                    dkd = _bdot(vnb, dsn, 1, 1)
                    dcd = jnp.sum(jnp.sum(s_ref[c, h] * dsn, axis=1, keepdims=True), axis=0, keepdims=True)
                    dw = -_dot(dvnb, stb, 1, 1)
                    dwb = dw.astype(BF16)
                    d_t = _bdot(dvnb, vh * beta, 1, 1) + _bdot(dwb, kbg, 1, 1)
                    dvb = _dot(t_mat, dvnb, 0, 0)
                    dkbg = _dot(t_mat, dwb, 0, 0)
                    d_l = -_dot(t_mat, _dot(d_t.astype(BF16), t_mat, 1, 1).astype(BF16), 0, 0)
                    d_l = jnp.where(m["strict"], d_l, 0.0)
                    mm = d_l * dm
                    nn = d_a * dm
                    dkb = _bdot(mm, kh, 1, 0) + dkbg * eg
                    dk_d = _bdot(mm, kb, 0, 0) + _bdot(nn, qh, 0, 0) + dkd * egl + dkb * beta
                    dq_d = _bdot(nn, kh, 1, 0) + dqd * eg
                    e = d_l * l_mat + nn * qk
                    rs = jnp.sum(e, axis=1, keepdims=True)
                    cs = _dot(e, ones, 0, 0, HI)[:, 0:1]
                    dkd_kd = dkd * (kh * egl)
                    dgc = (rs - cs + jnp.sum(dqd * (qh * eg), axis=1, keepdims=True)
                           - jnp.sum(dkd_kd, axis=1, keepdims=True) + jnp.sum(dkbg * kbg, axis=1, keepdims=True))
                    dgl = jnp.sum(jnp.sum(dkd_kd, axis=1, keepdims=True), axis=0, keepdims=True) + dcd * m["cd"]
                    dgc = dgc + jnp.where(rowi == (0 if d == 1 else CHUNK - 1), dgl, 0.0)
                    dbeta = jnp.sum(dkb * kh, axis=1, keepdims=True) + jnp.sum(dvb * vh, axis=1, keepdims=True)
                    dv_d = dvb * beta
                    dq = dq_d if dq is None else dq + dq_d
                    dk = dk_d if dk is None else dk + dk_d
                    dv = dv_d if dv is None else dv + dv_d
                    acc_bg = acc_bg + jnp.where(lane == col, dbeta, 0.0) + jnp.where(lane == 8 + col, dgc, 0.0)
                dq_ref[rows, cols] = dq
                dk_ref[rows, cols] = dk
                dv_ref[rows, cols] = dv
            dbg_ref[rows, :] = acc_bg
            return carry

        lax.fori_loop(0, ncb, chunk, 0)

    im = lambda i: (i, 0)
    im4 = lambda i: (i, 0, 0, 0)
    blk = (ts, GDN_W)
    ins = [(q, blk, im), (k, blk, im), (v, blk, im), (bg, (ts, 128), im), (gcr, (ncb, 8, CHUNK), lambda i: (i, 0, 0)),
           (do, blk, im)]
    for d in range(2):
        ins += [(loc[d][3], (ncb, GDN_H, CHUNK, CHUNK), im4), (fwd[d][2], (ncb, GDN_H, GDN_DK, GDN_DK), im4),
                (adj[d][1], (ncb, GDN_H, GDN_DK, GDN_DK), im4), (fwd[d][1], blk, im), (adj[d][0], blk, im)]
    sds = jax.ShapeDtypeStruct((S, GDN_W), F32)
    outs = [(sds, blk, im), (sds, blk, im), (sds, blk, im), (jax.ShapeDtypeStruct((S, 128), F32), (ts, 128), im)]
    return _rows("gdn_local_bwd", S, ts, ins, outs, body)


def _gdn_prep_bwd(dbg_all, p, prm):
    S = p.shape[0]
    ts = _tile(S, 512)

    def body(dbg_ref, p_ref, prm_ref, dba_ref, dprm_ref):
        i = pl.program_id(0)
        raw = p_ref[...]
        dbg = dbg_ref[...]
        lane = lax.broadcasted_iota(jnp.int32, (1, 128), 1)
        is_g = (lane >= 8) & (lane < 16)
        ea = jnp.exp(prm_ref[0:1, :])
        arg = raw + prm_ref[1:2, :]
        g = jnp.where(is_g, -ea * _softplus(arg), 0.0)
        beta = _sigmoid(raw)
        dgc = jnp.where(is_g, dbg, 0.0)
        ri, ci = _tri_masks()
        lower = (ri >= ci).astype(F32)
        upper = (ri <= ci).astype(F32)
        dgs = []
        for c in range(ts // CHUNK):
            ch = dgc[c * CHUNK:(c + 1) * CHUNK]
            dgs.append(jnp.where(lane < 12, _dot(upper, ch, 1, 0, HI), _dot(lower, ch, 1, 0, HI)))
        dg = jnp.concatenate(dgs, axis=0)
        dalpha = jnp.where(is_g, dg * (-ea) * _sigmoid(arg), 0.0)
        dba_ref[...] = jnp.where(lane < 8, dbg * beta * (1.0 - beta), dalpha).astype(BF16)
        rows = jnp.concatenate([jnp.sum(dg * g, axis=0, keepdims=True), jnp.sum(dalpha, axis=0, keepdims=True),
                                jnp.zeros((6, 128), F32)], axis=0)
        _colsum_into(dprm_ref, i, rows)

    im = lambda i: (i, 0)
    z0 = lambda i: (0, 0)
    return _rows("gdn_prep_bwd", S, ts,
                 [(dbg_all, (ts, 128), im), (p, (ts, 128), lambda i: (i, COL_BA // 128)), (prm, (8, 128), z0)],
                 [(jax.ShapeDtypeStruct((S, 128), BF16), (ts, 128), im), (jax.ShapeDtypeStruct((8, 128), F32), (8, 128), z0)],
                 body)


def _mm_plain(name, M, N, K, tm, tn, tk, a, am, b, bm, dtype):
    return _fused_mm(name, M, N, K, tm, tn, tk, [(a, am), (b, bm)], [(0, 1, 0)], [],
                     [(jax.ShapeDtypeStruct((M, N), dtype), (tm, tn), _mn)],
                     lambda i, accs, ex, out: out[0].__setitem__(Ellipsis, accs[0][...].astype(dtype)))[0]


def _layer_bwd(x0, W, R):
    S = x0.shape[0]
    tm = _tile(S, 512)
    tk_s = _tile(S, 512)
    G = {}
    dx2, G["ffn2_norm"], G["ffn2_w_gate"], G["ffn2_w_up"], G["ffn2_w_down"] = _ffn_bwd(
        "ffn2b", R["dx3"], R["x2"], W["ffn2_norm"], R["h3"], R["a2"], R["b2"], R["f2"],
        W["ffn2_w_gate"], W["ffn2_w_up"], W["ffn2_w_down"])
    G["w_out"] = _mm_plain("dw_out", D_MODEL, D_MODEL, S, D_MODEL, D_MODEL, tk_s, R["y"], "km", dx2, "kn", BF16)
    dy = _mm_plain("dy_mix", S, D_MODEL, D_MODEL, tm, D_MODEL, D_MODEL, dx2, "mk", W["w_out"], "nk", F32)
    p = R["p"]
    dhr, dgate, do, dz, G["gdn_norm"] = _mix_out_bwd(dy, R["h_f"], R["h_b"], R["o_f"], R["o_b"], p, W["gdn_norm"])
    lam_b, lam_f = _rg_scan_adj("rg_scan_bwd", R["a_b"], dhr, R["a_f"], dhr)
    dpre, dxc_direct, d_rgprm = _rg_gates_bwd(R["xc"], R["bd"], R["rg_prm"], lam_f, lam_b, R["h_f"], R["h_b"])
    tmg = _tile(S, 512)
    dxc = _fused_mm("rg_dxc", S, RG_W, 4 * RG_W, tmg, RG_W, 4 * RG_W, [(dpre, "mk"), (R["bd"], "nk")], [(0, 1, 0)],
                    [(dxc_direct, (tmg, RG_W), _mn)], [(jax.ShapeDtypeStruct((S, RG_W), F32), (tmg, RG_W), _mn)],
                    lambda i, accs, ex, out: out[0].__setitem__(Ellipsis, ex[0][...] + accs[0][...]))[0]
    d_bd = _mm_plain("rg_dbd", RG_W, 4 * RG_W, S, RG_W, 4 * RG_W, tk_s, R["xc"], "km", dpre, "kn", F32)
    dx_rg, G["rg_conv_w"], G["rg_conv_b"] = _conv_bwd("rg_conv_bwd", p, 0, W["rg_conv_w"], [dxc], "bias")
    blocks = jnp.einsum("nigmj,nm->gnij", d_bd.reshape(RG_BLOCKS, RG_BLOCK, 4, RG_BLOCKS, RG_BLOCK),
                        jnp.eye(RG_BLOCKS, dtype=F32))
    G["rg_gate_a_w"] = jnp.stack([blocks[0], blocks[2]])
    G["rg_gate_x_w"] = jnp.stack([blocks[1], blocks[3]])
    G["rg_gate_a_b"] = jnp.stack([d_rgprm[0], d_rgprm[2]])
    G["rg_gate_x_b"] = jnp.stack([d_rgprm[1], d_rgprm[3]])
    G["rg_lambda"] = d_rgprm[4:6]
    adj = _gdn_scan_bwd(R["q"], R["k"], R["bg"], R["gdn_loc"], do)
    dq, dk, dv, dbg = _gdn_local_bwd(R["q"], R["k"], R["v"], R["bg"], R["gcr"], do, R["gdn_loc"], R["gdn_fwd"], adj)
    cw = W["gdn_conv_w"]
    dpq, dwq, _ = _conv_bwd("gdn_conv_q_bwd", p, 2, cw[:, 0:512], [dq], "q")
    dpk, dwk, _ = _conv_bwd("gdn_conv_k_bwd", p, 3, cw[:, 512:1024], [dk], "k")
    dpv, dwv, _ = _conv_bwd("gdn_conv_v_bwd", p, 4, cw[:, 1024:1536], [dv], "v")
    G["gdn_conv_w"] = jnp.concatenate([dwq, dwk, dwv], axis=1)
    dba, d_gprm = _gdn_prep_bwd(dbg, p, R["gdn_prm"])
    G["gdn_a_log"] = d_gprm[0, 8:16].reshape(2, GDN_H)
    G["gdn_dt_bias"] = d_gprm[1, 8:16].reshape(2, GDN_H)
    dp = jnp.concatenate([dx_rg, dgate, dpq, dpk, dpv, dz, dba], axis=1)
    G["w_in"] = _mm_plain("dw_in", D_MODEL, D_IN_PAD, S, D_MODEL, 640, tk_s, R["h2"], "km", dp, "kn", BF16)

    def epi_dx1(i, accs, ex, out):
        dx, dgt = _rmsnorm_bwd_tile(accs[0][...], ex[0][...], ex[1][...])
        out[0][...] = ex[2][...] + dx
        _colsum_into(out[1], i, jnp.sum(dgt, axis=0, keepdims=True))

    dx1, G["mix_norm"] = _fused_mm(
        "mix_dx", S, D_MODEL, D_IN_PAD, tm, D_MODEL, 640, [(dp, "mk"), (W["w_in"], "nk")], [(0, 1, 0)],
        [(R["x1"], (tm, D_MODEL), _mn), (W["mix_norm"], (1, D_MODEL), _row0), (dx2, (tm, D_MODEL), _mn)],
        [(jax.ShapeDtypeStruct((S, D_MODEL), F32), (tm, D_MODEL), _mn),
         (jax.ShapeDtypeStruct((1, D_MODEL), F32), (1, D_MODEL), _row0)], epi_dx1)
    dx0, G["ffn1_norm"], G["ffn1_w_gate"], G["ffn1_w_up"], G["ffn1_w_down"] = _ffn_bwd(
        "ffn1b", dx1, x0, W["ffn1_norm"], R["h1"], R["a1"], R["b1"], R["f1"],
        W["ffn1_w_gate"], W["ffn1_w_up"], W["ffn1_w_down"])
    G["final_norm"] = R["d_final_norm"]
    return dx0, G


def _mesh_pos():
    x, y, c = lax.axis_index("x"), lax.axis_index("y"), lax.axis_index("c")
    return x, y, c, 4 * x + 2 * y + c


def _peer(x, y, c, r):
    px = 1 - x if r & 4 else x
    py = 1 - y if r & 2 else y
    pc = 1 - c if r & 1 else c
    return (px, py, pc), 4 * px + 2 * py + pc


def _exchange(name, arrays, all_to_all):
    n = len(arrays)

    def body(*refs):
        ins = refs[:n]
        outs = refs[n:2 * n]
        send_sems, recv_sems, local_sems = refs[2 * n:]
        x, y, c, me = _mesh_pos()
        started = []
        for a in range(n):
            src_local = ins[a].at[me] if all_to_all else ins[a]
            loc = pltpu.make_async_copy(src_local, outs[a].at[me], local_sems.at[a])
            loc.start()
            started.append(loc)
        sends = []
        for a in range(n):
            for r in range(1, N_DEV):
                peer, peer_idx = _peer(x, y, c, r)
                src = ins[a].at[peer_idx] if all_to_all else ins[a]
                cp = pltpu.make_async_remote_copy(
                    src_ref=src, dst_ref=outs[a].at[me], send_sem=send_sems.at[a * 7 + r - 1],
                    recv_sem=recv_sems.at[a * 7 + r - 1], device_id=peer, device_id_type=pl.DeviceIdType.MESH)
                cp.start()
                sends.append(cp)
        for a in range(n):
            for r in range(1, N_DEV):
                peer, peer_idx = _peer(x, y, c, r)
                src = ins[a].at[peer_idx] if all_to_all else ins[a]
                pltpu.make_async_remote_copy(
                    src_ref=src, dst_ref=outs[a].at[peer_idx], send_sem=send_sems.at[a * 7 + r - 1],
                    recv_sem=recv_sems.at[a * 7 + r - 1], device_id=peer, device_id_type=pl.DeviceIdType.MESH).wait_recv()
        for cp in sends:
            cp.wait_send()
        for loc in started:
            loc.wait()

    any_spec = pl.BlockSpec(memory_space=pl.ANY)
    out_shape = [jax.ShapeDtypeStruct(a.shape if all_to_all else (N_DEV,) + a.shape, a.dtype) for a in arrays]
    return pl.pallas_call(
        body, name=name, in_specs=[any_spec] * n, out_specs=[any_spec] * n, out_shape=out_shape,
        scratch_shapes=[pltpu.SemaphoreType.DMA((7 * n,)), pltpu.SemaphoreType.DMA((7 * n,)),
                        pltpu.SemaphoreType.DMA((n,))],
        compiler_params=pltpu.CompilerParams(has_side_effects=True),
    )(*arrays)


def _adamw_math(w, g, m, v):
    m2 = ADAM_B1 * m + (1.0 - ADAM_B1) * g
    v2 = ADAM_B2 * v + (1.0 - ADAM_B2) * (g * g)
    m_hat = m2 / (1.0 - ADAM_B1 ** ADAM_STEP)
    v_hat = v2 / (1.0 - ADAM_B2 ** ADAM_STEP)
    delta = -ADAM_LR * (m_hat / (jnp.sqrt(v_hat) + ADAM_EPS) + ADAM_WD * w)
    return delta, m2, v2


def _adamw_slabs(name, slabs, w, m, v, tr):
    R, C = w.shape

    def body(s_ref, w_ref, m_ref, v_ref, g_ref, d_ref, m2_ref, v2_ref):
        g = s_ref[0].astype(F32)
        for s in range(1, N_DEV):
            g = g + s_ref[s].astype(F32)
        delta, m2, v2 = _adamw_math(w_ref[...], g, m_ref[...], v_ref[...])
        g_ref[...] = g
        d_ref[...] = delta
        m2_ref[...] = m2
        v2_ref[...] = v2

    im = lambda i: (i, 0)
    sds = jax.ShapeDtypeStruct((R, C), F32)
    ins = [(slabs, (N_DEV, tr, C), lambda i: (0, i, 0)), (w, (tr, C), im), (m, (tr, C), im), (v, (tr, C), im)]
    return _rows(name, R, tr, ins, [(sds, (tr, C), im)] * 4, body)


def _sum_slots(name, slots):
    _, R, C = slots.shape

    def body(s_ref, o_ref):
        g = s_ref[0]
        for s in range(1, N_DEV):
            g = g + s_ref[s]
        o_ref[...] = g

    return _rows(name, R, R, [(slots, (N_DEV, R, C), lambda i: (0, 0, 0))],
                 [(jax.ShapeDtypeStruct((R, C), F32), (R, C), lambda i: (0, 0))], body)[0]


def _adamw_packed(name, g, w, m, v):
    R, C = g.shape

    def body(g_ref, w_ref, m_ref, v_ref, d_ref, m2_ref, v2_ref):
        delta, m2, v2 = _adamw_math(w_ref[...], g_ref[...], m_ref[...], v_ref[...])
        d_ref[...] = delta
        m2_ref[...] = m2
        v2_ref[...] = v2

    im = lambda i: (0, 0)
    sds = jax.ShapeDtypeStruct((R, C), F32)
    return _rows(name, R, R, [(a, (R, C), im) for a in (g, w, m, v)], [(sds, (R, C), im)] * 3, body)


def _pack(arrays):
    rows = []
    for a in arrays:
        flat = a.reshape(-1).astype(F32)
        pad = (-flat.shape[0]) % 128
        rows.append(jnp.pad(flat, (0, pad)).reshape(-1, 128))
    out = jnp.concatenate(rows, axis=0)
    return jnp.pad(out, ((0, (-out.shape[0]) % 8), (0, 0)))


def _unpack(packed, shapes):
    lead = packed.shape[:-2]
    outs = []
    r = 0
    for shp in shapes:
        n = math.prod(shp)
        nr = -(-n // 128)
        flat = packed[..., r:r + nr, :].reshape(lead + (nr * 128,))[..., :n]
        outs.append(flat.reshape(lead + tuple(shp)))
        r += nr
    return outs


BIG = ["ffn1_w_gate", "ffn1_w_up", "ffn1_w_down", "w_in", "w_out", "ffn2_w_gate", "ffn2_w_up", "ffn2_w_down"]
COL_SHARDED = {"ffn1_w_gate", "ffn1_w_up", "w_in", "ffn2_w_gate", "ffn2_w_up"}
SMALL_SHARDED = ["rg_conv_w", "rg_gate_a_b", "rg_gate_x_b", "rg_lambda", "gdn_conv_w"]
WEIGHTS = ["ffn1_norm", "ffn1_w_gate", "ffn1_w_up", "ffn1_w_down", "mix_norm", "w_in", "w_out", "rg_conv_w", "rg_conv_b",
           "rg_gate_a_w", "rg_gate_a_b", "rg_gate_x_w", "rg_gate_x_b", "rg_lambda", "gdn_conv_w", "gdn_a_log",
           "gdn_dt_bias", "gdn_norm", "ffn2_norm", "ffn2_w_gate", "ffn2_w_up", "ffn2_w_down", "final_norm"]
SMALL = [n for n in WEIGHTS if n not in BIG]
ROW_VECTORS = {"ffn1_norm", "mix_norm", "ffn2_norm", "gdn_norm", "rg_conv_b", "final_norm"}
ROW_TILE = {"ffn1_w_gate": 256, "ffn1_w_up": 256, "ffn1_w_down": 176, "w_in": 256, "w_out": 64,
            "ffn2_w_gate": 256, "ffn2_w_up": 256, "ffn2_w_down": 176}


def _unshard_cols(g):
    return g.transpose(1, 0, 2).reshape(g.shape[1], N_DEV * g.shape[2])


def _to_slabs(name, g):
    if name in COL_SHARDED:
        r, ctot = g.shape
        return g.reshape(r, N_DEV, ctot // N_DEV).transpose(1, 0, 2)
    return g.reshape(N_DEV, g.shape[0] // N_DEV, g.shape[1])


def _step(x, target, w, m, v):
    _, _, _, me = _mesh_pos()
    small_shards = [w[n] for n in SMALL_SHARDED]
    gathered = _exchange("gather_weights", [w[n].astype(BF16) for n in BIG] + [_pack(small_shards)], False)
    W = {}
    for n, gth in zip(BIG, gathered[:len(BIG)]):
        W[n] = _unshard_cols(gth) if n in COL_SHARDED else gth.reshape(-1, gth.shape[-1])
    W["w_in"] = jnp.pad(W["w_in"], ((0, 0), (0, D_IN_PAD - D_IN)))
    for n, gth in zip(SMALL_SHARDED, _unpack(gathered[-1], [s.shape for s in small_shards])):
        W[n] = jnp.moveaxis(gth, 0, -2).reshape(gth.shape[1:-1] + (N_DEV * gth.shape[-1],))
    for n in SMALL:
        if n not in SMALL_SHARDED:
            W[n] = w[n]
    R = _layer_fwd(x, target, W)
    grad_x, G = _layer_bwd(x, W, R)
    loss = lax.psum(R["loss"][0, 0], ("x", "y", "c"))
    G["w_in"] = G["w_in"][:, :D_IN]
    received = _exchange("scatter_grads", [_to_slabs(n, G[n]) for n in BIG], True)
    out = {}
    for n, slabs in zip(BIG, received):
        out[n] = _adamw_slabs(f"adamw_{n}", slabs, w[n], m[n], v[n], ROW_TILE[n])
    full_shapes = [G[n].shape for n in SMALL]
    slots = _exchange("gather_small_grads", [_pack([G[n] for n in SMALL])], False)[0]
    reduced = dict(zip(SMALL, _unpack(_sum_slots("sum_small_grads", slots), full_shapes)))
    g_small = []
    for n in SMALL:
        g = reduced[n]
        if n in SMALL_SHARDED:
            per = g.shape[-1] // N_DEV
            g = lax.dynamic_slice_in_dim(g, me * per, per, axis=g.ndim - 1)
        g_small.append(g.reshape(w[n].shape))
    shapes = [w[n].shape for n in SMALL]
    d_p, m_p, v_p = _adamw_packed("adamw_small", _pack(g_small), _pack([w[n] for n in SMALL]),
                                  _pack([m[n] for n in SMALL]), _pack([v[n] for n in SMALL]))
    for n, g, d_, m_, v_ in zip(SMALL, g_small, _unpack(d_p, shapes), _unpack(m_p, shapes), _unpack(v_p, shapes)):
        out[n] = (g, d_, m_, v_)
    return loss, grad_x, out


def kernel(x, ffn1_norm, ffn1_w_gate, ffn1_w_up, ffn1_w_down, mix_norm, w_in, w_out, rg_conv_w, rg_conv_b, rg_gate_a_w, rg_gate_a_b, rg_gate_x_w, rg_gate_x_b, rg_lambda, gdn_conv_w, gdn_a_log, gdn_dt_bias, gdn_norm, ffn2_norm, ffn2_w_gate, ffn2_w_up, ffn2_w_down, final_norm, loss_target, m_ffn1_norm, m_ffn1_w_gate, m_ffn1_w_up, m_ffn1_w_down, m_mix_norm, m_w_in, m_w_out, m_rg_conv_w, m_rg_conv_b, m_rg_gate_a_w, m_rg_gate_a_b, m_rg_gate_x_w, m_rg_gate_x_b, m_rg_lambda, m_gdn_conv_w, m_gdn_a_log, m_gdn_dt_bias, m_gdn_norm, m_ffn2_norm, m_ffn2_w_gate, m_ffn2_w_up, m_ffn2_w_down, m_final_norm, v_ffn1_norm, v_ffn1_w_gate, v_ffn1_w_up, v_ffn1_w_down, v_mix_norm, v_w_in, v_w_out, v_rg_conv_w, v_rg_conv_b, v_rg_gate_a_w, v_rg_gate_a_b, v_rg_gate_x_w, v_rg_gate_x_b, v_rg_lambda, v_gdn_conv_w, v_gdn_a_log, v_gdn_dt_bias, v_gdn_norm, v_ffn2_norm, v_ffn2_w_gate, v_ffn2_w_up, v_ffn2_w_down, v_final_norm):
    args = dict(locals())
    orig_shapes = {n: args[n].shape for n in WEIGHTS}

    def local(prefix):
        d = {}
        for n in WEIGHTS:
            a = args[prefix + n]
            d[n] = a.reshape(1, -1) if n in ROW_VECTORS else a[0]
        return d

    loss, grad_x, out = _step(x[0], loss_target[0], local(""), local("m_"), local("v_"))
    res = [loss, grad_x[None]]
    for k in range(4):
        res += [out[n][k].reshape(orig_shapes[n]) for n in WEIGHTS]
    return tuple(res)
```

```python
import functools
import math

import jax
import jax.numpy as jnp
from jax import lax
from jax.experimental import pallas as pl
from jax.experimental.pallas import tpu as pltpu

F32, BF16 = jnp.float32, jnp.bfloat16

D_MODEL = 1024
D_FF = 2816
RG_W = 512
RG_BLOCKS = 8
RG_BLOCK = 64
RG_C = 8.0
CONV_W = 4
GDN_H = 4
GDN_DK = 128
CHUNK = 64
EPS = 1e-6
D_IN = 3088
D_IN_PAD = 3200
COL_BA = 3072
N_DEV = 8
HALO = 8
VMEM_LIMIT = 48 * 1024 * 1024

ADAM_LR = 0.001
ADAM_B1 = 0.9
ADAM_B2 = 0.999
ADAM_EPS = 1e-08
ADAM_WD = 0.01
ADAM_STEP = 10

HI = lax.Precision.HIGHEST


def _cp(n):
    return pltpu.CompilerParams(dimension_semantics=("arbitrary",) * n, vmem_limit_bytes=VMEM_LIMIT)


def _tile(n, pref):
    return min(n, pref)


def _sigmoid(x):
    return jax.nn.sigmoid(x)


def _softplus(x):
    return jnp.maximum(x, 0.0) + jnp.log(1.0 + jnp.exp(-jnp.abs(x)))


def _dot(a, b, ca, cb, prec=None):
    return lax.dot_general(a, b, (((ca,), (cb,)), ((), ())), preferred_element_type=F32, precision=prec)


def _fused_mm(name, M, N, K, tm, tn, tk, ops, pairs, extras, outs, epilogue):
    nm, nn, nk = M // tm, N // tn, K // tk
    assert nm * tm == M and nn * tn == N and nk * tk == K, (name, M, N, K, tm, tn, tk)
    spec_of = {
        "mk": pl.BlockSpec((tm, tk), lambda i, j, k: (i, k)),
        "km": pl.BlockSpec((tk, tm), lambda i, j, k: (k, i)),
        "kn": pl.BlockSpec((tk, tn), lambda i, j, k: (k, j)),
        "nk": pl.BlockSpec((tn, tk), lambda i, j, k: (j, k)),
    }
    in_specs = [spec_of[m] for _, m in ops]
    in_specs += [pl.BlockSpec(bs, lambda i, j, k, im=im: im(i, j)) for _, bs, im in extras]
    out_specs = [pl.BlockSpec(bs, lambda i, j, k, im=im: im(i, j)) for _, bs, im in outs]
    n_ops, n_ex, n_out = len(ops), len(extras), len(outs)
    n_acc = 1 + max(g for _, _, g in pairs)
    modes = [m for _, m in ops]

    def body(*refs):
        op_refs = refs[:n_ops]
        ex_refs = refs[n_ops:n_ops + n_ex]
        out_refs = refs[n_ops + n_ex:n_ops + n_ex + n_out]
        accs = refs[n_ops + n_ex + n_out:]
        i = pl.program_id(0)
        k = pl.program_id(2)

        @pl.when(k == 0)
        def _():
            for a in accs:
                a[...] = jnp.zeros_like(a)

        vals = [r[...].astype(BF16) for r in op_refs]
        for ia, ib, g in pairs:
            ca = 1 if modes[ia] == "mk" else 0
            cb = 0 if modes[ib] == "kn" else 1
            accs[g][...] += _dot(vals[ia], vals[ib], ca, cb)

        @pl.when(k == nk - 1)
        def _():
            epilogue(i, accs, ex_refs, out_refs)

    res = pl.pallas_call(
        body, name=name, grid=(nm, nn, nk), in_specs=in_specs, out_specs=out_specs,
        out_shape=[o for o, _, _ in outs],
        scratch_shapes=[pltpu.VMEM((tm, tn), F32)] * n_acc,
        compiler_params=_cp(3),
    )(*[a for a, _ in ops], *[a for a, _, _ in extras])
    return res


def _mn(i, j):
    return (i, j)


def _row0(i, j):
    return (0, 0)


def _rows(name, S, ts, ins, outs, body, scratch=()):
    return pl.pallas_call(
        body, name=name, grid=(S // ts,),
        in_specs=[pl.BlockSpec(bs, im) for _, bs, im in ins],
        out_specs=[pl.BlockSpec(bs, im) for _, bs, im in outs],
        out_shape=[o for o, _, _ in outs],
        scratch_shapes=list(scratch),
        compiler_params=_cp(1),
    )(*[a for a, _, _ in ins])


def _halo_ins(arr, S, ts, width, colblk):
    per = ts // HALO
    last = S // HALO - 1
    return [
        (arr, (ts, width), lambda i: (i, colblk)),
        (arr, (HALO, width), lambda i: (jnp.maximum(i * per - 1, 0), colblk)),
        (arr, (HALO, width), lambda i: (jnp.minimum((i + 1) * per, last), colblk)),
    ]


def _ext(main_ref, prev_ref, next_ref, i, n_tiles):
    prev = jnp.where(i > 0, prev_ref[...].astype(F32), 0.0)
    nxt = jnp.where(i < n_tiles - 1, next_ref[...].astype(F32), 0.0)
    return jnp.concatenate([prev, main_ref[...].astype(F32), nxt], axis=0)


def _shift(ext, off, ts):
    n = ext.shape[0]
    if off == 0:
        return ext[HALO:HALO + ts]
    return pltpu.roll(ext, (-off) % n, 0)[HALO:HALO + ts]


def _rmsnorm_fwd(name, x, g):
    S, D = x.shape
    ts = _tile(S, 512)

    def body(x_ref, g_ref, o_ref):
        xv = x_ref[...]
        r = lax.rsqrt(jnp.mean(xv * xv, axis=-1, keepdims=True) + EPS)
        o_ref[...] = (xv * r * g_ref[...]).astype(BF16)

    return _rows(name, S, ts,
                 [(x, (ts, D), lambda i: (i, 0)), (g, (1, D), lambda i: (0, 0))],
                 [(jax.ShapeDtypeStruct((S, D), BF16), (ts, D), lambda i: (i, 0))], body)[0]


def _rmsnorm_bwd_tile(dh, x, g):
    r = lax.rsqrt(jnp.mean(x * x, axis=-1, keepdims=True) + EPS)
    xhat = x * r
    dxn = dh * g
    dx = r * (dxn - xhat * jnp.mean(dxn * xhat, axis=-1, keepdims=True))
    return dx, dh * xhat


def _ffn_fwd(tag, x, h, wg, wu, wd):
    S = x.shape[0]
    tm = _tile(S, 512)
    tn = 1408

    def epi_up(i, accs, ex, out):
        a = accs[0][...]
        b = accs[1][...]
        out[0][...] = a.astype(BF16)
        out[1][...] = b.astype(BF16)
        out[2][...] = (a * _sigmoid(a) * b).astype(BF16)

    sds = jax.ShapeDtypeStruct((S, D_FF), BF16)
    a, b, f = _fused_mm(f"{tag}_up", S, D_FF, D_MODEL, tm, tn, D_MODEL,
                        [(h, "mk"), (wg, "kn"), (wu, "kn")], [(0, 1, 0), (0, 2, 1)], [],
                        [(sds, (tm, tn), _mn)] * 3, epi_up)

    def epi_down(i, accs, ex, out):
        out[0][...] = ex[0][...] + 0.5 * accs[0][...]

    xo = _fused_mm(f"{tag}_down", S, D_MODEL, D_FF, tm, D_MODEL, 1408,
                   [(f, "mk"), (wd, "kn")], [(0, 1, 0)], [(x, (tm, D_MODEL), _mn)],
                   [(jax.ShapeDtypeStruct((S, D_MODEL), F32), (tm, D_MODEL), _mn)], epi_down)[0]
    return xo, a, b, f


def _conv_taps(ext, w_ref, ts):
    acc = None
    for j in range(CONV_W):
        term = w_ref[j:j + 1, :] * _shift(ext, j - 2, ts)
        acc = term if acc is None else acc + term
    return acc


def _l2norm_heads(s, scale):
    outs = []
    for h in range(GDN_H):
        sh = s[:, h * GDN_DK:(h + 1) * GDN_DK]
        outs.append(sh * (lax.rsqrt(jnp.sum(sh * sh, axis=-1, keepdims=True) + EPS) * scale))
    return jnp.concatenate(outs, axis=-1)


def _conv_fwd(name, p, colblk, w, bias, mode):
    S = p.shape[0]
    ts = _tile(S, 512)
    n_tiles = S // ts
    C = w.shape[1]

    def body(main, prev, nxt, w_ref, b_ref, o_ref):
        i = pl.program_id(0)
        c = _conv_taps(_ext(main, prev, nxt, i, n_tiles), w_ref, ts)
        if mode == "bias":
            o_ref[...] = c + b_ref[...]
        else:
            s = c * _sigmoid(c)
            if mode == "q":
                s = _l2norm_heads(s, GDN_DK ** -0.5)
            elif mode == "k":
                s = _l2norm_heads(s, 1.0)
            o_ref[...] = s

    ins = _halo_ins(p, S, ts, C, colblk) + [(w, (CONV_W, C), lambda i: (0, 0)), (bias, (1, C), lambda i: (0, 0))]
    return _rows(name, S, ts, ins, [(jax.ShapeDtypeStruct((S, C), F32), (ts, C), lambda i: (i, 0))], body)[0]


def _rg_gate_terms(pre, xc, prm_ref, d):
    r = _sigmoid(pre[:, d * 1024:d * 1024 + RG_W] + prm_ref[2 * d:2 * d + 1, :])
    ig = _sigmoid(pre[:, d * 1024 + RG_W:(d + 1) * 1024] + prm_ref[2 * d + 1:2 * d + 2, :])
    sp = _softplus(-prm_ref[4 + d:5 + d, :])
    log_a = -RG_C * r * sp
    a = jnp.exp(log_a)
    t = jnp.tanh(log_a)
    sq = jnp.sqrt(-2.0 * t / (1.0 - t))
    return r, ig, sp, a, sq


def _rg_gates_fwd(xc, bd, prm):
    S = xc.shape[0]
    tm = _tile(S, 256)

    def epi(i, accs, ex, out):
        pre = accs[0][...]
        xv = ex[0][...]
        for d in range(2):
            r, ig, sp, a, sq = _rg_gate_terms(pre, xv, ex[1], d)
            out[2 * d][...] = a
            out[2 * d + 1][...] = sq * ig * xv

    sds = jax.ShapeDtypeStruct((S, RG_W), F32)
    blk = (tm, RG_W)
    im = lambda i, j: (i, 0)
    return _fused_mm("rg_gates_fwd", S, 4 * RG_W, RG_W, tm, 4 * RG_W, RG_W,
                     [(xc, "mk"), (bd, "kn")], [(0, 1, 0)],
                     [(xc, blk, im), (prm, (8, RG_W), _row0)], [(sds, blk, im)] * 4, epi)


def _rg_scan(name, a_f, b_f, a_b, b_b):
    S, C = a_f.shape
    ts = _tile(S, 512)
    n_tiles = S // ts

    def body(af, bf, ab, bb, hf, hb, carry):
        @pl.when(pl.program_id(0) == 0)
        def _():
            carry[...] = jnp.zeros_like(carry)

        def step(t, c):
            cf, cb = c
            cf = af[pl.ds(t, 1), :] * cf + bf[pl.ds(t, 1), :]
            hf[pl.ds(t, 1), :] = cf
            tb = ts - 1 - t
            cb = ab[pl.ds(tb, 1), :] * cb + bb[pl.ds(tb, 1), :]
            hb[pl.ds(tb, 1), :] = cb
            return cf, cb

        cf, cb = lax.fori_loop(0, ts, step, (carry[0:1, :], carry[1:2, :]), unroll=8)
        carry[0:1, :] = cf
        carry[1:2, :] = cb

    fw = lambda i: (i, 0)
    bw = lambda i: (n_tiles - 1 - i, 0)
    sds = jax.ShapeDtypeStruct((S, C), F32)
    return _rows(name, S, ts,
                 [(a_f, (ts, C), fw), (b_f, (ts, C), fw), (a_b, (ts, C), bw), (b_b, (ts, C), bw)],
                 [(sds, (ts, C), fw), (sds, (ts, C), bw)], body, scratch=[pltpu.VMEM((8, C), F32)])


def _tri_masks():
    ri = lax.broadcasted_iota(jnp.int32, (CHUNK, CHUNK), 0)
    ci = lax.broadcasted_iota(jnp.int32, (CHUNK, CHUNK), 1)
    return ri, ci


def _gdn_prep_fwd(p, prm):
    S = p.shape[0]
    ts = _tile(S, 512)

    def body(p_ref, prm_ref, o_ref):
        raw = p_ref[...]
        lane = lax.broadcasted_iota(jnp.int32, (1, 128), 1)
        g = -jnp.exp(prm_ref[0:1, :]) * _softplus(raw + prm_ref[1:2, :])
        g = jnp.where((lane >= 8) & (lane < 16), g, 0.0)
        beta = _sigmoid(raw)
        ri, ci = _tri_masks()
        lower = (ri >= ci).astype(F32)
        upper = (ri <= ci).astype(F32)
        for c in range(ts // CHUNK):
            rows = slice(c * CHUNK, (c + 1) * CHUNK)
            gch = g[rows]
            gc = jnp.where(lane < 12, _dot(lower, gch, 1, 0, HI), _dot(upper, gch, 1, 0, HI))
            o_ref[rows, :] = jnp.where(lane < 8, beta[rows], gc)

    return _rows("gdn_prep_fwd", S, ts,
                 [(p, (ts, 128), lambda i: (i, COL_BA // 128)), (prm, (8, 128), lambda i: (0, 0))],
                 [(jax.ShapeDtypeStruct((S, 128), F32), (ts, 128), lambda i: (i, 0))], body)[0]


def _bdot(a, b, ca, cb):
    return _dot(a.astype(BF16), b.astype(BF16), ca, cb)


GDN_W = GDN_H * GDN_DK
GDN_TS = 256


def _gdn_decay(bg_ref, gcr_ref, c, rows, r0, col, rev, ri, ci):
    beta = bg_ref[rows, col:col + 1]
    gc = bg_ref[rows, 8 + col:9 + col]
    last = 0 if rev else CHUNK - 1
    gl = bg_ref[pl.ds(r0 + last, 1), 8 + col:9 + col]
    out = dict(beta=beta, gc=gc, gl=gl, eg=jnp.exp(gc), egl=jnp.exp(gl - gc), cd=jnp.exp(gl))
    if gcr_ref is not None:
        incl = (ri <= ci) if rev else (ri >= ci)
        out["strict"] = (ri < ci) if rev else (ri > ci)
        out["dm"] = jnp.where(incl, jnp.exp(jnp.where(incl, gc - gcr_ref[c, col:col + 1, :], 0.0)), 0.0)
    return out


def _dir_tile(d, n_tiles, flip):
    if (d == 1) != flip:
        return lambda i: n_tiles - 1 - i
    return lambda i: i


def _gdn_local_fwd(q, k, v, bg, gcr):
    S = q.shape[0]
    ts = _tile(S, GDN_TS)
    ncb = ts // CHUNK
    nch = S // CHUNK

    def body(q_ref, k_ref, v_ref, bg_ref, gcr_ref, u0, w0, a0, t0, u1, w1, a1, t1):
        ri, ci = _tri_masks()
        eye = (ri == ci).astype(F32)
        outs = ((u0, w0, a0, t0), (u1, w1, a1, t1))

        def chunk(c, carry):
            r0 = pl.multiple_of(c * CHUNK, CHUNK)
            rows = pl.ds(r0, CHUNK)
            chains = []
            for h in range(GDN_H):
                cols = slice(h * GDN_DK, (h + 1) * GDN_DK)
                qh, kh, vh = q_ref[rows, cols], k_ref[rows, cols], v_ref[rows, cols]
                both = _bdot(jnp.concatenate([qh, kh], axis=0), kh, 1, 1)
                for d in range(2):
                    chains.append(dict(h=h, d=d, cols=cols, kh=kh, vh=vh, qk=both[0:CHUNK], kk=both[CHUNK:2 * CHUNK]))
            for ch in chains:
                m = _gdn_decay(bg_ref, gcr_ref, c, rows, r0, ch["d"] * GDN_H + ch["h"], ch["d"] == 1, ri, ci)
                ch["m"] = m
                ch["x"] = -jnp.where(m["strict"], m["beta"] * ch["kk"] * m["dm"], 0.0)
                ch["t"] = eye + ch["x"]
            for ch in chains:
                ch["pw"] = _bdot(ch["x"], ch["x"], 1, 0)
            for level in range(1, 6):
                last_level = level == 5
                for ch in chains:
                    rhs = ch["t"] if last_level else jnp.concatenate([ch["t"], ch["pw"]], axis=1)
                    ch["prod"] = _bdot(ch["pw"], rhs, 1, 0)
                for ch in chains:
                    ch["t"] = ch["t"] + ch["prod"][:, 0:CHUNK]
                    if not last_level:
                        ch["pw"] = ch["prod"][:, CHUNK:2 * CHUNK]
            for ch in chains:
                m = ch["m"]
                rhs = jnp.concatenate([ch["vh"] * m["beta"], ch["kh"] * (m["beta"] * m["eg"])], axis=1)
                ch["uw"] = _bdot(ch["t"], rhs, 1, 0)
            for ch in chains:
                u_ref, w_ref, a_ref, t_ref = outs[ch["d"]]
                u_ref[rows, ch["cols"]] = ch["uw"][:, 0:GDN_DK]
                w_ref[rows, ch["cols"]] = ch["uw"][:, GDN_DK:2 * GDN_DK].astype(BF16)
                a_ref[c, ch["h"]] = (ch["qk"] * ch["m"]["dm"]).astype(BF16)
                t_ref[c, ch["h"]] = ch["t"].astype(BF16)
            return carry

        lax.fori_loop(0, ncb, chunk, 0)

    im = lambda i: (i, 0)
    im4 = lambda i: (i, 0, 0, 0)
    ins = [(q, (ts, GDN_W), im), (k, (ts, GDN_W), im), (v, (ts, GDN_W), im), (bg, (ts, 128), im),
           (gcr, (ncb, 8, CHUNK), lambda i: (i, 0, 0))]
    per_dir = [(jax.ShapeDtypeStruct((S, GDN_W), F32), (ts, GDN_W), im),
               (jax.ShapeDtypeStruct((S, GDN_W), BF16), (ts, GDN_W), im),
               (jax.ShapeDtypeStruct((nch, GDN_H, CHUNK, CHUNK), BF16), (ncb, GDN_H, CHUNK, CHUNK), im4),
               (jax.ShapeDtypeStruct((nch, GDN_H, CHUNK, CHUNK), BF16), (ncb, GDN_H, CHUNK, CHUNK), im4)]
    res = _rows("gdn_local_fwd", S, ts, ins, per_dir * 2, body)
    return res[0:4], res[4:8]


def _gdn_scan_fwd(q, k, bg, loc):
    S = q.shape[0]
    ts = _tile(S, GDN_TS)
    n_tiles = S // ts
    ncb = ts // CHUNK
    nch = S // CHUNK

    def body(*refs):
        ins = (refs[0:6], refs[6:12])
        outs = (refs[12:15], refs[15:18])
        state = refs[18]

        @pl.when(pl.program_id(0) == 0)
        def _():
            state[...] = jnp.zeros_like(state)

        def chunk(cc, carry):
            chains = []
            for d in range(2):
                c = cc if d == 0 else ncb - 1 - cc
                r0 = pl.multiple_of(c * CHUNK, CHUNK)
                rows = pl.ds(r0, CHUNK)
                for h in range(GDN_H):
                    cols = slice(h * GDN_DK, (h + 1) * GDN_DK)
                    m = _gdn_decay(ins[d][2], None, c, rows, r0, d * GDN_H + h, d == 1, None, None)
                    chains.append(dict(d=d, h=h, c=c, rows=rows, cols=cols, m=m, st=state[d * GDN_H + h]))
            for ch in chains:
                q_ref, k_ref, bg_ref, u_ref, w_ref, a_ref = ins[ch["d"]]
                rows, cols = ch["rows"], ch["cols"]
                lhs = jnp.concatenate([w_ref[rows, cols], (q_ref[rows, cols] * ch["m"]["eg"]).astype(BF16)], axis=0)
                ch["ws_qs"] = _dot(lhs, ch["st"].astype(BF16), 1, 0)
            for ch in chains:
                q_ref, k_ref, bg_ref, u_ref, w_ref, a_ref = ins[ch["d"]]
                rows, cols = ch["rows"], ch["cols"]
                vn = u_ref[rows, cols] - ch["ws_qs"][0:CHUNK]
                vnb = vn.astype(BF16)
                ch["vn"] = vn
                ch["avn"] = _dot(a_ref[ch["c"], ch["h"]], vnb, 1, 0)
                ch["kvn"] = _bdot(k_ref[rows, cols] * ch["m"]["egl"], vnb, 0, 0)
            for ch in chains:
                o_ref, vn_ref, s_ref = outs[ch["d"]]
                rows, cols = ch["rows"], ch["cols"]
                o_ref[rows, cols] = ch["ws_qs"][CHUNK:2 * CHUNK] + ch["avn"]
                vn_ref[rows, cols] = ch["vn"]
                s_ref[ch["c"], ch["h"]] = ch["st"]
                state[ch["d"] * GDN_H + ch["h"]] = ch["st"] * ch["m"]["cd"] + ch["kvn"]
            return carry

        lax.fori_loop(0, ncb, chunk, 0)

    ins, outs = [], []
    for d in range(2):
        tix = _dir_tile(d, n_tiles, False)
        im = lambda i, tix=tix: (tix(i), 0)
        im4 = lambda i, tix=tix: (tix(i), 0, 0, 0)
        u, w, a, _ = loc[d]
        ins += [(q, (ts, GDN_W), im), (k, (ts, GDN_W), im), (bg, (ts, 128), im), (u, (ts, GDN_W), im),
                (w, (ts, GDN_W), im), (a, (ncb, GDN_H, CHUNK, CHUNK), im4)]
        outs += [(jax.ShapeDtypeStruct((S, GDN_W), F32), (ts, GDN_W), im),
                 (jax.ShapeDtypeStruct((S, GDN_W), F32), (ts, GDN_W), im),
                 (jax.ShapeDtypeStruct((nch, GDN_H, GDN_DK, GDN_DK), F32), (ncb, GDN_H, GDN_DK, GDN_DK), im4)]
    res = _rows("gdn_scan_fwd", S, ts, ins, outs, body, scratch=[pltpu.VMEM((2 * GDN_H, GDN_DK, GDN_DK), F32)])
    return res[0:3], res[3:6]


def _gelu(x):
    c = math.sqrt(2.0 / math.pi)
    t = jnp.tanh(c * (x + 0.044715 * x * x * x))
    return 0.5 * x * (1.0 + t), t


def _mix_out_fwd(h_f, h_b, o_f, o_b, p, gn):
    S = h_f.shape[0]
    ts = _tile(S, 512)

    def body(hf, hb, of, ob, gate, z, gn_ref, y_ref):
        ge, _ = _gelu(gate[...])
        y_ref[:, 0:RG_W] = ((hf[...] + hb[...]) * ge).astype(BF16)
        o = of[...] + ob[...]
        zv = z[...]
        sz = zv * _sigmoid(zv)
        for h in range(GDN_H):
            cols = slice(h * GDN_DK, (h + 1) * GDN_DK)
            oh = o[:, cols]
            n = oh * lax.rsqrt(jnp.mean(oh * oh, axis=-1, keepdims=True) + EPS) * gn_ref[...]
            y_ref[:, RG_W + h * GDN_DK:RG_W + (h + 1) * GDN_DK] = (n * sz[:, cols]).astype(BF16)

    blk = (ts, RG_W)
    im = lambda i: (i, 0)
    ins = [(h_f, blk, im), (h_b, blk, im), (o_f, blk, im), (o_b, blk, im),
           (p, blk, lambda i: (i, 1)), (p, blk, lambda i: (i, 5)), (gn, (1, GDN_DK), lambda i: (0, 0))]
    return _rows("mix_out_fwd", S, ts, ins,
                 [(jax.ShapeDtypeStruct((S, D_MODEL), BF16), (ts, D_MODEL), im)], body)[0]


def _loss_head(x, target, g):
    S, D = x.shape
    ts = _tile(S, 512)

    def body(x_ref, t_ref, g_ref, dx_ref, loss_ref, dg_ref):
        @pl.when(pl.program_id(0) == 0)
        def _():
            loss_ref[...] = jnp.zeros_like(loss_ref)
            dg_ref[...] = jnp.zeros_like(dg_ref)

        xv = x_ref[...]
        gv = g_ref[...]
        r = lax.rsqrt(jnp.mean(xv * xv, axis=-1, keepdims=True) + EPS)
        err = xv * r * gv - t_ref[...]
        loss_ref[...] += jnp.sum(err * err) * (0.5 / D)
        dx, dgt = _rmsnorm_bwd_tile(err * (1.0 / D), xv, gv)
        dx_ref[...] = dx
        dg_ref[...] += jnp.sum(dgt, axis=0, keepdims=True)

    im = lambda i: (i, 0)
    z = lambda i: (0, 0)
    return _rows("loss_head", S, ts,
                 [(x, (ts, D), im), (target, (ts, D), im), (g, (1, D), z)],
                 [(jax.ShapeDtypeStruct((S, D), F32), (ts, D), im),
                  (jax.ShapeDtypeStruct((8, 128), F32), (8, 128), z),
                  (jax.ShapeDtypeStruct((1, D), F32), (1, D), z)], body)


def _block_diag(w):
    n = w.shape[0]
    return jnp.einsum("nij,nm->nimj", w, jnp.eye(n, dtype=w.dtype)).reshape(n * w.shape[1], n * w.shape[2])


def _rg_bd(a_w, x_w):
    return jnp.concatenate([_block_diag(a_w[0]), _block_diag(x_w[0]), _block_diag(a_w[1]), _block_diag(x_w[1])],
                           axis=1).astype(BF16)


def _rg_prm(ba, bx, lam):
    return jnp.concatenate([ba[0:1], bx[0:1], ba[1:2], bx[1:2], lam, jnp.zeros((2, RG_W), F32)], axis=0)


def _gdn_prm(a_log, dt_bias):
    rows = jnp.zeros((8, 128), F32)
    rows = rows.at[0, 8:16].set(a_log.reshape(-1))
    return rows.at[1, 8:16].set(dt_bias.reshape(-1))


def _gc_rows(bg):
    S = bg.shape[0]
    return bg[:, 8:16].reshape(S // CHUNK, CHUNK, 8).transpose(0, 2, 1)


def _layer_fwd(x0, target, W):
    S = x0.shape[0]
    R = {}
    R["h1"] = _rmsnorm_fwd("rms1", x0, W["ffn1_norm"])
    R["x1"], R["a1"], R["b1"], R["f1"] = _ffn_fwd("ffn1", x0, R["h1"], W["ffn1_w_gate"], W["ffn1_w_up"], W["ffn1_w_down"])
    R["h2"] = _rmsnorm_fwd("rms2", R["x1"], W["mix_norm"])
    tm = _tile(S, 512)
    R["p"] = _fused_mm("in_proj", S, D_IN_PAD, D_MODEL, tm, 640, D_MODEL, [(R["h2"], "mk"), (W["w_in"], "kn")],
                       [(0, 1, 0)], [], [(jax.ShapeDtypeStruct((S, D_IN_PAD), F32), (tm, 640), _mn)],
                       lambda i, accs, ex, out: out[0].__setitem__(Ellipsis, accs[0][...]))[0]
    p = R["p"]
    R["xc"] = _conv_fwd("rg_conv_fwd", p, 0, W["rg_conv_w"], W["rg_conv_b"], "bias")
    R["bd"] = _rg_bd(W["rg_gate_a_w"], W["rg_gate_x_w"])
    R["rg_prm"] = _rg_prm(W["rg_gate_a_b"], W["rg_gate_x_b"], W["rg_lambda"])
    a_f, b_f, a_b, b_b = _rg_gates_fwd(R["xc"], R["bd"], R["rg_prm"])
    R["a_f"], R["a_b"] = a_f, a_b
    R["h_f"], R["h_b"] = _rg_scan("rg_scan_fwd", a_f, b_f, a_b, b_b)
    zero_b = jnp.zeros((1, RG_W), F32)
    cw = W["gdn_conv_w"]
    R["q"] = _conv_fwd("gdn_conv_q", p, 2, cw[:, 0:512], zero_b, "q")
    R["k"] = _conv_fwd("gdn_conv_k", p, 3, cw[:, 512:1024], zero_b, "k")
    R["v"] = _conv_fwd("gdn_conv_v", p, 4, cw[:, 1024:1536], zero_b, "v")
    R["gdn_prm"] = _gdn_prm(W["gdn_a_log"], W["gdn_dt_bias"])
    R["bg"] = _gdn_prep_fwd(p, R["gdn_prm"])
    R["gcr"] = _gc_rows(R["bg"])
    R["gdn_loc"] = _gdn_local_fwd(R["q"], R["k"], R["v"], R["bg"], R["gcr"])
    R["gdn_fwd"] = _gdn_scan_fwd(R["q"], R["k"], R["bg"], R["gdn_loc"])
    R["o_f"], R["o_b"] = R["gdn_fwd"][0][0], R["gdn_fwd"][1][0]
    R["y"] = _mix_out_fwd(R["h_f"], R["h_b"], R["o_f"], R["o_b"], p, W["gdn_norm"])
    R["x2"] = _fused_mm("out_proj", S, D_MODEL, D_MODEL, tm, D_MODEL, D_MODEL, [(R["y"], "mk"), (W["w_out"], "kn")],
                        [(0, 1, 0)], [(R["x1"], (tm, D_MODEL), _mn)],
                        [(jax.ShapeDtypeStruct((S, D_MODEL), F32), (tm, D_MODEL), _mn)],
                        lambda i, accs, ex, out: out[0].__setitem__(Ellipsis, ex[0][...] + accs[0][...]))[0]
    R["h3"] = _rmsnorm_fwd("rms3", R["x2"], W["ffn2_norm"])
    R["x3"], R["a2"], R["b2"], R["f2"] = _ffn_fwd("ffn2", R["x2"], R["h3"], W["ffn2_w_gate"], W["ffn2_w_up"], W["ffn2_w_down"])
    R["dx3"], R["loss"], R["d_final_norm"] = _loss_head(R["x3"], target, W["final_norm"])
    return R


def _colsum_into(ref, i, val):
    @pl.when(i == 0)
    def _():
        ref[...] = val

    @pl.when(i > 0)
    def _():
        ref[...] += val


def _ffn_bwd(tag, dout, x, g, h, a, b, f, wg, wu, wd):
    S = x.shape[0]
    tm = _tile(S, 512)
    tk_s = _tile(S, 512)

    def epi_act(i, accs, ex, out):
        df = 0.5 * accs[0][...]
        av = ex[0][...].astype(F32)
        bv = ex[1][...].astype(F32)
        s = _sigmoid(av)
        out[0][...] = (df * bv * (s * (1.0 + av * (1.0 - s)))).astype(BF16)
        out[1][...] = (df * av * s).astype(BF16)

    sds = jax.ShapeDtypeStruct((S, D_FF), BF16)
    da, db = _fused_mm(f"{tag}_dact", S, D_FF, D_MODEL, tm, 1408, D_MODEL, [(dout, "mk"), (wd, "nk")], [(0, 1, 0)],
                       [(a, (tm, 1408), _mn), (b, (tm, 1408), _mn)], [(sds, (tm, 1408), _mn)] * 2, epi_act)

    def epi_dx(i, accs, ex, out):
        dx, dgt = _rmsnorm_bwd_tile(accs[0][...], ex[0][...], ex[1][...])
        out[0][...] = ex[2][...] + dx
        _colsum_into(out[1], i, jnp.sum(dgt, axis=0, keepdims=True))

    dx, dg = _fused_mm(f"{tag}_dx", S, D_MODEL, D_FF, tm, D_MODEL, 1408,
                       [(da, "mk"), (wg, "nk"), (db, "mk"), (wu, "nk")], [(0, 1, 0), (2, 3, 0)],
                       [(x, (tm, D_MODEL), _mn), (g, (1, D_MODEL), _row0), (dout, (tm, D_MODEL), _mn)],
                       [(jax.ShapeDtypeStruct((S, D_MODEL), F32), (tm, D_MODEL), _mn),
                        (jax.ShapeDtypeStruct((1, D_MODEL), F32), (1, D_MODEL), _row0)], epi_dx)

    def epi_w2(i, accs, ex, out):
        out[0][...] = accs[0][...].astype(BF16)
        out[1][...] = accs[1][...].astype(BF16)

    sdw = jax.ShapeDtypeStruct((D_MODEL, D_FF), BF16)
    dwg, dwu = _fused_mm(f"{tag}_dw_up", D_MODEL, D_FF, S, D_MODEL, 1408, tk_s,
                         [(h, "km"), (da, "kn"), (db, "kn")], [(0, 1, 0), (0, 2, 1)], [],
                         [(sdw, (D_MODEL, 1408), _mn)] * 2, epi_w2)
    dwd = _fused_mm(f"{tag}_dw_down", D_FF, D_MODEL, S, 1408, D_MODEL, tk_s, [(f, "km"), (dout, "kn")], [(0, 1, 0)], [],
                    [(jax.ShapeDtypeStruct((D_FF, D_MODEL), BF16), (1408, D_MODEL), _mn)],
                    lambda i, accs, ex, out: out[0].__setitem__(Ellipsis, (0.5 * accs[0][...]).astype(BF16)))[0]
    return dx, dg, dwg, dwu, dwd


def _mix_out_bwd(dy, h_f, h_b, o_f, o_b, p, gn):
    S = dy.shape[0]
    ts = _tile(S, 512)
    c0 = math.sqrt(2.0 / math.pi)

    def body(dy_ref, hf, hb, of, ob, gate, z, gn_ref, dhr_ref, dgate_ref, do_ref, dz_ref, dgn_ref):
        i = pl.program_id(0)
        gv = gate[...]
        ge, t = _gelu(gv)
        dy_rg = dy_ref[:, 0:RG_W]
        dhr_ref[...] = dy_rg * ge
        dgelu = 0.5 * (1.0 + t) + 0.5 * gv * (1.0 - t * t) * c0 * (1.0 + 3.0 * 0.044715 * gv * gv)
        dgate_ref[...] = (dy_rg * (hf[...] + hb[...]) * dgelu).astype(BF16)
        o = of[...] + ob[...]
        zv = z[...]
        sig = _sigmoid(zv)
        gnv = gn_ref[...]
        dgn = jnp.zeros((1, GDN_DK), F32)
        for h in range(GDN_H):
            cols = slice(h * GDN_DK, (h + 1) * GDN_DK)
            oh = o[:, cols]
            r = lax.rsqrt(jnp.mean(oh * oh, axis=-1, keepdims=True) + EPS)
            ohat = oh * r
            dyh = dy_ref[:, RG_W + h * GDN_DK:RG_W + (h + 1) * GDN_DK]
            zh = zv[:, cols]
            sh = sig[:, cols]
            dn = dyh * zh * sh
            dz_ref[:, cols] = (dyh * ohat * gnv * (sh * (1.0 + zh * (1.0 - sh)))).astype(BF16)
            dxn = dn * gnv
            do_ref[:, cols] = r * (dxn - ohat * jnp.mean(dxn * ohat, axis=-1, keepdims=True))
            dgn = dgn + jnp.sum(dn * ohat, axis=0, keepdims=True)
        _colsum_into(dgn_ref, i, dgn)

    blk = (ts, RG_W)
    im = lambda i: (i, 0)
    z0 = lambda i: (0, 0)
    ins = [(dy, (ts, D_MODEL), im), (h_f, blk, im), (h_b, blk, im), (o_f, blk, im), (o_b, blk, im),
           (p, blk, lambda i: (i, 1)), (p, blk, lambda i: (i, 5)), (gn, (1, GDN_DK), z0)]
    outs = [(jax.ShapeDtypeStruct((S, RG_W), F32), blk, im), (jax.ShapeDtypeStruct((S, RG_W), BF16), blk, im),
            (jax.ShapeDtypeStruct((S, RG_W), F32), blk, im), (jax.ShapeDtypeStruct((S, RG_W), BF16), blk, im),
            (jax.ShapeDtypeStruct((1, GDN_DK), F32), (1, GDN_DK), z0)]
    return _rows("mix_out_bwd", S, ts, ins, outs, body)


def _rg_scan_adj(name, a_up, b_up, a_dn, b_dn):
    S, C = a_up.shape
    ts = _tile(S, 512)
    n_tiles = S // ts

    def body(au, bu, ad, bd, mu_ref, lam_ref, carry):
        @pl.when(pl.program_id(0) == 0)
        def _():
            carry[...] = jnp.zeros_like(carry)

        def step(t, c):
            cu, cd = c
            mu = bu[pl.ds(t, 1), :] + cu
            mu_ref[pl.ds(t, 1), :] = mu
            cu = au[pl.ds(t, 1), :] * mu
            tb = ts - 1 - t
            lam = bd[pl.ds(tb, 1), :] + cd
            lam_ref[pl.ds(tb, 1), :] = lam
            cd = ad[pl.ds(tb, 1), :] * lam
            return cu, cd

        cu, cd = lax.fori_loop(0, ts, step, (carry[0:1, :], carry[1:2, :]), unroll=8)
        carry[0:1, :] = cu
        carry[1:2, :] = cd

    fw = lambda i: (i, 0)
    bw = lambda i: (n_tiles - 1 - i, 0)
    sds = jax.ShapeDtypeStruct((S, C), F32)
    return _rows(name, S, ts,
                 [(a_up, (ts, C), fw), (b_up, (ts, C), fw), (a_dn, (ts, C), bw), (b_dn, (ts, C), bw)],
                 [(sds, (ts, C), fw), (sds, (ts, C), bw)], body, scratch=[pltpu.VMEM((8, C), F32)])


def _halo_ex(arr, S, tm, width):
    per = tm // HALO
    last = S // HALO - 1
    return [
        (arr, (tm, width), lambda i, j: (i, 0)),
        (arr, (HALO, width), lambda i, j: (jnp.maximum(i * per - 1, 0), 0)),
        (arr, (HALO, width), lambda i, j: (jnp.minimum((i + 1) * per, last), 0)),
    ]


def _rg_gates_bwd(xc, bd, prm, lam_f, lam_b, h_f, h_b):
    S = xc.shape[0]
    tm = _tile(S, 256)
    n_tiles = S // tm

    def epi(i, accs, ex, out):
        pre = accs[0][...]
        xv = ex[0][...]
        prm_ref = ex[1]
        lams = (ex[2][...], ex[3][...])
        hprev = (_shift(_ext(ex[4], ex[5], ex[6], i, n_tiles), -1, tm),
                 _shift(_ext(ex[7], ex[8], ex[9], i, n_tiles), 1, tm))
        dxc = jnp.zeros_like(xv)
        rows = []
        dlam_rows = []
        for d in range(2):
            r, ig, sp, a, sq = _rg_gate_terms(pre, xv, prm_ref, d)
            lam = lams[d]
            da = lam * hprev[d]
            di = lam * sq * xv
            dxc = dxc + lam * sq * ig
            dsq = lam * ig * xv
            dlog_a = da * a - dsq * (a * a) / sq
            dpre_r = dlog_a * (-RG_C * sp) * r * (1.0 - r)
            dpre_i = di * ig * (1.0 - ig)
            out[0][:, d * 1024:d * 1024 + RG_W] = dpre_r.astype(BF16)
            out[0][:, d * 1024 + RG_W:(d + 1) * 1024] = dpre_i.astype(BF16)
            rows += [jnp.sum(dpre_r, axis=0, keepdims=True), jnp.sum(dpre_i, axis=0, keepdims=True)]
            dsp = jnp.sum(dlog_a * (-RG_C * r), axis=0, keepdims=True)
            dlam_rows.append(-dsp * _sigmoid(-prm_ref[4 + d:5 + d, :]))
        out[1][...] = dxc
        zero = jnp.zeros((2, RG_W), F32)
        _colsum_into(out[2], i, jnp.concatenate(rows + dlam_rows + [zero], axis=0))

    blk = (tm, RG_W)
    im = lambda i, j: (i, 0)
    extras = ([(xc, blk, im), (prm, (8, RG_W), _row0), (lam_f, blk, im), (lam_b, blk, im)]
              + _halo_ex(h_f, S, tm, RG_W) + _halo_ex(h_b, S, tm, RG_W))
    outs = [(jax.ShapeDtypeStruct((S, 4 * RG_W), BF16), (tm, 4 * RG_W), im),
            (jax.ShapeDtypeStruct((S, RG_W), F32), blk, im),
            (jax.ShapeDtypeStruct((8, RG_W), F32), (8, RG_W), _row0)]
    return _fused_mm("rg_gates_bwd", S, 4 * RG_W, RG_W, tm, 4 * RG_W, RG_W, [(xc, "mk"), (bd, "kn")], [(0, 1, 0)],
                     extras, outs, epi)


def _roll_rows(ext, off):
    if off == 0:
        return ext
    return pltpu.roll(ext, (-off) % ext.shape[0], 0)


def _conv_bwd(name, p, colblk, w, grads, mode):
    S = p.shape[0]
    ts = _tile(S, 512)
    n_tiles = S // ts
    C = w.shape[1]
    ng = len(grads)

    def body(*refs):
        p_refs = refs[0:3]
        g_refs = refs[3:3 + 3 * ng]
        w_ref = refs[3 + 3 * ng]
        dx_ref, dw_ref, db_ref = refs[4 + 3 * ng:]
        i = pl.program_id(0)
        ext_p = _ext(*p_refs, i, n_tiles)
        dn = _ext(*g_refs[0:3], i, n_tiles)
        for gi in range(1, ng):
            dn = dn + _ext(*g_refs[3 * gi:3 * gi + 3], i, n_tiles)
        if mode == "bias":
            dc = dn
        else:
            c = None
            for j in range(CONV_W):
                term = w_ref[j:j + 1, :] * _roll_rows(ext_p, j - 2)
                c = term if c is None else c + term
            sig = _sigmoid(c)
            s = c * sig
            if mode in ("q", "k"):
                scale = GDN_DK ** -0.5 if mode == "q" else 1.0
                parts = []
                for h in range(GDN_H):
                    cols = slice(h * GDN_DK, (h + 1) * GDN_DK)
                    sh = s[:, cols]
                    dnh = dn[:, cols]
                    rinv = lax.rsqrt(jnp.sum(sh * sh, axis=-1, keepdims=True) + EPS)
                    parts.append(scale * rinv * (dnh - sh * (rinv * rinv) * jnp.sum(dnh * sh, axis=-1, keepdims=True)))
                ds = jnp.concatenate(parts, axis=-1)
            else:
                ds = dn
            dc = ds * (sig * (1.0 + c * (1.0 - sig)))
        dx = None
        for j in range(CONV_W):
            term = w_ref[j:j + 1, :] * _shift(dc, 2 - j, ts)
            dx = term if dx is None else dx + term
        dx_ref[...] = dx.astype(BF16)
        dc_main = dc[HALO:HALO + ts]
        dw = jnp.concatenate([jnp.sum(dc_main * _shift(ext_p, j - 2, ts), axis=0, keepdims=True)
                              for j in range(CONV_W)], axis=0)
        _colsum_into(dw_ref, i, dw)
        _colsum_into(db_ref, i, jnp.sum(dc_main, axis=0, keepdims=True))

    ins = _halo_ins(p, S, ts, C, colblk)
    for garr in grads:
        ins += _halo_ins(garr, S, ts, C, 0)
    ins += [(w, (CONV_W, C), lambda i: (0, 0))]
    z0 = lambda i: (0, 0)
    outs = [(jax.ShapeDtypeStruct((S, C), BF16), (ts, C), lambda i: (i, 0)),
            (jax.ShapeDtypeStruct((CONV_W, C), F32), (CONV_W, C), z0),
            (jax.ShapeDtypeStruct((1, C), F32), (1, C), z0)]
    return _rows(name, S, ts, ins, outs, body)


def _gdn_scan_bwd(q, k, bg, loc, do):
    S = q.shape[0]
    ts = _tile(S, GDN_TS)
    n_tiles = S // ts
    ncb = ts // CHUNK
    nch = S // CHUNK

    def body(*refs):
        ins = (refs[0:6], refs[6:12])
        outs = (refs[12:14], refs[14:16])
        dstate = refs[16]

        @pl.when(pl.program_id(0) == 0)
        def _():
            dstate[...] = jnp.zeros_like(dstate)

        def chunk(cc, carry):
            chains = []
            for d in range(2):
                c = ncb - 1 - cc if d == 0 else cc
                r0 = pl.multiple_of(c * CHUNK, CHUNK)
                rows = pl.ds(r0, CHUNK)
                for h in range(GDN_H):
                    cols = slice(h * GDN_DK, (h + 1) * GDN_DK)
                    m = _gdn_decay(ins[d][2], None, c, rows, r0, d * GDN_H + h, d == 1, None, None)
                    chains.append(dict(d=d, h=h, c=c, rows=rows, cols=cols, m=m, dsn=dstate[d * GDN_H + h]))
            for ch in chains:
                q_ref, k_ref, bg_ref, w_ref, a_ref, do_ref = ins[ch["d"]]
                rows, cols = ch["rows"], ch["cols"]
                dob = do_ref[rows, cols].astype(BF16)
                ch["dvn"] = (_dot(a_ref[ch["c"], ch["h"]], dob, 0, 0)
                             + _bdot(k_ref[rows, cols] * ch["m"]["egl"], ch["dsn"], 1, 0))
                ch["qdo"] = _bdot(q_ref[rows, cols] * ch["m"]["eg"], dob, 0, 0)
            for ch in chains:
                w_ref = ins[ch["d"]][3]
                ch["wdvn"] = _dot(w_ref[ch["rows"], ch["cols"]], ch["dvn"].astype(BF16), 0, 0)
            for ch in chains:
                dvn_ref, ds_ref = outs[ch["d"]]
                dvn_ref[ch["rows"], ch["cols"]] = ch["dvn"]
                ds_ref[ch["c"], ch["h"]] = ch["dsn"]
                dstate[ch["d"] * GDN_H + ch["h"]] = ch["qdo"] + ch["m"]["cd"] * ch["dsn"] - ch["wdvn"]
            return carry

        lax.fori_loop(0, ncb, chunk, 0)

    ins, outs = [], []
    for d in range(2):
        tix = _dir_tile(d, n_tiles, True)
        im = lambda i, tix=tix: (tix(i), 0)
        im4 = lambda i, tix=tix: (tix(i), 0, 0, 0)
        _, w, a, _ = loc[d]
        ins += [(q, (ts, GDN_W), im), (k, (ts, GDN_W), im), (bg, (ts, 128), im), (w, (ts, GDN_W), im),
                (a, (ncb, GDN_H, CHUNK, CHUNK), im4), (do, (ts, GDN_W), im)]
        outs += [(jax.ShapeDtypeStruct((S, GDN_W), F32), (ts, GDN_W), im),
                 (jax.ShapeDtypeStruct((nch, GDN_H, GDN_DK, GDN_DK), F32), (ncb, GDN_H, GDN_DK, GDN_DK), im4)]
    res = _rows("gdn_scan_bwd", S, ts, ins, outs, body, scratch=[pltpu.VMEM((2 * GDN_H, GDN_DK, GDN_DK), F32)])
    return res[0:2], res[2:4]


def _gdn_local_bwd(q, k, v, bg, gcr, do, loc, fwd, adj):
    S = q.shape[0]
    ts = _tile(S, GDN_TS)
    ncb = ts // CHUNK

    def body(q_ref, k_ref, v_ref, bg_ref, gcr_ref, do_ref, *rest):
        per_dir = (rest[0:5], rest[5:10])
        dq_ref, dk_ref, dv_ref, dbg_ref = rest[10:14]
        ri, ci = _tri_masks()
        lane = lax.broadcasted_iota(jnp.int32, (CHUNK, 128), 1)
        rowi = lax.broadcasted_iota(jnp.int32, (CHUNK, 1), 0)
        ones = jnp.ones((CHUNK, 128), F32)

        def chunk(c, carry):
            r0 = pl.multiple_of(c * CHUNK, CHUNK)
            rows = pl.ds(r0, CHUNK)
            chains = []
            for h in range(GDN_H):
                cols = slice(h * GDN_DK, (h + 1) * GDN_DK)
                qh, kh, vh = q_ref[rows, cols], k_ref[rows, cols], v_ref[rows, cols]
                dob = do_ref[rows, cols].astype(BF16)
                both = _bdot(jnp.concatenate([qh, kh], axis=0), kh, 1, 1)
                for d in range(2):
                    chains.append(dict(h=h, d=d, cols=cols, qh=qh, kh=kh, vh=vh, dob=dob, qk=both[0:CHUNK],
                                       kk=both[CHUNK:2 * CHUNK], col=d * GDN_H + h))
            for ch in chains:
                m = _gdn_decay(bg_ref, gcr_ref, c, rows, r0, ch["col"], ch["d"] == 1, ri, ci)
                t_ref, s_ref, ds_ref, vn_ref, dvn_ref = per_dir[ch["d"]]
                h, cols = ch["h"], ch["cols"]
                ch["m"] = m
                ch["kb"] = ch["kh"] * m["beta"]
                ch["kbg"] = ch["kb"] * m["eg"]
                ch["t"] = t_ref[c, h]
                st = s_ref[c, h]
                stb = st.astype(BF16)
                ch["dsn"] = ds_ref[c, h]
                vnb = vn_ref[rows, cols].astype(BF16)
                dvnb = dvn_ref[rows, cols].astype(BF16)
                ch["dcd"] = jnp.sum(jnp.sum(st * ch["dsn"], axis=1, keepdims=True), axis=0, keepdims=True)
                ch["dqd"] = _dot(ch["dob"], stb, 1, 1)
                ch["d_a"] = _dot(ch["dob"], vnb, 1, 1)
                ch["dkd"] = _bdot(vnb, ch["dsn"], 1, 1)
                ch["dw"] = -_dot(dvnb, stb, 1, 1)
                ch["dvb"] = _dot(ch["t"], dvnb, 0, 0)
                ch["d_t"] = _bdot(dvnb, ch["vh"] * m["beta"], 1, 1)
            for ch in chains:
                dwb = ch["dw"].astype(BF16)
                ch["d_t"] = ch["d_t"] + _bdot(dwb, ch["kbg"], 1, 1)
                ch["dkbg"] = _dot(ch["t"], dwb, 0, 0)
                ch["nn"] = ch["d_a"] * ch["m"]["dm"]
                ch["nn_q"] = _bdot(ch["nn"], ch["qh"], 0, 0)
                ch["nn_k"] = _bdot(ch["nn"], ch["kh"], 1, 0)
            for ch in chains:
                ch["x"] = _dot(ch["d_t"].astype(BF16), ch["t"], 1, 1)
            for ch in chains:
                d_l = -_dot(ch["t"], ch["x"].astype(BF16), 0, 0)
                ch["d_l"] = jnp.where(ch["m"]["strict"], d_l, 0.0)
                ch["mm"] = ch["d_l"] * ch["m"]["dm"]
            for ch in chains:
                m = ch["m"]
                ch["mm_kh"] = _bdot(ch["mm"], ch["kh"], 1, 0)
                ch["mm_kb"] = _bdot(ch["mm"], ch["kb"], 0, 0)
                l_mat = jnp.where(m["strict"], m["beta"] * ch["kk"] * m["dm"], 0.0)
                ch["e"] = ch["d_l"] * l_mat + ch["nn"] * ch["qk"]
                ch["cs"] = _dot(ch["e"], ones, 0, 0, HI)[:, 0:1]
            acc_bg = jnp.zeros((CHUNK, 128), F32)
            acc = {}
            for ch in chains:
                m = ch["m"]
                beta, eg, egl = m["beta"], m["eg"], m["egl"]
                dkb = ch["mm_kh"] + ch["dkbg"] * eg
                dk_d = ch["mm_kb"] + ch["nn_q"] + ch["dkd"] * egl + dkb * beta
                dq_d = ch["nn_k"] + ch["dqd"] * eg
                dv_d = ch["dvb"] * beta
                rs = jnp.sum(ch["e"], axis=1, keepdims=True)
                dkd_kd = ch["dkd"] * (ch["kh"] * egl)
                dgc = (rs - ch["cs"] + jnp.sum(ch["dqd"] * (ch["qh"] * eg), axis=1, keepdims=True)
                       - jnp.sum(dkd_kd, axis=1, keepdims=True) + jnp.sum(ch["dkbg"] * ch["kbg"], axis=1, keepdims=True))
                dgl = jnp.sum(jnp.sum(dkd_kd, axis=1, keepdims=True), axis=0, keepdims=True) + ch["dcd"] * m["cd"]
                dgc = dgc + jnp.where(rowi == (0 if ch["d"] == 1 else CHUNK - 1), dgl, 0.0)
                dbeta = (jnp.sum(dkb * ch["kh"], axis=1, keepdims=True)
                         + jnp.sum(ch["dvb"] * ch["vh"], axis=1, keepdims=True))
                acc_bg = acc_bg + jnp.where(lane == ch["col"], dbeta, 0.0) + jnp.where(lane == 8 + ch["col"], dgc, 0.0)
                if ch["d"] == 0:
                    acc[ch["h"]] = (dq_d, dk_d, dv_d)
                else:
                    dq0, dk0, dv0 = acc[ch["h"]]
                    dq_ref[rows, ch["cols"]] = dq0 + dq_d
                    dk_ref[rows, ch["cols"]] = dk0 + dk_d
                    dv_ref[rows, ch["cols"]] = dv0 + dv_d
            dbg_ref[rows, :] = acc_bg
            return carry

        lax.fori_loop(0, ncb, chunk, 0)

    im = lambda i: (i, 0)
    im4 = lambda i: (i, 0, 0, 0)
    blk = (ts, GDN_W)
    ins = [(q, blk, im), (k, blk, im), (v, blk, im), (bg, (ts, 128), im), (gcr, (ncb, 8, CHUNK), lambda i: (i, 0, 0)),
           (do, blk, im)]
    for d in range(2):
        ins += [(loc[d][3], (ncb, GDN_H, CHUNK, CHUNK), im4), (fwd[d][2], (ncb, GDN_H, GDN_DK, GDN_DK), im4),
                (adj[d][1], (ncb, GDN_H, GDN_DK, GDN_DK), im4), (fwd[d][1], blk, im), (adj[d][0], blk, im)]
    sds = jax.ShapeDtypeStruct((S, GDN_W), F32)
    outs = [(sds, blk, im), (sds, blk, im), (sds, blk, im), (jax.ShapeDtypeStruct((S, 128), F32), (ts, 128), im)]
    return _rows("gdn_local_bwd", S, ts, ins, outs, body)


def _gdn_prep_bwd(dbg_all, p, prm):
    S = p.shape[0]
    ts = _tile(S, 512)

    def body(dbg_ref, p_ref, prm_ref, dba_ref, dprm_ref):
        i = pl.program_id(0)
        raw = p_ref[...]
        dbg = dbg_ref[...]
        lane = lax.broadcasted_iota(jnp.int32, (1, 128), 1)
        is_g = (lane >= 8) & (lane < 16)
        ea = jnp.exp(prm_ref[0:1, :])
        arg = raw + prm_ref[1:2, :]
        g = jnp.where(is_g, -ea * _softplus(arg), 0.0)
        beta = _sigmoid(raw)
        dgc = jnp.where(is_g, dbg, 0.0)
        ri, ci = _tri_masks()
        lower = (ri >= ci).astype(F32)
        upper = (ri <= ci).astype(F32)
        dgs = []
        for c in range(ts // CHUNK):
            ch = dgc[c * CHUNK:(c + 1) * CHUNK]
            dgs.append(jnp.where(lane < 12, _dot(upper, ch, 1, 0, HI), _dot(lower, ch, 1, 0, HI)))
        dg = jnp.concatenate(dgs, axis=0)
        dalpha = jnp.where(is_g, dg * (-ea) * _sigmoid(arg), 0.0)
        dba_ref[...] = jnp.where(lane < 8, dbg * beta * (1.0 - beta), dalpha).astype(BF16)
        rows = jnp.concatenate([jnp.sum(dg * g, axis=0, keepdims=True), jnp.sum(dalpha, axis=0, keepdims=True),
                                jnp.zeros((6, 128), F32)], axis=0)
        _colsum_into(dprm_ref, i, rows)

    im = lambda i: (i, 0)
    z0 = lambda i: (0, 0)
    return _rows("gdn_prep_bwd", S, ts,
                 [(dbg_all, (ts, 128), im), (p, (ts, 128), lambda i: (i, COL_BA // 128)), (prm, (8, 128), z0)],
                 [(jax.ShapeDtypeStruct((S, 128), BF16), (ts, 128), im), (jax.ShapeDtypeStruct((8, 128), F32), (8, 128), z0)],
                 body)


def _mm_plain(name, M, N, K, tm, tn, tk, a, am, b, bm, dtype):
    return _fused_mm(name, M, N, K, tm, tn, tk, [(a, am), (b, bm)], [(0, 1, 0)], [],
                     [(jax.ShapeDtypeStruct((M, N), dtype), (tm, tn), _mn)],
                     lambda i, accs, ex, out: out[0].__setitem__(Ellipsis, accs[0][...].astype(dtype)))[0]


def _layer_bwd(x0, W, R):
    S = x0.shape[0]
    tm = _tile(S, 512)
    tk_s = _tile(S, 512)
    G = {}
    dx2, G["ffn2_norm"], G["ffn2_w_gate"], G["ffn2_w_up"], G["ffn2_w_down"] = _ffn_bwd(
        "ffn2b", R["dx3"], R["x2"], W["ffn2_norm"], R["h3"], R["a2"], R["b2"], R["f2"],
        W["ffn2_w_gate"], W["ffn2_w_up"], W["ffn2_w_down"])
    G["w_out"] = _mm_plain("dw_out", D_MODEL, D_MODEL, S, D_MODEL, D_MODEL, tk_s, R["y"], "km", dx2, "kn", BF16)
    dy = _mm_plain("dy_mix", S, D_MODEL, D_MODEL, tm, D_MODEL, D_MODEL, dx2, "mk", W["w_out"], "nk", F32)
    p = R["p"]
    dhr, dgate, do, dz, G["gdn_norm"] = _mix_out_bwd(dy, R["h_f"], R["h_b"], R["o_f"], R["o_b"], p, W["gdn_norm"])
    lam_b, lam_f = _rg_scan_adj("rg_scan_bwd", R["a_b"], dhr, R["a_f"], dhr)
    dpre, dxc_direct, d_rgprm = _rg_gates_bwd(R["xc"], R["bd"], R["rg_prm"], lam_f, lam_b, R["h_f"], R["h_b"])
    tmg = _tile(S, 512)
    dxc = _fused_mm("rg_dxc", S, RG_W, 4 * RG_W, tmg, RG_W, 4 * RG_W, [(dpre, "mk"), (R["bd"], "nk")], [(0, 1, 0)],
                    [(dxc_direct, (tmg, RG_W), _mn)], [(jax.ShapeDtypeStruct((S, RG_W), F32), (tmg, RG_W), _mn)],
                    lambda i, accs, ex, out: out[0].__setitem__(Ellipsis, ex[0][...] + accs[0][...]))[0]
    d_bd = _mm_plain("rg_dbd", RG_W, 4 * RG_W, S, RG_W, 4 * RG_W, tk_s, R["xc"], "km", dpre, "kn", F32)
    dx_rg, G["rg_conv_w"], G["rg_conv_b"] = _conv_bwd("rg_conv_bwd", p, 0, W["rg_conv_w"], [dxc], "bias")
    blocks = jnp.einsum("nigmj,nm->gnij", d_bd.reshape(RG_BLOCKS, RG_BLOCK, 4, RG_BLOCKS, RG_BLOCK),
                        jnp.eye(RG_BLOCKS, dtype=F32))
    G["rg_gate_a_w"] = jnp.stack([blocks[0], blocks[2]])
    G["rg_gate_x_w"] = jnp.stack([blocks[1], blocks[3]])
    G["rg_gate_a_b"] = jnp.stack([d_rgprm[0], d_rgprm[2]])
    G["rg_gate_x_b"] = jnp.stack([d_rgprm[1], d_rgprm[3]])
    G["rg_lambda"] = d_rgprm[4:6]
    adj = _gdn_scan_bwd(R["q"], R["k"], R["bg"], R["gdn_loc"], do)
    dq, dk, dv, dbg = _gdn_local_bwd(R["q"], R["k"], R["v"], R["bg"], R["gcr"], do, R["gdn_loc"], R["gdn_fwd"], adj)
    cw = W["gdn_conv_w"]
    dpq, dwq, _ = _conv_bwd("gdn_conv_q_bwd", p, 2, cw[:, 0:512], [dq], "q")
    dpk, dwk, _ = _conv_bwd("gdn_conv_k_bwd", p, 3, cw[:, 512:1024], [dk], "k")
    dpv, dwv, _ = _conv_bwd("gdn_conv_v_bwd", p, 4, cw[:, 1024:1536], [dv], "v")
    G["gdn_conv_w"] = jnp.concatenate([dwq, dwk, dwv], axis=1)
    dba, d_gprm = _gdn_prep_bwd(dbg, p, R["gdn_prm"])
    G["gdn_a_log"] = d_gprm[0, 8:16].reshape(2, GDN_H)
    G["gdn_dt_bias"] = d_gprm[1, 8:16].reshape(2, GDN_H)
    dp = jnp.concatenate([dx_rg, dgate, dpq, dpk, dpv, dz, dba], axis=1)
    G["w_in"] = _mm_plain("dw_in", D_MODEL, D_IN_PAD, S, D_MODEL, 640, tk_s, R["h2"], "km", dp, "kn", BF16)

    def epi_dx1(i, accs, ex, out):
        dx, dgt = _rmsnorm_bwd_tile(accs[0][...], ex[0][...], ex[1][...])
        out[0][...] = ex[2][...] + dx
        _colsum_into(out[1], i, jnp.sum(dgt, axis=0, keepdims=True))

    dx1, G["mix_norm"] = _fused_mm(
        "mix_dx", S, D_MODEL, D_IN_PAD, tm, D_MODEL, 640, [(dp, "mk"), (W["w_in"], "nk")], [(0, 1, 0)],
        [(R["x1"], (tm, D_MODEL), _mn), (W["mix_norm"], (1, D_MODEL), _row0), (dx2, (tm, D_MODEL), _mn)],
        [(jax.ShapeDtypeStruct((S, D_MODEL), F32), (tm, D_MODEL), _mn),
         (jax.ShapeDtypeStruct((1, D_MODEL), F32), (1, D_MODEL), _row0)], epi_dx1)
    dx0, G["ffn1_norm"], G["ffn1_w_gate"], G["ffn1_w_up"], G["ffn1_w_down"] = _ffn_bwd(
        "ffn1b", dx1, x0, W["ffn1_norm"], R["h1"], R["a1"], R["b1"], R["f1"],
        W["ffn1_w_gate"], W["ffn1_w_up"], W["ffn1_w_down"])
    G["final_norm"] = R["d_final_norm"]
    return dx0, G


def _mesh_pos():
    x, y, c = lax.axis_index("x"), lax.axis_index("y"), lax.axis_index("c")
    return x, y, c, 4 * x + 2 * y + c


def _peer(x, y, c, r):
    px = 1 - x if r & 4 else x
    py = 1 - y if r & 2 else y
    pc = 1 - c if r & 1 else c
    return (px, py, pc), 4 * px + 2 * py + pc


def _exchange(name, arrays, all_to_all):
    n = len(arrays)

    def body(*refs):
        ins = refs[:n]
        outs = refs[n:2 * n]
        send_sems, recv_sems, local_sems = refs[2 * n:]
        x, y, c, me = _mesh_pos()
        started = []
        for a in range(n):
            src_local = ins[a].at[me] if all_to_all else ins[a]
            loc = pltpu.make_async_copy(src_local, outs[a].at[me], local_sems.at[a])
            loc.start()
            started.append(loc)
        sends = []
        for a in range(n):
            for r in range(1, N_DEV):
                peer, peer_idx = _peer(x, y, c, r)
                src = ins[a].at[peer_idx] if all_to_all else ins[a]
                cp = pltpu.make_async_remote_copy(
                    src_ref=src, dst_ref=outs[a].at[me], send_sem=send_sems.at[a * 7 + r - 1],
                    recv_sem=recv_sems.at[a * 7 + r - 1], device_id=peer, device_id_type=pl.DeviceIdType.MESH)
                cp.start()
                sends.append(cp)
        for a in range(n):
            for r in range(1, N_DEV):
                peer, peer_idx = _peer(x, y, c, r)
                src = ins[a].at[peer_idx] if all_to_all else ins[a]
                pltpu.make_async_remote_copy(
                    src_ref=src, dst_ref=outs[a].at[peer_idx], send_sem=send_sems.at[a * 7 + r - 1],
                    recv_sem=recv_sems.at[a * 7 + r - 1], device_id=peer, device_id_type=pl.DeviceIdType.MESH).wait_recv()
        for cp in sends:
            cp.wait_send()
        for loc in started:
            loc.wait()

    any_spec = pl.BlockSpec(memory_space=pl.ANY)
    out_shape = [jax.ShapeDtypeStruct(a.shape if all_to_all else (N_DEV,) + a.shape, a.dtype) for a in arrays]
    return pl.pallas_call(
        body, name=name, in_specs=[any_spec] * n, out_specs=[any_spec] * n, out_shape=out_shape,
        scratch_shapes=[pltpu.SemaphoreType.DMA((7 * n,)), pltpu.SemaphoreType.DMA((7 * n,)),
                        pltpu.SemaphoreType.DMA((n,))],
        compiler_params=pltpu.CompilerParams(has_side_effects=True),
    )(*arrays)


def _all_gather(name, arrays):
    n = len(arrays)

    def body(*refs):
        ins = refs[:n]
        outs = refs[n:2 * n]
        send_sems, recv_sems, local_sems = refs[2 * n:]
        x, y, c, me = _mesh_pos()
        sibling = (x, y, 1 - c)
        chips = [(1 - x, y), (x, 1 - y), (1 - x, 1 - y)]

        def idx(px, py, pc):
            return 4 * px + 2 * py + pc

        def copy(a, k, block, to, src=None):
            slot = outs[a].at[idx(*block)]
            return pltpu.make_async_remote_copy(
                src_ref=slot if src is None else src, dst_ref=slot, send_sem=send_sems.at[a * 7 + k],
                recv_sem=recv_sems.at[a * 7 + k], device_id=to, device_id_type=pl.DeviceIdType.MESH)

        locals_, sends = [], []
        for a in range(n):
            loc = pltpu.make_async_copy(ins[a], outs[a].at[me], local_sems.at[a])
            loc.start()
            locals_.append(loc)
            sends.append(copy(a, 0, (x, y, c), sibling, src=ins[a]))
            sends += [copy(a, 1 + j, (x, y, c), (*chip, c), src=ins[a]) for j, chip in enumerate(chips)]
        for cp in sends:
            cp.start()
        passed = []
        for a in range(n):
            for j, chip in enumerate(chips):
                copy(a, 1 + j, (*chip, c), (x, y, c)).wait_recv()
                fwd = copy(a, 4 + j, (*chip, c), sibling)
                fwd.start()
                passed.append(fwd)
        for a in range(n):
            copy(a, 0, sibling, (x, y, c)).wait_recv()
            for j, chip in enumerate(chips):
                copy(a, 4 + j, (*chip, 1 - c), (x, y, c)).wait_recv()
        for cp in sends + passed:
            cp.wait_send()
        for loc in locals_:
            loc.wait()

    any_spec = pl.BlockSpec(memory_space=pl.ANY)
    return pl.pallas_call(
        body, name=name, in_specs=[any_spec] * n, out_specs=[any_spec] * n,
        out_shape=[jax.ShapeDtypeStruct((N_DEV,) + a.shape, a.dtype) for a in arrays],
        scratch_shapes=[pltpu.SemaphoreType.DMA((7 * n,)), pltpu.SemaphoreType.DMA((7 * n,)),
                        pltpu.SemaphoreType.DMA((n,))],
        compiler_params=pltpu.CompilerParams(has_side_effects=True),
    )(*arrays)


def _adamw_math(w, g, m, v):
    m2 = ADAM_B1 * m + (1.0 - ADAM_B1) * g
    v2 = ADAM_B2 * v + (1.0 - ADAM_B2) * (g * g)
    m_hat = m2 / (1.0 - ADAM_B1 ** ADAM_STEP)
    v_hat = v2 / (1.0 - ADAM_B2 ** ADAM_STEP)
    delta = -ADAM_LR * (m_hat / (jnp.sqrt(v_hat) + ADAM_EPS) + ADAM_WD * w)
    return delta, m2, v2


def _adamw_slabs(name, slabs, w, m, v, tr):
    R, C = w.shape

    def body(s_ref, w_ref, m_ref, v_ref, g_ref, d_ref, m2_ref, v2_ref):
        g = s_ref[0].astype(F32)
        for s in range(1, N_DEV):
            g = g + s_ref[s].astype(F32)
        delta, m2, v2 = _adamw_math(w_ref[...], g, m_ref[...], v_ref[...])
        g_ref[...] = g
        d_ref[...] = delta
        m2_ref[...] = m2
        v2_ref[...] = v2

    im = lambda i: (i, 0)
    sds = jax.ShapeDtypeStruct((R, C), F32)
    ins = [(slabs, (N_DEV, tr, C), lambda i: (0, i, 0)), (w, (tr, C), im), (m, (tr, C), im), (v, (tr, C), im)]
    return _rows(name, R, tr, ins, [(sds, (tr, C), im)] * 4, body)


def _sum_slots(name, slots):
    _, R, C = slots.shape

    def body(s_ref, o_ref):
        g = s_ref[0]
        for s in range(1, N_DEV):
            g = g + s_ref[s]
        o_ref[...] = g

    return _rows(name, R, R, [(slots, (N_DEV, R, C), lambda i: (0, 0, 0))],
                 [(jax.ShapeDtypeStruct((R, C), F32), (R, C), lambda i: (0, 0))], body)[0]


def _adamw_packed(name, g, w, m, v):
    R, C = g.shape

    def body(g_ref, w_ref, m_ref, v_ref, d_ref, m2_ref, v2_ref):
        delta, m2, v2 = _adamw_math(w_ref[...], g_ref[...], m_ref[...], v_ref[...])
        d_ref[...] = delta
        m2_ref[...] = m2
        v2_ref[...] = v2

    im = lambda i: (0, 0)
    sds = jax.ShapeDtypeStruct((R, C), F32)
    return _rows(name, R, R, [(a, (R, C), im) for a in (g, w, m, v)], [(sds, (R, C), im)] * 3, body)


def _pack(arrays):
    rows = []
    for a in arrays:
        flat = a.reshape(-1).astype(F32)
        pad = (-flat.shape[0]) % 128
        rows.append(jnp.pad(flat, (0, pad)).reshape(-1, 128))
    out = jnp.concatenate(rows, axis=0)
    return jnp.pad(out, ((0, (-out.shape[0]) % 8), (0, 0)))


def _unpack(packed, shapes):
    lead = packed.shape[:-2]
    outs = []
    r = 0
    for shp in shapes:
        n = math.prod(shp)
        nr = -(-n // 128)
        flat = packed[..., r:r + nr, :].reshape(lead + (nr * 128,))[..., :n]
        outs.append(flat.reshape(lead + tuple(shp)))
        r += nr
    return outs


BIG = ["ffn1_w_gate", "ffn1_w_up", "ffn1_w_down", "w_in", "w_out", "ffn2_w_gate", "ffn2_w_up", "ffn2_w_down"]
COL_SHARDED = {"ffn1_w_gate", "ffn1_w_up", "w_in", "ffn2_w_gate", "ffn2_w_up"}
SMALL_SHARDED = ["rg_conv_w", "rg_gate_a_b", "rg_gate_x_b", "rg_lambda", "gdn_conv_w"]
WEIGHTS = ["ffn1_norm", "ffn1_w_gate", "ffn1_w_up", "ffn1_w_down", "mix_norm", "w_in", "w_out", "rg_conv_w", "rg_conv_b",
           "rg_gate_a_w", "rg_gate_a_b", "rg_gate_x_w", "rg_gate_x_b", "rg_lambda", "gdn_conv_w", "gdn_a_log",
           "gdn_dt_bias", "gdn_norm", "ffn2_norm", "ffn2_w_gate", "ffn2_w_up", "ffn2_w_down", "final_norm"]
SMALL = [n for n in WEIGHTS if n not in BIG]
ROW_VECTORS = {"ffn1_norm", "mix_norm", "ffn2_norm", "gdn_norm", "rg_conv_b", "final_norm"}
ROW_TILE = {"ffn1_w_gate": 256, "ffn1_w_up": 256, "ffn1_w_down": 176, "w_in": 256, "w_out": 64,
            "ffn2_w_gate": 256, "ffn2_w_up": 256, "ffn2_w_down": 176}


def _unshard_cols(g):
    return g.transpose(1, 0, 2).reshape(g.shape[1], N_DEV * g.shape[2])


def _to_slabs(name, g):
    if name in COL_SHARDED:
        r, ctot = g.shape
        return g.reshape(r, N_DEV, ctot // N_DEV).transpose(1, 0, 2)
    return g.reshape(N_DEV, g.shape[0] // N_DEV, g.shape[1])


def _step(x, target, w, m, v):
    _, _, _, me = _mesh_pos()
    small_shards = [w[n] for n in SMALL_SHARDED]
    gathered = _all_gather("gather_weights", [w[n].astype(BF16) for n in BIG] + [_pack(small_shards)])
    W = {}
    for n, gth in zip(BIG, gathered[:len(BIG)]):
        W[n] = _unshard_cols(gth) if n in COL_SHARDED else gth.reshape(-1, gth.shape[-1])
    W["w_in"] = jnp.pad(W["w_in"], ((0, 0), (0, D_IN_PAD - D_IN)))
    for n, gth in zip(SMALL_SHARDED, _unpack(gathered[-1], [s.shape for s in small_shards])):
        W[n] = jnp.moveaxis(gth, 0, -2).reshape(gth.shape[1:-1] + (N_DEV * gth.shape[-1],))
    for n in SMALL:
        if n not in SMALL_SHARDED:
            W[n] = w[n]
    R = _layer_fwd(x, target, W)
    grad_x, G = _layer_bwd(x, W, R)
    loss = lax.psum(R["loss"][0, 0], ("x", "y", "c"))
    G["w_in"] = G["w_in"][:, :D_IN]
    received = _exchange("scatter_grads", [_to_slabs(n, G[n]) for n in BIG], True)
    out = {}
    for n, slabs in zip(BIG, received):
        out[n] = _adamw_slabs(f"adamw_{n}", slabs, w[n], m[n], v[n], ROW_TILE[n])
    full_shapes = [G[n].shape for n in SMALL]
    slots = _all_gather("gather_small_grads", [_pack([G[n] for n in SMALL])])[0]
    reduced = dict(zip(SMALL, _unpack(_sum_slots("sum_small_grads", slots), full_shapes)))
    g_small = []
    for n in SMALL:
        g = reduced[n]
        if n in SMALL_SHARDED:
            per = g.shape[-1] // N_DEV
            g = lax.dynamic_slice_in_dim(g, me * per, per, axis=g.ndim - 1)
        g_small.append(g.reshape(w[n].shape))
    shapes = [w[n].shape for n in SMALL]
    d_p, m_p, v_p = _adamw_packed("adamw_small", _pack(g_small), _pack([w[n] for n in SMALL]),
                                  _pack([m[n] for n in SMALL]), _pack([v[n] for n in SMALL]))
    for n, g, d_, m_, v_ in zip(SMALL, g_small, _unpack(d_p, shapes), _unpack(m_p, shapes), _unpack(v_p, shapes)):
        out[n] = (g, d_, m_, v_)
    return loss, grad_x, out


def kernel(x, ffn1_norm, ffn1_w_gate, ffn1_w_up, ffn1_w_down, mix_norm, w_in, w_out, rg_conv_w, rg_conv_b, rg_gate_a_w, rg_gate_a_b, rg_gate_x_w, rg_gate_x_b, rg_lambda, gdn_conv_w, gdn_a_log, gdn_dt_bias, gdn_norm, ffn2_norm, ffn2_w_gate, ffn2_w_up, ffn2_w_down, final_norm, loss_target, m_ffn1_norm, m_ffn1_w_gate, m_ffn1_w_up, m_ffn1_w_down, m_mix_norm, m_w_in, m_w_out, m_rg_conv_w, m_rg_conv_b, m_rg_gate_a_w, m_rg_gate_a_b, m_rg_gate_x_w, m_rg_gate_x_b, m_rg_lambda, m_gdn_conv_w, m_gdn_a_log, m_gdn_dt_bias, m_gdn_norm, m_ffn2_norm, m_ffn2_w_gate, m_ffn2_w_up, m_ffn2_w_down, m_final_norm, v_ffn1_norm, v_ffn1_w_gate, v_ffn1_w_up, v_ffn1_w_down, v_mix_norm, v_w_in, v_w_out, v_rg_conv_w, v_rg_conv_b, v_rg_gate_a_w, v_rg_gate_a_b, v_rg_gate_x_w, v_rg_gate_x_b, v_rg_lambda, v_gdn_conv_w, v_gdn_a_log, v_gdn_dt_bias, v_gdn_norm, v_ffn2_norm, v_ffn2_w_gate, v_ffn2_w_up, v_ffn2_w_down, v_final_norm):
    args = dict(locals())
    orig_shapes = {n: args[n].shape for n in WEIGHTS}

    def local(prefix):
        d = {}
        for n in WEIGHTS:
            a = args[prefix + n]
            d[n] = a.reshape(1, -1) if n in ROW_VECTORS else a[0]
        return d

    loss, grad_x, out = _step(x[0], loss_target[0], local(""), local("m_"), local("v_"))
    res = [loss, grad_x[None]]
    for k in range(4):
        res += [out[n][k].reshape(orig_shapes[n]) for n in WEIGHTS]
    return tuple(res)
```

```python
import functools
import math

import jax
import jax.numpy as jnp
from jax import lax
from jax.experimental import pallas as pl
from jax.experimental.pallas import tpu as pltpu

F32, BF16 = jnp.float32, jnp.bfloat16

D_MODEL = 1024
D_FF = 2816
RG_W = 512
RG_BLOCKS = 8
RG_BLOCK = 64
RG_C = 8.0
CONV_W = 4
GDN_H = 4
GDN_DK = 128
CHUNK = 64
EPS = 1e-6
D_IN = 3088
D_IN_PAD = 3200
COL_BA = 3072
N_DEV = 8
HALO = 8
VMEM_LIMIT = 48 * 1024 * 1024

ADAM_LR = 0.001
ADAM_B1 = 0.9
ADAM_B2 = 0.999
ADAM_EPS = 1e-08
ADAM_WD = 0.01
ADAM_STEP = 10

HI = lax.Precision.HIGHEST


def _cp(n):
    return pltpu.CompilerParams(dimension_semantics=("arbitrary",) * n, vmem_limit_bytes=VMEM_LIMIT)


def _tile(n, pref):
    return min(n, pref)


def _sigmoid(x):
    return jax.nn.sigmoid(x)


def _softplus(x):
    return jnp.maximum(x, 0.0) + jnp.log(1.0 + jnp.exp(-jnp.abs(x)))


def _dot(a, b, ca, cb, prec=None):
    return lax.dot_general(a, b, (((ca,), (cb,)), ((), ())), preferred_element_type=F32, precision=prec)


def _fused_mm(name, M, N, K, tm, tn, tk, ops, pairs, extras, outs, epilogue):
    nm, nn, nk = M // tm, N // tn, K // tk
    assert nm * tm == M and nn * tn == N and nk * tk == K, (name, M, N, K, tm, tn, tk)
    spec_of = {
        "mk": pl.BlockSpec((tm, tk), lambda i, j, k: (i, k)),
        "km": pl.BlockSpec((tk, tm), lambda i, j, k: (k, i)),
        "kn": pl.BlockSpec((tk, tn), lambda i, j, k: (k, j)),
        "nk": pl.BlockSpec((tn, tk), lambda i, j, k: (j, k)),
    }
    in_specs = [spec_of[m] for _, m in ops]
    in_specs += [pl.BlockSpec(bs, lambda i, j, k, im=im: im(i, j)) for _, bs, im in extras]
    out_specs = [pl.BlockSpec(bs, lambda i, j, k, im=im: im(i, j)) for _, bs, im in outs]
    n_ops, n_ex, n_out = len(ops), len(extras), len(outs)
    n_acc = 1 + max(g for _, _, g in pairs)
    modes = [m for _, m in ops]

    def body(*refs):
        op_refs = refs[:n_ops]
        ex_refs = refs[n_ops:n_ops + n_ex]
        out_refs = refs[n_ops + n_ex:n_ops + n_ex + n_out]
        accs = refs[n_ops + n_ex + n_out:]
        i = pl.program_id(0)
        k = pl.program_id(2)

        @pl.when(k == 0)
        def _():
            for a in accs:
                a[...] = jnp.zeros_like(a)

        vals = [r[...].astype(BF16) for r in op_refs]
        for ia, ib, g in pairs:
            ca = 1 if modes[ia] == "mk" else 0
            cb = 0 if modes[ib] == "kn" else 1
            accs[g][...] += _dot(vals[ia], vals[ib], ca, cb)

        @pl.when(k == nk - 1)
        def _():
            epilogue(i, accs, ex_refs, out_refs)

    res = pl.pallas_call(
        body, name=name, grid=(nm, nn, nk), in_specs=in_specs, out_specs=out_specs,
        out_shape=[o for o, _, _ in outs],
        scratch_shapes=[pltpu.VMEM((tm, tn), F32)] * n_acc,
        compiler_params=_cp(3),
    )(*[a for a, _ in ops], *[a for a, _, _ in extras])
    return res


def _mn(i, j):
    return (i, j)


def _row0(i, j):
    return (0, 0)


def _rows(name, S, ts, ins, outs, body, scratch=()):
    return pl.pallas_call(
        body, name=name, grid=(S // ts,),
        in_specs=[pl.BlockSpec(bs, im) for _, bs, im in ins],
        out_specs=[pl.BlockSpec(bs, im) for _, bs, im in outs],
        out_shape=[o for o, _, _ in outs],
        scratch_shapes=list(scratch),
        compiler_params=_cp(1),
    )(*[a for a, _, _ in ins])


def _halo_ins(arr, S, ts, width, colblk):
    per = ts // HALO
    last = S // HALO - 1
    return [
        (arr, (ts, width), lambda i: (i, colblk)),
        (arr, (HALO, width), lambda i: (jnp.maximum(i * per - 1, 0), colblk)),
        (arr, (HALO, width), lambda i: (jnp.minimum((i + 1) * per, last), colblk)),
    ]


def _ext(main_ref, prev_ref, next_ref, i, n_tiles):
    prev = jnp.where(i > 0, prev_ref[...].astype(F32), 0.0)
    nxt = jnp.where(i < n_tiles - 1, next_ref[...].astype(F32), 0.0)
    return jnp.concatenate([prev, main_ref[...].astype(F32), nxt], axis=0)


def _shift(ext, off, ts):
    n = ext.shape[0]
    if off == 0:
        return ext[HALO:HALO + ts]
    return pltpu.roll(ext, (-off) % n, 0)[HALO:HALO + ts]


def _rmsnorm_fwd(name, x, g):
    S, D = x.shape
    ts = _tile(S, 512)

    def body(x_ref, g_ref, o_ref):
        xv = x_ref[...]
        r = lax.rsqrt(jnp.mean(xv * xv, axis=-1, keepdims=True) + EPS)
        o_ref[...] = (xv * r * g_ref[...]).astype(BF16)

    return _rows(name, S, ts,
                 [(x, (ts, D), lambda i: (i, 0)), (g, (1, D), lambda i: (0, 0))],
                 [(jax.ShapeDtypeStruct((S, D), BF16), (ts, D), lambda i: (i, 0))], body)[0]


def _rmsnorm_bwd_tile(dh, x, g):
    r = lax.rsqrt(jnp.mean(x * x, axis=-1, keepdims=True) + EPS)
    xhat = x * r
    dxn = dh * g
    dx = r * (dxn - xhat * jnp.mean(dxn * xhat, axis=-1, keepdims=True))
    return dx, dh * xhat


def _ffn_fwd(tag, x, h, wg, wu, wd):
    S = x.shape[0]
    tm = _tile(S, 512)
    tn = 1408

    def epi_up(i, accs, ex, out):
        a = accs[0][...]
        b = accs[1][...]
        out[0][...] = a.astype(BF16)
        out[1][...] = b.astype(BF16)
        out[2][...] = (a * _sigmoid(a) * b).astype(BF16)

    sds = jax.ShapeDtypeStruct((S, D_FF), BF16)
    a, b, f = _fused_mm(f"{tag}_up", S, D_FF, D_MODEL, tm, tn, D_MODEL,
                        [(h, "mk"), (wg, "kn"), (wu, "kn")], [(0, 1, 0), (0, 2, 1)], [],
                        [(sds, (tm, tn), _mn)] * 3, epi_up)

    def epi_down(i, accs, ex, out):
        out[0][...] = ex[0][...] + 0.5 * accs[0][...]

    xo = _fused_mm(f"{tag}_down", S, D_MODEL, D_FF, tm, D_MODEL, 1408,
                   [(f, "mk"), (wd, "kn")], [(0, 1, 0)], [(x, (tm, D_MODEL), _mn)],
                   [(jax.ShapeDtypeStruct((S, D_MODEL), F32), (tm, D_MODEL), _mn)], epi_down)[0]
    return xo, a, b, f


def _conv_taps(ext, w_ref, ts):
    acc = None
    for j in range(CONV_W):
        term = w_ref[j:j + 1, :] * _shift(ext, j - 2, ts)
        acc = term if acc is None else acc + term
    return acc


def _l2norm_heads(s, scale):
    outs = []
    for h in range(GDN_H):
        sh = s[:, h * GDN_DK:(h + 1) * GDN_DK]
        outs.append(sh * (lax.rsqrt(jnp.sum(sh * sh, axis=-1, keepdims=True) + EPS) * scale))
    return jnp.concatenate(outs, axis=-1)


def _conv_fwd(name, p, colblk, w, bias, mode):
    S = p.shape[0]
    ts = _tile(S, 512)
    n_tiles = S // ts
    C = w.shape[1]

    def body(main, prev, nxt, w_ref, b_ref, o_ref):
        i = pl.program_id(0)
        c = _conv_taps(_ext(main, prev, nxt, i, n_tiles), w_ref, ts)
        if mode == "bias":
            o_ref[...] = c + b_ref[...]
        else:
            s = c * _sigmoid(c)
            if mode == "q":
                s = _l2norm_heads(s, GDN_DK ** -0.5)
            elif mode == "k":
                s = _l2norm_heads(s, 1.0)
            o_ref[...] = s

    ins = _halo_ins(p, S, ts, C, colblk) + [(w, (CONV_W, C), lambda i: (0, 0)), (bias, (1, C), lambda i: (0, 0))]
    return _rows(name, S, ts, ins, [(jax.ShapeDtypeStruct((S, C), F32), (ts, C), lambda i: (i, 0))], body)[0]


def _rg_gate_terms(pre, xc, prm_ref, d):
    r = _sigmoid(pre[:, d * 1024:d * 1024 + RG_W] + prm_ref[2 * d:2 * d + 1, :])
    ig = _sigmoid(pre[:, d * 1024 + RG_W:(d + 1) * 1024] + prm_ref[2 * d + 1:2 * d + 2, :])
    sp = _softplus(-prm_ref[4 + d:5 + d, :])
    log_a = -RG_C * r * sp
    a = jnp.exp(log_a)
    t = jnp.tanh(log_a)
    sq = jnp.sqrt(-2.0 * t / (1.0 - t))
    return r, ig, sp, a, sq


def _rg_gates_fwd(xc, bd, prm):
    S = xc.shape[0]
    tm = _tile(S, 256)

    def epi(i, accs, ex, out):
        pre = accs[0][...]
        xv = ex[0][...]
        for d in range(2):
            r, ig, sp, a, sq = _rg_gate_terms(pre, xv, ex[1], d)
            out[2 * d][...] = a
            out[2 * d + 1][...] = sq * ig * xv

    sds = jax.ShapeDtypeStruct((S, RG_W), F32)
    blk = (tm, RG_W)
    im = lambda i, j: (i, 0)
    return _fused_mm("rg_gates_fwd", S, 4 * RG_W, RG_W, tm, 4 * RG_W, RG_W,
                     [(xc, "mk"), (bd, "kn")], [(0, 1, 0)],
                     [(xc, blk, im), (prm, (8, RG_W), _row0)], [(sds, blk, im)] * 4, epi)


def _rg_scan(name, a_f, b_f, a_b, b_b):
    S, C = a_f.shape
    ts = _tile(S, 512)
    n_tiles = S // ts

    def body(af, bf, ab, bb, hf, hb, carry):
        @pl.when(pl.program_id(0) == 0)
        def _():
            carry[...] = jnp.zeros_like(carry)

        def step(t, c):
            cf, cb = c
            cf = af[pl.ds(t, 1), :] * cf + bf[pl.ds(t, 1), :]
            hf[pl.ds(t, 1), :] = cf
            tb = ts - 1 - t
            cb = ab[pl.ds(tb, 1), :] * cb + bb[pl.ds(tb, 1), :]
            hb[pl.ds(tb, 1), :] = cb
            return cf, cb

        cf, cb = lax.fori_loop(0, ts, step, (carry[0:1, :], carry[1:2, :]), unroll=8)
        carry[0:1, :] = cf
        carry[1:2, :] = cb

    fw = lambda i: (i, 0)
    bw = lambda i: (n_tiles - 1 - i, 0)
    sds = jax.ShapeDtypeStruct((S, C), F32)
    return _rows(name, S, ts,
                 [(a_f, (ts, C), fw), (b_f, (ts, C), fw), (a_b, (ts, C), bw), (b_b, (ts, C), bw)],
                 [(sds, (ts, C), fw), (sds, (ts, C), bw)], body, scratch=[pltpu.VMEM((8, C), F32)])


def _tri_masks():
    ri = lax.broadcasted_iota(jnp.int32, (CHUNK, CHUNK), 0)
    ci = lax.broadcasted_iota(jnp.int32, (CHUNK, CHUNK), 1)
    return ri, ci


def _gdn_prep_fwd(p, prm):
    S = p.shape[0]
    ts = _tile(S, 512)

    def body(p_ref, prm_ref, o_ref):
        raw = p_ref[...]
        lane = lax.broadcasted_iota(jnp.int32, (1, 128), 1)
        g = -jnp.exp(prm_ref[0:1, :]) * _softplus(raw + prm_ref[1:2, :])
        g = jnp.where((lane >= 8) & (lane < 16), g, 0.0)
        beta = _sigmoid(raw)
        ri, ci = _tri_masks()
        lower = (ri >= ci).astype(F32)
        upper = (ri <= ci).astype(F32)
        for c in range(ts // CHUNK):
            rows = slice(c * CHUNK, (c + 1) * CHUNK)
            gch = g[rows]
            gc = jnp.where(lane < 12, _dot(lower, gch, 1, 0, HI), _dot(upper, gch, 1, 0, HI))
            o_ref[rows, :] = jnp.where(lane < 8, beta[rows], gc)

    return _rows("gdn_prep_fwd", S, ts,
                 [(p, (ts, 128), lambda i: (i, COL_BA // 128)), (prm, (8, 128), lambda i: (0, 0))],
                 [(jax.ShapeDtypeStruct((S, 128), F32), (ts, 128), lambda i: (i, 0))], body)[0]


def _bdot(a, b, ca, cb):
    return _dot(a.astype(BF16), b.astype(BF16), ca, cb)


GDN_W = GDN_H * GDN_DK
GDN_TS = 256


def _gdn_decay(bg_ref, gcr_ref, c, rows, r0, col, rev, ri, ci):
    beta = bg_ref[rows, col:col + 1]
    gc = bg_ref[rows, 8 + col:9 + col]
    last = 0 if rev else CHUNK - 1
    gl = bg_ref[pl.ds(r0 + last, 1), 8 + col:9 + col]
    out = dict(beta=beta, gc=gc, gl=gl, eg=jnp.exp(gc), egl=jnp.exp(gl - gc), cd=jnp.exp(gl))
    if gcr_ref is not None:
        incl = (ri <= ci) if rev else (ri >= ci)
        out["strict"] = (ri < ci) if rev else (ri > ci)
        out["dm"] = jnp.where(incl, jnp.exp(jnp.where(incl, gc - gcr_ref[c, col:col + 1, :], 0.0)), 0.0)
    return out


def _dir_tile(d, n_tiles, flip):
    if (d == 1) != flip:
        return lambda i: n_tiles - 1 - i
    return lambda i: i


def _gdn_local_fwd(q, k, v, bg, gcr):
    S = q.shape[0]
    ts = _tile(S, GDN_TS)
    ncb = ts // CHUNK
    nch = S // CHUNK

    def body(q_ref, k_ref, v_ref, bg_ref, gcr_ref, u0, w0, a0, t0, u1, w1, a1, t1):
        ri, ci = _tri_masks()
        eye = (ri == ci).astype(F32)
        outs = ((u0, w0, a0, t0), (u1, w1, a1, t1))

        def chunk(c, carry):
            r0 = pl.multiple_of(c * CHUNK, CHUNK)
            rows = pl.ds(r0, CHUNK)
            chains = []
            for h in range(GDN_H):
                cols = slice(h * GDN_DK, (h + 1) * GDN_DK)
                qh, kh, vh = q_ref[rows, cols], k_ref[rows, cols], v_ref[rows, cols]
                both = _bdot(jnp.concatenate([qh, kh], axis=0), kh, 1, 1)
                for d in range(2):
                    chains.append(dict(h=h, d=d, cols=cols, kh=kh, vh=vh, qk=both[0:CHUNK], kk=both[CHUNK:2 * CHUNK]))
            for ch in chains:
                m = _gdn_decay(bg_ref, gcr_ref, c, rows, r0, ch["d"] * GDN_H + ch["h"], ch["d"] == 1, ri, ci)
                ch["m"] = m
                ch["x"] = -jnp.where(m["strict"], m["beta"] * ch["kk"] * m["dm"], 0.0)
                ch["t"] = eye + ch["x"]
            for ch in chains:
                ch["pw"] = _bdot(ch["x"], ch["x"], 1, 0)
            for level in range(1, 6):
                last_level = level == 5
                for ch in chains:
                    rhs = ch["t"] if last_level else jnp.concatenate([ch["t"], ch["pw"]], axis=1)
                    ch["prod"] = _bdot(ch["pw"], rhs, 1, 0)
                for ch in chains:
                    ch["t"] = ch["t"] + ch["prod"][:, 0:CHUNK]
                    if not last_level:
                        ch["pw"] = ch["prod"][:, CHUNK:2 * CHUNK]
            for ch in chains:
                m = ch["m"]
                rhs = jnp.concatenate([ch["vh"] * m["beta"], ch["kh"] * (m["beta"] * m["eg"])], axis=1)
                ch["uw"] = _bdot(ch["t"], rhs, 1, 0)
            for ch in chains:
                u_ref, w_ref, a_ref, t_ref = outs[ch["d"]]
                u_ref[rows, ch["cols"]] = ch["uw"][:, 0:GDN_DK]
                w_ref[rows, ch["cols"]] = ch["uw"][:, GDN_DK:2 * GDN_DK].astype(BF16)
                a_ref[c, ch["h"]] = (ch["qk"] * ch["m"]["dm"]).astype(BF16)
                t_ref[c, ch["h"]] = ch["t"].astype(BF16)
            return carry

        lax.fori_loop(0, ncb, chunk, 0)

    im = lambda i: (i, 0)
    im4 = lambda i: (i, 0, 0, 0)
    ins = [(q, (ts, GDN_W), im), (k, (ts, GDN_W), im), (v, (ts, GDN_W), im), (bg, (ts, 128), im),
           (gcr, (ncb, 8, CHUNK), lambda i: (i, 0, 0))]
    per_dir = [(jax.ShapeDtypeStruct((S, GDN_W), F32), (ts, GDN_W), im),
               (jax.ShapeDtypeStruct((S, GDN_W), BF16), (ts, GDN_W), im),
               (jax.ShapeDtypeStruct((nch, GDN_H, CHUNK, CHUNK), BF16), (ncb, GDN_H, CHUNK, CHUNK), im4),
               (jax.ShapeDtypeStruct((nch, GDN_H, CHUNK, CHUNK), BF16), (ncb, GDN_H, CHUNK, CHUNK), im4)]
    res = _rows("gdn_local_fwd", S, ts, ins, per_dir * 2, body)
    return res[0:4], res[4:8]


def _gdn_scan_fwd(q, k, bg, loc):
    S = q.shape[0]
    ts = _tile(S, GDN_TS)
    n_tiles = S // ts
    ncb = ts // CHUNK
    nch = S // CHUNK

    def body(*refs):
        ins = (refs[0:6], refs[6:12])
        outs = (refs[12:15], refs[15:18])
        state = refs[18]

        @pl.when(pl.program_id(0) == 0)
        def _():
            state[...] = jnp.zeros_like(state)

        def chunk(cc, carry):
            chains = []
            for d in range(2):
                c = cc if d == 0 else ncb - 1 - cc
                r0 = pl.multiple_of(c * CHUNK, CHUNK)
                rows = pl.ds(r0, CHUNK)
                for h in range(GDN_H):
                    cols = slice(h * GDN_DK, (h + 1) * GDN_DK)
                    m = _gdn_decay(ins[d][2], None, c, rows, r0, d * GDN_H + h, d == 1, None, None)
                    chains.append(dict(d=d, h=h, c=c, rows=rows, cols=cols, m=m, st=state[d * GDN_H + h]))
            for ch in chains:
                q_ref, k_ref, bg_ref, u_ref, w_ref, a_ref = ins[ch["d"]]
                rows, cols = ch["rows"], ch["cols"]
                lhs = jnp.concatenate([w_ref[rows, cols], (q_ref[rows, cols] * ch["m"]["eg"]).astype(BF16)], axis=0)
                ch["ws_qs"] = _dot(lhs, ch["st"].astype(BF16), 1, 0)
            for ch in chains:
                q_ref, k_ref, bg_ref, u_ref, w_ref, a_ref = ins[ch["d"]]
                rows, cols = ch["rows"], ch["cols"]
                vn = u_ref[rows, cols] - ch["ws_qs"][0:CHUNK]
                vnb = vn.astype(BF16)
                ch["vn"] = vn
                ch["avn"] = _dot(a_ref[ch["c"], ch["h"]], vnb, 1, 0)
                ch["kvn"] = _bdot(k_ref[rows, cols] * ch["m"]["egl"], vnb, 0, 0)
            for ch in chains:
                o_ref, vn_ref, s_ref = outs[ch["d"]]
                rows, cols = ch["rows"], ch["cols"]
                o_ref[rows, cols] = ch["ws_qs"][CHUNK:2 * CHUNK] + ch["avn"]
                vn_ref[rows, cols] = ch["vn"]
                s_ref[ch["c"], ch["h"]] = ch["st"]
                state[ch["d"] * GDN_H + ch["h"]] = ch["st"] * ch["m"]["cd"] + ch["kvn"]
            return carry

        lax.fori_loop(0, ncb, chunk, 0)

    ins, outs = [], []
    for d in range(2):
        tix = _dir_tile(d, n_tiles, False)
        im = lambda i, tix=tix: (tix(i), 0)
        im4 = lambda i, tix=tix: (tix(i), 0, 0, 0)
        u, w, a, _ = loc[d]
        ins += [(q, (ts, GDN_W), im), (k, (ts, GDN_W), im), (bg, (ts, 128), im), (u, (ts, GDN_W), im),
                (w, (ts, GDN_W), im), (a, (ncb, GDN_H, CHUNK, CHUNK), im4)]
        outs += [(jax.ShapeDtypeStruct((S, GDN_W), F32), (ts, GDN_W), im),
                 (jax.ShapeDtypeStruct((S, GDN_W), F32), (ts, GDN_W), im),
                 (jax.ShapeDtypeStruct((nch, GDN_H, GDN_DK, GDN_DK), F32), (ncb, GDN_H, GDN_DK, GDN_DK), im4)]
    res = _rows("gdn_scan_fwd", S, ts, ins, outs, body, scratch=[pltpu.VMEM((2 * GDN_H, GDN_DK, GDN_DK), F32)])
    return res[0:3], res[3:6]


def _gelu(x):
    c = math.sqrt(2.0 / math.pi)
    t = jnp.tanh(c * (x + 0.044715 * x * x * x))
    return 0.5 * x * (1.0 + t), t


def _mix_out_fwd(h_f, h_b, o_f, o_b, p, gn):
    S = h_f.shape[0]
    ts = _tile(S, 512)

    def body(hf, hb, of, ob, gate, z, gn_ref, y_ref):
        ge, _ = _gelu(gate[...])
        y_ref[:, 0:RG_W] = ((hf[...] + hb[...]) * ge).astype(BF16)
        o = of[...] + ob[...]
        zv = z[...]
        sz = zv * _sigmoid(zv)
        for h in range(GDN_H):
            cols = slice(h * GDN_DK, (h + 1) * GDN_DK)
            oh = o[:, cols]
            n = oh * lax.rsqrt(jnp.mean(oh * oh, axis=-1, keepdims=True) + EPS) * gn_ref[...]
            y_ref[:, RG_W + h * GDN_DK:RG_W + (h + 1) * GDN_DK] = (n * sz[:, cols]).astype(BF16)

    blk = (ts, RG_W)
    im = lambda i: (i, 0)
    ins = [(h_f, blk, im), (h_b, blk, im), (o_f, blk, im), (o_b, blk, im),
           (p, blk, lambda i: (i, 1)), (p, blk, lambda i: (i, 5)), (gn, (1, GDN_DK), lambda i: (0, 0))]
    return _rows("mix_out_fwd", S, ts, ins,
                 [(jax.ShapeDtypeStruct((S, D_MODEL), BF16), (ts, D_MODEL), im)], body)[0]


def _loss_head(x, target, g):
    S, D = x.shape
    ts = _tile(S, 512)

    def body(x_ref, t_ref, g_ref, dx_ref, loss_ref, dg_ref):
        @pl.when(pl.program_id(0) == 0)
        def _():
            loss_ref[...] = jnp.zeros_like(loss_ref)
            dg_ref[...] = jnp.zeros_like(dg_ref)

        xv = x_ref[...]
        gv = g_ref[...]
        r = lax.rsqrt(jnp.mean(xv * xv, axis=-1, keepdims=True) + EPS)
        err = xv * r * gv - t_ref[...]
        loss_ref[...] += jnp.sum(err * err) * (0.5 / D)
        dx, dgt = _rmsnorm_bwd_tile(err * (1.0 / D), xv, gv)
        dx_ref[...] = dx
        dg_ref[...] += jnp.sum(dgt, axis=0, keepdims=True)

    im = lambda i: (i, 0)
    z = lambda i: (0, 0)
    return _rows("loss_head", S, ts,
                 [(x, (ts, D), im), (target, (ts, D), im), (g, (1, D), z)],
                 [(jax.ShapeDtypeStruct((S, D), F32), (ts, D), im),
                  (jax.ShapeDtypeStruct((8, 128), F32), (8, 128), z),
                  (jax.ShapeDtypeStruct((1, D), F32), (1, D), z)], body)


def _block_diag(w):
    n = w.shape[0]
    return jnp.einsum("nij,nm->nimj", w, jnp.eye(n, dtype=w.dtype)).reshape(n * w.shape[1], n * w.shape[2])


def _rg_bd(a_w, x_w):
    return jnp.concatenate([_block_diag(a_w[0]), _block_diag(x_w[0]), _block_diag(a_w[1]), _block_diag(x_w[1])],
                           axis=1).astype(BF16)


def _rg_prm(ba, bx, lam):
    return jnp.concatenate([ba[0:1], bx[0:1], ba[1:2], bx[1:2], lam, jnp.zeros((2, RG_W), F32)], axis=0)


def _gdn_prm(a_log, dt_bias):
    rows = jnp.zeros((8, 128), F32)
    rows = rows.at[0, 8:16].set(a_log.reshape(-1))
    return rows.at[1, 8:16].set(dt_bias.reshape(-1))


def _gc_rows(bg):
    S = bg.shape[0]
    return bg[:, 8:16].reshape(S // CHUNK, CHUNK, 8).transpose(0, 2, 1)


def _layer_fwd(x0, target, W):
    S = x0.shape[0]
    R = {}
    R["h1"] = _rmsnorm_fwd("rms1", x0, W["ffn1_norm"])
    R["x1"], R["a1"], R["b1"], R["f1"] = _ffn_fwd("ffn1", x0, R["h1"], W["ffn1_w_gate"], W["ffn1_w_up"], W["ffn1_w_down"])
    R["h2"] = _rmsnorm_fwd("rms2", R["x1"], W["mix_norm"])
    tm = _tile(S, 512)
    R["p"] = _fused_mm("in_proj", S, D_IN_PAD, D_MODEL, tm, 640, D_MODEL, [(R["h2"], "mk"), (W["w_in"], "kn")],
                       [(0, 1, 0)], [], [(jax.ShapeDtypeStruct((S, D_IN_PAD), F32), (tm, 640), _mn)],
                       lambda i, accs, ex, out: out[0].__setitem__(Ellipsis, accs[0][...]))[0]
    p = R["p"]
    R["xc"] = _conv_fwd("rg_conv_fwd", p, 0, W["rg_conv_w"], W["rg_conv_b"], "bias")
    R["bd"] = _rg_bd(W["rg_gate_a_w"], W["rg_gate_x_w"])
    R["rg_prm"] = _rg_prm(W["rg_gate_a_b"], W["rg_gate_x_b"], W["rg_lambda"])
    a_f, b_f, a_b, b_b = _rg_gates_fwd(R["xc"], R["bd"], R["rg_prm"])
    R["a_f"], R["a_b"] = a_f, a_b
    R["h_f"], R["h_b"] = _rg_scan("rg_scan_fwd", a_f, b_f, a_b, b_b)
    zero_b = jnp.zeros((1, RG_W), F32)
    cw = W["gdn_conv_w"]
    R["q"] = _conv_fwd("gdn_conv_q", p, 2, cw[:, 0:512], zero_b, "q")
    R["k"] = _conv_fwd("gdn_conv_k", p, 3, cw[:, 512:1024], zero_b, "k")
    R["v"] = _conv_fwd("gdn_conv_v", p, 4, cw[:, 1024:1536], zero_b, "v")
    R["gdn_prm"] = _gdn_prm(W["gdn_a_log"], W["gdn_dt_bias"])
    R["bg"] = _gdn_prep_fwd(p, R["gdn_prm"])
    R["gcr"] = _gc_rows(R["bg"])
    R["gdn_loc"] = _gdn_local_fwd(R["q"], R["k"], R["v"], R["bg"], R["gcr"])
    R["gdn_fwd"] = _gdn_scan_fwd(R["q"], R["k"], R["bg"], R["gdn_loc"])
    R["o_f"], R["o_b"] = R["gdn_fwd"][0][0], R["gdn_fwd"][1][0]
    R["y"] = _mix_out_fwd(R["h_f"], R["h_b"], R["o_f"], R["o_b"], p, W["gdn_norm"])
    R["x2"] = _fused_mm("out_proj", S, D_MODEL, D_MODEL, tm, D_MODEL, D_MODEL, [(R["y"], "mk"), (W["w_out"], "kn")],
                        [(0, 1, 0)], [(R["x1"], (tm, D_MODEL), _mn)],
                        [(jax.ShapeDtypeStruct((S, D_MODEL), F32), (tm, D_MODEL), _mn)],
                        lambda i, accs, ex, out: out[0].__setitem__(Ellipsis, ex[0][...] + accs[0][...]))[0]
    R["h3"] = _rmsnorm_fwd("rms3", R["x2"], W["ffn2_norm"])
    R["x3"], R["a2"], R["b2"], R["f2"] = _ffn_fwd("ffn2", R["x2"], R["h3"], W["ffn2_w_gate"], W["ffn2_w_up"], W["ffn2_w_down"])
    R["dx3"], R["loss"], R["d_final_norm"] = _loss_head(R["x3"], target, W["final_norm"])
    return R


def _colsum_into(ref, i, val):
    @pl.when(i == 0)
    def _():
        ref[...] = val

    @pl.when(i > 0)
    def _():
        ref[...] += val


def _ffn_bwd(tag, dout, x, g, h, a, b, f, wg, wu, wd, emit):
    S = x.shape[0]
    tm = _tile(S, 512)
    tk_s = _tile(S, 512)
    dwd = _fused_mm(f"{tag}_dw_down", D_FF, D_MODEL, S, 1408, D_MODEL, tk_s, [(f, "km"), (dout, "kn")], [(0, 1, 0)], [],
                    [(jax.ShapeDtypeStruct((D_FF, D_MODEL), BF16), (1408, D_MODEL), _mn)],
                    lambda i, accs, ex, out: out[0].__setitem__(Ellipsis, (0.5 * accs[0][...]).astype(BF16)))[0]
    emit(down=dwd)

    def epi_act(i, accs, ex, out):
        df = 0.5 * accs[0][...]
        av = ex[0][...].astype(F32)
        bv = ex[1][...].astype(F32)
        s = _sigmoid(av)
        out[0][...] = (df * bv * (s * (1.0 + av * (1.0 - s)))).astype(BF16)
        out[1][...] = (df * av * s).astype(BF16)

    sds = jax.ShapeDtypeStruct((S, D_FF), BF16)
    da, db = _fused_mm(f"{tag}_dact", S, D_FF, D_MODEL, tm, 1408, D_MODEL, [(dout, "mk"), (wd, "nk")], [(0, 1, 0)],
                       [(a, (tm, 1408), _mn), (b, (tm, 1408), _mn)], [(sds, (tm, 1408), _mn)] * 2, epi_act)

    def epi_w2(i, accs, ex, out):
        out[0][...] = accs[0][...].astype(BF16)
        out[1][...] = accs[1][...].astype(BF16)

    sdw = jax.ShapeDtypeStruct((D_MODEL, D_FF), BF16)
    dwg, dwu = _fused_mm(f"{tag}_dw_up", D_MODEL, D_FF, S, D_MODEL, 1408, tk_s,
                         [(h, "km"), (da, "kn"), (db, "kn")], [(0, 1, 0), (0, 2, 1)], [],
                         [(sdw, (D_MODEL, 1408), _mn)] * 2, epi_w2)
    tok = emit(gate=dwg, up=dwu)
    if tok is not None:
        g = g + tok

    def epi_dx(i, accs, ex, out):
        dx, dgt = _rmsnorm_bwd_tile(accs[0][...], ex[0][...], ex[1][...])
        out[0][...] = ex[2][...] + dx
        _colsum_into(out[1], i, jnp.sum(dgt, axis=0, keepdims=True))

    dx, dg = _fused_mm(f"{tag}_dx", S, D_MODEL, D_FF, tm, D_MODEL, 1408,
                       [(da, "mk"), (wg, "nk"), (db, "mk"), (wu, "nk")], [(0, 1, 0), (2, 3, 0)],
                       [(x, (tm, D_MODEL), _mn), (g, (1, D_MODEL), _row0), (dout, (tm, D_MODEL), _mn)],
                       [(jax.ShapeDtypeStruct((S, D_MODEL), F32), (tm, D_MODEL), _mn),
                        (jax.ShapeDtypeStruct((1, D_MODEL), F32), (1, D_MODEL), _row0)], epi_dx)
    return dx, dg


def _mix_out_bwd(dy, h_f, h_b, o_f, o_b, p, gn):
    S = dy.shape[0]
    ts = _tile(S, 512)
    c0 = math.sqrt(2.0 / math.pi)

    def body(dy_ref, hf, hb, of, ob, gate, z, gn_ref, dhr_ref, dgate_ref, do_ref, dz_ref, dgn_ref):
        i = pl.program_id(0)
        gv = gate[...]
        ge, t = _gelu(gv)
        dy_rg = dy_ref[:, 0:RG_W]
        dhr_ref[...] = dy_rg * ge
        dgelu = 0.5 * (1.0 + t) + 0.5 * gv * (1.0 - t * t) * c0 * (1.0 + 3.0 * 0.044715 * gv * gv)
        dgate_ref[...] = (dy_rg * (hf[...] + hb[...]) * dgelu).astype(BF16)
        o = of[...] + ob[...]
        zv = z[...]
        sig = _sigmoid(zv)
        gnv = gn_ref[...]
        dgn = jnp.zeros((1, GDN_DK), F32)
        for h in range(GDN_H):
            cols = slice(h * GDN_DK, (h + 1) * GDN_DK)
            oh = o[:, cols]
            r = lax.rsqrt(jnp.mean(oh * oh, axis=-1, keepdims=True) + EPS)
            ohat = oh * r
            dyh = dy_ref[:, RG_W + h * GDN_DK:RG_W + (h + 1) * GDN_DK]
            zh = zv[:, cols]
            sh = sig[:, cols]
            dn = dyh * zh * sh
            dz_ref[:, cols] = (dyh * ohat * gnv * (sh * (1.0 + zh * (1.0 - sh)))).astype(BF16)
            dxn = dn * gnv
            do_ref[:, cols] = r * (dxn - ohat * jnp.mean(dxn * ohat, axis=-1, keepdims=True))
            dgn = dgn + jnp.sum(dn * ohat, axis=0, keepdims=True)
        _colsum_into(dgn_ref, i, dgn)

    blk = (ts, RG_W)
    im = lambda i: (i, 0)
    z0 = lambda i: (0, 0)
    ins = [(dy, (ts, D_MODEL), im), (h_f, blk, im), (h_b, blk, im), (o_f, blk, im), (o_b, blk, im),
           (p, blk, lambda i: (i, 1)), (p, blk, lambda i: (i, 5)), (gn, (1, GDN_DK), z0)]
    outs = [(jax.ShapeDtypeStruct((S, RG_W), F32), blk, im), (jax.ShapeDtypeStruct((S, RG_W), BF16), blk, im),
            (jax.ShapeDtypeStruct((S, RG_W), F32), blk, im), (jax.ShapeDtypeStruct((S, RG_W), BF16), blk, im),
            (jax.ShapeDtypeStruct((1, GDN_DK), F32), (1, GDN_DK), z0)]
    return _rows("mix_out_bwd", S, ts, ins, outs, body)


def _rg_scan_adj(name, a_up, b_up, a_dn, b_dn):
    S, C = a_up.shape
    ts = _tile(S, 512)
    n_tiles = S // ts

    def body(au, bu, ad, bd, mu_ref, lam_ref, carry):
        @pl.when(pl.program_id(0) == 0)
        def _():
            carry[...] = jnp.zeros_like(carry)

        def step(t, c):
            cu, cd = c
            mu = bu[pl.ds(t, 1), :] + cu
            mu_ref[pl.ds(t, 1), :] = mu
            cu = au[pl.ds(t, 1), :] * mu
            tb = ts - 1 - t
            lam = bd[pl.ds(tb, 1), :] + cd
            lam_ref[pl.ds(tb, 1), :] = lam
            cd = ad[pl.ds(tb, 1), :] * lam
            return cu, cd

        cu, cd = lax.fori_loop(0, ts, step, (carry[0:1, :], carry[1:2, :]), unroll=8)
        carry[0:1, :] = cu
        carry[1:2, :] = cd

    fw = lambda i: (i, 0)
    bw = lambda i: (n_tiles - 1 - i, 0)
    sds = jax.ShapeDtypeStruct((S, C), F32)
    return _rows(name, S, ts,
                 [(a_up, (ts, C), fw), (b_up, (ts, C), fw), (a_dn, (ts, C), bw), (b_dn, (ts, C), bw)],
                 [(sds, (ts, C), fw), (sds, (ts, C), bw)], body, scratch=[pltpu.VMEM((8, C), F32)])


def _halo_ex(arr, S, tm, width):
    per = tm // HALO
    last = S // HALO - 1
    return [
        (arr, (tm, width), lambda i, j: (i, 0)),
        (arr, (HALO, width), lambda i, j: (jnp.maximum(i * per - 1, 0), 0)),
        (arr, (HALO, width), lambda i, j: (jnp.minimum((i + 1) * per, last), 0)),
    ]


def _rg_gates_bwd(xc, bd, prm, lam_f, lam_b, h_f, h_b):
    S = xc.shape[0]
    tm = _tile(S, 256)
    n_tiles = S // tm

    def epi(i, accs, ex, out):
        pre = accs[0][...]
        xv = ex[0][...]
        prm_ref = ex[1]
        lams = (ex[2][...], ex[3][...])
        hprev = (_shift(_ext(ex[4], ex[5], ex[6], i, n_tiles), -1, tm),
                 _shift(_ext(ex[7], ex[8], ex[9], i, n_tiles), 1, tm))
        dxc = jnp.zeros_like(xv)
        rows = []
        dlam_rows = []
        for d in range(2):
            r, ig, sp, a, sq = _rg_gate_terms(pre, xv, prm_ref, d)
            lam = lams[d]
            da = lam * hprev[d]
            di = lam * sq * xv
            dxc = dxc + lam * sq * ig
            dsq = lam * ig * xv
            dlog_a = da * a - dsq * (a * a) / sq
            dpre_r = dlog_a * (-RG_C * sp) * r * (1.0 - r)
            dpre_i = di * ig * (1.0 - ig)
            out[0][:, d * 1024:d * 1024 + RG_W] = dpre_r.astype(BF16)
            out[0][:, d * 1024 + RG_W:(d + 1) * 1024] = dpre_i.astype(BF16)
            rows += [jnp.sum(dpre_r, axis=0, keepdims=True), jnp.sum(dpre_i, axis=0, keepdims=True)]
            dsp = jnp.sum(dlog_a * (-RG_C * r), axis=0, keepdims=True)
            dlam_rows.append(-dsp * _sigmoid(-prm_ref[4 + d:5 + d, :]))
        out[1][...] = dxc
        zero = jnp.zeros((2, RG_W), F32)
        _colsum_into(out[2], i, jnp.concatenate(rows + dlam_rows + [zero], axis=0))

    blk = (tm, RG_W)
    im = lambda i, j: (i, 0)
    extras = ([(xc, blk, im), (prm, (8, RG_W), _row0), (lam_f, blk, im), (lam_b, blk, im)]
              + _halo_ex(h_f, S, tm, RG_W) + _halo_ex(h_b, S, tm, RG_W))
    outs = [(jax.ShapeDtypeStruct((S, 4 * RG_W), BF16), (tm, 4 * RG_W), im),
            (jax.ShapeDtypeStruct((S, RG_W), F32), blk, im),
            (jax.ShapeDtypeStruct((8, RG_W), F32), (8, RG_W), _row0)]
    return _fused_mm("rg_gates_bwd", S, 4 * RG_W, RG_W, tm, 4 * RG_W, RG_W, [(xc, "mk"), (bd, "kn")], [(0, 1, 0)],
                     extras, outs, epi)


def _roll_rows(ext, off):
    if off == 0:
        return ext
    return pltpu.roll(ext, (-off) % ext.shape[0], 0)


def _conv_bwd(name, p, colblk, w, grads, mode):
    S = p.shape[0]
    ts = _tile(S, 512)
    n_tiles = S // ts
    C = w.shape[1]
    ng = len(grads)

    def body(*refs):
        p_refs = refs[0:3]
        g_refs = refs[3:3 + 3 * ng]
        w_ref = refs[3 + 3 * ng]
        dx_ref, dw_ref, db_ref = refs[4 + 3 * ng:]
        i = pl.program_id(0)
        ext_p = _ext(*p_refs, i, n_tiles)
        dn = _ext(*g_refs[0:3], i, n_tiles)
        for gi in range(1, ng):
            dn = dn + _ext(*g_refs[3 * gi:3 * gi + 3], i, n_tiles)
        if mode == "bias":
            dc = dn
        else:
            c = None
            for j in range(CONV_W):
                term = w_ref[j:j + 1, :] * _roll_rows(ext_p, j - 2)
                c = term if c is None else c + term
            sig = _sigmoid(c)
            s = c * sig
            if mode in ("q", "k"):
                scale = GDN_DK ** -0.5 if mode == "q" else 1.0
                parts = []
                for h in range(GDN_H):
                    cols = slice(h * GDN_DK, (h + 1) * GDN_DK)
                    sh = s[:, cols]
                    dnh = dn[:, cols]
                    rinv = lax.rsqrt(jnp.sum(sh * sh, axis=-1, keepdims=True) + EPS)
                    parts.append(scale * rinv * (dnh - sh * (rinv * rinv) * jnp.sum(dnh * sh, axis=-1, keepdims=True)))
                ds = jnp.concatenate(parts, axis=-1)
            else:
                ds = dn
            dc = ds * (sig * (1.0 + c * (1.0 - sig)))
        dx = None
        for j in range(CONV_W):
            term = w_ref[j:j + 1, :] * _shift(dc, 2 - j, ts)
            dx = term if dx is None else dx + term
        dx_ref[...] = dx.astype(BF16)
        dc_main = dc[HALO:HALO + ts]
        dw = jnp.concatenate([jnp.sum(dc_main * _shift(ext_p, j - 2, ts), axis=0, keepdims=True)
                              for j in range(CONV_W)], axis=0)
        _colsum_into(dw_ref, i, dw)
        _colsum_into(db_ref, i, jnp.sum(dc_main, axis=0, keepdims=True))

    ins = _halo_ins(p, S, ts, C, colblk)
    for garr in grads:
        ins += _halo_ins(garr, S, ts, C, 0)
    ins += [(w, (CONV_W, C), lambda i: (0, 0))]
    z0 = lambda i: (0, 0)
    outs = [(jax.ShapeDtypeStruct((S, C), BF16), (ts, C), lambda i: (i, 0)),
            (jax.ShapeDtypeStruct((CONV_W, C), F32), (CONV_W, C), z0),
            (jax.ShapeDtypeStruct((1, C), F32), (1, C), z0)]
    return _rows(name, S, ts, ins, outs, body)


def _gdn_scan_bwd(q, k, bg, loc, do):
    S = q.shape[0]
    ts = _tile(S, GDN_TS)
    n_tiles = S // ts
    ncb = ts // CHUNK
    nch = S // CHUNK

    def body(*refs):
        ins = (refs[0:6], refs[6:12])
        outs = (refs[12:14], refs[14:16])
        dstate = refs[16]

        @pl.when(pl.program_id(0) == 0)
        def _():
            dstate[...] = jnp.zeros_like(dstate)

        def chunk(cc, carry):
            chains = []
            for d in range(2):
                c = ncb - 1 - cc if d == 0 else cc
                r0 = pl.multiple_of(c * CHUNK, CHUNK)
                rows = pl.ds(r0, CHUNK)
                for h in range(GDN_H):
                    cols = slice(h * GDN_DK, (h + 1) * GDN_DK)
                    m = _gdn_decay(ins[d][2], None, c, rows, r0, d * GDN_H + h, d == 1, None, None)
                    chains.append(dict(d=d, h=h, c=c, rows=rows, cols=cols, m=m, dsn=dstate[d * GDN_H + h]))
            for ch in chains:
                q_ref, k_ref, bg_ref, w_ref, a_ref, do_ref = ins[ch["d"]]
                rows, cols = ch["rows"], ch["cols"]
                dob = do_ref[rows, cols].astype(BF16)
                ch["dvn"] = (_dot(a_ref[ch["c"], ch["h"]], dob, 0, 0)
                             + _bdot(k_ref[rows, cols] * ch["m"]["egl"], ch["dsn"], 1, 0))
                ch["qdo"] = _bdot(q_ref[rows, cols] * ch["m"]["eg"], dob, 0, 0)
            for ch in chains:
                w_ref = ins[ch["d"]][3]
                ch["wdvn"] = _dot(w_ref[ch["rows"], ch["cols"]], ch["dvn"].astype(BF16), 0, 0)
            for ch in chains:
                dvn_ref, ds_ref = outs[ch["d"]]
                dvn_ref[ch["rows"], ch["cols"]] = ch["dvn"]
                ds_ref[ch["c"], ch["h"]] = ch["dsn"]
                dstate[ch["d"] * GDN_H + ch["h"]] = ch["qdo"] + ch["m"]["cd"] * ch["dsn"] - ch["wdvn"]
            return carry

        lax.fori_loop(0, ncb, chunk, 0)

    ins, outs = [], []
    for d in range(2):
        tix = _dir_tile(d, n_tiles, True)
        im = lambda i, tix=tix: (tix(i), 0)
        im4 = lambda i, tix=tix: (tix(i), 0, 0, 0)
        _, w, a, _ = loc[d]
        ins += [(q, (ts, GDN_W), im), (k, (ts, GDN_W), im), (bg, (ts, 128), im), (w, (ts, GDN_W), im),
                (a, (ncb, GDN_H, CHUNK, CHUNK), im4), (do, (ts, GDN_W), im)]
        outs += [(jax.ShapeDtypeStruct((S, GDN_W), F32), (ts, GDN_W), im),
                 (jax.ShapeDtypeStruct((nch, GDN_H, GDN_DK, GDN_DK), F32), (ncb, GDN_H, GDN_DK, GDN_DK), im4)]
    res = _rows("gdn_scan_bwd", S, ts, ins, outs, body, scratch=[pltpu.VMEM((2 * GDN_H, GDN_DK, GDN_DK), F32)])
    return res[0:2], res[2:4]


def _gdn_local_bwd(q, k, v, bg, gcr, do, loc, fwd, adj):
    S = q.shape[0]
    ts = _tile(S, GDN_TS)
    ncb = ts // CHUNK

    def body(q_ref, k_ref, v_ref, bg_ref, gcr_ref, do_ref, *rest):
        per_dir = (rest[0:5], rest[5:10])
        dq_ref, dk_ref, dv_ref, dbg_ref = rest[10:14]
        ri, ci = _tri_masks()
        lane = lax.broadcasted_iota(jnp.int32, (CHUNK, 128), 1)
        rowi = lax.broadcasted_iota(jnp.int32, (CHUNK, 1), 0)
        ones = jnp.ones((CHUNK, 128), F32)

        def chunk(c, carry):
            r0 = pl.multiple_of(c * CHUNK, CHUNK)
            rows = pl.ds(r0, CHUNK)
            chains = []
            for h in range(GDN_H):
                cols = slice(h * GDN_DK, (h + 1) * GDN_DK)
                qh, kh, vh = q_ref[rows, cols], k_ref[rows, cols], v_ref[rows, cols]
                dob = do_ref[rows, cols].astype(BF16)
                both = _bdot(jnp.concatenate([qh, kh], axis=0), kh, 1, 1)
                for d in range(2):
                    chains.append(dict(h=h, d=d, cols=cols, qh=qh, kh=kh, vh=vh, dob=dob, qk=both[0:CHUNK],
                                       kk=both[CHUNK:2 * CHUNK], col=d * GDN_H + h))
            for ch in chains:
                m = _gdn_decay(bg_ref, gcr_ref, c, rows, r0, ch["col"], ch["d"] == 1, ri, ci)
                t_ref, s_ref, ds_ref, vn_ref, dvn_ref = per_dir[ch["d"]]
                h, cols = ch["h"], ch["cols"]
                ch["m"] = m
                ch["kb"] = ch["kh"] * m["beta"]
                ch["kbg"] = ch["kb"] * m["eg"]
                ch["t"] = t_ref[c, h]
                st = s_ref[c, h]
                stb = st.astype(BF16)
                ch["dsn"] = ds_ref[c, h]
                vnb = vn_ref[rows, cols].astype(BF16)
                dvnb = dvn_ref[rows, cols].astype(BF16)
                ch["dcd"] = jnp.sum(jnp.sum(st * ch["dsn"], axis=1, keepdims=True), axis=0, keepdims=True)
                ch["dqd"] = _dot(ch["dob"], stb, 1, 1)
                ch["d_a"] = _dot(ch["dob"], vnb, 1, 1)
                ch["dkd"] = _bdot(vnb, ch["dsn"], 1, 1)
                ch["dw"] = -_dot(dvnb, stb, 1, 1)
                ch["dvb"] = _dot(ch["t"], dvnb, 0, 0)
                ch["d_t"] = _bdot(dvnb, ch["vh"] * m["beta"], 1, 1)
            for ch in chains:
                dwb = ch["dw"].astype(BF16)
                ch["d_t"] = ch["d_t"] + _bdot(dwb, ch["kbg"], 1, 1)
                ch["dkbg"] = _dot(ch["t"], dwb, 0, 0)
                ch["nn"] = ch["d_a"] * ch["m"]["dm"]
                ch["nn_q"] = _bdot(ch["nn"], ch["qh"], 0, 0)
                ch["nn_k"] = _bdot(ch["nn"], ch["kh"], 1, 0)
            for ch in chains:
                ch["x"] = _dot(ch["d_t"].astype(BF16), ch["t"], 1, 1)
            for ch in chains:
                d_l = -_dot(ch["t"], ch["x"].astype(BF16), 0, 0)
                ch["d_l"] = jnp.where(ch["m"]["strict"], d_l, 0.0)
                ch["mm"] = ch["d_l"] * ch["m"]["dm"]
            for ch in chains:
                m = ch["m"]
                ch["mm_kh"] = _bdot(ch["mm"], ch["kh"], 1, 0)
                ch["mm_kb"] = _bdot(ch["mm"], ch["kb"], 0, 0)
                l_mat = jnp.where(m["strict"], m["beta"] * ch["kk"] * m["dm"], 0.0)
                ch["e"] = ch["d_l"] * l_mat + ch["nn"] * ch["qk"]
                ch["cs"] = _dot(ch["e"], ones, 0, 0, HI)[:, 0:1]
            acc_bg = jnp.zeros((CHUNK, 128), F32)
            acc = {}
            for ch in chains:
                m = ch["m"]
                beta, eg, egl = m["beta"], m["eg"], m["egl"]
                dkb = ch["mm_kh"] + ch["dkbg"] * eg
                dk_d = ch["mm_kb"] + ch["nn_q"] + ch["dkd"] * egl + dkb * beta
                dq_d = ch["nn_k"] + ch["dqd"] * eg
                dv_d = ch["dvb"] * beta
                rs = jnp.sum(ch["e"], axis=1, keepdims=True)
                dkd_kd = ch["dkd"] * (ch["kh"] * egl)
                dgc = (rs - ch["cs"] + jnp.sum(ch["dqd"] * (ch["qh"] * eg), axis=1, keepdims=True)
                       - jnp.sum(dkd_kd, axis=1, keepdims=True) + jnp.sum(ch["dkbg"] * ch["kbg"], axis=1, keepdims=True))
                dgl = jnp.sum(jnp.sum(dkd_kd, axis=1, keepdims=True), axis=0, keepdims=True) + ch["dcd"] * m["cd"]
                dgc = dgc + jnp.where(rowi == (0 if ch["d"] == 1 else CHUNK - 1), dgl, 0.0)
                dbeta = (jnp.sum(dkb * ch["kh"], axis=1, keepdims=True)
                         + jnp.sum(ch["dvb"] * ch["vh"], axis=1, keepdims=True))
                acc_bg = acc_bg + jnp.where(lane == ch["col"], dbeta, 0.0) + jnp.where(lane == 8 + ch["col"], dgc, 0.0)
                if ch["d"] == 0:
                    acc[ch["h"]] = (dq_d, dk_d, dv_d)
                else:
                    dq0, dk0, dv0 = acc[ch["h"]]
                    dq_ref[rows, ch["cols"]] = dq0 + dq_d
                    dk_ref[rows, ch["cols"]] = dk0 + dk_d
                    dv_ref[rows, ch["cols"]] = dv0 + dv_d
            dbg_ref[rows, :] = acc_bg
            return carry

        lax.fori_loop(0, ncb, chunk, 0)

    im = lambda i: (i, 0)
    im4 = lambda i: (i, 0, 0, 0)
    blk = (ts, GDN_W)
    ins = [(q, blk, im), (k, blk, im), (v, blk, im), (bg, (ts, 128), im), (gcr, (ncb, 8, CHUNK), lambda i: (i, 0, 0)),
           (do, blk, im)]
    for d in range(2):
        ins += [(loc[d][3], (ncb, GDN_H, CHUNK, CHUNK), im4), (fwd[d][2], (ncb, GDN_H, GDN_DK, GDN_DK), im4),
                (adj[d][1], (ncb, GDN_H, GDN_DK, GDN_DK), im4), (fwd[d][1], blk, im), (adj[d][0], blk, im)]
    sds = jax.ShapeDtypeStruct((S, GDN_W), F32)
    outs = [(sds, blk, im), (sds, blk, im), (sds, blk, im), (jax.ShapeDtypeStruct((S, 128), F32), (ts, 128), im)]
    return _rows("gdn_local_bwd", S, ts, ins, outs, body)


def _gdn_prep_bwd(dbg_all, p, prm):
    S = p.shape[0]
    ts = _tile(S, 512)

    def body(dbg_ref, p_ref, prm_ref, dba_ref, dprm_ref):
        i = pl.program_id(0)
        raw = p_ref[...]
        dbg = dbg_ref[...]
        lane = lax.broadcasted_iota(jnp.int32, (1, 128), 1)
        is_g = (lane >= 8) & (lane < 16)
        ea = jnp.exp(prm_ref[0:1, :])
        arg = raw + prm_ref[1:2, :]
        g = jnp.where(is_g, -ea * _softplus(arg), 0.0)
        beta = _sigmoid(raw)
        dgc = jnp.where(is_g, dbg, 0.0)
        ri, ci = _tri_masks()
        lower = (ri >= ci).astype(F32)
        upper = (ri <= ci).astype(F32)
        dgs = []
        for c in range(ts // CHUNK):
            ch = dgc[c * CHUNK:(c + 1) * CHUNK]
            dgs.append(jnp.where(lane < 12, _dot(upper, ch, 1, 0, HI), _dot(lower, ch, 1, 0, HI)))
        dg = jnp.concatenate(dgs, axis=0)
        dalpha = jnp.where(is_g, dg * (-ea) * _sigmoid(arg), 0.0)
        dba_ref[...] = jnp.where(lane < 8, dbg * beta * (1.0 - beta), dalpha).astype(BF16)
        rows = jnp.concatenate([jnp.sum(dg * g, axis=0, keepdims=True), jnp.sum(dalpha, axis=0, keepdims=True),
                                jnp.zeros((6, 128), F32)], axis=0)
        _colsum_into(dprm_ref, i, rows)

    im = lambda i: (i, 0)
    z0 = lambda i: (0, 0)
    return _rows("gdn_prep_bwd", S, ts,
                 [(dbg_all, (ts, 128), im), (p, (ts, 128), lambda i: (i, COL_BA // 128)), (prm, (8, 128), z0)],
                 [(jax.ShapeDtypeStruct((S, 128), BF16), (ts, 128), im), (jax.ShapeDtypeStruct((8, 128), F32), (8, 128), z0)],
                 body)


def _mm_plain(name, M, N, K, tm, tn, tk, a, am, b, bm, dtype):
    return _fused_mm(name, M, N, K, tm, tn, tk, [(a, am), (b, bm)], [(0, 1, 0)], [],
                     [(jax.ShapeDtypeStruct((M, N), dtype), (tm, tn), _mn)],
                     lambda i, accs, ex, out: out[0].__setitem__(Ellipsis, accs[0][...].astype(dtype)))[0]


def _layer_bwd(x0, W, R, emit_big=None):
    S = x0.shape[0]
    tm = _tile(S, 512)
    tk_s = _tile(S, 512)
    G = {}

    def emit(**named):
        if emit_big is None:
            G.update(named)
            return None
        return emit_big(**named)

    def ffn_emit(prefix):
        return lambda **kw: emit(**{f"{prefix}_w_{k}": v for k, v in kw.items()})

    dx2, G["ffn2_norm"] = _ffn_bwd("ffn2b", R["dx3"], R["x2"], W["ffn2_norm"], R["h3"], R["a2"], R["b2"], R["f2"],
                                   W["ffn2_w_gate"], W["ffn2_w_up"], W["ffn2_w_down"], ffn_emit("ffn2"))
    tok = emit(w_out=_mm_plain("dw_out", D_MODEL, D_MODEL, S, D_MODEL, D_MODEL, tk_s, R["y"], "km", dx2, "kn", BF16))
    gn = W["gdn_norm"] if tok is None else W["gdn_norm"] + tok
    dy = _mm_plain("dy_mix", S, D_MODEL, D_MODEL, tm, D_MODEL, D_MODEL, dx2, "mk", W["w_out"], "nk", F32)
    p = R["p"]
    dhr, dgate, do, dz, G["gdn_norm"] = _mix_out_bwd(dy, R["h_f"], R["h_b"], R["o_f"], R["o_b"], p, gn)
    lam_b, lam_f = _rg_scan_adj("rg_scan_bwd", R["a_b"], dhr, R["a_f"], dhr)
    dpre, dxc_direct, d_rgprm = _rg_gates_bwd(R["xc"], R["bd"], R["rg_prm"], lam_f, lam_b, R["h_f"], R["h_b"])
    tmg = _tile(S, 512)
    dxc = _fused_mm("rg_dxc", S, RG_W, 4 * RG_W, tmg, RG_W, 4 * RG_W, [(dpre, "mk"), (R["bd"], "nk")], [(0, 1, 0)],
                    [(dxc_direct, (tmg, RG_W), _mn)], [(jax.ShapeDtypeStruct((S, RG_W), F32), (tmg, RG_W), _mn)],
                    lambda i, accs, ex, out: out[0].__setitem__(Ellipsis, ex[0][...] + accs[0][...]))[0]
    d_bd = _mm_plain("rg_dbd", RG_W, 4 * RG_W, S, RG_W, 4 * RG_W, tk_s, R["xc"], "km", dpre, "kn", F32)
    dx_rg, G["rg_conv_w"], G["rg_conv_b"] = _conv_bwd("rg_conv_bwd", p, 0, W["rg_conv_w"], [dxc], "bias")
    blocks = jnp.einsum("nigmj,nm->gnij", d_bd.reshape(RG_BLOCKS, RG_BLOCK, 4, RG_BLOCKS, RG_BLOCK),
                        jnp.eye(RG_BLOCKS, dtype=F32))
    G["rg_gate_a_w"] = jnp.stack([blocks[0], blocks[2]])
    G["rg_gate_x_w"] = jnp.stack([blocks[1], blocks[3]])
    G["rg_gate_a_b"] = jnp.stack([d_rgprm[0], d_rgprm[2]])
    G["rg_gate_x_b"] = jnp.stack([d_rgprm[1], d_rgprm[3]])
    G["rg_lambda"] = d_rgprm[4:6]
    adj = _gdn_scan_bwd(R["q"], R["k"], R["bg"], R["gdn_loc"], do)
    dq, dk, dv, dbg = _gdn_local_bwd(R["q"], R["k"], R["v"], R["bg"], R["gcr"], do, R["gdn_loc"], R["gdn_fwd"], adj)
    cw = W["gdn_conv_w"]
    dpq, dwq, _ = _conv_bwd("gdn_conv_q_bwd", p, 2, cw[:, 0:512], [dq], "q")
    dpk, dwk, _ = _conv_bwd("gdn_conv_k_bwd", p, 3, cw[:, 512:1024], [dk], "k")
    dpv, dwv, _ = _conv_bwd("gdn_conv_v_bwd", p, 4, cw[:, 1024:1536], [dv], "v")
    G["gdn_conv_w"] = jnp.concatenate([dwq, dwk, dwv], axis=1)
    dba, d_gprm = _gdn_prep_bwd(dbg, p, R["gdn_prm"])
    G["gdn_a_log"] = d_gprm[0, 8:16].reshape(2, GDN_H)
    G["gdn_dt_bias"] = d_gprm[1, 8:16].reshape(2, GDN_H)
    dp = jnp.concatenate([dx_rg, dgate, dpq, dpk, dpv, dz, dba], axis=1)
    tok = emit(w_in=_mm_plain("dw_in", D_MODEL, D_IN_PAD, S, D_MODEL, 640, tk_s, R["h2"], "km", dp, "kn", BF16))
    g_mix = W["mix_norm"] if tok is None else W["mix_norm"] + tok

    def epi_dx1(i, accs, ex, out):
        dx, dgt = _rmsnorm_bwd_tile(accs[0][...], ex[0][...], ex[1][...])
        out[0][...] = ex[2][...] + dx
        _colsum_into(out[1], i, jnp.sum(dgt, axis=0, keepdims=True))

    dx1, G["mix_norm"] = _fused_mm(
        "mix_dx", S, D_MODEL, D_IN_PAD, tm, D_MODEL, 640, [(dp, "mk"), (W["w_in"], "nk")], [(0, 1, 0)],
        [(R["x1"], (tm, D_MODEL), _mn), (g_mix, (1, D_MODEL), _row0), (dx2, (tm, D_MODEL), _mn)],
        [(jax.ShapeDtypeStruct((S, D_MODEL), F32), (tm, D_MODEL), _mn),
         (jax.ShapeDtypeStruct((1, D_MODEL), F32), (1, D_MODEL), _row0)], epi_dx1)
    dx0, G["ffn1_norm"] = _ffn_bwd("ffn1b", dx1, x0, W["ffn1_norm"], R["h1"], R["a1"], R["b1"], R["f1"],
                                   W["ffn1_w_gate"], W["ffn1_w_up"], W["ffn1_w_down"], ffn_emit("ffn1"))
    G["final_norm"] = R["d_final_norm"]
    return dx0, G


def _mesh_pos():
    x, y, c = lax.axis_index("x"), lax.axis_index("y"), lax.axis_index("c")
    return x, y, c, 4 * x + 2 * y + c


def _peer(x, y, c, r):
    px = 1 - x if r & 4 else x
    py = 1 - y if r & 2 else y
    pc = 1 - c if r & 1 else c
    return (px, py, pc), 4 * px + 2 * py + pc


_HBM = pl.BlockSpec(memory_space=pltpu.HBM)
_SEM = pl.BlockSpec(memory_space=pltpu.SEMAPHORE)


def _slab_copies(srcs, lands, send_sems, recv_sems):
    x, y, c, _ = _mesh_pos()
    copies = []
    for a, (src, land) in enumerate(zip(srcs, lands)):
        for r in range(1, N_DEV):
            peer, peer_idx = _peer(x, y, c, r)
            copies.append(pltpu.make_async_remote_copy(
                src_ref=src.at[peer_idx], dst_ref=land.at[r - 1], send_sem=send_sems.at[a * 7 + r - 1],
                recv_sem=recv_sems.at[a * 7 + r - 1], device_id=peer, device_id_type=pl.DeviceIdType.MESH))
    return copies


def _scatter_start(name, slabs):
    n = len(slabs)

    def body(*refs):
        srcs, lands = refs[0:n], refs[n:2 * n]
        send_sems, recv_sems = refs[2 * n], refs[2 * n + 1]
        token = refs[4 * n + 2]
        for cp in _slab_copies(srcs, lands, send_sems, recv_sems):
            cp.start()
        token[...] = jnp.zeros_like(token)

    land_shapes = [(N_DEV - 1,) + s.shape[1:] for s in slabs]
    out_shape = ([pltpu.SemaphoreType.DMA((7 * n,)), pltpu.SemaphoreType.DMA((7 * n,))]
                 + [pltpu.HBM(s.shape, s.dtype) for s in slabs]
                 + [pltpu.HBM(shp, s.dtype) for shp, s in zip(land_shapes, slabs)]
                 + [jax.ShapeDtypeStruct((8, 128), F32)])
    res = pl.pallas_call(
        body, name=name, out_shape=out_shape, in_specs=[_HBM] * (2 * n),
        out_specs=[_SEM, _SEM] + [_HBM] * (2 * n) + [pl.BlockSpec(memory_space=pltpu.VMEM)],
        input_output_aliases={i: 2 + i for i in range(2 * n)},
        compiler_params=pltpu.CompilerParams(has_side_effects=pltpu.SideEffectType.DATAFLOW_SIDE_EFFECTING),
    )(*[pltpu.with_memory_space_constraint(s, pltpu.HBM) for s in slabs],
      *[pltpu.with_memory_space_constraint(lax.empty(shp, s.dtype), pltpu.HBM) for shp, s in zip(land_shapes, slabs)])
    return dict(n=n, sems=res[0:2], srcs=res[2:2 + n], lands=res[2 + n:2 + 2 * n], token=res[2 + 2 * n])


def _scatter_wait(name, started, after):
    n = started["n"]

    def body(*refs):
        srcs, lands = refs[0:n], refs[n:2 * n]
        send_sems, recv_sems = refs[2 * n], refs[2 * n + 1]
        for cp in _slab_copies(srcs, lands, send_sems, recv_sems):
            cp.wait_send()
            cp.wait_recv()

    arrays = list(started["srcs"]) + list(started["lands"])
    res = pl.pallas_call(
        body, name=name, out_shape=[pltpu.HBM(a.shape, a.dtype) for a in arrays],
        in_specs=[_HBM] * (2 * n) + [_SEM, _SEM, pl.BlockSpec(memory_space=pl.ANY)], out_specs=[_HBM] * (2 * n),
        input_output_aliases={i: i for i in range(2 * n)},
        compiler_params=pltpu.CompilerParams(has_side_effects=pltpu.SideEffectType.DATAFLOW_SIDE_EFFECTING),
    )(*arrays, *started["sems"], after)
    return res[0:n], res[n:2 * n]


def _all_gather(name, arrays):
    n = len(arrays)

    def body(*refs):
        ins = refs[:n]
        outs = refs[n:2 * n]
        send_sems, recv_sems, local_sems = refs[2 * n:]
        x, y, c, me = _mesh_pos()
        sibling = (x, y, 1 - c)
        chips = [(1 - x, y), (x, 1 - y), (1 - x, 1 - y)]

        def idx(px, py, pc):
            return 4 * px + 2 * py + pc

        def copy(a, k, block, to, src=None):
            slot = outs[a].at[idx(*block)]
            return pltpu.make_async_remote_copy(
                src_ref=slot if src is None else src, dst_ref=slot, send_sem=send_sems.at[a * 7 + k],
                recv_sem=recv_sems.at[a * 7 + k], device_id=to, device_id_type=pl.DeviceIdType.MESH)

        locals_, sends = [], []
        for a in range(n):
            loc = pltpu.make_async_copy(ins[a], outs[a].at[me], local_sems.at[a])
            loc.start()
            locals_.append(loc)
            sends.append(copy(a, 0, (x, y, c), sibling, src=ins[a]))
            sends += [copy(a, 1 + j, (x, y, c), (*chip, c), src=ins[a]) for j, chip in enumerate(chips)]
        for cp in sends:
            cp.start()
        passed = []
        for a in range(n):
            for j, chip in enumerate(chips):
                copy(a, 1 + j, (*chip, c), (x, y, c)).wait_recv()
                fwd = copy(a, 4 + j, (*chip, c), sibling)
                fwd.start()
                passed.append(fwd)
        for a in range(n):
            copy(a, 0, sibling, (x, y, c)).wait_recv()
            for j, chip in enumerate(chips):
                copy(a, 4 + j, (*chip, 1 - c), (x, y, c)).wait_recv()
        for cp in sends + passed:
            cp.wait_send()
        for loc in locals_:
            loc.wait()

    any_spec = pl.BlockSpec(memory_space=pl.ANY)
    return pl.pallas_call(
        body, name=name, in_specs=[any_spec] * n, out_specs=[any_spec] * n,
        out_shape=[jax.ShapeDtypeStruct((N_DEV,) + a.shape, a.dtype) for a in arrays],
        scratch_shapes=[pltpu.SemaphoreType.DMA((7 * n,)), pltpu.SemaphoreType.DMA((7 * n,)),
                        pltpu.SemaphoreType.DMA((n,))],
        compiler_params=pltpu.CompilerParams(has_side_effects=True),
    )(*arrays)


def _adamw_math(w, g, m, v):
    m2 = ADAM_B1 * m + (1.0 - ADAM_B1) * g
    v2 = ADAM_B2 * v + (1.0 - ADAM_B2) * (g * g)
    m_hat = m2 / (1.0 - ADAM_B1 ** ADAM_STEP)
    v_hat = v2 / (1.0 - ADAM_B2 ** ADAM_STEP)
    delta = -ADAM_LR * (m_hat / (jnp.sqrt(v_hat) + ADAM_EPS) + ADAM_WD * w)
    return delta, m2, v2


def _adamw_slabs(name, src, land, me, w, m, v, tr):
    R, C = w.shape

    def body(me_ref, own_ref, land_ref, w_ref, m_ref, v_ref, g_ref, d_ref, m2_ref, v2_ref):
        g = own_ref[0].astype(F32)
        for s in range(N_DEV - 1):
            g = g + land_ref[s].astype(F32)
        delta, m2, v2 = _adamw_math(w_ref[...], g, m_ref[...], v_ref[...])
        g_ref[...] = g
        d_ref[...] = delta
        m2_ref[...] = m2
        v2_ref[...] = v2

    im = lambda i, me_ref: (i, 0)
    grid_spec = pltpu.PrefetchScalarGridSpec(
        num_scalar_prefetch=1, grid=(R // tr,),
        in_specs=[pl.BlockSpec((1, tr, C), lambda i, me_ref: (me_ref[0], i, 0)),
                  pl.BlockSpec((N_DEV - 1, tr, C), lambda i, me_ref: (0, i, 0)),
                  pl.BlockSpec((tr, C), im), pl.BlockSpec((tr, C), im), pl.BlockSpec((tr, C), im)],
        out_specs=[pl.BlockSpec((tr, C), im)] * 4)
    return pl.pallas_call(body, name=name, grid_spec=grid_spec, out_shape=[jax.ShapeDtypeStruct((R, C), F32)] * 4,
                          compiler_params=_cp(1))(me.reshape(1).astype(jnp.int32), src, land, w, m, v)


def _sum_slots(name, slots):
    _, R, C = slots.shape

    def body(s_ref, o_ref):
        g = s_ref[0]
        for s in range(1, N_DEV):
            g = g + s_ref[s]
        o_ref[...] = g

    return _rows(name, R, R, [(slots, (N_DEV, R, C), lambda i: (0, 0, 0))],
                 [(jax.ShapeDtypeStruct((R, C), F32), (R, C), lambda i: (0, 0))], body)[0]


def _adamw_packed(name, g, w, m, v):
    R, C = g.shape

    def body(g_ref, w_ref, m_ref, v_ref, d_ref, m2_ref, v2_ref):
        delta, m2, v2 = _adamw_math(w_ref[...], g_ref[...], m_ref[...], v_ref[...])
        d_ref[...] = delta
        m2_ref[...] = m2
        v2_ref[...] = v2

    im = lambda i: (0, 0)
    sds = jax.ShapeDtypeStruct((R, C), F32)
    return _rows(name, R, R, [(a, (R, C), im) for a in (g, w, m, v)], [(sds, (R, C), im)] * 3, body)


def _pack(arrays):
    rows = []
    for a in arrays:
        flat = a.reshape(-1).astype(F32)
        pad = (-flat.shape[0]) % 128
        rows.append(jnp.pad(flat, (0, pad)).reshape(-1, 128))
    out = jnp.concatenate(rows, axis=0)
    return jnp.pad(out, ((0, (-out.shape[0]) % 8), (0, 0)))


def _unpack(packed, shapes):
    lead = packed.shape[:-2]
    outs = []
    r = 0
    for shp in shapes:
        n = math.prod(shp)
        nr = -(-n // 128)
        flat = packed[..., r:r + nr, :].reshape(lead + (nr * 128,))[..., :n]
        outs.append(flat.reshape(lead + tuple(shp)))
        r += nr
    return outs


BIG = ["ffn1_w_gate", "ffn1_w_up", "ffn1_w_down", "w_in", "w_out", "ffn2_w_gate", "ffn2_w_up", "ffn2_w_down"]
COL_SHARDED = {"ffn1_w_gate", "ffn1_w_up", "w_in", "ffn2_w_gate", "ffn2_w_up"}
SMALL_SHARDED = ["rg_conv_w", "rg_gate_a_b", "rg_gate_x_b", "rg_lambda", "gdn_conv_w"]
WEIGHTS = ["ffn1_norm", "ffn1_w_gate", "ffn1_w_up", "ffn1_w_down", "mix_norm", "w_in", "w_out", "rg_conv_w", "rg_conv_b",
           "rg_gate_a_w", "rg_gate_a_b", "rg_gate_x_w", "rg_gate_x_b", "rg_lambda", "gdn_conv_w", "gdn_a_log",
           "gdn_dt_bias", "gdn_norm", "ffn2_norm", "ffn2_w_gate", "ffn2_w_up", "ffn2_w_down", "final_norm"]
SMALL = [n for n in WEIGHTS if n not in BIG]
ROW_VECTORS = {"ffn1_norm", "mix_norm", "ffn2_norm", "gdn_norm", "rg_conv_b", "final_norm"}
ROW_TILE = {"ffn1_w_gate": 256, "ffn1_w_up": 256, "ffn1_w_down": 176, "w_in": 256, "w_out": 64,
            "ffn2_w_gate": 256, "ffn2_w_up": 256, "ffn2_w_down": 176}


def _unshard_cols(g):
    return g.transpose(1, 0, 2).reshape(g.shape[1], N_DEV * g.shape[2])


def _to_slabs(name, g):
    if name in COL_SHARDED:
        r, ctot = g.shape
        return g.reshape(r, N_DEV, ctot // N_DEV).transpose(1, 0, 2)
    return g.reshape(N_DEV, g.shape[0] // N_DEV, g.shape[1])


def _step(x, target, w, m, v):
    _, _, _, me = _mesh_pos()
    small_shards = [w[n] for n in SMALL_SHARDED]
    gathered = _all_gather("gather_weights", [w[n].astype(BF16) for n in BIG] + [_pack(small_shards)])
    W = {}
    for n, gth in zip(BIG, gathered[:len(BIG)]):
        W[n] = _unshard_cols(gth) if n in COL_SHARDED else gth.reshape(-1, gth.shape[-1])
    W["w_in"] = jnp.pad(W["w_in"], ((0, 0), (0, D_IN_PAD - D_IN)))
    for n, gth in zip(SMALL_SHARDED, _unpack(gathered[-1], [s.shape for s in small_shards])):
        W[n] = jnp.moveaxis(gth, 0, -2).reshape(gth.shape[1:-1] + (N_DEV * gth.shape[-1],))
    for n in SMALL:
        if n not in SMALL_SHARDED:
            W[n] = w[n]
    R = _layer_fwd(x, target, W)
    pending = []

    def emit_big(**named):
        slabs = [_to_slabs(n, g[:, :D_IN] if n == "w_in" else g) for n, g in named.items()]
        started = _scatter_start(f"scatter_start_{len(pending)}", slabs)
        pending.append((list(named), started))
        return started["token"][0, 0]

    grad_x, G = _layer_bwd(x, W, R, emit_big)
    loss = lax.psum(R["loss"][0, 0], ("x", "y", "c"))
    out = {}
    for i, (names, started) in enumerate(pending):
        srcs, lands = _scatter_wait(f"scatter_wait_{i}", started, grad_x)
        for n, src, land in zip(names, srcs, lands):
            out[n] = _adamw_slabs(f"adamw_{n}", src, land, me, w[n], m[n], v[n], ROW_TILE[n])
    full_shapes = [G[n].shape for n in SMALL]
    slots = _all_gather("gather_small_grads", [_pack([G[n] for n in SMALL])])[0]
    reduced = dict(zip(SMALL, _unpack(_sum_slots("sum_small_grads", slots), full_shapes)))
    g_small = []
    for n in SMALL:
        g = reduced[n]
        if n in SMALL_SHARDED:
            per = g.shape[-1] // N_DEV
            g = lax.dynamic_slice_in_dim(g, me * per, per, axis=g.ndim - 1)
        g_small.append(g.reshape(w[n].shape))
    shapes = [w[n].shape for n in SMALL]
    d_p, m_p, v_p = _adamw_packed("adamw_small", _pack(g_small), _pack([w[n] for n in SMALL]),
                                  _pack([m[n] for n in SMALL]), _pack([v[n] for n in SMALL]))
    for n, g, d_, m_, v_ in zip(SMALL, g_small, _unpack(d_p, shapes), _unpack(m_p, shapes), _unpack(v_p, shapes)):
        out[n] = (g, d_, m_, v_)
    return loss, grad_x, out


def kernel(x, ffn1_norm, ffn1_w_gate, ffn1_w_up, ffn1_w_down, mix_norm, w_in, w_out, rg_conv_w, rg_conv_b, rg_gate_a_w, rg_gate_a_b, rg_gate_x_w, rg_gate_x_b, rg_lambda, gdn_conv_w, gdn_a_log, gdn_dt_bias, gdn_norm, ffn2_norm, ffn2_w_gate, ffn2_w_up, ffn2_w_down, final_norm, loss_target, m_ffn1_norm, m_ffn1_w_gate, m_ffn1_w_up, m_ffn1_w_down, m_mix_norm, m_w_in, m_w_out, m_rg_conv_w, m_rg_conv_b, m_rg_gate_a_w, m_rg_gate_a_b, m_rg_gate_x_w, m_rg_gate_x_b, m_rg_lambda, m_gdn_conv_w, m_gdn_a_log, m_gdn_dt_bias, m_gdn_norm, m_ffn2_norm, m_ffn2_w_gate, m_ffn2_w_up, m_ffn2_w_down, m_final_norm, v_ffn1_norm, v_ffn1_w_gate, v_ffn1_w_up, v_ffn1_w_down, v_mix_norm, v_w_in, v_w_out, v_rg_conv_w, v_rg_conv_b, v_rg_gate_a_w, v_rg_gate_a_b, v_rg_gate_x_w, v_rg_gate_x_b, v_rg_lambda, v_gdn_conv_w, v_gdn_a_log, v_gdn_dt_bias, v_gdn_norm, v_ffn2_norm, v_ffn2_w_gate, v_ffn2_w_up, v_ffn2_w_down, v_final_norm):
    args = dict(locals())
    orig_shapes = {n: args[n].shape for n in WEIGHTS}

    def local(prefix):
        d = {}
        for n in WEIGHTS:
            a = args[prefix + n]
            d[n] = a.reshape(1, -1) if n in ROW_VECTORS else a[0]
        return d

    loss, grad_x, out = _step(x[0], loss_target[0], local(""), local("m_"), local("v_"))
    res = [loss, grad_x[None]]
    for k in range(4):
        res += [out[n][k].reshape(orig_shapes[n]) for n in WEIGHTS]
    return tuple(res)
```

```python
import functools
import math

import jax
import jax.numpy as jnp
from jax import lax
from jax.experimental import pallas as pl
from jax.experimental.pallas import tpu as pltpu

F32, BF16 = jnp.float32, jnp.bfloat16

D_MODEL = 1024
D_FF = 2816
RG_W = 512
RG_BLOCKS = 8
RG_BLOCK = 64
RG_C = 8.0
CONV_W = 4
GDN_H = 4
GDN_DK = 128
CHUNK = 64
EPS = 1e-6
D_IN = 3088
D_IN_PAD = 3200
COL_BA = 3072
N_DEV = 8
HALO = 8
VMEM_LIMIT = 48 * 1024 * 1024

ADAM_LR = 0.001
ADAM_B1 = 0.9
ADAM_B2 = 0.999
ADAM_EPS = 1e-08
ADAM_WD = 0.01
ADAM_STEP = 10

HI = lax.Precision.HIGHEST


def _cp(n):
    return pltpu.CompilerParams(dimension_semantics=("arbitrary",) * n, vmem_limit_bytes=VMEM_LIMIT)


def _tile(n, pref):
    return min(n, pref)


def _sigmoid(x):
    return jax.nn.sigmoid(x)


def _softplus(x):
    return jnp.maximum(x, 0.0) + jnp.log(1.0 + jnp.exp(-jnp.abs(x)))


def _dot(a, b, ca, cb, prec=None):
    return lax.dot_general(a, b, (((ca,), (cb,)), ((), ())), preferred_element_type=F32, precision=prec)


def _fused_mm(name, M, N, K, tm, tn, tk, ops, pairs, extras, outs, epilogue):
    nm, nn, nk = M // tm, N // tn, K // tk
    assert nm * tm == M and nn * tn == N and nk * tk == K, (name, M, N, K, tm, tn, tk)
    spec_of = {
        "mk": pl.BlockSpec((tm, tk), lambda i, j, k: (i, k)),
        "km": pl.BlockSpec((tk, tm), lambda i, j, k: (k, i)),
        "kn": pl.BlockSpec((tk, tn), lambda i, j, k: (k, j)),
        "nk": pl.BlockSpec((tn, tk), lambda i, j, k: (j, k)),
    }
    in_specs = [spec_of[m] for _, m in ops]
    in_specs += [pl.BlockSpec(bs, lambda i, j, k, im=im: im(i, j)) for _, bs, im in extras]
    out_specs = [pl.BlockSpec(bs, lambda i, j, k, im=im: im(i, j)) for _, bs, im in outs]
    n_ops, n_ex, n_out = len(ops), len(extras), len(outs)
    n_acc = 1 + max(g for _, _, g in pairs)
    modes = [m for _, m in ops]

    def body(*refs):
        op_refs = refs[:n_ops]
        ex_refs = refs[n_ops:n_ops + n_ex]
        out_refs = refs[n_ops + n_ex:n_ops + n_ex + n_out]
        accs = refs[n_ops + n_ex + n_out:]
        i = pl.program_id(0)
        k = pl.program_id(2)

        @pl.when(k == 0)
        def _():
            for a in accs:
                a[...] = jnp.zeros_like(a)

        vals = [r[...].astype(BF16) for r in op_refs]
        for ia, ib, g in pairs:
            ca = 1 if modes[ia] == "mk" else 0
            cb = 0 if modes[ib] == "kn" else 1
            accs[g][...] += _dot(vals[ia], vals[ib], ca, cb)

        @pl.when(k == nk - 1)
        def _():
            epilogue(i, accs, ex_refs, out_refs)

    res = pl.pallas_call(
        body, name=name, grid=(nm, nn, nk), in_specs=in_specs, out_specs=out_specs,
        out_shape=[o for o, _, _ in outs],
        scratch_shapes=[pltpu.VMEM((tm, tn), F32)] * n_acc,
        compiler_params=_cp(3),
    )(*[a for a, _ in ops], *[a for a, _, _ in extras])
    return res


def _mn(i, j):
    return (i, j)


def _row0(i, j):
    return (0, 0)


def _rows(name, S, ts, ins, outs, body, scratch=()):
    return pl.pallas_call(
        body, name=name, grid=(S // ts,),
        in_specs=[pl.BlockSpec(bs, im) for _, bs, im in ins],
        out_specs=[pl.BlockSpec(bs, im) for _, bs, im in outs],
        out_shape=[o for o, _, _ in outs],
        scratch_shapes=list(scratch),
        compiler_params=_cp(1),
    )(*[a for a, _, _ in ins])


def _halo_ins(arr, S, ts, width, colblk):
    per = ts // HALO
    last = S // HALO - 1
    return [
        (arr, (ts, width), lambda i: (i, colblk)),
        (arr, (HALO, width), lambda i: (jnp.maximum(i * per - 1, 0), colblk)),
        (arr, (HALO, width), lambda i: (jnp.minimum((i + 1) * per, last), colblk)),
    ]


def _ext(main_ref, prev_ref, next_ref, i, n_tiles):
    prev = jnp.where(i > 0, prev_ref[...].astype(F32), 0.0)
    nxt = jnp.where(i < n_tiles - 1, next_ref[...].astype(F32), 0.0)
    return jnp.concatenate([prev, main_ref[...].astype(F32), nxt], axis=0)


def _shift(ext, off, ts):
    n = ext.shape[0]
    if off == 0:
        return ext[HALO:HALO + ts]
    return pltpu.roll(ext, (-off) % n, 0)[HALO:HALO + ts]


def _rmsnorm_fwd(name, x, g):
    S, D = x.shape
    ts = _tile(S, 512)

    def body(x_ref, g_ref, o_ref):
        xv = x_ref[...]
        r = lax.rsqrt(jnp.mean(xv * xv, axis=-1, keepdims=True) + EPS)
        o_ref[...] = (xv * r * g_ref[...]).astype(BF16)

    return _rows(name, S, ts,
                 [(x, (ts, D), lambda i: (i, 0)), (g, (1, D), lambda i: (0, 0))],
                 [(jax.ShapeDtypeStruct((S, D), BF16), (ts, D), lambda i: (i, 0))], body)[0]


def _rmsnorm_bwd_tile(dh, x, g):
    r = lax.rsqrt(jnp.mean(x * x, axis=-1, keepdims=True) + EPS)
    xhat = x * r
    dxn = dh * g
    dx = r * (dxn - xhat * jnp.mean(dxn * xhat, axis=-1, keepdims=True))
    return dx, dh * xhat


def _ffn_fwd(tag, x, h, wg, wu, wd):
    S = x.shape[0]
    tm = _tile(S, 512)
    tn = 1408

    def epi_up(i, accs, ex, out):
        a = accs[0][...]
        b = accs[1][...]
        out[0][...] = a.astype(BF16)
        out[1][...] = b.astype(BF16)
        out[2][...] = (a * _sigmoid(a) * b).astype(BF16)

    sds = jax.ShapeDtypeStruct((S, D_FF), BF16)
    a, b, f = _fused_mm(f"{tag}_up", S, D_FF, D_MODEL, tm, tn, D_MODEL,
                        [(h, "mk"), (wg, "kn"), (wu, "kn")], [(0, 1, 0), (0, 2, 1)], [],
                        [(sds, (tm, tn), _mn)] * 3, epi_up)

    def epi_down(i, accs, ex, out):
        out[0][...] = ex[0][...] + 0.5 * accs[0][...]

    xo = _fused_mm(f"{tag}_down", S, D_MODEL, D_FF, tm, D_MODEL, 1408,
                   [(f, "mk"), (wd, "kn")], [(0, 1, 0)], [(x, (tm, D_MODEL), _mn)],
                   [(jax.ShapeDtypeStruct((S, D_MODEL), F32), (tm, D_MODEL), _mn)], epi_down)[0]
    return xo, a, b, f


def _conv_taps(ext, w_ref, ts):
    acc = None
    for j in range(CONV_W):
        term = w_ref[j:j + 1, :] * _shift(ext, j - 2, ts)
        acc = term if acc is None else acc + term
    return acc


def _l2norm_heads(s, scale):
    outs = []
    for h in range(GDN_H):
        sh = s[:, h * GDN_DK:(h + 1) * GDN_DK]
        outs.append(sh * (lax.rsqrt(jnp.sum(sh * sh, axis=-1, keepdims=True) + EPS) * scale))
    return jnp.concatenate(outs, axis=-1)


def _conv_fwd(name, p, colblk, w, bias, mode):
    S = p.shape[0]
    ts = _tile(S, 512)
    n_tiles = S // ts
    C = w.shape[1]

    def body(main, prev, nxt, w_ref, b_ref, o_ref):
        i = pl.program_id(0)
        c = _conv_taps(_ext(main, prev, nxt, i, n_tiles), w_ref, ts)
        if mode == "bias":
            o_ref[...] = c + b_ref[...]
        else:
            s = c * _sigmoid(c)
            if mode == "q":
                s = _l2norm_heads(s, GDN_DK ** -0.5)
            elif mode == "k":
                s = _l2norm_heads(s, 1.0)
            o_ref[...] = s

    ins = _halo_ins(p, S, ts, C, colblk) + [(w, (CONV_W, C), lambda i: (0, 0)), (bias, (1, C), lambda i: (0, 0))]
    return _rows(name, S, ts, ins, [(jax.ShapeDtypeStruct((S, C), F32), (ts, C), lambda i: (i, 0))], body)[0]


def _rg_gate_terms(pre, xc, prm_ref, d):
    r = _sigmoid(pre[:, d * 1024:d * 1024 + RG_W] + prm_ref[2 * d:2 * d + 1, :])
    ig = _sigmoid(pre[:, d * 1024 + RG_W:(d + 1) * 1024] + prm_ref[2 * d + 1:2 * d + 2, :])
    sp = _softplus(-prm_ref[4 + d:5 + d, :])
    log_a = -RG_C * r * sp
    a = jnp.exp(log_a)
    t = jnp.tanh(log_a)
    sq = jnp.sqrt(-2.0 * t / (1.0 - t))
    return r, ig, sp, a, sq


def _rg_gates_fwd(xc, bd, prm):
    S = xc.shape[0]
    tm = _tile(S, 256)

    def epi(i, accs, ex, out):
        pre = accs[0][...]
        xv = ex[0][...]
        for d in range(2):
            r, ig, sp, a, sq = _rg_gate_terms(pre, xv, ex[1], d)
            out[2 * d][...] = a
            out[2 * d + 1][...] = sq * ig * xv

    sds = jax.ShapeDtypeStruct((S, RG_W), F32)
    blk = (tm, RG_W)
    im = lambda i, j: (i, 0)
    return _fused_mm("rg_gates_fwd", S, 4 * RG_W, RG_W, tm, 4 * RG_W, RG_W,
                     [(xc, "mk"), (bd, "kn")], [(0, 1, 0)],
                     [(xc, blk, im), (prm, (8, RG_W), _row0)], [(sds, blk, im)] * 4, epi)


def _rg_scan(name, a_f, b_f, a_b, b_b):
    S, C = a_f.shape
    ts = _tile(S, 512)
    n_tiles = S // ts

    def body(af, bf, ab, bb, hf, hb, carry):
        @pl.when(pl.program_id(0) == 0)
        def _():
            carry[...] = jnp.zeros_like(carry)

        def step(t, c):
            cf, cb = c
            cf = af[pl.ds(t, 1), :] * cf + bf[pl.ds(t, 1), :]
            hf[pl.ds(t, 1), :] = cf
            tb = ts - 1 - t
            cb = ab[pl.ds(tb, 1), :] * cb + bb[pl.ds(tb, 1), :]
            hb[pl.ds(tb, 1), :] = cb
            return cf, cb

        cf, cb = lax.fori_loop(0, ts, step, (carry[0:1, :], carry[1:2, :]), unroll=8)
        carry[0:1, :] = cf
        carry[1:2, :] = cb

    fw = lambda i: (i, 0)
    bw = lambda i: (n_tiles - 1 - i, 0)
    sds = jax.ShapeDtypeStruct((S, C), F32)
    return _rows(name, S, ts,
                 [(a_f, (ts, C), fw), (b_f, (ts, C), fw), (a_b, (ts, C), bw), (b_b, (ts, C), bw)],
                 [(sds, (ts, C), fw), (sds, (ts, C), bw)], body, scratch=[pltpu.VMEM((8, C), F32)])


def _tri_masks():
    ri = lax.broadcasted_iota(jnp.int32, (CHUNK, CHUNK), 0)
    ci = lax.broadcasted_iota(jnp.int32, (CHUNK, CHUNK), 1)
    return ri, ci


def _gdn_prep_fwd(p, prm):
    S = p.shape[0]
    ts = _tile(S, 512)

    def body(p_ref, prm_ref, o_ref):
        raw = p_ref[...]
        lane = lax.broadcasted_iota(jnp.int32, (1, 128), 1)
        g = -jnp.exp(prm_ref[0:1, :]) * _softplus(raw + prm_ref[1:2, :])
        g = jnp.where((lane >= 8) & (lane < 16), g, 0.0)
        beta = _sigmoid(raw)
        ri, ci = _tri_masks()
        lower = (ri >= ci).astype(F32)
        upper = (ri <= ci).astype(F32)
        for c in range(ts // CHUNK):
            rows = slice(c * CHUNK, (c + 1) * CHUNK)
            gch = g[rows]
            gc = jnp.where(lane < 12, _dot(lower, gch, 1, 0, HI), _dot(upper, gch, 1, 0, HI))
            o_ref[rows, :] = jnp.where(lane < 8, beta[rows], gc)

    return _rows("gdn_prep_fwd", S, ts,
                 [(p, (ts, 128), lambda i: (i, COL_BA // 128)), (prm, (8, 128), lambda i: (0, 0))],
                 [(jax.ShapeDtypeStruct((S, 128), F32), (ts, 128), lambda i: (i, 0))], body)[0]


def _bdot(a, b, ca, cb):
    return _dot(a.astype(BF16), b.astype(BF16), ca, cb)


GDN_W = GDN_H * GDN_DK
GDN_TS = 256


def _gdn_decay(bg_ref, gcr_ref, c, rows, r0, col, rev, ri, ci):
    beta = bg_ref[rows, col:col + 1]
    gc = bg_ref[rows, 8 + col:9 + col]
    last = 0 if rev else CHUNK - 1
    gl = bg_ref[pl.ds(r0 + last, 1), 8 + col:9 + col]
    out = dict(beta=beta, gc=gc, gl=gl, eg=jnp.exp(gc), egl=jnp.exp(gl - gc), cd=jnp.exp(gl))
    if gcr_ref is not None:
        incl = (ri <= ci) if rev else (ri >= ci)
        out["strict"] = (ri < ci) if rev else (ri > ci)
        out["dm"] = jnp.where(incl, jnp.exp(jnp.where(incl, gc - gcr_ref[c, col:col + 1, :], 0.0)), 0.0)
    return out


def _dir_tile(d, n_tiles, flip):
    if (d == 1) != flip:
        return lambda i: n_tiles - 1 - i
    return lambda i: i


def _gdn_local_fwd(q, k, v, bg, gcr):
    S = q.shape[0]
    ts = _tile(S, GDN_TS)
    ncb = ts // CHUNK
    nch = S // CHUNK

    def body(q_ref, k_ref, v_ref, bg_ref, gcr_ref, u0, w0, a0, t0, u1, w1, a1, t1):
        ri, ci = _tri_masks()
        eye = (ri == ci).astype(F32)
        outs = ((u0, w0, a0, t0), (u1, w1, a1, t1))

        def chunk(c, carry):
            r0 = pl.multiple_of(c * CHUNK, CHUNK)
            rows = pl.ds(r0, CHUNK)
            chains = []
            for h in range(GDN_H):
                cols = slice(h * GDN_DK, (h + 1) * GDN_DK)
                qh, kh, vh = q_ref[rows, cols], k_ref[rows, cols], v_ref[rows, cols]
                both = _bdot(jnp.concatenate([qh, kh], axis=0), kh, 1, 1)
                for d in range(2):
                    chains.append(dict(h=h, d=d, cols=cols, kh=kh, vh=vh, qk=both[0:CHUNK], kk=both[CHUNK:2 * CHUNK]))
            for ch in chains:
                m = _gdn_decay(bg_ref, gcr_ref, c, rows, r0, ch["d"] * GDN_H + ch["h"], ch["d"] == 1, ri, ci)
                ch["m"] = m
                ch["x"] = -jnp.where(m["strict"], m["beta"] * ch["kk"] * m["dm"], 0.0)
                ch["t"] = eye + ch["x"]
            for ch in chains:
                ch["pw"] = _bdot(ch["x"], ch["x"], 1, 0)
            for level in range(1, 6):
                last_level = level == 5
                for ch in chains:
                    rhs = ch["t"] if last_level else jnp.concatenate([ch["t"], ch["pw"]], axis=1)
                    ch["prod"] = _bdot(ch["pw"], rhs, 1, 0)
                for ch in chains:
                    ch["t"] = ch["t"] + ch["prod"][:, 0:CHUNK]
                    if not last_level:
                        ch["pw"] = ch["prod"][:, CHUNK:2 * CHUNK]
            for ch in chains:
                m = ch["m"]
                rhs = jnp.concatenate([ch["vh"] * m["beta"], ch["kh"] * (m["beta"] * m["eg"])], axis=1)
                ch["uw"] = _bdot(ch["t"], rhs, 1, 0)
            for ch in chains:
                u_ref, w_ref, a_ref, t_ref = outs[ch["d"]]
                u_ref[rows, ch["cols"]] = ch["uw"][:, 0:GDN_DK]
                w_ref[rows, ch["cols"]] = ch["uw"][:, GDN_DK:2 * GDN_DK].astype(BF16)
                a_ref[c, ch["h"]] = (ch["qk"] * ch["m"]["dm"]).astype(BF16)
                t_ref[c, ch["h"]] = ch["t"].astype(BF16)
            return carry

        lax.fori_loop(0, ncb, chunk, 0)

    im = lambda i: (i, 0)
    im4 = lambda i: (i, 0, 0, 0)
    ins = [(q, (ts, GDN_W), im), (k, (ts, GDN_W), im), (v, (ts, GDN_W), im), (bg, (ts, 128), im),
           (gcr, (ncb, 8, CHUNK), lambda i: (i, 0, 0))]
    per_dir = [(jax.ShapeDtypeStruct((S, GDN_W), F32), (ts, GDN_W), im),
               (jax.ShapeDtypeStruct((S, GDN_W), BF16), (ts, GDN_W), im),
               (jax.ShapeDtypeStruct((nch, GDN_H, CHUNK, CHUNK), BF16), (ncb, GDN_H, CHUNK, CHUNK), im4),
               (jax.ShapeDtypeStruct((nch, GDN_H, CHUNK, CHUNK), BF16), (ncb, GDN_H, CHUNK, CHUNK), im4)]
    res = _rows("gdn_local_fwd", S, ts, ins, per_dir * 2, body)
    return res[0:4], res[4:8]


def _gdn_scan_fwd(q, k, bg, loc):
    S = q.shape[0]
    ts = _tile(S, GDN_TS)
    n_tiles = S // ts
    ncb = ts // CHUNK
    nch = S // CHUNK

    def body(*refs):
        ins = (refs[0:6], refs[6:12])
        outs = (refs[12:15], refs[15:18])
        state = refs[18]

        @pl.when(pl.program_id(0) == 0)
        def _():
            state[...] = jnp.zeros_like(state)

        def chunk(cc, carry):
            chains = []
            for d in range(2):
                c = cc if d == 0 else ncb - 1 - cc
                r0 = pl.multiple_of(c * CHUNK, CHUNK)
                rows = pl.ds(r0, CHUNK)
                for h in range(GDN_H):
                    cols = slice(h * GDN_DK, (h + 1) * GDN_DK)
                    m = _gdn_decay(ins[d][2], None, c, rows, r0, d * GDN_H + h, d == 1, None, None)
                    chains.append(dict(d=d, h=h, c=c, rows=rows, cols=cols, m=m, st=state[d * GDN_H + h]))
            for ch in chains:
                q_ref, k_ref, bg_ref, u_ref, w_ref, a_ref = ins[ch["d"]]
                rows, cols = ch["rows"], ch["cols"]
                lhs = jnp.concatenate([w_ref[rows, cols], (q_ref[rows, cols] * ch["m"]["eg"]).astype(BF16)], axis=0)
                ch["ws_qs"] = _dot(lhs, ch["st"].astype(BF16), 1, 0)
            for ch in chains:
                q_ref, k_ref, bg_ref, u_ref, w_ref, a_ref = ins[ch["d"]]
                rows, cols = ch["rows"], ch["cols"]
                vn = u_ref[rows, cols] - ch["ws_qs"][0:CHUNK]
                vnb = vn.astype(BF16)
                ch["vn"] = vn
                ch["avn"] = _dot(a_ref[ch["c"], ch["h"]], vnb, 1, 0)
                ch["kvn"] = _bdot(k_ref[rows, cols] * ch["m"]["egl"], vnb, 0, 0)
            for ch in chains:
                o_ref, vn_ref, s_ref = outs[ch["d"]]
                rows, cols = ch["rows"], ch["cols"]
                o_ref[rows, cols] = ch["ws_qs"][CHUNK:2 * CHUNK] + ch["avn"]
                vn_ref[rows, cols] = ch["vn"]
                s_ref[ch["c"], ch["h"]] = ch["st"]
                state[ch["d"] * GDN_H + ch["h"]] = ch["st"] * ch["m"]["cd"] + ch["kvn"]
            return carry

        lax.fori_loop(0, ncb, chunk, 0)

    ins, outs = [], []
    for d in range(2):
        tix = _dir_tile(d, n_tiles, False)
        im = lambda i, tix=tix: (tix(i), 0)
        im4 = lambda i, tix=tix: (tix(i), 0, 0, 0)
        u, w, a, _ = loc[d]
        ins += [(q, (ts, GDN_W), im), (k, (ts, GDN_W), im), (bg, (ts, 128), im), (u, (ts, GDN_W), im),
                (w, (ts, GDN_W), im), (a, (ncb, GDN_H, CHUNK, CHUNK), im4)]
        outs += [(jax.ShapeDtypeStruct((S, GDN_W), F32), (ts, GDN_W), im),
                 (jax.ShapeDtypeStruct((S, GDN_W), F32), (ts, GDN_W), im),
                 (jax.ShapeDtypeStruct((nch, GDN_H, GDN_DK, GDN_DK), F32), (ncb, GDN_H, GDN_DK, GDN_DK), im4)]
    res = _rows("gdn_scan_fwd", S, ts, ins, outs, body, scratch=[pltpu.VMEM((2 * GDN_H, GDN_DK, GDN_DK), F32)])
    return res[0:3], res[3:6]


def _gelu(x):
    c = math.sqrt(2.0 / math.pi)
    t = jnp.tanh(c * (x + 0.044715 * x * x * x))
    return 0.5 * x * (1.0 + t), t


def _mix_out_fwd(h_f, h_b, o_f, o_b, p, gn):
    S = h_f.shape[0]
    ts = _tile(S, 512)

    def body(hf, hb, of, ob, gate, z, gn_ref, y_ref):
        ge, _ = _gelu(gate[...])
        y_ref[:, 0:RG_W] = ((hf[...] + hb[...]) * ge).astype(BF16)
        o = of[...] + ob[...]
        zv = z[...]
        sz = zv * _sigmoid(zv)
        for h in range(GDN_H):
            cols = slice(h * GDN_DK, (h + 1) * GDN_DK)
            oh = o[:, cols]
            n = oh * lax.rsqrt(jnp.mean(oh * oh, axis=-1, keepdims=True) + EPS) * gn_ref[...]
            y_ref[:, RG_W + h * GDN_DK:RG_W + (h + 1) * GDN_DK] = (n * sz[:, cols]).astype(BF16)

    blk = (ts, RG_W)
    im = lambda i: (i, 0)
    ins = [(h_f, blk, im), (h_b, blk, im), (o_f, blk, im), (o_b, blk, im),
           (p, blk, lambda i: (i, 1)), (p, blk, lambda i: (i, 5)), (gn, (1, GDN_DK), lambda i: (0, 0))]
    return _rows("mix_out_fwd", S, ts, ins,
                 [(jax.ShapeDtypeStruct((S, D_MODEL), BF16), (ts, D_MODEL), im)], body)[0]


def _loss_head(x, target, g):
    S, D = x.shape
    ts = _tile(S, 512)

    def body(x_ref, t_ref, g_ref, dx_ref, loss_ref, dg_ref):
        @pl.when(pl.program_id(0) == 0)
        def _():
            loss_ref[...] = jnp.zeros_like(loss_ref)
            dg_ref[...] = jnp.zeros_like(dg_ref)

        xv = x_ref[...]
        gv = g_ref[...]
        r = lax.rsqrt(jnp.mean(xv * xv, axis=-1, keepdims=True) + EPS)
        err = xv * r * gv - t_ref[...]
        loss_ref[...] += jnp.sum(err * err) * (0.5 / D)
        dx, dgt = _rmsnorm_bwd_tile(err * (1.0 / D), xv, gv)
        dx_ref[...] = dx
        dg_ref[...] += jnp.sum(dgt, axis=0, keepdims=True)

    im = lambda i: (i, 0)
    z = lambda i: (0, 0)
    return _rows("loss_head", S, ts,
                 [(x, (ts, D), im), (target, (ts, D), im), (g, (1, D), z)],
                 [(jax.ShapeDtypeStruct((S, D), F32), (ts, D), im),
                  (jax.ShapeDtypeStruct((8, 128), F32), (8, 128), z),
                  (jax.ShapeDtypeStruct((1, D), F32), (1, D), z)], body)


def _block_diag(w):
    n = w.shape[0]
    return jnp.einsum("nij,nm->nimj", w, jnp.eye(n, dtype=w.dtype)).reshape(n * w.shape[1], n * w.shape[2])


def _rg_bd(a_w, x_w):
    return jnp.concatenate([_block_diag(a_w[0]), _block_diag(x_w[0]), _block_diag(a_w[1]), _block_diag(x_w[1])],
                           axis=1).astype(BF16)


def _rg_prm(ba, bx, lam):
    return jnp.concatenate([ba[0:1], bx[0:1], ba[1:2], bx[1:2], lam, jnp.zeros((2, RG_W), F32)], axis=0)


def _gdn_prm(a_log, dt_bias):
    rows = jnp.zeros((8, 128), F32)
    rows = rows.at[0, 8:16].set(a_log.reshape(-1))
    return rows.at[1, 8:16].set(dt_bias.reshape(-1))


def _gc_rows(bg):
    S = bg.shape[0]
    return bg[:, 8:16].reshape(S // CHUNK, CHUNK, 8).transpose(0, 2, 1)


def _layer_fwd(x0, target, W, more=None):
    S = x0.shape[0]
    R = {}
    R["h1"] = _rmsnorm_fwd("rms1", x0, W["ffn1_norm"])
    R["x1"], R["a1"], R["b1"], R["f1"] = _ffn_fwd("ffn1", x0, R["h1"], W["ffn1_w_gate"], W["ffn1_w_up"], W["ffn1_w_down"])
    if more is not None:
        W = {**W, **more("mixer", R["x1"])}
    R["h2"] = _rmsnorm_fwd("rms2", R["x1"], W["mix_norm"])
    tm = _tile(S, 512)
    R["p"] = _fused_mm("in_proj", S, D_IN_PAD, D_MODEL, tm, 640, D_MODEL, [(R["h2"], "mk"), (W["w_in"], "kn")],
                       [(0, 1, 0)], [], [(jax.ShapeDtypeStruct((S, D_IN_PAD), F32), (tm, 640), _mn)],
                       lambda i, accs, ex, out: out[0].__setitem__(Ellipsis, accs[0][...]))[0]
    p = R["p"]
    R["xc"] = _conv_fwd("rg_conv_fwd", p, 0, W["rg_conv_w"], W["rg_conv_b"], "bias")
    R["bd"] = _rg_bd(W["rg_gate_a_w"], W["rg_gate_x_w"])
    R["rg_prm"] = _rg_prm(W["rg_gate_a_b"], W["rg_gate_x_b"], W["rg_lambda"])
    a_f, b_f, a_b, b_b = _rg_gates_fwd(R["xc"], R["bd"], R["rg_prm"])
    R["a_f"], R["a_b"] = a_f, a_b
    R["h_f"], R["h_b"] = _rg_scan("rg_scan_fwd", a_f, b_f, a_b, b_b)
    zero_b = jnp.zeros((1, RG_W), F32)
    cw = W["gdn_conv_w"]
    R["q"] = _conv_fwd("gdn_conv_q", p, 2, cw[:, 0:512], zero_b, "q")
    R["k"] = _conv_fwd("gdn_conv_k", p, 3, cw[:, 512:1024], zero_b, "k")
    R["v"] = _conv_fwd("gdn_conv_v", p, 4, cw[:, 1024:1536], zero_b, "v")
    R["gdn_prm"] = _gdn_prm(W["gdn_a_log"], W["gdn_dt_bias"])
    R["bg"] = _gdn_prep_fwd(p, R["gdn_prm"])
    R["gcr"] = _gc_rows(R["bg"])
    R["gdn_loc"] = _gdn_local_fwd(R["q"], R["k"], R["v"], R["bg"], R["gcr"])
    R["gdn_fwd"] = _gdn_scan_fwd(R["q"], R["k"], R["bg"], R["gdn_loc"])
    R["o_f"], R["o_b"] = R["gdn_fwd"][0][0], R["gdn_fwd"][1][0]
    R["y"] = _mix_out_fwd(R["h_f"], R["h_b"], R["o_f"], R["o_b"], p, W["gdn_norm"])
    R["x2"] = _fused_mm("out_proj", S, D_MODEL, D_MODEL, tm, D_MODEL, D_MODEL, [(R["y"], "mk"), (W["w_out"], "kn")],
                        [(0, 1, 0)], [(R["x1"], (tm, D_MODEL), _mn)],
                        [(jax.ShapeDtypeStruct((S, D_MODEL), F32), (tm, D_MODEL), _mn)],
                        lambda i, accs, ex, out: out[0].__setitem__(Ellipsis, ex[0][...] + accs[0][...]))[0]
    if more is not None:
        W = {**W, **more("ffn2", R["x2"])}
    R["h3"] = _rmsnorm_fwd("rms3", R["x2"], W["ffn2_norm"])
    R["x3"], R["a2"], R["b2"], R["f2"] = _ffn_fwd("ffn2", R["x2"], R["h3"], W["ffn2_w_gate"], W["ffn2_w_up"], W["ffn2_w_down"])
    R["dx3"], R["loss"], R["d_final_norm"] = _loss_head(R["x3"], target, W["final_norm"])
    R["W"] = W
    return R


def _colsum_into(ref, i, val):
    @pl.when(i == 0)
    def _():
        ref[...] = val

    @pl.when(i > 0)
    def _():
        ref[...] += val


def _ffn_bwd(tag, dout, x, g, h, a, b, f, wg, wu, wd, emit):
    S = x.shape[0]
    tm = _tile(S, 512)
    tk_s = _tile(S, 512)
    dwd = _fused_mm(f"{tag}_dw_down", D_FF, D_MODEL, S, 1408, D_MODEL, tk_s, [(f, "km"), (dout, "kn")], [(0, 1, 0)], [],
                    [(jax.ShapeDtypeStruct((D_FF, D_MODEL), BF16), (1408, D_MODEL), _mn)],
                    lambda i, accs, ex, out: out[0].__setitem__(Ellipsis, (0.5 * accs[0][...]).astype(BF16)))[0]
    emit(down=dwd)

    def epi_act(i, accs, ex, out):
        df = 0.5 * accs[0][...]
        av = ex[0][...].astype(F32)
        bv = ex[1][...].astype(F32)
        s = _sigmoid(av)
        out[0][...] = (df * bv * (s * (1.0 + av * (1.0 - s)))).astype(BF16)
        out[1][...] = (df * av * s).astype(BF16)

    sds = jax.ShapeDtypeStruct((S, D_FF), BF16)
    da, db = _fused_mm(f"{tag}_dact", S, D_FF, D_MODEL, tm, 1408, D_MODEL, [(dout, "mk"), (wd, "nk")], [(0, 1, 0)],
                       [(a, (tm, 1408), _mn), (b, (tm, 1408), _mn)], [(sds, (tm, 1408), _mn)] * 2, epi_act)

    def epi_w2(i, accs, ex, out):
        out[0][...] = accs[0][...].astype(BF16)
        out[1][...] = accs[1][...].astype(BF16)

    sdw = jax.ShapeDtypeStruct((D_MODEL, D_FF), BF16)
    dwg, dwu = _fused_mm(f"{tag}_dw_up", D_MODEL, D_FF, S, D_MODEL, 1408, tk_s,
                         [(h, "km"), (da, "kn"), (db, "kn")], [(0, 1, 0), (0, 2, 1)], [],
                         [(sdw, (D_MODEL, 1408), _mn)] * 2, epi_w2)
    tok = emit(gate=dwg, up=dwu)
    if tok is not None:
        g = g + tok

    def epi_dx(i, accs, ex, out):
        dx, dgt = _rmsnorm_bwd_tile(accs[0][...], ex[0][...], ex[1][...])
        out[0][...] = ex[2][...] + dx
        _colsum_into(out[1], i, jnp.sum(dgt, axis=0, keepdims=True))

    dx, dg = _fused_mm(f"{tag}_dx", S, D_MODEL, D_FF, tm, D_MODEL, 1408,
                       [(da, "mk"), (wg, "nk"), (db, "mk"), (wu, "nk")], [(0, 1, 0), (2, 3, 0)],
                       [(x, (tm, D_MODEL), _mn), (g, (1, D_MODEL), _row0), (dout, (tm, D_MODEL), _mn)],
                       [(jax.ShapeDtypeStruct((S, D_MODEL), F32), (tm, D_MODEL), _mn),
                        (jax.ShapeDtypeStruct((1, D_MODEL), F32), (1, D_MODEL), _row0)], epi_dx)
    return dx, dg


def _mix_out_bwd(dy, h_f, h_b, o_f, o_b, p, gn):
    S = dy.shape[0]
    ts = _tile(S, 512)
    c0 = math.sqrt(2.0 / math.pi)

    def body(dy_ref, hf, hb, of, ob, gate, z, gn_ref, dhr_ref, dgate_ref, do_ref, dz_ref, dgn_ref):
        i = pl.program_id(0)
        gv = gate[...]
        ge, t = _gelu(gv)
        dy_rg = dy_ref[:, 0:RG_W]
        dhr_ref[...] = dy_rg * ge
        dgelu = 0.5 * (1.0 + t) + 0.5 * gv * (1.0 - t * t) * c0 * (1.0 + 3.0 * 0.044715 * gv * gv)
        dgate_ref[...] = (dy_rg * (hf[...] + hb[...]) * dgelu).astype(BF16)
        o = of[...] + ob[...]
        zv = z[...]
        sig = _sigmoid(zv)
        gnv = gn_ref[...]
        dgn = jnp.zeros((1, GDN_DK), F32)
        for h in range(GDN_H):
            cols = slice(h * GDN_DK, (h + 1) * GDN_DK)
            oh = o[:, cols]
            r = lax.rsqrt(jnp.mean(oh * oh, axis=-1, keepdims=True) + EPS)
            ohat = oh * r
            dyh = dy_ref[:, RG_W + h * GDN_DK:RG_W + (h + 1) * GDN_DK]
            zh = zv[:, cols]
            sh = sig[:, cols]
            dn = dyh * zh * sh
            dz_ref[:, cols] = (dyh * ohat * gnv * (sh * (1.0 + zh * (1.0 - sh)))).astype(BF16)
            dxn = dn * gnv
            do_ref[:, cols] = r * (dxn - ohat * jnp.mean(dxn * ohat, axis=-1, keepdims=True))
            dgn = dgn + jnp.sum(dn * ohat, axis=0, keepdims=True)
        _colsum_into(dgn_ref, i, dgn)

    blk = (ts, RG_W)
    im = lambda i: (i, 0)
    z0 = lambda i: (0, 0)
    ins = [(dy, (ts, D_MODEL), im), (h_f, blk, im), (h_b, blk, im), (o_f, blk, im), (o_b, blk, im),
           (p, blk, lambda i: (i, 1)), (p, blk, lambda i: (i, 5)), (gn, (1, GDN_DK), z0)]
    outs = [(jax.ShapeDtypeStruct((S, RG_W), F32), blk, im), (jax.ShapeDtypeStruct((S, RG_W), BF16), blk, im),
            (jax.ShapeDtypeStruct((S, RG_W), F32), blk, im), (jax.ShapeDtypeStruct((S, RG_W), BF16), blk, im),
            (jax.ShapeDtypeStruct((1, GDN_DK), F32), (1, GDN_DK), z0)]
    return _rows("mix_out_bwd", S, ts, ins, outs, body)


def _rg_scan_adj(name, a_up, b_up, a_dn, b_dn):
    S, C = a_up.shape
    ts = _tile(S, 512)
    n_tiles = S // ts

    def body(au, bu, ad, bd, mu_ref, lam_ref, carry):
        @pl.when(pl.program_id(0) == 0)
        def _():
            carry[...] = jnp.zeros_like(carry)

        def step(t, c):
            cu, cd = c
            mu = bu[pl.ds(t, 1), :] + cu
            mu_ref[pl.ds(t, 1), :] = mu
            cu = au[pl.ds(t, 1), :] * mu
            tb = ts - 1 - t
            lam = bd[pl.ds(tb, 1), :] + cd
            lam_ref[pl.ds(tb, 1), :] = lam
            cd = ad[pl.ds(tb, 1), :] * lam
            return cu, cd

        cu, cd = lax.fori_loop(0, ts, step, (carry[0:1, :], carry[1:2, :]), unroll=8)
        carry[0:1, :] = cu
        carry[1:2, :] = cd

    fw = lambda i: (i, 0)
    bw = lambda i: (n_tiles - 1 - i, 0)
    sds = jax.ShapeDtypeStruct((S, C), F32)
    return _rows(name, S, ts,
                 [(a_up, (ts, C), fw), (b_up, (ts, C), fw), (a_dn, (ts, C), bw), (b_dn, (ts, C), bw)],
                 [(sds, (ts, C), fw), (sds, (ts, C), bw)], body, scratch=[pltpu.VMEM((8, C), F32)])


def _halo_ex(arr, S, tm, width):
    per = tm // HALO
    last = S // HALO - 1
    return [
        (arr, (tm, width), lambda i, j: (i, 0)),
        (arr, (HALO, width), lambda i, j: (jnp.maximum(i * per - 1, 0), 0)),
        (arr, (HALO, width), lambda i, j: (jnp.minimum((i + 1) * per, last), 0)),
    ]


def _rg_gates_bwd(xc, bd, prm, lam_f, lam_b, h_f, h_b):
    S = xc.shape[0]
    tm = _tile(S, 256)
    n_tiles = S // tm

    def epi(i, accs, ex, out):
        pre = accs[0][...]
        xv = ex[0][...]
        prm_ref = ex[1]
        lams = (ex[2][...], ex[3][...])
        hprev = (_shift(_ext(ex[4], ex[5], ex[6], i, n_tiles), -1, tm),
                 _shift(_ext(ex[7], ex[8], ex[9], i, n_tiles), 1, tm))
        dxc = jnp.zeros_like(xv)
        rows = []
        dlam_rows = []
        for d in range(2):
            r, ig, sp, a, sq = _rg_gate_terms(pre, xv, prm_ref, d)
            lam = lams[d]
            da = lam * hprev[d]
            di = lam * sq * xv
            dxc = dxc + lam * sq * ig
            dsq = lam * ig * xv
            dlog_a = da * a - dsq * (a * a) / sq
            dpre_r = dlog_a * (-RG_C * sp) * r * (1.0 - r)
            dpre_i = di * ig * (1.0 - ig)
            out[0][:, d * 1024:d * 1024 + RG_W] = dpre_r.astype(BF16)
            out[0][:, d * 1024 + RG_W:(d + 1) * 1024] = dpre_i.astype(BF16)
            rows += [jnp.sum(dpre_r, axis=0, keepdims=True), jnp.sum(dpre_i, axis=0, keepdims=True)]
            dsp = jnp.sum(dlog_a * (-RG_C * r), axis=0, keepdims=True)
            dlam_rows.append(-dsp * _sigmoid(-prm_ref[4 + d:5 + d, :]))
        out[1][...] = dxc
        zero = jnp.zeros((2, RG_W), F32)
        _colsum_into(out[2], i, jnp.concatenate(rows + dlam_rows + [zero], axis=0))

    blk = (tm, RG_W)
    im = lambda i, j: (i, 0)
    extras = ([(xc, blk, im), (prm, (8, RG_W), _row0), (lam_f, blk, im), (lam_b, blk, im)]
              + _halo_ex(h_f, S, tm, RG_W) + _halo_ex(h_b, S, tm, RG_W))
    outs = [(jax.ShapeDtypeStruct((S, 4 * RG_W), BF16), (tm, 4 * RG_W), im),
            (jax.ShapeDtypeStruct((S, RG_W), F32), blk, im),
            (jax.ShapeDtypeStruct((8, RG_W), F32), (8, RG_W), _row0)]
    return _fused_mm("rg_gates_bwd", S, 4 * RG_W, RG_W, tm, 4 * RG_W, RG_W, [(xc, "mk"), (bd, "kn")], [(0, 1, 0)],
                     extras, outs, epi)


def _roll_rows(ext, off):
    if off == 0:
        return ext
    return pltpu.roll(ext, (-off) % ext.shape[0], 0)


def _conv_bwd(name, p, colblk, w, grads, mode):
    S = p.shape[0]
    ts = _tile(S, 512)
    n_tiles = S // ts
    C = w.shape[1]
    ng = len(grads)

    def body(*refs):
        p_refs = refs[0:3]
        g_refs = refs[3:3 + 3 * ng]
        w_ref = refs[3 + 3 * ng]
        dx_ref, dw_ref, db_ref = refs[4 + 3 * ng:]
        i = pl.program_id(0)
        ext_p = _ext(*p_refs, i, n_tiles)
        dn = _ext(*g_refs[0:3], i, n_tiles)
        for gi in range(1, ng):
            dn = dn + _ext(*g_refs[3 * gi:3 * gi + 3], i, n_tiles)
        if mode == "bias":
            dc = dn
        else:
            c = None
            for j in range(CONV_W):
                term = w_ref[j:j + 1, :] * _roll_rows(ext_p, j - 2)
                c = term if c is None else c + term
            sig = _sigmoid(c)
            s = c * sig
            if mode in ("q", "k"):
                scale = GDN_DK ** -0.5 if mode == "q" else 1.0
                parts = []
                for h in range(GDN_H):
                    cols = slice(h * GDN_DK, (h + 1) * GDN_DK)
                    sh = s[:, cols]
                    dnh = dn[:, cols]
                    rinv = lax.rsqrt(jnp.sum(sh * sh, axis=-1, keepdims=True) + EPS)
                    parts.append(scale * rinv * (dnh - sh * (rinv * rinv) * jnp.sum(dnh * sh, axis=-1, keepdims=True)))
                ds = jnp.concatenate(parts, axis=-1)
            else:
                ds = dn
            dc = ds * (sig * (1.0 + c * (1.0 - sig)))
        dx = None
        for j in range(CONV_W):
            term = w_ref[j:j + 1, :] * _shift(dc, 2 - j, ts)
            dx = term if dx is None else dx + term
        dx_ref[...] = dx.astype(BF16)
        dc_main = dc[HALO:HALO + ts]
        dw = jnp.concatenate([jnp.sum(dc_main * _shift(ext_p, j - 2, ts), axis=0, keepdims=True)
                              for j in range(CONV_W)], axis=0)
        _colsum_into(dw_ref, i, dw)
        _colsum_into(db_ref, i, jnp.sum(dc_main, axis=0, keepdims=True))

    ins = _halo_ins(p, S, ts, C, colblk)
    for garr in grads:
        ins += _halo_ins(garr, S, ts, C, 0)
    ins += [(w, (CONV_W, C), lambda i: (0, 0))]
    z0 = lambda i: (0, 0)
    outs = [(jax.ShapeDtypeStruct((S, C), BF16), (ts, C), lambda i: (i, 0)),
            (jax.ShapeDtypeStruct((CONV_W, C), F32), (CONV_W, C), z0),
            (jax.ShapeDtypeStruct((1, C), F32), (1, C), z0)]
    return _rows(name, S, ts, ins, outs, body)


def _gdn_scan_bwd(q, k, bg, loc, do):
    S = q.shape[0]
    ts = _tile(S, GDN_TS)
    n_tiles = S // ts
    ncb = ts // CHUNK
    nch = S // CHUNK

    def body(*refs):
        ins = (refs[0:6], refs[6:12])
        outs = (refs[12:14], refs[14:16])
        dstate = refs[16]

        @pl.when(pl.program_id(0) == 0)
        def _():
            dstate[...] = jnp.zeros_like(dstate)

        def chunk(cc, carry):
            chains = []
            for d in range(2):
                c = ncb - 1 - cc if d == 0 else cc
                r0 = pl.multiple_of(c * CHUNK, CHUNK)
                rows = pl.ds(r0, CHUNK)
                for h in range(GDN_H):
                    cols = slice(h * GDN_DK, (h + 1) * GDN_DK)
                    m = _gdn_decay(ins[d][2], None, c, rows, r0, d * GDN_H + h, d == 1, None, None)
                    chains.append(dict(d=d, h=h, c=c, rows=rows, cols=cols, m=m, dsn=dstate[d * GDN_H + h]))
            for ch in chains:
                q_ref, k_ref, bg_ref, w_ref, a_ref, do_ref = ins[ch["d"]]
                rows, cols = ch["rows"], ch["cols"]
                dob = do_ref[rows, cols].astype(BF16)
                ch["dvn"] = (_dot(a_ref[ch["c"], ch["h"]], dob, 0, 0)
                             + _bdot(k_ref[rows, cols] * ch["m"]["egl"], ch["dsn"], 1, 0))
                ch["qdo"] = _bdot(q_ref[rows, cols] * ch["m"]["eg"], dob, 0, 0)
            for ch in chains:
                w_ref = ins[ch["d"]][3]
                ch["wdvn"] = _dot(w_ref[ch["rows"], ch["cols"]], ch["dvn"].astype(BF16), 0, 0)
            for ch in chains:
                dvn_ref, ds_ref = outs[ch["d"]]
                dvn_ref[ch["rows"], ch["cols"]] = ch["dvn"]
                ds_ref[ch["c"], ch["h"]] = ch["dsn"]
                dstate[ch["d"] * GDN_H + ch["h"]] = ch["qdo"] + ch["m"]["cd"] * ch["dsn"] - ch["wdvn"]
            return carry

        lax.fori_loop(0, ncb, chunk, 0)

    ins, outs = [], []
    for d in range(2):
        tix = _dir_tile(d, n_tiles, True)
        im = lambda i, tix=tix: (tix(i), 0)
        im4 = lambda i, tix=tix: (tix(i), 0, 0, 0)
        _, w, a, _ = loc[d]
        ins += [(q, (ts, GDN_W), im), (k, (ts, GDN_W), im), (bg, (ts, 128), im), (w, (ts, GDN_W), im),
                (a, (ncb, GDN_H, CHUNK, CHUNK), im4), (do, (ts, GDN_W), im)]
        outs += [(jax.ShapeDtypeStruct((S, GDN_W), F32), (ts, GDN_W), im),
                 (jax.ShapeDtypeStruct((nch, GDN_H, GDN_DK, GDN_DK), F32), (ncb, GDN_H, GDN_DK, GDN_DK), im4)]
    res = _rows("gdn_scan_bwd", S, ts, ins, outs, body, scratch=[pltpu.VMEM((2 * GDN_H, GDN_DK, GDN_DK), F32)])
    return res[0:2], res[2:4]


def _gdn_local_bwd(q, k, v, bg, gcr, do, loc, fwd, adj):
    S = q.shape[0]
    ts = _tile(S, GDN_TS)
    ncb = ts // CHUNK

    def body(q_ref, k_ref, v_ref, bg_ref, gcr_ref, do_ref, *rest):
        per_dir = (rest[0:5], rest[5:10])
        dq_ref, dk_ref, dv_ref, dbg_ref = rest[10:14]
        ri, ci = _tri_masks()
        lane = lax.broadcasted_iota(jnp.int32, (CHUNK, 128), 1)
        rowi = lax.broadcasted_iota(jnp.int32, (CHUNK, 1), 0)
        ones = jnp.ones((CHUNK, 128), F32)

        def chunk(c, carry):
            r0 = pl.multiple_of(c * CHUNK, CHUNK)
            rows = pl.ds(r0, CHUNK)
            chains = []
            for h in range(GDN_H):
                cols = slice(h * GDN_DK, (h + 1) * GDN_DK)
                qh, kh, vh = q_ref[rows, cols], k_ref[rows, cols], v_ref[rows, cols]
                dob = do_ref[rows, cols].astype(BF16)
                both = _bdot(jnp.concatenate([qh, kh], axis=0), kh, 1, 1)
                for d in range(2):
                    chains.append(dict(h=h, d=d, cols=cols, qh=qh, kh=kh, vh=vh, dob=dob, qk=both[0:CHUNK],
                                       kk=both[CHUNK:2 * CHUNK], col=d * GDN_H + h))
            for ch in chains:
                m = _gdn_decay(bg_ref, gcr_ref, c, rows, r0, ch["col"], ch["d"] == 1, ri, ci)
                t_ref, s_ref, ds_ref, vn_ref, dvn_ref = per_dir[ch["d"]]
                h, cols = ch["h"], ch["cols"]
                ch["m"] = m
                ch["kb"] = ch["kh"] * m["beta"]
                ch["kbg"] = ch["kb"] * m["eg"]
                ch["t"] = t_ref[c, h]
                st = s_ref[c, h]
                stb = st.astype(BF16)
                ch["dsn"] = ds_ref[c, h]
                vnb = vn_ref[rows, cols].astype(BF16)
                dvnb = dvn_ref[rows, cols].astype(BF16)
                ch["dcd"] = jnp.sum(jnp.sum(st * ch["dsn"], axis=1, keepdims=True), axis=0, keepdims=True)
                ch["dqd"] = _dot(ch["dob"], stb, 1, 1)
                ch["d_a"] = _dot(ch["dob"], vnb, 1, 1)
                ch["dkd"] = _bdot(vnb, ch["dsn"], 1, 1)
                ch["dw"] = -_dot(dvnb, stb, 1, 1)
                ch["dvb"] = _dot(ch["t"], dvnb, 0, 0)
                ch["d_t"] = _bdot(dvnb, ch["vh"] * m["beta"], 1, 1)
            for ch in chains:
                dwb = ch["dw"].astype(BF16)
                ch["d_t"] = ch["d_t"] + _bdot(dwb, ch["kbg"], 1, 1)
                ch["dkbg"] = _dot(ch["t"], dwb, 0, 0)
                ch["nn"] = ch["d_a"] * ch["m"]["dm"]
                ch["nn_q"] = _bdot(ch["nn"], ch["qh"], 0, 0)
                ch["nn_k"] = _bdot(ch["nn"], ch["kh"], 1, 0)
            for ch in chains:
                ch["x"] = _dot(ch["d_t"].astype(BF16), ch["t"], 1, 1)
            for ch in chains:
                d_l = -_dot(ch["t"], ch["x"].astype(BF16), 0, 0)
                ch["d_l"] = jnp.where(ch["m"]["strict"], d_l, 0.0)
                ch["mm"] = ch["d_l"] * ch["m"]["dm"]
            for ch in chains:
                m = ch["m"]
                ch["mm_kh"] = _bdot(ch["mm"], ch["kh"], 1, 0)
                ch["mm_kb"] = _bdot(ch["mm"], ch["kb"], 0, 0)
                l_mat = jnp.where(m["strict"], m["beta"] * ch["kk"] * m["dm"], 0.0)
                ch["e"] = ch["d_l"] * l_mat + ch["nn"] * ch["qk"]
                ch["cs"] = _dot(ch["e"], ones, 0, 0, HI)[:, 0:1]
            acc_bg = jnp.zeros((CHUNK, 128), F32)
            acc = {}
            for ch in chains:
                m = ch["m"]
                beta, eg, egl = m["beta"], m["eg"], m["egl"]
                dkb = ch["mm_kh"] + ch["dkbg"] * eg
                dk_d = ch["mm_kb"] + ch["nn_q"] + ch["dkd"] * egl + dkb * beta
                dq_d = ch["nn_k"] + ch["dqd"] * eg
                dv_d = ch["dvb"] * beta
                rs = jnp.sum(ch["e"], axis=1, keepdims=True)
                dkd_kd = ch["dkd"] * (ch["kh"] * egl)
                dgc = (rs - ch["cs"] + jnp.sum(ch["dqd"] * (ch["qh"] * eg), axis=1, keepdims=True)
                       - jnp.sum(dkd_kd, axis=1, keepdims=True) + jnp.sum(ch["dkbg"] * ch["kbg"], axis=1, keepdims=True))
                dgl = jnp.sum(jnp.sum(dkd_kd, axis=1, keepdims=True), axis=0, keepdims=True) + ch["dcd"] * m["cd"]
                dgc = dgc + jnp.where(rowi == (0 if ch["d"] == 1 else CHUNK - 1), dgl, 0.0)
                dbeta = (jnp.sum(dkb * ch["kh"], axis=1, keepdims=True)
                         + jnp.sum(ch["dvb"] * ch["vh"], axis=1, keepdims=True))
                acc_bg = acc_bg + jnp.where(lane == ch["col"], dbeta, 0.0) + jnp.where(lane == 8 + ch["col"], dgc, 0.0)
                if ch["d"] == 0:
                    acc[ch["h"]] = (dq_d, dk_d, dv_d)
                else:
                    dq0, dk0, dv0 = acc[ch["h"]]
                    dq_ref[rows, ch["cols"]] = dq0 + dq_d
                    dk_ref[rows, ch["cols"]] = dk0 + dk_d
                    dv_ref[rows, ch["cols"]] = dv0 + dv_d
            dbg_ref[rows, :] = acc_bg
            return carry

        lax.fori_loop(0, ncb, chunk, 0)

    im = lambda i: (i, 0)
    im4 = lambda i: (i, 0, 0, 0)
    blk = (ts, GDN_W)
    ins = [(q, blk, im), (k, blk, im), (v, blk, im), (bg, (ts, 128), im), (gcr, (ncb, 8, CHUNK), lambda i: (i, 0, 0)),
           (do, blk, im)]
    for d in range(2):
        ins += [(loc[d][3], (ncb, GDN_H, CHUNK, CHUNK), im4), (fwd[d][2], (ncb, GDN_H, GDN_DK, GDN_DK), im4),
                (adj[d][1], (ncb, GDN_H, GDN_DK, GDN_DK), im4), (fwd[d][1], blk, im), (adj[d][0], blk, im)]
    sds = jax.ShapeDtypeStruct((S, GDN_W), F32)
    outs = [(sds, blk, im), (sds, blk, im), (sds, blk, im), (jax.ShapeDtypeStruct((S, 128), F32), (ts, 128), im)]
    return _rows("gdn_local_bwd", S, ts, ins, outs, body)


def _gdn_prep_bwd(dbg_all, p, prm):
    S = p.shape[0]
    ts = _tile(S, 512)

    def body(dbg_ref, p_ref, prm_ref, dba_ref, dprm_ref):
        i = pl.program_id(0)
        raw = p_ref[...]
        dbg = dbg_ref[...]
        lane = lax.broadcasted_iota(jnp.int32, (1, 128), 1)
        is_g = (lane >= 8) & (lane < 16)
        ea = jnp.exp(prm_ref[0:1, :])
        arg = raw + prm_ref[1:2, :]
        g = jnp.where(is_g, -ea * _softplus(arg), 0.0)
        beta = _sigmoid(raw)
        dgc = jnp.where(is_g, dbg, 0.0)
        ri, ci = _tri_masks()
        lower = (ri >= ci).astype(F32)
        upper = (ri <= ci).astype(F32)
        dgs = []
        for c in range(ts // CHUNK):
            ch = dgc[c * CHUNK:(c + 1) * CHUNK]
            dgs.append(jnp.where(lane < 12, _dot(upper, ch, 1, 0, HI), _dot(lower, ch, 1, 0, HI)))
        dg = jnp.concatenate(dgs, axis=0)
        dalpha = jnp.where(is_g, dg * (-ea) * _sigmoid(arg), 0.0)
        dba_ref[...] = jnp.where(lane < 8, dbg * beta * (1.0 - beta), dalpha).astype(BF16)
        rows = jnp.concatenate([jnp.sum(dg * g, axis=0, keepdims=True), jnp.sum(dalpha, axis=0, keepdims=True),
                                jnp.zeros((6, 128), F32)], axis=0)
        _colsum_into(dprm_ref, i, rows)

    im = lambda i: (i, 0)
    z0 = lambda i: (0, 0)
    return _rows("gdn_prep_bwd", S, ts,
                 [(dbg_all, (ts, 128), im), (p, (ts, 128), lambda i: (i, COL_BA // 128)), (prm, (8, 128), z0)],
                 [(jax.ShapeDtypeStruct((S, 128), BF16), (ts, 128), im), (jax.ShapeDtypeStruct((8, 128), F32), (8, 128), z0)],
                 body)


def _mm_plain(name, M, N, K, tm, tn, tk, a, am, b, bm, dtype):
    return _fused_mm(name, M, N, K, tm, tn, tk, [(a, am), (b, bm)], [(0, 1, 0)], [],
                     [(jax.ShapeDtypeStruct((M, N), dtype), (tm, tn), _mn)],
                     lambda i, accs, ex, out: out[0].__setitem__(Ellipsis, accs[0][...].astype(dtype)))[0]


def _layer_bwd(x0, W, R, emit_big=None):
    S = x0.shape[0]
    tm = _tile(S, 512)
    tk_s = _tile(S, 512)
    G = {}

    def emit(**named):
        if emit_big is None:
            G.update(named)
            return None
        return emit_big(**named)

    def ffn_emit(prefix):
        return lambda **kw: emit(**{f"{prefix}_w_{k}": v for k, v in kw.items()})

    dx2, G["ffn2_norm"] = _ffn_bwd("ffn2b", R["dx3"], R["x2"], W["ffn2_norm"], R["h3"], R["a2"], R["b2"], R["f2"],
                                   W["ffn2_w_gate"], W["ffn2_w_up"], W["ffn2_w_down"], ffn_emit("ffn2"))
    tok = emit(w_out=_mm_plain("dw_out", D_MODEL, D_MODEL, S, D_MODEL, D_MODEL, tk_s, R["y"], "km", dx2, "kn", BF16))
    gn = W["gdn_norm"] if tok is None else W["gdn_norm"] + tok
    dy = _mm_plain("dy_mix", S, D_MODEL, D_MODEL, tm, D_MODEL, D_MODEL, dx2, "mk", W["w_out"], "nk", F32)
    p = R["p"]
    dhr, dgate, do, dz, G["gdn_norm"] = _mix_out_bwd(dy, R["h_f"], R["h_b"], R["o_f"], R["o_b"], p, gn)
    lam_b, lam_f = _rg_scan_adj("rg_scan_bwd", R["a_b"], dhr, R["a_f"], dhr)
    dpre, dxc_direct, d_rgprm = _rg_gates_bwd(R["xc"], R["bd"], R["rg_prm"], lam_f, lam_b, R["h_f"], R["h_b"])
    tmg = _tile(S, 512)
    dxc = _fused_mm("rg_dxc", S, RG_W, 4 * RG_W, tmg, RG_W, 4 * RG_W, [(dpre, "mk"), (R["bd"], "nk")], [(0, 1, 0)],
                    [(dxc_direct, (tmg, RG_W), _mn)], [(jax.ShapeDtypeStruct((S, RG_W), F32), (tmg, RG_W), _mn)],
                    lambda i, accs, ex, out: out[0].__setitem__(Ellipsis, ex[0][...] + accs[0][...]))[0]
    d_bd = _mm_plain("rg_dbd", RG_W, 4 * RG_W, S, RG_W, 4 * RG_W, tk_s, R["xc"], "km", dpre, "kn", F32)
    dx_rg, G["rg_conv_w"], G["rg_conv_b"] = _conv_bwd("rg_conv_bwd", p, 0, W["rg_conv_w"], [dxc], "bias")
    blocks = jnp.einsum("nigmj,nm->gnij", d_bd.reshape(RG_BLOCKS, RG_BLOCK, 4, RG_BLOCKS, RG_BLOCK),
                        jnp.eye(RG_BLOCKS, dtype=F32))
    G["rg_gate_a_w"] = jnp.stack([blocks[0], blocks[2]])
    G["rg_gate_x_w"] = jnp.stack([blocks[1], blocks[3]])
    G["rg_gate_a_b"] = jnp.stack([d_rgprm[0], d_rgprm[2]])
    G["rg_gate_x_b"] = jnp.stack([d_rgprm[1], d_rgprm[3]])
    G["rg_lambda"] = d_rgprm[4:6]
    adj = _gdn_scan_bwd(R["q"], R["k"], R["bg"], R["gdn_loc"], do)
    dq, dk, dv, dbg = _gdn_local_bwd(R["q"], R["k"], R["v"], R["bg"], R["gcr"], do, R["gdn_loc"], R["gdn_fwd"], adj)
    cw = W["gdn_conv_w"]
    dpq, dwq, _ = _conv_bwd("gdn_conv_q_bwd", p, 2, cw[:, 0:512], [dq], "q")
    dpk, dwk, _ = _conv_bwd("gdn_conv_k_bwd", p, 3, cw[:, 512:1024], [dk], "k")
    dpv, dwv, _ = _conv_bwd("gdn_conv_v_bwd", p, 4, cw[:, 1024:1536], [dv], "v")
    G["gdn_conv_w"] = jnp.concatenate([dwq, dwk, dwv], axis=1)
    dba, d_gprm = _gdn_prep_bwd(dbg, p, R["gdn_prm"])
    G["gdn_a_log"] = d_gprm[0, 8:16].reshape(2, GDN_H)
    G["gdn_dt_bias"] = d_gprm[1, 8:16].reshape(2, GDN_H)
    dp = jnp.concatenate([dx_rg, dgate, dpq, dpk, dpv, dz, dba], axis=1)
    tok = emit(w_in=_mm_plain("dw_in", D_MODEL, D_IN_PAD, S, D_MODEL, 640, tk_s, R["h2"], "km", dp, "kn", BF16))
    g_mix = W["mix_norm"] if tok is None else W["mix_norm"] + tok

    def epi_dx1(i, accs, ex, out):
        dx, dgt = _rmsnorm_bwd_tile(accs[0][...], ex[0][...], ex[1][...])
        out[0][...] = ex[2][...] + dx
        _colsum_into(out[1], i, jnp.sum(dgt, axis=0, keepdims=True))

    dx1, G["mix_norm"] = _fused_mm(
        "mix_dx", S, D_MODEL, D_IN_PAD, tm, D_MODEL, 640, [(dp, "mk"), (W["w_in"], "nk")], [(0, 1, 0)],
        [(R["x1"], (tm, D_MODEL), _mn), (g_mix, (1, D_MODEL), _row0), (dx2, (tm, D_MODEL), _mn)],
        [(jax.ShapeDtypeStruct((S, D_MODEL), F32), (tm, D_MODEL), _mn),
         (jax.ShapeDtypeStruct((1, D_MODEL), F32), (1, D_MODEL), _row0)], epi_dx1)
    dx0, G["ffn1_norm"] = _ffn_bwd("ffn1b", dx1, x0, W["ffn1_norm"], R["h1"], R["a1"], R["b1"], R["f1"],
                                   W["ffn1_w_gate"], W["ffn1_w_up"], W["ffn1_w_down"], ffn_emit("ffn1"))
    G["final_norm"] = R["d_final_norm"]
    return dx0, G


def _mesh_pos():
    x, y, c = lax.axis_index("x"), lax.axis_index("y"), lax.axis_index("c")
    return x, y, c, 4 * x + 2 * y + c


def _peer(x, y, c, r):
    px = 1 - x if r & 4 else x
    py = 1 - y if r & 2 else y
    pc = 1 - c if r & 1 else c
    return (px, py, pc), 4 * px + 2 * py + pc


_HBM = pl.BlockSpec(memory_space=pltpu.HBM)
_SEM = pl.BlockSpec(memory_space=pltpu.SEMAPHORE)


def _peer_copies(scatter, srcs, lands, send_sems, recv_sems):
    x, y, c, me = _mesh_pos()
    copies = []
    for a, (src, land) in enumerate(zip(srcs, lands)):
        for r in range(1, N_DEV):
            peer, peer_idx = _peer(x, y, c, r)
            copies.append(pltpu.make_async_remote_copy(
                src_ref=src.at[peer_idx] if scatter else src, dst_ref=land.at[r - 1] if scatter else land.at[me],
                send_sem=send_sems.at[a * 7 + r - 1], recv_sem=recv_sems.at[a * 7 + r - 1],
                device_id=peer, device_id_type=pl.DeviceIdType.MESH))
    return copies


def _exchange_start(name, scatter, arrays):
    slabs = arrays
    n = len(slabs)

    def body(*refs):
        srcs, lands = refs[0:n], refs[n:2 * n]
        send_sems, recv_sems = refs[2 * n], refs[2 * n + 1]
        token = refs[4 * n + 2]
        for cp in _peer_copies(scatter, srcs, lands, send_sems, recv_sems):
            cp.start()
        token[...] = jnp.zeros_like(token)

    land_shapes = [(N_DEV - 1,) + s.shape[1:] if scatter else (N_DEV,) + s.shape for s in slabs]
    out_shape = ([pltpu.SemaphoreType.DMA((7 * n,)), pltpu.SemaphoreType.DMA((7 * n,))]
                 + [pltpu.HBM(s.shape, s.dtype) for s in slabs]
                 + [pltpu.HBM(shp, s.dtype) for shp, s in zip(land_shapes, slabs)]
                 + [jax.ShapeDtypeStruct((8, 128), F32)])
    res = pl.pallas_call(
        body, name=name, out_shape=out_shape, in_specs=[_HBM] * (2 * n),
        out_specs=[_SEM, _SEM] + [_HBM] * (2 * n) + [pl.BlockSpec(memory_space=pltpu.VMEM)],
        input_output_aliases={i: 2 + i for i in range(2 * n)},
        compiler_params=pltpu.CompilerParams(has_side_effects=pltpu.SideEffectType.DATAFLOW_SIDE_EFFECTING),
    )(*[pltpu.with_memory_space_constraint(s, pltpu.HBM) for s in slabs],
      *[pltpu.with_memory_space_constraint(lax.empty(shp, s.dtype), pltpu.HBM) for shp, s in zip(land_shapes, slabs)])
    return dict(n=n, scatter=scatter, sems=res[0:2], srcs=res[2:2 + n], lands=res[2 + n:2 + 2 * n],
                token=res[2 + 2 * n][0, 0])


def _exchange_wait(name, started, after):
    n = started["n"]
    scatter = started["scatter"]

    def body(*refs):
        srcs, lands = refs[0:n], refs[n:2 * n]
        send_sems, recv_sems = refs[2 * n], refs[2 * n + 1]
        for cp in _peer_copies(scatter, srcs, lands, send_sems, recv_sems):
            cp.wait_send()
            cp.wait_recv()

    arrays = list(started["srcs"]) + list(started["lands"])
    res = pl.pallas_call(
        body, name=name, out_shape=[pltpu.HBM(a.shape, a.dtype) for a in arrays],
        in_specs=[_HBM] * (2 * n) + [_SEM, _SEM, pl.BlockSpec(memory_space=pl.ANY)], out_specs=[_HBM] * (2 * n),
        input_output_aliases={i: i for i in range(2 * n)},
        compiler_params=pltpu.CompilerParams(has_side_effects=pltpu.SideEffectType.DATAFLOW_SIDE_EFFECTING),
    )(*arrays, *started["sems"], after)
    return res[0:n], res[n:2 * n]


def _all_gather(name, arrays):
    n = len(arrays)

    def body(*refs):
        ins = refs[:n]
        outs = refs[n:2 * n]
        token = refs[2 * n]
        send_sems, recv_sems, local_sems = refs[2 * n + 1:]
        token[...] = jnp.zeros_like(token)
        x, y, c, me = _mesh_pos()
        sibling = (x, y, 1 - c)
        chips = [(1 - x, y), (x, 1 - y), (1 - x, 1 - y)]

        def idx(px, py, pc):
            return 4 * px + 2 * py + pc

        def copy(a, k, block, to, src=None):
            slot = outs[a].at[idx(*block)]
            return pltpu.make_async_remote_copy(
                src_ref=slot if src is None else src, dst_ref=slot, send_sem=send_sems.at[a * 7 + k],
                recv_sem=recv_sems.at[a * 7 + k], device_id=to, device_id_type=pl.DeviceIdType.MESH)

        locals_, sends = [], []
        for a in range(n):
            loc = pltpu.make_async_copy(ins[a], outs[a].at[me], local_sems.at[a])
            loc.start()
            locals_.append(loc)
            sends.append(copy(a, 0, (x, y, c), sibling, src=ins[a]))
            sends += [copy(a, 1 + j, (x, y, c), (*chip, c), src=ins[a]) for j, chip in enumerate(chips)]
        for cp in sends:
            cp.start()
        passed = []
        for a in range(n):
            for j, chip in enumerate(chips):
                copy(a, 1 + j, (*chip, c), (x, y, c)).wait_recv()
                fwd = copy(a, 4 + j, (*chip, c), sibling)
                fwd.start()
                passed.append(fwd)
        for a in range(n):
            copy(a, 0, sibling, (x, y, c)).wait_recv()
            for j, chip in enumerate(chips):
                copy(a, 4 + j, (*chip, 1 - c), (x, y, c)).wait_recv()
        for cp in sends + passed:
            cp.wait_send()
        for loc in locals_:
            loc.wait()

    any_spec = pl.BlockSpec(memory_space=pl.ANY)
    res = pl.pallas_call(
        body, name=name, in_specs=[any_spec] * n, out_specs=[any_spec] * n + [pl.BlockSpec(memory_space=pltpu.VMEM)],
        out_shape=[jax.ShapeDtypeStruct((N_DEV,) + a.shape, a.dtype) for a in arrays]
        + [jax.ShapeDtypeStruct((8, 128), F32)],
        scratch_shapes=[pltpu.SemaphoreType.DMA((7 * n,)), pltpu.SemaphoreType.DMA((7 * n,)),
                        pltpu.SemaphoreType.DMA((n,))],
        compiler_params=pltpu.CompilerParams(has_side_effects=True),
    )(*arrays)
    return res[:n], res[n][0, 0]


def _adamw_math(w, g, m, v):
    m2 = ADAM_B1 * m + (1.0 - ADAM_B1) * g
    v2 = ADAM_B2 * v + (1.0 - ADAM_B2) * (g * g)
    m_hat = m2 / (1.0 - ADAM_B1 ** ADAM_STEP)
    v_hat = v2 / (1.0 - ADAM_B2 ** ADAM_STEP)
    delta = -ADAM_LR * (m_hat / (jnp.sqrt(v_hat) + ADAM_EPS) + ADAM_WD * w)
    return delta, m2, v2


def _adamw_slabs(name, src, land, me, w, m, v, tr):
    R, C = w.shape

    def body(me_ref, own_ref, land_ref, w_ref, m_ref, v_ref, g_ref, d_ref, m2_ref, v2_ref):
        g = own_ref[0].astype(F32)
        for s in range(N_DEV - 1):
            g = g + land_ref[s].astype(F32)
        delta, m2, v2 = _adamw_math(w_ref[...], g, m_ref[...], v_ref[...])
        g_ref[...] = g
        d_ref[...] = delta
        m2_ref[...] = m2
        v2_ref[...] = v2

    im = lambda i, me_ref: (i, 0)
    grid_spec = pltpu.PrefetchScalarGridSpec(
        num_scalar_prefetch=1, grid=(R // tr,),
        in_specs=[pl.BlockSpec((1, tr, C), lambda i, me_ref: (me_ref[0], i, 0)),
                  pl.BlockSpec((N_DEV - 1, tr, C), lambda i, me_ref: (0, i, 0)),
                  pl.BlockSpec((tr, C), im), pl.BlockSpec((tr, C), im), pl.BlockSpec((tr, C), im)],
        out_specs=[pl.BlockSpec((tr, C), im)] * 4)
    return pl.pallas_call(body, name=name, grid_spec=grid_spec, out_shape=[jax.ShapeDtypeStruct((R, C), F32)] * 4,
                          compiler_params=_cp(1))(me.reshape(1).astype(jnp.int32), src, land, w, m, v)


def _sum_slots(name, slots):
    _, R, C = slots.shape

    def body(s_ref, o_ref):
        g = s_ref[0]
        for s in range(1, N_DEV):
            g = g + s_ref[s]
        o_ref[...] = g

    return _rows(name, R, R, [(slots, (N_DEV, R, C), lambda i: (0, 0, 0))],
                 [(jax.ShapeDtypeStruct((R, C), F32), (R, C), lambda i: (0, 0))], body)[0]


def _adamw_packed(name, g, w, m, v):
    R, C = g.shape

    def body(g_ref, w_ref, m_ref, v_ref, d_ref, m2_ref, v2_ref):
        delta, m2, v2 = _adamw_math(w_ref[...], g_ref[...], m_ref[...], v_ref[...])
        d_ref[...] = delta
        m2_ref[...] = m2
        v2_ref[...] = v2

    im = lambda i: (0, 0)
    sds = jax.ShapeDtypeStruct((R, C), F32)
    return _rows(name, R, R, [(a, (R, C), im) for a in (g, w, m, v)], [(sds, (R, C), im)] * 3, body)


def _pack(arrays):
    rows = []
    for a in arrays:
        flat = a.reshape(-1).astype(F32)
        pad = (-flat.shape[0]) % 128
        rows.append(jnp.pad(flat, (0, pad)).reshape(-1, 128))
    out = jnp.concatenate(rows, axis=0)
    return jnp.pad(out, ((0, (-out.shape[0]) % 8), (0, 0)))


def _unpack(packed, shapes):
    lead = packed.shape[:-2]
    outs = []
    r = 0
    for shp in shapes:
        n = math.prod(shp)
        nr = -(-n // 128)
        flat = packed[..., r:r + nr, :].reshape(lead + (nr * 128,))[..., :n]
        outs.append(flat.reshape(lead + tuple(shp)))
        r += nr
    return outs


FFN1_BIG = ["ffn1_w_gate", "ffn1_w_up", "ffn1_w_down"]
MIX_BIG = ["w_in", "w_out"]
FFN2_BIG = ["ffn2_w_gate", "ffn2_w_up", "ffn2_w_down"]
BIG = FFN1_BIG + MIX_BIG + FFN2_BIG
COL_SHARDED = {"ffn1_w_gate", "ffn1_w_up", "w_in", "ffn2_w_gate", "ffn2_w_up"}
SMALL_SHARDED = ["rg_conv_w", "rg_gate_a_b", "rg_gate_x_b", "rg_lambda", "gdn_conv_w"]
WEIGHTS = ["ffn1_norm", "ffn1_w_gate", "ffn1_w_up", "ffn1_w_down", "mix_norm", "w_in", "w_out", "rg_conv_w", "rg_conv_b",
           "rg_gate_a_w", "rg_gate_a_b", "rg_gate_x_w", "rg_gate_x_b", "rg_lambda", "gdn_conv_w", "gdn_a_log",
           "gdn_dt_bias", "gdn_norm", "ffn2_norm", "ffn2_w_gate", "ffn2_w_up", "ffn2_w_down", "final_norm"]
SMALL = [n for n in WEIGHTS if n not in BIG]
ROW_VECTORS = {"ffn1_norm", "mix_norm", "ffn2_norm", "gdn_norm", "rg_conv_b", "final_norm"}
ROW_TILE = {"ffn1_w_gate": 256, "ffn1_w_up": 256, "ffn1_w_down": 176, "w_in": 256, "w_out": 64,
            "ffn2_w_gate": 256, "ffn2_w_up": 256, "ffn2_w_down": 176}


def _unshard_cols(g):
    return g.transpose(1, 0, 2).reshape(g.shape[1], N_DEV * g.shape[2])


def _to_slabs(name, g):
    if name in COL_SHARDED:
        r, ctot = g.shape
        return g.reshape(r, N_DEV, ctot // N_DEV).transpose(1, 0, 2)
    return g.reshape(N_DEV, g.shape[0] // N_DEV, g.shape[1])


def _step(x, target, w, m, v):
    _, _, _, me = _mesh_pos()
    def unshard(n, gth):
        full = _unshard_cols(gth) if n in COL_SHARDED else gth.reshape(-1, gth.shape[-1])
        return jnp.pad(full, ((0, 0), (0, D_IN_PAD - D_IN))) if n == "w_in" else full

    def landed(started, name, after):
        srcs, lands = _exchange_wait(name, started, after)
        return [lax.dynamic_update_index_in_dim(land, src, me, 0) for src, land in zip(srcs, lands)]

    first, tok = _all_gather("gather_ffn1", [w[n].astype(BF16) for n in FFN1_BIG])
    W = {n: unshard(n, gth) for n, gth in zip(FFN1_BIG, first)}
    small_shards = [w[n] for n in SMALL_SHARDED]
    st_mix = _exchange_start("gather_mix_start", False,
                             [(w[n] + tok).astype(BF16) for n in MIX_BIG] + [_pack(small_shards) + tok])
    st_ffn2 = _exchange_start("gather_ffn2_start", False, [(w[n] + tok).astype(BF16) for n in FFN2_BIG])
    for n in SMALL:
        if n not in SMALL_SHARDED:
            W[n] = w[n]
    W["ffn1_norm"] = w["ffn1_norm"] + (st_mix["token"] + st_ffn2["token"])

    def more(stage, after):
        if stage == "ffn2":
            return {n: unshard(n, gth) for n, gth in zip(FFN2_BIG, landed(st_ffn2, "gather_ffn2_wait", after))}
        got = landed(st_mix, "gather_mix_wait", after)
        new = {n: unshard(n, gth) for n, gth in zip(MIX_BIG, got)}
        for n, gth in zip(SMALL_SHARDED, _unpack(got[-1], [s.shape for s in small_shards])):
            new[n] = jnp.moveaxis(gth, 0, -2).reshape(gth.shape[1:-1] + (N_DEV * gth.shape[-1],))
        return new

    R = _layer_fwd(x, target, W, more)
    W = R["W"]
    pending = []

    def emit_big(**named):
        slabs = [_to_slabs(n, g[:, :D_IN] if n == "w_in" else g) for n, g in named.items()]
        started = _exchange_start(f"scatter_start_{len(pending)}", True, slabs)
        pending.append((list(named), started))
        return started["token"]

    grad_x, G = _layer_bwd(x, W, R, emit_big)
    loss = lax.psum(R["loss"][0, 0], ("x", "y", "c"))
    out = {}
    for i, (names, started) in enumerate(pending):
        srcs, lands = _exchange_wait(f"scatter_wait_{i}", started, grad_x)
        for n, src, land in zip(names, srcs, lands):
            out[n] = _adamw_slabs(f"adamw_{n}", src, land, me, w[n], m[n], v[n], ROW_TILE[n])
    full_shapes = [G[n].shape for n in SMALL]
    slots = _all_gather("gather_small_grads", [_pack([G[n] for n in SMALL])])[0][0]
    reduced = dict(zip(SMALL, _unpack(_sum_slots("sum_small_grads", slots), full_shapes)))
    g_small = []
    for n in SMALL:
        g = reduced[n]
        if n in SMALL_SHARDED:
            per = g.shape[-1] // N_DEV
            g = lax.dynamic_slice_in_dim(g, me * per, per, axis=g.ndim - 1)
        g_small.append(g.reshape(w[n].shape))
    shapes = [w[n].shape for n in SMALL]
    d_p, m_p, v_p = _adamw_packed("adamw_small", _pack(g_small), _pack([w[n] for n in SMALL]),
                                  _pack([m[n] for n in SMALL]), _pack([v[n] for n in SMALL]))
    for n, g, d_, m_, v_ in zip(SMALL, g_small, _unpack(d_p, shapes), _unpack(m_p, shapes), _unpack(v_p, shapes)):
        out[n] = (g, d_, m_, v_)
    return loss, grad_x, out


def kernel(x, ffn1_norm, ffn1_w_gate, ffn1_w_up, ffn1_w_down, mix_norm, w_in, w_out, rg_conv_w, rg_conv_b, rg_gate_a_w, rg_gate_a_b, rg_gate_x_w, rg_gate_x_b, rg_lambda, gdn_conv_w, gdn_a_log, gdn_dt_bias, gdn_norm, ffn2_norm, ffn2_w_gate, ffn2_w_up, ffn2_w_down, final_norm, loss_target, m_ffn1_norm, m_ffn1_w_gate, m_ffn1_w_up, m_ffn1_w_down, m_mix_norm, m_w_in, m_w_out, m_rg_conv_w, m_rg_conv_b, m_rg_gate_a_w, m_rg_gate_a_b, m_rg_gate_x_w, m_rg_gate_x_b, m_rg_lambda, m_gdn_conv_w, m_gdn_a_log, m_gdn_dt_bias, m_gdn_norm, m_ffn2_norm, m_ffn2_w_gate, m_ffn2_w_up, m_ffn2_w_down, m_final_norm, v_ffn1_norm, v_ffn1_w_gate, v_ffn1_w_up, v_ffn1_w_down, v_mix_norm, v_w_in, v_w_out, v_rg_conv_w, v_rg_conv_b, v_rg_gate_a_w, v_rg_gate_a_b, v_rg_gate_x_w, v_rg_gate_x_b, v_rg_lambda, v_gdn_conv_w, v_gdn_a_log, v_gdn_dt_bias, v_gdn_norm, v_ffn2_norm, v_ffn2_w_gate, v_ffn2_w_up, v_ffn2_w_down, v_final_norm):
    args = dict(locals())
    orig_shapes = {n: args[n].shape for n in WEIGHTS}

    def local(prefix):
        d = {}
        for n in WEIGHTS:
            a = args[prefix + n]
            d[n] = a.reshape(1, -1) if n in ROW_VECTORS else a[0]
        return d

    loss, grad_x, out = _step(x[0], loss_target[0], local(""), local("m_"), local("v_"))
    res = [loss, grad_x[None]]
    for k in range(4):
        res += [out[n][k].reshape(orig_shapes[n]) for n in WEIGHTS]
    return tuple(res)
```

```python
import functools
import math

import jax
import jax.numpy as jnp
from jax import lax
from jax.experimental import pallas as pl
from jax.experimental.pallas import tpu as pltpu

F32, BF16 = jnp.float32, jnp.bfloat16

D_MODEL = 1024
D_FF = 2816
RG_W = 512
RG_BLOCKS = 8
RG_BLOCK = 64
RG_C = 8.0
CONV_W = 4
GDN_H = 4
GDN_DK = 128
CHUNK = 64
EPS = 1e-6
D_IN = 3088
D_IN_PAD = 3200
COL_BA = 3072
N_DEV = 8
HALO = 8
VMEM_LIMIT = 48 * 1024 * 1024

ADAM_LR = 0.001
ADAM_B1 = 0.9
ADAM_B2 = 0.999
ADAM_EPS = 1e-08
ADAM_WD = 0.01
ADAM_STEP = 10

HI = lax.Precision.HIGHEST


def _cp(n):
    return pltpu.CompilerParams(dimension_semantics=("arbitrary",) * n, vmem_limit_bytes=VMEM_LIMIT)


def _tile(n, pref):
    return min(n, pref)


def _sigmoid(x):
    return 0.5 * jnp.tanh(0.5 * x) + 0.5


def _softplus(x):
    return jnp.maximum(x, 0.0) + jnp.log(1.0 + jnp.exp(-jnp.abs(x)))


def _dot(a, b, ca, cb, prec=None):
    return lax.dot_general(a, b, (((ca,), (cb,)), ((), ())), preferred_element_type=F32, precision=prec)


def _fused_mm(name, M, N, K, tm, tn, tk, ops, pairs, extras, outs, epilogue):
    nm, nn, nk = M // tm, N // tn, K // tk
    assert nm * tm == M and nn * tn == N and nk * tk == K, (name, M, N, K, tm, tn, tk)
    spec_of = {
        "mk": pl.BlockSpec((tm, tk), lambda i, j, k: (i, k)),
        "km": pl.BlockSpec((tk, tm), lambda i, j, k: (k, i)),
        "kn": pl.BlockSpec((tk, tn), lambda i, j, k: (k, j)),
        "nk": pl.BlockSpec((tn, tk), lambda i, j, k: (j, k)),
    }
    in_specs = [spec_of[m] for _, m in ops]
    in_specs += [pl.BlockSpec(bs, lambda i, j, k, im=im: im(i, j)) for _, bs, im in extras]
    out_specs = [pl.BlockSpec(bs, lambda i, j, k, im=im: im(i, j)) for _, bs, im in outs]
    n_ops, n_ex, n_out = len(ops), len(extras), len(outs)
    n_acc = 1 + max(g for _, _, g in pairs)
    modes = [m for _, m in ops]

    def body(*refs):
        op_refs = refs[:n_ops]
        ex_refs = refs[n_ops:n_ops + n_ex]
        out_refs = refs[n_ops + n_ex:n_ops + n_ex + n_out]
        accs = refs[n_ops + n_ex + n_out:]
        i = pl.program_id(0)
        k = pl.program_id(2)
        vals = [r[...].astype(BF16) for r in op_refs]
        sums = [None] * n_acc
        for ia, ib, g in pairs:
            ca = 1 if modes[ia] == "mk" else 0
            cb = 0 if modes[ib] == "kn" else 1
            d = _dot(vals[ia], vals[ib], ca, cb)
            sums[g] = d if sums[g] is None else sums[g] + d
        if nk == 1:
            epilogue(i, [_Held(s) for s in sums], ex_refs, out_refs)
            return

        @pl.when(k == 0)
        def _():
            for a, s in zip(accs, sums):
                a[...] = s

        @pl.when(k > 0)
        def _():
            for a, s in zip(accs, sums):
                a[...] += s

        @pl.when(k == nk - 1)
        def _():
            epilogue(i, accs, ex_refs, out_refs)

    res = pl.pallas_call(
        body, name=name, grid=(nm, nn, nk), in_specs=in_specs, out_specs=out_specs,
        out_shape=[o for o, _, _ in outs],
        scratch_shapes=[pltpu.VMEM((tm, tn), F32)] * (n_acc if nk > 1 else 0),
        compiler_params=_cp(3),
    )(*[a for a, _ in ops], *[a for a, _, _ in extras])
    return res


class _Held:
    def __init__(self, value):
        self.value = value

    def __getitem__(self, idx):
        return self.value[idx]


def _mn(i, j):
    return (i, j)


def _row0(i, j):
    return (0, 0)


def _rows(name, S, ts, ins, outs, body, scratch=()):
    return pl.pallas_call(
        body, name=name, grid=(S // ts,),
        in_specs=[pl.BlockSpec(bs, im) for _, bs, im in ins],
        out_specs=[pl.BlockSpec(bs, im) for _, bs, im in outs],
        out_shape=[o for o, _, _ in outs],
        scratch_shapes=list(scratch),
        compiler_params=_cp(1),
    )(*[a for a, _, _ in ins])


def _halo_ins(arr, S, ts, width, colblk):
    per = ts // HALO
    last = S // HALO - 1
    return [
        (arr, (ts, width), lambda i: (i, colblk)),
        (arr, (HALO, width), lambda i: (jnp.maximum(i * per - 1, 0), colblk)),
        (arr, (HALO, width), lambda i: (jnp.minimum((i + 1) * per, last), colblk)),
    ]


def _ext(main_ref, prev_ref, next_ref, i, n_tiles):
    prev = jnp.where(i > 0, prev_ref[...].astype(F32), 0.0)
    nxt = jnp.where(i < n_tiles - 1, next_ref[...].astype(F32), 0.0)
    return jnp.concatenate([prev, main_ref[...].astype(F32), nxt], axis=0)


def _shift(ext, off, ts):
    n = ext.shape[0]
    if off == 0:
        return ext[HALO:HALO + ts]
    return pltpu.roll(ext, (-off) % n, 0)[HALO:HALO + ts]


def _rmsnorm_fwd(name, x, g):
    S, D = x.shape
    ts = _tile(S, 512)

    def body(x_ref, g_ref, o_ref):
        xv = x_ref[...]
        r = lax.rsqrt(jnp.mean(xv * xv, axis=-1, keepdims=True) + EPS)
        o_ref[...] = (xv * r * g_ref[...]).astype(BF16)

    return _rows(name, S, ts,
                 [(x, (ts, D), lambda i: (i, 0)), (g, (1, D), lambda i: (0, 0))],
                 [(jax.ShapeDtypeStruct((S, D), BF16), (ts, D), lambda i: (i, 0))], body)[0]


def _rmsnorm_bwd_tile(dh, x, g):
    r = lax.rsqrt(jnp.mean(x * x, axis=-1, keepdims=True) + EPS)
    xhat = x * r
    dxn = dh * g
    dx = r * (dxn - xhat * jnp.mean(dxn * xhat, axis=-1, keepdims=True))
    return dx, dh * xhat


def _ffn_fwd(tag, x, h, wg, wu, wd):
    S = x.shape[0]
    tm = _tile(S, 512)
    tn = 1408

    def epi_up(i, accs, ex, out):
        a = accs[0][...]
        b = accs[1][...]
        s = _sigmoid(a)
        sa = a * s
        out[0][...] = sa.astype(BF16)
        out[1][...] = (b * (s * (1.0 + a * (1.0 - s)))).astype(BF16)
        out[2][...] = (sa * b).astype(BF16)

    sds = jax.ShapeDtypeStruct((S, D_FF), BF16)
    a, b, f = _fused_mm(f"{tag}_up", S, D_FF, D_MODEL, tm, tn, D_MODEL,
                        [(h, "mk"), (wg, "kn"), (wu, "kn")], [(0, 1, 0), (0, 2, 1)], [],
                        [(sds, (tm, tn), _mn)] * 3, epi_up)

    def epi_down(i, accs, ex, out):
        out[0][...] = ex[0][...] + 0.5 * accs[0][...]

    xo = _fused_mm(f"{tag}_down", S, D_MODEL, D_FF, tm, D_MODEL, 1408,
                   [(f, "mk"), (wd, "kn")], [(0, 1, 0)], [(x, (tm, D_MODEL), _mn)],
                   [(jax.ShapeDtypeStruct((S, D_MODEL), F32), (tm, D_MODEL), _mn)], epi_down)[0]
    return xo, a, b, f


def _conv_taps(ext, w_ref, ts):
    acc = None
    for j in range(CONV_W):
        term = w_ref[j:j + 1, :] * _shift(ext, j - 2, ts)
        acc = term if acc is None else acc + term
    return acc


def _l2norm_heads(s, scale):
    outs = []
    for h in range(GDN_H):
        sh = s[:, h * GDN_DK:(h + 1) * GDN_DK]
        outs.append(sh * (lax.rsqrt(jnp.sum(sh * sh, axis=-1, keepdims=True) + EPS) * scale))
    return jnp.concatenate(outs, axis=-1)


def _conv_fwd(name, p, colblk, w, bias, mode):
    S = p.shape[0]
    ts = _tile(S, 512)
    n_tiles = S // ts
    C = w.shape[1]

    def body(main, prev, nxt, w_ref, b_ref, o_ref):
        i = pl.program_id(0)
        c = _conv_taps(_ext(main, prev, nxt, i, n_tiles), w_ref, ts)
        if mode == "bias":
            o_ref[...] = c + b_ref[...]
        else:
            s = c * _sigmoid(c)
            if mode == "q":
                s = _l2norm_heads(s, GDN_DK ** -0.5)
            elif mode == "k":
                s = _l2norm_heads(s, 1.0)
            o_ref[...] = s

    ins = _halo_ins(p, S, ts, C, colblk) + [(w, (CONV_W, C), lambda i: (0, 0)), (bias, (1, C), lambda i: (0, 0))]
    return _rows(name, S, ts, ins, [(jax.ShapeDtypeStruct((S, C), F32), (ts, C), lambda i: (i, 0))], body)[0]


def _rg_gate_terms(pre, xc, prm_ref, d):
    r = _sigmoid(pre[:, d * 1024:d * 1024 + RG_W] + prm_ref[2 * d:2 * d + 1, :])
    ig = _sigmoid(pre[:, d * 1024 + RG_W:(d + 1) * 1024] + prm_ref[2 * d + 1:2 * d + 2, :])
    sp = _softplus(-prm_ref[4 + d:5 + d, :])
    log_a = -RG_C * r * sp
    a = jnp.exp(log_a)
    t = jnp.tanh(log_a)
    sq = jnp.sqrt(-2.0 * t / (1.0 - t))
    return r, ig, sp, a, sq


def _rg_gates_fwd(xc, bd, prm):
    S = xc.shape[0]
    tm = _tile(S, 256)

    def epi(i, accs, ex, out):
        pre = accs[0][...]
        xv = ex[0][...]
        for d in range(2):
            r, ig, sp, a, sq = _rg_gate_terms(pre, xv, ex[1], d)
            out[2 * d][...] = a
            out[2 * d + 1][...] = sq * ig * xv

    sds = jax.ShapeDtypeStruct((S, RG_W), F32)
    blk = (tm, RG_W)
    im = lambda i, j: (i, 0)
    return _fused_mm("rg_gates_fwd", S, 4 * RG_W, RG_W, tm, 4 * RG_W, RG_W,
                     [(xc, "mk"), (bd, "kn")], [(0, 1, 0)],
                     [(xc, blk, im), (prm, (8, RG_W), _row0)], [(sds, blk, im)] * 4, epi)


SUBLANES = 8


def _scan_rows(a, b, reverse):
    rows = lax.broadcasted_iota(jnp.int32, a.shape, 0)
    s = 1
    while s < SUBLANES:
        shift = SUBLANES - s if reverse else s
        a_sh = pltpu.roll(a, shift, 0)
        b_sh = pltpu.roll(b, shift, 0)
        valid = (rows < SUBLANES - s) if reverse else (rows >= s)
        b = jnp.where(valid, a * b_sh + b, b)
        a = jnp.where(valid, a * a_sh, a)
        s *= 2
    return a, b


def _rg_scan(name, a_f, b_f, a_b, b_b):
    S, C = a_f.shape
    ts = _tile(S, 512)
    n_tiles = S // ts

    def body(af, bf, ab, bb, hf, hb, carry):
        @pl.when(pl.program_id(0) == 0)
        def _():
            carry[...] = jnp.zeros_like(carry)

        n_sub = ts // SUBLANES

        def step(j, c):
            cf, cb = c
            r0 = pl.multiple_of(j * SUBLANES, SUBLANES)
            cum_a, h0 = _scan_rows(af[pl.ds(r0, SUBLANES), :], bf[pl.ds(r0, SUBLANES), :], False)
            h = h0 + cum_a * cf
            hf[pl.ds(r0, SUBLANES), :] = h
            cf = h[SUBLANES - 1:SUBLANES, :]
            r1 = pl.multiple_of((n_sub - 1 - j) * SUBLANES, SUBLANES)
            cum_a, h0 = _scan_rows(ab[pl.ds(r1, SUBLANES), :], bb[pl.ds(r1, SUBLANES), :], True)
            h = h0 + cum_a * cb
            hb[pl.ds(r1, SUBLANES), :] = h
            cb = h[0:1, :]
            return cf, cb

        cf, cb = lax.fori_loop(0, n_sub, step, (carry[0:1, :], carry[1:2, :]), unroll=4)
        carry[0:1, :] = cf
        carry[1:2, :] = cb

    fw = lambda i: (i, 0)
    bw = lambda i: (n_tiles - 1 - i, 0)
    sds = jax.ShapeDtypeStruct((S, C), F32)
    return _rows(name, S, ts,
                 [(a_f, (ts, C), fw), (b_f, (ts, C), fw), (a_b, (ts, C), bw), (b_b, (ts, C), bw)],
                 [(sds, (ts, C), fw), (sds, (ts, C), bw)], body, scratch=[pltpu.VMEM((8, C), F32)])


def _tri_masks():
    ri = lax.broadcasted_iota(jnp.int32, (CHUNK, CHUNK), 0)
    ci = lax.broadcasted_iota(jnp.int32, (CHUNK, CHUNK), 1)
    return ri, ci


def _gdn_prep_fwd(p, prm):
    S = p.shape[0]
    ts = _tile(S, 512)

    def body(p_ref, prm_ref, o_ref):
        raw = p_ref[...]
        lane = lax.broadcasted_iota(jnp.int32, (1, 128), 1)
        g = -jnp.exp(prm_ref[0:1, :]) * _softplus(raw + prm_ref[1:2, :])
        g = jnp.where((lane >= 8) & (lane < 16), g, 0.0)
        beta = _sigmoid(raw)
        ri, ci = _tri_masks()
        lower = (ri >= ci).astype(F32)
        upper = (ri <= ci).astype(F32)
        for c in range(ts // CHUNK):
            rows = slice(c * CHUNK, (c + 1) * CHUNK)
            gch = g[rows]
            gc = jnp.where(lane < 12, _dot(lower, gch, 1, 0, HI), _dot(upper, gch, 1, 0, HI))
            o_ref[rows, :] = jnp.where(lane < 8, beta[rows], gc)

    return _rows("gdn_prep_fwd", S, ts,
                 [(p, (ts, 128), lambda i: (i, COL_BA // 128)), (prm, (8, 128), lambda i: (0, 0))],
                 [(jax.ShapeDtypeStruct((S, 128), F32), (ts, 128), lambda i: (i, 0))], body)[0]


def _bdot(a, b, ca, cb):
    return _dot(a.astype(BF16), b.astype(BF16), ca, cb)


GDN_W = GDN_H * GDN_DK
GDN_TS = 256


def _gdn_decay(bg_ref, gcr_ref, c, rows, r0, col, rev, ri, ci):
    beta = bg_ref[rows, col:col + 1]
    gc = bg_ref[rows, 8 + col:9 + col]
    last = 0 if rev else CHUNK - 1
    gl = bg_ref[pl.ds(r0 + last, 1), 8 + col:9 + col]
    out = dict(beta=beta, gc=gc, gl=gl, eg=jnp.exp(gc), egl=jnp.exp(gl - gc), cd=jnp.exp(gl))
    if gcr_ref is not None:
        incl = (ri <= ci) if rev else (ri >= ci)
        out["strict"] = (ri < ci) if rev else (ri > ci)
        out["dm"] = jnp.where(incl, jnp.exp(jnp.where(incl, gc - gcr_ref[c, col:col + 1, :], 0.0)), 0.0)
    return out


def _dir_tile(d, n_tiles, flip):
    if (d == 1) != flip:
        return lambda i: n_tiles - 1 - i
    return lambda i: i


def _gdn_local_fwd(q, k, v, bg, gcr):
    S = q.shape[0]
    ts = _tile(S, GDN_TS)
    ncb = ts // CHUNK
    nch = S // CHUNK

    def body(q_ref, k_ref, v_ref, bg_ref, gcr_ref, u0, w0, a0, t0, u1, w1, a1, t1):
        ri, ci = _tri_masks()
        eye = (ri == ci).astype(F32)
        outs = ((u0, w0, a0, t0), (u1, w1, a1, t1))

        def chunk(c, carry):
            r0 = pl.multiple_of(c * CHUNK, CHUNK)
            rows = pl.ds(r0, CHUNK)
            chains = []
            for h in range(GDN_H):
                cols = slice(h * GDN_DK, (h + 1) * GDN_DK)
                qh, kh, vh = q_ref[rows, cols], k_ref[rows, cols], v_ref[rows, cols]
                both = _bdot(jnp.concatenate([qh, kh], axis=0), kh, 1, 1)
                for d in range(2):
                    chains.append(dict(h=h, d=d, cols=cols, kh=kh, vh=vh, qk=both[0:CHUNK], kk=both[CHUNK:2 * CHUNK]))
            for ch in chains:
                m = _gdn_decay(bg_ref, gcr_ref, c, rows, r0, ch["d"] * GDN_H + ch["h"], ch["d"] == 1, ri, ci)
                ch["m"] = m
                ch["x"] = -jnp.where(m["strict"], m["beta"] * ch["kk"] * m["dm"], 0.0)
                ch["t"] = eye + ch["x"]
            for ch in chains:
                ch["pw"] = _bdot(ch["x"], ch["x"], 1, 0)
            for level in range(1, 6):
                last_level = level == 5
                for ch in chains:
                    rhs = ch["t"] if last_level else jnp.concatenate([ch["t"], ch["pw"]], axis=1)
                    ch["prod"] = _bdot(ch["pw"], rhs, 1, 0)
                for ch in chains:
                    ch["t"] = ch["t"] + ch["prod"][:, 0:CHUNK]
                    if not last_level:
                        ch["pw"] = ch["prod"][:, CHUNK:2 * CHUNK]
            for ch in chains:
                m = ch["m"]
                rhs = jnp.concatenate([ch["vh"] * m["beta"], ch["kh"] * (m["beta"] * m["eg"])], axis=1)
                ch["uw"] = _bdot(ch["t"], rhs, 1, 0)
            for ch in chains:
                u_ref, w_ref, a_ref, t_ref = outs[ch["d"]]
                u_ref[rows, ch["cols"]] = ch["uw"][:, 0:GDN_DK]
                w_ref[rows, ch["cols"]] = ch["uw"][:, GDN_DK:2 * GDN_DK].astype(BF16)
                a_ref[c, ch["h"]] = (ch["qk"] * ch["m"]["dm"]).astype(BF16)
                t_ref[c, ch["h"]] = ch["t"].astype(BF16)
            return carry

        lax.fori_loop(0, ncb, chunk, 0)

    im = lambda i: (i, 0)
    im4 = lambda i: (i, 0, 0, 0)
    ins = [(q, (ts, GDN_W), im), (k, (ts, GDN_W), im), (v, (ts, GDN_W), im), (bg, (ts, 128), im),
           (gcr, (ncb, 8, CHUNK), lambda i: (i, 0, 0))]
    per_dir = [(jax.ShapeDtypeStruct((S, GDN_W), F32), (ts, GDN_W), im),
               (jax.ShapeDtypeStruct((S, GDN_W), BF16), (ts, GDN_W), im),
               (jax.ShapeDtypeStruct((nch, GDN_H, CHUNK, CHUNK), BF16), (ncb, GDN_H, CHUNK, CHUNK), im4),
               (jax.ShapeDtypeStruct((nch, GDN_H, CHUNK, CHUNK), BF16), (ncb, GDN_H, CHUNK, CHUNK), im4)]
    res = _rows("gdn_local_fwd", S, ts, ins, per_dir * 2, body)
    return res[0:4], res[4:8]


def _gdn_scan_fwd(q, k, bg, loc):
    S = q.shape[0]
    ts = _tile(S, GDN_TS)
    n_tiles = S // ts
    ncb = ts // CHUNK
    nch = S // CHUNK

    def body(*refs):
        ins = (refs[0:6], refs[6:12])
        outs = (refs[12:15], refs[15:18])
        state = refs[18]

        @pl.when(pl.program_id(0) == 0)
        def _():
            state[...] = jnp.zeros_like(state)

        def chunk(cc, carry):
            chains = []
            for d in range(2):
                c = cc if d == 0 else ncb - 1 - cc
                r0 = pl.multiple_of(c * CHUNK, CHUNK)
                rows = pl.ds(r0, CHUNK)
                for h in range(GDN_H):
                    cols = slice(h * GDN_DK, (h + 1) * GDN_DK)
                    m = _gdn_decay(ins[d][2], None, c, rows, r0, d * GDN_H + h, d == 1, None, None)
                    chains.append(dict(d=d, h=h, c=c, rows=rows, cols=cols, m=m, st=state[d * GDN_H + h]))
            for ch in chains:
                q_ref, k_ref, bg_ref, u_ref, w_ref, a_ref = ins[ch["d"]]
                rows, cols = ch["rows"], ch["cols"]
                lhs = jnp.concatenate([w_ref[rows, cols], (q_ref[rows, cols] * ch["m"]["eg"]).astype(BF16)], axis=0)
                ch["ws_qs"] = _dot(lhs, ch["st"].astype(BF16), 1, 0)
            for ch in chains:
                q_ref, k_ref, bg_ref, u_ref, w_ref, a_ref = ins[ch["d"]]
                rows, cols = ch["rows"], ch["cols"]
                vn = u_ref[rows, cols] - ch["ws_qs"][0:CHUNK]
                vnb = vn.astype(BF16)
                ch["vn"] = vn
                ch["avn"] = _dot(a_ref[ch["c"], ch["h"]], vnb, 1, 0)
                ch["kvn"] = _bdot(k_ref[rows, cols] * ch["m"]["egl"], vnb, 0, 0)
            for ch in chains:
                o_ref, vn_ref, s_ref = outs[ch["d"]]
                rows, cols = ch["rows"], ch["cols"]
                o_ref[rows, cols] = ch["ws_qs"][CHUNK:2 * CHUNK] + ch["avn"]
                vn_ref[rows, cols] = ch["vn"]
                s_ref[ch["c"], ch["h"]] = ch["st"]
                state[ch["d"] * GDN_H + ch["h"]] = ch["st"] * ch["m"]["cd"] + ch["kvn"]
            return carry

        lax.fori_loop(0, ncb, chunk, 0)

    ins, outs = [], []
    for d in range(2):
        tix = _dir_tile(d, n_tiles, False)
        im = lambda i, tix=tix: (tix(i), 0)
        im4 = lambda i, tix=tix: (tix(i), 0, 0, 0)
        u, w, a, _ = loc[d]
        ins += [(q, (ts, GDN_W), im), (k, (ts, GDN_W), im), (bg, (ts, 128), im), (u, (ts, GDN_W), im),
                (w, (ts, GDN_W), im), (a, (ncb, GDN_H, CHUNK, CHUNK), im4)]
        outs += [(jax.ShapeDtypeStruct((S, GDN_W), F32), (ts, GDN_W), im),
                 (jax.ShapeDtypeStruct((S, GDN_W), F32), (ts, GDN_W), im),
                 (jax.ShapeDtypeStruct((nch, GDN_H, GDN_DK, GDN_DK), F32), (ncb, GDN_H, GDN_DK, GDN_DK), im4)]
    res = _rows("gdn_scan_fwd", S, ts, ins, outs, body, scratch=[pltpu.VMEM((2 * GDN_H, GDN_DK, GDN_DK), F32)])
    return res[0:3], res[3:6]


def _gelu(x):
    c = math.sqrt(2.0 / math.pi)
    t = jnp.tanh(c * (x + 0.044715 * x * x * x))
    return 0.5 * x * (1.0 + t), t


def _mix_out_fwd(h_f, h_b, o_f, o_b, p, gn):
    S = h_f.shape[0]
    ts = _tile(S, 512)

    def body(hf, hb, of, ob, gate, z, gn_ref, y_ref):
        ge, _ = _gelu(gate[...])
        y_ref[:, 0:RG_W] = ((hf[...] + hb[...]) * ge).astype(BF16)
        o = of[...] + ob[...]
        zv = z[...]
        sz = zv * _sigmoid(zv)
        for h in range(GDN_H):
            cols = slice(h * GDN_DK, (h + 1) * GDN_DK)
            oh = o[:, cols]
            n = oh * lax.rsqrt(jnp.mean(oh * oh, axis=-1, keepdims=True) + EPS) * gn_ref[...]
            y_ref[:, RG_W + h * GDN_DK:RG_W + (h + 1) * GDN_DK] = (n * sz[:, cols]).astype(BF16)

    blk = (ts, RG_W)
    im = lambda i: (i, 0)
    ins = [(h_f, blk, im), (h_b, blk, im), (o_f, blk, im), (o_b, blk, im),
           (p, blk, lambda i: (i, 1)), (p, blk, lambda i: (i, 5)), (gn, (1, GDN_DK), lambda i: (0, 0))]
    return _rows("mix_out_fwd", S, ts, ins,
                 [(jax.ShapeDtypeStruct((S, D_MODEL), BF16), (ts, D_MODEL), im)], body)[0]


def _loss_head(x, target, g):
    S, D = x.shape
    ts = _tile(S, 512)

    def body(x_ref, t_ref, g_ref, dx_ref, loss_ref, dg_ref):
        @pl.when(pl.program_id(0) == 0)
        def _():
            loss_ref[...] = jnp.zeros_like(loss_ref)
            dg_ref[...] = jnp.zeros_like(dg_ref)

        xv = x_ref[...]
        gv = g_ref[...]
        r = lax.rsqrt(jnp.mean(xv * xv, axis=-1, keepdims=True) + EPS)
        err = xv * r * gv - t_ref[...]
        loss_ref[...] += jnp.sum(err * err) * (0.5 / D)
        dx, dgt = _rmsnorm_bwd_tile(err * (1.0 / D), xv, gv)
        dx_ref[...] = dx
        dg_ref[...] += jnp.sum(dgt, axis=0, keepdims=True)

    im = lambda i: (i, 0)
    z = lambda i: (0, 0)
    return _rows("loss_head", S, ts,
                 [(x, (ts, D), im), (target, (ts, D), im), (g, (1, D), z)],
                 [(jax.ShapeDtypeStruct((S, D), F32), (ts, D), im),
                  (jax.ShapeDtypeStruct((8, 128), F32), (8, 128), z),
                  (jax.ShapeDtypeStruct((1, D), F32), (1, D), z)], body)


def _block_diag(w):
    n = w.shape[0]
    return jnp.einsum("nij,nm->nimj", w, jnp.eye(n, dtype=w.dtype)).reshape(n * w.shape[1], n * w.shape[2])


def _rg_bd(a_w, x_w):
    return jnp.concatenate([_block_diag(a_w[0]), _block_diag(x_w[0]), _block_diag(a_w[1]), _block_diag(x_w[1])],
                           axis=1).astype(BF16)


def _rg_prm(ba, bx, lam):
    return jnp.concatenate([ba[0:1], bx[0:1], ba[1:2], bx[1:2], lam, jnp.zeros((2, RG_W), F32)], axis=0)


def _gdn_prm(a_log, dt_bias):
    rows = jnp.zeros((8, 128), F32)
    rows = rows.at[0, 8:16].set(a_log.reshape(-1))
    return rows.at[1, 8:16].set(dt_bias.reshape(-1))


def _gc_rows(bg):
    S = bg.shape[0]
    return bg[:, 8:16].reshape(S // CHUNK, CHUNK, 8).transpose(0, 2, 1)


def _layer_fwd(x0, target, W, more=None):
    S = x0.shape[0]
    R = {}
    R["h1"] = _rmsnorm_fwd("rms1", x0, W["ffn1_norm"])
    R["x1"], R["a1"], R["b1"], R["f1"] = _ffn_fwd("ffn1", x0, R["h1"], W["ffn1_w_gate"], W["ffn1_w_up"], W["ffn1_w_down"])
    if more is not None:
        W = {**W, **more("mixer", R["x1"])}
    R["h2"] = _rmsnorm_fwd("rms2", R["x1"], W["mix_norm"])
    tm = _tile(S, 512)
    tmp = _tile(S, 1024)
    R["p"] = _fused_mm("in_proj", S, D_IN_PAD, D_MODEL, tmp, 640, D_MODEL, [(R["h2"], "mk"), (W["w_in"], "kn")],
                       [(0, 1, 0)], [], [(jax.ShapeDtypeStruct((S, D_IN_PAD), F32), (tmp, 640), _mn)],
                       lambda i, accs, ex, out: out[0].__setitem__(Ellipsis, accs[0][...]))[0]
    p = R["p"]
    R["xc"] = _conv_fwd("rg_conv_fwd", p, 0, W["rg_conv_w"], W["rg_conv_b"], "bias")
    R["bd"] = _rg_bd(W["rg_gate_a_w"], W["rg_gate_x_w"])
    R["rg_prm"] = _rg_prm(W["rg_gate_a_b"], W["rg_gate_x_b"], W["rg_lambda"])
    a_f, b_f, a_b, b_b = _rg_gates_fwd(R["xc"], R["bd"], R["rg_prm"])
    R["a_f"], R["a_b"] = a_f, a_b
    R["h_f"], R["h_b"] = _rg_scan("rg_scan_fwd", a_f, b_f, a_b, b_b)
    zero_b = jnp.zeros((1, RG_W), F32)
    cw = W["gdn_conv_w"]
    R["q"] = _conv_fwd("gdn_conv_q", p, 2, cw[:, 0:512], zero_b, "q")
    R["k"] = _conv_fwd("gdn_conv_k", p, 3, cw[:, 512:1024], zero_b, "k")
    R["v"] = _conv_fwd("gdn_conv_v", p, 4, cw[:, 1024:1536], zero_b, "v")
    R["gdn_prm"] = _gdn_prm(W["gdn_a_log"], W["gdn_dt_bias"])
    R["bg"] = _gdn_prep_fwd(p, R["gdn_prm"])
    R["gcr"] = _gc_rows(R["bg"])
    R["gdn_loc"] = _gdn_local_fwd(R["q"], R["k"], R["v"], R["bg"], R["gcr"])
    R["gdn_fwd"] = _gdn_scan_fwd(R["q"], R["k"], R["bg"], R["gdn_loc"])
    R["o_f"], R["o_b"] = R["gdn_fwd"][0][0], R["gdn_fwd"][1][0]
    R["y"] = _mix_out_fwd(R["h_f"], R["h_b"], R["o_f"], R["o_b"], p, W["gdn_norm"])
    R["x2"] = _fused_mm("out_proj", S, D_MODEL, D_MODEL, tm, D_MODEL, D_MODEL, [(R["y"], "mk"), (W["w_out"], "kn")],
                        [(0, 1, 0)], [(R["x1"], (tm, D_MODEL), _mn)],
                        [(jax.ShapeDtypeStruct((S, D_MODEL), F32), (tm, D_MODEL), _mn)],
                        lambda i, accs, ex, out: out[0].__setitem__(Ellipsis, ex[0][...] + accs[0][...]))[0]
    if more is not None:
        W = {**W, **more("ffn2", R["x2"])}
    R["h3"] = _rmsnorm_fwd("rms3", R["x2"], W["ffn2_norm"])
    R["x3"], R["a2"], R["b2"], R["f2"] = _ffn_fwd("ffn2", R["x2"], R["h3"], W["ffn2_w_gate"], W["ffn2_w_up"], W["ffn2_w_down"])
    R["dx3"], R["loss"], R["d_final_norm"] = _loss_head(R["x3"], target, W["final_norm"])
    R["W"] = W
    return R


def _colsum_into(ref, i, val):
    @pl.when(i == 0)
    def _():
        ref[...] = val

    @pl.when(i > 0)
    def _():
        ref[...] += val


def _ffn_bwd(tag, dout, x, g, h, a, b, f, wg, wu, wd, emit):
    S = x.shape[0]
    tm = _tile(S, 512)
    tk_s = _tile(S, 1024)
    dwd = _fused_mm(f"{tag}_dw_down", D_FF, D_MODEL, S, 1408, D_MODEL, tk_s, [(f, "km"), (dout, "kn")], [(0, 1, 0)], [],
                    [(jax.ShapeDtypeStruct((D_FF, D_MODEL), BF16), (1408, D_MODEL), _mn)],
                    lambda i, accs, ex, out: out[0].__setitem__(Ellipsis, (0.5 * accs[0][...]).astype(BF16)))[0]
    emit(down=dwd)

    def epi_act(i, accs, ex, out):
        df = 0.5 * accs[0][...]
        out[0][...] = (df * ex[1][...].astype(F32)).astype(BF16)
        out[1][...] = (df * ex[0][...].astype(F32)).astype(BF16)

    sds = jax.ShapeDtypeStruct((S, D_FF), BF16)
    da, db = _fused_mm(f"{tag}_dact", S, D_FF, D_MODEL, tm, 1408, D_MODEL, [(dout, "mk"), (wd, "nk")], [(0, 1, 0)],
                       [(a, (tm, 1408), _mn), (b, (tm, 1408), _mn)], [(sds, (tm, 1408), _mn)] * 2, epi_act)

    def epi_w2(i, accs, ex, out):
        out[0][...] = accs[0][...].astype(BF16)
        out[1][...] = accs[1][...].astype(BF16)

    sdw = jax.ShapeDtypeStruct((D_MODEL, D_FF), BF16)
    dwg, dwu = _fused_mm(f"{tag}_dw_up", D_MODEL, D_FF, S, D_MODEL, 1408, _tile(S, 512),
                         [(h, "km"), (da, "kn"), (db, "kn")], [(0, 1, 0), (0, 2, 1)], [],
                         [(sdw, (D_MODEL, 1408), _mn)] * 2, epi_w2)
    tok = emit(gate=dwg, up=dwu)
    if tok is not None:
        g = g + tok

    def epi_dx(i, accs, ex, out):
        dx, dgt = _rmsnorm_bwd_tile(accs[0][...], ex[0][...], ex[1][...])
        out[0][...] = ex[2][...] + dx
        _colsum_into(out[1], i, jnp.sum(dgt, axis=0, keepdims=True))

    dx, dg = _fused_mm(f"{tag}_dx", S, D_MODEL, D_FF, tm, D_MODEL, 1408,
                       [(da, "mk"), (wg, "nk"), (db, "mk"), (wu, "nk")], [(0, 1, 0), (2, 3, 0)],
                       [(x, (tm, D_MODEL), _mn), (g, (1, D_MODEL), _row0), (dout, (tm, D_MODEL), _mn)],
                       [(jax.ShapeDtypeStruct((S, D_MODEL), F32), (tm, D_MODEL), _mn),
                        (jax.ShapeDtypeStruct((1, D_MODEL), F32), (1, D_MODEL), _row0)], epi_dx)
    return dx, dg


def _mix_out_bwd(dy, h_f, h_b, o_f, o_b, p, gn):
    S = dy.shape[0]
    ts = _tile(S, 512)
    c0 = math.sqrt(2.0 / math.pi)

    def body(dy_ref, hf, hb, of, ob, gate, z, gn_ref, dhr_ref, dgate_ref, do_ref, dz_ref, dgn_ref):
        i = pl.program_id(0)
        gv = gate[...]
        ge, t = _gelu(gv)
        dy_rg = dy_ref[:, 0:RG_W]
        dhr_ref[...] = dy_rg * ge
        dgelu = 0.5 * (1.0 + t) + 0.5 * gv * (1.0 - t * t) * c0 * (1.0 + 3.0 * 0.044715 * gv * gv)
        dgate_ref[...] = (dy_rg * (hf[...] + hb[...]) * dgelu).astype(BF16)
        o = of[...] + ob[...]
        zv = z[...]
        sig = _sigmoid(zv)
        gnv = gn_ref[...]
        dgn = jnp.zeros((1, GDN_DK), F32)
        for h in range(GDN_H):
            cols = slice(h * GDN_DK, (h + 1) * GDN_DK)
            oh = o[:, cols]
            r = lax.rsqrt(jnp.mean(oh * oh, axis=-1, keepdims=True) + EPS)
            ohat = oh * r
            dyh = dy_ref[:, RG_W + h * GDN_DK:RG_W + (h + 1) * GDN_DK]
            zh = zv[:, cols]
            sh = sig[:, cols]
            dn = dyh * zh * sh
            dz_ref[:, cols] = (dyh * ohat * gnv * (sh * (1.0 + zh * (1.0 - sh)))).astype(BF16)
            dxn = dn * gnv
            do_ref[:, cols] = r * (dxn - ohat * jnp.mean(dxn * ohat, axis=-1, keepdims=True))
            dgn = dgn + jnp.sum(dn * ohat, axis=0, keepdims=True)
        _colsum_into(dgn_ref, i, dgn)

    blk = (ts, RG_W)
    im = lambda i: (i, 0)
    z0 = lambda i: (0, 0)
    ins = [(dy, (ts, D_MODEL), im), (h_f, blk, im), (h_b, blk, im), (o_f, blk, im), (o_b, blk, im),
           (p, blk, lambda i: (i, 1)), (p, blk, lambda i: (i, 5)), (gn, (1, GDN_DK), z0)]
    outs = [(jax.ShapeDtypeStruct((S, RG_W), F32), blk, im), (jax.ShapeDtypeStruct((S, RG_W), BF16), blk, im),
            (jax.ShapeDtypeStruct((S, RG_W), F32), blk, im), (jax.ShapeDtypeStruct((S, RG_W), BF16), blk, im),
            (jax.ShapeDtypeStruct((1, GDN_DK), F32), (1, GDN_DK), z0)]
    return _rows("mix_out_bwd", S, ts, ins, outs, body)


def _rg_scan_adj(name, a_up, b_up, a_dn, b_dn):
    S, C = a_up.shape
    ts = _tile(S, 512)
    n_tiles = S // ts

    def body(au, bu, ad, bd, mu_ref, lam_ref, carry):
        @pl.when(pl.program_id(0) == 0)
        def _():
            carry[...] = jnp.zeros_like(carry)

        n_sub = ts // SUBLANES
        rows = lax.broadcasted_iota(jnp.int32, (SUBLANES, C), 0)

        def half(a_ref, b_ref, out_ref, r0, c_in, reverse):
            a = a_ref[pl.ds(r0, SUBLANES), :]
            b = b_ref[pl.ds(r0, SUBLANES), :]
            cum_a, c0 = _scan_rows(a, a * b, reverse)
            c = c0 + cum_a * c_in
            edge = 0 if not reverse else SUBLANES - 1
            c_prev = jnp.where(rows == edge, c_in, pltpu.roll(c, SUBLANES - 1 if reverse else 1, 0))
            out_ref[pl.ds(r0, SUBLANES), :] = b + c_prev
            return c[0:1, :] if reverse else c[SUBLANES - 1:SUBLANES, :]

        def step(j, c):
            cu, cd = c
            cu = half(au, bu, mu_ref, pl.multiple_of(j * SUBLANES, SUBLANES), cu, False)
            cd = half(ad, bd, lam_ref, pl.multiple_of((n_sub - 1 - j) * SUBLANES, SUBLANES), cd, True)
            return cu, cd

        cu, cd = lax.fori_loop(0, n_sub, step, (carry[0:1, :], carry[1:2, :]), unroll=4)
        carry[0:1, :] = cu
        carry[1:2, :] = cd

    fw = lambda i: (i, 0)
    bw = lambda i: (n_tiles - 1 - i, 0)
    sds = jax.ShapeDtypeStruct((S, C), F32)
    return _rows(name, S, ts,
                 [(a_up, (ts, C), fw), (b_up, (ts, C), fw), (a_dn, (ts, C), bw), (b_dn, (ts, C), bw)],
                 [(sds, (ts, C), fw), (sds, (ts, C), bw)], body, scratch=[pltpu.VMEM((8, C), F32)])


def _halo_ex(arr, S, tm, width):
    per = tm // HALO
    last = S // HALO - 1
    return [
        (arr, (tm, width), lambda i, j: (i, 0)),
        (arr, (HALO, width), lambda i, j: (jnp.maximum(i * per - 1, 0), 0)),
        (arr, (HALO, width), lambda i, j: (jnp.minimum((i + 1) * per, last), 0)),
    ]


def _rg_gates_bwd(xc, bd, prm, lam_f, lam_b, h_f, h_b):
    S = xc.shape[0]
    tm = _tile(S, 256)
    n_tiles = S // tm

    def epi(i, accs, ex, out):
        pre = accs[0][...]
        xv = ex[0][...]
        prm_ref = ex[1]
        lams = (ex[2][...], ex[3][...])
        hprev = (_shift(_ext(ex[4], ex[5], ex[6], i, n_tiles), -1, tm),
                 _shift(_ext(ex[7], ex[8], ex[9], i, n_tiles), 1, tm))
        dxc = jnp.zeros_like(xv)
        rows = []
        dlam_rows = []
        for d in range(2):
            r, ig, sp, a, sq = _rg_gate_terms(pre, xv, prm_ref, d)
            lam = lams[d]
            da = lam * hprev[d]
            di = lam * sq * xv
            dxc = dxc + lam * sq * ig
            dsq = lam * ig * xv
            dlog_a = da * a - dsq * (a * a) / sq
            dpre_r = dlog_a * (-RG_C * sp) * r * (1.0 - r)
            dpre_i = di * ig * (1.0 - ig)
            out[0][:, d * 1024:d * 1024 + RG_W] = dpre_r.astype(BF16)
            out[0][:, d * 1024 + RG_W:(d + 1) * 1024] = dpre_i.astype(BF16)
            rows += [jnp.sum(dpre_r, axis=0, keepdims=True), jnp.sum(dpre_i, axis=0, keepdims=True)]
            dsp = jnp.sum(dlog_a * (-RG_C * r), axis=0, keepdims=True)
            dlam_rows.append(-dsp * _sigmoid(-prm_ref[4 + d:5 + d, :]))
        out[1][...] = dxc
        zero = jnp.zeros((2, RG_W), F32)
        _colsum_into(out[2], i, jnp.concatenate(rows + dlam_rows + [zero], axis=0))

    blk = (tm, RG_W)
    im = lambda i, j: (i, 0)
    extras = ([(xc, blk, im), (prm, (8, RG_W), _row0), (lam_f, blk, im), (lam_b, blk, im)]
              + _halo_ex(h_f, S, tm, RG_W) + _halo_ex(h_b, S, tm, RG_W))
    outs = [(jax.ShapeDtypeStruct((S, 4 * RG_W), BF16), (tm, 4 * RG_W), im),
            (jax.ShapeDtypeStruct((S, RG_W), F32), blk, im),
            (jax.ShapeDtypeStruct((8, RG_W), F32), (8, RG_W), _row0)]
    return _fused_mm("rg_gates_bwd", S, 4 * RG_W, RG_W, tm, 4 * RG_W, RG_W, [(xc, "mk"), (bd, "kn")], [(0, 1, 0)],
                     extras, outs, epi)


def _roll_rows(ext, off):
    if off == 0:
        return ext
    return pltpu.roll(ext, (-off) % ext.shape[0], 0)


def _conv_bwd(name, p, colblk, w, grads, mode):
    S = p.shape[0]
    ts = _tile(S, 512)
    n_tiles = S // ts
    C = w.shape[1]
    ng = len(grads)

    def body(*refs):
        p_refs = refs[0:3]
        g_refs = refs[3:3 + 3 * ng]
        w_ref = refs[3 + 3 * ng]
        dx_ref, dw_ref, db_ref = refs[4 + 3 * ng:]
        i = pl.program_id(0)
        ext_p = _ext(*p_refs, i, n_tiles)
        dn = _ext(*g_refs[0:3], i, n_tiles)
        for gi in range(1, ng):
            dn = dn + _ext(*g_refs[3 * gi:3 * gi + 3], i, n_tiles)
        if mode == "bias":
            dc = dn
        else:
            c = None
            for j in range(CONV_W):
                term = w_ref[j:j + 1, :] * _roll_rows(ext_p, j - 2)
                c = term if c is None else c + term
            sig = _sigmoid(c)
            s = c * sig
            if mode in ("q", "k"):
                scale = GDN_DK ** -0.5 if mode == "q" else 1.0
                parts = []
                for h in range(GDN_H):
                    cols = slice(h * GDN_DK, (h + 1) * GDN_DK)
                    sh = s[:, cols]
                    dnh = dn[:, cols]
                    rinv = lax.rsqrt(jnp.sum(sh * sh, axis=-1, keepdims=True) + EPS)
                    parts.append(scale * rinv * (dnh - sh * (rinv * rinv) * jnp.sum(dnh * sh, axis=-1, keepdims=True)))
                ds = jnp.concatenate(parts, axis=-1)
            else:
                ds = dn
            dc = ds * (sig * (1.0 + c * (1.0 - sig)))
        dx = None
        for j in range(CONV_W):
            term = w_ref[j:j + 1, :] * _shift(dc, 2 - j, ts)
            dx = term if dx is None else dx + term
        dx_ref[...] = dx.astype(BF16)
        dc_main = dc[HALO:HALO + ts]
        dw = jnp.concatenate([jnp.sum(dc_main * _shift(ext_p, j - 2, ts), axis=0, keepdims=True)
                              for j in range(CONV_W)], axis=0)
        _colsum_into(dw_ref, i, dw)
        _colsum_into(db_ref, i, jnp.sum(dc_main, axis=0, keepdims=True))

    ins = _halo_ins(p, S, ts, C, colblk)
    for garr in grads:
        ins += _halo_ins(garr, S, ts, C, 0)
    ins += [(w, (CONV_W, C), lambda i: (0, 0))]
    z0 = lambda i: (0, 0)
    outs = [(jax.ShapeDtypeStruct((S, C), BF16), (ts, C), lambda i: (i, 0)),
            (jax.ShapeDtypeStruct((CONV_W, C), F32), (CONV_W, C), z0),
            (jax.ShapeDtypeStruct((1, C), F32), (1, C), z0)]
    return _rows(name, S, ts, ins, outs, body)


def _gdn_scan_bwd(q, k, bg, loc, do):
    S = q.shape[0]
    ts = _tile(S, GDN_TS)
    n_tiles = S // ts
    ncb = ts // CHUNK
    nch = S // CHUNK

    def body(*refs):
        ins = (refs[0:6], refs[6:12])
        outs = (refs[12:14], refs[14:16])
        dstate = refs[16]

        @pl.when(pl.program_id(0) == 0)
        def _():
            dstate[...] = jnp.zeros_like(dstate)

        def chunk(cc, carry):
            chains = []
            for d in range(2):
                c = ncb - 1 - cc if d == 0 else cc
                r0 = pl.multiple_of(c * CHUNK, CHUNK)
                rows = pl.ds(r0, CHUNK)
                for h in range(GDN_H):
                    cols = slice(h * GDN_DK, (h + 1) * GDN_DK)
                    m = _gdn_decay(ins[d][2], None, c, rows, r0, d * GDN_H + h, d == 1, None, None)
                    chains.append(dict(d=d, h=h, c=c, rows=rows, cols=cols, m=m, dsn=dstate[d * GDN_H + h]))
            for ch in chains:
                q_ref, k_ref, bg_ref, w_ref, a_ref, do_ref = ins[ch["d"]]
                rows, cols = ch["rows"], ch["cols"]
                dob = do_ref[rows, cols].astype(BF16)
                ch["dvn"] = (_dot(a_ref[ch["c"], ch["h"]], dob, 0, 0)
                             + _bdot(k_ref[rows, cols] * ch["m"]["egl"], ch["dsn"], 1, 0))
                ch["qdo"] = _bdot(q_ref[rows, cols] * ch["m"]["eg"], dob, 0, 0)
            for ch in chains:
                w_ref = ins[ch["d"]][3]
                ch["wdvn"] = _dot(w_ref[ch["rows"], ch["cols"]], ch["dvn"].astype(BF16), 0, 0)
            for ch in chains:
                dvn_ref, ds_ref = outs[ch["d"]]
                dvn_ref[ch["rows"], ch["cols"]] = ch["dvn"]
                ds_ref[ch["c"], ch["h"]] = ch["dsn"]
                dstate[ch["d"] * GDN_H + ch["h"]] = ch["qdo"] + ch["m"]["cd"] * ch["dsn"] - ch["wdvn"]
            return carry

        lax.fori_loop(0, ncb, chunk, 0)

    ins, outs = [], []
    for d in range(2):
        tix = _dir_tile(d, n_tiles, True)
        im = lambda i, tix=tix: (tix(i), 0)
        im4 = lambda i, tix=tix: (tix(i), 0, 0, 0)
        _, w, a, _ = loc[d]
        ins += [(q, (ts, GDN_W), im), (k, (ts, GDN_W), im), (bg, (ts, 128), im), (w, (ts, GDN_W), im),
                (a, (ncb, GDN_H, CHUNK, CHUNK), im4), (do, (ts, GDN_W), im)]
        outs += [(jax.ShapeDtypeStruct((S, GDN_W), F32), (ts, GDN_W), im),
                 (jax.ShapeDtypeStruct((nch, GDN_H, GDN_DK, GDN_DK), F32), (ncb, GDN_H, GDN_DK, GDN_DK), im4)]
    res = _rows("gdn_scan_bwd", S, ts, ins, outs, body, scratch=[pltpu.VMEM((2 * GDN_H, GDN_DK, GDN_DK), F32)])
    return res[0:2], res[2:4]


def _gdn_local_bwd(q, k, v, bg, gcr, do, loc, fwd, adj):
    S = q.shape[0]
    ts = _tile(S, GDN_TS)
    ncb = ts // CHUNK

    def body(q_ref, k_ref, v_ref, bg_ref, gcr_ref, do_ref, *rest):
        per_dir = (rest[0:5], rest[5:10])
        dq_ref, dk_ref, dv_ref, dbg_ref = rest[10:14]
        ri, ci = _tri_masks()
        lane = lax.broadcasted_iota(jnp.int32, (CHUNK, 128), 1)
        rowi = lax.broadcasted_iota(jnp.int32, (CHUNK, 1), 0)
        ones = jnp.ones((CHUNK, 128), F32)

        def chunk(c, carry):
            r0 = pl.multiple_of(c * CHUNK, CHUNK)
            rows = pl.ds(r0, CHUNK)
            chains = []
            for h in range(GDN_H):
                cols = slice(h * GDN_DK, (h + 1) * GDN_DK)
                qh, kh, vh = q_ref[rows, cols], k_ref[rows, cols], v_ref[rows, cols]
                dob = do_ref[rows, cols].astype(BF16)
                both = _bdot(jnp.concatenate([qh, kh], axis=0), kh, 1, 1)
                for d in range(2):
                    chains.append(dict(h=h, d=d, cols=cols, qh=qh, kh=kh, vh=vh, dob=dob, qk=both[0:CHUNK],
                                       kk=both[CHUNK:2 * CHUNK], col=d * GDN_H + h))
            for ch in chains:
                m = _gdn_decay(bg_ref, gcr_ref, c, rows, r0, ch["col"], ch["d"] == 1, ri, ci)
                t_ref, s_ref, ds_ref, vn_ref, dvn_ref = per_dir[ch["d"]]
                h, cols = ch["h"], ch["cols"]
                ch["m"] = m
                ch["kb"] = ch["kh"] * m["beta"]
                ch["kbg"] = ch["kb"] * m["eg"]
                ch["t"] = t_ref[c, h]
                st = s_ref[c, h]
                stb = st.astype(BF16)
                ch["dsn"] = ds_ref[c, h]
                vnb = vn_ref[rows, cols].astype(BF16)
                dvnb = dvn_ref[rows, cols].astype(BF16)
                ch["dcd"] = jnp.sum(jnp.sum(st * ch["dsn"], axis=1, keepdims=True), axis=0, keepdims=True)
                ch["dqd"] = _dot(ch["dob"], stb, 1, 1)
                ch["d_a"] = _dot(ch["dob"], vnb, 1, 1)
                ch["dkd"] = _bdot(vnb, ch["dsn"], 1, 1)
                ch["dw"] = -_dot(dvnb, stb, 1, 1)
                ch["dvb"] = _dot(ch["t"], dvnb, 0, 0)
                ch["d_t"] = _bdot(dvnb, ch["vh"] * m["beta"], 1, 1)
            for ch in chains:
                dwb = ch["dw"].astype(BF16)
                ch["d_t"] = ch["d_t"] + _bdot(dwb, ch["kbg"], 1, 1)
                ch["dkbg"] = _dot(ch["t"], dwb, 0, 0)
                ch["nn"] = ch["d_a"] * ch["m"]["dm"]
                ch["nn_q"] = _bdot(ch["nn"], ch["qh"], 0, 0)
                ch["nn_k"] = _bdot(ch["nn"], ch["kh"], 1, 0)
            for ch in chains:
                ch["x"] = _dot(ch["d_t"].astype(BF16), ch["t"], 1, 1)
            for ch in chains:
                d_l = -_dot(ch["t"], ch["x"].astype(BF16), 0, 0)
                ch["d_l"] = jnp.where(ch["m"]["strict"], d_l, 0.0)
                ch["mm"] = ch["d_l"] * ch["m"]["dm"]
            for ch in chains:
                m = ch["m"]
                ch["mm_kh"] = _bdot(ch["mm"], ch["kh"], 1, 0)
                ch["mm_kb"] = _bdot(ch["mm"], ch["kb"], 0, 0)
                l_mat = jnp.where(m["strict"], m["beta"] * ch["kk"] * m["dm"], 0.0)
                ch["e"] = ch["d_l"] * l_mat + ch["nn"] * ch["qk"]
                ch["cs"] = _dot(ch["e"], ones, 0, 0, HI)[:, 0:1]
            acc_bg = jnp.zeros((CHUNK, 128), F32)
            acc = {}
            for ch in chains:
                m = ch["m"]
                beta, eg, egl = m["beta"], m["eg"], m["egl"]
                dkb = ch["mm_kh"] + ch["dkbg"] * eg
                dk_d = ch["mm_kb"] + ch["nn_q"] + ch["dkd"] * egl + dkb * beta
                dq_d = ch["nn_k"] + ch["dqd"] * eg
                dv_d = ch["dvb"] * beta
                rs = jnp.sum(ch["e"], axis=1, keepdims=True)
                dkd_kd = ch["dkd"] * (ch["kh"] * egl)
                dgc = (rs - ch["cs"] + jnp.sum(ch["dqd"] * (ch["qh"] * eg), axis=1, keepdims=True)
                       - jnp.sum(dkd_kd, axis=1, keepdims=True) + jnp.sum(ch["dkbg"] * ch["kbg"], axis=1, keepdims=True))
                dgl = jnp.sum(jnp.sum(dkd_kd, axis=1, keepdims=True), axis=0, keepdims=True) + ch["dcd"] * m["cd"]
                dgc = dgc + jnp.where(rowi == (0 if ch["d"] == 1 else CHUNK - 1), dgl, 0.0)
                dbeta = (jnp.sum(dkb * ch["kh"], axis=1, keepdims=True)
                         + jnp.sum(ch["dvb"] * ch["vh"], axis=1, keepdims=True))
                acc_bg = acc_bg + jnp.where(lane == ch["col"], dbeta, 0.0) + jnp.where(lane == 8 + ch["col"], dgc, 0.0)
                if ch["d"] == 0:
                    acc[ch["h"]] = (dq_d, dk_d, dv_d)
                else:
                    dq0, dk0, dv0 = acc[ch["h"]]
                    dq_ref[rows, ch["cols"]] = dq0 + dq_d
                    dk_ref[rows, ch["cols"]] = dk0 + dk_d
                    dv_ref[rows, ch["cols"]] = dv0 + dv_d
            dbg_ref[rows, :] = acc_bg
            return carry

        lax.fori_loop(0, ncb, chunk, 0)

    im = lambda i: (i, 0)
    im4 = lambda i: (i, 0, 0, 0)
    blk = (ts, GDN_W)
    ins = [(q, blk, im), (k, blk, im), (v, blk, im), (bg, (ts, 128), im), (gcr, (ncb, 8, CHUNK), lambda i: (i, 0, 0)),
           (do, blk, im)]
    for d in range(2):
        ins += [(loc[d][3], (ncb, GDN_H, CHUNK, CHUNK), im4), (fwd[d][2], (ncb, GDN_H, GDN_DK, GDN_DK), im4),
                (adj[d][1], (ncb, GDN_H, GDN_DK, GDN_DK), im4), (fwd[d][1], blk, im), (adj[d][0], blk, im)]
    sds = jax.ShapeDtypeStruct((S, GDN_W), F32)
    outs = [(sds, blk, im), (sds, blk, im), (sds, blk, im), (jax.ShapeDtypeStruct((S, 128), F32), (ts, 128), im)]
    return _rows("gdn_local_bwd", S, ts, ins, outs, body)


def _gdn_prep_bwd(dbg_all, p, prm):
    S = p.shape[0]
    ts = _tile(S, 512)

    def body(dbg_ref, p_ref, prm_ref, dba_ref, dprm_ref):
        i = pl.program_id(0)
        raw = p_ref[...]
        dbg = dbg_ref[...]
        lane = lax.broadcasted_iota(jnp.int32, (1, 128), 1)
        is_g = (lane >= 8) & (lane < 16)
        ea = jnp.exp(prm_ref[0:1, :])
        arg = raw + prm_ref[1:2, :]
        g = jnp.where(is_g, -ea * _softplus(arg), 0.0)
        beta = _sigmoid(raw)
        dgc = jnp.where(is_g, dbg, 0.0)
        ri, ci = _tri_masks()
        lower = (ri >= ci).astype(F32)
        upper = (ri <= ci).astype(F32)
        dgs = []
        for c in range(ts // CHUNK):
            ch = dgc[c * CHUNK:(c + 1) * CHUNK]
            dgs.append(jnp.where(lane < 12, _dot(upper, ch, 1, 0, HI), _dot(lower, ch, 1, 0, HI)))
        dg = jnp.concatenate(dgs, axis=0)
        dalpha = jnp.where(is_g, dg * (-ea) * _sigmoid(arg), 0.0)
        dba_ref[...] = jnp.where(lane < 8, dbg * beta * (1.0 - beta), dalpha).astype(BF16)
        rows = jnp.concatenate([jnp.sum(dg * g, axis=0, keepdims=True), jnp.sum(dalpha, axis=0, keepdims=True),
                                jnp.zeros((6, 128), F32)], axis=0)
        _colsum_into(dprm_ref, i, rows)

    im = lambda i: (i, 0)
    z0 = lambda i: (0, 0)
    return _rows("gdn_prep_bwd", S, ts,
                 [(dbg_all, (ts, 128), im), (p, (ts, 128), lambda i: (i, COL_BA // 128)), (prm, (8, 128), z0)],
                 [(jax.ShapeDtypeStruct((S, 128), BF16), (ts, 128), im), (jax.ShapeDtypeStruct((8, 128), F32), (8, 128), z0)],
                 body)


def _mm_plain(name, M, N, K, tm, tn, tk, a, am, b, bm, dtype):
    return _fused_mm(name, M, N, K, tm, tn, tk, [(a, am), (b, bm)], [(0, 1, 0)], [],
                     [(jax.ShapeDtypeStruct((M, N), dtype), (tm, tn), _mn)],
                     lambda i, accs, ex, out: out[0].__setitem__(Ellipsis, accs[0][...].astype(dtype)))[0]


def _layer_bwd(x0, W, R, emit_big=None):
    S = x0.shape[0]
    tm = _tile(S, 512)
    tk_s = _tile(S, 1024)
    G = {}

    def emit(**named):
        if emit_big is None:
            G.update(named)
            return None
        return emit_big(**named)

    def ffn_emit(prefix):
        return lambda **kw: emit(**{f"{prefix}_w_{k}": v for k, v in kw.items()})

    dx2, G["ffn2_norm"] = _ffn_bwd("ffn2b", R["dx3"], R["x2"], W["ffn2_norm"], R["h3"], R["a2"], R["b2"], R["f2"],
                                   W["ffn2_w_gate"], W["ffn2_w_up"], W["ffn2_w_down"], ffn_emit("ffn2"))
    tok = emit(w_out=_mm_plain("dw_out", D_MODEL, D_MODEL, S, D_MODEL, D_MODEL, tk_s, R["y"], "km", dx2, "kn", BF16))
    gn = W["gdn_norm"] if tok is None else W["gdn_norm"] + tok
    dy = _mm_plain("dy_mix", S, D_MODEL, D_MODEL, tm, D_MODEL, D_MODEL, dx2, "mk", W["w_out"], "nk", F32)
    p = R["p"]
    dhr, dgate, do, dz, G["gdn_norm"] = _mix_out_bwd(dy, R["h_f"], R["h_b"], R["o_f"], R["o_b"], p, gn)
    lam_b, lam_f = _rg_scan_adj("rg_scan_bwd", R["a_b"], dhr, R["a_f"], dhr)
    dpre, dxc_direct, d_rgprm = _rg_gates_bwd(R["xc"], R["bd"], R["rg_prm"], lam_f, lam_b, R["h_f"], R["h_b"])
    tmg = _tile(S, 512)
    dxc = _fused_mm("rg_dxc", S, RG_W, 4 * RG_W, tmg, RG_W, 4 * RG_W, [(dpre, "mk"), (R["bd"], "nk")], [(0, 1, 0)],
                    [(dxc_direct, (tmg, RG_W), _mn)], [(jax.ShapeDtypeStruct((S, RG_W), F32), (tmg, RG_W), _mn)],
                    lambda i, accs, ex, out: out[0].__setitem__(Ellipsis, ex[0][...] + accs[0][...]))[0]
    d_bd = _mm_plain("rg_dbd", RG_W, 4 * RG_W, S, RG_W, 4 * RG_W, tk_s, R["xc"], "km", dpre, "kn", F32)
    dx_rg, G["rg_conv_w"], G["rg_conv_b"] = _conv_bwd("rg_conv_bwd", p, 0, W["rg_conv_w"], [dxc], "bias")
    blocks = jnp.einsum("nigmj,nm->gnij", d_bd.reshape(RG_BLOCKS, RG_BLOCK, 4, RG_BLOCKS, RG_BLOCK),
                        jnp.eye(RG_BLOCKS, dtype=F32))
    G["rg_gate_a_w"] = jnp.stack([blocks[0], blocks[2]])
    G["rg_gate_x_w"] = jnp.stack([blocks[1], blocks[3]])
    G["rg_gate_a_b"] = jnp.stack([d_rgprm[0], d_rgprm[2]])
    G["rg_gate_x_b"] = jnp.stack([d_rgprm[1], d_rgprm[3]])
    G["rg_lambda"] = d_rgprm[4:6]
    adj = _gdn_scan_bwd(R["q"], R["k"], R["bg"], R["gdn_loc"], do)
    dq, dk, dv, dbg = _gdn_local_bwd(R["q"], R["k"], R["v"], R["bg"], R["gcr"], do, R["gdn_loc"], R["gdn_fwd"], adj)
    cw = W["gdn_conv_w"]
    dpq, dwq, _ = _conv_bwd("gdn_conv_q_bwd", p, 2, cw[:, 0:512], [dq], "q")
    dpk, dwk, _ = _conv_bwd("gdn_conv_k_bwd", p, 3, cw[:, 512:1024], [dk], "k")
    dpv, dwv, _ = _conv_bwd("gdn_conv_v_bwd", p, 4, cw[:, 1024:1536], [dv], "v")
    G["gdn_conv_w"] = jnp.concatenate([dwq, dwk, dwv], axis=1)
    dba, d_gprm = _gdn_prep_bwd(dbg, p, R["gdn_prm"])
    G["gdn_a_log"] = d_gprm[0, 8:16].reshape(2, GDN_H)
    G["gdn_dt_bias"] = d_gprm[1, 8:16].reshape(2, GDN_H)
    dp = jnp.concatenate([dx_rg, dgate, dpq, dpk, dpv, dz, dba], axis=1)
    tok = emit(w_in=_mm_plain("dw_in", D_MODEL, D_IN_PAD, S, D_MODEL, 640, tk_s, R["h2"], "km", dp, "kn", BF16))
    g_mix = W["mix_norm"] if tok is None else W["mix_norm"] + tok

    def epi_dx1(i, accs, ex, out):
        dx, dgt = _rmsnorm_bwd_tile(accs[0][...], ex[0][...], ex[1][...])
        out[0][...] = ex[2][...] + dx
        _colsum_into(out[1], i, jnp.sum(dgt, axis=0, keepdims=True))

    dx1, G["mix_norm"] = _fused_mm(
        "mix_dx", S, D_MODEL, D_IN_PAD, tm, D_MODEL, D_IN_PAD, [(dp, "mk"), (W["w_in"], "nk")], [(0, 1, 0)],
        [(R["x1"], (tm, D_MODEL), _mn), (g_mix, (1, D_MODEL), _row0), (dx2, (tm, D_MODEL), _mn)],
        [(jax.ShapeDtypeStruct((S, D_MODEL), F32), (tm, D_MODEL), _mn),
         (jax.ShapeDtypeStruct((1, D_MODEL), F32), (1, D_MODEL), _row0)], epi_dx1)
    dx0, G["ffn1_norm"] = _ffn_bwd("ffn1b", dx1, x0, W["ffn1_norm"], R["h1"], R["a1"], R["b1"], R["f1"],
                                   W["ffn1_w_gate"], W["ffn1_w_up"], W["ffn1_w_down"], ffn_emit("ffn1"))
    G["final_norm"] = R["d_final_norm"]
    return dx0, G


def _mesh_pos():
    x, y, c = lax.axis_index("x"), lax.axis_index("y"), lax.axis_index("c")
    return x, y, c, 4 * x + 2 * y + c


def _peer(x, y, c, r):
    px = 1 - x if r & 4 else x
    py = 1 - y if r & 2 else y
    pc = 1 - c if r & 1 else c
    return (px, py, pc), 4 * px + 2 * py + pc


_HBM = pl.BlockSpec(memory_space=pltpu.HBM)
_SEM = pl.BlockSpec(memory_space=pltpu.SEMAPHORE)


def _peer_copies(scatter, srcs, lands, send_sems, recv_sems):
    x, y, c, me = _mesh_pos()
    copies = []
    for a, (src, land) in enumerate(zip(srcs, lands)):
        for r in range(1, N_DEV):
            peer, peer_idx = _peer(x, y, c, r)
            copies.append(pltpu.make_async_remote_copy(
                src_ref=src.at[peer_idx] if scatter else src, dst_ref=land.at[r - 1] if scatter else land.at[me],
                send_sem=send_sems.at[a * 7 + r - 1], recv_sem=recv_sems.at[a * 7 + r - 1],
                device_id=peer, device_id_type=pl.DeviceIdType.MESH))
    return copies


def _exchange_start(name, scatter, arrays):
    slabs = arrays
    n = len(slabs)

    def body(*refs):
        srcs, lands = refs[0:n], refs[n:2 * n]
        send_sems, recv_sems = refs[2 * n], refs[2 * n + 1]
        token = refs[4 * n + 2]
        for cp in _peer_copies(scatter, srcs, lands, send_sems, recv_sems):
            cp.start()
        token[...] = jnp.zeros_like(token)

    land_shapes = [(N_DEV - 1,) + s.shape[1:] if scatter else (N_DEV,) + s.shape for s in slabs]
    out_shape = ([pltpu.SemaphoreType.DMA((7 * n,)), pltpu.SemaphoreType.DMA((7 * n,))]
                 + [pltpu.HBM(s.shape, s.dtype) for s in slabs]
                 + [pltpu.HBM(shp, s.dtype) for shp, s in zip(land_shapes, slabs)]
                 + [jax.ShapeDtypeStruct((8, 128), F32)])
    res = pl.pallas_call(
        body, name=name, out_shape=out_shape, in_specs=[_HBM] * (2 * n),
        out_specs=[_SEM, _SEM] + [_HBM] * (2 * n) + [pl.BlockSpec(memory_space=pltpu.VMEM)],
        input_output_aliases={i: 2 + i for i in range(2 * n)},
        compiler_params=pltpu.CompilerParams(has_side_effects=pltpu.SideEffectType.DATAFLOW_SIDE_EFFECTING),
    )(*[pltpu.with_memory_space_constraint(s, pltpu.HBM) for s in slabs],
      *[pltpu.with_memory_space_constraint(lax.empty(shp, s.dtype), pltpu.HBM) for shp, s in zip(land_shapes, slabs)])
    return dict(n=n, scatter=scatter, sems=res[0:2], srcs=res[2:2 + n], lands=res[2 + n:2 + 2 * n],
                token=res[2 + 2 * n][0, 0])


def _exchange_wait(name, started, after):
    n = started["n"]
    scatter = started["scatter"]

    def body(*refs):
        srcs, lands = refs[0:n], refs[n:2 * n]
        send_sems, recv_sems = refs[2 * n], refs[2 * n + 1]
        for cp in _peer_copies(scatter, srcs, lands, send_sems, recv_sems):
            cp.wait_send()
            cp.wait_recv()

    arrays = list(started["srcs"]) + list(started["lands"])
    res = pl.pallas_call(
        body, name=name, out_shape=[pltpu.HBM(a.shape, a.dtype) for a in arrays],
        in_specs=[_HBM] * (2 * n) + [_SEM, _SEM, pl.BlockSpec(memory_space=pl.ANY)], out_specs=[_HBM] * (2 * n),
        input_output_aliases={i: i for i in range(2 * n)},
        compiler_params=pltpu.CompilerParams(has_side_effects=pltpu.SideEffectType.DATAFLOW_SIDE_EFFECTING),
    )(*arrays, *started["sems"], after)
    return res[0:n], res[n:2 * n]


def _all_gather(name, arrays):
    n = len(arrays)

    def body(*refs):
        ins = refs[:n]
        outs = refs[n:2 * n]
        token = refs[2 * n]
        send_sems, recv_sems, local_sems = refs[2 * n + 1:]
        token[...] = jnp.zeros_like(token)
        x, y, c, me = _mesh_pos()
        sibling = (x, y, 1 - c)
        chips = [(1 - x, y), (x, 1 - y), (1 - x, 1 - y)]

        def idx(px, py, pc):
            return 4 * px + 2 * py + pc

        def copy(a, k, block, to, src=None):
            slot = outs[a].at[idx(*block)]
            return pltpu.make_async_remote_copy(
                src_ref=slot if src is None else src, dst_ref=slot, send_sem=send_sems.at[a * 7 + k],
                recv_sem=recv_sems.at[a * 7 + k], device_id=to, device_id_type=pl.DeviceIdType.MESH)

        locals_, sends = [], []
        for a in range(n):
            loc = pltpu.make_async_copy(ins[a], outs[a].at[me], local_sems.at[a])
            loc.start()
            locals_.append(loc)
            sends.append(copy(a, 0, (x, y, c), sibling, src=ins[a]))
            sends += [copy(a, 1 + j, (x, y, c), (*chip, c), src=ins[a]) for j, chip in enumerate(chips)]
        for cp in sends:
            cp.start()
        passed = []
        for a in range(n):
            for j, chip in enumerate(chips):
                copy(a, 1 + j, (*chip, c), (x, y, c)).wait_recv()
                fwd = copy(a, 4 + j, (*chip, c), sibling)
                fwd.start()
                passed.append(fwd)
        for a in range(n):
            copy(a, 0, sibling, (x, y, c)).wait_recv()
            for j, chip in enumerate(chips):
                copy(a, 4 + j, (*chip, 1 - c), (x, y, c)).wait_recv()
        for cp in sends + passed:
            cp.wait_send()
        for loc in locals_:
            loc.wait()

    any_spec = pl.BlockSpec(memory_space=pl.ANY)
    res = pl.pallas_call(
        body, name=name, in_specs=[any_spec] * n, out_specs=[any_spec] * n + [pl.BlockSpec(memory_space=pltpu.VMEM)],
        out_shape=[jax.ShapeDtypeStruct((N_DEV,) + a.shape, a.dtype) for a in arrays]
        + [jax.ShapeDtypeStruct((8, 128), F32)],
        scratch_shapes=[pltpu.SemaphoreType.DMA((7 * n,)), pltpu.SemaphoreType.DMA((7 * n,)),
                        pltpu.SemaphoreType.DMA((n,))],
        compiler_params=pltpu.CompilerParams(has_side_effects=True),
    )(*arrays)
    return res[:n], res[n][0, 0]


def _adamw_math(w, g, m, v):
    m2 = ADAM_B1 * m + (1.0 - ADAM_B1) * g
    v2 = ADAM_B2 * v + (1.0 - ADAM_B2) * (g * g)
    m_hat = m2 / (1.0 - ADAM_B1 ** ADAM_STEP)
    v_hat = v2 / (1.0 - ADAM_B2 ** ADAM_STEP)
    delta = -ADAM_LR * (m_hat / (jnp.sqrt(v_hat) + ADAM_EPS) + ADAM_WD * w)
    return delta, m2, v2


def _adamw_slabs(name, src, land, me, w, m, v, tr):
    R, C = w.shape

    def body(me_ref, own_ref, land_ref, w_ref, m_ref, v_ref, g_ref, d_ref, m2_ref, v2_ref):
        g = own_ref[0].astype(F32)
        for s in range(N_DEV - 1):
            g = g + land_ref[s].astype(F32)
        delta, m2, v2 = _adamw_math(w_ref[...], g, m_ref[...], v_ref[...])
        g_ref[...] = g
        d_ref[...] = delta
        m2_ref[...] = m2
        v2_ref[...] = v2

    im = lambda i, me_ref: (i, 0)
    grid_spec = pltpu.PrefetchScalarGridSpec(
        num_scalar_prefetch=1, grid=(R // tr,),
        in_specs=[pl.BlockSpec((1, tr, C), lambda i, me_ref: (me_ref[0], i, 0)),
                  pl.BlockSpec((N_DEV - 1, tr, C), lambda i, me_ref: (0, i, 0)),
                  pl.BlockSpec((tr, C), im), pl.BlockSpec((tr, C), im), pl.BlockSpec((tr, C), im)],
        out_specs=[pl.BlockSpec((tr, C), im)] * 4)
    return pl.pallas_call(body, name=name, grid_spec=grid_spec, out_shape=[jax.ShapeDtypeStruct((R, C), F32)] * 4,
                          compiler_params=_cp(1))(me.reshape(1).astype(jnp.int32), src, land, w, m, v)


def _sum_slots(name, slots):
    _, R, C = slots.shape

    def body(s_ref, o_ref):
        g = s_ref[0]
        for s in range(1, N_DEV):
            g = g + s_ref[s]
        o_ref[...] = g

    return _rows(name, R, R, [(slots, (N_DEV, R, C), lambda i: (0, 0, 0))],
                 [(jax.ShapeDtypeStruct((R, C), F32), (R, C), lambda i: (0, 0))], body)[0]


def _adamw_packed(name, g, w, m, v):
    R, C = g.shape

    def body(g_ref, w_ref, m_ref, v_ref, d_ref, m2_ref, v2_ref):
        delta, m2, v2 = _adamw_math(w_ref[...], g_ref[...], m_ref[...], v_ref[...])
        d_ref[...] = delta
        m2_ref[...] = m2
        v2_ref[...] = v2

    im = lambda i: (0, 0)
    sds = jax.ShapeDtypeStruct((R, C), F32)
    return _rows(name, R, R, [(a, (R, C), im) for a in (g, w, m, v)], [(sds, (R, C), im)] * 3, body)


def _pack(arrays):
    rows = []
    for a in arrays:
        flat = a.reshape(-1).astype(F32)
        pad = (-flat.shape[0]) % 128
        rows.append(jnp.pad(flat, (0, pad)).reshape(-1, 128))
    out = jnp.concatenate(rows, axis=0)
    return jnp.pad(out, ((0, (-out.shape[0]) % 8), (0, 0)))


def _unpack(packed, shapes):
    lead = packed.shape[:-2]
    outs = []
    r = 0
    for shp in shapes:
        n = math.prod(shp)
        nr = -(-n // 128)
        flat = packed[..., r:r + nr, :].reshape(lead + (nr * 128,))[..., :n]
        outs.append(flat.reshape(lead + tuple(shp)))
        r += nr
    return outs


FFN1_BIG = ["ffn1_w_gate", "ffn1_w_up", "ffn1_w_down"]
MIX_BIG = ["w_in", "w_out"]
FFN2_BIG = ["ffn2_w_gate", "ffn2_w_up", "ffn2_w_down"]
BIG = FFN1_BIG + MIX_BIG + FFN2_BIG
COL_SHARDED = {"ffn1_w_gate", "ffn1_w_up", "w_in", "ffn2_w_gate", "ffn2_w_up"}
SMALL_SHARDED = ["rg_conv_w", "rg_gate_a_b", "rg_gate_x_b", "rg_lambda", "gdn_conv_w"]
WEIGHTS = ["ffn1_norm", "ffn1_w_gate", "ffn1_w_up", "ffn1_w_down", "mix_norm", "w_in", "w_out", "rg_conv_w", "rg_conv_b",
           "rg_gate_a_w", "rg_gate_a_b", "rg_gate_x_w", "rg_gate_x_b", "rg_lambda", "gdn_conv_w", "gdn_a_log",
           "gdn_dt_bias", "gdn_norm", "ffn2_norm", "ffn2_w_gate", "ffn2_w_up", "ffn2_w_down", "final_norm"]
SMALL = [n for n in WEIGHTS if n not in BIG]
ROW_VECTORS = {"ffn1_norm", "mix_norm", "ffn2_norm", "gdn_norm", "rg_conv_b", "final_norm"}
ROW_TILE = {"ffn1_w_gate": 256, "ffn1_w_up": 256, "ffn1_w_down": 176, "w_in": 256, "w_out": 64,
            "ffn2_w_gate": 256, "ffn2_w_up": 256, "ffn2_w_down": 176}


def _unshard_cols(g):
    return g.transpose(1, 0, 2).reshape(g.shape[1], N_DEV * g.shape[2])


def _to_slabs(name, g):
    if name in COL_SHARDED:
        r, ctot = g.shape
        return g.reshape(r, N_DEV, ctot // N_DEV).transpose(1, 0, 2)
    return g.reshape(N_DEV, g.shape[0] // N_DEV, g.shape[1])


def _step(x, target, w, m, v):
    _, _, _, me = _mesh_pos()
    def unshard(n, gth):
        full = _unshard_cols(gth) if n in COL_SHARDED else gth.reshape(-1, gth.shape[-1])
        return jnp.pad(full, ((0, 0), (0, D_IN_PAD - D_IN))) if n == "w_in" else full

    def landed(started, name, after):
        srcs, lands = _exchange_wait(name, started, after)
        def with_own(src, land):
            slot = lax.broadcasted_iota(jnp.int32, (N_DEV,) + (1,) * src.ndim, 0)
            return jnp.where(slot == me, src[None], land)

        return [with_own(src, land) for src, land in zip(srcs, lands)]

    first, tok = _all_gather("gather_ffn1", [w[n].astype(BF16) for n in FFN1_BIG])
    W = {n: unshard(n, gth) for n, gth in zip(FFN1_BIG, first)}
    small_shards = [w[n] for n in SMALL_SHARDED]
    st_mix = _exchange_start("gather_mix_start", False,
                             [(w[n] + tok).astype(BF16) for n in MIX_BIG] + [_pack(small_shards) + tok])
    st_ffn2 = _exchange_start("gather_ffn2_start", False, [(w[n] + tok).astype(BF16) for n in FFN2_BIG])
    for n in SMALL:
        if n not in SMALL_SHARDED:
            W[n] = w[n]
    W["ffn1_norm"] = w["ffn1_norm"] + (st_mix["token"] + st_ffn2["token"])

    def more(stage, after):
        if stage == "ffn2":
            return {n: unshard(n, gth) for n, gth in zip(FFN2_BIG, landed(st_ffn2, "gather_ffn2_wait", after))}
        got = landed(st_mix, "gather_mix_wait", after)
        new = {n: unshard(n, gth) for n, gth in zip(MIX_BIG, got)}
        for n, gth in zip(SMALL_SHARDED, _unpack(got[-1], [s.shape for s in small_shards])):
            new[n] = jnp.moveaxis(gth, 0, -2).reshape(gth.shape[1:-1] + (N_DEV * gth.shape[-1],))
        return new

    R = _layer_fwd(x, target, W, more)
    W = R["W"]
    pending = []

    def emit_big(**named):
        slabs = [_to_slabs(n, g[:, :D_IN] if n == "w_in" else g) for n, g in named.items()]
        started = _exchange_start(f"scatter_start_{len(pending)}", True, slabs)
        pending.append((list(named), started))
        return started["token"]

    grad_x, G = _layer_bwd(x, W, R, emit_big)
    loss = lax.psum(R["loss"][0, 0], ("x", "y", "c"))
    out = {}
    for i, (names, started) in enumerate(pending):
        srcs, lands = _exchange_wait(f"scatter_wait_{i}", started, grad_x)
        for n, src, land in zip(names, srcs, lands):
            out[n] = _adamw_slabs(f"adamw_{n}", src, land, me, w[n], m[n], v[n], ROW_TILE[n])
    full_shapes = [G[n].shape for n in SMALL]
    slots = _all_gather("gather_small_grads", [_pack([G[n] for n in SMALL])])[0][0]
    reduced = dict(zip(SMALL, _unpack(_sum_slots("sum_small_grads", slots), full_shapes)))
    g_small = []
    for n in SMALL:
        g = reduced[n]
        if n in SMALL_SHARDED:
            per = g.shape[-1] // N_DEV
            g = lax.dynamic_slice_in_dim(g, me * per, per, axis=g.ndim - 1)
        g_small.append(g.reshape(w[n].shape))
    shapes = [w[n].shape for n in SMALL]
    d_p, m_p, v_p = _adamw_packed("adamw_small", _pack(g_small), _pack([w[n] for n in SMALL]),
                                  _pack([m[n] for n in SMALL]), _pack([v[n] for n in SMALL]))
    for n, g, d_, m_, v_ in zip(SMALL, g_small, _unpack(d_p, shapes), _unpack(m_p, shapes), _unpack(v_p, shapes)):
        out[n] = (g, d_, m_, v_)
    return loss, grad_x, out


def kernel(x, ffn1_norm, ffn1_w_gate, ffn1_w_up, ffn1_w_down, mix_norm, w_in, w_out, rg_conv_w, rg_conv_b, rg_gate_a_w, rg_gate_a_b, rg_gate_x_w, rg_gate_x_b, rg_lambda, gdn_conv_w, gdn_a_log, gdn_dt_bias, gdn_norm, ffn2_norm, ffn2_w_gate, ffn2_w_up, ffn2_w_down, final_norm, loss_target, m_ffn1_norm, m_ffn1_w_gate, m_ffn1_w_up, m_ffn1_w_down, m_mix_norm, m_w_in, m_w_out, m_rg_conv_w, m_rg_conv_b, m_rg_gate_a_w, m_rg_gate_a_b, m_rg_gate_x_w, m_rg_gate_x_b, m_rg_lambda, m_gdn_conv_w, m_gdn_a_log, m_gdn_dt_bias, m_gdn_norm, m_ffn2_norm, m_ffn2_w_gate, m_ffn2_w_up, m_ffn2_w_down, m_final_norm, v_ffn1_norm, v_ffn1_w_gate, v_ffn1_w_up, v_ffn1_w_down, v_mix_norm, v_w_in, v_w_out, v_rg_conv_w, v_rg_conv_b, v_rg_gate_a_w, v_rg_gate_a_b, v_rg_gate_x_w, v_rg_gate_x_b, v_rg_lambda, v_gdn_conv_w, v_gdn_a_log, v_gdn_dt_bias, v_gdn_norm, v_ffn2_norm, v_ffn2_w_gate, v_ffn2_w_up, v_ffn2_w_down, v_final_norm):
    args = dict(locals())
    orig_shapes = {n: args[n].shape for n in WEIGHTS}

    def local(prefix):
        d = {}
        for n in WEIGHTS:
            a = args[prefix + n]
            d[n] = a.reshape(1, -1) if n in ROW_VECTORS else a[0]
        return d

    loss, grad_x, out = _step(x[0], loss_target[0], local(""), local("m_"), local("v_"))
    res = [loss, grad_x[None]]
    for k in range(4):
        res += [out[n][k].reshape(orig_shapes[n]) for n in WEIGHTS]
    return tuple(res)
```

```python
import functools
import math

import jax
import jax.numpy as jnp
from jax import lax
from jax.experimental import pallas as pl
from jax.experimental.pallas import tpu as pltpu

F32, BF16 = jnp.float32, jnp.bfloat16

D_MODEL = 1024
D_FF = 2816
RG_W = 512
RG_BLOCKS = 8
RG_BLOCK = 64
RG_C = 8.0
CONV_W = 4
GDN_H = 4
GDN_DK = 128
CHUNK = 64
EPS = 1e-6
D_IN = 3088
D_IN_PAD = 3200
COL_BA = 3072
N_DEV = 8
HALO = 8
VMEM_LIMIT = 48 * 1024 * 1024

ADAM_LR = 0.001
ADAM_B1 = 0.9
ADAM_B2 = 0.999
ADAM_EPS = 1e-08
ADAM_WD = 0.01
ADAM_STEP = 10

HI = lax.Precision.HIGHEST


def _cp(n):
    return pltpu.CompilerParams(dimension_semantics=("arbitrary",) * n, vmem_limit_bytes=VMEM_LIMIT)


def _tile(n, pref):
    return min(n, pref)


def _sigmoid(x):
    return 0.5 * jnp.tanh(0.5 * x) + 0.5


def _softplus(x):
    return jnp.maximum(x, 0.0) + jnp.log(1.0 + jnp.exp(-jnp.abs(x)))


def _dot(a, b, ca, cb, prec=None):
    return lax.dot_general(a, b, (((ca,), (cb,)), ((), ())), preferred_element_type=F32, precision=prec)


def _fused_mm(name, M, N, K, tm, tn, tk, ops, pairs, extras, outs, epilogue):
    nm, nn, nk = M // tm, N // tn, K // tk
    assert nm * tm == M and nn * tn == N and nk * tk == K, (name, M, N, K, tm, tn, tk)
    spec_of = {
        "mk": pl.BlockSpec((tm, tk), lambda i, j, k: (i, k)),
        "km": pl.BlockSpec((tk, tm), lambda i, j, k: (k, i)),
        "kn": pl.BlockSpec((tk, tn), lambda i, j, k: (k, j)),
        "nk": pl.BlockSpec((tn, tk), lambda i, j, k: (j, k)),
    }
    in_specs = [spec_of[m] for _, m in ops]
    in_specs += [pl.BlockSpec(bs, lambda i, j, k, im=im: im(i, j)) for _, bs, im in extras]
    out_specs = [pl.BlockSpec(bs, lambda i, j, k, im=im: im(i, j)) for _, bs, im in outs]
    n_ops, n_ex, n_out = len(ops), len(extras), len(outs)
    n_acc = 1 + max(g for _, _, g in pairs)
    modes = [m for _, m in ops]

    def body(*refs):
        op_refs = refs[:n_ops]
        ex_refs = refs[n_ops:n_ops + n_ex]
        out_refs = refs[n_ops + n_ex:n_ops + n_ex + n_out]
        accs = refs[n_ops + n_ex + n_out:]
        i = pl.program_id(0)
        k = pl.program_id(2)
        vals = [r[...].astype(BF16) for r in op_refs]

        def dot_of(pair):
            ia, ib, _ = pair
            return _dot(vals[ia], vals[ib], 1 if modes[ia] == "mk" else 0, 0 if modes[ib] == "kn" else 1)

        if nk == 1:
            sums = [None] * n_acc
            for pair in pairs:
                sums[pair[2]] = dot_of(pair) if sums[pair[2]] is None else sums[pair[2]] + dot_of(pair)
            epilogue(i, [_Held(s) for s in sums], ex_refs, out_refs)
            return

        @pl.when(k == 0)
        def _():
            for a in accs:
                a[...] = jnp.zeros_like(a)

        for pair in pairs:
            accs[pair[2]][...] += dot_of(pair)

        @pl.when(k == nk - 1)
        def _():
            epilogue(i, accs, ex_refs, out_refs)

    res = pl.pallas_call(
        body, name=name, grid=(nm, nn, nk), in_specs=in_specs, out_specs=out_specs,
        out_shape=[o for o, _, _ in outs],
        scratch_shapes=[pltpu.VMEM((tm, tn), F32)] * (n_acc if nk > 1 else 0),
        compiler_params=_cp(3),
    )(*[a for a, _ in ops], *[a for a, _, _ in extras])
    return res


class _Held:
    def __init__(self, value):
        self.value = value

    def __getitem__(self, idx):
        return self.value[idx]


def _mn(i, j):
    return (i, j)


def _row0(i, j):
    return (0, 0)


def _rows(name, S, ts, ins, outs, body, scratch=()):
    return pl.pallas_call(
        body, name=name, grid=(S // ts,),
        in_specs=[pl.BlockSpec(bs, im) for _, bs, im in ins],
        out_specs=[pl.BlockSpec(bs, im) for _, bs, im in outs],
        out_shape=[o for o, _, _ in outs],
        scratch_shapes=list(scratch),
        compiler_params=_cp(1),
    )(*[a for a, _, _ in ins])


def _halo_ins(arr, S, ts, width, colblk):
    per = ts // HALO
    last = S // HALO - 1
    return [
        (arr, (ts, width), lambda i: (i, colblk)),
        (arr, (HALO, width), lambda i: (jnp.maximum(i * per - 1, 0), colblk)),
        (arr, (HALO, width), lambda i: (jnp.minimum((i + 1) * per, last), colblk)),
    ]


def _ext(main_ref, prev_ref, next_ref, i, n_tiles):
    prev = jnp.where(i > 0, prev_ref[...].astype(F32), 0.0)
    nxt = jnp.where(i < n_tiles - 1, next_ref[...].astype(F32), 0.0)
    return jnp.concatenate([prev, main_ref[...].astype(F32), nxt], axis=0)


def _shift(ext, off, ts):
    n = ext.shape[0]
    if off == 0:
        return ext[HALO:HALO + ts]
    return pltpu.roll(ext, (-off) % n, 0)[HALO:HALO + ts]


def _rmsnorm_fwd(name, x, g):
    S, D = x.shape
    ts = _tile(S, 512)

    def body(x_ref, g_ref, o_ref):
        xv = x_ref[...]
        r = lax.rsqrt(jnp.mean(xv * xv, axis=-1, keepdims=True) + EPS)
        o_ref[...] = (xv * r * g_ref[...]).astype(BF16)

    return _rows(name, S, ts,
                 [(x, (ts, D), lambda i: (i, 0)), (g, (1, D), lambda i: (0, 0))],
                 [(jax.ShapeDtypeStruct((S, D), BF16), (ts, D), lambda i: (i, 0))], body)[0]


def _rmsnorm_bwd_tile(dh, x, g):
    r = lax.rsqrt(jnp.mean(x * x, axis=-1, keepdims=True) + EPS)
    xhat = x * r
    dxn = dh * g
    dx = r * (dxn - xhat * jnp.mean(dxn * xhat, axis=-1, keepdims=True))
    return dx, dh * xhat


def _ffn_fwd(tag, x, h, wg, wu, wd):
    S = x.shape[0]
    tm = _tile(S, 512)
    tn = 1408

    def epi_up(i, accs, ex, out):
        a = accs[0][...]
        b = accs[1][...]
        s = _sigmoid(a)
        sa = a * s
        out[0][...] = sa.astype(BF16)
        out[1][...] = (b * (s * (1.0 + a * (1.0 - s)))).astype(BF16)
        out[2][...] = (sa * b).astype(BF16)

    sds = jax.ShapeDtypeStruct((S, D_FF), BF16)
    a, b, f = _fused_mm(f"{tag}_up", S, D_FF, D_MODEL, tm, tn, D_MODEL,
                        [(h, "mk"), (wg, "kn"), (wu, "kn")], [(0, 1, 0), (0, 2, 1)], [],
                        [(sds, (tm, tn), _mn)] * 3, epi_up)

    def epi_down(i, accs, ex, out):
        out[0][...] = ex[0][...] + 0.5 * accs[0][...]

    xo = _fused_mm(f"{tag}_down", S, D_MODEL, D_FF, tm, D_MODEL, 1408,
                   [(f, "mk"), (wd, "kn")], [(0, 1, 0)], [(x, (tm, D_MODEL), _mn)],
                   [(jax.ShapeDtypeStruct((S, D_MODEL), F32), (tm, D_MODEL), _mn)], epi_down)[0]
    return xo, a, b, f


def _conv_taps(ext, w_ref, ts):
    acc = None
    for j in range(CONV_W):
        term = w_ref[j:j + 1, :] * _shift(ext, j - 2, ts)
        acc = term if acc is None else acc + term
    return acc


def _l2norm_heads(s, scale):
    outs = []
    for h in range(GDN_H):
        sh = s[:, h * GDN_DK:(h + 1) * GDN_DK]
        outs.append(sh * (lax.rsqrt(jnp.sum(sh * sh, axis=-1, keepdims=True) + EPS) * scale))
    return jnp.concatenate(outs, axis=-1)


def _conv_fwd(name, p, colblk, w, bias, mode):
    S = p.shape[0]
    ts = _tile(S, 512)
    n_tiles = S // ts
    C = w.shape[1]

    def body(main, prev, nxt, w_ref, b_ref, o_ref):
        i = pl.program_id(0)
        c = _conv_taps(_ext(main, prev, nxt, i, n_tiles), w_ref, ts)
        if mode == "bias":
            o_ref[...] = c + b_ref[...]
        else:
            s = c * _sigmoid(c)
            if mode == "q":
                s = _l2norm_heads(s, GDN_DK ** -0.5)
            elif mode == "k":
                s = _l2norm_heads(s, 1.0)
            o_ref[...] = s

    ins = _halo_ins(p, S, ts, C, colblk) + [(w, (CONV_W, C), lambda i: (0, 0)), (bias, (1, C), lambda i: (0, 0))]
    return _rows(name, S, ts, ins, [(jax.ShapeDtypeStruct((S, C), F32), (ts, C), lambda i: (i, 0))], body)[0]


def _rg_gate_terms(pre, xc, prm_ref, d):
    r = _sigmoid(pre[:, d * 1024:d * 1024 + RG_W] + prm_ref[2 * d:2 * d + 1, :])
    ig = _sigmoid(pre[:, d * 1024 + RG_W:(d + 1) * 1024] + prm_ref[2 * d + 1:2 * d + 2, :])
    sp = _softplus(-prm_ref[4 + d:5 + d, :])
    log_a = -RG_C * r * sp
    a = jnp.exp(log_a)
    t = jnp.tanh(log_a)
    sq = jnp.sqrt(-2.0 * t / (1.0 - t))
    return r, ig, sp, a, sq


def _rg_gates_fwd(xc, bd, prm):
    S = xc.shape[0]
    tm = _tile(S, 256)

    def epi(i, accs, ex, out):
        pre = accs[0][...]
        xv = ex[0][...]
        for d in range(2):
            r, ig, sp, a, sq = _rg_gate_terms(pre, xv, ex[1], d)
            out[2 * d][...] = a
            out[2 * d + 1][...] = sq * ig * xv

    sds = jax.ShapeDtypeStruct((S, RG_W), F32)
    blk = (tm, RG_W)
    im = lambda i, j: (i, 0)
    return _fused_mm("rg_gates_fwd", S, 4 * RG_W, RG_W, tm, 4 * RG_W, RG_W,
                     [(xc, "mk"), (bd, "kn")], [(0, 1, 0)],
                     [(xc, blk, im), (prm, (8, RG_W), _row0)], [(sds, blk, im)] * 4, epi)


SUBLANES = 8


def _scan_rows(a, b, reverse):
    rows = lax.broadcasted_iota(jnp.int32, a.shape, 0)
    s = 1
    while s < SUBLANES:
        shift = SUBLANES - s if reverse else s
        a_sh = pltpu.roll(a, shift, 0)
        b_sh = pltpu.roll(b, shift, 0)
        valid = (rows < SUBLANES - s) if reverse else (rows >= s)
        b = jnp.where(valid, a * b_sh + b, b)
        a = jnp.where(valid, a * a_sh, a)
        s *= 2
    return a, b


def _rg_scan(name, a_f, b_f, a_b, b_b):
    S, C = a_f.shape
    ts = _tile(S, 512)
    n_tiles = S // ts

    def body(af, bf, ab, bb, hf, hb, carry):
        @pl.when(pl.program_id(0) == 0)
        def _():
            carry[...] = jnp.zeros_like(carry)

        n_sub = ts // SUBLANES

        def step(j, c):
            cf, cb = c
            r0 = pl.multiple_of(j * SUBLANES, SUBLANES)
            cum_a, h0 = _scan_rows(af[pl.ds(r0, SUBLANES), :], bf[pl.ds(r0, SUBLANES), :], False)
            h = h0 + cum_a * cf
            hf[pl.ds(r0, SUBLANES), :] = h
            cf = h[SUBLANES - 1:SUBLANES, :]
            r1 = pl.multiple_of((n_sub - 1 - j) * SUBLANES, SUBLANES)
            cum_a, h0 = _scan_rows(ab[pl.ds(r1, SUBLANES), :], bb[pl.ds(r1, SUBLANES), :], True)
            h = h0 + cum_a * cb
            hb[pl.ds(r1, SUBLANES), :] = h
            cb = h[0:1, :]
            return cf, cb

        cf, cb = lax.fori_loop(0, n_sub, step, (carry[0:1, :], carry[1:2, :]), unroll=4)
        carry[0:1, :] = cf
        carry[1:2, :] = cb

    fw = lambda i: (i, 0)
    bw = lambda i: (n_tiles - 1 - i, 0)
    sds = jax.ShapeDtypeStruct((S, C), F32)
    return _rows(name, S, ts,
                 [(a_f, (ts, C), fw), (b_f, (ts, C), fw), (a_b, (ts, C), bw), (b_b, (ts, C), bw)],
                 [(sds, (ts, C), fw), (sds, (ts, C), bw)], body, scratch=[pltpu.VMEM((8, C), F32)])


def _tri_masks():
    ri = lax.broadcasted_iota(jnp.int32, (CHUNK, CHUNK), 0)
    ci = lax.broadcasted_iota(jnp.int32, (CHUNK, CHUNK), 1)
    return ri, ci


def _gdn_prep_fwd(p, prm):
    S = p.shape[0]
    ts = _tile(S, 512)

    def body(p_ref, prm_ref, o_ref):
        raw = p_ref[...]
        lane = lax.broadcasted_iota(jnp.int32, (1, 128), 1)
        g = -jnp.exp(prm_ref[0:1, :]) * _softplus(raw + prm_ref[1:2, :])
        g = jnp.where((lane >= 8) & (lane < 16), g, 0.0)
        beta = _sigmoid(raw)
        ri, ci = _tri_masks()
        lower = (ri >= ci).astype(F32)
        upper = (ri <= ci).astype(F32)
        for c in range(ts // CHUNK):
            rows = slice(c * CHUNK, (c + 1) * CHUNK)
            gch = g[rows]
            gc = jnp.where(lane < 12, _dot(lower, gch, 1, 0, HI), _dot(upper, gch, 1, 0, HI))
            o_ref[rows, :] = jnp.where(lane < 8, beta[rows], gc)

    return _rows("gdn_prep_fwd", S, ts,
                 [(p, (ts, 128), lambda i: (i, COL_BA // 128)), (prm, (8, 128), lambda i: (0, 0))],
                 [(jax.ShapeDtypeStruct((S, 128), F32), (ts, 128), lambda i: (i, 0))], body)[0]


def _bdot(a, b, ca, cb):
    return _dot(a.astype(BF16), b.astype(BF16), ca, cb)


GDN_W = GDN_H * GDN_DK
GDN_TS = 256


def _gdn_decay(bg_ref, gcr_ref, c, rows, r0, col, rev, ri, ci):
    beta = bg_ref[rows, col:col + 1]
    gc = bg_ref[rows, 8 + col:9 + col]
    last = 0 if rev else CHUNK - 1
    gl = bg_ref[pl.ds(r0 + last, 1), 8 + col:9 + col]
    out = dict(beta=beta, gc=gc, gl=gl, eg=jnp.exp(gc), egl=jnp.exp(gl - gc), cd=jnp.exp(gl))
    if gcr_ref is not None:
        incl = (ri <= ci) if rev else (ri >= ci)
        out["strict"] = (ri < ci) if rev else (ri > ci)
        out["dm"] = jnp.where(incl, jnp.exp(jnp.where(incl, gc - gcr_ref[c, col:col + 1, :], 0.0)), 0.0)
    return out


def _dir_tile(d, n_tiles, flip):
    if (d == 1) != flip:
        return lambda i: n_tiles - 1 - i
    return lambda i: i


def _gdn_local_fwd(q, k, v, bg, gcr):
    S = q.shape[0]
    ts = _tile(S, GDN_TS)
    ncb = ts // CHUNK
    nch = S // CHUNK

    def body(q_ref, k_ref, v_ref, bg_ref, gcr_ref, u0, w0, a0, t0, u1, w1, a1, t1):
        ri, ci = _tri_masks()
        eye = (ri == ci).astype(F32)
        outs = ((u0, w0, a0, t0), (u1, w1, a1, t1))

        def chunk(c, carry):
            r0 = pl.multiple_of(c * CHUNK, CHUNK)
            rows = pl.ds(r0, CHUNK)
            chains = []
            for h in range(GDN_H):
                cols = slice(h * GDN_DK, (h + 1) * GDN_DK)
                qh, kh, vh = q_ref[rows, cols], k_ref[rows, cols], v_ref[rows, cols]
                both = _bdot(jnp.concatenate([qh, kh], axis=0), kh, 1, 1)
                for d in range(2):
                    chains.append(dict(h=h, d=d, cols=cols, kh=kh, vh=vh, qk=both[0:CHUNK], kk=both[CHUNK:2 * CHUNK]))
            for ch in chains:
                m = _gdn_decay(bg_ref, gcr_ref, c, rows, r0, ch["d"] * GDN_H + ch["h"], ch["d"] == 1, ri, ci)
                ch["m"] = m
                ch["x"] = -jnp.where(m["strict"], m["beta"] * ch["kk"] * m["dm"], 0.0)
                ch["t"] = eye + ch["x"]
            for ch in chains:
                ch["pw"] = _bdot(ch["x"], ch["x"], 1, 0)
            for level in range(1, 6):
                last_level = level == 5
                for ch in chains:
                    rhs = ch["t"] if last_level else jnp.concatenate([ch["t"], ch["pw"]], axis=1)
                    ch["prod"] = _bdot(ch["pw"], rhs, 1, 0)
                for ch in chains:
                    ch["t"] = ch["t"] + ch["prod"][:, 0:CHUNK]
                    if not last_level:
                        ch["pw"] = ch["prod"][:, CHUNK:2 * CHUNK]
            for ch in chains:
                m = ch["m"]
                rhs = jnp.concatenate([ch["vh"] * m["beta"], ch["kh"] * (m["beta"] * m["eg"])], axis=1)
                ch["uw"] = _bdot(ch["t"], rhs, 1, 0)
            for ch in chains:
                u_ref, w_ref, a_ref, t_ref = outs[ch["d"]]
                u_ref[rows, ch["cols"]] = ch["uw"][:, 0:GDN_DK]
                w_ref[rows, ch["cols"]] = ch["uw"][:, GDN_DK:2 * GDN_DK].astype(BF16)
                a_ref[c, ch["h"]] = (ch["qk"] * ch["m"]["dm"]).astype(BF16)
                t_ref[c, ch["h"]] = _bdot(ch["t"], eye, 0, 0).astype(BF16)
            return carry

        lax.fori_loop(0, ncb, chunk, 0)

    im = lambda i: (i, 0)
    im4 = lambda i: (i, 0, 0, 0)
    ins = [(q, (ts, GDN_W), im), (k, (ts, GDN_W), im), (v, (ts, GDN_W), im), (bg, (ts, 128), im),
           (gcr, (ncb, 8, CHUNK), lambda i: (i, 0, 0))]
    per_dir = [(jax.ShapeDtypeStruct((S, GDN_W), F32), (ts, GDN_W), im),
               (jax.ShapeDtypeStruct((S, GDN_W), BF16), (ts, GDN_W), im),
               (jax.ShapeDtypeStruct((nch, GDN_H, CHUNK, CHUNK), BF16), (ncb, GDN_H, CHUNK, CHUNK), im4),
               (jax.ShapeDtypeStruct((nch, GDN_H, CHUNK, CHUNK), BF16), (ncb, GDN_H, CHUNK, CHUNK), im4)]
    res = _rows("gdn_local_fwd", S, ts, ins, per_dir * 2, body)
    return res[0:4], res[4:8]


def _gdn_scan_fwd(q, k, bg, loc):
    S = q.shape[0]
    ts = _tile(S, GDN_TS)
    n_tiles = S // ts
    ncb = ts // CHUNK
    nch = S // CHUNK

    def body(*refs):
        ins = (refs[0:6], refs[6:12])
        outs = (refs[12:15], refs[15:18])
        state = refs[18]

        @pl.when(pl.program_id(0) == 0)
        def _():
            state[...] = jnp.zeros_like(state)

        def chunk(cc, carry):
            chains = []
            for d in range(2):
                c = cc if d == 0 else ncb - 1 - cc
                r0 = pl.multiple_of(c * CHUNK, CHUNK)
                rows = pl.ds(r0, CHUNK)
                for h in range(GDN_H):
                    cols = slice(h * GDN_DK, (h + 1) * GDN_DK)
                    m = _gdn_decay(ins[d][2], None, c, rows, r0, d * GDN_H + h, d == 1, None, None)
                    chains.append(dict(d=d, h=h, c=c, rows=rows, cols=cols, m=m, st=state[d * GDN_H + h]))
            for ch in chains:
                q_ref, k_ref, bg_ref, u_ref, w_ref, a_ref = ins[ch["d"]]
                rows, cols = ch["rows"], ch["cols"]
                lhs = jnp.concatenate([w_ref[rows, cols], (q_ref[rows, cols] * ch["m"]["eg"]).astype(BF16)], axis=0)
                ch["ws_qs"] = _dot(lhs, ch["st"].astype(BF16), 1, 0)
            for ch in chains:
                q_ref, k_ref, bg_ref, u_ref, w_ref, a_ref = ins[ch["d"]]
                rows, cols = ch["rows"], ch["cols"]
                vn = u_ref[rows, cols] - ch["ws_qs"][0:CHUNK]
                vnb = vn.astype(BF16)
                ch["vn"] = vn
                ch["avn"] = _dot(a_ref[ch["c"], ch["h"]], vnb, 1, 0)
                ch["kvn"] = _bdot(k_ref[rows, cols] * ch["m"]["egl"], vnb, 0, 0)
            for ch in chains:
                o_ref, vn_ref, s_ref = outs[ch["d"]]
                rows, cols = ch["rows"], ch["cols"]
                o_ref[rows, cols] = ch["ws_qs"][CHUNK:2 * CHUNK] + ch["avn"]
                vn_ref[rows, cols] = ch["vn"]
                s_ref[ch["c"], ch["h"]] = ch["st"]
                state[ch["d"] * GDN_H + ch["h"]] = ch["st"] * ch["m"]["cd"] + ch["kvn"]
            return carry

        lax.fori_loop(0, ncb, chunk, 0)

    ins, outs = [], []
    for d in range(2):
        tix = _dir_tile(d, n_tiles, False)
        im = lambda i, tix=tix: (tix(i), 0)
        im4 = lambda i, tix=tix: (tix(i), 0, 0, 0)
        u, w, a, _ = loc[d]
        ins += [(q, (ts, GDN_W), im), (k, (ts, GDN_W), im), (bg, (ts, 128), im), (u, (ts, GDN_W), im),
                (w, (ts, GDN_W), im), (a, (ncb, GDN_H, CHUNK, CHUNK), im4)]
        outs += [(jax.ShapeDtypeStruct((S, GDN_W), F32), (ts, GDN_W), im),
                 (jax.ShapeDtypeStruct((S, GDN_W), F32), (ts, GDN_W), im),
                 (jax.ShapeDtypeStruct((nch, GDN_H, GDN_DK, GDN_DK), F32), (ncb, GDN_H, GDN_DK, GDN_DK), im4)]
    res = _rows("gdn_scan_fwd", S, ts, ins, outs, body, scratch=[pltpu.VMEM((2 * GDN_H, GDN_DK, GDN_DK), F32)])
    return res[0:3], res[3:6]


def _gelu(x):
    c = math.sqrt(2.0 / math.pi)
    t = jnp.tanh(c * (x + 0.044715 * x * x * x))
    return 0.5 * x * (1.0 + t), t


def _mix_out_fwd(h_f, h_b, o_f, o_b, p, gn):
    S = h_f.shape[0]
    ts = _tile(S, 512)

    def body(hf, hb, of, ob, gate, z, gn_ref, y_ref):
        ge, _ = _gelu(gate[...])
        y_ref[:, 0:RG_W] = ((hf[...] + hb[...]) * ge).astype(BF16)
        o = of[...] + ob[...]
        zv = z[...]
        sz = zv * _sigmoid(zv)
        for h in range(GDN_H):
            cols = slice(h * GDN_DK, (h + 1) * GDN_DK)
            oh = o[:, cols]
            n = oh * lax.rsqrt(jnp.mean(oh * oh, axis=-1, keepdims=True) + EPS) * gn_ref[...]
            y_ref[:, RG_W + h * GDN_DK:RG_W + (h + 1) * GDN_DK] = (n * sz[:, cols]).astype(BF16)

    blk = (ts, RG_W)
    im = lambda i: (i, 0)
    ins = [(h_f, blk, im), (h_b, blk, im), (o_f, blk, im), (o_b, blk, im),
           (p, blk, lambda i: (i, 1)), (p, blk, lambda i: (i, 5)), (gn, (1, GDN_DK), lambda i: (0, 0))]
    return _rows("mix_out_fwd", S, ts, ins,
                 [(jax.ShapeDtypeStruct((S, D_MODEL), BF16), (ts, D_MODEL), im)], body)[0]


def _loss_head(x, target, g):
    S, D = x.shape
    ts = _tile(S, 512)

    def body(x_ref, t_ref, g_ref, dx_ref, loss_ref, dg_ref):
        @pl.when(pl.program_id(0) == 0)
        def _():
            loss_ref[...] = jnp.zeros_like(loss_ref)
            dg_ref[...] = jnp.zeros_like(dg_ref)

        xv = x_ref[...]
        gv = g_ref[...]
        r = lax.rsqrt(jnp.mean(xv * xv, axis=-1, keepdims=True) + EPS)
        err = xv * r * gv - t_ref[...]
        loss_ref[...] += jnp.sum(err * err) * (0.5 / D)
        dx, dgt = _rmsnorm_bwd_tile(err * (1.0 / D), xv, gv)
        dx_ref[...] = dx
        dg_ref[...] += jnp.sum(dgt, axis=0, keepdims=True)

    im = lambda i: (i, 0)
    z = lambda i: (0, 0)
    return _rows("loss_head", S, ts,
                 [(x, (ts, D), im), (target, (ts, D), im), (g, (1, D), z)],
                 [(jax.ShapeDtypeStruct((S, D), F32), (ts, D), im),
                  (jax.ShapeDtypeStruct((8, 128), F32), (8, 128), z),
                  (jax.ShapeDtypeStruct((1, D), F32), (1, D), z)], body)


def _block_diag(w):
    n = w.shape[0]
    return jnp.einsum("nij,nm->nimj", w, jnp.eye(n, dtype=w.dtype)).reshape(n * w.shape[1], n * w.shape[2])


def _rg_bd(a_w, x_w):
    return jnp.concatenate([_block_diag(a_w[0]), _block_diag(x_w[0]), _block_diag(a_w[1]), _block_diag(x_w[1])],
                           axis=1).astype(BF16)


def _rg_prm(ba, bx, lam):
    return jnp.concatenate([ba[0:1], bx[0:1], ba[1:2], bx[1:2], lam, jnp.zeros((2, RG_W), F32)], axis=0)


def _gdn_prm(a_log, dt_bias):
    rows = jnp.zeros((8, 128), F32)
    rows = rows.at[0, 8:16].set(a_log.reshape(-1))
    return rows.at[1, 8:16].set(dt_bias.reshape(-1))


def _gc_rows(bg):
    S = bg.shape[0]
    return bg[:, 8:16].reshape(S // CHUNK, CHUNK, 8).transpose(0, 2, 1)


def _layer_fwd(x0, target, W, more=None):
    S = x0.shape[0]
    R = {}
    R["h1"] = _rmsnorm_fwd("rms1", x0, W["ffn1_norm"])
    R["x1"], R["a1"], R["b1"], R["f1"] = _ffn_fwd("ffn1", x0, R["h1"], W["ffn1_w_gate"], W["ffn1_w_up"], W["ffn1_w_down"])
    if more is not None:
        W = {**W, **more("mixer", R["x1"])}
    R["h2"] = _rmsnorm_fwd("rms2", R["x1"], W["mix_norm"])
    tm = _tile(S, 512)
    tmp = _tile(S, 1024)
    R["p"] = _fused_mm("in_proj", S, D_IN_PAD, D_MODEL, tmp, 640, D_MODEL, [(R["h2"], "mk"), (W["w_in"], "kn")],
                       [(0, 1, 0)], [], [(jax.ShapeDtypeStruct((S, D_IN_PAD), F32), (tmp, 640), _mn)],
                       lambda i, accs, ex, out: out[0].__setitem__(Ellipsis, accs[0][...]))[0]
    p = R["p"]
    R["xc"] = _conv_fwd("rg_conv_fwd", p, 0, W["rg_conv_w"], W["rg_conv_b"], "bias")
    R["bd"] = _rg_bd(W["rg_gate_a_w"], W["rg_gate_x_w"])
    R["rg_prm"] = _rg_prm(W["rg_gate_a_b"], W["rg_gate_x_b"], W["rg_lambda"])
    a_f, b_f, a_b, b_b = _rg_gates_fwd(R["xc"], R["bd"], R["rg_prm"])
    R["a_f"], R["a_b"] = a_f, a_b
    R["h_f"], R["h_b"] = _rg_scan("rg_scan_fwd", a_f, b_f, a_b, b_b)
    zero_b = jnp.zeros((1, RG_W), F32)
    cw = W["gdn_conv_w"]
    R["q"] = _conv_fwd("gdn_conv_q", p, 2, cw[:, 0:512], zero_b, "q")
    R["k"] = _conv_fwd("gdn_conv_k", p, 3, cw[:, 512:1024], zero_b, "k")
    R["v"] = _conv_fwd("gdn_conv_v", p, 4, cw[:, 1024:1536], zero_b, "v")
    R["gdn_prm"] = _gdn_prm(W["gdn_a_log"], W["gdn_dt_bias"])
    R["bg"] = _gdn_prep_fwd(p, R["gdn_prm"])
    R["gcr"] = _gc_rows(R["bg"])
    R["gdn_loc"] = _gdn_local_fwd(R["q"], R["k"], R["v"], R["bg"], R["gcr"])
    R["gdn_fwd"] = _gdn_scan_fwd(R["q"], R["k"], R["bg"], R["gdn_loc"])
    R["o_f"], R["o_b"] = R["gdn_fwd"][0][0], R["gdn_fwd"][1][0]
    R["y"] = _mix_out_fwd(R["h_f"], R["h_b"], R["o_f"], R["o_b"], p, W["gdn_norm"])
    R["x2"] = _fused_mm("out_proj", S, D_MODEL, D_MODEL, tm, D_MODEL, D_MODEL, [(R["y"], "mk"), (W["w_out"], "kn")],
                        [(0, 1, 0)], [(R["x1"], (tm, D_MODEL), _mn)],
                        [(jax.ShapeDtypeStruct((S, D_MODEL), F32), (tm, D_MODEL), _mn)],
                        lambda i, accs, ex, out: out[0].__setitem__(Ellipsis, ex[0][...] + accs[0][...]))[0]
    if more is not None:
        W = {**W, **more("ffn2", R["x2"])}
    R["h3"] = _rmsnorm_fwd("rms3", R["x2"], W["ffn2_norm"])
    R["x3"], R["a2"], R["b2"], R["f2"] = _ffn_fwd("ffn2", R["x2"], R["h3"], W["ffn2_w_gate"], W["ffn2_w_up"], W["ffn2_w_down"])
    R["dx3"], R["loss"], R["d_final_norm"] = _loss_head(R["x3"], target, W["final_norm"])
    R["W"] = W
    return R


def _colsum_into(ref, i, val):
    @pl.when(i == 0)
    def _():
        ref[...] = val

    @pl.when(i > 0)
    def _():
        ref[...] += val


def _ffn_bwd(tag, dout, x, g, h, a, b, f, wg, wu, wd, emit):
    S = x.shape[0]
    tm = _tile(S, 512)
    tk_s = _tile(S, 1024)
    dwd = _fused_mm(f"{tag}_dw_down", D_FF, D_MODEL, S, 1408, D_MODEL, tk_s, [(f, "km"), (dout, "kn")], [(0, 1, 0)], [],
                    [(jax.ShapeDtypeStruct((D_FF, D_MODEL), BF16), (1408, D_MODEL), _mn)],
                    lambda i, accs, ex, out: out[0].__setitem__(Ellipsis, (0.5 * accs[0][...]).astype(BF16)))[0]
    emit(down=dwd)

    def epi_act(i, accs, ex, out):
        df = 0.5 * accs[0][...]
        out[0][...] = (df * ex[1][...].astype(F32)).astype(BF16)
        out[1][...] = (df * ex[0][...].astype(F32)).astype(BF16)

    sds = jax.ShapeDtypeStruct((S, D_FF), BF16)
    da, db = _fused_mm(f"{tag}_dact", S, D_FF, D_MODEL, tm, 1408, D_MODEL, [(dout, "mk"), (wd, "nk")], [(0, 1, 0)],
                       [(a, (tm, 1408), _mn), (b, (tm, 1408), _mn)], [(sds, (tm, 1408), _mn)] * 2, epi_act)

    def epi_w2(i, accs, ex, out):
        out[0][...] = accs[0][...].astype(BF16)
        out[1][...] = accs[1][...].astype(BF16)

    sdw = jax.ShapeDtypeStruct((D_MODEL, D_FF), BF16)
    dwg, dwu = _fused_mm(f"{tag}_dw_up", D_MODEL, D_FF, S, D_MODEL, 1408, _tile(S, 512),
                         [(h, "km"), (da, "kn"), (db, "kn")], [(0, 1, 0), (0, 2, 1)], [],
                         [(sdw, (D_MODEL, 1408), _mn)] * 2, epi_w2)
    tok = emit(gate=dwg, up=dwu)
    if tok is not None:
        g = g + tok

    def epi_dx(i, accs, ex, out):
        dx, dgt = _rmsnorm_bwd_tile(accs[0][...], ex[0][...], ex[1][...])
        out[0][...] = ex[2][...] + dx
        _colsum_into(out[1], i, jnp.sum(dgt, axis=0, keepdims=True))

    dx, dg = _fused_mm(f"{tag}_dx", S, D_MODEL, D_FF, tm, D_MODEL, 1408,
                       [(da, "mk"), (wg, "nk"), (db, "mk"), (wu, "nk")], [(0, 1, 0), (2, 3, 0)],
                       [(x, (tm, D_MODEL), _mn), (g, (1, D_MODEL), _row0), (dout, (tm, D_MODEL), _mn)],
                       [(jax.ShapeDtypeStruct((S, D_MODEL), F32), (tm, D_MODEL), _mn),
                        (jax.ShapeDtypeStruct((1, D_MODEL), F32), (1, D_MODEL), _row0)], epi_dx)
    return dx, dg


def _mix_out_bwd(dy, h_f, h_b, o_f, o_b, p, gn):
    S = dy.shape[0]
    ts = _tile(S, 512)
    c0 = math.sqrt(2.0 / math.pi)

    def body(dy_ref, hf, hb, of, ob, gate, z, gn_ref, dhr_ref, dgate_ref, do_ref, dz_ref, dgn_ref):
        i = pl.program_id(0)
        gv = gate[...]
        ge, t = _gelu(gv)
        dy_rg = dy_ref[:, 0:RG_W]
        dhr_ref[...] = dy_rg * ge
        dgelu = 0.5 * (1.0 + t) + 0.5 * gv * (1.0 - t * t) * c0 * (1.0 + 3.0 * 0.044715 * gv * gv)
        dgate_ref[...] = (dy_rg * (hf[...] + hb[...]) * dgelu).astype(BF16)
        o = of[...] + ob[...]
        zv = z[...]
        sig = _sigmoid(zv)
        gnv = gn_ref[...]
        dgn = jnp.zeros((1, GDN_DK), F32)
        for h in range(GDN_H):
            cols = slice(h * GDN_DK, (h + 1) * GDN_DK)
            oh = o[:, cols]
            r = lax.rsqrt(jnp.mean(oh * oh, axis=-1, keepdims=True) + EPS)
            ohat = oh * r
            dyh = dy_ref[:, RG_W + h * GDN_DK:RG_W + (h + 1) * GDN_DK]
            zh = zv[:, cols]
            sh = sig[:, cols]
            dn = dyh * zh * sh
            dz_ref[:, cols] = (dyh * ohat * gnv * (sh * (1.0 + zh * (1.0 - sh)))).astype(BF16)
            dxn = dn * gnv
            do_ref[:, cols] = r * (dxn - ohat * jnp.mean(dxn * ohat, axis=-1, keepdims=True))
            dgn = dgn + jnp.sum(dn * ohat, axis=0, keepdims=True)
        _colsum_into(dgn_ref, i, dgn)

    blk = (ts, RG_W)
    im = lambda i: (i, 0)
    z0 = lambda i: (0, 0)
    ins = [(dy, (ts, D_MODEL), im), (h_f, blk, im), (h_b, blk, im), (o_f, blk, im), (o_b, blk, im),
           (p, blk, lambda i: (i, 1)), (p, blk, lambda i: (i, 5)), (gn, (1, GDN_DK), z0)]
    outs = [(jax.ShapeDtypeStruct((S, RG_W), F32), blk, im), (jax.ShapeDtypeStruct((S, RG_W), BF16), blk, im),
            (jax.ShapeDtypeStruct((S, RG_W), F32), blk, im), (jax.ShapeDtypeStruct((S, RG_W), BF16), blk, im),
            (jax.ShapeDtypeStruct((1, GDN_DK), F32), (1, GDN_DK), z0)]
    return _rows("mix_out_bwd", S, ts, ins, outs, body)


def _rg_scan_adj(name, a_up, b_up, a_dn, b_dn):
    S, C = a_up.shape
    ts = _tile(S, 512)
    n_tiles = S // ts

    def body(au, bu, ad, bd, mu_ref, lam_ref, carry):
        @pl.when(pl.program_id(0) == 0)
        def _():
            carry[...] = jnp.zeros_like(carry)

        n_sub = ts // SUBLANES
        rows = lax.broadcasted_iota(jnp.int32, (SUBLANES, C), 0)

        def half(a_ref, b_ref, out_ref, r0, c_in, reverse):
            a = a_ref[pl.ds(r0, SUBLANES), :]
            b = b_ref[pl.ds(r0, SUBLANES), :]
            cum_a, c0 = _scan_rows(a, a * b, reverse)
            c = c0 + cum_a * c_in
            edge = 0 if not reverse else SUBLANES - 1
            c_prev = jnp.where(rows == edge, c_in, pltpu.roll(c, SUBLANES - 1 if reverse else 1, 0))
            out_ref[pl.ds(r0, SUBLANES), :] = b + c_prev
            return c[0:1, :] if reverse else c[SUBLANES - 1:SUBLANES, :]

        def step(j, c):
            cu, cd = c
            cu = half(au, bu, mu_ref, pl.multiple_of(j * SUBLANES, SUBLANES), cu, False)
            cd = half(ad, bd, lam_ref, pl.multiple_of((n_sub - 1 - j) * SUBLANES, SUBLANES), cd, True)
            return cu, cd

        cu, cd = lax.fori_loop(0, n_sub, step, (carry[0:1, :], carry[1:2, :]), unroll=4)
        carry[0:1, :] = cu
        carry[1:2, :] = cd

    fw = lambda i: (i, 0)
    bw = lambda i: (n_tiles - 1 - i, 0)
    sds = jax.ShapeDtypeStruct((S, C), F32)
    return _rows(name, S, ts,
                 [(a_up, (ts, C), fw), (b_up, (ts, C), fw), (a_dn, (ts, C), bw), (b_dn, (ts, C), bw)],
                 [(sds, (ts, C), fw), (sds, (ts, C), bw)], body, scratch=[pltpu.VMEM((8, C), F32)])


def _halo_ex(arr, S, tm, width):
    per = tm // HALO
    last = S // HALO - 1
    return [
        (arr, (tm, width), lambda i, j: (i, 0)),
        (arr, (HALO, width), lambda i, j: (jnp.maximum(i * per - 1, 0), 0)),
        (arr, (HALO, width), lambda i, j: (jnp.minimum((i + 1) * per, last), 0)),
    ]


def _rg_gates_bwd(xc, bd, prm, lam_f, lam_b, h_f, h_b):
    S = xc.shape[0]
    tm = _tile(S, 256)
    n_tiles = S // tm

    def epi(i, accs, ex, out):
        pre = accs[0][...]
        xv = ex[0][...]
        prm_ref = ex[1]
        lams = (ex[2][...], ex[3][...])
        hprev = (_shift(_ext(ex[4], ex[5], ex[6], i, n_tiles), -1, tm),
                 _shift(_ext(ex[7], ex[8], ex[9], i, n_tiles), 1, tm))
        dxc = jnp.zeros_like(xv)
        rows = []
        dlam_rows = []
        for d in range(2):
            r, ig, sp, a, sq = _rg_gate_terms(pre, xv, prm_ref, d)
            lam = lams[d]
            da = lam * hprev[d]
            di = lam * sq * xv
            dxc = dxc + lam * sq * ig
            dsq = lam * ig * xv
            dlog_a = da * a - dsq * (a * a) / sq
            dpre_r = dlog_a * (-RG_C * sp) * r * (1.0 - r)
            dpre_i = di * ig * (1.0 - ig)
            out[0][:, d * 1024:d * 1024 + RG_W] = dpre_r.astype(BF16)
            out[0][:, d * 1024 + RG_W:(d + 1) * 1024] = dpre_i.astype(BF16)
            rows += [jnp.sum(dpre_r, axis=0, keepdims=True), jnp.sum(dpre_i, axis=0, keepdims=True)]
            dsp = jnp.sum(dlog_a * (-RG_C * r), axis=0, keepdims=True)
            dlam_rows.append(-dsp * _sigmoid(-prm_ref[4 + d:5 + d, :]))
        out[1][...] = dxc
        zero = jnp.zeros((2, RG_W), F32)
        _colsum_into(out[2], i, jnp.concatenate(rows + dlam_rows + [zero], axis=0))

    blk = (tm, RG_W)
    im = lambda i, j: (i, 0)
    extras = ([(xc, blk, im), (prm, (8, RG_W), _row0), (lam_f, blk, im), (lam_b, blk, im)]
              + _halo_ex(h_f, S, tm, RG_W) + _halo_ex(h_b, S, tm, RG_W))
    outs = [(jax.ShapeDtypeStruct((S, 4 * RG_W), BF16), (tm, 4 * RG_W), im),
            (jax.ShapeDtypeStruct((S, RG_W), F32), blk, im),
            (jax.ShapeDtypeStruct((8, RG_W), F32), (8, RG_W), _row0)]
    return _fused_mm("rg_gates_bwd", S, 4 * RG_W, RG_W, tm, 4 * RG_W, RG_W, [(xc, "mk"), (bd, "kn")], [(0, 1, 0)],
                     extras, outs, epi)


def _roll_rows(ext, off):
    if off == 0:
        return ext
    return pltpu.roll(ext, (-off) % ext.shape[0], 0)


def _conv_bwd(name, p, colblk, w, grads, mode):
    S = p.shape[0]
    ts = _tile(S, 512)
    n_tiles = S // ts
    C = w.shape[1]
    ng = len(grads)

    def body(*refs):
        p_refs = refs[0:3]
        g_refs = refs[3:3 + 3 * ng]
        w_ref = refs[3 + 3 * ng]
        dx_ref, dw_ref, db_ref = refs[4 + 3 * ng:]
        i = pl.program_id(0)
        ext_p = _ext(*p_refs, i, n_tiles)
        dn = _ext(*g_refs[0:3], i, n_tiles)
        for gi in range(1, ng):
            dn = dn + _ext(*g_refs[3 * gi:3 * gi + 3], i, n_tiles)
        if mode == "bias":
            dc = dn
        else:
            c = None
            for j in range(CONV_W):
                term = w_ref[j:j + 1, :] * _roll_rows(ext_p, j - 2)
                c = term if c is None else c + term
            sig = _sigmoid(c)
            s = c * sig
            if mode in ("q", "k"):
                scale = GDN_DK ** -0.5 if mode == "q" else 1.0
                parts = []
                for h in range(GDN_H):
                    cols = slice(h * GDN_DK, (h + 1) * GDN_DK)
                    sh = s[:, cols]
                    dnh = dn[:, cols]
                    rinv = lax.rsqrt(jnp.sum(sh * sh, axis=-1, keepdims=True) + EPS)
                    parts.append(scale * rinv * (dnh - sh * (rinv * rinv) * jnp.sum(dnh * sh, axis=-1, keepdims=True)))
                ds = jnp.concatenate(parts, axis=-1)
            else:
                ds = dn
            dc = ds * (sig * (1.0 + c * (1.0 - sig)))
        dx = None
        for j in range(CONV_W):
            term = w_ref[j:j + 1, :] * _shift(dc, 2 - j, ts)
            dx = term if dx is None else dx + term
        dx_ref[...] = dx.astype(BF16)
        dc_main = dc[HALO:HALO + ts]
        dw = jnp.concatenate([jnp.sum(dc_main * _shift(ext_p, j - 2, ts), axis=0, keepdims=True)
                              for j in range(CONV_W)], axis=0)
        _colsum_into(dw_ref, i, dw)
        _colsum_into(db_ref, i, jnp.sum(dc_main, axis=0, keepdims=True))

    ins = _halo_ins(p, S, ts, C, colblk)
    for garr in grads:
        ins += _halo_ins(garr, S, ts, C, 0)
    ins += [(w, (CONV_W, C), lambda i: (0, 0))]
    z0 = lambda i: (0, 0)
    outs = [(jax.ShapeDtypeStruct((S, C), BF16), (ts, C), lambda i: (i, 0)),
            (jax.ShapeDtypeStruct((CONV_W, C), F32), (CONV_W, C), z0),
            (jax.ShapeDtypeStruct((1, C), F32), (1, C), z0)]
    return _rows(name, S, ts, ins, outs, body)


def _gdn_scan_bwd(q, k, bg, loc, do):
    S = q.shape[0]
    ts = _tile(S, GDN_TS)
    n_tiles = S // ts
    ncb = ts // CHUNK
    nch = S // CHUNK

    def body(*refs):
        ins = (refs[0:6], refs[6:12])
        outs = (refs[12:14], refs[14:16])
        dstate = refs[16]

        @pl.when(pl.program_id(0) == 0)
        def _():
            dstate[...] = jnp.zeros_like(dstate)

        def chunk(cc, carry):
            chains = []
            for d in range(2):
                c = ncb - 1 - cc if d == 0 else cc
                r0 = pl.multiple_of(c * CHUNK, CHUNK)
                rows = pl.ds(r0, CHUNK)
                for h in range(GDN_H):
                    cols = slice(h * GDN_DK, (h + 1) * GDN_DK)
                    m = _gdn_decay(ins[d][2], None, c, rows, r0, d * GDN_H + h, d == 1, None, None)
                    chains.append(dict(d=d, h=h, c=c, rows=rows, cols=cols, m=m, dsn=dstate[d * GDN_H + h]))
            for ch in chains:
                q_ref, k_ref, bg_ref, w_ref, a_ref, do_ref = ins[ch["d"]]
                rows, cols = ch["rows"], ch["cols"]
                dob = do_ref[rows, cols].astype(BF16)
                ch["dvn"] = (_dot(a_ref[ch["c"], ch["h"]], dob, 0, 0)
                             + _bdot(k_ref[rows, cols] * ch["m"]["egl"], ch["dsn"], 1, 0))
                ch["qdo"] = _bdot(q_ref[rows, cols] * ch["m"]["eg"], dob, 0, 0)
            for ch in chains:
                w_ref = ins[ch["d"]][3]
                ch["wdvn"] = _dot(w_ref[ch["rows"], ch["cols"]], ch["dvn"].astype(BF16), 0, 0)
            for ch in chains:
                dvn_ref, ds_ref = outs[ch["d"]]
                dvn_ref[ch["rows"], ch["cols"]] = ch["dvn"]
                ds_ref[ch["c"], ch["h"]] = ch["dsn"]
                dstate[ch["d"] * GDN_H + ch["h"]] = ch["qdo"] + ch["m"]["cd"] * ch["dsn"] - ch["wdvn"]
            return carry

        lax.fori_loop(0, ncb, chunk, 0)

    ins, outs = [], []
    for d in range(2):
        tix = _dir_tile(d, n_tiles, True)
        im = lambda i, tix=tix: (tix(i), 0)
        im4 = lambda i, tix=tix: (tix(i), 0, 0, 0)
        _, w, a, _ = loc[d]
        ins += [(q, (ts, GDN_W), im), (k, (ts, GDN_W), im), (bg, (ts, 128), im), (w, (ts, GDN_W), im),
                (a, (ncb, GDN_H, CHUNK, CHUNK), im4), (do, (ts, GDN_W), im)]
        outs += [(jax.ShapeDtypeStruct((S, GDN_W), F32), (ts, GDN_W), im),
                 (jax.ShapeDtypeStruct((nch, GDN_H, GDN_DK, GDN_DK), F32), (ncb, GDN_H, GDN_DK, GDN_DK), im4)]
    res = _rows("gdn_scan_bwd", S, ts, ins, outs, body, scratch=[pltpu.VMEM((2 * GDN_H, GDN_DK, GDN_DK), F32)])
    return res[0:2], res[2:4]


def _gdn_local_bwd(q, k, v, bg, gcr, do, loc, fwd, adj):
    S = q.shape[0]
    ts = _tile(S, GDN_TS)
    ncb = ts // CHUNK

    def body(q_ref, k_ref, v_ref, bg_ref, gcr_ref, do_ref, *rest):
        per_dir = (rest[0:5], rest[5:10])
        dq_ref, dk_ref, dv_ref, dbg_ref, dbgr_ref = rest[10:15]
        ri, ci = _tri_masks()
        lane = lax.broadcasted_iota(jnp.int32, (CHUNK, 128), 1)
        rowi = lax.broadcasted_iota(jnp.int32, (CHUNK, 1), 0)
        ones8 = jnp.ones((SUBLANES, CHUNK), F32)

        def chunk(c, carry):
            r0 = pl.multiple_of(c * CHUNK, CHUNK)
            rows = pl.ds(r0, CHUNK)
            chains = []
            for h in range(GDN_H):
                cols = slice(h * GDN_DK, (h + 1) * GDN_DK)
                qh, kh, vh = q_ref[rows, cols], k_ref[rows, cols], v_ref[rows, cols]
                dob = do_ref[rows, cols].astype(BF16)
                both = _bdot(jnp.concatenate([qh, kh], axis=0), kh, 1, 1)
                for d in range(2):
                    chains.append(dict(h=h, d=d, cols=cols, qh=qh, kh=kh, vh=vh, dob=dob, qk=both[0:CHUNK],
                                       kk=both[CHUNK:2 * CHUNK], col=d * GDN_H + h))
            for ch in chains:
                m = _gdn_decay(bg_ref, gcr_ref, c, rows, r0, ch["col"], ch["d"] == 1, ri, ci)
                t_ref, s_ref, ds_ref, vn_ref, dvn_ref = per_dir[ch["d"]]
                h, cols = ch["h"], ch["cols"]
                ch["m"] = m
                ch["kb"] = ch["kh"] * m["beta"]
                ch["kbg"] = ch["kb"] * m["eg"]
                ch["t"] = t_ref[c, h]
                st = s_ref[c, h]
                stb = st.astype(BF16)
                ch["dsn"] = ds_ref[c, h]
                vnb = vn_ref[rows, cols].astype(BF16)
                dvnb = dvn_ref[rows, cols].astype(BF16)
                ch["dcd"] = jnp.sum(jnp.sum(st * ch["dsn"], axis=1, keepdims=True), axis=0, keepdims=True)
                ch["dqd"] = _dot(ch["dob"], stb, 1, 1)
                ch["d_a"] = _dot(ch["dob"], vnb, 1, 1)
                ch["dkd"] = _bdot(vnb, ch["dsn"], 1, 1)
                ch["dw"] = -_dot(dvnb, stb, 1, 1)
                ch["dvb"] = _dot(ch["t"], dvnb, 1, 0)
                ch["d_t"] = _bdot(dvnb, ch["vh"] * m["beta"], 1, 1)
            for ch in chains:
                dwb = ch["dw"].astype(BF16)
                ch["d_t"] = ch["d_t"] + _bdot(dwb, ch["kbg"], 1, 1)
                ch["dkbg"] = _dot(ch["t"], dwb, 1, 0)
                ch["nn"] = ch["d_a"] * ch["m"]["dm"]
                ch["nn_q"] = _bdot(ch["nn"], ch["qh"], 0, 0)
                ch["nn_k"] = _bdot(ch["nn"], ch["kh"], 1, 0)
            for ch in chains:
                ch["x"] = _dot(ch["d_t"].astype(BF16), ch["t"], 1, 0)
            for ch in chains:
                d_l = -_dot(ch["t"], ch["x"].astype(BF16), 1, 0)
                ch["d_l"] = jnp.where(ch["m"]["strict"], d_l, 0.0)
                ch["mm"] = ch["d_l"] * ch["m"]["dm"]
            for ch in chains:
                m = ch["m"]
                ch["mm_kh"] = _bdot(ch["mm"], ch["kh"], 1, 0)
                ch["mm_kb"] = _bdot(ch["mm"], ch["kb"], 0, 0)
                l_mat = jnp.where(m["strict"], m["beta"] * ch["kk"] * m["dm"], 0.0)
                ch["e"] = ch["d_l"] * l_mat + ch["nn"] * ch["qk"]
                dbgr_ref[c, ch["col"]:ch["col"] + 1, :] = -_dot(ones8, ch["e"], 1, 0, HI)[0:1, :]
            acc_bg = jnp.zeros((CHUNK, 128), F32)
            acc = {}
            for ch in chains:
                m = ch["m"]
                beta, eg, egl = m["beta"], m["eg"], m["egl"]
                dkb = ch["mm_kh"] + ch["dkbg"] * eg
                dk_d = ch["mm_kb"] + ch["nn_q"] + ch["dkd"] * egl + dkb * beta
                dq_d = ch["nn_k"] + ch["dqd"] * eg
                dv_d = ch["dvb"] * beta
                dkd_kd = ch["dkd"] * (ch["kh"] * egl)
                dgc = (jnp.sum(ch["e"], axis=1, keepdims=True)
                       + jnp.sum(ch["dqd"] * (ch["qh"] * eg) - dkd_kd + ch["dkbg"] * ch["kbg"], axis=1, keepdims=True))
                dgl = jnp.sum(jnp.sum(dkd_kd, axis=1, keepdims=True), axis=0, keepdims=True) + ch["dcd"] * m["cd"]
                dgc = dgc + jnp.where(rowi == (0 if ch["d"] == 1 else CHUNK - 1), dgl, 0.0)
                dbeta = jnp.sum(dkb * ch["kh"] + ch["dvb"] * ch["vh"], axis=1, keepdims=True)
                acc_bg = acc_bg + jnp.where(lane == ch["col"], dbeta, 0.0) + jnp.where(lane == 8 + ch["col"], dgc, 0.0)
                if ch["d"] == 0:
                    acc[ch["h"]] = (dq_d, dk_d, dv_d)
                else:
                    dq0, dk0, dv0 = acc[ch["h"]]
                    dq_ref[rows, ch["cols"]] = dq0 + dq_d
                    dk_ref[rows, ch["cols"]] = dk0 + dk_d
                    dv_ref[rows, ch["cols"]] = dv0 + dv_d
            dbg_ref[rows, :] = acc_bg
            return carry

        lax.fori_loop(0, ncb, chunk, 0)

    im = lambda i: (i, 0)
    im4 = lambda i: (i, 0, 0, 0)
    blk = (ts, GDN_W)
    ins = [(q, blk, im), (k, blk, im), (v, blk, im), (bg, (ts, 128), im), (gcr, (ncb, 8, CHUNK), lambda i: (i, 0, 0)),
           (do, blk, im)]
    for d in range(2):
        ins += [(loc[d][3], (ncb, GDN_H, CHUNK, CHUNK), im4), (fwd[d][2], (ncb, GDN_H, GDN_DK, GDN_DK), im4),
                (adj[d][1], (ncb, GDN_H, GDN_DK, GDN_DK), im4), (fwd[d][1], blk, im), (adj[d][0], blk, im)]
    sds = jax.ShapeDtypeStruct((S, GDN_W), F32)
    outs = [(sds, blk, im), (sds, blk, im), (sds, blk, im), (jax.ShapeDtypeStruct((S, 128), F32), (ts, 128), im),
            (jax.ShapeDtypeStruct((S // CHUNK, 8, CHUNK), F32), (ncb, 8, CHUNK), lambda i: (i, 0, 0))]
    dq, dk, dv, dbg, dbg_rows = _rows("gdn_local_bwd", S, ts, ins, outs, body)
    dgc_cols = dbg_rows.transpose(0, 2, 1).reshape(S, 8)
    return dq, dk, dv, dbg + jnp.pad(dgc_cols, ((0, 0), (8, 112)))


def _gdn_prep_bwd(dbg_all, p, prm):
    S = p.shape[0]
    ts = _tile(S, 512)

    def body(dbg_ref, p_ref, prm_ref, dba_ref, dprm_ref):
        i = pl.program_id(0)
        raw = p_ref[...]
        dbg = dbg_ref[...]
        lane = lax.broadcasted_iota(jnp.int32, (1, 128), 1)
        is_g = (lane >= 8) & (lane < 16)
        ea = jnp.exp(prm_ref[0:1, :])
        arg = raw + prm_ref[1:2, :]
        g = jnp.where(is_g, -ea * _softplus(arg), 0.0)
        beta = _sigmoid(raw)
        dgc = jnp.where(is_g, dbg, 0.0)
        ri, ci = _tri_masks()
        lower = (ri >= ci).astype(F32)
        upper = (ri <= ci).astype(F32)
        dgs = []
        for c in range(ts // CHUNK):
            ch = dgc[c * CHUNK:(c + 1) * CHUNK]
            dgs.append(jnp.where(lane < 12, _dot(upper, ch, 1, 0, HI), _dot(lower, ch, 1, 0, HI)))
        dg = jnp.concatenate(dgs, axis=0)
        dalpha = jnp.where(is_g, dg * (-ea) * _sigmoid(arg), 0.0)
        dba_ref[...] = jnp.where(lane < 8, dbg * beta * (1.0 - beta), dalpha).astype(BF16)
        rows = jnp.concatenate([jnp.sum(dg * g, axis=0, keepdims=True), jnp.sum(dalpha, axis=0, keepdims=True),
                                jnp.zeros((6, 128), F32)], axis=0)
        _colsum_into(dprm_ref, i, rows)

    im = lambda i: (i, 0)
    z0 = lambda i: (0, 0)
    return _rows("gdn_prep_bwd", S, ts,
                 [(dbg_all, (ts, 128), im), (p, (ts, 128), lambda i: (i, COL_BA // 128)), (prm, (8, 128), z0)],
                 [(jax.ShapeDtypeStruct((S, 128), BF16), (ts, 128), im), (jax.ShapeDtypeStruct((8, 128), F32), (8, 128), z0)],
                 body)


def _mm_plain(name, M, N, K, tm, tn, tk, a, am, b, bm, dtype):
    return _fused_mm(name, M, N, K, tm, tn, tk, [(a, am), (b, bm)], [(0, 1, 0)], [],
                     [(jax.ShapeDtypeStruct((M, N), dtype), (tm, tn), _mn)],
                     lambda i, accs, ex, out: out[0].__setitem__(Ellipsis, accs[0][...].astype(dtype)))[0]


def _layer_bwd(x0, W, R, emit_big=None, emit_small=None):
    S = x0.shape[0]
    tm = _tile(S, 512)
    tk_s = _tile(S, 1024)
    G = {}

    def emit(**named):
        if emit_big is None:
            G.update(named)
            return None
        return emit_big(**named)

    def ffn_emit(prefix):
        return lambda **kw: emit(**{f"{prefix}_w_{k}": v for k, v in kw.items()})

    dx2, G["ffn2_norm"] = _ffn_bwd("ffn2b", R["dx3"], R["x2"], W["ffn2_norm"], R["h3"], R["a2"], R["b2"], R["f2"],
                                   W["ffn2_w_gate"], W["ffn2_w_up"], W["ffn2_w_down"], ffn_emit("ffn2"))
    tok = emit(w_out=_mm_plain("dw_out", D_MODEL, D_MODEL, S, D_MODEL, D_MODEL, tk_s, R["y"], "km", dx2, "kn", BF16))
    gn = W["gdn_norm"] if tok is None else W["gdn_norm"] + tok
    dy = _mm_plain("dy_mix", S, D_MODEL, D_MODEL, tm, D_MODEL, D_MODEL, dx2, "mk", W["w_out"], "nk", F32)
    p = R["p"]
    dhr, dgate, do, dz, G["gdn_norm"] = _mix_out_bwd(dy, R["h_f"], R["h_b"], R["o_f"], R["o_b"], p, gn)
    lam_b, lam_f = _rg_scan_adj("rg_scan_bwd", R["a_b"], dhr, R["a_f"], dhr)
    dpre, dxc_direct, d_rgprm = _rg_gates_bwd(R["xc"], R["bd"], R["rg_prm"], lam_f, lam_b, R["h_f"], R["h_b"])
    tmg = _tile(S, 512)
    dxc = _fused_mm("rg_dxc", S, RG_W, 4 * RG_W, tmg, RG_W, 4 * RG_W, [(dpre, "mk"), (R["bd"], "nk")], [(0, 1, 0)],
                    [(dxc_direct, (tmg, RG_W), _mn)], [(jax.ShapeDtypeStruct((S, RG_W), F32), (tmg, RG_W), _mn)],
                    lambda i, accs, ex, out: out[0].__setitem__(Ellipsis, ex[0][...] + accs[0][...]))[0]
    d_bd = _mm_plain("rg_dbd", RG_W, 4 * RG_W, S, RG_W, 4 * RG_W, tk_s, R["xc"], "km", dpre, "kn", F32)
    dx_rg, G["rg_conv_w"], G["rg_conv_b"] = _conv_bwd("rg_conv_bwd", p, 0, W["rg_conv_w"], [dxc], "bias")
    blocks = jnp.einsum("nigmj,nm->gnij", d_bd.reshape(RG_BLOCKS, RG_BLOCK, 4, RG_BLOCKS, RG_BLOCK),
                        jnp.eye(RG_BLOCKS, dtype=F32))
    G["rg_gate_a_w"] = jnp.stack([blocks[0], blocks[2]])
    G["rg_gate_x_w"] = jnp.stack([blocks[1], blocks[3]])
    G["rg_gate_a_b"] = jnp.stack([d_rgprm[0], d_rgprm[2]])
    G["rg_gate_x_b"] = jnp.stack([d_rgprm[1], d_rgprm[3]])
    G["rg_lambda"] = d_rgprm[4:6]
    adj = _gdn_scan_bwd(R["q"], R["k"], R["bg"], R["gdn_loc"], do)
    dq, dk, dv, dbg = _gdn_local_bwd(R["q"], R["k"], R["v"], R["bg"], R["gcr"], do, R["gdn_loc"], R["gdn_fwd"], adj)
    cw = W["gdn_conv_w"]
    dpq, dwq, _ = _conv_bwd("gdn_conv_q_bwd", p, 2, cw[:, 0:512], [dq], "q")
    dpk, dwk, _ = _conv_bwd("gdn_conv_k_bwd", p, 3, cw[:, 512:1024], [dk], "k")
    dpv, dwv, _ = _conv_bwd("gdn_conv_v_bwd", p, 4, cw[:, 1024:1536], [dv], "v")
    G["gdn_conv_w"] = jnp.concatenate([dwq, dwk, dwv], axis=1)
    dba, d_gprm = _gdn_prep_bwd(dbg, p, R["gdn_prm"])
    G["gdn_a_log"] = d_gprm[0, 8:16].reshape(2, GDN_H)
    G["gdn_dt_bias"] = d_gprm[1, 8:16].reshape(2, GDN_H)
    dp = jnp.concatenate([dx_rg, dgate, dpq, dpk, dpv, dz, dba], axis=1)
    tok = emit(w_in=_mm_plain("dw_in", D_MODEL, D_IN_PAD, S, D_MODEL, 640, tk_s, R["h2"], "km", dp, "kn", BF16))
    g_mix = W["mix_norm"] if tok is None else W["mix_norm"] + tok

    def epi_dx1(i, accs, ex, out):
        dx, dgt = _rmsnorm_bwd_tile(accs[0][...], ex[0][...], ex[1][...])
        out[0][...] = ex[2][...] + dx
        _colsum_into(out[1], i, jnp.sum(dgt, axis=0, keepdims=True))

    dx1, G["mix_norm"] = _fused_mm(
        "mix_dx", S, D_MODEL, D_IN_PAD, tm, D_MODEL, D_IN_PAD, [(dp, "mk"), (W["w_in"], "nk")], [(0, 1, 0)],
        [(R["x1"], (tm, D_MODEL), _mn), (g_mix, (1, D_MODEL), _row0), (dx2, (tm, D_MODEL), _mn)],
        [(jax.ShapeDtypeStruct((S, D_MODEL), F32), (tm, D_MODEL), _mn),
         (jax.ShapeDtypeStruct((1, D_MODEL), F32), (1, D_MODEL), _row0)], epi_dx1)
    G["final_norm"] = R["d_final_norm"]
    if emit_small is not None:
        emit_small(G)
    dx0, G["ffn1_norm"] = _ffn_bwd("ffn1b", dx1, x0, W["ffn1_norm"], R["h1"], R["a1"], R["b1"], R["f1"],
                                   W["ffn1_w_gate"], W["ffn1_w_up"], W["ffn1_w_down"], ffn_emit("ffn1"))
    return dx0, G


def _mesh_pos():
    x, y, c = lax.axis_index("x"), lax.axis_index("y"), lax.axis_index("c")
    return x, y, c, 4 * x + 2 * y + c


def _peer(x, y, c, r):
    px = 1 - x if r & 4 else x
    py = 1 - y if r & 2 else y
    pc = 1 - c if r & 1 else c
    return (px, py, pc), 4 * px + 2 * py + pc


_HBM = pl.BlockSpec(memory_space=pltpu.HBM)
_SEM = pl.BlockSpec(memory_space=pltpu.SEMAPHORE)


def _peer_copies(scatter, srcs, lands, send_sems, recv_sems):
    x, y, c, me = _mesh_pos()
    copies = []
    for a, (src, land) in enumerate(zip(srcs, lands)):
        for r in range(1, N_DEV):
            peer, peer_idx = _peer(x, y, c, r)
            copies.append(pltpu.make_async_remote_copy(
                src_ref=src.at[peer_idx] if scatter else src, dst_ref=land.at[r - 1] if scatter else land.at[me],
                send_sem=send_sems.at[a * 7 + r - 1], recv_sem=recv_sems.at[a * 7 + r - 1],
                device_id=peer, device_id_type=pl.DeviceIdType.MESH))
    return copies


def _exchange_start(name, scatter, arrays):
    slabs = arrays
    n = len(slabs)

    def body(*refs):
        srcs, lands = refs[0:n], refs[n:2 * n]
        send_sems, recv_sems = refs[2 * n], refs[2 * n + 1]
        token = refs[4 * n + 2]
        for cp in _peer_copies(scatter, srcs, lands, send_sems, recv_sems):
            cp.start()
        token[...] = jnp.zeros_like(token)

    land_shapes = [(N_DEV - 1,) + s.shape[1:] if scatter else (N_DEV,) + s.shape for s in slabs]
    out_shape = ([pltpu.SemaphoreType.DMA((7 * n,)), pltpu.SemaphoreType.DMA((7 * n,))]
                 + [pltpu.HBM(s.shape, s.dtype) for s in slabs]
                 + [pltpu.HBM(shp, s.dtype) for shp, s in zip(land_shapes, slabs)]
                 + [jax.ShapeDtypeStruct((8, 128), F32)])
    res = pl.pallas_call(
        body, name=name, out_shape=out_shape, in_specs=[_HBM] * (2 * n),
        out_specs=[_SEM, _SEM] + [_HBM] * (2 * n) + [pl.BlockSpec(memory_space=pltpu.VMEM)],
        input_output_aliases={i: 2 + i for i in range(2 * n)},
        compiler_params=pltpu.CompilerParams(has_side_effects=pltpu.SideEffectType.DATAFLOW_SIDE_EFFECTING),
    )(*[pltpu.with_memory_space_constraint(s, pltpu.HBM) for s in slabs],
      *[pltpu.with_memory_space_constraint(lax.empty(shp, s.dtype), pltpu.HBM) for shp, s in zip(land_shapes, slabs)])
    return dict(n=n, scatter=scatter, sems=res[0:2], srcs=res[2:2 + n], lands=res[2 + n:2 + 2 * n],
                token=res[2 + 2 * n][0, 0])


def _exchange_wait(name, started, after):
    n = started["n"]
    scatter = started["scatter"]

    def body(*refs):
        srcs, lands = refs[0:n], refs[n:2 * n]
        send_sems, recv_sems = refs[2 * n], refs[2 * n + 1]
        for cp in _peer_copies(scatter, srcs, lands, send_sems, recv_sems):
            cp.wait_send()
            cp.wait_recv()

    arrays = list(started["srcs"]) + list(started["lands"])
    res = pl.pallas_call(
        body, name=name, out_shape=[pltpu.HBM(a.shape, a.dtype) for a in arrays],
        in_specs=[_HBM] * (2 * n) + [_SEM, _SEM, pl.BlockSpec(memory_space=pl.ANY)], out_specs=[_HBM] * (2 * n),
        input_output_aliases={i: i for i in range(2 * n)},
        compiler_params=pltpu.CompilerParams(has_side_effects=pltpu.SideEffectType.DATAFLOW_SIDE_EFFECTING),
    )(*arrays, *started["sems"], after)
    return res[0:n], res[n:2 * n]


def _all_gather(name, arrays):
    n = len(arrays)

    def body(*refs):
        ins = refs[:n]
        outs = refs[n:2 * n]
        token = refs[2 * n]
        send_sems, recv_sems, local_sems = refs[2 * n + 1:]
        token[...] = jnp.zeros_like(token)
        x, y, c, me = _mesh_pos()
        sibling = (x, y, 1 - c)
        chips = [(1 - x, y), (x, 1 - y), (1 - x, 1 - y)]

        def idx(px, py, pc):
            return 4 * px + 2 * py + pc

        def copy(a, k, block, to, src=None):
            slot = outs[a].at[idx(*block)]
            return pltpu.make_async_remote_copy(
                src_ref=slot if src is None else src, dst_ref=slot, send_sem=send_sems.at[a * 7 + k],
                recv_sem=recv_sems.at[a * 7 + k], device_id=to, device_id_type=pl.DeviceIdType.MESH)

        locals_, sends = [], []
        for a in range(n):
            loc = pltpu.make_async_copy(ins[a], outs[a].at[me], local_sems.at[a])
            loc.start()
            locals_.append(loc)
            sends.append(copy(a, 0, (x, y, c), sibling, src=ins[a]))
            sends += [copy(a, 1 + j, (x, y, c), (*chip, c), src=ins[a]) for j, chip in enumerate(chips)]
        for cp in sends:
            cp.start()
        passed = []
        for a in range(n):
            for j, chip in enumerate(chips):
                copy(a, 1 + j, (*chip, c), (x, y, c)).wait_recv()
                fwd = copy(a, 4 + j, (*chip, c), sibling)
                fwd.start()
                passed.append(fwd)
        for a in range(n):
            copy(a, 0, sibling, (x, y, c)).wait_recv()
            for j, chip in enumerate(chips):
                copy(a, 4 + j, (*chip, 1 - c), (x, y, c)).wait_recv()
        for cp in sends + passed:
            cp.wait_send()
        for loc in locals_:
            loc.wait()

    any_spec = pl.BlockSpec(memory_space=pl.ANY)
    res = pl.pallas_call(
        body, name=name, in_specs=[any_spec] * n, out_specs=[any_spec] * n + [pl.BlockSpec(memory_space=pltpu.VMEM)],
        out_shape=[jax.ShapeDtypeStruct((N_DEV,) + a.shape, a.dtype) for a in arrays]
        + [jax.ShapeDtypeStruct((8, 128), F32)],
        scratch_shapes=[pltpu.SemaphoreType.DMA((7 * n,)), pltpu.SemaphoreType.DMA((7 * n,)),
                        pltpu.SemaphoreType.DMA((n,))],
        compiler_params=pltpu.CompilerParams(has_side_effects=True),
    )(*arrays)
    return res[:n], res[n][0, 0]


def _adamw_math(w, g, m, v):
    m2 = ADAM_B1 * m + (1.0 - ADAM_B1) * g
    v2 = ADAM_B2 * v + (1.0 - ADAM_B2) * (g * g)
    m_hat = m2 / (1.0 - ADAM_B1 ** ADAM_STEP)
    v_hat = v2 / (1.0 - ADAM_B2 ** ADAM_STEP)
    delta = -ADAM_LR * (m_hat / (jnp.sqrt(v_hat) + ADAM_EPS) + ADAM_WD * w)
    return delta, m2, v2


def _adamw_slabs(name, src, land, me, w, m, v, tr):
    R, C = w.shape

    def body(me_ref, own_ref, land_ref, w_ref, m_ref, v_ref, g_ref, d_ref, m2_ref, v2_ref):
        g = own_ref[0].astype(F32)
        for s in range(N_DEV - 1):
            g = g + land_ref[s].astype(F32)
        delta, m2, v2 = _adamw_math(w_ref[...], g, m_ref[...], v_ref[...])
        g_ref[...] = g
        d_ref[...] = delta
        m2_ref[...] = m2
        v2_ref[...] = v2

    im = lambda i, me_ref: (i, 0)
    grid_spec = pltpu.PrefetchScalarGridSpec(
        num_scalar_prefetch=1, grid=(R // tr,),
        in_specs=[pl.BlockSpec((1, tr, C), lambda i, me_ref: (me_ref[0], i, 0)),
                  pl.BlockSpec((N_DEV - 1, tr, C), lambda i, me_ref: (0, i, 0)),
                  pl.BlockSpec((tr, C), im), pl.BlockSpec((tr, C), im), pl.BlockSpec((tr, C), im)],
        out_specs=[pl.BlockSpec((tr, C), im)] * 4)
    return pl.pallas_call(body, name=name, grid_spec=grid_spec, out_shape=[jax.ShapeDtypeStruct((R, C), F32)] * 4,
                          compiler_params=_cp(1))(me.reshape(1).astype(jnp.int32), src, land, w, m, v)


def _sum_slots(name, slots):
    _, R, C = slots.shape

    def body(s_ref, o_ref):
        g = s_ref[0]
        for s in range(1, N_DEV):
            g = g + s_ref[s]
        o_ref[...] = g

    return _rows(name, R, R, [(slots, (N_DEV, R, C), lambda i: (0, 0, 0))],
                 [(jax.ShapeDtypeStruct((R, C), F32), (R, C), lambda i: (0, 0))], body)[0]


def _adamw_packed(name, g, w, m, v):
    R, C = g.shape

    def body(g_ref, w_ref, m_ref, v_ref, d_ref, m2_ref, v2_ref):
        delta, m2, v2 = _adamw_math(w_ref[...], g_ref[...], m_ref[...], v_ref[...])
        d_ref[...] = delta
        m2_ref[...] = m2
        v2_ref[...] = v2

    im = lambda i: (0, 0)
    sds = jax.ShapeDtypeStruct((R, C), F32)
    return _rows(name, R, R, [(a, (R, C), im) for a in (g, w, m, v)], [(sds, (R, C), im)] * 3, body)


def _pack(arrays):
    rows = []
    for a in arrays:
        flat = a.reshape(-1).astype(F32)
        pad = (-flat.shape[0]) % 128
        rows.append(jnp.pad(flat, (0, pad)).reshape(-1, 128))
    out = jnp.concatenate(rows, axis=0)
    return jnp.pad(out, ((0, (-out.shape[0]) % 8), (0, 0)))


def _unpack(packed, shapes):
    lead = packed.shape[:-2]
    outs = []
    r = 0
    for shp in shapes:
        n = math.prod(shp)
        nr = -(-n // 128)
        flat = packed[..., r:r + nr, :].reshape(lead + (nr * 128,))[..., :n]
        outs.append(flat.reshape(lead + tuple(shp)))
        r += nr
    return outs


FFN1_BIG = ["ffn1_w_gate", "ffn1_w_up", "ffn1_w_down"]
MIX_BIG = ["w_in", "w_out"]
FFN2_BIG = ["ffn2_w_gate", "ffn2_w_up", "ffn2_w_down"]
BIG = FFN1_BIG + MIX_BIG + FFN2_BIG
COL_SHARDED = {"ffn1_w_gate", "ffn1_w_up", "w_in", "ffn2_w_gate", "ffn2_w_up"}
SMALL_SHARDED = ["rg_conv_w", "rg_gate_a_b", "rg_gate_x_b", "rg_lambda", "gdn_conv_w"]
WEIGHTS = ["ffn1_norm", "ffn1_w_gate", "ffn1_w_up", "ffn1_w_down", "mix_norm", "w_in", "w_out", "rg_conv_w", "rg_conv_b",
           "rg_gate_a_w", "rg_gate_a_b", "rg_gate_x_w", "rg_gate_x_b", "rg_lambda", "gdn_conv_w", "gdn_a_log",
           "gdn_dt_bias", "gdn_norm", "ffn2_norm", "ffn2_w_gate", "ffn2_w_up", "ffn2_w_down", "final_norm"]
SMALL = [n for n in WEIGHTS if n not in BIG]
ROW_VECTORS = {"ffn1_norm", "mix_norm", "ffn2_norm", "gdn_norm", "rg_conv_b", "final_norm"}
ROW_TILE = {"ffn1_w_gate": 256, "ffn1_w_up": 256, "ffn1_w_down": 176, "w_in": 256, "w_out": 64,
            "ffn2_w_gate": 256, "ffn2_w_up": 256, "ffn2_w_down": 176}


def _unshard_cols(g):
    return g.transpose(1, 0, 2).reshape(g.shape[1], N_DEV * g.shape[2])


def _to_slabs(name, g):
    if name in COL_SHARDED:
        r, ctot = g.shape
        return g.reshape(r, N_DEV, ctot // N_DEV).transpose(1, 0, 2)
    return g.reshape(N_DEV, g.shape[0] // N_DEV, g.shape[1])


def _step(x, target, w, m, v):
    _, _, _, me = _mesh_pos()
    def unshard(n, gth):
        full = _unshard_cols(gth) if n in COL_SHARDED else gth.reshape(-1, gth.shape[-1])
        return jnp.pad(full, ((0, 0), (0, D_IN_PAD - D_IN))) if n == "w_in" else full

    def landed(started, name, after):
        srcs, lands = _exchange_wait(name, started, after)
        def with_own(src, land):
            slot = lax.broadcasted_iota(jnp.int32, (N_DEV,) + (1,) * src.ndim, 0)
            return jnp.where(slot == me, src[None], land)

        return [with_own(src, land) for src, land in zip(srcs, lands)]

    first, tok = _all_gather("gather_ffn1", [w[n].astype(BF16) for n in FFN1_BIG])
    W = {n: unshard(n, gth) for n, gth in zip(FFN1_BIG, first)}
    small_shards = [w[n] for n in SMALL_SHARDED]
    st_mix = _exchange_start("gather_mix_start", False,
                             [(w[n] + tok).astype(BF16) for n in MIX_BIG] + [_pack(small_shards) + tok])
    st_ffn2 = _exchange_start("gather_ffn2_start", False, [(w[n] + tok).astype(BF16) for n in FFN2_BIG])
    for n in SMALL:
        if n not in SMALL_SHARDED:
            W[n] = w[n]
    W["ffn1_norm"] = w["ffn1_norm"] + (st_mix["token"] + st_ffn2["token"])

    def more(stage, after):
        if stage == "ffn2":
            return {n: unshard(n, gth) for n, gth in zip(FFN2_BIG, landed(st_ffn2, "gather_ffn2_wait", after))}
        got = landed(st_mix, "gather_mix_wait", after)
        new = {n: unshard(n, gth) for n, gth in zip(MIX_BIG, got)}
        for n, gth in zip(SMALL_SHARDED, _unpack(got[-1], [s.shape for s in small_shards])):
            new[n] = jnp.moveaxis(gth, 0, -2).reshape(gth.shape[1:-1] + (N_DEV * gth.shape[-1],))
        return new

    R = _layer_fwd(x, target, W, more)
    W = R["W"]
    pending = []

    def emit_big(**named):
        slabs = [_to_slabs(n, g[:, :D_IN] if n == "w_in" else g) for n, g in named.items()]
        started = _exchange_start(f"scatter_start_{len(pending)}", True, slabs)
        pending.append((list(named), started))
        return started["token"]

    small_started = []

    def emit_small(G):
        packed = _pack([G[n] for n in SMALL if n != "ffn1_norm"])
        small_started.append(_exchange_start("gather_small_start", False, [packed]))

    grad_x, G = _layer_bwd(x, W, R, emit_big, emit_small)
    loss = lax.psum(R["loss"][0, 0], ("x", "y", "c"))
    out = {}
    for i, (names, started) in enumerate(pending):
        srcs, lands = _exchange_wait(f"scatter_wait_{i}", started, grad_x)
        for n, src, land in zip(names, srcs, lands):
            out[n] = _adamw_slabs(f"adamw_{n}", src, land, me, w[n], m[n], v[n], ROW_TILE[n])
    early = [n for n in SMALL if n != "ffn1_norm"]
    srcs, lands = _exchange_wait("gather_small_wait", small_started[0], grad_x)
    slot = lax.broadcasted_iota(jnp.int32, (N_DEV, 1, 1), 0)
    slots = jnp.where(slot == me, srcs[0][None], lands[0])
    reduced = dict(zip(early, _unpack(_sum_slots("sum_small_grads", slots), [G[n].shape for n in early])))
    late = _all_gather("gather_ffn1_norm_grad", [_pack([G["ffn1_norm"]])])[0][0]
    reduced["ffn1_norm"] = _unpack(_sum_slots("sum_ffn1_norm_grad", late), [G["ffn1_norm"].shape])[0]
    g_small = []
    for n in SMALL:
        g = reduced[n]
        if n in SMALL_SHARDED:
            per = g.shape[-1] // N_DEV
            g = lax.dynamic_slice_in_dim(g, me * per, per, axis=g.ndim - 1)
        g_small.append(g.reshape(w[n].shape))
    shapes = [w[n].shape for n in SMALL]
    d_p, m_p, v_p = _adamw_packed("adamw_small", _pack(g_small), _pack([w[n] for n in SMALL]),
                                  _pack([m[n] for n in SMALL]), _pack([v[n] for n in SMALL]))
    for n, g, d_, m_, v_ in zip(SMALL, g_small, _unpack(d_p, shapes), _unpack(m_p, shapes), _unpack(v_p, shapes)):
        out[n] = (g, d_, m_, v_)
    return loss, grad_x, out


def kernel(x, ffn1_norm, ffn1_w_gate, ffn1_w_up, ffn1_w_down, mix_norm, w_in, w_out, rg_conv_w, rg_conv_b, rg_gate_a_w, rg_gate_a_b, rg_gate_x_w, rg_gate_x_b, rg_lambda, gdn_conv_w, gdn_a_log, gdn_dt_bias, gdn_norm, ffn2_norm, ffn2_w_gate, ffn2_w_up, ffn2_w_down, final_norm, loss_target, m_ffn1_norm, m_ffn1_w_gate, m_ffn1_w_up, m_ffn1_w_down, m_mix_norm, m_w_in, m_w_out, m_rg_conv_w, m_rg_conv_b, m_rg_gate_a_w, m_rg_gate_a_b, m_rg_gate_x_w, m_rg_gate_x_b, m_rg_lambda, m_gdn_conv_w, m_gdn_a_log, m_gdn_dt_bias, m_gdn_norm, m_ffn2_norm, m_ffn2_w_gate, m_ffn2_w_up, m_ffn2_w_down, m_final_norm, v_ffn1_norm, v_ffn1_w_gate, v_ffn1_w_up, v_ffn1_w_down, v_mix_norm, v_w_in, v_w_out, v_rg_conv_w, v_rg_conv_b, v_rg_gate_a_w, v_rg_gate_a_b, v_rg_gate_x_w, v_rg_gate_x_b, v_rg_lambda, v_gdn_conv_w, v_gdn_a_log, v_gdn_dt_bias, v_gdn_norm, v_ffn2_norm, v_ffn2_w_gate, v_ffn2_w_up, v_ffn2_w_down, v_final_norm):
    args = dict(locals())
    orig_shapes = {n: args[n].shape for n in WEIGHTS}

    def local(prefix):
        d = {}
        for n in WEIGHTS:
            a = args[prefix + n]
            d[n] = a.reshape(1, -1) if n in ROW_VECTORS else a[0]
        return d

    loss, grad_x, out = _step(x[0], loss_target[0], local(""), local("m_"), local("v_"))
    res = [loss, grad_x[None]]
    for k in range(4):
        res += [out[n][k].reshape(orig_shapes[n]) for n in WEIGHTS]
    return tuple(res)
```

```python
import functools
import math

import jax
import jax.numpy as jnp
from jax import lax
from jax.experimental import pallas as pl
from jax.experimental.pallas import tpu as pltpu

F32, BF16 = jnp.float32, jnp.bfloat16

D_MODEL = 1024
D_FF = 2816
RG_W = 512
RG_BLOCKS = 8
RG_BLOCK = 64
RG_C = 8.0
CONV_W = 4
GDN_H = 4
GDN_DK = 128
CHUNK = 64
EPS = 1e-6
D_IN = 3088
D_IN_PAD = 3200
COL_BA = 3072
N_DEV = 8
HALO = 8
VMEM_LIMIT = 48 * 1024 * 1024

ADAM_LR = 0.001
ADAM_B1 = 0.9
ADAM_B2 = 0.999
ADAM_EPS = 1e-08
ADAM_WD = 0.01
ADAM_STEP = 10

HI = lax.Precision.HIGHEST


def _cp(n):
    return pltpu.CompilerParams(dimension_semantics=("arbitrary",) * n, vmem_limit_bytes=VMEM_LIMIT)


def _tile(n, pref):
    return min(n, pref)


def _sigmoid(x):
    return 0.5 * jnp.tanh(0.5 * x) + 0.5


def _softplus(x):
    return jnp.maximum(x, 0.0) + jnp.log(1.0 + jnp.exp(-jnp.abs(x)))


def _dot(a, b, ca, cb, prec=None):
    return lax.dot_general(a, b, (((ca,), (cb,)), ((), ())), preferred_element_type=F32, precision=prec)


def _fused_mm(name, M, N, K, tm, tn, tk, ops, pairs, extras, outs, epilogue):
    nm, nn, nk = M // tm, N // tn, K // tk
    assert nm * tm == M and nn * tn == N and nk * tk == K, (name, M, N, K, tm, tn, tk)
    spec_of = {
        "mk": pl.BlockSpec((tm, tk), lambda i, j, k: (i, k)),
        "km": pl.BlockSpec((tk, tm), lambda i, j, k: (k, i)),
        "kn": pl.BlockSpec((tk, tn), lambda i, j, k: (k, j)),
        "nk": pl.BlockSpec((tn, tk), lambda i, j, k: (j, k)),
    }
    in_specs = [spec_of[m] for _, m in ops]
    in_specs += [pl.BlockSpec(bs, lambda i, j, k, im=im: im(i, j)) for _, bs, im in extras]
    out_specs = [pl.BlockSpec(bs, lambda i, j, k, im=im: im(i, j)) for _, bs, im in outs]
    n_ops, n_ex, n_out = len(ops), len(extras), len(outs)
    n_acc = 1 + max(g for _, _, g in pairs)
    modes = [m for _, m in ops]

    def body(*refs):
        op_refs = refs[:n_ops]
        ex_refs = refs[n_ops:n_ops + n_ex]
        out_refs = refs[n_ops + n_ex:n_ops + n_ex + n_out]
        accs = refs[n_ops + n_ex + n_out:]
        i = pl.program_id(0)
        k = pl.program_id(2)
        def dots():
            vals = [r[...].astype(BF16) for r in op_refs]
            for ia, ib, g in pairs:
                yield g, _dot(vals[ia], vals[ib], 1 if modes[ia] == "mk" else 0, 0 if modes[ib] == "kn" else 1)

        if nk == 1:
            sums = [None] * n_acc
            for g, d in dots():
                sums[g] = d if sums[g] is None else sums[g] + d
            epilogue(i, [_Held(s) for s in sums], ex_refs, out_refs)
            return

        @pl.when(k == 0)
        def _():
            for a in accs:
                a[...] = jnp.zeros_like(a)

        for g, d in dots():
            accs[g][...] += d

        @pl.when(k == nk - 1)
        def _():
            epilogue(i, accs, ex_refs, out_refs)

    res = pl.pallas_call(
        body, name=name, grid=(nm, nn, nk), in_specs=in_specs, out_specs=out_specs,
        out_shape=[o for o, _, _ in outs],
        scratch_shapes=[pltpu.VMEM((tm, tn), F32)] * (n_acc if nk > 1 else 0),
        compiler_params=_cp(3),
    )(*[a for a, _ in ops], *[a for a, _, _ in extras])
    return res


class _Held:
    def __init__(self, value):
        self.value = value

    def __getitem__(self, idx):
        return self.value[idx]


def _mn(i, j):
    return (i, j)


def _row0(i, j):
    return (0, 0)


def _rows(name, S, ts, ins, outs, body, scratch=()):
    return pl.pallas_call(
        body, name=name, grid=(S // ts,),
        in_specs=[pl.BlockSpec(bs, im) for _, bs, im in ins],
        out_specs=[pl.BlockSpec(bs, im) for _, bs, im in outs],
        out_shape=[o for o, _, _ in outs],
        scratch_shapes=list(scratch),
        compiler_params=_cp(1),
    )(*[a for a, _, _ in ins])


def _halo_ins(arr, S, ts, width, colblk):
    per = ts // HALO
    last = S // HALO - 1
    return [
        (arr, (ts, width), lambda i: (i, colblk)),
        (arr, (HALO, width), lambda i: (jnp.maximum(i * per - 1, 0), colblk)),
        (arr, (HALO, width), lambda i: (jnp.minimum((i + 1) * per, last), colblk)),
    ]


def _ext(main_ref, prev_ref, next_ref, i, n_tiles):
    prev = jnp.where(i > 0, prev_ref[...].astype(F32), 0.0)
    nxt = jnp.where(i < n_tiles - 1, next_ref[...].astype(F32), 0.0)
    return jnp.concatenate([prev, main_ref[...].astype(F32), nxt], axis=0)


def _shift(ext, off, ts):
    n = ext.shape[0]
    if off == 0:
        return ext[HALO:HALO + ts]
    return pltpu.roll(ext, (-off) % n, 0)[HALO:HALO + ts]


def _rmsnorm_fwd(name, x, g):
    S, D = x.shape
    ts = _tile(S, 512)

    def body(x_ref, g_ref, o_ref):
        xv = x_ref[...]
        r = lax.rsqrt(jnp.mean(xv * xv, axis=-1, keepdims=True) + EPS)
        o_ref[...] = (xv * r * g_ref[...]).astype(BF16)

    return _rows(name, S, ts,
                 [(x, (ts, D), lambda i: (i, 0)), (g, (1, D), lambda i: (0, 0))],
                 [(jax.ShapeDtypeStruct((S, D), BF16), (ts, D), lambda i: (i, 0))], body)[0]


def _rmsnorm_bwd_tile(dh, x, g):
    r = lax.rsqrt(jnp.mean(x * x, axis=-1, keepdims=True) + EPS)
    xhat = x * r
    dxn = dh * g
    dx = r * (dxn - xhat * jnp.mean(dxn * xhat, axis=-1, keepdims=True))
    return dx, dh * xhat


def _ffn_fwd(tag, x, h, wg, wu, wd):
    S = x.shape[0]
    tm = _tile(S, 512)
    tn = 1408

    def epi_up(i, accs, ex, out):
        a = accs[0][...]
        b = accs[1][...]
        s = _sigmoid(a)
        sa = a * s
        out[0][...] = sa.astype(BF16)
        out[1][...] = (b * (s * (1.0 + a * (1.0 - s)))).astype(BF16)
        out[2][...] = (sa * b).astype(BF16)

    sds = jax.ShapeDtypeStruct((S, D_FF), BF16)
    a, b, f = _fused_mm(f"{tag}_up", S, D_FF, D_MODEL, tm, tn, D_MODEL,
                        [(h, "mk"), (wg, "kn"), (wu, "kn")], [(0, 1, 0), (0, 2, 1)], [],
                        [(sds, (tm, tn), _mn)] * 3, epi_up)

    def epi_down(i, accs, ex, out):
        out[0][...] = ex[0][...] + 0.5 * accs[0][...]

    xo = _fused_mm(f"{tag}_down", S, D_MODEL, D_FF, tm, D_MODEL, 1408,
                   [(f, "mk"), (wd, "kn")], [(0, 1, 0)], [(x, (tm, D_MODEL), _mn)],
                   [(jax.ShapeDtypeStruct((S, D_MODEL), F32), (tm, D_MODEL), _mn)], epi_down)[0]
    return xo, a, b, f


def _conv_taps(ext, w_ref, ts):
    acc = None
    for j in range(CONV_W):
        term = w_ref[j:j + 1, :] * _shift(ext, j - 2, ts)
        acc = term if acc is None else acc + term
    return acc


def _l2norm_heads(s, scale):
    outs = []
    for h in range(GDN_H):
        sh = s[:, h * GDN_DK:(h + 1) * GDN_DK]
        outs.append(sh * (lax.rsqrt(jnp.sum(sh * sh, axis=-1, keepdims=True) + EPS) * scale))
    return jnp.concatenate(outs, axis=-1)


def _conv_fwd(name, p, colblk, w, bias, mode):
    S = p.shape[0]
    ts = _tile(S, 512)
    n_tiles = S // ts
    C = w.shape[1]

    def body(main, prev, nxt, w_ref, b_ref, o_ref):
        i = pl.program_id(0)
        c = _conv_taps(_ext(main, prev, nxt, i, n_tiles), w_ref, ts)
        if mode == "bias":
            o_ref[...] = c + b_ref[...]
        else:
            s = c * _sigmoid(c)
            if mode == "q":
                s = _l2norm_heads(s, GDN_DK ** -0.5)
            elif mode == "k":
                s = _l2norm_heads(s, 1.0)
            o_ref[...] = s

    ins = _halo_ins(p, S, ts, C, colblk) + [(w, (CONV_W, C), lambda i: (0, 0)), (bias, (1, C), lambda i: (0, 0))]
    return _rows(name, S, ts, ins, [(jax.ShapeDtypeStruct((S, C), F32), (ts, C), lambda i: (i, 0))], body)[0]


def _rg_gate_terms(pre, xc, prm_ref, d):
    r = _sigmoid(pre[:, d * 1024:d * 1024 + RG_W] + prm_ref[2 * d:2 * d + 1, :])
    ig = _sigmoid(pre[:, d * 1024 + RG_W:(d + 1) * 1024] + prm_ref[2 * d + 1:2 * d + 2, :])
    sp = _softplus(-prm_ref[4 + d:5 + d, :])
    log_a = -RG_C * r * sp
    a = jnp.exp(log_a)
    t = jnp.tanh(log_a)
    sq = jnp.sqrt(-2.0 * t / (1.0 - t))
    return r, ig, sp, a, sq


def _rg_gates_fwd(xc, bd, prm):
    S = xc.shape[0]
    tm = _tile(S, 256)

    def epi(i, accs, ex, out):
        pre = accs[0][...]
        xv = ex[0][...]
        for d in range(2):
            r, ig, sp, a, sq = _rg_gate_terms(pre, xv, ex[1], d)
            out[2 * d][...] = a
            out[2 * d + 1][...] = sq * ig * xv

    sds = jax.ShapeDtypeStruct((S, RG_W), F32)
    blk = (tm, RG_W)
    im = lambda i, j: (i, 0)
    return _fused_mm("rg_gates_fwd", S, 4 * RG_W, RG_W, tm, 4 * RG_W, RG_W,
                     [(xc, "mk"), (bd, "kn")], [(0, 1, 0)],
                     [(xc, blk, im), (prm, (8, RG_W), _row0)], [(sds, blk, im)] * 4, epi)


SUBLANES = 8


def _scan_rows(a, b, reverse):
    rows = lax.broadcasted_iota(jnp.int32, a.shape, 0)
    s = 1
    while s < SUBLANES:
        shift = SUBLANES - s if reverse else s
        a_sh = pltpu.roll(a, shift, 0)
        b_sh = pltpu.roll(b, shift, 0)
        valid = (rows < SUBLANES - s) if reverse else (rows >= s)
        b = jnp.where(valid, a * b_sh + b, b)
        a = jnp.where(valid, a * a_sh, a)
        s *= 2
    return a, b


def _rg_scan(name, a_f, b_f, a_b, b_b):
    S, C = a_f.shape
    ts = _tile(S, 512)
    n_tiles = S // ts

    def body(af, bf, ab, bb, hf, hb, carry):
        @pl.when(pl.program_id(0) == 0)
        def _():
            carry[...] = jnp.zeros_like(carry)

        n_sub = ts // SUBLANES

        def step(j, c):
            cf, cb = c
            r0 = pl.multiple_of(j * SUBLANES, SUBLANES)
            cum_a, h0 = _scan_rows(af[pl.ds(r0, SUBLANES), :], bf[pl.ds(r0, SUBLANES), :], False)
            h = h0 + cum_a * cf
            hf[pl.ds(r0, SUBLANES), :] = h
            cf = h[SUBLANES - 1:SUBLANES, :]
            r1 = pl.multiple_of((n_sub - 1 - j) * SUBLANES, SUBLANES)
            cum_a, h0 = _scan_rows(ab[pl.ds(r1, SUBLANES), :], bb[pl.ds(r1, SUBLANES), :], True)
            h = h0 + cum_a * cb
            hb[pl.ds(r1, SUBLANES), :] = h
            cb = h[0:1, :]
            return cf, cb

        cf, cb = lax.fori_loop(0, n_sub, step, (carry[0:1, :], carry[1:2, :]), unroll=4)
        carry[0:1, :] = cf
        carry[1:2, :] = cb

    fw = lambda i: (i, 0)
    bw = lambda i: (n_tiles - 1 - i, 0)
    sds = jax.ShapeDtypeStruct((S, C), F32)
    return _rows(name, S, ts,
                 [(a_f, (ts, C), fw), (b_f, (ts, C), fw), (a_b, (ts, C), bw), (b_b, (ts, C), bw)],
                 [(sds, (ts, C), fw), (sds, (ts, C), bw)], body, scratch=[pltpu.VMEM((8, C), F32)])


def _tri_masks():
    ri = lax.broadcasted_iota(jnp.int32, (CHUNK, CHUNK), 0)
    ci = lax.broadcasted_iota(jnp.int32, (CHUNK, CHUNK), 1)
    return ri, ci


def _gdn_prep_fwd(p, prm):
    S = p.shape[0]
    ts = _tile(S, 512)

    def body(p_ref, prm_ref, o_ref):
        raw = p_ref[...]
        lane = lax.broadcasted_iota(jnp.int32, (1, 128), 1)
        g = -jnp.exp(prm_ref[0:1, :]) * _softplus(raw + prm_ref[1:2, :])
        g = jnp.where((lane >= 8) & (lane < 16), g, 0.0)
        beta = _sigmoid(raw)
        ri, ci = _tri_masks()
        lower = (ri >= ci).astype(F32)
        upper = (ri <= ci).astype(F32)
        for c in range(ts // CHUNK):
            rows = slice(c * CHUNK, (c + 1) * CHUNK)
            gch = g[rows]
            gc = jnp.where(lane < 12, _dot(lower, gch, 1, 0, HI), _dot(upper, gch, 1, 0, HI))
            o_ref[rows, :] = jnp.where(lane < 8, beta[rows], gc)

    return _rows("gdn_prep_fwd", S, ts,
                 [(p, (ts, 128), lambda i: (i, COL_BA // 128)), (prm, (8, 128), lambda i: (0, 0))],
                 [(jax.ShapeDtypeStruct((S, 128), F32), (ts, 128), lambda i: (i, 0))], body)[0]


def _bdot(a, b, ca, cb):
    return _dot(a.astype(BF16), b.astype(BF16), ca, cb)


GDN_W = GDN_H * GDN_DK
GDN_TS = 256


def _gdn_decay(bg_ref, gcr_ref, c, rows, r0, col, rev, ri, ci):
    beta = bg_ref[rows, col:col + 1]
    gc = bg_ref[rows, 8 + col:9 + col]
    last = 0 if rev else CHUNK - 1
    gl = bg_ref[pl.ds(r0 + last, 1), 8 + col:9 + col]
    out = dict(beta=beta, gc=gc, gl=gl, eg=jnp.exp(gc), egl=jnp.exp(gl - gc), cd=jnp.exp(gl))
    if gcr_ref is not None:
        incl = (ri <= ci) if rev else (ri >= ci)
        out["strict"] = (ri < ci) if rev else (ri > ci)
        out["dm"] = jnp.where(incl, jnp.exp(jnp.where(incl, gc - gcr_ref[c, col:col + 1, :], 0.0)), 0.0)
    return out


def _dir_tile(d, n_tiles, flip):
    if (d == 1) != flip:
        return lambda i: n_tiles - 1 - i
    return lambda i: i


def _gdn_local_fwd(q, k, v, bg, gcr):
    S = q.shape[0]
    ts = _tile(S, GDN_TS)
    ncb = ts // CHUNK
    nch = S // CHUNK

    def body(q_ref, k_ref, v_ref, bg_ref, gcr_ref, *out_refs):
        ri, ci = _tri_masks()
        eye = (ri == ci).astype(F32)
        outs = (out_refs[0:6], out_refs[6:12])
        cd_ref = out_refs[12]

        def chunk(c, carry):
            r0 = pl.multiple_of(c * CHUNK, CHUNK)
            rows = pl.ds(r0, CHUNK)
            chains = []
            for h in range(GDN_H):
                cols = slice(h * GDN_DK, (h + 1) * GDN_DK)
                qh, kh, vh = q_ref[rows, cols], k_ref[rows, cols], v_ref[rows, cols]
                both = _bdot(jnp.concatenate([qh, kh], axis=0), kh, 1, 1)
                for d in range(2):
                    chains.append(dict(h=h, d=d, cols=cols, qh=qh, kh=kh, vh=vh, qk=both[0:CHUNK],
                                       kk=both[CHUNK:2 * CHUNK]))
            for ch in chains:
                m = _gdn_decay(bg_ref, gcr_ref, c, rows, r0, ch["d"] * GDN_H + ch["h"], ch["d"] == 1, ri, ci)
                ch["m"] = m
                ch["x"] = -jnp.where(m["strict"], m["beta"] * ch["kk"] * m["dm"], 0.0)
                ch["t"] = eye + ch["x"]
            for ch in chains:
                ch["pw"] = _bdot(ch["x"], ch["x"], 1, 0)
            for level in range(1, 6):
                last_level = level == 5
                for ch in chains:
                    rhs = ch["t"] if last_level else jnp.concatenate([ch["t"], ch["pw"]], axis=1)
                    ch["prod"] = _bdot(ch["pw"], rhs, 1, 0)
                for ch in chains:
                    ch["t"] = ch["t"] + ch["prod"][:, 0:CHUNK]
                    if not last_level:
                        ch["pw"] = ch["prod"][:, CHUNK:2 * CHUNK]
            for ch in chains:
                m = ch["m"]
                rhs = jnp.concatenate([ch["vh"] * m["beta"], ch["kh"] * (m["beta"] * m["eg"])], axis=1)
                ch["uw"] = _bdot(ch["t"], rhs, 1, 0)
            for ch in chains:
                u_ref, w_ref, a_ref, t_ref, qd_ref, kd_ref = outs[ch["d"]]
                m = ch["m"]
                col = ch["d"] * GDN_H + ch["h"]
                u_ref[rows, ch["cols"]] = ch["uw"][:, 0:GDN_DK]
                w_ref[rows, ch["cols"]] = ch["uw"][:, GDN_DK:2 * GDN_DK].astype(BF16)
                a_ref[c, ch["h"]] = (ch["qk"] * m["dm"]).astype(BF16)
                t_ref[c, ch["h"]] = _bdot(ch["t"], eye, 0, 0).astype(BF16)
                qd_ref[rows, ch["cols"]] = (ch["qh"] * m["eg"]).astype(BF16)
                kd_ref[rows, ch["cols"]] = (ch["kh"] * m["egl"]).astype(BF16)
                cd_ref[c, col:col + 1, :] = jnp.broadcast_to(m["cd"], (1, 128))
            return carry

        lax.fori_loop(0, ncb, chunk, 0)

    im = lambda i: (i, 0)
    im4 = lambda i: (i, 0, 0, 0)
    ins = [(q, (ts, GDN_W), im), (k, (ts, GDN_W), im), (v, (ts, GDN_W), im), (bg, (ts, 128), im),
           (gcr, (ncb, 8, CHUNK), lambda i: (i, 0, 0))]
    per_dir = [(jax.ShapeDtypeStruct((S, GDN_W), F32), (ts, GDN_W), im),
               (jax.ShapeDtypeStruct((S, GDN_W), BF16), (ts, GDN_W), im),
               (jax.ShapeDtypeStruct((nch, GDN_H, CHUNK, CHUNK), BF16), (ncb, GDN_H, CHUNK, CHUNK), im4),
               (jax.ShapeDtypeStruct((nch, GDN_H, CHUNK, CHUNK), BF16), (ncb, GDN_H, CHUNK, CHUNK), im4),
               (jax.ShapeDtypeStruct((S, GDN_W), BF16), (ts, GDN_W), im),
               (jax.ShapeDtypeStruct((S, GDN_W), BF16), (ts, GDN_W), im)]
    cd_out = (jax.ShapeDtypeStruct((nch, 8, 128), F32), (ncb, 8, 128), lambda i: (i, 0, 0))
    res = _rows("gdn_local_fwd", S, ts, ins, per_dir * 2 + [cd_out], body)
    return res[0:6], res[6:12], res[12]


def _gdn_scan_fwd(loc):
    S = loc[0][0].shape[0]
    ts = _tile(S, GDN_TS)
    n_tiles = S // ts
    ncb = ts // CHUNK
    nch = S // CHUNK

    def body(*refs):
        ins = (refs[0:6], refs[6:12])
        outs = (refs[12:15], refs[15:18])
        state = refs[18]

        @pl.when(pl.program_id(0) == 0)
        def _():
            state[...] = jnp.zeros_like(state)

        def chunk(cc, carry):
            chains = []
            for d in range(2):
                c = cc if d == 0 else ncb - 1 - cc
                rows = pl.ds(pl.multiple_of(c * CHUNK, CHUNK), CHUNK)
                for h in range(GDN_H):
                    cols = slice(h * GDN_DK, (h + 1) * GDN_DK)
                    chains.append(dict(d=d, h=h, c=c, rows=rows, cols=cols, st=state[d * GDN_H + h]))
            for ch in chains:
                qd_ref, kd_ref, u_ref, w_ref, a_ref, cd_ref = ins[ch["d"]]
                rows, cols = ch["rows"], ch["cols"]
                lhs = jnp.concatenate([w_ref[rows, cols], qd_ref[rows, cols]], axis=0)
                ch["ws_qs"] = _dot(lhs, ch["st"].astype(BF16), 1, 0)
            for ch in chains:
                qd_ref, kd_ref, u_ref, w_ref, a_ref, cd_ref = ins[ch["d"]]
                rows, cols = ch["rows"], ch["cols"]
                vn = u_ref[rows, cols] - ch["ws_qs"][0:CHUNK]
                vnb = vn.astype(BF16)
                ch["vn"] = vn
                ch["avn"] = _dot(a_ref[ch["c"], ch["h"]], vnb, 1, 0)
                ch["kvn"] = _dot(kd_ref[rows, cols], vnb, 0, 0)
            for ch in chains:
                o_ref, vn_ref, s_ref = outs[ch["d"]]
                cd_ref = ins[ch["d"]][5]
                rows, cols = ch["rows"], ch["cols"]
                col = ch["d"] * GDN_H + ch["h"]
                o_ref[rows, cols] = ch["ws_qs"][CHUNK:2 * CHUNK] + ch["avn"]
                vn_ref[rows, cols] = ch["vn"]
                s_ref[ch["c"], ch["h"]] = ch["st"]
                state[ch["d"] * GDN_H + ch["h"]] = ch["st"] * cd_ref[ch["c"], col:col + 1, :] + ch["kvn"]
            return carry

        lax.fori_loop(0, ncb, chunk, 0)

    ins, outs = [], []
    for d in range(2):
        tix = _dir_tile(d, n_tiles, False)
        im = lambda i, tix=tix: (tix(i), 0)
        im4 = lambda i, tix=tix: (tix(i), 0, 0, 0)
        u, w, a, _, qd, kd = loc[d]
        ins += [(qd, (ts, GDN_W), im), (kd, (ts, GDN_W), im), (u, (ts, GDN_W), im), (w, (ts, GDN_W), im),
                (a, (ncb, GDN_H, CHUNK, CHUNK), im4), (loc[2], (ncb, 8, 128), lambda i, tix=tix: (tix(i), 0, 0))]
        outs += [(jax.ShapeDtypeStruct((S, GDN_W), F32), (ts, GDN_W), im),
                 (jax.ShapeDtypeStruct((S, GDN_W), F32), (ts, GDN_W), im),
                 (jax.ShapeDtypeStruct((nch, GDN_H, GDN_DK, GDN_DK), F32), (ncb, GDN_H, GDN_DK, GDN_DK), im4)]
    res = _rows("gdn_scan_fwd", S, ts, ins, outs, body, scratch=[pltpu.VMEM((2 * GDN_H, GDN_DK, GDN_DK), F32)])
    return res[0:3], res[3:6]


def _gelu(x):
    c = math.sqrt(2.0 / math.pi)
    t = jnp.tanh(c * (x + 0.044715 * x * x * x))
    return 0.5 * x * (1.0 + t), t


def _mix_out_fwd(h_f, h_b, o_f, o_b, p, gn):
    S = h_f.shape[0]
    ts = _tile(S, 512)

    def body(hf, hb, of, ob, gate, z, gn_ref, y_ref):
        ge, _ = _gelu(gate[...])
        y_ref[:, 0:RG_W] = ((hf[...] + hb[...]) * ge).astype(BF16)
        o = of[...] + ob[...]
        zv = z[...]
        sz = zv * _sigmoid(zv)
        for h in range(GDN_H):
            cols = slice(h * GDN_DK, (h + 1) * GDN_DK)
            oh = o[:, cols]
            n = oh * lax.rsqrt(jnp.mean(oh * oh, axis=-1, keepdims=True) + EPS) * gn_ref[...]
            y_ref[:, RG_W + h * GDN_DK:RG_W + (h + 1) * GDN_DK] = (n * sz[:, cols]).astype(BF16)

    blk = (ts, RG_W)
    im = lambda i: (i, 0)
    ins = [(h_f, blk, im), (h_b, blk, im), (o_f, blk, im), (o_b, blk, im),
           (p, blk, lambda i: (i, 1)), (p, blk, lambda i: (i, 5)), (gn, (1, GDN_DK), lambda i: (0, 0))]
    return _rows("mix_out_fwd", S, ts, ins,
                 [(jax.ShapeDtypeStruct((S, D_MODEL), BF16), (ts, D_MODEL), im)], body)[0]


def _loss_head(x, target, g):
    S, D = x.shape
    ts = _tile(S, 512)

    def body(x_ref, t_ref, g_ref, dx_ref, loss_ref, dg_ref):
        @pl.when(pl.program_id(0) == 0)
        def _():
            loss_ref[...] = jnp.zeros_like(loss_ref)
            dg_ref[...] = jnp.zeros_like(dg_ref)

        xv = x_ref[...]
        gv = g_ref[...]
        r = lax.rsqrt(jnp.mean(xv * xv, axis=-1, keepdims=True) + EPS)
        err = xv * r * gv - t_ref[...]
        loss_ref[...] += jnp.sum(err * err) * (0.5 / D)
        dx, dgt = _rmsnorm_bwd_tile(err * (1.0 / D), xv, gv)
        dx_ref[...] = dx
        dg_ref[...] += jnp.sum(dgt, axis=0, keepdims=True)

    im = lambda i: (i, 0)
    z = lambda i: (0, 0)
    return _rows("loss_head", S, ts,
                 [(x, (ts, D), im), (target, (ts, D), im), (g, (1, D), z)],
                 [(jax.ShapeDtypeStruct((S, D), F32), (ts, D), im),
                  (jax.ShapeDtypeStruct((8, 128), F32), (8, 128), z),
                  (jax.ShapeDtypeStruct((1, D), F32), (1, D), z)], body)


def _block_diag(w):
    n = w.shape[0]
    return jnp.einsum("nij,nm->nimj", w, jnp.eye(n, dtype=w.dtype)).reshape(n * w.shape[1], n * w.shape[2])


def _rg_bd(a_w, x_w):
    return jnp.concatenate([_block_diag(a_w[0]), _block_diag(x_w[0]), _block_diag(a_w[1]), _block_diag(x_w[1])],
                           axis=1).astype(BF16)


def _rg_prm(ba, bx, lam):
    return jnp.concatenate([ba[0:1], bx[0:1], ba[1:2], bx[1:2], lam, jnp.zeros((2, RG_W), F32)], axis=0)


def _gdn_prm(a_log, dt_bias):
    rows = jnp.zeros((8, 128), F32)
    rows = rows.at[0, 8:16].set(a_log.reshape(-1))
    return rows.at[1, 8:16].set(dt_bias.reshape(-1))


def _gc_rows(bg):
    S = bg.shape[0]
    return bg[:, 8:16].reshape(S // CHUNK, CHUNK, 8).transpose(0, 2, 1)


def _layer_fwd(x0, target, W, more=None):
    S = x0.shape[0]
    R = {}
    R["h1"] = _rmsnorm_fwd("rms1", x0, W["ffn1_norm"])
    R["x1"], R["a1"], R["b1"], R["f1"] = _ffn_fwd("ffn1", x0, R["h1"], W["ffn1_w_gate"], W["ffn1_w_up"], W["ffn1_w_down"])
    if more is not None:
        W = {**W, **more("mixer", R["x1"])}
    R["h2"] = _rmsnorm_fwd("rms2", R["x1"], W["mix_norm"])
    tm = _tile(S, 512)
    tmp = _tile(S, 1024)
    R["p"] = _fused_mm("in_proj", S, D_IN_PAD, D_MODEL, tmp, 640, D_MODEL, [(R["h2"], "mk"), (W["w_in"], "kn")],
                       [(0, 1, 0)], [], [(jax.ShapeDtypeStruct((S, D_IN_PAD), F32), (tmp, 640), _mn)],
                       lambda i, accs, ex, out: out[0].__setitem__(Ellipsis, accs[0][...]))[0]
    p = R["p"]
    R["xc"] = _conv_fwd("rg_conv_fwd", p, 0, W["rg_conv_w"], W["rg_conv_b"], "bias")
    R["bd"] = _rg_bd(W["rg_gate_a_w"], W["rg_gate_x_w"])
    R["rg_prm"] = _rg_prm(W["rg_gate_a_b"], W["rg_gate_x_b"], W["rg_lambda"])
    a_f, b_f, a_b, b_b = _rg_gates_fwd(R["xc"], R["bd"], R["rg_prm"])
    R["a_f"], R["a_b"] = a_f, a_b
    R["h_f"], R["h_b"] = _rg_scan("rg_scan_fwd", a_f, b_f, a_b, b_b)
    zero_b = jnp.zeros((1, RG_W), F32)
    cw = W["gdn_conv_w"]
    R["q"] = _conv_fwd("gdn_conv_q", p, 2, cw[:, 0:512], zero_b, "q")
    R["k"] = _conv_fwd("gdn_conv_k", p, 3, cw[:, 512:1024], zero_b, "k")
    R["v"] = _conv_fwd("gdn_conv_v", p, 4, cw[:, 1024:1536], zero_b, "v")
    R["gdn_prm"] = _gdn_prm(W["gdn_a_log"], W["gdn_dt_bias"])
    R["bg"] = _gdn_prep_fwd(p, R["gdn_prm"])
    R["gcr"] = _gc_rows(R["bg"])
    R["gdn_loc"] = _gdn_local_fwd(R["q"], R["k"], R["v"], R["bg"], R["gcr"])
    R["gdn_fwd"] = _gdn_scan_fwd(R["gdn_loc"])
    R["o_f"], R["o_b"] = R["gdn_fwd"][0][0], R["gdn_fwd"][1][0]
    R["y"] = _mix_out_fwd(R["h_f"], R["h_b"], R["o_f"], R["o_b"], p, W["gdn_norm"])
    R["x2"] = _fused_mm("out_proj", S, D_MODEL, D_MODEL, tm, D_MODEL, D_MODEL, [(R["y"], "mk"), (W["w_out"], "kn")],
                        [(0, 1, 0)], [(R["x1"], (tm, D_MODEL), _mn)],
                        [(jax.ShapeDtypeStruct((S, D_MODEL), F32), (tm, D_MODEL), _mn)],
                        lambda i, accs, ex, out: out[0].__setitem__(Ellipsis, ex[0][...] + accs[0][...]))[0]
    if more is not None:
        W = {**W, **more("ffn2", R["x2"])}
    R["h3"] = _rmsnorm_fwd("rms3", R["x2"], W["ffn2_norm"])
    R["x3"], R["a2"], R["b2"], R["f2"] = _ffn_fwd("ffn2", R["x2"], R["h3"], W["ffn2_w_gate"], W["ffn2_w_up"], W["ffn2_w_down"])
    R["dx3"], R["loss"], R["d_final_norm"] = _loss_head(R["x3"], target, W["final_norm"])
    R["W"] = W
    return R


def _colsum_into(ref, i, val):
    @pl.when(i == 0)
    def _():
        ref[...] = val

    @pl.when(i > 0)
    def _():
        ref[...] += val


def _ffn_bwd(tag, dout, x, g, h, a, b, f, wg, wu, wd, emit):
    S = x.shape[0]
    tm = _tile(S, 512)
    tk_s = _tile(S, 1024)
    dwd = _fused_mm(f"{tag}_dw_down", D_FF, D_MODEL, S, 1408, D_MODEL, tk_s, [(f, "km"), (dout, "kn")], [(0, 1, 0)], [],
                    [(jax.ShapeDtypeStruct((D_FF, D_MODEL), BF16), (1408, D_MODEL), _mn)],
                    lambda i, accs, ex, out: out[0].__setitem__(Ellipsis, (0.5 * accs[0][...]).astype(BF16)))[0]
    emit(down=dwd)

    def epi_act(i, accs, ex, out):
        df = 0.5 * accs[0][...]
        out[0][...] = (df * ex[1][...].astype(F32)).astype(BF16)
        out[1][...] = (df * ex[0][...].astype(F32)).astype(BF16)

    sds = jax.ShapeDtypeStruct((S, D_FF), BF16)
    da, db = _fused_mm(f"{tag}_dact", S, D_FF, D_MODEL, tm, 1408, D_MODEL, [(dout, "mk"), (wd, "nk")], [(0, 1, 0)],
                       [(a, (tm, 1408), _mn), (b, (tm, 1408), _mn)], [(sds, (tm, 1408), _mn)] * 2, epi_act)

    def epi_w2(i, accs, ex, out):
        out[0][...] = accs[0][...].astype(BF16)
        out[1][...] = accs[1][...].astype(BF16)

    sdw = jax.ShapeDtypeStruct((D_MODEL, D_FF), BF16)
    dwg, dwu = _fused_mm(f"{tag}_dw_up", D_MODEL, D_FF, S, D_MODEL, 1408, _tile(S, 512),
                         [(h, "km"), (da, "kn"), (db, "kn")], [(0, 1, 0), (0, 2, 1)], [],
                         [(sdw, (D_MODEL, 1408), _mn)] * 2, epi_w2)
    tok = emit(gate=dwg, up=dwu)
    if tok is not None:
        g = g + tok

    def epi_dx(i, accs, ex, out):
        dx, dgt = _rmsnorm_bwd_tile(accs[0][...], ex[0][...], ex[1][...])
        out[0][...] = ex[2][...] + dx
        _colsum_into(out[1], i, jnp.sum(dgt, axis=0, keepdims=True))

    dx, dg = _fused_mm(f"{tag}_dx", S, D_MODEL, D_FF, tm, D_MODEL, 1408,
                       [(da, "mk"), (wg, "nk"), (db, "mk"), (wu, "nk")], [(0, 1, 0), (2, 3, 0)],
                       [(x, (tm, D_MODEL), _mn), (g, (1, D_MODEL), _row0), (dout, (tm, D_MODEL), _mn)],
                       [(jax.ShapeDtypeStruct((S, D_MODEL), F32), (tm, D_MODEL), _mn),
                        (jax.ShapeDtypeStruct((1, D_MODEL), F32), (1, D_MODEL), _row0)], epi_dx)
    return dx, dg


def _mix_out_bwd(dy, h_f, h_b, o_f, o_b, p, gn):
    S = dy.shape[0]
    ts = _tile(S, 512)
    c0 = math.sqrt(2.0 / math.pi)

    def body(dy_ref, hf, hb, of, ob, gate, z, gn_ref, dhr_ref, dgate_ref, do_ref, dz_ref, dgn_ref):
        i = pl.program_id(0)
        gv = gate[...]
        ge, t = _gelu(gv)
        dy_rg = dy_ref[:, 0:RG_W]
        dhr_ref[...] = dy_rg * ge
        dgelu = 0.5 * (1.0 + t) + 0.5 * gv * (1.0 - t * t) * c0 * (1.0 + 3.0 * 0.044715 * gv * gv)
        dgate_ref[...] = (dy_rg * (hf[...] + hb[...]) * dgelu).astype(BF16)
        o = of[...] + ob[...]
        zv = z[...]
        sig = _sigmoid(zv)
        gnv = gn_ref[...]
        dgn = jnp.zeros((1, GDN_DK), F32)
        for h in range(GDN_H):
            cols = slice(h * GDN_DK, (h + 1) * GDN_DK)
            oh = o[:, cols]
            r = lax.rsqrt(jnp.mean(oh * oh, axis=-1, keepdims=True) + EPS)
            ohat = oh * r
            dyh = dy_ref[:, RG_W + h * GDN_DK:RG_W + (h + 1) * GDN_DK]
            zh = zv[:, cols]
            sh = sig[:, cols]
            dn = dyh * zh * sh
            dz_ref[:, cols] = (dyh * ohat * gnv * (sh * (1.0 + zh * (1.0 - sh)))).astype(BF16)
            dxn = dn * gnv
            do_ref[:, cols] = r * (dxn - ohat * jnp.mean(dxn * ohat, axis=-1, keepdims=True))
            dgn = dgn + jnp.sum(dn * ohat, axis=0, keepdims=True)
        _colsum_into(dgn_ref, i, dgn)

    blk = (ts, RG_W)
    im = lambda i: (i, 0)
    z0 = lambda i: (0, 0)
    ins = [(dy, (ts, D_MODEL), im), (h_f, blk, im), (h_b, blk, im), (o_f, blk, im), (o_b, blk, im),
           (p, blk, lambda i: (i, 1)), (p, blk, lambda i: (i, 5)), (gn, (1, GDN_DK), z0)]
    outs = [(jax.ShapeDtypeStruct((S, RG_W), F32), blk, im), (jax.ShapeDtypeStruct((S, RG_W), BF16), blk, im),
            (jax.ShapeDtypeStruct((S, RG_W), F32), blk, im), (jax.ShapeDtypeStruct((S, RG_W), BF16), blk, im),
            (jax.ShapeDtypeStruct((1, GDN_DK), F32), (1, GDN_DK), z0)]
    return _rows("mix_out_bwd", S, ts, ins, outs, body)


def _rg_scan_adj(name, a_up, b_up, a_dn, b_dn):
    S, C = a_up.shape
    ts = _tile(S, 512)
    n_tiles = S // ts

    def body(au, bu, ad, bd, mu_ref, lam_ref, carry):
        @pl.when(pl.program_id(0) == 0)
        def _():
            carry[...] = jnp.zeros_like(carry)

        n_sub = ts // SUBLANES
        rows = lax.broadcasted_iota(jnp.int32, (SUBLANES, C), 0)

        def half(a_ref, b_ref, out_ref, r0, c_in, reverse):
            a = a_ref[pl.ds(r0, SUBLANES), :]
            b = b_ref[pl.ds(r0, SUBLANES), :]
            cum_a, c0 = _scan_rows(a, a * b, reverse)
            c = c0 + cum_a * c_in
            edge = 0 if not reverse else SUBLANES - 1
            c_prev = jnp.where(rows == edge, c_in, pltpu.roll(c, SUBLANES - 1 if reverse else 1, 0))
            out_ref[pl.ds(r0, SUBLANES), :] = b + c_prev
            return c[0:1, :] if reverse else c[SUBLANES - 1:SUBLANES, :]

        def step(j, c):
            cu, cd = c
            cu = half(au, bu, mu_ref, pl.multiple_of(j * SUBLANES, SUBLANES), cu, False)
            cd = half(ad, bd, lam_ref, pl.multiple_of((n_sub - 1 - j) * SUBLANES, SUBLANES), cd, True)
            return cu, cd

        cu, cd = lax.fori_loop(0, n_sub, step, (carry[0:1, :], carry[1:2, :]), unroll=4)
        carry[0:1, :] = cu
        carry[1:2, :] = cd

    fw = lambda i: (i, 0)
    bw = lambda i: (n_tiles - 1 - i, 0)
    sds = jax.ShapeDtypeStruct((S, C), F32)
    return _rows(name, S, ts,
                 [(a_up, (ts, C), fw), (b_up, (ts, C), fw), (a_dn, (ts, C), bw), (b_dn, (ts, C), bw)],
                 [(sds, (ts, C), fw), (sds, (ts, C), bw)], body, scratch=[pltpu.VMEM((8, C), F32)])


def _halo_ex(arr, S, tm, width):
    per = tm // HALO
    last = S // HALO - 1
    return [
        (arr, (tm, width), lambda i, j: (i, 0)),
        (arr, (HALO, width), lambda i, j: (jnp.maximum(i * per - 1, 0), 0)),
        (arr, (HALO, width), lambda i, j: (jnp.minimum((i + 1) * per, last), 0)),
    ]


def _rg_gates_bwd(xc, bd, prm, lam_f, lam_b, h_f, h_b):
    S = xc.shape[0]
    tm = _tile(S, 256)
    n_tiles = S // tm

    def epi(i, accs, ex, out):
        pre = accs[0][...]
        xv = ex[0][...]
        prm_ref = ex[1]
        lams = (ex[2][...], ex[3][...])
        hprev = (_shift(_ext(ex[4], ex[5], ex[6], i, n_tiles), -1, tm),
                 _shift(_ext(ex[7], ex[8], ex[9], i, n_tiles), 1, tm))
        dxc = jnp.zeros_like(xv)
        rows = []
        dlam_rows = []
        for d in range(2):
            r, ig, sp, a, sq = _rg_gate_terms(pre, xv, prm_ref, d)
            lam = lams[d]
            da = lam * hprev[d]
            di = lam * sq * xv
            dxc = dxc + lam * sq * ig
            dsq = lam * ig * xv
            dlog_a = da * a - dsq * (a * a) / sq
            dpre_r = dlog_a * (-RG_C * sp) * r * (1.0 - r)
            dpre_i = di * ig * (1.0 - ig)
            out[0][:, d * 1024:d * 1024 + RG_W] = dpre_r.astype(BF16)
            out[0][:, d * 1024 + RG_W:(d + 1) * 1024] = dpre_i.astype(BF16)
            rows += [jnp.sum(dpre_r, axis=0, keepdims=True), jnp.sum(dpre_i, axis=0, keepdims=True)]
            dsp = jnp.sum(dlog_a * (-RG_C * r), axis=0, keepdims=True)
            dlam_rows.append(-dsp * _sigmoid(-prm_ref[4 + d:5 + d, :]))
        out[1][...] = dxc
        zero = jnp.zeros((2, RG_W), F32)
        _colsum_into(out[2], i, jnp.concatenate(rows + dlam_rows + [zero], axis=0))

    blk = (tm, RG_W)
    im = lambda i, j: (i, 0)
    extras = ([(xc, blk, im), (prm, (8, RG_W), _row0), (lam_f, blk, im), (lam_b, blk, im)]
              + _halo_ex(h_f, S, tm, RG_W) + _halo_ex(h_b, S, tm, RG_W))
    outs = [(jax.ShapeDtypeStruct((S, 4 * RG_W), BF16), (tm, 4 * RG_W), im),
            (jax.ShapeDtypeStruct((S, RG_W), F32), blk, im),
            (jax.ShapeDtypeStruct((8, RG_W), F32), (8, RG_W), _row0)]
    return _fused_mm("rg_gates_bwd", S, 4 * RG_W, RG_W, tm, 4 * RG_W, RG_W, [(xc, "mk"), (bd, "kn")], [(0, 1, 0)],
                     extras, outs, epi)


def _roll_rows(ext, off):
    if off == 0:
        return ext
    return pltpu.roll(ext, (-off) % ext.shape[0], 0)


def _conv_bwd(name, p, colblk, w, grads, mode):
    S = p.shape[0]
    ts = _tile(S, 512)
    n_tiles = S // ts
    C = w.shape[1]
    ng = len(grads)

    def body(*refs):
        p_refs = refs[0:3]
        g_refs = refs[3:3 + 3 * ng]
        w_ref = refs[3 + 3 * ng]
        dx_ref, dw_ref, db_ref = refs[4 + 3 * ng:]
        i = pl.program_id(0)
        ext_p = _ext(*p_refs, i, n_tiles)
        dn = _ext(*g_refs[0:3], i, n_tiles)
        for gi in range(1, ng):
            dn = dn + _ext(*g_refs[3 * gi:3 * gi + 3], i, n_tiles)
        if mode == "bias":
            dc = dn
        else:
            c = None
            for j in range(CONV_W):
                term = w_ref[j:j + 1, :] * _roll_rows(ext_p, j - 2)
                c = term if c is None else c + term
            sig = _sigmoid(c)
            s = c * sig
            if mode in ("q", "k"):
                scale = GDN_DK ** -0.5 if mode == "q" else 1.0
                parts = []
                for h in range(GDN_H):
                    cols = slice(h * GDN_DK, (h + 1) * GDN_DK)
                    sh = s[:, cols]
                    dnh = dn[:, cols]
                    rinv = lax.rsqrt(jnp.sum(sh * sh, axis=-1, keepdims=True) + EPS)
                    parts.append(scale * rinv * (dnh - sh * (rinv * rinv) * jnp.sum(dnh * sh, axis=-1, keepdims=True)))
                ds = jnp.concatenate(parts, axis=-1)
            else:
                ds = dn
            dc = ds * (sig * (1.0 + c * (1.0 - sig)))
        dx = None
        for j in range(CONV_W):
            term = w_ref[j:j + 1, :] * _shift(dc, 2 - j, ts)
            dx = term if dx is None else dx + term
        dx_ref[...] = dx.astype(BF16)
        dc_main = dc[HALO:HALO + ts]
        dw = jnp.concatenate([jnp.sum(dc_main * _shift(ext_p, j - 2, ts), axis=0, keepdims=True)
                              for j in range(CONV_W)], axis=0)
        _colsum_into(dw_ref, i, dw)
        _colsum_into(db_ref, i, jnp.sum(dc_main, axis=0, keepdims=True))

    ins = _halo_ins(p, S, ts, C, colblk)
    for garr in grads:
        ins += _halo_ins(garr, S, ts, C, 0)
    ins += [(w, (CONV_W, C), lambda i: (0, 0))]
    z0 = lambda i: (0, 0)
    outs = [(jax.ShapeDtypeStruct((S, C), BF16), (ts, C), lambda i: (i, 0)),
            (jax.ShapeDtypeStruct((CONV_W, C), F32), (CONV_W, C), z0),
            (jax.ShapeDtypeStruct((1, C), F32), (1, C), z0)]
    return _rows(name, S, ts, ins, outs, body)


def _gdn_scan_bwd(loc, do):
    S = do.shape[0]
    ts = _tile(S, GDN_TS)
    n_tiles = S // ts
    ncb = ts // CHUNK
    nch = S // CHUNK

    def body(*refs):
        ins = (refs[0:6], refs[6:12])
        outs = (refs[12:14], refs[14:16])
        dstate = refs[16]

        @pl.when(pl.program_id(0) == 0)
        def _():
            dstate[...] = jnp.zeros_like(dstate)

        def chunk(cc, carry):
            chains = []
            for d in range(2):
                c = ncb - 1 - cc if d == 0 else cc
                rows = pl.ds(pl.multiple_of(c * CHUNK, CHUNK), CHUNK)
                for h in range(GDN_H):
                    cols = slice(h * GDN_DK, (h + 1) * GDN_DK)
                    chains.append(dict(d=d, h=h, c=c, rows=rows, cols=cols, dsn=dstate[d * GDN_H + h]))
            for ch in chains:
                qd_ref, kd_ref, cd_ref, w_ref, a_ref, do_ref = ins[ch["d"]]
                rows, cols = ch["rows"], ch["cols"]
                dob = do_ref[rows, cols].astype(BF16)
                ch["dvn"] = (_dot(a_ref[ch["c"], ch["h"]], dob, 0, 0)
                             + _dot(kd_ref[rows, cols], ch["dsn"].astype(BF16), 1, 0))
                ch["qdo"] = _dot(qd_ref[rows, cols], dob, 0, 0)
            for ch in chains:
                w_ref = ins[ch["d"]][3]
                ch["wdvn"] = _dot(w_ref[ch["rows"], ch["cols"]], ch["dvn"].astype(BF16), 0, 0)
            for ch in chains:
                dvn_ref, ds_ref = outs[ch["d"]]
                cd_ref = ins[ch["d"]][2]
                col = ch["d"] * GDN_H + ch["h"]
                dvn_ref[ch["rows"], ch["cols"]] = ch["dvn"]
                ds_ref[ch["c"], ch["h"]] = ch["dsn"]
                dstate[ch["d"] * GDN_H + ch["h"]] = (ch["qdo"] + cd_ref[ch["c"], col:col + 1, :] * ch["dsn"]
                                                     - ch["wdvn"])
            return carry

        lax.fori_loop(0, ncb, chunk, 0)

    ins, outs = [], []
    for d in range(2):
        tix = _dir_tile(d, n_tiles, True)
        im = lambda i, tix=tix: (tix(i), 0)
        im4 = lambda i, tix=tix: (tix(i), 0, 0, 0)
        _, w, a, _, qd, kd = loc[d]
        ins += [(qd, (ts, GDN_W), im), (kd, (ts, GDN_W), im), (loc[2], (ncb, 8, 128), lambda i, tix=tix: (tix(i), 0, 0)),
                (w, (ts, GDN_W), im), (a, (ncb, GDN_H, CHUNK, CHUNK), im4), (do, (ts, GDN_W), im)]
        outs += [(jax.ShapeDtypeStruct((S, GDN_W), F32), (ts, GDN_W), im),
                 (jax.ShapeDtypeStruct((nch, GDN_H, GDN_DK, GDN_DK), F32), (ncb, GDN_H, GDN_DK, GDN_DK), im4)]
    res = _rows("gdn_scan_bwd", S, ts, ins, outs, body, scratch=[pltpu.VMEM((2 * GDN_H, GDN_DK, GDN_DK), F32)])
    return res[0:2], res[2:4]


def _gdn_local_bwd(q, k, v, bg, gcr, do, loc, fwd, adj):
    S = q.shape[0]
    ts = _tile(S, GDN_TS)
    ncb = ts // CHUNK

    def body(q_ref, k_ref, v_ref, bg_ref, gcr_ref, do_ref, *rest):
        per_dir = (rest[0:5], rest[5:10])
        dq_ref, dk_ref, dv_ref, dbg_ref, dbgr_ref = rest[10:15]
        ri, ci = _tri_masks()
        lane = lax.broadcasted_iota(jnp.int32, (CHUNK, 128), 1)
        rowi = lax.broadcasted_iota(jnp.int32, (CHUNK, 1), 0)
        ones8 = jnp.ones((SUBLANES, CHUNK), F32)

        def chunk(c, carry):
            r0 = pl.multiple_of(c * CHUNK, CHUNK)
            rows = pl.ds(r0, CHUNK)
            chains = []
            for h in range(GDN_H):
                cols = slice(h * GDN_DK, (h + 1) * GDN_DK)
                qh, kh, vh = q_ref[rows, cols], k_ref[rows, cols], v_ref[rows, cols]
                dob = do_ref[rows, cols].astype(BF16)
                both = _bdot(jnp.concatenate([qh, kh], axis=0), kh, 1, 1)
                for d in range(2):
                    chains.append(dict(h=h, d=d, cols=cols, qh=qh, kh=kh, vh=vh, dob=dob, qk=both[0:CHUNK],
                                       kk=both[CHUNK:2 * CHUNK], col=d * GDN_H + h))
            for ch in chains:
                m = _gdn_decay(bg_ref, gcr_ref, c, rows, r0, ch["col"], ch["d"] == 1, ri, ci)
                t_ref, s_ref, ds_ref, vn_ref, dvn_ref = per_dir[ch["d"]]
                h, cols = ch["h"], ch["cols"]
                ch["m"] = m
                ch["kb"] = ch["kh"] * m["beta"]
                ch["kbg"] = ch["kb"] * m["eg"]
                ch["t"] = t_ref[c, h]
                st = s_ref[c, h]
                stb = st.astype(BF16)
                ch["dsn"] = ds_ref[c, h]
                vnb = vn_ref[rows, cols].astype(BF16)
                dvnb = dvn_ref[rows, cols].astype(BF16)
                ch["dcd"] = jnp.sum(jnp.sum(st * ch["dsn"], axis=1, keepdims=True), axis=0, keepdims=True)
                ch["dqd"] = _dot(ch["dob"], stb, 1, 1)
                ch["d_a"] = _dot(ch["dob"], vnb, 1, 1)
                ch["dkd"] = _bdot(vnb, ch["dsn"], 1, 1)
                ch["dw"] = -_dot(dvnb, stb, 1, 1)
                ch["dvb"] = _dot(ch["t"], dvnb, 1, 0)
                ch["d_t"] = _bdot(dvnb, ch["vh"] * m["beta"], 1, 1)
            for ch in chains:
                dwb = ch["dw"].astype(BF16)
                ch["d_t"] = ch["d_t"] + _bdot(dwb, ch["kbg"], 1, 1)
                ch["dkbg"] = _dot(ch["t"], dwb, 1, 0)
                ch["nn"] = ch["d_a"] * ch["m"]["dm"]
                ch["nn_q"] = _bdot(ch["nn"], ch["qh"], 0, 0)
                ch["nn_k"] = _bdot(ch["nn"], ch["kh"], 1, 0)
            for ch in chains:
                ch["x"] = _dot(ch["d_t"].astype(BF16), ch["t"], 1, 0)
            for ch in chains:
                d_l = -_dot(ch["t"], ch["x"].astype(BF16), 1, 0)
                ch["d_l"] = jnp.where(ch["m"]["strict"], d_l, 0.0)
                ch["mm"] = ch["d_l"] * ch["m"]["dm"]
            for ch in chains:
                m = ch["m"]
                ch["mm_kh"] = _bdot(ch["mm"], ch["kh"], 1, 0)
                ch["mm_kb"] = _bdot(ch["mm"], ch["kb"], 0, 0)
                l_mat = jnp.where(m["strict"], m["beta"] * ch["kk"] * m["dm"], 0.0)
                ch["e"] = ch["d_l"] * l_mat + ch["nn"] * ch["qk"]
                dbgr_ref[c, ch["col"]:ch["col"] + 1, :] = -_dot(ones8, ch["e"], 1, 0, HI)[0:1, :]
            acc_bg = jnp.zeros((CHUNK, 128), F32)
            acc = {}
            for ch in chains:
                m = ch["m"]
                beta, eg, egl = m["beta"], m["eg"], m["egl"]
                dkb = ch["mm_kh"] + ch["dkbg"] * eg
                dk_d = ch["mm_kb"] + ch["nn_q"] + ch["dkd"] * egl + dkb * beta
                dq_d = ch["nn_k"] + ch["dqd"] * eg
                dv_d = ch["dvb"] * beta
                dkd_kd = ch["dkd"] * (ch["kh"] * egl)
                dgc = (jnp.sum(ch["e"], axis=1, keepdims=True)
                       + jnp.sum(ch["dqd"] * (ch["qh"] * eg) - dkd_kd + ch["dkbg"] * ch["kbg"], axis=1, keepdims=True))
                dgl = jnp.sum(jnp.sum(dkd_kd, axis=1, keepdims=True), axis=0, keepdims=True) + ch["dcd"] * m["cd"]
                dgc = dgc + jnp.where(rowi == (0 if ch["d"] == 1 else CHUNK - 1), dgl, 0.0)
                dbeta = jnp.sum(dkb * ch["kh"] + ch["dvb"] * ch["vh"], axis=1, keepdims=True)
                acc_bg = acc_bg + jnp.where(lane == ch["col"], dbeta, 0.0) + jnp.where(lane == 8 + ch["col"], dgc, 0.0)
                if ch["d"] == 0:
                    acc[ch["h"]] = (dq_d, dk_d, dv_d)
                else:
                    dq0, dk0, dv0 = acc[ch["h"]]
                    dq_ref[rows, ch["cols"]] = dq0 + dq_d
                    dk_ref[rows, ch["cols"]] = dk0 + dk_d
                    dv_ref[rows, ch["cols"]] = dv0 + dv_d
            dbg_ref[rows, :] = acc_bg
            return carry

        lax.fori_loop(0, ncb, chunk, 0)

    im = lambda i: (i, 0)
    im4 = lambda i: (i, 0, 0, 0)
    blk = (ts, GDN_W)
    ins = [(q, blk, im), (k, blk, im), (v, blk, im), (bg, (ts, 128), im), (gcr, (ncb, 8, CHUNK), lambda i: (i, 0, 0)),
           (do, blk, im)]
    for d in range(2):
        ins += [(loc[d][3], (ncb, GDN_H, CHUNK, CHUNK), im4), (fwd[d][2], (ncb, GDN_H, GDN_DK, GDN_DK), im4),
                (adj[d][1], (ncb, GDN_H, GDN_DK, GDN_DK), im4), (fwd[d][1], blk, im), (adj[d][0], blk, im)]
    sds = jax.ShapeDtypeStruct((S, GDN_W), F32)
    outs = [(sds, blk, im), (sds, blk, im), (sds, blk, im), (jax.ShapeDtypeStruct((S, 128), F32), (ts, 128), im),
            (jax.ShapeDtypeStruct((S // CHUNK, 8, CHUNK), F32), (ncb, 8, CHUNK), lambda i: (i, 0, 0))]
    dq, dk, dv, dbg, dbg_rows = _rows("gdn_local_bwd", S, ts, ins, outs, body)
    dgc_cols = dbg_rows.transpose(0, 2, 1).reshape(S, 8)
    return dq, dk, dv, dbg + jnp.pad(dgc_cols, ((0, 0), (8, 112)))


def _gdn_prep_bwd(dbg_all, p, prm):
    S = p.shape[0]
    ts = _tile(S, 512)

    def body(dbg_ref, p_ref, prm_ref, dba_ref, dprm_ref):
        i = pl.program_id(0)
        raw = p_ref[...]
        dbg = dbg_ref[...]
        lane = lax.broadcasted_iota(jnp.int32, (1, 128), 1)
        is_g = (lane >= 8) & (lane < 16)
        ea = jnp.exp(prm_ref[0:1, :])
        arg = raw + prm_ref[1:2, :]
        g = jnp.where(is_g, -ea * _softplus(arg), 0.0)
        beta = _sigmoid(raw)
        dgc = jnp.where(is_g, dbg, 0.0)
        ri, ci = _tri_masks()
        lower = (ri >= ci).astype(F32)
        upper = (ri <= ci).astype(F32)
        dgs = []
        for c in range(ts // CHUNK):
            ch = dgc[c * CHUNK:(c + 1) * CHUNK]
            dgs.append(jnp.where(lane < 12, _dot(upper, ch, 1, 0, HI), _dot(lower, ch, 1, 0, HI)))
        dg = jnp.concatenate(dgs, axis=0)
        dalpha = jnp.where(is_g, dg * (-ea) * _sigmoid(arg), 0.0)
        dba_ref[...] = jnp.where(lane < 8, dbg * beta * (1.0 - beta), dalpha).astype(BF16)
        rows = jnp.concatenate([jnp.sum(dg * g, axis=0, keepdims=True), jnp.sum(dalpha, axis=0, keepdims=True),
                                jnp.zeros((6, 128), F32)], axis=0)
        _colsum_into(dprm_ref, i, rows)

    im = lambda i: (i, 0)
    z0 = lambda i: (0, 0)
    return _rows("gdn_prep_bwd", S, ts,
                 [(dbg_all, (ts, 128), im), (p, (ts, 128), lambda i: (i, COL_BA // 128)), (prm, (8, 128), z0)],
                 [(jax.ShapeDtypeStruct((S, 128), BF16), (ts, 128), im), (jax.ShapeDtypeStruct((8, 128), F32), (8, 128), z0)],
                 body)


def _mm_plain(name, M, N, K, tm, tn, tk, a, am, b, bm, dtype):
    return _fused_mm(name, M, N, K, tm, tn, tk, [(a, am), (b, bm)], [(0, 1, 0)], [],
                     [(jax.ShapeDtypeStruct((M, N), dtype), (tm, tn), _mn)],
                     lambda i, accs, ex, out: out[0].__setitem__(Ellipsis, accs[0][...].astype(dtype)))[0]


def _layer_bwd(x0, W, R, emit_big=None, emit_small=None):
    S = x0.shape[0]
    tm = _tile(S, 512)
    tk_s = _tile(S, 1024)
    G = {}

    def emit(**named):
        if emit_big is None:
            G.update(named)
            return None
        return emit_big(**named)

    def ffn_emit(prefix):
        return lambda **kw: emit(**{f"{prefix}_w_{k}": v for k, v in kw.items()})

    dx2, G["ffn2_norm"] = _ffn_bwd("ffn2b", R["dx3"], R["x2"], W["ffn2_norm"], R["h3"], R["a2"], R["b2"], R["f2"],
                                   W["ffn2_w_gate"], W["ffn2_w_up"], W["ffn2_w_down"], ffn_emit("ffn2"))
    tok = emit(w_out=_mm_plain("dw_out", D_MODEL, D_MODEL, S, D_MODEL, D_MODEL, tk_s, R["y"], "km", dx2, "kn", BF16))
    gn = W["gdn_norm"] if tok is None else W["gdn_norm"] + tok
    dy = _mm_plain("dy_mix", S, D_MODEL, D_MODEL, tm, D_MODEL, D_MODEL, dx2, "mk", W["w_out"], "nk", F32)
    p = R["p"]
    dhr, dgate, do, dz, G["gdn_norm"] = _mix_out_bwd(dy, R["h_f"], R["h_b"], R["o_f"], R["o_b"], p, gn)
    lam_b, lam_f = _rg_scan_adj("rg_scan_bwd", R["a_b"], dhr, R["a_f"], dhr)
    dpre, dxc_direct, d_rgprm = _rg_gates_bwd(R["xc"], R["bd"], R["rg_prm"], lam_f, lam_b, R["h_f"], R["h_b"])
    tmg = _tile(S, 512)
    dxc = _fused_mm("rg_dxc", S, RG_W, 4 * RG_W, tmg, RG_W, 4 * RG_W, [(dpre, "mk"), (R["bd"], "nk")], [(0, 1, 0)],
                    [(dxc_direct, (tmg, RG_W), _mn)], [(jax.ShapeDtypeStruct((S, RG_W), F32), (tmg, RG_W), _mn)],
                    lambda i, accs, ex, out: out[0].__setitem__(Ellipsis, ex[0][...] + accs[0][...]))[0]
    d_bd = _mm_plain("rg_dbd", RG_W, 4 * RG_W, S, RG_W, 4 * RG_W, tk_s, R["xc"], "km", dpre, "kn", F32)
    dx_rg, G["rg_conv_w"], G["rg_conv_b"] = _conv_bwd("rg_conv_bwd", p, 0, W["rg_conv_w"], [dxc], "bias")
    blocks = jnp.einsum("nigmj,nm->gnij", d_bd.reshape(RG_BLOCKS, RG_BLOCK, 4, RG_BLOCKS, RG_BLOCK),
                        jnp.eye(RG_BLOCKS, dtype=F32))
    G["rg_gate_a_w"] = jnp.stack([blocks[0], blocks[2]])
    G["rg_gate_x_w"] = jnp.stack([blocks[1], blocks[3]])
    G["rg_gate_a_b"] = jnp.stack([d_rgprm[0], d_rgprm[2]])
    G["rg_gate_x_b"] = jnp.stack([d_rgprm[1], d_rgprm[3]])
    G["rg_lambda"] = d_rgprm[4:6]
    adj = _gdn_scan_bwd(R["gdn_loc"], do)
    dq, dk, dv, dbg = _gdn_local_bwd(R["q"], R["k"], R["v"], R["bg"], R["gcr"], do, R["gdn_loc"], R["gdn_fwd"], adj)
    cw = W["gdn_conv_w"]
    dpq, dwq, _ = _conv_bwd("gdn_conv_q_bwd", p, 2, cw[:, 0:512], [dq], "q")
    dpk, dwk, _ = _conv_bwd("gdn_conv_k_bwd", p, 3, cw[:, 512:1024], [dk], "k")
    dpv, dwv, _ = _conv_bwd("gdn_conv_v_bwd", p, 4, cw[:, 1024:1536], [dv], "v")
    G["gdn_conv_w"] = jnp.concatenate([dwq, dwk, dwv], axis=1)
    dba, d_gprm = _gdn_prep_bwd(dbg, p, R["gdn_prm"])
    G["gdn_a_log"] = d_gprm[0, 8:16].reshape(2, GDN_H)
    G["gdn_dt_bias"] = d_gprm[1, 8:16].reshape(2, GDN_H)
    dp = jnp.concatenate([dx_rg, dgate, dpq, dpk, dpv, dz, dba], axis=1)
    tok = emit(w_in=_mm_plain("dw_in", D_MODEL, D_IN_PAD, S, D_MODEL, 640, tk_s, R["h2"], "km", dp, "kn", BF16))
    g_mix = W["mix_norm"] if tok is None else W["mix_norm"] + tok

    def epi_dx1(i, accs, ex, out):
        dx, dgt = _rmsnorm_bwd_tile(accs[0][...], ex[0][...], ex[1][...])
        out[0][...] = ex[2][...] + dx
        _colsum_into(out[1], i, jnp.sum(dgt, axis=0, keepdims=True))

    dx1, G["mix_norm"] = _fused_mm(
        "mix_dx", S, D_MODEL, D_IN_PAD, tm, D_MODEL, D_IN_PAD, [(dp, "mk"), (W["w_in"], "nk")], [(0, 1, 0)],
        [(R["x1"], (tm, D_MODEL), _mn), (g_mix, (1, D_MODEL), _row0), (dx2, (tm, D_MODEL), _mn)],
        [(jax.ShapeDtypeStruct((S, D_MODEL), F32), (tm, D_MODEL), _mn),
         (jax.ShapeDtypeStruct((1, D_MODEL), F32), (1, D_MODEL), _row0)], epi_dx1)
    G["final_norm"] = R["d_final_norm"]
    if emit_small is not None:
        emit_small(G)
    dx0, G["ffn1_norm"] = _ffn_bwd("ffn1b", dx1, x0, W["ffn1_norm"], R["h1"], R["a1"], R["b1"], R["f1"],
                                   W["ffn1_w_gate"], W["ffn1_w_up"], W["ffn1_w_down"], ffn_emit("ffn1"))
    return dx0, G


def _mesh_pos():
    x, y, c = lax.axis_index("x"), lax.axis_index("y"), lax.axis_index("c")
    return x, y, c, 4 * x + 2 * y + c


def _peer(x, y, c, r):
    px = 1 - x if r & 4 else x
    py = 1 - y if r & 2 else y
    pc = 1 - c if r & 1 else c
    return (px, py, pc), 4 * px + 2 * py + pc


_HBM = pl.BlockSpec(memory_space=pltpu.HBM)
_SEM = pl.BlockSpec(memory_space=pltpu.SEMAPHORE)


def _peer_copies(scatter, srcs, lands, send_sems, recv_sems):
    x, y, c, me = _mesh_pos()
    copies = []
    for a, (src, land) in enumerate(zip(srcs, lands)):
        for r in range(1, N_DEV):
            peer, peer_idx = _peer(x, y, c, r)
            copies.append(pltpu.make_async_remote_copy(
                src_ref=src.at[peer_idx] if scatter else src, dst_ref=land.at[r - 1] if scatter else land.at[me],
                send_sem=send_sems.at[a * 7 + r - 1], recv_sem=recv_sems.at[a * 7 + r - 1],
                device_id=peer, device_id_type=pl.DeviceIdType.MESH))
    return copies


def _exchange_start(name, scatter, arrays):
    slabs = arrays
    n = len(slabs)

    def body(*refs):
        srcs, lands = refs[0:n], refs[n:2 * n]
        send_sems, recv_sems = refs[2 * n], refs[2 * n + 1]
        token = refs[4 * n + 2]
        for cp in _peer_copies(scatter, srcs, lands, send_sems, recv_sems):
            cp.start()
        token[...] = jnp.zeros_like(token)

    land_shapes = [(N_DEV - 1,) + s.shape[1:] if scatter else (N_DEV,) + s.shape for s in slabs]
    out_shape = ([pltpu.SemaphoreType.DMA((7 * n,)), pltpu.SemaphoreType.DMA((7 * n,))]
                 + [pltpu.HBM(s.shape, s.dtype) for s in slabs]
                 + [pltpu.HBM(shp, s.dtype) for shp, s in zip(land_shapes, slabs)]
                 + [jax.ShapeDtypeStruct((8, 128), F32)])
    res = pl.pallas_call(
        body, name=name, out_shape=out_shape, in_specs=[_HBM] * (2 * n),
        out_specs=[_SEM, _SEM] + [_HBM] * (2 * n) + [pl.BlockSpec(memory_space=pltpu.VMEM)],
        input_output_aliases={i: 2 + i for i in range(2 * n)},
        compiler_params=pltpu.CompilerParams(has_side_effects=pltpu.SideEffectType.DATAFLOW_SIDE_EFFECTING),
    )(*[pltpu.with_memory_space_constraint(s, pltpu.HBM) for s in slabs],
      *[pltpu.with_memory_space_constraint(lax.empty(shp, s.dtype), pltpu.HBM) for shp, s in zip(land_shapes, slabs)])
    return dict(n=n, scatter=scatter, sems=res[0:2], srcs=res[2:2 + n], lands=res[2 + n:2 + 2 * n],
                token=res[2 + 2 * n][0, 0])


def _exchange_wait(name, started, after):
    n = started["n"]
    scatter = started["scatter"]

    def body(*refs):
        srcs, lands = refs[0:n], refs[n:2 * n]
        send_sems, recv_sems = refs[2 * n], refs[2 * n + 1]
        for cp in _peer_copies(scatter, srcs, lands, send_sems, recv_sems):
            cp.wait_send()
            cp.wait_recv()

    arrays = list(started["srcs"]) + list(started["lands"])
    res = pl.pallas_call(
        body, name=name, out_shape=[pltpu.HBM(a.shape, a.dtype) for a in arrays],
        in_specs=[_HBM] * (2 * n) + [_SEM, _SEM, pl.BlockSpec(memory_space=pl.ANY)], out_specs=[_HBM] * (2 * n),
        input_output_aliases={i: i for i in range(2 * n)},
        compiler_params=pltpu.CompilerParams(has_side_effects=pltpu.SideEffectType.DATAFLOW_SIDE_EFFECTING),
    )(*arrays, *started["sems"], after)
    return res[0:n], res[n:2 * n]


def _all_gather(name, arrays):
    n = len(arrays)

    def body(*refs):
        ins = refs[:n]
        outs = refs[n:2 * n]
        token = refs[2 * n]
        send_sems, recv_sems, local_sems = refs[2 * n + 1:]
        token[...] = jnp.zeros_like(token)
        x, y, c, me = _mesh_pos()
        sibling = (x, y, 1 - c)
        chips = [(1 - x, y), (x, 1 - y), (1 - x, 1 - y)]

        def idx(px, py, pc):
            return 4 * px + 2 * py + pc

        def copy(a, k, block, to, src=None):
            slot = outs[a].at[idx(*block)]
            return pltpu.make_async_remote_copy(
                src_ref=slot if src is None else src, dst_ref=slot, send_sem=send_sems.at[a * 7 + k],
                recv_sem=recv_sems.at[a * 7 + k], device_id=to, device_id_type=pl.DeviceIdType.MESH)

        locals_, sends = [], []
        for a in range(n):
            loc = pltpu.make_async_copy(ins[a], outs[a].at[me], local_sems.at[a])
            loc.start()
            locals_.append(loc)
            sends.append(copy(a, 0, (x, y, c), sibling, src=ins[a]))
            sends += [copy(a, 1 + j, (x, y, c), (*chip, c), src=ins[a]) for j, chip in enumerate(chips)]
        for cp in sends:
            cp.start()
        passed = []
        for a in range(n):
            for j, chip in enumerate(chips):
                copy(a, 1 + j, (*chip, c), (x, y, c)).wait_recv()
                fwd = copy(a, 4 + j, (*chip, c), sibling)
                fwd.start()
                passed.append(fwd)
        for a in range(n):
            copy(a, 0, sibling, (x, y, c)).wait_recv()
            for j, chip in enumerate(chips):
                copy(a, 4 + j, (*chip, 1 - c), (x, y, c)).wait_recv()
        for cp in sends + passed:
            cp.wait_send()
        for loc in locals_:
            loc.wait()

    any_spec = pl.BlockSpec(memory_space=pl.ANY)
    res = pl.pallas_call(
        body, name=name, in_specs=[any_spec] * n, out_specs=[any_spec] * n + [pl.BlockSpec(memory_space=pltpu.VMEM)],
        out_shape=[jax.ShapeDtypeStruct((N_DEV,) + a.shape, a.dtype) for a in arrays]
        + [jax.ShapeDtypeStruct((8, 128), F32)],
        scratch_shapes=[pltpu.SemaphoreType.DMA((7 * n,)), pltpu.SemaphoreType.DMA((7 * n,)),
                        pltpu.SemaphoreType.DMA((n,))],
        compiler_params=pltpu.CompilerParams(has_side_effects=True),
    )(*arrays)
    return res[:n], res[n][0, 0]


def _adamw_math(w, g, m, v):
    m2 = ADAM_B1 * m + (1.0 - ADAM_B1) * g
    v2 = ADAM_B2 * v + (1.0 - ADAM_B2) * (g * g)
    m_hat = m2 / (1.0 - ADAM_B1 ** ADAM_STEP)
    v_hat = v2 / (1.0 - ADAM_B2 ** ADAM_STEP)
    delta = -ADAM_LR * (m_hat / (jnp.sqrt(v_hat) + ADAM_EPS) + ADAM_WD * w)
    return delta, m2, v2


def _adamw_slabs(name, src, land, me, w, m, v, tr):
    R, C = w.shape

    def body(me_ref, own_ref, land_ref, w_ref, m_ref, v_ref, g_ref, d_ref, m2_ref, v2_ref):
        g = own_ref[0].astype(F32)
        for s in range(N_DEV - 1):
            g = g + land_ref[s].astype(F32)
        delta, m2, v2 = _adamw_math(w_ref[...], g, m_ref[...], v_ref[...])
        g_ref[...] = g
        d_ref[...] = delta
        m2_ref[...] = m2
        v2_ref[...] = v2

    im = lambda i, me_ref: (i, 0)
    grid_spec = pltpu.PrefetchScalarGridSpec(
        num_scalar_prefetch=1, grid=(R // tr,),
        in_specs=[pl.BlockSpec((1, tr, C), lambda i, me_ref: (me_ref[0], i, 0)),
                  pl.BlockSpec((N_DEV - 1, tr, C), lambda i, me_ref: (0, i, 0)),
                  pl.BlockSpec((tr, C), im), pl.BlockSpec((tr, C), im), pl.BlockSpec((tr, C), im)],
        out_specs=[pl.BlockSpec((tr, C), im)] * 4)
    return pl.pallas_call(body, name=name, grid_spec=grid_spec, out_shape=[jax.ShapeDtypeStruct((R, C), F32)] * 4,
                          compiler_params=_cp(1))(me.reshape(1).astype(jnp.int32), src, land, w, m, v)


def _sum_slots(name, slots):
    _, R, C = slots.shape

    def body(s_ref, o_ref):
        g = s_ref[0]
        for s in range(1, N_DEV):
            g = g + s_ref[s]
        o_ref[...] = g

    return _rows(name, R, R, [(slots, (N_DEV, R, C), lambda i: (0, 0, 0))],
                 [(jax.ShapeDtypeStruct((R, C), F32), (R, C), lambda i: (0, 0))], body)[0]


def _adamw_packed(name, g, w, m, v):
    R, C = g.shape

    def body(g_ref, w_ref, m_ref, v_ref, d_ref, m2_ref, v2_ref):
        delta, m2, v2 = _adamw_math(w_ref[...], g_ref[...], m_ref[...], v_ref[...])
        d_ref[...] = delta
        m2_ref[...] = m2
        v2_ref[...] = v2

    im = lambda i: (0, 0)
    sds = jax.ShapeDtypeStruct((R, C), F32)
    return _rows(name, R, R, [(a, (R, C), im) for a in (g, w, m, v)], [(sds, (R, C), im)] * 3, body)


def _pack(arrays):
    rows = []
    for a in arrays:
        flat = a.reshape(-1).astype(F32)
        pad = (-flat.shape[0]) % 128
        rows.append(jnp.pad(flat, (0, pad)).reshape(-1, 128))
    out = jnp.concatenate(rows, axis=0)
    return jnp.pad(out, ((0, (-out.shape[0]) % 8), (0, 0)))


def _unpack(packed, shapes):
    lead = packed.shape[:-2]
    outs = []
    r = 0
    for shp in shapes:
        n = math.prod(shp)
        nr = -(-n // 128)
        flat = packed[..., r:r + nr, :].reshape(lead + (nr * 128,))[..., :n]
        outs.append(flat.reshape(lead + tuple(shp)))
        r += nr
    return outs


FFN1_BIG = ["ffn1_w_gate", "ffn1_w_up", "ffn1_w_down"]
MIX_BIG = ["w_in", "w_out"]
FFN2_BIG = ["ffn2_w_gate", "ffn2_w_up", "ffn2_w_down"]
BIG = FFN1_BIG + MIX_BIG + FFN2_BIG
COL_SHARDED = {"ffn1_w_gate", "ffn1_w_up", "w_in", "ffn2_w_gate", "ffn2_w_up"}
SMALL_SHARDED = ["rg_conv_w", "rg_gate_a_b", "rg_gate_x_b", "rg_lambda", "gdn_conv_w"]
WEIGHTS = ["ffn1_norm", "ffn1_w_gate", "ffn1_w_up", "ffn1_w_down", "mix_norm", "w_in", "w_out", "rg_conv_w", "rg_conv_b",
           "rg_gate_a_w", "rg_gate_a_b", "rg_gate_x_w", "rg_gate_x_b", "rg_lambda", "gdn_conv_w", "gdn_a_log",
           "gdn_dt_bias", "gdn_norm", "ffn2_norm", "ffn2_w_gate", "ffn2_w_up", "ffn2_w_down", "final_norm"]
SMALL = [n for n in WEIGHTS if n not in BIG]
ROW_VECTORS = {"ffn1_norm", "mix_norm", "ffn2_norm", "gdn_norm", "rg_conv_b", "final_norm"}
ROW_TILE = {"ffn1_w_gate": 256, "ffn1_w_up": 256, "ffn1_w_down": 176, "w_in": 256, "w_out": 64,
            "ffn2_w_gate": 256, "ffn2_w_up": 256, "ffn2_w_down": 176}


def _unshard_cols(g):
    return g.transpose(1, 0, 2).reshape(g.shape[1], N_DEV * g.shape[2])


def _to_slabs(name, g):
    if name in COL_SHARDED:
        r, ctot = g.shape
        return g.reshape(r, N_DEV, ctot // N_DEV).transpose(1, 0, 2)
    return g.reshape(N_DEV, g.shape[0] // N_DEV, g.shape[1])


def _step(x, target, w, m, v):
    _, _, _, me = _mesh_pos()
    def unshard(n, gth):
        full = _unshard_cols(gth) if n in COL_SHARDED else gth.reshape(-1, gth.shape[-1])
        return jnp.pad(full, ((0, 0), (0, D_IN_PAD - D_IN))) if n == "w_in" else full

    def landed(started, name, after):
        srcs, lands = _exchange_wait(name, started, after)
        def with_own(src, land):
            slot = lax.broadcasted_iota(jnp.int32, (N_DEV,) + (1,) * src.ndim, 0)
            return jnp.where(slot == me, src[None], land)

        return [with_own(src, land) for src, land in zip(srcs, lands)]

    first, tok = _all_gather("gather_ffn1", [w[n].astype(BF16) for n in FFN1_BIG])
    W = {n: unshard(n, gth) for n, gth in zip(FFN1_BIG, first)}
    small_shards = [w[n] for n in SMALL_SHARDED]
    st_mix = _exchange_start("gather_mix_start", False,
                             [(w[n] + tok).astype(BF16) for n in MIX_BIG] + [_pack(small_shards) + tok])
    st_ffn2 = _exchange_start("gather_ffn2_start", False, [(w[n] + tok).astype(BF16) for n in FFN2_BIG])
    for n in SMALL:
        if n not in SMALL_SHARDED:
            W[n] = w[n]
    W["ffn1_norm"] = w["ffn1_norm"] + (st_mix["token"] + st_ffn2["token"])

    def more(stage, after):
        if stage == "ffn2":
            return {n: unshard(n, gth) for n, gth in zip(FFN2_BIG, landed(st_ffn2, "gather_ffn2_wait", after))}
        got = landed(st_mix, "gather_mix_wait", after)
        new = {n: unshard(n, gth) for n, gth in zip(MIX_BIG, got)}
        for n, gth in zip(SMALL_SHARDED, _unpack(got[-1], [s.shape for s in small_shards])):
            new[n] = jnp.moveaxis(gth, 0, -2).reshape(gth.shape[1:-1] + (N_DEV * gth.shape[-1],))
        return new

    R = _layer_fwd(x, target, W, more)
    W = R["W"]
    pending = []

    def emit_big(**named):
        slabs = [_to_slabs(n, g[:, :D_IN] if n == "w_in" else g) for n, g in named.items()]
        started = _exchange_start(f"scatter_start_{len(pending)}", True, slabs)
        pending.append((list(named), started))
        return started["token"]

    small_started = []

    def emit_small(G):
        packed = _pack([G[n] for n in SMALL if n != "ffn1_norm"])
        small_started.append(_exchange_start("gather_small_start", False, [packed]))

    grad_x, G = _layer_bwd(x, W, R, emit_big, emit_small)
    loss = lax.psum(R["loss"][0, 0], ("x", "y", "c"))
    out = {}
    for i, (names, started) in enumerate(pending):
        srcs, lands = _exchange_wait(f"scatter_wait_{i}", started, grad_x)
        for n, src, land in zip(names, srcs, lands):
            out[n] = _adamw_slabs(f"adamw_{n}", src, land, me, w[n], m[n], v[n], ROW_TILE[n])
    early = [n for n in SMALL if n != "ffn1_norm"]
    srcs, lands = _exchange_wait("gather_small_wait", small_started[0], grad_x)
    slot = lax.broadcasted_iota(jnp.int32, (N_DEV, 1, 1), 0)
    slots = jnp.where(slot == me, srcs[0][None], lands[0])
    reduced = dict(zip(early, _unpack(_sum_slots("sum_small_grads", slots), [G[n].shape for n in early])))
    late = _all_gather("gather_ffn1_norm_grad", [_pack([G["ffn1_norm"]])])[0][0]
    reduced["ffn1_norm"] = _unpack(_sum_slots("sum_ffn1_norm_grad", late), [G["ffn1_norm"].shape])[0]
    g_small = []
    for n in SMALL:
        g = reduced[n]
        if n in SMALL_SHARDED:
            per = g.shape[-1] // N_DEV
            g = lax.dynamic_slice_in_dim(g, me * per, per, axis=g.ndim - 1)
        g_small.append(g.reshape(w[n].shape))
    shapes = [w[n].shape for n in SMALL]
    d_p, m_p, v_p = _adamw_packed("adamw_small", _pack(g_small), _pack([w[n] for n in SMALL]),
                                  _pack([m[n] for n in SMALL]), _pack([v[n] for n in SMALL]))
    for n, g, d_, m_, v_ in zip(SMALL, g_small, _unpack(d_p, shapes), _unpack(m_p, shapes), _unpack(v_p, shapes)):
        out[n] = (g, d_, m_, v_)
    return loss, grad_x, out


def kernel(x, ffn1_norm, ffn1_w_gate, ffn1_w_up, ffn1_w_down, mix_norm, w_in, w_out, rg_conv_w, rg_conv_b, rg_gate_a_w, rg_gate_a_b, rg_gate_x_w, rg_gate_x_b, rg_lambda, gdn_conv_w, gdn_a_log, gdn_dt_bias, gdn_norm, ffn2_norm, ffn2_w_gate, ffn2_w_up, ffn2_w_down, final_norm, loss_target, m_ffn1_norm, m_ffn1_w_gate, m_ffn1_w_up, m_ffn1_w_down, m_mix_norm, m_w_in, m_w_out, m_rg_conv_w, m_rg_conv_b, m_rg_gate_a_w, m_rg_gate_a_b, m_rg_gate_x_w, m_rg_gate_x_b, m_rg_lambda, m_gdn_conv_w, m_gdn_a_log, m_gdn_dt_bias, m_gdn_norm, m_ffn2_norm, m_ffn2_w_gate, m_ffn2_w_up, m_ffn2_w_down, m_final_norm, v_ffn1_norm, v_ffn1_w_gate, v_ffn1_w_up, v_ffn1_w_down, v_mix_norm, v_w_in, v_w_out, v_rg_conv_w, v_rg_conv_b, v_rg_gate_a_w, v_rg_gate_a_b, v_rg_gate_x_w, v_rg_gate_x_b, v_rg_lambda, v_gdn_conv_w, v_gdn_a_log, v_gdn_dt_bias, v_gdn_norm, v_ffn2_norm, v_ffn2_w_gate, v_ffn2_w_up, v_ffn2_w_down, v_final_norm):
    args = dict(locals())
    orig_shapes = {n: args[n].shape for n in WEIGHTS}

    def local(prefix):
        d = {}
        for n in WEIGHTS:
            a = args[prefix + n]
            d[n] = a.reshape(1, -1) if n in ROW_VECTORS else a[0]
        return d

    loss, grad_x, out = _step(x[0], loss_target[0], local(""), local("m_"), local("v_"))
    res = [loss, grad_x[None]]
    for k in range(4):
        res += [out[n][k].reshape(orig_shapes[n]) for n in WEIGHTS]
    return tuple(res)
```

```python
import functools
import math

import jax
import jax.numpy as jnp
from jax import lax
from jax.experimental import pallas as pl
from jax.experimental.pallas import tpu as pltpu

F32, BF16 = jnp.float32, jnp.bfloat16

D_MODEL = 1024
D_FF = 2816
RG_W = 512
RG_BLOCKS = 8
RG_BLOCK = 64
RG_C = 8.0
CONV_W = 4
GDN_H = 4
GDN_DK = 128
CHUNK = 64
EPS = 1e-6
D_IN = 3088
D_IN_PAD = 3200
COL_BA = 3072
N_DEV = 8
HALO = 8
VMEM_LIMIT = 48 * 1024 * 1024

ADAM_LR = 0.001
ADAM_B1 = 0.9
ADAM_B2 = 0.999
ADAM_EPS = 1e-08
ADAM_WD = 0.01
ADAM_STEP = 10

HI = lax.Precision.HIGHEST


def _cp(n):
    return pltpu.CompilerParams(dimension_semantics=("arbitrary",) * n, vmem_limit_bytes=VMEM_LIMIT)


def _tile(n, pref):
    return min(n, pref)


def _sigmoid(x):
    return 0.5 * jnp.tanh(0.5 * x) + 0.5


def _softplus(x):
    return jnp.maximum(x, 0.0) + jnp.log(1.0 + jnp.exp(-jnp.abs(x)))


def _dot(a, b, ca, cb, prec=None):
    return lax.dot_general(a, b, (((ca,), (cb,)), ((), ())), preferred_element_type=F32, precision=prec)


def _fused_mm(name, M, N, K, tm, tn, tk, ops, pairs, extras, outs, epilogue):
    nm, nn, nk = M // tm, N // tn, K // tk
    assert nm * tm == M and nn * tn == N and nk * tk == K, (name, M, N, K, tm, tn, tk)
    spec_of = {
        "mk": pl.BlockSpec((tm, tk), lambda i, j, k: (i, k)),
        "km": pl.BlockSpec((tk, tm), lambda i, j, k: (k, i)),
        "kn": pl.BlockSpec((tk, tn), lambda i, j, k: (k, j)),
        "nk": pl.BlockSpec((tn, tk), lambda i, j, k: (j, k)),
    }
    in_specs = [spec_of[m] for _, m in ops]
    in_specs += [pl.BlockSpec(bs, lambda i, j, k, im=im: im(i, j)) for _, bs, im in extras]
    out_specs = [pl.BlockSpec(bs, lambda i, j, k, im=im: im(i, j)) for _, bs, im in outs]
    n_ops, n_ex, n_out = len(ops), len(extras), len(outs)
    n_acc = 1 + max(g for _, _, g in pairs)
    modes = [m for _, m in ops]

    def body(*refs):
        op_refs = refs[:n_ops]
        ex_refs = refs[n_ops:n_ops + n_ex]
        out_refs = refs[n_ops + n_ex:n_ops + n_ex + n_out]
        accs = refs[n_ops + n_ex + n_out:]
        i = pl.program_id(0)
        k = pl.program_id(2)
        def dots():
            vals = [r[...].astype(BF16) for r in op_refs]
            for ia, ib, g in pairs:
                yield g, _dot(vals[ia], vals[ib], 1 if modes[ia] == "mk" else 0, 0 if modes[ib] == "kn" else 1)

        if nk == 1:
            sums = [None] * n_acc
            for g, d in dots():
                sums[g] = d if sums[g] is None else sums[g] + d
            epilogue(i, [_Held(s) for s in sums], ex_refs, out_refs)
            return

        @pl.when(k == 0)
        def _():
            for a in accs:
                a[...] = jnp.zeros_like(a)

        for g, d in dots():
            accs[g][...] += d

        @pl.when(k == nk - 1)
        def _():
            epilogue(i, accs, ex_refs, out_refs)

    res = pl.pallas_call(
        body, name=name, grid=(nm, nn, nk), in_specs=in_specs, out_specs=out_specs,
        out_shape=[o for o, _, _ in outs],
        scratch_shapes=[pltpu.VMEM((tm, tn), F32)] * (n_acc if nk > 1 else 0),
        compiler_params=_cp(3),
    )(*[a for a, _ in ops], *[a for a, _, _ in extras])
    return res


class _Held:
    def __init__(self, value):
        self.value = value

    def __getitem__(self, idx):
        return self.value[idx]


def _mn(i, j):
    return (i, j)


def _row0(i, j):
    return (0, 0)


def _rows(name, S, ts, ins, outs, body, scratch=()):
    return pl.pallas_call(
        body, name=name, grid=(S // ts,),
        in_specs=[pl.BlockSpec(bs, im) for _, bs, im in ins],
        out_specs=[pl.BlockSpec(bs, im) for _, bs, im in outs],
        out_shape=[o for o, _, _ in outs],
        scratch_shapes=list(scratch),
        compiler_params=_cp(1),
    )(*[a for a, _, _ in ins])


def _halo_ins(arr, S, ts, width, colblk):
    per = ts // HALO
    last = S // HALO - 1
    return [
        (arr, (ts, width), lambda i: (i, colblk)),
        (arr, (HALO, width), lambda i: (jnp.maximum(i * per - 1, 0), colblk)),
        (arr, (HALO, width), lambda i: (jnp.minimum((i + 1) * per, last), colblk)),
    ]


def _ext(main_ref, prev_ref, next_ref, i, n_tiles):
    prev = jnp.where(i > 0, prev_ref[...].astype(F32), 0.0)
    nxt = jnp.where(i < n_tiles - 1, next_ref[...].astype(F32), 0.0)
    return jnp.concatenate([prev, main_ref[...].astype(F32), nxt], axis=0)


def _shift(ext, off, ts):
    n = ext.shape[0]
    if off == 0:
        return ext[HALO:HALO + ts]
    return pltpu.roll(ext, (-off) % n, 0)[HALO:HALO + ts]


def _rmsnorm_fwd(name, x, g):
    S, D = x.shape
    ts = _tile(S, 512)

    def body(x_ref, g_ref, o_ref):
        xv = x_ref[...]
        r = lax.rsqrt(jnp.mean(xv * xv, axis=-1, keepdims=True) + EPS)
        o_ref[...] = (xv * r * g_ref[...]).astype(BF16)

    return _rows(name, S, ts,
                 [(x, (ts, D), lambda i: (i, 0)), (g, (1, D), lambda i: (0, 0))],
                 [(jax.ShapeDtypeStruct((S, D), BF16), (ts, D), lambda i: (i, 0))], body)[0]


def _rmsnorm_bwd_tile(dh, x, g):
    r = lax.rsqrt(jnp.mean(x * x, axis=-1, keepdims=True) + EPS)
    xhat = x * r
    dxn = dh * g
    dx = r * (dxn - xhat * jnp.mean(dxn * xhat, axis=-1, keepdims=True))
    return dx, dh * xhat


def _ffn_fwd(tag, x, h, wg, wu, wd):
    S = x.shape[0]
    tm = _tile(S, 512)
    tn = 1408

    def epi_up(i, accs, ex, out):
        a = accs[0][...]
        b = accs[1][...]
        s = _sigmoid(a)
        sa = a * s
        out[0][...] = sa.astype(BF16)
        out[1][...] = (b * (s * (1.0 + a * (1.0 - s)))).astype(BF16)
        out[2][...] = (sa * b).astype(BF16)

    sds = jax.ShapeDtypeStruct((S, D_FF), BF16)
    a, b, f = _fused_mm(f"{tag}_up", S, D_FF, D_MODEL, tm, tn, D_MODEL,
                        [(h, "mk"), (wg, "kn"), (wu, "kn")], [(0, 1, 0), (0, 2, 1)], [],
                        [(sds, (tm, tn), _mn)] * 3, epi_up)

    def epi_down(i, accs, ex, out):
        out[0][...] = ex[0][...] + 0.5 * accs[0][...]

    if callable(wd):
        wd = wd(f)
    xo = _fused_mm(f"{tag}_down", S, D_MODEL, D_FF, tm, D_MODEL, 1408,
                   [(f, "mk"), (wd, "kn")], [(0, 1, 0)], [(x, (tm, D_MODEL), _mn)],
                   [(jax.ShapeDtypeStruct((S, D_MODEL), F32), (tm, D_MODEL), _mn)], epi_down)[0]
    return xo, a, b, f


def _conv_taps(ext, w_ref, ts):
    acc = None
    for j in range(CONV_W):
        term = w_ref[j:j + 1, :] * _shift(ext, j - 2, ts)
        acc = term if acc is None else acc + term
    return acc


def _l2norm_heads(s, scale):
    outs = []
    for h in range(GDN_H):
        sh = s[:, h * GDN_DK:(h + 1) * GDN_DK]
        outs.append(sh * (lax.rsqrt(jnp.sum(sh * sh, axis=-1, keepdims=True) + EPS) * scale))
    return jnp.concatenate(outs, axis=-1)


def _conv_fwd(name, p, colblk, w, bias, mode):
    S = p.shape[0]
    ts = _tile(S, 512)
    n_tiles = S // ts
    C = w.shape[1]

    def body(main, prev, nxt, w_ref, b_ref, o_ref):
        i = pl.program_id(0)
        c = _conv_taps(_ext(main, prev, nxt, i, n_tiles), w_ref, ts)
        if mode == "bias":
            o_ref[...] = c + b_ref[...]
        else:
            s = c * _sigmoid(c)
            if mode == "q":
                s = _l2norm_heads(s, GDN_DK ** -0.5)
            elif mode == "k":
                s = _l2norm_heads(s, 1.0)
            o_ref[...] = s

    ins = _halo_ins(p, S, ts, C, colblk) + [(w, (CONV_W, C), lambda i: (0, 0)), (bias, (1, C), lambda i: (0, 0))]
    return _rows(name, S, ts, ins, [(jax.ShapeDtypeStruct((S, C), F32), (ts, C), lambda i: (i, 0))], body)[0]


def _rg_gate_terms(pre, xc, prm_ref, d):
    r = _sigmoid(pre[:, d * 1024:d * 1024 + RG_W] + prm_ref[2 * d:2 * d + 1, :])
    ig = _sigmoid(pre[:, d * 1024 + RG_W:(d + 1) * 1024] + prm_ref[2 * d + 1:2 * d + 2, :])
    sp = _softplus(-prm_ref[4 + d:5 + d, :])
    log_a = -RG_C * r * sp
    a = jnp.exp(log_a)
    t = jnp.tanh(log_a)
    sq = jnp.sqrt(-2.0 * t / (1.0 - t))
    return r, ig, sp, a, sq


def _rg_gates_fwd(xc, bd, prm):
    S = xc.shape[0]
    tm = _tile(S, 256)

    def epi(i, accs, ex, out):
        pre = accs[0][...]
        xv = ex[0][...]
        for d in range(2):
            r, ig, sp, a, sq = _rg_gate_terms(pre, xv, ex[1], d)
            out[2 * d][...] = a
            out[2 * d + 1][...] = sq * ig * xv

    sds = jax.ShapeDtypeStruct((S, RG_W), F32)
    blk = (tm, RG_W)
    im = lambda i, j: (i, 0)
    return _fused_mm("rg_gates_fwd", S, 4 * RG_W, RG_W, tm, 4 * RG_W, RG_W,
                     [(xc, "mk"), (bd, "kn")], [(0, 1, 0)],
                     [(xc, blk, im), (prm, (8, RG_W), _row0)], [(sds, blk, im)] * 4, epi)


SUBLANES = 8


def _scan_rows(a, b, reverse):
    rows = lax.broadcasted_iota(jnp.int32, a.shape, 0)
    s = 1
    while s < SUBLANES:
        shift = SUBLANES - s if reverse else s
        a_sh = pltpu.roll(a, shift, 0)
        b_sh = pltpu.roll(b, shift, 0)
        valid = (rows < SUBLANES - s) if reverse else (rows >= s)
        b = jnp.where(valid, a * b_sh + b, b)
        a = jnp.where(valid, a * a_sh, a)
        s *= 2
    return a, b


def _rg_scan(name, a_f, b_f, a_b, b_b):
    S, C = a_f.shape
    ts = _tile(S, 512)
    n_tiles = S // ts

    def body(af, bf, ab, bb, hf, hb, carry):
        @pl.when(pl.program_id(0) == 0)
        def _():
            carry[...] = jnp.zeros_like(carry)

        n_sub = ts // SUBLANES

        def step(j, c):
            cf, cb = c
            r0 = pl.multiple_of(j * SUBLANES, SUBLANES)
            cum_a, h0 = _scan_rows(af[pl.ds(r0, SUBLANES), :], bf[pl.ds(r0, SUBLANES), :], False)
            h = h0 + cum_a * cf
            hf[pl.ds(r0, SUBLANES), :] = h
            cf = h[SUBLANES - 1:SUBLANES, :]
            r1 = pl.multiple_of((n_sub - 1 - j) * SUBLANES, SUBLANES)
            cum_a, h0 = _scan_rows(ab[pl.ds(r1, SUBLANES), :], bb[pl.ds(r1, SUBLANES), :], True)
            h = h0 + cum_a * cb
            hb[pl.ds(r1, SUBLANES), :] = h
            cb = h[0:1, :]
            return cf, cb

        cf, cb = lax.fori_loop(0, n_sub, step, (carry[0:1, :], carry[1:2, :]), unroll=4)
        carry[0:1, :] = cf
        carry[1:2, :] = cb

    fw = lambda i: (i, 0)
    bw = lambda i: (n_tiles - 1 - i, 0)
    sds = jax.ShapeDtypeStruct((S, C), F32)
    return _rows(name, S, ts,
                 [(a_f, (ts, C), fw), (b_f, (ts, C), fw), (a_b, (ts, C), bw), (b_b, (ts, C), bw)],
                 [(sds, (ts, C), fw), (sds, (ts, C), bw)], body, scratch=[pltpu.VMEM((8, C), F32)])


def _tri_masks():
    ri = lax.broadcasted_iota(jnp.int32, (CHUNK, CHUNK), 0)
    ci = lax.broadcasted_iota(jnp.int32, (CHUNK, CHUNK), 1)
    return ri, ci


def _gdn_prep_fwd(p, prm):
    S = p.shape[0]
    ts = _tile(S, 512)

    def body(p_ref, prm_ref, o_ref):
        raw = p_ref[...]
        lane = lax.broadcasted_iota(jnp.int32, (1, 128), 1)
        g = -jnp.exp(prm_ref[0:1, :]) * _softplus(raw + prm_ref[1:2, :])
        g = jnp.where((lane >= 8) & (lane < 16), g, 0.0)
        beta = _sigmoid(raw)
        ri, ci = _tri_masks()
        lower = (ri >= ci).astype(F32)
        upper = (ri <= ci).astype(F32)
        for c in range(ts // CHUNK):
            rows = slice(c * CHUNK, (c + 1) * CHUNK)
            gch = g[rows]
            gc = jnp.where(lane < 12, _dot(lower, gch, 1, 0, HI), _dot(upper, gch, 1, 0, HI))
            o_ref[rows, :] = jnp.where(lane < 8, beta[rows], gc)

    return _rows("gdn_prep_fwd", S, ts,
                 [(p, (ts, 128), lambda i: (i, COL_BA // 128)), (prm, (8, 128), lambda i: (0, 0))],
                 [(jax.ShapeDtypeStruct((S, 128), F32), (ts, 128), lambda i: (i, 0))], body)[0]


def _bdot(a, b, ca, cb):
    return _dot(a.astype(BF16), b.astype(BF16), ca, cb)


GDN_W = GDN_H * GDN_DK
GDN_TS = 256


def _gdn_decay(bg_ref, gcr_ref, c, rows, r0, col, rev, ri, ci):
    beta = bg_ref[rows, col:col + 1]
    gc = bg_ref[rows, 8 + col:9 + col]
    last = 0 if rev else CHUNK - 1
    gl = bg_ref[pl.ds(r0 + last, 1), 8 + col:9 + col]
    out = dict(beta=beta, gc=gc, gl=gl, eg=jnp.exp(gc), egl=jnp.exp(gl - gc), cd=jnp.exp(gl))
    if gcr_ref is not None:
        incl = (ri <= ci) if rev else (ri >= ci)
        out["strict"] = (ri < ci) if rev else (ri > ci)
        out["dm"] = jnp.where(incl, jnp.exp(jnp.where(incl, gc - gcr_ref[c, col:col + 1, :], 0.0)), 0.0)
    return out


def _dir_tile(d, n_tiles, flip):
    if (d == 1) != flip:
        return lambda i: n_tiles - 1 - i
    return lambda i: i


def _gdn_local_fwd(q, k, v, bg, gcr):
    S = q.shape[0]
    ts = _tile(S, GDN_TS)
    ncb = ts // CHUNK
    nch = S // CHUNK

    def body(q_ref, k_ref, v_ref, bg_ref, gcr_ref, *out_refs):
        ri, ci = _tri_masks()
        eye = (ri == ci).astype(F32)
        outs = (out_refs[0:6], out_refs[6:12])
        cd_ref = out_refs[12]

        def chunk(c, carry):
            r0 = pl.multiple_of(c * CHUNK, CHUNK)
            rows = pl.ds(r0, CHUNK)
            chains = []
            for h in range(GDN_H):
                cols = slice(h * GDN_DK, (h + 1) * GDN_DK)
                qh, kh, vh = q_ref[rows, cols], k_ref[rows, cols], v_ref[rows, cols]
                both = _bdot(jnp.concatenate([qh, kh], axis=0), kh, 1, 1)
                for d in range(2):
                    chains.append(dict(h=h, d=d, cols=cols, qh=qh, kh=kh, vh=vh, qk=both[0:CHUNK],
                                       kk=both[CHUNK:2 * CHUNK]))
            for ch in chains:
                m = _gdn_decay(bg_ref, gcr_ref, c, rows, r0, ch["d"] * GDN_H + ch["h"], ch["d"] == 1, ri, ci)
                ch["m"] = m
                ch["x"] = -jnp.where(m["strict"], m["beta"] * ch["kk"] * m["dm"], 0.0)
                ch["t"] = eye + ch["x"]
            for ch in chains:
                ch["pw"] = _bdot(ch["x"], ch["x"], 1, 0)
            for level in range(1, 6):
                last_level = level == 5
                for ch in chains:
                    rhs = ch["t"] if last_level else jnp.concatenate([ch["t"], ch["pw"]], axis=1)
                    ch["prod"] = _bdot(ch["pw"], rhs, 1, 0)
                for ch in chains:
                    ch["t"] = ch["t"] + ch["prod"][:, 0:CHUNK]
                    if not last_level:
                        ch["pw"] = ch["prod"][:, CHUNK:2 * CHUNK]
            for ch in chains:
                m = ch["m"]
                rhs = jnp.concatenate([ch["vh"] * m["beta"], ch["kh"] * (m["beta"] * m["eg"])], axis=1)
                ch["uw"] = _bdot(ch["t"], rhs, 1, 0)
            for ch in chains:
                u_ref, w_ref, a_ref, t_ref, qd_ref, kd_ref = outs[ch["d"]]
                m = ch["m"]
                col = ch["d"] * GDN_H + ch["h"]
                u_ref[rows, ch["cols"]] = ch["uw"][:, 0:GDN_DK]
                w_ref[rows, ch["cols"]] = ch["uw"][:, GDN_DK:2 * GDN_DK].astype(BF16)
                a_ref[c, ch["h"]] = (ch["qk"] * m["dm"]).astype(BF16)
                t_ref[c, ch["h"]] = _bdot(ch["t"], eye, 0, 0).astype(BF16)
                qd_ref[rows, ch["cols"]] = (ch["qh"] * m["eg"]).astype(BF16)
                kd_ref[rows, ch["cols"]] = (ch["kh"] * m["egl"]).astype(BF16)
                cd_ref[c, col:col + 1, :] = jnp.broadcast_to(m["cd"], (1, 128))
            return carry

        lax.fori_loop(0, ncb, chunk, 0)

    im = lambda i: (i, 0)
    im4 = lambda i: (i, 0, 0, 0)
    ins = [(q, (ts, GDN_W), im), (k, (ts, GDN_W), im), (v, (ts, GDN_W), im), (bg, (ts, 128), im),
           (gcr, (ncb, 8, CHUNK), lambda i: (i, 0, 0))]
    per_dir = [(jax.ShapeDtypeStruct((S, GDN_W), F32), (ts, GDN_W), im),
               (jax.ShapeDtypeStruct((S, GDN_W), BF16), (ts, GDN_W), im),
               (jax.ShapeDtypeStruct((nch, GDN_H, CHUNK, CHUNK), BF16), (ncb, GDN_H, CHUNK, CHUNK), im4),
               (jax.ShapeDtypeStruct((nch, GDN_H, CHUNK, CHUNK), BF16), (ncb, GDN_H, CHUNK, CHUNK), im4),
               (jax.ShapeDtypeStruct((S, GDN_W), BF16), (ts, GDN_W), im),
               (jax.ShapeDtypeStruct((S, GDN_W), BF16), (ts, GDN_W), im)]
    cd_out = (jax.ShapeDtypeStruct((nch, 8, 128), F32), (ncb, 8, 128), lambda i: (i, 0, 0))
    res = _rows("gdn_local_fwd", S, ts, ins, per_dir * 2 + [cd_out], body)
    return res[0:6], res[6:12], res[12]


def _gdn_scan_fwd(loc):
    S = loc[0][0].shape[0]
    ts = _tile(S, GDN_TS)
    n_tiles = S // ts
    ncb = ts // CHUNK
    nch = S // CHUNK

    def body(*refs):
        ins = (refs[0:6], refs[6:12])
        outs = (refs[12:15], refs[15:18])
        state = refs[18]

        @pl.when(pl.program_id(0) == 0)
        def _():
            state[...] = jnp.zeros_like(state)

        def chunk(cc, carry):
            chains = []
            for d in range(2):
                c = cc if d == 0 else ncb - 1 - cc
                rows = pl.ds(pl.multiple_of(c * CHUNK, CHUNK), CHUNK)
                for h in range(GDN_H):
                    cols = slice(h * GDN_DK, (h + 1) * GDN_DK)
                    chains.append(dict(d=d, h=h, c=c, rows=rows, cols=cols, st=state[d * GDN_H + h]))
            for ch in chains:
                qd_ref, kd_ref, u_ref, w_ref, a_ref, cd_ref = ins[ch["d"]]
                rows, cols = ch["rows"], ch["cols"]
                lhs = jnp.concatenate([w_ref[rows, cols], qd_ref[rows, cols]], axis=0)
                ch["ws_qs"] = _dot(lhs, ch["st"].astype(BF16), 1, 0)
            for ch in chains:
                qd_ref, kd_ref, u_ref, w_ref, a_ref, cd_ref = ins[ch["d"]]
                rows, cols = ch["rows"], ch["cols"]
                vn = u_ref[rows, cols] - ch["ws_qs"][0:CHUNK]
                vnb = vn.astype(BF16)
                ch["vn"] = vn
                ch["avn"] = _dot(a_ref[ch["c"], ch["h"]], vnb, 1, 0)
                ch["kvn"] = _dot(kd_ref[rows, cols], vnb, 0, 0)
            for ch in chains:
                o_ref, vn_ref, s_ref = outs[ch["d"]]
                cd_ref = ins[ch["d"]][5]
                rows, cols = ch["rows"], ch["cols"]
                col = ch["d"] * GDN_H + ch["h"]
                o_ref[rows, cols] = ch["ws_qs"][CHUNK:2 * CHUNK] + ch["avn"]
                vn_ref[rows, cols] = ch["vn"].astype(BF16)
                s_ref[ch["c"], ch["h"]] = ch["st"].astype(BF16)
                state[ch["d"] * GDN_H + ch["h"]] = ch["st"] * cd_ref[ch["c"], col:col + 1, :] + ch["kvn"]
            return carry

        lax.fori_loop(0, ncb, chunk, 0)

    ins, outs = [], []
    for d in range(2):
        tix = _dir_tile(d, n_tiles, False)
        im = lambda i, tix=tix: (tix(i), 0)
        im4 = lambda i, tix=tix: (tix(i), 0, 0, 0)
        u, w, a, _, qd, kd = loc[d]
        ins += [(qd, (ts, GDN_W), im), (kd, (ts, GDN_W), im), (u, (ts, GDN_W), im), (w, (ts, GDN_W), im),
                (a, (ncb, GDN_H, CHUNK, CHUNK), im4), (loc[2], (ncb, 8, 128), lambda i, tix=tix: (tix(i), 0, 0))]
        outs += [(jax.ShapeDtypeStruct((S, GDN_W), F32), (ts, GDN_W), im),
                 (jax.ShapeDtypeStruct((S, GDN_W), BF16), (ts, GDN_W), im),
                 (jax.ShapeDtypeStruct((nch, GDN_H, GDN_DK, GDN_DK), BF16), (ncb, GDN_H, GDN_DK, GDN_DK), im4)]
    res = _rows("gdn_scan_fwd", S, ts, ins, outs, body, scratch=[pltpu.VMEM((2 * GDN_H, GDN_DK, GDN_DK), F32)])
    return res[0:3], res[3:6]


def _gelu(x):
    c = math.sqrt(2.0 / math.pi)
    t = jnp.tanh(c * (x + 0.044715 * x * x * x))
    return 0.5 * x * (1.0 + t), t


def _mix_out_fwd(h_f, h_b, o_f, o_b, p, gn):
    S = h_f.shape[0]
    ts = _tile(S, 512)

    def body(hf, hb, of, ob, gate, z, gn_ref, y_ref):
        ge, _ = _gelu(gate[...])
        y_ref[:, 0:RG_W] = ((hf[...] + hb[...]) * ge).astype(BF16)
        o = of[...] + ob[...]
        zv = z[...]
        sz = zv * _sigmoid(zv)
        for h in range(GDN_H):
            cols = slice(h * GDN_DK, (h + 1) * GDN_DK)
            oh = o[:, cols]
            n = oh * lax.rsqrt(jnp.mean(oh * oh, axis=-1, keepdims=True) + EPS) * gn_ref[...]
            y_ref[:, RG_W + h * GDN_DK:RG_W + (h + 1) * GDN_DK] = (n * sz[:, cols]).astype(BF16)

    blk = (ts, RG_W)
    im = lambda i: (i, 0)
    ins = [(h_f, blk, im), (h_b, blk, im), (o_f, blk, im), (o_b, blk, im),
           (p, blk, lambda i: (i, 1)), (p, blk, lambda i: (i, 5)), (gn, (1, GDN_DK), lambda i: (0, 0))]
    return _rows("mix_out_fwd", S, ts, ins,
                 [(jax.ShapeDtypeStruct((S, D_MODEL), BF16), (ts, D_MODEL), im)], body)[0]


def _loss_head(x, target, g):
    S, D = x.shape
    ts = _tile(S, 512)

    def body(x_ref, t_ref, g_ref, dx_ref, loss_ref, dg_ref):
        @pl.when(pl.program_id(0) == 0)
        def _():
            loss_ref[...] = jnp.zeros_like(loss_ref)
            dg_ref[...] = jnp.zeros_like(dg_ref)

        xv = x_ref[...]
        gv = g_ref[...]
        r = lax.rsqrt(jnp.mean(xv * xv, axis=-1, keepdims=True) + EPS)
        err = xv * r * gv - t_ref[...]
        loss_ref[...] += jnp.sum(err * err) * (0.5 / D)
        dx, dgt = _rmsnorm_bwd_tile(err * (1.0 / D), xv, gv)
        dx_ref[...] = dx
        dg_ref[...] += jnp.sum(dgt, axis=0, keepdims=True)

    im = lambda i: (i, 0)
    z = lambda i: (0, 0)
    return _rows("loss_head", S, ts,
                 [(x, (ts, D), im), (target, (ts, D), im), (g, (1, D), z)],
                 [(jax.ShapeDtypeStruct((S, D), F32), (ts, D), im),
                  (jax.ShapeDtypeStruct((8, 128), F32), (8, 128), z),
                  (jax.ShapeDtypeStruct((1, D), F32), (1, D), z)], body)


def _block_diag(w):
    n = w.shape[0]
    return jnp.einsum("nij,nm->nimj", w, jnp.eye(n, dtype=w.dtype)).reshape(n * w.shape[1], n * w.shape[2])


def _rg_bd(a_w, x_w):
    return jnp.concatenate([_block_diag(a_w[0]), _block_diag(x_w[0]), _block_diag(a_w[1]), _block_diag(x_w[1])],
                           axis=1).astype(BF16)


def _rg_prm(ba, bx, lam):
    return jnp.concatenate([ba[0:1], bx[0:1], ba[1:2], bx[1:2], lam, jnp.zeros((2, RG_W), F32)], axis=0)


def _gdn_prm(a_log, dt_bias):
    rows = jnp.zeros((8, 128), F32)
    rows = rows.at[0, 8:16].set(a_log.reshape(-1))
    return rows.at[1, 8:16].set(dt_bias.reshape(-1))


def _gc_rows(bg):
    S = bg.shape[0]
    return bg[:, 8:16].reshape(S // CHUNK, CHUNK, 8).transpose(0, 2, 1)


def _layer_fwd(x0, target, W, more=None):
    S = x0.shape[0]
    R = {}
    R["h1"] = _rmsnorm_fwd("rms1", x0, W["ffn1_norm"])
    late_wd = {}

    def ffn1_wd(after):
        late_wd.update(more("ffn1_down", after))
        return late_wd["ffn1_w_down"]

    R["x1"], R["a1"], R["b1"], R["f1"] = _ffn_fwd("ffn1", x0, R["h1"], W["ffn1_w_gate"], W["ffn1_w_up"],
                                                  ffn1_wd if more is not None else W["ffn1_w_down"])
    if more is not None:
        W = {**W, **late_wd, **more("mixer", R["x1"])}
    R["h2"] = _rmsnorm_fwd("rms2", R["x1"], W["mix_norm"])
    tm = _tile(S, 512)
    tmp = _tile(S, 1024)
    R["p"] = _fused_mm("in_proj", S, D_IN_PAD, D_MODEL, tmp, 640, D_MODEL, [(R["h2"], "mk"), (W["w_in"], "kn")],
                       [(0, 1, 0)], [], [(jax.ShapeDtypeStruct((S, D_IN_PAD), F32), (tmp, 640), _mn)],
                       lambda i, accs, ex, out: out[0].__setitem__(Ellipsis, accs[0][...]))[0]
    p = R["p"]
    R["xc"] = _conv_fwd("rg_conv_fwd", p, 0, W["rg_conv_w"], W["rg_conv_b"], "bias")
    R["bd"] = _rg_bd(W["rg_gate_a_w"], W["rg_gate_x_w"])
    R["rg_prm"] = _rg_prm(W["rg_gate_a_b"], W["rg_gate_x_b"], W["rg_lambda"])
    a_f, b_f, a_b, b_b = _rg_gates_fwd(R["xc"], R["bd"], R["rg_prm"])
    R["a_f"], R["a_b"] = a_f, a_b
    R["h_f"], R["h_b"] = _rg_scan("rg_scan_fwd", a_f, b_f, a_b, b_b)
    zero_b = jnp.zeros((1, RG_W), F32)
    cw = W["gdn_conv_w"]
    R["q"] = _conv_fwd("gdn_conv_q", p, 2, cw[:, 0:512], zero_b, "q")
    R["k"] = _conv_fwd("gdn_conv_k", p, 3, cw[:, 512:1024], zero_b, "k")
    R["v"] = _conv_fwd("gdn_conv_v", p, 4, cw[:, 1024:1536], zero_b, "v")
    R["gdn_prm"] = _gdn_prm(W["gdn_a_log"], W["gdn_dt_bias"])
    R["bg"] = _gdn_prep_fwd(p, R["gdn_prm"])
    R["gcr"] = _gc_rows(R["bg"])
    R["gdn_loc"] = _gdn_local_fwd(R["q"], R["k"], R["v"], R["bg"], R["gcr"])
    R["gdn_fwd"] = _gdn_scan_fwd(R["gdn_loc"])
    R["o_f"], R["o_b"] = R["gdn_fwd"][0][0], R["gdn_fwd"][1][0]
    R["y"] = _mix_out_fwd(R["h_f"], R["h_b"], R["o_f"], R["o_b"], p, W["gdn_norm"])
    R["x2"] = _fused_mm("out_proj", S, D_MODEL, D_MODEL, tm, D_MODEL, D_MODEL, [(R["y"], "mk"), (W["w_out"], "kn")],
                        [(0, 1, 0)], [(R["x1"], (tm, D_MODEL), _mn)],
                        [(jax.ShapeDtypeStruct((S, D_MODEL), F32), (tm, D_MODEL), _mn)],
                        lambda i, accs, ex, out: out[0].__setitem__(Ellipsis, ex[0][...] + accs[0][...]))[0]
    if more is not None:
        W = {**W, **more("ffn2", R["x2"])}
    R["h3"] = _rmsnorm_fwd("rms3", R["x2"], W["ffn2_norm"])
    R["x3"], R["a2"], R["b2"], R["f2"] = _ffn_fwd("ffn2", R["x2"], R["h3"], W["ffn2_w_gate"], W["ffn2_w_up"], W["ffn2_w_down"])
    R["dx3"], R["loss"], R["d_final_norm"] = _loss_head(R["x3"], target, W["final_norm"])
    R["W"] = W
    return R


def _colsum_into(ref, i, val):
    @pl.when(i == 0)
    def _():
        ref[...] = val

    @pl.when(i > 0)
    def _():
        ref[...] += val


def _ffn_bwd(tag, dout, x, g, h, a, b, f, wg, wu, wd, emit):
    S = x.shape[0]
    tm = _tile(S, 512)
    tk_s = _tile(S, 1024)

    def epi_act(i, accs, ex, out):
        df = 0.5 * accs[0][...]
        out[0][...] = (df * ex[1][...].astype(F32)).astype(BF16)
        out[1][...] = (df * ex[0][...].astype(F32)).astype(BF16)

    sds = jax.ShapeDtypeStruct((S, D_FF), BF16)
    da, db = _fused_mm(f"{tag}_dact", S, D_FF, D_MODEL, tm, 1408, D_MODEL, [(dout, "mk"), (wd, "nk")], [(0, 1, 0)],
                       [(a, (tm, 1408), _mn), (b, (tm, 1408), _mn)], [(sds, (tm, 1408), _mn)] * 2, epi_act)

    def epi_w2(i, accs, ex, out):
        out[0][...] = accs[0][...].astype(BF16)
        out[1][...] = accs[1][...].astype(BF16)

    sdw = jax.ShapeDtypeStruct((D_MODEL, D_FF), BF16)
    dwg, dwu = _fused_mm(f"{tag}_dw_up", D_MODEL, D_FF, S, D_MODEL, 1408, _tile(S, 512),
                         [(h, "km"), (da, "kn"), (db, "kn")], [(0, 1, 0), (0, 2, 1)], [],
                         [(sdw, (D_MODEL, 1408), _mn)] * 2, epi_w2)
    emit(gate=dwg, up=dwu)
    dwd = _fused_mm(f"{tag}_dw_down", D_FF, D_MODEL, S, 1408, D_MODEL, tk_s, [(f, "km"), (dout, "kn")], [(0, 1, 0)], [],
                    [(jax.ShapeDtypeStruct((D_FF, D_MODEL), BF16), (1408, D_MODEL), _mn)],
                    lambda i, accs, ex, out: out[0].__setitem__(Ellipsis, (0.5 * accs[0][...]).astype(BF16)))[0]
    tok = emit(down=dwd)
    if tok is not None:
        g = g + tok

    def epi_dx(i, accs, ex, out):
        dx, dgt = _rmsnorm_bwd_tile(accs[0][...], ex[0][...], ex[1][...])
        out[0][...] = ex[2][...] + dx
        _colsum_into(out[1], i, jnp.sum(dgt, axis=0, keepdims=True))

    dx, dg = _fused_mm(f"{tag}_dx", S, D_MODEL, D_FF, tm, D_MODEL, 1408,
                       [(da, "mk"), (wg, "nk"), (db, "mk"), (wu, "nk")], [(0, 1, 0), (2, 3, 0)],
                       [(x, (tm, D_MODEL), _mn), (g, (1, D_MODEL), _row0), (dout, (tm, D_MODEL), _mn)],
                       [(jax.ShapeDtypeStruct((S, D_MODEL), F32), (tm, D_MODEL), _mn),
                        (jax.ShapeDtypeStruct((1, D_MODEL), F32), (1, D_MODEL), _row0)], epi_dx)
    return dx, dg


def _mix_out_bwd(dy, h_f, h_b, o_f, o_b, p, gn):
    S = dy.shape[0]
    ts = _tile(S, 512)
    c0 = math.sqrt(2.0 / math.pi)

    def body(dy_ref, hf, hb, of, ob, gate, z, gn_ref, dhr_ref, dgate_ref, do_ref, dz_ref, dgn_ref):
        i = pl.program_id(0)
        gv = gate[...]
        ge, t = _gelu(gv)
        dy_rg = dy_ref[:, 0:RG_W]
        dhr_ref[...] = dy_rg * ge
        dgelu = 0.5 * (1.0 + t) + 0.5 * gv * (1.0 - t * t) * c0 * (1.0 + 3.0 * 0.044715 * gv * gv)
        dgate_ref[...] = (dy_rg * (hf[...] + hb[...]) * dgelu).astype(BF16)
        o = of[...] + ob[...]
        zv = z[...]
        sig = _sigmoid(zv)
        gnv = gn_ref[...]
        dgn = jnp.zeros((1, GDN_DK), F32)
        for h in range(GDN_H):
            cols = slice(h * GDN_DK, (h + 1) * GDN_DK)
            oh = o[:, cols]
            r = lax.rsqrt(jnp.mean(oh * oh, axis=-1, keepdims=True) + EPS)
            ohat = oh * r
            dyh = dy_ref[:, RG_W + h * GDN_DK:RG_W + (h + 1) * GDN_DK]
            zh = zv[:, cols]
            sh = sig[:, cols]
            dn = dyh * zh * sh
            dz_ref[:, cols] = (dyh * ohat * gnv * (sh * (1.0 + zh * (1.0 - sh)))).astype(BF16)
            dxn = dn * gnv
            do_ref[:, cols] = r * (dxn - ohat * jnp.mean(dxn * ohat, axis=-1, keepdims=True))
            dgn = dgn + jnp.sum(dn * ohat, axis=0, keepdims=True)
        _colsum_into(dgn_ref, i, dgn)

    blk = (ts, RG_W)
    im = lambda i: (i, 0)
    z0 = lambda i: (0, 0)
    ins = [(dy, (ts, D_MODEL), im), (h_f, blk, im), (h_b, blk, im), (o_f, blk, im), (o_b, blk, im),
           (p, blk, lambda i: (i, 1)), (p, blk, lambda i: (i, 5)), (gn, (1, GDN_DK), z0)]
    outs = [(jax.ShapeDtypeStruct((S, RG_W), F32), blk, im), (jax.ShapeDtypeStruct((S, RG_W), BF16), blk, im),
            (jax.ShapeDtypeStruct((S, RG_W), F32), blk, im), (jax.ShapeDtypeStruct((S, RG_W), BF16), blk, im),
            (jax.ShapeDtypeStruct((1, GDN_DK), F32), (1, GDN_DK), z0)]
    return _rows("mix_out_bwd", S, ts, ins, outs, body)


def _rg_scan_adj(name, a_up, b_up, a_dn, b_dn):
    S, C = a_up.shape
    ts = _tile(S, 512)
    n_tiles = S // ts

    def body(au, bu, ad, bd, mu_ref, lam_ref, carry):
        @pl.when(pl.program_id(0) == 0)
        def _():
            carry[...] = jnp.zeros_like(carry)

        n_sub = ts // SUBLANES
        rows = lax.broadcasted_iota(jnp.int32, (SUBLANES, C), 0)

        def half(a_ref, b_ref, out_ref, r0, c_in, reverse):
            a = a_ref[pl.ds(r0, SUBLANES), :]
            b = b_ref[pl.ds(r0, SUBLANES), :]
            cum_a, c0 = _scan_rows(a, a * b, reverse)
            c = c0 + cum_a * c_in
            edge = 0 if not reverse else SUBLANES - 1
            c_prev = jnp.where(rows == edge, c_in, pltpu.roll(c, SUBLANES - 1 if reverse else 1, 0))
            out_ref[pl.ds(r0, SUBLANES), :] = b + c_prev
            return c[0:1, :] if reverse else c[SUBLANES - 1:SUBLANES, :]

        def step(j, c):
            cu, cd = c
            cu = half(au, bu, mu_ref, pl.multiple_of(j * SUBLANES, SUBLANES), cu, False)
            cd = half(ad, bd, lam_ref, pl.multiple_of((n_sub - 1 - j) * SUBLANES, SUBLANES), cd, True)
            return cu, cd

        cu, cd = lax.fori_loop(0, n_sub, step, (carry[0:1, :], carry[1:2, :]), unroll=4)
        carry[0:1, :] = cu
        carry[1:2, :] = cd

    fw = lambda i: (i, 0)
    bw = lambda i: (n_tiles - 1 - i, 0)
    sds = jax.ShapeDtypeStruct((S, C), F32)
    return _rows(name, S, ts,
                 [(a_up, (ts, C), fw), (b_up, (ts, C), fw), (a_dn, (ts, C), bw), (b_dn, (ts, C), bw)],
                 [(sds, (ts, C), fw), (sds, (ts, C), bw)], body, scratch=[pltpu.VMEM((8, C), F32)])


def _halo_ex(arr, S, tm, width):
    per = tm // HALO
    last = S // HALO - 1
    return [
        (arr, (tm, width), lambda i, j: (i, 0)),
        (arr, (HALO, width), lambda i, j: (jnp.maximum(i * per - 1, 0), 0)),
        (arr, (HALO, width), lambda i, j: (jnp.minimum((i + 1) * per, last), 0)),
    ]


def _rg_gates_bwd(xc, bd, prm, lam_f, lam_b, h_f, h_b):
    S = xc.shape[0]
    tm = _tile(S, 256)
    n_tiles = S // tm

    def epi(i, accs, ex, out):
        pre = accs[0][...]
        xv = ex[0][...]
        prm_ref = ex[1]
        lams = (ex[2][...], ex[3][...])
        hprev = (_shift(_ext(ex[4], ex[5], ex[6], i, n_tiles), -1, tm),
                 _shift(_ext(ex[7], ex[8], ex[9], i, n_tiles), 1, tm))
        dxc = jnp.zeros_like(xv)
        rows = []
        dlam_rows = []
        for d in range(2):
            r, ig, sp, a, sq = _rg_gate_terms(pre, xv, prm_ref, d)
            lam = lams[d]
            da = lam * hprev[d]
            di = lam * sq * xv
            dxc = dxc + lam * sq * ig
            dsq = lam * ig * xv
            dlog_a = da * a - dsq * (a * a) / sq
            dpre_r = dlog_a * (-RG_C * sp) * r * (1.0 - r)
            dpre_i = di * ig * (1.0 - ig)
            out[0][:, d * 1024:d * 1024 + RG_W] = dpre_r.astype(BF16)
            out[0][:, d * 1024 + RG_W:(d + 1) * 1024] = dpre_i.astype(BF16)
            rows += [jnp.sum(dpre_r, axis=0, keepdims=True), jnp.sum(dpre_i, axis=0, keepdims=True)]
            dsp = jnp.sum(dlog_a * (-RG_C * r), axis=0, keepdims=True)
            dlam_rows.append(-dsp * _sigmoid(-prm_ref[4 + d:5 + d, :]))
        out[1][...] = dxc
        zero = jnp.zeros((2, RG_W), F32)
        _colsum_into(out[2], i, jnp.concatenate(rows + dlam_rows + [zero], axis=0))

    blk = (tm, RG_W)
    im = lambda i, j: (i, 0)
    extras = ([(xc, blk, im), (prm, (8, RG_W), _row0), (lam_f, blk, im), (lam_b, blk, im)]
              + _halo_ex(h_f, S, tm, RG_W) + _halo_ex(h_b, S, tm, RG_W))
    outs = [(jax.ShapeDtypeStruct((S, 4 * RG_W), BF16), (tm, 4 * RG_W), im),
            (jax.ShapeDtypeStruct((S, RG_W), F32), blk, im),
            (jax.ShapeDtypeStruct((8, RG_W), F32), (8, RG_W), _row0)]
    return _fused_mm("rg_gates_bwd", S, 4 * RG_W, RG_W, tm, 4 * RG_W, RG_W, [(xc, "mk"), (bd, "kn")], [(0, 1, 0)],
                     extras, outs, epi)


def _roll_rows(ext, off):
    if off == 0:
        return ext
    return pltpu.roll(ext, (-off) % ext.shape[0], 0)


def _conv_bwd(name, p, colblk, w, grads, mode):
    S = p.shape[0]
    ts = _tile(S, 512)
    n_tiles = S // ts
    C = w.shape[1]
    ng = len(grads)

    def body(*refs):
        p_refs = refs[0:3]
        g_refs = refs[3:3 + 3 * ng]
        w_ref = refs[3 + 3 * ng]
        dx_ref, dw_ref, db_ref = refs[4 + 3 * ng:]
        i = pl.program_id(0)
        ext_p = _ext(*p_refs, i, n_tiles)
        dn = _ext(*g_refs[0:3], i, n_tiles)
        for gi in range(1, ng):
            dn = dn + _ext(*g_refs[3 * gi:3 * gi + 3], i, n_tiles)
        if mode == "bias":
            dc = dn
        else:
            c = None
            for j in range(CONV_W):
                term = w_ref[j:j + 1, :] * _roll_rows(ext_p, j - 2)
                c = term if c is None else c + term
            sig = _sigmoid(c)
            s = c * sig
            if mode in ("q", "k"):
                scale = GDN_DK ** -0.5 if mode == "q" else 1.0
                parts = []
                for h in range(GDN_H):
                    cols = slice(h * GDN_DK, (h + 1) * GDN_DK)
                    sh = s[:, cols]
                    dnh = dn[:, cols]
                    rinv = lax.rsqrt(jnp.sum(sh * sh, axis=-1, keepdims=True) + EPS)
                    parts.append(scale * rinv * (dnh - sh * (rinv * rinv) * jnp.sum(dnh * sh, axis=-1, keepdims=True)))
                ds = jnp.concatenate(parts, axis=-1)
            else:
                ds = dn
            dc = ds * (sig * (1.0 + c * (1.0 - sig)))
        dx = None
        for j in range(CONV_W):
            term = w_ref[j:j + 1, :] * _shift(dc, 2 - j, ts)
            dx = term if dx is None else dx + term
        dx_ref[...] = dx.astype(BF16)
        dc_main = dc[HALO:HALO + ts]
        dw = jnp.concatenate([jnp.sum(dc_main * _shift(ext_p, j - 2, ts), axis=0, keepdims=True)
                              for j in range(CONV_W)], axis=0)
        _colsum_into(dw_ref, i, dw)
        _colsum_into(db_ref, i, jnp.sum(dc_main, axis=0, keepdims=True))

    ins = _halo_ins(p, S, ts, C, colblk)
    for garr in grads:
        ins += _halo_ins(garr, S, ts, C, 0)
    ins += [(w, (CONV_W, C), lambda i: (0, 0))]
    z0 = lambda i: (0, 0)
    outs = [(jax.ShapeDtypeStruct((S, C), BF16), (ts, C), lambda i: (i, 0)),
            (jax.ShapeDtypeStruct((CONV_W, C), F32), (CONV_W, C), z0),
            (jax.ShapeDtypeStruct((1, C), F32), (1, C), z0)]
    return _rows(name, S, ts, ins, outs, body)


def _gdn_scan_bwd(loc, do):
    S = do.shape[0]
    ts = _tile(S, GDN_TS)
    n_tiles = S // ts
    ncb = ts // CHUNK
    nch = S // CHUNK

    def body(*refs):
        ins = (refs[0:6], refs[6:12])
        outs = (refs[12:14], refs[14:16])
        dstate = refs[16]

        @pl.when(pl.program_id(0) == 0)
        def _():
            dstate[...] = jnp.zeros_like(dstate)

        def chunk(cc, carry):
            chains = []
            for d in range(2):
                c = ncb - 1 - cc if d == 0 else cc
                rows = pl.ds(pl.multiple_of(c * CHUNK, CHUNK), CHUNK)
                for h in range(GDN_H):
                    cols = slice(h * GDN_DK, (h + 1) * GDN_DK)
                    chains.append(dict(d=d, h=h, c=c, rows=rows, cols=cols, dsn=dstate[d * GDN_H + h]))
            for ch in chains:
                qd_ref, kd_ref, cd_ref, w_ref, a_ref, do_ref = ins[ch["d"]]
                rows, cols = ch["rows"], ch["cols"]
                dob = do_ref[rows, cols].astype(BF16)
                ch["dvn"] = (_dot(a_ref[ch["c"], ch["h"]], dob, 0, 0)
                             + _dot(kd_ref[rows, cols], ch["dsn"].astype(BF16), 1, 0))
                ch["qdo"] = _dot(qd_ref[rows, cols], dob, 0, 0)
            for ch in chains:
                w_ref = ins[ch["d"]][3]
                ch["wdvn"] = _dot(w_ref[ch["rows"], ch["cols"]], ch["dvn"].astype(BF16), 0, 0)
            for ch in chains:
                dvn_ref, ds_ref = outs[ch["d"]]
                cd_ref = ins[ch["d"]][2]
                col = ch["d"] * GDN_H + ch["h"]
                dvn_ref[ch["rows"], ch["cols"]] = ch["dvn"].astype(BF16)
                ds_ref[ch["c"], ch["h"]] = ch["dsn"].astype(BF16)
                dstate[ch["d"] * GDN_H + ch["h"]] = (ch["qdo"] + cd_ref[ch["c"], col:col + 1, :] * ch["dsn"]
                                                     - ch["wdvn"])
            return carry

        lax.fori_loop(0, ncb, chunk, 0)

    ins, outs = [], []
    for d in range(2):
        tix = _dir_tile(d, n_tiles, True)
        im = lambda i, tix=tix: (tix(i), 0)
        im4 = lambda i, tix=tix: (tix(i), 0, 0, 0)
        _, w, a, _, qd, kd = loc[d]
        ins += [(qd, (ts, GDN_W), im), (kd, (ts, GDN_W), im), (loc[2], (ncb, 8, 128), lambda i, tix=tix: (tix(i), 0, 0)),
                (w, (ts, GDN_W), im), (a, (ncb, GDN_H, CHUNK, CHUNK), im4), (do, (ts, GDN_W), im)]
        outs += [(jax.ShapeDtypeStruct((S, GDN_W), BF16), (ts, GDN_W), im),
                 (jax.ShapeDtypeStruct((nch, GDN_H, GDN_DK, GDN_DK), BF16), (ncb, GDN_H, GDN_DK, GDN_DK), im4)]
    res = _rows("gdn_scan_bwd", S, ts, ins, outs, body, scratch=[pltpu.VMEM((2 * GDN_H, GDN_DK, GDN_DK), F32)])
    return res[0:2], res[2:4]


def _gdn_local_bwd(q, k, v, bg, gcr, do, loc, fwd, adj):
    S = q.shape[0]
    ts = _tile(S, GDN_TS)
    ncb = ts // CHUNK

    def body(q_ref, k_ref, v_ref, bg_ref, gcr_ref, do_ref, *rest):
        per_dir = (rest[0:5], rest[5:10])
        dq_ref, dk_ref, dv_ref, dbg_ref, dbgr_ref = rest[10:15]
        ri, ci = _tri_masks()
        lane = lax.broadcasted_iota(jnp.int32, (CHUNK, 128), 1)
        rowi = lax.broadcasted_iota(jnp.int32, (CHUNK, 1), 0)
        ones8 = jnp.ones((SUBLANES, CHUNK), F32)

        def chunk(c, carry):
            r0 = pl.multiple_of(c * CHUNK, CHUNK)
            rows = pl.ds(r0, CHUNK)
            chains = []
            for h in range(GDN_H):
                cols = slice(h * GDN_DK, (h + 1) * GDN_DK)
                qh, kh, vh = q_ref[rows, cols], k_ref[rows, cols], v_ref[rows, cols]
                dob = do_ref[rows, cols].astype(BF16)
                both = _bdot(jnp.concatenate([qh, kh], axis=0), kh, 1, 1)
                for d in range(2):
                    chains.append(dict(h=h, d=d, cols=cols, qh=qh, kh=kh, vh=vh, dob=dob, qk=both[0:CHUNK],
                                       kk=both[CHUNK:2 * CHUNK], col=d * GDN_H + h))
            for ch in chains:
                m = _gdn_decay(bg_ref, gcr_ref, c, rows, r0, ch["col"], ch["d"] == 1, ri, ci)
                t_ref, s_ref, ds_ref, vn_ref, dvn_ref = per_dir[ch["d"]]
                h, cols = ch["h"], ch["cols"]
                ch["m"] = m
                ch["kb"] = ch["kh"] * m["beta"]
                ch["kbg"] = ch["kb"] * m["eg"]
                ch["t"] = t_ref[c, h]
                stb = s_ref[c, h]
                ch["dsn"] = ds_ref[c, h]
                vnb = vn_ref[rows, cols]
                dvnb = dvn_ref[rows, cols]
                ch["dcd"] = jnp.sum(jnp.sum(stb.astype(F32) * ch["dsn"].astype(F32), axis=1, keepdims=True),
                                    axis=0, keepdims=True)
                ch["dqd"] = _dot(ch["dob"], stb, 1, 1)
                ch["d_a"] = _dot(ch["dob"], vnb, 1, 1)
                ch["dkd"] = _bdot(vnb, ch["dsn"], 1, 1)
                ch["dw"] = -_dot(dvnb, stb, 1, 1)
                ch["dvb"] = _dot(ch["t"], dvnb, 1, 0)
                ch["d_t"] = _bdot(dvnb, ch["vh"] * m["beta"], 1, 1)
            for ch in chains:
                dwb = ch["dw"].astype(BF16)
                ch["d_t"] = ch["d_t"] + _bdot(dwb, ch["kbg"], 1, 1)
                ch["dkbg"] = _dot(ch["t"], dwb, 1, 0)
                ch["nn"] = ch["d_a"] * ch["m"]["dm"]
                ch["nn_q"] = _bdot(ch["nn"], ch["qh"], 0, 0)
                ch["nn_k"] = _bdot(ch["nn"], ch["kh"], 1, 0)
            for ch in chains:
                ch["x"] = _dot(ch["d_t"].astype(BF16), ch["t"], 1, 0)
            for ch in chains:
                d_l = -_dot(ch["t"], ch["x"].astype(BF16), 1, 0)
                ch["d_l"] = jnp.where(ch["m"]["strict"], d_l, 0.0)
                ch["mm"] = ch["d_l"] * ch["m"]["dm"]
            for ch in chains:
                m = ch["m"]
                ch["mm_kh"] = _bdot(ch["mm"], ch["kh"], 1, 0)
                ch["mm_kb"] = _bdot(ch["mm"], ch["kb"], 0, 0)
                l_mat = jnp.where(m["strict"], m["beta"] * ch["kk"] * m["dm"], 0.0)
                ch["e"] = ch["d_l"] * l_mat + ch["nn"] * ch["qk"]
                dbgr_ref[c, ch["col"]:ch["col"] + 1, :] = -_dot(ones8, ch["e"], 1, 0, HI)[0:1, :]
            acc_bg = jnp.zeros((CHUNK, 128), F32)
            acc = {}
            for ch in chains:
                m = ch["m"]
                beta, eg, egl = m["beta"], m["eg"], m["egl"]
                dkb = ch["mm_kh"] + ch["dkbg"] * eg
                dk_d = ch["mm_kb"] + ch["nn_q"] + ch["dkd"] * egl + dkb * beta
                dq_d = ch["nn_k"] + ch["dqd"] * eg
                dv_d = ch["dvb"] * beta
                dkd_kd = ch["dkd"] * (ch["kh"] * egl)
                dgc = (jnp.sum(ch["e"], axis=1, keepdims=True)
                       + jnp.sum(ch["dqd"] * (ch["qh"] * eg) - dkd_kd + ch["dkbg"] * ch["kbg"], axis=1, keepdims=True))
                dgl = jnp.sum(jnp.sum(dkd_kd, axis=1, keepdims=True), axis=0, keepdims=True) + ch["dcd"] * m["cd"]
                dgc = dgc + jnp.where(rowi == (0 if ch["d"] == 1 else CHUNK - 1), dgl, 0.0)
                dbeta = jnp.sum(dkb * ch["kh"] + ch["dvb"] * ch["vh"], axis=1, keepdims=True)
                acc_bg = acc_bg + jnp.where(lane == ch["col"], dbeta, 0.0) + jnp.where(lane == 8 + ch["col"], dgc, 0.0)
                if ch["d"] == 0:
                    acc[ch["h"]] = (dq_d, dk_d, dv_d)
                else:
                    dq0, dk0, dv0 = acc[ch["h"]]
                    dq_ref[rows, ch["cols"]] = dq0 + dq_d
                    dk_ref[rows, ch["cols"]] = dk0 + dk_d
                    dv_ref[rows, ch["cols"]] = dv0 + dv_d
            dbg_ref[rows, :] = acc_bg
            return carry

        lax.fori_loop(0, ncb, chunk, 0)

    im = lambda i: (i, 0)
    im4 = lambda i: (i, 0, 0, 0)
    blk = (ts, GDN_W)
    ins = [(q, blk, im), (k, blk, im), (v, blk, im), (bg, (ts, 128), im), (gcr, (ncb, 8, CHUNK), lambda i: (i, 0, 0)),
           (do, blk, im)]
    for d in range(2):
        ins += [(loc[d][3], (ncb, GDN_H, CHUNK, CHUNK), im4), (fwd[d][2], (ncb, GDN_H, GDN_DK, GDN_DK), im4),
                (adj[d][1], (ncb, GDN_H, GDN_DK, GDN_DK), im4), (fwd[d][1], blk, im), (adj[d][0], blk, im)]
    sds = jax.ShapeDtypeStruct((S, GDN_W), F32)
    outs = [(sds, blk, im), (sds, blk, im), (sds, blk, im), (jax.ShapeDtypeStruct((S, 128), F32), (ts, 128), im),
            (jax.ShapeDtypeStruct((S // CHUNK, 8, CHUNK), F32), (ncb, 8, CHUNK), lambda i: (i, 0, 0))]
    dq, dk, dv, dbg, dbg_rows = _rows("gdn_local_bwd", S, ts, ins, outs, body)
    dgc_cols = dbg_rows.transpose(0, 2, 1).reshape(S, 8)
    return dq, dk, dv, dbg + jnp.pad(dgc_cols, ((0, 0), (8, 112)))


def _gdn_prep_bwd(dbg_all, p, prm):
    S = p.shape[0]
    ts = _tile(S, 512)

    def body(dbg_ref, p_ref, prm_ref, dba_ref, dprm_ref):
        i = pl.program_id(0)
        raw = p_ref[...]
        dbg = dbg_ref[...]
        lane = lax.broadcasted_iota(jnp.int32, (1, 128), 1)
        is_g = (lane >= 8) & (lane < 16)
        ea = jnp.exp(prm_ref[0:1, :])
        arg = raw + prm_ref[1:2, :]
        g = jnp.where(is_g, -ea * _softplus(arg), 0.0)
        beta = _sigmoid(raw)
        dgc = jnp.where(is_g, dbg, 0.0)
        ri, ci = _tri_masks()
        lower = (ri >= ci).astype(F32)
        upper = (ri <= ci).astype(F32)
        dgs = []
        for c in range(ts // CHUNK):
            ch = dgc[c * CHUNK:(c + 1) * CHUNK]
            dgs.append(jnp.where(lane < 12, _dot(upper, ch, 1, 0, HI), _dot(lower, ch, 1, 0, HI)))
        dg = jnp.concatenate(dgs, axis=0)
        dalpha = jnp.where(is_g, dg * (-ea) * _sigmoid(arg), 0.0)
        dba_ref[...] = jnp.where(lane < 8, dbg * beta * (1.0 - beta), dalpha).astype(BF16)
        rows = jnp.concatenate([jnp.sum(dg * g, axis=0, keepdims=True), jnp.sum(dalpha, axis=0, keepdims=True),
                                jnp.zeros((6, 128), F32)], axis=0)
        _colsum_into(dprm_ref, i, rows)

    im = lambda i: (i, 0)
    z0 = lambda i: (0, 0)
    return _rows("gdn_prep_bwd", S, ts,
                 [(dbg_all, (ts, 128), im), (p, (ts, 128), lambda i: (i, COL_BA // 128)), (prm, (8, 128), z0)],
                 [(jax.ShapeDtypeStruct((S, 128), BF16), (ts, 128), im), (jax.ShapeDtypeStruct((8, 128), F32), (8, 128), z0)],
                 body)


def _mm_plain(name, M, N, K, tm, tn, tk, a, am, b, bm, dtype):
    return _fused_mm(name, M, N, K, tm, tn, tk, [(a, am), (b, bm)], [(0, 1, 0)], [],
                     [(jax.ShapeDtypeStruct((M, N), dtype), (tm, tn), _mn)],
                     lambda i, accs, ex, out: out[0].__setitem__(Ellipsis, accs[0][...].astype(dtype)))[0]


def _layer_bwd(x0, W, R, emit_big=None, emit_small=None):
    S = x0.shape[0]
    tm = _tile(S, 512)
    tk_s = _tile(S, 1024)
    G = {}

    def emit(**named):
        if emit_big is None:
            G.update(named)
            return None
        return emit_big(**named)

    def ffn_emit(prefix):
        return lambda **kw: emit(**{f"{prefix}_w_{k}": v for k, v in kw.items()})

    dx2, G["ffn2_norm"] = _ffn_bwd("ffn2b", R["dx3"], R["x2"], W["ffn2_norm"], R["h3"], R["a2"], R["b2"], R["f2"],
                                   W["ffn2_w_gate"], W["ffn2_w_up"], W["ffn2_w_down"], ffn_emit("ffn2"))
    tok = emit(w_out=_mm_plain("dw_out", D_MODEL, D_MODEL, S, D_MODEL, D_MODEL, tk_s, R["y"], "km", dx2, "kn", BF16))
    gn = W["gdn_norm"] if tok is None else W["gdn_norm"] + tok
    dy = _mm_plain("dy_mix", S, D_MODEL, D_MODEL, tm, D_MODEL, D_MODEL, dx2, "mk", W["w_out"], "nk", F32)
    p = R["p"]
    dhr, dgate, do, dz, G["gdn_norm"] = _mix_out_bwd(dy, R["h_f"], R["h_b"], R["o_f"], R["o_b"], p, gn)
    lam_b, lam_f = _rg_scan_adj("rg_scan_bwd", R["a_b"], dhr, R["a_f"], dhr)
    dpre, dxc_direct, d_rgprm = _rg_gates_bwd(R["xc"], R["bd"], R["rg_prm"], lam_f, lam_b, R["h_f"], R["h_b"])
    tmg = _tile(S, 512)
    dxc = _fused_mm("rg_dxc", S, RG_W, 4 * RG_W, tmg, RG_W, 4 * RG_W, [(dpre, "mk"), (R["bd"], "nk")], [(0, 1, 0)],
                    [(dxc_direct, (tmg, RG_W), _mn)], [(jax.ShapeDtypeStruct((S, RG_W), F32), (tmg, RG_W), _mn)],
                    lambda i, accs, ex, out: out[0].__setitem__(Ellipsis, ex[0][...] + accs[0][...]))[0]
    d_bd = _mm_plain("rg_dbd", RG_W, 4 * RG_W, S, RG_W, 4 * RG_W, tk_s, R["xc"], "km", dpre, "kn", F32)
    dx_rg, G["rg_conv_w"], G["rg_conv_b"] = _conv_bwd("rg_conv_bwd", p, 0, W["rg_conv_w"], [dxc], "bias")
    blocks = jnp.einsum("nigmj,nm->gnij", d_bd.reshape(RG_BLOCKS, RG_BLOCK, 4, RG_BLOCKS, RG_BLOCK),
                        jnp.eye(RG_BLOCKS, dtype=F32))
    G["rg_gate_a_w"] = jnp.stack([blocks[0], blocks[2]])
    G["rg_gate_x_w"] = jnp.stack([blocks[1], blocks[3]])
    G["rg_gate_a_b"] = jnp.stack([d_rgprm[0], d_rgprm[2]])
    G["rg_gate_x_b"] = jnp.stack([d_rgprm[1], d_rgprm[3]])
    G["rg_lambda"] = d_rgprm[4:6]
    adj = _gdn_scan_bwd(R["gdn_loc"], do)
    dq, dk, dv, dbg = _gdn_local_bwd(R["q"], R["k"], R["v"], R["bg"], R["gcr"], do, R["gdn_loc"], R["gdn_fwd"], adj)
    cw = W["gdn_conv_w"]
    dpq, dwq, _ = _conv_bwd("gdn_conv_q_bwd", p, 2, cw[:, 0:512], [dq], "q")
    dpk, dwk, _ = _conv_bwd("gdn_conv_k_bwd", p, 3, cw[:, 512:1024], [dk], "k")
    dpv, dwv, _ = _conv_bwd("gdn_conv_v_bwd", p, 4, cw[:, 1024:1536], [dv], "v")
    G["gdn_conv_w"] = jnp.concatenate([dwq, dwk, dwv], axis=1)
    dba, d_gprm = _gdn_prep_bwd(dbg, p, R["gdn_prm"])
    G["gdn_a_log"] = d_gprm[0, 8:16].reshape(2, GDN_H)
    G["gdn_dt_bias"] = d_gprm[1, 8:16].reshape(2, GDN_H)
    dp = jnp.concatenate([dx_rg, dgate, dpq, dpk, dpv, dz, dba], axis=1)
    tok = emit(w_in=_mm_plain("dw_in", D_MODEL, D_IN_PAD, S, D_MODEL, 640, tk_s, R["h2"], "km", dp, "kn", BF16))
    g_mix = W["mix_norm"] if tok is None else W["mix_norm"] + tok

    def epi_dx1(i, accs, ex, out):
        dx, dgt = _rmsnorm_bwd_tile(accs[0][...], ex[0][...], ex[1][...])
        out[0][...] = ex[2][...] + dx
        _colsum_into(out[1], i, jnp.sum(dgt, axis=0, keepdims=True))

    dx1, G["mix_norm"] = _fused_mm(
        "mix_dx", S, D_MODEL, D_IN_PAD, tm, D_MODEL, D_IN_PAD, [(dp, "mk"), (W["w_in"], "nk")], [(0, 1, 0)],
        [(R["x1"], (tm, D_MODEL), _mn), (g_mix, (1, D_MODEL), _row0), (dx2, (tm, D_MODEL), _mn)],
        [(jax.ShapeDtypeStruct((S, D_MODEL), F32), (tm, D_MODEL), _mn),
         (jax.ShapeDtypeStruct((1, D_MODEL), F32), (1, D_MODEL), _row0)], epi_dx1)
    G["final_norm"] = R["d_final_norm"]
    if emit_small is not None:
        emit_small(G)
    dx0, G["ffn1_norm"] = _ffn_bwd("ffn1b", dx1, x0, W["ffn1_norm"], R["h1"], R["a1"], R["b1"], R["f1"],
                                   W["ffn1_w_gate"], W["ffn1_w_up"], W["ffn1_w_down"], ffn_emit("ffn1"))
    return dx0, G


def _mesh_pos():
    x, y, c = lax.axis_index("x"), lax.axis_index("y"), lax.axis_index("c")
    return x, y, c, 4 * x + 2 * y + c


def _peer(x, y, c, r):
    px = 1 - x if r & 4 else x
    py = 1 - y if r & 2 else y
    pc = 1 - c if r & 1 else c
    return (px, py, pc), 4 * px + 2 * py + pc


_HBM = pl.BlockSpec(memory_space=pltpu.HBM)
_SEM = pl.BlockSpec(memory_space=pltpu.SEMAPHORE)


def _peer_copies(scatter, srcs, lands, send_sems, recv_sems):
    x, y, c, me = _mesh_pos()
    copies = []
    for a, (src, land) in enumerate(zip(srcs, lands)):
        for r in range(1, N_DEV):
            peer, peer_idx = _peer(x, y, c, r)
            copies.append(pltpu.make_async_remote_copy(
                src_ref=src.at[peer_idx] if scatter else src, dst_ref=land.at[r - 1] if scatter else land.at[me],
                send_sem=send_sems.at[a * 7 + r - 1], recv_sem=recv_sems.at[a * 7 + r - 1],
                device_id=peer, device_id_type=pl.DeviceIdType.MESH))
    return copies


def _exchange_start(name, scatter, arrays):
    slabs = arrays
    n = len(slabs)

    def body(*refs):
        srcs, lands = refs[0:n], refs[n:2 * n]
        send_sems, recv_sems = refs[2 * n], refs[2 * n + 1]
        token = refs[4 * n + 2]
        for cp in _peer_copies(scatter, srcs, lands, send_sems, recv_sems):
            cp.start()
        token[...] = jnp.zeros_like(token)

    land_shapes = [(N_DEV - 1,) + s.shape[1:] if scatter else (N_DEV,) + s.shape for s in slabs]
    out_shape = ([pltpu.SemaphoreType.DMA((7 * n,)), pltpu.SemaphoreType.DMA((7 * n,))]
                 + [pltpu.HBM(s.shape, s.dtype) for s in slabs]
                 + [pltpu.HBM(shp, s.dtype) for shp, s in zip(land_shapes, slabs)]
                 + [jax.ShapeDtypeStruct((8, 128), F32)])
    res = pl.pallas_call(
        body, name=name, out_shape=out_shape, in_specs=[_HBM] * (2 * n),
        out_specs=[_SEM, _SEM] + [_HBM] * (2 * n) + [pl.BlockSpec(memory_space=pltpu.VMEM)],
        input_output_aliases={i: 2 + i for i in range(2 * n)},
        compiler_params=pltpu.CompilerParams(has_side_effects=pltpu.SideEffectType.DATAFLOW_SIDE_EFFECTING),
    )(*[pltpu.with_memory_space_constraint(s, pltpu.HBM) for s in slabs],
      *[pltpu.with_memory_space_constraint(lax.empty(shp, s.dtype), pltpu.HBM) for shp, s in zip(land_shapes, slabs)])
    return dict(n=n, scatter=scatter, sems=res[0:2], srcs=res[2:2 + n], lands=res[2 + n:2 + 2 * n],
                token=res[2 + 2 * n][0, 0])


def _exchange_wait(name, started, after):
    n = started["n"]
    scatter = started["scatter"]

    def body(*refs):
        srcs, lands = refs[0:n], refs[n:2 * n]
        send_sems, recv_sems = refs[2 * n], refs[2 * n + 1]
        for cp in _peer_copies(scatter, srcs, lands, send_sems, recv_sems):
            cp.wait_send()
            cp.wait_recv()

    arrays = list(started["srcs"]) + list(started["lands"])
    res = pl.pallas_call(
        body, name=name, out_shape=[pltpu.HBM(a.shape, a.dtype) for a in arrays],
        in_specs=[_HBM] * (2 * n) + [_SEM, _SEM, pl.BlockSpec(memory_space=pl.ANY)], out_specs=[_HBM] * (2 * n),
        input_output_aliases={i: i for i in range(2 * n)},
        compiler_params=pltpu.CompilerParams(has_side_effects=pltpu.SideEffectType.DATAFLOW_SIDE_EFFECTING),
    )(*arrays, *started["sems"], after)
    return res[0:n], res[n:2 * n]


def _all_gather(name, arrays):
    n = len(arrays)

    def body(*refs):
        ins = refs[:n]
        outs = refs[n:2 * n]
        token = refs[2 * n]
        send_sems, recv_sems, local_sems = refs[2 * n + 1:]
        token[...] = jnp.zeros_like(token)
        x, y, c, me = _mesh_pos()
        sibling = (x, y, 1 - c)
        chips = [(1 - x, y), (x, 1 - y), (1 - x, 1 - y)]

        def idx(px, py, pc):
            return 4 * px + 2 * py + pc

        def copy(a, k, block, to, src=None):
            slot = outs[a].at[idx(*block)]
            return pltpu.make_async_remote_copy(
                src_ref=slot if src is None else src, dst_ref=slot, send_sem=send_sems.at[a * 7 + k],
                recv_sem=recv_sems.at[a * 7 + k], device_id=to, device_id_type=pl.DeviceIdType.MESH)

        locals_, sends = [], []
        for a in range(n):
            loc = pltpu.make_async_copy(ins[a], outs[a].at[me], local_sems.at[a])
            loc.start()
            locals_.append(loc)
            sends.append(copy(a, 0, (x, y, c), sibling, src=ins[a]))
            sends += [copy(a, 1 + j, (x, y, c), (*chip, c), src=ins[a]) for j, chip in enumerate(chips)]
        for cp in sends:
            cp.start()
        passed = []
        for a in range(n):
            for j, chip in enumerate(chips):
                copy(a, 1 + j, (*chip, c), (x, y, c)).wait_recv()
                fwd = copy(a, 4 + j, (*chip, c), sibling)
                fwd.start()
                passed.append(fwd)
        for a in range(n):
            copy(a, 0, sibling, (x, y, c)).wait_recv()
            for j, chip in enumerate(chips):
                copy(a, 4 + j, (*chip, 1 - c), (x, y, c)).wait_recv()
        for cp in sends + passed:
            cp.wait_send()
        for loc in locals_:
            loc.wait()

    any_spec = pl.BlockSpec(memory_space=pl.ANY)
    res = pl.pallas_call(
        body, name=name, in_specs=[any_spec] * n, out_specs=[any_spec] * n + [pl.BlockSpec(memory_space=pltpu.VMEM)],
        out_shape=[jax.ShapeDtypeStruct((N_DEV,) + a.shape, a.dtype) for a in arrays]
        + [jax.ShapeDtypeStruct((8, 128), F32)],
        scratch_shapes=[pltpu.SemaphoreType.DMA((7 * n,)), pltpu.SemaphoreType.DMA((7 * n,)),
                        pltpu.SemaphoreType.DMA((n,))],
        compiler_params=pltpu.CompilerParams(has_side_effects=True),
    )(*arrays)
    return res[:n], res[n][0, 0]


def _adamw_math(w, g, m, v):
    m2 = ADAM_B1 * m + (1.0 - ADAM_B1) * g
    v2 = ADAM_B2 * v + (1.0 - ADAM_B2) * (g * g)
    m_hat = m2 / (1.0 - ADAM_B1 ** ADAM_STEP)
    v_hat = v2 / (1.0 - ADAM_B2 ** ADAM_STEP)
    delta = -ADAM_LR * (m_hat / (jnp.sqrt(v_hat) + ADAM_EPS) + ADAM_WD * w)
    return delta, m2, v2


def _adamw_slabs(name, src, land, me, w, m, v, tr):
    R, C = w.shape

    def body(me_ref, own_ref, land_ref, w_ref, m_ref, v_ref, g_ref, d_ref, m2_ref, v2_ref):
        g = own_ref[0].astype(F32)
        for s in range(N_DEV - 1):
            g = g + land_ref[s].astype(F32)
        delta, m2, v2 = _adamw_math(w_ref[...], g, m_ref[...], v_ref[...])
        g_ref[...] = g
        d_ref[...] = delta
        m2_ref[...] = m2
        v2_ref[...] = v2

    im = lambda i, me_ref: (i, 0)
    grid_spec = pltpu.PrefetchScalarGridSpec(
        num_scalar_prefetch=1, grid=(R // tr,),
        in_specs=[pl.BlockSpec((1, tr, C), lambda i, me_ref: (me_ref[0], i, 0)),
                  pl.BlockSpec((N_DEV - 1, tr, C), lambda i, me_ref: (0, i, 0)),
                  pl.BlockSpec((tr, C), im), pl.BlockSpec((tr, C), im), pl.BlockSpec((tr, C), im)],
        out_specs=[pl.BlockSpec((tr, C), im)] * 4)
    return pl.pallas_call(body, name=name, grid_spec=grid_spec, out_shape=[jax.ShapeDtypeStruct((R, C), F32)] * 4,
                          compiler_params=_cp(1))(me.reshape(1).astype(jnp.int32), src, land, w, m, v)


def _sum_slots(name, slots):
    _, R, C = slots.shape

    def body(s_ref, o_ref):
        g = s_ref[0]
        for s in range(1, N_DEV):
            g = g + s_ref[s]
        o_ref[...] = g

    return _rows(name, R, R, [(slots, (N_DEV, R, C), lambda i: (0, 0, 0))],
                 [(jax.ShapeDtypeStruct((R, C), F32), (R, C), lambda i: (0, 0))], body)[0]


def _adamw_packed(name, g, w, m, v):
    R, C = g.shape

    def body(g_ref, w_ref, m_ref, v_ref, d_ref, m2_ref, v2_ref):
        delta, m2, v2 = _adamw_math(w_ref[...], g_ref[...], m_ref[...], v_ref[...])
        d_ref[...] = delta
        m2_ref[...] = m2
        v2_ref[...] = v2

    im = lambda i: (0, 0)
    sds = jax.ShapeDtypeStruct((R, C), F32)
    return _rows(name, R, R, [(a, (R, C), im) for a in (g, w, m, v)], [(sds, (R, C), im)] * 3, body)


def _pack(arrays):
    rows = []
    for a in arrays:
        flat = a.reshape(-1).astype(F32)
        pad = (-flat.shape[0]) % 128
        rows.append(jnp.pad(flat, (0, pad)).reshape(-1, 128))
    out = jnp.concatenate(rows, axis=0)
    return jnp.pad(out, ((0, (-out.shape[0]) % 8), (0, 0)))


def _unpack(packed, shapes):
    lead = packed.shape[:-2]
    outs = []
    r = 0
    for shp in shapes:
        n = math.prod(shp)
        nr = -(-n // 128)
        flat = packed[..., r:r + nr, :].reshape(lead + (nr * 128,))[..., :n]
        outs.append(flat.reshape(lead + tuple(shp)))
        r += nr
    return outs


FFN1_BIG = ["ffn1_w_gate", "ffn1_w_up", "ffn1_w_down"]
MIX_BIG = ["w_in", "w_out"]
FFN2_BIG = ["ffn2_w_gate", "ffn2_w_up", "ffn2_w_down"]
BIG = FFN1_BIG + MIX_BIG + FFN2_BIG
COL_SHARDED = {"ffn1_w_gate", "ffn1_w_up", "w_in", "ffn2_w_gate", "ffn2_w_up"}
SMALL_SHARDED = ["rg_conv_w", "rg_gate_a_b", "rg_gate_x_b", "rg_lambda", "gdn_conv_w"]
WEIGHTS = ["ffn1_norm", "ffn1_w_gate", "ffn1_w_up", "ffn1_w_down", "mix_norm", "w_in", "w_out", "rg_conv_w", "rg_conv_b",
           "rg_gate_a_w", "rg_gate_a_b", "rg_gate_x_w", "rg_gate_x_b", "rg_lambda", "gdn_conv_w", "gdn_a_log",
           "gdn_dt_bias", "gdn_norm", "ffn2_norm", "ffn2_w_gate", "ffn2_w_up", "ffn2_w_down", "final_norm"]
SMALL = [n for n in WEIGHTS if n not in BIG]
ROW_VECTORS = {"ffn1_norm", "mix_norm", "ffn2_norm", "gdn_norm", "rg_conv_b", "final_norm"}
ROW_TILE = {"ffn1_w_gate": 256, "ffn1_w_up": 256, "ffn1_w_down": 176, "w_in": 256, "w_out": 64,
            "ffn2_w_gate": 256, "ffn2_w_up": 256, "ffn2_w_down": 176}


def _unshard_cols(g):
    return g.transpose(1, 0, 2).reshape(g.shape[1], N_DEV * g.shape[2])


def _to_slabs(name, g):
    if name in COL_SHARDED:
        r, ctot = g.shape
        return g.reshape(r, N_DEV, ctot // N_DEV).transpose(1, 0, 2)
    return g.reshape(N_DEV, g.shape[0] // N_DEV, g.shape[1])


def _step(x, target, w, m, v):
    _, _, _, me = _mesh_pos()
    def unshard(n, gth):
        full = _unshard_cols(gth) if n in COL_SHARDED else gth.reshape(-1, gth.shape[-1])
        return jnp.pad(full, ((0, 0), (0, D_IN_PAD - D_IN))) if n == "w_in" else full

    def landed(started, name, after):
        srcs, lands = _exchange_wait(name, started, after)
        def with_own(src, land):
            slot = lax.broadcasted_iota(jnp.int32, (N_DEV,) + (1,) * src.ndim, 0)
            return jnp.where(slot == me, src[None], land)

        return [with_own(src, land) for src, land in zip(srcs, lands)]

    up_names = ["ffn1_w_gate", "ffn1_w_up"]
    first, tok = _all_gather("gather_ffn1", [w[n].astype(BF16) for n in up_names])
    W = {n: unshard(n, gth) for n, gth in zip(up_names, first)}
    small_shards = [w[n] for n in SMALL_SHARDED]
    st_down = _exchange_start("gather_ffn1_down_start", False, [(w["ffn1_w_down"] + tok).astype(BF16)])
    st_mix = _exchange_start("gather_mix_start", False,
                             [(w[n] + tok).astype(BF16) for n in MIX_BIG] + [_pack(small_shards) + tok])
    st_ffn2 = _exchange_start("gather_ffn2_start", False, [(w[n] + tok).astype(BF16) for n in FFN2_BIG])
    for n in SMALL:
        if n not in SMALL_SHARDED:
            W[n] = w[n]
    W["ffn1_norm"] = w["ffn1_norm"] + (st_down["token"] + st_mix["token"] + st_ffn2["token"])

    def more(stage, after):
        if stage == "ffn1_down":
            return {"ffn1_w_down": unshard("ffn1_w_down", landed(st_down, "gather_ffn1_down_wait", after)[0])}
        if stage == "ffn2":
            return {n: unshard(n, gth) for n, gth in zip(FFN2_BIG, landed(st_ffn2, "gather_ffn2_wait", after))}
        got = landed(st_mix, "gather_mix_wait", after)
        new = {n: unshard(n, gth) for n, gth in zip(MIX_BIG, got)}
        for n, gth in zip(SMALL_SHARDED, _unpack(got[-1], [s.shape for s in small_shards])):
            new[n] = jnp.moveaxis(gth, 0, -2).reshape(gth.shape[1:-1] + (N_DEV * gth.shape[-1],))
        return new

    R = _layer_fwd(x, target, W, more)
    W = R["W"]
    pending = []

    def emit_big(**named):
        slabs = [_to_slabs(n, g[:, :D_IN] if n == "w_in" else g) for n, g in named.items()]
        started = _exchange_start(f"scatter_start_{len(pending)}", True, slabs)
        pending.append((list(named), started))
        return started["token"]

    small_started = []

    def emit_small(G):
        packed = _pack([G[n] for n in SMALL if n != "ffn1_norm"])
        small_started.append(_exchange_start("gather_small_start", False, [packed]))

    grad_x, G = _layer_bwd(x, W, R, emit_big, emit_small)
    loss = lax.psum(R["loss"][0, 0], ("x", "y", "c"))
    out = {}
    for i, (names, started) in enumerate(pending):
        srcs, lands = _exchange_wait(f"scatter_wait_{i}", started, grad_x)
        for n, src, land in zip(names, srcs, lands):
            out[n] = _adamw_slabs(f"adamw_{n}", src, land, me, w[n], m[n], v[n], ROW_TILE[n])
    early = [n for n in SMALL if n != "ffn1_norm"]
    srcs, lands = _exchange_wait("gather_small_wait", small_started[0], grad_x)
    slot = lax.broadcasted_iota(jnp.int32, (N_DEV, 1, 1), 0)
    slots = jnp.where(slot == me, srcs[0][None], lands[0])
    reduced = dict(zip(early, _unpack(_sum_slots("sum_small_grads", slots), [G[n].shape for n in early])))
    late = _all_gather("gather_ffn1_norm_grad", [_pack([G["ffn1_norm"]])])[0][0]
    reduced["ffn1_norm"] = _unpack(_sum_slots("sum_ffn1_norm_grad", late), [G["ffn1_norm"].shape])[0]
    g_small = []
    for n in SMALL:
        g = reduced[n]
        if n in SMALL_SHARDED:
            per = g.shape[-1] // N_DEV
            g = lax.dynamic_slice_in_dim(g, me * per, per, axis=g.ndim - 1)
        g_small.append(g.reshape(w[n].shape))
    shapes = [w[n].shape for n in SMALL]
    d_p, m_p, v_p = _adamw_packed("adamw_small", _pack(g_small), _pack([w[n] for n in SMALL]),
                                  _pack([m[n] for n in SMALL]), _pack([v[n] for n in SMALL]))
    for n, g, d_, m_, v_ in zip(SMALL, g_small, _unpack(d_p, shapes), _unpack(m_p, shapes), _unpack(v_p, shapes)):
        out[n] = (g, d_, m_, v_)
    return loss, grad_x, out


def kernel(x, ffn1_norm, ffn1_w_gate, ffn1_w_up, ffn1_w_down, mix_norm, w_in, w_out, rg_conv_w, rg_conv_b, rg_gate_a_w, rg_gate_a_b, rg_gate_x_w, rg_gate_x_b, rg_lambda, gdn_conv_w, gdn_a_log, gdn_dt_bias, gdn_norm, ffn2_norm, ffn2_w_gate, ffn2_w_up, ffn2_w_down, final_norm, loss_target, m_ffn1_norm, m_ffn1_w_gate, m_ffn1_w_up, m_ffn1_w_down, m_mix_norm, m_w_in, m_w_out, m_rg_conv_w, m_rg_conv_b, m_rg_gate_a_w, m_rg_gate_a_b, m_rg_gate_x_w, m_rg_gate_x_b, m_rg_lambda, m_gdn_conv_w, m_gdn_a_log, m_gdn_dt_bias, m_gdn_norm, m_ffn2_norm, m_ffn2_w_gate, m_ffn2_w_up, m_ffn2_w_down, m_final_norm, v_ffn1_norm, v_ffn1_w_gate, v_ffn1_w_up, v_ffn1_w_down, v_mix_norm, v_w_in, v_w_out, v_rg_conv_w, v_rg_conv_b, v_rg_gate_a_w, v_rg_gate_a_b, v_rg_gate_x_w, v_rg_gate_x_b, v_rg_lambda, v_gdn_conv_w, v_gdn_a_log, v_gdn_dt_bias, v_gdn_norm, v_ffn2_norm, v_ffn2_w_gate, v_ffn2_w_up, v_ffn2_w_down, v_final_norm):
    args = dict(locals())
    orig_shapes = {n: args[n].shape for n in WEIGHTS}

    def local(prefix):
        d = {}
        for n in WEIGHTS:
            a = args[prefix + n]
            d[n] = a.reshape(1, -1) if n in ROW_VECTORS else a[0]
        return d

    loss, grad_x, out = _step(x[0], loss_target[0], local(""), local("m_"), local("v_"))
    res = [loss, grad_x[None]]
    for k in range(4):
        res += [out[n][k].reshape(orig_shapes[n]) for n in WEIGHTS]
    return tuple(res)
```

```python
import functools
import math

import jax
import jax.numpy as jnp
from jax import lax
from jax.experimental import pallas as pl
from jax.experimental.pallas import tpu as pltpu

F32, BF16 = jnp.float32, jnp.bfloat16

D_MODEL = 1024
D_FF = 2816
RG_W = 512
RG_BLOCKS = 8
RG_BLOCK = 64
RG_C = 8.0
CONV_W = 4
GDN_H = 4
GDN_DK = 128
CHUNK = 64
EPS = 1e-6
D_IN = 3088
D_IN_PAD = 3200
COL_BA = 3072
N_DEV = 8
HALO = 8
VMEM_LIMIT = 48 * 1024 * 1024

ADAM_LR = 0.001
ADAM_B1 = 0.9
ADAM_B2 = 0.999
ADAM_EPS = 1e-08
ADAM_WD = 0.01
ADAM_STEP = 10

HI = lax.Precision.HIGHEST


def _cp(n):
    return pltpu.CompilerParams(dimension_semantics=("arbitrary",) * n, vmem_limit_bytes=VMEM_LIMIT)


def _tile(n, pref):
    return min(n, pref)


def _sigmoid(x):
    return 0.5 * jnp.tanh(0.5 * x) + 0.5


def _softplus(x):
    return jnp.maximum(x, 0.0) + jnp.log(1.0 + jnp.exp(-jnp.abs(x)))


def _dot(a, b, ca, cb, prec=None):
    return lax.dot_general(a, b, (((ca,), (cb,)), ((), ())), preferred_element_type=F32, precision=prec)


def _fused_mm(name, M, N, K, tm, tn, tk, ops, pairs, extras, outs, epilogue):
    nm, nn, nk = M // tm, N // tn, K // tk
    assert nm * tm == M and nn * tn == N and nk * tk == K, (name, M, N, K, tm, tn, tk)
    spec_of = {
        "mk": pl.BlockSpec((tm, tk), lambda i, j, k: (i, k)),
        "km": pl.BlockSpec((tk, tm), lambda i, j, k: (k, i)),
        "kn": pl.BlockSpec((tk, tn), lambda i, j, k: (k, j)),
        "nk": pl.BlockSpec((tn, tk), lambda i, j, k: (j, k)),
    }
    in_specs = [spec_of[m] for _, m in ops]
    in_specs += [pl.BlockSpec(bs, lambda i, j, k, im=im: im(i, j)) for _, bs, im in extras]
    out_specs = [pl.BlockSpec(bs, lambda i, j, k, im=im: im(i, j)) for _, bs, im in outs]
    n_ops, n_ex, n_out = len(ops), len(extras), len(outs)
    n_acc = 1 + max(g for _, _, g in pairs)
    modes = [m for _, m in ops]

    def body(*refs):
        op_refs = refs[:n_ops]
        ex_refs = refs[n_ops:n_ops + n_ex]
        out_refs = refs[n_ops + n_ex:n_ops + n_ex + n_out]
        accs = refs[n_ops + n_ex + n_out:]
        i = pl.program_id(0)
        k = pl.program_id(2)
        def dots():
            vals = [r[...].astype(BF16) for r in op_refs]
            for ia, ib, g in pairs:
                yield g, _dot(vals[ia], vals[ib], 1 if modes[ia] == "mk" else 0, 0 if modes[ib] == "kn" else 1)

        if nk == 1:
            sums = [None] * n_acc
            for g, d in dots():
                sums[g] = d if sums[g] is None else sums[g] + d
            epilogue(i, [_Held(s) for s in sums], ex_refs, out_refs)
            return

        @pl.when(k == 0)
        def _():
            for a in accs:
                a[...] = jnp.zeros_like(a)

        for g, d in dots():
            accs[g][...] += d

        @pl.when(k == nk - 1)
        def _():
            epilogue(i, accs, ex_refs, out_refs)

    res = pl.pallas_call(
        body, name=name, grid=(nm, nn, nk), in_specs=in_specs, out_specs=out_specs,
        out_shape=[o for o, _, _ in outs],
        scratch_shapes=[pltpu.VMEM((tm, tn), F32)] * (n_acc if nk > 1 else 0),
        compiler_params=_cp(3),
    )(*[a for a, _ in ops], *[a for a, _, _ in extras])
    return res


class _Held:
    def __init__(self, value):
        self.value = value

    def __getitem__(self, idx):
        return self.value[idx]


def _mn(i, j):
    return (i, j)


def _row0(i, j):
    return (0, 0)


def _rows(name, S, ts, ins, outs, body, scratch=()):
    return pl.pallas_call(
        body, name=name, grid=(S // ts,),
        in_specs=[pl.BlockSpec(bs, im) for _, bs, im in ins],
        out_specs=[pl.BlockSpec(bs, im) for _, bs, im in outs],
        out_shape=[o for o, _, _ in outs],
        scratch_shapes=list(scratch),
        compiler_params=_cp(1),
    )(*[a for a, _, _ in ins])


def _halo_ins(arr, S, ts, width, colblk):
    per = ts // HALO
    last = S // HALO - 1
    return [
        (arr, (ts, width), lambda i: (i, colblk)),
        (arr, (HALO, width), lambda i: (jnp.maximum(i * per - 1, 0), colblk)),
        (arr, (HALO, width), lambda i: (jnp.minimum((i + 1) * per, last), colblk)),
    ]


def _ext(main_ref, prev_ref, next_ref, i, n_tiles):
    prev = jnp.where(i > 0, prev_ref[...].astype(F32), 0.0)
    nxt = jnp.where(i < n_tiles - 1, next_ref[...].astype(F32), 0.0)
    return jnp.concatenate([prev, main_ref[...].astype(F32), nxt], axis=0)


def _shift(ext, off, ts):
    n = ext.shape[0]
    if off == 0:
        return ext[HALO:HALO + ts]
    return pltpu.roll(ext, (-off) % n, 0)[HALO:HALO + ts]


def _rmsnorm_fwd(name, x, g):
    S, D = x.shape
    ts = _tile(S, 512)

    def body(x_ref, g_ref, o_ref):
        xv = x_ref[...]
        r = lax.rsqrt(jnp.mean(xv * xv, axis=-1, keepdims=True) + EPS)
        o_ref[...] = (xv * r * g_ref[...]).astype(BF16)

    return _rows(name, S, ts,
                 [(x, (ts, D), lambda i: (i, 0)), (g, (1, D), lambda i: (0, 0))],
                 [(jax.ShapeDtypeStruct((S, D), BF16), (ts, D), lambda i: (i, 0))], body)[0]


def _rmsnorm_bwd_tile(dh, x, g):
    r = lax.rsqrt(jnp.mean(x * x, axis=-1, keepdims=True) + EPS)
    xhat = x * r
    dxn = dh * g
    dx = r * (dxn - xhat * jnp.mean(dxn * xhat, axis=-1, keepdims=True))
    return dx, dh * xhat


def _ffn_fwd(tag, x, h, wg, wu, wd):
    S = x.shape[0]
    tm = _tile(S, 512)
    tn = 1408

    def epi_up(i, accs, ex, out):
        a = accs[0][...]
        b = accs[1][...]
        s = _sigmoid(a)
        sa = a * s
        out[0][...] = sa.astype(BF16)
        out[1][...] = (b * (s * (1.0 + a * (1.0 - s)))).astype(BF16)
        out[2][...] = (sa * b).astype(BF16)

    sds = jax.ShapeDtypeStruct((S, D_FF), BF16)
    a, b, f = _fused_mm(f"{tag}_up", S, D_FF, D_MODEL, tm, tn, D_MODEL,
                        [(h, "mk"), (wg, "kn"), (wu, "kn")], [(0, 1, 0), (0, 2, 1)], [],
                        [(sds, (tm, tn), _mn)] * 3, epi_up)

    def epi_down(i, accs, ex, out):
        out[0][...] = ex[0][...] + 0.5 * accs[0][...]

    if callable(wd):
        wd = wd(f)
    xo = _fused_mm(f"{tag}_down", S, D_MODEL, D_FF, tm, D_MODEL, 1408,
                   [(f, "mk"), (wd, "kn")], [(0, 1, 0)], [(x, (tm, D_MODEL), _mn)],
                   [(jax.ShapeDtypeStruct((S, D_MODEL), F32), (tm, D_MODEL), _mn)], epi_down)[0]
    return xo, a, b, f


def _conv_taps(ext, w_ref, ts):
    acc = None
    for j in range(CONV_W):
        term = w_ref[j:j + 1, :] * _shift(ext, j - 2, ts)
        acc = term if acc is None else acc + term
    return acc


def _l2norm_heads(s, scale):
    outs = []
    for h in range(GDN_H):
        sh = s[:, h * GDN_DK:(h + 1) * GDN_DK]
        outs.append(sh * (lax.rsqrt(jnp.sum(sh * sh, axis=-1, keepdims=True) + EPS) * scale))
    return jnp.concatenate(outs, axis=-1)


def _conv_fwd(name, p, colblk, w, bias, mode):
    S = p.shape[0]
    ts = _tile(S, 512)
    n_tiles = S // ts
    C = w.shape[1]

    def body(main, prev, nxt, w_ref, b_ref, o_ref):
        i = pl.program_id(0)
        c = _conv_taps(_ext(main, prev, nxt, i, n_tiles), w_ref, ts)
        if mode == "bias":
            o_ref[...] = c + b_ref[...]
        else:
            s = c * _sigmoid(c)
            if mode == "q":
                s = _l2norm_heads(s, GDN_DK ** -0.5)
            elif mode == "k":
                s = _l2norm_heads(s, 1.0)
            o_ref[...] = s

    ins = _halo_ins(p, S, ts, C, colblk) + [(w, (CONV_W, C), lambda i: (0, 0)), (bias, (1, C), lambda i: (0, 0))]
    return _rows(name, S, ts, ins, [(jax.ShapeDtypeStruct((S, C), F32), (ts, C), lambda i: (i, 0))], body)[0]


def _rg_gate_terms(pre, xc, prm_ref, d):
    r = _sigmoid(pre[:, d * 1024:d * 1024 + RG_W] + prm_ref[2 * d:2 * d + 1, :])
    ig = _sigmoid(pre[:, d * 1024 + RG_W:(d + 1) * 1024] + prm_ref[2 * d + 1:2 * d + 2, :])
    sp = _softplus(-prm_ref[4 + d:5 + d, :])
    log_a = -RG_C * r * sp
    a = jnp.exp(log_a)
    t = jnp.tanh(log_a)
    sq = jnp.sqrt(-2.0 * t / (1.0 - t))
    return r, ig, sp, a, sq


def _rg_gates_fwd(xc, bd, prm):
    S = xc.shape[0]
    tm = _tile(S, 256)

    def epi(i, accs, ex, out):
        pre = accs[0][...]
        xv = ex[0][...]
        for d in range(2):
            r, ig, sp, a, sq = _rg_gate_terms(pre, xv, ex[1], d)
            out[2 * d][...] = a
            out[2 * d + 1][...] = sq * ig * xv

    sds = jax.ShapeDtypeStruct((S, RG_W), F32)
    blk = (tm, RG_W)
    im = lambda i, j: (i, 0)
    return _fused_mm("rg_gates_fwd", S, 4 * RG_W, RG_W, tm, 4 * RG_W, RG_W,
                     [(xc, "mk"), (bd, "kn")], [(0, 1, 0)],
                     [(xc, blk, im), (prm, (8, RG_W), _row0)], [(sds, blk, im)] * 4, epi)


SUBLANES = 8


def _scan_rows(a, b, reverse):
    rows = lax.broadcasted_iota(jnp.int32, a.shape, 0)
    s = 1
    while s < SUBLANES:
        shift = SUBLANES - s if reverse else s
        a_sh = pltpu.roll(a, shift, 0)
        b_sh = pltpu.roll(b, shift, 0)
        valid = (rows < SUBLANES - s) if reverse else (rows >= s)
        b = jnp.where(valid, a * b_sh + b, b)
        a = jnp.where(valid, a * a_sh, a)
        s *= 2
    return a, b


def _rg_scan(name, a_f, b_f, a_b, b_b):
    S, C = a_f.shape
    ts = _tile(S, 512)
    n_tiles = S // ts

    def body(af, bf, ab, bb, hf, hb, carry):
        @pl.when(pl.program_id(0) == 0)
        def _():
            carry[...] = jnp.zeros_like(carry)

        n_sub = ts // SUBLANES

        def step(j, c):
            cf, cb = c
            r0 = pl.multiple_of(j * SUBLANES, SUBLANES)
            cum_a, h0 = _scan_rows(af[pl.ds(r0, SUBLANES), :], bf[pl.ds(r0, SUBLANES), :], False)
            h = h0 + cum_a * cf
            hf[pl.ds(r0, SUBLANES), :] = h
            cf = h[SUBLANES - 1:SUBLANES, :]
            r1 = pl.multiple_of((n_sub - 1 - j) * SUBLANES, SUBLANES)
            cum_a, h0 = _scan_rows(ab[pl.ds(r1, SUBLANES), :], bb[pl.ds(r1, SUBLANES), :], True)
            h = h0 + cum_a * cb
            hb[pl.ds(r1, SUBLANES), :] = h
            cb = h[0:1, :]
            return cf, cb

        cf, cb = lax.fori_loop(0, n_sub, step, (carry[0:1, :], carry[1:2, :]), unroll=4)
        carry[0:1, :] = cf
        carry[1:2, :] = cb

    fw = lambda i: (i, 0)
    bw = lambda i: (n_tiles - 1 - i, 0)
    sds = jax.ShapeDtypeStruct((S, C), F32)
    return _rows(name, S, ts,
                 [(a_f, (ts, C), fw), (b_f, (ts, C), fw), (a_b, (ts, C), bw), (b_b, (ts, C), bw)],
                 [(sds, (ts, C), fw), (sds, (ts, C), bw)], body, scratch=[pltpu.VMEM((8, C), F32)])


def _tri_masks():
    ri = lax.broadcasted_iota(jnp.int32, (CHUNK, CHUNK), 0)
    ci = lax.broadcasted_iota(jnp.int32, (CHUNK, CHUNK), 1)
    return ri, ci


def _gdn_prep_fwd(p, prm):
    S = p.shape[0]
    ts = _tile(S, 512)

    def body(p_ref, prm_ref, o_ref):
        raw = p_ref[...]
        lane = lax.broadcasted_iota(jnp.int32, (1, 128), 1)
        g = -jnp.exp(prm_ref[0:1, :]) * _softplus(raw + prm_ref[1:2, :])
        g = jnp.where((lane >= 8) & (lane < 16), g, 0.0)
        beta = _sigmoid(raw)
        ri, ci = _tri_masks()
        lower = (ri >= ci).astype(F32)
        upper = (ri <= ci).astype(F32)
        for c in range(ts // CHUNK):
            rows = slice(c * CHUNK, (c + 1) * CHUNK)
            gch = g[rows]
            gc = jnp.where(lane < 12, _dot(lower, gch, 1, 0, HI), _dot(upper, gch, 1, 0, HI))
            o_ref[rows, :] = jnp.where(lane < 8, beta[rows], gc)

    return _rows("gdn_prep_fwd", S, ts,
                 [(p, (ts, 128), lambda i: (i, COL_BA // 128)), (prm, (8, 128), lambda i: (0, 0))],
                 [(jax.ShapeDtypeStruct((S, 128), F32), (ts, 128), lambda i: (i, 0))], body)[0]


def _bdot(a, b, ca, cb):
    return _dot(a.astype(BF16), b.astype(BF16), ca, cb)


GDN_W = GDN_H * GDN_DK
GDN_TS = 256
LOCAL_CHUNKS = 2
BWD_CHUNKS = 1


def _gdn_decay(bg_ref, gcr_ref, c, rows, r0, col, rev, ri, ci):
    beta = bg_ref[rows, col:col + 1]
    gc = bg_ref[rows, 8 + col:9 + col]
    last = 0 if rev else CHUNK - 1
    gl = bg_ref[pl.ds(r0 + last, 1), 8 + col:9 + col]
    out = dict(beta=beta, gc=gc, gl=gl, eg=jnp.exp(gc), egl=jnp.exp(gl - gc), cd=jnp.exp(gl))
    if gcr_ref is not None:
        incl = (ri <= ci) if rev else (ri >= ci)
        out["strict"] = (ri < ci) if rev else (ri > ci)
        out["dm"] = jnp.where(incl, jnp.exp(jnp.where(incl, gc - gcr_ref[c, col:col + 1, :], 0.0)), 0.0)
    return out


def _dir_tile(d, n_tiles, flip):
    if (d == 1) != flip:
        return lambda i: n_tiles - 1 - i
    return lambda i: i


def _gdn_local_fwd(q, k, v, bg, gcr):
    S = q.shape[0]
    ts = _tile(S, GDN_TS)
    ncb = ts // CHUNK
    nch = S // CHUNK

    def body(q_ref, k_ref, v_ref, bg_ref, gcr_ref, *out_refs):
        ri, ci = _tri_masks()
        eye = (ri == ci).astype(F32)
        outs = (out_refs[0:6], out_refs[6:12])
        cd_ref = out_refs[12]

        def chunk(cc, carry):
            chains = []
            for c in (LOCAL_CHUNKS * cc + j for j in range(LOCAL_CHUNKS)):
                r0 = pl.multiple_of(c * CHUNK, CHUNK)
                rows = pl.ds(r0, CHUNK)
                for h in range(GDN_H):
                    cols = slice(h * GDN_DK, (h + 1) * GDN_DK)
                    qh, kh, vh = q_ref[rows, cols], k_ref[rows, cols], v_ref[rows, cols]
                    both = _bdot(jnp.concatenate([qh, kh], axis=0), kh, 1, 1)
                    for d in range(2):
                        chains.append(dict(c=c, r0=r0, rows=rows, h=h, d=d, cols=cols, qh=qh, kh=kh, vh=vh,
                                           qk=both[0:CHUNK], kk=both[CHUNK:2 * CHUNK]))
            for ch in chains:
                m = _gdn_decay(bg_ref, gcr_ref, ch["c"], ch["rows"], ch["r0"], ch["d"] * GDN_H + ch["h"], ch["d"] == 1,
                               ri, ci)
                ch["m"] = m
                ch["x"] = -jnp.where(m["strict"], m["beta"] * ch["kk"] * m["dm"], 0.0)
                ch["t"] = eye + ch["x"]
            for ch in chains:
                ch["pw"] = _bdot(ch["x"], ch["x"], 1, 0)
            for level in range(1, 6):
                last_level = level == 5
                for ch in chains:
                    rhs = ch["t"] if last_level else jnp.concatenate([ch["t"], ch["pw"]], axis=1)
                    ch["prod"] = _bdot(ch["pw"], rhs, 1, 0)
                for ch in chains:
                    ch["t"] = ch["t"] + ch["prod"][:, 0:CHUNK]
                    if not last_level:
                        ch["pw"] = ch["prod"][:, CHUNK:2 * CHUNK]
            for ch in chains:
                m = ch["m"]
                rhs = jnp.concatenate([ch["vh"] * m["beta"], ch["kh"] * (m["beta"] * m["eg"])], axis=1)
                ch["uw"] = _bdot(ch["t"], rhs, 1, 0)
            for ch in chains:
                u_ref, w_ref, a_ref, t_ref, qd_ref, kd_ref = outs[ch["d"]]
                m = ch["m"]
                c, rows = ch["c"], ch["rows"]
                col = ch["d"] * GDN_H + ch["h"]
                u_ref[rows, ch["cols"]] = ch["uw"][:, 0:GDN_DK]
                w_ref[rows, ch["cols"]] = ch["uw"][:, GDN_DK:2 * GDN_DK].astype(BF16)
                a_ref[c, ch["h"]] = (ch["qk"] * m["dm"]).astype(BF16)
                t_ref[c, ch["h"]] = _bdot(ch["t"], eye, 0, 0).astype(BF16)
                qd_ref[rows, ch["cols"]] = (ch["qh"] * m["eg"]).astype(BF16)
                kd_ref[rows, ch["cols"]] = (ch["kh"] * m["egl"]).astype(BF16)
                cd_ref[c, col:col + 1, :] = jnp.broadcast_to(m["cd"], (1, 128))
            return carry

        lax.fori_loop(0, ncb // LOCAL_CHUNKS, chunk, 0)

    im = lambda i: (i, 0)
    im4 = lambda i: (i, 0, 0, 0)
    ins = [(q, (ts, GDN_W), im), (k, (ts, GDN_W), im), (v, (ts, GDN_W), im), (bg, (ts, 128), im),
           (gcr, (ncb, 8, CHUNK), lambda i: (i, 0, 0))]
    per_dir = [(jax.ShapeDtypeStruct((S, GDN_W), F32), (ts, GDN_W), im),
               (jax.ShapeDtypeStruct((S, GDN_W), BF16), (ts, GDN_W), im),
               (jax.ShapeDtypeStruct((nch, GDN_H, CHUNK, CHUNK), BF16), (ncb, GDN_H, CHUNK, CHUNK), im4),
               (jax.ShapeDtypeStruct((nch, GDN_H, CHUNK, CHUNK), BF16), (ncb, GDN_H, CHUNK, CHUNK), im4),
               (jax.ShapeDtypeStruct((S, GDN_W), BF16), (ts, GDN_W), im),
               (jax.ShapeDtypeStruct((S, GDN_W), BF16), (ts, GDN_W), im)]
    cd_out = (jax.ShapeDtypeStruct((nch, 8, 128), F32), (ncb, 8, 128), lambda i: (i, 0, 0))
    res = _rows("gdn_local_fwd", S, ts, ins, per_dir * 2 + [cd_out], body)
    return res[0:6], res[6:12], res[12]


def _gdn_scan_fwd(loc):
    S = loc[0][0].shape[0]
    ts = _tile(S, GDN_TS)
    n_tiles = S // ts
    ncb = ts // CHUNK
    nch = S // CHUNK

    def body(*refs):
        ins = (refs[0:6], refs[6:12])
        outs = (refs[12:15], refs[15:18])
        state = refs[18]

        @pl.when(pl.program_id(0) == 0)
        def _():
            state[...] = jnp.zeros_like(state)

        def chunk(cc, carry):
            chains = []
            for d in range(2):
                c = cc if d == 0 else ncb - 1 - cc
                rows = pl.ds(pl.multiple_of(c * CHUNK, CHUNK), CHUNK)
                for h in range(GDN_H):
                    cols = slice(h * GDN_DK, (h + 1) * GDN_DK)
                    chains.append(dict(d=d, h=h, c=c, rows=rows, cols=cols, st=state[d * GDN_H + h]))
            for ch in chains:
                qd_ref, kd_ref, u_ref, w_ref, a_ref, cd_ref = ins[ch["d"]]
                rows, cols = ch["rows"], ch["cols"]
                lhs = jnp.concatenate([w_ref[rows, cols], qd_ref[rows, cols]], axis=0)
                ch["ws_qs"] = _dot(lhs, ch["st"].astype(BF16), 1, 0)
            for ch in chains:
                qd_ref, kd_ref, u_ref, w_ref, a_ref, cd_ref = ins[ch["d"]]
                rows, cols = ch["rows"], ch["cols"]
                vn = u_ref[rows, cols] - ch["ws_qs"][0:CHUNK]
                vnb = vn.astype(BF16)
                ch["vn"] = vn
                ch["avn"] = _dot(a_ref[ch["c"], ch["h"]], vnb, 1, 0)
                ch["kvn"] = _dot(kd_ref[rows, cols], vnb, 0, 0)
            for ch in chains:
                o_ref, vn_ref, s_ref = outs[ch["d"]]
                cd_ref = ins[ch["d"]][5]
                rows, cols = ch["rows"], ch["cols"]
                col = ch["d"] * GDN_H + ch["h"]
                o_ref[rows, cols] = ch["ws_qs"][CHUNK:2 * CHUNK] + ch["avn"]
                vn_ref[rows, cols] = ch["vn"].astype(BF16)
                s_ref[ch["c"], ch["h"]] = ch["st"].astype(BF16)
                state[ch["d"] * GDN_H + ch["h"]] = ch["st"] * cd_ref[ch["c"], col:col + 1, :] + ch["kvn"]
            return carry

        lax.fori_loop(0, ncb, chunk, 0)

    ins, outs = [], []
    for d in range(2):
        tix = _dir_tile(d, n_tiles, False)
        im = lambda i, tix=tix: (tix(i), 0)
        im4 = lambda i, tix=tix: (tix(i), 0, 0, 0)
        u, w, a, _, qd, kd = loc[d]
        ins += [(qd, (ts, GDN_W), im), (kd, (ts, GDN_W), im), (u, (ts, GDN_W), im), (w, (ts, GDN_W), im),
                (a, (ncb, GDN_H, CHUNK, CHUNK), im4), (loc[2], (ncb, 8, 128), lambda i, tix=tix: (tix(i), 0, 0))]
        outs += [(jax.ShapeDtypeStruct((S, GDN_W), F32), (ts, GDN_W), im),
                 (jax.ShapeDtypeStruct((S, GDN_W), BF16), (ts, GDN_W), im),
                 (jax.ShapeDtypeStruct((nch, GDN_H, GDN_DK, GDN_DK), BF16), (ncb, GDN_H, GDN_DK, GDN_DK), im4)]
    res = _rows("gdn_scan_fwd", S, ts, ins, outs, body, scratch=[pltpu.VMEM((2 * GDN_H, GDN_DK, GDN_DK), F32)])
    return res[0:3], res[3:6]


def _gelu(x):
    c = math.sqrt(2.0 / math.pi)
    t = jnp.tanh(c * (x + 0.044715 * x * x * x))
    return 0.5 * x * (1.0 + t), t


def _mix_out_fwd(h_f, h_b, o_f, o_b, p, gn):
    S = h_f.shape[0]
    ts = _tile(S, 512)

    def body(hf, hb, of, ob, gate, z, gn_ref, y_ref):
        ge, _ = _gelu(gate[...])
        y_ref[:, 0:RG_W] = ((hf[...] + hb[...]) * ge).astype(BF16)
        o = of[...] + ob[...]
        zv = z[...]
        sz = zv * _sigmoid(zv)
        for h in range(GDN_H):
            cols = slice(h * GDN_DK, (h + 1) * GDN_DK)
            oh = o[:, cols]
            n = oh * lax.rsqrt(jnp.mean(oh * oh, axis=-1, keepdims=True) + EPS) * gn_ref[...]
            y_ref[:, RG_W + h * GDN_DK:RG_W + (h + 1) * GDN_DK] = (n * sz[:, cols]).astype(BF16)

    blk = (ts, RG_W)
    im = lambda i: (i, 0)
    ins = [(h_f, blk, im), (h_b, blk, im), (o_f, blk, im), (o_b, blk, im),
           (p, blk, lambda i: (i, 1)), (p, blk, lambda i: (i, 5)), (gn, (1, GDN_DK), lambda i: (0, 0))]
    return _rows("mix_out_fwd", S, ts, ins,
                 [(jax.ShapeDtypeStruct((S, D_MODEL), BF16), (ts, D_MODEL), im)], body)[0]


def _loss_head(x, target, g):
    S, D = x.shape
    ts = _tile(S, 512)

    def body(x_ref, t_ref, g_ref, dx_ref, loss_ref, dg_ref):
        @pl.when(pl.program_id(0) == 0)
        def _():
            loss_ref[...] = jnp.zeros_like(loss_ref)
            dg_ref[...] = jnp.zeros_like(dg_ref)

        xv = x_ref[...]
        gv = g_ref[...]
        r = lax.rsqrt(jnp.mean(xv * xv, axis=-1, keepdims=True) + EPS)
        err = xv * r * gv - t_ref[...]
        loss_ref[...] += jnp.sum(err * err) * (0.5 / D)
        dx, dgt = _rmsnorm_bwd_tile(err * (1.0 / D), xv, gv)
        dx_ref[...] = dx
        dg_ref[...] += jnp.sum(dgt, axis=0, keepdims=True)

    im = lambda i: (i, 0)
    z = lambda i: (0, 0)
    return _rows("loss_head", S, ts,
                 [(x, (ts, D), im), (target, (ts, D), im), (g, (1, D), z)],
                 [(jax.ShapeDtypeStruct((S, D), F32), (ts, D), im),
                  (jax.ShapeDtypeStruct((8, 128), F32), (8, 128), z),
                  (jax.ShapeDtypeStruct((1, D), F32), (1, D), z)], body)


def _block_diag(w):
    n = w.shape[0]
    return jnp.einsum("nij,nm->nimj", w, jnp.eye(n, dtype=w.dtype)).reshape(n * w.shape[1], n * w.shape[2])


def _rg_bd(a_w, x_w):
    return jnp.concatenate([_block_diag(a_w[0]), _block_diag(x_w[0]), _block_diag(a_w[1]), _block_diag(x_w[1])],
                           axis=1).astype(BF16)


def _rg_prm(ba, bx, lam):
    return jnp.concatenate([ba[0:1], bx[0:1], ba[1:2], bx[1:2], lam, jnp.zeros((2, RG_W), F32)], axis=0)


def _gdn_prm(a_log, dt_bias):
    rows = jnp.zeros((8, 128), F32)
    rows = rows.at[0, 8:16].set(a_log.reshape(-1))
    return rows.at[1, 8:16].set(dt_bias.reshape(-1))


def _gc_rows(bg):
    S = bg.shape[0]
    return bg[:, 8:16].reshape(S // CHUNK, CHUNK, 8).transpose(0, 2, 1)


def _layer_fwd(x0, target, W, more=None):
    S = x0.shape[0]
    R = {}
    R["h1"] = _rmsnorm_fwd("rms1", x0, W["ffn1_norm"])
    late_wd = {}

    def ffn1_wd(after):
        late_wd.update(more("ffn1_down", after))
        return late_wd["ffn1_w_down"]

    R["x1"], R["a1"], R["b1"], R["f1"] = _ffn_fwd("ffn1", x0, R["h1"], W["ffn1_w_gate"], W["ffn1_w_up"],
                                                  ffn1_wd if more is not None else W["ffn1_w_down"])
    if more is not None:
        W = {**W, **late_wd, **more("mixer", R["x1"])}
    R["h2"] = _rmsnorm_fwd("rms2", R["x1"], W["mix_norm"])
    tm = _tile(S, 512)
    tmp = _tile(S, 1024)
    R["p"] = _fused_mm("in_proj", S, D_IN_PAD, D_MODEL, tmp, 640, D_MODEL, [(R["h2"], "mk"), (W["w_in"], "kn")],
                       [(0, 1, 0)], [], [(jax.ShapeDtypeStruct((S, D_IN_PAD), F32), (tmp, 640), _mn)],
                       lambda i, accs, ex, out: out[0].__setitem__(Ellipsis, accs[0][...]))[0]
    p = R["p"]
    R["xc"] = _conv_fwd("rg_conv_fwd", p, 0, W["rg_conv_w"], W["rg_conv_b"], "bias")
    R["bd"] = _rg_bd(W["rg_gate_a_w"], W["rg_gate_x_w"])
    R["rg_prm"] = _rg_prm(W["rg_gate_a_b"], W["rg_gate_x_b"], W["rg_lambda"])
    a_f, b_f, a_b, b_b = _rg_gates_fwd(R["xc"], R["bd"], R["rg_prm"])
    R["a_f"], R["a_b"] = a_f, a_b
    R["h_f"], R["h_b"] = _rg_scan("rg_scan_fwd", a_f, b_f, a_b, b_b)
    zero_b = jnp.zeros((1, RG_W), F32)
    cw = W["gdn_conv_w"]
    R["q"] = _conv_fwd("gdn_conv_q", p, 2, cw[:, 0:512], zero_b, "q")
    R["k"] = _conv_fwd("gdn_conv_k", p, 3, cw[:, 512:1024], zero_b, "k")
    R["v"] = _conv_fwd("gdn_conv_v", p, 4, cw[:, 1024:1536], zero_b, "v")
    R["gdn_prm"] = _gdn_prm(W["gdn_a_log"], W["gdn_dt_bias"])
    R["bg"] = _gdn_prep_fwd(p, R["gdn_prm"])
    R["gcr"] = _gc_rows(R["bg"])
    R["gdn_loc"] = _gdn_local_fwd(R["q"], R["k"], R["v"], R["bg"], R["gcr"])
    R["gdn_fwd"] = _gdn_scan_fwd(R["gdn_loc"])
    R["o_f"], R["o_b"] = R["gdn_fwd"][0][0], R["gdn_fwd"][1][0]
    R["y"] = _mix_out_fwd(R["h_f"], R["h_b"], R["o_f"], R["o_b"], p, W["gdn_norm"])
    R["x2"] = _fused_mm("out_proj", S, D_MODEL, D_MODEL, tm, D_MODEL, D_MODEL, [(R["y"], "mk"), (W["w_out"], "kn")],
                        [(0, 1, 0)], [(R["x1"], (tm, D_MODEL), _mn)],
                        [(jax.ShapeDtypeStruct((S, D_MODEL), F32), (tm, D_MODEL), _mn)],
                        lambda i, accs, ex, out: out[0].__setitem__(Ellipsis, ex[0][...] + accs[0][...]))[0]
    if more is not None:
        W = {**W, **more("ffn2", R["x2"])}
    R["h3"] = _rmsnorm_fwd("rms3", R["x2"], W["ffn2_norm"])
    R["x3"], R["a2"], R["b2"], R["f2"] = _ffn_fwd("ffn2", R["x2"], R["h3"], W["ffn2_w_gate"], W["ffn2_w_up"], W["ffn2_w_down"])
    R["dx3"], R["loss"], R["d_final_norm"] = _loss_head(R["x3"], target, W["final_norm"])
    R["W"] = W
    return R


def _colsum_into(ref, i, val):
    @pl.when(i == 0)
    def _():
        ref[...] = val

    @pl.when(i > 0)
    def _():
        ref[...] += val


def _ffn_bwd(tag, dout, x, g, h, a, b, f, wg, wu, wd, emit):
    S = x.shape[0]
    tm = _tile(S, 512)
    tk_s = _tile(S, 1024)
    dwd = _fused_mm(f"{tag}_dw_down", D_FF, D_MODEL, S, 1408, D_MODEL, tk_s, [(f, "km"), (dout, "kn")], [(0, 1, 0)], [],
                    [(jax.ShapeDtypeStruct((D_FF, D_MODEL), BF16), (1408, D_MODEL), _mn)],
                    lambda i, accs, ex, out: out[0].__setitem__(Ellipsis, (0.5 * accs[0][...]).astype(BF16)))[0]
    emit(down=dwd)

    def epi_act(i, accs, ex, out):
        df = 0.5 * accs[0][...]
        out[0][...] = (df * ex[1][...].astype(F32)).astype(BF16)
        out[1][...] = (df * ex[0][...].astype(F32)).astype(BF16)

    sds = jax.ShapeDtypeStruct((S, D_FF), BF16)
    da, db = _fused_mm(f"{tag}_dact", S, D_FF, D_MODEL, tm, 1408, D_MODEL, [(dout, "mk"), (wd, "nk")], [(0, 1, 0)],
                       [(a, (tm, 1408), _mn), (b, (tm, 1408), _mn)], [(sds, (tm, 1408), _mn)] * 2, epi_act)

    def epi_w2(i, accs, ex, out):
        out[0][...] = accs[0][...].astype(BF16)
        out[1][...] = accs[1][...].astype(BF16)

    sdw = jax.ShapeDtypeStruct((D_MODEL, D_FF), BF16)
    dwg, dwu = _fused_mm(f"{tag}_dw_up", D_MODEL, D_FF, S, D_MODEL, 1408, _tile(S, 512),
                         [(h, "km"), (da, "kn"), (db, "kn")], [(0, 1, 0), (0, 2, 1)], [],
                         [(sdw, (D_MODEL, 1408), _mn)] * 2, epi_w2)
    tok = emit(gate=dwg, up=dwu)
    if tok is not None:
        g = g + tok

    def epi_dx(i, accs, ex, out):
        dx, dgt = _rmsnorm_bwd_tile(accs[0][...], ex[0][...], ex[1][...])
        out[0][...] = ex[2][...] + dx
        _colsum_into(out[1], i, jnp.sum(dgt, axis=0, keepdims=True))

    dx, dg = _fused_mm(f"{tag}_dx", S, D_MODEL, D_FF, tm, D_MODEL, 1408,
                       [(da, "mk"), (wg, "nk"), (db, "mk"), (wu, "nk")], [(0, 1, 0), (2, 3, 0)],
                       [(x, (tm, D_MODEL), _mn), (g, (1, D_MODEL), _row0), (dout, (tm, D_MODEL), _mn)],
                       [(jax.ShapeDtypeStruct((S, D_MODEL), F32), (tm, D_MODEL), _mn),
                        (jax.ShapeDtypeStruct((1, D_MODEL), F32), (1, D_MODEL), _row0)], epi_dx)
    return dx, dg


def _mix_out_bwd(dy, h_f, h_b, o_f, o_b, p, gn):
    S = dy.shape[0]
    ts = _tile(S, 512)
    c0 = math.sqrt(2.0 / math.pi)

    def body(dy_ref, hf, hb, of, ob, gate, z, gn_ref, dhr_ref, dgate_ref, do_ref, dz_ref, dgn_ref):
        i = pl.program_id(0)
        gv = gate[...]
        ge, t = _gelu(gv)
        dy_rg = dy_ref[:, 0:RG_W]
        dhr_ref[...] = dy_rg * ge
        dgelu = 0.5 * (1.0 + t) + 0.5 * gv * (1.0 - t * t) * c0 * (1.0 + 3.0 * 0.044715 * gv * gv)
        dgate_ref[...] = (dy_rg * (hf[...] + hb[...]) * dgelu).astype(BF16)
        o = of[...] + ob[...]
        zv = z[...]
        sig = _sigmoid(zv)
        gnv = gn_ref[...]
        dgn = jnp.zeros((1, GDN_DK), F32)
        for h in range(GDN_H):
            cols = slice(h * GDN_DK, (h + 1) * GDN_DK)
            oh = o[:, cols]
            r = lax.rsqrt(jnp.mean(oh * oh, axis=-1, keepdims=True) + EPS)
            ohat = oh * r
            dyh = dy_ref[:, RG_W + h * GDN_DK:RG_W + (h + 1) * GDN_DK]
            zh = zv[:, cols]
            sh = sig[:, cols]
            dn = dyh * zh * sh
            dz_ref[:, cols] = (dyh * ohat * gnv * (sh * (1.0 + zh * (1.0 - sh)))).astype(BF16)
            dxn = dn * gnv
            do_ref[:, cols] = r * (dxn - ohat * jnp.mean(dxn * ohat, axis=-1, keepdims=True))
            dgn = dgn + jnp.sum(dn * ohat, axis=0, keepdims=True)
        _colsum_into(dgn_ref, i, dgn)

    blk = (ts, RG_W)
    im = lambda i: (i, 0)
    z0 = lambda i: (0, 0)
    ins = [(dy, (ts, D_MODEL), im), (h_f, blk, im), (h_b, blk, im), (o_f, blk, im), (o_b, blk, im),
           (p, blk, lambda i: (i, 1)), (p, blk, lambda i: (i, 5)), (gn, (1, GDN_DK), z0)]
    outs = [(jax.ShapeDtypeStruct((S, RG_W), F32), blk, im), (jax.ShapeDtypeStruct((S, RG_W), BF16), blk, im),
            (jax.ShapeDtypeStruct((S, RG_W), F32), blk, im), (jax.ShapeDtypeStruct((S, RG_W), BF16), blk, im),
            (jax.ShapeDtypeStruct((1, GDN_DK), F32), (1, GDN_DK), z0)]
    return _rows("mix_out_bwd", S, ts, ins, outs, body)


def _rg_scan_adj(name, a_up, b_up, a_dn, b_dn):
    S, C = a_up.shape
    ts = _tile(S, 512)
    n_tiles = S // ts

    def body(au, bu, ad, bd, mu_ref, lam_ref, carry):
        @pl.when(pl.program_id(0) == 0)
        def _():
            carry[...] = jnp.zeros_like(carry)

        n_sub = ts // SUBLANES
        rows = lax.broadcasted_iota(jnp.int32, (SUBLANES, C), 0)

        def half(a_ref, b_ref, out_ref, r0, c_in, reverse):
            a = a_ref[pl.ds(r0, SUBLANES), :]
            b = b_ref[pl.ds(r0, SUBLANES), :]
            cum_a, c0 = _scan_rows(a, a * b, reverse)
            c = c0 + cum_a * c_in
            edge = 0 if not reverse else SUBLANES - 1
            c_prev = jnp.where(rows == edge, c_in, pltpu.roll(c, SUBLANES - 1 if reverse else 1, 0))
            out_ref[pl.ds(r0, SUBLANES), :] = b + c_prev
            return c[0:1, :] if reverse else c[SUBLANES - 1:SUBLANES, :]

        def step(j, c):
            cu, cd = c
            cu = half(au, bu, mu_ref, pl.multiple_of(j * SUBLANES, SUBLANES), cu, False)
            cd = half(ad, bd, lam_ref, pl.multiple_of((n_sub - 1 - j) * SUBLANES, SUBLANES), cd, True)
            return cu, cd

        cu, cd = lax.fori_loop(0, n_sub, step, (carry[0:1, :], carry[1:2, :]), unroll=4)
        carry[0:1, :] = cu
        carry[1:2, :] = cd

    fw = lambda i: (i, 0)
    bw = lambda i: (n_tiles - 1 - i, 0)
    sds = jax.ShapeDtypeStruct((S, C), F32)
    return _rows(name, S, ts,
                 [(a_up, (ts, C), fw), (b_up, (ts, C), fw), (a_dn, (ts, C), bw), (b_dn, (ts, C), bw)],
                 [(sds, (ts, C), fw), (sds, (ts, C), bw)], body, scratch=[pltpu.VMEM((8, C), F32)])


def _halo_ex(arr, S, tm, width):
    per = tm // HALO
    last = S // HALO - 1
    return [
        (arr, (tm, width), lambda i, j: (i, 0)),
        (arr, (HALO, width), lambda i, j: (jnp.maximum(i * per - 1, 0), 0)),
        (arr, (HALO, width), lambda i, j: (jnp.minimum((i + 1) * per, last), 0)),
    ]


def _rg_gates_bwd(xc, bd, prm, lam_f, lam_b, h_f, h_b):
    S = xc.shape[0]
    tm = _tile(S, 256)
    n_tiles = S // tm

    def epi(i, accs, ex, out):
        pre = accs[0][...]
        xv = ex[0][...]
        prm_ref = ex[1]
        lams = (ex[2][...], ex[3][...])
        hprev = (_shift(_ext(ex[4], ex[5], ex[6], i, n_tiles), -1, tm),
                 _shift(_ext(ex[7], ex[8], ex[9], i, n_tiles), 1, tm))
        dxc = jnp.zeros_like(xv)
        rows = []
        dlam_rows = []
        for d in range(2):
            r, ig, sp, a, sq = _rg_gate_terms(pre, xv, prm_ref, d)
            lam = lams[d]
            da = lam * hprev[d]
            di = lam * sq * xv
            dxc = dxc + lam * sq * ig
            dsq = lam * ig * xv
            dlog_a = da * a - dsq * (a * a) / sq
            dpre_r = dlog_a * (-RG_C * sp) * r * (1.0 - r)
            dpre_i = di * ig * (1.0 - ig)
            out[0][:, d * 1024:d * 1024 + RG_W] = dpre_r.astype(BF16)
            out[0][:, d * 1024 + RG_W:(d + 1) * 1024] = dpre_i.astype(BF16)
            rows += [jnp.sum(dpre_r, axis=0, keepdims=True), jnp.sum(dpre_i, axis=0, keepdims=True)]
            dsp = jnp.sum(dlog_a * (-RG_C * r), axis=0, keepdims=True)
            dlam_rows.append(-dsp * _sigmoid(-prm_ref[4 + d:5 + d, :]))
        out[1][...] = dxc
        zero = jnp.zeros((2, RG_W), F32)
        _colsum_into(out[2], i, jnp.concatenate(rows + dlam_rows + [zero], axis=0))

    blk = (tm, RG_W)
    im = lambda i, j: (i, 0)
    extras = ([(xc, blk, im), (prm, (8, RG_W), _row0), (lam_f, blk, im), (lam_b, blk, im)]
              + _halo_ex(h_f, S, tm, RG_W) + _halo_ex(h_b, S, tm, RG_W))
    outs = [(jax.ShapeDtypeStruct((S, 4 * RG_W), BF16), (tm, 4 * RG_W), im),
            (jax.ShapeDtypeStruct((S, RG_W), F32), blk, im),
            (jax.ShapeDtypeStruct((8, RG_W), F32), (8, RG_W), _row0)]
    return _fused_mm("rg_gates_bwd", S, 4 * RG_W, RG_W, tm, 4 * RG_W, RG_W, [(xc, "mk"), (bd, "kn")], [(0, 1, 0)],
                     extras, outs, epi)


def _roll_rows(ext, off):
    if off == 0:
        return ext
    return pltpu.roll(ext, (-off) % ext.shape[0], 0)


def _conv_bwd(name, p, colblk, w, grads, mode):
    S = p.shape[0]
    ts = _tile(S, 512)
    n_tiles = S // ts
    C = w.shape[1]
    ng = len(grads)

    def body(*refs):
        p_refs = refs[0:3]
        g_refs = refs[3:3 + 3 * ng]
        w_ref = refs[3 + 3 * ng]
        dx_ref, dw_ref, db_ref = refs[4 + 3 * ng:]
        i = pl.program_id(0)
        ext_p = _ext(*p_refs, i, n_tiles)
        dn = _ext(*g_refs[0:3], i, n_tiles)
        for gi in range(1, ng):
            dn = dn + _ext(*g_refs[3 * gi:3 * gi + 3], i, n_tiles)
        if mode == "bias":
            dc = dn
        else:
            c = None
            for j in range(CONV_W):
                term = w_ref[j:j + 1, :] * _roll_rows(ext_p, j - 2)
                c = term if c is None else c + term
            sig = _sigmoid(c)
            s = c * sig
            if mode in ("q", "k"):
                scale = GDN_DK ** -0.5 if mode == "q" else 1.0
                parts = []
                for h in range(GDN_H):
                    cols = slice(h * GDN_DK, (h + 1) * GDN_DK)
                    sh = s[:, cols]
                    dnh = dn[:, cols]
                    rinv = lax.rsqrt(jnp.sum(sh * sh, axis=-1, keepdims=True) + EPS)
                    parts.append(scale * rinv * (dnh - sh * (rinv * rinv) * jnp.sum(dnh * sh, axis=-1, keepdims=True)))
                ds = jnp.concatenate(parts, axis=-1)
            else:
                ds = dn
            dc = ds * (sig * (1.0 + c * (1.0 - sig)))
        dx = None
        for j in range(CONV_W):
            term = w_ref[j:j + 1, :] * _shift(dc, 2 - j, ts)
            dx = term if dx is None else dx + term
        dx_ref[...] = dx.astype(BF16)
        dc_main = dc[HALO:HALO + ts]
        dw = jnp.concatenate([jnp.sum(dc_main * _shift(ext_p, j - 2, ts), axis=0, keepdims=True)
                              for j in range(CONV_W)], axis=0)
        _colsum_into(dw_ref, i, dw)
        _colsum_into(db_ref, i, jnp.sum(dc_main, axis=0, keepdims=True))

    ins = _halo_ins(p, S, ts, C, colblk)
    for garr in grads:
        ins += _halo_ins(garr, S, ts, C, 0)
    ins += [(w, (CONV_W, C), lambda i: (0, 0))]
    z0 = lambda i: (0, 0)
    outs = [(jax.ShapeDtypeStruct((S, C), BF16), (ts, C), lambda i: (i, 0)),
            (jax.ShapeDtypeStruct((CONV_W, C), F32), (CONV_W, C), z0),
            (jax.ShapeDtypeStruct((1, C), F32), (1, C), z0)]
    return _rows(name, S, ts, ins, outs, body)


def _gdn_scan_bwd(loc, do):
    S = do.shape[0]
    ts = _tile(S, GDN_TS)
    n_tiles = S // ts
    ncb = ts // CHUNK
    nch = S // CHUNK

    def body(*refs):
        ins = (refs[0:6], refs[6:12])
        outs = (refs[12:14], refs[14:16])
        dstate = refs[16]

        @pl.when(pl.program_id(0) == 0)
        def _():
            dstate[...] = jnp.zeros_like(dstate)

        def chunk(cc, carry):
            chains = []
            for d in range(2):
                c = ncb - 1 - cc if d == 0 else cc
                rows = pl.ds(pl.multiple_of(c * CHUNK, CHUNK), CHUNK)
                for h in range(GDN_H):
                    cols = slice(h * GDN_DK, (h + 1) * GDN_DK)
                    chains.append(dict(d=d, h=h, c=c, rows=rows, cols=cols, dsn=dstate[d * GDN_H + h]))
            for ch in chains:
                qd_ref, kd_ref, cd_ref, w_ref, a_ref, do_ref = ins[ch["d"]]
                rows, cols = ch["rows"], ch["cols"]
                dob = do_ref[rows, cols].astype(BF16)
                ch["dvn"] = (_dot(a_ref[ch["c"], ch["h"]], dob, 0, 0)
                             + _dot(kd_ref[rows, cols], ch["dsn"].astype(BF16), 1, 0))
                ch["qdo"] = _dot(qd_ref[rows, cols], dob, 0, 0)
            for ch in chains:
                w_ref = ins[ch["d"]][3]
                ch["wdvn"] = _dot(w_ref[ch["rows"], ch["cols"]], ch["dvn"].astype(BF16), 0, 0)
            for ch in chains:
                dvn_ref, ds_ref = outs[ch["d"]]
                cd_ref = ins[ch["d"]][2]
                col = ch["d"] * GDN_H + ch["h"]
                dvn_ref[ch["rows"], ch["cols"]] = ch["dvn"].astype(BF16)
                ds_ref[ch["c"], ch["h"]] = ch["dsn"].astype(BF16)
                dstate[ch["d"] * GDN_H + ch["h"]] = (ch["qdo"] + cd_ref[ch["c"], col:col + 1, :] * ch["dsn"]
                                                     - ch["wdvn"])
            return carry

        lax.fori_loop(0, ncb, chunk, 0)

    ins, outs = [], []
    for d in range(2):
        tix = _dir_tile(d, n_tiles, True)
        im = lambda i, tix=tix: (tix(i), 0)
        im4 = lambda i, tix=tix: (tix(i), 0, 0, 0)
        _, w, a, _, qd, kd = loc[d]
        ins += [(qd, (ts, GDN_W), im), (kd, (ts, GDN_W), im), (loc[2], (ncb, 8, 128), lambda i, tix=tix: (tix(i), 0, 0)),
                (w, (ts, GDN_W), im), (a, (ncb, GDN_H, CHUNK, CHUNK), im4), (do, (ts, GDN_W), im)]
        outs += [(jax.ShapeDtypeStruct((S, GDN_W), BF16), (ts, GDN_W), im),
                 (jax.ShapeDtypeStruct((nch, GDN_H, GDN_DK, GDN_DK), BF16), (ncb, GDN_H, GDN_DK, GDN_DK), im4)]
    res = _rows("gdn_scan_bwd", S, ts, ins, outs, body, scratch=[pltpu.VMEM((2 * GDN_H, GDN_DK, GDN_DK), F32)])
    return res[0:2], res[2:4]


def _gdn_local_bwd(q, k, v, bg, gcr, do, loc, fwd, adj):
    S = q.shape[0]
    ts = _tile(S, GDN_TS)
    ncb = ts // CHUNK

    def body(q_ref, k_ref, v_ref, bg_ref, gcr_ref, do_ref, *rest):
        per_dir = (rest[0:5], rest[5:10])
        dq_ref, dk_ref, dv_ref, dbg_ref, dbgr_ref = rest[10:15]
        ri, ci = _tri_masks()
        lane = lax.broadcasted_iota(jnp.int32, (CHUNK, 128), 1)
        rowi = lax.broadcasted_iota(jnp.int32, (CHUNK, 1), 0)
        ones8 = jnp.ones((SUBLANES, CHUNK), F32)

        def chunk(cc, carry):
            chains = []
            for c in (BWD_CHUNKS * cc + j for j in range(BWD_CHUNKS)):
                r0 = pl.multiple_of(c * CHUNK, CHUNK)
                rows = pl.ds(r0, CHUNK)
                for h in range(GDN_H):
                    cols = slice(h * GDN_DK, (h + 1) * GDN_DK)
                    qh, kh, vh = q_ref[rows, cols], k_ref[rows, cols], v_ref[rows, cols]
                    dob = do_ref[rows, cols].astype(BF16)
                    both = _bdot(jnp.concatenate([qh, kh], axis=0), kh, 1, 1)
                    for d in range(2):
                        chains.append(dict(c=c, r0=r0, rows=rows, h=h, d=d, cols=cols, qh=qh, kh=kh, vh=vh, dob=dob,
                                           qk=both[0:CHUNK], kk=both[CHUNK:2 * CHUNK], col=d * GDN_H + h))
            for ch in chains:
                c, rows = ch["c"], ch["rows"]
                m = _gdn_decay(bg_ref, gcr_ref, c, rows, ch["r0"], ch["col"], ch["d"] == 1, ri, ci)
                t_ref, s_ref, ds_ref, vn_ref, dvn_ref = per_dir[ch["d"]]
                h, cols = ch["h"], ch["cols"]
                ch["m"] = m
                ch["kb"] = ch["kh"] * m["beta"]
                ch["kbg"] = ch["kb"] * m["eg"]
                ch["t"] = t_ref[c, h]
                stb = s_ref[c, h]
                ch["dsn"] = ds_ref[c, h]
                vnb = vn_ref[rows, cols]
                dvnb = dvn_ref[rows, cols]
                ch["dcd"] = jnp.sum(jnp.sum(stb.astype(F32) * ch["dsn"].astype(F32), axis=1, keepdims=True),
                                    axis=0, keepdims=True)
                ch["dqd"] = _dot(ch["dob"], stb, 1, 1)
                ch["d_a"] = _dot(ch["dob"], vnb, 1, 1)
                ch["dkd"] = _bdot(vnb, ch["dsn"], 1, 1)
                ch["dw"] = -_dot(dvnb, stb, 1, 1)
                ch["dvb"] = _dot(ch["t"], dvnb, 1, 0)
                ch["d_t"] = _bdot(dvnb, ch["vh"] * m["beta"], 1, 1)
            for ch in chains:
                dwb = ch["dw"].astype(BF16)
                ch["d_t"] = ch["d_t"] + _bdot(dwb, ch["kbg"], 1, 1)
                ch["dkbg"] = _dot(ch["t"], dwb, 1, 0)
                ch["nn"] = ch["d_a"] * ch["m"]["dm"]
                ch["nn_q"] = _bdot(ch["nn"], ch["qh"], 0, 0)
                ch["nn_k"] = _bdot(ch["nn"], ch["kh"], 1, 0)
            for ch in chains:
                ch["x"] = _dot(ch["d_t"].astype(BF16), ch["t"], 1, 0)
            for ch in chains:
                d_l = -_dot(ch["t"], ch["x"].astype(BF16), 1, 0)
                ch["d_l"] = jnp.where(ch["m"]["strict"], d_l, 0.0)
                ch["mm"] = ch["d_l"] * ch["m"]["dm"]
            for ch in chains:
                m = ch["m"]
                ch["mm_kh"] = _bdot(ch["mm"], ch["kh"], 1, 0)
                ch["mm_kb"] = _bdot(ch["mm"], ch["kb"], 0, 0)
                l_mat = jnp.where(m["strict"], m["beta"] * ch["kk"] * m["dm"], 0.0)
                ch["e"] = ch["d_l"] * l_mat + ch["nn"] * ch["qk"]
                dbgr_ref[ch["c"], ch["col"]:ch["col"] + 1, :] = -_dot(ones8, ch["e"], 1, 0, HI)[0:1, :]
            acc_bg = None
            acc = {}
            for n_done, ch in enumerate(chains):
                m = ch["m"]
                rows = ch["rows"]
                if n_done % (2 * GDN_H) == 0:
                    acc_bg = jnp.zeros((CHUNK, 128), F32)
                beta, eg, egl = m["beta"], m["eg"], m["egl"]
                dkb = ch["mm_kh"] + ch["dkbg"] * eg
                dk_d = ch["mm_kb"] + ch["nn_q"] + ch["dkd"] * egl + dkb * beta
                dq_d = ch["nn_k"] + ch["dqd"] * eg
                dv_d = ch["dvb"] * beta
                dkd_kd = ch["dkd"] * (ch["kh"] * egl)
                dgc = (jnp.sum(ch["e"], axis=1, keepdims=True)
                       + jnp.sum(ch["dqd"] * (ch["qh"] * eg) - dkd_kd + ch["dkbg"] * ch["kbg"], axis=1, keepdims=True))
                dgl = jnp.sum(jnp.sum(dkd_kd, axis=1, keepdims=True), axis=0, keepdims=True) + ch["dcd"] * m["cd"]
                dgc = dgc + jnp.where(rowi == (0 if ch["d"] == 1 else CHUNK - 1), dgl, 0.0)
                dbeta = jnp.sum(dkb * ch["kh"] + ch["dvb"] * ch["vh"], axis=1, keepdims=True)
                acc_bg = acc_bg + jnp.where(lane == ch["col"], dbeta, 0.0) + jnp.where(lane == 8 + ch["col"], dgc, 0.0)
                if ch["d"] == 0:
                    acc[ch["h"]] = (dq_d, dk_d, dv_d)
                else:
                    dq0, dk0, dv0 = acc[ch["h"]]
                    dq_ref[rows, ch["cols"]] = dq0 + dq_d
                    dk_ref[rows, ch["cols"]] = dk0 + dk_d
                    dv_ref[rows, ch["cols"]] = dv0 + dv_d
                if n_done % (2 * GDN_H) == 2 * GDN_H - 1:
                    dbg_ref[rows, :] = acc_bg
            return carry

        lax.fori_loop(0, ncb // BWD_CHUNKS, chunk, 0)

    im = lambda i: (i, 0)
    im4 = lambda i: (i, 0, 0, 0)
    blk = (ts, GDN_W)
    ins = [(q, blk, im), (k, blk, im), (v, blk, im), (bg, (ts, 128), im), (gcr, (ncb, 8, CHUNK), lambda i: (i, 0, 0)),
           (do, blk, im)]
    for d in range(2):
        ins += [(loc[d][3], (ncb, GDN_H, CHUNK, CHUNK), im4), (fwd[d][2], (ncb, GDN_H, GDN_DK, GDN_DK), im4),
                (adj[d][1], (ncb, GDN_H, GDN_DK, GDN_DK), im4), (fwd[d][1], blk, im), (adj[d][0], blk, im)]
    sds = jax.ShapeDtypeStruct((S, GDN_W), F32)
    outs = [(sds, blk, im), (sds, blk, im), (sds, blk, im), (jax.ShapeDtypeStruct((S, 128), F32), (ts, 128), im),
            (jax.ShapeDtypeStruct((S // CHUNK, 8, CHUNK), F32), (ncb, 8, CHUNK), lambda i: (i, 0, 0))]
    dq, dk, dv, dbg, dbg_rows = _rows("gdn_local_bwd", S, ts, ins, outs, body)
    dgc_cols = dbg_rows.transpose(0, 2, 1).reshape(S, 8)
    return dq, dk, dv, dbg + jnp.pad(dgc_cols, ((0, 0), (8, 112)))


def _gdn_prep_bwd(dbg_all, p, prm):
    S = p.shape[0]
    ts = _tile(S, 512)

    def body(dbg_ref, p_ref, prm_ref, dba_ref, dprm_ref):
        i = pl.program_id(0)
        raw = p_ref[...]
        dbg = dbg_ref[...]
        lane = lax.broadcasted_iota(jnp.int32, (1, 128), 1)
        is_g = (lane >= 8) & (lane < 16)
        ea = jnp.exp(prm_ref[0:1, :])
        arg = raw + prm_ref[1:2, :]
        g = jnp.where(is_g, -ea * _softplus(arg), 0.0)
        beta = _sigmoid(raw)
        dgc = jnp.where(is_g, dbg, 0.0)
        ri, ci = _tri_masks()
        lower = (ri >= ci).astype(F32)
        upper = (ri <= ci).astype(F32)
        dgs = []
        for c in range(ts // CHUNK):
            ch = dgc[c * CHUNK:(c + 1) * CHUNK]
            dgs.append(jnp.where(lane < 12, _dot(upper, ch, 1, 0, HI), _dot(lower, ch, 1, 0, HI)))
        dg = jnp.concatenate(dgs, axis=0)
        dalpha = jnp.where(is_g, dg * (-ea) * _sigmoid(arg), 0.0)
        dba_ref[...] = jnp.where(lane < 8, dbg * beta * (1.0 - beta), dalpha).astype(BF16)
        rows = jnp.concatenate([jnp.sum(dg * g, axis=0, keepdims=True), jnp.sum(dalpha, axis=0, keepdims=True),
                                jnp.zeros((6, 128), F32)], axis=0)
        _colsum_into(dprm_ref, i, rows)

    im = lambda i: (i, 0)
    z0 = lambda i: (0, 0)
    return _rows("gdn_prep_bwd", S, ts,
                 [(dbg_all, (ts, 128), im), (p, (ts, 128), lambda i: (i, COL_BA // 128)), (prm, (8, 128), z0)],
                 [(jax.ShapeDtypeStruct((S, 128), BF16), (ts, 128), im), (jax.ShapeDtypeStruct((8, 128), F32), (8, 128), z0)],
                 body)


def _mm_plain(name, M, N, K, tm, tn, tk, a, am, b, bm, dtype):
    return _fused_mm(name, M, N, K, tm, tn, tk, [(a, am), (b, bm)], [(0, 1, 0)], [],
                     [(jax.ShapeDtypeStruct((M, N), dtype), (tm, tn), _mn)],
                     lambda i, accs, ex, out: out[0].__setitem__(Ellipsis, accs[0][...].astype(dtype)))[0]


def _layer_bwd(x0, W, R, emit_big=None, emit_small=None):
    S = x0.shape[0]
    tm = _tile(S, 512)
    tk_s = _tile(S, 1024)
    G = {}

    def emit(**named):
        if emit_big is None:
            G.update(named)
            return None
        return emit_big(**named)

    def ffn_emit(prefix):
        return lambda **kw: emit(**{f"{prefix}_w_{k}": v for k, v in kw.items()})

    dx2, G["ffn2_norm"] = _ffn_bwd("ffn2b", R["dx3"], R["x2"], W["ffn2_norm"], R["h3"], R["a2"], R["b2"], R["f2"],
                                   W["ffn2_w_gate"], W["ffn2_w_up"], W["ffn2_w_down"], ffn_emit("ffn2"))
    tok = emit(w_out=_mm_plain("dw_out", D_MODEL, D_MODEL, S, D_MODEL, D_MODEL, tk_s, R["y"], "km", dx2, "kn", BF16))
    gn = W["gdn_norm"] if tok is None else W["gdn_norm"] + tok
    dy = _mm_plain("dy_mix", S, D_MODEL, D_MODEL, tm, D_MODEL, D_MODEL, dx2, "mk", W["w_out"], "nk", F32)
    p = R["p"]
    dhr, dgate, do, dz, G["gdn_norm"] = _mix_out_bwd(dy, R["h_f"], R["h_b"], R["o_f"], R["o_b"], p, gn)
    lam_b, lam_f = _rg_scan_adj("rg_scan_bwd", R["a_b"], dhr, R["a_f"], dhr)
    dpre, dxc_direct, d_rgprm = _rg_gates_bwd(R["xc"], R["bd"], R["rg_prm"], lam_f, lam_b, R["h_f"], R["h_b"])
    tmg = _tile(S, 512)
    dxc = _fused_mm("rg_dxc", S, RG_W, 4 * RG_W, tmg, RG_W, 4 * RG_W, [(dpre, "mk"), (R["bd"], "nk")], [(0, 1, 0)],
                    [(dxc_direct, (tmg, RG_W), _mn)], [(jax.ShapeDtypeStruct((S, RG_W), F32), (tmg, RG_W), _mn)],
                    lambda i, accs, ex, out: out[0].__setitem__(Ellipsis, ex[0][...] + accs[0][...]))[0]
    d_bd = _mm_plain("rg_dbd", RG_W, 4 * RG_W, S, RG_W, 4 * RG_W, tk_s, R["xc"], "km", dpre, "kn", F32)
    dx_rg, G["rg_conv_w"], G["rg_conv_b"] = _conv_bwd("rg_conv_bwd", p, 0, W["rg_conv_w"], [dxc], "bias")
    blocks = jnp.einsum("nigmj,nm->gnij", d_bd.reshape(RG_BLOCKS, RG_BLOCK, 4, RG_BLOCKS, RG_BLOCK),
                        jnp.eye(RG_BLOCKS, dtype=F32))
    G["rg_gate_a_w"] = jnp.stack([blocks[0], blocks[2]])
    G["rg_gate_x_w"] = jnp.stack([blocks[1], blocks[3]])
    G["rg_gate_a_b"] = jnp.stack([d_rgprm[0], d_rgprm[2]])
    G["rg_gate_x_b"] = jnp.stack([d_rgprm[1], d_rgprm[3]])
    G["rg_lambda"] = d_rgprm[4:6]
    adj = _gdn_scan_bwd(R["gdn_loc"], do)
    dq, dk, dv, dbg = _gdn_local_bwd(R["q"], R["k"], R["v"], R["bg"], R["gcr"], do, R["gdn_loc"], R["gdn_fwd"], adj)
    cw = W["gdn_conv_w"]
    dpq, dwq, _ = _conv_bwd("gdn_conv_q_bwd", p, 2, cw[:, 0:512], [dq], "q")
    dpk, dwk, _ = _conv_bwd("gdn_conv_k_bwd", p, 3, cw[:, 512:1024], [dk], "k")
    dpv, dwv, _ = _conv_bwd("gdn_conv_v_bwd", p, 4, cw[:, 1024:1536], [dv], "v")
    G["gdn_conv_w"] = jnp.concatenate([dwq, dwk, dwv], axis=1)
    dba, d_gprm = _gdn_prep_bwd(dbg, p, R["gdn_prm"])
    G["gdn_a_log"] = d_gprm[0, 8:16].reshape(2, GDN_H)
    G["gdn_dt_bias"] = d_gprm[1, 8:16].reshape(2, GDN_H)
    dp = jnp.concatenate([dx_rg, dgate, dpq, dpk, dpv, dz, dba], axis=1)
    tok = emit(w_in=_mm_plain("dw_in", D_MODEL, D_IN_PAD, S, D_MODEL, 640, tk_s, R["h2"], "km", dp, "kn", BF16))
    g_mix = W["mix_norm"] if tok is None else W["mix_norm"] + tok

    def epi_dx1(i, accs, ex, out):
        dx, dgt = _rmsnorm_bwd_tile(accs[0][...], ex[0][...], ex[1][...])
        out[0][...] = ex[2][...] + dx
        _colsum_into(out[1], i, jnp.sum(dgt, axis=0, keepdims=True))

    dx1, G["mix_norm"] = _fused_mm(
        "mix_dx", S, D_MODEL, D_IN_PAD, tm, D_MODEL, D_IN_PAD, [(dp, "mk"), (W["w_in"], "nk")], [(0, 1, 0)],
        [(R["x1"], (tm, D_MODEL), _mn), (g_mix, (1, D_MODEL), _row0), (dx2, (tm, D_MODEL), _mn)],
        [(jax.ShapeDtypeStruct((S, D_MODEL), F32), (tm, D_MODEL), _mn),
         (jax.ShapeDtypeStruct((1, D_MODEL), F32), (1, D_MODEL), _row0)], epi_dx1)
    G["final_norm"] = R["d_final_norm"]
    if emit_small is not None:
        emit_small(G)
    dx0, G["ffn1_norm"] = _ffn_bwd("ffn1b", dx1, x0, W["ffn1_norm"], R["h1"], R["a1"], R["b1"], R["f1"],
                                   W["ffn1_w_gate"], W["ffn1_w_up"], W["ffn1_w_down"], ffn_emit("ffn1"))
    return dx0, G


def _mesh_pos():
    x, y, c = lax.axis_index("x"), lax.axis_index("y"), lax.axis_index("c")
    return x, y, c, 4 * x + 2 * y + c


def _peer(x, y, c, r):
    px = 1 - x if r & 4 else x
    py = 1 - y if r & 2 else y
    pc = 1 - c if r & 1 else c
    return (px, py, pc), 4 * px + 2 * py + pc


_HBM = pl.BlockSpec(memory_space=pltpu.HBM)
_SEM = pl.BlockSpec(memory_space=pltpu.SEMAPHORE)


def _peer_copies(scatter, srcs, lands, send_sems, recv_sems):
    x, y, c, me = _mesh_pos()
    copies = []
    for a, (src, land) in enumerate(zip(srcs, lands)):
        for r in range(1, N_DEV):
            peer, peer_idx = _peer(x, y, c, r)
            copies.append(pltpu.make_async_remote_copy(
                src_ref=src.at[peer_idx] if scatter else src, dst_ref=land.at[r - 1] if scatter else land.at[me],
                send_sem=send_sems.at[a * 7 + r - 1], recv_sem=recv_sems.at[a * 7 + r - 1],
                device_id=peer, device_id_type=pl.DeviceIdType.MESH))
    return copies


def _exchange_start(name, scatter, arrays):
    slabs = arrays
    n = len(slabs)

    def body(*refs):
        srcs, lands = refs[0:n], refs[n:2 * n]
        send_sems, recv_sems = refs[2 * n], refs[2 * n + 1]
        token = refs[4 * n + 2]
        for cp in _peer_copies(scatter, srcs, lands, send_sems, recv_sems):
            cp.start()
        token[...] = jnp.zeros_like(token)

    land_shapes = [(N_DEV - 1,) + s.shape[1:] if scatter else (N_DEV,) + s.shape for s in slabs]
    out_shape = ([pltpu.SemaphoreType.DMA((7 * n,)), pltpu.SemaphoreType.DMA((7 * n,))]
                 + [pltpu.HBM(s.shape, s.dtype) for s in slabs]
                 + [pltpu.HBM(shp, s.dtype) for shp, s in zip(land_shapes, slabs)]
                 + [jax.ShapeDtypeStruct((8, 128), F32)])
    res = pl.pallas_call(
        body, name=name, out_shape=out_shape, in_specs=[_HBM] * (2 * n),
        out_specs=[_SEM, _SEM] + [_HBM] * (2 * n) + [pl.BlockSpec(memory_space=pltpu.VMEM)],
        input_output_aliases={i: 2 + i for i in range(2 * n)},
        compiler_params=pltpu.CompilerParams(has_side_effects=pltpu.SideEffectType.DATAFLOW_SIDE_EFFECTING),
    )(*[pltpu.with_memory_space_constraint(s, pltpu.HBM) for s in slabs],
      *[pltpu.with_memory_space_constraint(lax.empty(shp, s.dtype), pltpu.HBM) for shp, s in zip(land_shapes, slabs)])
    return dict(n=n, scatter=scatter, sems=res[0:2], srcs=res[2:2 + n], lands=res[2 + n:2 + 2 * n],
                token=res[2 + 2 * n][0, 0])


def _exchange_wait(name, started, after):
    n = started["n"]
    scatter = started["scatter"]

    def body(*refs):
        srcs, lands = refs[0:n], refs[n:2 * n]
        send_sems, recv_sems = refs[2 * n], refs[2 * n + 1]
        for cp in _peer_copies(scatter, srcs, lands, send_sems, recv_sems):
            cp.wait_send()
            cp.wait_recv()

    arrays = list(started["srcs"]) + list(started["lands"])
    res = pl.pallas_call(
        body, name=name, out_shape=[pltpu.HBM(a.shape, a.dtype) for a in arrays],
        in_specs=[_HBM] * (2 * n) + [_SEM, _SEM, pl.BlockSpec(memory_space=pl.ANY)], out_specs=[_HBM] * (2 * n),
        input_output_aliases={i: i for i in range(2 * n)},
        compiler_params=pltpu.CompilerParams(has_side_effects=pltpu.SideEffectType.DATAFLOW_SIDE_EFFECTING),
    )(*arrays, *started["sems"], after)
    return res[0:n], res[n:2 * n]


def _all_gather(name, arrays):
    n = len(arrays)

    def body(*refs):
        ins = refs[:n]
        outs = refs[n:2 * n]
        token = refs[2 * n]
        send_sems, recv_sems, local_sems = refs[2 * n + 1:]
        token[...] = jnp.zeros_like(token)
        x, y, c, me = _mesh_pos()
        sibling = (x, y, 1 - c)
        chips = [(1 - x, y), (x, 1 - y), (1 - x, 1 - y)]

        def idx(px, py, pc):
            return 4 * px + 2 * py + pc

        def copy(a, k, block, to, src=None):
            slot = outs[a].at[idx(*block)]
            return pltpu.make_async_remote_copy(
                src_ref=slot if src is None else src, dst_ref=slot, send_sem=send_sems.at[a * 7 + k],
                recv_sem=recv_sems.at[a * 7 + k], device_id=to, device_id_type=pl.DeviceIdType.MESH)

        locals_, sends = [], []
        for a in range(n):
            loc = pltpu.make_async_copy(ins[a], outs[a].at[me], local_sems.at[a])
            loc.start()
            locals_.append(loc)
            sends.append(copy(a, 0, (x, y, c), sibling, src=ins[a]))
            sends += [copy(a, 1 + j, (x, y, c), (*chip, c), src=ins[a]) for j, chip in enumerate(chips)]
        for cp in sends:
            cp.start()
        passed = []
        for a in range(n):
            for j, chip in enumerate(chips):
                copy(a, 1 + j, (*chip, c), (x, y, c)).wait_recv()
                fwd = copy(a, 4 + j, (*chip, c), sibling)
                fwd.start()
                passed.append(fwd)
        for a in range(n):
            copy(a, 0, sibling, (x, y, c)).wait_recv()
            for j, chip in enumerate(chips):
                copy(a, 4 + j, (*chip, 1 - c), (x, y, c)).wait_recv()
        for cp in sends + passed:
            cp.wait_send()
        for loc in locals_:
            loc.wait()

    any_spec = pl.BlockSpec(memory_space=pl.ANY)
    res = pl.pallas_call(
        body, name=name, in_specs=[any_spec] * n, out_specs=[any_spec] * n + [pl.BlockSpec(memory_space=pltpu.VMEM)],
        out_shape=[jax.ShapeDtypeStruct((N_DEV,) + a.shape, a.dtype) for a in arrays]
        + [jax.ShapeDtypeStruct((8, 128), F32)],
        scratch_shapes=[pltpu.SemaphoreType.DMA((7 * n,)), pltpu.SemaphoreType.DMA((7 * n,)),
                        pltpu.SemaphoreType.DMA((n,))],
        compiler_params=pltpu.CompilerParams(has_side_effects=True),
    )(*arrays)
    return res[:n], res[n][0, 0]


def _adamw_math(w, g, m, v):
    m2 = ADAM_B1 * m + (1.0 - ADAM_B1) * g
    v2 = ADAM_B2 * v + (1.0 - ADAM_B2) * (g * g)
    m_hat = m2 / (1.0 - ADAM_B1 ** ADAM_STEP)
    v_hat = v2 / (1.0 - ADAM_B2 ** ADAM_STEP)
    delta = -ADAM_LR * (m_hat / (jnp.sqrt(v_hat) + ADAM_EPS) + ADAM_WD * w)
    return delta, m2, v2


def _adamw_slabs(name, src, land, me, w, m, v, tr):
    R, C = w.shape

    def body(me_ref, own_ref, land_ref, w_ref, m_ref, v_ref, g_ref, d_ref, m2_ref, v2_ref):
        g = own_ref[0].astype(F32)
        for s in range(N_DEV - 1):
            g = g + land_ref[s].astype(F32)
        delta, m2, v2 = _adamw_math(w_ref[...], g, m_ref[...], v_ref[...])
        g_ref[...] = g
        d_ref[...] = delta
        m2_ref[...] = m2
        v2_ref[...] = v2

    im = lambda i, me_ref: (i, 0)
    grid_spec = pltpu.PrefetchScalarGridSpec(
        num_scalar_prefetch=1, grid=(R // tr,),
        in_specs=[pl.BlockSpec((1, tr, C), lambda i, me_ref: (me_ref[0], i, 0)),
                  pl.BlockSpec((N_DEV - 1, tr, C), lambda i, me_ref: (0, i, 0)),
                  pl.BlockSpec((tr, C), im), pl.BlockSpec((tr, C), im), pl.BlockSpec((tr, C), im)],
        out_specs=[pl.BlockSpec((tr, C), im)] * 4)
    return pl.pallas_call(body, name=name, grid_spec=grid_spec, out_shape=[jax.ShapeDtypeStruct((R, C), F32)] * 4,
                          compiler_params=_cp(1))(me.reshape(1).astype(jnp.int32), src, land, w, m, v)


def _sum_slots(name, slots):
    _, R, C = slots.shape

    def body(s_ref, o_ref):
        g = s_ref[0]
        for s in range(1, N_DEV):
            g = g + s_ref[s]
        o_ref[...] = g

    return _rows(name, R, R, [(slots, (N_DEV, R, C), lambda i: (0, 0, 0))],
                 [(jax.ShapeDtypeStruct((R, C), F32), (R, C), lambda i: (0, 0))], body)[0]


def _adamw_packed(name, g, w, m, v):
    R, C = g.shape

    def body(g_ref, w_ref, m_ref, v_ref, d_ref, m2_ref, v2_ref):
        delta, m2, v2 = _adamw_math(w_ref[...], g_ref[...], m_ref[...], v_ref[...])
        d_ref[...] = delta
        m2_ref[...] = m2
        v2_ref[...] = v2

    im = lambda i: (0, 0)
    sds = jax.ShapeDtypeStruct((R, C), F32)
    return _rows(name, R, R, [(a, (R, C), im) for a in (g, w, m, v)], [(sds, (R, C), im)] * 3, body)


def _pack(arrays):
    rows = []
    for a in arrays:
        flat = a.reshape(-1).astype(F32)
        pad = (-flat.shape[0]) % 128
        rows.append(jnp.pad(flat, (0, pad)).reshape(-1, 128))
    out = jnp.concatenate(rows, axis=0)
    return jnp.pad(out, ((0, (-out.shape[0]) % 8), (0, 0)))


def _unpack(packed, shapes):
    lead = packed.shape[:-2]
    outs = []
    r = 0
    for shp in shapes:
        n = math.prod(shp)
        nr = -(-n // 128)
        flat = packed[..., r:r + nr, :].reshape(lead + (nr * 128,))[..., :n]
        outs.append(flat.reshape(lead + tuple(shp)))
        r += nr
    return outs


FFN1_BIG = ["ffn1_w_gate", "ffn1_w_up", "ffn1_w_down"]
MIX_BIG = ["w_in", "w_out"]
FFN2_BIG = ["ffn2_w_gate", "ffn2_w_up", "ffn2_w_down"]
BIG = FFN1_BIG + MIX_BIG + FFN2_BIG
COL_SHARDED = {"ffn1_w_gate", "ffn1_w_up", "w_in", "ffn2_w_gate", "ffn2_w_up"}
SMALL_SHARDED = ["rg_conv_w", "rg_gate_a_b", "rg_gate_x_b", "rg_lambda", "gdn_conv_w"]
WEIGHTS = ["ffn1_norm", "ffn1_w_gate", "ffn1_w_up", "ffn1_w_down", "mix_norm", "w_in", "w_out", "rg_conv_w", "rg_conv_b",
           "rg_gate_a_w", "rg_gate_a_b", "rg_gate_x_w", "rg_gate_x_b", "rg_lambda", "gdn_conv_w", "gdn_a_log",
           "gdn_dt_bias", "gdn_norm", "ffn2_norm", "ffn2_w_gate", "ffn2_w_up", "ffn2_w_down", "final_norm"]
SMALL = [n for n in WEIGHTS if n not in BIG]
ROW_VECTORS = {"ffn1_norm", "mix_norm", "ffn2_norm", "gdn_norm", "rg_conv_b", "final_norm"}
ROW_TILE = {"ffn1_w_gate": 256, "ffn1_w_up": 256, "ffn1_w_down": 176, "w_in": 256, "w_out": 64,
            "ffn2_w_gate": 256, "ffn2_w_up": 256, "ffn2_w_down": 176}


def _unshard_cols(g):
    return g.transpose(1, 0, 2).reshape(g.shape[1], N_DEV * g.shape[2])


def _to_slabs(name, g):
    if name in COL_SHARDED:
        r, ctot = g.shape
        return g.reshape(r, N_DEV, ctot // N_DEV).transpose(1, 0, 2)
    return g.reshape(N_DEV, g.shape[0] // N_DEV, g.shape[1])


def _step(x, target, w, m, v):
    _, _, _, me = _mesh_pos()
    def unshard(n, gth):
        full = _unshard_cols(gth) if n in COL_SHARDED else gth.reshape(-1, gth.shape[-1])
        return jnp.pad(full, ((0, 0), (0, D_IN_PAD - D_IN))) if n == "w_in" else full

    def landed(started, name, after):
        srcs, lands = _exchange_wait(name, started, after)
        def with_own(src, land):
            slot = lax.broadcasted_iota(jnp.int32, (N_DEV,) + (1,) * src.ndim, 0)
            return jnp.where(slot == me, src[None], land)

        return [with_own(src, land) for src, land in zip(srcs, lands)]

    up_names = ["ffn1_w_gate", "ffn1_w_up"]
    first, tok = _all_gather("gather_ffn1", [w[n].astype(BF16) for n in up_names])
    W = {n: unshard(n, gth) for n, gth in zip(up_names, first)}
    small_shards = [w[n] for n in SMALL_SHARDED]
    st_down = _exchange_start("gather_ffn1_down_start", False, [(w["ffn1_w_down"] + tok).astype(BF16)])
    st_mix = _exchange_start("gather_mix_start", False,
                             [(w[n] + tok).astype(BF16) for n in MIX_BIG] + [_pack(small_shards) + tok])
    st_ffn2 = _exchange_start("gather_ffn2_start", False, [(w[n] + tok).astype(BF16) for n in FFN2_BIG])
    for n in SMALL:
        if n not in SMALL_SHARDED:
            W[n] = w[n]
    W["ffn1_norm"] = w["ffn1_norm"] + (st_down["token"] + st_mix["token"] + st_ffn2["token"])

    def more(stage, after):
        if stage == "ffn1_down":
            return {"ffn1_w_down": unshard("ffn1_w_down", landed(st_down, "gather_ffn1_down_wait", after)[0])}
        if stage == "ffn2":
            return {n: unshard(n, gth) for n, gth in zip(FFN2_BIG, landed(st_ffn2, "gather_ffn2_wait", after))}
        got = landed(st_mix, "gather_mix_wait", after)
        new = {n: unshard(n, gth) for n, gth in zip(MIX_BIG, got)}
        for n, gth in zip(SMALL_SHARDED, _unpack(got[-1], [s.shape for s in small_shards])):
            new[n] = jnp.moveaxis(gth, 0, -2).reshape(gth.shape[1:-1] + (N_DEV * gth.shape[-1],))
        return new

    R = _layer_fwd(x, target, W, more)
    W = R["W"]
    pending = []

    def emit_big(**named):
        slabs = [_to_slabs(n, g[:, :D_IN] if n == "w_in" else g) for n, g in named.items()]
        started = _exchange_start(f"scatter_start_{len(pending)}", True, slabs)
        pending.append((list(named), started))
        return started["token"]

    small_started = []

    def emit_small(G):
        packed = _pack([G[n] for n in SMALL if n != "ffn1_norm"])
        small_started.append(_exchange_start("gather_small_start", False, [packed]))

    grad_x, G = _layer_bwd(x, W, R, emit_big, emit_small)
    loss = lax.psum(R["loss"][0, 0], ("x", "y", "c"))
    out = {}

    def finish(i, after):
        names, started = pending[i]
        srcs, lands = _exchange_wait(f"scatter_wait_{i}", started, after)
        for n, src, land in zip(names, srcs, lands):
            out[n] = _adamw_slabs(f"adamw_{n}", src, land, me, w[n], m[n], v[n], ROW_TILE[n])

    n_early = len(pending) - 2
    for i in range(n_early):
        finish(i, grad_x)
    early = [n for n in SMALL if n != "ffn1_norm"]
    srcs, lands = _exchange_wait("gather_small_wait", small_started[0], grad_x)
    slot = lax.broadcasted_iota(jnp.int32, (N_DEV, 1, 1), 0)
    slots = jnp.where(slot == me, srcs[0][None], lands[0])
    reduced = dict(zip(early, _unpack(_sum_slots("sum_small_grads", slots), [G[n].shape for n in early])))
    late = _all_gather("gather_ffn1_norm_grad", [_pack([G["ffn1_norm"]])])[0][0]
    reduced["ffn1_norm"] = _unpack(_sum_slots("sum_ffn1_norm_grad", late), [G["ffn1_norm"].shape])[0]
    g_small = []
    for n in SMALL:
        g = reduced[n]
        if n in SMALL_SHARDED:
            per = g.shape[-1] // N_DEV
            g = lax.dynamic_slice_in_dim(g, me * per, per, axis=g.ndim - 1)
        g_small.append(g.reshape(w[n].shape))
    shapes = [w[n].shape for n in SMALL]
    d_p, m_p, v_p = _adamw_packed("adamw_small", _pack(g_small), _pack([w[n] for n in SMALL]),
                                  _pack([m[n] for n in SMALL]), _pack([v[n] for n in SMALL]))
    for n, g, d_, m_, v_ in zip(SMALL, g_small, _unpack(d_p, shapes), _unpack(m_p, shapes), _unpack(v_p, shapes)):
        out[n] = (g, d_, m_, v_)
    for i in range(n_early, len(pending)):
        finish(i, d_p)
    return loss, grad_x, out


def kernel(x, ffn1_norm, ffn1_w_gate, ffn1_w_up, ffn1_w_down, mix_norm, w_in, w_out, rg_conv_w, rg_conv_b, rg_gate_a_w, rg_gate_a_b, rg_gate_x_w, rg_gate_x_b, rg_lambda, gdn_conv_w, gdn_a_log, gdn_dt_bias, gdn_norm, ffn2_norm, ffn2_w_gate, ffn2_w_up, ffn2_w_down, final_norm, loss_target, m_ffn1_norm, m_ffn1_w_gate, m_ffn1_w_up, m_ffn1_w_down, m_mix_norm, m_w_in, m_w_out, m_rg_conv_w, m_rg_conv_b, m_rg_gate_a_w, m_rg_gate_a_b, m_rg_gate_x_w, m_rg_gate_x_b, m_rg_lambda, m_gdn_conv_w, m_gdn_a_log, m_gdn_dt_bias, m_gdn_norm, m_ffn2_norm, m_ffn2_w_gate, m_ffn2_w_up, m_ffn2_w_down, m_final_norm, v_ffn1_norm, v_ffn1_w_gate, v_ffn1_w_up, v_ffn1_w_down, v_mix_norm, v_w_in, v_w_out, v_rg_conv_w, v_rg_conv_b, v_rg_gate_a_w, v_rg_gate_a_b, v_rg_gate_x_w, v_rg_gate_x_b, v_rg_lambda, v_gdn_conv_w, v_gdn_a_log, v_gdn_dt_bias, v_gdn_norm, v_ffn2_norm, v_ffn2_w_gate, v_ffn2_w_up, v_ffn2_w_down, v_final_norm):
    args = dict(locals())
    orig_shapes = {n: args[n].shape for n in WEIGHTS}

    def local(prefix):
        d = {}
        for n in WEIGHTS:
            a = args[prefix + n]
            d[n] = a.reshape(1, -1) if n in ROW_VECTORS else a[0]
        return d

    loss, grad_x, out = _step(x[0], loss_target[0], local(""), local("m_"), local("v_"))
    res = [loss, grad_x[None]]
    for k in range(4):
        res += [out[n][k].reshape(orig_shapes[n]) for n in WEIGHTS]
    return tuple(res)
```

```python
import functools
import math

import jax
import jax.numpy as jnp
from jax import lax
from jax.experimental import pallas as pl
from jax.experimental.pallas import tpu as pltpu

F32, BF16 = jnp.float32, jnp.bfloat16

D_MODEL = 1024
D_FF = 2816
RG_W = 512
RG_BLOCKS = 8
RG_BLOCK = 64
RG_C = 8.0
CONV_W = 4
GDN_H = 4
GDN_DK = 128
CHUNK = 64
EPS = 1e-6
D_IN = 3088
D_IN_PAD = 3200
COL_BA = 3072
N_DEV = 8
HALO = 8
VMEM_LIMIT = 48 * 1024 * 1024

ADAM_LR = 0.001
ADAM_B1 = 0.9
ADAM_B2 = 0.999
ADAM_EPS = 1e-08
ADAM_WD = 0.01
ADAM_STEP = 10

HI = lax.Precision.HIGHEST


def _cp(n):
    return pltpu.CompilerParams(dimension_semantics=("arbitrary",) * n, vmem_limit_bytes=VMEM_LIMIT)


def _tile(n, pref):
    return min(n, pref)


def _sigmoid(x):
    return 0.5 * jnp.tanh(0.5 * x) + 0.5


def _softplus(x):
    return jnp.maximum(x, 0.0) + jnp.log(1.0 + jnp.exp(-jnp.abs(x)))


def _dot(a, b, ca, cb, prec=None):
    return lax.dot_general(a, b, (((ca,), (cb,)), ((), ())), preferred_element_type=F32, precision=prec)


def _fused_mm(name, M, N, K, tm, tn, tk, ops, pairs, extras, outs, epilogue):
    nm, nn, nk = M // tm, N // tn, K // tk
    assert nm * tm == M and nn * tn == N and nk * tk == K, (name, M, N, K, tm, tn, tk)
    spec_of = {
        "mk": pl.BlockSpec((tm, tk), lambda i, j, k: (i, k)),
        "km": pl.BlockSpec((tk, tm), lambda i, j, k: (k, i)),
        "kn": pl.BlockSpec((tk, tn), lambda i, j, k: (k, j)),
        "nk": pl.BlockSpec((tn, tk), lambda i, j, k: (j, k)),
    }
    in_specs = [spec_of[m] for _, m in ops]
    in_specs += [pl.BlockSpec(bs, lambda i, j, k, im=im: im(i, j)) for _, bs, im in extras]
    out_specs = [pl.BlockSpec(bs, lambda i, j, k, im=im: im(i, j)) for _, bs, im in outs]
    n_ops, n_ex, n_out = len(ops), len(extras), len(outs)
    n_acc = 1 + max(g for _, _, g in pairs)
    modes = [m for _, m in ops]

    def body(*refs):
        op_refs = refs[:n_ops]
        ex_refs = refs[n_ops:n_ops + n_ex]
        out_refs = refs[n_ops + n_ex:n_ops + n_ex + n_out]
        accs = refs[n_ops + n_ex + n_out:]
        i = pl.program_id(0)
        k = pl.program_id(2)
        def dots():
            vals = [r[...].astype(BF16) for r in op_refs]
            for ia, ib, g in pairs:
                yield g, _dot(vals[ia], vals[ib], 1 if modes[ia] == "mk" else 0, 0 if modes[ib] == "kn" else 1)

        if nk == 1:
            sums = [None] * n_acc
            for g, d in dots():
                sums[g] = d if sums[g] is None else sums[g] + d
            epilogue(i, [_Held(s) for s in sums], ex_refs, out_refs)
            return

        @pl.when(k == 0)
        def _():
            for a in accs:
                a[...] = jnp.zeros_like(a)

        for g, d in dots():
            accs[g][...] += d

        @pl.when(k == nk - 1)
        def _():
            epilogue(i, accs, ex_refs, out_refs)

    res = pl.pallas_call(
        body, name=name, grid=(nm, nn, nk), in_specs=in_specs, out_specs=out_specs,
        out_shape=[o for o, _, _ in outs],
        scratch_shapes=[pltpu.VMEM((tm, tn), F32)] * (n_acc if nk > 1 else 0),
        compiler_params=_cp(3),
    )(*[a for a, _ in ops], *[a for a, _, _ in extras])
    return res


class _Held:
    def __init__(self, value):
        self.value = value

    def __getitem__(self, idx):
        return self.value[idx]


def _mn(i, j):
    return (i, j)


def _row0(i, j):
    return (0, 0)


def _rows(name, S, ts, ins, outs, body, scratch=()):
    return pl.pallas_call(
        body, name=name, grid=(S // ts,),
        in_specs=[pl.BlockSpec(bs, im) for _, bs, im in ins],
        out_specs=[pl.BlockSpec(bs, im) for _, bs, im in outs],
        out_shape=[o for o, _, _ in outs],
        scratch_shapes=list(scratch),
        compiler_params=_cp(1),
    )(*[a for a, _, _ in ins])


def _halo_ins(arr, S, ts, width, colblk):
    per = ts // HALO
    last = S // HALO - 1
    return [
        (arr, (ts, width), lambda i: (i, colblk)),
        (arr, (HALO, width), lambda i: (jnp.maximum(i * per - 1, 0), colblk)),
        (arr, (HALO, width), lambda i: (jnp.minimum((i + 1) * per, last), colblk)),
    ]


def _ext(main_ref, prev_ref, next_ref, i, n_tiles):
    prev = jnp.where(i > 0, prev_ref[...].astype(F32), 0.0)
    nxt = jnp.where(i < n_tiles - 1, next_ref[...].astype(F32), 0.0)
    return jnp.concatenate([prev, main_ref[...].astype(F32), nxt], axis=0)


def _shift(ext, off, ts):
    n = ext.shape[0]
    if off == 0:
        return ext[HALO:HALO + ts]
    return pltpu.roll(ext, (-off) % n, 0)[HALO:HALO + ts]


def _rmsnorm_fwd(name, x, g):
    S, D = x.shape
    ts = _tile(S, 512)

    def body(x_ref, g_ref, o_ref):
        xv = x_ref[...]
        r = lax.rsqrt(jnp.mean(xv * xv, axis=-1, keepdims=True) + EPS)
        o_ref[...] = (xv * r * g_ref[...]).astype(BF16)

    return _rows(name, S, ts,
                 [(x, (ts, D), lambda i: (i, 0)), (g, (1, D), lambda i: (0, 0))],
                 [(jax.ShapeDtypeStruct((S, D), BF16), (ts, D), lambda i: (i, 0))], body)[0]


def _rmsnorm_bwd_tile(dh, x, g):
    r = lax.rsqrt(jnp.mean(x * x, axis=-1, keepdims=True) + EPS)
    xhat = x * r
    dxn = dh * g
    dx = r * (dxn - xhat * jnp.mean(dxn * xhat, axis=-1, keepdims=True))
    return dx, dh * xhat


def _ffn_fwd(tag, x, h, wg, wu, wd):
    S = x.shape[0]
    tm = _tile(S, 512)
    tn = 1408

    def epi_up(i, accs, ex, out):
        a = accs[0][...]
        b = accs[1][...]
        s = _sigmoid(a)
        sa = a * s
        out[0][...] = sa.astype(BF16)
        out[1][...] = (b * (s * (1.0 + a * (1.0 - s)))).astype(BF16)
        out[2][...] = (sa * b).astype(BF16)

    sds = jax.ShapeDtypeStruct((S, D_FF), BF16)
    a, b, f = _fused_mm(f"{tag}_up", S, D_FF, D_MODEL, tm, tn, D_MODEL,
                        [(h, "mk"), (wg, "kn"), (wu, "kn")], [(0, 1, 0), (0, 2, 1)], [],
                        [(sds, (tm, tn), _mn)] * 3, epi_up)

    def epi_down(i, accs, ex, out):
        out[0][...] = ex[0][...] + 0.5 * accs[0][...]

    if callable(wd):
        wd = wd(f)
    xo = _fused_mm(f"{tag}_down", S, D_MODEL, D_FF, tm, D_MODEL, 1408,
                   [(f, "mk"), (wd, "kn")], [(0, 1, 0)], [(x, (tm, D_MODEL), _mn)],
                   [(jax.ShapeDtypeStruct((S, D_MODEL), F32), (tm, D_MODEL), _mn)], epi_down)[0]
    return xo, a, b, f


def _conv_taps(ext, w_ref, ts):
    acc = None
    for j in range(CONV_W):
        term = w_ref[j:j + 1, :] * _shift(ext, j - 2, ts)
        acc = term if acc is None else acc + term
    return acc


def _l2norm_heads(s, scale):
    outs = []
    for h in range(GDN_H):
        sh = s[:, h * GDN_DK:(h + 1) * GDN_DK]
        outs.append(sh * (lax.rsqrt(jnp.sum(sh * sh, axis=-1, keepdims=True) + EPS) * scale))
    return jnp.concatenate(outs, axis=-1)


def _conv_fwd(name, p, colblk, w, bias, mode):
    S = p.shape[0]
    ts = _tile(S, 512)
    n_tiles = S // ts
    C = w.shape[1]

    def body(main, prev, nxt, w_ref, b_ref, o_ref):
        i = pl.program_id(0)
        c = _conv_taps(_ext(main, prev, nxt, i, n_tiles), w_ref, ts)
        if mode == "bias":
            o_ref[...] = c + b_ref[...]
        else:
            s = c * _sigmoid(c)
            if mode == "q":
                s = _l2norm_heads(s, GDN_DK ** -0.5)
            elif mode == "k":
                s = _l2norm_heads(s, 1.0)
            o_ref[...] = s

    ins = _halo_ins(p, S, ts, C, colblk) + [(w, (CONV_W, C), lambda i: (0, 0)), (bias, (1, C), lambda i: (0, 0))]
    return _rows(name, S, ts, ins, [(jax.ShapeDtypeStruct((S, C), F32), (ts, C), lambda i: (i, 0))], body)[0]


def _rg_gate_terms(pre, xc, prm_ref, d):
    r = _sigmoid(pre[:, d * 1024:d * 1024 + RG_W] + prm_ref[2 * d:2 * d + 1, :])
    ig = _sigmoid(pre[:, d * 1024 + RG_W:(d + 1) * 1024] + prm_ref[2 * d + 1:2 * d + 2, :])
    sp = _softplus(-prm_ref[4 + d:5 + d, :])
    log_a = -RG_C * r * sp
    a = jnp.exp(log_a)
    t = jnp.tanh(log_a)
    sq = jnp.sqrt(-2.0 * t / (1.0 - t))
    return r, ig, sp, a, sq


def _rg_gates_fwd(xc, bd, prm):
    S = xc.shape[0]
    tm = _tile(S, 256)

    def epi(i, accs, ex, out):
        pre = accs[0][...]
        xv = ex[0][...]
        for d in range(2):
            r, ig, sp, a, sq = _rg_gate_terms(pre, xv, ex[1], d)
            out[2 * d][...] = a
            out[2 * d + 1][...] = sq * ig * xv

    sds = jax.ShapeDtypeStruct((S, RG_W), F32)
    blk = (tm, RG_W)
    im = lambda i, j: (i, 0)
    return _fused_mm("rg_gates_fwd", S, 4 * RG_W, RG_W, tm, 4 * RG_W, RG_W,
                     [(xc, "mk"), (bd, "kn")], [(0, 1, 0)],
                     [(xc, blk, im), (prm, (8, RG_W), _row0)], [(sds, blk, im)] * 4, epi)


SUBLANES = 8


def _scan_rows(a, b, reverse):
    rows = lax.broadcasted_iota(jnp.int32, a.shape, 0)
    s = 1
    while s < SUBLANES:
        shift = SUBLANES - s if reverse else s
        a_sh = pltpu.roll(a, shift, 0)
        b_sh = pltpu.roll(b, shift, 0)
        valid = (rows < SUBLANES - s) if reverse else (rows >= s)
        b = jnp.where(valid, a * b_sh + b, b)
        a = jnp.where(valid, a * a_sh, a)
        s *= 2
    return a, b


def _rg_scan(name, a_f, b_f, a_b, b_b):
    S, C = a_f.shape
    ts = _tile(S, 512)
    n_tiles = S // ts

    def body(af, bf, ab, bb, hf, hb, carry):
        @pl.when(pl.program_id(0) == 0)
        def _():
            carry[...] = jnp.zeros_like(carry)

        n_sub = ts // SUBLANES

        def step(j, c):
            cf, cb = c
            r0 = pl.multiple_of(j * SUBLANES, SUBLANES)
            cum_a, h0 = _scan_rows(af[pl.ds(r0, SUBLANES), :], bf[pl.ds(r0, SUBLANES), :], False)
            h = h0 + cum_a * cf
            hf[pl.ds(r0, SUBLANES), :] = h
            cf = h[SUBLANES - 1:SUBLANES, :]
            r1 = pl.multiple_of((n_sub - 1 - j) * SUBLANES, SUBLANES)
            cum_a, h0 = _scan_rows(ab[pl.ds(r1, SUBLANES), :], bb[pl.ds(r1, SUBLANES), :], True)
            h = h0 + cum_a * cb
            hb[pl.ds(r1, SUBLANES), :] = h
            cb = h[0:1, :]
            return cf, cb

        cf, cb = lax.fori_loop(0, n_sub, step, (carry[0:1, :], carry[1:2, :]), unroll=4)
        carry[0:1, :] = cf
        carry[1:2, :] = cb

    fw = lambda i: (i, 0)
    bw = lambda i: (n_tiles - 1 - i, 0)
    sds = jax.ShapeDtypeStruct((S, C), F32)
    return _rows(name, S, ts,
                 [(a_f, (ts, C), fw), (b_f, (ts, C), fw), (a_b, (ts, C), bw), (b_b, (ts, C), bw)],
                 [(sds, (ts, C), fw), (sds, (ts, C), bw)], body, scratch=[pltpu.VMEM((8, C), F32)])


def _tri_masks():
    ri = lax.broadcasted_iota(jnp.int32, (CHUNK, CHUNK), 0)
    ci = lax.broadcasted_iota(jnp.int32, (CHUNK, CHUNK), 1)
    return ri, ci


def _gdn_prep_fwd(p, prm):
    S = p.shape[0]
    ts = _tile(S, 512)

    def body(p_ref, prm_ref, o_ref):
        raw = p_ref[...]
        lane = lax.broadcasted_iota(jnp.int32, (1, 128), 1)
        g = -jnp.exp(prm_ref[0:1, :]) * _softplus(raw + prm_ref[1:2, :])
        g = jnp.where((lane >= 8) & (lane < 16), g, 0.0)
        beta = _sigmoid(raw)
        ri, ci = _tri_masks()
        lower = (ri >= ci).astype(F32)
        upper = (ri <= ci).astype(F32)
        for c in range(ts // CHUNK):
            rows = slice(c * CHUNK, (c + 1) * CHUNK)
            gch = g[rows]
            gc = jnp.where(lane < 12, _dot(lower, gch, 1, 0, HI), _dot(upper, gch, 1, 0, HI))
            o_ref[rows, :] = jnp.where(lane < 8, beta[rows], gc)

    return _rows("gdn_prep_fwd", S, ts,
                 [(p, (ts, 128), lambda i: (i, COL_BA // 128)), (prm, (8, 128), lambda i: (0, 0))],
                 [(jax.ShapeDtypeStruct((S, 128), F32), (ts, 128), lambda i: (i, 0))], body)[0]


def _bdot(a, b, ca, cb):
    return _dot(a.astype(BF16), b.astype(BF16), ca, cb)


GDN_W = GDN_H * GDN_DK
GDN_TS = 256
LOCAL_CHUNKS = 2
BWD_CHUNKS = 1


def _gdn_decay(bg_ref, gcr_ref, c, rows, r0, col, rev, ri, ci):
    beta = bg_ref[rows, col:col + 1]
    gc = bg_ref[rows, 8 + col:9 + col]
    last = 0 if rev else CHUNK - 1
    gl = bg_ref[pl.ds(r0 + last, 1), 8 + col:9 + col]
    out = dict(beta=beta, gc=gc, gl=gl, eg=jnp.exp(gc), egl=jnp.exp(gl - gc), cd=jnp.exp(gl))
    if gcr_ref is not None:
        incl = (ri <= ci) if rev else (ri >= ci)
        out["strict"] = (ri < ci) if rev else (ri > ci)
        out["dm"] = jnp.where(incl, jnp.exp(jnp.where(incl, gc - gcr_ref[c, col:col + 1, :], 0.0)), 0.0)
    return out


def _dir_tile(d, n_tiles, flip):
    if (d == 1) != flip:
        return lambda i: n_tiles - 1 - i
    return lambda i: i


def _gdn_local_fwd(q, k, v, bg, gcr):
    S = q.shape[0]
    ts = _tile(S, GDN_TS)
    ncb = ts // CHUNK
    nch = S // CHUNK

    def body(q_ref, k_ref, v_ref, bg_ref, gcr_ref, *out_refs):
        ri, ci = _tri_masks()
        eye = (ri == ci).astype(F32)
        outs = (out_refs[0:6], out_refs[6:12])
        cd_ref = out_refs[12]

        def chunk(cc, carry):
            chains = []
            for c in (LOCAL_CHUNKS * cc + j for j in range(LOCAL_CHUNKS)):
                r0 = pl.multiple_of(c * CHUNK, CHUNK)
                rows = pl.ds(r0, CHUNK)
                for h in range(GDN_H):
                    cols = slice(h * GDN_DK, (h + 1) * GDN_DK)
                    qh, kh, vh = q_ref[rows, cols], k_ref[rows, cols], v_ref[rows, cols]
                    both = _bdot(jnp.concatenate([qh, kh], axis=0), kh, 1, 1)
                    for d in range(2):
                        chains.append(dict(c=c, r0=r0, rows=rows, h=h, d=d, cols=cols, qh=qh, kh=kh, vh=vh,
                                           qk=both[0:CHUNK], kk=both[CHUNK:2 * CHUNK]))
            for ch in chains:
                m = _gdn_decay(bg_ref, gcr_ref, ch["c"], ch["rows"], ch["r0"], ch["d"] * GDN_H + ch["h"], ch["d"] == 1,
                               ri, ci)
                ch["m"] = m
                ch["x"] = -jnp.where(m["strict"], m["beta"] * ch["kk"] * m["dm"], 0.0)
                ch["t"] = eye + ch["x"]
            for ch in chains:
                ch["pw"] = _bdot(ch["x"], ch["x"], 1, 0)
            for level in range(1, 6):
                last_level = level == 5
                for ch in chains:
                    rhs = ch["t"] if last_level else jnp.concatenate([ch["t"], ch["pw"]], axis=1)
                    ch["prod"] = _bdot(ch["pw"], rhs, 1, 0)
                for ch in chains:
                    ch["t"] = ch["t"] + ch["prod"][:, 0:CHUNK]
                    if not last_level:
                        ch["pw"] = ch["prod"][:, CHUNK:2 * CHUNK]
            for ch in chains:
                m = ch["m"]
                rhs = jnp.concatenate([ch["vh"] * m["beta"], ch["kh"] * (m["beta"] * m["eg"])], axis=1)
                ch["uw"] = _bdot(ch["t"], rhs, 1, 0)
            for ch in chains:
                u_ref, w_ref, a_ref, t_ref, qd_ref, kd_ref = outs[ch["d"]]
                m = ch["m"]
                c, rows = ch["c"], ch["rows"]
                col = ch["d"] * GDN_H + ch["h"]
                u_ref[rows, ch["cols"]] = ch["uw"][:, 0:GDN_DK]
                w_ref[rows, ch["cols"]] = ch["uw"][:, GDN_DK:2 * GDN_DK].astype(BF16)
                a_ref[c, ch["h"]] = (ch["qk"] * m["dm"]).astype(BF16)
                t_ref[c, ch["h"]] = _bdot(ch["t"], eye, 0, 0).astype(BF16)
                qd_ref[rows, ch["cols"]] = (ch["qh"] * m["eg"]).astype(BF16)
                kd_ref[rows, ch["cols"]] = (ch["kh"] * m["egl"]).astype(BF16)
                cd_ref[c, col:col + 1, :] = jnp.broadcast_to(m["cd"], (1, 128))
            return carry

        lax.fori_loop(0, ncb // LOCAL_CHUNKS, chunk, 0)

    im = lambda i: (i, 0)
    im4 = lambda i: (i, 0, 0, 0)
    ins = [(q, (ts, GDN_W), im), (k, (ts, GDN_W), im), (v, (ts, GDN_W), im), (bg, (ts, 128), im),
           (gcr, (ncb, 8, CHUNK), lambda i: (i, 0, 0))]
    per_dir = [(jax.ShapeDtypeStruct((S, GDN_W), F32), (ts, GDN_W), im),
               (jax.ShapeDtypeStruct((S, GDN_W), BF16), (ts, GDN_W), im),
               (jax.ShapeDtypeStruct((nch, GDN_H, CHUNK, CHUNK), BF16), (ncb, GDN_H, CHUNK, CHUNK), im4),
               (jax.ShapeDtypeStruct((nch, GDN_H, CHUNK, CHUNK), BF16), (ncb, GDN_H, CHUNK, CHUNK), im4),
               (jax.ShapeDtypeStruct((S, GDN_W), BF16), (ts, GDN_W), im),
               (jax.ShapeDtypeStruct((S, GDN_W), BF16), (ts, GDN_W), im)]
    cd_out = (jax.ShapeDtypeStruct((nch, 8, 128), F32), (ncb, 8, 128), lambda i: (i, 0, 0))
    res = _rows("gdn_local_fwd", S, ts, ins, per_dir * 2 + [cd_out], body)
    return res[0:6], res[6:12], res[12]


def _gdn_scan_fwd(loc):
    S = loc[0][0].shape[0]
    ts = _tile(S, GDN_TS)
    n_tiles = S // ts
    ncb = ts // CHUNK
    nch = S // CHUNK

    def body(*refs):
        ins = (refs[0:6], refs[6:12])
        outs = (refs[12:15], refs[15:18])
        state = refs[18]

        @pl.when(pl.program_id(0) == 0)
        def _():
            state[...] = jnp.zeros_like(state)

        def chunk(cc, carry):
            chains = []
            for d in range(2):
                c = cc if d == 0 else ncb - 1 - cc
                rows = pl.ds(pl.multiple_of(c * CHUNK, CHUNK), CHUNK)
                for h in range(GDN_H):
                    cols = slice(h * GDN_DK, (h + 1) * GDN_DK)
                    chains.append(dict(d=d, h=h, c=c, rows=rows, cols=cols, st=state[d * GDN_H + h]))
            for ch in chains:
                qd_ref, kd_ref, u_ref, w_ref, a_ref, cd_ref = ins[ch["d"]]
                rows, cols = ch["rows"], ch["cols"]
                lhs = jnp.concatenate([w_ref[rows, cols], qd_ref[rows, cols]], axis=0)
                ch["ws_qs"] = _dot(lhs, ch["st"].astype(BF16), 1, 0)
            for ch in chains:
                qd_ref, kd_ref, u_ref, w_ref, a_ref, cd_ref = ins[ch["d"]]
                rows, cols = ch["rows"], ch["cols"]
                vn = u_ref[rows, cols] - ch["ws_qs"][0:CHUNK]
                vnb = vn.astype(BF16)
                ch["vn"] = vn
                ch["avn"] = _dot(a_ref[ch["c"], ch["h"]], vnb, 1, 0)
                ch["kvn"] = _dot(kd_ref[rows, cols], vnb, 0, 0)
            for ch in chains:
                o_ref, vn_ref, s_ref = outs[ch["d"]]
                cd_ref = ins[ch["d"]][5]
                rows, cols = ch["rows"], ch["cols"]
                col = ch["d"] * GDN_H + ch["h"]
                o_ref[rows, cols] = ch["ws_qs"][CHUNK:2 * CHUNK] + ch["avn"]
                vn_ref[rows, cols] = ch["vn"].astype(BF16)
                s_ref[ch["c"], ch["h"]] = ch["st"].astype(BF16)
                state[ch["d"] * GDN_H + ch["h"]] = ch["st"] * cd_ref[ch["c"], col:col + 1, :] + ch["kvn"]
            return carry

        lax.fori_loop(0, ncb, chunk, 0)

    ins, outs = [], []
    for d in range(2):
        tix = _dir_tile(d, n_tiles, False)
        im = lambda i, tix=tix: (tix(i), 0)
        im4 = lambda i, tix=tix: (tix(i), 0, 0, 0)
        u, w, a, _, qd, kd = loc[d]
        ins += [(qd, (ts, GDN_W), im), (kd, (ts, GDN_W), im), (u, (ts, GDN_W), im), (w, (ts, GDN_W), im),
                (a, (ncb, GDN_H, CHUNK, CHUNK), im4), (loc[2], (ncb, 8, 128), lambda i, tix=tix: (tix(i), 0, 0))]
        outs += [(jax.ShapeDtypeStruct((S, GDN_W), F32), (ts, GDN_W), im),
                 (jax.ShapeDtypeStruct((S, GDN_W), BF16), (ts, GDN_W), im),
                 (jax.ShapeDtypeStruct((nch, GDN_H, GDN_DK, GDN_DK), BF16), (ncb, GDN_H, GDN_DK, GDN_DK), im4)]
    res = _rows("gdn_scan_fwd", S, ts, ins, outs, body, scratch=[pltpu.VMEM((2 * GDN_H, GDN_DK, GDN_DK), F32)])
    return res[0:3], res[3:6]


def _gelu(x):
    c = math.sqrt(2.0 / math.pi)
    t = jnp.tanh(c * (x + 0.044715 * x * x * x))
    return 0.5 * x * (1.0 + t), t


def _mix_out_fwd(h_f, h_b, o_f, o_b, p, gn):
    S = h_f.shape[0]
    ts = _tile(S, 512)

    def body(hf, hb, of, ob, gate, z, gn_ref, y_ref):
        ge, _ = _gelu(gate[...])
        y_ref[:, 0:RG_W] = ((hf[...] + hb[...]) * ge).astype(BF16)
        o = of[...] + ob[...]
        zv = z[...]
        sz = zv * _sigmoid(zv)
        for h in range(GDN_H):
            cols = slice(h * GDN_DK, (h + 1) * GDN_DK)
            oh = o[:, cols]
            n = oh * lax.rsqrt(jnp.mean(oh * oh, axis=-1, keepdims=True) + EPS) * gn_ref[...]
            y_ref[:, RG_W + h * GDN_DK:RG_W + (h + 1) * GDN_DK] = (n * sz[:, cols]).astype(BF16)

    blk = (ts, RG_W)
    im = lambda i: (i, 0)
    ins = [(h_f, blk, im), (h_b, blk, im), (o_f, blk, im), (o_b, blk, im),
           (p, blk, lambda i: (i, 1)), (p, blk, lambda i: (i, 5)), (gn, (1, GDN_DK), lambda i: (0, 0))]
    return _rows("mix_out_fwd", S, ts, ins,
                 [(jax.ShapeDtypeStruct((S, D_MODEL), BF16), (ts, D_MODEL), im)], body)[0]


def _loss_head(x, target, g):
    S, D = x.shape
    ts = _tile(S, 512)

    def body(x_ref, t_ref, g_ref, dx_ref, loss_ref, dg_ref):
        @pl.when(pl.program_id(0) == 0)
        def _():
            loss_ref[...] = jnp.zeros_like(loss_ref)
            dg_ref[...] = jnp.zeros_like(dg_ref)

        xv = x_ref[...]
        gv = g_ref[...]
        r = lax.rsqrt(jnp.mean(xv * xv, axis=-1, keepdims=True) + EPS)
        err = xv * r * gv - t_ref[...]
        loss_ref[...] += jnp.sum(err * err) * (0.5 / D)
        dx, dgt = _rmsnorm_bwd_tile(err * (1.0 / D), xv, gv)
        dx_ref[...] = dx
        dg_ref[...] += jnp.sum(dgt, axis=0, keepdims=True)

    im = lambda i: (i, 0)
    z = lambda i: (0, 0)
    return _rows("loss_head", S, ts,
                 [(x, (ts, D), im), (target, (ts, D), im), (g, (1, D), z)],
                 [(jax.ShapeDtypeStruct((S, D), F32), (ts, D), im),
                  (jax.ShapeDtypeStruct((8, 128), F32), (8, 128), z),
                  (jax.ShapeDtypeStruct((1, D), F32), (1, D), z)], body)


def _block_diag(w):
    n = w.shape[0]
    return jnp.einsum("nij,nm->nimj", w, jnp.eye(n, dtype=w.dtype)).reshape(n * w.shape[1], n * w.shape[2])


def _rg_bd(a_w, x_w):
    return jnp.concatenate([_block_diag(a_w[0]), _block_diag(x_w[0]), _block_diag(a_w[1]), _block_diag(x_w[1])],
                           axis=1).astype(BF16)


def _rg_prm(ba, bx, lam):
    return jnp.concatenate([ba[0:1], bx[0:1], ba[1:2], bx[1:2], lam, jnp.zeros((2, RG_W), F32)], axis=0)


def _gdn_prm(a_log, dt_bias):
    rows = jnp.zeros((8, 128), F32)
    rows = rows.at[0, 8:16].set(a_log.reshape(-1))
    return rows.at[1, 8:16].set(dt_bias.reshape(-1))


def _gc_rows(bg):
    S = bg.shape[0]
    return bg[:, 8:16].reshape(S // CHUNK, CHUNK, 8).transpose(0, 2, 1)


def _layer_fwd(x0, target, W, more=None):
    S = x0.shape[0]
    R = {}
    R["h1"] = _rmsnorm_fwd("rms1", x0, W["ffn1_norm"])
    late_wd = {}

    def ffn1_wd(after):
        late_wd.update(more("ffn1_down", after))
        return late_wd["ffn1_w_down"]

    R["x1"], R["a1"], R["b1"], R["f1"] = _ffn_fwd("ffn1", x0, R["h1"], W["ffn1_w_gate"], W["ffn1_w_up"],
                                                  ffn1_wd if more is not None else W["ffn1_w_down"])
    if more is not None:
        W = {**W, **late_wd, **more("mixer", R["x1"])}
    R["h2"] = _rmsnorm_fwd("rms2", R["x1"], W["mix_norm"])
    tm = _tile(S, 512)
    tmp = _tile(S, 1024)
    R["p"] = _fused_mm("in_proj", S, D_IN_PAD, D_MODEL, tmp, 640, D_MODEL, [(R["h2"], "mk"), (W["w_in"], "kn")],
                       [(0, 1, 0)], [], [(jax.ShapeDtypeStruct((S, D_IN_PAD), F32), (tmp, 640), _mn)],
                       lambda i, accs, ex, out: out[0].__setitem__(Ellipsis, accs[0][...]))[0]
    p = R["p"]
    R["xc"] = _conv_fwd("rg_conv_fwd", p, 0, W["rg_conv_w"], W["rg_conv_b"], "bias")
    R["bd"] = _rg_bd(W["rg_gate_a_w"], W["rg_gate_x_w"])
    R["rg_prm"] = _rg_prm(W["rg_gate_a_b"], W["rg_gate_x_b"], W["rg_lambda"])
    a_f, b_f, a_b, b_b = _rg_gates_fwd(R["xc"], R["bd"], R["rg_prm"])
    R["a_f"], R["a_b"] = a_f, a_b
    R["h_f"], R["h_b"] = _rg_scan("rg_scan_fwd", a_f, b_f, a_b, b_b)
    zero_b = jnp.zeros((1, RG_W), F32)
    cw = W["gdn_conv_w"]
    R["q"] = _conv_fwd("gdn_conv_q", p, 2, cw[:, 0:512], zero_b, "q")
    R["k"] = _conv_fwd("gdn_conv_k", p, 3, cw[:, 512:1024], zero_b, "k")
    R["v"] = _conv_fwd("gdn_conv_v", p, 4, cw[:, 1024:1536], zero_b, "v")
    R["gdn_prm"] = _gdn_prm(W["gdn_a_log"], W["gdn_dt_bias"])
    R["bg"] = _gdn_prep_fwd(p, R["gdn_prm"])
    R["gcr"] = _gc_rows(R["bg"])
    R["gdn_loc"] = _gdn_local_fwd(R["q"], R["k"], R["v"], R["bg"], R["gcr"])
    R["gdn_fwd"] = _gdn_scan_fwd(R["gdn_loc"])
    R["o_f"], R["o_b"] = R["gdn_fwd"][0][0], R["gdn_fwd"][1][0]
    R["y"] = _mix_out_fwd(R["h_f"], R["h_b"], R["o_f"], R["o_b"], p, W["gdn_norm"])
    R["x2"] = _fused_mm("out_proj", S, D_MODEL, D_MODEL, tm, D_MODEL, D_MODEL, [(R["y"], "mk"), (W["w_out"], "kn")],
                        [(0, 1, 0)], [(R["x1"], (tm, D_MODEL), _mn)],
                        [(jax.ShapeDtypeStruct((S, D_MODEL), F32), (tm, D_MODEL), _mn)],
                        lambda i, accs, ex, out: out[0].__setitem__(Ellipsis, ex[0][...] + accs[0][...]))[0]
    if more is not None:
        W = {**W, **more("ffn2", R["x2"])}
    R["h3"] = _rmsnorm_fwd("rms3", R["x2"], W["ffn2_norm"])
    R["x3"], R["a2"], R["b2"], R["f2"] = _ffn_fwd("ffn2", R["x2"], R["h3"], W["ffn2_w_gate"], W["ffn2_w_up"], W["ffn2_w_down"])
    R["dx3"], R["loss"], R["d_final_norm"] = _loss_head(R["x3"], target, W["final_norm"])
    R["W"] = W
    return R


def _colsum_into(ref, i, val):
    @pl.when(i == 0)
    def _():
        ref[...] = val

    @pl.when(i > 0)
    def _():
        ref[...] += val


def _ffn_bwd(tag, dout, x, g, h, a, b, f, wg, wu, wd, emit):
    S = x.shape[0]
    tm = _tile(S, 512)
    tk_s = _tile(S, 1024)
    dwd = _fused_mm(f"{tag}_dw_down", D_FF, D_MODEL, S, 1408, D_MODEL, tk_s, [(f, "km"), (dout, "kn")], [(0, 1, 0)], [],
                    [(jax.ShapeDtypeStruct((D_FF, D_MODEL), BF16), (1408, D_MODEL), _mn)],
                    lambda i, accs, ex, out: out[0].__setitem__(Ellipsis, (0.5 * accs[0][...]).astype(BF16)))[0]
    emit(down=dwd)

    def epi_act(i, accs, ex, out):
        df = 0.5 * accs[0][...]
        out[0][...] = (df * ex[1][...].astype(F32)).astype(BF16)
        out[1][...] = (df * ex[0][...].astype(F32)).astype(BF16)

    sds = jax.ShapeDtypeStruct((S, D_FF), BF16)
    da, db = _fused_mm(f"{tag}_dact", S, D_FF, D_MODEL, tm, 1408, D_MODEL, [(dout, "mk"), (wd, "nk")], [(0, 1, 0)],
                       [(a, (tm, 1408), _mn), (b, (tm, 1408), _mn)], [(sds, (tm, 1408), _mn)] * 2, epi_act)

    def epi_w2(i, accs, ex, out):
        out[0][...] = accs[0][...].astype(BF16)
        out[1][...] = accs[1][...].astype(BF16)

    sdw = jax.ShapeDtypeStruct((D_MODEL, D_FF), BF16)
    dwg, dwu = _fused_mm(f"{tag}_dw_up", D_MODEL, D_FF, S, D_MODEL, 1408, _tile(S, 512),
                         [(h, "km"), (da, "kn"), (db, "kn")], [(0, 1, 0), (0, 2, 1)], [],
                         [(sdw, (D_MODEL, 1408), _mn)] * 2, epi_w2)
    tok = emit(gate=dwg, up=dwu)
    if tok is not None:
        g = g + tok

    def epi_dx(i, accs, ex, out):
        dx, dgt = _rmsnorm_bwd_tile(accs[0][...], ex[0][...], ex[1][...])
        out[0][...] = ex[2][...] + dx
        _colsum_into(out[1], i, jnp.sum(dgt, axis=0, keepdims=True))

    dx, dg = _fused_mm(f"{tag}_dx", S, D_MODEL, D_FF, tm, D_MODEL, 1408,
                       [(da, "mk"), (wg, "nk"), (db, "mk"), (wu, "nk")], [(0, 1, 0), (2, 3, 0)],
                       [(x, (tm, D_MODEL), _mn), (g, (1, D_MODEL), _row0), (dout, (tm, D_MODEL), _mn)],
                       [(jax.ShapeDtypeStruct((S, D_MODEL), F32), (tm, D_MODEL), _mn),
                        (jax.ShapeDtypeStruct((1, D_MODEL), F32), (1, D_MODEL), _row0)], epi_dx)
    return dx, dg


def _mix_out_bwd(dy, h_f, h_b, o_f, o_b, p, gn):
    S = dy.shape[0]
    ts = _tile(S, 512)
    c0 = math.sqrt(2.0 / math.pi)

    def body(dy_ref, hf, hb, of, ob, gate, z, gn_ref, dhr_ref, dgate_ref, do_ref, dz_ref, dgn_ref):
        i = pl.program_id(0)
        gv = gate[...]
        ge, t = _gelu(gv)
        dy_rg = dy_ref[:, 0:RG_W]
        dhr_ref[...] = dy_rg * ge
        dgelu = 0.5 * (1.0 + t) + 0.5 * gv * (1.0 - t * t) * c0 * (1.0 + 3.0 * 0.044715 * gv * gv)
        dgate_ref[...] = (dy_rg * (hf[...] + hb[...]) * dgelu).astype(BF16)
        o = of[...] + ob[...]
        zv = z[...]
        sig = _sigmoid(zv)
        gnv = gn_ref[...]
        dgn = jnp.zeros((1, GDN_DK), F32)
        for h in range(GDN_H):
            cols = slice(h * GDN_DK, (h + 1) * GDN_DK)
            oh = o[:, cols]
            r = lax.rsqrt(jnp.mean(oh * oh, axis=-1, keepdims=True) + EPS)
            ohat = oh * r
            dyh = dy_ref[:, RG_W + h * GDN_DK:RG_W + (h + 1) * GDN_DK]
            zh = zv[:, cols]
            sh = sig[:, cols]
            dn = dyh * zh * sh
            dz_ref[:, cols] = (dyh * ohat * gnv * (sh * (1.0 + zh * (1.0 - sh)))).astype(BF16)
            dxn = dn * gnv
            do_ref[:, cols] = r * (dxn - ohat * jnp.mean(dxn * ohat, axis=-1, keepdims=True))
            dgn = dgn + jnp.sum(dn * ohat, axis=0, keepdims=True)
        _colsum_into(dgn_ref, i, dgn)

    blk = (ts, RG_W)
    im = lambda i: (i, 0)
    z0 = lambda i: (0, 0)
    ins = [(dy, (ts, D_MODEL), im), (h_f, blk, im), (h_b, blk, im), (o_f, blk, im), (o_b, blk, im),
           (p, blk, lambda i: (i, 1)), (p, blk, lambda i: (i, 5)), (gn, (1, GDN_DK), z0)]
    outs = [(jax.ShapeDtypeStruct((S, RG_W), F32), blk, im), (jax.ShapeDtypeStruct((S, RG_W), BF16), blk, im),
            (jax.ShapeDtypeStruct((S, RG_W), F32), blk, im), (jax.ShapeDtypeStruct((S, RG_W), BF16), blk, im),
            (jax.ShapeDtypeStruct((1, GDN_DK), F32), (1, GDN_DK), z0)]
    return _rows("mix_out_bwd", S, ts, ins, outs, body)


def _rg_scan_adj(name, a_up, b_up, a_dn, b_dn):
    S, C = a_up.shape
    ts = _tile(S, 512)
    n_tiles = S // ts

    def body(au, bu, ad, bd, mu_ref, lam_ref, carry):
        @pl.when(pl.program_id(0) == 0)
        def _():
            carry[...] = jnp.zeros_like(carry)

        n_sub = ts // SUBLANES
        rows = lax.broadcasted_iota(jnp.int32, (SUBLANES, C), 0)

        def half(a_ref, b_ref, out_ref, r0, c_in, reverse):
            a = a_ref[pl.ds(r0, SUBLANES), :]
            b = b_ref[pl.ds(r0, SUBLANES), :]
            cum_a, c0 = _scan_rows(a, a * b, reverse)
            c = c0 + cum_a * c_in
            edge = 0 if not reverse else SUBLANES - 1
            c_prev = jnp.where(rows == edge, c_in, pltpu.roll(c, SUBLANES - 1 if reverse else 1, 0))
            out_ref[pl.ds(r0, SUBLANES), :] = b + c_prev
            return c[0:1, :] if reverse else c[SUBLANES - 1:SUBLANES, :]

        def step(j, c):
            cu, cd = c
            cu = half(au, bu, mu_ref, pl.multiple_of(j * SUBLANES, SUBLANES), cu, False)
            cd = half(ad, bd, lam_ref, pl.multiple_of((n_sub - 1 - j) * SUBLANES, SUBLANES), cd, True)
            return cu, cd

        cu, cd = lax.fori_loop(0, n_sub, step, (carry[0:1, :], carry[1:2, :]), unroll=4)
        carry[0:1, :] = cu
        carry[1:2, :] = cd

    fw = lambda i: (i, 0)
    bw = lambda i: (n_tiles - 1 - i, 0)
    sds = jax.ShapeDtypeStruct((S, C), F32)
    return _rows(name, S, ts,
                 [(a_up, (ts, C), fw), (b_up, (ts, C), fw), (a_dn, (ts, C), bw), (b_dn, (ts, C), bw)],
                 [(sds, (ts, C), fw), (sds, (ts, C), bw)], body, scratch=[pltpu.VMEM((8, C), F32)])


def _halo_ex(arr, S, tm, width):
    per = tm // HALO
    last = S // HALO - 1
    return [
        (arr, (tm, width), lambda i, j: (i, 0)),
        (arr, (HALO, width), lambda i, j: (jnp.maximum(i * per - 1, 0), 0)),
        (arr, (HALO, width), lambda i, j: (jnp.minimum((i + 1) * per, last), 0)),
    ]


def _rg_gates_bwd(xc, bd, prm, lam_f, lam_b, h_f, h_b):
    S = xc.shape[0]
    tm = _tile(S, 256)
    n_tiles = S // tm

    def epi(i, accs, ex, out):
        pre = accs[0][...]
        xv = ex[0][...]
        prm_ref = ex[1]
        lams = (ex[2][...], ex[3][...])
        hprev = (_shift(_ext(ex[4], ex[5], ex[6], i, n_tiles), -1, tm),
                 _shift(_ext(ex[7], ex[8], ex[9], i, n_tiles), 1, tm))
        dxc = jnp.zeros_like(xv)
        rows = []
        dlam_rows = []
        for d in range(2):
            r, ig, sp, a, sq = _rg_gate_terms(pre, xv, prm_ref, d)
            lam = lams[d]
            da = lam * hprev[d]
            di = lam * sq * xv
            dxc = dxc + lam * sq * ig
            dsq = lam * ig * xv
            dlog_a = da * a - dsq * (a * a) / sq
            dpre_r = dlog_a * (-RG_C * sp) * r * (1.0 - r)
            dpre_i = di * ig * (1.0 - ig)
            out[0][:, d * 1024:d * 1024 + RG_W] = dpre_r.astype(BF16)
            out[0][:, d * 1024 + RG_W:(d + 1) * 1024] = dpre_i.astype(BF16)
            rows += [jnp.sum(dpre_r, axis=0, keepdims=True), jnp.sum(dpre_i, axis=0, keepdims=True)]
            dsp = jnp.sum(dlog_a * (-RG_C * r), axis=0, keepdims=True)
            dlam_rows.append(-dsp * _sigmoid(-prm_ref[4 + d:5 + d, :]))
        out[1][...] = dxc
        zero = jnp.zeros((2, RG_W), F32)
        _colsum_into(out[2], i, jnp.concatenate(rows + dlam_rows + [zero], axis=0))

    blk = (tm, RG_W)
    im = lambda i, j: (i, 0)
    extras = ([(xc, blk, im), (prm, (8, RG_W), _row0), (lam_f, blk, im), (lam_b, blk, im)]
              + _halo_ex(h_f, S, tm, RG_W) + _halo_ex(h_b, S, tm, RG_W))
    outs = [(jax.ShapeDtypeStruct((S, 4 * RG_W), BF16), (tm, 4 * RG_W), im),
            (jax.ShapeDtypeStruct((S, RG_W), F32), blk, im),
            (jax.ShapeDtypeStruct((8, RG_W), F32), (8, RG_W), _row0)]
    return _fused_mm("rg_gates_bwd", S, 4 * RG_W, RG_W, tm, 4 * RG_W, RG_W, [(xc, "mk"), (bd, "kn")], [(0, 1, 0)],
                     extras, outs, epi)


def _roll_rows(ext, off):
    if off == 0:
        return ext
    return pltpu.roll(ext, (-off) % ext.shape[0], 0)


def _conv_bwd(name, p, colblk, w, grads, mode):
    S = p.shape[0]
    ts = _tile(S, 512)
    n_tiles = S // ts
    C = w.shape[1]
    ng = len(grads)

    def body(*refs):
        p_refs = refs[0:3]
        g_refs = refs[3:3 + 3 * ng]
        w_ref = refs[3 + 3 * ng]
        dx_ref, dw_ref, db_ref = refs[4 + 3 * ng:]
        i = pl.program_id(0)
        ext_p = _ext(*p_refs, i, n_tiles)
        dn = _ext(*g_refs[0:3], i, n_tiles)
        for gi in range(1, ng):
            dn = dn + _ext(*g_refs[3 * gi:3 * gi + 3], i, n_tiles)
        if mode == "bias":
            dc = dn
        else:
            c = None
            for j in range(CONV_W):
                term = w_ref[j:j + 1, :] * _roll_rows(ext_p, j - 2)
                c = term if c is None else c + term
            sig = _sigmoid(c)
            s = c * sig
            if mode in ("q", "k"):
                scale = GDN_DK ** -0.5 if mode == "q" else 1.0
                parts = []
                for h in range(GDN_H):
                    cols = slice(h * GDN_DK, (h + 1) * GDN_DK)
                    sh = s[:, cols]
                    dnh = dn[:, cols]
                    rinv = lax.rsqrt(jnp.sum(sh * sh, axis=-1, keepdims=True) + EPS)
                    parts.append(scale * rinv * (dnh - sh * (rinv * rinv) * jnp.sum(dnh * sh, axis=-1, keepdims=True)))
                ds = jnp.concatenate(parts, axis=-1)
            else:
                ds = dn
            dc = ds * (sig * (1.0 + c * (1.0 - sig)))
        dx = None
        for j in range(CONV_W):
            term = w_ref[j:j + 1, :] * _shift(dc, 2 - j, ts)
            dx = term if dx is None else dx + term
        dx_ref[...] = dx.astype(BF16)
        dc_main = dc[HALO:HALO + ts]
        dw = jnp.concatenate([jnp.sum(dc_main * _shift(ext_p, j - 2, ts), axis=0, keepdims=True)
                              for j in range(CONV_W)], axis=0)
        _colsum_into(dw_ref, i, dw)
        _colsum_into(db_ref, i, jnp.sum(dc_main, axis=0, keepdims=True))

    ins = _halo_ins(p, S, ts, C, colblk)
    for garr in grads:
        ins += _halo_ins(garr, S, ts, C, 0)
    ins += [(w, (CONV_W, C), lambda i: (0, 0))]
    z0 = lambda i: (0, 0)
    outs = [(jax.ShapeDtypeStruct((S, C), BF16), (ts, C), lambda i: (i, 0)),
            (jax.ShapeDtypeStruct((CONV_W, C), F32), (CONV_W, C), z0),
            (jax.ShapeDtypeStruct((1, C), F32), (1, C), z0)]
    return _rows(name, S, ts, ins, outs, body)


def _gdn_scan_bwd(loc, do):
    S = do.shape[0]
    ts = _tile(S, GDN_TS)
    n_tiles = S // ts
    ncb = ts // CHUNK
    nch = S // CHUNK

    def body(*refs):
        ins = (refs[0:6], refs[6:12])
        outs = (refs[12:14], refs[14:16])
        dstate = refs[16]

        @pl.when(pl.program_id(0) == 0)
        def _():
            dstate[...] = jnp.zeros_like(dstate)

        def chunk(cc, carry):
            chains = []
            for d in range(2):
                c = ncb - 1 - cc if d == 0 else cc
                rows = pl.ds(pl.multiple_of(c * CHUNK, CHUNK), CHUNK)
                for h in range(GDN_H):
                    cols = slice(h * GDN_DK, (h + 1) * GDN_DK)
                    chains.append(dict(d=d, h=h, c=c, rows=rows, cols=cols, dsn=dstate[d * GDN_H + h]))
            for ch in chains:
                qd_ref, kd_ref, cd_ref, w_ref, a_ref, do_ref = ins[ch["d"]]
                rows, cols = ch["rows"], ch["cols"]
                dob = do_ref[rows, cols].astype(BF16)
                ch["dvn"] = (_dot(a_ref[ch["c"], ch["h"]], dob, 0, 0)
                             + _dot(kd_ref[rows, cols], ch["dsn"].astype(BF16), 1, 0))
                ch["qdo"] = _dot(qd_ref[rows, cols], dob, 0, 0)
            for ch in chains:
                w_ref = ins[ch["d"]][3]
                ch["wdvn"] = _dot(w_ref[ch["rows"], ch["cols"]], ch["dvn"].astype(BF16), 0, 0)
            for ch in chains:
                dvn_ref, ds_ref = outs[ch["d"]]
                cd_ref = ins[ch["d"]][2]
                col = ch["d"] * GDN_H + ch["h"]
                dvn_ref[ch["rows"], ch["cols"]] = ch["dvn"].astype(BF16)
                ds_ref[ch["c"], ch["h"]] = ch["dsn"].astype(BF16)
                dstate[ch["d"] * GDN_H + ch["h"]] = (ch["qdo"] + cd_ref[ch["c"], col:col + 1, :] * ch["dsn"]
                                                     - ch["wdvn"])
            return carry

        lax.fori_loop(0, ncb, chunk, 0)

    ins, outs = [], []
    for d in range(2):
        tix = _dir_tile(d, n_tiles, True)
        im = lambda i, tix=tix: (tix(i), 0)
        im4 = lambda i, tix=tix: (tix(i), 0, 0, 0)
        _, w, a, _, qd, kd = loc[d]
        ins += [(qd, (ts, GDN_W), im), (kd, (ts, GDN_W), im), (loc[2], (ncb, 8, 128), lambda i, tix=tix: (tix(i), 0, 0)),
                (w, (ts, GDN_W), im), (a, (ncb, GDN_H, CHUNK, CHUNK), im4), (do, (ts, GDN_W), im)]
        outs += [(jax.ShapeDtypeStruct((S, GDN_W), BF16), (ts, GDN_W), im),
                 (jax.ShapeDtypeStruct((nch, GDN_H, GDN_DK, GDN_DK), BF16), (ncb, GDN_H, GDN_DK, GDN_DK), im4)]
    res = _rows("gdn_scan_bwd", S, ts, ins, outs, body, scratch=[pltpu.VMEM((2 * GDN_H, GDN_DK, GDN_DK), F32)])
    return res[0:2], res[2:4]


def _gdn_local_bwd(q, k, v, bg, gcr, do, loc, fwd, adj):
    S = q.shape[0]
    ts = _tile(S, GDN_TS)
    ncb = ts // CHUNK

    def body(q_ref, k_ref, v_ref, bg_ref, gcr_ref, do_ref, *rest):
        per_dir = (rest[0:5], rest[5:10])
        dq_ref, dk_ref, dv_ref, dbg_ref, dbgr_ref = rest[10:15]
        ri, ci = _tri_masks()
        lane = lax.broadcasted_iota(jnp.int32, (CHUNK, 128), 1)
        rowi = lax.broadcasted_iota(jnp.int32, (CHUNK, 1), 0)
        ones8 = jnp.ones((SUBLANES, CHUNK), F32)

        def chunk(cc, carry):
            chains = []
            for c in (BWD_CHUNKS * cc + j for j in range(BWD_CHUNKS)):
                r0 = pl.multiple_of(c * CHUNK, CHUNK)
                rows = pl.ds(r0, CHUNK)
                for h in range(GDN_H):
                    cols = slice(h * GDN_DK, (h + 1) * GDN_DK)
                    qh, kh, vh = q_ref[rows, cols], k_ref[rows, cols], v_ref[rows, cols]
                    dob = do_ref[rows, cols].astype(BF16)
                    both = _bdot(jnp.concatenate([qh, kh], axis=0), kh, 1, 1)
                    for d in range(2):
                        chains.append(dict(c=c, r0=r0, rows=rows, h=h, d=d, cols=cols, qh=qh, kh=kh, vh=vh, dob=dob,
                                           qk=both[0:CHUNK], kk=both[CHUNK:2 * CHUNK], col=d * GDN_H + h))
            for ch in chains:
                c, rows = ch["c"], ch["rows"]
                m = _gdn_decay(bg_ref, gcr_ref, c, rows, ch["r0"], ch["col"], ch["d"] == 1, ri, ci)
                t_ref, s_ref, ds_ref, vn_ref, dvn_ref = per_dir[ch["d"]]
                h, cols = ch["h"], ch["cols"]
                ch["m"] = m
                ch["kb"] = ch["kh"] * m["beta"]
                ch["kbg"] = ch["kb"] * m["eg"]
                ch["t"] = t_ref[c, h]
                stb = s_ref[c, h]
                ch["dsn"] = ds_ref[c, h]
                vnb = vn_ref[rows, cols]
                dvnb = dvn_ref[rows, cols]
                ch["dcd"] = jnp.sum(jnp.sum(stb.astype(F32) * ch["dsn"].astype(F32), axis=1, keepdims=True),
                                    axis=0, keepdims=True)
                ch["dqd"] = _dot(ch["dob"], stb, 1, 1)
                ch["d_a"] = _dot(ch["dob"], vnb, 1, 1)
                ch["dkd"] = _bdot(vnb, ch["dsn"], 1, 1)
                ch["dw"] = -_dot(dvnb, stb, 1, 1)
                ch["dvb"] = _dot(ch["t"], dvnb, 1, 0)
                ch["d_t"] = _bdot(dvnb, ch["vh"] * m["beta"], 1, 1)
            for ch in chains:
                dwb = ch["dw"].astype(BF16)
                ch["d_t"] = ch["d_t"] + _bdot(dwb, ch["kbg"], 1, 1)
                ch["dkbg"] = _dot(ch["t"], dwb, 1, 0)
                ch["nn"] = ch["d_a"] * ch["m"]["dm"]
                ch["nn_q"] = _bdot(ch["nn"], ch["qh"], 0, 0)
                ch["nn_k"] = _bdot(ch["nn"], ch["kh"], 1, 0)
            for ch in chains:
                ch["x"] = _dot(ch["d_t"].astype(BF16), ch["t"], 1, 0)
            for ch in chains:
                d_l = -_dot(ch["t"], ch["x"].astype(BF16), 1, 0)
                ch["d_l"] = jnp.where(ch["m"]["strict"], d_l, 0.0)
                ch["mm"] = ch["d_l"] * ch["m"]["dm"]
            for ch in chains:
                m = ch["m"]
                ch["mm_kh"] = _bdot(ch["mm"], ch["kh"], 1, 0)
                ch["mm_kb"] = _bdot(ch["mm"], ch["kb"], 0, 0)
                l_mat = jnp.where(m["strict"], m["beta"] * ch["kk"] * m["dm"], 0.0)
                ch["e"] = ch["d_l"] * l_mat + ch["nn"] * ch["qk"]
                dbgr_ref[ch["c"], ch["col"]:ch["col"] + 1, :] = -_dot(ones8, ch["e"], 1, 0, HI)[0:1, :]
            acc_bg = None
            acc = {}
            for n_done, ch in enumerate(chains):
                m = ch["m"]
                rows = ch["rows"]
                if n_done % (2 * GDN_H) == 0:
                    acc_bg = jnp.zeros((CHUNK, 128), F32)
                beta, eg, egl = m["beta"], m["eg"], m["egl"]
                dkb = ch["mm_kh"] + ch["dkbg"] * eg
                dk_d = ch["mm_kb"] + ch["nn_q"] + ch["dkd"] * egl + dkb * beta
                dq_d = ch["nn_k"] + ch["dqd"] * eg
                dv_d = ch["dvb"] * beta
                dkd_kd = ch["dkd"] * (ch["kh"] * egl)
                dgc = (jnp.sum(ch["e"], axis=1, keepdims=True)
                       + jnp.sum(ch["dqd"] * (ch["qh"] * eg) - dkd_kd + ch["dkbg"] * ch["kbg"], axis=1, keepdims=True))
                dgl = jnp.sum(jnp.sum(dkd_kd, axis=1, keepdims=True), axis=0, keepdims=True) + ch["dcd"] * m["cd"]
                dgc = dgc + jnp.where(rowi == (0 if ch["d"] == 1 else CHUNK - 1), dgl, 0.0)
                dbeta = jnp.sum(dkb * ch["kh"] + ch["dvb"] * ch["vh"], axis=1, keepdims=True)
                acc_bg = acc_bg + jnp.where(lane == ch["col"], dbeta, 0.0) + jnp.where(lane == 8 + ch["col"], dgc, 0.0)
                if ch["d"] == 0:
                    acc[ch["h"]] = (dq_d, dk_d, dv_d)
                else:
                    dq0, dk0, dv0 = acc[ch["h"]]
                    dq_ref[rows, ch["cols"]] = dq0 + dq_d
                    dk_ref[rows, ch["cols"]] = dk0 + dk_d
                    dv_ref[rows, ch["cols"]] = dv0 + dv_d
                if n_done % (2 * GDN_H) == 2 * GDN_H - 1:
                    dbg_ref[rows, :] = acc_bg
            return carry

        lax.fori_loop(0, ncb // BWD_CHUNKS, chunk, 0)

    im = lambda i: (i, 0)
    im4 = lambda i: (i, 0, 0, 0)
    blk = (ts, GDN_W)
    ins = [(q, blk, im), (k, blk, im), (v, blk, im), (bg, (ts, 128), im), (gcr, (ncb, 8, CHUNK), lambda i: (i, 0, 0)),
           (do, blk, im)]
    for d in range(2):
        ins += [(loc[d][3], (ncb, GDN_H, CHUNK, CHUNK), im4), (fwd[d][2], (ncb, GDN_H, GDN_DK, GDN_DK), im4),
                (adj[d][1], (ncb, GDN_H, GDN_DK, GDN_DK), im4), (fwd[d][1], blk, im), (adj[d][0], blk, im)]
    sds = jax.ShapeDtypeStruct((S, GDN_W), F32)
    outs = [(sds, blk, im), (sds, blk, im), (sds, blk, im), (jax.ShapeDtypeStruct((S, 128), F32), (ts, 128), im),
            (jax.ShapeDtypeStruct((S // CHUNK, 8, CHUNK), F32), (ncb, 8, CHUNK), lambda i: (i, 0, 0))]
    dq, dk, dv, dbg, dbg_rows = _rows("gdn_local_bwd", S, ts, ins, outs, body)
    dgc_cols = dbg_rows.transpose(0, 2, 1).reshape(S, 8)
    return dq, dk, dv, dbg + jnp.pad(dgc_cols, ((0, 0), (8, 112)))


def _gdn_prep_bwd(dbg_all, p, prm):
    S = p.shape[0]
    ts = _tile(S, 512)

    def body(dbg_ref, p_ref, prm_ref, dba_ref, dprm_ref):
        i = pl.program_id(0)
        raw = p_ref[...]
        dbg = dbg_ref[...]
        lane = lax.broadcasted_iota(jnp.int32, (1, 128), 1)
        is_g = (lane >= 8) & (lane < 16)
        ea = jnp.exp(prm_ref[0:1, :])
        arg = raw + prm_ref[1:2, :]
        g = jnp.where(is_g, -ea * _softplus(arg), 0.0)
        beta = _sigmoid(raw)
        dgc = jnp.where(is_g, dbg, 0.0)
        ri, ci = _tri_masks()
        lower = (ri >= ci).astype(F32)
        upper = (ri <= ci).astype(F32)
        dgs = []
        for c in range(ts // CHUNK):
            ch = dgc[c * CHUNK:(c + 1) * CHUNK]
            dgs.append(jnp.where(lane < 12, _dot(upper, ch, 1, 0, HI), _dot(lower, ch, 1, 0, HI)))
        dg = jnp.concatenate(dgs, axis=0)
        dalpha = jnp.where(is_g, dg * (-ea) * _sigmoid(arg), 0.0)
        dba_ref[...] = jnp.where(lane < 8, dbg * beta * (1.0 - beta), dalpha).astype(BF16)
        rows = jnp.concatenate([jnp.sum(dg * g, axis=0, keepdims=True), jnp.sum(dalpha, axis=0, keepdims=True),
                                jnp.zeros((6, 128), F32)], axis=0)
        _colsum_into(dprm_ref, i, rows)

    im = lambda i: (i, 0)
    z0 = lambda i: (0, 0)
    return _rows("gdn_prep_bwd", S, ts,
                 [(dbg_all, (ts, 128), im), (p, (ts, 128), lambda i: (i, COL_BA // 128)), (prm, (8, 128), z0)],
                 [(jax.ShapeDtypeStruct((S, 128), BF16), (ts, 128), im), (jax.ShapeDtypeStruct((8, 128), F32), (8, 128), z0)],
                 body)


def _mm_plain(name, M, N, K, tm, tn, tk, a, am, b, bm, dtype):
    return _fused_mm(name, M, N, K, tm, tn, tk, [(a, am), (b, bm)], [(0, 1, 0)], [],
                     [(jax.ShapeDtypeStruct((M, N), dtype), (tm, tn), _mn)],
                     lambda i, accs, ex, out: out[0].__setitem__(Ellipsis, accs[0][...].astype(dtype)))[0]


def _layer_bwd(x0, W, R, emit_big=None, emit_small=None):
    S = x0.shape[0]
    tm = _tile(S, 512)
    tk_s = _tile(S, 1024)
    G = {}

    def emit(**named):
        if emit_big is None:
            G.update(named)
            return None
        return emit_big(**named)

    def ffn_emit(prefix):
        return lambda **kw: emit(**{f"{prefix}_w_{k}": v for k, v in kw.items()})

    dx2, G["ffn2_norm"] = _ffn_bwd("ffn2b", R["dx3"], R["x2"], W["ffn2_norm"], R["h3"], R["a2"], R["b2"], R["f2"],
                                   W["ffn2_w_gate"], W["ffn2_w_up"], W["ffn2_w_down"], ffn_emit("ffn2"))
    tok = emit(w_out=_mm_plain("dw_out", D_MODEL, D_MODEL, S, D_MODEL, D_MODEL, tk_s, R["y"], "km", dx2, "kn", BF16))
    gn = W["gdn_norm"] if tok is None else W["gdn_norm"] + tok
    dy = _mm_plain("dy_mix", S, D_MODEL, D_MODEL, tm, D_MODEL, D_MODEL, dx2, "mk", W["w_out"], "nk", F32)
    p = R["p"]
    dhr, dgate, do, dz, G["gdn_norm"] = _mix_out_bwd(dy, R["h_f"], R["h_b"], R["o_f"], R["o_b"], p, gn)
    lam_b, lam_f = _rg_scan_adj("rg_scan_bwd", R["a_b"], dhr, R["a_f"], dhr)
    dpre, dxc_direct, d_rgprm = _rg_gates_bwd(R["xc"], R["bd"], R["rg_prm"], lam_f, lam_b, R["h_f"], R["h_b"])
    tmg = _tile(S, 512)
    dxc = _fused_mm("rg_dxc", S, RG_W, 4 * RG_W, tmg, RG_W, 4 * RG_W, [(dpre, "mk"), (R["bd"], "nk")], [(0, 1, 0)],
                    [(dxc_direct, (tmg, RG_W), _mn)], [(jax.ShapeDtypeStruct((S, RG_W), F32), (tmg, RG_W), _mn)],
                    lambda i, accs, ex, out: out[0].__setitem__(Ellipsis, ex[0][...] + accs[0][...]))[0]
    d_bd = _mm_plain("rg_dbd", RG_W, 4 * RG_W, S, RG_W, 4 * RG_W, tk_s, R["xc"], "km", dpre, "kn", F32)
    dx_rg, G["rg_conv_w"], G["rg_conv_b"] = _conv_bwd("rg_conv_bwd", p, 0, W["rg_conv_w"], [dxc], "bias")
    blocks = jnp.einsum("nigmj,nm->gnij", d_bd.reshape(RG_BLOCKS, RG_BLOCK, 4, RG_BLOCKS, RG_BLOCK),
                        jnp.eye(RG_BLOCKS, dtype=F32))
    G["rg_gate_a_w"] = jnp.stack([blocks[0], blocks[2]])
    G["rg_gate_x_w"] = jnp.stack([blocks[1], blocks[3]])
    G["rg_gate_a_b"] = jnp.stack([d_rgprm[0], d_rgprm[2]])
    G["rg_gate_x_b"] = jnp.stack([d_rgprm[1], d_rgprm[3]])
    G["rg_lambda"] = d_rgprm[4:6]
    adj = _gdn_scan_bwd(R["gdn_loc"], do)
    dq, dk, dv, dbg = _gdn_local_bwd(R["q"], R["k"], R["v"], R["bg"], R["gcr"], do, R["gdn_loc"], R["gdn_fwd"], adj)
    cw = W["gdn_conv_w"]
    dpq, dwq, _ = _conv_bwd("gdn_conv_q_bwd", p, 2, cw[:, 0:512], [dq], "q")
    dpk, dwk, _ = _conv_bwd("gdn_conv_k_bwd", p, 3, cw[:, 512:1024], [dk], "k")
    dpv, dwv, _ = _conv_bwd("gdn_conv_v_bwd", p, 4, cw[:, 1024:1536], [dv], "v")
    G["gdn_conv_w"] = jnp.concatenate([dwq, dwk, dwv], axis=1)
    dba, d_gprm = _gdn_prep_bwd(dbg, p, R["gdn_prm"])
    G["gdn_a_log"] = d_gprm[0, 8:16].reshape(2, GDN_H)
    G["gdn_dt_bias"] = d_gprm[1, 8:16].reshape(2, GDN_H)
    dp = jnp.concatenate([dx_rg, dgate, dpq, dpk, dpv, dz, dba], axis=1)
    tok = emit(w_in=_mm_plain("dw_in", D_MODEL, D_IN_PAD, S, D_MODEL, 640, tk_s, R["h2"], "km", dp, "kn", BF16))
    g_mix = W["mix_norm"] if tok is None else W["mix_norm"] + tok

    def epi_dx1(i, accs, ex, out):
        dx, dgt = _rmsnorm_bwd_tile(accs[0][...], ex[0][...], ex[1][...])
        out[0][...] = ex[2][...] + dx
        _colsum_into(out[1], i, jnp.sum(dgt, axis=0, keepdims=True))

    dx1, G["mix_norm"] = _fused_mm(
        "mix_dx", S, D_MODEL, D_IN_PAD, tm, D_MODEL, D_IN_PAD, [(dp, "mk"), (W["w_in"], "nk")], [(0, 1, 0)],
        [(R["x1"], (tm, D_MODEL), _mn), (g_mix, (1, D_MODEL), _row0), (dx2, (tm, D_MODEL), _mn)],
        [(jax.ShapeDtypeStruct((S, D_MODEL), F32), (tm, D_MODEL), _mn),
         (jax.ShapeDtypeStruct((1, D_MODEL), F32), (1, D_MODEL), _row0)], epi_dx1)
    G["final_norm"] = R["d_final_norm"]
    if emit_small is not None:
        emit_small(G)
    dx0, G["ffn1_norm"] = _ffn_bwd("ffn1b", dx1, x0, W["ffn1_norm"], R["h1"], R["a1"], R["b1"], R["f1"],
                                   W["ffn1_w_gate"], W["ffn1_w_up"], W["ffn1_w_down"], ffn_emit("ffn1"))
    return dx0, G


def _mesh_pos():
    x, y, c = lax.axis_index("x"), lax.axis_index("y"), lax.axis_index("c")
    return x, y, c, 4 * x + 2 * y + c


def _peer(x, y, c, r):
    px = 1 - x if r & 4 else x
    py = 1 - y if r & 2 else y
    pc = 1 - c if r & 1 else c
    return (px, py, pc), 4 * px + 2 * py + pc


_HBM = pl.BlockSpec(memory_space=pltpu.HBM)
_SEM = pl.BlockSpec(memory_space=pltpu.SEMAPHORE)


def _peer_copies(scatter, srcs, lands, send_sems, recv_sems):
    x, y, c, me = _mesh_pos()
    copies = []
    for a, (src, land) in enumerate(zip(srcs, lands)):
        for r in range(1, N_DEV):
            peer, peer_idx = _peer(x, y, c, r)
            block = src.at[peer_idx] if scatter else src
            slot = land.at[r - 1] if scatter else land.at[me]
            parts = _row_parts(block.shape[0], block.dtype)
            band = block.shape[0] // parts
            for p in range(parts):
                k = (a * 7 + r - 1) * COPY_PARTS + p
                copies.append(pltpu.make_async_remote_copy(
                    src_ref=block.at[pl.ds(p * band, band)], dst_ref=slot.at[pl.ds(p * band, band)],
                    send_sem=send_sems.at[k], recv_sem=recv_sems.at[k],
                    device_id=peer, device_id_type=pl.DeviceIdType.MESH))
    return copies


COPY_PARTS = 4


def _row_parts(rows, dtype):
    tile = 16 if dtype == BF16 else 8
    parts = COPY_PARTS
    while parts > 1 and rows % (parts * tile):
        parts //= 2
    return parts


def _exchange_start(name, scatter, arrays):
    slabs = arrays
    n = len(slabs)

    def body(*refs):
        srcs, lands = refs[0:n], refs[n:2 * n]
        send_sems, recv_sems = refs[2 * n], refs[2 * n + 1]
        token = refs[4 * n + 2]
        for cp in _peer_copies(scatter, srcs, lands, send_sems, recv_sems):
            cp.start()
        token[...] = jnp.zeros_like(token)

    land_shapes = [(N_DEV - 1,) + s.shape[1:] if scatter else (N_DEV,) + s.shape for s in slabs]
    n_sems = 7 * n * COPY_PARTS
    out_shape = ([pltpu.SemaphoreType.DMA((n_sems,)), pltpu.SemaphoreType.DMA((n_sems,))]
                 + [pltpu.HBM(s.shape, s.dtype) for s in slabs]
                 + [pltpu.HBM(shp, s.dtype) for shp, s in zip(land_shapes, slabs)]
                 + [jax.ShapeDtypeStruct((8, 128), F32)])
    res = pl.pallas_call(
        body, name=name, out_shape=out_shape, in_specs=[_HBM] * (2 * n),
        out_specs=[_SEM, _SEM] + [_HBM] * (2 * n) + [pl.BlockSpec(memory_space=pltpu.VMEM)],
        input_output_aliases={i: 2 + i for i in range(2 * n)},
        compiler_params=pltpu.CompilerParams(has_side_effects=pltpu.SideEffectType.DATAFLOW_SIDE_EFFECTING),
    )(*[pltpu.with_memory_space_constraint(s, pltpu.HBM) for s in slabs],
      *[pltpu.with_memory_space_constraint(lax.empty(shp, s.dtype), pltpu.HBM) for shp, s in zip(land_shapes, slabs)])
    return dict(n=n, scatter=scatter, sems=res[0:2], srcs=res[2:2 + n], lands=res[2 + n:2 + 2 * n],
                token=res[2 + 2 * n][0, 0])


def _exchange_wait(name, started, after):
    n = started["n"]
    scatter = started["scatter"]

    def body(*refs):
        srcs, lands = refs[0:n], refs[n:2 * n]
        send_sems, recv_sems = refs[2 * n], refs[2 * n + 1]
        for cp in _peer_copies(scatter, srcs, lands, send_sems, recv_sems):
            cp.wait_send()
            cp.wait_recv()

    arrays = list(started["srcs"]) + list(started["lands"])
    res = pl.pallas_call(
        body, name=name, out_shape=[pltpu.HBM(a.shape, a.dtype) for a in arrays],
        in_specs=[_HBM] * (2 * n) + [_SEM, _SEM, pl.BlockSpec(memory_space=pl.ANY)], out_specs=[_HBM] * (2 * n),
        input_output_aliases={i: i for i in range(2 * n)},
        compiler_params=pltpu.CompilerParams(has_side_effects=pltpu.SideEffectType.DATAFLOW_SIDE_EFFECTING),
    )(*arrays, *started["sems"], after)
    return res[0:n], res[n:2 * n]


def _all_gather(name, arrays):
    n = len(arrays)

    def body(*refs):
        ins = refs[:n]
        outs = refs[n:2 * n]
        token = refs[2 * n]
        send_sems, recv_sems, local_sems = refs[2 * n + 1:]
        token[...] = jnp.zeros_like(token)
        x, y, c, me = _mesh_pos()
        sibling = (x, y, 1 - c)
        chips = [(1 - x, y), (x, 1 - y), (1 - x, 1 - y)]

        def idx(px, py, pc):
            return 4 * px + 2 * py + pc

        def copy(a, k, block, to, src=None):
            slot = outs[a].at[idx(*block)]
            return pltpu.make_async_remote_copy(
                src_ref=slot if src is None else src, dst_ref=slot, send_sem=send_sems.at[a * 7 + k],
                recv_sem=recv_sems.at[a * 7 + k], device_id=to, device_id_type=pl.DeviceIdType.MESH)

        locals_, sends = [], []
        for a in range(n):
            loc = pltpu.make_async_copy(ins[a], outs[a].at[me], local_sems.at[a])
            loc.start()
            locals_.append(loc)
            sends.append(copy(a, 0, (x, y, c), sibling, src=ins[a]))
            sends += [copy(a, 1 + j, (x, y, c), (*chip, c), src=ins[a]) for j, chip in enumerate(chips)]
        for cp in sends:
            cp.start()
        passed = []
        for a in range(n):
            for j, chip in enumerate(chips):
                copy(a, 1 + j, (*chip, c), (x, y, c)).wait_recv()
                fwd = copy(a, 4 + j, (*chip, c), sibling)
                fwd.start()
                passed.append(fwd)
        for a in range(n):
            copy(a, 0, sibling, (x, y, c)).wait_recv()
            for j, chip in enumerate(chips):
                copy(a, 4 + j, (*chip, 1 - c), (x, y, c)).wait_recv()
        for cp in sends + passed:
            cp.wait_send()
        for loc in locals_:
            loc.wait()

    any_spec = pl.BlockSpec(memory_space=pl.ANY)
    res = pl.pallas_call(
        body, name=name, in_specs=[any_spec] * n, out_specs=[any_spec] * n + [pl.BlockSpec(memory_space=pltpu.VMEM)],
        out_shape=[jax.ShapeDtypeStruct((N_DEV,) + a.shape, a.dtype) for a in arrays]
        + [jax.ShapeDtypeStruct((8, 128), F32)],
        scratch_shapes=[pltpu.SemaphoreType.DMA((7 * n,)), pltpu.SemaphoreType.DMA((7 * n,)),
                        pltpu.SemaphoreType.DMA((n,))],
        compiler_params=pltpu.CompilerParams(has_side_effects=True),
    )(*arrays)
    return res[:n], res[n][0, 0]


def _adamw_math(w, g, m, v):
    m2 = ADAM_B1 * m + (1.0 - ADAM_B1) * g
    v2 = ADAM_B2 * v + (1.0 - ADAM_B2) * (g * g)
    m_hat = m2 / (1.0 - ADAM_B1 ** ADAM_STEP)
    v_hat = v2 / (1.0 - ADAM_B2 ** ADAM_STEP)
    delta = -ADAM_LR * (m_hat / (jnp.sqrt(v_hat) + ADAM_EPS) + ADAM_WD * w)
    return delta, m2, v2


def _adamw_slabs(name, src, land, me, w, m, v, tr):
    R, C = w.shape

    def body(me_ref, own_ref, land_ref, w_ref, m_ref, v_ref, g_ref, d_ref, m2_ref, v2_ref):
        g = own_ref[0].astype(F32)
        for s in range(N_DEV - 1):
            g = g + land_ref[s].astype(F32)
        delta, m2, v2 = _adamw_math(w_ref[...], g, m_ref[...], v_ref[...])
        g_ref[...] = g
        d_ref[...] = delta
        m2_ref[...] = m2
        v2_ref[...] = v2

    im = lambda i, me_ref: (i, 0)
    grid_spec = pltpu.PrefetchScalarGridSpec(
        num_scalar_prefetch=1, grid=(R // tr,),
        in_specs=[pl.BlockSpec((1, tr, C), lambda i, me_ref: (me_ref[0], i, 0)),
                  pl.BlockSpec((N_DEV - 1, tr, C), lambda i, me_ref: (0, i, 0)),
                  pl.BlockSpec((tr, C), im), pl.BlockSpec((tr, C), im), pl.BlockSpec((tr, C), im)],
        out_specs=[pl.BlockSpec((tr, C), im)] * 4)
    return pl.pallas_call(body, name=name, grid_spec=grid_spec, out_shape=[jax.ShapeDtypeStruct((R, C), F32)] * 4,
                          compiler_params=_cp(1))(me.reshape(1).astype(jnp.int32), src, land, w, m, v)


def _sum_slots(name, slots):
    _, R, C = slots.shape

    def body(s_ref, o_ref):
        g = s_ref[0]
        for s in range(1, N_DEV):
            g = g + s_ref[s]
        o_ref[...] = g

    return _rows(name, R, R, [(slots, (N_DEV, R, C), lambda i: (0, 0, 0))],
                 [(jax.ShapeDtypeStruct((R, C), F32), (R, C), lambda i: (0, 0))], body)[0]


def _adamw_packed(name, g, w, m, v):
    R, C = g.shape

    def body(g_ref, w_ref, m_ref, v_ref, d_ref, m2_ref, v2_ref):
        delta, m2, v2 = _adamw_math(w_ref[...], g_ref[...], m_ref[...], v_ref[...])
        d_ref[...] = delta
        m2_ref[...] = m2
        v2_ref[...] = v2

    im = lambda i: (0, 0)
    sds = jax.ShapeDtypeStruct((R, C), F32)
    return _rows(name, R, R, [(a, (R, C), im) for a in (g, w, m, v)], [(sds, (R, C), im)] * 3, body)


def _pack(arrays):
    rows = []
    for a in arrays:
        flat = a.reshape(-1).astype(F32)
        pad = (-flat.shape[0]) % 128
        rows.append(jnp.pad(flat, (0, pad)).reshape(-1, 128))
    out = jnp.concatenate(rows, axis=0)
    return jnp.pad(out, ((0, (-out.shape[0]) % 8), (0, 0)))


def _unpack(packed, shapes):
    lead = packed.shape[:-2]
    outs = []
    r = 0
    for shp in shapes:
        n = math.prod(shp)
        nr = -(-n // 128)
        flat = packed[..., r:r + nr, :].reshape(lead + (nr * 128,))[..., :n]
        outs.append(flat.reshape(lead + tuple(shp)))
        r += nr
    return outs


FFN1_BIG = ["ffn1_w_gate", "ffn1_w_up", "ffn1_w_down"]
MIX_BIG = ["w_in", "w_out"]
FFN2_BIG = ["ffn2_w_gate", "ffn2_w_up", "ffn2_w_down"]
BIG = FFN1_BIG + MIX_BIG + FFN2_BIG
COL_SHARDED = {"ffn1_w_gate", "ffn1_w_up", "w_in", "ffn2_w_gate", "ffn2_w_up"}
SMALL_SHARDED = ["rg_conv_w", "rg_gate_a_b", "rg_gate_x_b", "rg_lambda", "gdn_conv_w"]
WEIGHTS = ["ffn1_norm", "ffn1_w_gate", "ffn1_w_up", "ffn1_w_down", "mix_norm", "w_in", "w_out", "rg_conv_w", "rg_conv_b",
           "rg_gate_a_w", "rg_gate_a_b", "rg_gate_x_w", "rg_gate_x_b", "rg_lambda", "gdn_conv_w", "gdn_a_log",
           "gdn_dt_bias", "gdn_norm", "ffn2_norm", "ffn2_w_gate", "ffn2_w_up", "ffn2_w_down", "final_norm"]
SMALL = [n for n in WEIGHTS if n not in BIG]
ROW_VECTORS = {"ffn1_norm", "mix_norm", "ffn2_norm", "gdn_norm", "rg_conv_b", "final_norm"}
ROW_TILE = {"ffn1_w_gate": 256, "ffn1_w_up": 256, "ffn1_w_down": 176, "w_in": 256, "w_out": 64,
            "ffn2_w_gate": 256, "ffn2_w_up": 256, "ffn2_w_down": 176}


def _unshard_cols(g):
    return g.transpose(1, 0, 2).reshape(g.shape[1], N_DEV * g.shape[2])


def _to_slabs(name, g):
    if name in COL_SHARDED:
        r, ctot = g.shape
        return g.reshape(r, N_DEV, ctot // N_DEV).transpose(1, 0, 2)
    return g.reshape(N_DEV, g.shape[0] // N_DEV, g.shape[1])


def _step(x, target, w, m, v):
    _, _, _, me = _mesh_pos()
    def unshard(n, gth):
        full = _unshard_cols(gth) if n in COL_SHARDED else gth.reshape(-1, gth.shape[-1])
        return jnp.pad(full, ((0, 0), (0, D_IN_PAD - D_IN))) if n == "w_in" else full

    def landed(started, name, after):
        srcs, lands = _exchange_wait(name, started, after)
        def with_own(src, land):
            slot = lax.broadcasted_iota(jnp.int32, (N_DEV,) + (1,) * src.ndim, 0)
            return jnp.where(slot == me, src[None], land)

        return [with_own(src, land) for src, land in zip(srcs, lands)]

    up_names = ["ffn1_w_gate", "ffn1_w_up"]
    first, tok = _all_gather("gather_ffn1", [w[n].astype(BF16) for n in up_names])
    W = {n: unshard(n, gth) for n, gth in zip(up_names, first)}
    small_shards = [w[n] for n in SMALL_SHARDED]
    st_down = _exchange_start("gather_ffn1_down_start", False, [(w["ffn1_w_down"] + tok).astype(BF16)])
    st_mix = _exchange_start("gather_mix_start", False,
                             [(w[n] + tok).astype(BF16) for n in MIX_BIG] + [_pack(small_shards) + tok])
    st_ffn2 = _exchange_start("gather_ffn2_start", False, [(w[n] + tok).astype(BF16) for n in FFN2_BIG])
    for n in SMALL:
        if n not in SMALL_SHARDED:
            W[n] = w[n]
    W["ffn1_norm"] = w["ffn1_norm"] + (st_down["token"] + st_mix["token"] + st_ffn2["token"])

    def more(stage, after):
        if stage == "ffn1_down":
            return {"ffn1_w_down": unshard("ffn1_w_down", landed(st_down, "gather_ffn1_down_wait", after)[0])}
        if stage == "ffn2":
            return {n: unshard(n, gth) for n, gth in zip(FFN2_BIG, landed(st_ffn2, "gather_ffn2_wait", after))}
        got = landed(st_mix, "gather_mix_wait", after)
        new = {n: unshard(n, gth) for n, gth in zip(MIX_BIG, got)}
        for n, gth in zip(SMALL_SHARDED, _unpack(got[-1], [s.shape for s in small_shards])):
            new[n] = jnp.moveaxis(gth, 0, -2).reshape(gth.shape[1:-1] + (N_DEV * gth.shape[-1],))
        return new

    R = _layer_fwd(x, target, W, more)
    W = R["W"]
    pending = []

    def emit_big(**named):
        slabs = [_to_slabs(n, g[:, :D_IN] if n == "w_in" else g) for n, g in named.items()]
        started = _exchange_start(f"scatter_start_{len(pending)}", True, slabs)
        pending.append((list(named), started))
        return started["token"]

    small_started = []

    def emit_small(G):
        packed = _pack([G[n] for n in SMALL if n != "ffn1_norm"])
        small_started.append(_exchange_start("gather_small_start", False, [packed]))

    grad_x, G = _layer_bwd(x, W, R, emit_big, emit_small)
    loss = lax.psum(R["loss"][0, 0], ("x", "y", "c"))
    out = {}

    def finish(i, after):
        names, started = pending[i]
        srcs, lands = _exchange_wait(f"scatter_wait_{i}", started, after)
        for n, src, land in zip(names, srcs, lands):
            out[n] = _adamw_slabs(f"adamw_{n}", src, land, me, w[n], m[n], v[n], ROW_TILE[n])

    n_early = len(pending) - 2
    for i in range(n_early):
        finish(i, grad_x)
    early = [n for n in SMALL if n != "ffn1_norm"]
    srcs, lands = _exchange_wait("gather_small_wait", small_started[0], grad_x)
    slot = lax.broadcasted_iota(jnp.int32, (N_DEV, 1, 1), 0)
    slots = jnp.where(slot == me, srcs[0][None], lands[0])
    reduced = dict(zip(early, _unpack(_sum_slots("sum_small_grads", slots), [G[n].shape for n in early])))
    late = _all_gather("gather_ffn1_norm_grad", [_pack([G["ffn1_norm"]])])[0][0]
    reduced["ffn1_norm"] = _unpack(_sum_slots("sum_ffn1_norm_grad", late), [G["ffn1_norm"].shape])[0]
    g_small = []
    for n in SMALL:
        g = reduced[n]
        if n in SMALL_SHARDED:
            per = g.shape[-1] // N_DEV
            g = lax.dynamic_slice_in_dim(g, me * per, per, axis=g.ndim - 1)
        g_small.append(g.reshape(w[n].shape))
    shapes = [w[n].shape for n in SMALL]
    d_p, m_p, v_p = _adamw_packed("adamw_small", _pack(g_small), _pack([w[n] for n in SMALL]),
                                  _pack([m[n] for n in SMALL]), _pack([v[n] for n in SMALL]))
    for n, g, d_, m_, v_ in zip(SMALL, g_small, _unpack(d_p, shapes), _unpack(m_p, shapes), _unpack(v_p, shapes)):
        out[n] = (g, d_, m_, v_)
    for i in range(n_early, len(pending)):
        finish(i, d_p)
    return loss, grad_x, out


def kernel(x, ffn1_norm, ffn1_w_gate, ffn1_w_up, ffn1_w_down, mix_norm, w_in, w_out, rg_conv_w, rg_conv_b, rg_gate_a_w, rg_gate_a_b, rg_gate_x_w, rg_gate_x_b, rg_lambda, gdn_conv_w, gdn_a_log, gdn_dt_bias, gdn_norm, ffn2_norm, ffn2_w_gate, ffn2_w_up, ffn2_w_down, final_norm, loss_target, m_ffn1_norm, m_ffn1_w_gate, m_ffn1_w_up, m_ffn1_w_down, m_mix_norm, m_w_in, m_w_out, m_rg_conv_w, m_rg_conv_b, m_rg_gate_a_w, m_rg_gate_a_b, m_rg_gate_x_w, m_rg_gate_x_b, m_rg_lambda, m_gdn_conv_w, m_gdn_a_log, m_gdn_dt_bias, m_gdn_norm, m_ffn2_norm, m_ffn2_w_gate, m_ffn2_w_up, m_ffn2_w_down, m_final_norm, v_ffn1_norm, v_ffn1_w_gate, v_ffn1_w_up, v_ffn1_w_down, v_mix_norm, v_w_in, v_w_out, v_rg_conv_w, v_rg_conv_b, v_rg_gate_a_w, v_rg_gate_a_b, v_rg_gate_x_w, v_rg_gate_x_b, v_rg_lambda, v_gdn_conv_w, v_gdn_a_log, v_gdn_dt_bias, v_gdn_norm, v_ffn2_norm, v_ffn2_w_gate, v_ffn2_w_up, v_ffn2_w_down, v_final_norm):
    args = dict(locals())
    orig_shapes = {n: args[n].shape for n in WEIGHTS}

    def local(prefix):
        d = {}
        for n in WEIGHTS:
            a = args[prefix + n]
            d[n] = a.reshape(1, -1) if n in ROW_VECTORS else a[0]
        return d

    loss, grad_x, out = _step(x[0], loss_target[0], local(""), local("m_"), local("v_"))
    res = [loss, grad_x[None]]
    for k in range(4):
        res += [out[n][k].reshape(orig_shapes[n]) for n in WEIGHTS]
    return tuple(res)
```

```python
import functools
import math

import jax
import jax.numpy as jnp
from jax import lax
from jax.experimental import pallas as pl
from jax.experimental.pallas import tpu as pltpu

F32, BF16 = jnp.float32, jnp.bfloat16

D_MODEL = 1024
D_FF = 2816
RG_W = 512
RG_BLOCKS = 8
RG_BLOCK = 64
RG_C = 8.0
CONV_W = 4
GDN_H = 4
GDN_DK = 128
CHUNK = 64
EPS = 1e-6
D_IN = 3088
D_IN_PAD = 3200
COL_BA = 3072
N_DEV = 8
HALO = 16
VMEM_LIMIT = 60 * 1024 * 1024

ADAM_LR = 0.001
ADAM_B1 = 0.9
ADAM_B2 = 0.999
ADAM_EPS = 1e-08
ADAM_WD = 0.01
ADAM_STEP = 10

HI = lax.Precision.HIGHEST


def _cp(n):
    return pltpu.CompilerParams(dimension_semantics=("arbitrary",) * n, vmem_limit_bytes=VMEM_LIMIT)


def _tile(n, pref):
    return min(n, pref)


def _sigmoid(x):
    return 0.5 * jnp.tanh(0.5 * x) + 0.5


def _softplus(x):
    return jnp.maximum(x, 0.0) + jnp.log(1.0 + jnp.exp(-jnp.abs(x)))


def _dot(a, b, ca, cb, prec=None):
    return lax.dot_general(a, b, (((ca,), (cb,)), ((), ())), preferred_element_type=F32, precision=prec)


def _fused_mm(name, M, N, K, tm, tn, tk, ops, pairs, extras, outs, epilogue):
    nm, nn, nk = M // tm, N // tn, K // tk
    assert nm * tm == M and nn * tn == N and nk * tk == K, (name, M, N, K, tm, tn, tk)
    spec_of = {
        "mk": pl.BlockSpec((tm, tk), lambda i, j, k: (i, k)),
        "km": pl.BlockSpec((tk, tm), lambda i, j, k: (k, i)),
        "kn": pl.BlockSpec((tk, tn), lambda i, j, k: (k, j)),
        "nk": pl.BlockSpec((tn, tk), lambda i, j, k: (j, k)),
    }
    in_specs = [spec_of[m] for _, m in ops]
    in_specs += [pl.BlockSpec(bs, lambda i, j, k, im=im: im(i, j)) for _, bs, im in extras]
    out_specs = [pl.BlockSpec(bs, lambda i, j, k, im=im: im(i, j)) for _, bs, im in outs]
    n_ops, n_ex, n_out = len(ops), len(extras), len(outs)
    n_acc = 1 + max(g for _, _, g in pairs)
    modes = [m for _, m in ops]

    def body(*refs):
        op_refs = refs[:n_ops]
        ex_refs = refs[n_ops:n_ops + n_ex]
        out_refs = refs[n_ops + n_ex:n_ops + n_ex + n_out]
        accs = refs[n_ops + n_ex + n_out:]
        i = pl.program_id(0)
        k = pl.program_id(2)
        def dots():
            vals = [r[...].astype(BF16) for r in op_refs]
            for ia, ib, g in pairs:
                yield g, _dot(vals[ia], vals[ib], 1 if modes[ia] == "mk" else 0, 0 if modes[ib] == "kn" else 1)

        if nk == 1:
            sums = [None] * n_acc
            for g, d in dots():
                sums[g] = d if sums[g] is None else sums[g] + d
            epilogue(i, [_Held(s) for s in sums], ex_refs, out_refs)
            return

        @pl.when(k == 0)
        def _():
            for a in accs:
                a[...] = jnp.zeros_like(a)

        for g, d in dots():
            accs[g][...] += d

        @pl.when(k == nk - 1)
        def _():
            epilogue(i, accs, ex_refs, out_refs)

    res = pl.pallas_call(
        body, name=name, grid=(nm, nn, nk), in_specs=in_specs, out_specs=out_specs,
        out_shape=[o for o, _, _ in outs],
        scratch_shapes=[pltpu.VMEM((tm, tn), F32)] * (n_acc if nk > 1 else 0),
        compiler_params=_cp(3),
    )(*[a for a, _ in ops], *[a for a, _, _ in extras])
    return res


class _Held:
    def __init__(self, value):
        self.value = value

    def __getitem__(self, idx):
        return self.value[idx]


def _mn(i, j):
    return (i, j)


def _row0(i, j):
    return (0, 0)


def _rows(name, S, ts, ins, outs, body, scratch=()):
    return pl.pallas_call(
        body, name=name, grid=(S // ts,),
        in_specs=[pl.BlockSpec(bs, im) for _, bs, im in ins],
        out_specs=[pl.BlockSpec(bs, im) for _, bs, im in outs],
        out_shape=[o for o, _, _ in outs],
        scratch_shapes=list(scratch),
        compiler_params=_cp(1),
    )(*[a for a, _, _ in ins])


def _halo_ins(arr, S, ts, width, colblk):
    per = ts // HALO
    last = S // HALO - 1
    return [
        (arr, (ts, width), lambda i: (i, colblk)),
        (arr, (HALO, width), lambda i: (jnp.maximum(i * per - 1, 0), colblk)),
        (arr, (HALO, width), lambda i: (jnp.minimum((i + 1) * per, last), colblk)),
    ]


def _ext(main_ref, prev_ref, next_ref, i, n_tiles):
    prev = jnp.where(i > 0, prev_ref[...].astype(F32), 0.0)
    nxt = jnp.where(i < n_tiles - 1, next_ref[...].astype(F32), 0.0)
    return jnp.concatenate([prev, main_ref[...].astype(F32), nxt], axis=0)


def _shift(ext, off, ts):
    n = ext.shape[0]
    if off == 0:
        return ext[HALO:HALO + ts]
    return pltpu.roll(ext, (-off) % n, 0)[HALO:HALO + ts]


def _rmsnorm_fwd(name, x, g):
    S, D = x.shape
    ts = _tile(S, 512)

    def body(x_ref, g_ref, o_ref):
        xv = x_ref[...]
        r = lax.rsqrt(jnp.mean(xv * xv, axis=-1, keepdims=True) + EPS)
        o_ref[...] = (xv * r * g_ref[...]).astype(BF16)

    return _rows(name, S, ts,
                 [(x, (ts, D), lambda i: (i, 0)), (g, (1, D), lambda i: (0, 0))],
                 [(jax.ShapeDtypeStruct((S, D), BF16), (ts, D), lambda i: (i, 0))], body)[0]


def _rmsnorm_bwd_tile(dh, x, g):
    r = lax.rsqrt(jnp.mean(x * x, axis=-1, keepdims=True) + EPS)
    xhat = x * r
    dxn = dh * g
    dx = r * (dxn - xhat * jnp.mean(dxn * xhat, axis=-1, keepdims=True))
    return dx, dh * xhat


def _ffn_fwd(tag, x, h, wg, wu, wd):
    S = x.shape[0]
    tm = _tile(S, 1024)
    tn = 1408

    def epi_up(i, accs, ex, out):
        a = accs[0][...]
        b = accs[1][...]
        s = _sigmoid(a)
        sa = a * s
        out[0][...] = sa.astype(BF16)
        out[1][...] = (b * (s * (1.0 + a * (1.0 - s)))).astype(BF16)
        out[2][...] = (sa * b).astype(BF16)

    sds = jax.ShapeDtypeStruct((S, D_FF), BF16)
    a, b, f = _fused_mm(f"{tag}_up", S, D_FF, D_MODEL, tm, tn, D_MODEL,
                        [(h, "mk"), (wg, "kn"), (wu, "kn")], [(0, 1, 0), (0, 2, 1)], [],
                        [(sds, (tm, tn), _mn)] * 3, epi_up)

    def epi_down(i, accs, ex, out):
        out[0][...] = ex[0][...] + 0.5 * accs[0][...]

    if callable(wd):
        wd = wd(f)
    xo = _fused_mm(f"{tag}_down", S, D_MODEL, D_FF, tm, D_MODEL, 1408,
                   [(f, "mk"), (wd, "kn")], [(0, 1, 0)], [(x, (tm, D_MODEL), _mn)],
                   [(jax.ShapeDtypeStruct((S, D_MODEL), F32), (tm, D_MODEL), _mn)], epi_down)[0]
    return xo, a, b, f


def _conv_taps(ext, w_ref, ts):
    acc = None
    for j in range(CONV_W):
        term = w_ref[j:j + 1, :] * _shift(ext, j - 2, ts)
        acc = term if acc is None else acc + term
    return acc


def _l2norm_heads(s, scale):
    outs = []
    for h in range(GDN_H):
        sh = s[:, h * GDN_DK:(h + 1) * GDN_DK]
        outs.append(sh * (lax.rsqrt(jnp.sum(sh * sh, axis=-1, keepdims=True) + EPS) * scale))
    return jnp.concatenate(outs, axis=-1)


def _conv_fwd(name, p, colblk, w, bias, mode):
    S = p.shape[0]
    ts = _tile(S, 512)
    n_tiles = S // ts
    C = w.shape[1]

    def body(main, prev, nxt, w_ref, b_ref, o_ref):
        i = pl.program_id(0)
        c = _conv_taps(_ext(main, prev, nxt, i, n_tiles), w_ref, ts)
        if mode == "bias":
            o_ref[...] = c + b_ref[...]
        else:
            s = c * _sigmoid(c)
            if mode == "q":
                s = _l2norm_heads(s, GDN_DK ** -0.5)
            elif mode == "k":
                s = _l2norm_heads(s, 1.0)
            o_ref[...] = s

    ins = _halo_ins(p, S, ts, C, colblk) + [(w, (CONV_W, C), lambda i: (0, 0)), (bias, (1, C), lambda i: (0, 0))]
    return _rows(name, S, ts, ins, [(jax.ShapeDtypeStruct((S, C), F32), (ts, C), lambda i: (i, 0))], body)[0]


def _rg_gate_terms(pre, xc, prm_ref, d):
    r = _sigmoid(pre[:, d * 1024:d * 1024 + RG_W] + prm_ref[2 * d:2 * d + 1, :])
    ig = _sigmoid(pre[:, d * 1024 + RG_W:(d + 1) * 1024] + prm_ref[2 * d + 1:2 * d + 2, :])
    sp = _softplus(-prm_ref[4 + d:5 + d, :])
    log_a = -RG_C * r * sp
    a = jnp.exp(log_a)
    t = jnp.tanh(log_a)
    sq = jnp.sqrt(-2.0 * t / (1.0 - t))
    return r, ig, sp, a, sq


def _rg_gates_fwd(xc, bd, prm):
    S = xc.shape[0]
    tm = _tile(S, 256)

    def epi(i, accs, ex, out):
        pre = accs[0][...]
        xv = ex[0][...]
        for d in range(2):
            r, ig, sp, a, sq = _rg_gate_terms(pre, xv, ex[1], d)
            out[2 * d][...] = a
            out[2 * d + 1][...] = sq * ig * xv

    sds = jax.ShapeDtypeStruct((S, RG_W), F32)
    blk = (tm, RG_W)
    im = lambda i, j: (i, 0)
    return _fused_mm("rg_gates_fwd", S, 4 * RG_W, RG_W, tm, 4 * RG_W, RG_W,
                     [(xc, "mk"), (bd, "kn")], [(0, 1, 0)],
                     [(xc, blk, im), (prm, (8, RG_W), _row0)], [(sds, blk, im)] * 4, epi)


SUBLANES = 8


def _scan_rows(a, b, reverse):
    rows = lax.broadcasted_iota(jnp.int32, a.shape, 0)
    s = 1
    while s < SUBLANES:
        shift = SUBLANES - s if reverse else s
        a_sh = pltpu.roll(a, shift, 0)
        b_sh = pltpu.roll(b, shift, 0)
        valid = (rows < SUBLANES - s) if reverse else (rows >= s)
        b = jnp.where(valid, a * b_sh + b, b)
        a = jnp.where(valid, a * a_sh, a)
        s *= 2
    return a, b


def _rg_scan(name, a_f, b_f, a_b, b_b):
    S, C = a_f.shape
    ts = _tile(S, 512)
    n_tiles = S // ts

    def body(af, bf, ab, bb, hf, hb, carry):
        @pl.when(pl.program_id(0) == 0)
        def _():
            carry[...] = jnp.zeros_like(carry)

        n_sub = ts // SUBLANES

        def step(j, c):
            cf, cb = c
            r0 = pl.multiple_of(j * SUBLANES, SUBLANES)
            cum_a, h0 = _scan_rows(af[pl.ds(r0, SUBLANES), :], bf[pl.ds(r0, SUBLANES), :], False)
            h = h0 + cum_a * cf
            hf[pl.ds(r0, SUBLANES), :] = h
            cf = h[SUBLANES - 1:SUBLANES, :]
            r1 = pl.multiple_of((n_sub - 1 - j) * SUBLANES, SUBLANES)
            cum_a, h0 = _scan_rows(ab[pl.ds(r1, SUBLANES), :], bb[pl.ds(r1, SUBLANES), :], True)
            h = h0 + cum_a * cb
            hb[pl.ds(r1, SUBLANES), :] = h
            cb = h[0:1, :]
            return cf, cb

        cf, cb = lax.fori_loop(0, n_sub, step, (carry[0:1, :], carry[1:2, :]), unroll=4)
        carry[0:1, :] = cf
        carry[1:2, :] = cb

    fw = lambda i: (i, 0)
    bw = lambda i: (n_tiles - 1 - i, 0)
    sds = jax.ShapeDtypeStruct((S, C), F32)
    return _rows(name, S, ts,
                 [(a_f, (ts, C), fw), (b_f, (ts, C), fw), (a_b, (ts, C), bw), (b_b, (ts, C), bw)],
                 [(sds, (ts, C), fw), (sds, (ts, C), bw)], body, scratch=[pltpu.VMEM((8, C), F32)])


def _tri_masks():
    ri = lax.broadcasted_iota(jnp.int32, (CHUNK, CHUNK), 0)
    ci = lax.broadcasted_iota(jnp.int32, (CHUNK, CHUNK), 1)
    return ri, ci


def _gdn_prep_fwd(p, prm):
    S = p.shape[0]
    ts = _tile(S, 512)

    def body(p_ref, prm_ref, o_ref):
        raw = p_ref[...].astype(F32)
        lane = lax.broadcasted_iota(jnp.int32, (1, 128), 1)
        g = -jnp.exp(prm_ref[0:1, :]) * _softplus(raw + prm_ref[1:2, :])
        g = jnp.where((lane >= 8) & (lane < 16), g, 0.0)
        beta = _sigmoid(raw)
        ri, ci = _tri_masks()
        lower = (ri >= ci).astype(F32)
        upper = (ri <= ci).astype(F32)
        for c in range(ts // CHUNK):
            rows = slice(c * CHUNK, (c + 1) * CHUNK)
            gch = g[rows]
            gc = jnp.where(lane < 12, _dot(lower, gch, 1, 0, HI), _dot(upper, gch, 1, 0, HI))
            o_ref[rows, :] = jnp.where(lane < 8, beta[rows], gc)

    return _rows("gdn_prep_fwd", S, ts,
                 [(p, (ts, 128), lambda i: (i, COL_BA // 128)), (prm, (8, 128), lambda i: (0, 0))],
                 [(jax.ShapeDtypeStruct((S, 128), F32), (ts, 128), lambda i: (i, 0))], body)[0]


def _bdot(a, b, ca, cb):
    return _dot(a.astype(BF16), b.astype(BF16), ca, cb)


GDN_W = GDN_H * GDN_DK
GDN_TS = 256
LOCAL_CHUNKS = 2

def _gdn_decay(bg_ref, gcr_ref, c, rows, r0, col, rev, ri, ci):
    beta = bg_ref[rows, col:col + 1]
    gc = bg_ref[rows, 8 + col:9 + col]
    last = 0 if rev else CHUNK - 1
    gl = bg_ref[pl.ds(r0 + last, 1), 8 + col:9 + col]
    out = dict(beta=beta, gc=gc, gl=gl, eg=jnp.exp(gc), egl=jnp.exp(gl - gc), cd=jnp.exp(gl))
    if gcr_ref is not None:
        incl = (ri <= ci) if rev else (ri >= ci)
        out["strict"] = (ri < ci) if rev else (ri > ci)
        out["dm"] = jnp.where(incl, jnp.exp(jnp.where(incl, gc - gcr_ref[c, col:col + 1, :], 0.0)), 0.0)
    return out


def _dir_tile(d, n_tiles, flip):
    if (d == 1) != flip:
        return lambda i: n_tiles - 1 - i
    return lambda i: i


def _gdn_local_fwd(q, k, v, bg, gcr):
    S = q.shape[0]
    ts = _tile(S, GDN_TS)
    ncb = ts // CHUNK
    nch = S // CHUNK

    def body(q_ref, k_ref, v_ref, bg_ref, gcr_ref, *out_refs):
        ri, ci = _tri_masks()
        eye = (ri == ci).astype(F32)
        outs = (out_refs[0:6], out_refs[6:12])
        cd_ref = out_refs[12]

        def chunk(cc, carry):
            chains = []
            for c in (LOCAL_CHUNKS * cc + j for j in range(LOCAL_CHUNKS)):
                r0 = pl.multiple_of(c * CHUNK, CHUNK)
                rows = pl.ds(r0, CHUNK)
                for h in range(GDN_H):
                    cols = slice(h * GDN_DK, (h + 1) * GDN_DK)
                    qh, kh, vh = q_ref[rows, cols], k_ref[rows, cols], v_ref[rows, cols]
                    both = _bdot(jnp.concatenate([qh, kh], axis=0), kh, 1, 1)
                    for d in range(2):
                        chains.append(dict(c=c, r0=r0, rows=rows, h=h, d=d, cols=cols, qh=qh, kh=kh, vh=vh,
                                           qk=both[0:CHUNK], kk=both[CHUNK:2 * CHUNK]))
            for ch in chains:
                m = _gdn_decay(bg_ref, gcr_ref, ch["c"], ch["rows"], ch["r0"], ch["d"] * GDN_H + ch["h"], ch["d"] == 1,
                               ri, ci)
                ch["m"] = m
                ch["x"] = -jnp.where(m["strict"], m["beta"] * ch["kk"] * m["dm"], 0.0)
                ch["t"] = eye + ch["x"]
            for ch in chains:
                ch["pw"] = _bdot(ch["x"], ch["x"], 1, 0)
            for level in range(1, 6):
                last_level = level == 5
                for ch in chains:
                    rhs = ch["t"] if last_level else jnp.concatenate([ch["t"], ch["pw"]], axis=1)
                    ch["prod"] = _bdot(ch["pw"], rhs, 1, 0)
                for ch in chains:
                    ch["t"] = ch["t"] + ch["prod"][:, 0:CHUNK]
                    if not last_level:
                        ch["pw"] = ch["prod"][:, CHUNK:2 * CHUNK]
            for ch in chains:
                m = ch["m"]
                rhs = jnp.concatenate([ch["vh"] * m["beta"], ch["kh"] * (m["beta"] * m["eg"])], axis=1)
                ch["uw"] = _bdot(ch["t"], rhs, 1, 0)
            for ch in chains:
                u_ref, w_ref, a_ref, t_ref, qd_ref, kd_ref = outs[ch["d"]]
                m = ch["m"]
                c, rows = ch["c"], ch["rows"]
                col = ch["d"] * GDN_H + ch["h"]
                u_ref[rows, ch["cols"]] = ch["uw"][:, 0:GDN_DK]
                w_ref[rows, ch["cols"]] = ch["uw"][:, GDN_DK:2 * GDN_DK].astype(BF16)
                a_ref[c, ch["h"]] = (ch["qk"] * m["dm"]).astype(BF16)
                t_ref[c, ch["h"]] = _bdot(ch["t"], eye, 0, 0).astype(BF16)
                qd_ref[rows, ch["cols"]] = (ch["qh"] * m["eg"]).astype(BF16)
                kd_ref[rows, ch["cols"]] = (ch["kh"] * m["egl"]).astype(BF16)
                cd_ref[c, col:col + 1, :] = jnp.broadcast_to(m["cd"], (1, 128))
            return carry

        lax.fori_loop(0, ncb // LOCAL_CHUNKS, chunk, 0)

    im = lambda i: (i, 0)
    im4 = lambda i: (i, 0, 0, 0)
    ins = [(q, (ts, GDN_W), im), (k, (ts, GDN_W), im), (v, (ts, GDN_W), im), (bg, (ts, 128), im),
           (gcr, (ncb, 8, CHUNK), lambda i: (i, 0, 0))]
    per_dir = [(jax.ShapeDtypeStruct((S, GDN_W), F32), (ts, GDN_W), im),
               (jax.ShapeDtypeStruct((S, GDN_W), BF16), (ts, GDN_W), im),
               (jax.ShapeDtypeStruct((nch, GDN_H, CHUNK, CHUNK), BF16), (ncb, GDN_H, CHUNK, CHUNK), im4),
               (jax.ShapeDtypeStruct((nch, GDN_H, CHUNK, CHUNK), BF16), (ncb, GDN_H, CHUNK, CHUNK), im4),
               (jax.ShapeDtypeStruct((S, GDN_W), BF16), (ts, GDN_W), im),
               (jax.ShapeDtypeStruct((S, GDN_W), BF16), (ts, GDN_W), im)]
    cd_out = (jax.ShapeDtypeStruct((nch, 8, 128), F32), (ncb, 8, 128), lambda i: (i, 0, 0))
    res = _rows("gdn_local_fwd", S, ts, ins, per_dir * 2 + [cd_out], body)
    return res[0:6], res[6:12], res[12]


def _gdn_scan_fwd(loc):
    S = loc[0][0].shape[0]
    ts = _tile(S, GDN_TS)
    n_tiles = S // ts
    ncb = ts // CHUNK
    nch = S // CHUNK

    def body(*refs):
        ins = (refs[0:6], refs[6:12])
        outs = (refs[12:15], refs[15:18])
        state = refs[18]

        @pl.when(pl.program_id(0) == 0)
        def _():
            state[...] = jnp.zeros_like(state)

        def chunk(cc, carry):
            chains = []
            for d in range(2):
                c = cc if d == 0 else ncb - 1 - cc
                rows = pl.ds(pl.multiple_of(c * CHUNK, CHUNK), CHUNK)
                for h in range(GDN_H):
                    cols = slice(h * GDN_DK, (h + 1) * GDN_DK)
                    chains.append(dict(d=d, h=h, c=c, rows=rows, cols=cols, st=state[d * GDN_H + h]))
            for ch in chains:
                qd_ref, kd_ref, u_ref, w_ref, a_ref, cd_ref = ins[ch["d"]]
                rows, cols = ch["rows"], ch["cols"]
                lhs = jnp.concatenate([w_ref[rows, cols], qd_ref[rows, cols]], axis=0)
                ch["ws_qs"] = _dot(lhs, ch["st"].astype(BF16), 1, 0)
            for ch in chains:
                qd_ref, kd_ref, u_ref, w_ref, a_ref, cd_ref = ins[ch["d"]]
                rows, cols = ch["rows"], ch["cols"]
                vn = u_ref[rows, cols] - ch["ws_qs"][0:CHUNK]
                vnb = vn.astype(BF16)
                ch["vn"] = vn
                ch["avn"] = _dot(a_ref[ch["c"], ch["h"]], vnb, 1, 0)
                ch["kvn"] = _dot(kd_ref[rows, cols], vnb, 0, 0)
            for ch in chains:
                o_ref, vn_ref, s_ref = outs[ch["d"]]
                cd_ref = ins[ch["d"]][5]
                rows, cols = ch["rows"], ch["cols"]
                col = ch["d"] * GDN_H + ch["h"]
                o_ref[rows, cols] = ch["ws_qs"][CHUNK:2 * CHUNK] + ch["avn"]
                vn_ref[rows, cols] = ch["vn"].astype(BF16)
                s_ref[ch["c"], ch["h"]] = ch["st"].astype(BF16)
                state[ch["d"] * GDN_H + ch["h"]] = ch["st"] * cd_ref[ch["c"], col:col + 1, :] + ch["kvn"]
            return carry

        lax.fori_loop(0, ncb, chunk, 0)

    ins, outs = [], []
    for d in range(2):
        tix = _dir_tile(d, n_tiles, False)
        im = lambda i, tix=tix: (tix(i), 0)
        im4 = lambda i, tix=tix: (tix(i), 0, 0, 0)
        u, w, a, _, qd, kd = loc[d]
        ins += [(qd, (ts, GDN_W), im), (kd, (ts, GDN_W), im), (u, (ts, GDN_W), im), (w, (ts, GDN_W), im),
                (a, (ncb, GDN_H, CHUNK, CHUNK), im4), (loc[2], (ncb, 8, 128), lambda i, tix=tix: (tix(i), 0, 0))]
        outs += [(jax.ShapeDtypeStruct((S, GDN_W), F32), (ts, GDN_W), im),
                 (jax.ShapeDtypeStruct((S, GDN_W), BF16), (ts, GDN_W), im),
                 (jax.ShapeDtypeStruct((nch, GDN_H, GDN_DK, GDN_DK), BF16), (ncb, GDN_H, GDN_DK, GDN_DK), im4)]
    res = _rows("gdn_scan_fwd", S, ts, ins, outs, body, scratch=[pltpu.VMEM((2 * GDN_H, GDN_DK, GDN_DK), F32)])
    return res[0:3], res[3:6]


def _gelu(x):
    c = math.sqrt(2.0 / math.pi)
    t = jnp.tanh(c * (x + 0.044715 * x * x * x))
    return 0.5 * x * (1.0 + t), t


def _mix_out_fwd(h_f, h_b, o_f, o_b, p, gn):
    S = h_f.shape[0]
    ts = _tile(S, 512)

    def body(hf, hb, of, ob, gate, z, gn_ref, y_ref):
        ge, _ = _gelu(gate[...].astype(F32))
        y_ref[:, 0:RG_W] = ((hf[...] + hb[...]) * ge).astype(BF16)
        o = of[...] + ob[...]
        zv = z[...].astype(F32)
        sz = zv * _sigmoid(zv)
        for h in range(GDN_H):
            cols = slice(h * GDN_DK, (h + 1) * GDN_DK)
            oh = o[:, cols]
            n = oh * lax.rsqrt(jnp.mean(oh * oh, axis=-1, keepdims=True) + EPS) * gn_ref[...]
            y_ref[:, RG_W + h * GDN_DK:RG_W + (h + 1) * GDN_DK] = (n * sz[:, cols]).astype(BF16)

    blk = (ts, RG_W)
    im = lambda i: (i, 0)
    ins = [(h_f, blk, im), (h_b, blk, im), (o_f, blk, im), (o_b, blk, im),
           (p, blk, lambda i: (i, 1)), (p, blk, lambda i: (i, 5)), (gn, (1, GDN_DK), lambda i: (0, 0))]
    return _rows("mix_out_fwd", S, ts, ins,
                 [(jax.ShapeDtypeStruct((S, D_MODEL), BF16), (ts, D_MODEL), im)], body)[0]


def _loss_head(x, target, g):
    S, D = x.shape
    ts = _tile(S, 512)

    def body(x_ref, t_ref, g_ref, dx_ref, loss_ref, dg_ref):
        @pl.when(pl.program_id(0) == 0)
        def _():
            loss_ref[...] = jnp.zeros_like(loss_ref)
            dg_ref[...] = jnp.zeros_like(dg_ref)

        xv = x_ref[...]
        gv = g_ref[...]
        r = lax.rsqrt(jnp.mean(xv * xv, axis=-1, keepdims=True) + EPS)
        err = xv * r * gv - t_ref[...]
        loss_ref[...] += jnp.sum(err * err) * (0.5 / D)
        dx, dgt = _rmsnorm_bwd_tile(err * (1.0 / D), xv, gv)
        dx_ref[...] = dx
        dg_ref[...] += jnp.sum(dgt, axis=0, keepdims=True)

    im = lambda i: (i, 0)
    z = lambda i: (0, 0)
    return _rows("loss_head", S, ts,
                 [(x, (ts, D), im), (target, (ts, D), im), (g, (1, D), z)],
                 [(jax.ShapeDtypeStruct((S, D), F32), (ts, D), im),
                  (jax.ShapeDtypeStruct((8, 128), F32), (8, 128), z),
                  (jax.ShapeDtypeStruct((1, D), F32), (1, D), z)], body)


def _block_diag(w):
    n = w.shape[0]
    return jnp.einsum("nij,nm->nimj", w, jnp.eye(n, dtype=w.dtype)).reshape(n * w.shape[1], n * w.shape[2])


def _rg_bd(a_w, x_w):
    return jnp.concatenate([_block_diag(a_w[0]), _block_diag(x_w[0]), _block_diag(a_w[1]), _block_diag(x_w[1])],
                           axis=1).astype(BF16)


def _rg_prm(ba, bx, lam):
    return jnp.concatenate([ba[0:1], bx[0:1], ba[1:2], bx[1:2], lam, jnp.zeros((2, RG_W), F32)], axis=0)


def _gdn_prm(a_log, dt_bias):
    rows = jnp.zeros((8, 128), F32)
    rows = rows.at[0, 8:16].set(a_log.reshape(-1))
    return rows.at[1, 8:16].set(dt_bias.reshape(-1))


def _gc_rows(bg):
    S = bg.shape[0]
    return bg[:, 8:16].reshape(S // CHUNK, CHUNK, 8).transpose(0, 2, 1)


def _layer_fwd(x0, target, W, more=None):
    S = x0.shape[0]
    R = {}
    R["h1"] = _rmsnorm_fwd("rms1", x0, W["ffn1_norm"])
    late_wd = {}

    def ffn1_wd(after):
        late_wd.update(more("ffn1_down", after))
        return late_wd["ffn1_w_down"]

    R["x1"], R["a1"], R["b1"], R["f1"] = _ffn_fwd("ffn1", x0, R["h1"], W["ffn1_w_gate"], W["ffn1_w_up"],
                                                  ffn1_wd if more is not None else W["ffn1_w_down"])
    if more is not None:
        W = {**W, **late_wd, **more("mixer", R["x1"])}
    R["h2"] = _rmsnorm_fwd("rms2", R["x1"], W["mix_norm"])
    tm = _tile(S, 512)
    tmp = _tile(S, 1024)
    R["p"] = _fused_mm("in_proj", S, D_IN_PAD, D_MODEL, tmp, 640, D_MODEL, [(R["h2"], "mk"), (W["w_in"], "kn")],
                       [(0, 1, 0)], [], [(jax.ShapeDtypeStruct((S, D_IN_PAD), BF16), (tmp, 640), _mn)],
                       lambda i, accs, ex, out: out[0].__setitem__(Ellipsis, accs[0][...].astype(BF16)))[0]
    p = R["p"]
    R["xc"] = _conv_fwd("rg_conv_fwd", p, 0, W["rg_conv_w"], W["rg_conv_b"], "bias")
    R["bd"] = _rg_bd(W["rg_gate_a_w"], W["rg_gate_x_w"])
    R["rg_prm"] = _rg_prm(W["rg_gate_a_b"], W["rg_gate_x_b"], W["rg_lambda"])
    a_f, b_f, a_b, b_b = _rg_gates_fwd(R["xc"], R["bd"], R["rg_prm"])
    R["a_f"], R["a_b"] = a_f, a_b
    R["h_f"], R["h_b"] = _rg_scan("rg_scan_fwd", a_f, b_f, a_b, b_b)
    zero_b = jnp.zeros((1, RG_W), F32)
    cw = W["gdn_conv_w"]
    R["q"] = _conv_fwd("gdn_conv_q", p, 2, cw[:, 0:512], zero_b, "q")
    R["k"] = _conv_fwd("gdn_conv_k", p, 3, cw[:, 512:1024], zero_b, "k")
    R["v"] = _conv_fwd("gdn_conv_v", p, 4, cw[:, 1024:1536], zero_b, "v")
    R["gdn_prm"] = _gdn_prm(W["gdn_a_log"], W["gdn_dt_bias"])
    R["bg"] = _gdn_prep_fwd(p, R["gdn_prm"])
    R["gcr"] = _gc_rows(R["bg"])
    R["gdn_loc"] = _gdn_local_fwd(R["q"], R["k"], R["v"], R["bg"], R["gcr"])
    R["gdn_fwd"] = _gdn_scan_fwd(R["gdn_loc"])
    R["o_f"], R["o_b"] = R["gdn_fwd"][0][0], R["gdn_fwd"][1][0]
    R["y"] = _mix_out_fwd(R["h_f"], R["h_b"], R["o_f"], R["o_b"], p, W["gdn_norm"])
    R["x2"] = _fused_mm("out_proj", S, D_MODEL, D_MODEL, tm, D_MODEL, D_MODEL, [(R["y"], "mk"), (W["w_out"], "kn")],
                        [(0, 1, 0)], [(R["x1"], (tm, D_MODEL), _mn)],
                        [(jax.ShapeDtypeStruct((S, D_MODEL), F32), (tm, D_MODEL), _mn)],
                        lambda i, accs, ex, out: out[0].__setitem__(Ellipsis, ex[0][...] + accs[0][...]))[0]
    if more is not None:
        W = {**W, **more("ffn2", R["x2"])}
    R["h3"] = _rmsnorm_fwd("rms3", R["x2"], W["ffn2_norm"])
    R["x3"], R["a2"], R["b2"], R["f2"] = _ffn_fwd("ffn2", R["x2"], R["h3"], W["ffn2_w_gate"], W["ffn2_w_up"], W["ffn2_w_down"])
    R["dx3"], R["loss"], R["d_final_norm"] = _loss_head(R["x3"], target, W["final_norm"])
    R["W"] = W
    return R


def _colsum_into(ref, i, val):
    @pl.when(i == 0)
    def _():
        ref[...] = val

    @pl.when(i > 0)
    def _():
        ref[...] += val


def _ffn_bwd(tag, dout, x, g, h, a, b, f, wg, wu, wd, emit):
    S = x.shape[0]
    tm = _tile(S, 512)
    tk_s = _tile(S, 1024)
    dwd = _fused_mm(f"{tag}_dw_down", D_FF, D_MODEL, S, 1408, D_MODEL, tk_s, [(f, "km"), (dout, "kn")], [(0, 1, 0)], [],
                    [(jax.ShapeDtypeStruct((D_FF, D_MODEL), BF16), (1408, D_MODEL), _mn)],
                    lambda i, accs, ex, out: out[0].__setitem__(Ellipsis, (0.5 * accs[0][...]).astype(BF16)))[0]
    emit(down=dwd)

    def epi_act(i, accs, ex, out):
        df = 0.5 * accs[0][...]
        out[0][...] = (df * ex[1][...].astype(F32)).astype(BF16)
        out[1][...] = (df * ex[0][...].astype(F32)).astype(BF16)

    sds = jax.ShapeDtypeStruct((S, D_FF), BF16)
    da, db = _fused_mm(f"{tag}_dact", S, D_FF, D_MODEL, tm, 1408, D_MODEL, [(dout, "mk"), (wd, "nk")], [(0, 1, 0)],
                       [(a, (tm, 1408), _mn), (b, (tm, 1408), _mn)], [(sds, (tm, 1408), _mn)] * 2, epi_act)

    def epi_w2(i, accs, ex, out):
        out[0][...] = accs[0][...].astype(BF16)
        out[1][...] = accs[1][...].astype(BF16)

    sdw = jax.ShapeDtypeStruct((D_MODEL, D_FF), BF16)
    dwg, dwu = _fused_mm(f"{tag}_dw_up", D_MODEL, D_FF, S, D_MODEL, 1408, tk_s,
                         [(h, "km"), (da, "kn"), (db, "kn")], [(0, 1, 0), (0, 2, 1)], [],
                         [(sdw, (D_MODEL, 1408), _mn)] * 2, epi_w2)
    tok = emit(gate=dwg, up=dwu)
    if tok is not None:
        g = g + tok

    def epi_dx(i, accs, ex, out):
        dx, dgt = _rmsnorm_bwd_tile(accs[0][...], ex[0][...], ex[1][...])
        out[0][...] = ex[2][...] + dx
        _colsum_into(out[1], i, jnp.sum(dgt, axis=0, keepdims=True))

    tmx = _tile(S, 1024)
    dx, dg = _fused_mm(f"{tag}_dx", S, D_MODEL, D_FF, tmx, D_MODEL, 1408,
                       [(da, "mk"), (wg, "nk"), (db, "mk"), (wu, "nk")], [(0, 1, 0), (2, 3, 0)],
                       [(x, (tmx, D_MODEL), _mn), (g, (1, D_MODEL), _row0), (dout, (tmx, D_MODEL), _mn)],
                       [(jax.ShapeDtypeStruct((S, D_MODEL), F32), (tmx, D_MODEL), _mn),
                        (jax.ShapeDtypeStruct((1, D_MODEL), F32), (1, D_MODEL), _row0)], epi_dx)
    return dx, dg


def _mix_out_bwd(dy, h_f, h_b, o_f, o_b, p, gn):
    S = dy.shape[0]
    ts = _tile(S, 512)
    c0 = math.sqrt(2.0 / math.pi)

    def body(dy_ref, hf, hb, of, ob, gate, z, gn_ref, dhr_ref, dgate_ref, do_ref, dz_ref, dgn_ref):
        i = pl.program_id(0)
        gv = gate[...].astype(F32)
        ge, t = _gelu(gv)
        dy_rg = dy_ref[:, 0:RG_W]
        dhr_ref[...] = dy_rg * ge
        dgelu = 0.5 * (1.0 + t) + 0.5 * gv * (1.0 - t * t) * c0 * (1.0 + 3.0 * 0.044715 * gv * gv)
        dgate_ref[...] = (dy_rg * (hf[...] + hb[...]) * dgelu).astype(BF16)
        o = of[...] + ob[...]
        zv = z[...].astype(F32)
        sig = _sigmoid(zv)
        gnv = gn_ref[...]
        dgn = jnp.zeros((1, GDN_DK), F32)
        for h in range(GDN_H):
            cols = slice(h * GDN_DK, (h + 1) * GDN_DK)
            oh = o[:, cols]
            r = lax.rsqrt(jnp.mean(oh * oh, axis=-1, keepdims=True) + EPS)
            ohat = oh * r
            dyh = dy_ref[:, RG_W + h * GDN_DK:RG_W + (h + 1) * GDN_DK]
            zh = zv[:, cols]
            sh = sig[:, cols]
            dn = dyh * zh * sh
            dz_ref[:, cols] = (dyh * ohat * gnv * (sh * (1.0 + zh * (1.0 - sh)))).astype(BF16)
            dxn = dn * gnv
            do_ref[:, cols] = r * (dxn - ohat * jnp.mean(dxn * ohat, axis=-1, keepdims=True))
            dgn = dgn + jnp.sum(dn * ohat, axis=0, keepdims=True)
        _colsum_into(dgn_ref, i, dgn)

    blk = (ts, RG_W)
    im = lambda i: (i, 0)
    z0 = lambda i: (0, 0)
    ins = [(dy, (ts, D_MODEL), im), (h_f, blk, im), (h_b, blk, im), (o_f, blk, im), (o_b, blk, im),
           (p, blk, lambda i: (i, 1)), (p, blk, lambda i: (i, 5)), (gn, (1, GDN_DK), z0)]
    outs = [(jax.ShapeDtypeStruct((S, RG_W), F32), blk, im), (jax.ShapeDtypeStruct((S, RG_W), BF16), blk, im),
            (jax.ShapeDtypeStruct((S, RG_W), F32), blk, im), (jax.ShapeDtypeStruct((S, RG_W), BF16), blk, im),
            (jax.ShapeDtypeStruct((1, GDN_DK), F32), (1, GDN_DK), z0)]
    return _rows("mix_out_bwd", S, ts, ins, outs, body)


def _rg_scan_adj(name, a_up, b_up, a_dn, b_dn):
    S, C = a_up.shape
    ts = _tile(S, 512)
    n_tiles = S // ts

    def body(au, bu, ad, bd, mu_ref, lam_ref, carry):
        @pl.when(pl.program_id(0) == 0)
        def _():
            carry[...] = jnp.zeros_like(carry)

        n_sub = ts // SUBLANES
        rows = lax.broadcasted_iota(jnp.int32, (SUBLANES, C), 0)

        def half(a_ref, b_ref, out_ref, r0, c_in, reverse):
            a = a_ref[pl.ds(r0, SUBLANES), :]
            b = b_ref[pl.ds(r0, SUBLANES), :]
            cum_a, c0 = _scan_rows(a, a * b, reverse)
            c = c0 + cum_a * c_in
            edge = 0 if not reverse else SUBLANES - 1
            c_prev = jnp.where(rows == edge, c_in, pltpu.roll(c, SUBLANES - 1 if reverse else 1, 0))
            out_ref[pl.ds(r0, SUBLANES), :] = b + c_prev
            return c[0:1, :] if reverse else c[SUBLANES - 1:SUBLANES, :]

        def step(j, c):
            cu, cd = c
            cu = half(au, bu, mu_ref, pl.multiple_of(j * SUBLANES, SUBLANES), cu, False)
            cd = half(ad, bd, lam_ref, pl.multiple_of((n_sub - 1 - j) * SUBLANES, SUBLANES), cd, True)
            return cu, cd

        cu, cd = lax.fori_loop(0, n_sub, step, (carry[0:1, :], carry[1:2, :]), unroll=4)
        carry[0:1, :] = cu
        carry[1:2, :] = cd

    fw = lambda i: (i, 0)
    bw = lambda i: (n_tiles - 1 - i, 0)
    sds = jax.ShapeDtypeStruct((S, C), F32)
    return _rows(name, S, ts,
                 [(a_up, (ts, C), fw), (b_up, (ts, C), fw), (a_dn, (ts, C), bw), (b_dn, (ts, C), bw)],
                 [(sds, (ts, C), fw), (sds, (ts, C), bw)], body, scratch=[pltpu.VMEM((8, C), F32)])


def _halo_ex(arr, S, tm, width):
    per = tm // HALO
    last = S // HALO - 1
    return [
        (arr, (tm, width), lambda i, j: (i, 0)),
        (arr, (HALO, width), lambda i, j: (jnp.maximum(i * per - 1, 0), 0)),
        (arr, (HALO, width), lambda i, j: (jnp.minimum((i + 1) * per, last), 0)),
    ]


def _rg_gates_bwd(xc, bd, prm, lam_f, lam_b, h_f, h_b):
    S = xc.shape[0]
    tm = _tile(S, 256)
    n_tiles = S // tm

    def epi(i, accs, ex, out):
        pre = accs[0][...]
        xv = ex[0][...]
        prm_ref = ex[1]
        lams = (ex[2][...], ex[3][...])
        hprev = (_shift(_ext(ex[4], ex[5], ex[6], i, n_tiles), -1, tm),
                 _shift(_ext(ex[7], ex[8], ex[9], i, n_tiles), 1, tm))
        dxc = jnp.zeros_like(xv)
        rows = []
        dlam_rows = []
        for d in range(2):
            r, ig, sp, a, sq = _rg_gate_terms(pre, xv, prm_ref, d)
            lam = lams[d]
            da = lam * hprev[d]
            di = lam * sq * xv
            dxc = dxc + lam * sq * ig
            dsq = lam * ig * xv
            dlog_a = da * a - dsq * (a * a) / sq
            dpre_r = dlog_a * (-RG_C * sp) * r * (1.0 - r)
            dpre_i = di * ig * (1.0 - ig)
            out[0][:, d * 1024:d * 1024 + RG_W] = dpre_r.astype(BF16)
            out[0][:, d * 1024 + RG_W:(d + 1) * 1024] = dpre_i.astype(BF16)
            rows += [jnp.sum(dpre_r, axis=0, keepdims=True), jnp.sum(dpre_i, axis=0, keepdims=True)]
            dsp = jnp.sum(dlog_a * (-RG_C * r), axis=0, keepdims=True)
            dlam_rows.append(-dsp * _sigmoid(-prm_ref[4 + d:5 + d, :]))
        out[1][...] = dxc
        zero = jnp.zeros((2, RG_W), F32)
        _colsum_into(out[2], i, jnp.concatenate(rows + dlam_rows + [zero], axis=0))

    blk = (tm, RG_W)
    im = lambda i, j: (i, 0)
    extras = ([(xc, blk, im), (prm, (8, RG_W), _row0), (lam_f, blk, im), (lam_b, blk, im)]
              + _halo_ex(h_f, S, tm, RG_W) + _halo_ex(h_b, S, tm, RG_W))
    outs = [(jax.ShapeDtypeStruct((S, 4 * RG_W), BF16), (tm, 4 * RG_W), im),
            (jax.ShapeDtypeStruct((S, RG_W), F32), blk, im),
            (jax.ShapeDtypeStruct((8, RG_W), F32), (8, RG_W), _row0)]
    return _fused_mm("rg_gates_bwd", S, 4 * RG_W, RG_W, tm, 4 * RG_W, RG_W, [(xc, "mk"), (bd, "kn")], [(0, 1, 0)],
                     extras, outs, epi)


def _roll_rows(ext, off):
    if off == 0:
        return ext
    return pltpu.roll(ext, (-off) % ext.shape[0], 0)


def _conv_bwd(name, p, colblk, w, grads, mode):
    S = p.shape[0]
    ts = _tile(S, 512)
    n_tiles = S // ts
    C = w.shape[1]
    ng = len(grads)

    def body(*refs):
        p_refs = refs[0:3]
        g_refs = refs[3:3 + 3 * ng]
        w_ref = refs[3 + 3 * ng]
        dx_ref, dw_ref, db_ref = refs[4 + 3 * ng:]
        i = pl.program_id(0)
        ext_p = _ext(*p_refs, i, n_tiles)
        dn = _ext(*g_refs[0:3], i, n_tiles)
        for gi in range(1, ng):
            dn = dn + _ext(*g_refs[3 * gi:3 * gi + 3], i, n_tiles)
        if mode == "bias":
            dc = dn
        else:
            c = None
            for j in range(CONV_W):
                term = w_ref[j:j + 1, :] * _roll_rows(ext_p, j - 2)
                c = term if c is None else c + term
            sig = _sigmoid(c)
            s = c * sig
            if mode in ("q", "k"):
                scale = GDN_DK ** -0.5 if mode == "q" else 1.0
                parts = []
                for h in range(GDN_H):
                    cols = slice(h * GDN_DK, (h + 1) * GDN_DK)
                    sh = s[:, cols]
                    dnh = dn[:, cols]
                    rinv = lax.rsqrt(jnp.sum(sh * sh, axis=-1, keepdims=True) + EPS)
                    parts.append(scale * rinv * (dnh - sh * (rinv * rinv) * jnp.sum(dnh * sh, axis=-1, keepdims=True)))
                ds = jnp.concatenate(parts, axis=-1)
            else:
                ds = dn
            dc = ds * (sig * (1.0 + c * (1.0 - sig)))
        dx = None
        for j in range(CONV_W):
            term = w_ref[j:j + 1, :] * _shift(dc, 2 - j, ts)
            dx = term if dx is None else dx + term
        dx_ref[...] = dx.astype(BF16)
        dc_main = dc[HALO:HALO + ts]
        dw = jnp.concatenate([jnp.sum(dc_main * _shift(ext_p, j - 2, ts), axis=0, keepdims=True)
                              for j in range(CONV_W)], axis=0)
        _colsum_into(dw_ref, i, dw)
        _colsum_into(db_ref, i, jnp.sum(dc_main, axis=0, keepdims=True))

    ins = _halo_ins(p, S, ts, C, colblk)
    for garr in grads:
        ins += _halo_ins(garr, S, ts, C, 0)
    ins += [(w, (CONV_W, C), lambda i: (0, 0))]
    z0 = lambda i: (0, 0)
    outs = [(jax.ShapeDtypeStruct((S, C), BF16), (ts, C), lambda i: (i, 0)),
            (jax.ShapeDtypeStruct((CONV_W, C), F32), (CONV_W, C), z0),
            (jax.ShapeDtypeStruct((1, C), F32), (1, C), z0)]
    return _rows(name, S, ts, ins, outs, body)


def _gdn_scan_bwd(loc, do):
    S = do.shape[0]
    ts = _tile(S, GDN_TS)
    n_tiles = S // ts
    ncb = ts // CHUNK
    nch = S // CHUNK

    def body(*refs):
        ins = (refs[0:6], refs[6:12])
        outs = (refs[12:14], refs[14:16])
        dstate = refs[16]

        @pl.when(pl.program_id(0) == 0)
        def _():
            dstate[...] = jnp.zeros_like(dstate)

        def chunk(cc, carry):
            chains = []
            for d in range(2):
                c = ncb - 1 - cc if d == 0 else cc
                rows = pl.ds(pl.multiple_of(c * CHUNK, CHUNK), CHUNK)
                for h in range(GDN_H):
                    cols = slice(h * GDN_DK, (h + 1) * GDN_DK)
                    chains.append(dict(d=d, h=h, c=c, rows=rows, cols=cols, dsn=dstate[d * GDN_H + h]))
            for ch in chains:
                qd_ref, kd_ref, cd_ref, w_ref, a_ref, do_ref = ins[ch["d"]]
                rows, cols = ch["rows"], ch["cols"]
                dob = do_ref[rows, cols].astype(BF16)
                ch["dvn"] = (_dot(a_ref[ch["c"], ch["h"]], dob, 0, 0)
                             + _dot(kd_ref[rows, cols], ch["dsn"].astype(BF16), 1, 0))
                ch["qdo"] = _dot(qd_ref[rows, cols], dob, 0, 0)
            for ch in chains:
                w_ref = ins[ch["d"]][3]
                ch["wdvn"] = _dot(w_ref[ch["rows"], ch["cols"]], ch["dvn"].astype(BF16), 0, 0)
            for ch in chains:
                dvn_ref, ds_ref = outs[ch["d"]]
                cd_ref = ins[ch["d"]][2]
                col = ch["d"] * GDN_H + ch["h"]
                dvn_ref[ch["rows"], ch["cols"]] = ch["dvn"].astype(BF16)
                ds_ref[ch["c"], ch["h"]] = ch["dsn"].astype(BF16)
                dstate[ch["d"] * GDN_H + ch["h"]] = (ch["qdo"] + cd_ref[ch["c"], col:col + 1, :] * ch["dsn"]
                                                     - ch["wdvn"])
            return carry

        lax.fori_loop(0, ncb, chunk, 0)

    ins, outs = [], []
    for d in range(2):
        tix = _dir_tile(d, n_tiles, True)
        im = lambda i, tix=tix: (tix(i), 0)
        im4 = lambda i, tix=tix: (tix(i), 0, 0, 0)
        _, w, a, _, qd, kd = loc[d]
        ins += [(qd, (ts, GDN_W), im), (kd, (ts, GDN_W), im), (loc[2], (ncb, 8, 128), lambda i, tix=tix: (tix(i), 0, 0)),
                (w, (ts, GDN_W), im), (a, (ncb, GDN_H, CHUNK, CHUNK), im4), (do, (ts, GDN_W), im)]
        outs += [(jax.ShapeDtypeStruct((S, GDN_W), BF16), (ts, GDN_W), im),
                 (jax.ShapeDtypeStruct((nch, GDN_H, GDN_DK, GDN_DK), BF16), (ncb, GDN_H, GDN_DK, GDN_DK), im4)]
    res = _rows("gdn_scan_bwd", S, ts, ins, outs, body, scratch=[pltpu.VMEM((2 * GDN_H, GDN_DK, GDN_DK), F32)])
    return res[0:2], res[2:4]


def _gdn_local_bwd(q, k, v, bg, gcr, do, loc, fwd, adj):
    S = q.shape[0]
    ts = _tile(S, GDN_TS)
    ncb = ts // CHUNK

    def body(q_ref, k_ref, v_ref, bg_ref, gcr_ref, do_ref, *rest):
        per_dir = (rest[0:5], rest[5:10])
        dq_ref, dk_ref, dv_ref, dbg_ref, dbgr_ref = rest[10:15]
        ri, ci = _tri_masks()
        lane = lax.broadcasted_iota(jnp.int32, (CHUNK, 128), 1)
        rowi = lax.broadcasted_iota(jnp.int32, (CHUNK, 1), 0)
        ones8 = jnp.ones((SUBLANES, CHUNK), F32)

        def chunk(c, carry):
            r0 = pl.multiple_of(c * CHUNK, CHUNK)
            rows = pl.ds(r0, CHUNK)
            chains = []
            for h in range(GDN_H):
                cols = slice(h * GDN_DK, (h + 1) * GDN_DK)
                qh, kh, vh = q_ref[rows, cols], k_ref[rows, cols], v_ref[rows, cols]
                dob = do_ref[rows, cols].astype(BF16)
                both = _bdot(jnp.concatenate([qh, kh], axis=0), kh, 1, 1)
                for d in range(2):
                    chains.append(dict(h=h, d=d, cols=cols, qh=qh, kh=kh, vh=vh, dob=dob, qk=both[0:CHUNK],
                                       kk=both[CHUNK:2 * CHUNK], col=d * GDN_H + h))
            for ch in chains:
                m = _gdn_decay(bg_ref, gcr_ref, c, rows, r0, ch["col"], ch["d"] == 1, ri, ci)
                t_ref, s_ref, ds_ref, vn_ref, dvn_ref = per_dir[ch["d"]]
                h, cols = ch["h"], ch["cols"]
                ch["m"] = m
                ch["kb"] = ch["kh"] * m["beta"]
                ch["kbg"] = ch["kb"] * m["eg"]
                ch["t"] = t_ref[c, h]
                stb = s_ref[c, h]
                ch["dsn"] = ds_ref[c, h]
                vnb = vn_ref[rows, cols]
                dvnb = dvn_ref[rows, cols]
                ch["dcd"] = jnp.sum(jnp.sum(stb.astype(F32) * ch["dsn"].astype(F32), axis=1, keepdims=True),
                                    axis=0, keepdims=True)
                ch["dqd"] = _dot(ch["dob"], stb, 1, 1)
                ch["d_a"] = _dot(ch["dob"], vnb, 1, 1)
                ch["dkd"] = _bdot(vnb, ch["dsn"], 1, 1)
                ch["dw"] = -_dot(dvnb, stb, 1, 1)
                ch["dvb"] = _dot(ch["t"], dvnb, 1, 0)
                ch["d_t"] = _bdot(dvnb, ch["vh"] * m["beta"], 1, 1)
            for ch in chains:
                dwb = ch["dw"].astype(BF16)
                ch["d_t"] = ch["d_t"] + _bdot(dwb, ch["kbg"], 1, 1)
                ch["dkbg"] = _dot(ch["t"], dwb, 1, 0)
                ch["nn"] = ch["d_a"] * ch["m"]["dm"]
                ch["nn_q"] = _bdot(ch["nn"], ch["qh"], 0, 0)
                ch["nn_k"] = _bdot(ch["nn"], ch["kh"], 1, 0)
            for ch in chains:
                ch["x"] = _dot(ch["d_t"].astype(BF16), ch["t"], 1, 0)
            for ch in chains:
                d_l = -_dot(ch["t"], ch["x"].astype(BF16), 1, 0)
                ch["d_l"] = jnp.where(ch["m"]["strict"], d_l, 0.0)
                ch["mm"] = ch["d_l"] * ch["m"]["dm"]
            for ch in chains:
                m = ch["m"]
                ch["mm_kh"] = _bdot(ch["mm"], ch["kh"], 1, 0)
                ch["mm_kb"] = _bdot(ch["mm"], ch["kb"], 0, 0)
                l_mat = jnp.where(m["strict"], m["beta"] * ch["kk"] * m["dm"], 0.0)
                ch["e"] = ch["d_l"] * l_mat + ch["nn"] * ch["qk"]
                dbgr_ref[c, ch["col"]:ch["col"] + 1, :] = -_dot(ones8, ch["e"], 1, 0, HI)[0:1, :]
            acc_bg = jnp.zeros((CHUNK, 128), F32)
            acc = {}
            for ch in chains:
                m = ch["m"]
                beta, eg, egl = m["beta"], m["eg"], m["egl"]
                dkb = ch["mm_kh"] + ch["dkbg"] * eg
                dk_d = ch["mm_kb"] + ch["nn_q"] + ch["dkd"] * egl + dkb * beta
                dq_d = ch["nn_k"] + ch["dqd"] * eg
                dv_d = ch["dvb"] * beta
                dkd_kd = ch["dkd"] * (ch["kh"] * egl)
                dgc = (jnp.sum(ch["e"], axis=1, keepdims=True)
                       + jnp.sum(ch["dqd"] * (ch["qh"] * eg) - dkd_kd + ch["dkbg"] * ch["kbg"], axis=1, keepdims=True))
                dgl = jnp.sum(jnp.sum(dkd_kd, axis=1, keepdims=True), axis=0, keepdims=True) + ch["dcd"] * m["cd"]
                dgc = dgc + jnp.where(rowi == (0 if ch["d"] == 1 else CHUNK - 1), dgl, 0.0)
                dbeta = jnp.sum(dkb * ch["kh"] + ch["dvb"] * ch["vh"], axis=1, keepdims=True)
                acc_bg = acc_bg + jnp.where(lane == ch["col"], dbeta, 0.0) + jnp.where(lane == 8 + ch["col"], dgc, 0.0)
                if ch["d"] == 0:
                    acc[ch["h"]] = (dq_d, dk_d, dv_d)
                else:
                    dq0, dk0, dv0 = acc[ch["h"]]
                    dq_ref[rows, ch["cols"]] = dq0 + dq_d
                    dk_ref[rows, ch["cols"]] = dk0 + dk_d
                    dv_ref[rows, ch["cols"]] = dv0 + dv_d
            dbg_ref[rows, :] = acc_bg
            return carry

        lax.fori_loop(0, ncb, chunk, 0)

    im = lambda i: (i, 0)
    im4 = lambda i: (i, 0, 0, 0)
    blk = (ts, GDN_W)
    ins = [(q, blk, im), (k, blk, im), (v, blk, im), (bg, (ts, 128), im), (gcr, (ncb, 8, CHUNK), lambda i: (i, 0, 0)),
           (do, blk, im)]
    for d in range(2):
        ins += [(loc[d][3], (ncb, GDN_H, CHUNK, CHUNK), im4), (fwd[d][2], (ncb, GDN_H, GDN_DK, GDN_DK), im4),
                (adj[d][1], (ncb, GDN_H, GDN_DK, GDN_DK), im4), (fwd[d][1], blk, im), (adj[d][0], blk, im)]
    sds = jax.ShapeDtypeStruct((S, GDN_W), F32)
    outs = [(sds, blk, im), (sds, blk, im), (sds, blk, im), (jax.ShapeDtypeStruct((S, 128), F32), (ts, 128), im),
            (jax.ShapeDtypeStruct((S // CHUNK, 8, CHUNK), F32), (ncb, 8, CHUNK), lambda i: (i, 0, 0))]
    dq, dk, dv, dbg, dbg_rows = _rows("gdn_local_bwd", S, ts, ins, outs, body)
    dgc_cols = dbg_rows.transpose(0, 2, 1).reshape(S, 8)
    return dq, dk, dv, dbg + jnp.pad(dgc_cols, ((0, 0), (8, 112)))


def _gdn_prep_bwd(dbg_all, p, prm):
    S = p.shape[0]
    ts = _tile(S, 512)

    def body(dbg_ref, p_ref, prm_ref, dba_ref, dprm_ref):
        i = pl.program_id(0)
        raw = p_ref[...].astype(F32)
        dbg = dbg_ref[...]
        lane = lax.broadcasted_iota(jnp.int32, (1, 128), 1)
        is_g = (lane >= 8) & (lane < 16)
        ea = jnp.exp(prm_ref[0:1, :])
        arg = raw + prm_ref[1:2, :]
        g = jnp.where(is_g, -ea * _softplus(arg), 0.0)
        beta = _sigmoid(raw)
        dgc = jnp.where(is_g, dbg, 0.0)
        ri, ci = _tri_masks()
        lower = (ri >= ci).astype(F32)
        upper = (ri <= ci).astype(F32)
        dgs = []
        for c in range(ts // CHUNK):
            ch = dgc[c * CHUNK:(c + 1) * CHUNK]
            dgs.append(jnp.where(lane < 12, _dot(upper, ch, 1, 0, HI), _dot(lower, ch, 1, 0, HI)))
        dg = jnp.concatenate(dgs, axis=0)
        dalpha = jnp.where(is_g, dg * (-ea) * _sigmoid(arg), 0.0)
        dba_ref[...] = jnp.where(lane < 8, dbg * beta * (1.0 - beta), dalpha).astype(BF16)
        rows = jnp.concatenate([jnp.sum(dg * g, axis=0, keepdims=True), jnp.sum(dalpha, axis=0, keepdims=True),
                                jnp.zeros((6, 128), F32)], axis=0)
        _colsum_into(dprm_ref, i, rows)

    im = lambda i: (i, 0)
    z0 = lambda i: (0, 0)
    return _rows("gdn_prep_bwd", S, ts,
                 [(dbg_all, (ts, 128), im), (p, (ts, 128), lambda i: (i, COL_BA // 128)), (prm, (8, 128), z0)],
                 [(jax.ShapeDtypeStruct((S, 128), BF16), (ts, 128), im), (jax.ShapeDtypeStruct((8, 128), F32), (8, 128), z0)],
                 body)


def _mm_plain(name, M, N, K, tm, tn, tk, a, am, b, bm, dtype):
    return _fused_mm(name, M, N, K, tm, tn, tk, [(a, am), (b, bm)], [(0, 1, 0)], [],
                     [(jax.ShapeDtypeStruct((M, N), dtype), (tm, tn), _mn)],
                     lambda i, accs, ex, out: out[0].__setitem__(Ellipsis, accs[0][...].astype(dtype)))[0]


def _layer_bwd(x0, W, R, emit_big=None, emit_small=None):
    S = x0.shape[0]
    tm = _tile(S, 512)
    tk_s = _tile(S, 1024)
    G = {}

    def emit(**named):
        if emit_big is None:
            G.update(named)
            return None
        return emit_big(**named)

    def ffn_emit(prefix):
        return lambda **kw: emit(**{f"{prefix}_w_{k}": v for k, v in kw.items()})

    dx2, G["ffn2_norm"] = _ffn_bwd("ffn2b", R["dx3"], R["x2"], W["ffn2_norm"], R["h3"], R["a2"], R["b2"], R["f2"],
                                   W["ffn2_w_gate"], W["ffn2_w_up"], W["ffn2_w_down"], ffn_emit("ffn2"))
    tok = emit(w_out=_mm_plain("dw_out", D_MODEL, D_MODEL, S, D_MODEL, D_MODEL, tk_s, R["y"], "km", dx2, "kn", BF16))
    gn = W["gdn_norm"] if tok is None else W["gdn_norm"] + tok
    dy = _mm_plain("dy_mix", S, D_MODEL, D_MODEL, tm, D_MODEL, D_MODEL, dx2, "mk", W["w_out"], "nk", F32)
    p = R["p"]
    dhr, dgate, do, dz, G["gdn_norm"] = _mix_out_bwd(dy, R["h_f"], R["h_b"], R["o_f"], R["o_b"], p, gn)
    lam_b, lam_f = _rg_scan_adj("rg_scan_bwd", R["a_b"], dhr, R["a_f"], dhr)
    dpre, dxc_direct, d_rgprm = _rg_gates_bwd(R["xc"], R["bd"], R["rg_prm"], lam_f, lam_b, R["h_f"], R["h_b"])
    tmg = _tile(S, 512)
    dxc = _fused_mm("rg_dxc", S, RG_W, 4 * RG_W, tmg, RG_W, 4 * RG_W, [(dpre, "mk"), (R["bd"], "nk")], [(0, 1, 0)],
                    [(dxc_direct, (tmg, RG_W), _mn)], [(jax.ShapeDtypeStruct((S, RG_W), F32), (tmg, RG_W), _mn)],
                    lambda i, accs, ex, out: out[0].__setitem__(Ellipsis, ex[0][...] + accs[0][...]))[0]
    d_bd = _mm_plain("rg_dbd", RG_W, 4 * RG_W, S, RG_W, 4 * RG_W, tk_s, R["xc"], "km", dpre, "kn", F32)
    dx_rg, G["rg_conv_w"], G["rg_conv_b"] = _conv_bwd("rg_conv_bwd", p, 0, W["rg_conv_w"], [dxc], "bias")
    blocks = jnp.einsum("nigmj,nm->gnij", d_bd.reshape(RG_BLOCKS, RG_BLOCK, 4, RG_BLOCKS, RG_BLOCK),
                        jnp.eye(RG_BLOCKS, dtype=F32))
    G["rg_gate_a_w"] = jnp.stack([blocks[0], blocks[2]])
    G["rg_gate_x_w"] = jnp.stack([blocks[1], blocks[3]])
    G["rg_gate_a_b"] = jnp.stack([d_rgprm[0], d_rgprm[2]])
    G["rg_gate_x_b"] = jnp.stack([d_rgprm[1], d_rgprm[3]])
    G["rg_lambda"] = d_rgprm[4:6]
    adj = _gdn_scan_bwd(R["gdn_loc"], do)
    dq, dk, dv, dbg = _gdn_local_bwd(R["q"], R["k"], R["v"], R["bg"], R["gcr"], do, R["gdn_loc"], R["gdn_fwd"], adj)
    cw = W["gdn_conv_w"]
    dpq, dwq, _ = _conv_bwd("gdn_conv_q_bwd", p, 2, cw[:, 0:512], [dq], "q")
    dpk, dwk, _ = _conv_bwd("gdn_conv_k_bwd", p, 3, cw[:, 512:1024], [dk], "k")
    dpv, dwv, _ = _conv_bwd("gdn_conv_v_bwd", p, 4, cw[:, 1024:1536], [dv], "v")
    G["gdn_conv_w"] = jnp.concatenate([dwq, dwk, dwv], axis=1)
    dba, d_gprm = _gdn_prep_bwd(dbg, p, R["gdn_prm"])
    G["gdn_a_log"] = d_gprm[0, 8:16].reshape(2, GDN_H)
    G["gdn_dt_bias"] = d_gprm[1, 8:16].reshape(2, GDN_H)
    dp = jnp.concatenate([dx_rg, dgate, dpq, dpk, dpv, dz, dba], axis=1)
    tok = emit(w_in=_mm_plain("dw_in", D_MODEL, D_IN_PAD, S, D_MODEL, 640, tk_s, R["h2"], "km", dp, "kn", BF16))
    g_mix = W["mix_norm"] if tok is None else W["mix_norm"] + tok

    def epi_dx1(i, accs, ex, out):
        dx, dgt = _rmsnorm_bwd_tile(accs[0][...], ex[0][...], ex[1][...])
        out[0][...] = ex[2][...] + dx
        _colsum_into(out[1], i, jnp.sum(dgt, axis=0, keepdims=True))

    dx1, G["mix_norm"] = _fused_mm(
        "mix_dx", S, D_MODEL, D_IN_PAD, tm, D_MODEL, D_IN_PAD, [(dp, "mk"), (W["w_in"], "nk")], [(0, 1, 0)],
        [(R["x1"], (tm, D_MODEL), _mn), (g_mix, (1, D_MODEL), _row0), (dx2, (tm, D_MODEL), _mn)],
        [(jax.ShapeDtypeStruct((S, D_MODEL), F32), (tm, D_MODEL), _mn),
         (jax.ShapeDtypeStruct((1, D_MODEL), F32), (1, D_MODEL), _row0)], epi_dx1)
    G["final_norm"] = R["d_final_norm"]
    if emit_small is not None:
        emit_small(G)
    dx0, G["ffn1_norm"] = _ffn_bwd("ffn1b", dx1, x0, W["ffn1_norm"], R["h1"], R["a1"], R["b1"], R["f1"],
                                   W["ffn1_w_gate"], W["ffn1_w_up"], W["ffn1_w_down"], ffn_emit("ffn1"))
    return dx0, G


def _mesh_pos():
    x, y, c = lax.axis_index("x"), lax.axis_index("y"), lax.axis_index("c")
    return x, y, c, 4 * x + 2 * y + c


def _peer(x, y, c, r):
    px = 1 - x if r & 4 else x
    py = 1 - y if r & 2 else y
    pc = 1 - c if r & 1 else c
    return (px, py, pc), 4 * px + 2 * py + pc


_HBM = pl.BlockSpec(memory_space=pltpu.HBM)
_SEM = pl.BlockSpec(memory_space=pltpu.SEMAPHORE)


def _peer_copies(scatter, srcs, lands, send_sems, recv_sems):
    x, y, c, me = _mesh_pos()
    copies = []
    for a, (src, land) in enumerate(zip(srcs, lands)):
        for r in range(1, N_DEV):
            peer, peer_idx = _peer(x, y, c, r)
            copies.append(pltpu.make_async_remote_copy(
                src_ref=src.at[peer_idx] if scatter else src, dst_ref=land.at[r - 1] if scatter else land.at[me],
                send_sem=send_sems.at[a * 7 + r - 1], recv_sem=recv_sems.at[a * 7 + r - 1],
                device_id=peer, device_id_type=pl.DeviceIdType.MESH))
    return copies


def _exchange_start(name, scatter, arrays):
    slabs = arrays
    n = len(slabs)

    def body(*refs):
        srcs, lands = refs[0:n], refs[n:2 * n]
        send_sems, recv_sems = refs[2 * n], refs[2 * n + 1]
        token = refs[4 * n + 2]
        for cp in _peer_copies(scatter, srcs, lands, send_sems, recv_sems):
            cp.start()
        token[...] = jnp.zeros_like(token)

    land_shapes = [(N_DEV - 1,) + s.shape[1:] if scatter else (N_DEV,) + s.shape for s in slabs]
    n_sems = 7 * n
    out_shape = ([pltpu.SemaphoreType.DMA((n_sems,)), pltpu.SemaphoreType.DMA((n_sems,))]
                 + [pltpu.HBM(s.shape, s.dtype) for s in slabs]
                 + [pltpu.HBM(shp, s.dtype) for shp, s in zip(land_shapes, slabs)]
                 + [jax.ShapeDtypeStruct((8, 128), F32)])
    res = pl.pallas_call(
        body, name=name, out_shape=out_shape, in_specs=[_HBM] * (2 * n),
        out_specs=[_SEM, _SEM] + [_HBM] * (2 * n) + [pl.BlockSpec(memory_space=pltpu.VMEM)],
        input_output_aliases={i: 2 + i for i in range(2 * n)},
        compiler_params=pltpu.CompilerParams(has_side_effects=pltpu.SideEffectType.DATAFLOW_SIDE_EFFECTING),
    )(*[pltpu.with_memory_space_constraint(s, pltpu.HBM) for s in slabs],
      *[pltpu.with_memory_space_constraint(lax.empty(shp, s.dtype), pltpu.HBM) for shp, s in zip(land_shapes, slabs)])
    return dict(n=n, scatter=scatter, sems=res[0:2], srcs=res[2:2 + n], lands=res[2 + n:2 + 2 * n],
                token=res[2 + 2 * n][0, 0])


def _exchange_wait(name, started, after):
    n = started["n"]
    scatter = started["scatter"]

    def body(*refs):
        srcs, lands = refs[0:n], refs[n:2 * n]
        send_sems, recv_sems = refs[2 * n], refs[2 * n + 1]
        for cp in _peer_copies(scatter, srcs, lands, send_sems, recv_sems):
            cp.wait_send()
            cp.wait_recv()

    arrays = list(started["srcs"]) + list(started["lands"])
    res = pl.pallas_call(
        body, name=name, out_shape=[pltpu.HBM(a.shape, a.dtype) for a in arrays],
        in_specs=[_HBM] * (2 * n) + [_SEM, _SEM, pl.BlockSpec(memory_space=pl.ANY)], out_specs=[_HBM] * (2 * n),
        input_output_aliases={i: i for i in range(2 * n)},
        compiler_params=pltpu.CompilerParams(has_side_effects=pltpu.SideEffectType.DATAFLOW_SIDE_EFFECTING),
    )(*arrays, *started["sems"], after)
    return res[0:n], res[n:2 * n]


def _all_gather(name, arrays):
    n = len(arrays)

    def body(*refs):
        ins = refs[:n]
        outs = refs[n:2 * n]
        token = refs[2 * n]
        send_sems, recv_sems, local_sems = refs[2 * n + 1:]
        token[...] = jnp.zeros_like(token)
        x, y, c, me = _mesh_pos()
        sibling = (x, y, 1 - c)
        chips = [(1 - x, y), (x, 1 - y), (1 - x, 1 - y)]

        def idx(px, py, pc):
            return 4 * px + 2 * py + pc

        def copy(a, k, block, to, src=None):
            slot = outs[a].at[idx(*block)]
            return pltpu.make_async_remote_copy(
                src_ref=slot if src is None else src, dst_ref=slot, send_sem=send_sems.at[a * 7 + k],
                recv_sem=recv_sems.at[a * 7 + k], device_id=to, device_id_type=pl.DeviceIdType.MESH)

        locals_, sends = [], []
        for a in range(n):
            loc = pltpu.make_async_copy(ins[a], outs[a].at[me], local_sems.at[a])
            loc.start()
            locals_.append(loc)
            sends.append(copy(a, 0, (x, y, c), sibling, src=ins[a]))
            sends += [copy(a, 1 + j, (x, y, c), (*chip, c), src=ins[a]) for j, chip in enumerate(chips)]
        for cp in sends:
            cp.start()
        passed = []
        for a in range(n):
            for j, chip in enumerate(chips):
                copy(a, 1 + j, (*chip, c), (x, y, c)).wait_recv()
                fwd = copy(a, 4 + j, (*chip, c), sibling)
                fwd.start()
                passed.append(fwd)
        for a in range(n):
            copy(a, 0, sibling, (x, y, c)).wait_recv()
            for j, chip in enumerate(chips):
                copy(a, 4 + j, (*chip, 1 - c), (x, y, c)).wait_recv()
        for cp in sends + passed:
            cp.wait_send()
        for loc in locals_:
            loc.wait()

    any_spec = pl.BlockSpec(memory_space=pl.ANY)
    res = pl.pallas_call(
        body, name=name, in_specs=[any_spec] * n, out_specs=[any_spec] * n + [pl.BlockSpec(memory_space=pltpu.VMEM)],
        out_shape=[jax.ShapeDtypeStruct((N_DEV,) + a.shape, a.dtype) for a in arrays]
        + [jax.ShapeDtypeStruct((8, 128), F32)],
        scratch_shapes=[pltpu.SemaphoreType.DMA((7 * n,)), pltpu.SemaphoreType.DMA((7 * n,)),
                        pltpu.SemaphoreType.DMA((n,))],
        compiler_params=pltpu.CompilerParams(has_side_effects=True),
    )(*arrays)
    return res[:n], res[n][0, 0]


def _adamw_math(w, g, m, v):
    m2 = ADAM_B1 * m + (1.0 - ADAM_B1) * g
    v2 = ADAM_B2 * v + (1.0 - ADAM_B2) * (g * g)
    m_hat = m2 / (1.0 - ADAM_B1 ** ADAM_STEP)
    v_hat = v2 / (1.0 - ADAM_B2 ** ADAM_STEP)
    delta = -ADAM_LR * (m_hat / (jnp.sqrt(v_hat) + ADAM_EPS) + ADAM_WD * w)
    return delta, m2, v2


def _adamw_slabs(name, src, land, me, w, m, v, tr):
    R, C = w.shape

    def body(me_ref, own_ref, land_ref, w_ref, m_ref, v_ref, g_ref, d_ref, m2_ref, v2_ref):
        g = own_ref[0].astype(F32)
        for s in range(N_DEV - 1):
            g = g + land_ref[s].astype(F32)
        delta, m2, v2 = _adamw_math(w_ref[...], g, m_ref[...], v_ref[...])
        g_ref[...] = g
        d_ref[...] = delta
        m2_ref[...] = m2
        v2_ref[...] = v2

    im = lambda i, me_ref: (i, 0)
    grid_spec = pltpu.PrefetchScalarGridSpec(
        num_scalar_prefetch=1, grid=(R // tr,),
        in_specs=[pl.BlockSpec((1, tr, C), lambda i, me_ref: (me_ref[0], i, 0)),
                  pl.BlockSpec((N_DEV - 1, tr, C), lambda i, me_ref: (0, i, 0)),
                  pl.BlockSpec((tr, C), im), pl.BlockSpec((tr, C), im), pl.BlockSpec((tr, C), im)],
        out_specs=[pl.BlockSpec((tr, C), im)] * 4)
    return pl.pallas_call(body, name=name, grid_spec=grid_spec, out_shape=[jax.ShapeDtypeStruct((R, C), F32)] * 4,
                          compiler_params=_cp(1))(me.reshape(1).astype(jnp.int32), src, land, w, m, v)


def _sum_slots(name, slots):
    _, R, C = slots.shape

    def body(s_ref, o_ref):
        g = s_ref[0]
        for s in range(1, N_DEV):
            g = g + s_ref[s]
        o_ref[...] = g

    return _rows(name, R, R, [(slots, (N_DEV, R, C), lambda i: (0, 0, 0))],
                 [(jax.ShapeDtypeStruct((R, C), F32), (R, C), lambda i: (0, 0))], body)[0]


def _adamw_packed(name, g, w, m, v):
    R, C = g.shape

    def body(g_ref, w_ref, m_ref, v_ref, d_ref, m2_ref, v2_ref):
        delta, m2, v2 = _adamw_math(w_ref[...], g_ref[...], m_ref[...], v_ref[...])
        d_ref[...] = delta
        m2_ref[...] = m2
        v2_ref[...] = v2

    im = lambda i: (0, 0)
    sds = jax.ShapeDtypeStruct((R, C), F32)
    return _rows(name, R, R, [(a, (R, C), im) for a in (g, w, m, v)], [(sds, (R, C), im)] * 3, body)


def _pack(arrays):
    rows = []
    for a in arrays:
        flat = a.reshape(-1).astype(F32)
        pad = (-flat.shape[0]) % 128
        rows.append(jnp.pad(flat, (0, pad)).reshape(-1, 128))
    out = jnp.concatenate(rows, axis=0)
    return jnp.pad(out, ((0, (-out.shape[0]) % 8), (0, 0)))


def _unpack(packed, shapes):
    lead = packed.shape[:-2]
    outs = []
    r = 0
    for shp in shapes:
        n = math.prod(shp)
        nr = -(-n // 128)
        flat = packed[..., r:r + nr, :].reshape(lead + (nr * 128,))[..., :n]
        outs.append(flat.reshape(lead + tuple(shp)))
        r += nr
    return outs


FFN1_BIG = ["ffn1_w_gate", "ffn1_w_up", "ffn1_w_down"]
MIX_BIG = ["w_in", "w_out"]
FFN2_BIG = ["ffn2_w_gate", "ffn2_w_up", "ffn2_w_down"]
BIG = FFN1_BIG + MIX_BIG + FFN2_BIG
COL_SHARDED = {"ffn1_w_gate", "ffn1_w_up", "w_in", "ffn2_w_gate", "ffn2_w_up"}
SMALL_SHARDED = ["rg_conv_w", "rg_gate_a_b", "rg_gate_x_b", "rg_lambda", "gdn_conv_w"]
WEIGHTS = ["ffn1_norm", "ffn1_w_gate", "ffn1_w_up", "ffn1_w_down", "mix_norm", "w_in", "w_out", "rg_conv_w", "rg_conv_b",
           "rg_gate_a_w", "rg_gate_a_b", "rg_gate_x_w", "rg_gate_x_b", "rg_lambda", "gdn_conv_w", "gdn_a_log",
           "gdn_dt_bias", "gdn_norm", "ffn2_norm", "ffn2_w_gate", "ffn2_w_up", "ffn2_w_down", "final_norm"]
SMALL = [n for n in WEIGHTS if n not in BIG]
ROW_VECTORS = {"ffn1_norm", "mix_norm", "ffn2_norm", "gdn_norm", "rg_conv_b", "final_norm"}
ROW_TILE = {"ffn1_w_gate": 256, "ffn1_w_up": 256, "ffn1_w_down": 176, "w_in": 256, "w_out": 64,
            "ffn2_w_gate": 256, "ffn2_w_up": 256, "ffn2_w_down": 176}


def _unshard_cols(g):
    return g.transpose(1, 0, 2).reshape(g.shape[1], N_DEV * g.shape[2])


def _to_slabs(name, g):
    if name in COL_SHARDED:
        r, ctot = g.shape
        return g.reshape(r, N_DEV, ctot // N_DEV).transpose(1, 0, 2)
    return g.reshape(N_DEV, g.shape[0] // N_DEV, g.shape[1])


def _step(x, target, w, m, v):
    _, _, _, me = _mesh_pos()
    def unshard(n, gth):
        full = _unshard_cols(gth) if n in COL_SHARDED else gth.reshape(-1, gth.shape[-1])
        return jnp.pad(full, ((0, 0), (0, D_IN_PAD - D_IN))) if n == "w_in" else full

    def landed(started, name, after):
        srcs, lands = _exchange_wait(name, started, after)
        def with_own(src, land):
            slot = lax.broadcasted_iota(jnp.int32, (N_DEV,) + (1,) * src.ndim, 0)
            return jnp.where(slot == me, src[None], land)

        return [with_own(src, land) for src, land in zip(srcs, lands)]

    up_names = ["ffn1_w_gate", "ffn1_w_up"]
    first, tok = _all_gather("gather_ffn1", [w[n].astype(BF16) for n in up_names])
    W = {n: unshard(n, gth) for n, gth in zip(up_names, first)}
    small_shards = [w[n] for n in SMALL_SHARDED]
    st_down = _exchange_start("gather_ffn1_down_start", False, [(w["ffn1_w_down"] + tok).astype(BF16)])
    st_mix = _exchange_start("gather_mix_start", False,
                             [(w[n] + tok).astype(BF16) for n in MIX_BIG] + [_pack(small_shards) + tok])
    st_ffn2 = _exchange_start("gather_ffn2_start", False, [(w[n] + tok).astype(BF16) for n in FFN2_BIG])
    for n in SMALL:
        if n not in SMALL_SHARDED:
            W[n] = w[n]
    W["ffn1_norm"] = w["ffn1_norm"] + (st_down["token"] + st_mix["token"] + st_ffn2["token"])

    def more(stage, after):
        if stage == "ffn1_down":
            return {"ffn1_w_down": unshard("ffn1_w_down", landed(st_down, "gather_ffn1_down_wait", after)[0])}
        if stage == "ffn2":
            return {n: unshard(n, gth) for n, gth in zip(FFN2_BIG, landed(st_ffn2, "gather_ffn2_wait", after))}
        got = landed(st_mix, "gather_mix_wait", after)
        new = {n: unshard(n, gth) for n, gth in zip(MIX_BIG, got)}
        for n, gth in zip(SMALL_SHARDED, _unpack(got[-1], [s.shape for s in small_shards])):
            new[n] = jnp.moveaxis(gth, 0, -2).reshape(gth.shape[1:-1] + (N_DEV * gth.shape[-1],))
        return new

    R = _layer_fwd(x, target, W, more)
    W = R["W"]
    pending = []

    def emit_big(**named):
        slabs = [_to_slabs(n, g[:, :D_IN] if n == "w_in" else g) for n, g in named.items()]
        started = _exchange_start(f"scatter_start_{len(pending)}", True, slabs)
        pending.append((list(named), started))
        return started["token"]

    small_started = []

    def emit_small(G):
        packed = _pack([G[n] for n in SMALL if n != "ffn1_norm"])
        small_started.append(_exchange_start("gather_small_start", False, [packed]))

    grad_x, G = _layer_bwd(x, W, R, emit_big, emit_small)
    loss = lax.psum(R["loss"][0, 0], ("x", "y", "c"))
    out = {}

    def finish(i, after):
        names, started = pending[i]
        srcs, lands = _exchange_wait(f"scatter_wait_{i}", started, after)
        for n, src, land in zip(names, srcs, lands):
            out[n] = _adamw_slabs(f"adamw_{n}", src, land, me, w[n], m[n], v[n], ROW_TILE[n])

    n_early = len(pending) - 2
    for i in range(n_early):
        finish(i, grad_x)
    early = [n for n in SMALL if n != "ffn1_norm"]
    srcs, lands = _exchange_wait("gather_small_wait", small_started[0], grad_x)
    slot = lax.broadcasted_iota(jnp.int32, (N_DEV, 1, 1), 0)
    slots = jnp.where(slot == me, srcs[0][None], lands[0])
    reduced = dict(zip(early, _unpack(_sum_slots("sum_small_grads", slots), [G[n].shape for n in early])))
    late = _all_gather("gather_ffn1_norm_grad", [_pack([G["ffn1_norm"]])])[0][0]
    reduced["ffn1_norm"] = _unpack(_sum_slots("sum_ffn1_norm_grad", late), [G["ffn1_norm"].shape])[0]
    g_small = []
    for n in SMALL:
        g = reduced[n]
        if n in SMALL_SHARDED:
            per = g.shape[-1] // N_DEV
            g = lax.dynamic_slice_in_dim(g, me * per, per, axis=g.ndim - 1)
        g_small.append(g.reshape(w[n].shape))
    shapes = [w[n].shape for n in SMALL]
    d_p, m_p, v_p = _adamw_packed("adamw_small", _pack(g_small), _pack([w[n] for n in SMALL]),
                                  _pack([m[n] for n in SMALL]), _pack([v[n] for n in SMALL]))
    for n, g, d_, m_, v_ in zip(SMALL, g_small, _unpack(d_p, shapes), _unpack(m_p, shapes), _unpack(v_p, shapes)):
        out[n] = (g, d_, m_, v_)
    for i in range(n_early, len(pending)):
        finish(i, d_p)
    return loss, grad_x, out


def kernel(x, ffn1_norm, ffn1_w_gate, ffn1_w_up, ffn1_w_down, mix_norm, w_in, w_out, rg_conv_w, rg_conv_b, rg_gate_a_w, rg_gate_a_b, rg_gate_x_w, rg_gate_x_b, rg_lambda, gdn_conv_w, gdn_a_log, gdn_dt_bias, gdn_norm, ffn2_norm, ffn2_w_gate, ffn2_w_up, ffn2_w_down, final_norm, loss_target, m_ffn1_norm, m_ffn1_w_gate, m_ffn1_w_up, m_ffn1_w_down, m_mix_norm, m_w_in, m_w_out, m_rg_conv_w, m_rg_conv_b, m_rg_gate_a_w, m_rg_gate_a_b, m_rg_gate_x_w, m_rg_gate_x_b, m_rg_lambda, m_gdn_conv_w, m_gdn_a_log, m_gdn_dt_bias, m_gdn_norm, m_ffn2_norm, m_ffn2_w_gate, m_ffn2_w_up, m_ffn2_w_down, m_final_norm, v_ffn1_norm, v_ffn1_w_gate, v_ffn1_w_up, v_ffn1_w_down, v_mix_norm, v_w_in, v_w_out, v_rg_conv_w, v_rg_conv_b, v_rg_gate_a_w, v_rg_gate_a_b, v_rg_gate_x_w, v_rg_gate_x_b, v_rg_lambda, v_gdn_conv_w, v_gdn_a_log, v_gdn_dt_bias, v_gdn_norm, v_ffn2_norm, v_ffn2_w_gate, v_ffn2_w_up, v_ffn2_w_down, v_final_norm):
    args = dict(locals())
    orig_shapes = {n: args[n].shape for n in WEIGHTS}

    def local(prefix):
        d = {}
        for n in WEIGHTS:
            a = args[prefix + n]
            d[n] = a.reshape(1, -1) if n in ROW_VECTORS else a[0]
        return d

    loss, grad_x, out = _step(x[0], loss_target[0], local(""), local("m_"), local("v_"))
    res = [loss, grad_x[None]]
    for k in range(4):
        res += [out[n][k].reshape(orig_shapes[n]) for n in WEIGHTS]
    return tuple(res)
```

```python
import functools
import math

import jax
import jax.numpy as jnp
from jax import lax
from jax.experimental import pallas as pl
from jax.experimental.pallas import tpu as pltpu

F32, BF16 = jnp.float32, jnp.bfloat16

D_MODEL = 1024
D_FF = 2816
RG_W = 512
RG_BLOCKS = 8
RG_BLOCK = 64
RG_C = 8.0
CONV_W = 4
GDN_H = 4
GDN_DK = 128
CHUNK = 64
EPS = 1e-6
D_IN = 3088
D_IN_PAD = 3200
COL_BA = 3072
N_DEV = 8
HALO = 16
VMEM_LIMIT = 48 * 1024 * 1024
VMEM_CAP = 60 * 1024 * 1024

ADAM_LR = 0.001
ADAM_B1 = 0.9
ADAM_B2 = 0.999
ADAM_EPS = 1e-08
ADAM_WD = 0.01
ADAM_STEP = 10

HI = lax.Precision.HIGHEST


def _cp(n, vmem_limit=None):
    return pltpu.CompilerParams(dimension_semantics=("arbitrary",) * n,
                                vmem_limit_bytes=VMEM_LIMIT if vmem_limit is None else vmem_limit)


def _matmul_vmem_limit(block_bytes, acc_bytes):
    need = 2 * block_bytes + 2 * acc_bytes
    return int(min(VMEM_CAP, max(VMEM_LIMIT, need * 4 // 3)))


def _tile(n, pref):
    return min(n, pref)


def _sigmoid(x):
    return 0.5 * jnp.tanh(0.5 * x) + 0.5


def _softplus(x):
    return jnp.maximum(x, 0.0) + jnp.log(1.0 + jnp.exp(-jnp.abs(x)))


def _dot(a, b, ca, cb, prec=None):
    return lax.dot_general(a, b, (((ca,), (cb,)), ((), ())), preferred_element_type=F32, precision=prec)


def _fused_mm(name, M, N, K, tm, tn, tk, ops, pairs, extras, outs, epilogue):
    nm, nn, nk = M // tm, N // tn, K // tk
    assert nm * tm == M and nn * tn == N and nk * tk == K, (name, M, N, K, tm, tn, tk)
    spec_of = {
        "mk": pl.BlockSpec((tm, tk), lambda i, j, k: (i, k)),
        "km": pl.BlockSpec((tk, tm), lambda i, j, k: (k, i)),
        "kn": pl.BlockSpec((tk, tn), lambda i, j, k: (k, j)),
        "nk": pl.BlockSpec((tn, tk), lambda i, j, k: (j, k)),
    }
    in_specs = [spec_of[m] for _, m in ops]
    in_specs += [pl.BlockSpec(bs, lambda i, j, k, im=im: im(i, j)) for _, bs, im in extras]
    out_specs = [pl.BlockSpec(bs, lambda i, j, k, im=im: im(i, j)) for _, bs, im in outs]
    n_ops, n_ex, n_out = len(ops), len(extras), len(outs)
    n_acc = 1 + max(g for _, _, g in pairs)
    modes = [m for _, m in ops]

    def body(*refs):
        op_refs = refs[:n_ops]
        ex_refs = refs[n_ops:n_ops + n_ex]
        out_refs = refs[n_ops + n_ex:n_ops + n_ex + n_out]
        accs = refs[n_ops + n_ex + n_out:]
        i = pl.program_id(0)
        k = pl.program_id(2)
        def dots():
            vals = [r[...].astype(BF16) for r in op_refs]
            for ia, ib, g in pairs:
                yield g, _dot(vals[ia], vals[ib], 1 if modes[ia] == "mk" else 0, 0 if modes[ib] == "kn" else 1)

        if nk == 1:
            sums = [None] * n_acc
            for g, d in dots():
                sums[g] = d if sums[g] is None else sums[g] + d
            epilogue(i, [_Held(s) for s in sums], ex_refs, out_refs)
            return

        @pl.when(k == 0)
        def _():
            for a in accs:
                a[...] = jnp.zeros_like(a)

        for g, d in dots():
            accs[g][...] += d

        @pl.when(k == nk - 1)
        def _():
            epilogue(i, accs, ex_refs, out_refs)

    op_block = {"mk": tm * tk, "km": tm * tk, "kn": tk * tn, "nk": tk * tn}
    block_bytes = sum(op_block[m] * a.dtype.itemsize for a, m in ops)
    block_bytes += sum(math.prod(bs) * jnp.dtype(a.dtype).itemsize for a, bs, _ in list(extras) + list(outs))
    res = pl.pallas_call(
        body, name=name, grid=(nm, nn, nk), in_specs=in_specs, out_specs=out_specs,
        out_shape=[o for o, _, _ in outs],
        scratch_shapes=[pltpu.VMEM((tm, tn), F32)] * (n_acc if nk > 1 else 0),
        compiler_params=_cp(3, _matmul_vmem_limit(block_bytes, n_acc * tm * tn * 4)),
    )(*[a for a, _ in ops], *[a for a, _, _ in extras])
    return res


class _Held:
    def __init__(self, value):
        self.value = value

    def __getitem__(self, idx):
        return self.value[idx]


def _mn(i, j):
    return (i, j)


def _row0(i, j):
    return (0, 0)


def _rows(name, S, ts, ins, outs, body, scratch=()):
    return pl.pallas_call(
        body, name=name, grid=(S // ts,),
        in_specs=[pl.BlockSpec(bs, im) for _, bs, im in ins],
        out_specs=[pl.BlockSpec(bs, im) for _, bs, im in outs],
        out_shape=[o for o, _, _ in outs],
        scratch_shapes=list(scratch),
        compiler_params=_cp(1),
    )(*[a for a, _, _ in ins])


def _halo_ins(arr, S, ts, width, colblk):
    per = ts // HALO
    last = S // HALO - 1
    return [
        (arr, (ts, width), lambda i: (i, colblk)),
        (arr, (HALO, width), lambda i: (jnp.maximum(i * per - 1, 0), colblk)),
        (arr, (HALO, width), lambda i: (jnp.minimum((i + 1) * per, last), colblk)),
    ]


def _ext(main_ref, prev_ref, next_ref, i, n_tiles):
    prev = jnp.where(i > 0, prev_ref[...].astype(F32), 0.0)
    nxt = jnp.where(i < n_tiles - 1, next_ref[...].astype(F32), 0.0)
    return jnp.concatenate([prev, main_ref[...].astype(F32), nxt], axis=0)


def _shift(ext, off, ts):
    n = ext.shape[0]
    if off == 0:
        return ext[HALO:HALO + ts]
    return pltpu.roll(ext, (-off) % n, 0)[HALO:HALO + ts]


def _rmsnorm_fwd(name, x, g):
    S, D = x.shape
    ts = _tile(S, 512)

    def body(x_ref, g_ref, o_ref):
        xv = x_ref[...]
        r = lax.rsqrt(jnp.mean(xv * xv, axis=-1, keepdims=True) + EPS)
        o_ref[...] = (xv * r * g_ref[...]).astype(BF16)

    return _rows(name, S, ts,
                 [(x, (ts, D), lambda i: (i, 0)), (g, (1, D), lambda i: (0, 0))],
                 [(jax.ShapeDtypeStruct((S, D), BF16), (ts, D), lambda i: (i, 0))], body)[0]


def _rmsnorm_bwd_tile(dh, x, g):
    r = lax.rsqrt(jnp.mean(x * x, axis=-1, keepdims=True) + EPS)
    xhat = x * r
    dxn = dh * g
    dx = r * (dxn - xhat * jnp.mean(dxn * xhat, axis=-1, keepdims=True))
    return dx, dh * xhat


def _ffn_fwd(tag, x, h, wg, wu, wd):
    S = x.shape[0]
    tm = _tile(S, 1024)
    tn = 1408

    def epi_up(i, accs, ex, out):
        a = accs[0][...]
        b = accs[1][...]
        s = _sigmoid(a)
        sa = a * s
        out[0][...] = sa.astype(BF16)
        out[1][...] = (b * (s * (1.0 + a * (1.0 - s)))).astype(BF16)
        out[2][...] = (sa * b).astype(BF16)

    sds = jax.ShapeDtypeStruct((S, D_FF), BF16)
    a, b, f = _fused_mm(f"{tag}_up", S, D_FF, D_MODEL, tm, tn, D_MODEL,
                        [(h, "mk"), (wg, "kn"), (wu, "kn")], [(0, 1, 0), (0, 2, 1)], [],
                        [(sds, (tm, tn), _mn)] * 3, epi_up)

    def epi_down(i, accs, ex, out):
        out[0][...] = ex[0][...] + 0.5 * accs[0][...]

    if callable(wd):
        wd = wd(f)
    xo = _fused_mm(f"{tag}_down", S, D_MODEL, D_FF, tm, D_MODEL, 1408,
                   [(f, "mk"), (wd, "kn")], [(0, 1, 0)], [(x, (tm, D_MODEL), _mn)],
                   [(jax.ShapeDtypeStruct((S, D_MODEL), F32), (tm, D_MODEL), _mn)], epi_down)[0]
    return xo, a, b, f


def _conv_taps(ext, w_ref, ts):
    acc = None
    for j in range(CONV_W):
        term = w_ref[j:j + 1, :] * _shift(ext, j - 2, ts)
        acc = term if acc is None else acc + term
    return acc


def _l2norm_heads(s, scale):
    outs = []
    for h in range(GDN_H):
        sh = s[:, h * GDN_DK:(h + 1) * GDN_DK]
        outs.append(sh * (lax.rsqrt(jnp.sum(sh * sh, axis=-1, keepdims=True) + EPS) * scale))
    return jnp.concatenate(outs, axis=-1)


def _conv_fwd(name, p, colblk, w, bias, mode):
    S = p.shape[0]
    ts = _tile(S, 512)
    n_tiles = S // ts
    C = w.shape[1]

    def body(main, prev, nxt, w_ref, b_ref, o_ref):
        i = pl.program_id(0)
        c = _conv_taps(_ext(main, prev, nxt, i, n_tiles), w_ref, ts)
        if mode == "bias":
            o_ref[...] = c + b_ref[...]
        else:
            s = c * _sigmoid(c)
            if mode == "q":
                s = _l2norm_heads(s, GDN_DK ** -0.5)
            elif mode == "k":
                s = _l2norm_heads(s, 1.0)
            o_ref[...] = s

    ins = _halo_ins(p, S, ts, C, colblk) + [(w, (CONV_W, C), lambda i: (0, 0)), (bias, (1, C), lambda i: (0, 0))]
    return _rows(name, S, ts, ins, [(jax.ShapeDtypeStruct((S, C), F32), (ts, C), lambda i: (i, 0))], body)[0]


def _rg_gate_terms(pre, xc, prm_ref, d):
    r = _sigmoid(pre[:, d * 1024:d * 1024 + RG_W] + prm_ref[2 * d:2 * d + 1, :])
    ig = _sigmoid(pre[:, d * 1024 + RG_W:(d + 1) * 1024] + prm_ref[2 * d + 1:2 * d + 2, :])
    sp = _softplus(-prm_ref[4 + d:5 + d, :])
    log_a = -RG_C * r * sp
    a = jnp.exp(log_a)
    t = jnp.tanh(log_a)
    sq = jnp.sqrt(-2.0 * t / (1.0 - t))
    return r, ig, sp, a, sq


def _rg_gates_fwd(xc, bd, prm):
    S = xc.shape[0]
    tm = _tile(S, 256)

    def epi(i, accs, ex, out):
        pre = accs[0][...]
        xv = ex[0][...]
        for d in range(2):
            r, ig, sp, a, sq = _rg_gate_terms(pre, xv, ex[1], d)
            out[2 * d][...] = a
            out[2 * d + 1][...] = sq * ig * xv

    sds = jax.ShapeDtypeStruct((S, RG_W), F32)
    blk = (tm, RG_W)
    im = lambda i, j: (i, 0)
    return _fused_mm("rg_gates_fwd", S, 4 * RG_W, RG_W, tm, 4 * RG_W, RG_W,
                     [(xc, "mk"), (bd, "kn")], [(0, 1, 0)],
                     [(xc, blk, im), (prm, (8, RG_W), _row0)], [(sds, blk, im)] * 4, epi)


SUBLANES = 8


def _scan_rows(a, b, reverse):
    rows = lax.broadcasted_iota(jnp.int32, a.shape, 0)
    s = 1
    while s < SUBLANES:
        shift = SUBLANES - s if reverse else s
        a_sh = pltpu.roll(a, shift, 0)
        b_sh = pltpu.roll(b, shift, 0)
        valid = (rows < SUBLANES - s) if reverse else (rows >= s)
        b = jnp.where(valid, a * b_sh + b, b)
        a = jnp.where(valid, a * a_sh, a)
        s *= 2
    return a, b


def _rg_scan(name, a_f, b_f, a_b, b_b):
    S, C = a_f.shape
    ts = _tile(S, 512)
    n_tiles = S // ts

    def body(af, bf, ab, bb, hf, hb, carry):
        @pl.when(pl.program_id(0) == 0)
        def _():
            carry[...] = jnp.zeros_like(carry)

        n_sub = ts // SUBLANES

        def step(j, c):
            cf, cb = c
            r0 = pl.multiple_of(j * SUBLANES, SUBLANES)
            cum_a, h0 = _scan_rows(af[pl.ds(r0, SUBLANES), :], bf[pl.ds(r0, SUBLANES), :], False)
            h = h0 + cum_a * cf
            hf[pl.ds(r0, SUBLANES), :] = h
            cf = h[SUBLANES - 1:SUBLANES, :]
            r1 = pl.multiple_of((n_sub - 1 - j) * SUBLANES, SUBLANES)
            cum_a, h0 = _scan_rows(ab[pl.ds(r1, SUBLANES), :], bb[pl.ds(r1, SUBLANES), :], True)
            h = h0 + cum_a * cb
            hb[pl.ds(r1, SUBLANES), :] = h
            cb = h[0:1, :]
            return cf, cb

        cf, cb = lax.fori_loop(0, n_sub, step, (carry[0:1, :], carry[1:2, :]), unroll=4)
        carry[0:1, :] = cf
        carry[1:2, :] = cb

    fw = lambda i: (i, 0)
    bw = lambda i: (n_tiles - 1 - i, 0)
    sds = jax.ShapeDtypeStruct((S, C), F32)
    return _rows(name, S, ts,
                 [(a_f, (ts, C), fw), (b_f, (ts, C), fw), (a_b, (ts, C), bw), (b_b, (ts, C), bw)],
                 [(sds, (ts, C), fw), (sds, (ts, C), bw)], body, scratch=[pltpu.VMEM((8, C), F32)])


def _tri_masks():
    ri = lax.broadcasted_iota(jnp.int32, (CHUNK, CHUNK), 0)
    ci = lax.broadcasted_iota(jnp.int32, (CHUNK, CHUNK), 1)
    return ri, ci


def _gdn_prep_fwd(p, prm):
    S = p.shape[0]
    ts = _tile(S, 512)

    def body(p_ref, prm_ref, o_ref):
        raw = p_ref[...].astype(F32)
        lane = lax.broadcasted_iota(jnp.int32, (1, 128), 1)
        g = -jnp.exp(prm_ref[0:1, :]) * _softplus(raw + prm_ref[1:2, :])
        g = jnp.where((lane >= 8) & (lane < 16), g, 0.0)
        beta = _sigmoid(raw)
        ri, ci = _tri_masks()
        lower = (ri >= ci).astype(F32)
        upper = (ri <= ci).astype(F32)
        for c in range(ts // CHUNK):
            rows = slice(c * CHUNK, (c + 1) * CHUNK)
            gch = g[rows]
            gc = jnp.where(lane < 12, _dot(lower, gch, 1, 0, HI), _dot(upper, gch, 1, 0, HI))
            o_ref[rows, :] = jnp.where(lane < 8, beta[rows], gc)

    return _rows("gdn_prep_fwd", S, ts,
                 [(p, (ts, 128), lambda i: (i, COL_BA // 128)), (prm, (8, 128), lambda i: (0, 0))],
                 [(jax.ShapeDtypeStruct((S, 128), F32), (ts, 128), lambda i: (i, 0))], body)[0]


def _bdot(a, b, ca, cb):
    return _dot(a.astype(BF16), b.astype(BF16), ca, cb)


GDN_W = GDN_H * GDN_DK
GDN_TS = 256
LOCAL_CHUNKS = 2

def _gdn_decay(bg_ref, gcr_ref, c, rows, r0, col, rev, ri, ci):
    beta = bg_ref[rows, col:col + 1]
    gc = bg_ref[rows, 8 + col:9 + col]
    last = 0 if rev else CHUNK - 1
    gl = bg_ref[pl.ds(r0 + last, 1), 8 + col:9 + col]
    out = dict(beta=beta, gc=gc, gl=gl, eg=jnp.exp(gc), egl=jnp.exp(gl - gc), cd=jnp.exp(gl))
    if gcr_ref is not None:
        incl = (ri <= ci) if rev else (ri >= ci)
        out["strict"] = (ri < ci) if rev else (ri > ci)
        out["dm"] = jnp.where(incl, jnp.exp(jnp.where(incl, gc - gcr_ref[c, col:col + 1, :], 0.0)), 0.0)
    return out


def _dir_tile(d, n_tiles, flip):
    if (d == 1) != flip:
        return lambda i: n_tiles - 1 - i
    return lambda i: i


def _gdn_local_fwd(q, k, v, bg, gcr):
    S = q.shape[0]
    ts = _tile(S, GDN_TS)
    ncb = ts // CHUNK
    nch = S // CHUNK

    def body(q_ref, k_ref, v_ref, bg_ref, gcr_ref, *out_refs):
        ri, ci = _tri_masks()
        eye = (ri == ci).astype(F32)
        outs = (out_refs[0:6], out_refs[6:12])
        cd_ref = out_refs[12]

        def chunk(cc, carry):
            chains = []
            for c in (LOCAL_CHUNKS * cc + j for j in range(LOCAL_CHUNKS)):
                r0 = pl.multiple_of(c * CHUNK, CHUNK)
                rows = pl.ds(r0, CHUNK)
                for h in range(GDN_H):
                    cols = slice(h * GDN_DK, (h + 1) * GDN_DK)
                    qh, kh, vh = q_ref[rows, cols], k_ref[rows, cols], v_ref[rows, cols]
                    both = _bdot(jnp.concatenate([qh, kh], axis=0), kh, 1, 1)
                    for d in range(2):
                        chains.append(dict(c=c, r0=r0, rows=rows, h=h, d=d, cols=cols, qh=qh, kh=kh, vh=vh,
                                           qk=both[0:CHUNK], kk=both[CHUNK:2 * CHUNK]))
            for ch in chains:
                m = _gdn_decay(bg_ref, gcr_ref, ch["c"], ch["rows"], ch["r0"], ch["d"] * GDN_H + ch["h"], ch["d"] == 1,
                               ri, ci)
                ch["m"] = m
                ch["x"] = -jnp.where(m["strict"], m["beta"] * ch["kk"] * m["dm"], 0.0)
                ch["t"] = eye + ch["x"]
            for ch in chains:
                ch["pw"] = _bdot(ch["x"], ch["x"], 1, 0)
            for level in range(1, 6):
                last_level = level == 5
                for ch in chains:
                    rhs = ch["t"] if last_level else jnp.concatenate([ch["t"], ch["pw"]], axis=1)
                    ch["prod"] = _bdot(ch["pw"], rhs, 1, 0)
                for ch in chains:
                    ch["t"] = ch["t"] + ch["prod"][:, 0:CHUNK]
                    if not last_level:
                        ch["pw"] = ch["prod"][:, CHUNK:2 * CHUNK]
            for ch in chains:
                m = ch["m"]
                rhs = jnp.concatenate([ch["vh"] * m["beta"], ch["kh"] * (m["beta"] * m["eg"])], axis=1)
                ch["uw"] = _bdot(ch["t"], rhs, 1, 0)
            for ch in chains:
                u_ref, w_ref, a_ref, t_ref, qd_ref, kd_ref = outs[ch["d"]]
                m = ch["m"]
                c, rows = ch["c"], ch["rows"]
                col = ch["d"] * GDN_H + ch["h"]
                u_ref[rows, ch["cols"]] = ch["uw"][:, 0:GDN_DK]
                w_ref[rows, ch["cols"]] = ch["uw"][:, GDN_DK:2 * GDN_DK].astype(BF16)
                a_ref[c, ch["h"]] = (ch["qk"] * m["dm"]).astype(BF16)
                t_ref[c, ch["h"]] = _bdot(ch["t"], eye, 0, 0).astype(BF16)
                qd_ref[rows, ch["cols"]] = (ch["qh"] * m["eg"]).astype(BF16)
                kd_ref[rows, ch["cols"]] = (ch["kh"] * m["egl"]).astype(BF16)
                cd_ref[c, col:col + 1, :] = jnp.broadcast_to(m["cd"], (1, 128))
            return carry

        lax.fori_loop(0, ncb // LOCAL_CHUNKS, chunk, 0)

    im = lambda i: (i, 0)
    im4 = lambda i: (i, 0, 0, 0)
    ins = [(q, (ts, GDN_W), im), (k, (ts, GDN_W), im), (v, (ts, GDN_W), im), (bg, (ts, 128), im),
           (gcr, (ncb, 8, CHUNK), lambda i: (i, 0, 0))]
    per_dir = [(jax.ShapeDtypeStruct((S, GDN_W), F32), (ts, GDN_W), im),
               (jax.ShapeDtypeStruct((S, GDN_W), BF16), (ts, GDN_W), im),
               (jax.ShapeDtypeStruct((nch, GDN_H, CHUNK, CHUNK), BF16), (ncb, GDN_H, CHUNK, CHUNK), im4),
               (jax.ShapeDtypeStruct((nch, GDN_H, CHUNK, CHUNK), BF16), (ncb, GDN_H, CHUNK, CHUNK), im4),
               (jax.ShapeDtypeStruct((S, GDN_W), BF16), (ts, GDN_W), im),
               (jax.ShapeDtypeStruct((S, GDN_W), BF16), (ts, GDN_W), im)]
    cd_out = (jax.ShapeDtypeStruct((nch, 8, 128), F32), (ncb, 8, 128), lambda i: (i, 0, 0))
    res = _rows("gdn_local_fwd", S, ts, ins, per_dir * 2 + [cd_out], body)
    return res[0:6], res[6:12], res[12]


def _gdn_scan_fwd(loc):
    S = loc[0][0].shape[0]
    ts = _tile(S, GDN_TS)
    n_tiles = S // ts
    ncb = ts // CHUNK
    nch = S // CHUNK

    def body(*refs):
        ins = (refs[0:6], refs[6:12])
        outs = (refs[12:15], refs[15:18])
        state = refs[18]

        @pl.when(pl.program_id(0) == 0)
        def _():
            state[...] = jnp.zeros_like(state)

        def chunk(cc, carry):
            chains = []
            for d in range(2):
                c = cc if d == 0 else ncb - 1 - cc
                rows = pl.ds(pl.multiple_of(c * CHUNK, CHUNK), CHUNK)
                for h in range(GDN_H):
                    cols = slice(h * GDN_DK, (h + 1) * GDN_DK)
                    chains.append(dict(d=d, h=h, c=c, rows=rows, cols=cols, st=state[d * GDN_H + h]))
            for ch in chains:
                qd_ref, kd_ref, u_ref, w_ref, a_ref, cd_ref = ins[ch["d"]]
                rows, cols = ch["rows"], ch["cols"]
                lhs = jnp.concatenate([w_ref[rows, cols], qd_ref[rows, cols]], axis=0)
                ch["ws_qs"] = _dot(lhs, ch["st"].astype(BF16), 1, 0)
            for ch in chains:
                qd_ref, kd_ref, u_ref, w_ref, a_ref, cd_ref = ins[ch["d"]]
                rows, cols = ch["rows"], ch["cols"]
                vn = u_ref[rows, cols] - ch["ws_qs"][0:CHUNK]
                vnb = vn.astype(BF16)
                ch["vn"] = vn
                ch["avn"] = _dot(a_ref[ch["c"], ch["h"]], vnb, 1, 0)
                ch["kvn"] = _dot(kd_ref[rows, cols], vnb, 0, 0)
            for ch in chains:
                o_ref, vn_ref, s_ref = outs[ch["d"]]
                cd_ref = ins[ch["d"]][5]
                rows, cols = ch["rows"], ch["cols"]
                col = ch["d"] * GDN_H + ch["h"]
                o_ref[rows, cols] = ch["ws_qs"][CHUNK:2 * CHUNK] + ch["avn"]
                vn_ref[rows, cols] = ch["vn"].astype(BF16)
                s_ref[ch["c"], ch["h"]] = ch["st"].astype(BF16)
                state[ch["d"] * GDN_H + ch["h"]] = ch["st"] * cd_ref[ch["c"], col:col + 1, :] + ch["kvn"]
            return carry

        lax.fori_loop(0, ncb, chunk, 0)

    ins, outs = [], []
    for d in range(2):
        tix = _dir_tile(d, n_tiles, False)
        im = lambda i, tix=tix: (tix(i), 0)
        im4 = lambda i, tix=tix: (tix(i), 0, 0, 0)
        u, w, a, _, qd, kd = loc[d]
        ins += [(qd, (ts, GDN_W), im), (kd, (ts, GDN_W), im), (u, (ts, GDN_W), im), (w, (ts, GDN_W), im),
                (a, (ncb, GDN_H, CHUNK, CHUNK), im4), (loc[2], (ncb, 8, 128), lambda i, tix=tix: (tix(i), 0, 0))]
        outs += [(jax.ShapeDtypeStruct((S, GDN_W), F32), (ts, GDN_W), im),
                 (jax.ShapeDtypeStruct((S, GDN_W), BF16), (ts, GDN_W), im),
                 (jax.ShapeDtypeStruct((nch, GDN_H, GDN_DK, GDN_DK), BF16), (ncb, GDN_H, GDN_DK, GDN_DK), im4)]
    res = _rows("gdn_scan_fwd", S, ts, ins, outs, body, scratch=[pltpu.VMEM((2 * GDN_H, GDN_DK, GDN_DK), F32)])
    return res[0:3], res[3:6]


def _gelu(x):
    c = math.sqrt(2.0 / math.pi)
    t = jnp.tanh(c * (x + 0.044715 * x * x * x))
    return 0.5 * x * (1.0 + t), t


def _mix_out_fwd(h_f, h_b, o_f, o_b, p, gn):
    S = h_f.shape[0]
    ts = _tile(S, 512)

    def body(hf, hb, of, ob, gate, z, gn_ref, y_ref):
        ge, _ = _gelu(gate[...].astype(F32))
        y_ref[:, 0:RG_W] = ((hf[...] + hb[...]) * ge).astype(BF16)
        o = of[...] + ob[...]
        zv = z[...].astype(F32)
        sz = zv * _sigmoid(zv)
        for h in range(GDN_H):
            cols = slice(h * GDN_DK, (h + 1) * GDN_DK)
            oh = o[:, cols]
            n = oh * lax.rsqrt(jnp.mean(oh * oh, axis=-1, keepdims=True) + EPS) * gn_ref[...]
            y_ref[:, RG_W + h * GDN_DK:RG_W + (h + 1) * GDN_DK] = (n * sz[:, cols]).astype(BF16)

    blk = (ts, RG_W)
    im = lambda i: (i, 0)
    ins = [(h_f, blk, im), (h_b, blk, im), (o_f, blk, im), (o_b, blk, im),
           (p, blk, lambda i: (i, 1)), (p, blk, lambda i: (i, 5)), (gn, (1, GDN_DK), lambda i: (0, 0))]
    return _rows("mix_out_fwd", S, ts, ins,
                 [(jax.ShapeDtypeStruct((S, D_MODEL), BF16), (ts, D_MODEL), im)], body)[0]


def _loss_head(x, target, g):
    S, D = x.shape
    ts = _tile(S, 512)

    def body(x_ref, t_ref, g_ref, dx_ref, loss_ref, dg_ref):
        @pl.when(pl.program_id(0) == 0)
        def _():
            loss_ref[...] = jnp.zeros_like(loss_ref)
            dg_ref[...] = jnp.zeros_like(dg_ref)

        xv = x_ref[...]
        gv = g_ref[...]
        r = lax.rsqrt(jnp.mean(xv * xv, axis=-1, keepdims=True) + EPS)
        err = xv * r * gv - t_ref[...]
        loss_ref[...] += jnp.sum(err * err) * (0.5 / D)
        dx, dgt = _rmsnorm_bwd_tile(err * (1.0 / D), xv, gv)
        dx_ref[...] = dx
        dg_ref[...] += jnp.sum(dgt, axis=0, keepdims=True)

    im = lambda i: (i, 0)
    z = lambda i: (0, 0)
    return _rows("loss_head", S, ts,
                 [(x, (ts, D), im), (target, (ts, D), im), (g, (1, D), z)],
                 [(jax.ShapeDtypeStruct((S, D), F32), (ts, D), im),
                  (jax.ShapeDtypeStruct((8, 128), F32), (8, 128), z),
                  (jax.ShapeDtypeStruct((1, D), F32), (1, D), z)], body)


def _block_diag(w):
    n = w.shape[0]
    return jnp.einsum("nij,nm->nimj", w, jnp.eye(n, dtype=w.dtype)).reshape(n * w.shape[1], n * w.shape[2])


def _rg_bd(a_w, x_w):
    return jnp.concatenate([_block_diag(a_w[0]), _block_diag(x_w[0]), _block_diag(a_w[1]), _block_diag(x_w[1])],
                           axis=1).astype(BF16)


def _rg_prm(ba, bx, lam):
    return jnp.concatenate([ba[0:1], bx[0:1], ba[1:2], bx[1:2], lam, jnp.zeros((2, RG_W), F32)], axis=0)


def _gdn_prm(a_log, dt_bias):
    rows = jnp.zeros((8, 128), F32)
    rows = rows.at[0, 8:16].set(a_log.reshape(-1))
    return rows.at[1, 8:16].set(dt_bias.reshape(-1))


def _gc_rows(bg):
    S = bg.shape[0]
    return bg[:, 8:16].reshape(S // CHUNK, CHUNK, 8).transpose(0, 2, 1)


def _layer_fwd(x0, target, W, more=None):
    S = x0.shape[0]
    R = {}
    R["h1"] = _rmsnorm_fwd("rms1", x0, W["ffn1_norm"])
    late_wd = {}

    def ffn1_wd(after):
        late_wd.update(more("ffn1_down", after))
        return late_wd["ffn1_w_down"]

    R["x1"], R["a1"], R["b1"], R["f1"] = _ffn_fwd("ffn1", x0, R["h1"], W["ffn1_w_gate"], W["ffn1_w_up"],
                                                  ffn1_wd if more is not None else W["ffn1_w_down"])
    if more is not None:
        W = {**W, **late_wd, **more("mixer", R["x1"])}
    R["h2"] = _rmsnorm_fwd("rms2", R["x1"], W["mix_norm"])
    tm = _tile(S, 512)
    tmp = _tile(S, 1024)
    R["p"] = _fused_mm("in_proj", S, D_IN_PAD, D_MODEL, tmp, 640, D_MODEL, [(R["h2"], "mk"), (W["w_in"], "kn")],
                       [(0, 1, 0)], [], [(jax.ShapeDtypeStruct((S, D_IN_PAD), BF16), (tmp, 640), _mn)],
                       lambda i, accs, ex, out: out[0].__setitem__(Ellipsis, accs[0][...].astype(BF16)))[0]
    p = R["p"]
    R["xc"] = _conv_fwd("rg_conv_fwd", p, 0, W["rg_conv_w"], W["rg_conv_b"], "bias")
    R["bd"] = _rg_bd(W["rg_gate_a_w"], W["rg_gate_x_w"])
    R["rg_prm"] = _rg_prm(W["rg_gate_a_b"], W["rg_gate_x_b"], W["rg_lambda"])
    a_f, b_f, a_b, b_b = _rg_gates_fwd(R["xc"], R["bd"], R["rg_prm"])
    R["a_f"], R["a_b"] = a_f, a_b
    R["h_f"], R["h_b"] = _rg_scan("rg_scan_fwd", a_f, b_f, a_b, b_b)
    zero_b = jnp.zeros((1, RG_W), F32)
    cw = W["gdn_conv_w"]
    R["q"] = _conv_fwd("gdn_conv_q", p, 2, cw[:, 0:512], zero_b, "q")
    R["k"] = _conv_fwd("gdn_conv_k", p, 3, cw[:, 512:1024], zero_b, "k")
    R["v"] = _conv_fwd("gdn_conv_v", p, 4, cw[:, 1024:1536], zero_b, "v")
    R["gdn_prm"] = _gdn_prm(W["gdn_a_log"], W["gdn_dt_bias"])
    R["bg"] = _gdn_prep_fwd(p, R["gdn_prm"])
    R["gcr"] = _gc_rows(R["bg"])
    R["gdn_loc"] = _gdn_local_fwd(R["q"], R["k"], R["v"], R["bg"], R["gcr"])
    R["gdn_fwd"] = _gdn_scan_fwd(R["gdn_loc"])
    R["o_f"], R["o_b"] = R["gdn_fwd"][0][0], R["gdn_fwd"][1][0]
    R["y"] = _mix_out_fwd(R["h_f"], R["h_b"], R["o_f"], R["o_b"], p, W["gdn_norm"])
    R["x2"] = _fused_mm("out_proj", S, D_MODEL, D_MODEL, tm, D_MODEL, D_MODEL, [(R["y"], "mk"), (W["w_out"], "kn")],
                        [(0, 1, 0)], [(R["x1"], (tm, D_MODEL), _mn)],
                        [(jax.ShapeDtypeStruct((S, D_MODEL), F32), (tm, D_MODEL), _mn)],
                        lambda i, accs, ex, out: out[0].__setitem__(Ellipsis, ex[0][...] + accs[0][...]))[0]
    if more is not None:
        W = {**W, **more("ffn2", R["x2"])}
    R["h3"] = _rmsnorm_fwd("rms3", R["x2"], W["ffn2_norm"])
    R["x3"], R["a2"], R["b2"], R["f2"] = _ffn_fwd("ffn2", R["x2"], R["h3"], W["ffn2_w_gate"], W["ffn2_w_up"], W["ffn2_w_down"])
    R["dx3"], R["loss"], R["d_final_norm"] = _loss_head(R["x3"], target, W["final_norm"])
    R["W"] = W
    return R


def _colsum_into(ref, i, val):
    @pl.when(i == 0)
    def _():
        ref[...] = val

    @pl.when(i > 0)
    def _():
        ref[...] += val


def _ffn_bwd(tag, dout, x, g, h, a, b, f, wg, wu, wd, emit):
    S = x.shape[0]
    tm = _tile(S, 512)
    tk_s = _tile(S, 1024)
    dwd = _fused_mm(f"{tag}_dw_down", D_FF, D_MODEL, S, 1408, D_MODEL, tk_s, [(f, "km"), (dout, "kn")], [(0, 1, 0)], [],
                    [(jax.ShapeDtypeStruct((D_FF, D_MODEL), BF16), (1408, D_MODEL), _mn)],
                    lambda i, accs, ex, out: out[0].__setitem__(Ellipsis, (0.5 * accs[0][...]).astype(BF16)))[0]
    emit(down=dwd)

    def epi_act(i, accs, ex, out):
        df = 0.5 * accs[0][...]
        out[0][...] = (df * ex[1][...].astype(F32)).astype(BF16)
        out[1][...] = (df * ex[0][...].astype(F32)).astype(BF16)

    sds = jax.ShapeDtypeStruct((S, D_FF), BF16)
    da, db = _fused_mm(f"{tag}_dact", S, D_FF, D_MODEL, tm, 1408, D_MODEL, [(dout, "mk"), (wd, "nk")], [(0, 1, 0)],
                       [(a, (tm, 1408), _mn), (b, (tm, 1408), _mn)], [(sds, (tm, 1408), _mn)] * 2, epi_act)

    def epi_w2(i, accs, ex, out):
        out[0][...] = accs[0][...].astype(BF16)
        out[1][...] = accs[1][...].astype(BF16)

    sdw = jax.ShapeDtypeStruct((D_MODEL, D_FF), BF16)
    dwg, dwu = _fused_mm(f"{tag}_dw_up", D_MODEL, D_FF, S, D_MODEL, 1408, tk_s,
                         [(h, "km"), (da, "kn"), (db, "kn")], [(0, 1, 0), (0, 2, 1)], [],
                         [(sdw, (D_MODEL, 1408), _mn)] * 2, epi_w2)
    tok = emit(gate=dwg, up=dwu)
    if tok is not None:
        g = g + tok

    def epi_dx(i, accs, ex, out):
        dx, dgt = _rmsnorm_bwd_tile(accs[0][...], ex[0][...], ex[1][...])
        out[0][...] = ex[2][...] + dx
        _colsum_into(out[1], i, jnp.sum(dgt, axis=0, keepdims=True))

    tmx = _tile(S, 1024)
    dx, dg = _fused_mm(f"{tag}_dx", S, D_MODEL, D_FF, tmx, D_MODEL, 1408,
                       [(da, "mk"), (wg, "nk"), (db, "mk"), (wu, "nk")], [(0, 1, 0), (2, 3, 0)],
                       [(x, (tmx, D_MODEL), _mn), (g, (1, D_MODEL), _row0), (dout, (tmx, D_MODEL), _mn)],
                       [(jax.ShapeDtypeStruct((S, D_MODEL), F32), (tmx, D_MODEL), _mn),
                        (jax.ShapeDtypeStruct((1, D_MODEL), F32), (1, D_MODEL), _row0)], epi_dx)
    return dx, dg


def _mix_out_bwd(dy, h_f, h_b, o_f, o_b, p, gn):
    S = dy.shape[0]
    ts = _tile(S, 512)
    c0 = math.sqrt(2.0 / math.pi)

    def body(dy_ref, hf, hb, of, ob, gate, z, gn_ref, dhr_ref, dgate_ref, do_ref, dz_ref, dgn_ref):
        i = pl.program_id(0)
        gv = gate[...].astype(F32)
        ge, t = _gelu(gv)
        dy_rg = dy_ref[:, 0:RG_W]
        dhr_ref[...] = dy_rg * ge
        dgelu = 0.5 * (1.0 + t) + 0.5 * gv * (1.0 - t * t) * c0 * (1.0 + 3.0 * 0.044715 * gv * gv)
        dgate_ref[...] = (dy_rg * (hf[...] + hb[...]) * dgelu).astype(BF16)
        o = of[...] + ob[...]
        zv = z[...].astype(F32)
        sig = _sigmoid(zv)
        gnv = gn_ref[...]
        dgn = jnp.zeros((1, GDN_DK), F32)
        for h in range(GDN_H):
            cols = slice(h * GDN_DK, (h + 1) * GDN_DK)
            oh = o[:, cols]
            r = lax.rsqrt(jnp.mean(oh * oh, axis=-1, keepdims=True) + EPS)
            ohat = oh * r
            dyh = dy_ref[:, RG_W + h * GDN_DK:RG_W + (h + 1) * GDN_DK]
            zh = zv[:, cols]
            sh = sig[:, cols]
            dn = dyh * zh * sh
            dz_ref[:, cols] = (dyh * ohat * gnv * (sh * (1.0 + zh * (1.0 - sh)))).astype(BF16)
            dxn = dn * gnv
            do_ref[:, cols] = r * (dxn - ohat * jnp.mean(dxn * ohat, axis=-1, keepdims=True))
            dgn = dgn + jnp.sum(dn * ohat, axis=0, keepdims=True)
        _colsum_into(dgn_ref, i, dgn)

    blk = (ts, RG_W)
    im = lambda i: (i, 0)
    z0 = lambda i: (0, 0)
    ins = [(dy, (ts, D_MODEL), im), (h_f, blk, im), (h_b, blk, im), (o_f, blk, im), (o_b, blk, im),
           (p, blk, lambda i: (i, 1)), (p, blk, lambda i: (i, 5)), (gn, (1, GDN_DK), z0)]
    outs = [(jax.ShapeDtypeStruct((S, RG_W), F32), blk, im), (jax.ShapeDtypeStruct((S, RG_W), BF16), blk, im),
            (jax.ShapeDtypeStruct((S, RG_W), F32), blk, im), (jax.ShapeDtypeStruct((S, RG_W), BF16), blk, im),
            (jax.ShapeDtypeStruct((1, GDN_DK), F32), (1, GDN_DK), z0)]
    return _rows("mix_out_bwd", S, ts, ins, outs, body)


def _rg_scan_adj(name, a_up, b_up, a_dn, b_dn):
    S, C = a_up.shape
    ts = _tile(S, 512)
    n_tiles = S // ts

    def body(au, bu, ad, bd, mu_ref, lam_ref, carry):
        @pl.when(pl.program_id(0) == 0)
        def _():
            carry[...] = jnp.zeros_like(carry)

        n_sub = ts // SUBLANES
        rows = lax.broadcasted_iota(jnp.int32, (SUBLANES, C), 0)

        def half(a_ref, b_ref, out_ref, r0, c_in, reverse):
            a = a_ref[pl.ds(r0, SUBLANES), :]
            b = b_ref[pl.ds(r0, SUBLANES), :]
            cum_a, c0 = _scan_rows(a, a * b, reverse)
            c = c0 + cum_a * c_in
            edge = 0 if not reverse else SUBLANES - 1
            c_prev = jnp.where(rows == edge, c_in, pltpu.roll(c, SUBLANES - 1 if reverse else 1, 0))
            out_ref[pl.ds(r0, SUBLANES), :] = b + c_prev
            return c[0:1, :] if reverse else c[SUBLANES - 1:SUBLANES, :]

        def step(j, c):
            cu, cd = c
            cu = half(au, bu, mu_ref, pl.multiple_of(j * SUBLANES, SUBLANES), cu, False)
            cd = half(ad, bd, lam_ref, pl.multiple_of((n_sub - 1 - j) * SUBLANES, SUBLANES), cd, True)
            return cu, cd

        cu, cd = lax.fori_loop(0, n_sub, step, (carry[0:1, :], carry[1:2, :]), unroll=4)
        carry[0:1, :] = cu
        carry[1:2, :] = cd

    fw = lambda i: (i, 0)
    bw = lambda i: (n_tiles - 1 - i, 0)
    sds = jax.ShapeDtypeStruct((S, C), F32)
    return _rows(name, S, ts,
                 [(a_up, (ts, C), fw), (b_up, (ts, C), fw), (a_dn, (ts, C), bw), (b_dn, (ts, C), bw)],
                 [(sds, (ts, C), fw), (sds, (ts, C), bw)], body, scratch=[pltpu.VMEM((8, C), F32)])


def _halo_ex(arr, S, tm, width):
    per = tm // HALO
    last = S // HALO - 1
    return [
        (arr, (tm, width), lambda i, j: (i, 0)),
        (arr, (HALO, width), lambda i, j: (jnp.maximum(i * per - 1, 0), 0)),
        (arr, (HALO, width), lambda i, j: (jnp.minimum((i + 1) * per, last), 0)),
    ]


def _rg_gates_bwd(xc, bd, prm, lam_f, lam_b, h_f, h_b):
    S = xc.shape[0]
    tm = _tile(S, 256)
    n_tiles = S // tm

    def epi(i, accs, ex, out):
        pre = accs[0][...]
        xv = ex[0][...]
        prm_ref = ex[1]
        lams = (ex[2][...], ex[3][...])
        hprev = (_shift(_ext(ex[4], ex[5], ex[6], i, n_tiles), -1, tm),
                 _shift(_ext(ex[7], ex[8], ex[9], i, n_tiles), 1, tm))
        dxc = jnp.zeros_like(xv)
        rows = []
        dlam_rows = []
        for d in range(2):
            r, ig, sp, a, sq = _rg_gate_terms(pre, xv, prm_ref, d)
            lam = lams[d]
            da = lam * hprev[d]
            di = lam * sq * xv
            dxc = dxc + lam * sq * ig
            dsq = lam * ig * xv
            dlog_a = da * a - dsq * (a * a) / sq
            dpre_r = dlog_a * (-RG_C * sp) * r * (1.0 - r)
            dpre_i = di * ig * (1.0 - ig)
            out[0][:, d * 1024:d * 1024 + RG_W] = dpre_r.astype(BF16)
            out[0][:, d * 1024 + RG_W:(d + 1) * 1024] = dpre_i.astype(BF16)
            rows += [jnp.sum(dpre_r, axis=0, keepdims=True), jnp.sum(dpre_i, axis=0, keepdims=True)]
            dsp = jnp.sum(dlog_a * (-RG_C * r), axis=0, keepdims=True)
            dlam_rows.append(-dsp * _sigmoid(-prm_ref[4 + d:5 + d, :]))
        out[1][...] = dxc
        zero = jnp.zeros((2, RG_W), F32)
        _colsum_into(out[2], i, jnp.concatenate(rows + dlam_rows + [zero], axis=0))

    blk = (tm, RG_W)
    im = lambda i, j: (i, 0)
    extras = ([(xc, blk, im), (prm, (8, RG_W), _row0), (lam_f, blk, im), (lam_b, blk, im)]
              + _halo_ex(h_f, S, tm, RG_W) + _halo_ex(h_b, S, tm, RG_W))
    outs = [(jax.ShapeDtypeStruct((S, 4 * RG_W), BF16), (tm, 4 * RG_W), im),
            (jax.ShapeDtypeStruct((S, RG_W), F32), blk, im),
            (jax.ShapeDtypeStruct((8, RG_W), F32), (8, RG_W), _row0)]
    return _fused_mm("rg_gates_bwd", S, 4 * RG_W, RG_W, tm, 4 * RG_W, RG_W, [(xc, "mk"), (bd, "kn")], [(0, 1, 0)],
                     extras, outs, epi)


def _roll_rows(ext, off):
    if off == 0:
        return ext
    return pltpu.roll(ext, (-off) % ext.shape[0], 0)


def _conv_bwd(name, p, colblk, w, grads, mode):
    S = p.shape[0]
    ts = _tile(S, 512)
    n_tiles = S // ts
    C = w.shape[1]
    ng = len(grads)

    def body(*refs):
        p_refs = refs[0:3]
        g_refs = refs[3:3 + 3 * ng]
        w_ref = refs[3 + 3 * ng]
        dx_ref, dw_ref, db_ref = refs[4 + 3 * ng:]
        i = pl.program_id(0)
        ext_p = _ext(*p_refs, i, n_tiles)
        dn = _ext(*g_refs[0:3], i, n_tiles)
        for gi in range(1, ng):
            dn = dn + _ext(*g_refs[3 * gi:3 * gi + 3], i, n_tiles)
        if mode == "bias":
            dc = dn
        else:
            c = None
            for j in range(CONV_W):
                term = w_ref[j:j + 1, :] * _roll_rows(ext_p, j - 2)
                c = term if c is None else c + term
            sig = _sigmoid(c)
            s = c * sig
            if mode in ("q", "k"):
                scale = GDN_DK ** -0.5 if mode == "q" else 1.0
                parts = []
                for h in range(GDN_H):
                    cols = slice(h * GDN_DK, (h + 1) * GDN_DK)
                    sh = s[:, cols]
                    dnh = dn[:, cols]
                    rinv = lax.rsqrt(jnp.sum(sh * sh, axis=-1, keepdims=True) + EPS)
                    parts.append(scale * rinv * (dnh - sh * (rinv * rinv) * jnp.sum(dnh * sh, axis=-1, keepdims=True)))
                ds = jnp.concatenate(parts, axis=-1)
            else:
                ds = dn
            dc = ds * (sig * (1.0 + c * (1.0 - sig)))
        dx = None
        for j in range(CONV_W):
            term = w_ref[j:j + 1, :] * _shift(dc, 2 - j, ts)
            dx = term if dx is None else dx + term
        dx_ref[...] = dx.astype(BF16)
        dc_main = dc[HALO:HALO + ts]
        dw = jnp.concatenate([jnp.sum(dc_main * _shift(ext_p, j - 2, ts), axis=0, keepdims=True)
                              for j in range(CONV_W)], axis=0)
        _colsum_into(dw_ref, i, dw)
        _colsum_into(db_ref, i, jnp.sum(dc_main, axis=0, keepdims=True))

    ins = _halo_ins(p, S, ts, C, colblk)
    for garr in grads:
        ins += _halo_ins(garr, S, ts, C, 0)
    ins += [(w, (CONV_W, C), lambda i: (0, 0))]
    z0 = lambda i: (0, 0)
    outs = [(jax.ShapeDtypeStruct((S, C), BF16), (ts, C), lambda i: (i, 0)),
            (jax.ShapeDtypeStruct((CONV_W, C), F32), (CONV_W, C), z0),
            (jax.ShapeDtypeStruct((1, C), F32), (1, C), z0)]
    return _rows(name, S, ts, ins, outs, body)


def _gdn_scan_bwd(loc, do):
    S = do.shape[0]
    ts = _tile(S, GDN_TS)
    n_tiles = S // ts
    ncb = ts // CHUNK
    nch = S // CHUNK

    def body(*refs):
        ins = (refs[0:6], refs[6:12])
        outs = (refs[12:14], refs[14:16])
        dstate = refs[16]

        @pl.when(pl.program_id(0) == 0)
        def _():
            dstate[...] = jnp.zeros_like(dstate)

        def chunk(cc, carry):
            chains = []
            for d in range(2):
                c = ncb - 1 - cc if d == 0 else cc
                rows = pl.ds(pl.multiple_of(c * CHUNK, CHUNK), CHUNK)
                for h in range(GDN_H):
                    cols = slice(h * GDN_DK, (h + 1) * GDN_DK)
                    chains.append(dict(d=d, h=h, c=c, rows=rows, cols=cols, dsn=dstate[d * GDN_H + h]))
            for ch in chains:
                qd_ref, kd_ref, cd_ref, w_ref, a_ref, do_ref = ins[ch["d"]]
                rows, cols = ch["rows"], ch["cols"]
                dob = do_ref[rows, cols].astype(BF16)
                ch["dvn"] = (_dot(a_ref[ch["c"], ch["h"]], dob, 0, 0)
                             + _dot(kd_ref[rows, cols], ch["dsn"].astype(BF16), 1, 0))
                ch["qdo"] = _dot(qd_ref[rows, cols], dob, 0, 0)
            for ch in chains:
                w_ref = ins[ch["d"]][3]
                ch["wdvn"] = _dot(w_ref[ch["rows"], ch["cols"]], ch["dvn"].astype(BF16), 0, 0)
            for ch in chains:
                dvn_ref, ds_ref = outs[ch["d"]]
                cd_ref = ins[ch["d"]][2]
                col = ch["d"] * GDN_H + ch["h"]
                dvn_ref[ch["rows"], ch["cols"]] = ch["dvn"].astype(BF16)
                ds_ref[ch["c"], ch["h"]] = ch["dsn"].astype(BF16)
                dstate[ch["d"] * GDN_H + ch["h"]] = (ch["qdo"] + cd_ref[ch["c"], col:col + 1, :] * ch["dsn"]
                                                     - ch["wdvn"])
            return carry

        lax.fori_loop(0, ncb, chunk, 0)

    ins, outs = [], []
    for d in range(2):
        tix = _dir_tile(d, n_tiles, True)
        im = lambda i, tix=tix: (tix(i), 0)
        im4 = lambda i, tix=tix: (tix(i), 0, 0, 0)
        _, w, a, _, qd, kd = loc[d]
        ins += [(qd, (ts, GDN_W), im), (kd, (ts, GDN_W), im), (loc[2], (ncb, 8, 128), lambda i, tix=tix: (tix(i), 0, 0)),
                (w, (ts, GDN_W), im), (a, (ncb, GDN_H, CHUNK, CHUNK), im4), (do, (ts, GDN_W), im)]
        outs += [(jax.ShapeDtypeStruct((S, GDN_W), BF16), (ts, GDN_W), im),
                 (jax.ShapeDtypeStruct((nch, GDN_H, GDN_DK, GDN_DK), BF16), (ncb, GDN_H, GDN_DK, GDN_DK), im4)]
    res = _rows("gdn_scan_bwd", S, ts, ins, outs, body, scratch=[pltpu.VMEM((2 * GDN_H, GDN_DK, GDN_DK), F32)])
    return res[0:2], res[2:4]


def _gdn_local_bwd(q, k, v, bg, gcr, do, loc, fwd, adj):
    S = q.shape[0]
    ts = _tile(S, GDN_TS)
    ncb = ts // CHUNK

    def body(q_ref, k_ref, v_ref, bg_ref, gcr_ref, do_ref, *rest):
        per_dir = (rest[0:5], rest[5:10])
        dq_ref, dk_ref, dv_ref, dbg_ref, dbgr_ref = rest[10:15]
        ri, ci = _tri_masks()
        lane = lax.broadcasted_iota(jnp.int32, (CHUNK, 128), 1)
        rowi = lax.broadcasted_iota(jnp.int32, (CHUNK, 1), 0)
        ones8 = jnp.ones((SUBLANES, CHUNK), F32)

        def chunk(c, carry):
            r0 = pl.multiple_of(c * CHUNK, CHUNK)
            rows = pl.ds(r0, CHUNK)
            chains = []
            for h in range(GDN_H):
                cols = slice(h * GDN_DK, (h + 1) * GDN_DK)
                qh, kh, vh = q_ref[rows, cols], k_ref[rows, cols], v_ref[rows, cols]
                dob = do_ref[rows, cols].astype(BF16)
                both = _bdot(jnp.concatenate([qh, kh], axis=0), kh, 1, 1)
                for d in range(2):
                    chains.append(dict(h=h, d=d, cols=cols, qh=qh, kh=kh, vh=vh, dob=dob, qk=both[0:CHUNK],
                                       kk=both[CHUNK:2 * CHUNK], col=d * GDN_H + h))
            for ch in chains:
                m = _gdn_decay(bg_ref, gcr_ref, c, rows, r0, ch["col"], ch["d"] == 1, ri, ci)
                t_ref, s_ref, ds_ref, vn_ref, dvn_ref = per_dir[ch["d"]]
                h, cols = ch["h"], ch["cols"]
                ch["m"] = m
                ch["kb"] = ch["kh"] * m["beta"]
                ch["kbg"] = ch["kb"] * m["eg"]
                ch["t"] = t_ref[c, h]
                stb = s_ref[c, h]
                ch["dsn"] = ds_ref[c, h]
                vnb = vn_ref[rows, cols]
                dvnb = dvn_ref[rows, cols]
                ch["dcd"] = jnp.sum(jnp.sum(stb.astype(F32) * ch["dsn"].astype(F32), axis=1, keepdims=True),
                                    axis=0, keepdims=True)
                ch["dqd"] = _dot(ch["dob"], stb, 1, 1)
                ch["d_a"] = _dot(ch["dob"], vnb, 1, 1)
                ch["dkd"] = _bdot(vnb, ch["dsn"], 1, 1)
                ch["dw"] = -_dot(dvnb, stb, 1, 1)
                ch["dvb"] = _dot(ch["t"], dvnb, 1, 0)
                ch["d_t"] = _bdot(dvnb, ch["vh"] * m["beta"], 1, 1)
            for ch in chains:
                dwb = ch["dw"].astype(BF16)
                ch["d_t"] = ch["d_t"] + _bdot(dwb, ch["kbg"], 1, 1)
                ch["dkbg"] = _dot(ch["t"], dwb, 1, 0)
                ch["nn"] = ch["d_a"] * ch["m"]["dm"]
                ch["nn_q"] = _bdot(ch["nn"], ch["qh"], 0, 0)
                ch["nn_k"] = _bdot(ch["nn"], ch["kh"], 1, 0)
            for ch in chains:
                ch["x"] = _dot(ch["d_t"].astype(BF16), ch["t"], 1, 0)
            for ch in chains:
                d_l = -_dot(ch["t"], ch["x"].astype(BF16), 1, 0)
                ch["d_l"] = jnp.where(ch["m"]["strict"], d_l, 0.0)
                ch["mm"] = ch["d_l"] * ch["m"]["dm"]
            for ch in chains:
                m = ch["m"]
                ch["mm_kh"] = _bdot(ch["mm"], ch["kh"], 1, 0)
                ch["mm_kb"] = _bdot(ch["mm"], ch["kb"], 0, 0)
                l_mat = jnp.where(m["strict"], m["beta"] * ch["kk"] * m["dm"], 0.0)
                ch["e"] = ch["d_l"] * l_mat + ch["nn"] * ch["qk"]
                dbgr_ref[c, ch["col"]:ch["col"] + 1, :] = -_dot(ones8, ch["e"], 1, 0, HI)[0:1, :]
            acc_bg = jnp.zeros((CHUNK, 128), F32)
            acc = {}
            for ch in chains:
                m = ch["m"]
                beta, eg, egl = m["beta"], m["eg"], m["egl"]
                dkb = ch["mm_kh"] + ch["dkbg"] * eg
                dk_d = ch["mm_kb"] + ch["nn_q"] + ch["dkd"] * egl + dkb * beta
                dq_d = ch["nn_k"] + ch["dqd"] * eg
                dv_d = ch["dvb"] * beta
                dkd_kd = ch["dkd"] * (ch["kh"] * egl)
                dgc = (jnp.sum(ch["e"], axis=1, keepdims=True)
                       + jnp.sum(ch["dqd"] * (ch["qh"] * eg) - dkd_kd + ch["dkbg"] * ch["kbg"], axis=1, keepdims=True))
                dgl = jnp.sum(jnp.sum(dkd_kd, axis=1, keepdims=True), axis=0, keepdims=True) + ch["dcd"] * m["cd"]
                dgc = dgc + jnp.where(rowi == (0 if ch["d"] == 1 else CHUNK - 1), dgl, 0.0)
                dbeta = jnp.sum(dkb * ch["kh"] + ch["dvb"] * ch["vh"], axis=1, keepdims=True)
                acc_bg = acc_bg + jnp.where(lane == ch["col"], dbeta, 0.0) + jnp.where(lane == 8 + ch["col"], dgc, 0.0)
                if ch["d"] == 0:
                    acc[ch["h"]] = (dq_d, dk_d, dv_d)
                else:
                    dq0, dk0, dv0 = acc[ch["h"]]
                    dq_ref[rows, ch["cols"]] = dq0 + dq_d
                    dk_ref[rows, ch["cols"]] = dk0 + dk_d
                    dv_ref[rows, ch["cols"]] = dv0 + dv_d
            dbg_ref[rows, :] = acc_bg
            return carry

        lax.fori_loop(0, ncb, chunk, 0)

    im = lambda i: (i, 0)
    im4 = lambda i: (i, 0, 0, 0)
    blk = (ts, GDN_W)
    ins = [(q, blk, im), (k, blk, im), (v, blk, im), (bg, (ts, 128), im), (gcr, (ncb, 8, CHUNK), lambda i: (i, 0, 0)),
           (do, blk, im)]
    for d in range(2):
        ins += [(loc[d][3], (ncb, GDN_H, CHUNK, CHUNK), im4), (fwd[d][2], (ncb, GDN_H, GDN_DK, GDN_DK), im4),
                (adj[d][1], (ncb, GDN_H, GDN_DK, GDN_DK), im4), (fwd[d][1], blk, im), (adj[d][0], blk, im)]
    sds = jax.ShapeDtypeStruct((S, GDN_W), F32)
    outs = [(sds, blk, im), (sds, blk, im), (sds, blk, im), (jax.ShapeDtypeStruct((S, 128), F32), (ts, 128), im),
            (jax.ShapeDtypeStruct((S // CHUNK, 8, CHUNK), F32), (ncb, 8, CHUNK), lambda i: (i, 0, 0))]
    dq, dk, dv, dbg, dbg_rows = _rows("gdn_local_bwd", S, ts, ins, outs, body)
    dgc_cols = dbg_rows.transpose(0, 2, 1).reshape(S, 8)
    return dq, dk, dv, dbg + jnp.pad(dgc_cols, ((0, 0), (8, 112)))


def _gdn_prep_bwd(dbg_all, p, prm):
    S = p.shape[0]
    ts = _tile(S, 512)

    def body(dbg_ref, p_ref, prm_ref, dba_ref, dprm_ref):
        i = pl.program_id(0)
        raw = p_ref[...].astype(F32)
        dbg = dbg_ref[...]
        lane = lax.broadcasted_iota(jnp.int32, (1, 128), 1)
        is_g = (lane >= 8) & (lane < 16)
        ea = jnp.exp(prm_ref[0:1, :])
        arg = raw + prm_ref[1:2, :]
        g = jnp.where(is_g, -ea * _softplus(arg), 0.0)
        beta = _sigmoid(raw)
        dgc = jnp.where(is_g, dbg, 0.0)
        ri, ci = _tri_masks()
        lower = (ri >= ci).astype(F32)
        upper = (ri <= ci).astype(F32)
        dgs = []
        for c in range(ts // CHUNK):
            ch = dgc[c * CHUNK:(c + 1) * CHUNK]
            dgs.append(jnp.where(lane < 12, _dot(upper, ch, 1, 0, HI), _dot(lower, ch, 1, 0, HI)))
        dg = jnp.concatenate(dgs, axis=0)
        dalpha = jnp.where(is_g, dg * (-ea) * _sigmoid(arg), 0.0)
        dba_ref[...] = jnp.where(lane < 8, dbg * beta * (1.0 - beta), dalpha).astype(BF16)
        rows = jnp.concatenate([jnp.sum(dg * g, axis=0, keepdims=True), jnp.sum(dalpha, axis=0, keepdims=True),
                                jnp.zeros((6, 128), F32)], axis=0)
        _colsum_into(dprm_ref, i, rows)

    im = lambda i: (i, 0)
    z0 = lambda i: (0, 0)
    return _rows("gdn_prep_bwd", S, ts,
                 [(dbg_all, (ts, 128), im), (p, (ts, 128), lambda i: (i, COL_BA // 128)), (prm, (8, 128), z0)],
                 [(jax.ShapeDtypeStruct((S, 128), BF16), (ts, 128), im), (jax.ShapeDtypeStruct((8, 128), F32), (8, 128), z0)],
                 body)


def _mm_plain(name, M, N, K, tm, tn, tk, a, am, b, bm, dtype):
    return _fused_mm(name, M, N, K, tm, tn, tk, [(a, am), (b, bm)], [(0, 1, 0)], [],
                     [(jax.ShapeDtypeStruct((M, N), dtype), (tm, tn), _mn)],
                     lambda i, accs, ex, out: out[0].__setitem__(Ellipsis, accs[0][...].astype(dtype)))[0]


def _layer_bwd(x0, W, R, emit_big=None, emit_small=None):
    S = x0.shape[0]
    tm = _tile(S, 512)
    tk_s = _tile(S, 1024)
    G = {}

    def emit(**named):
        if emit_big is None:
            G.update(named)
            return None
        return emit_big(**named)

    def ffn_emit(prefix):
        return lambda **kw: emit(**{f"{prefix}_w_{k}": v for k, v in kw.items()})

    dx2, G["ffn2_norm"] = _ffn_bwd("ffn2b", R["dx3"], R["x2"], W["ffn2_norm"], R["h3"], R["a2"], R["b2"], R["f2"],
                                   W["ffn2_w_gate"], W["ffn2_w_up"], W["ffn2_w_down"], ffn_emit("ffn2"))
    tok = emit(w_out=_mm_plain("dw_out", D_MODEL, D_MODEL, S, D_MODEL, D_MODEL, tk_s, R["y"], "km", dx2, "kn", BF16))
    gn = W["gdn_norm"] if tok is None else W["gdn_norm"] + tok
    dy = _mm_plain("dy_mix", S, D_MODEL, D_MODEL, tm, D_MODEL, D_MODEL, dx2, "mk", W["w_out"], "nk", F32)
    p = R["p"]
    dhr, dgate, do, dz, G["gdn_norm"] = _mix_out_bwd(dy, R["h_f"], R["h_b"], R["o_f"], R["o_b"], p, gn)
    lam_b, lam_f = _rg_scan_adj("rg_scan_bwd", R["a_b"], dhr, R["a_f"], dhr)
    dpre, dxc_direct, d_rgprm = _rg_gates_bwd(R["xc"], R["bd"], R["rg_prm"], lam_f, lam_b, R["h_f"], R["h_b"])
    tmg = _tile(S, 512)
    dxc = _fused_mm("rg_dxc", S, RG_W, 4 * RG_W, tmg, RG_W, 4 * RG_W, [(dpre, "mk"), (R["bd"], "nk")], [(0, 1, 0)],
                    [(dxc_direct, (tmg, RG_W), _mn)], [(jax.ShapeDtypeStruct((S, RG_W), F32), (tmg, RG_W), _mn)],
                    lambda i, accs, ex, out: out[0].__setitem__(Ellipsis, ex[0][...] + accs[0][...]))[0]
    d_bd = _mm_plain("rg_dbd", RG_W, 4 * RG_W, S, RG_W, 4 * RG_W, tk_s, R["xc"], "km", dpre, "kn", F32)
    dx_rg, G["rg_conv_w"], G["rg_conv_b"] = _conv_bwd("rg_conv_bwd", p, 0, W["rg_conv_w"], [dxc], "bias")
    blocks = jnp.einsum("nigmj,nm->gnij", d_bd.reshape(RG_BLOCKS, RG_BLOCK, 4, RG_BLOCKS, RG_BLOCK),
                        jnp.eye(RG_BLOCKS, dtype=F32))
    G["rg_gate_a_w"] = jnp.stack([blocks[0], blocks[2]])
    G["rg_gate_x_w"] = jnp.stack([blocks[1], blocks[3]])
    G["rg_gate_a_b"] = jnp.stack([d_rgprm[0], d_rgprm[2]])
    G["rg_gate_x_b"] = jnp.stack([d_rgprm[1], d_rgprm[3]])
    G["rg_lambda"] = d_rgprm[4:6]
    adj = _gdn_scan_bwd(R["gdn_loc"], do)
    dq, dk, dv, dbg = _gdn_local_bwd(R["q"], R["k"], R["v"], R["bg"], R["gcr"], do, R["gdn_loc"], R["gdn_fwd"], adj)
    cw = W["gdn_conv_w"]
    dpq, dwq, _ = _conv_bwd("gdn_conv_q_bwd", p, 2, cw[:, 0:512], [dq], "q")
    dpk, dwk, _ = _conv_bwd("gdn_conv_k_bwd", p, 3, cw[:, 512:1024], [dk], "k")
    dpv, dwv, _ = _conv_bwd("gdn_conv_v_bwd", p, 4, cw[:, 1024:1536], [dv], "v")
    G["gdn_conv_w"] = jnp.concatenate([dwq, dwk, dwv], axis=1)
    dba, d_gprm = _gdn_prep_bwd(dbg, p, R["gdn_prm"])
    G["gdn_a_log"] = d_gprm[0, 8:16].reshape(2, GDN_H)
    G["gdn_dt_bias"] = d_gprm[1, 8:16].reshape(2, GDN_H)
    dp = jnp.concatenate([dx_rg, dgate, dpq, dpk, dpv, dz, dba], axis=1)
    tok = emit(w_in=_mm_plain("dw_in", D_MODEL, D_IN_PAD, S, D_MODEL, 640, tk_s, R["h2"], "km", dp, "kn", BF16))
    g_mix = W["mix_norm"] if tok is None else W["mix_norm"] + tok

    def epi_dx1(i, accs, ex, out):
        dx, dgt = _rmsnorm_bwd_tile(accs[0][...], ex[0][...], ex[1][...])
        out[0][...] = ex[2][...] + dx
        _colsum_into(out[1], i, jnp.sum(dgt, axis=0, keepdims=True))

    dx1, G["mix_norm"] = _fused_mm(
        "mix_dx", S, D_MODEL, D_IN_PAD, tm, D_MODEL, D_IN_PAD, [(dp, "mk"), (W["w_in"], "nk")], [(0, 1, 0)],
        [(R["x1"], (tm, D_MODEL), _mn), (g_mix, (1, D_MODEL), _row0), (dx2, (tm, D_MODEL), _mn)],
        [(jax.ShapeDtypeStruct((S, D_MODEL), F32), (tm, D_MODEL), _mn),
         (jax.ShapeDtypeStruct((1, D_MODEL), F32), (1, D_MODEL), _row0)], epi_dx1)
    G["final_norm"] = R["d_final_norm"]
    if emit_small is not None:
        emit_small(G)
    dx0, G["ffn1_norm"] = _ffn_bwd("ffn1b", dx1, x0, W["ffn1_norm"], R["h1"], R["a1"], R["b1"], R["f1"],
                                   W["ffn1_w_gate"], W["ffn1_w_up"], W["ffn1_w_down"], ffn_emit("ffn1"))
    return dx0, G


def _mesh_pos():
    x, y, c = lax.axis_index("x"), lax.axis_index("y"), lax.axis_index("c")
    return x, y, c, 4 * x + 2 * y + c


def _peer(x, y, c, r):
    px = 1 - x if r & 4 else x
    py = 1 - y if r & 2 else y
    pc = 1 - c if r & 1 else c
    return (px, py, pc), 4 * px + 2 * py + pc


_HBM = pl.BlockSpec(memory_space=pltpu.HBM)
_SEM = pl.BlockSpec(memory_space=pltpu.SEMAPHORE)


def _peer_copies(scatter, srcs, lands, send_sems, recv_sems):
    x, y, c, me = _mesh_pos()
    copies = []
    for a, (src, land) in enumerate(zip(srcs, lands)):
        for r in range(1, N_DEV):
            peer, peer_idx = _peer(x, y, c, r)
            copies.append(pltpu.make_async_remote_copy(
                src_ref=src.at[peer_idx] if scatter else src, dst_ref=land.at[r - 1] if scatter else land.at[me],
                send_sem=send_sems.at[a * 7 + r - 1], recv_sem=recv_sems.at[a * 7 + r - 1],
                device_id=peer, device_id_type=pl.DeviceIdType.MESH))
    return copies


def _exchange_start(name, scatter, arrays):
    slabs = arrays
    n = len(slabs)

    def body(*refs):
        srcs, lands = refs[0:n], refs[n:2 * n]
        send_sems, recv_sems = refs[2 * n], refs[2 * n + 1]
        token = refs[4 * n + 2]
        for cp in _peer_copies(scatter, srcs, lands, send_sems, recv_sems):
            cp.start()
        token[...] = jnp.zeros_like(token)

    land_shapes = [(N_DEV - 1,) + s.shape[1:] if scatter else (N_DEV,) + s.shape for s in slabs]
    n_sems = 7 * n
    out_shape = ([pltpu.SemaphoreType.DMA((n_sems,)), pltpu.SemaphoreType.DMA((n_sems,))]
                 + [pltpu.HBM(s.shape, s.dtype) for s in slabs]
                 + [pltpu.HBM(shp, s.dtype) for shp, s in zip(land_shapes, slabs)]
                 + [jax.ShapeDtypeStruct((8, 128), F32)])
    res = pl.pallas_call(
        body, name=name, out_shape=out_shape, in_specs=[_HBM] * (2 * n),
        out_specs=[_SEM, _SEM] + [_HBM] * (2 * n) + [pl.BlockSpec(memory_space=pltpu.VMEM)],
        input_output_aliases={i: 2 + i for i in range(2 * n)},
        compiler_params=pltpu.CompilerParams(has_side_effects=pltpu.SideEffectType.DATAFLOW_SIDE_EFFECTING),
    )(*[pltpu.with_memory_space_constraint(s, pltpu.HBM) for s in slabs],
      *[pltpu.with_memory_space_constraint(lax.empty(shp, s.dtype), pltpu.HBM) for shp, s in zip(land_shapes, slabs)])
    return dict(n=n, scatter=scatter, sems=res[0:2], srcs=res[2:2 + n], lands=res[2 + n:2 + 2 * n],
                token=res[2 + 2 * n][0, 0])


def _exchange_wait(name, started, after):
    n = started["n"]
    scatter = started["scatter"]

    def body(*refs):
        srcs, lands = refs[0:n], refs[n:2 * n]
        send_sems, recv_sems = refs[2 * n], refs[2 * n + 1]
        for cp in _peer_copies(scatter, srcs, lands, send_sems, recv_sems):
            cp.wait_send()
            cp.wait_recv()

    arrays = list(started["srcs"]) + list(started["lands"])
    res = pl.pallas_call(
        body, name=name, out_shape=[pltpu.HBM(a.shape, a.dtype) for a in arrays],
        in_specs=[_HBM] * (2 * n) + [_SEM, _SEM, pl.BlockSpec(memory_space=pl.ANY)], out_specs=[_HBM] * (2 * n),
        input_output_aliases={i: i for i in range(2 * n)},
        compiler_params=pltpu.CompilerParams(has_side_effects=pltpu.SideEffectType.DATAFLOW_SIDE_EFFECTING),
    )(*arrays, *started["sems"], after)
    return res[0:n], res[n:2 * n]


def _all_gather(name, arrays):
    n = len(arrays)

    def body(*refs):
        ins = refs[:n]
        outs = refs[n:2 * n]
        token = refs[2 * n]
        send_sems, recv_sems, local_sems = refs[2 * n + 1:]
        token[...] = jnp.zeros_like(token)
        x, y, c, me = _mesh_pos()
        sibling = (x, y, 1 - c)
        chips = [(1 - x, y), (x, 1 - y), (1 - x, 1 - y)]

        def idx(px, py, pc):
            return 4 * px + 2 * py + pc

        def copy(a, k, block, to, src=None):
            slot = outs[a].at[idx(*block)]
            return pltpu.make_async_remote_copy(
                src_ref=slot if src is None else src, dst_ref=slot, send_sem=send_sems.at[a * 7 + k],
                recv_sem=recv_sems.at[a * 7 + k], device_id=to, device_id_type=pl.DeviceIdType.MESH)

        locals_, sends = [], []
        for a in range(n):
            loc = pltpu.make_async_copy(ins[a], outs[a].at[me], local_sems.at[a])
            loc.start()
            locals_.append(loc)
            sends.append(copy(a, 0, (x, y, c), sibling, src=ins[a]))
            sends += [copy(a, 1 + j, (x, y, c), (*chip, c), src=ins[a]) for j, chip in enumerate(chips)]
        for cp in sends:
            cp.start()
        passed = []
        for a in range(n):
            for j, chip in enumerate(chips):
                copy(a, 1 + j, (*chip, c), (x, y, c)).wait_recv()
                fwd = copy(a, 4 + j, (*chip, c), sibling)
                fwd.start()
                passed.append(fwd)
        for a in range(n):
            copy(a, 0, sibling, (x, y, c)).wait_recv()
            for j, chip in enumerate(chips):
                copy(a, 4 + j, (*chip, 1 - c), (x, y, c)).wait_recv()
        for cp in sends + passed:
            cp.wait_send()
        for loc in locals_:
            loc.wait()

    any_spec = pl.BlockSpec(memory_space=pl.ANY)
    res = pl.pallas_call(
        body, name=name, in_specs=[any_spec] * n, out_specs=[any_spec] * n + [pl.BlockSpec(memory_space=pltpu.VMEM)],
        out_shape=[jax.ShapeDtypeStruct((N_DEV,) + a.shape, a.dtype) for a in arrays]
        + [jax.ShapeDtypeStruct((8, 128), F32)],
        scratch_shapes=[pltpu.SemaphoreType.DMA((7 * n,)), pltpu.SemaphoreType.DMA((7 * n,)),
                        pltpu.SemaphoreType.DMA((n,))],
        compiler_params=pltpu.CompilerParams(has_side_effects=True),
    )(*arrays)
    return res[:n], res[n][0, 0]


def _adamw_math(w, g, m, v):
    m2 = ADAM_B1 * m + (1.0 - ADAM_B1) * g
    v2 = ADAM_B2 * v + (1.0 - ADAM_B2) * (g * g)
    m_hat = m2 / (1.0 - ADAM_B1 ** ADAM_STEP)
    v_hat = v2 / (1.0 - ADAM_B2 ** ADAM_STEP)
    delta = -ADAM_LR * (m_hat / (jnp.sqrt(v_hat) + ADAM_EPS) + ADAM_WD * w)
    return delta, m2, v2


def _adamw_slabs(name, src, land, me, w, m, v, tr):
    R, C = w.shape

    def body(me_ref, own_ref, land_ref, w_ref, m_ref, v_ref, g_ref, d_ref, m2_ref, v2_ref):
        g = own_ref[0].astype(F32)
        for s in range(N_DEV - 1):
            g = g + land_ref[s].astype(F32)
        delta, m2, v2 = _adamw_math(w_ref[...], g, m_ref[...], v_ref[...])
        g_ref[...] = g
        d_ref[...] = delta
        m2_ref[...] = m2
        v2_ref[...] = v2

    im = lambda i, me_ref: (i, 0)
    grid_spec = pltpu.PrefetchScalarGridSpec(
        num_scalar_prefetch=1, grid=(R // tr,),
        in_specs=[pl.BlockSpec((1, tr, C), lambda i, me_ref: (me_ref[0], i, 0)),
                  pl.BlockSpec((N_DEV - 1, tr, C), lambda i, me_ref: (0, i, 0)),
                  pl.BlockSpec((tr, C), im), pl.BlockSpec((tr, C), im), pl.BlockSpec((tr, C), im)],
        out_specs=[pl.BlockSpec((tr, C), im)] * 4)
    return pl.pallas_call(body, name=name, grid_spec=grid_spec, out_shape=[jax.ShapeDtypeStruct((R, C), F32)] * 4,
                          compiler_params=_cp(1))(me.reshape(1).astype(jnp.int32), src, land, w, m, v)


def _sum_slots(name, slots):
    _, R, C = slots.shape

    def body(s_ref, o_ref):
        g = s_ref[0]
        for s in range(1, N_DEV):
            g = g + s_ref[s]
        o_ref[...] = g

    return _rows(name, R, R, [(slots, (N_DEV, R, C), lambda i: (0, 0, 0))],
                 [(jax.ShapeDtypeStruct((R, C), F32), (R, C), lambda i: (0, 0))], body)[0]


def _adamw_packed(name, g, w, m, v):
    R, C = g.shape

    def body(g_ref, w_ref, m_ref, v_ref, d_ref, m2_ref, v2_ref):
        delta, m2, v2 = _adamw_math(w_ref[...], g_ref[...], m_ref[...], v_ref[...])
        d_ref[...] = delta
        m2_ref[...] = m2
        v2_ref[...] = v2

    im = lambda i: (0, 0)
    sds = jax.ShapeDtypeStruct((R, C), F32)
    return _rows(name, R, R, [(a, (R, C), im) for a in (g, w, m, v)], [(sds, (R, C), im)] * 3, body)


def _pack(arrays):
    rows = []
    for a in arrays:
        flat = a.reshape(-1).astype(F32)
        pad = (-flat.shape[0]) % 128
        rows.append(jnp.pad(flat, (0, pad)).reshape(-1, 128))
    out = jnp.concatenate(rows, axis=0)
    return jnp.pad(out, ((0, (-out.shape[0]) % 8), (0, 0)))


def _unpack(packed, shapes):
    lead = packed.shape[:-2]
    outs = []
    r = 0
    for shp in shapes:
        n = math.prod(shp)
        nr = -(-n // 128)
        flat = packed[..., r:r + nr, :].reshape(lead + (nr * 128,))[..., :n]
        outs.append(flat.reshape(lead + tuple(shp)))
        r += nr
    return outs


FFN1_BIG = ["ffn1_w_gate", "ffn1_w_up", "ffn1_w_down"]
MIX_BIG = ["w_in", "w_out"]
FFN2_BIG = ["ffn2_w_gate", "ffn2_w_up", "ffn2_w_down"]
BIG = FFN1_BIG + MIX_BIG + FFN2_BIG
COL_SHARDED = {"ffn1_w_gate", "ffn1_w_up", "w_in", "ffn2_w_gate", "ffn2_w_up"}
SMALL_SHARDED = ["rg_conv_w", "rg_gate_a_b", "rg_gate_x_b", "rg_lambda", "gdn_conv_w"]
WEIGHTS = ["ffn1_norm", "ffn1_w_gate", "ffn1_w_up", "ffn1_w_down", "mix_norm", "w_in", "w_out", "rg_conv_w", "rg_conv_b",
           "rg_gate_a_w", "rg_gate_a_b", "rg_gate_x_w", "rg_gate_x_b", "rg_lambda", "gdn_conv_w", "gdn_a_log",
           "gdn_dt_bias", "gdn_norm", "ffn2_norm", "ffn2_w_gate", "ffn2_w_up", "ffn2_w_down", "final_norm"]
SMALL = [n for n in WEIGHTS if n not in BIG]
ROW_VECTORS = {"ffn1_norm", "mix_norm", "ffn2_norm", "gdn_norm", "rg_conv_b", "final_norm"}
ROW_TILE = {"ffn1_w_gate": 256, "ffn1_w_up": 256, "ffn1_w_down": 176, "w_in": 256, "w_out": 64,
            "ffn2_w_gate": 256, "ffn2_w_up": 256, "ffn2_w_down": 176}


def _unshard_cols(g):
    return g.transpose(1, 0, 2).reshape(g.shape[1], N_DEV * g.shape[2])


def _to_slabs(name, g):
    if name in COL_SHARDED:
        r, ctot = g.shape
        return g.reshape(r, N_DEV, ctot // N_DEV).transpose(1, 0, 2)
    return g.reshape(N_DEV, g.shape[0] // N_DEV, g.shape[1])


def _step(x, target, w, m, v):
    _, _, _, me = _mesh_pos()
    def unshard(n, gth):
        full = _unshard_cols(gth) if n in COL_SHARDED else gth.reshape(-1, gth.shape[-1])
        return jnp.pad(full, ((0, 0), (0, D_IN_PAD - D_IN))) if n == "w_in" else full

    def landed(started, name, after):
        srcs, lands = _exchange_wait(name, started, after)
        def with_own(src, land):
            slot = lax.broadcasted_iota(jnp.int32, (N_DEV,) + (1,) * src.ndim, 0)
            return jnp.where(slot == me, src[None], land)

        return [with_own(src, land) for src, land in zip(srcs, lands)]

    up_names = ["ffn1_w_gate", "ffn1_w_up"]
    first, tok = _all_gather("gather_ffn1", [w[n].astype(BF16) for n in up_names])
    W = {n: unshard(n, gth) for n, gth in zip(up_names, first)}
    small_shards = [w[n] for n in SMALL_SHARDED]
    st_down = _exchange_start("gather_ffn1_down_start", False, [(w["ffn1_w_down"] + tok).astype(BF16)])
    st_mix = _exchange_start("gather_mix_start", False,
                             [(w[n] + tok).astype(BF16) for n in MIX_BIG] + [_pack(small_shards) + tok])
    st_ffn2 = _exchange_start("gather_ffn2_start", False, [(w[n] + tok).astype(BF16) for n in FFN2_BIG])
    for n in SMALL:
        if n not in SMALL_SHARDED:
            W[n] = w[n]
    W["ffn1_norm"] = w["ffn1_norm"] + (st_down["token"] + st_mix["token"] + st_ffn2["token"])

    def more(stage, after):
        if stage == "ffn1_down":
            return {"ffn1_w_down": unshard("ffn1_w_down", landed(st_down, "gather_ffn1_down_wait", after)[0])}
        if stage == "ffn2":
            return {n: unshard(n, gth) for n, gth in zip(FFN2_BIG, landed(st_ffn2, "gather_ffn2_wait", after))}
        got = landed(st_mix, "gather_mix_wait", after)
        new = {n: unshard(n, gth) for n, gth in zip(MIX_BIG, got)}
        for n, gth in zip(SMALL_SHARDED, _unpack(got[-1], [s.shape for s in small_shards])):
            new[n] = jnp.moveaxis(gth, 0, -2).reshape(gth.shape[1:-1] + (N_DEV * gth.shape[-1],))
        return new

    R = _layer_fwd(x, target, W, more)
    W = R["W"]
    pending = []

    def emit_big(**named):
        slabs = [_to_slabs(n, g[:, :D_IN] if n == "w_in" else g) for n, g in named.items()]
        started = _exchange_start(f"scatter_start_{len(pending)}", True, slabs)
        pending.append((list(named), started))
        return started["token"]

    small_started = []

    def emit_small(G):
        packed = _pack([G[n] for n in SMALL if n != "ffn1_norm"])
        small_started.append(_exchange_start("gather_small_start", False, [packed]))

    grad_x, G = _layer_bwd(x, W, R, emit_big, emit_small)
    loss = lax.psum(R["loss"][0, 0], ("x", "y", "c"))
    out = {}

    def finish(i, after):
        names, started = pending[i]
        srcs, lands = _exchange_wait(f"scatter_wait_{i}", started, after)
        for n, src, land in zip(names, srcs, lands):
            out[n] = _adamw_slabs(f"adamw_{n}", src, land, me, w[n], m[n], v[n], ROW_TILE[n])

    n_early = len(pending) - 2
    for i in range(n_early):
        finish(i, grad_x)
    early = [n for n in SMALL if n != "ffn1_norm"]
    srcs, lands = _exchange_wait("gather_small_wait", small_started[0], grad_x)
    slot = lax.broadcasted_iota(jnp.int32, (N_DEV, 1, 1), 0)
    slots = jnp.where(slot == me, srcs[0][None], lands[0])
    reduced = dict(zip(early, _unpack(_sum_slots("sum_small_grads", slots), [G[n].shape for n in early])))
    late = _all_gather("gather_ffn1_norm_grad", [_pack([G["ffn1_norm"]])])[0][0]
    reduced["ffn1_norm"] = _unpack(_sum_slots("sum_ffn1_norm_grad", late), [G["ffn1_norm"].shape])[0]
    g_small = []
    for n in SMALL:
        g = reduced[n]
        if n in SMALL_SHARDED:
            per = g.shape[-1] // N_DEV
            g = lax.dynamic_slice_in_dim(g, me * per, per, axis=g.ndim - 1)
        g_small.append(g.reshape(w[n].shape))
    shapes = [w[n].shape for n in SMALL]
    d_p, m_p, v_p = _adamw_packed("adamw_small", _pack(g_small), _pack([w[n] for n in SMALL]),
                                  _pack([m[n] for n in SMALL]), _pack([v[n] for n in SMALL]))
    for n, g, d_, m_, v_ in zip(SMALL, g_small, _unpack(d_p, shapes), _unpack(m_p, shapes), _unpack(v_p, shapes)):
        out[n] = (g, d_, m_, v_)
    for i in range(n_early, len(pending)):
        finish(i, d_p)
    return loss, grad_x, out


def kernel(x, ffn1_norm, ffn1_w_gate, ffn1_w_up, ffn1_w_down, mix_norm, w_in, w_out, rg_conv_w, rg_conv_b, rg_gate_a_w, rg_gate_a_b, rg_gate_x_w, rg_gate_x_b, rg_lambda, gdn_conv_w, gdn_a_log, gdn_dt_bias, gdn_norm, ffn2_norm, ffn2_w_gate, ffn2_w_up, ffn2_w_down, final_norm, loss_target, m_ffn1_norm, m_ffn1_w_gate, m_ffn1_w_up, m_ffn1_w_down, m_mix_norm, m_w_in, m_w_out, m_rg_conv_w, m_rg_conv_b, m_rg_gate_a_w, m_rg_gate_a_b, m_rg_gate_x_w, m_rg_gate_x_b, m_rg_lambda, m_gdn_conv_w, m_gdn_a_log, m_gdn_dt_bias, m_gdn_norm, m_ffn2_norm, m_ffn2_w_gate, m_ffn2_w_up, m_ffn2_w_down, m_final_norm, v_ffn1_norm, v_ffn1_w_gate, v_ffn1_w_up, v_ffn1_w_down, v_mix_norm, v_w_in, v_w_out, v_rg_conv_w, v_rg_conv_b, v_rg_gate_a_w, v_rg_gate_a_b, v_rg_gate_x_w, v_rg_gate_x_b, v_rg_lambda, v_gdn_conv_w, v_gdn_a_log, v_gdn_dt_bias, v_gdn_norm, v_ffn2_norm, v_ffn2_w_gate, v_ffn2_w_up, v_ffn2_w_down, v_final_norm):
    args = dict(locals())
    orig_shapes = {n: args[n].shape for n in WEIGHTS}

    def local(prefix):
        d = {}
        for n in WEIGHTS:
            a = args[prefix + n]
            d[n] = a.reshape(1, -1) if n in ROW_VECTORS else a[0]
        return d

    loss, grad_x, out = _step(x[0], loss_target[0], local(""), local("m_"), local("v_"))
    res = [loss, grad_x[None]]
    for k in range(4):
        res += [out[n][k].reshape(orig_shapes[n]) for n in WEIGHTS]
    return tuple(res)
```

```python
import functools
import math

import jax
import jax.numpy as jnp
from jax import lax
from jax.experimental import pallas as pl
from jax.experimental.pallas import tpu as pltpu

F32, BF16 = jnp.float32, jnp.bfloat16

D_MODEL = 1024
D_FF = 2816
RG_W = 512
RG_BLOCKS = 8
RG_BLOCK = 64
RG_C = 8.0
CONV_W = 4
GDN_H = 4
GDN_DK = 128
CHUNK = 64
EPS = 1e-6
D_IN = 3088
D_IN_PAD = 3200
COL_BA = 3072
N_DEV = 8
HALO = 16
VMEM_LIMIT = 32 * 1024 * 1024
VMEM_CAP = 60 * 1024 * 1024

ADAM_LR = 0.001
ADAM_B1 = 0.9
ADAM_B2 = 0.999
ADAM_EPS = 1e-08
ADAM_WD = 0.01
ADAM_STEP = 10

HI = lax.Precision.HIGHEST


def _cp(n, vmem_limit=None):
    return pltpu.CompilerParams(dimension_semantics=("arbitrary",) * n,
                                vmem_limit_bytes=VMEM_LIMIT if vmem_limit is None else vmem_limit)


def _matmul_vmem_limit(block_bytes, acc_bytes):
    need = 2 * block_bytes + 2 * acc_bytes
    return int(min(VMEM_CAP, max(VMEM_LIMIT, need * 4 // 3)))


def _tile(n, pref):
    return min(n, pref)


def _sigmoid(x):
    return 0.5 * jnp.tanh(0.5 * x) + 0.5


def _softplus(x):
    return jnp.maximum(x, 0.0) + jnp.log(1.0 + jnp.exp(-jnp.abs(x)))


def _dot(a, b, ca, cb, prec=None):
    return lax.dot_general(a, b, (((ca,), (cb,)), ((), ())), preferred_element_type=F32, precision=prec)


def _fused_mm(name, M, N, K, tm, tn, tk, ops, pairs, extras, outs, epilogue):
    nm, nn, nk = M // tm, N // tn, K // tk
    assert nm * tm == M and nn * tn == N and nk * tk == K, (name, M, N, K, tm, tn, tk)
    spec_of = {
        "mk": pl.BlockSpec((tm, tk), lambda i, j, k: (i, k)),
        "km": pl.BlockSpec((tk, tm), lambda i, j, k: (k, i)),
        "kn": pl.BlockSpec((tk, tn), lambda i, j, k: (k, j)),
        "nk": pl.BlockSpec((tn, tk), lambda i, j, k: (j, k)),
    }
    in_specs = [spec_of[m] for _, m in ops]
    in_specs += [pl.BlockSpec(bs, lambda i, j, k, im=im: im(i, j)) for _, bs, im in extras]
    out_specs = [pl.BlockSpec(bs, lambda i, j, k, im=im: im(i, j)) for _, bs, im in outs]
    n_ops, n_ex, n_out = len(ops), len(extras), len(outs)
    n_acc = 1 + max(g for _, _, g in pairs)
    modes = [m for _, m in ops]

    def body(*refs):
        op_refs = refs[:n_ops]
        ex_refs = refs[n_ops:n_ops + n_ex]
        out_refs = refs[n_ops + n_ex:n_ops + n_ex + n_out]
        accs = refs[n_ops + n_ex + n_out:]
        i = pl.program_id(0)
        k = pl.program_id(2)
        def dots():
            vals = [r[...].astype(BF16) for r in op_refs]
            for ia, ib, g in pairs:
                yield g, _dot(vals[ia], vals[ib], 1 if modes[ia] == "mk" else 0, 0 if modes[ib] == "kn" else 1)

        if nk == 1:
            sums = [None] * n_acc
            for g, d in dots():
                sums[g] = d if sums[g] is None else sums[g] + d
            epilogue(i, [_Held(s) for s in sums], ex_refs, out_refs)
            return

        @pl.when(k == 0)
        def _():
            for a in accs:
                a[...] = jnp.zeros_like(a)

        for g, d in dots():
            accs[g][...] += d

        @pl.when(k == nk - 1)
        def _():
            epilogue(i, accs, ex_refs, out_refs)

    op_block = {"mk": tm * tk, "km": tm * tk, "kn": tk * tn, "nk": tk * tn}
    block_bytes = sum(op_block[m] * a.dtype.itemsize for a, m in ops)
    block_bytes += sum(math.prod(bs) * jnp.dtype(a.dtype).itemsize for a, bs, _ in list(extras) + list(outs))
    res = pl.pallas_call(
        body, name=name, grid=(nm, nn, nk), in_specs=in_specs, out_specs=out_specs,
        out_shape=[o for o, _, _ in outs],
        scratch_shapes=[pltpu.VMEM((tm, tn), F32)] * (n_acc if nk > 1 else 0),
        compiler_params=_cp(3, _matmul_vmem_limit(block_bytes, n_acc * tm * tn * 4)),
    )(*[a for a, _ in ops], *[a for a, _, _ in extras])
    return res


class _Held:
    def __init__(self, value):
        self.value = value

    def __getitem__(self, idx):
        return self.value[idx]


def _mn(i, j):
    return (i, j)


def _row0(i, j):
    return (0, 0)


def _rows(name, S, ts, ins, outs, body, scratch=()):
    return pl.pallas_call(
        body, name=name, grid=(S // ts,),
        in_specs=[pl.BlockSpec(bs, im) for _, bs, im in ins],
        out_specs=[pl.BlockSpec(bs, im) for _, bs, im in outs],
        out_shape=[o for o, _, _ in outs],
        scratch_shapes=list(scratch),
        compiler_params=_cp(1),
    )(*[a for a, _, _ in ins])


def _halo_ins(arr, S, ts, width, colblk):
    per = ts // HALO
    last = S // HALO - 1
    return [
        (arr, (ts, width), lambda i: (i, colblk)),
        (arr, (HALO, width), lambda i: (jnp.maximum(i * per - 1, 0), colblk)),
        (arr, (HALO, width), lambda i: (jnp.minimum((i + 1) * per, last), colblk)),
    ]


def _ext(main_ref, prev_ref, next_ref, i, n_tiles):
    prev = jnp.where(i > 0, prev_ref[...].astype(F32), 0.0)
    nxt = jnp.where(i < n_tiles - 1, next_ref[...].astype(F32), 0.0)
    return jnp.concatenate([prev, main_ref[...].astype(F32), nxt], axis=0)


def _shift(ext, off, ts):
    n = ext.shape[0]
    if off == 0:
        return ext[HALO:HALO + ts]
    return pltpu.roll(ext, (-off) % n, 0)[HALO:HALO + ts]


def _rmsnorm_fwd(name, x, g):
    S, D = x.shape
    ts = _tile(S, 512)

    def body(x_ref, g_ref, o_ref):
        xv = x_ref[...]
        r = lax.rsqrt(jnp.mean(xv * xv, axis=-1, keepdims=True) + EPS)
        o_ref[...] = (xv * r * g_ref[...]).astype(BF16)

    return _rows(name, S, ts,
                 [(x, (ts, D), lambda i: (i, 0)), (g, (1, D), lambda i: (0, 0))],
                 [(jax.ShapeDtypeStruct((S, D), BF16), (ts, D), lambda i: (i, 0))], body)[0]


def _rmsnorm_bwd_tile(dh, x, g):
    r = lax.rsqrt(jnp.mean(x * x, axis=-1, keepdims=True) + EPS)
    xhat = x * r
    dxn = dh * g
    dx = r * (dxn - xhat * jnp.mean(dxn * xhat, axis=-1, keepdims=True))
    return dx, dh * xhat


def _ffn_fwd(tag, x, h, wg, wu, wd):
    S = x.shape[0]
    tm = _tile(S, 1024)
    tn = 1408

    def epi_up(i, accs, ex, out):
        a = accs[0][...]
        b = accs[1][...]
        s = _sigmoid(a)
        sa = a * s
        out[0][...] = sa.astype(BF16)
        out[1][...] = (b * (s * (1.0 + a * (1.0 - s)))).astype(BF16)
        out[2][...] = (sa * b).astype(BF16)

    sds = jax.ShapeDtypeStruct((S, D_FF), BF16)
    a, b, f = _fused_mm(f"{tag}_up", S, D_FF, D_MODEL, tm, tn, D_MODEL,
                        [(h, "mk"), (wg, "kn"), (wu, "kn")], [(0, 1, 0), (0, 2, 1)], [],
                        [(sds, (tm, tn), _mn)] * 3, epi_up)

    def epi_down(i, accs, ex, out):
        out[0][...] = ex[0][...] + 0.5 * accs[0][...]

    if callable(wd):
        wd = wd(f)
    xo = _fused_mm(f"{tag}_down", S, D_MODEL, D_FF, tm, D_MODEL, 1408,
                   [(f, "mk"), (wd, "kn")], [(0, 1, 0)], [(x, (tm, D_MODEL), _mn)],
                   [(jax.ShapeDtypeStruct((S, D_MODEL), F32), (tm, D_MODEL), _mn)], epi_down)[0]
    return xo, a, b, f


def _conv_taps(ext, w_ref, ts):
    acc = None
    for j in range(CONV_W):
        term = w_ref[j:j + 1, :] * _shift(ext, j - 2, ts)
        acc = term if acc is None else acc + term
    return acc


def _l2norm_heads(s, scale):
    outs = []
    for h in range(GDN_H):
        sh = s[:, h * GDN_DK:(h + 1) * GDN_DK]
        outs.append(sh * (lax.rsqrt(jnp.sum(sh * sh, axis=-1, keepdims=True) + EPS) * scale))
    return jnp.concatenate(outs, axis=-1)


def _conv_fwd(name, p, colblk, w, bias, mode):
    S = p.shape[0]
    ts = _tile(S, 512)
    n_tiles = S // ts
    C = w.shape[1]

    def body(main, prev, nxt, w_ref, b_ref, o_ref):
        i = pl.program_id(0)
        c = _conv_taps(_ext(main, prev, nxt, i, n_tiles), w_ref, ts)
        if mode == "bias":
            o_ref[...] = c + b_ref[...]
        else:
            s = c * _sigmoid(c)
            if mode == "q":
                s = _l2norm_heads(s, GDN_DK ** -0.5)
            elif mode == "k":
                s = _l2norm_heads(s, 1.0)
            o_ref[...] = s

    ins = _halo_ins(p, S, ts, C, colblk) + [(w, (CONV_W, C), lambda i: (0, 0)), (bias, (1, C), lambda i: (0, 0))]
    return _rows(name, S, ts, ins, [(jax.ShapeDtypeStruct((S, C), F32), (ts, C), lambda i: (i, 0))], body)[0]


def _rg_gate_terms(pre, xc, prm_ref, d):
    r = _sigmoid(pre[:, d * 1024:d * 1024 + RG_W] + prm_ref[2 * d:2 * d + 1, :])
    ig = _sigmoid(pre[:, d * 1024 + RG_W:(d + 1) * 1024] + prm_ref[2 * d + 1:2 * d + 2, :])
    sp = _softplus(-prm_ref[4 + d:5 + d, :])
    log_a = -RG_C * r * sp
    a = jnp.exp(log_a)
    t = jnp.tanh(log_a)
    sq = jnp.sqrt(-2.0 * t / (1.0 - t))
    return r, ig, sp, a, sq


def _rg_gates_fwd(xc, bd, prm):
    S = xc.shape[0]
    tm = _tile(S, 256)

    def epi(i, accs, ex, out):
        pre = accs[0][...]
        xv = ex[0][...]
        for d in range(2):
            r, ig, sp, a, sq = _rg_gate_terms(pre, xv, ex[1], d)
            out[2 * d][...] = a
            out[2 * d + 1][...] = sq * ig * xv

    sds = jax.ShapeDtypeStruct((S, RG_W), F32)
    blk = (tm, RG_W)
    im = lambda i, j: (i, 0)
    return _fused_mm("rg_gates_fwd", S, 4 * RG_W, RG_W, tm, 4 * RG_W, RG_W,
                     [(xc, "mk"), (bd, "kn")], [(0, 1, 0)],
                     [(xc, blk, im), (prm, (8, RG_W), _row0)], [(sds, blk, im)] * 4, epi)


SUBLANES = 8


def _scan_rows(a, b, reverse):
    rows = lax.broadcasted_iota(jnp.int32, a.shape, 0)
    s = 1
    while s < SUBLANES:
        shift = SUBLANES - s if reverse else s
        a_sh = pltpu.roll(a, shift, 0)
        b_sh = pltpu.roll(b, shift, 0)
        valid = (rows < SUBLANES - s) if reverse else (rows >= s)
        b = jnp.where(valid, a * b_sh + b, b)
        a = jnp.where(valid, a * a_sh, a)
        s *= 2
    return a, b


def _rg_scan(name, a_f, b_f, a_b, b_b):
    S, C = a_f.shape
    ts = _tile(S, 512)
    n_tiles = S // ts

    def body(af, bf, ab, bb, hf, hb, carry):
        @pl.when(pl.program_id(0) == 0)
        def _():
            carry[...] = jnp.zeros_like(carry)

        n_sub = ts // SUBLANES

        def step(j, c):
            cf, cb = c
            r0 = pl.multiple_of(j * SUBLANES, SUBLANES)
            cum_a, h0 = _scan_rows(af[pl.ds(r0, SUBLANES), :], bf[pl.ds(r0, SUBLANES), :], False)
            h = h0 + cum_a * cf
            hf[pl.ds(r0, SUBLANES), :] = h
            cf = h[SUBLANES - 1:SUBLANES, :]
            r1 = pl.multiple_of((n_sub - 1 - j) * SUBLANES, SUBLANES)
            cum_a, h0 = _scan_rows(ab[pl.ds(r1, SUBLANES), :], bb[pl.ds(r1, SUBLANES), :], True)
            h = h0 + cum_a * cb
            hb[pl.ds(r1, SUBLANES), :] = h
            cb = h[0:1, :]
            return cf, cb

        cf, cb = lax.fori_loop(0, n_sub, step, (carry[0:1, :], carry[1:2, :]), unroll=4)
        carry[0:1, :] = cf
        carry[1:2, :] = cb

    fw = lambda i: (i, 0)
    bw = lambda i: (n_tiles - 1 - i, 0)
    sds = jax.ShapeDtypeStruct((S, C), F32)
    return _rows(name, S, ts,
                 [(a_f, (ts, C), fw), (b_f, (ts, C), fw), (a_b, (ts, C), bw), (b_b, (ts, C), bw)],
                 [(sds, (ts, C), fw), (sds, (ts, C), bw)], body, scratch=[pltpu.VMEM((8, C), F32)])


def _tri_masks():
    ri = lax.broadcasted_iota(jnp.int32, (CHUNK, CHUNK), 0)
    ci = lax.broadcasted_iota(jnp.int32, (CHUNK, CHUNK), 1)
    return ri, ci


def _gdn_prep_fwd(p, prm):
    S = p.shape[0]
    ts = _tile(S, 512)

    def body(p_ref, prm_ref, o_ref):
        raw = p_ref[...].astype(F32)
        lane = lax.broadcasted_iota(jnp.int32, (1, 128), 1)
        g = -jnp.exp(prm_ref[0:1, :]) * _softplus(raw + prm_ref[1:2, :])
        g = jnp.where((lane >= 8) & (lane < 16), g, 0.0)
        beta = _sigmoid(raw)
        ri, ci = _tri_masks()
        lower = (ri >= ci).astype(F32)
        upper = (ri <= ci).astype(F32)
        for c in range(ts // CHUNK):
            rows = slice(c * CHUNK, (c + 1) * CHUNK)
            gch = g[rows]
            gc = jnp.where(lane < 12, _dot(lower, gch, 1, 0, HI), _dot(upper, gch, 1, 0, HI))
            o_ref[rows, :] = jnp.where(lane < 8, beta[rows], gc)

    return _rows("gdn_prep_fwd", S, ts,
                 [(p, (ts, 128), lambda i: (i, COL_BA // 128)), (prm, (8, 128), lambda i: (0, 0))],
                 [(jax.ShapeDtypeStruct((S, 128), F32), (ts, 128), lambda i: (i, 0))], body)[0]


def _bdot(a, b, ca, cb):
    return _dot(a.astype(BF16), b.astype(BF16), ca, cb)


GDN_W = GDN_H * GDN_DK
GDN_TS = 256
LOCAL_CHUNKS = 2

def _gdn_decay(bg_ref, gcr_ref, c, rows, r0, col, rev, ri, ci):
    beta = bg_ref[rows, col:col + 1]
    gc = bg_ref[rows, 8 + col:9 + col]
    last = 0 if rev else CHUNK - 1
    gl = bg_ref[pl.ds(r0 + last, 1), 8 + col:9 + col]
    out = dict(beta=beta, gc=gc, gl=gl, eg=jnp.exp(gc), egl=jnp.exp(gl - gc), cd=jnp.exp(gl))
    if gcr_ref is not None:
        incl = (ri <= ci) if rev else (ri >= ci)
        out["strict"] = (ri < ci) if rev else (ri > ci)
        out["dm"] = jnp.where(incl, jnp.exp(jnp.where(incl, gc - gcr_ref[c, col:col + 1, :], 0.0)), 0.0)
    return out


def _dir_tile(d, n_tiles, flip):
    if (d == 1) != flip:
        return lambda i: n_tiles - 1 - i
    return lambda i: i


def _gdn_local_fwd(q, k, v, bg, gcr):
    S = q.shape[0]
    ts = _tile(S, GDN_TS)
    ncb = ts // CHUNK
    nch = S // CHUNK

    def body(q_ref, k_ref, v_ref, bg_ref, gcr_ref, *out_refs):
        ri, ci = _tri_masks()
        eye = (ri == ci).astype(F32)
        outs = (out_refs[0:6], out_refs[6:12])
        cd_ref = out_refs[12]

        def chunk(cc, carry):
            chains = []
            for c in (LOCAL_CHUNKS * cc + j for j in range(LOCAL_CHUNKS)):
                r0 = pl.multiple_of(c * CHUNK, CHUNK)
                rows = pl.ds(r0, CHUNK)
                for h in range(GDN_H):
                    cols = slice(h * GDN_DK, (h + 1) * GDN_DK)
                    qh, kh, vh = q_ref[rows, cols], k_ref[rows, cols], v_ref[rows, cols]
                    both = _bdot(jnp.concatenate([qh, kh], axis=0), kh, 1, 1)
                    for d in range(2):
                        chains.append(dict(c=c, r0=r0, rows=rows, h=h, d=d, cols=cols, qh=qh, kh=kh, vh=vh,
                                           qk=both[0:CHUNK], kk=both[CHUNK:2 * CHUNK]))
            for ch in chains:
                m = _gdn_decay(bg_ref, gcr_ref, ch["c"], ch["rows"], ch["r0"], ch["d"] * GDN_H + ch["h"], ch["d"] == 1,
                               ri, ci)
                ch["m"] = m
                ch["x"] = -jnp.where(m["strict"], m["beta"] * ch["kk"] * m["dm"], 0.0)
                ch["t"] = eye + ch["x"]
            for ch in chains:
                ch["pw"] = _bdot(ch["x"], ch["x"], 1, 0)
            for level in range(1, 6):
                last_level = level == 5
                for ch in chains:
                    rhs = ch["t"] if last_level else jnp.concatenate([ch["t"], ch["pw"]], axis=1)
                    ch["prod"] = _bdot(ch["pw"], rhs, 1, 0)
                for ch in chains:
                    ch["t"] = ch["t"] + ch["prod"][:, 0:CHUNK]
                    if not last_level:
                        ch["pw"] = ch["prod"][:, CHUNK:2 * CHUNK]
            for ch in chains:
                m = ch["m"]
                rhs = jnp.concatenate([ch["vh"] * m["beta"], ch["kh"] * (m["beta"] * m["eg"])], axis=1)
                ch["uw"] = _bdot(ch["t"], rhs, 1, 0)
            for ch in chains:
                u_ref, w_ref, a_ref, t_ref, qd_ref, kd_ref = outs[ch["d"]]
                m = ch["m"]
                c, rows = ch["c"], ch["rows"]
                col = ch["d"] * GDN_H + ch["h"]
                u_ref[rows, ch["cols"]] = ch["uw"][:, 0:GDN_DK]
                w_ref[rows, ch["cols"]] = ch["uw"][:, GDN_DK:2 * GDN_DK].astype(BF16)
                a_ref[c, ch["h"]] = (ch["qk"] * m["dm"]).astype(BF16)
                t_ref[c, ch["h"]] = _bdot(ch["t"], eye, 0, 0).astype(BF16)
                qd_ref[rows, ch["cols"]] = (ch["qh"] * m["eg"]).astype(BF16)
                kd_ref[rows, ch["cols"]] = (ch["kh"] * m["egl"]).astype(BF16)
                cd_ref[c, col:col + 1, :] = jnp.broadcast_to(m["cd"], (1, 128))
            return carry

        lax.fori_loop(0, ncb // LOCAL_CHUNKS, chunk, 0)

    im = lambda i: (i, 0)
    im4 = lambda i: (i, 0, 0, 0)
    ins = [(q, (ts, GDN_W), im), (k, (ts, GDN_W), im), (v, (ts, GDN_W), im), (bg, (ts, 128), im),
           (gcr, (ncb, 8, CHUNK), lambda i: (i, 0, 0))]
    per_dir = [(jax.ShapeDtypeStruct((S, GDN_W), F32), (ts, GDN_W), im),
               (jax.ShapeDtypeStruct((S, GDN_W), BF16), (ts, GDN_W), im),
               (jax.ShapeDtypeStruct((nch, GDN_H, CHUNK, CHUNK), BF16), (ncb, GDN_H, CHUNK, CHUNK), im4),
               (jax.ShapeDtypeStruct((nch, GDN_H, CHUNK, CHUNK), BF16), (ncb, GDN_H, CHUNK, CHUNK), im4),
               (jax.ShapeDtypeStruct((S, GDN_W), BF16), (ts, GDN_W), im),
               (jax.ShapeDtypeStruct((S, GDN_W), BF16), (ts, GDN_W), im)]
    cd_out = (jax.ShapeDtypeStruct((nch, 8, 128), F32), (ncb, 8, 128), lambda i: (i, 0, 0))
    res = _rows("gdn_local_fwd", S, ts, ins, per_dir * 2 + [cd_out], body)
    return res[0:6], res[6:12], res[12]


def _gdn_scan_fwd(loc):
    S = loc[0][0].shape[0]
    ts = _tile(S, GDN_TS)
    n_tiles = S // ts
    ncb = ts // CHUNK
    nch = S // CHUNK

    def body(*refs):
        ins = (refs[0:6], refs[6:12])
        outs = (refs[12:15], refs[15:18])
        state = refs[18]

        @pl.when(pl.program_id(0) == 0)
        def _():
            state[...] = jnp.zeros_like(state)

        def chunk(cc, carry):
            chains = []
            for d in range(2):
                c = cc if d == 0 else ncb - 1 - cc
                rows = pl.ds(pl.multiple_of(c * CHUNK, CHUNK), CHUNK)
                for h in range(GDN_H):
                    cols = slice(h * GDN_DK, (h + 1) * GDN_DK)
                    chains.append(dict(d=d, h=h, c=c, rows=rows, cols=cols, st=state[d * GDN_H + h]))
            for ch in chains:
                qd_ref, kd_ref, u_ref, w_ref, a_ref, cd_ref = ins[ch["d"]]
                rows, cols = ch["rows"], ch["cols"]
                lhs = jnp.concatenate([w_ref[rows, cols], qd_ref[rows, cols]], axis=0)
                ch["ws_qs"] = _dot(lhs, ch["st"].astype(BF16), 1, 0)
            for ch in chains:
                qd_ref, kd_ref, u_ref, w_ref, a_ref, cd_ref = ins[ch["d"]]
                rows, cols = ch["rows"], ch["cols"]
                vn = u_ref[rows, cols] - ch["ws_qs"][0:CHUNK]
                vnb = vn.astype(BF16)
                ch["vn"] = vn
                ch["avn"] = _dot(a_ref[ch["c"], ch["h"]], vnb, 1, 0)
                ch["kvn"] = _dot(kd_ref[rows, cols], vnb, 0, 0)
            for ch in chains:
                o_ref, vn_ref, s_ref = outs[ch["d"]]
                cd_ref = ins[ch["d"]][5]
                rows, cols = ch["rows"], ch["cols"]
                col = ch["d"] * GDN_H + ch["h"]
                o_ref[rows, cols] = ch["ws_qs"][CHUNK:2 * CHUNK] + ch["avn"]
                vn_ref[rows, cols] = ch["vn"].astype(BF16)
                s_ref[ch["c"], ch["h"]] = ch["st"].astype(BF16)
                state[ch["d"] * GDN_H + ch["h"]] = ch["st"] * cd_ref[ch["c"], col:col + 1, :] + ch["kvn"]
            return carry

        lax.fori_loop(0, ncb, chunk, 0)

    ins, outs = [], []
    for d in range(2):
        tix = _dir_tile(d, n_tiles, False)
        im = lambda i, tix=tix: (tix(i), 0)
        im4 = lambda i, tix=tix: (tix(i), 0, 0, 0)
        u, w, a, _, qd, kd = loc[d]
        ins += [(qd, (ts, GDN_W), im), (kd, (ts, GDN_W), im), (u, (ts, GDN_W), im), (w, (ts, GDN_W), im),
                (a, (ncb, GDN_H, CHUNK, CHUNK), im4), (loc[2], (ncb, 8, 128), lambda i, tix=tix: (tix(i), 0, 0))]
        outs += [(jax.ShapeDtypeStruct((S, GDN_W), F32), (ts, GDN_W), im),
                 (jax.ShapeDtypeStruct((S, GDN_W), BF16), (ts, GDN_W), im),
                 (jax.ShapeDtypeStruct((nch, GDN_H, GDN_DK, GDN_DK), BF16), (ncb, GDN_H, GDN_DK, GDN_DK), im4)]
    res = _rows("gdn_scan_fwd", S, ts, ins, outs, body, scratch=[pltpu.VMEM((2 * GDN_H, GDN_DK, GDN_DK), F32)])
    return res[0:3], res[3:6]


def _gelu(x):
    c = math.sqrt(2.0 / math.pi)
    t = jnp.tanh(c * (x + 0.044715 * x * x * x))
    return 0.5 * x * (1.0 + t), t


def _mix_out_fwd(h_f, h_b, o_f, o_b, p, gn):
    S = h_f.shape[0]
    ts = _tile(S, 512)

    def body(hf, hb, of, ob, gate, z, gn_ref, y_ref):
        ge, _ = _gelu(gate[...].astype(F32))
        y_ref[:, 0:RG_W] = ((hf[...] + hb[...]) * ge).astype(BF16)
        o = of[...] + ob[...]
        zv = z[...].astype(F32)
        sz = zv * _sigmoid(zv)
        for h in range(GDN_H):
            cols = slice(h * GDN_DK, (h + 1) * GDN_DK)
            oh = o[:, cols]
            n = oh * lax.rsqrt(jnp.mean(oh * oh, axis=-1, keepdims=True) + EPS) * gn_ref[...]
            y_ref[:, RG_W + h * GDN_DK:RG_W + (h + 1) * GDN_DK] = (n * sz[:, cols]).astype(BF16)

    blk = (ts, RG_W)
    im = lambda i: (i, 0)
    ins = [(h_f, blk, im), (h_b, blk, im), (o_f, blk, im), (o_b, blk, im),
           (p, blk, lambda i: (i, 1)), (p, blk, lambda i: (i, 5)), (gn, (1, GDN_DK), lambda i: (0, 0))]
    return _rows("mix_out_fwd", S, ts, ins,
                 [(jax.ShapeDtypeStruct((S, D_MODEL), BF16), (ts, D_MODEL), im)], body)[0]


def _loss_head(x, target, g):
    S, D = x.shape
    ts = _tile(S, 512)

    def body(x_ref, t_ref, g_ref, dx_ref, loss_ref, dg_ref):
        @pl.when(pl.program_id(0) == 0)
        def _():
            loss_ref[...] = jnp.zeros_like(loss_ref)
            dg_ref[...] = jnp.zeros_like(dg_ref)

        xv = x_ref[...]
        gv = g_ref[...]
        r = lax.rsqrt(jnp.mean(xv * xv, axis=-1, keepdims=True) + EPS)
        err = xv * r * gv - t_ref[...]
        loss_ref[...] += jnp.sum(err * err) * (0.5 / D)
        dx, dgt = _rmsnorm_bwd_tile(err * (1.0 / D), xv, gv)
        dx_ref[...] = dx
        dg_ref[...] += jnp.sum(dgt, axis=0, keepdims=True)

    im = lambda i: (i, 0)
    z = lambda i: (0, 0)
    return _rows("loss_head", S, ts,
                 [(x, (ts, D), im), (target, (ts, D), im), (g, (1, D), z)],
                 [(jax.ShapeDtypeStruct((S, D), F32), (ts, D), im),
                  (jax.ShapeDtypeStruct((8, 128), F32), (8, 128), z),
                  (jax.ShapeDtypeStruct((1, D), F32), (1, D), z)], body)


def _block_diag(w):
    n = w.shape[0]
    return jnp.einsum("nij,nm->nimj", w, jnp.eye(n, dtype=w.dtype)).reshape(n * w.shape[1], n * w.shape[2])


def _rg_bd(a_w, x_w):
    return jnp.concatenate([_block_diag(a_w[0]), _block_diag(x_w[0]), _block_diag(a_w[1]), _block_diag(x_w[1])],
                           axis=1).astype(BF16)


def _rg_prm(ba, bx, lam):
    return jnp.concatenate([ba[0:1], bx[0:1], ba[1:2], bx[1:2], lam, jnp.zeros((2, RG_W), F32)], axis=0)


def _gdn_prm(a_log, dt_bias):
    rows = jnp.zeros((8, 128), F32)
    rows = rows.at[0, 8:16].set(a_log.reshape(-1))
    return rows.at[1, 8:16].set(dt_bias.reshape(-1))


def _gc_rows(bg):
    S = bg.shape[0]
    return bg[:, 8:16].reshape(S // CHUNK, CHUNK, 8).transpose(0, 2, 1)


def _layer_fwd(x0, target, W, more=None):
    S = x0.shape[0]
    R = {}
    R["h1"] = _rmsnorm_fwd("rms1", x0, W["ffn1_norm"])
    late_wd = {}

    def ffn1_wd(after):
        late_wd.update(more("ffn1_down", after))
        return late_wd["ffn1_w_down"]

    R["x1"], R["a1"], R["b1"], R["f1"] = _ffn_fwd("ffn1", x0, R["h1"], W["ffn1_w_gate"], W["ffn1_w_up"],
                                                  ffn1_wd if more is not None else W["ffn1_w_down"])
    if more is not None:
        W = {**W, **late_wd, **more("mixer", R["x1"])}
    R["h2"] = _rmsnorm_fwd("rms2", R["x1"], W["mix_norm"])
    tm = _tile(S, 512)
    tmp = _tile(S, 1024)
    R["p"] = _fused_mm("in_proj", S, D_IN_PAD, D_MODEL, tmp, 640, D_MODEL, [(R["h2"], "mk"), (W["w_in"], "kn")],
                       [(0, 1, 0)], [], [(jax.ShapeDtypeStruct((S, D_IN_PAD), BF16), (tmp, 640), _mn)],
                       lambda i, accs, ex, out: out[0].__setitem__(Ellipsis, accs[0][...].astype(BF16)))[0]
    p = R["p"]
    R["xc"] = _conv_fwd("rg_conv_fwd", p, 0, W["rg_conv_w"], W["rg_conv_b"], "bias")
    R["bd"] = _rg_bd(W["rg_gate_a_w"], W["rg_gate_x_w"])
    R["rg_prm"] = _rg_prm(W["rg_gate_a_b"], W["rg_gate_x_b"], W["rg_lambda"])
    a_f, b_f, a_b, b_b = _rg_gates_fwd(R["xc"], R["bd"], R["rg_prm"])
    R["a_f"], R["a_b"] = a_f, a_b
    R["h_f"], R["h_b"] = _rg_scan("rg_scan_fwd", a_f, b_f, a_b, b_b)
    zero_b = jnp.zeros((1, RG_W), F32)
    cw = W["gdn_conv_w"]
    R["q"] = _conv_fwd("gdn_conv_q", p, 2, cw[:, 0:512], zero_b, "q")
    R["k"] = _conv_fwd("gdn_conv_k", p, 3, cw[:, 512:1024], zero_b, "k")
    R["v"] = _conv_fwd("gdn_conv_v", p, 4, cw[:, 1024:1536], zero_b, "v")
    R["gdn_prm"] = _gdn_prm(W["gdn_a_log"], W["gdn_dt_bias"])
    R["bg"] = _gdn_prep_fwd(p, R["gdn_prm"])
    R["gcr"] = _gc_rows(R["bg"])
    R["gdn_loc"] = _gdn_local_fwd(R["q"], R["k"], R["v"], R["bg"], R["gcr"])
    R["gdn_fwd"] = _gdn_scan_fwd(R["gdn_loc"])
    R["o_f"], R["o_b"] = R["gdn_fwd"][0][0], R["gdn_fwd"][1][0]
    R["y"] = _mix_out_fwd(R["h_f"], R["h_b"], R["o_f"], R["o_b"], p, W["gdn_norm"])
    R["x2"] = _fused_mm("out_proj", S, D_MODEL, D_MODEL, tm, D_MODEL, D_MODEL, [(R["y"], "mk"), (W["w_out"], "kn")],
                        [(0, 1, 0)], [(R["x1"], (tm, D_MODEL), _mn)],
                        [(jax.ShapeDtypeStruct((S, D_MODEL), F32), (tm, D_MODEL), _mn)],
                        lambda i, accs, ex, out: out[0].__setitem__(Ellipsis, ex[0][...] + accs[0][...]))[0]
    if more is not None:
        W = {**W, **more("ffn2", R["x2"])}
    R["h3"] = _rmsnorm_fwd("rms3", R["x2"], W["ffn2_norm"])
    R["x3"], R["a2"], R["b2"], R["f2"] = _ffn_fwd("ffn2", R["x2"], R["h3"], W["ffn2_w_gate"], W["ffn2_w_up"], W["ffn2_w_down"])
    R["dx3"], R["loss"], R["d_final_norm"] = _loss_head(R["x3"], target, W["final_norm"])
    R["W"] = W
    return R


def _colsum_into(ref, i, val):
    @pl.when(i == 0)
    def _():
        ref[...] = val

    @pl.when(i > 0)
    def _():
        ref[...] += val


def _ffn_bwd(tag, dout, x, g, h, a, b, f, wg, wu, wd, emit):
    S = x.shape[0]
    tm = _tile(S, 512)
    tk_s = _tile(S, 1024)
    dwd = _fused_mm(f"{tag}_dw_down", D_FF, D_MODEL, S, 1408, D_MODEL, tk_s, [(f, "km"), (dout, "kn")], [(0, 1, 0)], [],
                    [(jax.ShapeDtypeStruct((D_FF, D_MODEL), BF16), (1408, D_MODEL), _mn)],
                    lambda i, accs, ex, out: out[0].__setitem__(Ellipsis, (0.5 * accs[0][...]).astype(BF16)))[0]
    emit(down=dwd)

    def epi_act(i, accs, ex, out):
        df = 0.5 * accs[0][...]
        out[0][...] = (df * ex[1][...].astype(F32)).astype(BF16)
        out[1][...] = (df * ex[0][...].astype(F32)).astype(BF16)

    sds = jax.ShapeDtypeStruct((S, D_FF), BF16)
    da, db = _fused_mm(f"{tag}_dact", S, D_FF, D_MODEL, tm, 1408, D_MODEL, [(dout, "mk"), (wd, "nk")], [(0, 1, 0)],
                       [(a, (tm, 1408), _mn), (b, (tm, 1408), _mn)], [(sds, (tm, 1408), _mn)] * 2, epi_act)

    def epi_w2(i, accs, ex, out):
        out[0][...] = accs[0][...].astype(BF16)
        out[1][...] = accs[1][...].astype(BF16)

    sdw = jax.ShapeDtypeStruct((D_MODEL, D_FF), BF16)
    dwg, dwu = _fused_mm(f"{tag}_dw_up", D_MODEL, D_FF, S, D_MODEL, 1408, tk_s,
                         [(h, "km"), (da, "kn"), (db, "kn")], [(0, 1, 0), (0, 2, 1)], [],
                         [(sdw, (D_MODEL, 1408), _mn)] * 2, epi_w2)
    tok = emit(gate=dwg, up=dwu)
    if tok is not None:
        g = g + tok

    def epi_dx(i, accs, ex, out):
        dx, dgt = _rmsnorm_bwd_tile(accs[0][...], ex[0][...], ex[1][...])
        out[0][...] = ex[2][...] + dx
        _colsum_into(out[1], i, jnp.sum(dgt, axis=0, keepdims=True))

    tmx = _tile(S, 1024)
    dx, dg = _fused_mm(f"{tag}_dx", S, D_MODEL, D_FF, tmx, D_MODEL, 1408,
                       [(da, "mk"), (wg, "nk"), (db, "mk"), (wu, "nk")], [(0, 1, 0), (2, 3, 0)],
                       [(x, (tmx, D_MODEL), _mn), (g, (1, D_MODEL), _row0), (dout, (tmx, D_MODEL), _mn)],
                       [(jax.ShapeDtypeStruct((S, D_MODEL), F32), (tmx, D_MODEL), _mn),
                        (jax.ShapeDtypeStruct((1, D_MODEL), F32), (1, D_MODEL), _row0)], epi_dx)
    return dx, dg


def _mix_out_bwd(dy, h_f, h_b, o_f, o_b, p, gn):
    S = dy.shape[0]
    ts = _tile(S, 512)
    c0 = math.sqrt(2.0 / math.pi)

    def body(dy_ref, hf, hb, of, ob, gate, z, gn_ref, dhr_ref, dgate_ref, do_ref, dz_ref, dgn_ref):
        i = pl.program_id(0)
        gv = gate[...].astype(F32)
        ge, t = _gelu(gv)
        dy_rg = dy_ref[:, 0:RG_W]
        dhr_ref[...] = dy_rg * ge
        dgelu = 0.5 * (1.0 + t) + 0.5 * gv * (1.0 - t * t) * c0 * (1.0 + 3.0 * 0.044715 * gv * gv)
        dgate_ref[...] = (dy_rg * (hf[...] + hb[...]) * dgelu).astype(BF16)
        o = of[...] + ob[...]
        zv = z[...].astype(F32)
        sig = _sigmoid(zv)
        gnv = gn_ref[...]
        dgn = jnp.zeros((1, GDN_DK), F32)
        for h in range(GDN_H):
            cols = slice(h * GDN_DK, (h + 1) * GDN_DK)
            oh = o[:, cols]
            r = lax.rsqrt(jnp.mean(oh * oh, axis=-1, keepdims=True) + EPS)
            ohat = oh * r
            dyh = dy_ref[:, RG_W + h * GDN_DK:RG_W + (h + 1) * GDN_DK]
            zh = zv[:, cols]
            sh = sig[:, cols]
            dn = dyh * zh * sh
            dz_ref[:, cols] = (dyh * ohat * gnv * (sh * (1.0 + zh * (1.0 - sh)))).astype(BF16)
            dxn = dn * gnv
            do_ref[:, cols] = r * (dxn - ohat * jnp.mean(dxn * ohat, axis=-1, keepdims=True))
            dgn = dgn + jnp.sum(dn * ohat, axis=0, keepdims=True)
        _colsum_into(dgn_ref, i, dgn)

    blk = (ts, RG_W)
    im = lambda i: (i, 0)
    z0 = lambda i: (0, 0)
    ins = [(dy, (ts, D_MODEL), im), (h_f, blk, im), (h_b, blk, im), (o_f, blk, im), (o_b, blk, im),
           (p, blk, lambda i: (i, 1)), (p, blk, lambda i: (i, 5)), (gn, (1, GDN_DK), z0)]
    outs = [(jax.ShapeDtypeStruct((S, RG_W), F32), blk, im), (jax.ShapeDtypeStruct((S, RG_W), BF16), blk, im),
            (jax.ShapeDtypeStruct((S, RG_W), F32), blk, im), (jax.ShapeDtypeStruct((S, RG_W), BF16), blk, im),
            (jax.ShapeDtypeStruct((1, GDN_DK), F32), (1, GDN_DK), z0)]
    return _rows("mix_out_bwd", S, ts, ins, outs, body)


def _rg_scan_adj(name, a_up, b_up, a_dn, b_dn):
    S, C = a_up.shape
    ts = _tile(S, 512)
    n_tiles = S // ts

    def body(au, bu, ad, bd, mu_ref, lam_ref, carry):
        @pl.when(pl.program_id(0) == 0)
        def _():
            carry[...] = jnp.zeros_like(carry)

        n_sub = ts // SUBLANES
        rows = lax.broadcasted_iota(jnp.int32, (SUBLANES, C), 0)

        def half(a_ref, b_ref, out_ref, r0, c_in, reverse):
            a = a_ref[pl.ds(r0, SUBLANES), :]
            b = b_ref[pl.ds(r0, SUBLANES), :]
            cum_a, c0 = _scan_rows(a, a * b, reverse)
            c = c0 + cum_a * c_in
            edge = 0 if not reverse else SUBLANES - 1
            c_prev = jnp.where(rows == edge, c_in, pltpu.roll(c, SUBLANES - 1 if reverse else 1, 0))
            out_ref[pl.ds(r0, SUBLANES), :] = b + c_prev
            return c[0:1, :] if reverse else c[SUBLANES - 1:SUBLANES, :]

        def step(j, c):
            cu, cd = c
            cu = half(au, bu, mu_ref, pl.multiple_of(j * SUBLANES, SUBLANES), cu, False)
            cd = half(ad, bd, lam_ref, pl.multiple_of((n_sub - 1 - j) * SUBLANES, SUBLANES), cd, True)
            return cu, cd

        cu, cd = lax.fori_loop(0, n_sub, step, (carry[0:1, :], carry[1:2, :]), unroll=4)
        carry[0:1, :] = cu
        carry[1:2, :] = cd

    fw = lambda i: (i, 0)
    bw = lambda i: (n_tiles - 1 - i, 0)
    sds = jax.ShapeDtypeStruct((S, C), F32)
    return _rows(name, S, ts,
                 [(a_up, (ts, C), fw), (b_up, (ts, C), fw), (a_dn, (ts, C), bw), (b_dn, (ts, C), bw)],
                 [(sds, (ts, C), fw), (sds, (ts, C), bw)], body, scratch=[pltpu.VMEM((8, C), F32)])


def _halo_ex(arr, S, tm, width):
    per = tm // HALO
    last = S // HALO - 1
    return [
        (arr, (tm, width), lambda i, j: (i, 0)),
        (arr, (HALO, width), lambda i, j: (jnp.maximum(i * per - 1, 0), 0)),
        (arr, (HALO, width), lambda i, j: (jnp.minimum((i + 1) * per, last), 0)),
    ]


def _rg_gates_bwd(xc, bd, prm, lam_f, lam_b, h_f, h_b):
    S = xc.shape[0]
    tm = _tile(S, 256)
    n_tiles = S // tm

    def epi(i, accs, ex, out):
        pre = accs[0][...]
        xv = ex[0][...]
        prm_ref = ex[1]
        lams = (ex[2][...], ex[3][...])
        hprev = (_shift(_ext(ex[4], ex[5], ex[6], i, n_tiles), -1, tm),
                 _shift(_ext(ex[7], ex[8], ex[9], i, n_tiles), 1, tm))
        dxc = jnp.zeros_like(xv)
        rows = []
        dlam_rows = []
        for d in range(2):
            r, ig, sp, a, sq = _rg_gate_terms(pre, xv, prm_ref, d)
            lam = lams[d]
            da = lam * hprev[d]
            di = lam * sq * xv
            dxc = dxc + lam * sq * ig
            dsq = lam * ig * xv
            dlog_a = da * a - dsq * (a * a) / sq
            dpre_r = dlog_a * (-RG_C * sp) * r * (1.0 - r)
            dpre_i = di * ig * (1.0 - ig)
            out[0][:, d * 1024:d * 1024 + RG_W] = dpre_r.astype(BF16)
            out[0][:, d * 1024 + RG_W:(d + 1) * 1024] = dpre_i.astype(BF16)
            rows += [jnp.sum(dpre_r, axis=0, keepdims=True), jnp.sum(dpre_i, axis=0, keepdims=True)]
            dsp = jnp.sum(dlog_a * (-RG_C * r), axis=0, keepdims=True)
            dlam_rows.append(-dsp * _sigmoid(-prm_ref[4 + d:5 + d, :]))
        out[1][...] = dxc
        zero = jnp.zeros((2, RG_W), F32)
        _colsum_into(out[2], i, jnp.concatenate(rows + dlam_rows + [zero], axis=0))

    blk = (tm, RG_W)
    im = lambda i, j: (i, 0)
    extras = ([(xc, blk, im), (prm, (8, RG_W), _row0), (lam_f, blk, im), (lam_b, blk, im)]
              + _halo_ex(h_f, S, tm, RG_W) + _halo_ex(h_b, S, tm, RG_W))
    outs = [(jax.ShapeDtypeStruct((S, 4 * RG_W), BF16), (tm, 4 * RG_W), im),
            (jax.ShapeDtypeStruct((S, RG_W), F32), blk, im),
            (jax.ShapeDtypeStruct((8, RG_W), F32), (8, RG_W), _row0)]
    return _fused_mm("rg_gates_bwd", S, 4 * RG_W, RG_W, tm, 4 * RG_W, RG_W, [(xc, "mk"), (bd, "kn")], [(0, 1, 0)],
                     extras, outs, epi)


def _roll_rows(ext, off):
    if off == 0:
        return ext
    return pltpu.roll(ext, (-off) % ext.shape[0], 0)


def _conv_bwd(name, p, colblk, w, grads, mode):
    S = p.shape[0]
    ts = _tile(S, 512)
    n_tiles = S // ts
    C = w.shape[1]
    ng = len(grads)

    def body(*refs):
        p_refs = refs[0:3]
        g_refs = refs[3:3 + 3 * ng]
        w_ref = refs[3 + 3 * ng]
        dx_ref, dw_ref, db_ref = refs[4 + 3 * ng:]
        i = pl.program_id(0)
        ext_p = _ext(*p_refs, i, n_tiles)
        dn = _ext(*g_refs[0:3], i, n_tiles)
        for gi in range(1, ng):
            dn = dn + _ext(*g_refs[3 * gi:3 * gi + 3], i, n_tiles)
        if mode == "bias":
            dc = dn
        else:
            c = None
            for j in range(CONV_W):
                term = w_ref[j:j + 1, :] * _roll_rows(ext_p, j - 2)
                c = term if c is None else c + term
            sig = _sigmoid(c)
            s = c * sig
            if mode in ("q", "k"):
                scale = GDN_DK ** -0.5 if mode == "q" else 1.0
                parts = []
                for h in range(GDN_H):
                    cols = slice(h * GDN_DK, (h + 1) * GDN_DK)
                    sh = s[:, cols]
                    dnh = dn[:, cols]
                    rinv = lax.rsqrt(jnp.sum(sh * sh, axis=-1, keepdims=True) + EPS)
                    parts.append(scale * rinv * (dnh - sh * (rinv * rinv) * jnp.sum(dnh * sh, axis=-1, keepdims=True)))
                ds = jnp.concatenate(parts, axis=-1)
            else:
                ds = dn
            dc = ds * (sig * (1.0 + c * (1.0 - sig)))
        dx = None
        for j in range(CONV_W):
            term = w_ref[j:j + 1, :] * _shift(dc, 2 - j, ts)
            dx = term if dx is None else dx + term
        dx_ref[...] = dx.astype(BF16)
        dc_main = dc[HALO:HALO + ts]
        dw = jnp.concatenate([jnp.sum(dc_main * _shift(ext_p, j - 2, ts), axis=0, keepdims=True)
                              for j in range(CONV_W)], axis=0)
        _colsum_into(dw_ref, i, dw)
        _colsum_into(db_ref, i, jnp.sum(dc_main, axis=0, keepdims=True))

    ins = _halo_ins(p, S, ts, C, colblk)
    for garr in grads:
        ins += _halo_ins(garr, S, ts, C, 0)
    ins += [(w, (CONV_W, C), lambda i: (0, 0))]
    z0 = lambda i: (0, 0)
    outs = [(jax.ShapeDtypeStruct((S, C), BF16), (ts, C), lambda i: (i, 0)),
            (jax.ShapeDtypeStruct((CONV_W, C), F32), (CONV_W, C), z0),
            (jax.ShapeDtypeStruct((1, C), F32), (1, C), z0)]
    return _rows(name, S, ts, ins, outs, body)


def _gdn_scan_bwd(loc, do):
    S = do.shape[0]
    ts = _tile(S, GDN_TS)
    n_tiles = S // ts
    ncb = ts // CHUNK
    nch = S // CHUNK

    def body(*refs):
        ins = (refs[0:6], refs[6:12])
        outs = (refs[12:14], refs[14:16])
        dstate = refs[16]

        @pl.when(pl.program_id(0) == 0)
        def _():
            dstate[...] = jnp.zeros_like(dstate)

        def chunk(cc, carry):
            chains = []
            for d in range(2):
                c = ncb - 1 - cc if d == 0 else cc
                rows = pl.ds(pl.multiple_of(c * CHUNK, CHUNK), CHUNK)
                for h in range(GDN_H):
                    cols = slice(h * GDN_DK, (h + 1) * GDN_DK)
                    chains.append(dict(d=d, h=h, c=c, rows=rows, cols=cols, dsn=dstate[d * GDN_H + h]))
            for ch in chains:
                qd_ref, kd_ref, cd_ref, w_ref, a_ref, do_ref = ins[ch["d"]]
                rows, cols = ch["rows"], ch["cols"]
                dob = do_ref[rows, cols].astype(BF16)
                ch["dvn"] = (_dot(a_ref[ch["c"], ch["h"]], dob, 0, 0)
                             + _dot(kd_ref[rows, cols], ch["dsn"].astype(BF16), 1, 0))
                ch["qdo"] = _dot(qd_ref[rows, cols], dob, 0, 0)
            for ch in chains:
                w_ref = ins[ch["d"]][3]
                ch["wdvn"] = _dot(w_ref[ch["rows"], ch["cols"]], ch["dvn"].astype(BF16), 0, 0)
            for ch in chains:
                dvn_ref, ds_ref = outs[ch["d"]]
                cd_ref = ins[ch["d"]][2]
                col = ch["d"] * GDN_H + ch["h"]
                dvn_ref[ch["rows"], ch["cols"]] = ch["dvn"].astype(BF16)
                ds_ref[ch["c"], ch["h"]] = ch["dsn"].astype(BF16)
                dstate[ch["d"] * GDN_H + ch["h"]] = (ch["qdo"] + cd_ref[ch["c"], col:col + 1, :] * ch["dsn"]
                                                     - ch["wdvn"])
            return carry

        lax.fori_loop(0, ncb, chunk, 0)

    ins, outs = [], []
    for d in range(2):
        tix = _dir_tile(d, n_tiles, True)
        im = lambda i, tix=tix: (tix(i), 0)
        im4 = lambda i, tix=tix: (tix(i), 0, 0, 0)
        _, w, a, _, qd, kd = loc[d]
        ins += [(qd, (ts, GDN_W), im), (kd, (ts, GDN_W), im), (loc[2], (ncb, 8, 128), lambda i, tix=tix: (tix(i), 0, 0)),
                (w, (ts, GDN_W), im), (a, (ncb, GDN_H, CHUNK, CHUNK), im4), (do, (ts, GDN_W), im)]
        outs += [(jax.ShapeDtypeStruct((S, GDN_W), BF16), (ts, GDN_W), im),
                 (jax.ShapeDtypeStruct((nch, GDN_H, GDN_DK, GDN_DK), BF16), (ncb, GDN_H, GDN_DK, GDN_DK), im4)]
    res = _rows("gdn_scan_bwd", S, ts, ins, outs, body, scratch=[pltpu.VMEM((2 * GDN_H, GDN_DK, GDN_DK), F32)])
    return res[0:2], res[2:4]


def _gdn_local_bwd(q, k, v, bg, gcr, do, loc, fwd, adj):
    S = q.shape[0]
    ts = _tile(S, GDN_TS)
    ncb = ts // CHUNK

    def body(q_ref, k_ref, v_ref, bg_ref, gcr_ref, do_ref, *rest):
        per_dir = (rest[0:5], rest[5:10])
        dq_ref, dk_ref, dv_ref, dbg_ref, dbgr_ref = rest[10:15]
        ri, ci = _tri_masks()
        lane = lax.broadcasted_iota(jnp.int32, (CHUNK, 128), 1)
        rowi = lax.broadcasted_iota(jnp.int32, (CHUNK, 1), 0)
        ones8 = jnp.ones((SUBLANES, CHUNK), F32)

        def chunk(c, carry):
            r0 = pl.multiple_of(c * CHUNK, CHUNK)
            rows = pl.ds(r0, CHUNK)
            chains = []
            for h in range(GDN_H):
                cols = slice(h * GDN_DK, (h + 1) * GDN_DK)
                qh, kh, vh = q_ref[rows, cols], k_ref[rows, cols], v_ref[rows, cols]
                dob = do_ref[rows, cols].astype(BF16)
                both = _bdot(jnp.concatenate([qh, kh], axis=0), kh, 1, 1)
                for d in range(2):
                    chains.append(dict(h=h, d=d, cols=cols, qh=qh, kh=kh, vh=vh, dob=dob, qk=both[0:CHUNK],
                                       kk=both[CHUNK:2 * CHUNK], col=d * GDN_H + h))
            for ch in chains:
                m = _gdn_decay(bg_ref, gcr_ref, c, rows, r0, ch["col"], ch["d"] == 1, ri, ci)
                t_ref, s_ref, ds_ref, vn_ref, dvn_ref = per_dir[ch["d"]]
                h, cols = ch["h"], ch["cols"]
                ch["m"] = m
                ch["kb"] = ch["kh"] * m["beta"]
                ch["kbg"] = ch["kb"] * m["eg"]
                ch["t"] = t_ref[c, h]
                stb = s_ref[c, h]
                ch["dsn"] = ds_ref[c, h]
                vnb = vn_ref[rows, cols]
                dvnb = dvn_ref[rows, cols]
                ch["dcd"] = jnp.sum(jnp.sum(stb.astype(F32) * ch["dsn"].astype(F32), axis=1, keepdims=True),
                                    axis=0, keepdims=True)
                ch["dqd"] = _dot(ch["dob"], stb, 1, 1)
                ch["d_a"] = _dot(ch["dob"], vnb, 1, 1)
                ch["dkd"] = _bdot(vnb, ch["dsn"], 1, 1)
                ch["dw"] = -_dot(dvnb, stb, 1, 1)
                ch["dvb"] = _dot(ch["t"], dvnb, 1, 0)
                ch["d_t"] = _bdot(dvnb, ch["vh"] * m["beta"], 1, 1)
            for ch in chains:
                dwb = ch["dw"].astype(BF16)
                ch["d_t"] = ch["d_t"] + _bdot(dwb, ch["kbg"], 1, 1)
                ch["dkbg"] = _dot(ch["t"], dwb, 1, 0)
                ch["nn"] = ch["d_a"] * ch["m"]["dm"]
                ch["nn_q"] = _bdot(ch["nn"], ch["qh"], 0, 0)
                ch["nn_k"] = _bdot(ch["nn"], ch["kh"], 1, 0)
            for ch in chains:
                ch["x"] = _dot(ch["d_t"].astype(BF16), ch["t"], 1, 0)
            for ch in chains:
                d_l = -_dot(ch["t"], ch["x"].astype(BF16), 1, 0)
                ch["d_l"] = jnp.where(ch["m"]["strict"], d_l, 0.0)
                ch["mm"] = ch["d_l"] * ch["m"]["dm"]
            for ch in chains:
                m = ch["m"]
                ch["mm_kh"] = _bdot(ch["mm"], ch["kh"], 1, 0)
                ch["mm_kb"] = _bdot(ch["mm"], ch["kb"], 0, 0)
                l_mat = jnp.where(m["strict"], m["beta"] * ch["kk"] * m["dm"], 0.0)
                ch["e"] = ch["d_l"] * l_mat + ch["nn"] * ch["qk"]
                dbgr_ref[c, ch["col"]:ch["col"] + 1, :] = -_dot(ones8, ch["e"], 1, 0, HI)[0:1, :]
            acc_bg = jnp.zeros((CHUNK, 128), F32)
            acc = {}
            for ch in chains:
                m = ch["m"]
                beta, eg, egl = m["beta"], m["eg"], m["egl"]
                dkb = ch["mm_kh"] + ch["dkbg"] * eg
                dk_d = ch["mm_kb"] + ch["nn_q"] + ch["dkd"] * egl + dkb * beta
                dq_d = ch["nn_k"] + ch["dqd"] * eg
                dv_d = ch["dvb"] * beta
                dkd_kd = ch["dkd"] * (ch["kh"] * egl)
                dgc = (jnp.sum(ch["e"], axis=1, keepdims=True)
                       + jnp.sum(ch["dqd"] * (ch["qh"] * eg) - dkd_kd + ch["dkbg"] * ch["kbg"], axis=1, keepdims=True))
                dgl = jnp.sum(jnp.sum(dkd_kd, axis=1, keepdims=True), axis=0, keepdims=True) + ch["dcd"] * m["cd"]
                dgc = dgc + jnp.where(rowi == (0 if ch["d"] == 1 else CHUNK - 1), dgl, 0.0)
                dbeta = jnp.sum(dkb * ch["kh"] + ch["dvb"] * ch["vh"], axis=1, keepdims=True)
                acc_bg = acc_bg + jnp.where(lane == ch["col"], dbeta, 0.0) + jnp.where(lane == 8 + ch["col"], dgc, 0.0)
                if ch["d"] == 0:
                    acc[ch["h"]] = (dq_d, dk_d, dv_d)
                else:
                    dq0, dk0, dv0 = acc[ch["h"]]
                    dq_ref[rows, ch["cols"]] = dq0 + dq_d
                    dk_ref[rows, ch["cols"]] = dk0 + dk_d
                    dv_ref[rows, ch["cols"]] = dv0 + dv_d
            dbg_ref[rows, :] = acc_bg
            return carry

        lax.fori_loop(0, ncb, chunk, 0)

    im = lambda i: (i, 0)
    im4 = lambda i: (i, 0, 0, 0)
    blk = (ts, GDN_W)
    ins = [(q, blk, im), (k, blk, im), (v, blk, im), (bg, (ts, 128), im), (gcr, (ncb, 8, CHUNK), lambda i: (i, 0, 0)),
           (do, blk, im)]
    for d in range(2):
        ins += [(loc[d][3], (ncb, GDN_H, CHUNK, CHUNK), im4), (fwd[d][2], (ncb, GDN_H, GDN_DK, GDN_DK), im4),
                (adj[d][1], (ncb, GDN_H, GDN_DK, GDN_DK), im4), (fwd[d][1], blk, im), (adj[d][0], blk, im)]
    sds = jax.ShapeDtypeStruct((S, GDN_W), F32)
    outs = [(sds, blk, im), (sds, blk, im), (sds, blk, im), (jax.ShapeDtypeStruct((S, 128), F32), (ts, 128), im),
            (jax.ShapeDtypeStruct((S // CHUNK, 8, CHUNK), F32), (ncb, 8, CHUNK), lambda i: (i, 0, 0))]
    dq, dk, dv, dbg, dbg_rows = _rows("gdn_local_bwd", S, ts, ins, outs, body)
    dgc_cols = dbg_rows.transpose(0, 2, 1).reshape(S, 8)
    return dq, dk, dv, dbg + jnp.pad(dgc_cols, ((0, 0), (8, 112)))


def _gdn_prep_bwd(dbg_all, p, prm):
    S = p.shape[0]
    ts = _tile(S, 512)

    def body(dbg_ref, p_ref, prm_ref, dba_ref, dprm_ref):
        i = pl.program_id(0)
        raw = p_ref[...].astype(F32)
        dbg = dbg_ref[...]
        lane = lax.broadcasted_iota(jnp.int32, (1, 128), 1)
        is_g = (lane >= 8) & (lane < 16)
        ea = jnp.exp(prm_ref[0:1, :])
        arg = raw + prm_ref[1:2, :]
        g = jnp.where(is_g, -ea * _softplus(arg), 0.0)
        beta = _sigmoid(raw)
        dgc = jnp.where(is_g, dbg, 0.0)
        ri, ci = _tri_masks()
        lower = (ri >= ci).astype(F32)
        upper = (ri <= ci).astype(F32)
        dgs = []
        for c in range(ts // CHUNK):
            ch = dgc[c * CHUNK:(c + 1) * CHUNK]
            dgs.append(jnp.where(lane < 12, _dot(upper, ch, 1, 0, HI), _dot(lower, ch, 1, 0, HI)))
        dg = jnp.concatenate(dgs, axis=0)
        dalpha = jnp.where(is_g, dg * (-ea) * _sigmoid(arg), 0.0)
        dba_ref[...] = jnp.where(lane < 8, dbg * beta * (1.0 - beta), dalpha).astype(BF16)
        rows = jnp.concatenate([jnp.sum(dg * g, axis=0, keepdims=True), jnp.sum(dalpha, axis=0, keepdims=True),
                                jnp.zeros((6, 128), F32)], axis=0)
        _colsum_into(dprm_ref, i, rows)

    im = lambda i: (i, 0)
    z0 = lambda i: (0, 0)
    return _rows("gdn_prep_bwd", S, ts,
                 [(dbg_all, (ts, 128), im), (p, (ts, 128), lambda i: (i, COL_BA // 128)), (prm, (8, 128), z0)],
                 [(jax.ShapeDtypeStruct((S, 128), BF16), (ts, 128), im), (jax.ShapeDtypeStruct((8, 128), F32), (8, 128), z0)],
                 body)


def _mm_plain(name, M, N, K, tm, tn, tk, a, am, b, bm, dtype):
    return _fused_mm(name, M, N, K, tm, tn, tk, [(a, am), (b, bm)], [(0, 1, 0)], [],
                     [(jax.ShapeDtypeStruct((M, N), dtype), (tm, tn), _mn)],
                     lambda i, accs, ex, out: out[0].__setitem__(Ellipsis, accs[0][...].astype(dtype)))[0]


def _layer_bwd(x0, W, R, emit_big=None, emit_small=None):
    S = x0.shape[0]
    tm = _tile(S, 512)
    tk_s = _tile(S, 1024)
    G = {}

    def emit(**named):
        if emit_big is None:
            G.update(named)
            return None
        return emit_big(**named)

    def ffn_emit(prefix):
        return lambda **kw: emit(**{f"{prefix}_w_{k}": v for k, v in kw.items()})

    dx2, G["ffn2_norm"] = _ffn_bwd("ffn2b", R["dx3"], R["x2"], W["ffn2_norm"], R["h3"], R["a2"], R["b2"], R["f2"],
                                   W["ffn2_w_gate"], W["ffn2_w_up"], W["ffn2_w_down"], ffn_emit("ffn2"))
    tok = emit(w_out=_mm_plain("dw_out", D_MODEL, D_MODEL, S, D_MODEL, D_MODEL, tk_s, R["y"], "km", dx2, "kn", BF16))
    gn = W["gdn_norm"] if tok is None else W["gdn_norm"] + tok
    dy = _mm_plain("dy_mix", S, D_MODEL, D_MODEL, tm, D_MODEL, D_MODEL, dx2, "mk", W["w_out"], "nk", F32)
    p = R["p"]
    dhr, dgate, do, dz, G["gdn_norm"] = _mix_out_bwd(dy, R["h_f"], R["h_b"], R["o_f"], R["o_b"], p, gn)
    lam_b, lam_f = _rg_scan_adj("rg_scan_bwd", R["a_b"], dhr, R["a_f"], dhr)
    dpre, dxc_direct, d_rgprm = _rg_gates_bwd(R["xc"], R["bd"], R["rg_prm"], lam_f, lam_b, R["h_f"], R["h_b"])
    tmg = _tile(S, 512)
    dxc = _fused_mm("rg_dxc", S, RG_W, 4 * RG_W, tmg, RG_W, 4 * RG_W, [(dpre, "mk"), (R["bd"], "nk")], [(0, 1, 0)],
                    [(dxc_direct, (tmg, RG_W), _mn)], [(jax.ShapeDtypeStruct((S, RG_W), F32), (tmg, RG_W), _mn)],
                    lambda i, accs, ex, out: out[0].__setitem__(Ellipsis, ex[0][...] + accs[0][...]))[0]
    d_bd = _mm_plain("rg_dbd", RG_W, 4 * RG_W, S, RG_W, 4 * RG_W, tk_s, R["xc"], "km", dpre, "kn", F32)
    dx_rg, G["rg_conv_w"], G["rg_conv_b"] = _conv_bwd("rg_conv_bwd", p, 0, W["rg_conv_w"], [dxc], "bias")
    blocks = jnp.einsum("nigmj,nm->gnij", d_bd.reshape(RG_BLOCKS, RG_BLOCK, 4, RG_BLOCKS, RG_BLOCK),
                        jnp.eye(RG_BLOCKS, dtype=F32))
    G["rg_gate_a_w"] = jnp.stack([blocks[0], blocks[2]])
    G["rg_gate_x_w"] = jnp.stack([blocks[1], blocks[3]])
    G["rg_gate_a_b"] = jnp.stack([d_rgprm[0], d_rgprm[2]])
    G["rg_gate_x_b"] = jnp.stack([d_rgprm[1], d_rgprm[3]])
    G["rg_lambda"] = d_rgprm[4:6]
    adj = _gdn_scan_bwd(R["gdn_loc"], do)
    dq, dk, dv, dbg = _gdn_local_bwd(R["q"], R["k"], R["v"], R["bg"], R["gcr"], do, R["gdn_loc"], R["gdn_fwd"], adj)
    cw = W["gdn_conv_w"]
    dpq, dwq, _ = _conv_bwd("gdn_conv_q_bwd", p, 2, cw[:, 0:512], [dq], "q")
    dpk, dwk, _ = _conv_bwd("gdn_conv_k_bwd", p, 3, cw[:, 512:1024], [dk], "k")
    dpv, dwv, _ = _conv_bwd("gdn_conv_v_bwd", p, 4, cw[:, 1024:1536], [dv], "v")
    G["gdn_conv_w"] = jnp.concatenate([dwq, dwk, dwv], axis=1)
    dba, d_gprm = _gdn_prep_bwd(dbg, p, R["gdn_prm"])
    G["gdn_a_log"] = d_gprm[0, 8:16].reshape(2, GDN_H)
    G["gdn_dt_bias"] = d_gprm[1, 8:16].reshape(2, GDN_H)
    dp = jnp.concatenate([dx_rg, dgate, dpq, dpk, dpv, dz, dba], axis=1)
    tok = emit(w_in=_mm_plain("dw_in", D_MODEL, D_IN_PAD, S, D_MODEL, 640, tk_s, R["h2"], "km", dp, "kn", BF16))
    g_mix = W["mix_norm"] if tok is None else W["mix_norm"] + tok

    def epi_dx1(i, accs, ex, out):
        dx, dgt = _rmsnorm_bwd_tile(accs[0][...], ex[0][...], ex[1][...])
        out[0][...] = ex[2][...] + dx
        _colsum_into(out[1], i, jnp.sum(dgt, axis=0, keepdims=True))

    dx1, G["mix_norm"] = _fused_mm(
        "mix_dx", S, D_MODEL, D_IN_PAD, tm, D_MODEL, D_IN_PAD, [(dp, "mk"), (W["w_in"], "nk")], [(0, 1, 0)],
        [(R["x1"], (tm, D_MODEL), _mn), (g_mix, (1, D_MODEL), _row0), (dx2, (tm, D_MODEL), _mn)],
        [(jax.ShapeDtypeStruct((S, D_MODEL), F32), (tm, D_MODEL), _mn),
         (jax.ShapeDtypeStruct((1, D_MODEL), F32), (1, D_MODEL), _row0)], epi_dx1)
    G["final_norm"] = R["d_final_norm"]
    if emit_small is not None:
        emit_small(G)
    dx0, G["ffn1_norm"] = _ffn_bwd("ffn1b", dx1, x0, W["ffn1_norm"], R["h1"], R["a1"], R["b1"], R["f1"],
                                   W["ffn1_w_gate"], W["ffn1_w_up"], W["ffn1_w_down"], ffn_emit("ffn1"))
    return dx0, G


def _mesh_pos():
    x, y, c = lax.axis_index("x"), lax.axis_index("y"), lax.axis_index("c")
    return x, y, c, 4 * x + 2 * y + c


def _peer(x, y, c, r):
    px = 1 - x if r & 4 else x
    py = 1 - y if r & 2 else y
    pc = 1 - c if r & 1 else c
    return (px, py, pc), 4 * px + 2 * py + pc


_HBM = pl.BlockSpec(memory_space=pltpu.HBM)
_SEM = pl.BlockSpec(memory_space=pltpu.SEMAPHORE)


def _peer_copies(scatter, srcs, lands, send_sems, recv_sems):
    x, y, c, me = _mesh_pos()
    copies = []
    for a, (src, land) in enumerate(zip(srcs, lands)):
        for r in range(1, N_DEV):
            peer, peer_idx = _peer(x, y, c, r)
            copies.append(pltpu.make_async_remote_copy(
                src_ref=src.at[peer_idx] if scatter else src, dst_ref=land.at[r - 1] if scatter else land.at[me],
                send_sem=send_sems.at[a * 7 + r - 1], recv_sem=recv_sems.at[a * 7 + r - 1],
                device_id=peer, device_id_type=pl.DeviceIdType.MESH))
    return copies


def _exchange_start(name, scatter, arrays):
    slabs = arrays
    n = len(slabs)

    def body(*refs):
        srcs, lands = refs[0:n], refs[n:2 * n]
        send_sems, recv_sems = refs[2 * n], refs[2 * n + 1]
        token = refs[4 * n + 2]
        for cp in _peer_copies(scatter, srcs, lands, send_sems, recv_sems):
            cp.start()
        token[...] = jnp.zeros_like(token)

    land_shapes = [(N_DEV - 1,) + s.shape[1:] if scatter else (N_DEV,) + s.shape for s in slabs]
    n_sems = 7 * n
    out_shape = ([pltpu.SemaphoreType.DMA((n_sems,)), pltpu.SemaphoreType.DMA((n_sems,))]
                 + [pltpu.HBM(s.shape, s.dtype) for s in slabs]
                 + [pltpu.HBM(shp, s.dtype) for shp, s in zip(land_shapes, slabs)]
                 + [jax.ShapeDtypeStruct((8, 128), F32)])
    res = pl.pallas_call(
        body, name=name, out_shape=out_shape, in_specs=[_HBM] * (2 * n),
        out_specs=[_SEM, _SEM] + [_HBM] * (2 * n) + [pl.BlockSpec(memory_space=pltpu.VMEM)],
        input_output_aliases={i: 2 + i for i in range(2 * n)},
        compiler_params=pltpu.CompilerParams(has_side_effects=pltpu.SideEffectType.DATAFLOW_SIDE_EFFECTING),
    )(*[pltpu.with_memory_space_constraint(s, pltpu.HBM) for s in slabs],
      *[pltpu.with_memory_space_constraint(lax.empty(shp, s.dtype), pltpu.HBM) for shp, s in zip(land_shapes, slabs)])
    return dict(n=n, scatter=scatter, sems=res[0:2], srcs=res[2:2 + n], lands=res[2 + n:2 + 2 * n],
                token=res[2 + 2 * n][0, 0])


def _exchange_wait(name, started, after):
    n = started["n"]
    scatter = started["scatter"]

    def body(*refs):
        srcs, lands = refs[0:n], refs[n:2 * n]
        send_sems, recv_sems = refs[2 * n], refs[2 * n + 1]
        for cp in _peer_copies(scatter, srcs, lands, send_sems, recv_sems):
            cp.wait_send()
            cp.wait_recv()

    arrays = list(started["srcs"]) + list(started["lands"])
    res = pl.pallas_call(
        body, name=name, out_shape=[pltpu.HBM(a.shape, a.dtype) for a in arrays],
        in_specs=[_HBM] * (2 * n) + [_SEM, _SEM, pl.BlockSpec(memory_space=pl.ANY)], out_specs=[_HBM] * (2 * n),
        input_output_aliases={i: i for i in range(2 * n)},
        compiler_params=pltpu.CompilerParams(has_side_effects=pltpu.SideEffectType.DATAFLOW_SIDE_EFFECTING),
    )(*arrays, *started["sems"], after)
    return res[0:n], res[n:2 * n]


def _all_gather(name, arrays):
    n = len(arrays)

    def body(*refs):
        ins = refs[:n]
        outs = refs[n:2 * n]
        token = refs[2 * n]
        send_sems, recv_sems, local_sems = refs[2 * n + 1:]
        token[...] = jnp.zeros_like(token)
        x, y, c, me = _mesh_pos()
        sibling = (x, y, 1 - c)
        chips = [(1 - x, y), (x, 1 - y), (1 - x, 1 - y)]

        def idx(px, py, pc):
            return 4 * px + 2 * py + pc

        def copy(a, k, block, to, src=None):
            slot = outs[a].at[idx(*block)]
            return pltpu.make_async_remote_copy(
                src_ref=slot if src is None else src, dst_ref=slot, send_sem=send_sems.at[a * 7 + k],
                recv_sem=recv_sems.at[a * 7 + k], device_id=to, device_id_type=pl.DeviceIdType.MESH)

        locals_, sends = [], []
        for a in range(n):
            loc = pltpu.make_async_copy(ins[a], outs[a].at[me], local_sems.at[a])
            loc.start()
            locals_.append(loc)
            sends.append(copy(a, 0, (x, y, c), sibling, src=ins[a]))
            sends += [copy(a, 1 + j, (x, y, c), (*chip, c), src=ins[a]) for j, chip in enumerate(chips)]
        for cp in sends:
            cp.start()
        passed = []
        for a in range(n):
            for j, chip in enumerate(chips):
                copy(a, 1 + j, (*chip, c), (x, y, c)).wait_recv()
                fwd = copy(a, 4 + j, (*chip, c), sibling)
                fwd.start()
                passed.append(fwd)
        for a in range(n):
            copy(a, 0, sibling, (x, y, c)).wait_recv()
            for j, chip in enumerate(chips):
                copy(a, 4 + j, (*chip, 1 - c), (x, y, c)).wait_recv()
        for cp in sends + passed:
            cp.wait_send()
        for loc in locals_:
            loc.wait()

    any_spec = pl.BlockSpec(memory_space=pl.ANY)
    res = pl.pallas_call(
        body, name=name, in_specs=[any_spec] * n, out_specs=[any_spec] * n + [pl.BlockSpec(memory_space=pltpu.VMEM)],
        out_shape=[jax.ShapeDtypeStruct((N_DEV,) + a.shape, a.dtype) for a in arrays]
        + [jax.ShapeDtypeStruct((8, 128), F32)],
        scratch_shapes=[pltpu.SemaphoreType.DMA((7 * n,)), pltpu.SemaphoreType.DMA((7 * n,)),
                        pltpu.SemaphoreType.DMA((n,))],
        compiler_params=pltpu.CompilerParams(has_side_effects=True),
    )(*arrays)
    return res[:n], res[n][0, 0]


def _adamw_math(w, g, m, v):
    m2 = ADAM_B1 * m + (1.0 - ADAM_B1) * g
    v2 = ADAM_B2 * v + (1.0 - ADAM_B2) * (g * g)
    m_hat = m2 / (1.0 - ADAM_B1 ** ADAM_STEP)
    v_hat = v2 / (1.0 - ADAM_B2 ** ADAM_STEP)
    delta = -ADAM_LR * (m_hat / (jnp.sqrt(v_hat) + ADAM_EPS) + ADAM_WD * w)
    return delta, m2, v2


def _adamw_slabs(name, src, land, me, w, m, v, tr):
    R, C = w.shape

    def body(me_ref, own_ref, land_ref, w_ref, m_ref, v_ref, g_ref, d_ref, m2_ref, v2_ref):
        g = own_ref[0].astype(F32)
        for s in range(N_DEV - 1):
            g = g + land_ref[s].astype(F32)
        delta, m2, v2 = _adamw_math(w_ref[...], g, m_ref[...], v_ref[...])
        g_ref[...] = g
        d_ref[...] = delta
        m2_ref[...] = m2
        v2_ref[...] = v2

    im = lambda i, me_ref: (i, 0)
    grid_spec = pltpu.PrefetchScalarGridSpec(
        num_scalar_prefetch=1, grid=(R // tr,),
        in_specs=[pl.BlockSpec((1, tr, C), lambda i, me_ref: (me_ref[0], i, 0)),
                  pl.BlockSpec((N_DEV - 1, tr, C), lambda i, me_ref: (0, i, 0)),
                  pl.BlockSpec((tr, C), im), pl.BlockSpec((tr, C), im), pl.BlockSpec((tr, C), im)],
        out_specs=[pl.BlockSpec((tr, C), im)] * 4)
    return pl.pallas_call(body, name=name, grid_spec=grid_spec, out_shape=[jax.ShapeDtypeStruct((R, C), F32)] * 4,
                          compiler_params=_cp(1))(me.reshape(1).astype(jnp.int32), src, land, w, m, v)


def _sum_slots(name, slots):
    _, R, C = slots.shape

    def body(s_ref, o_ref):
        g = s_ref[0]
        for s in range(1, N_DEV):
            g = g + s_ref[s]
        o_ref[...] = g

    return _rows(name, R, R, [(slots, (N_DEV, R, C), lambda i: (0, 0, 0))],
                 [(jax.ShapeDtypeStruct((R, C), F32), (R, C), lambda i: (0, 0))], body)[0]


def _adamw_packed(name, g, w, m, v):
    R, C = g.shape

    def body(g_ref, w_ref, m_ref, v_ref, d_ref, m2_ref, v2_ref):
        delta, m2, v2 = _adamw_math(w_ref[...], g_ref[...], m_ref[...], v_ref[...])
        d_ref[...] = delta
        m2_ref[...] = m2
        v2_ref[...] = v2

    im = lambda i: (0, 0)
    sds = jax.ShapeDtypeStruct((R, C), F32)
    return _rows(name, R, R, [(a, (R, C), im) for a in (g, w, m, v)], [(sds, (R, C), im)] * 3, body)


def _pack(arrays):
    rows = []
    for a in arrays:
        flat = a.reshape(-1).astype(F32)
        pad = (-flat.shape[0]) % 128
        rows.append(jnp.pad(flat, (0, pad)).reshape(-1, 128))
    out = jnp.concatenate(rows, axis=0)
    return jnp.pad(out, ((0, (-out.shape[0]) % 8), (0, 0)))


def _unpack(packed, shapes):
    lead = packed.shape[:-2]
    outs = []
    r = 0
    for shp in shapes:
        n = math.prod(shp)
        nr = -(-n // 128)
        flat = packed[..., r:r + nr, :].reshape(lead + (nr * 128,))[..., :n]
        outs.append(flat.reshape(lead + tuple(shp)))
        r += nr
    return outs


FFN1_BIG = ["ffn1_w_gate", "ffn1_w_up", "ffn1_w_down"]
MIX_BIG = ["w_in", "w_out"]
FFN2_BIG = ["ffn2_w_gate", "ffn2_w_up", "ffn2_w_down"]
BIG = FFN1_BIG + MIX_BIG + FFN2_BIG
COL_SHARDED = {"ffn1_w_gate", "ffn1_w_up", "w_in", "ffn2_w_gate", "ffn2_w_up"}
SMALL_SHARDED = ["rg_conv_w", "rg_gate_a_b", "rg_gate_x_b", "rg_lambda", "gdn_conv_w"]
WEIGHTS = ["ffn1_norm", "ffn1_w_gate", "ffn1_w_up", "ffn1_w_down", "mix_norm", "w_in", "w_out", "rg_conv_w", "rg_conv_b",
           "rg_gate_a_w", "rg_gate_a_b", "rg_gate_x_w", "rg_gate_x_b", "rg_lambda", "gdn_conv_w", "gdn_a_log",
           "gdn_dt_bias", "gdn_norm", "ffn2_norm", "ffn2_w_gate", "ffn2_w_up", "ffn2_w_down", "final_norm"]
SMALL = [n for n in WEIGHTS if n not in BIG]
ROW_VECTORS = {"ffn1_norm", "mix_norm", "ffn2_norm", "gdn_norm", "rg_conv_b", "final_norm"}
ROW_TILE = {"ffn1_w_gate": 256, "ffn1_w_up": 256, "ffn1_w_down": 176, "w_in": 256, "w_out": 64,
            "ffn2_w_gate": 256, "ffn2_w_up": 256, "ffn2_w_down": 176}


def _unshard_cols(g):
    return g.transpose(1, 0, 2).reshape(g.shape[1], N_DEV * g.shape[2])


def _to_slabs(name, g):
    if name in COL_SHARDED:
        r, ctot = g.shape
        return g.reshape(r, N_DEV, ctot // N_DEV).transpose(1, 0, 2)
    return g.reshape(N_DEV, g.shape[0] // N_DEV, g.shape[1])


def _step(x, target, w, m, v):
    _, _, _, me = _mesh_pos()
    def unshard(n, gth):
        full = _unshard_cols(gth) if n in COL_SHARDED else gth.reshape(-1, gth.shape[-1])
        return jnp.pad(full, ((0, 0), (0, D_IN_PAD - D_IN))) if n == "w_in" else full

    def landed(started, name, after):
        srcs, lands = _exchange_wait(name, started, after)
        def with_own(src, land):
            slot = lax.broadcasted_iota(jnp.int32, (N_DEV,) + (1,) * src.ndim, 0)
            return jnp.where(slot == me, src[None], land)

        return [with_own(src, land) for src, land in zip(srcs, lands)]

    up_names = ["ffn1_w_gate", "ffn1_w_up"]
    first, tok = _all_gather("gather_ffn1", [w[n].astype(BF16) for n in up_names])
    W = {n: unshard(n, gth) for n, gth in zip(up_names, first)}
    small_shards = [w[n] for n in SMALL_SHARDED]
    st_down = _exchange_start("gather_ffn1_down_start", False, [(w["ffn1_w_down"] + tok).astype(BF16)])
    st_mix = _exchange_start("gather_mix_start", False,
                             [(w[n] + tok).astype(BF16) for n in MIX_BIG] + [_pack(small_shards) + tok])
    st_ffn2 = _exchange_start("gather_ffn2_start", False, [(w[n] + tok).astype(BF16) for n in FFN2_BIG])
    for n in SMALL:
        if n not in SMALL_SHARDED:
            W[n] = w[n]
    W["ffn1_norm"] = w["ffn1_norm"] + (st_down["token"] + st_mix["token"] + st_ffn2["token"])

    def more(stage, after):
        if stage == "ffn1_down":
            return {"ffn1_w_down": unshard("ffn1_w_down", landed(st_down, "gather_ffn1_down_wait", after)[0])}
        if stage == "ffn2":
            return {n: unshard(n, gth) for n, gth in zip(FFN2_BIG, landed(st_ffn2, "gather_ffn2_wait", after))}
        got = landed(st_mix, "gather_mix_wait", after)
        new = {n: unshard(n, gth) for n, gth in zip(MIX_BIG, got)}
        for n, gth in zip(SMALL_SHARDED, _unpack(got[-1], [s.shape for s in small_shards])):
            new[n] = jnp.moveaxis(gth, 0, -2).reshape(gth.shape[1:-1] + (N_DEV * gth.shape[-1],))
        return new

    R = _layer_fwd(x, target, W, more)
    W = R["W"]
    pending = []

    def emit_big(**named):
        slabs = [_to_slabs(n, g[:, :D_IN] if n == "w_in" else g) for n, g in named.items()]
        started = _exchange_start(f"scatter_start_{len(pending)}", True, slabs)
        pending.append((list(named), started))
        return started["token"]

    small_started = []

    def emit_small(G):
        packed = _pack([G[n] for n in SMALL if n != "ffn1_norm"])
        small_started.append(_exchange_start("gather_small_start", False, [packed]))

    grad_x, G = _layer_bwd(x, W, R, emit_big, emit_small)
    loss = lax.psum(R["loss"][0, 0], ("x", "y", "c"))
    out = {}

    def finish(i, after):
        names, started = pending[i]
        srcs, lands = _exchange_wait(f"scatter_wait_{i}", started, after)
        for n, src, land in zip(names, srcs, lands):
            out[n] = _adamw_slabs(f"adamw_{n}", src, land, me, w[n], m[n], v[n], ROW_TILE[n])

    n_early = len(pending) - 2
    for i in range(n_early):
        finish(i, grad_x)
    early = [n for n in SMALL if n != "ffn1_norm"]
    srcs, lands = _exchange_wait("gather_small_wait", small_started[0], grad_x)
    slot = lax.broadcasted_iota(jnp.int32, (N_DEV, 1, 1), 0)
    slots = jnp.where(slot == me, srcs[0][None], lands[0])
    reduced = dict(zip(early, _unpack(_sum_slots("sum_small_grads", slots), [G[n].shape for n in early])))
    late = _all_gather("gather_ffn1_norm_grad", [_pack([G["ffn1_norm"]])])[0][0]
    reduced["ffn1_norm"] = _unpack(_sum_slots("sum_ffn1_norm_grad", late), [G["ffn1_norm"].shape])[0]
    g_small = []
    for n in SMALL:
        g = reduced[n]
        if n in SMALL_SHARDED:
            per = g.shape[-1] // N_DEV
            g = lax.dynamic_slice_in_dim(g, me * per, per, axis=g.ndim - 1)
        g_small.append(g.reshape(w[n].shape))
    shapes = [w[n].shape for n in SMALL]
    d_p, m_p, v_p = _adamw_packed("adamw_small", _pack(g_small), _pack([w[n] for n in SMALL]),
                                  _pack([m[n] for n in SMALL]), _pack([v[n] for n in SMALL]))
    for n, g, d_, m_, v_ in zip(SMALL, g_small, _unpack(d_p, shapes), _unpack(m_p, shapes), _unpack(v_p, shapes)):
        out[n] = (g, d_, m_, v_)
    for i in range(n_early, len(pending)):
        finish(i, d_p)
    return loss, grad_x, out


def kernel(x, ffn1_norm, ffn1_w_gate, ffn1_w_up, ffn1_w_down, mix_norm, w_in, w_out, rg_conv_w, rg_conv_b, rg_gate_a_w, rg_gate_a_b, rg_gate_x_w, rg_gate_x_b, rg_lambda, gdn_conv_w, gdn_a_log, gdn_dt_bias, gdn_norm, ffn2_norm, ffn2_w_gate, ffn2_w_up, ffn2_w_down, final_norm, loss_target, m_ffn1_norm, m_ffn1_w_gate, m_ffn1_w_up, m_ffn1_w_down, m_mix_norm, m_w_in, m_w_out, m_rg_conv_w, m_rg_conv_b, m_rg_gate_a_w, m_rg_gate_a_b, m_rg_gate_x_w, m_rg_gate_x_b, m_rg_lambda, m_gdn_conv_w, m_gdn_a_log, m_gdn_dt_bias, m_gdn_norm, m_ffn2_norm, m_ffn2_w_gate, m_ffn2_w_up, m_ffn2_w_down, m_final_norm, v_ffn1_norm, v_ffn1_w_gate, v_ffn1_w_up, v_ffn1_w_down, v_mix_norm, v_w_in, v_w_out, v_rg_conv_w, v_rg_conv_b, v_rg_gate_a_w, v_rg_gate_a_b, v_rg_gate_x_w, v_rg_gate_x_b, v_rg_lambda, v_gdn_conv_w, v_gdn_a_log, v_gdn_dt_bias, v_gdn_norm, v_ffn2_norm, v_ffn2_w_gate, v_ffn2_w_up, v_ffn2_w_down, v_final_norm):
    args = dict(locals())
    orig_shapes = {n: args[n].shape for n in WEIGHTS}

    def local(prefix):
        d = {}
        for n in WEIGHTS:
            a = args[prefix + n]
            d[n] = a.reshape(1, -1) if n in ROW_VECTORS else a[0]
        return d

    loss, grad_x, out = _step(x[0], loss_target[0], local(""), local("m_"), local("v_"))
    res = [loss, grad_x[None]]
    for k in range(4):
        res += [out[n][k].reshape(orig_shapes[n]) for n in WEIGHTS]
    return tuple(res)
```

```python
import functools
import math

import jax
import jax.numpy as jnp
from jax import lax
from jax.experimental import pallas as pl
from jax.experimental.pallas import tpu as pltpu

F32, BF16 = jnp.float32, jnp.bfloat16

D_MODEL = 1024
D_FF = 2816
RG_W = 512
RG_BLOCKS = 8
RG_BLOCK = 64
RG_C = 8.0
CONV_W = 4
GDN_H = 4
GDN_DK = 128
CHUNK = 64
EPS = 1e-6
D_IN = 3088
D_IN_PAD = 3200
COL_BA = 3072
N_DEV = 8
HALO = 16
VMEM_LIMIT = 48 * 1024 * 1024
VMEM_CAP = 60 * 1024 * 1024

ADAM_LR = 0.001
ADAM_B1 = 0.9
ADAM_B2 = 0.999
ADAM_EPS = 1e-08
ADAM_WD = 0.01
ADAM_STEP = 10

HI = lax.Precision.HIGHEST


def _cp(n, vmem_limit=None):
    return pltpu.CompilerParams(dimension_semantics=("arbitrary",) * n,
                                vmem_limit_bytes=VMEM_LIMIT if vmem_limit is None else vmem_limit)


def _matmul_vmem_limit(block_bytes, acc_bytes):
    need = 2 * block_bytes + 2 * acc_bytes
    return int(min(VMEM_CAP, max(VMEM_LIMIT, need * 4 // 3)))


def _tile(n, pref):
    return min(n, pref)


def _sigmoid(x):
    return 0.5 * jnp.tanh(0.5 * x) + 0.5


def _softplus(x):
    return jnp.maximum(x, 0.0) + jnp.log(1.0 + jnp.exp(-jnp.abs(x)))


def _dot(a, b, ca, cb, prec=None):
    return lax.dot_general(a, b, (((ca,), (cb,)), ((), ())), preferred_element_type=F32, precision=prec)


def _fused_mm(name, M, N, K, tm, tn, tk, ops, pairs, extras, outs, epilogue):
    nm, nn, nk = M // tm, N // tn, K // tk
    assert nm * tm == M and nn * tn == N and nk * tk == K, (name, M, N, K, tm, tn, tk)
    spec_of = {
        "mk": pl.BlockSpec((tm, tk), lambda i, j, k: (i, k)),
        "km": pl.BlockSpec((tk, tm), lambda i, j, k: (k, i)),
        "kn": pl.BlockSpec((tk, tn), lambda i, j, k: (k, j)),
        "nk": pl.BlockSpec((tn, tk), lambda i, j, k: (j, k)),
    }
    in_specs = [spec_of[m] for _, m in ops]
    in_specs += [pl.BlockSpec(bs, lambda i, j, k, im=im: im(i, j)) for _, bs, im in extras]
    out_specs = [pl.BlockSpec(bs, lambda i, j, k, im=im: im(i, j)) for _, bs, im in outs]
    n_ops, n_ex, n_out = len(ops), len(extras), len(outs)
    n_acc = 1 + max(g for _, _, g in pairs)
    modes = [m for _, m in ops]

    def body(*refs):
        op_refs = refs[:n_ops]
        ex_refs = refs[n_ops:n_ops + n_ex]
        out_refs = refs[n_ops + n_ex:n_ops + n_ex + n_out]
        accs = refs[n_ops + n_ex + n_out:]
        i = pl.program_id(0)
        k = pl.program_id(2)
        def dots():
            vals = [r[...].astype(BF16) for r in op_refs]
            for ia, ib, g in pairs:
                yield g, _dot(vals[ia], vals[ib], 1 if modes[ia] == "mk" else 0, 0 if modes[ib] == "kn" else 1)

        if nk == 1:
            sums = [None] * n_acc
            for g, d in dots():
                sums[g] = d if sums[g] is None else sums[g] + d
            epilogue(i, [_Held(s) for s in sums], ex_refs, out_refs)
            return

        @pl.when(k == 0)
        def _():
            for a in accs:
                a[...] = jnp.zeros_like(a)

        for g, d in dots():
            accs[g][...] += d

        @pl.when(k == nk - 1)
        def _():
            epilogue(i, accs, ex_refs, out_refs)

    op_block = {"mk": tm * tk, "km": tm * tk, "kn": tk * tn, "nk": tk * tn}
    block_bytes = sum(op_block[m] * a.dtype.itemsize for a, m in ops)
    block_bytes += sum(math.prod(bs) * jnp.dtype(a.dtype).itemsize for a, bs, _ in list(extras) + list(outs))
    res = pl.pallas_call(
        body, name=name, grid=(nm, nn, nk), in_specs=in_specs, out_specs=out_specs,
        out_shape=[o for o, _, _ in outs],
        scratch_shapes=[pltpu.VMEM((tm, tn), F32)] * (n_acc if nk > 1 else 0),
        compiler_params=_cp(3, _matmul_vmem_limit(block_bytes, n_acc * tm * tn * 4)),
    )(*[a for a, _ in ops], *[a for a, _, _ in extras])
    return res


class _Held:
    def __init__(self, value):
        self.value = value

    def __getitem__(self, idx):
        return self.value[idx]


def _mn(i, j):
    return (i, j)


def _row0(i, j):
    return (0, 0)


def _rows(name, S, ts, ins, outs, body, scratch=()):
    return pl.pallas_call(
        body, name=name, grid=(S // ts,),
        in_specs=[pl.BlockSpec(bs, im) for _, bs, im in ins],
        out_specs=[pl.BlockSpec(bs, im) for _, bs, im in outs],
        out_shape=[o for o, _, _ in outs],
        scratch_shapes=list(scratch),
        compiler_params=_cp(1),
    )(*[a for a, _, _ in ins])


def _halo_ins(arr, S, ts, width, colblk):
    per = ts // HALO
    last = S // HALO - 1
    return [
        (arr, (ts, width), lambda i: (i, colblk)),
        (arr, (HALO, width), lambda i: (jnp.maximum(i * per - 1, 0), colblk)),
        (arr, (HALO, width), lambda i: (jnp.minimum((i + 1) * per, last), colblk)),
    ]


def _ext(main_ref, prev_ref, next_ref, i, n_tiles):
    prev = jnp.where(i > 0, prev_ref[...].astype(F32), 0.0)
    nxt = jnp.where(i < n_tiles - 1, next_ref[...].astype(F32), 0.0)
    return jnp.concatenate([prev, main_ref[...].astype(F32), nxt], axis=0)


def _shift(ext, off, ts):
    n = ext.shape[0]
    if off == 0:
        return ext[HALO:HALO + ts]
    return pltpu.roll(ext, (-off) % n, 0)[HALO:HALO + ts]


def _rmsnorm_fwd(name, x, g):
    S, D = x.shape
    ts = _tile(S, 512)

    def body(x_ref, g_ref, o_ref):
        xv = x_ref[...]
        r = lax.rsqrt(jnp.mean(xv * xv, axis=-1, keepdims=True) + EPS)
        o_ref[...] = (xv * r * g_ref[...]).astype(BF16)

    return _rows(name, S, ts,
                 [(x, (ts, D), lambda i: (i, 0)), (g, (1, D), lambda i: (0, 0))],
                 [(jax.ShapeDtypeStruct((S, D), BF16), (ts, D), lambda i: (i, 0))], body)[0]


def _rmsnorm_bwd_tile(dh, x, g):
    r = lax.rsqrt(jnp.mean(x * x, axis=-1, keepdims=True) + EPS)
    xhat = x * r
    dxn = dh * g
    dx = r * (dxn - xhat * jnp.mean(dxn * xhat, axis=-1, keepdims=True))
    return dx, dh * xhat


def _ffn_fwd(tag, x, h, wg, wu, wd):
    S = x.shape[0]
    tm = _tile(S, 1024)
    tn = 1408

    def epi_up(i, accs, ex, out):
        a = accs[0][...]
        b = accs[1][...]
        s = _sigmoid(a)
        sa = a * s
        out[0][...] = sa.astype(BF16)
        out[1][...] = (b * (s * (1.0 + a * (1.0 - s)))).astype(BF16)
        out[2][...] = (sa * b).astype(BF16)

    sds = jax.ShapeDtypeStruct((S, D_FF), BF16)
    a, b, f = _fused_mm(f"{tag}_up", S, D_FF, D_MODEL, tm, tn, D_MODEL,
                        [(h, "mk"), (wg, "kn"), (wu, "kn")], [(0, 1, 0), (0, 2, 1)], [],
                        [(sds, (tm, tn), _mn)] * 3, epi_up)

    def epi_down(i, accs, ex, out):
        out[0][...] = ex[0][...] + 0.5 * accs[0][...]

    if callable(wd):
        wd = wd(f)
    xo = _fused_mm(f"{tag}_down", S, D_MODEL, D_FF, tm, D_MODEL, 1408,
                   [(f, "mk"), (wd, "kn")], [(0, 1, 0)], [(x, (tm, D_MODEL), _mn)],
                   [(jax.ShapeDtypeStruct((S, D_MODEL), F32), (tm, D_MODEL), _mn)], epi_down)[0]
    return xo, a, b, f


def _conv_taps(ext, w_ref, ts):
    acc = None
    for j in range(CONV_W):
        term = w_ref[j:j + 1, :] * _shift(ext, j - 2, ts)
        acc = term if acc is None else acc + term
    return acc


def _l2norm_heads(s, scale):
    outs = []
    for h in range(GDN_H):
        sh = s[:, h * GDN_DK:(h + 1) * GDN_DK]
        outs.append(sh * (lax.rsqrt(jnp.sum(sh * sh, axis=-1, keepdims=True) + EPS) * scale))
    return jnp.concatenate(outs, axis=-1)


def _conv_fwd(name, p, colblk, w, bias, mode):
    S = p.shape[0]
    ts = _tile(S, 512)
    n_tiles = S // ts
    C = w.shape[1]

    def body(main, prev, nxt, w_ref, b_ref, o_ref):
        i = pl.program_id(0)
        c = _conv_taps(_ext(main, prev, nxt, i, n_tiles), w_ref, ts)
        if mode == "bias":
            o_ref[...] = c + b_ref[...]
        else:
            s = c * _sigmoid(c)
            if mode == "q":
                s = _l2norm_heads(s, GDN_DK ** -0.5)
            elif mode == "k":
                s = _l2norm_heads(s, 1.0)
            o_ref[...] = s

    ins = _halo_ins(p, S, ts, C, colblk) + [(w, (CONV_W, C), lambda i: (0, 0)), (bias, (1, C), lambda i: (0, 0))]
    return _rows(name, S, ts, ins, [(jax.ShapeDtypeStruct((S, C), F32), (ts, C), lambda i: (i, 0))], body)[0]


def _rg_gate_terms(pre, xc, prm_ref, d):
    r = _sigmoid(pre[:, d * 1024:d * 1024 + RG_W] + prm_ref[2 * d:2 * d + 1, :])
    ig = _sigmoid(pre[:, d * 1024 + RG_W:(d + 1) * 1024] + prm_ref[2 * d + 1:2 * d + 2, :])
    sp = _softplus(-prm_ref[4 + d:5 + d, :])
    log_a = -RG_C * r * sp
    a = jnp.exp(log_a)
    t = jnp.tanh(log_a)
    sq = jnp.sqrt(-2.0 * t / (1.0 - t))
    return r, ig, sp, a, sq


def _rg_gates_fwd(xc, bd, prm):
    S = xc.shape[0]
    tm = _tile(S, 256)

    def epi(i, accs, ex, out):
        pre = accs[0][...]
        xv = ex[0][...]
        for d in range(2):
            r, ig, sp, a, sq = _rg_gate_terms(pre, xv, ex[1], d)
            out[2 * d][...] = a
            out[2 * d + 1][...] = sq * ig * xv

    sds = jax.ShapeDtypeStruct((S, RG_W), F32)
    blk = (tm, RG_W)
    im = lambda i, j: (i, 0)
    return _fused_mm("rg_gates_fwd", S, 4 * RG_W, RG_W, tm, 4 * RG_W, RG_W,
                     [(xc, "mk"), (bd, "kn")], [(0, 1, 0)],
                     [(xc, blk, im), (prm, (8, RG_W), _row0)], [(sds, blk, im)] * 4, epi)


SUBLANES = 8


def _scan_rows(a, b, reverse):
    rows = lax.broadcasted_iota(jnp.int32, a.shape, 0)
    s = 1
    while s < SUBLANES:
        shift = SUBLANES - s if reverse else s
        a_sh = pltpu.roll(a, shift, 0)
        b_sh = pltpu.roll(b, shift, 0)
        valid = (rows < SUBLANES - s) if reverse else (rows >= s)
        b = jnp.where(valid, a * b_sh + b, b)
        a = jnp.where(valid, a * a_sh, a)
        s *= 2
    return a, b


def _rg_scan(name, a_f, b_f, a_b, b_b):
    S, C = a_f.shape
    ts = _tile(S, 512)
    n_tiles = S // ts

    def body(af, bf, ab, bb, hf, hb, carry):
        @pl.when(pl.program_id(0) == 0)
        def _():
            carry[...] = jnp.zeros_like(carry)

        n_sub = ts // SUBLANES

        def step(j, c):
            cf, cb = c
            r0 = pl.multiple_of(j * SUBLANES, SUBLANES)
            cum_a, h0 = _scan_rows(af[pl.ds(r0, SUBLANES), :], bf[pl.ds(r0, SUBLANES), :], False)
            h = h0 + cum_a * cf
            hf[pl.ds(r0, SUBLANES), :] = h
            cf = h[SUBLANES - 1:SUBLANES, :]
            r1 = pl.multiple_of((n_sub - 1 - j) * SUBLANES, SUBLANES)
            cum_a, h0 = _scan_rows(ab[pl.ds(r1, SUBLANES), :], bb[pl.ds(r1, SUBLANES), :], True)
            h = h0 + cum_a * cb
            hb[pl.ds(r1, SUBLANES), :] = h
            cb = h[0:1, :]
            return cf, cb

        cf, cb = lax.fori_loop(0, n_sub, step, (carry[0:1, :], carry[1:2, :]), unroll=4)
        carry[0:1, :] = cf
        carry[1:2, :] = cb

    fw = lambda i: (i, 0)
    bw = lambda i: (n_tiles - 1 - i, 0)
    sds = jax.ShapeDtypeStruct((S, C), F32)
    return _rows(name, S, ts,
                 [(a_f, (ts, C), fw), (b_f, (ts, C), fw), (a_b, (ts, C), bw), (b_b, (ts, C), bw)],
                 [(sds, (ts, C), fw), (sds, (ts, C), bw)], body, scratch=[pltpu.VMEM((8, C), F32)])


def _tri_masks():
    ri = lax.broadcasted_iota(jnp.int32, (CHUNK, CHUNK), 0)
    ci = lax.broadcasted_iota(jnp.int32, (CHUNK, CHUNK), 1)
    return ri, ci


def _gdn_prep_fwd(p, prm):
    S = p.shape[0]
    ts = _tile(S, 512)

    def body(p_ref, prm_ref, o_ref):
        raw = p_ref[...].astype(F32)
        lane = lax.broadcasted_iota(jnp.int32, (1, 128), 1)
        g = -jnp.exp(prm_ref[0:1, :]) * _softplus(raw + prm_ref[1:2, :])
        g = jnp.where((lane >= 8) & (lane < 16), g, 0.0)
        beta = _sigmoid(raw)
        ri, ci = _tri_masks()
        lower = (ri >= ci).astype(F32)
        upper = (ri <= ci).astype(F32)
        for c in range(ts // CHUNK):
            rows = slice(c * CHUNK, (c + 1) * CHUNK)
            gch = g[rows]
            gc = jnp.where(lane < 12, _dot(lower, gch, 1, 0, HI), _dot(upper, gch, 1, 0, HI))
            o_ref[rows, :] = jnp.where(lane < 8, beta[rows], gc)

    return _rows("gdn_prep_fwd", S, ts,
                 [(p, (ts, 128), lambda i: (i, COL_BA // 128)), (prm, (8, 128), lambda i: (0, 0))],
                 [(jax.ShapeDtypeStruct((S, 128), F32), (ts, 128), lambda i: (i, 0))], body)[0]


def _bdot(a, b, ca, cb):
    return _dot(a.astype(BF16), b.astype(BF16), ca, cb)


GDN_W = GDN_H * GDN_DK
GDN_TS = 256
LOCAL_CHUNKS = 2

def _gdn_decay(bg_ref, gcr_ref, c, rows, r0, col, rev, ri, ci):
    beta = bg_ref[rows, col:col + 1]
    gc = bg_ref[rows, 8 + col:9 + col]
    last = 0 if rev else CHUNK - 1
    gl = bg_ref[pl.ds(r0 + last, 1), 8 + col:9 + col]
    out = dict(beta=beta, gc=gc, gl=gl, eg=jnp.exp(gc), egl=jnp.exp(gl - gc), cd=jnp.exp(gl))
    if gcr_ref is not None:
        incl = (ri <= ci) if rev else (ri >= ci)
        out["strict"] = (ri < ci) if rev else (ri > ci)
        out["dm"] = jnp.where(incl, jnp.exp(jnp.where(incl, gc - gcr_ref[c, col:col + 1, :], 0.0)), 0.0)
    return out


def _dir_tile(d, n_tiles, flip):
    if (d == 1) != flip:
        return lambda i: n_tiles - 1 - i
    return lambda i: i


def _gdn_local_fwd(q, k, v, bg, gcr):
    S = q.shape[0]
    ts = _tile(S, GDN_TS)
    ncb = ts // CHUNK
    nch = S // CHUNK

    def body(q_ref, k_ref, v_ref, bg_ref, gcr_ref, *out_refs):
        ri, ci = _tri_masks()
        eye = (ri == ci).astype(F32)
        outs = (out_refs[0:6], out_refs[6:12])
        cd_ref = out_refs[12]

        def chunk(cc, carry):
            chains = []
            for c in (LOCAL_CHUNKS * cc + j for j in range(LOCAL_CHUNKS)):
                r0 = pl.multiple_of(c * CHUNK, CHUNK)
                rows = pl.ds(r0, CHUNK)
                for h in range(GDN_H):
                    cols = slice(h * GDN_DK, (h + 1) * GDN_DK)
                    qh, kh, vh = q_ref[rows, cols], k_ref[rows, cols], v_ref[rows, cols]
                    both = _bdot(jnp.concatenate([qh, kh], axis=0), kh, 1, 1)
                    for d in range(2):
                        chains.append(dict(c=c, r0=r0, rows=rows, h=h, d=d, cols=cols, qh=qh, kh=kh, vh=vh,
                                           qk=both[0:CHUNK], kk=both[CHUNK:2 * CHUNK]))
            for ch in chains:
                m = _gdn_decay(bg_ref, gcr_ref, ch["c"], ch["rows"], ch["r0"], ch["d"] * GDN_H + ch["h"], ch["d"] == 1,
                               ri, ci)
                ch["m"] = m
                ch["x"] = -jnp.where(m["strict"], m["beta"] * ch["kk"] * m["dm"], 0.0)
                ch["t"] = eye + ch["x"]
            for ch in chains:
                ch["pw"] = _bdot(ch["x"], ch["x"], 1, 0)
            for level in range(1, 6):
                last_level = level == 5
                for ch in chains:
                    rhs = ch["t"] if last_level else jnp.concatenate([ch["t"], ch["pw"]], axis=1)
                    ch["prod"] = _bdot(ch["pw"], rhs, 1, 0)
                for ch in chains:
                    ch["t"] = ch["t"] + ch["prod"][:, 0:CHUNK]
                    if not last_level:
                        ch["pw"] = ch["prod"][:, CHUNK:2 * CHUNK]
            for ch in chains:
                m = ch["m"]
                rhs = jnp.concatenate([ch["vh"] * m["beta"], ch["kh"] * (m["beta"] * m["eg"])], axis=1)
                ch["uw"] = _bdot(ch["t"], rhs, 1, 0)
            for ch in chains:
                u_ref, w_ref, a_ref, t_ref, qd_ref, kd_ref = outs[ch["d"]]
                m = ch["m"]
                c, rows = ch["c"], ch["rows"]
                col = ch["d"] * GDN_H + ch["h"]
                u_ref[rows, ch["cols"]] = ch["uw"][:, 0:GDN_DK]
                w_ref[rows, ch["cols"]] = ch["uw"][:, GDN_DK:2 * GDN_DK].astype(BF16)
                a_ref[c, ch["h"]] = (ch["qk"] * m["dm"]).astype(BF16)
                t_ref[c, ch["h"]] = _bdot(ch["t"], eye, 0, 0).astype(BF16)
                qd_ref[rows, ch["cols"]] = (ch["qh"] * m["eg"]).astype(BF16)
                kd_ref[rows, ch["cols"]] = (ch["kh"] * m["egl"]).astype(BF16)
                cd_ref[c, col:col + 1, :] = jnp.broadcast_to(m["cd"], (1, 128))
            return carry

        lax.fori_loop(0, ncb // LOCAL_CHUNKS, chunk, 0)

    im = lambda i: (i, 0)
    im4 = lambda i: (i, 0, 0, 0)
    ins = [(q, (ts, GDN_W), im), (k, (ts, GDN_W), im), (v, (ts, GDN_W), im), (bg, (ts, 128), im),
           (gcr, (ncb, 8, CHUNK), lambda i: (i, 0, 0))]
    per_dir = [(jax.ShapeDtypeStruct((S, GDN_W), F32), (ts, GDN_W), im),
               (jax.ShapeDtypeStruct((S, GDN_W), BF16), (ts, GDN_W), im),
               (jax.ShapeDtypeStruct((nch, GDN_H, CHUNK, CHUNK), BF16), (ncb, GDN_H, CHUNK, CHUNK), im4),
               (jax.ShapeDtypeStruct((nch, GDN_H, CHUNK, CHUNK), BF16), (ncb, GDN_H, CHUNK, CHUNK), im4),
               (jax.ShapeDtypeStruct((S, GDN_W), BF16), (ts, GDN_W), im),
               (jax.ShapeDtypeStruct((S, GDN_W), BF16), (ts, GDN_W), im)]
    cd_out = (jax.ShapeDtypeStruct((nch, 8, 128), F32), (ncb, 8, 128), lambda i: (i, 0, 0))
    res = _rows("gdn_local_fwd", S, ts, ins, per_dir * 2 + [cd_out], body)
    return res[0:6], res[6:12], res[12]


def _gdn_scan_fwd(loc):
    S = loc[0][0].shape[0]
    ts = _tile(S, GDN_TS)
    n_tiles = S // ts
    ncb = ts // CHUNK
    nch = S // CHUNK

    def body(*refs):
        ins = (refs[0:6], refs[6:12])
        outs = (refs[12:15], refs[15:18])
        state = refs[18]

        @pl.when(pl.program_id(0) == 0)
        def _():
            state[...] = jnp.zeros_like(state)

        def chunk(cc, carry):
            chains = []
            for d in range(2):
                c = cc if d == 0 else ncb - 1 - cc
                rows = pl.ds(pl.multiple_of(c * CHUNK, CHUNK), CHUNK)
                for h in range(GDN_H):
                    cols = slice(h * GDN_DK, (h + 1) * GDN_DK)
                    chains.append(dict(d=d, h=h, c=c, rows=rows, cols=cols, st=state[d * GDN_H + h]))
            for ch in chains:
                qd_ref, kd_ref, u_ref, w_ref, a_ref, cd_ref = ins[ch["d"]]
                rows, cols = ch["rows"], ch["cols"]
                lhs = jnp.concatenate([w_ref[rows, cols], qd_ref[rows, cols]], axis=0)
                ch["ws_qs"] = _dot(lhs, ch["st"].astype(BF16), 1, 0)
            for ch in chains:
                qd_ref, kd_ref, u_ref, w_ref, a_ref, cd_ref = ins[ch["d"]]
                rows, cols = ch["rows"], ch["cols"]
                vn = u_ref[rows, cols] - ch["ws_qs"][0:CHUNK]
                vnb = vn.astype(BF16)
                ch["vn"] = vn
                ch["avn"] = _dot(a_ref[ch["c"], ch["h"]], vnb, 1, 0)
                ch["kvn"] = _dot(kd_ref[rows, cols], vnb, 0, 0)
            for ch in chains:
                o_ref, vn_ref, s_ref = outs[ch["d"]]
                cd_ref = ins[ch["d"]][5]
                rows, cols = ch["rows"], ch["cols"]
                col = ch["d"] * GDN_H + ch["h"]
                o_ref[rows, cols] = ch["ws_qs"][CHUNK:2 * CHUNK] + ch["avn"]
                vn_ref[rows, cols] = ch["vn"].astype(BF16)
                s_ref[ch["c"], ch["h"]] = ch["st"].astype(BF16)
                state[ch["d"] * GDN_H + ch["h"]] = ch["st"] * cd_ref[ch["c"], col:col + 1, :] + ch["kvn"]
            return carry

        lax.fori_loop(0, ncb, chunk, 0)

    ins, outs = [], []
    for d in range(2):
        tix = _dir_tile(d, n_tiles, False)
        im = lambda i, tix=tix: (tix(i), 0)
        im4 = lambda i, tix=tix: (tix(i), 0, 0, 0)
        u, w, a, _, qd, kd = loc[d]
        ins += [(qd, (ts, GDN_W), im), (kd, (ts, GDN_W), im), (u, (ts, GDN_W), im), (w, (ts, GDN_W), im),
                (a, (ncb, GDN_H, CHUNK, CHUNK), im4), (loc[2], (ncb, 8, 128), lambda i, tix=tix: (tix(i), 0, 0))]
        outs += [(jax.ShapeDtypeStruct((S, GDN_W), F32), (ts, GDN_W), im),
                 (jax.ShapeDtypeStruct((S, GDN_W), BF16), (ts, GDN_W), im),
                 (jax.ShapeDtypeStruct((nch, GDN_H, GDN_DK, GDN_DK), BF16), (ncb, GDN_H, GDN_DK, GDN_DK), im4)]
    res = _rows("gdn_scan_fwd", S, ts, ins, outs, body, scratch=[pltpu.VMEM((2 * GDN_H, GDN_DK, GDN_DK), F32)])
    return res[0:3], res[3:6]


def _gelu(x):
    c = math.sqrt(2.0 / math.pi)
    t = jnp.tanh(c * (x + 0.044715 * x * x * x))
    return 0.5 * x * (1.0 + t), t


def _mix_out_fwd(h_f, h_b, o_f, o_b, p, gn):
    S = h_f.shape[0]
    ts = _tile(S, 512)

    def body(hf, hb, of, ob, gate, z, gn_ref, y_ref):
        ge, _ = _gelu(gate[...].astype(F32))
        y_ref[:, 0:RG_W] = ((hf[...] + hb[...]) * ge).astype(BF16)
        o = of[...] + ob[...]
        zv = z[...].astype(F32)
        sz = zv * _sigmoid(zv)
        for h in range(GDN_H):
            cols = slice(h * GDN_DK, (h + 1) * GDN_DK)
            oh = o[:, cols]
            n = oh * lax.rsqrt(jnp.mean(oh * oh, axis=-1, keepdims=True) + EPS) * gn_ref[...]
            y_ref[:, RG_W + h * GDN_DK:RG_W + (h + 1) * GDN_DK] = (n * sz[:, cols]).astype(BF16)

    blk = (ts, RG_W)
    im = lambda i: (i, 0)
    ins = [(h_f, blk, im), (h_b, blk, im), (o_f, blk, im), (o_b, blk, im),
           (p, blk, lambda i: (i, 1)), (p, blk, lambda i: (i, 5)), (gn, (1, GDN_DK), lambda i: (0, 0))]
    return _rows("mix_out_fwd", S, ts, ins,
                 [(jax.ShapeDtypeStruct((S, D_MODEL), BF16), (ts, D_MODEL), im)], body)[0]


def _loss_head(x, target, g):
    S, D = x.shape
    ts = _tile(S, 512)

    def body(x_ref, t_ref, g_ref, dx_ref, loss_ref, dg_ref):
        @pl.when(pl.program_id(0) == 0)
        def _():
            loss_ref[...] = jnp.zeros_like(loss_ref)
            dg_ref[...] = jnp.zeros_like(dg_ref)

        xv = x_ref[...]
        gv = g_ref[...]
        r = lax.rsqrt(jnp.mean(xv * xv, axis=-1, keepdims=True) + EPS)
        err = xv * r * gv - t_ref[...]
        loss_ref[...] += jnp.sum(err * err) * (0.5 / D)
        dx, dgt = _rmsnorm_bwd_tile(err * (1.0 / D), xv, gv)
        dx_ref[...] = dx
        dg_ref[...] += jnp.sum(dgt, axis=0, keepdims=True)

    im = lambda i: (i, 0)
    z = lambda i: (0, 0)
    return _rows("loss_head", S, ts,
                 [(x, (ts, D), im), (target, (ts, D), im), (g, (1, D), z)],
                 [(jax.ShapeDtypeStruct((S, D), F32), (ts, D), im),
                  (jax.ShapeDtypeStruct((8, 128), F32), (8, 128), z),
                  (jax.ShapeDtypeStruct((1, D), F32), (1, D), z)], body)


def _block_diag(w):
    n = w.shape[0]
    return jnp.einsum("nij,nm->nimj", w, jnp.eye(n, dtype=w.dtype)).reshape(n * w.shape[1], n * w.shape[2])


def _rg_bd(a_w, x_w):
    return jnp.concatenate([_block_diag(a_w[0]), _block_diag(x_w[0]), _block_diag(a_w[1]), _block_diag(x_w[1])],
                           axis=1).astype(BF16)


def _rg_prm(ba, bx, lam):
    return jnp.concatenate([ba[0:1], bx[0:1], ba[1:2], bx[1:2], lam, jnp.zeros((2, RG_W), F32)], axis=0)


def _gdn_prm(a_log, dt_bias):
    rows = jnp.zeros((8, 128), F32)
    rows = rows.at[0, 8:16].set(a_log.reshape(-1))
    return rows.at[1, 8:16].set(dt_bias.reshape(-1))


def _gc_rows(bg):
    S = bg.shape[0]
    return bg[:, 8:16].reshape(S // CHUNK, CHUNK, 8).transpose(0, 2, 1)


def _layer_fwd(x0, target, W, more=None):
    S = x0.shape[0]
    R = {}
    R["h1"] = _rmsnorm_fwd("rms1", x0, W["ffn1_norm"])
    late_wd = {}

    def ffn1_wd(after):
        late_wd.update(more("ffn1_down", after))
        return late_wd["ffn1_w_down"]

    R["x1"], R["a1"], R["b1"], R["f1"] = _ffn_fwd("ffn1", x0, R["h1"], W["ffn1_w_gate"], W["ffn1_w_up"],
                                                  ffn1_wd if more is not None else W["ffn1_w_down"])
    if more is not None:
        W = {**W, **late_wd, **more("mixer", R["x1"])}
    R["h2"] = _rmsnorm_fwd("rms2", R["x1"], W["mix_norm"])
    tm = _tile(S, 512)
    tmp = _tile(S, 1024)
    R["p"] = _fused_mm("in_proj", S, D_IN_PAD, D_MODEL, tmp, 640, D_MODEL, [(R["h2"], "mk"), (W["w_in"], "kn")],
                       [(0, 1, 0)], [], [(jax.ShapeDtypeStruct((S, D_IN_PAD), BF16), (tmp, 640), _mn)],
                       lambda i, accs, ex, out: out[0].__setitem__(Ellipsis, accs[0][...].astype(BF16)))[0]
    p = R["p"]
    R["xc"] = _conv_fwd("rg_conv_fwd", p, 0, W["rg_conv_w"], W["rg_conv_b"], "bias")
    R["bd"] = _rg_bd(W["rg_gate_a_w"], W["rg_gate_x_w"])
    R["rg_prm"] = _rg_prm(W["rg_gate_a_b"], W["rg_gate_x_b"], W["rg_lambda"])
    a_f, b_f, a_b, b_b = _rg_gates_fwd(R["xc"], R["bd"], R["rg_prm"])
    R["a_f"], R["a_b"] = a_f, a_b
    R["h_f"], R["h_b"] = _rg_scan("rg_scan_fwd", a_f, b_f, a_b, b_b)
    zero_b = jnp.zeros((1, RG_W), F32)
    cw = W["gdn_conv_w"]
    R["q"] = _conv_fwd("gdn_conv_q", p, 2, cw[:, 0:512], zero_b, "q")
    R["k"] = _conv_fwd("gdn_conv_k", p, 3, cw[:, 512:1024], zero_b, "k")
    R["v"] = _conv_fwd("gdn_conv_v", p, 4, cw[:, 1024:1536], zero_b, "v")
    R["gdn_prm"] = _gdn_prm(W["gdn_a_log"], W["gdn_dt_bias"])
    R["bg"] = _gdn_prep_fwd(p, R["gdn_prm"])
    R["gcr"] = _gc_rows(R["bg"])
    R["gdn_loc"] = _gdn_local_fwd(R["q"], R["k"], R["v"], R["bg"], R["gcr"])
    R["gdn_fwd"] = _gdn_scan_fwd(R["gdn_loc"])
    R["o_f"], R["o_b"] = R["gdn_fwd"][0][0], R["gdn_fwd"][1][0]
    R["y"] = _mix_out_fwd(R["h_f"], R["h_b"], R["o_f"], R["o_b"], p, W["gdn_norm"])
    R["x2"] = _fused_mm("out_proj", S, D_MODEL, D_MODEL, tm, D_MODEL, D_MODEL, [(R["y"], "mk"), (W["w_out"], "kn")],
                        [(0, 1, 0)], [(R["x1"], (tm, D_MODEL), _mn)],
                        [(jax.ShapeDtypeStruct((S, D_MODEL), F32), (tm, D_MODEL), _mn)],
                        lambda i, accs, ex, out: out[0].__setitem__(Ellipsis, ex[0][...] + accs[0][...]))[0]
    if more is not None:
        W = {**W, **more("ffn2", R["x2"])}
    R["h3"] = _rmsnorm_fwd("rms3", R["x2"], W["ffn2_norm"])
    R["x3"], R["a2"], R["b2"], R["f2"] = _ffn_fwd("ffn2", R["x2"], R["h3"], W["ffn2_w_gate"], W["ffn2_w_up"], W["ffn2_w_down"])
    R["dx3"], R["loss"], R["d_final_norm"] = _loss_head(R["x3"], target, W["final_norm"])
    R["W"] = W
    return R


def _colsum_into(ref, i, val):
    @pl.when(i == 0)
    def _():
        ref[...] = val

    @pl.when(i > 0)
    def _():
        ref[...] += val


def _ffn_bwd(tag, dout, x, g, h, a, b, f, wg, wu, wd, emit):
    S = x.shape[0]
    tm = _tile(S, 512)
    tk_s = _tile(S, 1024)
    dwd = _fused_mm(f"{tag}_dw_down", D_FF, D_MODEL, S, 1408, D_MODEL, tk_s, [(f, "km"), (dout, "kn")], [(0, 1, 0)], [],
                    [(jax.ShapeDtypeStruct((D_FF, D_MODEL), BF16), (1408, D_MODEL), _mn)],
                    lambda i, accs, ex, out: out[0].__setitem__(Ellipsis, (0.5 * accs[0][...]).astype(BF16)))[0]
    emit(down=dwd)

    def epi_act(i, accs, ex, out):
        df = 0.5 * accs[0][...]
        out[0][...] = (df * ex[1][...].astype(F32)).astype(BF16)
        out[1][...] = (df * ex[0][...].astype(F32)).astype(BF16)

    sds = jax.ShapeDtypeStruct((S, D_FF), BF16)
    da, db = _fused_mm(f"{tag}_dact", S, D_FF, D_MODEL, tm, 1408, D_MODEL, [(dout, "mk"), (wd, "nk")], [(0, 1, 0)],
                       [(a, (tm, 1408), _mn), (b, (tm, 1408), _mn)], [(sds, (tm, 1408), _mn)] * 2, epi_act)

    def epi_w2(i, accs, ex, out):
        out[0][...] = accs[0][...].astype(BF16)
        out[1][...] = accs[1][...].astype(BF16)

    sdw = jax.ShapeDtypeStruct((D_MODEL, D_FF), BF16)
    dwg, dwu = _fused_mm(f"{tag}_dw_up", D_MODEL, D_FF, S, D_MODEL, 1408, tk_s,
                         [(h, "km"), (da, "kn"), (db, "kn")], [(0, 1, 0), (0, 2, 1)], [],
                         [(sdw, (D_MODEL, 1408), _mn)] * 2, epi_w2)
    tok = emit(gate=dwg, up=dwu)
    if tok is not None:
        g = g + tok

    def epi_dx(i, accs, ex, out):
        dx, dgt = _rmsnorm_bwd_tile(accs[0][...], ex[0][...], ex[1][...])
        out[0][...] = ex[2][...] + dx
        _colsum_into(out[1], i, jnp.sum(dgt, axis=0, keepdims=True))

    tmx = _tile(S, 1024)
    dx, dg = _fused_mm(f"{tag}_dx", S, D_MODEL, D_FF, tmx, D_MODEL, 1408,
                       [(da, "mk"), (wg, "nk"), (db, "mk"), (wu, "nk")], [(0, 1, 0), (2, 3, 0)],
                       [(x, (tmx, D_MODEL), _mn), (g, (1, D_MODEL), _row0), (dout, (tmx, D_MODEL), _mn)],
                       [(jax.ShapeDtypeStruct((S, D_MODEL), F32), (tmx, D_MODEL), _mn),
                        (jax.ShapeDtypeStruct((1, D_MODEL), F32), (1, D_MODEL), _row0)], epi_dx)
    return dx, dg


def _mix_out_bwd(dy, h_f, h_b, o_f, o_b, p, gn):
    S = dy.shape[0]
    ts = _tile(S, 512)
    c0 = math.sqrt(2.0 / math.pi)

    def body(dy_ref, hf, hb, of, ob, gate, z, gn_ref, dhr_ref, dgate_ref, do_ref, dz_ref, dgn_ref):
        i = pl.program_id(0)
        gv = gate[...].astype(F32)
        ge, t = _gelu(gv)
        dy_rg = dy_ref[:, 0:RG_W]
        dhr_ref[...] = dy_rg * ge
        dgelu = 0.5 * (1.0 + t) + 0.5 * gv * (1.0 - t * t) * c0 * (1.0 + 3.0 * 0.044715 * gv * gv)
        dgate_ref[...] = (dy_rg * (hf[...] + hb[...]) * dgelu).astype(BF16)
        o = of[...] + ob[...]
        zv = z[...].astype(F32)
        sig = _sigmoid(zv)
        gnv = gn_ref[...]
        dgn = jnp.zeros((1, GDN_DK), F32)
        for h in range(GDN_H):
            cols = slice(h * GDN_DK, (h + 1) * GDN_DK)
            oh = o[:, cols]
            r = lax.rsqrt(jnp.mean(oh * oh, axis=-1, keepdims=True) + EPS)
            ohat = oh * r
            dyh = dy_ref[:, RG_W + h * GDN_DK:RG_W + (h + 1) * GDN_DK]
            zh = zv[:, cols]
            sh = sig[:, cols]
            dn = dyh * zh * sh
            dz_ref[:, cols] = (dyh * ohat * gnv * (sh * (1.0 + zh * (1.0 - sh)))).astype(BF16)
            dxn = dn * gnv
            do_ref[:, cols] = r * (dxn - ohat * jnp.mean(dxn * ohat, axis=-1, keepdims=True))
            dgn = dgn + jnp.sum(dn * ohat, axis=0, keepdims=True)
        _colsum_into(dgn_ref, i, dgn)

    blk = (ts, RG_W)
    im = lambda i: (i, 0)
    z0 = lambda i: (0, 0)
    ins = [(dy, (ts, D_MODEL), im), (h_f, blk, im), (h_b, blk, im), (o_f, blk, im), (o_b, blk, im),
           (p, blk, lambda i: (i, 1)), (p, blk, lambda i: (i, 5)), (gn, (1, GDN_DK), z0)]
    outs = [(jax.ShapeDtypeStruct((S, RG_W), F32), blk, im), (jax.ShapeDtypeStruct((S, RG_W), BF16), blk, im),
            (jax.ShapeDtypeStruct((S, RG_W), F32), blk, im), (jax.ShapeDtypeStruct((S, RG_W), BF16), blk, im),
            (jax.ShapeDtypeStruct((1, GDN_DK), F32), (1, GDN_DK), z0)]
    return _rows("mix_out_bwd", S, ts, ins, outs, body)


def _rg_scan_adj(name, a_up, b_up, a_dn, b_dn):
    S, C = a_up.shape
    ts = _tile(S, 512)
    n_tiles = S // ts

    def body(au, bu, ad, bd, mu_ref, lam_ref, carry):
        @pl.when(pl.program_id(0) == 0)
        def _():
            carry[...] = jnp.zeros_like(carry)

        n_sub = ts // SUBLANES
        rows = lax.broadcasted_iota(jnp.int32, (SUBLANES, C), 0)

        def half(a_ref, b_ref, out_ref, r0, c_in, reverse):
            a = a_ref[pl.ds(r0, SUBLANES), :]
            b = b_ref[pl.ds(r0, SUBLANES), :]
            cum_a, c0 = _scan_rows(a, a * b, reverse)
            c = c0 + cum_a * c_in
            edge = 0 if not reverse else SUBLANES - 1
            c_prev = jnp.where(rows == edge, c_in, pltpu.roll(c, SUBLANES - 1 if reverse else 1, 0))
            out_ref[pl.ds(r0, SUBLANES), :] = b + c_prev
            return c[0:1, :] if reverse else c[SUBLANES - 1:SUBLANES, :]

        def step(j, c):
            cu, cd = c
            cu = half(au, bu, mu_ref, pl.multiple_of(j * SUBLANES, SUBLANES), cu, False)
            cd = half(ad, bd, lam_ref, pl.multiple_of((n_sub - 1 - j) * SUBLANES, SUBLANES), cd, True)
            return cu, cd

        cu, cd = lax.fori_loop(0, n_sub, step, (carry[0:1, :], carry[1:2, :]), unroll=4)
        carry[0:1, :] = cu
        carry[1:2, :] = cd

    fw = lambda i: (i, 0)
    bw = lambda i: (n_tiles - 1 - i, 0)
    sds = jax.ShapeDtypeStruct((S, C), F32)
    return _rows(name, S, ts,
                 [(a_up, (ts, C), fw), (b_up, (ts, C), fw), (a_dn, (ts, C), bw), (b_dn, (ts, C), bw)],
                 [(sds, (ts, C), fw), (sds, (ts, C), bw)], body, scratch=[pltpu.VMEM((8, C), F32)])


def _halo_ex(arr, S, tm, width):
    per = tm // HALO
    last = S // HALO - 1
    return [
        (arr, (tm, width), lambda i, j: (i, 0)),
        (arr, (HALO, width), lambda i, j: (jnp.maximum(i * per - 1, 0), 0)),
        (arr, (HALO, width), lambda i, j: (jnp.minimum((i + 1) * per, last), 0)),
    ]


def _rg_gates_bwd(xc, bd, prm, lam_f, lam_b, h_f, h_b):
    S = xc.shape[0]
    tm = _tile(S, 256)
    n_tiles = S // tm

    def epi(i, accs, ex, out):
        pre = accs[0][...]
        xv = ex[0][...]
        prm_ref = ex[1]
        lams = (ex[2][...], ex[3][...])
        hprev = (_shift(_ext(ex[4], ex[5], ex[6], i, n_tiles), -1, tm),
                 _shift(_ext(ex[7], ex[8], ex[9], i, n_tiles), 1, tm))
        dxc = jnp.zeros_like(xv)
        rows = []
        dlam_rows = []
        for d in range(2):
            r, ig, sp, a, sq = _rg_gate_terms(pre, xv, prm_ref, d)
            lam = lams[d]
            da = lam * hprev[d]
            di = lam * sq * xv
            dxc = dxc + lam * sq * ig
            dsq = lam * ig * xv
            dlog_a = da * a - dsq * (a * a) / sq
            dpre_r = dlog_a * (-RG_C * sp) * r * (1.0 - r)
            dpre_i = di * ig * (1.0 - ig)
            out[0][:, d * 1024:d * 1024 + RG_W] = dpre_r.astype(BF16)
            out[0][:, d * 1024 + RG_W:(d + 1) * 1024] = dpre_i.astype(BF16)
            rows += [jnp.sum(dpre_r, axis=0, keepdims=True), jnp.sum(dpre_i, axis=0, keepdims=True)]
            dsp = jnp.sum(dlog_a * (-RG_C * r), axis=0, keepdims=True)
            dlam_rows.append(-dsp * _sigmoid(-prm_ref[4 + d:5 + d, :]))
        out[1][...] = dxc
        zero = jnp.zeros((2, RG_W), F32)
        _colsum_into(out[2], i, jnp.concatenate(rows + dlam_rows + [zero], axis=0))

    blk = (tm, RG_W)
    im = lambda i, j: (i, 0)
    extras = ([(xc, blk, im), (prm, (8, RG_W), _row0), (lam_f, blk, im), (lam_b, blk, im)]
              + _halo_ex(h_f, S, tm, RG_W) + _halo_ex(h_b, S, tm, RG_W))
    outs = [(jax.ShapeDtypeStruct((S, 4 * RG_W), BF16), (tm, 4 * RG_W), im),
            (jax.ShapeDtypeStruct((S, RG_W), F32), blk, im),
            (jax.ShapeDtypeStruct((8, RG_W), F32), (8, RG_W), _row0)]
    return _fused_mm("rg_gates_bwd", S, 4 * RG_W, RG_W, tm, 4 * RG_W, RG_W, [(xc, "mk"), (bd, "kn")], [(0, 1, 0)],
                     extras, outs, epi)


def _roll_rows(ext, off):
    if off == 0:
        return ext
    return pltpu.roll(ext, (-off) % ext.shape[0], 0)


def _conv_bwd(name, p, colblk, w, grads, mode):
    S = p.shape[0]
    ts = _tile(S, 512)
    n_tiles = S // ts
    C = w.shape[1]
    ng = len(grads)

    def body(*refs):
        p_refs = refs[0:3]
        g_refs = refs[3:3 + 3 * ng]
        w_ref = refs[3 + 3 * ng]
        dx_ref, dw_ref, db_ref = refs[4 + 3 * ng:]
        i = pl.program_id(0)
        ext_p = _ext(*p_refs, i, n_tiles)
        dn = _ext(*g_refs[0:3], i, n_tiles)
        for gi in range(1, ng):
            dn = dn + _ext(*g_refs[3 * gi:3 * gi + 3], i, n_tiles)
        if mode == "bias":
            dc = dn
        else:
            c = None
            for j in range(CONV_W):
                term = w_ref[j:j + 1, :] * _roll_rows(ext_p, j - 2)
                c = term if c is None else c + term
            sig = _sigmoid(c)
            s = c * sig
            if mode in ("q", "k"):
                scale = GDN_DK ** -0.5 if mode == "q" else 1.0
                parts = []
                for h in range(GDN_H):
                    cols = slice(h * GDN_DK, (h + 1) * GDN_DK)
                    sh = s[:, cols]
                    dnh = dn[:, cols]
                    rinv = lax.rsqrt(jnp.sum(sh * sh, axis=-1, keepdims=True) + EPS)
                    parts.append(scale * rinv * (dnh - sh * (rinv * rinv) * jnp.sum(dnh * sh, axis=-1, keepdims=True)))
                ds = jnp.concatenate(parts, axis=-1)
            else:
                ds = dn
            dc = ds * (sig * (1.0 + c * (1.0 - sig)))
        dx = None
        for j in range(CONV_W):
            term = w_ref[j:j + 1, :] * _shift(dc, 2 - j, ts)
            dx = term if dx is None else dx + term
        dx_ref[...] = dx.astype(BF16)
        dc_main = dc[HALO:HALO + ts]
        dw = jnp.concatenate([jnp.sum(dc_main * _shift(ext_p, j - 2, ts), axis=0, keepdims=True)
                              for j in range(CONV_W)], axis=0)
        _colsum_into(dw_ref, i, dw)
        _colsum_into(db_ref, i, jnp.sum(dc_main, axis=0, keepdims=True))

    ins = _halo_ins(p, S, ts, C, colblk)
    for garr in grads:
        ins += _halo_ins(garr, S, ts, C, 0)
    ins += [(w, (CONV_W, C), lambda i: (0, 0))]
    z0 = lambda i: (0, 0)
    outs = [(jax.ShapeDtypeStruct((S, C), BF16), (ts, C), lambda i: (i, 0)),
            (jax.ShapeDtypeStruct((CONV_W, C), F32), (CONV_W, C), z0),
            (jax.ShapeDtypeStruct((1, C), F32), (1, C), z0)]
    return _rows(name, S, ts, ins, outs, body)


def _gdn_scan_bwd(loc, do):
    S = do.shape[0]
    ts = _tile(S, GDN_TS)
    n_tiles = S // ts
    ncb = ts // CHUNK
    nch = S // CHUNK

    def body(*refs):
        ins = (refs[0:6], refs[6:12])
        outs = (refs[12:14], refs[14:16])
        dstate = refs[16]

        @pl.when(pl.program_id(0) == 0)
        def _():
            dstate[...] = jnp.zeros_like(dstate)

        def chunk(cc, carry):
            chains = []
            for d in range(2):
                c = ncb - 1 - cc if d == 0 else cc
                rows = pl.ds(pl.multiple_of(c * CHUNK, CHUNK), CHUNK)
                for h in range(GDN_H):
                    cols = slice(h * GDN_DK, (h + 1) * GDN_DK)
                    chains.append(dict(d=d, h=h, c=c, rows=rows, cols=cols, dsn=dstate[d * GDN_H + h]))
            for ch in chains:
                qd_ref, kd_ref, cd_ref, w_ref, a_ref, do_ref = ins[ch["d"]]
                rows, cols = ch["rows"], ch["cols"]
                dob = do_ref[rows, cols].astype(BF16)
                ch["dvn"] = (_dot(a_ref[ch["c"], ch["h"]], dob, 0, 0)
                             + _dot(kd_ref[rows, cols], ch["dsn"].astype(BF16), 1, 0))
                ch["qdo"] = _dot(qd_ref[rows, cols], dob, 0, 0)
            for ch in chains:
                w_ref = ins[ch["d"]][3]
                ch["wdvn"] = _dot(w_ref[ch["rows"], ch["cols"]], ch["dvn"].astype(BF16), 0, 0)
            for ch in chains:
                dvn_ref, ds_ref = outs[ch["d"]]
                cd_ref = ins[ch["d"]][2]
                col = ch["d"] * GDN_H + ch["h"]
                dvn_ref[ch["rows"], ch["cols"]] = ch["dvn"].astype(BF16)
                ds_ref[ch["c"], ch["h"]] = ch["dsn"].astype(BF16)
                dstate[ch["d"] * GDN_H + ch["h"]] = (ch["qdo"] + cd_ref[ch["c"], col:col + 1, :] * ch["dsn"]
                                                     - ch["wdvn"])
            return carry

        lax.fori_loop(0, ncb, chunk, 0)

    ins, outs = [], []
    for d in range(2):
        tix = _dir_tile(d, n_tiles, True)
        im = lambda i, tix=tix: (tix(i), 0)
        im4 = lambda i, tix=tix: (tix(i), 0, 0, 0)
        _, w, a, _, qd, kd = loc[d]
        ins += [(qd, (ts, GDN_W), im), (kd, (ts, GDN_W), im), (loc[2], (ncb, 8, 128), lambda i, tix=tix: (tix(i), 0, 0)),
                (w, (ts, GDN_W), im), (a, (ncb, GDN_H, CHUNK, CHUNK), im4), (do, (ts, GDN_W), im)]
        outs += [(jax.ShapeDtypeStruct((S, GDN_W), BF16), (ts, GDN_W), im),
                 (jax.ShapeDtypeStruct((nch, GDN_H, GDN_DK, GDN_DK), BF16), (ncb, GDN_H, GDN_DK, GDN_DK), im4)]
    res = _rows("gdn_scan_bwd", S, ts, ins, outs, body, scratch=[pltpu.VMEM((2 * GDN_H, GDN_DK, GDN_DK), F32)])
    return res[0:2], res[2:4]


def _gdn_local_bwd(q, k, v, bg, gcr, do, loc, fwd, adj):
    S = q.shape[0]
    ts = _tile(S, GDN_TS)
    ncb = ts // CHUNK

    def body(q_ref, k_ref, v_ref, bg_ref, gcr_ref, do_ref, *rest):
        per_dir = (rest[0:5], rest[5:10])
        dq_ref, dk_ref, dv_ref, dbg_ref, dbgr_ref = rest[10:15]
        ri, ci = _tri_masks()
        lane = lax.broadcasted_iota(jnp.int32, (CHUNK, 128), 1)
        rowi = lax.broadcasted_iota(jnp.int32, (CHUNK, 1), 0)
        ones8 = jnp.ones((SUBLANES, CHUNK), F32)

        def chunk(c, carry):
            r0 = pl.multiple_of(c * CHUNK, CHUNK)
            rows = pl.ds(r0, CHUNK)
            chains = []
            for h in range(GDN_H):
                cols = slice(h * GDN_DK, (h + 1) * GDN_DK)
                qh, kh, vh = q_ref[rows, cols], k_ref[rows, cols], v_ref[rows, cols]
                dob = do_ref[rows, cols].astype(BF16)
                both = _bdot(jnp.concatenate([qh, kh], axis=0), kh, 1, 1)
                for d in range(2):
                    chains.append(dict(h=h, d=d, cols=cols, qh=qh, kh=kh, vh=vh, dob=dob, qk=both[0:CHUNK],
                                       kk=both[CHUNK:2 * CHUNK], col=d * GDN_H + h))
            for ch in chains:
                m = _gdn_decay(bg_ref, gcr_ref, c, rows, r0, ch["col"], ch["d"] == 1, ri, ci)
                t_ref, s_ref, ds_ref, vn_ref, dvn_ref = per_dir[ch["d"]]
                h, cols = ch["h"], ch["cols"]
                ch["m"] = m
                ch["kb"] = ch["kh"] * m["beta"]
                ch["kbg"] = ch["kb"] * m["eg"]
                ch["t"] = t_ref[c, h]
                stb = s_ref[c, h]
                ch["dsn"] = ds_ref[c, h]
                vnb = vn_ref[rows, cols]
                dvnb = dvn_ref[rows, cols]
                ch["dcd"] = jnp.sum(jnp.sum(stb.astype(F32) * ch["dsn"].astype(F32), axis=1, keepdims=True),
                                    axis=0, keepdims=True)
                ch["dqd"] = _dot(ch["dob"], stb, 1, 1)
                ch["d_a"] = _dot(ch["dob"], vnb, 1, 1)
                ch["dkd"] = _bdot(vnb, ch["dsn"], 1, 1)
                ch["dw"] = -_dot(dvnb, stb, 1, 1)
                ch["dvb"] = _dot(ch["t"], dvnb, 1, 0)
                ch["d_t"] = _bdot(dvnb, ch["vh"] * m["beta"], 1, 1)
            for ch in chains:
                dwb = ch["dw"].astype(BF16)
                ch["d_t"] = ch["d_t"] + _bdot(dwb, ch["kbg"], 1, 1)
                ch["dkbg"] = _dot(ch["t"], dwb, 1, 0)
                ch["nn"] = ch["d_a"] * ch["m"]["dm"]
                ch["nn_q"] = _bdot(ch["nn"], ch["qh"], 0, 0)
                ch["nn_k"] = _bdot(ch["nn"], ch["kh"], 1, 0)
            for ch in chains:
                ch["x"] = _dot(ch["d_t"].astype(BF16), ch["t"], 1, 0)
            for ch in chains:
                d_l = -_dot(ch["t"], ch["x"].astype(BF16), 1, 0)
                ch["d_l"] = jnp.where(ch["m"]["strict"], d_l, 0.0)
                ch["mm"] = ch["d_l"] * ch["m"]["dm"]
            for ch in chains:
                m = ch["m"]
                ch["mm_kh"] = _bdot(ch["mm"], ch["kh"], 1, 0)
                ch["mm_kb"] = _bdot(ch["mm"], ch["kb"], 0, 0)
                l_mat = jnp.where(m["strict"], m["beta"] * ch["kk"] * m["dm"], 0.0)
                ch["e"] = ch["d_l"] * l_mat + ch["nn"] * ch["qk"]
                dbgr_ref[c, ch["col"]:ch["col"] + 1, :] = -_dot(ones8, ch["e"], 1, 0, HI)[0:1, :]
            acc_bg = jnp.zeros((CHUNK, 128), F32)
            acc = {}
            for ch in chains:
                m = ch["m"]
                beta, eg, egl = m["beta"], m["eg"], m["egl"]
                dkb = ch["mm_kh"] + ch["dkbg"] * eg
                dk_d = ch["mm_kb"] + ch["nn_q"] + ch["dkd"] * egl + dkb * beta
                dq_d = ch["nn_k"] + ch["dqd"] * eg
                dv_d = ch["dvb"] * beta
                dkd_kd = ch["dkd"] * (ch["kh"] * egl)
                dgc = (jnp.sum(ch["e"], axis=1, keepdims=True)
                       + jnp.sum(ch["dqd"] * (ch["qh"] * eg) - dkd_kd + ch["dkbg"] * ch["kbg"], axis=1, keepdims=True))
                dgl = jnp.sum(jnp.sum(dkd_kd, axis=1, keepdims=True), axis=0, keepdims=True) + ch["dcd"] * m["cd"]
                dgc = dgc + jnp.where(rowi == (0 if ch["d"] == 1 else CHUNK - 1), dgl, 0.0)
                dbeta = jnp.sum(dkb * ch["kh"] + ch["dvb"] * ch["vh"], axis=1, keepdims=True)
                acc_bg = acc_bg + jnp.where(lane == ch["col"], dbeta, 0.0) + jnp.where(lane == 8 + ch["col"], dgc, 0.0)
                if ch["d"] == 0:
                    acc[ch["h"]] = (dq_d, dk_d, dv_d)
                else:
                    dq0, dk0, dv0 = acc[ch["h"]]
                    dq_ref[rows, ch["cols"]] = dq0 + dq_d
                    dk_ref[rows, ch["cols"]] = dk0 + dk_d
                    dv_ref[rows, ch["cols"]] = dv0 + dv_d
            dbg_ref[rows, :] = acc_bg
            return carry

        lax.fori_loop(0, ncb, chunk, 0)

    im = lambda i: (i, 0)
    im4 = lambda i: (i, 0, 0, 0)
    blk = (ts, GDN_W)
    ins = [(q, blk, im), (k, blk, im), (v, blk, im), (bg, (ts, 128), im), (gcr, (ncb, 8, CHUNK), lambda i: (i, 0, 0)),
           (do, blk, im)]
    for d in range(2):
        ins += [(loc[d][3], (ncb, GDN_H, CHUNK, CHUNK), im4), (fwd[d][2], (ncb, GDN_H, GDN_DK, GDN_DK), im4),
                (adj[d][1], (ncb, GDN_H, GDN_DK, GDN_DK), im4), (fwd[d][1], blk, im), (adj[d][0], blk, im)]
    sds = jax.ShapeDtypeStruct((S, GDN_W), F32)
    outs = [(sds, blk, im), (sds, blk, im), (sds, blk, im), (jax.ShapeDtypeStruct((S, 128), F32), (ts, 128), im),
            (jax.ShapeDtypeStruct((S // CHUNK, 8, CHUNK), F32), (ncb, 8, CHUNK), lambda i: (i, 0, 0))]
    dq, dk, dv, dbg, dbg_rows = _rows("gdn_local_bwd", S, ts, ins, outs, body)
    dgc_cols = dbg_rows.transpose(0, 2, 1).reshape(S, 8)
    return dq, dk, dv, dbg + jnp.pad(dgc_cols, ((0, 0), (8, 112)))


def _gdn_prep_bwd(dbg_all, p, prm):
    S = p.shape[0]
    ts = _tile(S, 512)

    def body(dbg_ref, p_ref, prm_ref, dba_ref, dprm_ref):
        i = pl.program_id(0)
        raw = p_ref[...].astype(F32)
        dbg = dbg_ref[...]
        lane = lax.broadcasted_iota(jnp.int32, (1, 128), 1)
        is_g = (lane >= 8) & (lane < 16)
        ea = jnp.exp(prm_ref[0:1, :])
        arg = raw + prm_ref[1:2, :]
        g = jnp.where(is_g, -ea * _softplus(arg), 0.0)
        beta = _sigmoid(raw)
        dgc = jnp.where(is_g, dbg, 0.0)
        ri, ci = _tri_masks()
        lower = (ri >= ci).astype(F32)
        upper = (ri <= ci).astype(F32)
        dgs = []
        for c in range(ts // CHUNK):
            ch = dgc[c * CHUNK:(c + 1) * CHUNK]
            dgs.append(jnp.where(lane < 12, _dot(upper, ch, 1, 0, HI), _dot(lower, ch, 1, 0, HI)))
        dg = jnp.concatenate(dgs, axis=0)
        dalpha = jnp.where(is_g, dg * (-ea) * _sigmoid(arg), 0.0)
        dba_ref[...] = jnp.where(lane < 8, dbg * beta * (1.0 - beta), dalpha).astype(BF16)
        rows = jnp.concatenate([jnp.sum(dg * g, axis=0, keepdims=True), jnp.sum(dalpha, axis=0, keepdims=True),
                                jnp.zeros((6, 128), F32)], axis=0)
        _colsum_into(dprm_ref, i, rows)

    im = lambda i: (i, 0)
    z0 = lambda i: (0, 0)
    return _rows("gdn_prep_bwd", S, ts,
                 [(dbg_all, (ts, 128), im), (p, (ts, 128), lambda i: (i, COL_BA // 128)), (prm, (8, 128), z0)],
                 [(jax.ShapeDtypeStruct((S, 128), BF16), (ts, 128), im), (jax.ShapeDtypeStruct((8, 128), F32), (8, 128), z0)],
                 body)


def _mm_plain(name, M, N, K, tm, tn, tk, a, am, b, bm, dtype):
    return _fused_mm(name, M, N, K, tm, tn, tk, [(a, am), (b, bm)], [(0, 1, 0)], [],
                     [(jax.ShapeDtypeStruct((M, N), dtype), (tm, tn), _mn)],
                     lambda i, accs, ex, out: out[0].__setitem__(Ellipsis, accs[0][...].astype(dtype)))[0]


def _layer_bwd(x0, W, R, emit_big=None, emit_small=None):
    S = x0.shape[0]
    tm = _tile(S, 512)
    tk_s = _tile(S, 1024)
    G = {}

    def emit(**named):
        if emit_big is None:
            G.update(named)
            return None
        return emit_big(**named)

    def ffn_emit(prefix):
        return lambda **kw: emit(**{f"{prefix}_w_{k}": v for k, v in kw.items()})

    dx2, G["ffn2_norm"] = _ffn_bwd("ffn2b", R["dx3"], R["x2"], W["ffn2_norm"], R["h3"], R["a2"], R["b2"], R["f2"],
                                   W["ffn2_w_gate"], W["ffn2_w_up"], W["ffn2_w_down"], ffn_emit("ffn2"))
    tok = emit(w_out=_mm_plain("dw_out", D_MODEL, D_MODEL, S, D_MODEL, D_MODEL, tk_s, R["y"], "km", dx2, "kn", BF16))
    gn = W["gdn_norm"] if tok is None else W["gdn_norm"] + tok
    dy = _mm_plain("dy_mix", S, D_MODEL, D_MODEL, tm, D_MODEL, D_MODEL, dx2, "mk", W["w_out"], "nk", F32)
    p = R["p"]
    dhr, dgate, do, dz, G["gdn_norm"] = _mix_out_bwd(dy, R["h_f"], R["h_b"], R["o_f"], R["o_b"], p, gn)
    lam_b, lam_f = _rg_scan_adj("rg_scan_bwd", R["a_b"], dhr, R["a_f"], dhr)
    dpre, dxc_direct, d_rgprm = _rg_gates_bwd(R["xc"], R["bd"], R["rg_prm"], lam_f, lam_b, R["h_f"], R["h_b"])
    tmg = _tile(S, 512)
    dxc = _fused_mm("rg_dxc", S, RG_W, 4 * RG_W, tmg, RG_W, 4 * RG_W, [(dpre, "mk"), (R["bd"], "nk")], [(0, 1, 0)],
                    [(dxc_direct, (tmg, RG_W), _mn)], [(jax.ShapeDtypeStruct((S, RG_W), F32), (tmg, RG_W), _mn)],
                    lambda i, accs, ex, out: out[0].__setitem__(Ellipsis, ex[0][...] + accs[0][...]))[0]
    d_bd = _mm_plain("rg_dbd", RG_W, 4 * RG_W, S, RG_W, 4 * RG_W, tk_s, R["xc"], "km", dpre, "kn", F32)
    dx_rg, G["rg_conv_w"], G["rg_conv_b"] = _conv_bwd("rg_conv_bwd", p, 0, W["rg_conv_w"], [dxc], "bias")
    blocks = jnp.einsum("nigmj,nm->gnij", d_bd.reshape(RG_BLOCKS, RG_BLOCK, 4, RG_BLOCKS, RG_BLOCK),
                        jnp.eye(RG_BLOCKS, dtype=F32))
    G["rg_gate_a_w"] = jnp.stack([blocks[0], blocks[2]])
    G["rg_gate_x_w"] = jnp.stack([blocks[1], blocks[3]])
    G["rg_gate_a_b"] = jnp.stack([d_rgprm[0], d_rgprm[2]])
    G["rg_gate_x_b"] = jnp.stack([d_rgprm[1], d_rgprm[3]])
    G["rg_lambda"] = d_rgprm[4:6]
    adj = _gdn_scan_bwd(R["gdn_loc"], do)
    dq, dk, dv, dbg = _gdn_local_bwd(R["q"], R["k"], R["v"], R["bg"], R["gcr"], do, R["gdn_loc"], R["gdn_fwd"], adj)
    cw = W["gdn_conv_w"]
    dpq, dwq, _ = _conv_bwd("gdn_conv_q_bwd", p, 2, cw[:, 0:512], [dq], "q")
    dpk, dwk, _ = _conv_bwd("gdn_conv_k_bwd", p, 3, cw[:, 512:1024], [dk], "k")
    dpv, dwv, _ = _conv_bwd("gdn_conv_v_bwd", p, 4, cw[:, 1024:1536], [dv], "v")
    G["gdn_conv_w"] = jnp.concatenate([dwq, dwk, dwv], axis=1)
    dba, d_gprm = _gdn_prep_bwd(dbg, p, R["gdn_prm"])
    G["gdn_a_log"] = d_gprm[0, 8:16].reshape(2, GDN_H)
    G["gdn_dt_bias"] = d_gprm[1, 8:16].reshape(2, GDN_H)
    dp = jnp.concatenate([dx_rg, dgate, dpq, dpk, dpv, dz, dba], axis=1)
    tok = emit(w_in=_mm_plain("dw_in", D_MODEL, D_IN_PAD, S, D_MODEL, 640, tk_s, R["h2"], "km", dp, "kn", BF16))
    g_mix = W["mix_norm"] if tok is None else W["mix_norm"] + tok

    def epi_dx1(i, accs, ex, out):
        dx, dgt = _rmsnorm_bwd_tile(accs[0][...], ex[0][...], ex[1][...])
        out[0][...] = ex[2][...] + dx
        _colsum_into(out[1], i, jnp.sum(dgt, axis=0, keepdims=True))

    dx1, G["mix_norm"] = _fused_mm(
        "mix_dx", S, D_MODEL, D_IN_PAD, tm, D_MODEL, D_IN_PAD, [(dp, "mk"), (W["w_in"], "nk")], [(0, 1, 0)],
        [(R["x1"], (tm, D_MODEL), _mn), (g_mix, (1, D_MODEL), _row0), (dx2, (tm, D_MODEL), _mn)],
        [(jax.ShapeDtypeStruct((S, D_MODEL), F32), (tm, D_MODEL), _mn),
         (jax.ShapeDtypeStruct((1, D_MODEL), F32), (1, D_MODEL), _row0)], epi_dx1)
    G["final_norm"] = R["d_final_norm"]
    if emit_small is not None:
        emit_small(G)
    dx0, G["ffn1_norm"] = _ffn_bwd("ffn1b", dx1, x0, W["ffn1_norm"], R["h1"], R["a1"], R["b1"], R["f1"],
                                   W["ffn1_w_gate"], W["ffn1_w_up"], W["ffn1_w_down"], ffn_emit("ffn1"))
    return dx0, G


def _mesh_pos():
    x, y, c = lax.axis_index("x"), lax.axis_index("y"), lax.axis_index("c")
    return x, y, c, 4 * x + 2 * y + c


def _peer(x, y, c, r):
    px = 1 - x if r & 4 else x
    py = 1 - y if r & 2 else y
    pc = 1 - c if r & 1 else c
    return (px, py, pc), 4 * px + 2 * py + pc


_HBM = pl.BlockSpec(memory_space=pltpu.HBM)
_SEM = pl.BlockSpec(memory_space=pltpu.SEMAPHORE)


def _peer_copies(scatter, srcs, lands, send_sems, recv_sems):
    x, y, c, me = _mesh_pos()
    copies = []
    for a, (src, land) in enumerate(zip(srcs, lands)):
        for r in range(1, N_DEV):
            peer, peer_idx = _peer(x, y, c, r)
            copies.append(pltpu.make_async_remote_copy(
                src_ref=src.at[peer_idx] if scatter else src, dst_ref=land.at[r - 1] if scatter else land.at[me],
                send_sem=send_sems.at[a * 7 + r - 1], recv_sem=recv_sems.at[a * 7 + r - 1],
                device_id=peer, device_id_type=pl.DeviceIdType.MESH))
    return copies


def _exchange_start(name, scatter, arrays):
    slabs = arrays
    n = len(slabs)

    def body(*refs):
        srcs, lands = refs[0:n], refs[n:2 * n]
        send_sems, recv_sems = refs[2 * n], refs[2 * n + 1]
        token = refs[4 * n + 2]
        for cp in _peer_copies(scatter, srcs, lands, send_sems, recv_sems):
            cp.start()
        token[...] = jnp.zeros_like(token)

    land_shapes = [(N_DEV - 1,) + s.shape[1:] if scatter else (N_DEV,) + s.shape for s in slabs]
    n_sems = 7 * n
    out_shape = ([pltpu.SemaphoreType.DMA((n_sems,)), pltpu.SemaphoreType.DMA((n_sems,))]
                 + [pltpu.HBM(s.shape, s.dtype) for s in slabs]
                 + [pltpu.HBM(shp, s.dtype) for shp, s in zip(land_shapes, slabs)]
                 + [jax.ShapeDtypeStruct((8, 128), F32)])
    res = pl.pallas_call(
        body, name=name, out_shape=out_shape, in_specs=[_HBM] * (2 * n),
        out_specs=[_SEM, _SEM] + [_HBM] * (2 * n) + [pl.BlockSpec(memory_space=pltpu.VMEM)],
        input_output_aliases={i: 2 + i for i in range(2 * n)},
        compiler_params=pltpu.CompilerParams(has_side_effects=pltpu.SideEffectType.DATAFLOW_SIDE_EFFECTING),
    )(*[pltpu.with_memory_space_constraint(s, pltpu.HBM) for s in slabs],
      *[pltpu.with_memory_space_constraint(lax.empty(shp, s.dtype), pltpu.HBM) for shp, s in zip(land_shapes, slabs)])
    return dict(n=n, scatter=scatter, sems=res[0:2], srcs=res[2:2 + n], lands=res[2 + n:2 + 2 * n],
                token=res[2 + 2 * n][0, 0])


def _exchange_wait(name, started, after):
    n = started["n"]
    scatter = started["scatter"]

    def body(*refs):
        srcs, lands = refs[0:n], refs[n:2 * n]
        send_sems, recv_sems = refs[2 * n], refs[2 * n + 1]
        for cp in _peer_copies(scatter, srcs, lands, send_sems, recv_sems):
            cp.wait_send()
            cp.wait_recv()

    arrays = list(started["srcs"]) + list(started["lands"])
    res = pl.pallas_call(
        body, name=name, out_shape=[pltpu.HBM(a.shape, a.dtype) for a in arrays],
        in_specs=[_HBM] * (2 * n) + [_SEM, _SEM, pl.BlockSpec(memory_space=pl.ANY)], out_specs=[_HBM] * (2 * n),
        input_output_aliases={i: i for i in range(2 * n)},
        compiler_params=pltpu.CompilerParams(has_side_effects=pltpu.SideEffectType.DATAFLOW_SIDE_EFFECTING),
    )(*arrays, *started["sems"], after)
    return res[0:n], res[n:2 * n]


def _all_gather(name, arrays):
    n = len(arrays)

    def body(*refs):
        ins = refs[:n]
        outs = refs[n:2 * n]
        token = refs[2 * n]
        send_sems, recv_sems, local_sems = refs[2 * n + 1:]
        token[...] = jnp.zeros_like(token)
        x, y, c, me = _mesh_pos()
        sibling = (x, y, 1 - c)
        chips = [(1 - x, y), (x, 1 - y), (1 - x, 1 - y)]

        def idx(px, py, pc):
            return 4 * px + 2 * py + pc

        def copy(a, k, block, to, src=None):
            slot = outs[a].at[idx(*block)]
            return pltpu.make_async_remote_copy(
                src_ref=slot if src is None else src, dst_ref=slot, send_sem=send_sems.at[a * 7 + k],
                recv_sem=recv_sems.at[a * 7 + k], device_id=to, device_id_type=pl.DeviceIdType.MESH)

        locals_, sends = [], []
        for a in range(n):
            loc = pltpu.make_async_copy(ins[a], outs[a].at[me], local_sems.at[a])
            loc.start()
            locals_.append(loc)
            sends.append(copy(a, 0, (x, y, c), sibling, src=ins[a]))
            sends += [copy(a, 1 + j, (x, y, c), (*chip, c), src=ins[a]) for j, chip in enumerate(chips)]
        for cp in sends:
            cp.start()
        passed = []
        for a in range(n):
            for j, chip in enumerate(chips):
                copy(a, 1 + j, (*chip, c), (x, y, c)).wait_recv()
                fwd = copy(a, 4 + j, (*chip, c), sibling)
                fwd.start()
                passed.append(fwd)
        for a in range(n):
            copy(a, 0, sibling, (x, y, c)).wait_recv()
            for j, chip in enumerate(chips):
                copy(a, 4 + j, (*chip, 1 - c), (x, y, c)).wait_recv()
        for cp in sends + passed:
            cp.wait_send()
        for loc in locals_:
            loc.wait()

    any_spec = pl.BlockSpec(memory_space=pl.ANY)
    res = pl.pallas_call(
        body, name=name, in_specs=[any_spec] * n, out_specs=[any_spec] * n + [pl.BlockSpec(memory_space=pltpu.VMEM)],
        out_shape=[jax.ShapeDtypeStruct((N_DEV,) + a.shape, a.dtype) for a in arrays]
        + [jax.ShapeDtypeStruct((8, 128), F32)],
        scratch_shapes=[pltpu.SemaphoreType.DMA((7 * n,)), pltpu.SemaphoreType.DMA((7 * n,)),
                        pltpu.SemaphoreType.DMA((n,))],
        compiler_params=pltpu.CompilerParams(has_side_effects=True),
    )(*arrays)
    return res[:n], res[n][0, 0]


def _adamw_math(w, g, m, v):
    m2 = ADAM_B1 * m + (1.0 - ADAM_B1) * g
    v2 = ADAM_B2 * v + (1.0 - ADAM_B2) * (g * g)
    m_hat = m2 / (1.0 - ADAM_B1 ** ADAM_STEP)
    v_hat = v2 / (1.0 - ADAM_B2 ** ADAM_STEP)
    delta = -ADAM_LR * (m_hat / (jnp.sqrt(v_hat) + ADAM_EPS) + ADAM_WD * w)
    return delta, m2, v2


def _adamw_slabs(name, src, land, me, w, m, v, tr):
    R, C = w.shape

    def body(me_ref, own_ref, land_ref, w_ref, m_ref, v_ref, g_ref, d_ref, m2_ref, v2_ref):
        g = own_ref[0].astype(F32)
        for s in range(N_DEV - 1):
            g = g + land_ref[s].astype(F32)
        delta, m2, v2 = _adamw_math(w_ref[...], g, m_ref[...], v_ref[...])
        g_ref[...] = g
        d_ref[...] = delta
        m2_ref[...] = m2
        v2_ref[...] = v2

    im = lambda i, me_ref: (i, 0)
    grid_spec = pltpu.PrefetchScalarGridSpec(
        num_scalar_prefetch=1, grid=(R // tr,),
        in_specs=[pl.BlockSpec((1, tr, C), lambda i, me_ref: (me_ref[0], i, 0)),
                  pl.BlockSpec((N_DEV - 1, tr, C), lambda i, me_ref: (0, i, 0)),
                  pl.BlockSpec((tr, C), im), pl.BlockSpec((tr, C), im), pl.BlockSpec((tr, C), im)],
        out_specs=[pl.BlockSpec((tr, C), im)] * 4)
    return pl.pallas_call(body, name=name, grid_spec=grid_spec, out_shape=[jax.ShapeDtypeStruct((R, C), F32)] * 4,
                          compiler_params=_cp(1))(me.reshape(1).astype(jnp.int32), src, land, w, m, v)


def _sum_slots(name, slots):
    _, R, C = slots.shape

    def body(s_ref, o_ref):
        g = s_ref[0]
        for s in range(1, N_DEV):
            g = g + s_ref[s]
        o_ref[...] = g

    return _rows(name, R, R, [(slots, (N_DEV, R, C), lambda i: (0, 0, 0))],
                 [(jax.ShapeDtypeStruct((R, C), F32), (R, C), lambda i: (0, 0))], body)[0]


def _adamw_packed(name, g, w, m, v):
    R, C = g.shape

    def body(g_ref, w_ref, m_ref, v_ref, d_ref, m2_ref, v2_ref):
        delta, m2, v2 = _adamw_math(w_ref[...], g_ref[...], m_ref[...], v_ref[...])
        d_ref[...] = delta
        m2_ref[...] = m2
        v2_ref[...] = v2

    im = lambda i: (0, 0)
    sds = jax.ShapeDtypeStruct((R, C), F32)
    return _rows(name, R, R, [(a, (R, C), im) for a in (g, w, m, v)], [(sds, (R, C), im)] * 3, body)


def _pack(arrays):
    rows = []
    for a in arrays:
        flat = a.reshape(-1).astype(F32)
        pad = (-flat.shape[0]) % 128
        rows.append(jnp.pad(flat, (0, pad)).reshape(-1, 128))
    out = jnp.concatenate(rows, axis=0)
    return jnp.pad(out, ((0, (-out.shape[0]) % 8), (0, 0)))


def _unpack(packed, shapes):
    lead = packed.shape[:-2]
    outs = []
    r = 0
    for shp in shapes:
        n = math.prod(shp)
        nr = -(-n // 128)
        flat = packed[..., r:r + nr, :].reshape(lead + (nr * 128,))[..., :n]
        outs.append(flat.reshape(lead + tuple(shp)))
        r += nr
    return outs


FFN1_BIG = ["ffn1_w_gate", "ffn1_w_up", "ffn1_w_down"]
MIX_BIG = ["w_in", "w_out"]
FFN2_BIG = ["ffn2_w_gate", "ffn2_w_up", "ffn2_w_down"]
BIG = FFN1_BIG + MIX_BIG + FFN2_BIG
COL_SHARDED = {"ffn1_w_gate", "ffn1_w_up", "w_in", "ffn2_w_gate", "ffn2_w_up"}
SMALL_SHARDED = ["rg_conv_w", "rg_gate_a_b", "rg_gate_x_b", "rg_lambda", "gdn_conv_w"]
WEIGHTS = ["ffn1_norm", "ffn1_w_gate", "ffn1_w_up", "ffn1_w_down", "mix_norm", "w_in", "w_out", "rg_conv_w", "rg_conv_b",
           "rg_gate_a_w", "rg_gate_a_b", "rg_gate_x_w", "rg_gate_x_b", "rg_lambda", "gdn_conv_w", "gdn_a_log",
           "gdn_dt_bias", "gdn_norm", "ffn2_norm", "ffn2_w_gate", "ffn2_w_up", "ffn2_w_down", "final_norm"]
SMALL = [n for n in WEIGHTS if n not in BIG]
ROW_VECTORS = {"ffn1_norm", "mix_norm", "ffn2_norm", "gdn_norm", "rg_conv_b", "final_norm"}
ROW_TILE = {"ffn1_w_gate": 256, "ffn1_w_up": 256, "ffn1_w_down": 176, "w_in": 256, "w_out": 64,
            "ffn2_w_gate": 256, "ffn2_w_up": 256, "ffn2_w_down": 176}


def _unshard_cols(g):
    return g.transpose(1, 0, 2).reshape(g.shape[1], N_DEV * g.shape[2])


def _to_slabs(name, g):
    if name in COL_SHARDED:
        r, ctot = g.shape
        return g.reshape(r, N_DEV, ctot // N_DEV).transpose(1, 0, 2)
    return g.reshape(N_DEV, g.shape[0] // N_DEV, g.shape[1])


def _step(x, target, w, m, v):
    _, _, _, me = _mesh_pos()
    def unshard(n, gth):
        full = _unshard_cols(gth) if n in COL_SHARDED else gth.reshape(-1, gth.shape[-1])
        return jnp.pad(full, ((0, 0), (0, D_IN_PAD - D_IN))) if n == "w_in" else full

    def landed(started, name, after):
        srcs, lands = _exchange_wait(name, started, after)
        def with_own(src, land):
            slot = lax.broadcasted_iota(jnp.int32, (N_DEV,) + (1,) * src.ndim, 0)
            return jnp.where(slot == me, src[None], land)

        return [with_own(src, land) for src, land in zip(srcs, lands)]

    up_names = ["ffn1_w_gate", "ffn1_w_up"]
    first, tok = _all_gather("gather_ffn1", [w[n].astype(BF16) for n in up_names])
    W = {n: unshard(n, gth) for n, gth in zip(up_names, first)}
    small_shards = [w[n] for n in SMALL_SHARDED]
    st_down = _exchange_start("gather_ffn1_down_start", False, [(w["ffn1_w_down"] + tok).astype(BF16)])
    st_mix = _exchange_start("gather_mix_start", False,
                             [(w[n] + tok).astype(BF16) for n in MIX_BIG] + [_pack(small_shards) + tok])
    st_ffn2 = _exchange_start("gather_ffn2_start", False, [(w[n] + tok).astype(BF16) for n in FFN2_BIG])
    for n in SMALL:
        if n not in SMALL_SHARDED:
            W[n] = w[n]
    W["ffn1_norm"] = w["ffn1_norm"] + (st_down["token"] + st_mix["token"] + st_ffn2["token"])

    def more(stage, after):
        if stage == "ffn1_down":
            return {"ffn1_w_down": unshard("ffn1_w_down", landed(st_down, "gather_ffn1_down_wait", after)[0])}
        if stage == "ffn2":
            return {n: unshard(n, gth) for n, gth in zip(FFN2_BIG, landed(st_ffn2, "gather_ffn2_wait", after))}
        got = landed(st_mix, "gather_mix_wait", after)
        new = {n: unshard(n, gth) for n, gth in zip(MIX_BIG, got)}
        for n, gth in zip(SMALL_SHARDED, _unpack(got[-1], [s.shape for s in small_shards])):
            new[n] = jnp.moveaxis(gth, 0, -2).reshape(gth.shape[1:-1] + (N_DEV * gth.shape[-1],))
        return new

    R = _layer_fwd(x, target, W, more)
    W = R["W"]
    pending = []

    def emit_big(**named):
        slabs = [_to_slabs(n, g[:, :D_IN] if n == "w_in" else g) for n, g in named.items()]
        started = _exchange_start(f"scatter_start_{len(pending)}", True, slabs)
        pending.append((list(named), started))
        return started["token"]

    small_started = []

    def emit_small(G):
        packed = _pack([G[n] for n in SMALL if n != "ffn1_norm"])
        small_started.append(_exchange_start("gather_small_start", False, [packed]))

    grad_x, G = _layer_bwd(x, W, R, emit_big, emit_small)
    loss = lax.psum(R["loss"][0, 0], ("x", "y", "c"))
    out = {}

    def finish(i, after):
        names, started = pending[i]
        srcs, lands = _exchange_wait(f"scatter_wait_{i}", started, after)
        for n, src, land in zip(names, srcs, lands):
            out[n] = _adamw_slabs(f"adamw_{n}", src, land, me, w[n], m[n], v[n], ROW_TILE[n])

    n_early = len(pending) - 2
    for i in range(n_early):
        finish(i, grad_x)
    early = [n for n in SMALL if n != "ffn1_norm"]
    srcs, lands = _exchange_wait("gather_small_wait", small_started[0], grad_x)
    slot = lax.broadcasted_iota(jnp.int32, (N_DEV, 1, 1), 0)
    slots = jnp.where(slot == me, srcs[0][None], lands[0])
    reduced = dict(zip(early, _unpack(_sum_slots("sum_small_grads", slots), [G[n].shape for n in early])))

    def adamw_small(name, names):
        g_small = []
        for n in names:
            g = reduced[n]
            if n in SMALL_SHARDED:
                per = g.shape[-1] // N_DEV
                g = lax.dynamic_slice_in_dim(g, me * per, per, axis=g.ndim - 1)
            g_small.append(g.reshape(w[n].shape))
        shapes = [w[n].shape for n in names]
        d_p, m_p, v_p = _adamw_packed(name, _pack(g_small), _pack([w[n] for n in names]),
                                      _pack([m[n] for n in names]), _pack([v[n] for n in names]))
        for n, g, d_, m_, v_ in zip(names, g_small, _unpack(d_p, shapes), _unpack(m_p, shapes), _unpack(v_p, shapes)):
            out[n] = (g, d_, m_, v_)
        return d_p

    adamw_small("adamw_small", early)
    late = _all_gather("gather_ffn1_norm_grad", [_pack([G["ffn1_norm"]])])[0][0]
    reduced["ffn1_norm"] = _unpack(_sum_slots("sum_ffn1_norm_grad", late), [G["ffn1_norm"].shape])[0]
    done = adamw_small("adamw_ffn1_norm", ["ffn1_norm"])
    for i in range(n_early, len(pending)):
        finish(i, done)
    return loss, grad_x, out


def kernel(x, ffn1_norm, ffn1_w_gate, ffn1_w_up, ffn1_w_down, mix_norm, w_in, w_out, rg_conv_w, rg_conv_b, rg_gate_a_w, rg_gate_a_b, rg_gate_x_w, rg_gate_x_b, rg_lambda, gdn_conv_w, gdn_a_log, gdn_dt_bias, gdn_norm, ffn2_norm, ffn2_w_gate, ffn2_w_up, ffn2_w_down, final_norm, loss_target, m_ffn1_norm, m_ffn1_w_gate, m_ffn1_w_up, m_ffn1_w_down, m_mix_norm, m_w_in, m_w_out, m_rg_conv_w, m_rg_conv_b, m_rg_gate_a_w, m_rg_gate_a_b, m_rg_gate_x_w, m_rg_gate_x_b, m_rg_lambda, m_gdn_conv_w, m_gdn_a_log, m_gdn_dt_bias, m_gdn_norm, m_ffn2_norm, m_ffn2_w_gate, m_ffn2_w_up, m_ffn2_w_down, m_final_norm, v_ffn1_norm, v_ffn1_w_gate, v_ffn1_w_up, v_ffn1_w_down, v_mix_norm, v_w_in, v_w_out, v_rg_conv_w, v_rg_conv_b, v_rg_gate_a_w, v_rg_gate_a_b, v_rg_gate_x_w, v_rg_gate_x_b, v_rg_lambda, v_gdn_conv_w, v_gdn_a_log, v_gdn_dt_bias, v_gdn_norm, v_ffn2_norm, v_ffn2_w_gate, v_ffn2_w_up, v_ffn2_w_down, v_final_norm):
    args = dict(locals())
    orig_shapes = {n: args[n].shape for n in WEIGHTS}

    def local(prefix):
        d = {}
        for n in WEIGHTS:
            a = args[prefix + n]
            d[n] = a.reshape(1, -1) if n in ROW_VECTORS else a[0]
        return d

    loss, grad_x, out = _step(x[0], loss_target[0], local(""), local("m_"), local("v_"))
    res = [loss, grad_x[None]]
    for k in range(4):
        res += [out[n][k].reshape(orig_shapes[n]) for n in WEIGHTS]
    return tuple(res)
```

```python
import functools
import math

import jax
import jax.numpy as jnp
from jax import lax
from jax.experimental import pallas as pl
from jax.experimental.pallas import tpu as pltpu

F32, BF16 = jnp.float32, jnp.bfloat16

D_MODEL = 1024
D_FF = 2816
RG_W = 512
RG_BLOCKS = 8
RG_BLOCK = 64
RG_C = 8.0
CONV_W = 4
GDN_H = 4
GDN_DK = 128
CHUNK = 64
EPS = 1e-6
D_IN = 3088
D_IN_PAD = 3200
COL_BA = 3072
N_DEV = 8
HALO = 16
VMEM_LIMIT = 48 * 1024 * 1024
VMEM_CAP = 60 * 1024 * 1024

ADAM_LR = 0.001
ADAM_B1 = 0.9
ADAM_B2 = 0.999
ADAM_EPS = 1e-08
ADAM_WD = 0.01
ADAM_STEP = 10

HI = lax.Precision.HIGHEST


def _cp(n, vmem_limit=None):
    return pltpu.CompilerParams(dimension_semantics=("arbitrary",) * n,
                                vmem_limit_bytes=VMEM_LIMIT if vmem_limit is None else vmem_limit)


def _matmul_vmem_limit(block_bytes, acc_bytes):
    need = 2 * block_bytes + 2 * acc_bytes
    return int(min(VMEM_CAP, max(VMEM_LIMIT, need * 4 // 3)))


def _tile(n, pref):
    return min(n, pref)


def _sigmoid(x):
    return 0.5 * jnp.tanh(0.5 * x) + 0.5


def _softplus(x):
    return jnp.maximum(x, 0.0) + jnp.log(1.0 + jnp.exp(-jnp.abs(x)))


def _dot(a, b, ca, cb, prec=None):
    return lax.dot_general(a, b, (((ca,), (cb,)), ((), ())), preferred_element_type=F32, precision=prec)


def _fused_mm(name, M, N, K, tm, tn, tk, ops, pairs, extras, outs, epilogue):
    nm, nn, nk = M // tm, N // tn, K // tk
    assert nm * tm == M and nn * tn == N and nk * tk == K, (name, M, N, K, tm, tn, tk)
    spec_of = {
        "mk": pl.BlockSpec((tm, tk), lambda i, j, k: (i, k)),
        "km": pl.BlockSpec((tk, tm), lambda i, j, k: (k, i)),
        "kn": pl.BlockSpec((tk, tn), lambda i, j, k: (k, j)),
        "nk": pl.BlockSpec((tn, tk), lambda i, j, k: (j, k)),
    }
    in_specs = [spec_of[m] for _, m in ops]
    in_specs += [pl.BlockSpec(bs, lambda i, j, k, im=im: im(i, j)) for _, bs, im in extras]
    out_specs = [pl.BlockSpec(bs, lambda i, j, k, im=im: im(i, j)) for _, bs, im in outs]
    n_ops, n_ex, n_out = len(ops), len(extras), len(outs)
    n_acc = 1 + max(g for _, _, g in pairs)
    modes = [m for _, m in ops]

    def body(*refs):
        op_refs = refs[:n_ops]
        ex_refs = refs[n_ops:n_ops + n_ex]
        out_refs = refs[n_ops + n_ex:n_ops + n_ex + n_out]
        accs = refs[n_ops + n_ex + n_out:]
        i = pl.program_id(0)
        k = pl.program_id(2)
        def dots():
            vals = [r[...].astype(BF16) for r in op_refs]
            for ia, ib, g in pairs:
                yield g, _dot(vals[ia], vals[ib], 1 if modes[ia] == "mk" else 0, 0 if modes[ib] == "kn" else 1)

        if nk == 1:
            sums = [None] * n_acc
            for g, d in dots():
                sums[g] = d if sums[g] is None else sums[g] + d
            epilogue(i, [_Held(s) for s in sums], ex_refs, out_refs)
            return

        @pl.when(k == 0)
        def _():
            for a in accs:
                a[...] = jnp.zeros_like(a)

        for g, d in dots():
            accs[g][...] += d

        @pl.when(k == nk - 1)
        def _():
            epilogue(i, accs, ex_refs, out_refs)

    op_block = {"mk": tm * tk, "km": tm * tk, "kn": tk * tn, "nk": tk * tn}
    block_bytes = sum(op_block[m] * a.dtype.itemsize for a, m in ops)
    block_bytes += sum(math.prod(bs) * jnp.dtype(a.dtype).itemsize for a, bs, _ in list(extras) + list(outs))
    res = pl.pallas_call(
        body, name=name, grid=(nm, nn, nk), in_specs=in_specs, out_specs=out_specs,
        out_shape=[o for o, _, _ in outs],
        scratch_shapes=[pltpu.VMEM((tm, tn), F32)] * (n_acc if nk > 1 else 0),
        compiler_params=_cp(3, _matmul_vmem_limit(block_bytes, n_acc * tm * tn * 4)),
    )(*[a for a, _ in ops], *[a for a, _, _ in extras])
    return res


class _Held:
    def __init__(self, value):
        self.value = value

    def __getitem__(self, idx):
        return self.value[idx]


def _mn(i, j):
    return (i, j)


def _row0(i, j):
    return (0, 0)


def _rows(name, S, ts, ins, outs, body, scratch=()):
    return pl.pallas_call(
        body, name=name, grid=(S // ts,),
        in_specs=[pl.BlockSpec(bs, im) for _, bs, im in ins],
        out_specs=[pl.BlockSpec(bs, im) for _, bs, im in outs],
        out_shape=[o for o, _, _ in outs],
        scratch_shapes=list(scratch),
        compiler_params=_cp(1),
    )(*[a for a, _, _ in ins])


def _halo_ins(arr, S, ts, width, colblk):
    per = ts // HALO
    last = S // HALO - 1
    return [
        (arr, (ts, width), lambda i: (i, colblk)),
        (arr, (HALO, width), lambda i: (jnp.maximum(i * per - 1, 0), colblk)),
        (arr, (HALO, width), lambda i: (jnp.minimum((i + 1) * per, last), colblk)),
    ]


def _ext(main_ref, prev_ref, next_ref, i, n_tiles):
    prev = jnp.where(i > 0, prev_ref[...].astype(F32), 0.0)
    nxt = jnp.where(i < n_tiles - 1, next_ref[...].astype(F32), 0.0)
    return jnp.concatenate([prev, main_ref[...].astype(F32), nxt], axis=0)


def _shift(ext, off, ts):
    n = ext.shape[0]
    if off == 0:
        return ext[HALO:HALO + ts]
    return pltpu.roll(ext, (-off) % n, 0)[HALO:HALO + ts]


def _rmsnorm_fwd(name, x, g):
    S, D = x.shape
    ts = _tile(S, 512)

    def body(x_ref, g_ref, o_ref):
        xv = x_ref[...]
        r = lax.rsqrt(jnp.mean(xv * xv, axis=-1, keepdims=True) + EPS)
        o_ref[...] = (xv * r * g_ref[...]).astype(BF16)

    return _rows(name, S, ts,
                 [(x, (ts, D), lambda i: (i, 0)), (g, (1, D), lambda i: (0, 0))],
                 [(jax.ShapeDtypeStruct((S, D), BF16), (ts, D), lambda i: (i, 0))], body)[0]


def _rmsnorm_bwd_tile(dh, x, g):
    r = lax.rsqrt(jnp.mean(x * x, axis=-1, keepdims=True) + EPS)
    xhat = x * r
    dxn = dh * g
    dx = r * (dxn - xhat * jnp.mean(dxn * xhat, axis=-1, keepdims=True))
    return dx, dh * xhat


def _ffn_fwd(tag, x, h, wg, wu, wd):
    S = x.shape[0]
    tm = _tile(S, 1024)
    tn = 1408

    def epi_up(i, accs, ex, out):
        a = accs[0][...]
        b = accs[1][...]
        s = _sigmoid(a)
        sa = a * s
        out[0][...] = sa.astype(BF16)
        out[1][...] = (b * (s * (1.0 + a * (1.0 - s)))).astype(BF16)
        out[2][...] = (sa * b).astype(BF16)

    sds = jax.ShapeDtypeStruct((S, D_FF), BF16)
    a, b, f = _fused_mm(f"{tag}_up", S, D_FF, D_MODEL, tm, tn, D_MODEL,
                        [(h, "mk"), (wg, "kn"), (wu, "kn")], [(0, 1, 0), (0, 2, 1)], [],
                        [(sds, (tm, tn), _mn)] * 3, epi_up)

    def epi_down(i, accs, ex, out):
        out[0][...] = ex[0][...] + 0.5 * accs[0][...]

    if callable(wd):
        wd = wd(f)
    xo = _fused_mm(f"{tag}_down", S, D_MODEL, D_FF, tm, D_MODEL, 1408,
                   [(f, "mk"), (wd, "kn")], [(0, 1, 0)], [(x, (tm, D_MODEL), _mn)],
                   [(jax.ShapeDtypeStruct((S, D_MODEL), F32), (tm, D_MODEL), _mn)], epi_down)[0]
    return xo, a, b, f


def _conv_taps(ext, w_ref, ts):
    acc = None
    for j in range(CONV_W):
        term = w_ref[j:j + 1, :] * _shift(ext, j - 2, ts)
        acc = term if acc is None else acc + term
    return acc


def _l2norm_heads(s, scale):
    outs = []
    for h in range(GDN_H):
        sh = s[:, h * GDN_DK:(h + 1) * GDN_DK]
        outs.append(sh * (lax.rsqrt(jnp.sum(sh * sh, axis=-1, keepdims=True) + EPS) * scale))
    return jnp.concatenate(outs, axis=-1)


def _conv_fwd(name, p, colblk, w, bias, mode):
    S = p.shape[0]
    ts = _tile(S, 512)
    n_tiles = S // ts
    C = w.shape[1]

    def body(main, prev, nxt, w_ref, b_ref, o_ref):
        i = pl.program_id(0)
        c = _conv_taps(_ext(main, prev, nxt, i, n_tiles), w_ref, ts)
        if mode == "bias":
            o_ref[...] = c + b_ref[...]
        else:
            s = c * _sigmoid(c)
            if mode == "q":
                s = _l2norm_heads(s, GDN_DK ** -0.5)
            elif mode == "k":
                s = _l2norm_heads(s, 1.0)
            o_ref[...] = s

    ins = _halo_ins(p, S, ts, C, colblk) + [(w, (CONV_W, C), lambda i: (0, 0)), (bias, (1, C), lambda i: (0, 0))]
    return _rows(name, S, ts, ins, [(jax.ShapeDtypeStruct((S, C), F32), (ts, C), lambda i: (i, 0))], body)[0]


def _rg_gate_terms(pre, xc, prm_ref, d):
    r = _sigmoid(pre[:, d * 1024:d * 1024 + RG_W] + prm_ref[2 * d:2 * d + 1, :])
    ig = _sigmoid(pre[:, d * 1024 + RG_W:(d + 1) * 1024] + prm_ref[2 * d + 1:2 * d + 2, :])
    sp = _softplus(-prm_ref[4 + d:5 + d, :])
    log_a = -RG_C * r * sp
    a = jnp.exp(log_a)
    t = jnp.tanh(log_a)
    sq = jnp.sqrt(-2.0 * t / (1.0 - t))
    return r, ig, sp, a, sq


def _rg_gates_fwd(xc, bd, prm):
    S = xc.shape[0]
    tm = _tile(S, 256)

    def epi(i, accs, ex, out):
        pre = accs[0][...]
        xv = ex[0][...]
        for d in range(2):
            r, ig, sp, a, sq = _rg_gate_terms(pre, xv, ex[1], d)
            out[2 * d][...] = a
            out[2 * d + 1][...] = sq * ig * xv

    sds = jax.ShapeDtypeStruct((S, RG_W), F32)
    blk = (tm, RG_W)
    im = lambda i, j: (i, 0)
    return _fused_mm("rg_gates_fwd", S, 4 * RG_W, RG_W, tm, 4 * RG_W, RG_W,
                     [(xc, "mk"), (bd, "kn")], [(0, 1, 0)],
                     [(xc, blk, im), (prm, (8, RG_W), _row0)], [(sds, blk, im)] * 4, epi)


SUBLANES = 8


def _scan_rows(a, b, reverse):
    rows = lax.broadcasted_iota(jnp.int32, a.shape, 0)
    s = 1
    while s < SUBLANES:
        shift = SUBLANES - s if reverse else s
        a_sh = pltpu.roll(a, shift, 0)
        b_sh = pltpu.roll(b, shift, 0)
        valid = (rows < SUBLANES - s) if reverse else (rows >= s)
        b = jnp.where(valid, a * b_sh + b, b)
        a = jnp.where(valid, a * a_sh, a)
        s *= 2
    return a, b


def _rg_scan(name, a_f, b_f, a_b, b_b):
    S, C = a_f.shape
    ts = _tile(S, 512)
    n_tiles = S // ts

    def body(af, bf, ab, bb, hf, hb, carry):
        @pl.when(pl.program_id(0) == 0)
        def _():
            carry[...] = jnp.zeros_like(carry)

        n_sub = ts // SUBLANES

        def step(j, c):
            cf, cb = c
            r0 = pl.multiple_of(j * SUBLANES, SUBLANES)
            cum_a, h0 = _scan_rows(af[pl.ds(r0, SUBLANES), :], bf[pl.ds(r0, SUBLANES), :], False)
            h = h0 + cum_a * cf
            hf[pl.ds(r0, SUBLANES), :] = h
            cf = h[SUBLANES - 1:SUBLANES, :]
            r1 = pl.multiple_of((n_sub - 1 - j) * SUBLANES, SUBLANES)
            cum_a, h0 = _scan_rows(ab[pl.ds(r1, SUBLANES), :], bb[pl.ds(r1, SUBLANES), :], True)
            h = h0 + cum_a * cb
            hb[pl.ds(r1, SUBLANES), :] = h
            cb = h[0:1, :]
            return cf, cb

        cf, cb = lax.fori_loop(0, n_sub, step, (carry[0:1, :], carry[1:2, :]), unroll=4)
        carry[0:1, :] = cf
        carry[1:2, :] = cb

    fw = lambda i: (i, 0)
    bw = lambda i: (n_tiles - 1 - i, 0)
    sds = jax.ShapeDtypeStruct((S, C), F32)
    return _rows(name, S, ts,
                 [(a_f, (ts, C), fw), (b_f, (ts, C), fw), (a_b, (ts, C), bw), (b_b, (ts, C), bw)],
                 [(sds, (ts, C), fw), (sds, (ts, C), bw)], body, scratch=[pltpu.VMEM((8, C), F32)])


def _tri_masks():
    ri = lax.broadcasted_iota(jnp.int32, (CHUNK, CHUNK), 0)
    ci = lax.broadcasted_iota(jnp.int32, (CHUNK, CHUNK), 1)
    return ri, ci


def _gdn_prep_fwd(p, prm):
    S = p.shape[0]
    ts = _tile(S, 512)

    def body(p_ref, prm_ref, o_ref):
        raw = p_ref[...].astype(F32)
        lane = lax.broadcasted_iota(jnp.int32, (1, 128), 1)
        g = -jnp.exp(prm_ref[0:1, :]) * _softplus(raw + prm_ref[1:2, :])
        g = jnp.where((lane >= 8) & (lane < 16), g, 0.0)
        beta = _sigmoid(raw)
        ri, ci = _tri_masks()
        lower = (ri >= ci).astype(F32)
        upper = (ri <= ci).astype(F32)
        for c in range(ts // CHUNK):
            rows = slice(c * CHUNK, (c + 1) * CHUNK)
            gch = g[rows]
            gc = jnp.where(lane < 12, _dot(lower, gch, 1, 0, HI), _dot(upper, gch, 1, 0, HI))
            o_ref[rows, :] = jnp.where(lane < 8, beta[rows], gc)

    return _rows("gdn_prep_fwd", S, ts,
                 [(p, (ts, 128), lambda i: (i, COL_BA // 128)), (prm, (8, 128), lambda i: (0, 0))],
                 [(jax.ShapeDtypeStruct((S, 128), F32), (ts, 128), lambda i: (i, 0))], body)[0]


def _bdot(a, b, ca, cb):
    return _dot(a.astype(BF16), b.astype(BF16), ca, cb)


GDN_W = GDN_H * GDN_DK
GDN_TS = 256
LOCAL_CHUNKS = 2

def _gdn_decay(bg_ref, gcr_ref, c, rows, r0, col, rev, ri, ci):
    beta = bg_ref[rows, col:col + 1]
    gc = bg_ref[rows, 8 + col:9 + col]
    last = 0 if rev else CHUNK - 1
    gl = bg_ref[pl.ds(r0 + last, 1), 8 + col:9 + col]
    out = dict(beta=beta, gc=gc, gl=gl, eg=jnp.exp(gc), egl=jnp.exp(gl - gc), cd=jnp.exp(gl))
    if gcr_ref is not None:
        incl = (ri <= ci) if rev else (ri >= ci)
        out["strict"] = (ri < ci) if rev else (ri > ci)
        out["dm"] = jnp.where(incl, jnp.exp(jnp.where(incl, gc - gcr_ref[c, col:col + 1, :], 0.0)), 0.0)
    return out


def _dir_tile(d, n_tiles, flip):
    if (d == 1) != flip:
        return lambda i: n_tiles - 1 - i
    return lambda i: i


def _gdn_local_fwd(q, k, v, bg, gcr):
    S = q.shape[0]
    ts = _tile(S, GDN_TS)
    ncb = ts // CHUNK
    nch = S // CHUNK

    def body(q_ref, k_ref, v_ref, bg_ref, gcr_ref, *out_refs):
        ri, ci = _tri_masks()
        eye = (ri == ci).astype(F32)
        outs = (out_refs[0:6], out_refs[6:12])
        cd_ref = out_refs[12]

        def chunk(cc, carry):
            chains = []
            for c in (LOCAL_CHUNKS * cc + j for j in range(LOCAL_CHUNKS)):
                r0 = pl.multiple_of(c * CHUNK, CHUNK)
                rows = pl.ds(r0, CHUNK)
                for h in range(GDN_H):
                    cols = slice(h * GDN_DK, (h + 1) * GDN_DK)
                    qh, kh, vh = q_ref[rows, cols], k_ref[rows, cols], v_ref[rows, cols]
                    both = _bdot(jnp.concatenate([qh, kh], axis=0), kh, 1, 1)
                    for d in range(2):
                        chains.append(dict(c=c, r0=r0, rows=rows, h=h, d=d, cols=cols, qh=qh, kh=kh, vh=vh,
                                           qk=both[0:CHUNK], kk=both[CHUNK:2 * CHUNK]))
            for ch in chains:
                m = _gdn_decay(bg_ref, gcr_ref, ch["c"], ch["rows"], ch["r0"], ch["d"] * GDN_H + ch["h"], ch["d"] == 1,
                               ri, ci)
                ch["m"] = m
                ch["x"] = -jnp.where(m["strict"], m["beta"] * ch["kk"] * m["dm"], 0.0)
                ch["t"] = eye + ch["x"]
            for ch in chains:
                ch["pw"] = _bdot(ch["x"], ch["x"], 1, 0)
            for level in range(1, 6):
                last_level = level == 5
                for ch in chains:
                    rhs = ch["t"] if last_level else jnp.concatenate([ch["t"], ch["pw"]], axis=1)
                    ch["prod"] = _bdot(ch["pw"], rhs, 1, 0)
                for ch in chains:
                    ch["t"] = ch["t"] + ch["prod"][:, 0:CHUNK]
                    if not last_level:
                        ch["pw"] = ch["prod"][:, CHUNK:2 * CHUNK]
            for ch in chains:
                m = ch["m"]
                rhs = jnp.concatenate([ch["vh"] * m["beta"], ch["kh"] * (m["beta"] * m["eg"])], axis=1)
                ch["uw"] = _bdot(ch["t"], rhs, 1, 0)
            for ch in chains:
                u_ref, w_ref, a_ref, t_ref, qd_ref, kd_ref = outs[ch["d"]]
                m = ch["m"]
                c, rows = ch["c"], ch["rows"]
                col = ch["d"] * GDN_H + ch["h"]
                u_ref[rows, ch["cols"]] = ch["uw"][:, 0:GDN_DK]
                w_ref[rows, ch["cols"]] = ch["uw"][:, GDN_DK:2 * GDN_DK].astype(BF16)
                a_ref[c, ch["h"]] = (ch["qk"] * m["dm"]).astype(BF16)
                t_ref[c, ch["h"]] = _bdot(ch["t"], eye, 0, 0).astype(BF16)
                qd_ref[rows, ch["cols"]] = (ch["qh"] * m["eg"]).astype(BF16)
                kd_ref[rows, ch["cols"]] = (ch["kh"] * m["egl"]).astype(BF16)
                cd_ref[c, col:col + 1, :] = jnp.broadcast_to(m["cd"], (1, 128))
            return carry

        lax.fori_loop(0, ncb // LOCAL_CHUNKS, chunk, 0)

    im = lambda i: (i, 0)
    im4 = lambda i: (i, 0, 0, 0)
    ins = [(q, (ts, GDN_W), im), (k, (ts, GDN_W), im), (v, (ts, GDN_W), im), (bg, (ts, 128), im),
           (gcr, (ncb, 8, CHUNK), lambda i: (i, 0, 0))]
    per_dir = [(jax.ShapeDtypeStruct((S, GDN_W), F32), (ts, GDN_W), im),
               (jax.ShapeDtypeStruct((S, GDN_W), BF16), (ts, GDN_W), im),
               (jax.ShapeDtypeStruct((nch, GDN_H, CHUNK, CHUNK), BF16), (ncb, GDN_H, CHUNK, CHUNK), im4),
               (jax.ShapeDtypeStruct((nch, GDN_H, CHUNK, CHUNK), BF16), (ncb, GDN_H, CHUNK, CHUNK), im4),
               (jax.ShapeDtypeStruct((S, GDN_W), BF16), (ts, GDN_W), im),
               (jax.ShapeDtypeStruct((S, GDN_W), BF16), (ts, GDN_W), im)]
    cd_out = (jax.ShapeDtypeStruct((nch, 8, 128), F32), (ncb, 8, 128), lambda i: (i, 0, 0))
    res = _rows("gdn_local_fwd", S, ts, ins, per_dir * 2 + [cd_out], body)
    return res[0:6], res[6:12], res[12]


def _gdn_scan_fwd(loc):
    S = loc[0][0].shape[0]
    ts = _tile(S, GDN_TS)
    n_tiles = S // ts
    ncb = ts // CHUNK
    nch = S // CHUNK

    def body(*refs):
        ins = (refs[0:6], refs[6:12])
        outs = (refs[12:15], refs[15:18])
        state = refs[18]

        @pl.when(pl.program_id(0) == 0)
        def _():
            state[...] = jnp.zeros_like(state)

        def chunk(cc, carry):
            chains = []
            for d in range(2):
                c = cc if d == 0 else ncb - 1 - cc
                rows = pl.ds(pl.multiple_of(c * CHUNK, CHUNK), CHUNK)
                for h in range(GDN_H):
                    cols = slice(h * GDN_DK, (h + 1) * GDN_DK)
                    chains.append(dict(d=d, h=h, c=c, rows=rows, cols=cols, st=state[d * GDN_H + h]))
            for ch in chains:
                qd_ref, kd_ref, u_ref, w_ref, a_ref, cd_ref = ins[ch["d"]]
                rows, cols = ch["rows"], ch["cols"]
                lhs = jnp.concatenate([w_ref[rows, cols], qd_ref[rows, cols]], axis=0)
                ch["ws_qs"] = _dot(lhs, ch["st"].astype(BF16), 1, 0)
            for ch in chains:
                qd_ref, kd_ref, u_ref, w_ref, a_ref, cd_ref = ins[ch["d"]]
                rows, cols = ch["rows"], ch["cols"]
                vn = u_ref[rows, cols] - ch["ws_qs"][0:CHUNK]
                vnb = vn.astype(BF16)
                ch["vn"] = vn
                ch["avn"] = _dot(a_ref[ch["c"], ch["h"]], vnb, 1, 0)
                ch["kvn"] = _dot(kd_ref[rows, cols], vnb, 0, 0)
            for ch in chains:
                o_ref, vn_ref, s_ref = outs[ch["d"]]
                cd_ref = ins[ch["d"]][5]
                rows, cols = ch["rows"], ch["cols"]
                col = ch["d"] * GDN_H + ch["h"]
                o_ref[rows, cols] = ch["ws_qs"][CHUNK:2 * CHUNK] + ch["avn"]
                vn_ref[rows, cols] = ch["vn"].astype(BF16)
                s_ref[ch["c"], ch["h"]] = ch["st"].astype(BF16)
                state[ch["d"] * GDN_H + ch["h"]] = ch["st"] * cd_ref[ch["c"], col:col + 1, :] + ch["kvn"]
            return carry

        lax.fori_loop(0, ncb, chunk, 0)

    ins, outs = [], []
    for d in range(2):
        tix = _dir_tile(d, n_tiles, False)
        im = lambda i, tix=tix: (tix(i), 0)
        im4 = lambda i, tix=tix: (tix(i), 0, 0, 0)
        u, w, a, _, qd, kd = loc[d]
        ins += [(qd, (ts, GDN_W), im), (kd, (ts, GDN_W), im), (u, (ts, GDN_W), im), (w, (ts, GDN_W), im),
                (a, (ncb, GDN_H, CHUNK, CHUNK), im4), (loc[2], (ncb, 8, 128), lambda i, tix=tix: (tix(i), 0, 0))]
        outs += [(jax.ShapeDtypeStruct((S, GDN_W), F32), (ts, GDN_W), im),
                 (jax.ShapeDtypeStruct((S, GDN_W), BF16), (ts, GDN_W), im),
                 (jax.ShapeDtypeStruct((nch, GDN_H, GDN_DK, GDN_DK), BF16), (ncb, GDN_H, GDN_DK, GDN_DK), im4)]
    res = _rows("gdn_scan_fwd", S, ts, ins, outs, body, scratch=[pltpu.VMEM((2 * GDN_H, GDN_DK, GDN_DK), F32)])
    return res[0:3], res[3:6]


def _gelu(x):
    c = math.sqrt(2.0 / math.pi)
    t = jnp.tanh(c * (x + 0.044715 * x * x * x))
    return 0.5 * x * (1.0 + t), t


def _mix_out_fwd(h_f, h_b, o_f, o_b, p, gn):
    S = h_f.shape[0]
    ts = _tile(S, 512)

    def body(hf, hb, of, ob, gate, z, gn_ref, y_ref):
        ge, _ = _gelu(gate[...].astype(F32))
        y_ref[:, 0:RG_W] = ((hf[...] + hb[...]) * ge).astype(BF16)
        o = of[...] + ob[...]
        zv = z[...].astype(F32)
        sz = zv * _sigmoid(zv)
        for h in range(GDN_H):
            cols = slice(h * GDN_DK, (h + 1) * GDN_DK)
            oh = o[:, cols]
            n = oh * lax.rsqrt(jnp.mean(oh * oh, axis=-1, keepdims=True) + EPS) * gn_ref[...]
            y_ref[:, RG_W + h * GDN_DK:RG_W + (h + 1) * GDN_DK] = (n * sz[:, cols]).astype(BF16)

    blk = (ts, RG_W)
    im = lambda i: (i, 0)
    ins = [(h_f, blk, im), (h_b, blk, im), (o_f, blk, im), (o_b, blk, im),
           (p, blk, lambda i: (i, 1)), (p, blk, lambda i: (i, 5)), (gn, (1, GDN_DK), lambda i: (0, 0))]
    return _rows("mix_out_fwd", S, ts, ins,
                 [(jax.ShapeDtypeStruct((S, D_MODEL), BF16), (ts, D_MODEL), im)], body)[0]


def _loss_head(x, target, g):
    S, D = x.shape
    ts = _tile(S, 512)

    def body(x_ref, t_ref, g_ref, dx_ref, loss_ref, dg_ref):
        @pl.when(pl.program_id(0) == 0)
        def _():
            loss_ref[...] = jnp.zeros_like(loss_ref)
            dg_ref[...] = jnp.zeros_like(dg_ref)

        xv = x_ref[...]
        gv = g_ref[...]
        r = lax.rsqrt(jnp.mean(xv * xv, axis=-1, keepdims=True) + EPS)
        err = xv * r * gv - t_ref[...]
        loss_ref[...] += jnp.sum(err * err) * (0.5 / D)
        dx, dgt = _rmsnorm_bwd_tile(err * (1.0 / D), xv, gv)
        dx_ref[...] = dx
        dg_ref[...] += jnp.sum(dgt, axis=0, keepdims=True)

    im = lambda i: (i, 0)
    z = lambda i: (0, 0)
    return _rows("loss_head", S, ts,
                 [(x, (ts, D), im), (target, (ts, D), im), (g, (1, D), z)],
                 [(jax.ShapeDtypeStruct((S, D), F32), (ts, D), im),
                  (jax.ShapeDtypeStruct((8, 128), F32), (8, 128), z),
                  (jax.ShapeDtypeStruct((1, D), F32), (1, D), z)], body)


def _block_diag(w):
    n = w.shape[0]
    return jnp.einsum("nij,nm->nimj", w, jnp.eye(n, dtype=w.dtype)).reshape(n * w.shape[1], n * w.shape[2])


def _rg_bd(a_w, x_w):
    return jnp.concatenate([_block_diag(a_w[0]), _block_diag(x_w[0]), _block_diag(a_w[1]), _block_diag(x_w[1])],
                           axis=1).astype(BF16)


def _rg_prm(ba, bx, lam):
    return jnp.concatenate([ba[0:1], bx[0:1], ba[1:2], bx[1:2], lam, jnp.zeros((2, RG_W), F32)], axis=0)


def _gdn_prm(a_log, dt_bias):
    rows = jnp.zeros((8, 128), F32)
    rows = rows.at[0, 8:16].set(a_log.reshape(-1))
    return rows.at[1, 8:16].set(dt_bias.reshape(-1))


def _gc_rows(bg):
    S = bg.shape[0]
    return bg[:, 8:16].reshape(S // CHUNK, CHUNK, 8).transpose(0, 2, 1)


def _layer_fwd(x0, target, W, more=None):
    S = x0.shape[0]
    R = {}
    R["h1"] = _rmsnorm_fwd("rms1", x0, W["ffn1_norm"])
    if more is not None:
        W = {**W, **more("ffn1_up", R["h1"])}
    late_wd = {}

    def ffn1_wd(after):
        late_wd.update(more("ffn1_down", after))
        return late_wd["ffn1_w_down"]

    R["x1"], R["a1"], R["b1"], R["f1"] = _ffn_fwd("ffn1", x0, R["h1"], W["ffn1_w_gate"], W["ffn1_w_up"],
                                                  ffn1_wd if more is not None else W["ffn1_w_down"])
    if more is not None:
        W = {**W, **late_wd, **more("mixer", R["x1"])}
    R["h2"] = _rmsnorm_fwd("rms2", R["x1"], W["mix_norm"])
    tm = _tile(S, 512)
    tmp = _tile(S, 1024)
    R["p"] = _fused_mm("in_proj", S, D_IN_PAD, D_MODEL, tmp, 640, D_MODEL, [(R["h2"], "mk"), (W["w_in"], "kn")],
                       [(0, 1, 0)], [], [(jax.ShapeDtypeStruct((S, D_IN_PAD), BF16), (tmp, 640), _mn)],
                       lambda i, accs, ex, out: out[0].__setitem__(Ellipsis, accs[0][...].astype(BF16)))[0]
    p = R["p"]
    R["xc"] = _conv_fwd("rg_conv_fwd", p, 0, W["rg_conv_w"], W["rg_conv_b"], "bias")
    R["bd"] = _rg_bd(W["rg_gate_a_w"], W["rg_gate_x_w"])
    R["rg_prm"] = _rg_prm(W["rg_gate_a_b"], W["rg_gate_x_b"], W["rg_lambda"])
    a_f, b_f, a_b, b_b = _rg_gates_fwd(R["xc"], R["bd"], R["rg_prm"])
    R["a_f"], R["a_b"] = a_f, a_b
    R["h_f"], R["h_b"] = _rg_scan("rg_scan_fwd", a_f, b_f, a_b, b_b)
    zero_b = jnp.zeros((1, RG_W), F32)
    cw = W["gdn_conv_w"]
    R["q"] = _conv_fwd("gdn_conv_q", p, 2, cw[:, 0:512], zero_b, "q")
    R["k"] = _conv_fwd("gdn_conv_k", p, 3, cw[:, 512:1024], zero_b, "k")
    R["v"] = _conv_fwd("gdn_conv_v", p, 4, cw[:, 1024:1536], zero_b, "v")
    R["gdn_prm"] = _gdn_prm(W["gdn_a_log"], W["gdn_dt_bias"])
    R["bg"] = _gdn_prep_fwd(p, R["gdn_prm"])
    R["gcr"] = _gc_rows(R["bg"])
    R["gdn_loc"] = _gdn_local_fwd(R["q"], R["k"], R["v"], R["bg"], R["gcr"])
    R["gdn_fwd"] = _gdn_scan_fwd(R["gdn_loc"])
    R["o_f"], R["o_b"] = R["gdn_fwd"][0][0], R["gdn_fwd"][1][0]
    R["y"] = _mix_out_fwd(R["h_f"], R["h_b"], R["o_f"], R["o_b"], p, W["gdn_norm"])
    R["x2"] = _fused_mm("out_proj", S, D_MODEL, D_MODEL, tm, D_MODEL, D_MODEL, [(R["y"], "mk"), (W["w_out"], "kn")],
                        [(0, 1, 0)], [(R["x1"], (tm, D_MODEL), _mn)],
                        [(jax.ShapeDtypeStruct((S, D_MODEL), F32), (tm, D_MODEL), _mn)],
                        lambda i, accs, ex, out: out[0].__setitem__(Ellipsis, ex[0][...] + accs[0][...]))[0]
    if more is not None:
        W = {**W, **more("ffn2", R["x2"])}
    R["h3"] = _rmsnorm_fwd("rms3", R["x2"], W["ffn2_norm"])
    R["x3"], R["a2"], R["b2"], R["f2"] = _ffn_fwd("ffn2", R["x2"], R["h3"], W["ffn2_w_gate"], W["ffn2_w_up"], W["ffn2_w_down"])
    R["dx3"], R["loss"], R["d_final_norm"] = _loss_head(R["x3"], target, W["final_norm"])
    R["W"] = W
    return R


def _colsum_into(ref, i, val):
    @pl.when(i == 0)
    def _():
        ref[...] = val

    @pl.when(i > 0)
    def _():
        ref[...] += val


def _ffn_bwd(tag, dout, x, g, h, a, b, f, wg, wu, wd, emit):
    S = x.shape[0]
    tm = _tile(S, 512)
    tk_s = _tile(S, 1024)
    dwd = _fused_mm(f"{tag}_dw_down", D_FF, D_MODEL, S, 1408, D_MODEL, tk_s, [(f, "km"), (dout, "kn")], [(0, 1, 0)], [],
                    [(jax.ShapeDtypeStruct((D_FF, D_MODEL), BF16), (1408, D_MODEL), _mn)],
                    lambda i, accs, ex, out: out[0].__setitem__(Ellipsis, (0.5 * accs[0][...]).astype(BF16)))[0]
    emit(down=dwd)

    def epi_act(i, accs, ex, out):
        df = 0.5 * accs[0][...]
        out[0][...] = (df * ex[1][...].astype(F32)).astype(BF16)
        out[1][...] = (df * ex[0][...].astype(F32)).astype(BF16)

    sds = jax.ShapeDtypeStruct((S, D_FF), BF16)
    da, db = _fused_mm(f"{tag}_dact", S, D_FF, D_MODEL, tm, 1408, D_MODEL, [(dout, "mk"), (wd, "nk")], [(0, 1, 0)],
                       [(a, (tm, 1408), _mn), (b, (tm, 1408), _mn)], [(sds, (tm, 1408), _mn)] * 2, epi_act)

    def epi_w2(i, accs, ex, out):
        out[0][...] = accs[0][...].astype(BF16)
        out[1][...] = accs[1][...].astype(BF16)

    sdw = jax.ShapeDtypeStruct((D_MODEL, D_FF), BF16)
    dwg, dwu = _fused_mm(f"{tag}_dw_up", D_MODEL, D_FF, S, D_MODEL, 1408, tk_s,
                         [(h, "km"), (da, "kn"), (db, "kn")], [(0, 1, 0), (0, 2, 1)], [],
                         [(sdw, (D_MODEL, 1408), _mn)] * 2, epi_w2)
    tok = emit(gate=dwg, up=dwu)
    if tok is not None:
        g = g + tok

    def epi_dx(i, accs, ex, out):
        dx, dgt = _rmsnorm_bwd_tile(accs[0][...], ex[0][...], ex[1][...])
        out[0][...] = ex[2][...] + dx
        _colsum_into(out[1], i, jnp.sum(dgt, axis=0, keepdims=True))

    tmx = _tile(S, 1024)
    dx, dg = _fused_mm(f"{tag}_dx", S, D_MODEL, D_FF, tmx, D_MODEL, 1408,
                       [(da, "mk"), (wg, "nk"), (db, "mk"), (wu, "nk")], [(0, 1, 0), (2, 3, 0)],
                       [(x, (tmx, D_MODEL), _mn), (g, (1, D_MODEL), _row0), (dout, (tmx, D_MODEL), _mn)],
                       [(jax.ShapeDtypeStruct((S, D_MODEL), F32), (tmx, D_MODEL), _mn),
                        (jax.ShapeDtypeStruct((1, D_MODEL), F32), (1, D_MODEL), _row0)], epi_dx)
    return dx, dg


def _mix_out_bwd(dy, h_f, h_b, o_f, o_b, p, gn):
    S = dy.shape[0]
    ts = _tile(S, 512)
    c0 = math.sqrt(2.0 / math.pi)

    def body(dy_ref, hf, hb, of, ob, gate, z, gn_ref, dhr_ref, dgate_ref, do_ref, dz_ref, dgn_ref):
        i = pl.program_id(0)
        gv = gate[...].astype(F32)
        ge, t = _gelu(gv)
        dy_rg = dy_ref[:, 0:RG_W]
        dhr_ref[...] = dy_rg * ge
        dgelu = 0.5 * (1.0 + t) + 0.5 * gv * (1.0 - t * t) * c0 * (1.0 + 3.0 * 0.044715 * gv * gv)
        dgate_ref[...] = (dy_rg * (hf[...] + hb[...]) * dgelu).astype(BF16)
        o = of[...] + ob[...]
        zv = z[...].astype(F32)
        sig = _sigmoid(zv)
        gnv = gn_ref[...]
        dgn = jnp.zeros((1, GDN_DK), F32)
        for h in range(GDN_H):
            cols = slice(h * GDN_DK, (h + 1) * GDN_DK)
            oh = o[:, cols]
            r = lax.rsqrt(jnp.mean(oh * oh, axis=-1, keepdims=True) + EPS)
            ohat = oh * r
            dyh = dy_ref[:, RG_W + h * GDN_DK:RG_W + (h + 1) * GDN_DK]
            zh = zv[:, cols]
            sh = sig[:, cols]
            dn = dyh * zh * sh
            dz_ref[:, cols] = (dyh * ohat * gnv * (sh * (1.0 + zh * (1.0 - sh)))).astype(BF16)
            dxn = dn * gnv
            do_ref[:, cols] = r * (dxn - ohat * jnp.mean(dxn * ohat, axis=-1, keepdims=True))
            dgn = dgn + jnp.sum(dn * ohat, axis=0, keepdims=True)
        _colsum_into(dgn_ref, i, dgn)

    blk = (ts, RG_W)
    im = lambda i: (i, 0)
    z0 = lambda i: (0, 0)
    ins = [(dy, (ts, D_MODEL), im), (h_f, blk, im), (h_b, blk, im), (o_f, blk, im), (o_b, blk, im),
           (p, blk, lambda i: (i, 1)), (p, blk, lambda i: (i, 5)), (gn, (1, GDN_DK), z0)]
    outs = [(jax.ShapeDtypeStruct((S, RG_W), F32), blk, im), (jax.ShapeDtypeStruct((S, RG_W), BF16), blk, im),
            (jax.ShapeDtypeStruct((S, RG_W), F32), blk, im), (jax.ShapeDtypeStruct((S, RG_W), BF16), blk, im),
            (jax.ShapeDtypeStruct((1, GDN_DK), F32), (1, GDN_DK), z0)]
    return _rows("mix_out_bwd", S, ts, ins, outs, body)


def _rg_scan_adj(name, a_up, b_up, a_dn, b_dn):
    S, C = a_up.shape
    ts = _tile(S, 512)
    n_tiles = S // ts

    def body(au, bu, ad, bd, mu_ref, lam_ref, carry):
        @pl.when(pl.program_id(0) == 0)
        def _():
            carry[...] = jnp.zeros_like(carry)

        n_sub = ts // SUBLANES
        rows = lax.broadcasted_iota(jnp.int32, (SUBLANES, C), 0)

        def half(a_ref, b_ref, out_ref, r0, c_in, reverse):
            a = a_ref[pl.ds(r0, SUBLANES), :]
            b = b_ref[pl.ds(r0, SUBLANES), :]
            cum_a, c0 = _scan_rows(a, a * b, reverse)
            c = c0 + cum_a * c_in
            edge = 0 if not reverse else SUBLANES - 1
            c_prev = jnp.where(rows == edge, c_in, pltpu.roll(c, SUBLANES - 1 if reverse else 1, 0))
            out_ref[pl.ds(r0, SUBLANES), :] = b + c_prev
            return c[0:1, :] if reverse else c[SUBLANES - 1:SUBLANES, :]

        def step(j, c):
            cu, cd = c
            cu = half(au, bu, mu_ref, pl.multiple_of(j * SUBLANES, SUBLANES), cu, False)
            cd = half(ad, bd, lam_ref, pl.multiple_of((n_sub - 1 - j) * SUBLANES, SUBLANES), cd, True)
            return cu, cd

        cu, cd = lax.fori_loop(0, n_sub, step, (carry[0:1, :], carry[1:2, :]), unroll=4)
        carry[0:1, :] = cu
        carry[1:2, :] = cd

    fw = lambda i: (i, 0)
    bw = lambda i: (n_tiles - 1 - i, 0)
    sds = jax.ShapeDtypeStruct((S, C), F32)
    return _rows(name, S, ts,
                 [(a_up, (ts, C), fw), (b_up, (ts, C), fw), (a_dn, (ts, C), bw), (b_dn, (ts, C), bw)],
                 [(sds, (ts, C), fw), (sds, (ts, C), bw)], body, scratch=[pltpu.VMEM((8, C), F32)])


def _halo_ex(arr, S, tm, width):
    per = tm // HALO
    last = S // HALO - 1
    return [
        (arr, (tm, width), lambda i, j: (i, 0)),
        (arr, (HALO, width), lambda i, j: (jnp.maximum(i * per - 1, 0), 0)),
        (arr, (HALO, width), lambda i, j: (jnp.minimum((i + 1) * per, last), 0)),
    ]


def _rg_gates_bwd(xc, bd, prm, lam_f, lam_b, h_f, h_b):
    S = xc.shape[0]
    tm = _tile(S, 256)
    n_tiles = S // tm

    def epi(i, accs, ex, out):
        pre = accs[0][...]
        xv = ex[0][...]
        prm_ref = ex[1]
        lams = (ex[2][...], ex[3][...])
        hprev = (_shift(_ext(ex[4], ex[5], ex[6], i, n_tiles), -1, tm),
                 _shift(_ext(ex[7], ex[8], ex[9], i, n_tiles), 1, tm))
        dxc = jnp.zeros_like(xv)
        rows = []
        dlam_rows = []
        for d in range(2):
            r, ig, sp, a, sq = _rg_gate_terms(pre, xv, prm_ref, d)
            lam = lams[d]
            da = lam * hprev[d]
            di = lam * sq * xv
            dxc = dxc + lam * sq * ig
            dsq = lam * ig * xv
            dlog_a = da * a - dsq * (a * a) / sq
            dpre_r = dlog_a * (-RG_C * sp) * r * (1.0 - r)
            dpre_i = di * ig * (1.0 - ig)
            out[0][:, d * 1024:d * 1024 + RG_W] = dpre_r.astype(BF16)
            out[0][:, d * 1024 + RG_W:(d + 1) * 1024] = dpre_i.astype(BF16)
            rows += [jnp.sum(dpre_r, axis=0, keepdims=True), jnp.sum(dpre_i, axis=0, keepdims=True)]
            dsp = jnp.sum(dlog_a * (-RG_C * r), axis=0, keepdims=True)
            dlam_rows.append(-dsp * _sigmoid(-prm_ref[4 + d:5 + d, :]))
        out[1][...] = dxc
        zero = jnp.zeros((2, RG_W), F32)
        _colsum_into(out[2], i, jnp.concatenate(rows + dlam_rows + [zero], axis=0))

    blk = (tm, RG_W)
    im = lambda i, j: (i, 0)
    extras = ([(xc, blk, im), (prm, (8, RG_W), _row0), (lam_f, blk, im), (lam_b, blk, im)]
              + _halo_ex(h_f, S, tm, RG_W) + _halo_ex(h_b, S, tm, RG_W))
    outs = [(jax.ShapeDtypeStruct((S, 4 * RG_W), BF16), (tm, 4 * RG_W), im),
            (jax.ShapeDtypeStruct((S, RG_W), F32), blk, im),
            (jax.ShapeDtypeStruct((8, RG_W), F32), (8, RG_W), _row0)]
    return _fused_mm("rg_gates_bwd", S, 4 * RG_W, RG_W, tm, 4 * RG_W, RG_W, [(xc, "mk"), (bd, "kn")], [(0, 1, 0)],
                     extras, outs, epi)


def _roll_rows(ext, off):
    if off == 0:
        return ext
    return pltpu.roll(ext, (-off) % ext.shape[0], 0)


def _conv_bwd(name, p, colblk, w, grads, mode):
    S = p.shape[0]
    ts = _tile(S, 512)
    n_tiles = S // ts
    C = w.shape[1]
    ng = len(grads)

    def body(*refs):
        p_refs = refs[0:3]
        g_refs = refs[3:3 + 3 * ng]
        w_ref = refs[3 + 3 * ng]
        dx_ref, dw_ref, db_ref = refs[4 + 3 * ng:]
        i = pl.program_id(0)
        ext_p = _ext(*p_refs, i, n_tiles)
        dn = _ext(*g_refs[0:3], i, n_tiles)
        for gi in range(1, ng):
            dn = dn + _ext(*g_refs[3 * gi:3 * gi + 3], i, n_tiles)
        if mode == "bias":
            dc = dn
        else:
            c = None
            for j in range(CONV_W):
                term = w_ref[j:j + 1, :] * _roll_rows(ext_p, j - 2)
                c = term if c is None else c + term
            sig = _sigmoid(c)
            s = c * sig
            if mode in ("q", "k"):
                scale = GDN_DK ** -0.5 if mode == "q" else 1.0
                parts = []
                for h in range(GDN_H):
                    cols = slice(h * GDN_DK, (h + 1) * GDN_DK)
                    sh = s[:, cols]
                    dnh = dn[:, cols]
                    rinv = lax.rsqrt(jnp.sum(sh * sh, axis=-1, keepdims=True) + EPS)
                    parts.append(scale * rinv * (dnh - sh * (rinv * rinv) * jnp.sum(dnh * sh, axis=-1, keepdims=True)))
                ds = jnp.concatenate(parts, axis=-1)
            else:
                ds = dn
            dc = ds * (sig * (1.0 + c * (1.0 - sig)))
        dx = None
        for j in range(CONV_W):
            term = w_ref[j:j + 1, :] * _shift(dc, 2 - j, ts)
            dx = term if dx is None else dx + term
        dx_ref[...] = dx.astype(BF16)
        dc_main = dc[HALO:HALO + ts]
        dw = jnp.concatenate([jnp.sum(dc_main * _shift(ext_p, j - 2, ts), axis=0, keepdims=True)
                              for j in range(CONV_W)], axis=0)
        _colsum_into(dw_ref, i, dw)
        _colsum_into(db_ref, i, jnp.sum(dc_main, axis=0, keepdims=True))

    ins = _halo_ins(p, S, ts, C, colblk)
    for garr in grads:
        ins += _halo_ins(garr, S, ts, C, 0)
    ins += [(w, (CONV_W, C), lambda i: (0, 0))]
    z0 = lambda i: (0, 0)
    outs = [(jax.ShapeDtypeStruct((S, C), BF16), (ts, C), lambda i: (i, 0)),
            (jax.ShapeDtypeStruct((CONV_W, C), F32), (CONV_W, C), z0),
            (jax.ShapeDtypeStruct((1, C), F32), (1, C), z0)]
    return _rows(name, S, ts, ins, outs, body)


def _gdn_scan_bwd(loc, do):
    S = do.shape[0]
    ts = _tile(S, GDN_TS)
    n_tiles = S // ts
    ncb = ts // CHUNK
    nch = S // CHUNK

    def body(*refs):
        ins = (refs[0:6], refs[6:12])
        outs = (refs[12:14], refs[14:16])
        dstate = refs[16]

        @pl.when(pl.program_id(0) == 0)
        def _():
            dstate[...] = jnp.zeros_like(dstate)

        def chunk(cc, carry):
            chains = []
            for d in range(2):
                c = ncb - 1 - cc if d == 0 else cc
                rows = pl.ds(pl.multiple_of(c * CHUNK, CHUNK), CHUNK)
                for h in range(GDN_H):
                    cols = slice(h * GDN_DK, (h + 1) * GDN_DK)
                    chains.append(dict(d=d, h=h, c=c, rows=rows, cols=cols, dsn=dstate[d * GDN_H + h]))
            for ch in chains:
                qd_ref, kd_ref, cd_ref, w_ref, a_ref, do_ref = ins[ch["d"]]
                rows, cols = ch["rows"], ch["cols"]
                dob = do_ref[rows, cols].astype(BF16)
                ch["dvn"] = (_dot(a_ref[ch["c"], ch["h"]], dob, 0, 0)
                             + _dot(kd_ref[rows, cols], ch["dsn"].astype(BF16), 1, 0))
                ch["qdo"] = _dot(qd_ref[rows, cols], dob, 0, 0)
            for ch in chains:
                w_ref = ins[ch["d"]][3]
                ch["wdvn"] = _dot(w_ref[ch["rows"], ch["cols"]], ch["dvn"].astype(BF16), 0, 0)
            for ch in chains:
                dvn_ref, ds_ref = outs[ch["d"]]
                cd_ref = ins[ch["d"]][2]
                col = ch["d"] * GDN_H + ch["h"]
                dvn_ref[ch["rows"], ch["cols"]] = ch["dvn"].astype(BF16)
                ds_ref[ch["c"], ch["h"]] = ch["dsn"].astype(BF16)
                dstate[ch["d"] * GDN_H + ch["h"]] = (ch["qdo"] + cd_ref[ch["c"], col:col + 1, :] * ch["dsn"]
                                                     - ch["wdvn"])
            return carry

        lax.fori_loop(0, ncb, chunk, 0)

    ins, outs = [], []
    for d in range(2):
        tix = _dir_tile(d, n_tiles, True)
        im = lambda i, tix=tix: (tix(i), 0)
        im4 = lambda i, tix=tix: (tix(i), 0, 0, 0)
        _, w, a, _, qd, kd = loc[d]
        ins += [(qd, (ts, GDN_W), im), (kd, (ts, GDN_W), im), (loc[2], (ncb, 8, 128), lambda i, tix=tix: (tix(i), 0, 0)),
                (w, (ts, GDN_W), im), (a, (ncb, GDN_H, CHUNK, CHUNK), im4), (do, (ts, GDN_W), im)]
        outs += [(jax.ShapeDtypeStruct((S, GDN_W), BF16), (ts, GDN_W), im),
                 (jax.ShapeDtypeStruct((nch, GDN_H, GDN_DK, GDN_DK), BF16), (ncb, GDN_H, GDN_DK, GDN_DK), im4)]
    res = _rows("gdn_scan_bwd", S, ts, ins, outs, body, scratch=[pltpu.VMEM((2 * GDN_H, GDN_DK, GDN_DK), F32)])
    return res[0:2], res[2:4]


def _gdn_local_bwd(q, k, v, bg, gcr, do, loc, fwd, adj):
    S = q.shape[0]
    ts = _tile(S, GDN_TS)
    ncb = ts // CHUNK

    def body(q_ref, k_ref, v_ref, bg_ref, gcr_ref, do_ref, *rest):
        per_dir = (rest[0:5], rest[5:10])
        dq_ref, dk_ref, dv_ref, dbg_ref, dbgr_ref = rest[10:15]
        ri, ci = _tri_masks()
        lane = lax.broadcasted_iota(jnp.int32, (CHUNK, 128), 1)
        rowi = lax.broadcasted_iota(jnp.int32, (CHUNK, 1), 0)
        ones8 = jnp.ones((SUBLANES, CHUNK), F32)

        def chunk(c, carry):
            r0 = pl.multiple_of(c * CHUNK, CHUNK)
            rows = pl.ds(r0, CHUNK)
            chains = []
            for h in range(GDN_H):
                cols = slice(h * GDN_DK, (h + 1) * GDN_DK)
                qh, kh, vh = q_ref[rows, cols], k_ref[rows, cols], v_ref[rows, cols]
                dob = do_ref[rows, cols].astype(BF16)
                both = _bdot(jnp.concatenate([qh, kh], axis=0), kh, 1, 1)
                for d in range(2):
                    chains.append(dict(h=h, d=d, cols=cols, qh=qh, kh=kh, vh=vh, dob=dob, qk=both[0:CHUNK],
                                       kk=both[CHUNK:2 * CHUNK], col=d * GDN_H + h))
            for ch in chains:
                m = _gdn_decay(bg_ref, gcr_ref, c, rows, r0, ch["col"], ch["d"] == 1, ri, ci)
                t_ref, s_ref, ds_ref, vn_ref, dvn_ref = per_dir[ch["d"]]
                h, cols = ch["h"], ch["cols"]
                ch["m"] = m
                ch["kb"] = ch["kh"] * m["beta"]
                ch["kbg"] = ch["kb"] * m["eg"]
                ch["t"] = t_ref[c, h]
                stb = s_ref[c, h]
                ch["dsn"] = ds_ref[c, h]
                vnb = vn_ref[rows, cols]
                dvnb = dvn_ref[rows, cols]
                ch["dcd"] = jnp.sum(jnp.sum(stb.astype(F32) * ch["dsn"].astype(F32), axis=1, keepdims=True),
                                    axis=0, keepdims=True)
                ch["dqd"] = _dot(ch["dob"], stb, 1, 1)
                ch["d_a"] = _dot(ch["dob"], vnb, 1, 1)
                ch["dkd"] = _bdot(vnb, ch["dsn"], 1, 1)
                ch["dw"] = -_dot(dvnb, stb, 1, 1)
                ch["dvb"] = _dot(ch["t"], dvnb, 1, 0)
                ch["d_t"] = _bdot(dvnb, ch["vh"] * m["beta"], 1, 1)
            for ch in chains:
                dwb = ch["dw"].astype(BF16)
                ch["d_t"] = ch["d_t"] + _bdot(dwb, ch["kbg"], 1, 1)
                ch["dkbg"] = _dot(ch["t"], dwb, 1, 0)
                ch["nn"] = ch["d_a"] * ch["m"]["dm"]
                ch["nn_q"] = _bdot(ch["nn"], ch["qh"], 0, 0)
                ch["nn_k"] = _bdot(ch["nn"], ch["kh"], 1, 0)
            for ch in chains:
                ch["x"] = _dot(ch["d_t"].astype(BF16), ch["t"], 1, 0)
            for ch in chains:
                d_l = -_dot(ch["t"], ch["x"].astype(BF16), 1, 0)
                ch["d_l"] = jnp.where(ch["m"]["strict"], d_l, 0.0)
                ch["mm"] = ch["d_l"] * ch["m"]["dm"]
            for ch in chains:
                m = ch["m"]
                ch["mm_kh"] = _bdot(ch["mm"], ch["kh"], 1, 0)
                ch["mm_kb"] = _bdot(ch["mm"], ch["kb"], 0, 0)
                l_mat = jnp.where(m["strict"], m["beta"] * ch["kk"] * m["dm"], 0.0)
                ch["e"] = ch["d_l"] * l_mat + ch["nn"] * ch["qk"]
                dbgr_ref[c, ch["col"]:ch["col"] + 1, :] = -_dot(ones8, ch["e"], 1, 0, HI)[0:1, :]
            acc_bg = jnp.zeros((CHUNK, 128), F32)
            acc = {}
            for ch in chains:
                m = ch["m"]
                beta, eg, egl = m["beta"], m["eg"], m["egl"]
                dkb = ch["mm_kh"] + ch["dkbg"] * eg
                dk_d = ch["mm_kb"] + ch["nn_q"] + ch["dkd"] * egl + dkb * beta
                dq_d = ch["nn_k"] + ch["dqd"] * eg
                dv_d = ch["dvb"] * beta
                dkd_kd = ch["dkd"] * (ch["kh"] * egl)
                dgc = (jnp.sum(ch["e"], axis=1, keepdims=True)
                       + jnp.sum(ch["dqd"] * (ch["qh"] * eg) - dkd_kd + ch["dkbg"] * ch["kbg"], axis=1, keepdims=True))
                dgl = jnp.sum(jnp.sum(dkd_kd, axis=1, keepdims=True), axis=0, keepdims=True) + ch["dcd"] * m["cd"]
                dgc = dgc + jnp.where(rowi == (0 if ch["d"] == 1 else CHUNK - 1), dgl, 0.0)
                dbeta = jnp.sum(dkb * ch["kh"] + ch["dvb"] * ch["vh"], axis=1, keepdims=True)
                acc_bg = acc_bg + jnp.where(lane == ch["col"], dbeta, 0.0) + jnp.where(lane == 8 + ch["col"], dgc, 0.0)
                if ch["d"] == 0:
                    acc[ch["h"]] = (dq_d, dk_d, dv_d)
                else:
                    dq0, dk0, dv0 = acc[ch["h"]]
                    dq_ref[rows, ch["cols"]] = dq0 + dq_d
                    dk_ref[rows, ch["cols"]] = dk0 + dk_d
                    dv_ref[rows, ch["cols"]] = dv0 + dv_d
            dbg_ref[rows, :] = acc_bg
            return carry

        lax.fori_loop(0, ncb, chunk, 0)

    im = lambda i: (i, 0)
    im4 = lambda i: (i, 0, 0, 0)
    blk = (ts, GDN_W)
    ins = [(q, blk, im), (k, blk, im), (v, blk, im), (bg, (ts, 128), im), (gcr, (ncb, 8, CHUNK), lambda i: (i, 0, 0)),
           (do, blk, im)]
    for d in range(2):
        ins += [(loc[d][3], (ncb, GDN_H, CHUNK, CHUNK), im4), (fwd[d][2], (ncb, GDN_H, GDN_DK, GDN_DK), im4),
                (adj[d][1], (ncb, GDN_H, GDN_DK, GDN_DK), im4), (fwd[d][1], blk, im), (adj[d][0], blk, im)]
    sds = jax.ShapeDtypeStruct((S, GDN_W), F32)
    outs = [(sds, blk, im), (sds, blk, im), (sds, blk, im), (jax.ShapeDtypeStruct((S, 128), F32), (ts, 128), im),
            (jax.ShapeDtypeStruct((S // CHUNK, 8, CHUNK), F32), (ncb, 8, CHUNK), lambda i: (i, 0, 0))]
    dq, dk, dv, dbg, dbg_rows = _rows("gdn_local_bwd", S, ts, ins, outs, body)
    dgc_cols = dbg_rows.transpose(0, 2, 1).reshape(S, 8)
    return dq, dk, dv, dbg + jnp.pad(dgc_cols, ((0, 0), (8, 112)))


def _gdn_prep_bwd(dbg_all, p, prm):
    S = p.shape[0]
    ts = _tile(S, 512)

    def body(dbg_ref, p_ref, prm_ref, dba_ref, dprm_ref):
        i = pl.program_id(0)
        raw = p_ref[...].astype(F32)
        dbg = dbg_ref[...]
        lane = lax.broadcasted_iota(jnp.int32, (1, 128), 1)
        is_g = (lane >= 8) & (lane < 16)
        ea = jnp.exp(prm_ref[0:1, :])
        arg = raw + prm_ref[1:2, :]
        g = jnp.where(is_g, -ea * _softplus(arg), 0.0)
        beta = _sigmoid(raw)
        dgc = jnp.where(is_g, dbg, 0.0)
        ri, ci = _tri_masks()
        lower = (ri >= ci).astype(F32)
        upper = (ri <= ci).astype(F32)
        dgs = []
        for c in range(ts // CHUNK):
            ch = dgc[c * CHUNK:(c + 1) * CHUNK]
            dgs.append(jnp.where(lane < 12, _dot(upper, ch, 1, 0, HI), _dot(lower, ch, 1, 0, HI)))
        dg = jnp.concatenate(dgs, axis=0)
        dalpha = jnp.where(is_g, dg * (-ea) * _sigmoid(arg), 0.0)
        dba_ref[...] = jnp.where(lane < 8, dbg * beta * (1.0 - beta), dalpha).astype(BF16)
        rows = jnp.concatenate([jnp.sum(dg * g, axis=0, keepdims=True), jnp.sum(dalpha, axis=0, keepdims=True),
                                jnp.zeros((6, 128), F32)], axis=0)
        _colsum_into(dprm_ref, i, rows)

    im = lambda i: (i, 0)
    z0 = lambda i: (0, 0)
    return _rows("gdn_prep_bwd", S, ts,
                 [(dbg_all, (ts, 128), im), (p, (ts, 128), lambda i: (i, COL_BA // 128)), (prm, (8, 128), z0)],
                 [(jax.ShapeDtypeStruct((S, 128), BF16), (ts, 128), im), (jax.ShapeDtypeStruct((8, 128), F32), (8, 128), z0)],
                 body)


def _mm_plain(name, M, N, K, tm, tn, tk, a, am, b, bm, dtype):
    return _fused_mm(name, M, N, K, tm, tn, tk, [(a, am), (b, bm)], [(0, 1, 0)], [],
                     [(jax.ShapeDtypeStruct((M, N), dtype), (tm, tn), _mn)],
                     lambda i, accs, ex, out: out[0].__setitem__(Ellipsis, accs[0][...].astype(dtype)))[0]


def _layer_bwd(x0, W, R, emit_big=None, emit_small=None):
    S = x0.shape[0]
    tm = _tile(S, 512)
    tk_s = _tile(S, 1024)
    G = {}

    def emit(**named):
        if emit_big is None:
            G.update(named)
            return None
        return emit_big(**named)

    def ffn_emit(prefix):
        return lambda **kw: emit(**{f"{prefix}_w_{k}": v for k, v in kw.items()})

    dx2, G["ffn2_norm"] = _ffn_bwd("ffn2b", R["dx3"], R["x2"], W["ffn2_norm"], R["h3"], R["a2"], R["b2"], R["f2"],
                                   W["ffn2_w_gate"], W["ffn2_w_up"], W["ffn2_w_down"], ffn_emit("ffn2"))
    tok = emit(w_out=_mm_plain("dw_out", D_MODEL, D_MODEL, S, D_MODEL, D_MODEL, tk_s, R["y"], "km", dx2, "kn", BF16))
    gn = W["gdn_norm"] if tok is None else W["gdn_norm"] + tok
    dy = _mm_plain("dy_mix", S, D_MODEL, D_MODEL, tm, D_MODEL, D_MODEL, dx2, "mk", W["w_out"], "nk", F32)
    p = R["p"]
    dhr, dgate, do, dz, G["gdn_norm"] = _mix_out_bwd(dy, R["h_f"], R["h_b"], R["o_f"], R["o_b"], p, gn)
    lam_b, lam_f = _rg_scan_adj("rg_scan_bwd", R["a_b"], dhr, R["a_f"], dhr)
    dpre, dxc_direct, d_rgprm = _rg_gates_bwd(R["xc"], R["bd"], R["rg_prm"], lam_f, lam_b, R["h_f"], R["h_b"])
    tmg = _tile(S, 512)
    dxc = _fused_mm("rg_dxc", S, RG_W, 4 * RG_W, tmg, RG_W, 4 * RG_W, [(dpre, "mk"), (R["bd"], "nk")], [(0, 1, 0)],
                    [(dxc_direct, (tmg, RG_W), _mn)], [(jax.ShapeDtypeStruct((S, RG_W), F32), (tmg, RG_W), _mn)],
                    lambda i, accs, ex, out: out[0].__setitem__(Ellipsis, ex[0][...] + accs[0][...]))[0]
    d_bd = _mm_plain("rg_dbd", RG_W, 4 * RG_W, S, RG_W, 4 * RG_W, tk_s, R["xc"], "km", dpre, "kn", F32)
    dx_rg, G["rg_conv_w"], G["rg_conv_b"] = _conv_bwd("rg_conv_bwd", p, 0, W["rg_conv_w"], [dxc], "bias")
    blocks = jnp.einsum("nigmj,nm->gnij", d_bd.reshape(RG_BLOCKS, RG_BLOCK, 4, RG_BLOCKS, RG_BLOCK),
                        jnp.eye(RG_BLOCKS, dtype=F32))
    G["rg_gate_a_w"] = jnp.stack([blocks[0], blocks[2]])
    G["rg_gate_x_w"] = jnp.stack([blocks[1], blocks[3]])
    G["rg_gate_a_b"] = jnp.stack([d_rgprm[0], d_rgprm[2]])
    G["rg_gate_x_b"] = jnp.stack([d_rgprm[1], d_rgprm[3]])
    G["rg_lambda"] = d_rgprm[4:6]
    adj = _gdn_scan_bwd(R["gdn_loc"], do)
    dq, dk, dv, dbg = _gdn_local_bwd(R["q"], R["k"], R["v"], R["bg"], R["gcr"], do, R["gdn_loc"], R["gdn_fwd"], adj)
    cw = W["gdn_conv_w"]
    dpq, dwq, _ = _conv_bwd("gdn_conv_q_bwd", p, 2, cw[:, 0:512], [dq], "q")
    dpk, dwk, _ = _conv_bwd("gdn_conv_k_bwd", p, 3, cw[:, 512:1024], [dk], "k")
    dpv, dwv, _ = _conv_bwd("gdn_conv_v_bwd", p, 4, cw[:, 1024:1536], [dv], "v")
    G["gdn_conv_w"] = jnp.concatenate([dwq, dwk, dwv], axis=1)
    dba, d_gprm = _gdn_prep_bwd(dbg, p, R["gdn_prm"])
    G["gdn_a_log"] = d_gprm[0, 8:16].reshape(2, GDN_H)
    G["gdn_dt_bias"] = d_gprm[1, 8:16].reshape(2, GDN_H)
    dp = jnp.concatenate([dx_rg, dgate, dpq, dpk, dpv, dz, dba], axis=1)
    tok = emit(w_in=_mm_plain("dw_in", D_MODEL, D_IN_PAD, S, D_MODEL, 640, tk_s, R["h2"], "km", dp, "kn", BF16))
    g_mix = W["mix_norm"] if tok is None else W["mix_norm"] + tok

    def epi_dx1(i, accs, ex, out):
        dx, dgt = _rmsnorm_bwd_tile(accs[0][...], ex[0][...], ex[1][...])
        out[0][...] = ex[2][...] + dx
        _colsum_into(out[1], i, jnp.sum(dgt, axis=0, keepdims=True))

    dx1, G["mix_norm"] = _fused_mm(
        "mix_dx", S, D_MODEL, D_IN_PAD, tm, D_MODEL, D_IN_PAD, [(dp, "mk"), (W["w_in"], "nk")], [(0, 1, 0)],
        [(R["x1"], (tm, D_MODEL), _mn), (g_mix, (1, D_MODEL), _row0), (dx2, (tm, D_MODEL), _mn)],
        [(jax.ShapeDtypeStruct((S, D_MODEL), F32), (tm, D_MODEL), _mn),
         (jax.ShapeDtypeStruct((1, D_MODEL), F32), (1, D_MODEL), _row0)], epi_dx1)
    G["final_norm"] = R["d_final_norm"]
    if emit_small is not None:
        emit_small(G)
    dx0, G["ffn1_norm"] = _ffn_bwd("ffn1b", dx1, x0, W["ffn1_norm"], R["h1"], R["a1"], R["b1"], R["f1"],
                                   W["ffn1_w_gate"], W["ffn1_w_up"], W["ffn1_w_down"], ffn_emit("ffn1"))
    return dx0, G


def _mesh_pos():
    x, y, c = lax.axis_index("x"), lax.axis_index("y"), lax.axis_index("c")
    return x, y, c, 4 * x + 2 * y + c


def _peer(x, y, c, r):
    px = 1 - x if r & 4 else x
    py = 1 - y if r & 2 else y
    pc = 1 - c if r & 1 else c
    return (px, py, pc), 4 * px + 2 * py + pc


_HBM = pl.BlockSpec(memory_space=pltpu.HBM)
_SEM = pl.BlockSpec(memory_space=pltpu.SEMAPHORE)


def _peer_copies(scatter, srcs, lands, send_sems, recv_sems):
    x, y, c, me = _mesh_pos()
    copies = []
    for a, (src, land) in enumerate(zip(srcs, lands)):
        for r in range(1, N_DEV):
            peer, peer_idx = _peer(x, y, c, r)
            copies.append(pltpu.make_async_remote_copy(
                src_ref=src.at[peer_idx] if scatter else src, dst_ref=land.at[r - 1] if scatter else land.at[me],
                send_sem=send_sems.at[a * 7 + r - 1], recv_sem=recv_sems.at[a * 7 + r - 1],
                device_id=peer, device_id_type=pl.DeviceIdType.MESH))
    return copies


def _exchange_start(name, scatter, arrays):
    slabs = arrays
    n = len(slabs)

    def body(*refs):
        srcs, lands = refs[0:n], refs[n:2 * n]
        send_sems, recv_sems = refs[2 * n], refs[2 * n + 1]
        token = refs[4 * n + 2]
        for cp in _peer_copies(scatter, srcs, lands, send_sems, recv_sems):
            cp.start()
        token[...] = jnp.zeros_like(token)

    land_shapes = [(N_DEV - 1,) + s.shape[1:] if scatter else (N_DEV,) + s.shape for s in slabs]
    n_sems = 7 * n
    out_shape = ([pltpu.SemaphoreType.DMA((n_sems,)), pltpu.SemaphoreType.DMA((n_sems,))]
                 + [pltpu.HBM(s.shape, s.dtype) for s in slabs]
                 + [pltpu.HBM(shp, s.dtype) for shp, s in zip(land_shapes, slabs)]
                 + [jax.ShapeDtypeStruct((8, 128), F32)])
    res = pl.pallas_call(
        body, name=name, out_shape=out_shape, in_specs=[_HBM] * (2 * n),
        out_specs=[_SEM, _SEM] + [_HBM] * (2 * n) + [pl.BlockSpec(memory_space=pltpu.VMEM)],
        input_output_aliases={i: 2 + i for i in range(2 * n)},
        compiler_params=pltpu.CompilerParams(has_side_effects=pltpu.SideEffectType.DATAFLOW_SIDE_EFFECTING),
    )(*[pltpu.with_memory_space_constraint(s, pltpu.HBM) for s in slabs],
      *[pltpu.with_memory_space_constraint(lax.empty(shp, s.dtype), pltpu.HBM) for shp, s in zip(land_shapes, slabs)])
    return dict(n=n, scatter=scatter, sems=res[0:2], srcs=res[2:2 + n], lands=res[2 + n:2 + 2 * n],
                token=res[2 + 2 * n][0, 0])


def _exchange_wait(name, started, after):
    n = started["n"]
    scatter = started["scatter"]

    def body(*refs):
        srcs, lands = refs[0:n], refs[n:2 * n]
        send_sems, recv_sems = refs[2 * n], refs[2 * n + 1]
        for cp in _peer_copies(scatter, srcs, lands, send_sems, recv_sems):
            cp.wait_send()
            cp.wait_recv()

    arrays = list(started["srcs"]) + list(started["lands"])
    res = pl.pallas_call(
        body, name=name, out_shape=[pltpu.HBM(a.shape, a.dtype) for a in arrays],
        in_specs=[_HBM] * (2 * n) + [_SEM, _SEM, pl.BlockSpec(memory_space=pl.ANY)], out_specs=[_HBM] * (2 * n),
        input_output_aliases={i: i for i in range(2 * n)},
        compiler_params=pltpu.CompilerParams(has_side_effects=pltpu.SideEffectType.DATAFLOW_SIDE_EFFECTING),
    )(*arrays, *started["sems"], after)
    return res[0:n], res[n:2 * n]


def _all_gather(name, arrays):
    n = len(arrays)

    def body(*refs):
        ins = refs[:n]
        outs = refs[n:2 * n]
        token = refs[2 * n]
        send_sems, recv_sems, local_sems = refs[2 * n + 1:]
        token[...] = jnp.zeros_like(token)
        x, y, c, me = _mesh_pos()
        sibling = (x, y, 1 - c)
        chips = [(1 - x, y), (x, 1 - y), (1 - x, 1 - y)]

        def idx(px, py, pc):
            return 4 * px + 2 * py + pc

        def copy(a, k, block, to, src=None):
            slot = outs[a].at[idx(*block)]
            return pltpu.make_async_remote_copy(
                src_ref=slot if src is None else src, dst_ref=slot, send_sem=send_sems.at[a * 7 + k],
                recv_sem=recv_sems.at[a * 7 + k], device_id=to, device_id_type=pl.DeviceIdType.MESH)

        locals_, sends = [], []
        for a in range(n):
            loc = pltpu.make_async_copy(ins[a], outs[a].at[me], local_sems.at[a])
            loc.start()
            locals_.append(loc)
            sends.append(copy(a, 0, (x, y, c), sibling, src=ins[a]))
            sends += [copy(a, 1 + j, (x, y, c), (*chip, c), src=ins[a]) for j, chip in enumerate(chips)]
        for cp in sends:
            cp.start()
        passed = []
        for a in range(n):
            for j, chip in enumerate(chips):
                copy(a, 1 + j, (*chip, c), (x, y, c)).wait_recv()
                fwd = copy(a, 4 + j, (*chip, c), sibling)
                fwd.start()
                passed.append(fwd)
        for a in range(n):
            copy(a, 0, sibling, (x, y, c)).wait_recv()
            for j, chip in enumerate(chips):
                copy(a, 4 + j, (*chip, 1 - c), (x, y, c)).wait_recv()
        for cp in sends + passed:
            cp.wait_send()
        for loc in locals_:
            loc.wait()

    any_spec = pl.BlockSpec(memory_space=pl.ANY)
    res = pl.pallas_call(
        body, name=name, in_specs=[any_spec] * n, out_specs=[any_spec] * n + [pl.BlockSpec(memory_space=pltpu.VMEM)],
        out_shape=[jax.ShapeDtypeStruct((N_DEV,) + a.shape, a.dtype) for a in arrays]
        + [jax.ShapeDtypeStruct((8, 128), F32)],
        scratch_shapes=[pltpu.SemaphoreType.DMA((7 * n,)), pltpu.SemaphoreType.DMA((7 * n,)),
                        pltpu.SemaphoreType.DMA((n,))],
        compiler_params=pltpu.CompilerParams(has_side_effects=True),
    )(*arrays)
    return res[:n], res[n][0, 0]


def _adamw_math(w, g, m, v):
    m2 = ADAM_B1 * m + (1.0 - ADAM_B1) * g
    v2 = ADAM_B2 * v + (1.0 - ADAM_B2) * (g * g)
    m_hat = m2 / (1.0 - ADAM_B1 ** ADAM_STEP)
    v_hat = v2 / (1.0 - ADAM_B2 ** ADAM_STEP)
    delta = -ADAM_LR * (m_hat / (jnp.sqrt(v_hat) + ADAM_EPS) + ADAM_WD * w)
    return delta, m2, v2


def _adamw_slabs(name, src, land, me, w, m, v, tr):
    R, C = w.shape

    def body(me_ref, own_ref, land_ref, w_ref, m_ref, v_ref, g_ref, d_ref, m2_ref, v2_ref):
        g = own_ref[0].astype(F32)
        for s in range(N_DEV - 1):
            g = g + land_ref[s].astype(F32)
        delta, m2, v2 = _adamw_math(w_ref[...], g, m_ref[...], v_ref[...])
        g_ref[...] = g
        d_ref[...] = delta
        m2_ref[...] = m2
        v2_ref[...] = v2

    im = lambda i, me_ref: (i, 0)
    grid_spec = pltpu.PrefetchScalarGridSpec(
        num_scalar_prefetch=1, grid=(R // tr,),
        in_specs=[pl.BlockSpec((1, tr, C), lambda i, me_ref: (me_ref[0], i, 0)),
                  pl.BlockSpec((N_DEV - 1, tr, C), lambda i, me_ref: (0, i, 0)),
                  pl.BlockSpec((tr, C), im), pl.BlockSpec((tr, C), im), pl.BlockSpec((tr, C), im)],
        out_specs=[pl.BlockSpec((tr, C), im)] * 4)
    return pl.pallas_call(body, name=name, grid_spec=grid_spec, out_shape=[jax.ShapeDtypeStruct((R, C), F32)] * 4,
                          compiler_params=_cp(1))(me.reshape(1).astype(jnp.int32), src, land, w, m, v)


def _sum_slots(name, slots):
    _, R, C = slots.shape

    def body(s_ref, o_ref):
        g = s_ref[0]
        for s in range(1, N_DEV):
            g = g + s_ref[s]
        o_ref[...] = g

    return _rows(name, R, R, [(slots, (N_DEV, R, C), lambda i: (0, 0, 0))],
                 [(jax.ShapeDtypeStruct((R, C), F32), (R, C), lambda i: (0, 0))], body)[0]


def _adamw_packed(name, g, w, m, v):
    R, C = g.shape

    def body(g_ref, w_ref, m_ref, v_ref, d_ref, m2_ref, v2_ref):
        delta, m2, v2 = _adamw_math(w_ref[...], g_ref[...], m_ref[...], v_ref[...])
        d_ref[...] = delta
        m2_ref[...] = m2
        v2_ref[...] = v2

    im = lambda i: (0, 0)
    sds = jax.ShapeDtypeStruct((R, C), F32)
    return _rows(name, R, R, [(a, (R, C), im) for a in (g, w, m, v)], [(sds, (R, C), im)] * 3, body)


def _pack(arrays):
    rows = []
    for a in arrays:
        flat = a.reshape(-1).astype(F32)
        pad = (-flat.shape[0]) % 128
        rows.append(jnp.pad(flat, (0, pad)).reshape(-1, 128))
    out = jnp.concatenate(rows, axis=0)
    return jnp.pad(out, ((0, (-out.shape[0]) % 8), (0, 0)))


def _unpack(packed, shapes):
    lead = packed.shape[:-2]
    outs = []
    r = 0
    for shp in shapes:
        n = math.prod(shp)
        nr = -(-n // 128)
        flat = packed[..., r:r + nr, :].reshape(lead + (nr * 128,))[..., :n]
        outs.append(flat.reshape(lead + tuple(shp)))
        r += nr
    return outs


FFN1_BIG = ["ffn1_w_gate", "ffn1_w_up", "ffn1_w_down"]
MIX_BIG = ["w_in", "w_out"]
FFN2_BIG = ["ffn2_w_gate", "ffn2_w_up", "ffn2_w_down"]
BIG = FFN1_BIG + MIX_BIG + FFN2_BIG
COL_SHARDED = {"ffn1_w_gate", "ffn1_w_up", "w_in", "ffn2_w_gate", "ffn2_w_up"}
SMALL_SHARDED = ["rg_conv_w", "rg_gate_a_b", "rg_gate_x_b", "rg_lambda", "gdn_conv_w"]
WEIGHTS = ["ffn1_norm", "ffn1_w_gate", "ffn1_w_up", "ffn1_w_down", "mix_norm", "w_in", "w_out", "rg_conv_w", "rg_conv_b",
           "rg_gate_a_w", "rg_gate_a_b", "rg_gate_x_w", "rg_gate_x_b", "rg_lambda", "gdn_conv_w", "gdn_a_log",
           "gdn_dt_bias", "gdn_norm", "ffn2_norm", "ffn2_w_gate", "ffn2_w_up", "ffn2_w_down", "final_norm"]
SMALL = [n for n in WEIGHTS if n not in BIG]
ROW_VECTORS = {"ffn1_norm", "mix_norm", "ffn2_norm", "gdn_norm", "rg_conv_b", "final_norm"}
ROW_TILE = {"ffn1_w_gate": 256, "ffn1_w_up": 256, "ffn1_w_down": 176, "w_in": 256, "w_out": 64,
            "ffn2_w_gate": 256, "ffn2_w_up": 256, "ffn2_w_down": 176}


def _unshard_cols(g):
    return g.transpose(1, 0, 2).reshape(g.shape[1], N_DEV * g.shape[2])


def _to_slabs(name, g):
    if name in COL_SHARDED:
        r, ctot = g.shape
        return g.reshape(r, N_DEV, ctot // N_DEV).transpose(1, 0, 2)
    return g.reshape(N_DEV, g.shape[0] // N_DEV, g.shape[1])


def _step(x, target, w, m, v):
    _, _, _, me = _mesh_pos()
    def unshard(n, gth):
        full = _unshard_cols(gth) if n in COL_SHARDED else gth.reshape(-1, gth.shape[-1])
        return jnp.pad(full, ((0, 0), (0, D_IN_PAD - D_IN))) if n == "w_in" else full

    def landed(started, name, after):
        srcs, lands = _exchange_wait(name, started, after)
        def with_own(src, land):
            slot = lax.broadcasted_iota(jnp.int32, (N_DEV,) + (1,) * src.ndim, 0)
            return jnp.where(slot == me, src[None], land)

        return [with_own(src, land) for src, land in zip(srcs, lands)]

    up_names = ["ffn1_w_gate", "ffn1_w_up"]
    small_shards = [w[n] for n in SMALL_SHARDED]
    st_up = _exchange_start("gather_ffn1_up_start", False, [w[n].astype(BF16) for n in up_names])
    tok = st_up["token"]
    st_down = _exchange_start("gather_ffn1_down_start", False, [(w["ffn1_w_down"] + tok).astype(BF16)])
    tok = tok + st_down["token"]
    st_mix = _exchange_start("gather_mix_start", False,
                             [(w[n] + tok).astype(BF16) for n in MIX_BIG] + [_pack(small_shards) + tok])
    tok = tok + st_mix["token"]
    st_ffn2 = _exchange_start("gather_ffn2_start", False, [(w[n] + tok).astype(BF16) for n in FFN2_BIG])
    W = {n: w[n] for n in SMALL if n not in SMALL_SHARDED}
    W["ffn1_norm"] = w["ffn1_norm"] + (tok + st_ffn2["token"])

    def more(stage, after):
        if stage == "ffn1_up":
            return {n: unshard(n, gth) for n, gth in zip(up_names, landed(st_up, "gather_ffn1_up_wait", after))}
        if stage == "ffn1_down":
            return {"ffn1_w_down": unshard("ffn1_w_down", landed(st_down, "gather_ffn1_down_wait", after)[0])}
        if stage == "ffn2":
            return {n: unshard(n, gth) for n, gth in zip(FFN2_BIG, landed(st_ffn2, "gather_ffn2_wait", after))}
        got = landed(st_mix, "gather_mix_wait", after)
        new = {n: unshard(n, gth) for n, gth in zip(MIX_BIG, got)}
        for n, gth in zip(SMALL_SHARDED, _unpack(got[-1], [s.shape for s in small_shards])):
            new[n] = jnp.moveaxis(gth, 0, -2).reshape(gth.shape[1:-1] + (N_DEV * gth.shape[-1],))
        return new

    R = _layer_fwd(x, target, W, more)
    W = R["W"]
    pending = []

    def emit_big(**named):
        slabs = [_to_slabs(n, g[:, :D_IN] if n == "w_in" else g) for n, g in named.items()]
        started = _exchange_start(f"scatter_start_{len(pending)}", True, slabs)
        pending.append((list(named), started))
        return started["token"]

    small_started = []

    def emit_small(G):
        packed = _pack([G[n] for n in SMALL if n != "ffn1_norm"])
        small_started.append(_exchange_start("gather_small_start", False, [packed]))

    grad_x, G = _layer_bwd(x, W, R, emit_big, emit_small)
    st_late = _exchange_start("gather_ffn1_norm_start", False, [_pack([G["ffn1_norm"]])])
    loss = lax.psum(R["loss"][0, 0], ("x", "y", "c"))
    out = {}

    def finish(i, after):
        names, started = pending[i]
        srcs, lands = _exchange_wait(f"scatter_wait_{i}", started, after)
        for n, src, land in zip(names, srcs, lands):
            out[n] = _adamw_slabs(f"adamw_{n}", src, land, me, w[n], m[n], v[n], ROW_TILE[n])

    n_early = len(pending) - 2
    for i in range(n_early):
        finish(i, grad_x)
    early = [n for n in SMALL if n != "ffn1_norm"]
    srcs, lands = _exchange_wait("gather_small_wait", small_started[0], grad_x)
    slot = lax.broadcasted_iota(jnp.int32, (N_DEV, 1, 1), 0)
    slots = jnp.where(slot == me, srcs[0][None], lands[0])
    reduced = dict(zip(early, _unpack(_sum_slots("sum_small_grads", slots), [G[n].shape for n in early])))

    def adamw_small(name, names):
        g_small = []
        for n in names:
            g = reduced[n]
            if n in SMALL_SHARDED:
                per = g.shape[-1] // N_DEV
                g = lax.dynamic_slice_in_dim(g, me * per, per, axis=g.ndim - 1)
            g_small.append(g.reshape(w[n].shape))
        shapes = [w[n].shape for n in names]
        d_p, m_p, v_p = _adamw_packed(name, _pack(g_small), _pack([w[n] for n in names]),
                                      _pack([m[n] for n in names]), _pack([v[n] for n in names]))
        for n, g, d_, m_, v_ in zip(names, g_small, _unpack(d_p, shapes), _unpack(m_p, shapes), _unpack(v_p, shapes)):
            out[n] = (g, d_, m_, v_)
        return d_p

    done_early = adamw_small("adamw_small", early)
    srcs, lands = _exchange_wait("gather_ffn1_norm_wait", st_late, done_early)
    late = jnp.where(slot == me, srcs[0][None], lands[0])
    reduced["ffn1_norm"] = _unpack(_sum_slots("sum_ffn1_norm_grad", late), [G["ffn1_norm"].shape])[0]
    done = adamw_small("adamw_ffn1_norm", ["ffn1_norm"])
    for i in range(n_early, len(pending)):
        finish(i, done)
    return loss, grad_x, out


def kernel(x, ffn1_norm, ffn1_w_gate, ffn1_w_up, ffn1_w_down, mix_norm, w_in, w_out, rg_conv_w, rg_conv_b, rg_gate_a_w, rg_gate_a_b, rg_gate_x_w, rg_gate_x_b, rg_lambda, gdn_conv_w, gdn_a_log, gdn_dt_bias, gdn_norm, ffn2_norm, ffn2_w_gate, ffn2_w_up, ffn2_w_down, final_norm, loss_target, m_ffn1_norm, m_ffn1_w_gate, m_ffn1_w_up, m_ffn1_w_down, m_mix_norm, m_w_in, m_w_out, m_rg_conv_w, m_rg_conv_b, m_rg_gate_a_w, m_rg_gate_a_b, m_rg_gate_x_w, m_rg_gate_x_b, m_rg_lambda, m_gdn_conv_w, m_gdn_a_log, m_gdn_dt_bias, m_gdn_norm, m_ffn2_norm, m_ffn2_w_gate, m_ffn2_w_up, m_ffn2_w_down, m_final_norm, v_ffn1_norm, v_ffn1_w_gate, v_ffn1_w_up, v_ffn1_w_down, v_mix_norm, v_w_in, v_w_out, v_rg_conv_w, v_rg_conv_b, v_rg_gate_a_w, v_rg_gate_a_b, v_rg_gate_x_w, v_rg_gate_x_b, v_rg_lambda, v_gdn_conv_w, v_gdn_a_log, v_gdn_dt_bias, v_gdn_norm, v_ffn2_norm, v_ffn2_w_gate, v_ffn2_w_up, v_ffn2_w_down, v_final_norm):
    args = dict(locals())
    orig_shapes = {n: args[n].shape for n in WEIGHTS}

    def local(prefix):
        d = {}
        for n in WEIGHTS:
            a = args[prefix + n]
            d[n] = a.reshape(1, -1) if n in ROW_VECTORS else a[0]
        return d

    loss, grad_x, out = _step(x[0], loss_target[0], local(""), local("m_"), local("v_"))
    res = [loss, grad_x[None]]
    for k in range(4):
        res += [out[n][k].reshape(orig_shapes[n]) for n in WEIGHTS]
    return tuple(res)
```

```python
import functools
import math

import jax
import jax.numpy as jnp
from jax import lax
from jax.experimental import pallas as pl
from jax.experimental.pallas import tpu as pltpu

F32, BF16 = jnp.float32, jnp.bfloat16

D_MODEL = 1024
D_FF = 2816
RG_W = 512
RG_BLOCKS = 8
RG_BLOCK = 64
RG_C = 8.0
CONV_W = 4
GDN_H = 4
GDN_DK = 128
CHUNK = 64
EPS = 1e-6
D_IN = 3088
D_IN_PAD = 3200
COL_BA = 3072
N_DEV = 8
HALO = 16
VMEM_LIMIT = 48 * 1024 * 1024
VMEM_CAP = 60 * 1024 * 1024

ADAM_LR = 0.001
ADAM_B1 = 0.9
ADAM_B2 = 0.999
ADAM_EPS = 1e-08
ADAM_WD = 0.01
ADAM_STEP = 10

HI = lax.Precision.HIGHEST


def _cp(n, vmem_limit=None):
    return pltpu.CompilerParams(dimension_semantics=("arbitrary",) * n,
                                vmem_limit_bytes=VMEM_LIMIT if vmem_limit is None else vmem_limit)


def _matmul_vmem_limit(block_bytes, acc_bytes):
    need = 2 * block_bytes + 2 * acc_bytes
    return int(min(VMEM_CAP, max(VMEM_LIMIT, need * 4 // 3)))


def _tile(n, pref):
    return min(n, pref)


def _sigmoid(x):
    return 0.5 * jnp.tanh(0.5 * x) + 0.5


def _softplus(x):
    return jnp.maximum(x, 0.0) + jnp.log(1.0 + jnp.exp(-jnp.abs(x)))


def _dot(a, b, ca, cb, prec=None):
    return lax.dot_general(a, b, (((ca,), (cb,)), ((), ())), preferred_element_type=F32, precision=prec)


def _fused_mm(name, M, N, K, tm, tn, tk, ops, pairs, extras, outs, epilogue):
    nm, nn, nk = M // tm, N // tn, K // tk
    assert nm * tm == M and nn * tn == N and nk * tk == K, (name, M, N, K, tm, tn, tk)
    spec_of = {
        "mk": pl.BlockSpec((tm, tk), lambda i, j, k: (i, k)),
        "km": pl.BlockSpec((tk, tm), lambda i, j, k: (k, i)),
        "kn": pl.BlockSpec((tk, tn), lambda i, j, k: (k, j)),
        "nk": pl.BlockSpec((tn, tk), lambda i, j, k: (j, k)),
    }
    in_specs = [spec_of[m] for _, m in ops]
    in_specs += [pl.BlockSpec(bs, lambda i, j, k, im=im: im(i, j)) for _, bs, im in extras]
    out_specs = [pl.BlockSpec(bs, lambda i, j, k, im=im: im(i, j)) for _, bs, im in outs]
    n_ops, n_ex, n_out = len(ops), len(extras), len(outs)
    n_acc = 1 + max(g for _, _, g in pairs)
    modes = [m for _, m in ops]

    def body(*refs):
        op_refs = refs[:n_ops]
        ex_refs = refs[n_ops:n_ops + n_ex]
        out_refs = refs[n_ops + n_ex:n_ops + n_ex + n_out]
        accs = refs[n_ops + n_ex + n_out:]
        i = pl.program_id(0)
        k = pl.program_id(2)
        def dots():
            vals = [r[...].astype(BF16) for r in op_refs]
            for ia, ib, g in pairs:
                yield g, _dot(vals[ia], vals[ib], 1 if modes[ia] == "mk" else 0, 0 if modes[ib] == "kn" else 1)

        if nk == 1:
            sums = [None] * n_acc
            for g, d in dots():
                sums[g] = d if sums[g] is None else sums[g] + d
            epilogue(i, [_Held(s) for s in sums], ex_refs, out_refs)
            return

        @pl.when(k == 0)
        def _():
            for a in accs:
                a[...] = jnp.zeros_like(a)

        for g, d in dots():
            accs[g][...] += d

        @pl.when(k == nk - 1)
        def _():
            epilogue(i, accs, ex_refs, out_refs)

    op_block = {"mk": tm * tk, "km": tm * tk, "kn": tk * tn, "nk": tk * tn}
    block_bytes = sum(op_block[m] * a.dtype.itemsize for a, m in ops)
    block_bytes += sum(math.prod(bs) * jnp.dtype(a.dtype).itemsize for a, bs, _ in list(extras) + list(outs))
    res = pl.pallas_call(
        body, name=name, grid=(nm, nn, nk), in_specs=in_specs, out_specs=out_specs,
        out_shape=[o for o, _, _ in outs],
        scratch_shapes=[pltpu.VMEM((tm, tn), F32)] * (n_acc if nk > 1 else 0),
        compiler_params=_cp(3, _matmul_vmem_limit(block_bytes, n_acc * tm * tn * 4)),
    )(*[a for a, _ in ops], *[a for a, _, _ in extras])
    return res


class _Held:
    def __init__(self, value):
        self.value = value

    def __getitem__(self, idx):
        return self.value[idx]


def _mn(i, j):
    return (i, j)


def _row0(i, j):
    return (0, 0)


def _rows(name, S, ts, ins, outs, body, scratch=()):
    return pl.pallas_call(
        body, name=name, grid=(S // ts,),
        in_specs=[pl.BlockSpec(bs, im) for _, bs, im in ins],
        out_specs=[pl.BlockSpec(bs, im) for _, bs, im in outs],
        out_shape=[o for o, _, _ in outs],
        scratch_shapes=list(scratch),
        compiler_params=_cp(1),
    )(*[a for a, _, _ in ins])


def _halo_ins(arr, S, ts, width, colblk):
    per = ts // HALO
    last = S // HALO - 1
    return [
        (arr, (ts, width), lambda i: (i, colblk)),
        (arr, (HALO, width), lambda i: (jnp.maximum(i * per - 1, 0), colblk)),
        (arr, (HALO, width), lambda i: (jnp.minimum((i + 1) * per, last), colblk)),
    ]


def _ext(main_ref, prev_ref, next_ref, i, n_tiles):
    prev = jnp.where(i > 0, prev_ref[...].astype(F32), 0.0)
    nxt = jnp.where(i < n_tiles - 1, next_ref[...].astype(F32), 0.0)
    return jnp.concatenate([prev, main_ref[...].astype(F32), nxt], axis=0)


def _shift(ext, off, ts):
    n = ext.shape[0]
    if off == 0:
        return ext[HALO:HALO + ts]
    return pltpu.roll(ext, (-off) % n, 0)[HALO:HALO + ts]


def _rmsnorm_fwd(name, x, g):
    S, D = x.shape
    ts = _tile(S, 512)

    def body(x_ref, g_ref, o_ref):
        xv = x_ref[...]
        r = lax.rsqrt(jnp.mean(xv * xv, axis=-1, keepdims=True) + EPS)
        o_ref[...] = (xv * r * g_ref[...]).astype(BF16)

    return _rows(name, S, ts,
                 [(x, (ts, D), lambda i: (i, 0)), (g, (1, D), lambda i: (0, 0))],
                 [(jax.ShapeDtypeStruct((S, D), BF16), (ts, D), lambda i: (i, 0))], body)[0]


def _rmsnorm_bwd_tile(dh, x, g):
    r = lax.rsqrt(jnp.mean(x * x, axis=-1, keepdims=True) + EPS)
    xhat = x * r
    dxn = dh * g
    dx = r * (dxn - xhat * jnp.mean(dxn * xhat, axis=-1, keepdims=True))
    return dx, dh * xhat


def _ffn_fwd(tag, x, h, wg, wu, wd, extras, outs, finish):
    S = x.shape[0]
    tm = _tile(S, 1024)
    tn = 1408

    def epi_up(i, accs, ex, out):
        a = accs[0][...]
        b = accs[1][...]
        s = _sigmoid(a)
        sa = a * s
        out[0][...] = sa.astype(BF16)
        out[1][...] = (b * (s * (1.0 + a * (1.0 - s)))).astype(BF16)
        out[2][...] = (sa * b).astype(BF16)

    sds = jax.ShapeDtypeStruct((S, D_FF), BF16)
    a, b, f = _fused_mm(f"{tag}_up", S, D_FF, D_MODEL, tm, tn, D_MODEL,
                        [(h, "mk"), (wg, "kn"), (wu, "kn")], [(0, 1, 0), (0, 2, 1)], [],
                        [(sds, (tm, tn), _mn)] * 3, epi_up)

    def epi_down(i, accs, ex, out):
        finish(i, ex[0][...] + 0.5 * accs[0][...], ex[1:], out)

    if callable(wd):
        wd = wd(f)
    res = _fused_mm(f"{tag}_down", S, D_MODEL, D_FF, tm, D_MODEL, 1408,
                    [(f, "mk"), (wd, "kn")], [(0, 1, 0)], [(x, (tm, D_MODEL), _mn)] + extras(tm), outs(tm), epi_down)
    return res, a, b, f


def _rmsnorm_tile(xv, g):
    return (xv * lax.rsqrt(jnp.mean(xv * xv, axis=-1, keepdims=True) + EPS) * g).astype(BF16)


def _conv_taps(ext, w_ref, ts):
    acc = None
    for j in range(CONV_W):
        term = w_ref[j:j + 1, :] * _shift(ext, j - 2, ts)
        acc = term if acc is None else acc + term
    return acc


def _l2norm_heads(s, scale):
    outs = []
    for h in range(GDN_H):
        sh = s[:, h * GDN_DK:(h + 1) * GDN_DK]
        outs.append(sh * (lax.rsqrt(jnp.sum(sh * sh, axis=-1, keepdims=True) + EPS) * scale))
    return jnp.concatenate(outs, axis=-1)


def _conv_fwd(name, p, colblk, w, bias, mode):
    S = p.shape[0]
    ts = _tile(S, 512)
    n_tiles = S // ts
    C = w.shape[1]

    def body(main, prev, nxt, w_ref, b_ref, o_ref):
        i = pl.program_id(0)
        c = _conv_taps(_ext(main, prev, nxt, i, n_tiles), w_ref, ts)
        if mode == "bias":
            o_ref[...] = c + b_ref[...]
        else:
            s = c * _sigmoid(c)
            if mode == "q":
                s = _l2norm_heads(s, GDN_DK ** -0.5)
            elif mode == "k":
                s = _l2norm_heads(s, 1.0)
            o_ref[...] = s

    ins = _halo_ins(p, S, ts, C, colblk) + [(w, (CONV_W, C), lambda i: (0, 0)), (bias, (1, C), lambda i: (0, 0))]
    return _rows(name, S, ts, ins, [(jax.ShapeDtypeStruct((S, C), F32), (ts, C), lambda i: (i, 0))], body)[0]


def _rg_gate_terms(pre, xc, prm_ref, d):
    r = _sigmoid(pre[:, d * 1024:d * 1024 + RG_W] + prm_ref[2 * d:2 * d + 1, :])
    ig = _sigmoid(pre[:, d * 1024 + RG_W:(d + 1) * 1024] + prm_ref[2 * d + 1:2 * d + 2, :])
    sp = _softplus(-prm_ref[4 + d:5 + d, :])
    log_a = -RG_C * r * sp
    a = jnp.exp(log_a)
    t = jnp.tanh(log_a)
    sq = jnp.sqrt(-2.0 * t / (1.0 - t))
    return r, ig, sp, a, sq


def _rg_gates_fwd(xc, bd, prm):
    S = xc.shape[0]
    tm = _tile(S, 256)

    def epi(i, accs, ex, out):
        pre = accs[0][...]
        xv = ex[0][...]
        for d in range(2):
            r, ig, sp, a, sq = _rg_gate_terms(pre, xv, ex[1], d)
            out[2 * d][...] = a
            out[2 * d + 1][...] = sq * ig * xv

    sds = jax.ShapeDtypeStruct((S, RG_W), F32)
    blk = (tm, RG_W)
    im = lambda i, j: (i, 0)
    return _fused_mm("rg_gates_fwd", S, 4 * RG_W, RG_W, tm, 4 * RG_W, RG_W,
                     [(xc, "mk"), (bd, "kn")], [(0, 1, 0)],
                     [(xc, blk, im), (prm, (8, RG_W), _row0)], [(sds, blk, im)] * 4, epi)


SUBLANES = 8


def _scan_rows(a, b, reverse):
    rows = lax.broadcasted_iota(jnp.int32, a.shape, 0)
    s = 1
    while s < SUBLANES:
        shift = SUBLANES - s if reverse else s
        a_sh = pltpu.roll(a, shift, 0)
        b_sh = pltpu.roll(b, shift, 0)
        valid = (rows < SUBLANES - s) if reverse else (rows >= s)
        b = jnp.where(valid, a * b_sh + b, b)
        a = jnp.where(valid, a * a_sh, a)
        s *= 2
    return a, b


def _rg_scan(name, a_f, b_f, a_b, b_b):
    S, C = a_f.shape
    ts = _tile(S, 512)
    n_tiles = S // ts

    def body(af, bf, ab, bb, hf, hb, carry):
        @pl.when(pl.program_id(0) == 0)
        def _():
            carry[...] = jnp.zeros_like(carry)

        n_sub = ts // SUBLANES

        def step(j, c):
            cf, cb = c
            r0 = pl.multiple_of(j * SUBLANES, SUBLANES)
            cum_a, h0 = _scan_rows(af[pl.ds(r0, SUBLANES), :], bf[pl.ds(r0, SUBLANES), :], False)
            h = h0 + cum_a * cf
            hf[pl.ds(r0, SUBLANES), :] = h
            cf = h[SUBLANES - 1:SUBLANES, :]
            r1 = pl.multiple_of((n_sub - 1 - j) * SUBLANES, SUBLANES)
            cum_a, h0 = _scan_rows(ab[pl.ds(r1, SUBLANES), :], bb[pl.ds(r1, SUBLANES), :], True)
            h = h0 + cum_a * cb
            hb[pl.ds(r1, SUBLANES), :] = h
            cb = h[0:1, :]
            return cf, cb

        cf, cb = lax.fori_loop(0, n_sub, step, (carry[0:1, :], carry[1:2, :]), unroll=4)
        carry[0:1, :] = cf
        carry[1:2, :] = cb

    fw = lambda i: (i, 0)
    bw = lambda i: (n_tiles - 1 - i, 0)
    sds = jax.ShapeDtypeStruct((S, C), F32)
    return _rows(name, S, ts,
                 [(a_f, (ts, C), fw), (b_f, (ts, C), fw), (a_b, (ts, C), bw), (b_b, (ts, C), bw)],
                 [(sds, (ts, C), fw), (sds, (ts, C), bw)], body, scratch=[pltpu.VMEM((8, C), F32)])


def _tri_masks():
    ri = lax.broadcasted_iota(jnp.int32, (CHUNK, CHUNK), 0)
    ci = lax.broadcasted_iota(jnp.int32, (CHUNK, CHUNK), 1)
    return ri, ci


def _gdn_prep_fwd(p, prm):
    S = p.shape[0]
    ts = _tile(S, 512)

    def body(p_ref, prm_ref, o_ref):
        raw = p_ref[...].astype(F32)
        lane = lax.broadcasted_iota(jnp.int32, (1, 128), 1)
        g = -jnp.exp(prm_ref[0:1, :]) * _softplus(raw + prm_ref[1:2, :])
        g = jnp.where((lane >= 8) & (lane < 16), g, 0.0)
        beta = _sigmoid(raw)
        ri, ci = _tri_masks()
        lower = (ri >= ci).astype(F32)
        upper = (ri <= ci).astype(F32)
        for c in range(ts // CHUNK):
            rows = slice(c * CHUNK, (c + 1) * CHUNK)
            gch = g[rows]
            gc = jnp.where(lane < 12, _dot(lower, gch, 1, 0, HI), _dot(upper, gch, 1, 0, HI))
            o_ref[rows, :] = jnp.where(lane < 8, beta[rows], gc)

    return _rows("gdn_prep_fwd", S, ts,
                 [(p, (ts, 128), lambda i: (i, COL_BA // 128)), (prm, (8, 128), lambda i: (0, 0))],
                 [(jax.ShapeDtypeStruct((S, 128), F32), (ts, 128), lambda i: (i, 0))], body)[0]


def _bdot(a, b, ca, cb):
    return _dot(a.astype(BF16), b.astype(BF16), ca, cb)


GDN_W = GDN_H * GDN_DK
GDN_TS = 256
LOCAL_CHUNKS = 2

def _gdn_decay(bg_ref, gcr_ref, c, rows, r0, col, rev, ri, ci):
    beta = bg_ref[rows, col:col + 1]
    gc = bg_ref[rows, 8 + col:9 + col]
    last = 0 if rev else CHUNK - 1
    gl = bg_ref[pl.ds(r0 + last, 1), 8 + col:9 + col]
    out = dict(beta=beta, gc=gc, gl=gl, eg=jnp.exp(gc), egl=jnp.exp(gl - gc), cd=jnp.exp(gl))
    if gcr_ref is not None:
        incl = (ri <= ci) if rev else (ri >= ci)
        out["strict"] = (ri < ci) if rev else (ri > ci)
        out["dm"] = jnp.where(incl, jnp.exp(jnp.where(incl, gc - gcr_ref[c, col:col + 1, :], 0.0)), 0.0)
    return out


def _dir_tile(d, n_tiles, flip):
    if (d == 1) != flip:
        return lambda i: n_tiles - 1 - i
    return lambda i: i


def _gdn_local_fwd(q, k, v, bg, gcr):
    S = q.shape[0]
    ts = _tile(S, GDN_TS)
    ncb = ts // CHUNK
    nch = S // CHUNK

    def body(q_ref, k_ref, v_ref, bg_ref, gcr_ref, *out_refs):
        ri, ci = _tri_masks()
        eye = (ri == ci).astype(F32)
        outs = (out_refs[0:6], out_refs[6:12])
        cd_ref = out_refs[12]

        def chunk(cc, carry):
            chains = []
            for c in (LOCAL_CHUNKS * cc + j for j in range(LOCAL_CHUNKS)):
                r0 = pl.multiple_of(c * CHUNK, CHUNK)
                rows = pl.ds(r0, CHUNK)
                for h in range(GDN_H):
                    cols = slice(h * GDN_DK, (h + 1) * GDN_DK)
                    qh, kh, vh = q_ref[rows, cols], k_ref[rows, cols], v_ref[rows, cols]
                    both = _bdot(jnp.concatenate([qh, kh], axis=0), kh, 1, 1)
                    for d in range(2):
                        chains.append(dict(c=c, r0=r0, rows=rows, h=h, d=d, cols=cols, qh=qh, kh=kh, vh=vh,
                                           qk=both[0:CHUNK], kk=both[CHUNK:2 * CHUNK]))
            for ch in chains:
                m = _gdn_decay(bg_ref, gcr_ref, ch["c"], ch["rows"], ch["r0"], ch["d"] * GDN_H + ch["h"], ch["d"] == 1,
                               ri, ci)
                ch["m"] = m
                ch["x"] = -jnp.where(m["strict"], m["beta"] * ch["kk"] * m["dm"], 0.0)
                ch["t"] = eye + ch["x"]
            for ch in chains:
                ch["pw"] = _bdot(ch["x"], ch["x"], 1, 0)
            for level in range(1, 6):
                last_level = level == 5
                for ch in chains:
                    rhs = ch["t"] if last_level else jnp.concatenate([ch["t"], ch["pw"]], axis=1)
                    ch["prod"] = _bdot(ch["pw"], rhs, 1, 0)
                for ch in chains:
                    ch["t"] = ch["t"] + ch["prod"][:, 0:CHUNK]
                    if not last_level:
                        ch["pw"] = ch["prod"][:, CHUNK:2 * CHUNK]
            for ch in chains:
                m = ch["m"]
                rhs = jnp.concatenate([ch["vh"] * m["beta"], ch["kh"] * (m["beta"] * m["eg"])], axis=1)
                ch["uw"] = _bdot(ch["t"], rhs, 1, 0)
            for ch in chains:
                u_ref, w_ref, a_ref, t_ref, qd_ref, kd_ref = outs[ch["d"]]
                m = ch["m"]
                c, rows = ch["c"], ch["rows"]
                col = ch["d"] * GDN_H + ch["h"]
                u_ref[rows, ch["cols"]] = ch["uw"][:, 0:GDN_DK]
                w_ref[rows, ch["cols"]] = ch["uw"][:, GDN_DK:2 * GDN_DK].astype(BF16)
                a_ref[c, ch["h"]] = (ch["qk"] * m["dm"]).astype(BF16)
                t_ref[c, ch["h"]] = _bdot(ch["t"], eye, 0, 0).astype(BF16)
                qd_ref[rows, ch["cols"]] = (ch["qh"] * m["eg"]).astype(BF16)
                kd_ref[rows, ch["cols"]] = (ch["kh"] * m["egl"]).astype(BF16)
                cd_ref[c, col:col + 1, :] = jnp.broadcast_to(m["cd"], (1, 128))
            return carry

        lax.fori_loop(0, ncb // LOCAL_CHUNKS, chunk, 0)

    im = lambda i: (i, 0)
    im4 = lambda i: (i, 0, 0, 0)
    ins = [(q, (ts, GDN_W), im), (k, (ts, GDN_W), im), (v, (ts, GDN_W), im), (bg, (ts, 128), im),
           (gcr, (ncb, 8, CHUNK), lambda i: (i, 0, 0))]
    per_dir = [(jax.ShapeDtypeStruct((S, GDN_W), F32), (ts, GDN_W), im),
               (jax.ShapeDtypeStruct((S, GDN_W), BF16), (ts, GDN_W), im),
               (jax.ShapeDtypeStruct((nch, GDN_H, CHUNK, CHUNK), BF16), (ncb, GDN_H, CHUNK, CHUNK), im4),
               (jax.ShapeDtypeStruct((nch, GDN_H, CHUNK, CHUNK), BF16), (ncb, GDN_H, CHUNK, CHUNK), im4),
               (jax.ShapeDtypeStruct((S, GDN_W), BF16), (ts, GDN_W), im),
               (jax.ShapeDtypeStruct((S, GDN_W), BF16), (ts, GDN_W), im)]
    cd_out = (jax.ShapeDtypeStruct((nch, 8, 128), F32), (ncb, 8, 128), lambda i: (i, 0, 0))
    res = _rows("gdn_local_fwd", S, ts, ins, per_dir * 2 + [cd_out], body)
    return res[0:6], res[6:12], res[12]


def _gdn_scan_fwd(loc):
    S = loc[0][0].shape[0]
    ts = _tile(S, GDN_TS)
    n_tiles = S // ts
    ncb = ts // CHUNK
    nch = S // CHUNK

    def body(*refs):
        ins = (refs[0:6], refs[6:12])
        outs = (refs[12:15], refs[15:18])
        state = refs[18]

        @pl.when(pl.program_id(0) == 0)
        def _():
            state[...] = jnp.zeros_like(state)

        def chunk(cc, carry):
            chains = []
            for d in range(2):
                c = cc if d == 0 else ncb - 1 - cc
                rows = pl.ds(pl.multiple_of(c * CHUNK, CHUNK), CHUNK)
                for h in range(GDN_H):
                    cols = slice(h * GDN_DK, (h + 1) * GDN_DK)
                    chains.append(dict(d=d, h=h, c=c, rows=rows, cols=cols, st=state[d * GDN_H + h]))
            for ch in chains:
                qd_ref, kd_ref, u_ref, w_ref, a_ref, cd_ref = ins[ch["d"]]
                rows, cols = ch["rows"], ch["cols"]
                lhs = jnp.concatenate([w_ref[rows, cols], qd_ref[rows, cols]], axis=0)
                ch["ws_qs"] = _dot(lhs, ch["st"].astype(BF16), 1, 0)
            for ch in chains:
                qd_ref, kd_ref, u_ref, w_ref, a_ref, cd_ref = ins[ch["d"]]
                rows, cols = ch["rows"], ch["cols"]
                vn = u_ref[rows, cols] - ch["ws_qs"][0:CHUNK]
                vnb = vn.astype(BF16)
                ch["vn"] = vn
                ch["avn"] = _dot(a_ref[ch["c"], ch["h"]], vnb, 1, 0)
                ch["kvn"] = _dot(kd_ref[rows, cols], vnb, 0, 0)
            for ch in chains:
                o_ref, vn_ref, s_ref = outs[ch["d"]]
                cd_ref = ins[ch["d"]][5]
                rows, cols = ch["rows"], ch["cols"]
                col = ch["d"] * GDN_H + ch["h"]
                o_ref[rows, cols] = ch["ws_qs"][CHUNK:2 * CHUNK] + ch["avn"]
                vn_ref[rows, cols] = ch["vn"].astype(BF16)
                s_ref[ch["c"], ch["h"]] = ch["st"].astype(BF16)
                state[ch["d"] * GDN_H + ch["h"]] = ch["st"] * cd_ref[ch["c"], col:col + 1, :] + ch["kvn"]
            return carry

        lax.fori_loop(0, ncb, chunk, 0)

    ins, outs = [], []
    for d in range(2):
        tix = _dir_tile(d, n_tiles, False)
        im = lambda i, tix=tix: (tix(i), 0)
        im4 = lambda i, tix=tix: (tix(i), 0, 0, 0)
        u, w, a, _, qd, kd = loc[d]
        ins += [(qd, (ts, GDN_W), im), (kd, (ts, GDN_W), im), (u, (ts, GDN_W), im), (w, (ts, GDN_W), im),
                (a, (ncb, GDN_H, CHUNK, CHUNK), im4), (loc[2], (ncb, 8, 128), lambda i, tix=tix: (tix(i), 0, 0))]
        outs += [(jax.ShapeDtypeStruct((S, GDN_W), F32), (ts, GDN_W), im),
                 (jax.ShapeDtypeStruct((S, GDN_W), BF16), (ts, GDN_W), im),
                 (jax.ShapeDtypeStruct((nch, GDN_H, GDN_DK, GDN_DK), BF16), (ncb, GDN_H, GDN_DK, GDN_DK), im4)]
    res = _rows("gdn_scan_fwd", S, ts, ins, outs, body, scratch=[pltpu.VMEM((2 * GDN_H, GDN_DK, GDN_DK), F32)])
    return res[0:3], res[3:6]


def _gelu(x):
    c = math.sqrt(2.0 / math.pi)
    t = jnp.tanh(c * (x + 0.044715 * x * x * x))
    return 0.5 * x * (1.0 + t), t


def _mix_out_fwd(h_f, h_b, o_f, o_b, p, gn):
    S = h_f.shape[0]
    ts = _tile(S, 512)

    def body(hf, hb, of, ob, gate, z, gn_ref, y_ref):
        ge, _ = _gelu(gate[...].astype(F32))
        y_ref[:, 0:RG_W] = ((hf[...] + hb[...]) * ge).astype(BF16)
        o = of[...] + ob[...]
        zv = z[...].astype(F32)
        sz = zv * _sigmoid(zv)
        for h in range(GDN_H):
            cols = slice(h * GDN_DK, (h + 1) * GDN_DK)
            oh = o[:, cols]
            n = oh * lax.rsqrt(jnp.mean(oh * oh, axis=-1, keepdims=True) + EPS) * gn_ref[...]
            y_ref[:, RG_W + h * GDN_DK:RG_W + (h + 1) * GDN_DK] = (n * sz[:, cols]).astype(BF16)

    blk = (ts, RG_W)
    im = lambda i: (i, 0)
    ins = [(h_f, blk, im), (h_b, blk, im), (o_f, blk, im), (o_b, blk, im),
           (p, blk, lambda i: (i, 1)), (p, blk, lambda i: (i, 5)), (gn, (1, GDN_DK), lambda i: (0, 0))]
    return _rows("mix_out_fwd", S, ts, ins,
                 [(jax.ShapeDtypeStruct((S, D_MODEL), BF16), (ts, D_MODEL), im)], body)[0]


def _block_diag(w):
    n = w.shape[0]
    return jnp.einsum("nij,nm->nimj", w, jnp.eye(n, dtype=w.dtype)).reshape(n * w.shape[1], n * w.shape[2])


def _rg_bd(a_w, x_w):
    return jnp.concatenate([_block_diag(a_w[0]), _block_diag(x_w[0]), _block_diag(a_w[1]), _block_diag(x_w[1])],
                           axis=1).astype(BF16)


def _rg_prm(ba, bx, lam):
    return jnp.concatenate([ba[0:1], bx[0:1], ba[1:2], bx[1:2], lam, jnp.zeros((2, RG_W), F32)], axis=0)


def _gdn_prm(a_log, dt_bias):
    rows = jnp.zeros((8, 128), F32)
    rows = rows.at[0, 8:16].set(a_log.reshape(-1))
    return rows.at[1, 8:16].set(dt_bias.reshape(-1))


def _gc_rows(bg):
    S = bg.shape[0]
    return bg[:, 8:16].reshape(S // CHUNK, CHUNK, 8).transpose(0, 2, 1)


def _layer_fwd(x0, target, W, more=None):
    S = x0.shape[0]
    R = {}
    R["h1"] = _rmsnorm_fwd("rms1", x0, W["ffn1_norm"])
    if more is not None:
        W = {**W, **more("ffn1_up", R["h1"])}
    late_wd = {}

    def ffn1_wd(after):
        late_wd.update(more("ffn1_down", after))
        return late_wd["ffn1_w_down"]

    sd_x = jax.ShapeDtypeStruct((S, D_MODEL), F32)
    sd_h = jax.ShapeDtypeStruct((S, D_MODEL), BF16)

    def norm_after(gain):
        extras = lambda t: [(gain, (1, D_MODEL), _row0)]
        outs = lambda t: [(sd_x, (t, D_MODEL), _mn), (sd_h, (t, D_MODEL), _mn)]

        def finish(i, xo, ex, out):
            out[0][...] = xo
            out[1][...] = _rmsnorm_tile(xo, ex[0][...])

        return extras, outs, finish

    (R["x1"], R["h2"]), R["a1"], R["b1"], R["f1"] = _ffn_fwd(
        "ffn1", x0, R["h1"], W["ffn1_w_gate"], W["ffn1_w_up"], ffn1_wd if more is not None else W["ffn1_w_down"],
        *norm_after(W["mix_norm"]))
    if more is not None:
        W = {**W, **late_wd, **more("mixer", R["x1"])}
    tm = _tile(S, 512)
    tmp = _tile(S, 1024)
    R["p"] = _fused_mm("in_proj", S, D_IN_PAD, D_MODEL, tmp, 640, D_MODEL, [(R["h2"], "mk"), (W["w_in"], "kn")],
                       [(0, 1, 0)], [], [(jax.ShapeDtypeStruct((S, D_IN_PAD), BF16), (tmp, 640), _mn)],
                       lambda i, accs, ex, out: out[0].__setitem__(Ellipsis, accs[0][...].astype(BF16)))[0]
    p = R["p"]
    R["xc"] = _conv_fwd("rg_conv_fwd", p, 0, W["rg_conv_w"], W["rg_conv_b"], "bias")
    R["bd"] = _rg_bd(W["rg_gate_a_w"], W["rg_gate_x_w"])
    R["rg_prm"] = _rg_prm(W["rg_gate_a_b"], W["rg_gate_x_b"], W["rg_lambda"])
    a_f, b_f, a_b, b_b = _rg_gates_fwd(R["xc"], R["bd"], R["rg_prm"])
    R["a_f"], R["a_b"] = a_f, a_b
    R["h_f"], R["h_b"] = _rg_scan("rg_scan_fwd", a_f, b_f, a_b, b_b)
    zero_b = jnp.zeros((1, RG_W), F32)
    cw = W["gdn_conv_w"]
    R["q"] = _conv_fwd("gdn_conv_q", p, 2, cw[:, 0:512], zero_b, "q")
    R["k"] = _conv_fwd("gdn_conv_k", p, 3, cw[:, 512:1024], zero_b, "k")
    R["v"] = _conv_fwd("gdn_conv_v", p, 4, cw[:, 1024:1536], zero_b, "v")
    R["gdn_prm"] = _gdn_prm(W["gdn_a_log"], W["gdn_dt_bias"])
    R["bg"] = _gdn_prep_fwd(p, R["gdn_prm"])
    R["gcr"] = _gc_rows(R["bg"])
    R["gdn_loc"] = _gdn_local_fwd(R["q"], R["k"], R["v"], R["bg"], R["gcr"])
    R["gdn_fwd"] = _gdn_scan_fwd(R["gdn_loc"])
    R["o_f"], R["o_b"] = R["gdn_fwd"][0][0], R["gdn_fwd"][1][0]
    R["y"] = _mix_out_fwd(R["h_f"], R["h_b"], R["o_f"], R["o_b"], p, W["gdn_norm"])
    def epi_out(i, accs, ex, out):
        x2 = ex[0][...] + accs[0][...]
        out[0][...] = x2
        out[1][...] = _rmsnorm_tile(x2, ex[1][...])

    R["x2"], R["h3"] = _fused_mm("out_proj", S, D_MODEL, D_MODEL, tm, D_MODEL, D_MODEL,
                                 [(R["y"], "mk"), (W["w_out"], "kn")], [(0, 1, 0)],
                                 [(R["x1"], (tm, D_MODEL), _mn), (W["ffn2_norm"], (1, D_MODEL), _row0)],
                                 [(sd_x, (tm, D_MODEL), _mn), (sd_h, (tm, D_MODEL), _mn)], epi_out)
    if more is not None:
        W = {**W, **more("ffn2", R["x2"])}

    def loss_finish(i, xo, ex, out):
        gv = ex[1][...]
        r = lax.rsqrt(jnp.mean(xo * xo, axis=-1, keepdims=True) + EPS)
        err = xo * r * gv - ex[0][...]
        dx, dgt = _rmsnorm_bwd_tile(err * (1.0 / D_MODEL), xo, gv)
        out[0][...] = dx
        _colsum_into(out[1], i, jnp.zeros((8, 128), F32) + jnp.sum(err * err) * (0.5 / D_MODEL))
        _colsum_into(out[2], i, jnp.sum(dgt, axis=0, keepdims=True))

    (R["dx3"], R["loss"], R["d_final_norm"]), R["a2"], R["b2"], R["f2"] = _ffn_fwd(
        "ffn2", R["x2"], R["h3"], W["ffn2_w_gate"], W["ffn2_w_up"], W["ffn2_w_down"],
        lambda t: [(target, (t, D_MODEL), _mn), (W["final_norm"], (1, D_MODEL), _row0)],
        lambda t: [(sd_x, (t, D_MODEL), _mn), (jax.ShapeDtypeStruct((8, 128), F32), (8, 128), _row0),
                   (jax.ShapeDtypeStruct((1, D_MODEL), F32), (1, D_MODEL), _row0)],
        loss_finish)
    R["W"] = W
    return R


def _colsum_into(ref, i, val):
    @pl.when(i == 0)
    def _():
        ref[...] = val

    @pl.when(i > 0)
    def _():
        ref[...] += val


def _ffn_bwd(tag, dout, x, g, h, a, b, f, wg, wu, wd, emit):
    S = x.shape[0]
    tm = _tile(S, 512)
    tk_s = _tile(S, 1024)
    dwd = _fused_mm(f"{tag}_dw_down", D_FF, D_MODEL, S, 1408, D_MODEL, tk_s, [(f, "km"), (dout, "kn")], [(0, 1, 0)], [],
                    [(jax.ShapeDtypeStruct((D_FF, D_MODEL), BF16), (1408, D_MODEL), _mn)],
                    lambda i, accs, ex, out: out[0].__setitem__(Ellipsis, (0.5 * accs[0][...]).astype(BF16)))[0]
    emit(down=dwd)

    def epi_act(i, accs, ex, out):
        df = 0.5 * accs[0][...]
        out[0][...] = (df * ex[1][...].astype(F32)).astype(BF16)
        out[1][...] = (df * ex[0][...].astype(F32)).astype(BF16)

    sds = jax.ShapeDtypeStruct((S, D_FF), BF16)
    da, db = _fused_mm(f"{tag}_dact", S, D_FF, D_MODEL, tm, 1408, D_MODEL, [(dout, "mk"), (wd, "nk")], [(0, 1, 0)],
                       [(a, (tm, 1408), _mn), (b, (tm, 1408), _mn)], [(sds, (tm, 1408), _mn)] * 2, epi_act)

    def epi_w2(i, accs, ex, out):
        out[0][...] = accs[0][...].astype(BF16)
        out[1][...] = accs[1][...].astype(BF16)

    sdw = jax.ShapeDtypeStruct((D_MODEL, D_FF), BF16)
    dwg, dwu = _fused_mm(f"{tag}_dw_up", D_MODEL, D_FF, S, D_MODEL, 1408, tk_s,
                         [(h, "km"), (da, "kn"), (db, "kn")], [(0, 1, 0), (0, 2, 1)], [],
                         [(sdw, (D_MODEL, 1408), _mn)] * 2, epi_w2)
    tok = emit(gate=dwg, up=dwu)
    if tok is not None:
        g = g + tok

    def epi_dx(i, accs, ex, out):
        dx, dgt = _rmsnorm_bwd_tile(accs[0][...], ex[0][...], ex[1][...])
        out[0][...] = ex[2][...] + dx
        _colsum_into(out[1], i, jnp.sum(dgt, axis=0, keepdims=True))

    tmx = _tile(S, 1024)
    dx, dg = _fused_mm(f"{tag}_dx", S, D_MODEL, D_FF, tmx, D_MODEL, 1408,
                       [(da, "mk"), (wg, "nk"), (db, "mk"), (wu, "nk")], [(0, 1, 0), (2, 3, 0)],
                       [(x, (tmx, D_MODEL), _mn), (g, (1, D_MODEL), _row0), (dout, (tmx, D_MODEL), _mn)],
                       [(jax.ShapeDtypeStruct((S, D_MODEL), F32), (tmx, D_MODEL), _mn),
                        (jax.ShapeDtypeStruct((1, D_MODEL), F32), (1, D_MODEL), _row0)], epi_dx)
    return dx, dg


def _mix_out_bwd(dy, h_f, h_b, o_f, o_b, p, gn):
    S = dy.shape[0]
    ts = _tile(S, 512)
    c0 = math.sqrt(2.0 / math.pi)

    def body(dy_ref, hf, hb, of, ob, gate, z, gn_ref, dhr_ref, dgate_ref, do_ref, dz_ref, dgn_ref):
        i = pl.program_id(0)
        gv = gate[...].astype(F32)
        ge, t = _gelu(gv)
        dy_rg = dy_ref[:, 0:RG_W]
        dhr_ref[...] = dy_rg * ge
        dgelu = 0.5 * (1.0 + t) + 0.5 * gv * (1.0 - t * t) * c0 * (1.0 + 3.0 * 0.044715 * gv * gv)
        dgate_ref[...] = (dy_rg * (hf[...] + hb[...]) * dgelu).astype(BF16)
        o = of[...] + ob[...]
        zv = z[...].astype(F32)
        sig = _sigmoid(zv)
        gnv = gn_ref[...]
        dgn = jnp.zeros((1, GDN_DK), F32)
        for h in range(GDN_H):
            cols = slice(h * GDN_DK, (h + 1) * GDN_DK)
            oh = o[:, cols]
            r = lax.rsqrt(jnp.mean(oh * oh, axis=-1, keepdims=True) + EPS)
            ohat = oh * r
            dyh = dy_ref[:, RG_W + h * GDN_DK:RG_W + (h + 1) * GDN_DK]
            zh = zv[:, cols]
            sh = sig[:, cols]
            dn = dyh * zh * sh
            dz_ref[:, cols] = (dyh * ohat * gnv * (sh * (1.0 + zh * (1.0 - sh)))).astype(BF16)
            dxn = dn * gnv
            do_ref[:, cols] = r * (dxn - ohat * jnp.mean(dxn * ohat, axis=-1, keepdims=True))
            dgn = dgn + jnp.sum(dn * ohat, axis=0, keepdims=True)
        _colsum_into(dgn_ref, i, dgn)

    blk = (ts, RG_W)
    im = lambda i: (i, 0)
    z0 = lambda i: (0, 0)
    ins = [(dy, (ts, D_MODEL), im), (h_f, blk, im), (h_b, blk, im), (o_f, blk, im), (o_b, blk, im),
           (p, blk, lambda i: (i, 1)), (p, blk, lambda i: (i, 5)), (gn, (1, GDN_DK), z0)]
    outs = [(jax.ShapeDtypeStruct((S, RG_W), F32), blk, im), (jax.ShapeDtypeStruct((S, RG_W), BF16), blk, im),
            (jax.ShapeDtypeStruct((S, RG_W), F32), blk, im), (jax.ShapeDtypeStruct((S, RG_W), BF16), blk, im),
            (jax.ShapeDtypeStruct((1, GDN_DK), F32), (1, GDN_DK), z0)]
    return _rows("mix_out_bwd", S, ts, ins, outs, body)


def _rg_scan_adj(name, a_up, b_up, a_dn, b_dn):
    S, C = a_up.shape
    ts = _tile(S, 512)
    n_tiles = S // ts

    def body(au, bu, ad, bd, mu_ref, lam_ref, carry):
        @pl.when(pl.program_id(0) == 0)
        def _():
            carry[...] = jnp.zeros_like(carry)

        n_sub = ts // SUBLANES
        rows = lax.broadcasted_iota(jnp.int32, (SUBLANES, C), 0)

        def half(a_ref, b_ref, out_ref, r0, c_in, reverse):
            a = a_ref[pl.ds(r0, SUBLANES), :]
            b = b_ref[pl.ds(r0, SUBLANES), :]
            cum_a, c0 = _scan_rows(a, a * b, reverse)
            c = c0 + cum_a * c_in
            edge = 0 if not reverse else SUBLANES - 1
            c_prev = jnp.where(rows == edge, c_in, pltpu.roll(c, SUBLANES - 1 if reverse else 1, 0))
            out_ref[pl.ds(r0, SUBLANES), :] = b + c_prev
            return c[0:1, :] if reverse else c[SUBLANES - 1:SUBLANES, :]

        def step(j, c):
            cu, cd = c
            cu = half(au, bu, mu_ref, pl.multiple_of(j * SUBLANES, SUBLANES), cu, False)
            cd = half(ad, bd, lam_ref, pl.multiple_of((n_sub - 1 - j) * SUBLANES, SUBLANES), cd, True)
            return cu, cd

        cu, cd = lax.fori_loop(0, n_sub, step, (carry[0:1, :], carry[1:2, :]), unroll=4)
        carry[0:1, :] = cu
        carry[1:2, :] = cd

    fw = lambda i: (i, 0)
    bw = lambda i: (n_tiles - 1 - i, 0)
    sds = jax.ShapeDtypeStruct((S, C), F32)
    return _rows(name, S, ts,
                 [(a_up, (ts, C), fw), (b_up, (ts, C), fw), (a_dn, (ts, C), bw), (b_dn, (ts, C), bw)],
                 [(sds, (ts, C), fw), (sds, (ts, C), bw)], body, scratch=[pltpu.VMEM((8, C), F32)])


def _halo_ex(arr, S, tm, width):
    per = tm // HALO
    last = S // HALO - 1
    return [
        (arr, (tm, width), lambda i, j: (i, 0)),
        (arr, (HALO, width), lambda i, j: (jnp.maximum(i * per - 1, 0), 0)),
        (arr, (HALO, width), lambda i, j: (jnp.minimum((i + 1) * per, last), 0)),
    ]


def _rg_gates_bwd(xc, bd, prm, lam_f, lam_b, h_f, h_b):
    S = xc.shape[0]
    tm = _tile(S, 256)
    n_tiles = S // tm

    def epi(i, accs, ex, out):
        pre = accs[0][...]
        xv = ex[0][...]
        prm_ref = ex[1]
        lams = (ex[2][...], ex[3][...])
        hprev = (_shift(_ext(ex[4], ex[5], ex[6], i, n_tiles), -1, tm),
                 _shift(_ext(ex[7], ex[8], ex[9], i, n_tiles), 1, tm))
        dxc = jnp.zeros_like(xv)
        rows = []
        dlam_rows = []
        for d in range(2):
            r, ig, sp, a, sq = _rg_gate_terms(pre, xv, prm_ref, d)
            lam = lams[d]
            da = lam * hprev[d]
            di = lam * sq * xv
            dxc = dxc + lam * sq * ig
            dsq = lam * ig * xv
            dlog_a = da * a - dsq * (a * a) / sq
            dpre_r = dlog_a * (-RG_C * sp) * r * (1.0 - r)
            dpre_i = di * ig * (1.0 - ig)
            out[0][:, d * 1024:d * 1024 + RG_W] = dpre_r.astype(BF16)
            out[0][:, d * 1024 + RG_W:(d + 1) * 1024] = dpre_i.astype(BF16)
            rows += [jnp.sum(dpre_r, axis=0, keepdims=True), jnp.sum(dpre_i, axis=0, keepdims=True)]
            dsp = jnp.sum(dlog_a * (-RG_C * r), axis=0, keepdims=True)
            dlam_rows.append(-dsp * _sigmoid(-prm_ref[4 + d:5 + d, :]))
        out[1][...] = dxc
        zero = jnp.zeros((2, RG_W), F32)
        _colsum_into(out[2], i, jnp.concatenate(rows + dlam_rows + [zero], axis=0))

    blk = (tm, RG_W)
    im = lambda i, j: (i, 0)
    extras = ([(xc, blk, im), (prm, (8, RG_W), _row0), (lam_f, blk, im), (lam_b, blk, im)]
              + _halo_ex(h_f, S, tm, RG_W) + _halo_ex(h_b, S, tm, RG_W))
    outs = [(jax.ShapeDtypeStruct((S, 4 * RG_W), BF16), (tm, 4 * RG_W), im),
            (jax.ShapeDtypeStruct((S, RG_W), F32), blk, im),
            (jax.ShapeDtypeStruct((8, RG_W), F32), (8, RG_W), _row0)]
    return _fused_mm("rg_gates_bwd", S, 4 * RG_W, RG_W, tm, 4 * RG_W, RG_W, [(xc, "mk"), (bd, "kn")], [(0, 1, 0)],
                     extras, outs, epi)


def _roll_rows(ext, off):
    if off == 0:
        return ext
    return pltpu.roll(ext, (-off) % ext.shape[0], 0)


def _conv_bwd(name, p, colblk, w, grads, mode):
    S = p.shape[0]
    ts = _tile(S, 512)
    n_tiles = S // ts
    C = w.shape[1]
    ng = len(grads)

    def body(*refs):
        p_refs = refs[0:3]
        g_refs = refs[3:3 + 3 * ng]
        w_ref = refs[3 + 3 * ng]
        dx_ref, dw_ref, db_ref = refs[4 + 3 * ng:]
        i = pl.program_id(0)
        ext_p = _ext(*p_refs, i, n_tiles)
        dn = _ext(*g_refs[0:3], i, n_tiles)
        for gi in range(1, ng):
            dn = dn + _ext(*g_refs[3 * gi:3 * gi + 3], i, n_tiles)
        if mode == "bias":
            dc = dn
        else:
            c = None
            for j in range(CONV_W):
                term = w_ref[j:j + 1, :] * _roll_rows(ext_p, j - 2)
                c = term if c is None else c + term
            sig = _sigmoid(c)
            s = c * sig
            if mode in ("q", "k"):
                scale = GDN_DK ** -0.5 if mode == "q" else 1.0
                parts = []
                for h in range(GDN_H):
                    cols = slice(h * GDN_DK, (h + 1) * GDN_DK)
                    sh = s[:, cols]
                    dnh = dn[:, cols]
                    rinv = lax.rsqrt(jnp.sum(sh * sh, axis=-1, keepdims=True) + EPS)
                    parts.append(scale * rinv * (dnh - sh * (rinv * rinv) * jnp.sum(dnh * sh, axis=-1, keepdims=True)))
                ds = jnp.concatenate(parts, axis=-1)
            else:
                ds = dn
            dc = ds * (sig * (1.0 + c * (1.0 - sig)))
        dx = None
        for j in range(CONV_W):
            term = w_ref[j:j + 1, :] * _shift(dc, 2 - j, ts)
            dx = term if dx is None else dx + term
        dx_ref[...] = dx.astype(BF16)
        dc_main = dc[HALO:HALO + ts]
        dw = jnp.concatenate([jnp.sum(dc_main * _shift(ext_p, j - 2, ts), axis=0, keepdims=True)
                              for j in range(CONV_W)], axis=0)
        _colsum_into(dw_ref, i, dw)
        _colsum_into(db_ref, i, jnp.sum(dc_main, axis=0, keepdims=True))

    ins = _halo_ins(p, S, ts, C, colblk)
    for garr in grads:
        ins += _halo_ins(garr, S, ts, C, 0)
    ins += [(w, (CONV_W, C), lambda i: (0, 0))]
    z0 = lambda i: (0, 0)
    outs = [(jax.ShapeDtypeStruct((S, C), BF16), (ts, C), lambda i: (i, 0)),
            (jax.ShapeDtypeStruct((CONV_W, C), F32), (CONV_W, C), z0),
            (jax.ShapeDtypeStruct((1, C), F32), (1, C), z0)]
    return _rows(name, S, ts, ins, outs, body)


def _gdn_scan_bwd(loc, do):
    S = do.shape[0]
    ts = _tile(S, GDN_TS)
    n_tiles = S // ts
    ncb = ts // CHUNK
    nch = S // CHUNK

    def body(*refs):
        ins = (refs[0:6], refs[6:12])
        outs = (refs[12:14], refs[14:16])
        dstate = refs[16]

        @pl.when(pl.program_id(0) == 0)
        def _():
            dstate[...] = jnp.zeros_like(dstate)

        def chunk(cc, carry):
            chains = []
            for d in range(2):
                c = ncb - 1 - cc if d == 0 else cc
                rows = pl.ds(pl.multiple_of(c * CHUNK, CHUNK), CHUNK)
                for h in range(GDN_H):
                    cols = slice(h * GDN_DK, (h + 1) * GDN_DK)
                    chains.append(dict(d=d, h=h, c=c, rows=rows, cols=cols, dsn=dstate[d * GDN_H + h]))
            for ch in chains:
                qd_ref, kd_ref, cd_ref, w_ref, a_ref, do_ref = ins[ch["d"]]
                rows, cols = ch["rows"], ch["cols"]
                dob = do_ref[rows, cols].astype(BF16)
                ch["dvn"] = (_dot(a_ref[ch["c"], ch["h"]], dob, 0, 0)
                             + _dot(kd_ref[rows, cols], ch["dsn"].astype(BF16), 1, 0))
                ch["qdo"] = _dot(qd_ref[rows, cols], dob, 0, 0)
            for ch in chains:
                w_ref = ins[ch["d"]][3]
                ch["wdvn"] = _dot(w_ref[ch["rows"], ch["cols"]], ch["dvn"].astype(BF16), 0, 0)
            for ch in chains:
                dvn_ref, ds_ref = outs[ch["d"]]
                cd_ref = ins[ch["d"]][2]
                col = ch["d"] * GDN_H + ch["h"]
                dvn_ref[ch["rows"], ch["cols"]] = ch["dvn"].astype(BF16)
                ds_ref[ch["c"], ch["h"]] = ch["dsn"].astype(BF16)
                dstate[ch["d"] * GDN_H + ch["h"]] = (ch["qdo"] + cd_ref[ch["c"], col:col + 1, :] * ch["dsn"]
                                                     - ch["wdvn"])
            return carry

        lax.fori_loop(0, ncb, chunk, 0)

    ins, outs = [], []
    for d in range(2):
        tix = _dir_tile(d, n_tiles, True)
        im = lambda i, tix=tix: (tix(i), 0)
        im4 = lambda i, tix=tix: (tix(i), 0, 0, 0)
        _, w, a, _, qd, kd = loc[d]
        ins += [(qd, (ts, GDN_W), im), (kd, (ts, GDN_W), im), (loc[2], (ncb, 8, 128), lambda i, tix=tix: (tix(i), 0, 0)),
                (w, (ts, GDN_W), im), (a, (ncb, GDN_H, CHUNK, CHUNK), im4), (do, (ts, GDN_W), im)]
        outs += [(jax.ShapeDtypeStruct((S, GDN_W), BF16), (ts, GDN_W), im),
                 (jax.ShapeDtypeStruct((nch, GDN_H, GDN_DK, GDN_DK), BF16), (ncb, GDN_H, GDN_DK, GDN_DK), im4)]
    res = _rows("gdn_scan_bwd", S, ts, ins, outs, body, scratch=[pltpu.VMEM((2 * GDN_H, GDN_DK, GDN_DK), F32)])
    return res[0:2], res[2:4]


def _gdn_local_bwd(q, k, v, bg, gcr, do, loc, fwd, adj):
    S = q.shape[0]
    ts = _tile(S, GDN_TS)
    ncb = ts // CHUNK

    def body(q_ref, k_ref, v_ref, bg_ref, gcr_ref, do_ref, *rest):
        per_dir = (rest[0:5], rest[5:10])
        dq_ref, dk_ref, dv_ref, dbg_ref, dbgr_ref = rest[10:15]
        ri, ci = _tri_masks()
        lane = lax.broadcasted_iota(jnp.int32, (CHUNK, 128), 1)
        rowi = lax.broadcasted_iota(jnp.int32, (CHUNK, 1), 0)
        ones8 = jnp.ones((SUBLANES, CHUNK), F32)

        def chunk(c, carry):
            r0 = pl.multiple_of(c * CHUNK, CHUNK)
            rows = pl.ds(r0, CHUNK)
            chains = []
            for h in range(GDN_H):
                cols = slice(h * GDN_DK, (h + 1) * GDN_DK)
                qh, kh, vh = q_ref[rows, cols], k_ref[rows, cols], v_ref[rows, cols]
                dob = do_ref[rows, cols].astype(BF16)
                both = _bdot(jnp.concatenate([qh, kh], axis=0), kh, 1, 1)
                for d in range(2):
                    chains.append(dict(h=h, d=d, cols=cols, qh=qh, kh=kh, vh=vh, dob=dob, qk=both[0:CHUNK],
                                       kk=both[CHUNK:2 * CHUNK], col=d * GDN_H + h))
            for ch in chains:
                m = _gdn_decay(bg_ref, gcr_ref, c, rows, r0, ch["col"], ch["d"] == 1, ri, ci)
                t_ref, s_ref, ds_ref, vn_ref, dvn_ref = per_dir[ch["d"]]
                h, cols = ch["h"], ch["cols"]
                ch["m"] = m
                ch["kb"] = ch["kh"] * m["beta"]
                ch["kbg"] = ch["kb"] * m["eg"]
                ch["t"] = t_ref[c, h]
                stb = s_ref[c, h]
                ch["dsn"] = ds_ref[c, h]
                vnb = vn_ref[rows, cols]
                dvnb = dvn_ref[rows, cols]
                ch["dcd"] = jnp.sum(jnp.sum(stb.astype(F32) * ch["dsn"].astype(F32), axis=1, keepdims=True),
                                    axis=0, keepdims=True)
                ch["dqd"] = _dot(ch["dob"], stb, 1, 1)
                ch["d_a"] = _dot(ch["dob"], vnb, 1, 1)
                ch["dkd"] = _bdot(vnb, ch["dsn"], 1, 1)
                ch["dw"] = -_dot(dvnb, stb, 1, 1)
                ch["dvb"] = _dot(ch["t"], dvnb, 1, 0)
                ch["d_t"] = _bdot(dvnb, ch["vh"] * m["beta"], 1, 1)
            for ch in chains:
                dwb = ch["dw"].astype(BF16)
                ch["d_t"] = ch["d_t"] + _bdot(dwb, ch["kbg"], 1, 1)
                ch["dkbg"] = _dot(ch["t"], dwb, 1, 0)
                ch["nn"] = ch["d_a"] * ch["m"]["dm"]
                ch["nn_q"] = _bdot(ch["nn"], ch["qh"], 0, 0)
                ch["nn_k"] = _bdot(ch["nn"], ch["kh"], 1, 0)
            for ch in chains:
                ch["x"] = _dot(ch["d_t"].astype(BF16), ch["t"], 1, 0)
            for ch in chains:
                d_l = -_dot(ch["t"], ch["x"].astype(BF16), 1, 0)
                ch["d_l"] = jnp.where(ch["m"]["strict"], d_l, 0.0)
                ch["mm"] = ch["d_l"] * ch["m"]["dm"]
            for ch in chains:
                m = ch["m"]
                ch["mm_kh"] = _bdot(ch["mm"], ch["kh"], 1, 0)
                ch["mm_kb"] = _bdot(ch["mm"], ch["kb"], 0, 0)
                l_mat = jnp.where(m["strict"], m["beta"] * ch["kk"] * m["dm"], 0.0)
                ch["e"] = ch["d_l"] * l_mat + ch["nn"] * ch["qk"]
                dbgr_ref[c, ch["col"]:ch["col"] + 1, :] = -_dot(ones8, ch["e"], 1, 0, HI)[0:1, :]
            acc_bg = jnp.zeros((CHUNK, 128), F32)
            acc = {}
            for ch in chains:
                m = ch["m"]
                beta, eg, egl = m["beta"], m["eg"], m["egl"]
                dkb = ch["mm_kh"] + ch["dkbg"] * eg
                dk_d = ch["mm_kb"] + ch["nn_q"] + ch["dkd"] * egl + dkb * beta
                dq_d = ch["nn_k"] + ch["dqd"] * eg
                dv_d = ch["dvb"] * beta
                dkd_kd = ch["dkd"] * (ch["kh"] * egl)
                dgc = (jnp.sum(ch["e"], axis=1, keepdims=True)
                       + jnp.sum(ch["dqd"] * (ch["qh"] * eg) - dkd_kd + ch["dkbg"] * ch["kbg"], axis=1, keepdims=True))
                dgl = jnp.sum(jnp.sum(dkd_kd, axis=1, keepdims=True), axis=0, keepdims=True) + ch["dcd"] * m["cd"]
                dgc = dgc + jnp.where(rowi == (0 if ch["d"] == 1 else CHUNK - 1), dgl, 0.0)
                dbeta = jnp.sum(dkb * ch["kh"] + ch["dvb"] * ch["vh"], axis=1, keepdims=True)
                acc_bg = acc_bg + jnp.where(lane == ch["col"], dbeta, 0.0) + jnp.where(lane == 8 + ch["col"], dgc, 0.0)
                if ch["d"] == 0:
                    acc[ch["h"]] = (dq_d, dk_d, dv_d)
                else:
                    dq0, dk0, dv0 = acc[ch["h"]]
                    dq_ref[rows, ch["cols"]] = dq0 + dq_d
                    dk_ref[rows, ch["cols"]] = dk0 + dk_d
                    dv_ref[rows, ch["cols"]] = dv0 + dv_d
            dbg_ref[rows, :] = acc_bg
            return carry

        lax.fori_loop(0, ncb, chunk, 0)

    im = lambda i: (i, 0)
    im4 = lambda i: (i, 0, 0, 0)
    blk = (ts, GDN_W)
    ins = [(q, blk, im), (k, blk, im), (v, blk, im), (bg, (ts, 128), im), (gcr, (ncb, 8, CHUNK), lambda i: (i, 0, 0)),
           (do, blk, im)]
    for d in range(2):
        ins += [(loc[d][3], (ncb, GDN_H, CHUNK, CHUNK), im4), (fwd[d][2], (ncb, GDN_H, GDN_DK, GDN_DK), im4),
                (adj[d][1], (ncb, GDN_H, GDN_DK, GDN_DK), im4), (fwd[d][1], blk, im), (adj[d][0], blk, im)]
    sds = jax.ShapeDtypeStruct((S, GDN_W), F32)
    outs = [(sds, blk, im), (sds, blk, im), (sds, blk, im), (jax.ShapeDtypeStruct((S, 128), F32), (ts, 128), im),
            (jax.ShapeDtypeStruct((S // CHUNK, 8, CHUNK), F32), (ncb, 8, CHUNK), lambda i: (i, 0, 0))]
    dq, dk, dv, dbg, dbg_rows = _rows("gdn_local_bwd", S, ts, ins, outs, body)
    dgc_cols = dbg_rows.transpose(0, 2, 1).reshape(S, 8)
    return dq, dk, dv, dbg + jnp.pad(dgc_cols, ((0, 0), (8, 112)))


def _gdn_prep_bwd(dbg_all, p, prm):
    S = p.shape[0]
    ts = _tile(S, 512)

    def body(dbg_ref, p_ref, prm_ref, dba_ref, dprm_ref):
        i = pl.program_id(0)
        raw = p_ref[...].astype(F32)
        dbg = dbg_ref[...]
        lane = lax.broadcasted_iota(jnp.int32, (1, 128), 1)
        is_g = (lane >= 8) & (lane < 16)
        ea = jnp.exp(prm_ref[0:1, :])
        arg = raw + prm_ref[1:2, :]
        g = jnp.where(is_g, -ea * _softplus(arg), 0.0)
        beta = _sigmoid(raw)
        dgc = jnp.where(is_g, dbg, 0.0)
        ri, ci = _tri_masks()
        lower = (ri >= ci).astype(F32)
        upper = (ri <= ci).astype(F32)
        dgs = []
        for c in range(ts // CHUNK):
            ch = dgc[c * CHUNK:(c + 1) * CHUNK]
            dgs.append(jnp.where(lane < 12, _dot(upper, ch, 1, 0, HI), _dot(lower, ch, 1, 0, HI)))
        dg = jnp.concatenate(dgs, axis=0)
        dalpha = jnp.where(is_g, dg * (-ea) * _sigmoid(arg), 0.0)
        dba_ref[...] = jnp.where(lane < 8, dbg * beta * (1.0 - beta), dalpha).astype(BF16)
        rows = jnp.concatenate([jnp.sum(dg * g, axis=0, keepdims=True), jnp.sum(dalpha, axis=0, keepdims=True),
                                jnp.zeros((6, 128), F32)], axis=0)
        _colsum_into(dprm_ref, i, rows)

    im = lambda i: (i, 0)
    z0 = lambda i: (0, 0)
    return _rows("gdn_prep_bwd", S, ts,
                 [(dbg_all, (ts, 128), im), (p, (ts, 128), lambda i: (i, COL_BA // 128)), (prm, (8, 128), z0)],
                 [(jax.ShapeDtypeStruct((S, 128), BF16), (ts, 128), im), (jax.ShapeDtypeStruct((8, 128), F32), (8, 128), z0)],
                 body)


def _mm_plain(name, M, N, K, tm, tn, tk, a, am, b, bm, dtype):
    return _fused_mm(name, M, N, K, tm, tn, tk, [(a, am), (b, bm)], [(0, 1, 0)], [],
                     [(jax.ShapeDtypeStruct((M, N), dtype), (tm, tn), _mn)],
                     lambda i, accs, ex, out: out[0].__setitem__(Ellipsis, accs[0][...].astype(dtype)))[0]


def _layer_bwd(x0, W, R, emit_big=None, emit_small=None):
    S = x0.shape[0]
    tm = _tile(S, 512)
    tk_s = _tile(S, 1024)
    G = {}

    def emit(**named):
        if emit_big is None:
            G.update(named)
            return None
        return emit_big(**named)

    def ffn_emit(prefix):
        return lambda **kw: emit(**{f"{prefix}_w_{k}": v for k, v in kw.items()})

    dx2, G["ffn2_norm"] = _ffn_bwd("ffn2b", R["dx3"], R["x2"], W["ffn2_norm"], R["h3"], R["a2"], R["b2"], R["f2"],
                                   W["ffn2_w_gate"], W["ffn2_w_up"], W["ffn2_w_down"], ffn_emit("ffn2"))
    tok = emit(w_out=_mm_plain("dw_out", D_MODEL, D_MODEL, S, D_MODEL, D_MODEL, tk_s, R["y"], "km", dx2, "kn", BF16))
    gn = W["gdn_norm"] if tok is None else W["gdn_norm"] + tok
    dy = _mm_plain("dy_mix", S, D_MODEL, D_MODEL, tm, D_MODEL, D_MODEL, dx2, "mk", W["w_out"], "nk", F32)
    p = R["p"]
    dhr, dgate, do, dz, G["gdn_norm"] = _mix_out_bwd(dy, R["h_f"], R["h_b"], R["o_f"], R["o_b"], p, gn)
    lam_b, lam_f = _rg_scan_adj("rg_scan_bwd", R["a_b"], dhr, R["a_f"], dhr)
    dpre, dxc_direct, d_rgprm = _rg_gates_bwd(R["xc"], R["bd"], R["rg_prm"], lam_f, lam_b, R["h_f"], R["h_b"])
    tmg = _tile(S, 512)
    dxc = _fused_mm("rg_dxc", S, RG_W, 4 * RG_W, tmg, RG_W, 4 * RG_W, [(dpre, "mk"), (R["bd"], "nk")], [(0, 1, 0)],
                    [(dxc_direct, (tmg, RG_W), _mn)], [(jax.ShapeDtypeStruct((S, RG_W), F32), (tmg, RG_W), _mn)],
                    lambda i, accs, ex, out: out[0].__setitem__(Ellipsis, ex[0][...] + accs[0][...]))[0]
    d_bd = _mm_plain("rg_dbd", RG_W, 4 * RG_W, S, RG_W, 4 * RG_W, tk_s, R["xc"], "km", dpre, "kn", F32)
    dx_rg, G["rg_conv_w"], G["rg_conv_b"] = _conv_bwd("rg_conv_bwd", p, 0, W["rg_conv_w"], [dxc], "bias")
    blocks = jnp.einsum("nigmj,nm->gnij", d_bd.reshape(RG_BLOCKS, RG_BLOCK, 4, RG_BLOCKS, RG_BLOCK),
                        jnp.eye(RG_BLOCKS, dtype=F32))
    G["rg_gate_a_w"] = jnp.stack([blocks[0], blocks[2]])
    G["rg_gate_x_w"] = jnp.stack([blocks[1], blocks[3]])
    G["rg_gate_a_b"] = jnp.stack([d_rgprm[0], d_rgprm[2]])
    G["rg_gate_x_b"] = jnp.stack([d_rgprm[1], d_rgprm[3]])
    G["rg_lambda"] = d_rgprm[4:6]
    adj = _gdn_scan_bwd(R["gdn_loc"], do)
    dq, dk, dv, dbg = _gdn_local_bwd(R["q"], R["k"], R["v"], R["bg"], R["gcr"], do, R["gdn_loc"], R["gdn_fwd"], adj)
    cw = W["gdn_conv_w"]
    dpq, dwq, _ = _conv_bwd("gdn_conv_q_bwd", p, 2, cw[:, 0:512], [dq], "q")
    dpk, dwk, _ = _conv_bwd("gdn_conv_k_bwd", p, 3, cw[:, 512:1024], [dk], "k")
    dpv, dwv, _ = _conv_bwd("gdn_conv_v_bwd", p, 4, cw[:, 1024:1536], [dv], "v")
    G["gdn_conv_w"] = jnp.concatenate([dwq, dwk, dwv], axis=1)
    dba, d_gprm = _gdn_prep_bwd(dbg, p, R["gdn_prm"])
    G["gdn_a_log"] = d_gprm[0, 8:16].reshape(2, GDN_H)
    G["gdn_dt_bias"] = d_gprm[1, 8:16].reshape(2, GDN_H)
    dp = jnp.concatenate([dx_rg, dgate, dpq, dpk, dpv, dz, dba], axis=1)
    tok = emit(w_in=_mm_plain("dw_in", D_MODEL, D_IN_PAD, S, D_MODEL, 640, tk_s, R["h2"], "km", dp, "kn", BF16))
    g_mix = W["mix_norm"] if tok is None else W["mix_norm"] + tok

    def epi_dx1(i, accs, ex, out):
        dx, dgt = _rmsnorm_bwd_tile(accs[0][...], ex[0][...], ex[1][...])
        out[0][...] = ex[2][...] + dx
        _colsum_into(out[1], i, jnp.sum(dgt, axis=0, keepdims=True))

    dx1, G["mix_norm"] = _fused_mm(
        "mix_dx", S, D_MODEL, D_IN_PAD, tm, D_MODEL, D_IN_PAD, [(dp, "mk"), (W["w_in"], "nk")], [(0, 1, 0)],
        [(R["x1"], (tm, D_MODEL), _mn), (g_mix, (1, D_MODEL), _row0), (dx2, (tm, D_MODEL), _mn)],
        [(jax.ShapeDtypeStruct((S, D_MODEL), F32), (tm, D_MODEL), _mn),
         (jax.ShapeDtypeStruct((1, D_MODEL), F32), (1, D_MODEL), _row0)], epi_dx1)
    G["final_norm"] = R["d_final_norm"]
    if emit_small is not None:
        emit_small(G)
    dx0, G["ffn1_norm"] = _ffn_bwd("ffn1b", dx1, x0, W["ffn1_norm"], R["h1"], R["a1"], R["b1"], R["f1"],
                                   W["ffn1_w_gate"], W["ffn1_w_up"], W["ffn1_w_down"], ffn_emit("ffn1"))
    return dx0, G


def _mesh_pos():
    x, y, c = lax.axis_index("x"), lax.axis_index("y"), lax.axis_index("c")
    return x, y, c, 4 * x + 2 * y + c


def _peer(x, y, c, r):
    px = 1 - x if r & 4 else x
    py = 1 - y if r & 2 else y
    pc = 1 - c if r & 1 else c
    return (px, py, pc), 4 * px + 2 * py + pc


_HBM = pl.BlockSpec(memory_space=pltpu.HBM)
_SEM = pl.BlockSpec(memory_space=pltpu.SEMAPHORE)


def _peer_copies(scatter, srcs, lands, send_sems, recv_sems):
    x, y, c, me = _mesh_pos()
    copies = []
    for a, (src, land) in enumerate(zip(srcs, lands)):
        for r in range(1, N_DEV):
            peer, peer_idx = _peer(x, y, c, r)
            copies.append(pltpu.make_async_remote_copy(
                src_ref=src.at[peer_idx] if scatter else src, dst_ref=land.at[r - 1] if scatter else land.at[me],
                send_sem=send_sems.at[a * 7 + r - 1], recv_sem=recv_sems.at[a * 7 + r - 1],
                device_id=peer, device_id_type=pl.DeviceIdType.MESH))
    return copies


def _exchange_start(name, scatter, arrays):
    slabs = arrays
    n = len(slabs)

    def body(*refs):
        srcs, lands = refs[0:n], refs[n:2 * n]
        send_sems, recv_sems = refs[2 * n], refs[2 * n + 1]
        token = refs[4 * n + 2]
        for cp in _peer_copies(scatter, srcs, lands, send_sems, recv_sems):
            cp.start()
        token[...] = jnp.zeros_like(token)

    land_shapes = [(N_DEV - 1,) + s.shape[1:] if scatter else (N_DEV,) + s.shape for s in slabs]
    n_sems = 7 * n
    out_shape = ([pltpu.SemaphoreType.DMA((n_sems,)), pltpu.SemaphoreType.DMA((n_sems,))]
                 + [pltpu.HBM(s.shape, s.dtype) for s in slabs]
                 + [pltpu.HBM(shp, s.dtype) for shp, s in zip(land_shapes, slabs)]
                 + [jax.ShapeDtypeStruct((8, 128), F32)])
    res = pl.pallas_call(
        body, name=name, out_shape=out_shape, in_specs=[_HBM] * (2 * n),
        out_specs=[_SEM, _SEM] + [_HBM] * (2 * n) + [pl.BlockSpec(memory_space=pltpu.VMEM)],
        input_output_aliases={i: 2 + i for i in range(2 * n)},
        compiler_params=pltpu.CompilerParams(has_side_effects=pltpu.SideEffectType.DATAFLOW_SIDE_EFFECTING),
    )(*[pltpu.with_memory_space_constraint(s, pltpu.HBM) for s in slabs],
      *[pltpu.with_memory_space_constraint(lax.empty(shp, s.dtype), pltpu.HBM) for shp, s in zip(land_shapes, slabs)])
    return dict(n=n, scatter=scatter, sems=res[0:2], srcs=res[2:2 + n], lands=res[2 + n:2 + 2 * n],
                token=res[2 + 2 * n][0, 0])


def _exchange_wait(name, started, after):
    n = started["n"]
    scatter = started["scatter"]

    def body(*refs):
        srcs, lands = refs[0:n], refs[n:2 * n]
        send_sems, recv_sems = refs[2 * n], refs[2 * n + 1]
        for cp in _peer_copies(scatter, srcs, lands, send_sems, recv_sems):
            cp.wait_send()
            cp.wait_recv()

    arrays = list(started["srcs"]) + list(started["lands"])
    res = pl.pallas_call(
        body, name=name, out_shape=[pltpu.HBM(a.shape, a.dtype) for a in arrays],
        in_specs=[_HBM] * (2 * n) + [_SEM, _SEM, pl.BlockSpec(memory_space=pl.ANY)], out_specs=[_HBM] * (2 * n),
        input_output_aliases={i: i for i in range(2 * n)},
        compiler_params=pltpu.CompilerParams(has_side_effects=pltpu.SideEffectType.DATAFLOW_SIDE_EFFECTING),
    )(*arrays, *started["sems"], after)
    return res[0:n], res[n:2 * n]


def _adamw_math(w, g, m, v):
    m2 = ADAM_B1 * m + (1.0 - ADAM_B1) * g
    v2 = ADAM_B2 * v + (1.0 - ADAM_B2) * (g * g)
    m_hat = m2 / (1.0 - ADAM_B1 ** ADAM_STEP)
    v_hat = v2 / (1.0 - ADAM_B2 ** ADAM_STEP)
    delta = -ADAM_LR * (m_hat / (jnp.sqrt(v_hat) + ADAM_EPS) + ADAM_WD * w)
    return delta, m2, v2


def _adamw_slabs(name, src, land, me, w, m, v, tr):
    R, C = w.shape

    def body(me_ref, own_ref, land_ref, w_ref, m_ref, v_ref, g_ref, d_ref, m2_ref, v2_ref):
        g = own_ref[0].astype(F32)
        for s in range(N_DEV - 1):
            g = g + land_ref[s].astype(F32)
        delta, m2, v2 = _adamw_math(w_ref[...], g, m_ref[...], v_ref[...])
        g_ref[...] = g
        d_ref[...] = delta
        m2_ref[...] = m2
        v2_ref[...] = v2

    im = lambda i, me_ref: (i, 0)
    grid_spec = pltpu.PrefetchScalarGridSpec(
        num_scalar_prefetch=1, grid=(R // tr,),
        in_specs=[pl.BlockSpec((1, tr, C), lambda i, me_ref: (me_ref[0], i, 0)),
                  pl.BlockSpec((N_DEV - 1, tr, C), lambda i, me_ref: (0, i, 0)),
                  pl.BlockSpec((tr, C), im), pl.BlockSpec((tr, C), im), pl.BlockSpec((tr, C), im)],
        out_specs=[pl.BlockSpec((tr, C), im)] * 4)
    return pl.pallas_call(body, name=name, grid_spec=grid_spec, out_shape=[jax.ShapeDtypeStruct((R, C), F32)] * 4,
                          compiler_params=_cp(1))(me.reshape(1).astype(jnp.int32), src, land, w, m, v)


def _sum_slots(name, slots):
    _, R, C = slots.shape

    def body(s_ref, o_ref):
        g = s_ref[0]
        for s in range(1, N_DEV):
            g = g + s_ref[s]
        o_ref[...] = g

    return _rows(name, R, R, [(slots, (N_DEV, R, C), lambda i: (0, 0, 0))],
                 [(jax.ShapeDtypeStruct((R, C), F32), (R, C), lambda i: (0, 0))], body)[0]


def _adamw_packed(name, g, w, m, v):
    R, C = g.shape

    def body(g_ref, w_ref, m_ref, v_ref, d_ref, m2_ref, v2_ref):
        delta, m2, v2 = _adamw_math(w_ref[...], g_ref[...], m_ref[...], v_ref[...])
        d_ref[...] = delta
        m2_ref[...] = m2
        v2_ref[...] = v2

    im = lambda i: (0, 0)
    sds = jax.ShapeDtypeStruct((R, C), F32)
    return _rows(name, R, R, [(a, (R, C), im) for a in (g, w, m, v)], [(sds, (R, C), im)] * 3, body)


def _pack(arrays):
    rows = []
    for a in arrays:
        flat = a.reshape(-1).astype(F32)
        pad = (-flat.shape[0]) % 128
        rows.append(jnp.pad(flat, (0, pad)).reshape(-1, 128))
    out = jnp.concatenate(rows, axis=0)
    return jnp.pad(out, ((0, (-out.shape[0]) % 8), (0, 0)))


def _unpack(packed, shapes):
    lead = packed.shape[:-2]
    outs = []
    r = 0
    for shp in shapes:
        n = math.prod(shp)
        nr = -(-n // 128)
        flat = packed[..., r:r + nr, :].reshape(lead + (nr * 128,))[..., :n]
        outs.append(flat.reshape(lead + tuple(shp)))
        r += nr
    return outs


FFN1_BIG = ["ffn1_w_gate", "ffn1_w_up", "ffn1_w_down"]
MIX_BIG = ["w_in", "w_out"]
FFN2_BIG = ["ffn2_w_gate", "ffn2_w_up", "ffn2_w_down"]
BIG = FFN1_BIG + MIX_BIG + FFN2_BIG
COL_SHARDED = {"ffn1_w_gate", "ffn1_w_up", "w_in", "ffn2_w_gate", "ffn2_w_up"}
SMALL_SHARDED = ["rg_conv_w", "rg_gate_a_b", "rg_gate_x_b", "rg_lambda", "gdn_conv_w"]
WEIGHTS = ["ffn1_norm", "ffn1_w_gate", "ffn1_w_up", "ffn1_w_down", "mix_norm", "w_in", "w_out", "rg_conv_w", "rg_conv_b",
           "rg_gate_a_w", "rg_gate_a_b", "rg_gate_x_w", "rg_gate_x_b", "rg_lambda", "gdn_conv_w", "gdn_a_log",
           "gdn_dt_bias", "gdn_norm", "ffn2_norm", "ffn2_w_gate", "ffn2_w_up", "ffn2_w_down", "final_norm"]
SMALL = [n for n in WEIGHTS if n not in BIG]
ROW_VECTORS = {"ffn1_norm", "mix_norm", "ffn2_norm", "gdn_norm", "rg_conv_b", "final_norm"}
ROW_TILE = {"ffn1_w_gate": 256, "ffn1_w_up": 256, "ffn1_w_down": 176, "w_in": 256, "w_out": 64,
            "ffn2_w_gate": 256, "ffn2_w_up": 256, "ffn2_w_down": 176}


def _unshard_cols(g):
    return g.transpose(1, 0, 2).reshape(g.shape[1], N_DEV * g.shape[2])


def _to_slabs(name, g):
    if name in COL_SHARDED:
        r, ctot = g.shape
        return g.reshape(r, N_DEV, ctot // N_DEV).transpose(1, 0, 2)
    return g.reshape(N_DEV, g.shape[0] // N_DEV, g.shape[1])


def _step(x, target, w, m, v):
    _, _, _, me = _mesh_pos()
    def unshard(n, gth):
        full = _unshard_cols(gth) if n in COL_SHARDED else gth.reshape(-1, gth.shape[-1])
        return jnp.pad(full, ((0, 0), (0, D_IN_PAD - D_IN))) if n == "w_in" else full

    def landed(started, name, after):
        srcs, lands = _exchange_wait(name, started, after)
        def with_own(src, land):
            slot = lax.broadcasted_iota(jnp.int32, (N_DEV,) + (1,) * src.ndim, 0)
            return jnp.where(slot == me, src[None], land)

        return [with_own(src, land) for src, land in zip(srcs, lands)]

    up_names = ["ffn1_w_gate", "ffn1_w_up"]
    small_shards = [w[n] for n in SMALL_SHARDED]
    st_up = _exchange_start("gather_ffn1_up_start", False, [w[n].astype(BF16) for n in up_names])
    tok = st_up["token"]
    st_down = _exchange_start("gather_ffn1_down_start", False, [(w["ffn1_w_down"] + tok).astype(BF16)])
    tok = tok + st_down["token"]
    st_mix = _exchange_start("gather_mix_start", False,
                             [(w[n] + tok).astype(BF16) for n in MIX_BIG] + [_pack(small_shards) + tok])
    tok = tok + st_mix["token"]
    st_ffn2 = _exchange_start("gather_ffn2_start", False, [(w[n] + tok).astype(BF16) for n in FFN2_BIG])
    W = {n: w[n] for n in SMALL if n not in SMALL_SHARDED}
    W["ffn1_norm"] = w["ffn1_norm"] + (tok + st_ffn2["token"])

    def more(stage, after):
        if stage == "ffn1_up":
            return {n: unshard(n, gth) for n, gth in zip(up_names, landed(st_up, "gather_ffn1_up_wait", after))}
        if stage == "ffn1_down":
            return {"ffn1_w_down": unshard("ffn1_w_down", landed(st_down, "gather_ffn1_down_wait", after)[0])}
        if stage == "ffn2":
            return {n: unshard(n, gth) for n, gth in zip(FFN2_BIG, landed(st_ffn2, "gather_ffn2_wait", after))}
        got = landed(st_mix, "gather_mix_wait", after)
        new = {n: unshard(n, gth) for n, gth in zip(MIX_BIG, got)}
        for n, gth in zip(SMALL_SHARDED, _unpack(got[-1], [s.shape for s in small_shards])):
            new[n] = jnp.moveaxis(gth, 0, -2).reshape(gth.shape[1:-1] + (N_DEV * gth.shape[-1],))
        return new

    R = _layer_fwd(x, target, W, more)
    W = R["W"]
    pending = []

    def emit_big(**named):
        slabs = [_to_slabs(n, g[:, :D_IN] if n == "w_in" else g) for n, g in named.items()]
        started = _exchange_start(f"scatter_start_{len(pending)}", True, slabs)
        pending.append((list(named), started))
        return started["token"]

    small_started = []

    def emit_small(G):
        packed = _pack([G[n] for n in SMALL if n != "ffn1_norm"])
        small_started.append(_exchange_start("gather_small_start", False, [packed]))

    grad_x, G = _layer_bwd(x, W, R, emit_big, emit_small)
    st_late = _exchange_start("gather_ffn1_norm_start", False, [_pack([G["ffn1_norm"]])])
    loss = lax.psum(R["loss"][0, 0], ("x", "y", "c"))
    out = {}

    def finish(i, after):
        names, started = pending[i]
        srcs, lands = _exchange_wait(f"scatter_wait_{i}", started, after)
        for n, src, land in zip(names, srcs, lands):
            out[n] = _adamw_slabs(f"adamw_{n}", src, land, me, w[n], m[n], v[n], ROW_TILE[n])

    n_early = len(pending) - 2
    for i in range(n_early):
        finish(i, grad_x)
    early = [n for n in SMALL if n != "ffn1_norm"]
    srcs, lands = _exchange_wait("gather_small_wait", small_started[0], grad_x)
    slot = lax.broadcasted_iota(jnp.int32, (N_DEV, 1, 1), 0)
    slots = jnp.where(slot == me, srcs[0][None], lands[0])
    reduced = dict(zip(early, _unpack(_sum_slots("sum_small_grads", slots), [G[n].shape for n in early])))

    def adamw_small(name, names):
        g_small = []
        for n in names:
            g = reduced[n]
            if n in SMALL_SHARDED:
                per = g.shape[-1] // N_DEV
                g = lax.dynamic_slice_in_dim(g, me * per, per, axis=g.ndim - 1)
            g_small.append(g.reshape(w[n].shape))
        shapes = [w[n].shape for n in names]
        d_p, m_p, v_p = _adamw_packed(name, _pack(g_small), _pack([w[n] for n in names]),
                                      _pack([m[n] for n in names]), _pack([v[n] for n in names]))
        for n, g, d_, m_, v_ in zip(names, g_small, _unpack(d_p, shapes), _unpack(m_p, shapes), _unpack(v_p, shapes)):
            out[n] = (g, d_, m_, v_)
        return d_p

    done_early = adamw_small("adamw_small", early)
    srcs, lands = _exchange_wait("gather_ffn1_norm_wait", st_late, done_early)
    late = jnp.where(slot == me, srcs[0][None], lands[0])
    reduced["ffn1_norm"] = _unpack(_sum_slots("sum_ffn1_norm_grad", late), [G["ffn1_norm"].shape])[0]
    done = adamw_small("adamw_ffn1_norm", ["ffn1_norm"])
    for i in range(n_early, len(pending)):
        finish(i, done)
    return loss, grad_x, out


def kernel(x, ffn1_norm, ffn1_w_gate, ffn1_w_up, ffn1_w_down, mix_norm, w_in, w_out, rg_conv_w, rg_conv_b, rg_gate_a_w, rg_gate_a_b, rg_gate_x_w, rg_gate_x_b, rg_lambda, gdn_conv_w, gdn_a_log, gdn_dt_bias, gdn_norm, ffn2_norm, ffn2_w_gate, ffn2_w_up, ffn2_w_down, final_norm, loss_target, m_ffn1_norm, m_ffn1_w_gate, m_ffn1_w_up, m_ffn1_w_down, m_mix_norm, m_w_in, m_w_out, m_rg_conv_w, m_rg_conv_b, m_rg_gate_a_w, m_rg_gate_a_b, m_rg_gate_x_w, m_rg_gate_x_b, m_rg_lambda, m_gdn_conv_w, m_gdn_a_log, m_gdn_dt_bias, m_gdn_norm, m_ffn2_norm, m_ffn2_w_gate, m_ffn2_w_up, m_ffn2_w_down, m_final_norm, v_ffn1_norm, v_ffn1_w_gate, v_ffn1_w_up, v_ffn1_w_down, v_mix_norm, v_w_in, v_w_out, v_rg_conv_w, v_rg_conv_b, v_rg_gate_a_w, v_rg_gate_a_b, v_rg_gate_x_w, v_rg_gate_x_b, v_rg_lambda, v_gdn_conv_w, v_gdn_a_log, v_gdn_dt_bias, v_gdn_norm, v_ffn2_norm, v_ffn2_w_gate, v_ffn2_w_up, v_ffn2_w_down, v_final_norm):
    args = dict(locals())
    orig_shapes = {n: args[n].shape for n in WEIGHTS}

    def local(prefix):
        d = {}
        for n in WEIGHTS:
            a = args[prefix + n]
            d[n] = a.reshape(1, -1) if n in ROW_VECTORS else a[0]
        return d

    loss, grad_x, out = _step(x[0], loss_target[0], local(""), local("m_"), local("v_"))
    res = [loss, grad_x[None]]
    for k in range(4):
        res += [out[n][k].reshape(orig_shapes[n]) for n in WEIGHTS]
    return tuple(res)
```

```python
import functools
import math

import jax
import jax.numpy as jnp
from jax import lax
from jax.experimental import pallas as pl
from jax.experimental.pallas import tpu as pltpu

F32, BF16 = jnp.float32, jnp.bfloat16

D_MODEL = 1024
D_FF = 2816
RG_W = 512
RG_BLOCKS = 8
RG_BLOCK = 64
RG_C = 8.0
CONV_W = 4
GDN_H = 4
GDN_DK = 128
CHUNK = 64
EPS = 1e-6
D_IN = 3088
D_IN_PAD = 3200
COL_BA = 3072
N_DEV = 8
HALO = 16
VMEM_LIMIT = 48 * 1024 * 1024
VMEM_CAP = 60 * 1024 * 1024

ADAM_LR = 0.001
ADAM_B1 = 0.9
ADAM_B2 = 0.999
ADAM_EPS = 1e-08
ADAM_WD = 0.01
ADAM_STEP = 10

HI = lax.Precision.HIGHEST


def _cp(n, vmem_limit=None):
    return pltpu.CompilerParams(dimension_semantics=("arbitrary",) * n,
                                vmem_limit_bytes=VMEM_LIMIT if vmem_limit is None else vmem_limit)


def _matmul_vmem_limit(block_bytes, acc_bytes):
    need = 2 * block_bytes + 2 * acc_bytes
    return int(min(VMEM_CAP, max(VMEM_LIMIT, need * 4 // 3)))


def _tile(n, pref):
    return min(n, pref)


def _sigmoid(x):
    return 0.5 * jnp.tanh(0.5 * x) + 0.5


def _softplus(x):
    return jnp.maximum(x, 0.0) + jnp.log(1.0 + jnp.exp(-jnp.abs(x)))


def _dot(a, b, ca, cb, prec=None):
    return lax.dot_general(a, b, (((ca,), (cb,)), ((), ())), preferred_element_type=F32, precision=prec)


def _fused_mm(name, M, N, K, tm, tn, tk, ops, pairs, extras, outs, epilogue):
    nm, nn, nk = M // tm, N // tn, K // tk
    assert nm * tm == M and nn * tn == N and nk * tk == K, (name, M, N, K, tm, tn, tk)
    spec_of = {
        "mk": pl.BlockSpec((tm, tk), lambda i, j, k: (i, k)),
        "km": pl.BlockSpec((tk, tm), lambda i, j, k: (k, i)),
        "kn": pl.BlockSpec((tk, tn), lambda i, j, k: (k, j)),
        "nk": pl.BlockSpec((tn, tk), lambda i, j, k: (j, k)),
    }
    in_specs = [spec_of[m] for _, m in ops]
    in_specs += [pl.BlockSpec(bs, lambda i, j, k, im=im: im(i, j)) for _, bs, im in extras]
    out_specs = [pl.BlockSpec(bs, lambda i, j, k, im=im: im(i, j)) for _, bs, im in outs]
    n_ops, n_ex, n_out = len(ops), len(extras), len(outs)
    n_acc = 1 + max(g for _, _, g in pairs)
    modes = [m for _, m in ops]

    def body(*refs):
        op_refs = refs[:n_ops]
        ex_refs = refs[n_ops:n_ops + n_ex]
        out_refs = refs[n_ops + n_ex:n_ops + n_ex + n_out]
        accs = refs[n_ops + n_ex + n_out:]
        i = pl.program_id(0)
        k = pl.program_id(2)
        def dots():
            vals = [r[...].astype(BF16) for r in op_refs]
            for ia, ib, g in pairs:
                yield g, _dot(vals[ia], vals[ib], 1 if modes[ia] == "mk" else 0, 0 if modes[ib] == "kn" else 1)

        if nk == 1:
            sums = [None] * n_acc
            for g, d in dots():
                sums[g] = d if sums[g] is None else sums[g] + d
            epilogue(i, [_Held(s) for s in sums], ex_refs, out_refs)
            return

        @pl.when(k == 0)
        def _():
            for a in accs:
                a[...] = jnp.zeros_like(a)

        for g, d in dots():
            accs[g][...] += d

        @pl.when(k == nk - 1)
        def _():
            epilogue(i, accs, ex_refs, out_refs)

    op_block = {"mk": tm * tk, "km": tm * tk, "kn": tk * tn, "nk": tk * tn}
    block_bytes = sum(op_block[m] * a.dtype.itemsize for a, m in ops)
    block_bytes += sum(math.prod(bs) * jnp.dtype(a.dtype).itemsize for a, bs, _ in list(extras) + list(outs))
    res = pl.pallas_call(
        body, name=name, grid=(nm, nn, nk), in_specs=in_specs, out_specs=out_specs,
        out_shape=[o for o, _, _ in outs],
        scratch_shapes=[pltpu.VMEM((tm, tn), F32)] * (n_acc if nk > 1 else 0),
        compiler_params=_cp(3, _matmul_vmem_limit(block_bytes, n_acc * tm * tn * 4)),
    )(*[a for a, _ in ops], *[a for a, _, _ in extras])
    return res


class _Held:
    def __init__(self, value):
        self.value = value

    def __getitem__(self, idx):
        return self.value[idx]


def _mn(i, j):
    return (i, j)


def _row0(i, j):
    return (0, 0)


def _rows(name, S, ts, ins, outs, body, scratch=()):
    return pl.pallas_call(
        body, name=name, grid=(S // ts,),
        in_specs=[pl.BlockSpec(bs, im) for _, bs, im in ins],
        out_specs=[pl.BlockSpec(bs, im) for _, bs, im in outs],
        out_shape=[o for o, _, _ in outs],
        scratch_shapes=list(scratch),
        compiler_params=_cp(1),
    )(*[a for a, _, _ in ins])


def _halo_ins(arr, S, ts, width, colblk):
    per = ts // HALO
    last = S // HALO - 1
    return [
        (arr, (ts, width), lambda i: (i, colblk)),
        (arr, (HALO, width), lambda i: (jnp.maximum(i * per - 1, 0), colblk)),
        (arr, (HALO, width), lambda i: (jnp.minimum((i + 1) * per, last), colblk)),
    ]


def _ext(main_ref, prev_ref, next_ref, i, n_tiles):
    prev = jnp.where(i > 0, prev_ref[...].astype(F32), 0.0)
    nxt = jnp.where(i < n_tiles - 1, next_ref[...].astype(F32), 0.0)
    return jnp.concatenate([prev, main_ref[...].astype(F32), nxt], axis=0)


def _shift(ext, off, ts):
    n = ext.shape[0]
    if off == 0:
        return ext[HALO:HALO + ts]
    return pltpu.roll(ext, (-off) % n, 0)[HALO:HALO + ts]


def _rmsnorm_fwd(name, x, g):
    S, D = x.shape
    ts = _tile(S, 512)

    def body(x_ref, g_ref, o_ref):
        xv = x_ref[...]
        r = lax.rsqrt(jnp.mean(xv * xv, axis=-1, keepdims=True) + EPS)
        o_ref[...] = (xv * r * g_ref[...]).astype(BF16)

    return _rows(name, S, ts,
                 [(x, (ts, D), lambda i: (i, 0)), (g, (1, D), lambda i: (0, 0))],
                 [(jax.ShapeDtypeStruct((S, D), BF16), (ts, D), lambda i: (i, 0))], body)[0]


def _rmsnorm_bwd_tile(dh, x, g):
    r = lax.rsqrt(jnp.mean(x * x, axis=-1, keepdims=True) + EPS)
    xhat = x * r
    dxn = dh * g
    dx = r * (dxn - xhat * jnp.mean(dxn * xhat, axis=-1, keepdims=True))
    return dx, dh * xhat


def _ffn_fwd(tag, x, h, wg, wu, wd, extras, outs, finish):
    S = x.shape[0]
    tm = _tile(S, 1024)
    tn = 1408

    def epi_up(i, accs, ex, out):
        a = accs[0][...]
        b = accs[1][...]
        s = _sigmoid(a)
        sa = a * s
        out[0][...] = sa.astype(BF16)
        out[1][...] = (b * (s * (1.0 + a * (1.0 - s)))).astype(BF16)
        out[2][...] = (sa * b).astype(BF16)

    sds = jax.ShapeDtypeStruct((S, D_FF), BF16)
    a, b, f = _fused_mm(f"{tag}_up", S, D_FF, D_MODEL, tm, tn, D_MODEL,
                        [(h, "mk"), (wg, "kn"), (wu, "kn")], [(0, 1, 0), (0, 2, 1)], [],
                        [(sds, (tm, tn), _mn)] * 3, epi_up)

    def epi_down(i, accs, ex, out):
        finish(i, ex[0][...] + 0.5 * accs[0][...], ex[1:], out)

    if callable(wd):
        wd = wd(f)
    res = _fused_mm(f"{tag}_down", S, D_MODEL, D_FF, tm, D_MODEL, 1408,
                    [(f, "mk"), (wd, "kn")], [(0, 1, 0)], [(x, (tm, D_MODEL), _mn)] + extras(tm), outs(tm), epi_down)
    return res, a, b, f


def _rmsnorm_tile(xv, g):
    return (xv * lax.rsqrt(jnp.mean(xv * xv, axis=-1, keepdims=True) + EPS) * g).astype(BF16)


def _conv_taps(ext, w_ref, ts):
    acc = None
    for j in range(CONV_W):
        term = w_ref[j:j + 1, :] * _shift(ext, j - 2, ts)
        acc = term if acc is None else acc + term
    return acc


def _l2norm_heads(s, scale):
    outs = []
    for h in range(GDN_H):
        sh = s[:, h * GDN_DK:(h + 1) * GDN_DK]
        outs.append(sh * (lax.rsqrt(jnp.sum(sh * sh, axis=-1, keepdims=True) + EPS) * scale))
    return jnp.concatenate(outs, axis=-1)


def _conv_fwd(name, p, colblk, w, bias, mode):
    S = p.shape[0]
    ts = _tile(S, 512)
    n_tiles = S // ts
    C = w.shape[1]

    def body(main, prev, nxt, w_ref, b_ref, o_ref):
        i = pl.program_id(0)
        c = _conv_taps(_ext(main, prev, nxt, i, n_tiles), w_ref, ts)
        if mode == "bias":
            o_ref[...] = c + b_ref[...]
        else:
            s = c * _sigmoid(c)
            if mode == "q":
                s = _l2norm_heads(s, GDN_DK ** -0.5)
            elif mode == "k":
                s = _l2norm_heads(s, 1.0)
            o_ref[...] = s

    ins = _halo_ins(p, S, ts, C, colblk) + [(w, (CONV_W, C), lambda i: (0, 0)), (bias, (1, C), lambda i: (0, 0))]
    return _rows(name, S, ts, ins, [(jax.ShapeDtypeStruct((S, C), F32), (ts, C), lambda i: (i, 0))], body)[0]


def _rg_gate_terms(pre, xc, prm_ref, d):
    r = _sigmoid(pre[:, d * 1024:d * 1024 + RG_W] + prm_ref[2 * d:2 * d + 1, :])
    ig = _sigmoid(pre[:, d * 1024 + RG_W:(d + 1) * 1024] + prm_ref[2 * d + 1:2 * d + 2, :])
    sp = _softplus(-prm_ref[4 + d:5 + d, :])
    log_a = -RG_C * r * sp
    a = jnp.exp(log_a)
    t = jnp.tanh(log_a)
    sq = jnp.sqrt(-2.0 * t / (1.0 - t))
    return r, ig, sp, a, sq


def _rg_gates_fwd(xc, bd, prm):
    S = xc.shape[0]
    tm = _tile(S, 256)

    def epi(i, accs, ex, out):
        pre = accs[0][...]
        xv = ex[0][...]
        for d in range(2):
            r, ig, sp, a, sq = _rg_gate_terms(pre, xv, ex[1], d)
            out[2 * d][...] = a
            out[2 * d + 1][...] = sq * ig * xv

    sds = jax.ShapeDtypeStruct((S, RG_W), F32)
    blk = (tm, RG_W)
    im = lambda i, j: (i, 0)
    return _fused_mm("rg_gates_fwd", S, 4 * RG_W, RG_W, tm, 4 * RG_W, RG_W,
                     [(xc, "mk"), (bd, "kn")], [(0, 1, 0)],
                     [(xc, blk, im), (prm, (8, RG_W), _row0)], [(sds, blk, im)] * 4, epi)


SUBLANES = 8


def _scan_rows(a, b, reverse):
    rows = lax.broadcasted_iota(jnp.int32, a.shape, 0)
    s = 1
    while s < SUBLANES:
        shift = SUBLANES - s if reverse else s
        a_sh = pltpu.roll(a, shift, 0)
        b_sh = pltpu.roll(b, shift, 0)
        valid = (rows < SUBLANES - s) if reverse else (rows >= s)
        b = jnp.where(valid, a * b_sh + b, b)
        a = jnp.where(valid, a * a_sh, a)
        s *= 2
    return a, b


def _rg_scan(name, a_f, b_f, a_b, b_b):
    S, C = a_f.shape
    ts = _tile(S, 512)
    n_tiles = S // ts

    def body(af, bf, ab, bb, hf, hb, carry):
        @pl.when(pl.program_id(0) == 0)
        def _():
            carry[...] = jnp.zeros_like(carry)

        n_sub = ts // SUBLANES

        def step(j, c):
            cf, cb = c
            r0 = pl.multiple_of(j * SUBLANES, SUBLANES)
            cum_a, h0 = _scan_rows(af[pl.ds(r0, SUBLANES), :], bf[pl.ds(r0, SUBLANES), :], False)
            h = h0 + cum_a * cf
            hf[pl.ds(r0, SUBLANES), :] = h
            cf = h[SUBLANES - 1:SUBLANES, :]
            r1 = pl.multiple_of((n_sub - 1 - j) * SUBLANES, SUBLANES)
            cum_a, h0 = _scan_rows(ab[pl.ds(r1, SUBLANES), :], bb[pl.ds(r1, SUBLANES), :], True)
            h = h0 + cum_a * cb
            hb[pl.ds(r1, SUBLANES), :] = h
            cb = h[0:1, :]
            return cf, cb

        cf, cb = lax.fori_loop(0, n_sub, step, (carry[0:1, :], carry[1:2, :]), unroll=4)
        carry[0:1, :] = cf
        carry[1:2, :] = cb

    fw = lambda i: (i, 0)
    bw = lambda i: (n_tiles - 1 - i, 0)
    sds = jax.ShapeDtypeStruct((S, C), F32)
    return _rows(name, S, ts,
                 [(a_f, (ts, C), fw), (b_f, (ts, C), fw), (a_b, (ts, C), bw), (b_b, (ts, C), bw)],
                 [(sds, (ts, C), fw), (sds, (ts, C), bw)], body, scratch=[pltpu.VMEM((8, C), F32)])


def _tri_masks():
    ri = lax.broadcasted_iota(jnp.int32, (CHUNK, CHUNK), 0)
    ci = lax.broadcasted_iota(jnp.int32, (CHUNK, CHUNK), 1)
    return ri, ci


def _gdn_prep_fwd(p, prm):
    S = p.shape[0]
    ts = _tile(S, 512)

    def body(p_ref, prm_ref, o_ref):
        raw = p_ref[...].astype(F32)
        lane = lax.broadcasted_iota(jnp.int32, (1, 128), 1)
        g = -jnp.exp(prm_ref[0:1, :]) * _softplus(raw + prm_ref[1:2, :])
        g = jnp.where((lane >= 8) & (lane < 16), g, 0.0)
        beta = _sigmoid(raw)
        ri, ci = _tri_masks()
        lower = (ri >= ci).astype(F32)
        upper = (ri <= ci).astype(F32)
        for c in range(ts // CHUNK):
            rows = slice(c * CHUNK, (c + 1) * CHUNK)
            gch = g[rows]
            gc = jnp.where(lane < 12, _dot(lower, gch, 1, 0, HI), _dot(upper, gch, 1, 0, HI))
            o_ref[rows, :] = jnp.where(lane < 8, beta[rows], gc)

    return _rows("gdn_prep_fwd", S, ts,
                 [(p, (ts, 128), lambda i: (i, COL_BA // 128)), (prm, (8, 128), lambda i: (0, 0))],
                 [(jax.ShapeDtypeStruct((S, 128), F32), (ts, 128), lambda i: (i, 0))], body)[0]


def _bdot(a, b, ca, cb):
    return _dot(a.astype(BF16), b.astype(BF16), ca, cb)


GDN_W = GDN_H * GDN_DK
GDN_TS = 256
LOCAL_CHUNKS = 2

def _gdn_decay(bg_ref, gcr_ref, c, rows, r0, col, rev, ri, ci):
    beta = bg_ref[rows, col:col + 1]
    gc = bg_ref[rows, 8 + col:9 + col]
    last = 0 if rev else CHUNK - 1
    gl = bg_ref[pl.ds(r0 + last, 1), 8 + col:9 + col]
    out = dict(beta=beta, gc=gc, gl=gl, eg=jnp.exp(gc), egl=jnp.exp(gl - gc), cd=jnp.exp(gl))
    if gcr_ref is not None:
        incl = (ri <= ci) if rev else (ri >= ci)
        out["strict"] = (ri < ci) if rev else (ri > ci)
        out["dm"] = jnp.where(incl, jnp.exp(jnp.where(incl, gc - gcr_ref[c, col:col + 1, :], 0.0)), 0.0)
    return out


def _dir_tile(d, n_tiles, flip):
    if (d == 1) != flip:
        return lambda i: n_tiles - 1 - i
    return lambda i: i


def _gdn_local_fwd(q, k, v, bg, gcr):
    S = q.shape[0]
    ts = _tile(S, GDN_TS)
    ncb = ts // CHUNK
    nch = S // CHUNK

    def body(q_ref, k_ref, v_ref, bg_ref, gcr_ref, *out_refs):
        ri, ci = _tri_masks()
        eye = (ri == ci).astype(F32)
        outs = (out_refs[0:6], out_refs[6:12])
        cd_ref = out_refs[12]

        def chunk(cc, carry):
            chains = []
            for c in (LOCAL_CHUNKS * cc + j for j in range(LOCAL_CHUNKS)):
                r0 = pl.multiple_of(c * CHUNK, CHUNK)
                rows = pl.ds(r0, CHUNK)
                for h in range(GDN_H):
                    cols = slice(h * GDN_DK, (h + 1) * GDN_DK)
                    qh, kh, vh = q_ref[rows, cols], k_ref[rows, cols], v_ref[rows, cols]
                    both = _bdot(jnp.concatenate([qh, kh], axis=0), kh, 1, 1)
                    for d in range(2):
                        chains.append(dict(c=c, r0=r0, rows=rows, h=h, d=d, cols=cols, qh=qh, kh=kh, vh=vh,
                                           qk=both[0:CHUNK], kk=both[CHUNK:2 * CHUNK]))
            for ch in chains:
                m = _gdn_decay(bg_ref, gcr_ref, ch["c"], ch["rows"], ch["r0"], ch["d"] * GDN_H + ch["h"], ch["d"] == 1,
                               ri, ci)
                ch["m"] = m
                ch["x"] = -jnp.where(m["strict"], m["beta"] * ch["kk"] * m["dm"], 0.0)
                ch["t"] = eye + ch["x"]
            for ch in chains:
                ch["pw"] = _bdot(ch["x"], ch["x"], 1, 0)
            for level in range(1, 6):
                last_level = level == 5
                for ch in chains:
                    rhs = ch["t"] if last_level else jnp.concatenate([ch["t"], ch["pw"]], axis=1)
                    ch["prod"] = _bdot(ch["pw"], rhs, 1, 0)
                for ch in chains:
                    ch["t"] = ch["t"] + ch["prod"][:, 0:CHUNK]
                    if not last_level:
                        ch["pw"] = ch["prod"][:, CHUNK:2 * CHUNK]
            for ch in chains:
                m = ch["m"]
                rhs = jnp.concatenate([ch["vh"] * m["beta"], ch["kh"] * (m["beta"] * m["eg"])], axis=1)
                ch["uw"] = _bdot(ch["t"], rhs, 1, 0)
            for ch in chains:
                u_ref, w_ref, a_ref, t_ref, qd_ref, kd_ref = outs[ch["d"]]
                m = ch["m"]
                c, rows = ch["c"], ch["rows"]
                col = ch["d"] * GDN_H + ch["h"]
                u_ref[rows, ch["cols"]] = ch["uw"][:, 0:GDN_DK]
                w_ref[rows, ch["cols"]] = ch["uw"][:, GDN_DK:2 * GDN_DK].astype(BF16)
                a_ref[c, ch["h"]] = (ch["qk"] * m["dm"]).astype(BF16)
                t_ref[c, ch["h"]] = _bdot(ch["t"], eye, 0, 0).astype(BF16)
                qd_ref[rows, ch["cols"]] = (ch["qh"] * m["eg"]).astype(BF16)
                kd_ref[rows, ch["cols"]] = (ch["kh"] * m["egl"]).astype(BF16)
                cd_ref[c, col:col + 1, :] = jnp.broadcast_to(m["cd"], (1, 128))
            return carry

        lax.fori_loop(0, ncb // LOCAL_CHUNKS, chunk, 0)

    im = lambda i: (i, 0)
    im4 = lambda i: (i, 0, 0, 0)
    ins = [(q, (ts, GDN_W), im), (k, (ts, GDN_W), im), (v, (ts, GDN_W), im), (bg, (ts, 128), im),
           (gcr, (ncb, 8, CHUNK), lambda i: (i, 0, 0))]
    per_dir = [(jax.ShapeDtypeStruct((S, GDN_W), F32), (ts, GDN_W), im),
               (jax.ShapeDtypeStruct((S, GDN_W), BF16), (ts, GDN_W), im),
               (jax.ShapeDtypeStruct((nch, GDN_H, CHUNK, CHUNK), BF16), (ncb, GDN_H, CHUNK, CHUNK), im4),
               (jax.ShapeDtypeStruct((nch, GDN_H, CHUNK, CHUNK), BF16), (ncb, GDN_H, CHUNK, CHUNK), im4),
               (jax.ShapeDtypeStruct((S, GDN_W), BF16), (ts, GDN_W), im),
               (jax.ShapeDtypeStruct((S, GDN_W), BF16), (ts, GDN_W), im)]
    cd_out = (jax.ShapeDtypeStruct((nch, 8, 128), F32), (ncb, 8, 128), lambda i: (i, 0, 0))
    res = _rows("gdn_local_fwd", S, ts, ins, per_dir * 2 + [cd_out], body)
    return res[0:6], res[6:12], res[12]


def _gdn_scan_fwd(loc):
    S = loc[0][0].shape[0]
    ts = _tile(S, GDN_TS)
    n_tiles = S // ts
    ncb = ts // CHUNK
    nch = S // CHUNK

    def body(*refs):
        ins = (refs[0:6], refs[6:12])
        outs = (refs[12:15], refs[15:18])
        state = refs[18]

        @pl.when(pl.program_id(0) == 0)
        def _():
            state[...] = jnp.zeros_like(state)

        def chunk(cc, carry):
            chains = []
            for d in range(2):
                c = cc if d == 0 else ncb - 1 - cc
                rows = pl.ds(pl.multiple_of(c * CHUNK, CHUNK), CHUNK)
                for h in range(GDN_H):
                    cols = slice(h * GDN_DK, (h + 1) * GDN_DK)
                    chains.append(dict(d=d, h=h, c=c, rows=rows, cols=cols, st=state[d * GDN_H + h]))
            for ch in chains:
                qd_ref, kd_ref, u_ref, w_ref, a_ref, cd_ref = ins[ch["d"]]
                rows, cols = ch["rows"], ch["cols"]
                lhs = jnp.concatenate([w_ref[rows, cols], qd_ref[rows, cols]], axis=0)
                ch["ws_qs"] = _dot(lhs, ch["st"].astype(BF16), 1, 0)
            for ch in chains:
                qd_ref, kd_ref, u_ref, w_ref, a_ref, cd_ref = ins[ch["d"]]
                rows, cols = ch["rows"], ch["cols"]
                vn = u_ref[rows, cols] - ch["ws_qs"][0:CHUNK]
                vnb = vn.astype(BF16)
                ch["vn"] = vn
                ch["avn"] = _dot(a_ref[ch["c"], ch["h"]], vnb, 1, 0)
                ch["kvn"] = _dot(kd_ref[rows, cols], vnb, 0, 0)
            for ch in chains:
                o_ref, vn_ref, s_ref = outs[ch["d"]]
                cd_ref = ins[ch["d"]][5]
                rows, cols = ch["rows"], ch["cols"]
                col = ch["d"] * GDN_H + ch["h"]
                o_ref[rows, cols] = ch["ws_qs"][CHUNK:2 * CHUNK] + ch["avn"]
                vn_ref[rows, cols] = ch["vn"].astype(BF16)
                s_ref[ch["c"], ch["h"]] = ch["st"].astype(BF16)
                state[ch["d"] * GDN_H + ch["h"]] = ch["st"] * cd_ref[ch["c"], col:col + 1, :] + ch["kvn"]
            return carry

        lax.fori_loop(0, ncb, chunk, 0)

    ins, outs = [], []
    for d in range(2):
        tix = _dir_tile(d, n_tiles, False)
        im = lambda i, tix=tix: (tix(i), 0)
        im4 = lambda i, tix=tix: (tix(i), 0, 0, 0)
        u, w, a, _, qd, kd = loc[d]
        ins += [(qd, (ts, GDN_W), im), (kd, (ts, GDN_W), im), (u, (ts, GDN_W), im), (w, (ts, GDN_W), im),
                (a, (ncb, GDN_H, CHUNK, CHUNK), im4), (loc[2], (ncb, 8, 128), lambda i, tix=tix: (tix(i), 0, 0))]
        outs += [(jax.ShapeDtypeStruct((S, GDN_W), F32), (ts, GDN_W), im),
                 (jax.ShapeDtypeStruct((S, GDN_W), BF16), (ts, GDN_W), im),
                 (jax.ShapeDtypeStruct((nch, GDN_H, GDN_DK, GDN_DK), BF16), (ncb, GDN_H, GDN_DK, GDN_DK), im4)]
    res = _rows("gdn_scan_fwd", S, ts, ins, outs, body, scratch=[pltpu.VMEM((2 * GDN_H, GDN_DK, GDN_DK), F32)])
    return res[0:3], res[3:6]


def _gelu(x):
    c = math.sqrt(2.0 / math.pi)
    t = jnp.tanh(c * (x + 0.044715 * x * x * x))
    return 0.5 * x * (1.0 + t), t


def _mix_out_fwd(h_f, h_b, o_f, o_b, p, gn):
    S = h_f.shape[0]
    ts = _tile(S, 512)

    def body(hf, hb, of, ob, gate, z, gn_ref, y_ref):
        ge, _ = _gelu(gate[...].astype(F32))
        y_ref[:, 0:RG_W] = ((hf[...] + hb[...]) * ge).astype(BF16)
        o = of[...] + ob[...]
        zv = z[...].astype(F32)
        sz = zv * _sigmoid(zv)
        for h in range(GDN_H):
            cols = slice(h * GDN_DK, (h + 1) * GDN_DK)
            oh = o[:, cols]
            n = oh * lax.rsqrt(jnp.mean(oh * oh, axis=-1, keepdims=True) + EPS) * gn_ref[...]
            y_ref[:, RG_W + h * GDN_DK:RG_W + (h + 1) * GDN_DK] = (n * sz[:, cols]).astype(BF16)

    blk = (ts, RG_W)
    im = lambda i: (i, 0)
    ins = [(h_f, blk, im), (h_b, blk, im), (o_f, blk, im), (o_b, blk, im),
           (p, blk, lambda i: (i, 1)), (p, blk, lambda i: (i, 5)), (gn, (1, GDN_DK), lambda i: (0, 0))]
    return _rows("mix_out_fwd", S, ts, ins,
                 [(jax.ShapeDtypeStruct((S, D_MODEL), BF16), (ts, D_MODEL), im)], body)[0]


def _block_diag(w):
    n = w.shape[0]
    return jnp.einsum("nij,nm->nimj", w, jnp.eye(n, dtype=w.dtype)).reshape(n * w.shape[1], n * w.shape[2])


def _rg_bd(a_w, x_w):
    return jnp.concatenate([_block_diag(a_w[0]), _block_diag(x_w[0]), _block_diag(a_w[1]), _block_diag(x_w[1])],
                           axis=1).astype(BF16)


def _rg_prm(ba, bx, lam):
    return jnp.concatenate([ba[0:1], bx[0:1], ba[1:2], bx[1:2], lam, jnp.zeros((2, RG_W), F32)], axis=0)


def _gdn_prm(a_log, dt_bias):
    rows = jnp.zeros((8, 128), F32)
    rows = rows.at[0, 8:16].set(a_log.reshape(-1))
    return rows.at[1, 8:16].set(dt_bias.reshape(-1))


def _gc_rows(bg):
    S = bg.shape[0]
    return bg[:, 8:16].reshape(S // CHUNK, CHUNK, 8).transpose(0, 2, 1)


def _layer_fwd(x0, target, W, more=None):
    S = x0.shape[0]
    R = {}
    R["h1"] = _rmsnorm_fwd("rms1", x0, W["ffn1_norm"])
    if more is not None:
        W = {**W, **more("ffn1_up", R["h1"])}
    late_wd = {}

    def ffn1_wd(after):
        late_wd.update(more("ffn1_down", after))
        return late_wd["ffn1_w_down"]

    sd_x = jax.ShapeDtypeStruct((S, D_MODEL), F32)
    sd_h = jax.ShapeDtypeStruct((S, D_MODEL), BF16)

    def norm_after(gain):
        extras = lambda t: [(gain, (1, D_MODEL), _row0)]
        outs = lambda t: [(sd_x, (t, D_MODEL), _mn), (sd_h, (t, D_MODEL), _mn)]

        def finish(i, xo, ex, out):
            out[0][...] = xo
            out[1][...] = _rmsnorm_tile(xo, ex[0][...])

        return extras, outs, finish

    (R["x1"], R["h2"]), R["a1"], R["b1"], R["f1"] = _ffn_fwd(
        "ffn1", x0, R["h1"], W["ffn1_w_gate"], W["ffn1_w_up"], ffn1_wd if more is not None else W["ffn1_w_down"],
        *norm_after(W["mix_norm"]))
    if more is not None:
        W = {**W, **late_wd, **more("mixer", R["x1"])}
    tm = _tile(S, 512)
    tmp = _tile(S, 1024)
    R["p"] = _fused_mm("in_proj", S, D_IN_PAD, D_MODEL, tmp, 640, D_MODEL, [(R["h2"], "mk"), (W["w_in"], "kn")],
                       [(0, 1, 0)], [], [(jax.ShapeDtypeStruct((S, D_IN_PAD), BF16), (tmp, 640), _mn)],
                       lambda i, accs, ex, out: out[0].__setitem__(Ellipsis, accs[0][...].astype(BF16)))[0]
    p = R["p"]
    R["xc"] = _conv_fwd("rg_conv_fwd", p, 0, W["rg_conv_w"], W["rg_conv_b"], "bias")
    R["bd"] = _rg_bd(W["rg_gate_a_w"], W["rg_gate_x_w"])
    R["rg_prm"] = _rg_prm(W["rg_gate_a_b"], W["rg_gate_x_b"], W["rg_lambda"])
    a_f, b_f, a_b, b_b = _rg_gates_fwd(R["xc"], R["bd"], R["rg_prm"])
    R["a_f"], R["a_b"] = a_f, a_b
    R["h_f"], R["h_b"] = _rg_scan("rg_scan_fwd", a_f, b_f, a_b, b_b)
    zero_b = jnp.zeros((1, RG_W), F32)
    cw = W["gdn_conv_w"]
    R["q"] = _conv_fwd("gdn_conv_q", p, 2, cw[:, 0:512], zero_b, "q")
    R["k"] = _conv_fwd("gdn_conv_k", p, 3, cw[:, 512:1024], zero_b, "k")
    R["v"] = _conv_fwd("gdn_conv_v", p, 4, cw[:, 1024:1536], zero_b, "v")
    R["gdn_prm"] = _gdn_prm(W["gdn_a_log"], W["gdn_dt_bias"])
    R["bg"] = _gdn_prep_fwd(p, R["gdn_prm"])
    R["gcr"] = _gc_rows(R["bg"])
    R["gdn_loc"] = _gdn_local_fwd(R["q"], R["k"], R["v"], R["bg"], R["gcr"])
    R["gdn_fwd"] = _gdn_scan_fwd(R["gdn_loc"])
    R["o_f"], R["o_b"] = R["gdn_fwd"][0][0], R["gdn_fwd"][1][0]
    R["y"] = _mix_out_fwd(R["h_f"], R["h_b"], R["o_f"], R["o_b"], p, W["gdn_norm"])
    def epi_out(i, accs, ex, out):
        x2 = ex[0][...] + accs[0][...]
        out[0][...] = x2
        out[1][...] = _rmsnorm_tile(x2, ex[1][...])

    R["x2"], R["h3"] = _fused_mm("out_proj", S, D_MODEL, D_MODEL, tm, D_MODEL, D_MODEL,
                                 [(R["y"], "mk"), (W["w_out"], "kn")], [(0, 1, 0)],
                                 [(R["x1"], (tm, D_MODEL), _mn), (W["ffn2_norm"], (1, D_MODEL), _row0)],
                                 [(sd_x, (tm, D_MODEL), _mn), (sd_h, (tm, D_MODEL), _mn)], epi_out)
    if more is not None:
        W = {**W, **more("ffn2", R["x2"])}

    def loss_finish(i, xo, ex, out):
        gv = ex[1][...]
        r = lax.rsqrt(jnp.mean(xo * xo, axis=-1, keepdims=True) + EPS)
        err = xo * r * gv - ex[0][...]
        dx, dgt = _rmsnorm_bwd_tile(err * (1.0 / D_MODEL), xo, gv)
        out[0][...] = dx
        _colsum_into(out[1], i, jnp.zeros((8, 128), F32) + jnp.sum(err * err) * (0.5 / D_MODEL))
        _colsum_into(out[2], i, jnp.sum(dgt, axis=0, keepdims=True))

    (R["dx3"], R["loss"], R["d_final_norm"]), R["a2"], R["b2"], R["f2"] = _ffn_fwd(
        "ffn2", R["x2"], R["h3"], W["ffn2_w_gate"], W["ffn2_w_up"], W["ffn2_w_down"],
        lambda t: [(target, (t, D_MODEL), _mn), (W["final_norm"], (1, D_MODEL), _row0)],
        lambda t: [(sd_x, (t, D_MODEL), _mn), (jax.ShapeDtypeStruct((8, 128), F32), (8, 128), _row0),
                   (jax.ShapeDtypeStruct((1, D_MODEL), F32), (1, D_MODEL), _row0)],
        loss_finish)
    R["W"] = W
    return R


def _colsum_into(ref, i, val):
    @pl.when(i == 0)
    def _():
        ref[...] = val

    @pl.when(i > 0)
    def _():
        ref[...] += val


def _ffn_bwd(tag, dout, x, g, h, a, b, f, wg, wu, wd, emit):
    S = x.shape[0]
    tm = _tile(S, 512)
    tk_s = _tile(S, 1024)
    dwd = _fused_mm(f"{tag}_dw_down", D_FF, D_MODEL, S, 1408, D_MODEL, tk_s, [(f, "km"), (dout, "kn")], [(0, 1, 0)], [],
                    [(jax.ShapeDtypeStruct((D_FF, D_MODEL), BF16), (1408, D_MODEL), _mn)],
                    lambda i, accs, ex, out: out[0].__setitem__(Ellipsis, (0.5 * accs[0][...]).astype(BF16)))[0]
    emit(down=dwd)

    def epi_act(i, accs, ex, out):
        df = 0.5 * accs[0][...]
        out[0][...] = (df * ex[1][...].astype(F32)).astype(BF16)
        out[1][...] = (df * ex[0][...].astype(F32)).astype(BF16)

    sds = jax.ShapeDtypeStruct((S, D_FF), BF16)
    da, db = _fused_mm(f"{tag}_dact", S, D_FF, D_MODEL, tm, 1408, D_MODEL, [(dout, "mk"), (wd, "nk")], [(0, 1, 0)],
                       [(a, (tm, 1408), _mn), (b, (tm, 1408), _mn)], [(sds, (tm, 1408), _mn)] * 2, epi_act)

    def epi_w2(i, accs, ex, out):
        out[0][...] = accs[0][...].astype(BF16)
        out[1][...] = accs[1][...].astype(BF16)

    sdw = jax.ShapeDtypeStruct((D_MODEL, D_FF), BF16)
    dwg, dwu = _fused_mm(f"{tag}_dw_up", D_MODEL, D_FF, S, D_MODEL, 1408, tk_s,
                         [(h, "km"), (da, "kn"), (db, "kn")], [(0, 1, 0), (0, 2, 1)], [],
                         [(sdw, (D_MODEL, 1408), _mn)] * 2, epi_w2)
    tok = emit(gate=dwg, up=dwu)
    if tok is not None:
        g = g + tok

    def epi_dx(i, accs, ex, out):
        dx, dgt = _rmsnorm_bwd_tile(accs[0][...], ex[0][...], ex[1][...])
        out[0][...] = ex[2][...] + dx
        _colsum_into(out[1], i, jnp.sum(dgt, axis=0, keepdims=True))

    tmx = _tile(S, 1024)
    dx, dg = _fused_mm(f"{tag}_dx", S, D_MODEL, D_FF, tmx, D_MODEL, 1408,
                       [(da, "mk"), (wg, "nk"), (db, "mk"), (wu, "nk")], [(0, 1, 0), (2, 3, 0)],
                       [(x, (tmx, D_MODEL), _mn), (g, (1, D_MODEL), _row0), (dout, (tmx, D_MODEL), _mn)],
                       [(jax.ShapeDtypeStruct((S, D_MODEL), F32), (tmx, D_MODEL), _mn),
                        (jax.ShapeDtypeStruct((1, D_MODEL), F32), (1, D_MODEL), _row0)], epi_dx)
    return dx, dg


def _mix_out_bwd(dx2, w_out, h_f, h_b, o_f, o_b, p, gn):
    S = dx2.shape[0]
    ts = _tile(S, 512)
    c0 = math.sqrt(2.0 / math.pi)

    def epi(i, accs, ex, out):
        hf, hb, of, ob, gate, z, gn_ref = ex
        dhr_ref, dgate_ref, do_ref, dz_ref, dgn_ref = out
        dy_ref = accs[0]
        gv = gate[...].astype(F32)
        ge, t = _gelu(gv)
        dy_rg = dy_ref[:, 0:RG_W]
        dhr_ref[...] = dy_rg * ge
        dgelu = 0.5 * (1.0 + t) + 0.5 * gv * (1.0 - t * t) * c0 * (1.0 + 3.0 * 0.044715 * gv * gv)
        dgate_ref[...] = (dy_rg * (hf[...] + hb[...]) * dgelu).astype(BF16)
        o = of[...] + ob[...]
        zv = z[...].astype(F32)
        sig = _sigmoid(zv)
        gnv = gn_ref[...]
        dgn = jnp.zeros((1, GDN_DK), F32)
        for h in range(GDN_H):
            cols = slice(h * GDN_DK, (h + 1) * GDN_DK)
            oh = o[:, cols]
            r = lax.rsqrt(jnp.mean(oh * oh, axis=-1, keepdims=True) + EPS)
            ohat = oh * r
            dyh = dy_ref[:, RG_W + h * GDN_DK:RG_W + (h + 1) * GDN_DK]
            zh = zv[:, cols]
            sh = sig[:, cols]
            dn = dyh * zh * sh
            dz_ref[:, cols] = (dyh * ohat * gnv * (sh * (1.0 + zh * (1.0 - sh)))).astype(BF16)
            dxn = dn * gnv
            do_ref[:, cols] = r * (dxn - ohat * jnp.mean(dxn * ohat, axis=-1, keepdims=True))
            dgn = dgn + jnp.sum(dn * ohat, axis=0, keepdims=True)
        _colsum_into(dgn_ref, i, dgn)

    blk = (ts, RG_W)
    im = lambda i, j: (i, 0)
    extras = [(h_f, blk, im), (h_b, blk, im), (o_f, blk, im), (o_b, blk, im),
              (p, blk, lambda i, j: (i, 1)), (p, blk, lambda i, j: (i, 5)), (gn, (1, GDN_DK), _row0)]
    outs = [(jax.ShapeDtypeStruct((S, RG_W), F32), blk, im), (jax.ShapeDtypeStruct((S, RG_W), BF16), blk, im),
            (jax.ShapeDtypeStruct((S, RG_W), F32), blk, im), (jax.ShapeDtypeStruct((S, RG_W), BF16), blk, im),
            (jax.ShapeDtypeStruct((1, GDN_DK), F32), (1, GDN_DK), _row0)]
    return _fused_mm("mix_out_bwd", S, D_MODEL, D_MODEL, ts, D_MODEL, D_MODEL, [(dx2, "mk"), (w_out, "nk")], [(0, 1, 0)],
                     extras, outs, epi)


def _rg_scan_adj(name, a_up, b_up, a_dn, b_dn):
    S, C = a_up.shape
    ts = _tile(S, 512)
    n_tiles = S // ts

    def body(au, bu, ad, bd, mu_ref, lam_ref, carry):
        @pl.when(pl.program_id(0) == 0)
        def _():
            carry[...] = jnp.zeros_like(carry)

        n_sub = ts // SUBLANES
        rows = lax.broadcasted_iota(jnp.int32, (SUBLANES, C), 0)

        def half(a_ref, b_ref, out_ref, r0, c_in, reverse):
            a = a_ref[pl.ds(r0, SUBLANES), :]
            b = b_ref[pl.ds(r0, SUBLANES), :]
            cum_a, c0 = _scan_rows(a, a * b, reverse)
            c = c0 + cum_a * c_in
            edge = 0 if not reverse else SUBLANES - 1
            c_prev = jnp.where(rows == edge, c_in, pltpu.roll(c, SUBLANES - 1 if reverse else 1, 0))
            out_ref[pl.ds(r0, SUBLANES), :] = b + c_prev
            return c[0:1, :] if reverse else c[SUBLANES - 1:SUBLANES, :]

        def step(j, c):
            cu, cd = c
            cu = half(au, bu, mu_ref, pl.multiple_of(j * SUBLANES, SUBLANES), cu, False)
            cd = half(ad, bd, lam_ref, pl.multiple_of((n_sub - 1 - j) * SUBLANES, SUBLANES), cd, True)
            return cu, cd

        cu, cd = lax.fori_loop(0, n_sub, step, (carry[0:1, :], carry[1:2, :]), unroll=4)
        carry[0:1, :] = cu
        carry[1:2, :] = cd

    fw = lambda i: (i, 0)
    bw = lambda i: (n_tiles - 1 - i, 0)
    sds = jax.ShapeDtypeStruct((S, C), F32)
    return _rows(name, S, ts,
                 [(a_up, (ts, C), fw), (b_up, (ts, C), fw), (a_dn, (ts, C), bw), (b_dn, (ts, C), bw)],
                 [(sds, (ts, C), fw), (sds, (ts, C), bw)], body, scratch=[pltpu.VMEM((8, C), F32)])


def _halo_ex(arr, S, tm, width):
    per = tm // HALO
    last = S // HALO - 1
    return [
        (arr, (tm, width), lambda i, j: (i, 0)),
        (arr, (HALO, width), lambda i, j: (jnp.maximum(i * per - 1, 0), 0)),
        (arr, (HALO, width), lambda i, j: (jnp.minimum((i + 1) * per, last), 0)),
    ]


def _rg_gates_bwd(xc, bd, prm, lam_f, lam_b, h_f, h_b):
    S = xc.shape[0]
    tm = _tile(S, 256)
    n_tiles = S // tm

    def epi(i, accs, ex, out):
        pre = accs[0][...]
        xv = ex[0][...]
        prm_ref = ex[1]
        lams = (ex[2][...], ex[3][...])
        hprev = (_shift(_ext(ex[4], ex[5], ex[6], i, n_tiles), -1, tm),
                 _shift(_ext(ex[7], ex[8], ex[9], i, n_tiles), 1, tm))
        dxc = jnp.zeros_like(xv)
        rows = []
        dlam_rows = []
        for d in range(2):
            r, ig, sp, a, sq = _rg_gate_terms(pre, xv, prm_ref, d)
            lam = lams[d]
            da = lam * hprev[d]
            di = lam * sq * xv
            dxc = dxc + lam * sq * ig
            dsq = lam * ig * xv
            dlog_a = da * a - dsq * (a * a) / sq
            dpre_r = dlog_a * (-RG_C * sp) * r * (1.0 - r)
            dpre_i = di * ig * (1.0 - ig)
            out[0][:, d * 1024:d * 1024 + RG_W] = dpre_r.astype(BF16)
            out[0][:, d * 1024 + RG_W:(d + 1) * 1024] = dpre_i.astype(BF16)
            rows += [jnp.sum(dpre_r, axis=0, keepdims=True), jnp.sum(dpre_i, axis=0, keepdims=True)]
            dsp = jnp.sum(dlog_a * (-RG_C * r), axis=0, keepdims=True)
            dlam_rows.append(-dsp * _sigmoid(-prm_ref[4 + d:5 + d, :]))
        out[1][...] = dxc
        zero = jnp.zeros((2, RG_W), F32)
        _colsum_into(out[2], i, jnp.concatenate(rows + dlam_rows + [zero], axis=0))

    blk = (tm, RG_W)
    im = lambda i, j: (i, 0)
    extras = ([(xc, blk, im), (prm, (8, RG_W), _row0), (lam_f, blk, im), (lam_b, blk, im)]
              + _halo_ex(h_f, S, tm, RG_W) + _halo_ex(h_b, S, tm, RG_W))
    outs = [(jax.ShapeDtypeStruct((S, 4 * RG_W), BF16), (tm, 4 * RG_W), im),
            (jax.ShapeDtypeStruct((S, RG_W), F32), blk, im),
            (jax.ShapeDtypeStruct((8, RG_W), F32), (8, RG_W), _row0)]
    return _fused_mm("rg_gates_bwd", S, 4 * RG_W, RG_W, tm, 4 * RG_W, RG_W, [(xc, "mk"), (bd, "kn")], [(0, 1, 0)],
                     extras, outs, epi)


def _roll_rows(ext, off):
    if off == 0:
        return ext
    return pltpu.roll(ext, (-off) % ext.shape[0], 0)


def _conv_bwd(name, p, colblk, w, grads, mode):
    S = p.shape[0]
    ts = _tile(S, 512)
    n_tiles = S // ts
    C = w.shape[1]
    ng = len(grads)

    def body(*refs):
        p_refs = refs[0:3]
        g_refs = refs[3:3 + 3 * ng]
        w_ref = refs[3 + 3 * ng]
        dx_ref, dw_ref, db_ref = refs[4 + 3 * ng:]
        i = pl.program_id(0)
        ext_p = _ext(*p_refs, i, n_tiles)
        dn = _ext(*g_refs[0:3], i, n_tiles)
        for gi in range(1, ng):
            dn = dn + _ext(*g_refs[3 * gi:3 * gi + 3], i, n_tiles)
        if mode == "bias":
            dc = dn
        else:
            c = None
            for j in range(CONV_W):
                term = w_ref[j:j + 1, :] * _roll_rows(ext_p, j - 2)
                c = term if c is None else c + term
            sig = _sigmoid(c)
            s = c * sig
            if mode in ("q", "k"):
                scale = GDN_DK ** -0.5 if mode == "q" else 1.0
                parts = []
                for h in range(GDN_H):
                    cols = slice(h * GDN_DK, (h + 1) * GDN_DK)
                    sh = s[:, cols]
                    dnh = dn[:, cols]
                    rinv = lax.rsqrt(jnp.sum(sh * sh, axis=-1, keepdims=True) + EPS)
                    parts.append(scale * rinv * (dnh - sh * (rinv * rinv) * jnp.sum(dnh * sh, axis=-1, keepdims=True)))
                ds = jnp.concatenate(parts, axis=-1)
            else:
                ds = dn
            dc = ds * (sig * (1.0 + c * (1.0 - sig)))
        dx = None
        for j in range(CONV_W):
            term = w_ref[j:j + 1, :] * _shift(dc, 2 - j, ts)
            dx = term if dx is None else dx + term
        dx_ref[...] = dx.astype(BF16)
        dc_main = dc[HALO:HALO + ts]
        dw = jnp.concatenate([jnp.sum(dc_main * _shift(ext_p, j - 2, ts), axis=0, keepdims=True)
                              for j in range(CONV_W)], axis=0)
        _colsum_into(dw_ref, i, dw)
        _colsum_into(db_ref, i, jnp.sum(dc_main, axis=0, keepdims=True))

    ins = _halo_ins(p, S, ts, C, colblk)
    for garr in grads:
        ins += _halo_ins(garr, S, ts, C, 0)
    ins += [(w, (CONV_W, C), lambda i: (0, 0))]
    z0 = lambda i: (0, 0)
    outs = [(jax.ShapeDtypeStruct((S, C), BF16), (ts, C), lambda i: (i, 0)),
            (jax.ShapeDtypeStruct((CONV_W, C), F32), (CONV_W, C), z0),
            (jax.ShapeDtypeStruct((1, C), F32), (1, C), z0)]
    return _rows(name, S, ts, ins, outs, body)


def _gdn_scan_bwd(loc, do):
    S = do.shape[0]
    ts = _tile(S, GDN_TS)
    n_tiles = S // ts
    ncb = ts // CHUNK
    nch = S // CHUNK

    def body(*refs):
        ins = (refs[0:6], refs[6:12])
        outs = (refs[12:14], refs[14:16])
        dstate = refs[16]

        @pl.when(pl.program_id(0) == 0)
        def _():
            dstate[...] = jnp.zeros_like(dstate)

        def chunk(cc, carry):
            chains = []
            for d in range(2):
                c = ncb - 1 - cc if d == 0 else cc
                rows = pl.ds(pl.multiple_of(c * CHUNK, CHUNK), CHUNK)
                for h in range(GDN_H):
                    cols = slice(h * GDN_DK, (h + 1) * GDN_DK)
                    chains.append(dict(d=d, h=h, c=c, rows=rows, cols=cols, dsn=dstate[d * GDN_H + h]))
            for ch in chains:
                qd_ref, kd_ref, cd_ref, w_ref, a_ref, do_ref = ins[ch["d"]]
                rows, cols = ch["rows"], ch["cols"]
                dob = do_ref[rows, cols].astype(BF16)
                ch["dvn"] = (_dot(a_ref[ch["c"], ch["h"]], dob, 0, 0)
                             + _dot(kd_ref[rows, cols], ch["dsn"].astype(BF16), 1, 0))
                ch["qdo"] = _dot(qd_ref[rows, cols], dob, 0, 0)
            for ch in chains:
                w_ref = ins[ch["d"]][3]
                ch["wdvn"] = _dot(w_ref[ch["rows"], ch["cols"]], ch["dvn"].astype(BF16), 0, 0)
            for ch in chains:
                dvn_ref, ds_ref = outs[ch["d"]]
                cd_ref = ins[ch["d"]][2]
                col = ch["d"] * GDN_H + ch["h"]
                dvn_ref[ch["rows"], ch["cols"]] = ch["dvn"].astype(BF16)
                ds_ref[ch["c"], ch["h"]] = ch["dsn"].astype(BF16)
                dstate[ch["d"] * GDN_H + ch["h"]] = (ch["qdo"] + cd_ref[ch["c"], col:col + 1, :] * ch["dsn"]
                                                     - ch["wdvn"])
            return carry

        lax.fori_loop(0, ncb, chunk, 0)

    ins, outs = [], []
    for d in range(2):
        tix = _dir_tile(d, n_tiles, True)
        im = lambda i, tix=tix: (tix(i), 0)
        im4 = lambda i, tix=tix: (tix(i), 0, 0, 0)
        _, w, a, _, qd, kd = loc[d]
        ins += [(qd, (ts, GDN_W), im), (kd, (ts, GDN_W), im), (loc[2], (ncb, 8, 128), lambda i, tix=tix: (tix(i), 0, 0)),
                (w, (ts, GDN_W), im), (a, (ncb, GDN_H, CHUNK, CHUNK), im4), (do, (ts, GDN_W), im)]
        outs += [(jax.ShapeDtypeStruct((S, GDN_W), BF16), (ts, GDN_W), im),
                 (jax.ShapeDtypeStruct((nch, GDN_H, GDN_DK, GDN_DK), BF16), (ncb, GDN_H, GDN_DK, GDN_DK), im4)]
    res = _rows("gdn_scan_bwd", S, ts, ins, outs, body, scratch=[pltpu.VMEM((2 * GDN_H, GDN_DK, GDN_DK), F32)])
    return res[0:2], res[2:4]


def _gdn_local_bwd(q, k, v, bg, gcr, do, loc, fwd, adj):
    S = q.shape[0]
    ts = _tile(S, GDN_TS)
    ncb = ts // CHUNK

    def body(q_ref, k_ref, v_ref, bg_ref, gcr_ref, do_ref, *rest):
        per_dir = (rest[0:5], rest[5:10])
        dq_ref, dk_ref, dv_ref, dbg_ref, dbgr_ref = rest[10:15]
        ri, ci = _tri_masks()
        lane = lax.broadcasted_iota(jnp.int32, (CHUNK, 128), 1)
        rowi = lax.broadcasted_iota(jnp.int32, (CHUNK, 1), 0)
        ones8 = jnp.ones((SUBLANES, CHUNK), F32)

        def chunk(c, carry):
            r0 = pl.multiple_of(c * CHUNK, CHUNK)
            rows = pl.ds(r0, CHUNK)
            chains = []
            for h in range(GDN_H):
                cols = slice(h * GDN_DK, (h + 1) * GDN_DK)
                qh, kh, vh = q_ref[rows, cols], k_ref[rows, cols], v_ref[rows, cols]
                dob = do_ref[rows, cols].astype(BF16)
                both = _bdot(jnp.concatenate([qh, kh], axis=0), kh, 1, 1)
                for d in range(2):
                    chains.append(dict(h=h, d=d, cols=cols, qh=qh, kh=kh, vh=vh, dob=dob, qk=both[0:CHUNK],
                                       kk=both[CHUNK:2 * CHUNK], col=d * GDN_H + h))
            for ch in chains:
                m = _gdn_decay(bg_ref, gcr_ref, c, rows, r0, ch["col"], ch["d"] == 1, ri, ci)
                t_ref, s_ref, ds_ref, vn_ref, dvn_ref = per_dir[ch["d"]]
                h, cols = ch["h"], ch["cols"]
                ch["m"] = m
                ch["kb"] = ch["kh"] * m["beta"]
                ch["kbg"] = ch["kb"] * m["eg"]
                ch["t"] = t_ref[c, h]
                stb = s_ref[c, h]
                ch["dsn"] = ds_ref[c, h]
                vnb = vn_ref[rows, cols]
                dvnb = dvn_ref[rows, cols]
                ch["dcd"] = jnp.sum(jnp.sum(stb.astype(F32) * ch["dsn"].astype(F32), axis=1, keepdims=True),
                                    axis=0, keepdims=True)
                ch["dqd"] = _dot(ch["dob"], stb, 1, 1)
                ch["d_a"] = _dot(ch["dob"], vnb, 1, 1)
                ch["dkd"] = _bdot(vnb, ch["dsn"], 1, 1)
                ch["dw"] = -_dot(dvnb, stb, 1, 1)
                ch["dvb"] = _dot(ch["t"], dvnb, 1, 0)
                ch["d_t"] = _bdot(dvnb, ch["vh"] * m["beta"], 1, 1)
            for ch in chains:
                dwb = ch["dw"].astype(BF16)
                ch["d_t"] = ch["d_t"] + _bdot(dwb, ch["kbg"], 1, 1)
                ch["dkbg"] = _dot(ch["t"], dwb, 1, 0)
                ch["nn"] = ch["d_a"] * ch["m"]["dm"]
                ch["nn_q"] = _bdot(ch["nn"], ch["qh"], 0, 0)
                ch["nn_k"] = _bdot(ch["nn"], ch["kh"], 1, 0)
            for ch in chains:
                ch["x"] = _dot(ch["d_t"].astype(BF16), ch["t"], 1, 0)
            for ch in chains:
                d_l = -_dot(ch["t"], ch["x"].astype(BF16), 1, 0)
                ch["d_l"] = jnp.where(ch["m"]["strict"], d_l, 0.0)
                ch["mm"] = ch["d_l"] * ch["m"]["dm"]
            for ch in chains:
                m = ch["m"]
                ch["mm_kh"] = _bdot(ch["mm"], ch["kh"], 1, 0)
                ch["mm_kb"] = _bdot(ch["mm"], ch["kb"], 0, 0)
                l_mat = jnp.where(m["strict"], m["beta"] * ch["kk"] * m["dm"], 0.0)
                ch["e"] = ch["d_l"] * l_mat + ch["nn"] * ch["qk"]
                dbgr_ref[c, ch["col"]:ch["col"] + 1, :] = -_dot(ones8, ch["e"], 1, 0, HI)[0:1, :]
            acc_bg = jnp.zeros((CHUNK, 128), F32)
            acc = {}
            for ch in chains:
                m = ch["m"]
                beta, eg, egl = m["beta"], m["eg"], m["egl"]
                dkb = ch["mm_kh"] + ch["dkbg"] * eg
                dk_d = ch["mm_kb"] + ch["nn_q"] + ch["dkd"] * egl + dkb * beta
                dq_d = ch["nn_k"] + ch["dqd"] * eg
                dv_d = ch["dvb"] * beta
                dkd_kd = ch["dkd"] * (ch["kh"] * egl)
                dgc = (jnp.sum(ch["e"], axis=1, keepdims=True)
                       + jnp.sum(ch["dqd"] * (ch["qh"] * eg) - dkd_kd + ch["dkbg"] * ch["kbg"], axis=1, keepdims=True))
                dgl = jnp.sum(jnp.sum(dkd_kd, axis=1, keepdims=True), axis=0, keepdims=True) + ch["dcd"] * m["cd"]
                dgc = dgc + jnp.where(rowi == (0 if ch["d"] == 1 else CHUNK - 1), dgl, 0.0)
                dbeta = jnp.sum(dkb * ch["kh"] + ch["dvb"] * ch["vh"], axis=1, keepdims=True)
                acc_bg = acc_bg + jnp.where(lane == ch["col"], dbeta, 0.0) + jnp.where(lane == 8 + ch["col"], dgc, 0.0)
                if ch["d"] == 0:
                    acc[ch["h"]] = (dq_d, dk_d, dv_d)
                else:
                    dq0, dk0, dv0 = acc[ch["h"]]
                    dq_ref[rows, ch["cols"]] = dq0 + dq_d
                    dk_ref[rows, ch["cols"]] = dk0 + dk_d
                    dv_ref[rows, ch["cols"]] = dv0 + dv_d
            dbg_ref[rows, :] = acc_bg
            return carry

        lax.fori_loop(0, ncb, chunk, 0)

    im = lambda i: (i, 0)
    im4 = lambda i: (i, 0, 0, 0)
    blk = (ts, GDN_W)
    ins = [(q, blk, im), (k, blk, im), (v, blk, im), (bg, (ts, 128), im), (gcr, (ncb, 8, CHUNK), lambda i: (i, 0, 0)),
           (do, blk, im)]
    for d in range(2):
        ins += [(loc[d][3], (ncb, GDN_H, CHUNK, CHUNK), im4), (fwd[d][2], (ncb, GDN_H, GDN_DK, GDN_DK), im4),
                (adj[d][1], (ncb, GDN_H, GDN_DK, GDN_DK), im4), (fwd[d][1], blk, im), (adj[d][0], blk, im)]
    sds = jax.ShapeDtypeStruct((S, GDN_W), F32)
    outs = [(sds, blk, im), (sds, blk, im), (sds, blk, im), (jax.ShapeDtypeStruct((S, 128), F32), (ts, 128), im),
            (jax.ShapeDtypeStruct((S // CHUNK, 8, CHUNK), F32), (ncb, 8, CHUNK), lambda i: (i, 0, 0))]
    dq, dk, dv, dbg, dbg_rows = _rows("gdn_local_bwd", S, ts, ins, outs, body)
    dgc_cols = dbg_rows.transpose(0, 2, 1).reshape(S, 8)
    return dq, dk, dv, dbg + jnp.pad(dgc_cols, ((0, 0), (8, 112)))


def _gdn_prep_bwd(dbg_all, p, prm):
    S = p.shape[0]
    ts = _tile(S, 512)

    def body(dbg_ref, p_ref, prm_ref, dba_ref, dprm_ref):
        i = pl.program_id(0)
        raw = p_ref[...].astype(F32)
        dbg = dbg_ref[...]
        lane = lax.broadcasted_iota(jnp.int32, (1, 128), 1)
        is_g = (lane >= 8) & (lane < 16)
        ea = jnp.exp(prm_ref[0:1, :])
        arg = raw + prm_ref[1:2, :]
        g = jnp.where(is_g, -ea * _softplus(arg), 0.0)
        beta = _sigmoid(raw)
        dgc = jnp.where(is_g, dbg, 0.0)
        ri, ci = _tri_masks()
        lower = (ri >= ci).astype(F32)
        upper = (ri <= ci).astype(F32)
        dgs = []
        for c in range(ts // CHUNK):
            ch = dgc[c * CHUNK:(c + 1) * CHUNK]
            dgs.append(jnp.where(lane < 12, _dot(upper, ch, 1, 0, HI), _dot(lower, ch, 1, 0, HI)))
        dg = jnp.concatenate(dgs, axis=0)
        dalpha = jnp.where(is_g, dg * (-ea) * _sigmoid(arg), 0.0)
        dba_ref[...] = jnp.where(lane < 8, dbg * beta * (1.0 - beta), dalpha).astype(BF16)
        rows = jnp.concatenate([jnp.sum(dg * g, axis=0, keepdims=True), jnp.sum(dalpha, axis=0, keepdims=True),
                                jnp.zeros((6, 128), F32)], axis=0)
        _colsum_into(dprm_ref, i, rows)

    im = lambda i: (i, 0)
    z0 = lambda i: (0, 0)
    return _rows("gdn_prep_bwd", S, ts,
                 [(dbg_all, (ts, 128), im), (p, (ts, 128), lambda i: (i, COL_BA // 128)), (prm, (8, 128), z0)],
                 [(jax.ShapeDtypeStruct((S, 128), BF16), (ts, 128), im), (jax.ShapeDtypeStruct((8, 128), F32), (8, 128), z0)],
                 body)


def _mm_plain(name, M, N, K, tm, tn, tk, a, am, b, bm, dtype):
    return _fused_mm(name, M, N, K, tm, tn, tk, [(a, am), (b, bm)], [(0, 1, 0)], [],
                     [(jax.ShapeDtypeStruct((M, N), dtype), (tm, tn), _mn)],
                     lambda i, accs, ex, out: out[0].__setitem__(Ellipsis, accs[0][...].astype(dtype)))[0]


def _layer_bwd(x0, W, R, emit_big=None, emit_small=None):
    S = x0.shape[0]
    tm = _tile(S, 512)
    tk_s = _tile(S, 1024)
    G = {}

    def emit(**named):
        if emit_big is None:
            G.update(named)
            return None
        return emit_big(**named)

    def ffn_emit(prefix):
        return lambda **kw: emit(**{f"{prefix}_w_{k}": v for k, v in kw.items()})

    dx2, G["ffn2_norm"] = _ffn_bwd("ffn2b", R["dx3"], R["x2"], W["ffn2_norm"], R["h3"], R["a2"], R["b2"], R["f2"],
                                   W["ffn2_w_gate"], W["ffn2_w_up"], W["ffn2_w_down"], ffn_emit("ffn2"))
    tok = emit(w_out=_mm_plain("dw_out", D_MODEL, D_MODEL, S, D_MODEL, D_MODEL, tk_s, R["y"], "km", dx2, "kn", BF16))
    gn = W["gdn_norm"] if tok is None else W["gdn_norm"] + tok
    p = R["p"]
    dhr, dgate, do, dz, G["gdn_norm"] = _mix_out_bwd(dx2, W["w_out"], R["h_f"], R["h_b"], R["o_f"], R["o_b"], p, gn)
    lam_b, lam_f = _rg_scan_adj("rg_scan_bwd", R["a_b"], dhr, R["a_f"], dhr)
    dpre, dxc_direct, d_rgprm = _rg_gates_bwd(R["xc"], R["bd"], R["rg_prm"], lam_f, lam_b, R["h_f"], R["h_b"])
    tmg = _tile(S, 512)
    dxc = _fused_mm("rg_dxc", S, RG_W, 4 * RG_W, tmg, RG_W, 4 * RG_W, [(dpre, "mk"), (R["bd"], "nk")], [(0, 1, 0)],
                    [(dxc_direct, (tmg, RG_W), _mn)], [(jax.ShapeDtypeStruct((S, RG_W), F32), (tmg, RG_W), _mn)],
                    lambda i, accs, ex, out: out[0].__setitem__(Ellipsis, ex[0][...] + accs[0][...]))[0]
    d_bd = _mm_plain("rg_dbd", RG_W, 4 * RG_W, S, RG_W, 4 * RG_W, tk_s, R["xc"], "km", dpre, "kn", F32)
    dx_rg, G["rg_conv_w"], G["rg_conv_b"] = _conv_bwd("rg_conv_bwd", p, 0, W["rg_conv_w"], [dxc], "bias")
    blocks = jnp.einsum("nigmj,nm->gnij", d_bd.reshape(RG_BLOCKS, RG_BLOCK, 4, RG_BLOCKS, RG_BLOCK),
                        jnp.eye(RG_BLOCKS, dtype=F32))
    G["rg_gate_a_w"] = jnp.stack([blocks[0], blocks[2]])
    G["rg_gate_x_w"] = jnp.stack([blocks[1], blocks[3]])
    G["rg_gate_a_b"] = jnp.stack([d_rgprm[0], d_rgprm[2]])
    G["rg_gate_x_b"] = jnp.stack([d_rgprm[1], d_rgprm[3]])
    G["rg_lambda"] = d_rgprm[4:6]
    adj = _gdn_scan_bwd(R["gdn_loc"], do)
    dq, dk, dv, dbg = _gdn_local_bwd(R["q"], R["k"], R["v"], R["bg"], R["gcr"], do, R["gdn_loc"], R["gdn_fwd"], adj)
    cw = W["gdn_conv_w"]
    dpq, dwq, _ = _conv_bwd("gdn_conv_q_bwd", p, 2, cw[:, 0:512], [dq], "q")
    dpk, dwk, _ = _conv_bwd("gdn_conv_k_bwd", p, 3, cw[:, 512:1024], [dk], "k")
    dpv, dwv, _ = _conv_bwd("gdn_conv_v_bwd", p, 4, cw[:, 1024:1536], [dv], "v")
    G["gdn_conv_w"] = jnp.concatenate([dwq, dwk, dwv], axis=1)
    dba, d_gprm = _gdn_prep_bwd(dbg, p, R["gdn_prm"])
    G["gdn_a_log"] = d_gprm[0, 8:16].reshape(2, GDN_H)
    G["gdn_dt_bias"] = d_gprm[1, 8:16].reshape(2, GDN_H)
    dp = jnp.concatenate([dx_rg, dgate, dpq, dpk, dpv, dz, dba], axis=1)
    tok = emit(w_in=_mm_plain("dw_in", D_MODEL, D_IN_PAD, S, D_MODEL, 640, tk_s, R["h2"], "km", dp, "kn", BF16))
    g_mix = W["mix_norm"] if tok is None else W["mix_norm"] + tok

    def epi_dx1(i, accs, ex, out):
        dx, dgt = _rmsnorm_bwd_tile(accs[0][...], ex[0][...], ex[1][...])
        out[0][...] = ex[2][...] + dx
        _colsum_into(out[1], i, jnp.sum(dgt, axis=0, keepdims=True))

    dx1, G["mix_norm"] = _fused_mm(
        "mix_dx", S, D_MODEL, D_IN_PAD, tm, D_MODEL, D_IN_PAD, [(dp, "mk"), (W["w_in"], "nk")], [(0, 1, 0)],
        [(R["x1"], (tm, D_MODEL), _mn), (g_mix, (1, D_MODEL), _row0), (dx2, (tm, D_MODEL), _mn)],
        [(jax.ShapeDtypeStruct((S, D_MODEL), F32), (tm, D_MODEL), _mn),
         (jax.ShapeDtypeStruct((1, D_MODEL), F32), (1, D_MODEL), _row0)], epi_dx1)
    G["final_norm"] = R["d_final_norm"]
    if emit_small is not None:
        emit_small(G)
    dx0, G["ffn1_norm"] = _ffn_bwd("ffn1b", dx1, x0, W["ffn1_norm"], R["h1"], R["a1"], R["b1"], R["f1"],
                                   W["ffn1_w_gate"], W["ffn1_w_up"], W["ffn1_w_down"], ffn_emit("ffn1"))
    return dx0, G


def _mesh_pos():
    x, y, c = lax.axis_index("x"), lax.axis_index("y"), lax.axis_index("c")
    return x, y, c, 4 * x + 2 * y + c


def _peer(x, y, c, r):
    px = 1 - x if r & 4 else x
    py = 1 - y if r & 2 else y
    pc = 1 - c if r & 1 else c
    return (px, py, pc), 4 * px + 2 * py + pc


_HBM = pl.BlockSpec(memory_space=pltpu.HBM)
_SEM = pl.BlockSpec(memory_space=pltpu.SEMAPHORE)


def _peer_copies(scatter, srcs, lands, send_sems, recv_sems):
    x, y, c, me = _mesh_pos()
    copies = []
    for a, (src, land) in enumerate(zip(srcs, lands)):
        for r in range(1, N_DEV):
            peer, peer_idx = _peer(x, y, c, r)
            copies.append(pltpu.make_async_remote_copy(
                src_ref=src.at[peer_idx] if scatter else src, dst_ref=land.at[r - 1] if scatter else land.at[me],
                send_sem=send_sems.at[a * 7 + r - 1], recv_sem=recv_sems.at[a * 7 + r - 1],
                device_id=peer, device_id_type=pl.DeviceIdType.MESH))
    return copies


def _exchange_start(name, scatter, arrays):
    slabs = arrays
    n = len(slabs)

    def body(*refs):
        srcs, lands = refs[0:n], refs[n:2 * n]
        send_sems, recv_sems = refs[2 * n], refs[2 * n + 1]
        token = refs[4 * n + 2]
        for cp in _peer_copies(scatter, srcs, lands, send_sems, recv_sems):
            cp.start()
        token[...] = jnp.zeros_like(token)

    land_shapes = [(N_DEV - 1,) + s.shape[1:] if scatter else (N_DEV,) + s.shape for s in slabs]
    n_sems = 7 * n
    out_shape = ([pltpu.SemaphoreType.DMA((n_sems,)), pltpu.SemaphoreType.DMA((n_sems,))]
                 + [pltpu.HBM(s.shape, s.dtype) for s in slabs]
                 + [pltpu.HBM(shp, s.dtype) for shp, s in zip(land_shapes, slabs)]
                 + [jax.ShapeDtypeStruct((8, 128), F32)])
    res = pl.pallas_call(
        body, name=name, out_shape=out_shape, in_specs=[_HBM] * (2 * n),
        out_specs=[_SEM, _SEM] + [_HBM] * (2 * n) + [pl.BlockSpec(memory_space=pltpu.VMEM)],
        input_output_aliases={i: 2 + i for i in range(2 * n)},
        compiler_params=pltpu.CompilerParams(has_side_effects=pltpu.SideEffectType.DATAFLOW_SIDE_EFFECTING),
    )(*[pltpu.with_memory_space_constraint(s, pltpu.HBM) for s in slabs],
      *[pltpu.with_memory_space_constraint(lax.empty(shp, s.dtype), pltpu.HBM) for shp, s in zip(land_shapes, slabs)])
    return dict(n=n, scatter=scatter, sems=res[0:2], srcs=res[2:2 + n], lands=res[2 + n:2 + 2 * n],
                token=res[2 + 2 * n][0, 0])


def _exchange_wait(name, started, after):
    n = started["n"]
    scatter = started["scatter"]

    def body(*refs):
        srcs, lands = refs[0:n], refs[n:2 * n]
        send_sems, recv_sems = refs[2 * n], refs[2 * n + 1]
        for cp in _peer_copies(scatter, srcs, lands, send_sems, recv_sems):
            cp.wait_send()
            cp.wait_recv()

    arrays = list(started["srcs"]) + list(started["lands"])
    res = pl.pallas_call(
        body, name=name, out_shape=[pltpu.HBM(a.shape, a.dtype) for a in arrays],
        in_specs=[_HBM] * (2 * n) + [_SEM, _SEM, pl.BlockSpec(memory_space=pl.ANY)], out_specs=[_HBM] * (2 * n),
        input_output_aliases={i: i for i in range(2 * n)},
        compiler_params=pltpu.CompilerParams(has_side_effects=pltpu.SideEffectType.DATAFLOW_SIDE_EFFECTING),
    )(*arrays, *started["sems"], after)
    return res[0:n], res[n:2 * n]


def _adamw_math(w, g, m, v):
    m2 = ADAM_B1 * m + (1.0 - ADAM_B1) * g
    v2 = ADAM_B2 * v + (1.0 - ADAM_B2) * (g * g)
    m_hat = m2 / (1.0 - ADAM_B1 ** ADAM_STEP)
    v_hat = v2 / (1.0 - ADAM_B2 ** ADAM_STEP)
    delta = -ADAM_LR * (m_hat / (jnp.sqrt(v_hat) + ADAM_EPS) + ADAM_WD * w)
    return delta, m2, v2


def _adamw_slabs(name, src, land, me, w, m, v, tr):
    R, C = w.shape

    def body(me_ref, own_ref, land_ref, w_ref, m_ref, v_ref, g_ref, d_ref, m2_ref, v2_ref):
        g = own_ref[0].astype(F32)
        for s in range(N_DEV - 1):
            g = g + land_ref[s].astype(F32)
        delta, m2, v2 = _adamw_math(w_ref[...], g, m_ref[...], v_ref[...])
        g_ref[...] = g
        d_ref[...] = delta
        m2_ref[...] = m2
        v2_ref[...] = v2

    im = lambda i, me_ref: (i, 0)
    grid_spec = pltpu.PrefetchScalarGridSpec(
        num_scalar_prefetch=1, grid=(R // tr,),
        in_specs=[pl.BlockSpec((1, tr, C), lambda i, me_ref: (me_ref[0], i, 0)),
                  pl.BlockSpec((N_DEV - 1, tr, C), lambda i, me_ref: (0, i, 0)),
                  pl.BlockSpec((tr, C), im), pl.BlockSpec((tr, C), im), pl.BlockSpec((tr, C), im)],
        out_specs=[pl.BlockSpec((tr, C), im)] * 4)
    return pl.pallas_call(body, name=name, grid_spec=grid_spec, out_shape=[jax.ShapeDtypeStruct((R, C), F32)] * 4,
                          compiler_params=_cp(1))(me.reshape(1).astype(jnp.int32), src, land, w, m, v)


def _sum_slots(name, slots):
    _, R, C = slots.shape

    def body(s_ref, o_ref):
        g = s_ref[0]
        for s in range(1, N_DEV):
            g = g + s_ref[s]
        o_ref[...] = g

    return _rows(name, R, R, [(slots, (N_DEV, R, C), lambda i: (0, 0, 0))],
                 [(jax.ShapeDtypeStruct((R, C), F32), (R, C), lambda i: (0, 0))], body)[0]


def _adamw_packed(name, g, w, m, v):
    R, C = g.shape

    def body(g_ref, w_ref, m_ref, v_ref, d_ref, m2_ref, v2_ref):
        delta, m2, v2 = _adamw_math(w_ref[...], g_ref[...], m_ref[...], v_ref[...])
        d_ref[...] = delta
        m2_ref[...] = m2
        v2_ref[...] = v2

    im = lambda i: (0, 0)
    sds = jax.ShapeDtypeStruct((R, C), F32)
    return _rows(name, R, R, [(a, (R, C), im) for a in (g, w, m, v)], [(sds, (R, C), im)] * 3, body)


def _pack(arrays):
    rows = []
    for a in arrays:
        flat = a.reshape(-1).astype(F32)
        pad = (-flat.shape[0]) % 128
        rows.append(jnp.pad(flat, (0, pad)).reshape(-1, 128))
    out = jnp.concatenate(rows, axis=0)
    return jnp.pad(out, ((0, (-out.shape[0]) % 8), (0, 0)))


def _unpack(packed, shapes):
    lead = packed.shape[:-2]
    outs = []
    r = 0
    for shp in shapes:
        n = math.prod(shp)
        nr = -(-n // 128)
        flat = packed[..., r:r + nr, :].reshape(lead + (nr * 128,))[..., :n]
        outs.append(flat.reshape(lead + tuple(shp)))
        r += nr
    return outs


FFN1_BIG = ["ffn1_w_gate", "ffn1_w_up", "ffn1_w_down"]
MIX_BIG = ["w_in", "w_out"]
FFN2_BIG = ["ffn2_w_gate", "ffn2_w_up", "ffn2_w_down"]
BIG = FFN1_BIG + MIX_BIG + FFN2_BIG
COL_SHARDED = {"ffn1_w_gate", "ffn1_w_up", "w_in", "ffn2_w_gate", "ffn2_w_up"}
SMALL_SHARDED = ["rg_conv_w", "rg_gate_a_b", "rg_gate_x_b", "rg_lambda", "gdn_conv_w"]
WEIGHTS = ["ffn1_norm", "ffn1_w_gate", "ffn1_w_up", "ffn1_w_down", "mix_norm", "w_in", "w_out", "rg_conv_w", "rg_conv_b",
           "rg_gate_a_w", "rg_gate_a_b", "rg_gate_x_w", "rg_gate_x_b", "rg_lambda", "gdn_conv_w", "gdn_a_log",
           "gdn_dt_bias", "gdn_norm", "ffn2_norm", "ffn2_w_gate", "ffn2_w_up", "ffn2_w_down", "final_norm"]
SMALL = [n for n in WEIGHTS if n not in BIG]
ROW_VECTORS = {"ffn1_norm", "mix_norm", "ffn2_norm", "gdn_norm", "rg_conv_b", "final_norm"}
ROW_TILE = {"ffn1_w_gate": 256, "ffn1_w_up": 256, "ffn1_w_down": 176, "w_in": 256, "w_out": 64,
            "ffn2_w_gate": 256, "ffn2_w_up": 256, "ffn2_w_down": 176}


def _unshard_cols(g):
    return g.transpose(1, 0, 2).reshape(g.shape[1], N_DEV * g.shape[2])


def _to_slabs(name, g):
    if name in COL_SHARDED:
        r, ctot = g.shape
        return g.reshape(r, N_DEV, ctot // N_DEV).transpose(1, 0, 2)
    return g.reshape(N_DEV, g.shape[0] // N_DEV, g.shape[1])


def _step(x, target, w, m, v):
    _, _, _, me = _mesh_pos()
    def unshard(n, gth):
        full = _unshard_cols(gth) if n in COL_SHARDED else gth.reshape(-1, gth.shape[-1])
        return jnp.pad(full, ((0, 0), (0, D_IN_PAD - D_IN))) if n == "w_in" else full

    def landed(started, name, after):
        srcs, lands = _exchange_wait(name, started, after)
        def with_own(src, land):
            slot = lax.broadcasted_iota(jnp.int32, (N_DEV,) + (1,) * src.ndim, 0)
            return jnp.where(slot == me, src[None], land)

        return [with_own(src, land) for src, land in zip(srcs, lands)]

    up_names = ["ffn1_w_gate", "ffn1_w_up"]
    small_shards = [w[n] for n in SMALL_SHARDED]
    st_up = _exchange_start("gather_ffn1_up_start", False, [w[n].astype(BF16) for n in up_names])
    tok = st_up["token"]
    st_down = _exchange_start("gather_ffn1_down_start", False, [(w["ffn1_w_down"] + tok).astype(BF16)])
    tok = tok + st_down["token"]
    st_mix = _exchange_start("gather_mix_start", False,
                             [(w[n] + tok).astype(BF16) for n in MIX_BIG] + [_pack(small_shards) + tok])
    tok = tok + st_mix["token"]
    st_ffn2 = _exchange_start("gather_ffn2_start", False, [(w[n] + tok).astype(BF16) for n in FFN2_BIG])
    W = {n: w[n] for n in SMALL if n not in SMALL_SHARDED}
    W["ffn1_norm"] = w["ffn1_norm"] + (tok + st_ffn2["token"])

    def more(stage, after):
        if stage == "ffn1_up":
            return {n: unshard(n, gth) for n, gth in zip(up_names, landed(st_up, "gather_ffn1_up_wait", after))}
        if stage == "ffn1_down":
            return {"ffn1_w_down": unshard("ffn1_w_down", landed(st_down, "gather_ffn1_down_wait", after)[0])}
        if stage == "ffn2":
            return {n: unshard(n, gth) for n, gth in zip(FFN2_BIG, landed(st_ffn2, "gather_ffn2_wait", after))}
        got = landed(st_mix, "gather_mix_wait", after)
        new = {n: unshard(n, gth) for n, gth in zip(MIX_BIG, got)}
        for n, gth in zip(SMALL_SHARDED, _unpack(got[-1], [s.shape for s in small_shards])):
            new[n] = jnp.moveaxis(gth, 0, -2).reshape(gth.shape[1:-1] + (N_DEV * gth.shape[-1],))
        return new

    R = _layer_fwd(x, target, W, more)
    W = R["W"]
    pending = []

    def emit_big(**named):
        slabs = [_to_slabs(n, g[:, :D_IN] if n == "w_in" else g) for n, g in named.items()]
        started = _exchange_start(f"scatter_start_{len(pending)}", True, slabs)
        pending.append((list(named), started))
        return started["token"]

    small_started = []

    def emit_small(G):
        packed = _pack([G[n] for n in SMALL if n != "ffn1_norm"])
        small_started.append(_exchange_start("gather_small_start", False, [packed]))

    grad_x, G = _layer_bwd(x, W, R, emit_big, emit_small)
    st_late = _exchange_start("gather_ffn1_norm_start", False, [_pack([G["ffn1_norm"]])])
    loss = lax.psum(R["loss"][0, 0], ("x", "y", "c"))
    out = {}

    def finish(i, after):
        names, started = pending[i]
        srcs, lands = _exchange_wait(f"scatter_wait_{i}", started, after)
        for n, src, land in zip(names, srcs, lands):
            out[n] = _adamw_slabs(f"adamw_{n}", src, land, me, w[n], m[n], v[n], ROW_TILE[n])

    n_early = len(pending) - 2
    for i in range(n_early):
        finish(i, grad_x)
    early = [n for n in SMALL if n != "ffn1_norm"]
    srcs, lands = _exchange_wait("gather_small_wait", small_started[0], grad_x)
    slot = lax.broadcasted_iota(jnp.int32, (N_DEV, 1, 1), 0)
    slots = jnp.where(slot == me, srcs[0][None], lands[0])
    reduced = dict(zip(early, _unpack(_sum_slots("sum_small_grads", slots), [G[n].shape for n in early])))

    def adamw_small(name, names):
        g_small = []
        for n in names:
            g = reduced[n]
            if n in SMALL_SHARDED:
                per = g.shape[-1] // N_DEV
                g = lax.dynamic_slice_in_dim(g, me * per, per, axis=g.ndim - 1)
            g_small.append(g.reshape(w[n].shape))
        shapes = [w[n].shape for n in names]
        d_p, m_p, v_p = _adamw_packed(name, _pack(g_small), _pack([w[n] for n in names]),
                                      _pack([m[n] for n in names]), _pack([v[n] for n in names]))
        for n, g, d_, m_, v_ in zip(names, g_small, _unpack(d_p, shapes), _unpack(m_p, shapes), _unpack(v_p, shapes)):
            out[n] = (g, d_, m_, v_)
        return d_p

    done_early = adamw_small("adamw_small", early)
    srcs, lands = _exchange_wait("gather_ffn1_norm_wait", st_late, done_early)
    late = jnp.where(slot == me, srcs[0][None], lands[0])
    reduced["ffn1_norm"] = _unpack(_sum_slots("sum_ffn1_norm_grad", late), [G["ffn1_norm"].shape])[0]
    done = adamw_small("adamw_ffn1_norm", ["ffn1_norm"])
    for i in range(n_early, len(pending)):
        finish(i, done)
    return loss, grad_x, out


def kernel(x, ffn1_norm, ffn1_w_gate, ffn1_w_up, ffn1_w_down, mix_norm, w_in, w_out, rg_conv_w, rg_conv_b, rg_gate_a_w, rg_gate_a_b, rg_gate_x_w, rg_gate_x_b, rg_lambda, gdn_conv_w, gdn_a_log, gdn_dt_bias, gdn_norm, ffn2_norm, ffn2_w_gate, ffn2_w_up, ffn2_w_down, final_norm, loss_target, m_ffn1_norm, m_ffn1_w_gate, m_ffn1_w_up, m_ffn1_w_down, m_mix_norm, m_w_in, m_w_out, m_rg_conv_w, m_rg_conv_b, m_rg_gate_a_w, m_rg_gate_a_b, m_rg_gate_x_w, m_rg_gate_x_b, m_rg_lambda, m_gdn_conv_w, m_gdn_a_log, m_gdn_dt_bias, m_gdn_norm, m_ffn2_norm, m_ffn2_w_gate, m_ffn2_w_up, m_ffn2_w_down, m_final_norm, v_ffn1_norm, v_ffn1_w_gate, v_ffn1_w_up, v_ffn1_w_down, v_mix_norm, v_w_in, v_w_out, v_rg_conv_w, v_rg_conv_b, v_rg_gate_a_w, v_rg_gate_a_b, v_rg_gate_x_w, v_rg_gate_x_b, v_rg_lambda, v_gdn_conv_w, v_gdn_a_log, v_gdn_dt_bias, v_gdn_norm, v_ffn2_norm, v_ffn2_w_gate, v_ffn2_w_up, v_ffn2_w_down, v_final_norm):
    args = dict(locals())
    orig_shapes = {n: args[n].shape for n in WEIGHTS}

    def local(prefix):
        d = {}
        for n in WEIGHTS:
            a = args[prefix + n]
            d[n] = a.reshape(1, -1) if n in ROW_VECTORS else a[0]
        return d

    loss, grad_x, out = _step(x[0], loss_target[0], local(""), local("m_"), local("v_"))
    res = [loss, grad_x[None]]
    for k in range(4):
        res += [out[n][k].reshape(orig_shapes[n]) for n in WEIGHTS]
    return tuple(res)
```

```python
import functools
import math

import jax
import jax.numpy as jnp
from jax import lax
from jax.experimental import pallas as pl
from jax.experimental.pallas import tpu as pltpu

F32, BF16 = jnp.float32, jnp.bfloat16

D_MODEL = 1024
D_FF = 2816
RG_W = 512
RG_BLOCKS = 8
RG_BLOCK = 64
RG_C = 8.0
CONV_W = 4
GDN_H = 4
GDN_DK = 128
CHUNK = 64
EPS = 1e-6
D_IN = 3088
D_IN_PAD = 3200
COL_BA = 3072
N_DEV = 8
HALO = 16
VMEM_LIMIT = 48 * 1024 * 1024
VMEM_CAP = 60 * 1024 * 1024

ADAM_LR = 0.001
ADAM_B1 = 0.9
ADAM_B2 = 0.999
ADAM_EPS = 1e-08
ADAM_WD = 0.01
ADAM_STEP = 10

HI = lax.Precision.HIGHEST


def _cp(n, vmem_limit=None):
    return pltpu.CompilerParams(dimension_semantics=("arbitrary",) * n,
                                vmem_limit_bytes=VMEM_LIMIT if vmem_limit is None else vmem_limit)


def _matmul_vmem_limit(block_bytes, acc_bytes):
    need = 2 * block_bytes + 2 * acc_bytes
    return int(min(VMEM_CAP, max(VMEM_LIMIT, need * 4 // 3)))


def _tile(n, pref):
    return min(n, pref)


def _sigmoid(x):
    return 0.5 * jnp.tanh(0.5 * x) + 0.5


def _softplus(x):
    return jnp.maximum(x, 0.0) + jnp.log(1.0 + jnp.exp(-jnp.abs(x)))


def _dot(a, b, ca, cb, prec=None):
    return lax.dot_general(a, b, (((ca,), (cb,)), ((), ())), preferred_element_type=F32, precision=prec)


def _fused_mm(name, M, N, K, tm, tn, tk, ops, pairs, extras, outs, epilogue):
    nm, nn, nk = M // tm, N // tn, K // tk
    assert nm * tm == M and nn * tn == N and nk * tk == K, (name, M, N, K, tm, tn, tk)
    spec_of = {
        "mk": pl.BlockSpec((tm, tk), lambda i, j, k: (i, k)),
        "km": pl.BlockSpec((tk, tm), lambda i, j, k: (k, i)),
        "kn": pl.BlockSpec((tk, tn), lambda i, j, k: (k, j)),
        "nk": pl.BlockSpec((tn, tk), lambda i, j, k: (j, k)),
    }
    in_specs = [spec_of[m] for _, m in ops]
    in_specs += [pl.BlockSpec(bs, lambda i, j, k, im=im: im(i, j)) for _, bs, im in extras]
    out_specs = [pl.BlockSpec(bs, lambda i, j, k, im=im: im(i, j)) for _, bs, im in outs]
    n_ops, n_ex, n_out = len(ops), len(extras), len(outs)
    n_acc = 1 + max(g for _, _, g in pairs)
    modes = [m for _, m in ops]

    def body(*refs):
        op_refs = refs[:n_ops]
        ex_refs = refs[n_ops:n_ops + n_ex]
        out_refs = refs[n_ops + n_ex:n_ops + n_ex + n_out]
        accs = refs[n_ops + n_ex + n_out:]
        i = pl.program_id(0)
        k = pl.program_id(2)
        def dots():
            vals = [r[...].astype(BF16) for r in op_refs]
            for ia, ib, g in pairs:
                yield g, _dot(vals[ia], vals[ib], 1 if modes[ia] == "mk" else 0, 0 if modes[ib] == "kn" else 1)

        if nk == 1:
            sums = [None] * n_acc
            for g, d in dots():
                sums[g] = d if sums[g] is None else sums[g] + d
            epilogue(i, [_Held(s) for s in sums], ex_refs, out_refs)
            return

        @pl.when(k == 0)
        def _():
            for a in accs:
                a[...] = jnp.zeros_like(a)

        for g, d in dots():
            accs[g][...] += d

        @pl.when(k == nk - 1)
        def _():
            epilogue(i, accs, ex_refs, out_refs)

    op_block = {"mk": tm * tk, "km": tm * tk, "kn": tk * tn, "nk": tk * tn}
    block_bytes = sum(op_block[m] * a.dtype.itemsize for a, m in ops)
    block_bytes += sum(math.prod(bs) * jnp.dtype(a.dtype).itemsize for a, bs, _ in list(extras) + list(outs))
    res = pl.pallas_call(
        body, name=name, grid=(nm, nn, nk), in_specs=in_specs, out_specs=out_specs,
        out_shape=[o for o, _, _ in outs],
        scratch_shapes=[pltpu.VMEM((tm, tn), F32)] * (n_acc if nk > 1 else 0),
        compiler_params=_cp(3, _matmul_vmem_limit(block_bytes, n_acc * tm * tn * 4)),
    )(*[a for a, _ in ops], *[a for a, _, _ in extras])
    return res


class _Held:
    def __init__(self, value):
        self.value = value

    def __getitem__(self, idx):
        return self.value[idx]


def _mn(i, j):
    return (i, j)


def _row0(i, j):
    return (0, 0)


def _rows(name, S, ts, ins, outs, body, scratch=()):
    return pl.pallas_call(
        body, name=name, grid=(S // ts,),
        in_specs=[pl.BlockSpec(bs, im) for _, bs, im in ins],
        out_specs=[pl.BlockSpec(bs, im) for _, bs, im in outs],
        out_shape=[o for o, _, _ in outs],
        scratch_shapes=list(scratch),
        compiler_params=_cp(1),
    )(*[a for a, _, _ in ins])


def _halo_ins(arr, S, ts, width, colblk):
    per = ts // HALO
    last = S // HALO - 1
    return [
        (arr, (ts, width), lambda i: (i, colblk)),
        (arr, (HALO, width), lambda i: (jnp.maximum(i * per - 1, 0), colblk)),
        (arr, (HALO, width), lambda i: (jnp.minimum((i + 1) * per, last), colblk)),
    ]


def _ext(main_ref, prev_ref, next_ref, i, n_tiles):
    prev = jnp.where(i > 0, prev_ref[...].astype(F32), 0.0)
    nxt = jnp.where(i < n_tiles - 1, next_ref[...].astype(F32), 0.0)
    return jnp.concatenate([prev, main_ref[...].astype(F32), nxt], axis=0)


def _shift(ext, off, ts):
    n = ext.shape[0]
    if off == 0:
        return ext[HALO:HALO + ts]
    return pltpu.roll(ext, (-off) % n, 0)[HALO:HALO + ts]


def _rmsnorm_fwd(name, x, g):
    S, D = x.shape
    ts = _tile(S, 512)

    def body(x_ref, g_ref, o_ref):
        xv = x_ref[...]
        r = lax.rsqrt(jnp.mean(xv * xv, axis=-1, keepdims=True) + EPS)
        o_ref[...] = (xv * r * g_ref[...]).astype(BF16)

    return _rows(name, S, ts,
                 [(x, (ts, D), lambda i: (i, 0)), (g, (1, D), lambda i: (0, 0))],
                 [(jax.ShapeDtypeStruct((S, D), BF16), (ts, D), lambda i: (i, 0))], body)[0]


def _rmsnorm_bwd_tile(dh, x, g):
    r = lax.rsqrt(jnp.mean(x * x, axis=-1, keepdims=True) + EPS)
    xhat = x * r
    dxn = dh * g
    dx = r * (dxn - xhat * jnp.mean(dxn * xhat, axis=-1, keepdims=True))
    return dx, dh * xhat


def _ffn_fwd(tag, x, h, wg, wu, wd, extras, outs, finish):
    S = x.shape[0]
    tm = _tile(S, 1024)
    tn = 1408

    def epi_up(i, accs, ex, out):
        a = accs[0][...]
        b = accs[1][...]
        s = _sigmoid(a)
        sa = a * s
        out[0][...] = sa.astype(BF16)
        out[1][...] = (b * (s * (1.0 + a * (1.0 - s)))).astype(BF16)
        out[2][...] = (sa * b).astype(BF16)

    sds = jax.ShapeDtypeStruct((S, D_FF), BF16)
    a, b, f = _fused_mm(f"{tag}_up", S, D_FF, D_MODEL, tm, tn, D_MODEL,
                        [(h, "mk"), (wg, "nk"), (wu, "nk")], [(0, 1, 0), (0, 2, 1)], [],
                        [(sds, (tm, tn), _mn)] * 3, epi_up)

    def epi_down(i, accs, ex, out):
        finish(i, ex[0][...] + 0.5 * accs[0][...], ex[1:], out)

    if callable(wd):
        wd = wd(f)
    res = _fused_mm(f"{tag}_down", S, D_MODEL, D_FF, tm, D_MODEL, 1408,
                    [(f, "mk"), (wd, "kn")], [(0, 1, 0)], [(x, (tm, D_MODEL), _mn)] + extras(tm), outs(tm), epi_down)
    return res, a, b, f


def _rmsnorm_tile(xv, g):
    return (xv * lax.rsqrt(jnp.mean(xv * xv, axis=-1, keepdims=True) + EPS) * g).astype(BF16)


def _conv_taps(ext, w_ref, ts):
    acc = None
    for j in range(CONV_W):
        term = w_ref[j:j + 1, :] * _shift(ext, j - 2, ts)
        acc = term if acc is None else acc + term
    return acc


def _l2norm_heads(s, scale):
    outs = []
    for h in range(GDN_H):
        sh = s[:, h * GDN_DK:(h + 1) * GDN_DK]
        outs.append(sh * (lax.rsqrt(jnp.sum(sh * sh, axis=-1, keepdims=True) + EPS) * scale))
    return jnp.concatenate(outs, axis=-1)


def _conv_fwd(name, p, colblk, w, bias, mode):
    S = p.shape[0]
    ts = _tile(S, 512)
    n_tiles = S // ts
    C = w.shape[1]

    def body(main, prev, nxt, w_ref, b_ref, o_ref):
        i = pl.program_id(0)
        c = _conv_taps(_ext(main, prev, nxt, i, n_tiles), w_ref, ts)
        if mode == "bias":
            o_ref[...] = c + b_ref[...]
        else:
            s = c * _sigmoid(c)
            if mode == "q":
                s = _l2norm_heads(s, GDN_DK ** -0.5)
            elif mode == "k":
                s = _l2norm_heads(s, 1.0)
            o_ref[...] = s

    ins = _halo_ins(p, S, ts, C, colblk) + [(w, (CONV_W, C), lambda i: (0, 0)), (bias, (1, C), lambda i: (0, 0))]
    return _rows(name, S, ts, ins, [(jax.ShapeDtypeStruct((S, C), F32), (ts, C), lambda i: (i, 0))], body)[0]


def _rg_gate_terms(pre, xc, prm_ref, d):
    r = _sigmoid(pre[:, d * 1024:d * 1024 + RG_W] + prm_ref[2 * d:2 * d + 1, :])
    ig = _sigmoid(pre[:, d * 1024 + RG_W:(d + 1) * 1024] + prm_ref[2 * d + 1:2 * d + 2, :])
    sp = _softplus(-prm_ref[4 + d:5 + d, :])
    log_a = -RG_C * r * sp
    a = jnp.exp(log_a)
    t = jnp.tanh(log_a)
    sq = jnp.sqrt(-2.0 * t / (1.0 - t))
    return r, ig, sp, a, sq


def _rg_gates_fwd(xc, bd, prm):
    S = xc.shape[0]
    tm = _tile(S, 256)

    def epi(i, accs, ex, out):
        pre = accs[0][...]
        xv = ex[0][...]
        for d in range(2):
            r, ig, sp, a, sq = _rg_gate_terms(pre, xv, ex[1], d)
            out[2 * d][...] = a
            out[2 * d + 1][...] = sq * ig * xv

    sds = jax.ShapeDtypeStruct((S, RG_W), F32)
    blk = (tm, RG_W)
    im = lambda i, j: (i, 0)
    return _fused_mm("rg_gates_fwd", S, 4 * RG_W, RG_W, tm, 4 * RG_W, RG_W,
                     [(xc, "mk"), (bd, "kn")], [(0, 1, 0)],
                     [(xc, blk, im), (prm, (8, RG_W), _row0)], [(sds, blk, im)] * 4, epi)


SUBLANES = 8


def _scan_rows(a, b, reverse):
    rows = lax.broadcasted_iota(jnp.int32, a.shape, 0)
    s = 1
    while s < SUBLANES:
        shift = SUBLANES - s if reverse else s
        a_sh = pltpu.roll(a, shift, 0)
        b_sh = pltpu.roll(b, shift, 0)
        valid = (rows < SUBLANES - s) if reverse else (rows >= s)
        b = jnp.where(valid, a * b_sh + b, b)
        a = jnp.where(valid, a * a_sh, a)
        s *= 2
    return a, b


def _rg_scan(name, a_f, b_f, a_b, b_b):
    S, C = a_f.shape
    ts = _tile(S, 512)
    n_tiles = S // ts

    def body(af, bf, ab, bb, hf, hb, carry):
        @pl.when(pl.program_id(0) == 0)
        def _():
            carry[...] = jnp.zeros_like(carry)

        n_sub = ts // SUBLANES

        def step(j, c):
            cf, cb = c
            r0 = pl.multiple_of(j * SUBLANES, SUBLANES)
            cum_a, h0 = _scan_rows(af[pl.ds(r0, SUBLANES), :], bf[pl.ds(r0, SUBLANES), :], False)
            h = h0 + cum_a * cf
            hf[pl.ds(r0, SUBLANES), :] = h
            cf = h[SUBLANES - 1:SUBLANES, :]
            r1 = pl.multiple_of((n_sub - 1 - j) * SUBLANES, SUBLANES)
            cum_a, h0 = _scan_rows(ab[pl.ds(r1, SUBLANES), :], bb[pl.ds(r1, SUBLANES), :], True)
            h = h0 + cum_a * cb
            hb[pl.ds(r1, SUBLANES), :] = h
            cb = h[0:1, :]
            return cf, cb

        cf, cb = lax.fori_loop(0, n_sub, step, (carry[0:1, :], carry[1:2, :]), unroll=4)
        carry[0:1, :] = cf
        carry[1:2, :] = cb

    fw = lambda i: (i, 0)
    bw = lambda i: (n_tiles - 1 - i, 0)
    sds = jax.ShapeDtypeStruct((S, C), F32)
    return _rows(name, S, ts,
                 [(a_f, (ts, C), fw), (b_f, (ts, C), fw), (a_b, (ts, C), bw), (b_b, (ts, C), bw)],
                 [(sds, (ts, C), fw), (sds, (ts, C), bw)], body, scratch=[pltpu.VMEM((8, C), F32)])


def _tri_masks():
    ri = lax.broadcasted_iota(jnp.int32, (CHUNK, CHUNK), 0)
    ci = lax.broadcasted_iota(jnp.int32, (CHUNK, CHUNK), 1)
    return ri, ci


def _gdn_prep_fwd(p, prm):
    S = p.shape[0]
    ts = _tile(S, 512)

    def body(p_ref, prm_ref, o_ref):
        raw = p_ref[...].astype(F32)
        lane = lax.broadcasted_iota(jnp.int32, (1, 128), 1)
        g = -jnp.exp(prm_ref[0:1, :]) * _softplus(raw + prm_ref[1:2, :])
        g = jnp.where((lane >= 8) & (lane < 16), g, 0.0)
        beta = _sigmoid(raw)
        ri, ci = _tri_masks()
        lower = (ri >= ci).astype(F32)
        upper = (ri <= ci).astype(F32)
        for c in range(ts // CHUNK):
            rows = slice(c * CHUNK, (c + 1) * CHUNK)
            gch = g[rows]
            gc = jnp.where(lane < 12, _dot(lower, gch, 1, 0, HI), _dot(upper, gch, 1, 0, HI))
            o_ref[rows, :] = jnp.where(lane < 8, beta[rows], gc)

    return _rows("gdn_prep_fwd", S, ts,
                 [(p, (ts, 128), lambda i: (i, COL_BA // 128)), (prm, (8, 128), lambda i: (0, 0))],
                 [(jax.ShapeDtypeStruct((S, 128), F32), (ts, 128), lambda i: (i, 0))], body)[0]


def _bdot(a, b, ca, cb):
    return _dot(a.astype(BF16), b.astype(BF16), ca, cb)


GDN_W = GDN_H * GDN_DK
GDN_TS = 256
LOCAL_CHUNKS = 2

def _gdn_decay(bg_ref, gcr_ref, c, rows, r0, col, rev, ri, ci):
    beta = bg_ref[rows, col:col + 1]
    gc = bg_ref[rows, 8 + col:9 + col]
    last = 0 if rev else CHUNK - 1
    gl = bg_ref[pl.ds(r0 + last, 1), 8 + col:9 + col]
    out = dict(beta=beta, gc=gc, gl=gl, eg=jnp.exp(gc), egl=jnp.exp(gl - gc), cd=jnp.exp(gl))
    if gcr_ref is not None:
        incl = (ri <= ci) if rev else (ri >= ci)
        out["strict"] = (ri < ci) if rev else (ri > ci)
        out["dm"] = jnp.where(incl, jnp.exp(jnp.where(incl, gc - gcr_ref[c, col:col + 1, :], 0.0)), 0.0)
    return out


def _dir_tile(d, n_tiles, flip):
    if (d == 1) != flip:
        return lambda i: n_tiles - 1 - i
    return lambda i: i


def _gdn_local_fwd(q, k, v, bg, gcr):
    S = q.shape[0]
    ts = _tile(S, GDN_TS)
    ncb = ts // CHUNK
    nch = S // CHUNK

    def body(q_ref, k_ref, v_ref, bg_ref, gcr_ref, *out_refs):
        ri, ci = _tri_masks()
        eye = (ri == ci).astype(F32)
        outs = (out_refs[0:6], out_refs[6:12])
        cd_ref = out_refs[12]

        def chunk(cc, carry):
            chains = []
            for c in (LOCAL_CHUNKS * cc + j for j in range(LOCAL_CHUNKS)):
                r0 = pl.multiple_of(c * CHUNK, CHUNK)
                rows = pl.ds(r0, CHUNK)
                for h in range(GDN_H):
                    cols = slice(h * GDN_DK, (h + 1) * GDN_DK)
                    qh, kh, vh = q_ref[rows, cols], k_ref[rows, cols], v_ref[rows, cols]
                    both = _bdot(jnp.concatenate([qh, kh], axis=0), kh, 1, 1)
                    for d in range(2):
                        chains.append(dict(c=c, r0=r0, rows=rows, h=h, d=d, cols=cols, qh=qh, kh=kh, vh=vh,
                                           qk=both[0:CHUNK], kk=both[CHUNK:2 * CHUNK]))
            for ch in chains:
                m = _gdn_decay(bg_ref, gcr_ref, ch["c"], ch["rows"], ch["r0"], ch["d"] * GDN_H + ch["h"], ch["d"] == 1,
                               ri, ci)
                ch["m"] = m
                ch["x"] = -jnp.where(m["strict"], m["beta"] * ch["kk"] * m["dm"], 0.0)
                ch["t"] = eye + ch["x"]
            for ch in chains:
                ch["pw"] = _bdot(ch["x"], ch["x"], 1, 0)
            for level in range(1, 6):
                last_level = level == 5
                for ch in chains:
                    rhs = ch["t"] if last_level else jnp.concatenate([ch["t"], ch["pw"]], axis=1)
                    ch["prod"] = _bdot(ch["pw"], rhs, 1, 0)
                for ch in chains:
                    ch["t"] = ch["t"] + ch["prod"][:, 0:CHUNK]
                    if not last_level:
                        ch["pw"] = ch["prod"][:, CHUNK:2 * CHUNK]
            for ch in chains:
                m = ch["m"]
                rhs = jnp.concatenate([ch["vh"] * m["beta"], ch["kh"] * (m["beta"] * m["eg"])], axis=1)
                ch["uw"] = _bdot(ch["t"], rhs, 1, 0)
            for ch in chains:
                u_ref, w_ref, a_ref, t_ref, qd_ref, kd_ref = outs[ch["d"]]
                m = ch["m"]
                c, rows = ch["c"], ch["rows"]
                col = ch["d"] * GDN_H + ch["h"]
                u_ref[rows, ch["cols"]] = ch["uw"][:, 0:GDN_DK]
                w_ref[rows, ch["cols"]] = ch["uw"][:, GDN_DK:2 * GDN_DK].astype(BF16)
                a_ref[c, ch["h"]] = (ch["qk"] * m["dm"]).astype(BF16)
                t_ref[c, ch["h"]] = _bdot(ch["t"], eye, 0, 0).astype(BF16)
                qd_ref[rows, ch["cols"]] = (ch["qh"] * m["eg"]).astype(BF16)
                kd_ref[rows, ch["cols"]] = (ch["kh"] * m["egl"]).astype(BF16)
                cd_ref[c, col:col + 1, :] = jnp.broadcast_to(m["cd"], (1, 128))
            return carry

        lax.fori_loop(0, ncb // LOCAL_CHUNKS, chunk, 0)

    im = lambda i: (i, 0)
    im4 = lambda i: (i, 0, 0, 0)
    ins = [(q, (ts, GDN_W), im), (k, (ts, GDN_W), im), (v, (ts, GDN_W), im), (bg, (ts, 128), im),
           (gcr, (ncb, 8, CHUNK), lambda i: (i, 0, 0))]
    per_dir = [(jax.ShapeDtypeStruct((S, GDN_W), F32), (ts, GDN_W), im),
               (jax.ShapeDtypeStruct((S, GDN_W), BF16), (ts, GDN_W), im),
               (jax.ShapeDtypeStruct((nch, GDN_H, CHUNK, CHUNK), BF16), (ncb, GDN_H, CHUNK, CHUNK), im4),
               (jax.ShapeDtypeStruct((nch, GDN_H, CHUNK, CHUNK), BF16), (ncb, GDN_H, CHUNK, CHUNK), im4),
               (jax.ShapeDtypeStruct((S, GDN_W), BF16), (ts, GDN_W), im),
               (jax.ShapeDtypeStruct((S, GDN_W), BF16), (ts, GDN_W), im)]
    cd_out = (jax.ShapeDtypeStruct((nch, 8, 128), F32), (ncb, 8, 128), lambda i: (i, 0, 0))
    res = _rows("gdn_local_fwd", S, ts, ins, per_dir * 2 + [cd_out], body)
    return res[0:6], res[6:12], res[12]


def _gdn_scan_fwd(loc):
    S = loc[0][0].shape[0]
    ts = _tile(S, GDN_TS)
    n_tiles = S // ts
    ncb = ts // CHUNK
    nch = S // CHUNK

    def body(*refs):
        ins = (refs[0:6], refs[6:12])
        outs = (refs[12:15], refs[15:18])
        state = refs[18]

        @pl.when(pl.program_id(0) == 0)
        def _():
            state[...] = jnp.zeros_like(state)

        def chunk(cc, carry):
            chains = []
            for d in range(2):
                c = cc if d == 0 else ncb - 1 - cc
                rows = pl.ds(pl.multiple_of(c * CHUNK, CHUNK), CHUNK)
                for h in range(GDN_H):
                    cols = slice(h * GDN_DK, (h + 1) * GDN_DK)
                    chains.append(dict(d=d, h=h, c=c, rows=rows, cols=cols, st=state[d * GDN_H + h]))
            for ch in chains:
                qd_ref, kd_ref, u_ref, w_ref, a_ref, cd_ref = ins[ch["d"]]
                rows, cols = ch["rows"], ch["cols"]
                lhs = jnp.concatenate([w_ref[rows, cols], qd_ref[rows, cols]], axis=0)
                ch["ws_qs"] = _dot(lhs, ch["st"].astype(BF16), 1, 0)
            for ch in chains:
                qd_ref, kd_ref, u_ref, w_ref, a_ref, cd_ref = ins[ch["d"]]
                rows, cols = ch["rows"], ch["cols"]
                vn = u_ref[rows, cols] - ch["ws_qs"][0:CHUNK]
                vnb = vn.astype(BF16)
                ch["vn"] = vn
                ch["avn"] = _dot(a_ref[ch["c"], ch["h"]], vnb, 1, 0)
                ch["kvn"] = _dot(kd_ref[rows, cols], vnb, 0, 0)
            for ch in chains:
                o_ref, vn_ref, s_ref = outs[ch["d"]]
                cd_ref = ins[ch["d"]][5]
                rows, cols = ch["rows"], ch["cols"]
                col = ch["d"] * GDN_H + ch["h"]
                o_ref[rows, cols] = ch["ws_qs"][CHUNK:2 * CHUNK] + ch["avn"]
                vn_ref[rows, cols] = ch["vn"].astype(BF16)
                s_ref[ch["c"], ch["h"]] = ch["st"].astype(BF16)
                state[ch["d"] * GDN_H + ch["h"]] = ch["st"] * cd_ref[ch["c"], col:col + 1, :] + ch["kvn"]
            return carry

        lax.fori_loop(0, ncb, chunk, 0)

    ins, outs = [], []
    for d in range(2):
        tix = _dir_tile(d, n_tiles, False)
        im = lambda i, tix=tix: (tix(i), 0)
        im4 = lambda i, tix=tix: (tix(i), 0, 0, 0)
        u, w, a, _, qd, kd = loc[d]
        ins += [(qd, (ts, GDN_W), im), (kd, (ts, GDN_W), im), (u, (ts, GDN_W), im), (w, (ts, GDN_W), im),
                (a, (ncb, GDN_H, CHUNK, CHUNK), im4), (loc[2], (ncb, 8, 128), lambda i, tix=tix: (tix(i), 0, 0))]
        outs += [(jax.ShapeDtypeStruct((S, GDN_W), F32), (ts, GDN_W), im),
                 (jax.ShapeDtypeStruct((S, GDN_W), BF16), (ts, GDN_W), im),
                 (jax.ShapeDtypeStruct((nch, GDN_H, GDN_DK, GDN_DK), BF16), (ncb, GDN_H, GDN_DK, GDN_DK), im4)]
    res = _rows("gdn_scan_fwd", S, ts, ins, outs, body, scratch=[pltpu.VMEM((2 * GDN_H, GDN_DK, GDN_DK), F32)])
    return res[0:3], res[3:6]


def _gelu(x):
    c = math.sqrt(2.0 / math.pi)
    t = jnp.tanh(c * (x + 0.044715 * x * x * x))
    return 0.5 * x * (1.0 + t), t


def _mix_out_fwd(h_f, h_b, o_f, o_b, p, gn):
    S = h_f.shape[0]
    ts = _tile(S, 512)

    def body(hf, hb, of, ob, gate, z, gn_ref, y_ref):
        ge, _ = _gelu(gate[...].astype(F32))
        y_ref[:, 0:RG_W] = ((hf[...] + hb[...]) * ge).astype(BF16)
        o = of[...] + ob[...]
        zv = z[...].astype(F32)
        sz = zv * _sigmoid(zv)
        for h in range(GDN_H):
            cols = slice(h * GDN_DK, (h + 1) * GDN_DK)
            oh = o[:, cols]
            n = oh * lax.rsqrt(jnp.mean(oh * oh, axis=-1, keepdims=True) + EPS) * gn_ref[...]
            y_ref[:, RG_W + h * GDN_DK:RG_W + (h + 1) * GDN_DK] = (n * sz[:, cols]).astype(BF16)

    blk = (ts, RG_W)
    im = lambda i: (i, 0)
    ins = [(h_f, blk, im), (h_b, blk, im), (o_f, blk, im), (o_b, blk, im),
           (p, blk, lambda i: (i, 1)), (p, blk, lambda i: (i, 5)), (gn, (1, GDN_DK), lambda i: (0, 0))]
    return _rows("mix_out_fwd", S, ts, ins,
                 [(jax.ShapeDtypeStruct((S, D_MODEL), BF16), (ts, D_MODEL), im)], body)[0]


def _block_diag(w):
    n = w.shape[0]
    return jnp.einsum("nij,nm->nimj", w, jnp.eye(n, dtype=w.dtype)).reshape(n * w.shape[1], n * w.shape[2])


def _rg_bd(a_w, x_w):
    return jnp.concatenate([_block_diag(a_w[0]), _block_diag(x_w[0]), _block_diag(a_w[1]), _block_diag(x_w[1])],
                           axis=1).astype(BF16)


def _rg_prm(ba, bx, lam):
    return jnp.concatenate([ba[0:1], bx[0:1], ba[1:2], bx[1:2], lam, jnp.zeros((2, RG_W), F32)], axis=0)


def _gdn_prm(a_log, dt_bias):
    rows = jnp.zeros((8, 128), F32)
    rows = rows.at[0, 8:16].set(a_log.reshape(-1))
    return rows.at[1, 8:16].set(dt_bias.reshape(-1))


def _gc_rows(bg):
    S = bg.shape[0]
    return bg[:, 8:16].reshape(S // CHUNK, CHUNK, 8).transpose(0, 2, 1)


def _layer_fwd(x0, target, W, more=None):
    S = x0.shape[0]
    R = {}
    R["h1"] = _rmsnorm_fwd("rms1", x0, W["ffn1_norm"])
    if more is not None:
        W = {**W, **more("ffn1_up", R["h1"])}
    late_wd = {}

    def ffn1_wd(after):
        late_wd.update(more("ffn1_down", after))
        return late_wd["ffn1_w_down"]

    sd_x = jax.ShapeDtypeStruct((S, D_MODEL), F32)
    sd_h = jax.ShapeDtypeStruct((S, D_MODEL), BF16)

    def norm_after(gain):
        extras = lambda t: [(gain, (1, D_MODEL), _row0)]
        outs = lambda t: [(sd_x, (t, D_MODEL), _mn), (sd_h, (t, D_MODEL), _mn)]

        def finish(i, xo, ex, out):
            out[0][...] = xo
            out[1][...] = _rmsnorm_tile(xo, ex[0][...])

        return extras, outs, finish

    (R["x1"], R["h2"]), R["a1"], R["b1"], R["f1"] = _ffn_fwd(
        "ffn1", x0, R["h1"], W["ffn1_w_gate"], W["ffn1_w_up"], ffn1_wd if more is not None else W["ffn1_w_down"],
        *norm_after(W["mix_norm"]))
    if more is not None:
        W = {**W, **late_wd, **more("mixer", R["x1"])}
    tm = _tile(S, 512)
    tmp = _tile(S, 1024)
    R["p"] = _fused_mm("in_proj", S, D_IN_PAD, D_MODEL, tmp, 640, D_MODEL, [(R["h2"], "mk"), (W["w_in"], "nk")],
                       [(0, 1, 0)], [], [(jax.ShapeDtypeStruct((S, D_IN_PAD), BF16), (tmp, 640), _mn)],
                       lambda i, accs, ex, out: out[0].__setitem__(Ellipsis, accs[0][...].astype(BF16)))[0]
    p = R["p"]
    R["xc"] = _conv_fwd("rg_conv_fwd", p, 0, W["rg_conv_w"], W["rg_conv_b"], "bias")
    R["bd"] = _rg_bd(W["rg_gate_a_w"], W["rg_gate_x_w"])
    R["rg_prm"] = _rg_prm(W["rg_gate_a_b"], W["rg_gate_x_b"], W["rg_lambda"])
    a_f, b_f, a_b, b_b = _rg_gates_fwd(R["xc"], R["bd"], R["rg_prm"])
    R["a_f"], R["a_b"] = a_f, a_b
    R["h_f"], R["h_b"] = _rg_scan("rg_scan_fwd", a_f, b_f, a_b, b_b)
    zero_b = jnp.zeros((1, RG_W), F32)
    cw = W["gdn_conv_w"]
    R["q"] = _conv_fwd("gdn_conv_q", p, 2, cw[:, 0:512], zero_b, "q")
    R["k"] = _conv_fwd("gdn_conv_k", p, 3, cw[:, 512:1024], zero_b, "k")
    R["v"] = _conv_fwd("gdn_conv_v", p, 4, cw[:, 1024:1536], zero_b, "v")
    R["gdn_prm"] = _gdn_prm(W["gdn_a_log"], W["gdn_dt_bias"])
    R["bg"] = _gdn_prep_fwd(p, R["gdn_prm"])
    R["gcr"] = _gc_rows(R["bg"])
    R["gdn_loc"] = _gdn_local_fwd(R["q"], R["k"], R["v"], R["bg"], R["gcr"])
    R["gdn_fwd"] = _gdn_scan_fwd(R["gdn_loc"])
    R["o_f"], R["o_b"] = R["gdn_fwd"][0][0], R["gdn_fwd"][1][0]
    R["y"] = _mix_out_fwd(R["h_f"], R["h_b"], R["o_f"], R["o_b"], p, W["gdn_norm"])
    def epi_out(i, accs, ex, out):
        x2 = ex[0][...] + accs[0][...]
        out[0][...] = x2
        out[1][...] = _rmsnorm_tile(x2, ex[1][...])

    R["x2"], R["h3"] = _fused_mm("out_proj", S, D_MODEL, D_MODEL, tm, D_MODEL, D_MODEL,
                                 [(R["y"], "mk"), (W["w_out"], "kn")], [(0, 1, 0)],
                                 [(R["x1"], (tm, D_MODEL), _mn), (W["ffn2_norm"], (1, D_MODEL), _row0)],
                                 [(sd_x, (tm, D_MODEL), _mn), (sd_h, (tm, D_MODEL), _mn)], epi_out)
    if more is not None:
        W = {**W, **more("ffn2", R["x2"])}

    def loss_finish(i, xo, ex, out):
        gv = ex[1][...]
        r = lax.rsqrt(jnp.mean(xo * xo, axis=-1, keepdims=True) + EPS)
        err = xo * r * gv - ex[0][...]
        dx, dgt = _rmsnorm_bwd_tile(err * (1.0 / D_MODEL), xo, gv)
        out[0][...] = dx
        _colsum_into(out[1], i, jnp.zeros((8, 128), F32) + jnp.sum(err * err) * (0.5 / D_MODEL))
        _colsum_into(out[2], i, jnp.sum(dgt, axis=0, keepdims=True))

    (R["dx3"], R["loss"], R["d_final_norm"]), R["a2"], R["b2"], R["f2"] = _ffn_fwd(
        "ffn2", R["x2"], R["h3"], W["ffn2_w_gate"], W["ffn2_w_up"], W["ffn2_w_down"],
        lambda t: [(target, (t, D_MODEL), _mn), (W["final_norm"], (1, D_MODEL), _row0)],
        lambda t: [(sd_x, (t, D_MODEL), _mn), (jax.ShapeDtypeStruct((8, 128), F32), (8, 128), _row0),
                   (jax.ShapeDtypeStruct((1, D_MODEL), F32), (1, D_MODEL), _row0)],
        loss_finish)
    R["W"] = W
    return R


def _colsum_into(ref, i, val):
    @pl.when(i == 0)
    def _():
        ref[...] = val

    @pl.when(i > 0)
    def _():
        ref[...] += val


def _ffn_bwd(tag, dout, x, g, h, a, b, f, wg, wu, wd, emit):
    S = x.shape[0]
    tm = _tile(S, 512)
    tk_s = _tile(S, 1024)
    dwd = _fused_mm(f"{tag}_dw_down", D_FF, D_MODEL, S, 1408, D_MODEL, tk_s, [(f, "km"), (dout, "kn")], [(0, 1, 0)], [],
                    [(jax.ShapeDtypeStruct((D_FF, D_MODEL), BF16), (1408, D_MODEL), _mn)],
                    lambda i, accs, ex, out: out[0].__setitem__(Ellipsis, (0.5 * accs[0][...]).astype(BF16)))[0]
    emit(down=dwd)

    def epi_act(i, accs, ex, out):
        df = 0.5 * accs[0][...]
        out[0][...] = (df * ex[1][...].astype(F32)).astype(BF16)
        out[1][...] = (df * ex[0][...].astype(F32)).astype(BF16)

    sds = jax.ShapeDtypeStruct((S, D_FF), BF16)
    da, db = _fused_mm(f"{tag}_dact", S, D_FF, D_MODEL, tm, 1408, D_MODEL, [(dout, "mk"), (wd, "nk")], [(0, 1, 0)],
                       [(a, (tm, 1408), _mn), (b, (tm, 1408), _mn)], [(sds, (tm, 1408), _mn)] * 2, epi_act)

    def epi_w2(i, accs, ex, out):
        out[0][...] = accs[0][...].astype(BF16)
        out[1][...] = accs[1][...].astype(BF16)

    sdw = jax.ShapeDtypeStruct((D_MODEL, D_FF), BF16)
    dwg, dwu = _fused_mm(f"{tag}_dw_up", D_MODEL, D_FF, S, D_MODEL, 1408, tk_s,
                         [(h, "km"), (da, "kn"), (db, "kn")], [(0, 1, 0), (0, 2, 1)], [],
                         [(sdw, (D_MODEL, 1408), _mn)] * 2, epi_w2)
    tok = emit(gate=dwg, up=dwu)
    if tok is not None:
        g = g + tok

    def epi_dx(i, accs, ex, out):
        dx, dgt = _rmsnorm_bwd_tile(accs[0][...], ex[0][...], ex[1][...])
        out[0][...] = ex[2][...] + dx
        _colsum_into(out[1], i, jnp.sum(dgt, axis=0, keepdims=True))

    tmx = _tile(S, 1024)
    dx, dg = _fused_mm(f"{tag}_dx", S, D_MODEL, D_FF, tmx, D_MODEL, 1408,
                       [(da, "mk"), (wg, "kn"), (db, "mk"), (wu, "kn")], [(0, 1, 0), (2, 3, 0)],
                       [(x, (tmx, D_MODEL), _mn), (g, (1, D_MODEL), _row0), (dout, (tmx, D_MODEL), _mn)],
                       [(jax.ShapeDtypeStruct((S, D_MODEL), F32), (tmx, D_MODEL), _mn),
                        (jax.ShapeDtypeStruct((1, D_MODEL), F32), (1, D_MODEL), _row0)], epi_dx)
    return dx, dg


def _mix_out_bwd(dx2, w_out, h_f, h_b, o_f, o_b, p, gn):
    S = dx2.shape[0]
    ts = _tile(S, 512)
    c0 = math.sqrt(2.0 / math.pi)

    def epi(i, accs, ex, out):
        hf, hb, of, ob, gate, z, gn_ref = ex
        dhr_ref, dgate_ref, do_ref, dz_ref, dgn_ref = out
        dy_ref = accs[0]
        gv = gate[...].astype(F32)
        ge, t = _gelu(gv)
        dy_rg = dy_ref[:, 0:RG_W]
        dhr_ref[...] = dy_rg * ge
        dgelu = 0.5 * (1.0 + t) + 0.5 * gv * (1.0 - t * t) * c0 * (1.0 + 3.0 * 0.044715 * gv * gv)
        dgate_ref[...] = (dy_rg * (hf[...] + hb[...]) * dgelu).astype(BF16)
        o = of[...] + ob[...]
        zv = z[...].astype(F32)
        sig = _sigmoid(zv)
        gnv = gn_ref[...]
        dgn = jnp.zeros((1, GDN_DK), F32)
        for h in range(GDN_H):
            cols = slice(h * GDN_DK, (h + 1) * GDN_DK)
            oh = o[:, cols]
            r = lax.rsqrt(jnp.mean(oh * oh, axis=-1, keepdims=True) + EPS)
            ohat = oh * r
            dyh = dy_ref[:, RG_W + h * GDN_DK:RG_W + (h + 1) * GDN_DK]
            zh = zv[:, cols]
            sh = sig[:, cols]
            dn = dyh * zh * sh
            dz_ref[:, cols] = (dyh * ohat * gnv * (sh * (1.0 + zh * (1.0 - sh)))).astype(BF16)
            dxn = dn * gnv
            do_ref[:, cols] = r * (dxn - ohat * jnp.mean(dxn * ohat, axis=-1, keepdims=True))
            dgn = dgn + jnp.sum(dn * ohat, axis=0, keepdims=True)
        _colsum_into(dgn_ref, i, dgn)

    blk = (ts, RG_W)
    im = lambda i, j: (i, 0)
    extras = [(h_f, blk, im), (h_b, blk, im), (o_f, blk, im), (o_b, blk, im),
              (p, blk, lambda i, j: (i, 1)), (p, blk, lambda i, j: (i, 5)), (gn, (1, GDN_DK), _row0)]
    outs = [(jax.ShapeDtypeStruct((S, RG_W), F32), blk, im), (jax.ShapeDtypeStruct((S, RG_W), BF16), blk, im),
            (jax.ShapeDtypeStruct((S, RG_W), F32), blk, im), (jax.ShapeDtypeStruct((S, RG_W), BF16), blk, im),
            (jax.ShapeDtypeStruct((1, GDN_DK), F32), (1, GDN_DK), _row0)]
    return _fused_mm("mix_out_bwd", S, D_MODEL, D_MODEL, ts, D_MODEL, D_MODEL, [(dx2, "mk"), (w_out, "nk")], [(0, 1, 0)],
                     extras, outs, epi)


def _rg_scan_adj(name, a_up, b_up, a_dn, b_dn):
    S, C = a_up.shape
    ts = _tile(S, 512)
    n_tiles = S // ts

    def body(au, bu, ad, bd, mu_ref, lam_ref, carry):
        @pl.when(pl.program_id(0) == 0)
        def _():
            carry[...] = jnp.zeros_like(carry)

        n_sub = ts // SUBLANES
        rows = lax.broadcasted_iota(jnp.int32, (SUBLANES, C), 0)

        def half(a_ref, b_ref, out_ref, r0, c_in, reverse):
            a = a_ref[pl.ds(r0, SUBLANES), :]
            b = b_ref[pl.ds(r0, SUBLANES), :]
            cum_a, c0 = _scan_rows(a, a * b, reverse)
            c = c0 + cum_a * c_in
            edge = 0 if not reverse else SUBLANES - 1
            c_prev = jnp.where(rows == edge, c_in, pltpu.roll(c, SUBLANES - 1 if reverse else 1, 0))
            out_ref[pl.ds(r0, SUBLANES), :] = b + c_prev
            return c[0:1, :] if reverse else c[SUBLANES - 1:SUBLANES, :]

        def step(j, c):
            cu, cd = c
            cu = half(au, bu, mu_ref, pl.multiple_of(j * SUBLANES, SUBLANES), cu, False)
            cd = half(ad, bd, lam_ref, pl.multiple_of((n_sub - 1 - j) * SUBLANES, SUBLANES), cd, True)
            return cu, cd

        cu, cd = lax.fori_loop(0, n_sub, step, (carry[0:1, :], carry[1:2, :]), unroll=4)
        carry[0:1, :] = cu
        carry[1:2, :] = cd

    fw = lambda i: (i, 0)
    bw = lambda i: (n_tiles - 1 - i, 0)
    sds = jax.ShapeDtypeStruct((S, C), F32)
    return _rows(name, S, ts,
                 [(a_up, (ts, C), fw), (b_up, (ts, C), fw), (a_dn, (ts, C), bw), (b_dn, (ts, C), bw)],
                 [(sds, (ts, C), fw), (sds, (ts, C), bw)], body, scratch=[pltpu.VMEM((8, C), F32)])


def _halo_ex(arr, S, tm, width):
    per = tm // HALO
    last = S // HALO - 1
    return [
        (arr, (tm, width), lambda i, j: (i, 0)),
        (arr, (HALO, width), lambda i, j: (jnp.maximum(i * per - 1, 0), 0)),
        (arr, (HALO, width), lambda i, j: (jnp.minimum((i + 1) * per, last), 0)),
    ]


def _rg_gates_bwd(xc, bd, prm, lam_f, lam_b, h_f, h_b):
    S = xc.shape[0]
    tm = _tile(S, 256)
    n_tiles = S // tm

    def epi(i, accs, ex, out):
        pre = accs[0][...]
        xv = ex[0][...]
        prm_ref = ex[1]
        lams = (ex[2][...], ex[3][...])
        hprev = (_shift(_ext(ex[4], ex[5], ex[6], i, n_tiles), -1, tm),
                 _shift(_ext(ex[7], ex[8], ex[9], i, n_tiles), 1, tm))
        dxc = jnp.zeros_like(xv)
        rows = []
        dlam_rows = []
        for d in range(2):
            r, ig, sp, a, sq = _rg_gate_terms(pre, xv, prm_ref, d)
            lam = lams[d]
            da = lam * hprev[d]
            di = lam * sq * xv
            dxc = dxc + lam * sq * ig
            dsq = lam * ig * xv
            dlog_a = da * a - dsq * (a * a) / sq
            dpre_r = dlog_a * (-RG_C * sp) * r * (1.0 - r)
            dpre_i = di * ig * (1.0 - ig)
            out[0][:, d * 1024:d * 1024 + RG_W] = dpre_r.astype(BF16)
            out[0][:, d * 1024 + RG_W:(d + 1) * 1024] = dpre_i.astype(BF16)
            rows += [jnp.sum(dpre_r, axis=0, keepdims=True), jnp.sum(dpre_i, axis=0, keepdims=True)]
            dsp = jnp.sum(dlog_a * (-RG_C * r), axis=0, keepdims=True)
            dlam_rows.append(-dsp * _sigmoid(-prm_ref[4 + d:5 + d, :]))
        out[1][...] = dxc
        zero = jnp.zeros((2, RG_W), F32)
        _colsum_into(out[2], i, jnp.concatenate(rows + dlam_rows + [zero], axis=0))

    blk = (tm, RG_W)
    im = lambda i, j: (i, 0)
    extras = ([(xc, blk, im), (prm, (8, RG_W), _row0), (lam_f, blk, im), (lam_b, blk, im)]
              + _halo_ex(h_f, S, tm, RG_W) + _halo_ex(h_b, S, tm, RG_W))
    outs = [(jax.ShapeDtypeStruct((S, 4 * RG_W), BF16), (tm, 4 * RG_W), im),
            (jax.ShapeDtypeStruct((S, RG_W), F32), blk, im),
            (jax.ShapeDtypeStruct((8, RG_W), F32), (8, RG_W), _row0)]
    return _fused_mm("rg_gates_bwd", S, 4 * RG_W, RG_W, tm, 4 * RG_W, RG_W, [(xc, "mk"), (bd, "kn")], [(0, 1, 0)],
                     extras, outs, epi)


def _roll_rows(ext, off):
    if off == 0:
        return ext
    return pltpu.roll(ext, (-off) % ext.shape[0], 0)


def _conv_bwd(name, p, colblk, w, grads, mode):
    S = p.shape[0]
    ts = _tile(S, 512)
    n_tiles = S // ts
    C = w.shape[1]
    ng = len(grads)

    def body(*refs):
        p_refs = refs[0:3]
        g_refs = refs[3:3 + 3 * ng]
        w_ref = refs[3 + 3 * ng]
        dx_ref, dw_ref, db_ref = refs[4 + 3 * ng:]
        i = pl.program_id(0)
        ext_p = _ext(*p_refs, i, n_tiles)
        dn = _ext(*g_refs[0:3], i, n_tiles)
        for gi in range(1, ng):
            dn = dn + _ext(*g_refs[3 * gi:3 * gi + 3], i, n_tiles)
        if mode == "bias":
            dc = dn
        else:
            c = None
            for j in range(CONV_W):
                term = w_ref[j:j + 1, :] * _roll_rows(ext_p, j - 2)
                c = term if c is None else c + term
            sig = _sigmoid(c)
            s = c * sig
            if mode in ("q", "k"):
                scale = GDN_DK ** -0.5 if mode == "q" else 1.0
                parts = []
                for h in range(GDN_H):
                    cols = slice(h * GDN_DK, (h + 1) * GDN_DK)
                    sh = s[:, cols]
                    dnh = dn[:, cols]
                    rinv = lax.rsqrt(jnp.sum(sh * sh, axis=-1, keepdims=True) + EPS)
                    parts.append(scale * rinv * (dnh - sh * (rinv * rinv) * jnp.sum(dnh * sh, axis=-1, keepdims=True)))
                ds = jnp.concatenate(parts, axis=-1)
            else:
                ds = dn
            dc = ds * (sig * (1.0 + c * (1.0 - sig)))
        dx = None
        for j in range(CONV_W):
            term = w_ref[j:j + 1, :] * _shift(dc, 2 - j, ts)
            dx = term if dx is None else dx + term
        dx_ref[...] = dx.astype(BF16)
        dc_main = dc[HALO:HALO + ts]
        dw = jnp.concatenate([jnp.sum(dc_main * _shift(ext_p, j - 2, ts), axis=0, keepdims=True)
                              for j in range(CONV_W)], axis=0)
        _colsum_into(dw_ref, i, dw)
        _colsum_into(db_ref, i, jnp.sum(dc_main, axis=0, keepdims=True))

    ins = _halo_ins(p, S, ts, C, colblk)
    for garr in grads:
        ins += _halo_ins(garr, S, ts, C, 0)
    ins += [(w, (CONV_W, C), lambda i: (0, 0))]
    z0 = lambda i: (0, 0)
    outs = [(jax.ShapeDtypeStruct((S, C), BF16), (ts, C), lambda i: (i, 0)),
            (jax.ShapeDtypeStruct((CONV_W, C), F32), (CONV_W, C), z0),
            (jax.ShapeDtypeStruct((1, C), F32), (1, C), z0)]
    return _rows(name, S, ts, ins, outs, body)


def _gdn_scan_bwd(loc, do):
    S = do.shape[0]
    ts = _tile(S, GDN_TS)
    n_tiles = S // ts
    ncb = ts // CHUNK
    nch = S // CHUNK

    def body(*refs):
        ins = (refs[0:6], refs[6:12])
        outs = (refs[12:14], refs[14:16])
        dstate = refs[16]

        @pl.when(pl.program_id(0) == 0)
        def _():
            dstate[...] = jnp.zeros_like(dstate)

        def chunk(cc, carry):
            chains = []
            for d in range(2):
                c = ncb - 1 - cc if d == 0 else cc
                rows = pl.ds(pl.multiple_of(c * CHUNK, CHUNK), CHUNK)
                for h in range(GDN_H):
                    cols = slice(h * GDN_DK, (h + 1) * GDN_DK)
                    chains.append(dict(d=d, h=h, c=c, rows=rows, cols=cols, dsn=dstate[d * GDN_H + h]))
            for ch in chains:
                qd_ref, kd_ref, cd_ref, w_ref, a_ref, do_ref = ins[ch["d"]]
                rows, cols = ch["rows"], ch["cols"]
                dob = do_ref[rows, cols].astype(BF16)
                ch["dvn"] = (_dot(a_ref[ch["c"], ch["h"]], dob, 0, 0)
                             + _dot(kd_ref[rows, cols], ch["dsn"].astype(BF16), 1, 0))
                ch["qdo"] = _dot(qd_ref[rows, cols], dob, 0, 0)
            for ch in chains:
                w_ref = ins[ch["d"]][3]
                ch["wdvn"] = _dot(w_ref[ch["rows"], ch["cols"]], ch["dvn"].astype(BF16), 0, 0)
            for ch in chains:
                dvn_ref, ds_ref = outs[ch["d"]]
                cd_ref = ins[ch["d"]][2]
                col = ch["d"] * GDN_H + ch["h"]
                dvn_ref[ch["rows"], ch["cols"]] = ch["dvn"].astype(BF16)
                ds_ref[ch["c"], ch["h"]] = ch["dsn"].astype(BF16)
                dstate[ch["d"] * GDN_H + ch["h"]] = (ch["qdo"] + cd_ref[ch["c"], col:col + 1, :] * ch["dsn"]
                                                     - ch["wdvn"])
            return carry

        lax.fori_loop(0, ncb, chunk, 0)

    ins, outs = [], []
    for d in range(2):
        tix = _dir_tile(d, n_tiles, True)
        im = lambda i, tix=tix: (tix(i), 0)
        im4 = lambda i, tix=tix: (tix(i), 0, 0, 0)
        _, w, a, _, qd, kd = loc[d]
        ins += [(qd, (ts, GDN_W), im), (kd, (ts, GDN_W), im), (loc[2], (ncb, 8, 128), lambda i, tix=tix: (tix(i), 0, 0)),
                (w, (ts, GDN_W), im), (a, (ncb, GDN_H, CHUNK, CHUNK), im4), (do, (ts, GDN_W), im)]
        outs += [(jax.ShapeDtypeStruct((S, GDN_W), BF16), (ts, GDN_W), im),
                 (jax.ShapeDtypeStruct((nch, GDN_H, GDN_DK, GDN_DK), BF16), (ncb, GDN_H, GDN_DK, GDN_DK), im4)]
    res = _rows("gdn_scan_bwd", S, ts, ins, outs, body, scratch=[pltpu.VMEM((2 * GDN_H, GDN_DK, GDN_DK), F32)])
    return res[0:2], res[2:4]


def _gdn_local_bwd(q, k, v, bg, gcr, do, loc, fwd, adj):
    S = q.shape[0]
    ts = _tile(S, GDN_TS)
    ncb = ts // CHUNK

    def body(q_ref, k_ref, v_ref, bg_ref, gcr_ref, do_ref, *rest):
        per_dir = (rest[0:5], rest[5:10])
        dq_ref, dk_ref, dv_ref, dbg_ref, dbgr_ref = rest[10:15]
        ri, ci = _tri_masks()
        lane = lax.broadcasted_iota(jnp.int32, (CHUNK, 128), 1)
        rowi = lax.broadcasted_iota(jnp.int32, (CHUNK, 1), 0)
        ones8 = jnp.ones((SUBLANES, CHUNK), F32)

        def chunk(c, carry):
            r0 = pl.multiple_of(c * CHUNK, CHUNK)
            rows = pl.ds(r0, CHUNK)
            chains = []
            for h in range(GDN_H):
                cols = slice(h * GDN_DK, (h + 1) * GDN_DK)
                qh, kh, vh = q_ref[rows, cols], k_ref[rows, cols], v_ref[rows, cols]
                dob = do_ref[rows, cols].astype(BF16)
                both = _bdot(jnp.concatenate([qh, kh], axis=0), kh, 1, 1)
                for d in range(2):
                    chains.append(dict(h=h, d=d, cols=cols, qh=qh, kh=kh, vh=vh, dob=dob, qk=both[0:CHUNK],
                                       kk=both[CHUNK:2 * CHUNK], col=d * GDN_H + h))
            for ch in chains:
                m = _gdn_decay(bg_ref, gcr_ref, c, rows, r0, ch["col"], ch["d"] == 1, ri, ci)
                t_ref, s_ref, ds_ref, vn_ref, dvn_ref = per_dir[ch["d"]]
                h, cols = ch["h"], ch["cols"]
                ch["m"] = m
                ch["kb"] = ch["kh"] * m["beta"]
                ch["kbg"] = ch["kb"] * m["eg"]
                ch["t"] = t_ref[c, h]
                stb = s_ref[c, h]
                ch["dsn"] = ds_ref[c, h]
                vnb = vn_ref[rows, cols]
                dvnb = dvn_ref[rows, cols]
                ch["dcd"] = jnp.sum(jnp.sum(stb.astype(F32) * ch["dsn"].astype(F32), axis=1, keepdims=True),
                                    axis=0, keepdims=True)
                ch["dqd"] = _dot(ch["dob"], stb, 1, 1)
                ch["d_a"] = _dot(ch["dob"], vnb, 1, 1)
                ch["dkd"] = _bdot(vnb, ch["dsn"], 1, 1)
                ch["dw"] = -_dot(dvnb, stb, 1, 1)
                ch["dvb"] = _dot(ch["t"], dvnb, 1, 0)
                ch["d_t"] = _bdot(dvnb, ch["vh"] * m["beta"], 1, 1)
            for ch in chains:
                dwb = ch["dw"].astype(BF16)
                ch["d_t"] = ch["d_t"] + _bdot(dwb, ch["kbg"], 1, 1)
                ch["dkbg"] = _dot(ch["t"], dwb, 1, 0)
                ch["nn"] = ch["d_a"] * ch["m"]["dm"]
                ch["nn_q"] = _bdot(ch["nn"], ch["qh"], 0, 0)
                ch["nn_k"] = _bdot(ch["nn"], ch["kh"], 1, 0)
            for ch in chains:
                ch["x"] = _dot(ch["d_t"].astype(BF16), ch["t"], 1, 0)
            for ch in chains:
                d_l = -_dot(ch["t"], ch["x"].astype(BF16), 1, 0)
                ch["d_l"] = jnp.where(ch["m"]["strict"], d_l, 0.0)
                ch["mm"] = ch["d_l"] * ch["m"]["dm"]
            for ch in chains:
                m = ch["m"]
                ch["mm_kh"] = _bdot(ch["mm"], ch["kh"], 1, 0)
                ch["mm_kb"] = _bdot(ch["mm"], ch["kb"], 0, 0)
                l_mat = jnp.where(m["strict"], m["beta"] * ch["kk"] * m["dm"], 0.0)
                ch["e"] = ch["d_l"] * l_mat + ch["nn"] * ch["qk"]
                dbgr_ref[c, ch["col"]:ch["col"] + 1, :] = -_dot(ones8, ch["e"], 1, 0, HI)[0:1, :]
            acc_bg = jnp.zeros((CHUNK, 128), F32)
            acc = {}
            for ch in chains:
                m = ch["m"]
                beta, eg, egl = m["beta"], m["eg"], m["egl"]
                dkb = ch["mm_kh"] + ch["dkbg"] * eg
                dk_d = ch["mm_kb"] + ch["nn_q"] + ch["dkd"] * egl + dkb * beta
                dq_d = ch["nn_k"] + ch["dqd"] * eg
                dv_d = ch["dvb"] * beta
                dkd_kd = ch["dkd"] * (ch["kh"] * egl)
                dgc = (jnp.sum(ch["e"], axis=1, keepdims=True)
                       + jnp.sum(ch["dqd"] * (ch["qh"] * eg) - dkd_kd + ch["dkbg"] * ch["kbg"], axis=1, keepdims=True))
                dgl = jnp.sum(jnp.sum(dkd_kd, axis=1, keepdims=True), axis=0, keepdims=True) + ch["dcd"] * m["cd"]
                dgc = dgc + jnp.where(rowi == (0 if ch["d"] == 1 else CHUNK - 1), dgl, 0.0)
                dbeta = jnp.sum(dkb * ch["kh"] + ch["dvb"] * ch["vh"], axis=1, keepdims=True)
                acc_bg = acc_bg + jnp.where(lane == ch["col"], dbeta, 0.0) + jnp.where(lane == 8 + ch["col"], dgc, 0.0)
                if ch["d"] == 0:
                    acc[ch["h"]] = (dq_d, dk_d, dv_d)
                else:
                    dq0, dk0, dv0 = acc[ch["h"]]
                    dq_ref[rows, ch["cols"]] = dq0 + dq_d
                    dk_ref[rows, ch["cols"]] = dk0 + dk_d
                    dv_ref[rows, ch["cols"]] = dv0 + dv_d
            dbg_ref[rows, :] = acc_bg
            return carry

        lax.fori_loop(0, ncb, chunk, 0)

    im = lambda i: (i, 0)
    im4 = lambda i: (i, 0, 0, 0)
    blk = (ts, GDN_W)
    ins = [(q, blk, im), (k, blk, im), (v, blk, im), (bg, (ts, 128), im), (gcr, (ncb, 8, CHUNK), lambda i: (i, 0, 0)),
           (do, blk, im)]
    for d in range(2):
        ins += [(loc[d][3], (ncb, GDN_H, CHUNK, CHUNK), im4), (fwd[d][2], (ncb, GDN_H, GDN_DK, GDN_DK), im4),
                (adj[d][1], (ncb, GDN_H, GDN_DK, GDN_DK), im4), (fwd[d][1], blk, im), (adj[d][0], blk, im)]
    sds = jax.ShapeDtypeStruct((S, GDN_W), F32)
    outs = [(sds, blk, im), (sds, blk, im), (sds, blk, im), (jax.ShapeDtypeStruct((S, 128), F32), (ts, 128), im),
            (jax.ShapeDtypeStruct((S // CHUNK, 8, CHUNK), F32), (ncb, 8, CHUNK), lambda i: (i, 0, 0))]
    dq, dk, dv, dbg, dbg_rows = _rows("gdn_local_bwd", S, ts, ins, outs, body)
    dgc_cols = dbg_rows.transpose(0, 2, 1).reshape(S, 8)
    return dq, dk, dv, dbg + jnp.pad(dgc_cols, ((0, 0), (8, 112)))


def _gdn_prep_bwd(dbg_all, p, prm):
    S = p.shape[0]
    ts = _tile(S, 512)

    def body(dbg_ref, p_ref, prm_ref, dba_ref, dprm_ref):
        i = pl.program_id(0)
        raw = p_ref[...].astype(F32)
        dbg = dbg_ref[...]
        lane = lax.broadcasted_iota(jnp.int32, (1, 128), 1)
        is_g = (lane >= 8) & (lane < 16)
        ea = jnp.exp(prm_ref[0:1, :])
        arg = raw + prm_ref[1:2, :]
        g = jnp.where(is_g, -ea * _softplus(arg), 0.0)
        beta = _sigmoid(raw)
        dgc = jnp.where(is_g, dbg, 0.0)
        ri, ci = _tri_masks()
        lower = (ri >= ci).astype(F32)
        upper = (ri <= ci).astype(F32)
        dgs = []
        for c in range(ts // CHUNK):
            ch = dgc[c * CHUNK:(c + 1) * CHUNK]
            dgs.append(jnp.where(lane < 12, _dot(upper, ch, 1, 0, HI), _dot(lower, ch, 1, 0, HI)))
        dg = jnp.concatenate(dgs, axis=0)
        dalpha = jnp.where(is_g, dg * (-ea) * _sigmoid(arg), 0.0)
        dba_ref[...] = jnp.where(lane < 8, dbg * beta * (1.0 - beta), dalpha).astype(BF16)
        rows = jnp.concatenate([jnp.sum(dg * g, axis=0, keepdims=True), jnp.sum(dalpha, axis=0, keepdims=True),
                                jnp.zeros((6, 128), F32)], axis=0)
        _colsum_into(dprm_ref, i, rows)

    im = lambda i: (i, 0)
    z0 = lambda i: (0, 0)
    return _rows("gdn_prep_bwd", S, ts,
                 [(dbg_all, (ts, 128), im), (p, (ts, 128), lambda i: (i, COL_BA // 128)), (prm, (8, 128), z0)],
                 [(jax.ShapeDtypeStruct((S, 128), BF16), (ts, 128), im), (jax.ShapeDtypeStruct((8, 128), F32), (8, 128), z0)],
                 body)


def _mm_plain(name, M, N, K, tm, tn, tk, a, am, b, bm, dtype):
    return _fused_mm(name, M, N, K, tm, tn, tk, [(a, am), (b, bm)], [(0, 1, 0)], [],
                     [(jax.ShapeDtypeStruct((M, N), dtype), (tm, tn), _mn)],
                     lambda i, accs, ex, out: out[0].__setitem__(Ellipsis, accs[0][...].astype(dtype)))[0]


def _layer_bwd(x0, W, R, emit_big=None, emit_small=None):
    S = x0.shape[0]
    tm = _tile(S, 512)
    tk_s = _tile(S, 1024)
    G = {}

    def emit(**named):
        if emit_big is None:
            G.update(named)
            return None
        return emit_big(**named)

    def ffn_emit(prefix):
        return lambda **kw: emit(**{f"{prefix}_w_{k}": v for k, v in kw.items()})

    dx2, G["ffn2_norm"] = _ffn_bwd("ffn2b", R["dx3"], R["x2"], W["ffn2_norm"], R["h3"], R["a2"], R["b2"], R["f2"],
                                   W["ffn2_w_gate"], W["ffn2_w_up"], W["ffn2_w_down"], ffn_emit("ffn2"))
    tok = emit(w_out=_mm_plain("dw_out", D_MODEL, D_MODEL, S, D_MODEL, D_MODEL, tk_s, R["y"], "km", dx2, "kn", BF16))
    gn = W["gdn_norm"] if tok is None else W["gdn_norm"] + tok
    p = R["p"]
    dhr, dgate, do, dz, G["gdn_norm"] = _mix_out_bwd(dx2, W["w_out"], R["h_f"], R["h_b"], R["o_f"], R["o_b"], p, gn)
    lam_b, lam_f = _rg_scan_adj("rg_scan_bwd", R["a_b"], dhr, R["a_f"], dhr)
    dpre, dxc_direct, d_rgprm = _rg_gates_bwd(R["xc"], R["bd"], R["rg_prm"], lam_f, lam_b, R["h_f"], R["h_b"])
    tmg = _tile(S, 512)
    dxc = _fused_mm("rg_dxc", S, RG_W, 4 * RG_W, tmg, RG_W, 4 * RG_W, [(dpre, "mk"), (R["bd"], "nk")], [(0, 1, 0)],
                    [(dxc_direct, (tmg, RG_W), _mn)], [(jax.ShapeDtypeStruct((S, RG_W), F32), (tmg, RG_W), _mn)],
                    lambda i, accs, ex, out: out[0].__setitem__(Ellipsis, ex[0][...] + accs[0][...]))[0]
    d_bd = _mm_plain("rg_dbd", RG_W, 4 * RG_W, S, RG_W, 4 * RG_W, tk_s, R["xc"], "km", dpre, "kn", F32)
    dx_rg, G["rg_conv_w"], G["rg_conv_b"] = _conv_bwd("rg_conv_bwd", p, 0, W["rg_conv_w"], [dxc], "bias")
    blocks = jnp.einsum("nigmj,nm->gnij", d_bd.reshape(RG_BLOCKS, RG_BLOCK, 4, RG_BLOCKS, RG_BLOCK),
                        jnp.eye(RG_BLOCKS, dtype=F32))
    G["rg_gate_a_w"] = jnp.stack([blocks[0], blocks[2]])
    G["rg_gate_x_w"] = jnp.stack([blocks[1], blocks[3]])
    G["rg_gate_a_b"] = jnp.stack([d_rgprm[0], d_rgprm[2]])
    G["rg_gate_x_b"] = jnp.stack([d_rgprm[1], d_rgprm[3]])
    G["rg_lambda"] = d_rgprm[4:6]
    adj = _gdn_scan_bwd(R["gdn_loc"], do)
    dq, dk, dv, dbg = _gdn_local_bwd(R["q"], R["k"], R["v"], R["bg"], R["gcr"], do, R["gdn_loc"], R["gdn_fwd"], adj)
    cw = W["gdn_conv_w"]
    dpq, dwq, _ = _conv_bwd("gdn_conv_q_bwd", p, 2, cw[:, 0:512], [dq], "q")
    dpk, dwk, _ = _conv_bwd("gdn_conv_k_bwd", p, 3, cw[:, 512:1024], [dk], "k")
    dpv, dwv, _ = _conv_bwd("gdn_conv_v_bwd", p, 4, cw[:, 1024:1536], [dv], "v")
    G["gdn_conv_w"] = jnp.concatenate([dwq, dwk, dwv], axis=1)
    dba, d_gprm = _gdn_prep_bwd(dbg, p, R["gdn_prm"])
    G["gdn_a_log"] = d_gprm[0, 8:16].reshape(2, GDN_H)
    G["gdn_dt_bias"] = d_gprm[1, 8:16].reshape(2, GDN_H)
    dp = jnp.concatenate([dx_rg, dgate, dpq, dpk, dpv, dz, dba], axis=1)
    tok = emit(w_in=_mm_plain("dw_in", D_MODEL, D_IN_PAD, S, D_MODEL, 640, tk_s, R["h2"], "km", dp, "kn", BF16))
    g_mix = W["mix_norm"] if tok is None else W["mix_norm"] + tok

    def epi_dx1(i, accs, ex, out):
        dx, dgt = _rmsnorm_bwd_tile(accs[0][...], ex[0][...], ex[1][...])
        out[0][...] = ex[2][...] + dx
        _colsum_into(out[1], i, jnp.sum(dgt, axis=0, keepdims=True))

    dx1, G["mix_norm"] = _fused_mm(
        "mix_dx", S, D_MODEL, D_IN_PAD, tm, D_MODEL, D_IN_PAD, [(dp, "mk"), (W["w_in"], "kn")], [(0, 1, 0)],
        [(R["x1"], (tm, D_MODEL), _mn), (g_mix, (1, D_MODEL), _row0), (dx2, (tm, D_MODEL), _mn)],
        [(jax.ShapeDtypeStruct((S, D_MODEL), F32), (tm, D_MODEL), _mn),
         (jax.ShapeDtypeStruct((1, D_MODEL), F32), (1, D_MODEL), _row0)], epi_dx1)
    G["final_norm"] = R["d_final_norm"]
    if emit_small is not None:
        emit_small(G)
    dx0, G["ffn1_norm"] = _ffn_bwd("ffn1b", dx1, x0, W["ffn1_norm"], R["h1"], R["a1"], R["b1"], R["f1"],
                                   W["ffn1_w_gate"], W["ffn1_w_up"], W["ffn1_w_down"], ffn_emit("ffn1"))
    return dx0, G


def _mesh_pos():
    x, y, c = lax.axis_index("x"), lax.axis_index("y"), lax.axis_index("c")
    return x, y, c, 4 * x + 2 * y + c


def _peer(x, y, c, r):
    px = 1 - x if r & 4 else x
    py = 1 - y if r & 2 else y
    pc = 1 - c if r & 1 else c
    return (px, py, pc), 4 * px + 2 * py + pc


_HBM = pl.BlockSpec(memory_space=pltpu.HBM)
_SEM = pl.BlockSpec(memory_space=pltpu.SEMAPHORE)


def _peer_copies(scatter, srcs, lands, send_sems, recv_sems):
    x, y, c, me = _mesh_pos()
    copies = []
    for a, (src, land) in enumerate(zip(srcs, lands)):
        for r in range(1, N_DEV):
            peer, peer_idx = _peer(x, y, c, r)
            copies.append(pltpu.make_async_remote_copy(
                src_ref=src.at[peer_idx] if scatter else src, dst_ref=land.at[r - 1] if scatter else land.at[me],
                send_sem=send_sems.at[a * 7 + r - 1], recv_sem=recv_sems.at[a * 7 + r - 1],
                device_id=peer, device_id_type=pl.DeviceIdType.MESH))
    return copies


def _exchange_start(name, scatter, arrays):
    slabs = arrays
    n = len(slabs)

    def body(*refs):
        srcs, lands = refs[0:n], refs[n:2 * n]
        send_sems, recv_sems = refs[2 * n], refs[2 * n + 1]
        token = refs[4 * n + 2]
        for cp in _peer_copies(scatter, srcs, lands, send_sems, recv_sems):
            cp.start()
        token[...] = jnp.zeros_like(token)

    land_shapes = [(N_DEV - 1,) + s.shape[1:] if scatter else (N_DEV,) + s.shape for s in slabs]
    n_sems = 7 * n
    out_shape = ([pltpu.SemaphoreType.DMA((n_sems,)), pltpu.SemaphoreType.DMA((n_sems,))]
                 + [pltpu.HBM(s.shape, s.dtype) for s in slabs]
                 + [pltpu.HBM(shp, s.dtype) for shp, s in zip(land_shapes, slabs)]
                 + [jax.ShapeDtypeStruct((8, 128), F32)])
    res = pl.pallas_call(
        body, name=name, out_shape=out_shape, in_specs=[_HBM] * (2 * n),
        out_specs=[_SEM, _SEM] + [_HBM] * (2 * n) + [pl.BlockSpec(memory_space=pltpu.VMEM)],
        input_output_aliases={i: 2 + i for i in range(2 * n)},
        compiler_params=pltpu.CompilerParams(has_side_effects=pltpu.SideEffectType.DATAFLOW_SIDE_EFFECTING),
    )(*[pltpu.with_memory_space_constraint(s, pltpu.HBM) for s in slabs],
      *[pltpu.with_memory_space_constraint(lax.empty(shp, s.dtype), pltpu.HBM) for shp, s in zip(land_shapes, slabs)])
    return dict(n=n, scatter=scatter, sems=res[0:2], srcs=res[2:2 + n], lands=res[2 + n:2 + 2 * n],
                token=res[2 + 2 * n][0, 0])


def _exchange_wait(name, started, after):
    n = started["n"]
    scatter = started["scatter"]

    def body(*refs):
        srcs, lands = refs[0:n], refs[n:2 * n]
        send_sems, recv_sems = refs[2 * n], refs[2 * n + 1]
        for cp in _peer_copies(scatter, srcs, lands, send_sems, recv_sems):
            cp.wait_send()
            cp.wait_recv()

    arrays = list(started["srcs"]) + list(started["lands"])
    res = pl.pallas_call(
        body, name=name, out_shape=[pltpu.HBM(a.shape, a.dtype) for a in arrays],
        in_specs=[_HBM] * (2 * n) + [_SEM, _SEM, pl.BlockSpec(memory_space=pl.ANY)], out_specs=[_HBM] * (2 * n),
        input_output_aliases={i: i for i in range(2 * n)},
        compiler_params=pltpu.CompilerParams(has_side_effects=pltpu.SideEffectType.DATAFLOW_SIDE_EFFECTING),
    )(*arrays, *started["sems"], after)
    return res[0:n], res[n:2 * n]


def _adamw_math(w, g, m, v):
    m2 = ADAM_B1 * m + (1.0 - ADAM_B1) * g
    v2 = ADAM_B2 * v + (1.0 - ADAM_B2) * (g * g)
    m_hat = m2 / (1.0 - ADAM_B1 ** ADAM_STEP)
    v_hat = v2 / (1.0 - ADAM_B2 ** ADAM_STEP)
    delta = -ADAM_LR * (m_hat / (jnp.sqrt(v_hat) + ADAM_EPS) + ADAM_WD * w)
    return delta, m2, v2


def _adamw_slabs(name, src, land, me, w, m, v, tr):
    R, C = w.shape

    def body(me_ref, own_ref, land_ref, w_ref, m_ref, v_ref, g_ref, d_ref, m2_ref, v2_ref):
        g = own_ref[0].astype(F32)
        for s in range(N_DEV - 1):
            g = g + land_ref[s].astype(F32)
        delta, m2, v2 = _adamw_math(w_ref[...], g, m_ref[...], v_ref[...])
        g_ref[...] = g
        d_ref[...] = delta
        m2_ref[...] = m2
        v2_ref[...] = v2

    im = lambda i, me_ref: (i, 0)
    grid_spec = pltpu.PrefetchScalarGridSpec(
        num_scalar_prefetch=1, grid=(R // tr,),
        in_specs=[pl.BlockSpec((1, tr, C), lambda i, me_ref: (me_ref[0], i, 0)),
                  pl.BlockSpec((N_DEV - 1, tr, C), lambda i, me_ref: (0, i, 0)),
                  pl.BlockSpec((tr, C), im), pl.BlockSpec((tr, C), im), pl.BlockSpec((tr, C), im)],
        out_specs=[pl.BlockSpec((tr, C), im)] * 4)
    return pl.pallas_call(body, name=name, grid_spec=grid_spec, out_shape=[jax.ShapeDtypeStruct((R, C), F32)] * 4,
                          compiler_params=_cp(1))(me.reshape(1).astype(jnp.int32), src, land, w, m, v)


def _sum_slots(name, slots):
    _, R, C = slots.shape

    def body(s_ref, o_ref):
        g = s_ref[0]
        for s in range(1, N_DEV):
            g = g + s_ref[s]
        o_ref[...] = g

    return _rows(name, R, R, [(slots, (N_DEV, R, C), lambda i: (0, 0, 0))],
                 [(jax.ShapeDtypeStruct((R, C), F32), (R, C), lambda i: (0, 0))], body)[0]


def _adamw_packed(name, g, w, m, v):
    R, C = g.shape

    def body(g_ref, w_ref, m_ref, v_ref, d_ref, m2_ref, v2_ref):
        delta, m2, v2 = _adamw_math(w_ref[...], g_ref[...], m_ref[...], v_ref[...])
        d_ref[...] = delta
        m2_ref[...] = m2
        v2_ref[...] = v2

    im = lambda i: (0, 0)
    sds = jax.ShapeDtypeStruct((R, C), F32)
    return _rows(name, R, R, [(a, (R, C), im) for a in (g, w, m, v)], [(sds, (R, C), im)] * 3, body)


def _pack(arrays):
    rows = []
    for a in arrays:
        flat = a.reshape(-1).astype(F32)
        pad = (-flat.shape[0]) % 128
        rows.append(jnp.pad(flat, (0, pad)).reshape(-1, 128))
    out = jnp.concatenate(rows, axis=0)
    return jnp.pad(out, ((0, (-out.shape[0]) % 8), (0, 0)))


def _unpack(packed, shapes):
    lead = packed.shape[:-2]
    outs = []
    r = 0
    for shp in shapes:
        n = math.prod(shp)
        nr = -(-n // 128)
        flat = packed[..., r:r + nr, :].reshape(lead + (nr * 128,))[..., :n]
        outs.append(flat.reshape(lead + tuple(shp)))
        r += nr
    return outs


FFN1_BIG = ["ffn1_w_gate", "ffn1_w_up", "ffn1_w_down"]
MIX_BIG = ["w_in", "w_out"]
FFN2_BIG = ["ffn2_w_gate", "ffn2_w_up", "ffn2_w_down"]
BIG = FFN1_BIG + MIX_BIG + FFN2_BIG
COL_SHARDED = {"ffn1_w_gate", "ffn1_w_up", "w_in", "ffn2_w_gate", "ffn2_w_up"}
SMALL_SHARDED = ["rg_conv_w", "rg_gate_a_b", "rg_gate_x_b", "rg_lambda", "gdn_conv_w"]
WEIGHTS = ["ffn1_norm", "ffn1_w_gate", "ffn1_w_up", "ffn1_w_down", "mix_norm", "w_in", "w_out", "rg_conv_w", "rg_conv_b",
           "rg_gate_a_w", "rg_gate_a_b", "rg_gate_x_w", "rg_gate_x_b", "rg_lambda", "gdn_conv_w", "gdn_a_log",
           "gdn_dt_bias", "gdn_norm", "ffn2_norm", "ffn2_w_gate", "ffn2_w_up", "ffn2_w_down", "final_norm"]
SMALL = [n for n in WEIGHTS if n not in BIG]
ROW_VECTORS = {"ffn1_norm", "mix_norm", "ffn2_norm", "gdn_norm", "rg_conv_b", "final_norm"}
ROW_TILE = {"ffn1_w_gate": 256, "ffn1_w_up": 256, "ffn1_w_down": 176, "w_in": 256, "w_out": 64,
            "ffn2_w_gate": 256, "ffn2_w_up": 256, "ffn2_w_down": 176}


def _to_slabs(name, g):
    if name in COL_SHARDED:
        r, ctot = g.shape
        return g.reshape(r, N_DEV, ctot // N_DEV).transpose(1, 0, 2)
    return g.reshape(N_DEV, g.shape[0] // N_DEV, g.shape[1])


def _step(x, target, w, m, v):
    _, _, _, me = _mesh_pos()
    def shard_to_send(n, tok=None):
        s = w[n] if tok is None else w[n] + tok
        return (s.T if n in COL_SHARDED else s).astype(BF16)

    def unshard(n, gth):
        full = gth.reshape(-1, gth.shape[-1])
        return jnp.pad(full, ((0, D_IN_PAD - D_IN), (0, 0))) if n == "w_in" else full

    def landed(started, name, after):
        srcs, lands = _exchange_wait(name, started, after)
        def with_own(src, land):
            slot = lax.broadcasted_iota(jnp.int32, (N_DEV,) + (1,) * src.ndim, 0)
            return jnp.where(slot == me, src[None], land)

        return [with_own(src, land) for src, land in zip(srcs, lands)]

    up_names = ["ffn1_w_gate", "ffn1_w_up"]
    small_shards = [w[n] for n in SMALL_SHARDED]
    st_up = _exchange_start("gather_ffn1_up_start", False, [shard_to_send(n) for n in up_names])
    tok = st_up["token"]
    st_down = _exchange_start("gather_ffn1_down_start", False, [shard_to_send("ffn1_w_down", tok)])
    tok = tok + st_down["token"]
    st_mix = _exchange_start("gather_mix_start", False,
                             [shard_to_send(n, tok) for n in MIX_BIG] + [_pack(small_shards) + tok])
    tok = tok + st_mix["token"]
    st_ffn2 = _exchange_start("gather_ffn2_start", False, [shard_to_send(n, tok) for n in FFN2_BIG])
    W = {n: w[n] for n in SMALL if n not in SMALL_SHARDED}
    W["ffn1_norm"] = w["ffn1_norm"] + (tok + st_ffn2["token"])

    def more(stage, after):
        if stage == "ffn1_up":
            return {n: unshard(n, gth) for n, gth in zip(up_names, landed(st_up, "gather_ffn1_up_wait", after))}
        if stage == "ffn1_down":
            return {"ffn1_w_down": unshard("ffn1_w_down", landed(st_down, "gather_ffn1_down_wait", after)[0])}
        if stage == "ffn2":
            return {n: unshard(n, gth) for n, gth in zip(FFN2_BIG, landed(st_ffn2, "gather_ffn2_wait", after))}
        got = landed(st_mix, "gather_mix_wait", after)
        new = {n: unshard(n, gth) for n, gth in zip(MIX_BIG, got)}
        for n, gth in zip(SMALL_SHARDED, _unpack(got[-1], [s.shape for s in small_shards])):
            new[n] = jnp.moveaxis(gth, 0, -2).reshape(gth.shape[1:-1] + (N_DEV * gth.shape[-1],))
        return new

    R = _layer_fwd(x, target, W, more)
    W = R["W"]
    pending = []

    def emit_big(**named):
        slabs = [_to_slabs(n, g[:, :D_IN] if n == "w_in" else g) for n, g in named.items()]
        started = _exchange_start(f"scatter_start_{len(pending)}", True, slabs)
        pending.append((list(named), started))
        return started["token"]

    small_started = []

    def emit_small(G):
        packed = _pack([G[n] for n in SMALL if n != "ffn1_norm"])
        small_started.append(_exchange_start("gather_small_start", False, [packed]))

    grad_x, G = _layer_bwd(x, W, R, emit_big, emit_small)
    st_late = _exchange_start("gather_ffn1_norm_start", False, [_pack([G["ffn1_norm"]])])
    loss = lax.psum(R["loss"][0, 0], ("x", "y", "c"))
    out = {}

    def finish(i, after):
        names, started = pending[i]
        srcs, lands = _exchange_wait(f"scatter_wait_{i}", started, after)
        for n, src, land in zip(names, srcs, lands):
            out[n] = _adamw_slabs(f"adamw_{n}", src, land, me, w[n], m[n], v[n], ROW_TILE[n])

    n_early = len(pending) - 2
    for i in range(n_early):
        finish(i, grad_x)
    early = [n for n in SMALL if n != "ffn1_norm"]
    srcs, lands = _exchange_wait("gather_small_wait", small_started[0], grad_x)
    slot = lax.broadcasted_iota(jnp.int32, (N_DEV, 1, 1), 0)
    slots = jnp.where(slot == me, srcs[0][None], lands[0])
    reduced = dict(zip(early, _unpack(_sum_slots("sum_small_grads", slots), [G[n].shape for n in early])))

    def adamw_small(name, names):
        g_small = []
        for n in names:
            g = reduced[n]
            if n in SMALL_SHARDED:
                per = g.shape[-1] // N_DEV
                g = lax.dynamic_slice_in_dim(g, me * per, per, axis=g.ndim - 1)
            g_small.append(g.reshape(w[n].shape))
        shapes = [w[n].shape for n in names]
        d_p, m_p, v_p = _adamw_packed(name, _pack(g_small), _pack([w[n] for n in names]),
                                      _pack([m[n] for n in names]), _pack([v[n] for n in names]))
        for n, g, d_, m_, v_ in zip(names, g_small, _unpack(d_p, shapes), _unpack(m_p, shapes), _unpack(v_p, shapes)):
            out[n] = (g, d_, m_, v_)
        return d_p

    done_early = adamw_small("adamw_small", early)
    srcs, lands = _exchange_wait("gather_ffn1_norm_wait", st_late, done_early)
    late = jnp.where(slot == me, srcs[0][None], lands[0])
    reduced["ffn1_norm"] = _unpack(_sum_slots("sum_ffn1_norm_grad", late), [G["ffn1_norm"].shape])[0]
    done = adamw_small("adamw_ffn1_norm", ["ffn1_norm"])
    for i in range(n_early, len(pending)):
        finish(i, done)
    return loss, grad_x, out


def kernel(x, ffn1_norm, ffn1_w_gate, ffn1_w_up, ffn1_w_down, mix_norm, w_in, w_out, rg_conv_w, rg_conv_b, rg_gate_a_w, rg_gate_a_b, rg_gate_x_w, rg_gate_x_b, rg_lambda, gdn_conv_w, gdn_a_log, gdn_dt_bias, gdn_norm, ffn2_norm, ffn2_w_gate, ffn2_w_up, ffn2_w_down, final_norm, loss_target, m_ffn1_norm, m_ffn1_w_gate, m_ffn1_w_up, m_ffn1_w_down, m_mix_norm, m_w_in, m_w_out, m_rg_conv_w, m_rg_conv_b, m_rg_gate_a_w, m_rg_gate_a_b, m_rg_gate_x_w, m_rg_gate_x_b, m_rg_lambda, m_gdn_conv_w, m_gdn_a_log, m_gdn_dt_bias, m_gdn_norm, m_ffn2_norm, m_ffn2_w_gate, m_ffn2_w_up, m_ffn2_w_down, m_final_norm, v_ffn1_norm, v_ffn1_w_gate, v_ffn1_w_up, v_ffn1_w_down, v_mix_norm, v_w_in, v_w_out, v_rg_conv_w, v_rg_conv_b, v_rg_gate_a_w, v_rg_gate_a_b, v_rg_gate_x_w, v_rg_gate_x_b, v_rg_lambda, v_gdn_conv_w, v_gdn_a_log, v_gdn_dt_bias, v_gdn_norm, v_ffn2_norm, v_ffn2_w_gate, v_ffn2_w_up, v_ffn2_w_down, v_final_norm):
    args = dict(locals())
    orig_shapes = {n: args[n].shape for n in WEIGHTS}

    def local(prefix):
        d = {}
        for n in WEIGHTS:
            a = args[prefix + n]
            d[n] = a.reshape(1, -1) if n in ROW_VECTORS else a[0]
        return d

    loss, grad_x, out = _step(x[0], loss_target[0], local(""), local("m_"), local("v_"))
    res = [loss, grad_x[None]]
    for k in range(4):
        res += [out[n][k].reshape(orig_shapes[n]) for n in WEIGHTS]
    return tuple(res)
```

```python
import functools
import math

import jax
import jax.numpy as jnp
from jax import lax
from jax.experimental import pallas as pl
from jax.experimental.pallas import tpu as pltpu

F32, BF16 = jnp.float32, jnp.bfloat16

D_MODEL = 1024
D_FF = 2816
RG_W = 512
RG_BLOCKS = 8
RG_BLOCK = 64
RG_C = 8.0
CONV_W = 4
GDN_H = 4
GDN_DK = 128
CHUNK = 64
EPS = 1e-6
D_IN = 3088
D_IN_PAD = 3200
COL_BA = 3072
N_DEV = 8
HALO = 16
VMEM_LIMIT = 48 * 1024 * 1024
VMEM_CAP = 60 * 1024 * 1024

ADAM_LR = 0.001
ADAM_B1 = 0.9
ADAM_B2 = 0.999
ADAM_EPS = 1e-08
ADAM_WD = 0.01
ADAM_STEP = 10

HI = lax.Precision.HIGHEST


def _cp(n, vmem_limit=None):
    return pltpu.CompilerParams(dimension_semantics=("arbitrary",) * n,
                                vmem_limit_bytes=VMEM_LIMIT if vmem_limit is None else vmem_limit)


def _matmul_vmem_limit(block_bytes, acc_bytes):
    need = 2 * block_bytes + 2 * acc_bytes
    return int(min(VMEM_CAP, max(VMEM_LIMIT, need * 4 // 3)))


def _tile(n, pref):
    return min(n, pref)


def _sigmoid(x):
    return 0.5 * jnp.tanh(0.5 * x) + 0.5


def _softplus(x):
    return jnp.maximum(x, 0.0) + jnp.log(1.0 + jnp.exp(-jnp.abs(x)))


def _dot(a, b, ca, cb, prec=None):
    return lax.dot_general(a, b, (((ca,), (cb,)), ((), ())), preferred_element_type=F32, precision=prec)


def _fused_mm(name, M, N, K, tm, tn, tk, ops, pairs, extras, outs, epilogue):
    nm, nn, nk = M // tm, N // tn, K // tk
    assert nm * tm == M and nn * tn == N and nk * tk == K, (name, M, N, K, tm, tn, tk)
    spec_of = {
        "mk": pl.BlockSpec((tm, tk), lambda i, j, k: (i, k)),
        "km": pl.BlockSpec((tk, tm), lambda i, j, k: (k, i)),
        "kn": pl.BlockSpec((tk, tn), lambda i, j, k: (k, j)),
        "nk": pl.BlockSpec((tn, tk), lambda i, j, k: (j, k)),
    }
    in_specs = [spec_of[m] for _, m in ops]
    in_specs += [pl.BlockSpec(bs, lambda i, j, k, im=im: im(i, j)) for _, bs, im in extras]
    out_specs = [pl.BlockSpec(bs, lambda i, j, k, im=im: im(i, j)) for _, bs, im in outs]
    n_ops, n_ex, n_out = len(ops), len(extras), len(outs)
    n_acc = 1 + max(g for _, _, g in pairs)
    modes = [m for _, m in ops]

    def body(*refs):
        op_refs = refs[:n_ops]
        ex_refs = refs[n_ops:n_ops + n_ex]
        out_refs = refs[n_ops + n_ex:n_ops + n_ex + n_out]
        accs = refs[n_ops + n_ex + n_out:]
        i = pl.program_id(0)
        k = pl.program_id(2)
        def dots():
            vals = [r[...].astype(BF16) for r in op_refs]
            for ia, ib, g in pairs:
                yield g, _dot(vals[ia], vals[ib], 1 if modes[ia] == "mk" else 0, 0 if modes[ib] == "kn" else 1)

        if nk == 1:
            sums = [None] * n_acc
            for g, d in dots():
                sums[g] = d if sums[g] is None else sums[g] + d
            epilogue(i, [_Held(s) for s in sums], ex_refs, out_refs)
            return

        @pl.when(k == 0)
        def _():
            for a in accs:
                a[...] = jnp.zeros_like(a)

        for g, d in dots():
            accs[g][...] += d

        @pl.when(k == nk - 1)
        def _():
            epilogue(i, accs, ex_refs, out_refs)

    op_block = {"mk": tm * tk, "km": tm * tk, "kn": tk * tn, "nk": tk * tn}
    block_bytes = sum(op_block[m] * a.dtype.itemsize for a, m in ops)
    block_bytes += sum(math.prod(bs) * jnp.dtype(a.dtype).itemsize for a, bs, _ in list(extras) + list(outs))
    res = pl.pallas_call(
        body, name=name, grid=(nm, nn, nk), in_specs=in_specs, out_specs=out_specs,
        out_shape=[o for o, _, _ in outs],
        scratch_shapes=[pltpu.VMEM((tm, tn), F32)] * (n_acc if nk > 1 else 0),
        compiler_params=_cp(3, _matmul_vmem_limit(block_bytes, n_acc * tm * tn * 4)),
    )(*[a for a, _ in ops], *[a for a, _, _ in extras])
    return res


class _Held:
    def __init__(self, value):
        self.value = value

    def __getitem__(self, idx):
        return self.value[idx]


def _mn(i, j):
    return (i, j)


def _row0(i, j):
    return (0, 0)


def _rows(name, S, ts, ins, outs, body, scratch=()):
    return pl.pallas_call(
        body, name=name, grid=(S // ts,),
        in_specs=[pl.BlockSpec(bs, im) for _, bs, im in ins],
        out_specs=[pl.BlockSpec(bs, im) for _, bs, im in outs],
        out_shape=[o for o, _, _ in outs],
        scratch_shapes=list(scratch),
        compiler_params=_cp(1),
    )(*[a for a, _, _ in ins])


def _halo_ins(arr, S, ts, width, colblk):
    per = ts // HALO
    last = S // HALO - 1
    return [
        (arr, (ts, width), lambda i: (i, colblk)),
        (arr, (HALO, width), lambda i: (jnp.maximum(i * per - 1, 0), colblk)),
        (arr, (HALO, width), lambda i: (jnp.minimum((i + 1) * per, last), colblk)),
    ]


def _ext(main_ref, prev_ref, next_ref, i, n_tiles):
    prev = jnp.where(i > 0, prev_ref[...].astype(F32), 0.0)
    nxt = jnp.where(i < n_tiles - 1, next_ref[...].astype(F32), 0.0)
    return jnp.concatenate([prev, main_ref[...].astype(F32), nxt], axis=0)


def _shift(ext, off, ts):
    n = ext.shape[0]
    if off == 0:
        return ext[HALO:HALO + ts]
    return pltpu.roll(ext, (-off) % n, 0)[HALO:HALO + ts]


def _rmsnorm_fwd(name, x, g):
    S, D = x.shape
    ts = _tile(S, 512)

    def body(x_ref, g_ref, o_ref):
        xv = x_ref[...]
        r = lax.rsqrt(jnp.mean(xv * xv, axis=-1, keepdims=True) + EPS)
        o_ref[...] = (xv * r * g_ref[...]).astype(BF16)

    return _rows(name, S, ts,
                 [(x, (ts, D), lambda i: (i, 0)), (g, (1, D), lambda i: (0, 0))],
                 [(jax.ShapeDtypeStruct((S, D), BF16), (ts, D), lambda i: (i, 0))], body)[0]


def _rmsnorm_bwd_tile(dh, x, g):
    r = lax.rsqrt(jnp.mean(x * x, axis=-1, keepdims=True) + EPS)
    xhat = x * r
    dxn = dh * g
    dx = r * (dxn - xhat * jnp.mean(dxn * xhat, axis=-1, keepdims=True))
    return dx, dh * xhat


def _ffn_fwd(tag, x, h, wg, wu, wd, extras, outs, finish):
    S = x.shape[0]
    tm = _tile(S, 1024)
    tn = 1408

    def epi_up(i, accs, ex, out):
        a = accs[0][...]
        b = accs[1][...]
        s = _sigmoid(a)
        sa = a * s
        out[0][...] = sa.astype(BF16)
        out[1][...] = (b * (s * (1.0 + a * (1.0 - s)))).astype(BF16)
        out[2][...] = (sa * b).astype(BF16)

    sds = jax.ShapeDtypeStruct((S, D_FF), BF16)
    a, b, f = _fused_mm(f"{tag}_up", S, D_FF, D_MODEL, tm, tn, D_MODEL,
                        [(h, "mk"), (wg, "nk"), (wu, "nk")], [(0, 1, 0), (0, 2, 1)], [],
                        [(sds, (tm, tn), _mn)] * 3, epi_up)

    def epi_down(i, accs, ex, out):
        finish(i, ex[0][...] + 0.5 * accs[0][...], ex[1:], out)

    if callable(wd):
        wd = wd(f)
    res = _fused_mm(f"{tag}_down", S, D_MODEL, D_FF, tm, D_MODEL, 1408,
                    [(f, "mk"), (wd, "kn")], [(0, 1, 0)], [(x, (tm, D_MODEL), _mn)] + extras(tm), outs(tm), epi_down)
    return res, a, b, f


def _rmsnorm_tile(xv, g):
    return (xv * lax.rsqrt(jnp.mean(xv * xv, axis=-1, keepdims=True) + EPS) * g).astype(BF16)


def _conv_taps(ext, w_ref, ts):
    acc = None
    for j in range(CONV_W):
        term = w_ref[j:j + 1, :] * _shift(ext, j - 2, ts)
        acc = term if acc is None else acc + term
    return acc


def _l2norm_heads(s, scale):
    outs = []
    for h in range(GDN_H):
        sh = s[:, h * GDN_DK:(h + 1) * GDN_DK]
        outs.append(sh * (lax.rsqrt(jnp.sum(sh * sh, axis=-1, keepdims=True) + EPS) * scale))
    return jnp.concatenate(outs, axis=-1)


def _conv_fwd(name, p, colblk, w, bias, mode):
    S = p.shape[0]
    ts = _tile(S, 512)
    n_tiles = S // ts
    C = w.shape[1]

    def body(main, prev, nxt, w_ref, b_ref, o_ref):
        i = pl.program_id(0)
        c = _conv_taps(_ext(main, prev, nxt, i, n_tiles), w_ref, ts)
        if mode == "bias":
            o_ref[...] = c + b_ref[...]
        else:
            s = c * _sigmoid(c)
            if mode == "q":
                s = _l2norm_heads(s, GDN_DK ** -0.5)
            elif mode == "k":
                s = _l2norm_heads(s, 1.0)
            o_ref[...] = s

    ins = _halo_ins(p, S, ts, C, colblk) + [(w, (CONV_W, C), lambda i: (0, 0)), (bias, (1, C), lambda i: (0, 0))]
    return _rows(name, S, ts, ins, [(jax.ShapeDtypeStruct((S, C), F32), (ts, C), lambda i: (i, 0))], body)[0]


def _rg_gate_terms(pre, xc, prm_ref, d):
    r = _sigmoid(pre[:, d * 1024:d * 1024 + RG_W] + prm_ref[2 * d:2 * d + 1, :])
    ig = _sigmoid(pre[:, d * 1024 + RG_W:(d + 1) * 1024] + prm_ref[2 * d + 1:2 * d + 2, :])
    sp = _softplus(-prm_ref[4 + d:5 + d, :])
    log_a = -RG_C * r * sp
    a = jnp.exp(log_a)
    t = jnp.tanh(log_a)
    sq = jnp.sqrt(-2.0 * t / (1.0 - t))
    return r, ig, sp, a, sq


def _rg_gates_fwd(xc, bd, prm):
    S = xc.shape[0]
    tm = _tile(S, 256)

    def epi(i, accs, ex, out):
        pre = accs[0][...]
        xv = ex[0][...]
        for d in range(2):
            r, ig, sp, a, sq = _rg_gate_terms(pre, xv, ex[1], d)
            out[2 * d][...] = a
            out[2 * d + 1][...] = sq * ig * xv

    sds = jax.ShapeDtypeStruct((S, RG_W), F32)
    blk = (tm, RG_W)
    im = lambda i, j: (i, 0)
    return _fused_mm("rg_gates_fwd", S, 4 * RG_W, RG_W, tm, 4 * RG_W, RG_W,
                     [(xc, "mk"), (bd, "kn")], [(0, 1, 0)],
                     [(xc, blk, im), (prm, (8, RG_W), _row0)], [(sds, blk, im)] * 4, epi)


SUBLANES = 8


def _scan_rows(a, b, reverse):
    rows = lax.broadcasted_iota(jnp.int32, a.shape, 0)
    s = 1
    while s < SUBLANES:
        shift = SUBLANES - s if reverse else s
        a_sh = pltpu.roll(a, shift, 0)
        b_sh = pltpu.roll(b, shift, 0)
        valid = (rows < SUBLANES - s) if reverse else (rows >= s)
        b = jnp.where(valid, a * b_sh + b, b)
        a = jnp.where(valid, a * a_sh, a)
        s *= 2
    return a, b


def _rg_scan(name, a_f, b_f, a_b, b_b):
    S, C = a_f.shape
    ts = _tile(S, 512)
    n_tiles = S // ts

    def body(af, bf, ab, bb, hf, hb, carry):
        @pl.when(pl.program_id(0) == 0)
        def _():
            carry[...] = jnp.zeros_like(carry)

        n_sub = ts // SUBLANES

        def step(j, c):
            cf, cb = c
            r0 = pl.multiple_of(j * SUBLANES, SUBLANES)
            cum_a, h0 = _scan_rows(af[pl.ds(r0, SUBLANES), :], bf[pl.ds(r0, SUBLANES), :], False)
            h = h0 + cum_a * cf
            hf[pl.ds(r0, SUBLANES), :] = h
            cf = h[SUBLANES - 1:SUBLANES, :]
            r1 = pl.multiple_of((n_sub - 1 - j) * SUBLANES, SUBLANES)
            cum_a, h0 = _scan_rows(ab[pl.ds(r1, SUBLANES), :], bb[pl.ds(r1, SUBLANES), :], True)
            h = h0 + cum_a * cb
            hb[pl.ds(r1, SUBLANES), :] = h
            cb = h[0:1, :]
            return cf, cb

        cf, cb = lax.fori_loop(0, n_sub, step, (carry[0:1, :], carry[1:2, :]), unroll=4)
        carry[0:1, :] = cf
        carry[1:2, :] = cb

    fw = lambda i: (i, 0)
    bw = lambda i: (n_tiles - 1 - i, 0)
    sds = jax.ShapeDtypeStruct((S, C), F32)
    return _rows(name, S, ts,
                 [(a_f, (ts, C), fw), (b_f, (ts, C), fw), (a_b, (ts, C), bw), (b_b, (ts, C), bw)],
                 [(sds, (ts, C), fw), (sds, (ts, C), bw)], body, scratch=[pltpu.VMEM((8, C), F32)])


def _tri_masks():
    ri = lax.broadcasted_iota(jnp.int32, (CHUNK, CHUNK), 0)
    ci = lax.broadcasted_iota(jnp.int32, (CHUNK, CHUNK), 1)
    return ri, ci


def _gdn_prep_fwd(p, prm):
    S = p.shape[0]
    ts = _tile(S, 512)

    def body(p_ref, prm_ref, o_ref):
        raw = p_ref[...].astype(F32)
        lane = lax.broadcasted_iota(jnp.int32, (1, 128), 1)
        g = -jnp.exp(prm_ref[0:1, :]) * _softplus(raw + prm_ref[1:2, :])
        g = jnp.where((lane >= 8) & (lane < 16), g, 0.0)
        beta = _sigmoid(raw)
        ri, ci = _tri_masks()
        lower = (ri >= ci).astype(F32)
        upper = (ri <= ci).astype(F32)
        for c in range(ts // CHUNK):
            rows = slice(c * CHUNK, (c + 1) * CHUNK)
            gch = g[rows]
            gc = jnp.where(lane < 12, _dot(lower, gch, 1, 0, HI), _dot(upper, gch, 1, 0, HI))
            o_ref[rows, :] = jnp.where(lane < 8, beta[rows], gc)

    return _rows("gdn_prep_fwd", S, ts,
                 [(p, (ts, 128), lambda i: (i, COL_BA // 128)), (prm, (8, 128), lambda i: (0, 0))],
                 [(jax.ShapeDtypeStruct((S, 128), F32), (ts, 128), lambda i: (i, 0))], body)[0]


def _bdot(a, b, ca, cb):
    return _dot(a.astype(BF16), b.astype(BF16), ca, cb)


GDN_W = GDN_H * GDN_DK
GDN_TS = 256
LOCAL_CHUNKS = 2

def _gdn_decay(bg_ref, gcr_ref, c, rows, r0, col, rev, ri, ci):
    beta = bg_ref[rows, col:col + 1]
    gc = bg_ref[rows, 8 + col:9 + col]
    last = 0 if rev else CHUNK - 1
    gl = bg_ref[pl.ds(r0 + last, 1), 8 + col:9 + col]
    out = dict(beta=beta, gc=gc, gl=gl, eg=jnp.exp(gc), egl=jnp.exp(gl - gc), cd=jnp.exp(gl))
    if gcr_ref is not None:
        incl = (ri <= ci) if rev else (ri >= ci)
        out["strict"] = (ri < ci) if rev else (ri > ci)
        out["dm"] = jnp.where(incl, jnp.exp(jnp.where(incl, gc - gcr_ref[c, col:col + 1, :], 0.0)), 0.0)
    return out


def _dir_tile(d, n_tiles, flip):
    if (d == 1) != flip:
        return lambda i: n_tiles - 1 - i
    return lambda i: i


def _gdn_local_fwd(q, k, v, bg, gcr):
    S = q.shape[0]
    ts = _tile(S, GDN_TS)
    ncb = ts // CHUNK
    nch = S // CHUNK

    def body(q_ref, k_ref, v_ref, bg_ref, gcr_ref, *out_refs):
        ri, ci = _tri_masks()
        eye = (ri == ci).astype(F32)
        outs = (out_refs[0:6], out_refs[6:12])
        cd_ref = out_refs[12]

        def chunk(cc, carry):
            chains = []
            for c in (LOCAL_CHUNKS * cc + j for j in range(LOCAL_CHUNKS)):
                r0 = pl.multiple_of(c * CHUNK, CHUNK)
                rows = pl.ds(r0, CHUNK)
                for h in range(GDN_H):
                    cols = slice(h * GDN_DK, (h + 1) * GDN_DK)
                    qh, kh, vh = q_ref[rows, cols], k_ref[rows, cols], v_ref[rows, cols]
                    both = _bdot(jnp.concatenate([qh, kh], axis=0), kh, 1, 1)
                    for d in range(2):
                        chains.append(dict(c=c, r0=r0, rows=rows, h=h, d=d, cols=cols, qh=qh, kh=kh, vh=vh,
                                           qk=both[0:CHUNK], kk=both[CHUNK:2 * CHUNK]))
            for ch in chains:
                m = _gdn_decay(bg_ref, gcr_ref, ch["c"], ch["rows"], ch["r0"], ch["d"] * GDN_H + ch["h"], ch["d"] == 1,
                               ri, ci)
                ch["m"] = m
                ch["x"] = -jnp.where(m["strict"], m["beta"] * ch["kk"] * m["dm"], 0.0)
                ch["t"] = eye + ch["x"]
            for ch in chains:
                ch["pw"] = _bdot(ch["x"], ch["x"], 1, 0)
            for level in range(1, 6):
                last_level = level == 5
                for ch in chains:
                    rhs = ch["t"] if last_level else jnp.concatenate([ch["t"], ch["pw"]], axis=1)
                    ch["prod"] = _bdot(ch["pw"], rhs, 1, 0)
                for ch in chains:
                    ch["t"] = ch["t"] + ch["prod"][:, 0:CHUNK]
                    if not last_level:
                        ch["pw"] = ch["prod"][:, CHUNK:2 * CHUNK]
            for ch in chains:
                m = ch["m"]
                rhs = jnp.concatenate([ch["vh"] * m["beta"], ch["kh"] * (m["beta"] * m["eg"])], axis=1)
                ch["uw"] = _bdot(ch["t"], rhs, 1, 0)
            for ch in chains:
                u_ref, w_ref, a_ref, t_ref, qd_ref, kd_ref = outs[ch["d"]]
                m = ch["m"]
                c, rows = ch["c"], ch["rows"]
                col = ch["d"] * GDN_H + ch["h"]
                u_ref[rows, ch["cols"]] = ch["uw"][:, 0:GDN_DK]
                w_ref[rows, ch["cols"]] = ch["uw"][:, GDN_DK:2 * GDN_DK].astype(BF16)
                a_ref[c, ch["h"]] = (ch["qk"] * m["dm"]).astype(BF16)
                t_ref[c, ch["h"]] = _bdot(ch["t"], eye, 0, 0).astype(BF16)
                qd_ref[rows, ch["cols"]] = (ch["qh"] * m["eg"]).astype(BF16)
                kd_ref[rows, ch["cols"]] = (ch["kh"] * m["egl"]).astype(BF16)
                cd_ref[c, col:col + 1, :] = jnp.broadcast_to(m["cd"], (1, 128))
            return carry

        lax.fori_loop(0, ncb // LOCAL_CHUNKS, chunk, 0)

    im = lambda i: (i, 0)
    im4 = lambda i: (i, 0, 0, 0)
    ins = [(q, (ts, GDN_W), im), (k, (ts, GDN_W), im), (v, (ts, GDN_W), im), (bg, (ts, 128), im),
           (gcr, (ncb, 8, CHUNK), lambda i: (i, 0, 0))]
    per_dir = [(jax.ShapeDtypeStruct((S, GDN_W), F32), (ts, GDN_W), im),
               (jax.ShapeDtypeStruct((S, GDN_W), BF16), (ts, GDN_W), im),
               (jax.ShapeDtypeStruct((nch, GDN_H, CHUNK, CHUNK), BF16), (ncb, GDN_H, CHUNK, CHUNK), im4),
               (jax.ShapeDtypeStruct((nch, GDN_H, CHUNK, CHUNK), BF16), (ncb, GDN_H, CHUNK, CHUNK), im4),
               (jax.ShapeDtypeStruct((S, GDN_W), BF16), (ts, GDN_W), im),
               (jax.ShapeDtypeStruct((S, GDN_W), BF16), (ts, GDN_W), im)]
    cd_out = (jax.ShapeDtypeStruct((nch, 8, 128), F32), (ncb, 8, 128), lambda i: (i, 0, 0))
    res = _rows("gdn_local_fwd", S, ts, ins, per_dir * 2 + [cd_out], body)
    return res[0:6], res[6:12], res[12]


def _gdn_scan_fwd(loc):
    S = loc[0][0].shape[0]
    ts = _tile(S, GDN_TS)
    n_tiles = S // ts
    ncb = ts // CHUNK
    nch = S // CHUNK

    def body(*refs):
        ins = (refs[0:6], refs[6:12])
        outs = (refs[12:15], refs[15:18])
        state = refs[18]

        @pl.when(pl.program_id(0) == 0)
        def _():
            state[...] = jnp.zeros_like(state)

        def chunk(cc, carry):
            chains = []
            for d in range(2):
                c = cc if d == 0 else ncb - 1 - cc
                rows = pl.ds(pl.multiple_of(c * CHUNK, CHUNK), CHUNK)
                for h in range(GDN_H):
                    cols = slice(h * GDN_DK, (h + 1) * GDN_DK)
                    chains.append(dict(d=d, h=h, c=c, rows=rows, cols=cols, st=state[d * GDN_H + h]))
            for ch in chains:
                qd_ref, kd_ref, u_ref, w_ref, a_ref, cd_ref = ins[ch["d"]]
                rows, cols = ch["rows"], ch["cols"]
                lhs = jnp.concatenate([w_ref[rows, cols], qd_ref[rows, cols]], axis=0)
                ch["ws_qs"] = _dot(lhs, ch["st"].astype(BF16), 1, 0)
            for ch in chains:
                qd_ref, kd_ref, u_ref, w_ref, a_ref, cd_ref = ins[ch["d"]]
                rows, cols = ch["rows"], ch["cols"]
                vn = u_ref[rows, cols] - ch["ws_qs"][0:CHUNK]
                vnb = vn.astype(BF16)
                ch["vn"] = vn
                ch["avn"] = _dot(a_ref[ch["c"], ch["h"]], vnb, 1, 0)
                ch["kvn"] = _dot(kd_ref[rows, cols], vnb, 0, 0)
            for ch in chains:
                o_ref, vn_ref, s_ref = outs[ch["d"]]
                cd_ref = ins[ch["d"]][5]
                rows, cols = ch["rows"], ch["cols"]
                col = ch["d"] * GDN_H + ch["h"]
                o_ref[rows, cols] = ch["ws_qs"][CHUNK:2 * CHUNK] + ch["avn"]
                vn_ref[rows, cols] = ch["vn"].astype(BF16)
                s_ref[ch["c"], ch["h"]] = ch["st"].astype(BF16)
                state[ch["d"] * GDN_H + ch["h"]] = ch["st"] * cd_ref[ch["c"], col:col + 1, :] + ch["kvn"]
            return carry

        lax.fori_loop(0, ncb, chunk, 0)

    ins, outs = [], []
    for d in range(2):
        tix = _dir_tile(d, n_tiles, False)
        im = lambda i, tix=tix: (tix(i), 0)
        im4 = lambda i, tix=tix: (tix(i), 0, 0, 0)
        u, w, a, _, qd, kd = loc[d]
        ins += [(qd, (ts, GDN_W), im), (kd, (ts, GDN_W), im), (u, (ts, GDN_W), im), (w, (ts, GDN_W), im),
                (a, (ncb, GDN_H, CHUNK, CHUNK), im4), (loc[2], (ncb, 8, 128), lambda i, tix=tix: (tix(i), 0, 0))]
        outs += [(jax.ShapeDtypeStruct((S, GDN_W), F32), (ts, GDN_W), im),
                 (jax.ShapeDtypeStruct((S, GDN_W), BF16), (ts, GDN_W), im),
                 (jax.ShapeDtypeStruct((nch, GDN_H, GDN_DK, GDN_DK), BF16), (ncb, GDN_H, GDN_DK, GDN_DK), im4)]
    res = _rows("gdn_scan_fwd", S, ts, ins, outs, body, scratch=[pltpu.VMEM((2 * GDN_H, GDN_DK, GDN_DK), F32)])
    return res[0:3], res[3:6]


def _gelu(x):
    c = math.sqrt(2.0 / math.pi)
    t = jnp.tanh(c * (x + 0.044715 * x * x * x))
    return 0.5 * x * (1.0 + t), t


def _mix_out_fwd(h_f, h_b, o_f, o_b, p, gn):
    S = h_f.shape[0]
    ts = _tile(S, 512)

    def body(hf, hb, of, ob, gate, z, gn_ref, y_ref):
        ge, _ = _gelu(gate[...].astype(F32))
        y_ref[:, 0:RG_W] = ((hf[...] + hb[...]) * ge).astype(BF16)
        o = of[...] + ob[...]
        zv = z[...].astype(F32)
        sz = zv * _sigmoid(zv)
        for h in range(GDN_H):
            cols = slice(h * GDN_DK, (h + 1) * GDN_DK)
            oh = o[:, cols]
            n = oh * lax.rsqrt(jnp.mean(oh * oh, axis=-1, keepdims=True) + EPS) * gn_ref[...]
            y_ref[:, RG_W + h * GDN_DK:RG_W + (h + 1) * GDN_DK] = (n * sz[:, cols]).astype(BF16)

    blk = (ts, RG_W)
    im = lambda i: (i, 0)
    ins = [(h_f, blk, im), (h_b, blk, im), (o_f, blk, im), (o_b, blk, im),
           (p, blk, lambda i: (i, 1)), (p, blk, lambda i: (i, 5)), (gn, (1, GDN_DK), lambda i: (0, 0))]
    return _rows("mix_out_fwd", S, ts, ins,
                 [(jax.ShapeDtypeStruct((S, D_MODEL), BF16), (ts, D_MODEL), im)], body)[0]


def _block_diag(w):
    n = w.shape[0]
    return jnp.einsum("nij,nm->nimj", w, jnp.eye(n, dtype=w.dtype)).reshape(n * w.shape[1], n * w.shape[2])


def _rg_bd(a_w, x_w):
    return jnp.concatenate([_block_diag(a_w[0]), _block_diag(x_w[0]), _block_diag(a_w[1]), _block_diag(x_w[1])],
                           axis=1).astype(BF16)


def _rg_prm(ba, bx, lam):
    return jnp.concatenate([ba[0:1], bx[0:1], ba[1:2], bx[1:2], lam, jnp.zeros((2, RG_W), F32)], axis=0)


def _gdn_prm(a_log, dt_bias):
    rows = jnp.zeros((8, 128), F32)
    rows = rows.at[0, 8:16].set(a_log.reshape(-1))
    return rows.at[1, 8:16].set(dt_bias.reshape(-1))


def _gc_rows(bg):
    S = bg.shape[0]
    return bg[:, 8:16].reshape(S // CHUNK, CHUNK, 8).transpose(0, 2, 1)


def _layer_fwd(x0, target, W, more=None):
    S = x0.shape[0]
    R = {}
    R["h1"] = _rmsnorm_fwd("rms1", x0, W["ffn1_norm"])
    if more is not None:
        W = {**W, **more("ffn1_up", R["h1"])}
    late_wd = {}

    def ffn1_wd(after):
        late_wd.update(more("ffn1_down", after))
        return late_wd["ffn1_w_down"]

    sd_x = jax.ShapeDtypeStruct((S, D_MODEL), F32)
    sd_h = jax.ShapeDtypeStruct((S, D_MODEL), BF16)

    def norm_after(gain):
        extras = lambda t: [(gain, (1, D_MODEL), _row0)]
        outs = lambda t: [(sd_x, (t, D_MODEL), _mn), (sd_h, (t, D_MODEL), _mn)]

        def finish(i, xo, ex, out):
            out[0][...] = xo
            out[1][...] = _rmsnorm_tile(xo, ex[0][...])

        return extras, outs, finish

    (R["x1"], R["h2"]), R["a1"], R["b1"], R["f1"] = _ffn_fwd(
        "ffn1", x0, R["h1"], W["ffn1_w_gate"], W["ffn1_w_up"], ffn1_wd if more is not None else W["ffn1_w_down"],
        *norm_after(W["mix_norm"]))
    if more is not None:
        W = {**W, **late_wd, **more("mixer", R["x1"])}
    tm = _tile(S, 512)
    tmp = _tile(S, 1024)
    tmp = _tile(S, 512)
    R["p"] = _fused_mm("in_proj", S, D_IN_PAD, D_MODEL, tmp, D_IN_PAD, D_MODEL, [(R["h2"], "mk"), (W["w_in"], "nk")],
                       [(0, 1, 0)], [], [(jax.ShapeDtypeStruct((S, D_IN_PAD), BF16), (tmp, D_IN_PAD), _mn)],
                       lambda i, accs, ex, out: out[0].__setitem__(Ellipsis, accs[0][...].astype(BF16)))[0]
    p = R["p"]
    R["xc"] = _conv_fwd("rg_conv_fwd", p, 0, W["rg_conv_w"], W["rg_conv_b"], "bias")
    R["bd"] = _rg_bd(W["rg_gate_a_w"], W["rg_gate_x_w"])
    R["rg_prm"] = _rg_prm(W["rg_gate_a_b"], W["rg_gate_x_b"], W["rg_lambda"])
    a_f, b_f, a_b, b_b = _rg_gates_fwd(R["xc"], R["bd"], R["rg_prm"])
    R["a_f"], R["a_b"] = a_f, a_b
    R["h_f"], R["h_b"] = _rg_scan("rg_scan_fwd", a_f, b_f, a_b, b_b)
    zero_b = jnp.zeros((1, RG_W), F32)
    cw = W["gdn_conv_w"]
    R["q"] = _conv_fwd("gdn_conv_q", p, 2, cw[:, 0:512], zero_b, "q")
    R["k"] = _conv_fwd("gdn_conv_k", p, 3, cw[:, 512:1024], zero_b, "k")
    R["v"] = _conv_fwd("gdn_conv_v", p, 4, cw[:, 1024:1536], zero_b, "v")
    R["gdn_prm"] = _gdn_prm(W["gdn_a_log"], W["gdn_dt_bias"])
    R["bg"] = _gdn_prep_fwd(p, R["gdn_prm"])
    R["gcr"] = _gc_rows(R["bg"])
    R["gdn_loc"] = _gdn_local_fwd(R["q"], R["k"], R["v"], R["bg"], R["gcr"])
    R["gdn_fwd"] = _gdn_scan_fwd(R["gdn_loc"])
    R["o_f"], R["o_b"] = R["gdn_fwd"][0][0], R["gdn_fwd"][1][0]
    R["y"] = _mix_out_fwd(R["h_f"], R["h_b"], R["o_f"], R["o_b"], p, W["gdn_norm"])
    def epi_out(i, accs, ex, out):
        x2 = ex[0][...] + accs[0][...]
        out[0][...] = x2
        out[1][...] = _rmsnorm_tile(x2, ex[1][...])

    R["x2"], R["h3"] = _fused_mm("out_proj", S, D_MODEL, D_MODEL, tm, D_MODEL, D_MODEL,
                                 [(R["y"], "mk"), (W["w_out"], "kn")], [(0, 1, 0)],
                                 [(R["x1"], (tm, D_MODEL), _mn), (W["ffn2_norm"], (1, D_MODEL), _row0)],
                                 [(sd_x, (tm, D_MODEL), _mn), (sd_h, (tm, D_MODEL), _mn)], epi_out)
    if more is not None:
        W = {**W, **more("ffn2", R["x2"])}

    def loss_finish(i, xo, ex, out):
        gv = ex[1][...]
        r = lax.rsqrt(jnp.mean(xo * xo, axis=-1, keepdims=True) + EPS)
        err = xo * r * gv - ex[0][...]
        dx, dgt = _rmsnorm_bwd_tile(err * (1.0 / D_MODEL), xo, gv)
        out[0][...] = dx
        _colsum_into(out[1], i, jnp.zeros((8, 128), F32) + jnp.sum(err * err) * (0.5 / D_MODEL))
        _colsum_into(out[2], i, jnp.sum(dgt, axis=0, keepdims=True))

    (R["dx3"], R["loss"], R["d_final_norm"]), R["a2"], R["b2"], R["f2"] = _ffn_fwd(
        "ffn2", R["x2"], R["h3"], W["ffn2_w_gate"], W["ffn2_w_up"], W["ffn2_w_down"],
        lambda t: [(target, (t, D_MODEL), _mn), (W["final_norm"], (1, D_MODEL), _row0)],
        lambda t: [(sd_x, (t, D_MODEL), _mn), (jax.ShapeDtypeStruct((8, 128), F32), (8, 128), _row0),
                   (jax.ShapeDtypeStruct((1, D_MODEL), F32), (1, D_MODEL), _row0)],
        loss_finish)
    R["W"] = W
    return R


def _colsum_into(ref, i, val):
    @pl.when(i == 0)
    def _():
        ref[...] = val

    @pl.when(i > 0)
    def _():
        ref[...] += val


def _ffn_bwd(tag, dout, x, g, h, a, b, f, wg, wu, wd, emit):
    S = x.shape[0]
    tm = _tile(S, 512)
    tk_s = _tile(S, 1024)
    dwd = _fused_mm(f"{tag}_dw_down", D_FF, D_MODEL, S, 1408, D_MODEL, tk_s, [(f, "km"), (dout, "kn")], [(0, 1, 0)], [],
                    [(jax.ShapeDtypeStruct((D_FF, D_MODEL), BF16), (1408, D_MODEL), _mn)],
                    lambda i, accs, ex, out: out[0].__setitem__(Ellipsis, (0.5 * accs[0][...]).astype(BF16)))[0]
    emit(down=dwd)

    def epi_act(i, accs, ex, out):
        df = 0.5 * accs[0][...]
        out[0][...] = (df * ex[1][...].astype(F32)).astype(BF16)
        out[1][...] = (df * ex[0][...].astype(F32)).astype(BF16)

    sds = jax.ShapeDtypeStruct((S, D_FF), BF16)
    da, db = _fused_mm(f"{tag}_dact", S, D_FF, D_MODEL, tm, 1408, D_MODEL, [(dout, "mk"), (wd, "nk")], [(0, 1, 0)],
                       [(a, (tm, 1408), _mn), (b, (tm, 1408), _mn)], [(sds, (tm, 1408), _mn)] * 2, epi_act)

    def epi_w2(i, accs, ex, out):
        out[0][...] = accs[0][...].astype(BF16)
        out[1][...] = accs[1][...].astype(BF16)

    sdw = jax.ShapeDtypeStruct((D_MODEL, D_FF), BF16)
    dwg, dwu = _fused_mm(f"{tag}_dw_up", D_MODEL, D_FF, S, D_MODEL, 1408, tk_s,
                         [(h, "km"), (da, "kn"), (db, "kn")], [(0, 1, 0), (0, 2, 1)], [],
                         [(sdw, (D_MODEL, 1408), _mn)] * 2, epi_w2)
    tok = emit(gate=dwg, up=dwu)
    if tok is not None:
        g = g + tok

    def epi_dx(i, accs, ex, out):
        dx, dgt = _rmsnorm_bwd_tile(accs[0][...], ex[0][...], ex[1][...])
        out[0][...] = ex[2][...] + dx
        _colsum_into(out[1], i, jnp.sum(dgt, axis=0, keepdims=True))

    tmx = _tile(S, 1024)
    dx, dg = _fused_mm(f"{tag}_dx", S, D_MODEL, D_FF, tmx, D_MODEL, 1408,
                       [(da, "mk"), (wg, "kn"), (db, "mk"), (wu, "kn")], [(0, 1, 0), (2, 3, 0)],
                       [(x, (tmx, D_MODEL), _mn), (g, (1, D_MODEL), _row0), (dout, (tmx, D_MODEL), _mn)],
                       [(jax.ShapeDtypeStruct((S, D_MODEL), F32), (tmx, D_MODEL), _mn),
                        (jax.ShapeDtypeStruct((1, D_MODEL), F32), (1, D_MODEL), _row0)], epi_dx)
    return dx, dg


def _mix_out_bwd(dx2, w_out, h_f, h_b, o_f, o_b, p, gn):
    S = dx2.shape[0]
    ts = _tile(S, 512)
    c0 = math.sqrt(2.0 / math.pi)

    def epi(i, accs, ex, out):
        hf, hb, of, ob, gate, z, gn_ref = ex
        dhr_ref, dgate_ref, do_ref, dz_ref, dgn_ref = out
        dy_ref = accs[0]
        gv = gate[...].astype(F32)
        ge, t = _gelu(gv)
        dy_rg = dy_ref[:, 0:RG_W]
        dhr_ref[...] = dy_rg * ge
        dgelu = 0.5 * (1.0 + t) + 0.5 * gv * (1.0 - t * t) * c0 * (1.0 + 3.0 * 0.044715 * gv * gv)
        dgate_ref[...] = (dy_rg * (hf[...] + hb[...]) * dgelu).astype(BF16)
        o = of[...] + ob[...]
        zv = z[...].astype(F32)
        sig = _sigmoid(zv)
        gnv = gn_ref[...]
        dgn = jnp.zeros((1, GDN_DK), F32)
        for h in range(GDN_H):
            cols = slice(h * GDN_DK, (h + 1) * GDN_DK)
            oh = o[:, cols]
            r = lax.rsqrt(jnp.mean(oh * oh, axis=-1, keepdims=True) + EPS)
            ohat = oh * r
            dyh = dy_ref[:, RG_W + h * GDN_DK:RG_W + (h + 1) * GDN_DK]
            zh = zv[:, cols]
            sh = sig[:, cols]
            dn = dyh * zh * sh
            dz_ref[:, cols] = (dyh * ohat * gnv * (sh * (1.0 + zh * (1.0 - sh)))).astype(BF16)
            dxn = dn * gnv
            do_ref[:, cols] = r * (dxn - ohat * jnp.mean(dxn * ohat, axis=-1, keepdims=True))
            dgn = dgn + jnp.sum(dn * ohat, axis=0, keepdims=True)
        _colsum_into(dgn_ref, i, dgn)

    blk = (ts, RG_W)
    im = lambda i, j: (i, 0)
    extras = [(h_f, blk, im), (h_b, blk, im), (o_f, blk, im), (o_b, blk, im),
              (p, blk, lambda i, j: (i, 1)), (p, blk, lambda i, j: (i, 5)), (gn, (1, GDN_DK), _row0)]
    outs = [(jax.ShapeDtypeStruct((S, RG_W), F32), blk, im), (jax.ShapeDtypeStruct((S, RG_W), BF16), blk, im),
            (jax.ShapeDtypeStruct((S, RG_W), F32), blk, im), (jax.ShapeDtypeStruct((S, RG_W), BF16), blk, im),
            (jax.ShapeDtypeStruct((1, GDN_DK), F32), (1, GDN_DK), _row0)]
    return _fused_mm("mix_out_bwd", S, D_MODEL, D_MODEL, ts, D_MODEL, D_MODEL, [(dx2, "mk"), (w_out, "nk")], [(0, 1, 0)],
                     extras, outs, epi)


def _rg_scan_adj(name, a_up, b_up, a_dn, b_dn):
    S, C = a_up.shape
    ts = _tile(S, 512)
    n_tiles = S // ts

    def body(au, bu, ad, bd, mu_ref, lam_ref, carry):
        @pl.when(pl.program_id(0) == 0)
        def _():
            carry[...] = jnp.zeros_like(carry)

        n_sub = ts // SUBLANES
        rows = lax.broadcasted_iota(jnp.int32, (SUBLANES, C), 0)

        def half(a_ref, b_ref, out_ref, r0, c_in, reverse):
            a = a_ref[pl.ds(r0, SUBLANES), :]
            b = b_ref[pl.ds(r0, SUBLANES), :]
            cum_a, c0 = _scan_rows(a, a * b, reverse)
            c = c0 + cum_a * c_in
            edge = 0 if not reverse else SUBLANES - 1
            c_prev = jnp.where(rows == edge, c_in, pltpu.roll(c, SUBLANES - 1 if reverse else 1, 0))
            out_ref[pl.ds(r0, SUBLANES), :] = b + c_prev
            return c[0:1, :] if reverse else c[SUBLANES - 1:SUBLANES, :]

        def step(j, c):
            cu, cd = c
            cu = half(au, bu, mu_ref, pl.multiple_of(j * SUBLANES, SUBLANES), cu, False)
            cd = half(ad, bd, lam_ref, pl.multiple_of((n_sub - 1 - j) * SUBLANES, SUBLANES), cd, True)
            return cu, cd

        cu, cd = lax.fori_loop(0, n_sub, step, (carry[0:1, :], carry[1:2, :]), unroll=4)
        carry[0:1, :] = cu
        carry[1:2, :] = cd

    fw = lambda i: (i, 0)
    bw = lambda i: (n_tiles - 1 - i, 0)
    sds = jax.ShapeDtypeStruct((S, C), F32)
    return _rows(name, S, ts,
                 [(a_up, (ts, C), fw), (b_up, (ts, C), fw), (a_dn, (ts, C), bw), (b_dn, (ts, C), bw)],
                 [(sds, (ts, C), fw), (sds, (ts, C), bw)], body, scratch=[pltpu.VMEM((8, C), F32)])


def _halo_ex(arr, S, tm, width):
    per = tm // HALO
    last = S // HALO - 1
    return [
        (arr, (tm, width), lambda i, j: (i, 0)),
        (arr, (HALO, width), lambda i, j: (jnp.maximum(i * per - 1, 0), 0)),
        (arr, (HALO, width), lambda i, j: (jnp.minimum((i + 1) * per, last), 0)),
    ]


def _rg_gates_bwd(xc, bd, prm, lam_f, lam_b, h_f, h_b):
    S = xc.shape[0]
    tm = _tile(S, 256)
    n_tiles = S // tm

    def epi(i, accs, ex, out):
        pre = accs[0][...]
        xv = ex[0][...]
        prm_ref = ex[1]
        lams = (ex[2][...], ex[3][...])
        hprev = (_shift(_ext(ex[4], ex[5], ex[6], i, n_tiles), -1, tm),
                 _shift(_ext(ex[7], ex[8], ex[9], i, n_tiles), 1, tm))
        dxc = jnp.zeros_like(xv)
        rows = []
        dlam_rows = []
        for d in range(2):
            r, ig, sp, a, sq = _rg_gate_terms(pre, xv, prm_ref, d)
            lam = lams[d]
            da = lam * hprev[d]
            di = lam * sq * xv
            dxc = dxc + lam * sq * ig
            dsq = lam * ig * xv
            dlog_a = da * a - dsq * (a * a) / sq
            dpre_r = dlog_a * (-RG_C * sp) * r * (1.0 - r)
            dpre_i = di * ig * (1.0 - ig)
            out[0][:, d * 1024:d * 1024 + RG_W] = dpre_r.astype(BF16)
            out[0][:, d * 1024 + RG_W:(d + 1) * 1024] = dpre_i.astype(BF16)
            rows += [jnp.sum(dpre_r, axis=0, keepdims=True), jnp.sum(dpre_i, axis=0, keepdims=True)]
            dsp = jnp.sum(dlog_a * (-RG_C * r), axis=0, keepdims=True)
            dlam_rows.append(-dsp * _sigmoid(-prm_ref[4 + d:5 + d, :]))
        out[1][...] = dxc + _dot(out[0][...], ex[10][...], 1, 1)
        zero = jnp.zeros((2, RG_W), F32)
        _colsum_into(out[2], i, jnp.concatenate(rows + dlam_rows + [zero], axis=0))

    blk = (tm, RG_W)
    im = lambda i, j: (i, 0)
    extras = ([(xc, blk, im), (prm, (8, RG_W), _row0), (lam_f, blk, im), (lam_b, blk, im)]
              + _halo_ex(h_f, S, tm, RG_W) + _halo_ex(h_b, S, tm, RG_W) + [(bd, (RG_W, 4 * RG_W), _row0)])
    outs = [(jax.ShapeDtypeStruct((S, 4 * RG_W), BF16), (tm, 4 * RG_W), im),
            (jax.ShapeDtypeStruct((S, RG_W), F32), blk, im),
            (jax.ShapeDtypeStruct((8, RG_W), F32), (8, RG_W), _row0)]
    return _fused_mm("rg_gates_bwd", S, 4 * RG_W, RG_W, tm, 4 * RG_W, RG_W, [(xc, "mk"), (bd, "kn")], [(0, 1, 0)],
                     extras, outs, epi)


def _roll_rows(ext, off):
    if off == 0:
        return ext
    return pltpu.roll(ext, (-off) % ext.shape[0], 0)


def _conv_bwd(name, p, colblk, w, grads, mode):
    S = p.shape[0]
    ts = _tile(S, 512)
    n_tiles = S // ts
    C = w.shape[1]
    ng = len(grads)

    def body(*refs):
        p_refs = refs[0:3]
        g_refs = refs[3:3 + 3 * ng]
        w_ref = refs[3 + 3 * ng]
        dx_ref, dw_ref, db_ref = refs[4 + 3 * ng:]
        i = pl.program_id(0)
        ext_p = _ext(*p_refs, i, n_tiles)
        dn = _ext(*g_refs[0:3], i, n_tiles)
        for gi in range(1, ng):
            dn = dn + _ext(*g_refs[3 * gi:3 * gi + 3], i, n_tiles)
        if mode == "bias":
            dc = dn
        else:
            c = None
            for j in range(CONV_W):
                term = w_ref[j:j + 1, :] * _roll_rows(ext_p, j - 2)
                c = term if c is None else c + term
            sig = _sigmoid(c)
            s = c * sig
            if mode in ("q", "k"):
                scale = GDN_DK ** -0.5 if mode == "q" else 1.0
                parts = []
                for h in range(GDN_H):
                    cols = slice(h * GDN_DK, (h + 1) * GDN_DK)
                    sh = s[:, cols]
                    dnh = dn[:, cols]
                    rinv = lax.rsqrt(jnp.sum(sh * sh, axis=-1, keepdims=True) + EPS)
                    parts.append(scale * rinv * (dnh - sh * (rinv * rinv) * jnp.sum(dnh * sh, axis=-1, keepdims=True)))
                ds = jnp.concatenate(parts, axis=-1)
            else:
                ds = dn
            dc = ds * (sig * (1.0 + c * (1.0 - sig)))
        dx = None
        for j in range(CONV_W):
            term = w_ref[j:j + 1, :] * _shift(dc, 2 - j, ts)
            dx = term if dx is None else dx + term
        dx_ref[...] = dx.astype(BF16)
        dc_main = dc[HALO:HALO + ts]
        dw = jnp.concatenate([jnp.sum(dc_main * _shift(ext_p, j - 2, ts), axis=0, keepdims=True)
                              for j in range(CONV_W)], axis=0)
        _colsum_into(dw_ref, i, dw)
        _colsum_into(db_ref, i, jnp.sum(dc_main, axis=0, keepdims=True))

    ins = _halo_ins(p, S, ts, C, colblk)
    for garr in grads:
        ins += _halo_ins(garr, S, ts, C, 0)
    ins += [(w, (CONV_W, C), lambda i: (0, 0))]
    z0 = lambda i: (0, 0)
    outs = [(jax.ShapeDtypeStruct((S, C), BF16), (ts, C), lambda i: (i, 0)),
            (jax.ShapeDtypeStruct((CONV_W, C), F32), (CONV_W, C), z0),
            (jax.ShapeDtypeStruct((1, C), F32), (1, C), z0)]
    return _rows(name, S, ts, ins, outs, body)


def _gdn_scan_bwd(loc, do):
    S = do.shape[0]
    ts = _tile(S, GDN_TS)
    n_tiles = S // ts
    ncb = ts // CHUNK
    nch = S // CHUNK

    def body(*refs):
        ins = (refs[0:6], refs[6:12])
        outs = (refs[12:14], refs[14:16])
        dstate = refs[16]

        @pl.when(pl.program_id(0) == 0)
        def _():
            dstate[...] = jnp.zeros_like(dstate)

        def chunk(cc, carry):
            chains = []
            for d in range(2):
                c = ncb - 1 - cc if d == 0 else cc
                rows = pl.ds(pl.multiple_of(c * CHUNK, CHUNK), CHUNK)
                for h in range(GDN_H):
                    cols = slice(h * GDN_DK, (h + 1) * GDN_DK)
                    chains.append(dict(d=d, h=h, c=c, rows=rows, cols=cols, dsn=dstate[d * GDN_H + h]))
            for ch in chains:
                qd_ref, kd_ref, cd_ref, w_ref, a_ref, do_ref = ins[ch["d"]]
                rows, cols = ch["rows"], ch["cols"]
                dob = do_ref[rows, cols].astype(BF16)
                ch["dvn"] = (_dot(a_ref[ch["c"], ch["h"]], dob, 0, 0)
                             + _dot(kd_ref[rows, cols], ch["dsn"].astype(BF16), 1, 0))
                ch["qdo"] = _dot(qd_ref[rows, cols], dob, 0, 0)
            for ch in chains:
                w_ref = ins[ch["d"]][3]
                ch["wdvn"] = _dot(w_ref[ch["rows"], ch["cols"]], ch["dvn"].astype(BF16), 0, 0)
            for ch in chains:
                dvn_ref, ds_ref = outs[ch["d"]]
                cd_ref = ins[ch["d"]][2]
                col = ch["d"] * GDN_H + ch["h"]
                dvn_ref[ch["rows"], ch["cols"]] = ch["dvn"].astype(BF16)
                ds_ref[ch["c"], ch["h"]] = ch["dsn"].astype(BF16)
                dstate[ch["d"] * GDN_H + ch["h"]] = (ch["qdo"] + cd_ref[ch["c"], col:col + 1, :] * ch["dsn"]
                                                     - ch["wdvn"])
            return carry

        lax.fori_loop(0, ncb, chunk, 0)

    ins, outs = [], []
    for d in range(2):
        tix = _dir_tile(d, n_tiles, True)
        im = lambda i, tix=tix: (tix(i), 0)
        im4 = lambda i, tix=tix: (tix(i), 0, 0, 0)
        _, w, a, _, qd, kd = loc[d]
        ins += [(qd, (ts, GDN_W), im), (kd, (ts, GDN_W), im), (loc[2], (ncb, 8, 128), lambda i, tix=tix: (tix(i), 0, 0)),
                (w, (ts, GDN_W), im), (a, (ncb, GDN_H, CHUNK, CHUNK), im4), (do, (ts, GDN_W), im)]
        outs += [(jax.ShapeDtypeStruct((S, GDN_W), BF16), (ts, GDN_W), im),
                 (jax.ShapeDtypeStruct((nch, GDN_H, GDN_DK, GDN_DK), BF16), (ncb, GDN_H, GDN_DK, GDN_DK), im4)]
    res = _rows("gdn_scan_bwd", S, ts, ins, outs, body, scratch=[pltpu.VMEM((2 * GDN_H, GDN_DK, GDN_DK), F32)])
    return res[0:2], res[2:4]


def _gdn_local_bwd(q, k, v, bg, gcr, do, loc, fwd, adj):
    S = q.shape[0]
    ts = _tile(S, GDN_TS)
    ncb = ts // CHUNK

    def body(q_ref, k_ref, v_ref, bg_ref, gcr_ref, do_ref, *rest):
        per_dir = (rest[0:5], rest[5:10])
        dq_ref, dk_ref, dv_ref, dbg_ref, dbgr_ref = rest[10:15]
        ri, ci = _tri_masks()
        lane = lax.broadcasted_iota(jnp.int32, (CHUNK, 128), 1)
        rowi = lax.broadcasted_iota(jnp.int32, (CHUNK, 1), 0)
        ones8 = jnp.ones((SUBLANES, CHUNK), F32)

        def chunk(c, carry):
            r0 = pl.multiple_of(c * CHUNK, CHUNK)
            rows = pl.ds(r0, CHUNK)
            chains = []
            for h in range(GDN_H):
                cols = slice(h * GDN_DK, (h + 1) * GDN_DK)
                qh, kh, vh = q_ref[rows, cols], k_ref[rows, cols], v_ref[rows, cols]
                dob = do_ref[rows, cols].astype(BF16)
                both = _bdot(jnp.concatenate([qh, kh], axis=0), kh, 1, 1)
                for d in range(2):
                    chains.append(dict(h=h, d=d, cols=cols, qh=qh, kh=kh, vh=vh, dob=dob, qk=both[0:CHUNK],
                                       kk=both[CHUNK:2 * CHUNK], col=d * GDN_H + h))
            for ch in chains:
                m = _gdn_decay(bg_ref, gcr_ref, c, rows, r0, ch["col"], ch["d"] == 1, ri, ci)
                t_ref, s_ref, ds_ref, vn_ref, dvn_ref = per_dir[ch["d"]]
                h, cols = ch["h"], ch["cols"]
                ch["m"] = m
                ch["kb"] = ch["kh"] * m["beta"]
                ch["kbg"] = ch["kb"] * m["eg"]
                ch["t"] = t_ref[c, h]
                stb = s_ref[c, h]
                ch["dsn"] = ds_ref[c, h]
                vnb = vn_ref[rows, cols]
                dvnb = dvn_ref[rows, cols]
                ch["dcd"] = jnp.sum(jnp.sum(stb.astype(F32) * ch["dsn"].astype(F32), axis=1, keepdims=True),
                                    axis=0, keepdims=True)
                ch["dqd"] = _dot(ch["dob"], stb, 1, 1)
                ch["d_a"] = _dot(ch["dob"], vnb, 1, 1)
                ch["dkd"] = _bdot(vnb, ch["dsn"], 1, 1)
                ch["dw"] = -_dot(dvnb, stb, 1, 1)
                ch["dvb"] = _dot(ch["t"], dvnb, 1, 0)
                ch["d_t"] = _bdot(dvnb, ch["vh"] * m["beta"], 1, 1)
            for ch in chains:
                dwb = ch["dw"].astype(BF16)
                ch["d_t"] = ch["d_t"] + _bdot(dwb, ch["kbg"], 1, 1)
                ch["dkbg"] = _dot(ch["t"], dwb, 1, 0)
                ch["nn"] = ch["d_a"] * ch["m"]["dm"]
                ch["nn_q"] = _bdot(ch["nn"], ch["qh"], 0, 0)
                ch["nn_k"] = _bdot(ch["nn"], ch["kh"], 1, 0)
            for ch in chains:
                ch["x"] = _dot(ch["d_t"].astype(BF16), ch["t"], 1, 0)
            for ch in chains:
                d_l = -_dot(ch["t"], ch["x"].astype(BF16), 1, 0)
                ch["d_l"] = jnp.where(ch["m"]["strict"], d_l, 0.0)
                ch["mm"] = ch["d_l"] * ch["m"]["dm"]
            for ch in chains:
                m = ch["m"]
                ch["mm_kh"] = _bdot(ch["mm"], ch["kh"], 1, 0)
                ch["mm_kb"] = _bdot(ch["mm"], ch["kb"], 0, 0)
                l_mat = jnp.where(m["strict"], m["beta"] * ch["kk"] * m["dm"], 0.0)
                ch["e"] = ch["d_l"] * l_mat + ch["nn"] * ch["qk"]
                dbgr_ref[c, ch["col"]:ch["col"] + 1, :] = -_dot(ones8, ch["e"], 1, 0, HI)[0:1, :]
            acc_bg = jnp.zeros((CHUNK, 128), F32)
            acc = {}
            for ch in chains:
                m = ch["m"]
                beta, eg, egl = m["beta"], m["eg"], m["egl"]
                dkb = ch["mm_kh"] + ch["dkbg"] * eg
                dk_d = ch["mm_kb"] + ch["nn_q"] + ch["dkd"] * egl + dkb * beta
                dq_d = ch["nn_k"] + ch["dqd"] * eg
                dv_d = ch["dvb"] * beta
                dkd_kd = ch["dkd"] * (ch["kh"] * egl)
                dgc = (jnp.sum(ch["e"], axis=1, keepdims=True)
                       + jnp.sum(ch["dqd"] * (ch["qh"] * eg) - dkd_kd + ch["dkbg"] * ch["kbg"], axis=1, keepdims=True))
                dgl = jnp.sum(jnp.sum(dkd_kd, axis=1, keepdims=True), axis=0, keepdims=True) + ch["dcd"] * m["cd"]
                dgc = dgc + jnp.where(rowi == (0 if ch["d"] == 1 else CHUNK - 1), dgl, 0.0)
                dbeta = jnp.sum(dkb * ch["kh"] + ch["dvb"] * ch["vh"], axis=1, keepdims=True)
                acc_bg = acc_bg + jnp.where(lane == ch["col"], dbeta, 0.0) + jnp.where(lane == 8 + ch["col"], dgc, 0.0)
                if ch["d"] == 0:
                    acc[ch["h"]] = (dq_d, dk_d, dv_d)
                else:
                    dq0, dk0, dv0 = acc[ch["h"]]
                    dq_ref[rows, ch["cols"]] = dq0 + dq_d
                    dk_ref[rows, ch["cols"]] = dk0 + dk_d
                    dv_ref[rows, ch["cols"]] = dv0 + dv_d
            dbg_ref[rows, :] = acc_bg
            return carry

        lax.fori_loop(0, ncb, chunk, 0)

    im = lambda i: (i, 0)
    im4 = lambda i: (i, 0, 0, 0)
    blk = (ts, GDN_W)
    ins = [(q, blk, im), (k, blk, im), (v, blk, im), (bg, (ts, 128), im), (gcr, (ncb, 8, CHUNK), lambda i: (i, 0, 0)),
           (do, blk, im)]
    for d in range(2):
        ins += [(loc[d][3], (ncb, GDN_H, CHUNK, CHUNK), im4), (fwd[d][2], (ncb, GDN_H, GDN_DK, GDN_DK), im4),
                (adj[d][1], (ncb, GDN_H, GDN_DK, GDN_DK), im4), (fwd[d][1], blk, im), (adj[d][0], blk, im)]
    sds = jax.ShapeDtypeStruct((S, GDN_W), F32)
    outs = [(sds, blk, im), (sds, blk, im), (sds, blk, im), (jax.ShapeDtypeStruct((S, 128), F32), (ts, 128), im),
            (jax.ShapeDtypeStruct((S // CHUNK, 8, CHUNK), F32), (ncb, 8, CHUNK), lambda i: (i, 0, 0))]
    dq, dk, dv, dbg, dbg_rows = _rows("gdn_local_bwd", S, ts, ins, outs, body)
    dgc_cols = dbg_rows.transpose(0, 2, 1).reshape(S, 8)
    return dq, dk, dv, dbg + jnp.pad(dgc_cols, ((0, 0), (8, 112)))


def _gdn_prep_bwd(dbg_all, p, prm):
    S = p.shape[0]
    ts = _tile(S, 512)

    def body(dbg_ref, p_ref, prm_ref, dba_ref, dprm_ref):
        i = pl.program_id(0)
        raw = p_ref[...].astype(F32)
        dbg = dbg_ref[...]
        lane = lax.broadcasted_iota(jnp.int32, (1, 128), 1)
        is_g = (lane >= 8) & (lane < 16)
        ea = jnp.exp(prm_ref[0:1, :])
        arg = raw + prm_ref[1:2, :]
        g = jnp.where(is_g, -ea * _softplus(arg), 0.0)
        beta = _sigmoid(raw)
        dgc = jnp.where(is_g, dbg, 0.0)
        ri, ci = _tri_masks()
        lower = (ri >= ci).astype(F32)
        upper = (ri <= ci).astype(F32)
        dgs = []
        for c in range(ts // CHUNK):
            ch = dgc[c * CHUNK:(c + 1) * CHUNK]
            dgs.append(jnp.where(lane < 12, _dot(upper, ch, 1, 0, HI), _dot(lower, ch, 1, 0, HI)))
        dg = jnp.concatenate(dgs, axis=0)
        dalpha = jnp.where(is_g, dg * (-ea) * _sigmoid(arg), 0.0)
        dba_ref[...] = jnp.where(lane < 8, dbg * beta * (1.0 - beta), dalpha).astype(BF16)
        rows = jnp.concatenate([jnp.sum(dg * g, axis=0, keepdims=True), jnp.sum(dalpha, axis=0, keepdims=True),
                                jnp.zeros((6, 128), F32)], axis=0)
        _colsum_into(dprm_ref, i, rows)

    im = lambda i: (i, 0)
    z0 = lambda i: (0, 0)
    return _rows("gdn_prep_bwd", S, ts,
                 [(dbg_all, (ts, 128), im), (p, (ts, 128), lambda i: (i, COL_BA // 128)), (prm, (8, 128), z0)],
                 [(jax.ShapeDtypeStruct((S, 128), BF16), (ts, 128), im), (jax.ShapeDtypeStruct((8, 128), F32), (8, 128), z0)],
                 body)


def _mm_plain(name, M, N, K, tm, tn, tk, a, am, b, bm, dtype):
    return _fused_mm(name, M, N, K, tm, tn, tk, [(a, am), (b, bm)], [(0, 1, 0)], [],
                     [(jax.ShapeDtypeStruct((M, N), dtype), (tm, tn), _mn)],
                     lambda i, accs, ex, out: out[0].__setitem__(Ellipsis, accs[0][...].astype(dtype)))[0]


def _layer_bwd(x0, W, R, emit_big=None, emit_small=None):
    S = x0.shape[0]
    tm = _tile(S, 512)
    tk_s = _tile(S, 1024)
    G = {}

    def emit(**named):
        if emit_big is None:
            G.update(named)
            return None
        return emit_big(**named)

    def ffn_emit(prefix):
        return lambda **kw: emit(**{f"{prefix}_w_{k}": v for k, v in kw.items()})

    dx2, G["ffn2_norm"] = _ffn_bwd("ffn2b", R["dx3"], R["x2"], W["ffn2_norm"], R["h3"], R["a2"], R["b2"], R["f2"],
                                   W["ffn2_w_gate"], W["ffn2_w_up"], W["ffn2_w_down"], ffn_emit("ffn2"))
    tok = emit(w_out=_mm_plain("dw_out", D_MODEL, D_MODEL, S, D_MODEL, D_MODEL, tk_s, R["y"], "km", dx2, "kn", BF16))
    gn = W["gdn_norm"] if tok is None else W["gdn_norm"] + tok
    p = R["p"]
    dhr, dgate, do, dz, G["gdn_norm"] = _mix_out_bwd(dx2, W["w_out"], R["h_f"], R["h_b"], R["o_f"], R["o_b"], p, gn)
    lam_b, lam_f = _rg_scan_adj("rg_scan_bwd", R["a_b"], dhr, R["a_f"], dhr)
    dpre, dxc, d_rgprm = _rg_gates_bwd(R["xc"], R["bd"], R["rg_prm"], lam_f, lam_b, R["h_f"], R["h_b"])
    d_bd = _mm_plain("rg_dbd", RG_W, 4 * RG_W, S, RG_W, 4 * RG_W, tk_s, R["xc"], "km", dpre, "kn", F32)
    dx_rg, G["rg_conv_w"], G["rg_conv_b"] = _conv_bwd("rg_conv_bwd", p, 0, W["rg_conv_w"], [dxc], "bias")
    blocks = jnp.einsum("nigmj,nm->gnij", d_bd.reshape(RG_BLOCKS, RG_BLOCK, 4, RG_BLOCKS, RG_BLOCK),
                        jnp.eye(RG_BLOCKS, dtype=F32))
    G["rg_gate_a_w"] = jnp.stack([blocks[0], blocks[2]])
    G["rg_gate_x_w"] = jnp.stack([blocks[1], blocks[3]])
    G["rg_gate_a_b"] = jnp.stack([d_rgprm[0], d_rgprm[2]])
    G["rg_gate_x_b"] = jnp.stack([d_rgprm[1], d_rgprm[3]])
    G["rg_lambda"] = d_rgprm[4:6]
    adj = _gdn_scan_bwd(R["gdn_loc"], do)
    dq, dk, dv, dbg = _gdn_local_bwd(R["q"], R["k"], R["v"], R["bg"], R["gcr"], do, R["gdn_loc"], R["gdn_fwd"], adj)
    cw = W["gdn_conv_w"]
    dpq, dwq, _ = _conv_bwd("gdn_conv_q_bwd", p, 2, cw[:, 0:512], [dq], "q")
    dpk, dwk, _ = _conv_bwd("gdn_conv_k_bwd", p, 3, cw[:, 512:1024], [dk], "k")
    dpv, dwv, _ = _conv_bwd("gdn_conv_v_bwd", p, 4, cw[:, 1024:1536], [dv], "v")
    G["gdn_conv_w"] = jnp.concatenate([dwq, dwk, dwv], axis=1)
    dba, d_gprm = _gdn_prep_bwd(dbg, p, R["gdn_prm"])
    G["gdn_a_log"] = d_gprm[0, 8:16].reshape(2, GDN_H)
    G["gdn_dt_bias"] = d_gprm[1, 8:16].reshape(2, GDN_H)
    dp = jnp.concatenate([dx_rg, dgate, dpq, dpk, dpv, dz, dba], axis=1)
    tok = emit(w_in=_mm_plain("dw_in", D_MODEL, D_IN_PAD, S, D_MODEL, 640, tk_s, R["h2"], "km", dp, "kn", BF16))
    g_mix = W["mix_norm"] if tok is None else W["mix_norm"] + tok

    def epi_dx1(i, accs, ex, out):
        dx, dgt = _rmsnorm_bwd_tile(accs[0][...], ex[0][...], ex[1][...])
        out[0][...] = ex[2][...] + dx
        _colsum_into(out[1], i, jnp.sum(dgt, axis=0, keepdims=True))

    dx1, G["mix_norm"] = _fused_mm(
        "mix_dx", S, D_MODEL, D_IN_PAD, tm, D_MODEL, D_IN_PAD, [(dp, "mk"), (W["w_in"], "kn")], [(0, 1, 0)],
        [(R["x1"], (tm, D_MODEL), _mn), (g_mix, (1, D_MODEL), _row0), (dx2, (tm, D_MODEL), _mn)],
        [(jax.ShapeDtypeStruct((S, D_MODEL), F32), (tm, D_MODEL), _mn),
         (jax.ShapeDtypeStruct((1, D_MODEL), F32), (1, D_MODEL), _row0)], epi_dx1)
    G["final_norm"] = R["d_final_norm"]
    if emit_small is not None:
        emit_small(G)
    dx0, G["ffn1_norm"] = _ffn_bwd("ffn1b", dx1, x0, W["ffn1_norm"], R["h1"], R["a1"], R["b1"], R["f1"],
                                   W["ffn1_w_gate"], W["ffn1_w_up"], W["ffn1_w_down"], ffn_emit("ffn1"))
    return dx0, G


def _mesh_pos():
    x, y, c = lax.axis_index("x"), lax.axis_index("y"), lax.axis_index("c")
    return x, y, c, 4 * x + 2 * y + c


def _peer(x, y, c, r):
    px = 1 - x if r & 4 else x
    py = 1 - y if r & 2 else y
    pc = 1 - c if r & 1 else c
    return (px, py, pc), 4 * px + 2 * py + pc


_HBM = pl.BlockSpec(memory_space=pltpu.HBM)
_SEM = pl.BlockSpec(memory_space=pltpu.SEMAPHORE)


def _peer_copies(scatter, srcs, lands, send_sems, recv_sems):
    x, y, c, me = _mesh_pos()
    copies = []
    for a, (src, land) in enumerate(zip(srcs, lands)):
        for r in range(1, N_DEV):
            peer, peer_idx = _peer(x, y, c, r)
            copies.append(pltpu.make_async_remote_copy(
                src_ref=src.at[peer_idx] if scatter else src, dst_ref=land.at[r - 1] if scatter else land.at[me],
                send_sem=send_sems.at[a * 7 + r - 1], recv_sem=recv_sems.at[a * 7 + r - 1],
                device_id=peer, device_id_type=pl.DeviceIdType.MESH))
    return copies


def _exchange_start(name, scatter, arrays):
    slabs = arrays
    n = len(slabs)

    def body(*refs):
        srcs, lands = refs[0:n], refs[n:2 * n]
        send_sems, recv_sems = refs[2 * n], refs[2 * n + 1]
        token = refs[4 * n + 2]
        for cp in _peer_copies(scatter, srcs, lands, send_sems, recv_sems):
            cp.start()
        token[...] = jnp.zeros_like(token)

    land_shapes = [(N_DEV - 1,) + s.shape[1:] if scatter else (N_DEV,) + s.shape for s in slabs]
    n_sems = 7 * n
    out_shape = ([pltpu.SemaphoreType.DMA((n_sems,)), pltpu.SemaphoreType.DMA((n_sems,))]
                 + [pltpu.HBM(s.shape, s.dtype) for s in slabs]
                 + [pltpu.HBM(shp, s.dtype) for shp, s in zip(land_shapes, slabs)]
                 + [jax.ShapeDtypeStruct((8, 128), F32)])
    res = pl.pallas_call(
        body, name=name, out_shape=out_shape, in_specs=[_HBM] * (2 * n),
        out_specs=[_SEM, _SEM] + [_HBM] * (2 * n) + [pl.BlockSpec(memory_space=pltpu.VMEM)],
        input_output_aliases={i: 2 + i for i in range(2 * n)},
        compiler_params=pltpu.CompilerParams(has_side_effects=pltpu.SideEffectType.DATAFLOW_SIDE_EFFECTING),
    )(*[pltpu.with_memory_space_constraint(s, pltpu.HBM) for s in slabs],
      *[pltpu.with_memory_space_constraint(lax.empty(shp, s.dtype), pltpu.HBM) for shp, s in zip(land_shapes, slabs)])
    return dict(n=n, scatter=scatter, sems=res[0:2], srcs=res[2:2 + n], lands=res[2 + n:2 + 2 * n],
                token=res[2 + 2 * n][0, 0])


def _exchange_wait(name, started, after):
    n = started["n"]
    scatter = started["scatter"]

    def body(*refs):
        srcs, lands = refs[0:n], refs[n:2 * n]
        send_sems, recv_sems = refs[2 * n], refs[2 * n + 1]
        for cp in _peer_copies(scatter, srcs, lands, send_sems, recv_sems):
            cp.wait_send()
            cp.wait_recv()

    arrays = list(started["srcs"]) + list(started["lands"])
    res = pl.pallas_call(
        body, name=name, out_shape=[pltpu.HBM(a.shape, a.dtype) for a in arrays],
        in_specs=[_HBM] * (2 * n) + [_SEM, _SEM, pl.BlockSpec(memory_space=pl.ANY)], out_specs=[_HBM] * (2 * n),
        input_output_aliases={i: i for i in range(2 * n)},
        compiler_params=pltpu.CompilerParams(has_side_effects=pltpu.SideEffectType.DATAFLOW_SIDE_EFFECTING),
    )(*arrays, *started["sems"], after)
    return res[0:n], res[n:2 * n]


def _adamw_math(w, g, m, v):
    m2 = ADAM_B1 * m + (1.0 - ADAM_B1) * g
    v2 = ADAM_B2 * v + (1.0 - ADAM_B2) * (g * g)
    m_hat = m2 / (1.0 - ADAM_B1 ** ADAM_STEP)
    v_hat = v2 / (1.0 - ADAM_B2 ** ADAM_STEP)
    delta = -ADAM_LR * (m_hat / (jnp.sqrt(v_hat) + ADAM_EPS) + ADAM_WD * w)
    return delta, m2, v2


def _adamw_slabs(name, src, land, me, w, m, v, tr):
    R, C = w.shape

    def body(me_ref, own_ref, land_ref, w_ref, m_ref, v_ref, g_ref, d_ref, m2_ref, v2_ref):
        g = own_ref[0].astype(F32)
        for s in range(N_DEV - 1):
            g = g + land_ref[s].astype(F32)
        delta, m2, v2 = _adamw_math(w_ref[...], g, m_ref[...], v_ref[...])
        g_ref[...] = g
        d_ref[...] = delta
        m2_ref[...] = m2
        v2_ref[...] = v2

    im = lambda i, me_ref: (i, 0)
    grid_spec = pltpu.PrefetchScalarGridSpec(
        num_scalar_prefetch=1, grid=(R // tr,),
        in_specs=[pl.BlockSpec((1, tr, C), lambda i, me_ref: (me_ref[0], i, 0)),
                  pl.BlockSpec((N_DEV - 1, tr, C), lambda i, me_ref: (0, i, 0)),
                  pl.BlockSpec((tr, C), im), pl.BlockSpec((tr, C), im), pl.BlockSpec((tr, C), im)],
        out_specs=[pl.BlockSpec((tr, C), im)] * 4)
    return pl.pallas_call(body, name=name, grid_spec=grid_spec, out_shape=[jax.ShapeDtypeStruct((R, C), F32)] * 4,
                          compiler_params=_cp(1))(me.reshape(1).astype(jnp.int32), src, land, w, m, v)


def _sum_slots(name, slots):
    _, R, C = slots.shape

    def body(s_ref, o_ref):
        g = s_ref[0]
        for s in range(1, N_DEV):
            g = g + s_ref[s]
        o_ref[...] = g

    return _rows(name, R, R, [(slots, (N_DEV, R, C), lambda i: (0, 0, 0))],
                 [(jax.ShapeDtypeStruct((R, C), F32), (R, C), lambda i: (0, 0))], body)[0]


def _adamw_packed(name, g, w, m, v):
    R, C = g.shape

    def body(g_ref, w_ref, m_ref, v_ref, d_ref, m2_ref, v2_ref):
        delta, m2, v2 = _adamw_math(w_ref[...], g_ref[...], m_ref[...], v_ref[...])
        d_ref[...] = delta
        m2_ref[...] = m2
        v2_ref[...] = v2

    im = lambda i: (0, 0)
    sds = jax.ShapeDtypeStruct((R, C), F32)
    return _rows(name, R, R, [(a, (R, C), im) for a in (g, w, m, v)], [(sds, (R, C), im)] * 3, body)


def _pack(arrays):
    rows = []
    for a in arrays:
        flat = a.reshape(-1).astype(F32)
        pad = (-flat.shape[0]) % 128
        rows.append(jnp.pad(flat, (0, pad)).reshape(-1, 128))
    out = jnp.concatenate(rows, axis=0)
    return jnp.pad(out, ((0, (-out.shape[0]) % 8), (0, 0)))


def _unpack(packed, shapes):
    lead = packed.shape[:-2]
    outs = []
    r = 0
    for shp in shapes:
        n = math.prod(shp)
        nr = -(-n // 128)
        flat = packed[..., r:r + nr, :].reshape(lead + (nr * 128,))[..., :n]
        outs.append(flat.reshape(lead + tuple(shp)))
        r += nr
    return outs


FFN1_BIG = ["ffn1_w_gate", "ffn1_w_up", "ffn1_w_down"]
MIX_BIG = ["w_in", "w_out"]
FFN2_BIG = ["ffn2_w_gate", "ffn2_w_up", "ffn2_w_down"]
BIG = FFN1_BIG + MIX_BIG + FFN2_BIG
COL_SHARDED = {"ffn1_w_gate", "ffn1_w_up", "w_in", "ffn2_w_gate", "ffn2_w_up"}
SMALL_SHARDED = ["rg_conv_w", "rg_gate_a_b", "rg_gate_x_b", "rg_lambda", "gdn_conv_w"]
WEIGHTS = ["ffn1_norm", "ffn1_w_gate", "ffn1_w_up", "ffn1_w_down", "mix_norm", "w_in", "w_out", "rg_conv_w", "rg_conv_b",
           "rg_gate_a_w", "rg_gate_a_b", "rg_gate_x_w", "rg_gate_x_b", "rg_lambda", "gdn_conv_w", "gdn_a_log",
           "gdn_dt_bias", "gdn_norm", "ffn2_norm", "ffn2_w_gate", "ffn2_w_up", "ffn2_w_down", "final_norm"]
SMALL = [n for n in WEIGHTS if n not in BIG]
ROW_VECTORS = {"ffn1_norm", "mix_norm", "ffn2_norm", "gdn_norm", "rg_conv_b", "final_norm"}
ROW_TILE = {"ffn1_w_gate": 256, "ffn1_w_up": 256, "ffn1_w_down": 176, "w_in": 256, "w_out": 64,
            "ffn2_w_gate": 256, "ffn2_w_up": 256, "ffn2_w_down": 176}


def _to_slabs(name, g):
    if name in COL_SHARDED:
        r, ctot = g.shape
        return g.reshape(r, N_DEV, ctot // N_DEV).transpose(1, 0, 2)
    return g.reshape(N_DEV, g.shape[0] // N_DEV, g.shape[1])


def _step(x, target, w, m, v):
    _, _, _, me = _mesh_pos()
    def shard_to_send(n, tok=None):
        s = w[n] if tok is None else w[n] + tok
        return (s.T if n in COL_SHARDED else s).astype(BF16)

    def unshard(n, gth):
        full = gth.reshape(-1, gth.shape[-1])
        return jnp.pad(full, ((0, D_IN_PAD - D_IN), (0, 0))) if n == "w_in" else full

    def landed(started, name, after):
        srcs, lands = _exchange_wait(name, started, after)
        def with_own(src, land):
            slot = lax.broadcasted_iota(jnp.int32, (N_DEV,) + (1,) * src.ndim, 0)
            return jnp.where(slot == me, src[None], land)

        return [with_own(src, land) for src, land in zip(srcs, lands)]

    up_names = ["ffn1_w_gate", "ffn1_w_up"]
    small_shards = [w[n] for n in SMALL_SHARDED]
    st_up = _exchange_start("gather_ffn1_up_start", False, [shard_to_send(n) for n in up_names])
    tok = st_up["token"]
    st_down = _exchange_start("gather_ffn1_down_start", False, [shard_to_send("ffn1_w_down", tok)])
    tok = tok + st_down["token"]
    st_mix = _exchange_start("gather_mix_start", False,
                             [shard_to_send(n, tok) for n in MIX_BIG] + [_pack(small_shards) + tok])
    tok = tok + st_mix["token"]
    st_ffn2 = _exchange_start("gather_ffn2_start", False, [shard_to_send(n, tok) for n in FFN2_BIG])
    W = {n: w[n] for n in SMALL if n not in SMALL_SHARDED}
    W["ffn1_norm"] = w["ffn1_norm"] + (tok + st_ffn2["token"])

    def more(stage, after):
        if stage == "ffn1_up":
            return {n: unshard(n, gth) for n, gth in zip(up_names, landed(st_up, "gather_ffn1_up_wait", after))}
        if stage == "ffn1_down":
            return {"ffn1_w_down": unshard("ffn1_w_down", landed(st_down, "gather_ffn1_down_wait", after)[0])}
        if stage == "ffn2":
            return {n: unshard(n, gth) for n, gth in zip(FFN2_BIG, landed(st_ffn2, "gather_ffn2_wait", after))}
        got = landed(st_mix, "gather_mix_wait", after)
        new = {n: unshard(n, gth) for n, gth in zip(MIX_BIG, got)}
        for n, gth in zip(SMALL_SHARDED, _unpack(got[-1], [s.shape for s in small_shards])):
            new[n] = jnp.moveaxis(gth, 0, -2).reshape(gth.shape[1:-1] + (N_DEV * gth.shape[-1],))
        return new

    R = _layer_fwd(x, target, W, more)
    W = R["W"]
    pending = []

    def emit_big(**named):
        slabs = [_to_slabs(n, g[:, :D_IN] if n == "w_in" else g) for n, g in named.items()]
        started = _exchange_start(f"scatter_start_{len(pending)}", True, slabs)
        pending.append((list(named), started))
        return started["token"]

    small_started = []

    def emit_small(G):
        packed = _pack([G[n] for n in SMALL if n != "ffn1_norm"])
        small_started.append(_exchange_start("gather_small_start", False, [packed]))

    grad_x, G = _layer_bwd(x, W, R, emit_big, emit_small)
    st_late = _exchange_start("gather_ffn1_norm_start", False, [_pack([G["ffn1_norm"]])])
    loss = lax.psum(R["loss"][0, 0], ("x", "y", "c"))
    out = {}

    def finish(i, after):
        names, started = pending[i]
        srcs, lands = _exchange_wait(f"scatter_wait_{i}", started, after)
        for n, src, land in zip(names, srcs, lands):
            out[n] = _adamw_slabs(f"adamw_{n}", src, land, me, w[n], m[n], v[n], ROW_TILE[n])

    n_early = len(pending) - 2
    for i in range(n_early):
        finish(i, grad_x)
    early = [n for n in SMALL if n != "ffn1_norm"]
    srcs, lands = _exchange_wait("gather_small_wait", small_started[0], grad_x)
    slot = lax.broadcasted_iota(jnp.int32, (N_DEV, 1, 1), 0)
    slots = jnp.where(slot == me, srcs[0][None], lands[0])
    reduced = dict(zip(early, _unpack(_sum_slots("sum_small_grads", slots), [G[n].shape for n in early])))

    def adamw_small(name, names):
        g_small = []
        for n in names:
            g = reduced[n]
            if n in SMALL_SHARDED:
                per = g.shape[-1] // N_DEV
                g = lax.dynamic_slice_in_dim(g, me * per, per, axis=g.ndim - 1)
            g_small.append(g.reshape(w[n].shape))
        shapes = [w[n].shape for n in names]
        d_p, m_p, v_p = _adamw_packed(name, _pack(g_small), _pack([w[n] for n in names]),
                                      _pack([m[n] for n in names]), _pack([v[n] for n in names]))
        for n, g, d_, m_, v_ in zip(names, g_small, _unpack(d_p, shapes), _unpack(m_p, shapes), _unpack(v_p, shapes)):
            out[n] = (g, d_, m_, v_)
        return d_p

    done_early = adamw_small("adamw_small", early)
    srcs, lands = _exchange_wait("gather_ffn1_norm_wait", st_late, done_early)
    late = jnp.where(slot == me, srcs[0][None], lands[0])
    reduced["ffn1_norm"] = _unpack(_sum_slots("sum_ffn1_norm_grad", late), [G["ffn1_norm"].shape])[0]
    done = adamw_small("adamw_ffn1_norm", ["ffn1_norm"])
    for i in range(n_early, len(pending)):
        finish(i, done)
    return loss, grad_x, out


def kernel(x, ffn1_norm, ffn1_w_gate, ffn1_w_up, ffn1_w_down, mix_norm, w_in, w_out, rg_conv_w, rg_conv_b, rg_gate_a_w, rg_gate_a_b, rg_gate_x_w, rg_gate_x_b, rg_lambda, gdn_conv_w, gdn_a_log, gdn_dt_bias, gdn_norm, ffn2_norm, ffn2_w_gate, ffn2_w_up, ffn2_w_down, final_norm, loss_target, m_ffn1_norm, m_ffn1_w_gate, m_ffn1_w_up, m_ffn1_w_down, m_mix_norm, m_w_in, m_w_out, m_rg_conv_w, m_rg_conv_b, m_rg_gate_a_w, m_rg_gate_a_b, m_rg_gate_x_w, m_rg_gate_x_b, m_rg_lambda, m_gdn_conv_w, m_gdn_a_log, m_gdn_dt_bias, m_gdn_norm, m_ffn2_norm, m_ffn2_w_gate, m_ffn2_w_up, m_ffn2_w_down, m_final_norm, v_ffn1_norm, v_ffn1_w_gate, v_ffn1_w_up, v_ffn1_w_down, v_mix_norm, v_w_in, v_w_out, v_rg_conv_w, v_rg_conv_b, v_rg_gate_a_w, v_rg_gate_a_b, v_rg_gate_x_w, v_rg_gate_x_b, v_rg_lambda, v_gdn_conv_w, v_gdn_a_log, v_gdn_dt_bias, v_gdn_norm, v_ffn2_norm, v_ffn2_w_gate, v_ffn2_w_up, v_ffn2_w_down, v_final_norm):
    args = dict(locals())
    orig_shapes = {n: args[n].shape for n in WEIGHTS}

    def local(prefix):
        d = {}
        for n in WEIGHTS:
            a = args[prefix + n]
            d[n] = a.reshape(1, -1) if n in ROW_VECTORS else a[0]
        return d

    loss, grad_x, out = _step(x[0], loss_target[0], local(""), local("m_"), local("v_"))
    res = [loss, grad_x[None]]
    for k in range(4):
        res += [out[n][k].reshape(orig_shapes[n]) for n in WEIGHTS]
    return tuple(res)
```

```python
import functools
import math

import jax
import jax.numpy as jnp
from jax import lax
from jax.experimental import pallas as pl
from jax.experimental.pallas import tpu as pltpu

F32, BF16 = jnp.float32, jnp.bfloat16

D_MODEL = 1024
D_FF = 2816
RG_W = 512
RG_BLOCKS = 8
RG_BLOCK = 64
RG_C = 8.0
CONV_W = 4
GDN_H = 4
GDN_DK = 128
CHUNK = 64
EPS = 1e-6
D_IN = 3088
D_IN_PAD = 3200
COL_BA = 3072
N_DEV = 8
HALO = 16
VMEM_LIMIT = 48 * 1024 * 1024
VMEM_CAP = 60 * 1024 * 1024

ADAM_LR = 0.001
ADAM_B1 = 0.9
ADAM_B2 = 0.999
ADAM_EPS = 1e-08
ADAM_WD = 0.01
ADAM_STEP = 10

HI = lax.Precision.HIGHEST


def _cp(n, vmem_limit=None):
    return pltpu.CompilerParams(dimension_semantics=("arbitrary",) * n,
                                vmem_limit_bytes=VMEM_LIMIT if vmem_limit is None else vmem_limit)


def _matmul_vmem_limit(block_bytes, acc_bytes):
    need = 2 * block_bytes + 2 * acc_bytes
    return int(min(VMEM_CAP, max(VMEM_LIMIT, need * 4 // 3)))


def _tile(n, pref):
    return min(n, pref)


def _sigmoid(x):
    return 0.5 * jnp.tanh(0.5 * x) + 0.5


def _softplus(x):
    return jnp.maximum(x, 0.0) + jnp.log(1.0 + jnp.exp(-jnp.abs(x)))


def _dot(a, b, ca, cb, prec=None):
    return lax.dot_general(a, b, (((ca,), (cb,)), ((), ())), preferred_element_type=F32, precision=prec)


def _fused_mm(name, M, N, K, tm, tn, tk, ops, pairs, extras, outs, epilogue):
    nm, nn, nk = M // tm, N // tn, K // tk
    assert nm * tm == M and nn * tn == N and nk * tk == K, (name, M, N, K, tm, tn, tk)
    spec_of = {
        "mk": pl.BlockSpec((tm, tk), lambda i, j, k: (i, k)),
        "km": pl.BlockSpec((tk, tm), lambda i, j, k: (k, i)),
        "kn": pl.BlockSpec((tk, tn), lambda i, j, k: (k, j)),
        "nk": pl.BlockSpec((tn, tk), lambda i, j, k: (j, k)),
    }
    in_specs = [spec_of[m] for _, m in ops]
    in_specs += [pl.BlockSpec(bs, lambda i, j, k, im=im: im(i, j)) for _, bs, im in extras]
    out_specs = [pl.BlockSpec(bs, lambda i, j, k, im=im: im(i, j)) for _, bs, im in outs]
    n_ops, n_ex, n_out = len(ops), len(extras), len(outs)
    n_acc = 1 + max(g for _, _, g in pairs)
    modes = [m for _, m in ops]

    def body(*refs):
        op_refs = refs[:n_ops]
        ex_refs = refs[n_ops:n_ops + n_ex]
        out_refs = refs[n_ops + n_ex:n_ops + n_ex + n_out]
        accs = refs[n_ops + n_ex + n_out:]
        i = pl.program_id(0)
        k = pl.program_id(2)
        def dots():
            vals = [r[...].astype(BF16) for r in op_refs]
            for ia, ib, g in pairs:
                yield g, _dot(vals[ia], vals[ib], 1 if modes[ia] == "mk" else 0, 0 if modes[ib] == "kn" else 1)

        if nk == 1:
            sums = [None] * n_acc
            for g, d in dots():
                sums[g] = d if sums[g] is None else sums[g] + d
            epilogue(i, [_Held(s) for s in sums], ex_refs, out_refs)
            return

        @pl.when(k == 0)
        def _():
            for a in accs:
                a[...] = jnp.zeros_like(a)

        for g, d in dots():
            accs[g][...] += d

        @pl.when(k == nk - 1)
        def _():
            epilogue(i, accs, ex_refs, out_refs)

    op_block = {"mk": tm * tk, "km": tm * tk, "kn": tk * tn, "nk": tk * tn}
    block_bytes = sum(op_block[m] * a.dtype.itemsize for a, m in ops)
    block_bytes += sum(math.prod(bs) * jnp.dtype(a.dtype).itemsize for a, bs, _ in list(extras) + list(outs))
    res = pl.pallas_call(
        body, name=name, grid=(nm, nn, nk), in_specs=in_specs, out_specs=out_specs,
        out_shape=[o for o, _, _ in outs],
        scratch_shapes=[pltpu.VMEM((tm, tn), F32)] * (n_acc if nk > 1 else 0),
        compiler_params=_cp(3, _matmul_vmem_limit(block_bytes, n_acc * tm * tn * 4)),
    )(*[a for a, _ in ops], *[a for a, _, _ in extras])
    return res


class _Held:
    def __init__(self, value):
        self.value = value

    def __getitem__(self, idx):
        return self.value[idx]


def _mn(i, j):
    return (i, j)


def _row0(i, j):
    return (0, 0)


def _rows(name, S, ts, ins, outs, body, scratch=()):
    return pl.pallas_call(
        body, name=name, grid=(S // ts,),
        in_specs=[pl.BlockSpec(bs, im) for _, bs, im in ins],
        out_specs=[pl.BlockSpec(bs, im) for _, bs, im in outs],
        out_shape=[o for o, _, _ in outs],
        scratch_shapes=list(scratch),
        compiler_params=_cp(1),
    )(*[a for a, _, _ in ins])


def _halo_ins(arr, S, ts, width, colblk):
    per = ts // HALO
    last = S // HALO - 1
    return [
        (arr, (ts, width), lambda i: (i, colblk)),
        (arr, (HALO, width), lambda i: (jnp.maximum(i * per - 1, 0), colblk)),
        (arr, (HALO, width), lambda i: (jnp.minimum((i + 1) * per, last), colblk)),
    ]


def _ext(main_ref, prev_ref, next_ref, i, n_tiles):
    prev = jnp.where(i > 0, prev_ref[...].astype(F32), 0.0)
    nxt = jnp.where(i < n_tiles - 1, next_ref[...].astype(F32), 0.0)
    return jnp.concatenate([prev, main_ref[...].astype(F32), nxt], axis=0)


def _shift(ext, off, ts):
    n = ext.shape[0]
    if off == 0:
        return ext[HALO:HALO + ts]
    return pltpu.roll(ext, (-off) % n, 0)[HALO:HALO + ts]


def _rmsnorm_fwd(name, x, g):
    S, D = x.shape
    ts = _tile(S, 512)

    def body(x_ref, g_ref, o_ref):
        xv = x_ref[...]
        r = lax.rsqrt(jnp.mean(xv * xv, axis=-1, keepdims=True) + EPS)
        o_ref[...] = (xv * r * g_ref[...]).astype(BF16)

    return _rows(name, S, ts,
                 [(x, (ts, D), lambda i: (i, 0)), (g, (1, D), lambda i: (0, 0))],
                 [(jax.ShapeDtypeStruct((S, D), BF16), (ts, D), lambda i: (i, 0))], body)[0]


def _rmsnorm_bwd_tile(dh, x, g):
    r = lax.rsqrt(jnp.mean(x * x, axis=-1, keepdims=True) + EPS)
    xhat = x * r
    dxn = dh * g
    dx = r * (dxn - xhat * jnp.mean(dxn * xhat, axis=-1, keepdims=True))
    return dx, dh * xhat


def _ffn_fwd(tag, x, h, wg, wu, wd, extras, outs, finish):
    S = x.shape[0]
    tm = _tile(S, 1024)
    tn = 1408

    def epi_up(i, accs, ex, out):
        a = accs[0][...]
        b = accs[1][...]
        s = _sigmoid(a)
        sa = a * s
        out[0][...] = sa.astype(BF16)
        out[1][...] = (b * (s * (1.0 + a * (1.0 - s)))).astype(BF16)
        out[2][...] = (sa * b).astype(BF16)

    sds = jax.ShapeDtypeStruct((S, D_FF), BF16)
    a, b, f = _fused_mm(f"{tag}_up", S, D_FF, D_MODEL, tm, tn, D_MODEL,
                        [(h, "mk"), (wg, "nk"), (wu, "nk")], [(0, 1, 0), (0, 2, 1)], [],
                        [(sds, (tm, tn), _mn)] * 3, epi_up)

    def epi_down(i, accs, ex, out):
        finish(i, ex[0][...] + 0.5 * accs[0][...], ex[1:], out)

    if callable(wd):
        wd = wd(f)
    res = _fused_mm(f"{tag}_down", S, D_MODEL, D_FF, tm, D_MODEL, 1408,
                    [(f, "mk"), (wd, "kn")], [(0, 1, 0)], [(x, (tm, D_MODEL), _mn)] + extras(tm), outs(tm), epi_down)
    return res, a, b, f


def _rmsnorm_tile(xv, g):
    return (xv * lax.rsqrt(jnp.mean(xv * xv, axis=-1, keepdims=True) + EPS) * g).astype(BF16)


def _conv_taps(ext, w_ref, ts):
    acc = None
    for j in range(CONV_W):
        term = w_ref[j:j + 1, :] * _shift(ext, j - 2, ts)
        acc = term if acc is None else acc + term
    return acc


def _l2norm_heads(s, scale):
    outs = []
    for h in range(GDN_H):
        sh = s[:, h * GDN_DK:(h + 1) * GDN_DK]
        outs.append(sh * (lax.rsqrt(jnp.sum(sh * sh, axis=-1, keepdims=True) + EPS) * scale))
    return jnp.concatenate(outs, axis=-1)


def _conv_fwd(name, p, colblk, w, bias, mode):
    S = p.shape[0]
    ts = _tile(S, 512)
    n_tiles = S // ts
    C = w.shape[1]

    def body(main, prev, nxt, w_ref, b_ref, o_ref):
        i = pl.program_id(0)
        c = _conv_taps(_ext(main, prev, nxt, i, n_tiles), w_ref, ts)
        if mode == "bias":
            o_ref[...] = c + b_ref[...]
        else:
            s = c * _sigmoid(c)
            if mode == "q":
                s = _l2norm_heads(s, GDN_DK ** -0.5)
            elif mode == "k":
                s = _l2norm_heads(s, 1.0)
            o_ref[...] = s

    ins = _halo_ins(p, S, ts, C, colblk) + [(w, (CONV_W, C), lambda i: (0, 0)), (bias, (1, C), lambda i: (0, 0))]
    return _rows(name, S, ts, ins, [(jax.ShapeDtypeStruct((S, C), F32), (ts, C), lambda i: (i, 0))], body)[0]


def _rg_gate_terms(pre, xc, prm_ref, d):
    r = _sigmoid(pre[:, d * 1024:d * 1024 + RG_W] + prm_ref[2 * d:2 * d + 1, :])
    ig = _sigmoid(pre[:, d * 1024 + RG_W:(d + 1) * 1024] + prm_ref[2 * d + 1:2 * d + 2, :])
    sp = _softplus(-prm_ref[4 + d:5 + d, :])
    log_a = -RG_C * r * sp
    a = jnp.exp(log_a)
    t = jnp.tanh(log_a)
    sq = jnp.sqrt(-2.0 * t / (1.0 - t))
    return r, ig, sp, a, sq


def _rg_gates_fwd(xc, bd, prm):
    S = xc.shape[0]
    tm = _tile(S, 256)

    def epi(i, accs, ex, out):
        pre = accs[0][...]
        xv = ex[0][...]
        for d in range(2):
            r, ig, sp, a, sq = _rg_gate_terms(pre, xv, ex[1], d)
            out[2 * d][...] = a
            out[2 * d + 1][...] = sq * ig * xv

    sds = jax.ShapeDtypeStruct((S, RG_W), F32)
    blk = (tm, RG_W)
    im = lambda i, j: (i, 0)
    return _fused_mm("rg_gates_fwd", S, 4 * RG_W, RG_W, tm, 4 * RG_W, RG_W,
                     [(xc, "mk"), (bd, "kn")], [(0, 1, 0)],
                     [(xc, blk, im), (prm, (8, RG_W), _row0)], [(sds, blk, im)] * 4, epi)


SUBLANES = 8


def _scan_rows(a, b, reverse):
    rows = lax.broadcasted_iota(jnp.int32, a.shape, 0)
    s = 1
    while s < SUBLANES:
        shift = SUBLANES - s if reverse else s
        a_sh = pltpu.roll(a, shift, 0)
        b_sh = pltpu.roll(b, shift, 0)
        valid = (rows < SUBLANES - s) if reverse else (rows >= s)
        b = jnp.where(valid, a * b_sh + b, b)
        a = jnp.where(valid, a * a_sh, a)
        s *= 2
    return a, b


def _rg_scan(name, a_f, b_f, a_b, b_b):
    S, C = a_f.shape
    ts = _tile(S, 512)
    n_tiles = S // ts

    def body(af, bf, ab, bb, hf, hb, carry):
        @pl.when(pl.program_id(0) == 0)
        def _():
            carry[...] = jnp.zeros_like(carry)

        n_sub = ts // SUBLANES

        def step(j, c):
            cf, cb = c
            r0 = pl.multiple_of(j * SUBLANES, SUBLANES)
            cum_a, h0 = _scan_rows(af[pl.ds(r0, SUBLANES), :], bf[pl.ds(r0, SUBLANES), :], False)
            h = h0 + cum_a * cf
            hf[pl.ds(r0, SUBLANES), :] = h
            cf = h[SUBLANES - 1:SUBLANES, :]
            r1 = pl.multiple_of((n_sub - 1 - j) * SUBLANES, SUBLANES)
            cum_a, h0 = _scan_rows(ab[pl.ds(r1, SUBLANES), :], bb[pl.ds(r1, SUBLANES), :], True)
            h = h0 + cum_a * cb
            hb[pl.ds(r1, SUBLANES), :] = h
            cb = h[0:1, :]
            return cf, cb

        cf, cb = lax.fori_loop(0, n_sub, step, (carry[0:1, :], carry[1:2, :]), unroll=4)
        carry[0:1, :] = cf
        carry[1:2, :] = cb

    fw = lambda i: (i, 0)
    bw = lambda i: (n_tiles - 1 - i, 0)
    sds = jax.ShapeDtypeStruct((S, C), F32)
    return _rows(name, S, ts,
                 [(a_f, (ts, C), fw), (b_f, (ts, C), fw), (a_b, (ts, C), bw), (b_b, (ts, C), bw)],
                 [(sds, (ts, C), fw), (sds, (ts, C), bw)], body, scratch=[pltpu.VMEM((8, C), F32)])


def _tri_masks():
    ri = lax.broadcasted_iota(jnp.int32, (CHUNK, CHUNK), 0)
    ci = lax.broadcasted_iota(jnp.int32, (CHUNK, CHUNK), 1)
    return ri, ci


def _gdn_prep_fwd(p, prm):
    S = p.shape[0]
    ts = _tile(S, 512)

    def body(p_ref, prm_ref, o_ref):
        raw = p_ref[...].astype(F32)
        lane = lax.broadcasted_iota(jnp.int32, (1, 128), 1)
        g = -jnp.exp(prm_ref[0:1, :]) * _softplus(raw + prm_ref[1:2, :])
        g = jnp.where((lane >= 8) & (lane < 16), g, 0.0)
        beta = _sigmoid(raw)
        ri, ci = _tri_masks()
        lower = (ri >= ci).astype(F32)
        upper = (ri <= ci).astype(F32)
        for c in range(ts // CHUNK):
            rows = slice(c * CHUNK, (c + 1) * CHUNK)
            gch = g[rows]
            gc = jnp.where(lane < 12, _dot(lower, gch, 1, 0, HI), _dot(upper, gch, 1, 0, HI))
            o_ref[rows, :] = jnp.where(lane < 8, beta[rows], gc)

    return _rows("gdn_prep_fwd", S, ts,
                 [(p, (ts, 128), lambda i: (i, COL_BA // 128)), (prm, (8, 128), lambda i: (0, 0))],
                 [(jax.ShapeDtypeStruct((S, 128), F32), (ts, 128), lambda i: (i, 0))], body)[0]


def _bdot(a, b, ca, cb):
    return _dot(a.astype(BF16), b.astype(BF16), ca, cb)


GDN_W = GDN_H * GDN_DK
GDN_TS = 256
LOCAL_CHUNKS = 2

def _gdn_decay(bg_ref, gcr_ref, c, rows, r0, col, rev, ri, ci):
    beta = bg_ref[rows, col:col + 1]
    gc = bg_ref[rows, 8 + col:9 + col]
    last = 0 if rev else CHUNK - 1
    gl = bg_ref[pl.ds(r0 + last, 1), 8 + col:9 + col]
    out = dict(beta=beta, gc=gc, gl=gl, eg=jnp.exp(gc), egl=jnp.exp(gl - gc), cd=jnp.exp(gl))
    if gcr_ref is not None:
        incl = (ri <= ci) if rev else (ri >= ci)
        out["strict"] = (ri < ci) if rev else (ri > ci)
        out["dm"] = jnp.where(incl, jnp.exp(jnp.where(incl, gc - gcr_ref[c, col:col + 1, :], 0.0)), 0.0)
    return out


def _dir_tile(d, n_tiles, flip):
    if (d == 1) != flip:
        return lambda i: n_tiles - 1 - i
    return lambda i: i


def _gdn_local_fwd(q, k, v, bg, gcr):
    S = q.shape[0]
    ts = _tile(S, GDN_TS)
    ncb = ts // CHUNK
    nch = S // CHUNK

    def body(q_ref, k_ref, v_ref, bg_ref, gcr_ref, *out_refs):
        ri, ci = _tri_masks()
        eye = (ri == ci).astype(F32)
        outs = (out_refs[0:6], out_refs[6:12])
        cd_ref = out_refs[12]

        def chunk(cc, carry):
            chains = []
            for c in (LOCAL_CHUNKS * cc + j for j in range(LOCAL_CHUNKS)):
                r0 = pl.multiple_of(c * CHUNK, CHUNK)
                rows = pl.ds(r0, CHUNK)
                for h in range(GDN_H):
                    cols = slice(h * GDN_DK, (h + 1) * GDN_DK)
                    qh, kh, vh = q_ref[rows, cols], k_ref[rows, cols], v_ref[rows, cols]
                    both = _bdot(jnp.concatenate([qh, kh], axis=0), kh, 1, 1)
                    for d in range(2):
                        chains.append(dict(c=c, r0=r0, rows=rows, h=h, d=d, cols=cols, qh=qh, kh=kh, vh=vh,
                                           qk=both[0:CHUNK], kk=both[CHUNK:2 * CHUNK]))
            for ch in chains:
                m = _gdn_decay(bg_ref, gcr_ref, ch["c"], ch["rows"], ch["r0"], ch["d"] * GDN_H + ch["h"], ch["d"] == 1,
                               ri, ci)
                ch["m"] = m
                ch["x"] = -jnp.where(m["strict"], m["beta"] * ch["kk"] * m["dm"], 0.0)
                ch["t"] = eye + ch["x"]
            for ch in chains:
                ch["pw"] = _bdot(ch["x"], ch["x"], 1, 0)
            for level in range(1, 6):
                last_level = level == 5
                for ch in chains:
                    rhs = ch["t"] if last_level else jnp.concatenate([ch["t"], ch["pw"]], axis=1)
                    ch["prod"] = _bdot(ch["pw"], rhs, 1, 0)
                for ch in chains:
                    ch["t"] = ch["t"] + ch["prod"][:, 0:CHUNK]
                    if not last_level:
                        ch["pw"] = ch["prod"][:, CHUNK:2 * CHUNK]
            for ch in chains:
                m = ch["m"]
                rhs = jnp.concatenate([ch["vh"] * m["beta"], ch["kh"] * (m["beta"] * m["eg"])], axis=1)
                ch["uw"] = _bdot(ch["t"], rhs, 1, 0)
            for ch in chains:
                u_ref, w_ref, a_ref, t_ref, qd_ref, kd_ref = outs[ch["d"]]
                m = ch["m"]
                c, rows = ch["c"], ch["rows"]
                col = ch["d"] * GDN_H + ch["h"]
                u_ref[rows, ch["cols"]] = ch["uw"][:, 0:GDN_DK]
                w_ref[rows, ch["cols"]] = ch["uw"][:, GDN_DK:2 * GDN_DK].astype(BF16)
                a_ref[c, ch["h"]] = (ch["qk"] * m["dm"]).astype(BF16)
                t_ref[c, ch["h"]] = _bdot(ch["t"], eye, 0, 0).astype(BF16)
                qd_ref[rows, ch["cols"]] = (ch["qh"] * m["eg"]).astype(BF16)
                kd_ref[rows, ch["cols"]] = (ch["kh"] * m["egl"]).astype(BF16)
                cd_ref[c, col:col + 1, :] = jnp.broadcast_to(m["cd"], (1, 128))
            return carry

        lax.fori_loop(0, ncb // LOCAL_CHUNKS, chunk, 0)

    im = lambda i: (i, 0)
    im4 = lambda i: (i, 0, 0, 0)
    ins = [(q, (ts, GDN_W), im), (k, (ts, GDN_W), im), (v, (ts, GDN_W), im), (bg, (ts, 128), im),
           (gcr, (ncb, 8, CHUNK), lambda i: (i, 0, 0))]
    per_dir = [(jax.ShapeDtypeStruct((S, GDN_W), F32), (ts, GDN_W), im),
               (jax.ShapeDtypeStruct((S, GDN_W), BF16), (ts, GDN_W), im),
               (jax.ShapeDtypeStruct((nch, GDN_H, CHUNK, CHUNK), BF16), (ncb, GDN_H, CHUNK, CHUNK), im4),
               (jax.ShapeDtypeStruct((nch, GDN_H, CHUNK, CHUNK), BF16), (ncb, GDN_H, CHUNK, CHUNK), im4),
               (jax.ShapeDtypeStruct((S, GDN_W), BF16), (ts, GDN_W), im),
               (jax.ShapeDtypeStruct((S, GDN_W), BF16), (ts, GDN_W), im)]
    cd_out = (jax.ShapeDtypeStruct((nch, 8, 128), F32), (ncb, 8, 128), lambda i: (i, 0, 0))
    res = _rows("gdn_local_fwd", S, ts, ins, per_dir * 2 + [cd_out], body)
    return res[0:6], res[6:12], res[12]


def _gdn_scan_fwd(loc):
    S = loc[0][0].shape[0]
    ts = _tile(S, GDN_TS)
    n_tiles = S // ts
    ncb = ts // CHUNK
    nch = S // CHUNK

    def body(*refs):
        ins = (refs[0:6], refs[6:12])
        outs = (refs[12:15], refs[15:18])
        state = refs[18]

        @pl.when(pl.program_id(0) == 0)
        def _():
            state[...] = jnp.zeros_like(state)

        def chunk(cc, carry):
            chains = []
            for d in range(2):
                c = cc if d == 0 else ncb - 1 - cc
                rows = pl.ds(pl.multiple_of(c * CHUNK, CHUNK), CHUNK)
                for h in range(GDN_H):
                    cols = slice(h * GDN_DK, (h + 1) * GDN_DK)
                    chains.append(dict(d=d, h=h, c=c, rows=rows, cols=cols, st=state[d * GDN_H + h]))
            for ch in chains:
                qd_ref, kd_ref, u_ref, w_ref, a_ref, cd_ref = ins[ch["d"]]
                rows, cols = ch["rows"], ch["cols"]
                lhs = jnp.concatenate([w_ref[rows, cols], qd_ref[rows, cols]], axis=0)
                ch["ws_qs"] = _dot(lhs, ch["st"].astype(BF16), 1, 0)
            for ch in chains:
                qd_ref, kd_ref, u_ref, w_ref, a_ref, cd_ref = ins[ch["d"]]
                rows, cols = ch["rows"], ch["cols"]
                vn = u_ref[rows, cols] - ch["ws_qs"][0:CHUNK]
                vnb = vn.astype(BF16)
                ch["vn"] = vn
                ch["avn"] = _dot(a_ref[ch["c"], ch["h"]], vnb, 1, 0)
                ch["kvn"] = _dot(kd_ref[rows, cols], vnb, 0, 0)
            for ch in chains:
                o_ref, vn_ref, s_ref = outs[ch["d"]]
                cd_ref = ins[ch["d"]][5]
                rows, cols = ch["rows"], ch["cols"]
                col = ch["d"] * GDN_H + ch["h"]
                o_ref[rows, cols] = ch["ws_qs"][CHUNK:2 * CHUNK] + ch["avn"]
                vn_ref[rows, cols] = ch["vn"].astype(BF16)
                s_ref[ch["c"], ch["h"]] = ch["st"].astype(BF16)
                state[ch["d"] * GDN_H + ch["h"]] = ch["st"] * cd_ref[ch["c"], col:col + 1, :] + ch["kvn"]
            return carry

        lax.fori_loop(0, ncb, chunk, 0)

    ins, outs = [], []
    for d in range(2):
        tix = _dir_tile(d, n_tiles, False)
        im = lambda i, tix=tix: (tix(i), 0)
        im4 = lambda i, tix=tix: (tix(i), 0, 0, 0)
        u, w, a, _, qd, kd = loc[d]
        ins += [(qd, (ts, GDN_W), im), (kd, (ts, GDN_W), im), (u, (ts, GDN_W), im), (w, (ts, GDN_W), im),
                (a, (ncb, GDN_H, CHUNK, CHUNK), im4), (loc[2], (ncb, 8, 128), lambda i, tix=tix: (tix(i), 0, 0))]
        outs += [(jax.ShapeDtypeStruct((S, GDN_W), F32), (ts, GDN_W), im),
                 (jax.ShapeDtypeStruct((S, GDN_W), BF16), (ts, GDN_W), im),
                 (jax.ShapeDtypeStruct((nch, GDN_H, GDN_DK, GDN_DK), BF16), (ncb, GDN_H, GDN_DK, GDN_DK), im4)]
    res = _rows("gdn_scan_fwd", S, ts, ins, outs, body, scratch=[pltpu.VMEM((2 * GDN_H, GDN_DK, GDN_DK), F32)])
    return res[0:3], res[3:6]


def _gelu(x):
    c = math.sqrt(2.0 / math.pi)
    t = jnp.tanh(c * (x + 0.044715 * x * x * x))
    return 0.5 * x * (1.0 + t), t


def _mix_out_fwd(h_f, h_b, o_f, o_b, p, gn):
    S = h_f.shape[0]
    ts = _tile(S, 512)

    def body(hf, hb, of, ob, gate, z, gn_ref, y_ref):
        ge, _ = _gelu(gate[...].astype(F32))
        y_ref[:, 0:RG_W] = ((hf[...] + hb[...]) * ge).astype(BF16)
        o = of[...] + ob[...]
        zv = z[...].astype(F32)
        sz = zv * _sigmoid(zv)
        for h in range(GDN_H):
            cols = slice(h * GDN_DK, (h + 1) * GDN_DK)
            oh = o[:, cols]
            n = oh * lax.rsqrt(jnp.mean(oh * oh, axis=-1, keepdims=True) + EPS) * gn_ref[...]
            y_ref[:, RG_W + h * GDN_DK:RG_W + (h + 1) * GDN_DK] = (n * sz[:, cols]).astype(BF16)

    blk = (ts, RG_W)
    im = lambda i: (i, 0)
    ins = [(h_f, blk, im), (h_b, blk, im), (o_f, blk, im), (o_b, blk, im),
           (p, blk, lambda i: (i, 1)), (p, blk, lambda i: (i, 5)), (gn, (1, GDN_DK), lambda i: (0, 0))]
    return _rows("mix_out_fwd", S, ts, ins,
                 [(jax.ShapeDtypeStruct((S, D_MODEL), BF16), (ts, D_MODEL), im)], body)[0]


def _block_diag(w):
    n = w.shape[0]
    return jnp.einsum("nij,nm->nimj", w, jnp.eye(n, dtype=w.dtype)).reshape(n * w.shape[1], n * w.shape[2])


def _rg_bd(a_w, x_w):
    return jnp.concatenate([_block_diag(a_w[0]), _block_diag(x_w[0]), _block_diag(a_w[1]), _block_diag(x_w[1])],
                           axis=1).astype(BF16)


def _rg_prm(ba, bx, lam):
    return jnp.concatenate([ba[0:1], bx[0:1], ba[1:2], bx[1:2], lam, jnp.zeros((2, RG_W), F32)], axis=0)


def _gdn_prm(a_log, dt_bias):
    rows = jnp.zeros((8, 128), F32)
    rows = rows.at[0, 8:16].set(a_log.reshape(-1))
    return rows.at[1, 8:16].set(dt_bias.reshape(-1))


def _gc_rows(bg):
    S = bg.shape[0]
    return bg[:, 8:16].reshape(S // CHUNK, CHUNK, 8).transpose(0, 2, 1)


def _layer_fwd(x0, target, W, more=None):
    S = x0.shape[0]
    R = {}
    R["h1"] = _rmsnorm_fwd("rms1", x0, W["ffn1_norm"])
    if more is not None:
        W = {**W, **more("ffn1_up", R["h1"])}
    late_wd = {}

    def ffn1_wd(after):
        late_wd.update(more("ffn1_down", after))
        return late_wd["ffn1_w_down"]

    sd_x = jax.ShapeDtypeStruct((S, D_MODEL), F32)
    sd_h = jax.ShapeDtypeStruct((S, D_MODEL), BF16)

    def norm_after(gain):
        extras = lambda t: [(gain, (1, D_MODEL), _row0)]
        outs = lambda t: [(sd_x, (t, D_MODEL), _mn), (sd_h, (t, D_MODEL), _mn)]

        def finish(i, xo, ex, out):
            out[0][...] = xo
            out[1][...] = _rmsnorm_tile(xo, ex[0][...])

        return extras, outs, finish

    (R["x1"], R["h2"]), R["a1"], R["b1"], R["f1"] = _ffn_fwd(
        "ffn1", x0, R["h1"], W["ffn1_w_gate"], W["ffn1_w_up"], ffn1_wd if more is not None else W["ffn1_w_down"],
        *norm_after(W["mix_norm"]))
    if more is not None:
        W = {**W, **late_wd, **more("mixer", R["x1"])}
    tm = _tile(S, 512)
    tmp = _tile(S, 1024)
    tmp = _tile(S, 512)
    R["p"] = _fused_mm("in_proj", S, D_IN_PAD, D_MODEL, tmp, D_IN_PAD, D_MODEL, [(R["h2"], "mk"), (W["w_in"], "nk")],
                       [(0, 1, 0)], [], [(jax.ShapeDtypeStruct((S, D_IN_PAD), BF16), (tmp, D_IN_PAD), _mn)],
                       lambda i, accs, ex, out: out[0].__setitem__(Ellipsis, accs[0][...].astype(BF16)))[0]
    p = R["p"]
    R["xc"] = _conv_fwd("rg_conv_fwd", p, 0, W["rg_conv_w"], W["rg_conv_b"], "bias")
    R["bd"] = _rg_bd(W["rg_gate_a_w"], W["rg_gate_x_w"])
    R["rg_prm"] = _rg_prm(W["rg_gate_a_b"], W["rg_gate_x_b"], W["rg_lambda"])
    a_f, b_f, a_b, b_b = _rg_gates_fwd(R["xc"], R["bd"], R["rg_prm"])
    R["a_f"], R["a_b"] = a_f, a_b
    R["h_f"], R["h_b"] = _rg_scan("rg_scan_fwd", a_f, b_f, a_b, b_b)
    zero_b = jnp.zeros((1, RG_W), F32)
    cw = W["gdn_conv_w"]
    R["q"] = _conv_fwd("gdn_conv_q", p, 2, cw[:, 0:512], zero_b, "q")
    R["k"] = _conv_fwd("gdn_conv_k", p, 3, cw[:, 512:1024], zero_b, "k")
    R["v"] = _conv_fwd("gdn_conv_v", p, 4, cw[:, 1024:1536], zero_b, "v")
    R["gdn_prm"] = _gdn_prm(W["gdn_a_log"], W["gdn_dt_bias"])
    R["bg"] = _gdn_prep_fwd(p, R["gdn_prm"])
    R["gcr"] = _gc_rows(R["bg"])
    R["gdn_loc"] = _gdn_local_fwd(R["q"], R["k"], R["v"], R["bg"], R["gcr"])
    R["gdn_fwd"] = _gdn_scan_fwd(R["gdn_loc"])
    R["o_f"], R["o_b"] = R["gdn_fwd"][0][0], R["gdn_fwd"][1][0]
    R["y"] = _mix_out_fwd(R["h_f"], R["h_b"], R["o_f"], R["o_b"], p, W["gdn_norm"])
    def epi_out(i, accs, ex, out):
        x2 = ex[0][...] + accs[0][...]
        out[0][...] = x2
        out[1][...] = _rmsnorm_tile(x2, ex[1][...])

    R["x2"], R["h3"] = _fused_mm("out_proj", S, D_MODEL, D_MODEL, tm, D_MODEL, D_MODEL,
                                 [(R["y"], "mk"), (W["w_out"], "kn")], [(0, 1, 0)],
                                 [(R["x1"], (tm, D_MODEL), _mn), (W["ffn2_norm"], (1, D_MODEL), _row0)],
                                 [(sd_x, (tm, D_MODEL), _mn), (sd_h, (tm, D_MODEL), _mn)], epi_out)
    if more is not None:
        W = {**W, **more("ffn2", R["x2"])}

    def loss_finish(i, xo, ex, out):
        gv = ex[1][...]
        r = lax.rsqrt(jnp.mean(xo * xo, axis=-1, keepdims=True) + EPS)
        err = xo * r * gv - ex[0][...]
        dx, dgt = _rmsnorm_bwd_tile(err * (1.0 / D_MODEL), xo, gv)
        out[0][...] = dx
        _colsum_into(out[1], i, jnp.zeros((8, 128), F32) + jnp.sum(err * err) * (0.5 / D_MODEL))
        _colsum_into(out[2], i, jnp.sum(dgt, axis=0, keepdims=True))

    (R["dx3"], R["loss"], R["d_final_norm"]), R["a2"], R["b2"], R["f2"] = _ffn_fwd(
        "ffn2", R["x2"], R["h3"], W["ffn2_w_gate"], W["ffn2_w_up"], W["ffn2_w_down"],
        lambda t: [(target, (t, D_MODEL), _mn), (W["final_norm"], (1, D_MODEL), _row0)],
        lambda t: [(sd_x, (t, D_MODEL), _mn), (jax.ShapeDtypeStruct((8, 128), F32), (8, 128), _row0),
                   (jax.ShapeDtypeStruct((1, D_MODEL), F32), (1, D_MODEL), _row0)],
        loss_finish)
    R["W"] = W
    return R


def _colsum_into(ref, i, val):
    @pl.when(i == 0)
    def _():
        ref[...] = val

    @pl.when(i > 0)
    def _():
        ref[...] += val


def _ffn_bwd(tag, dout, x, g, h, a, b, f, wg, wu, wd, emit):
    S = x.shape[0]
    tm = _tile(S, 512)
    tk_s = _tile(S, 1024)
    dwd = _fused_mm(f"{tag}_dw_down", D_FF, D_MODEL, S, 1408, D_MODEL, tk_s, [(f, "km"), (dout, "kn")], [(0, 1, 0)], [],
                    [(jax.ShapeDtypeStruct((D_FF, D_MODEL), BF16), (1408, D_MODEL), _mn)],
                    lambda i, accs, ex, out: out[0].__setitem__(Ellipsis, (0.5 * accs[0][...]).astype(BF16)))[0]
    emit(down=dwd)

    def epi_act(i, accs, ex, out):
        df = 0.5 * accs[0][...]
        out[0][...] = (df * ex[1][...].astype(F32)).astype(BF16)
        out[1][...] = (df * ex[0][...].astype(F32)).astype(BF16)

    sds = jax.ShapeDtypeStruct((S, D_FF), BF16)
    tma = _tile(S, 256)
    da, db = _fused_mm(f"{tag}_dact", S, D_FF, D_MODEL, tma, D_FF, D_MODEL, [(dout, "mk"), (wd, "nk")], [(0, 1, 0)],
                       [(a, (tma, D_FF), _mn), (b, (tma, D_FF), _mn)], [(sds, (tma, D_FF), _mn)] * 2, epi_act)

    def epi_w2(i, accs, ex, out):
        out[0][...] = accs[0][...].astype(BF16)
        out[1][...] = accs[1][...].astype(BF16)

    sdw = jax.ShapeDtypeStruct((D_MODEL, D_FF), BF16)
    dwg, dwu = _fused_mm(f"{tag}_dw_up", D_MODEL, D_FF, S, D_MODEL, 1408, tk_s,
                         [(h, "km"), (da, "kn"), (db, "kn")], [(0, 1, 0), (0, 2, 1)], [],
                         [(sdw, (D_MODEL, 1408), _mn)] * 2, epi_w2)
    tok = emit(gate=dwg, up=dwu)
    if tok is not None:
        g = g + tok

    def epi_dx(i, accs, ex, out):
        dx, dgt = _rmsnorm_bwd_tile(accs[0][...], ex[0][...], ex[1][...])
        out[0][...] = ex[2][...] + dx
        _colsum_into(out[1], i, jnp.sum(dgt, axis=0, keepdims=True))

    tmx = _tile(S, 1024)
    dx, dg = _fused_mm(f"{tag}_dx", S, D_MODEL, D_FF, tmx, D_MODEL, 1408,
                       [(da, "mk"), (wg, "kn"), (db, "mk"), (wu, "kn")], [(0, 1, 0), (2, 3, 0)],
                       [(x, (tmx, D_MODEL), _mn), (g, (1, D_MODEL), _row0), (dout, (tmx, D_MODEL), _mn)],
                       [(jax.ShapeDtypeStruct((S, D_MODEL), F32), (tmx, D_MODEL), _mn),
                        (jax.ShapeDtypeStruct((1, D_MODEL), F32), (1, D_MODEL), _row0)], epi_dx)
    return dx, dg


def _mix_out_bwd(dx2, w_out, h_f, h_b, o_f, o_b, p, gn):
    S = dx2.shape[0]
    ts = _tile(S, 512)
    c0 = math.sqrt(2.0 / math.pi)

    def epi(i, accs, ex, out):
        hf, hb, of, ob, gate, z, gn_ref = ex
        dhr_ref, dgate_ref, do_ref, dz_ref, dgn_ref = out
        dy_ref = accs[0]
        gv = gate[...].astype(F32)
        ge, t = _gelu(gv)
        dy_rg = dy_ref[:, 0:RG_W]
        dhr_ref[...] = dy_rg * ge
        dgelu = 0.5 * (1.0 + t) + 0.5 * gv * (1.0 - t * t) * c0 * (1.0 + 3.0 * 0.044715 * gv * gv)
        dgate_ref[...] = (dy_rg * (hf[...] + hb[...]) * dgelu).astype(BF16)
        o = of[...] + ob[...]
        zv = z[...].astype(F32)
        sig = _sigmoid(zv)
        gnv = gn_ref[...]
        dgn = jnp.zeros((1, GDN_DK), F32)
        for h in range(GDN_H):
            cols = slice(h * GDN_DK, (h + 1) * GDN_DK)
            oh = o[:, cols]
            r = lax.rsqrt(jnp.mean(oh * oh, axis=-1, keepdims=True) + EPS)
            ohat = oh * r
            dyh = dy_ref[:, RG_W + h * GDN_DK:RG_W + (h + 1) * GDN_DK]
            zh = zv[:, cols]
            sh = sig[:, cols]
            dn = dyh * zh * sh
            dz_ref[:, cols] = (dyh * ohat * gnv * (sh * (1.0 + zh * (1.0 - sh)))).astype(BF16)
            dxn = dn * gnv
            do_ref[:, cols] = r * (dxn - ohat * jnp.mean(dxn * ohat, axis=-1, keepdims=True))
            dgn = dgn + jnp.sum(dn * ohat, axis=0, keepdims=True)
        _colsum_into(dgn_ref, i, dgn)

    blk = (ts, RG_W)
    im = lambda i, j: (i, 0)
    extras = [(h_f, blk, im), (h_b, blk, im), (o_f, blk, im), (o_b, blk, im),
              (p, blk, lambda i, j: (i, 1)), (p, blk, lambda i, j: (i, 5)), (gn, (1, GDN_DK), _row0)]
    outs = [(jax.ShapeDtypeStruct((S, RG_W), F32), blk, im), (jax.ShapeDtypeStruct((S, RG_W), BF16), blk, im),
            (jax.ShapeDtypeStruct((S, RG_W), F32), blk, im), (jax.ShapeDtypeStruct((S, RG_W), BF16), blk, im),
            (jax.ShapeDtypeStruct((1, GDN_DK), F32), (1, GDN_DK), _row0)]
    return _fused_mm("mix_out_bwd", S, D_MODEL, D_MODEL, ts, D_MODEL, D_MODEL, [(dx2, "mk"), (w_out, "nk")], [(0, 1, 0)],
                     extras, outs, epi)


def _rg_scan_adj(name, a_up, b_up, a_dn, b_dn):
    S, C = a_up.shape
    ts = _tile(S, 512)
    n_tiles = S // ts

    def body(au, bu, ad, bd, mu_ref, lam_ref, carry):
        @pl.when(pl.program_id(0) == 0)
        def _():
            carry[...] = jnp.zeros_like(carry)

        n_sub = ts // SUBLANES
        rows = lax.broadcasted_iota(jnp.int32, (SUBLANES, C), 0)

        def half(a_ref, b_ref, out_ref, r0, c_in, reverse):
            a = a_ref[pl.ds(r0, SUBLANES), :]
            b = b_ref[pl.ds(r0, SUBLANES), :]
            cum_a, c0 = _scan_rows(a, a * b, reverse)
            c = c0 + cum_a * c_in
            edge = 0 if not reverse else SUBLANES - 1
            c_prev = jnp.where(rows == edge, c_in, pltpu.roll(c, SUBLANES - 1 if reverse else 1, 0))
            out_ref[pl.ds(r0, SUBLANES), :] = b + c_prev
            return c[0:1, :] if reverse else c[SUBLANES - 1:SUBLANES, :]

        def step(j, c):
            cu, cd = c
            cu = half(au, bu, mu_ref, pl.multiple_of(j * SUBLANES, SUBLANES), cu, False)
            cd = half(ad, bd, lam_ref, pl.multiple_of((n_sub - 1 - j) * SUBLANES, SUBLANES), cd, True)
            return cu, cd

        cu, cd = lax.fori_loop(0, n_sub, step, (carry[0:1, :], carry[1:2, :]), unroll=4)
        carry[0:1, :] = cu
        carry[1:2, :] = cd

    fw = lambda i: (i, 0)
    bw = lambda i: (n_tiles - 1 - i, 0)
    sds = jax.ShapeDtypeStruct((S, C), F32)
    return _rows(name, S, ts,
                 [(a_up, (ts, C), fw), (b_up, (ts, C), fw), (a_dn, (ts, C), bw), (b_dn, (ts, C), bw)],
                 [(sds, (ts, C), fw), (sds, (ts, C), bw)], body, scratch=[pltpu.VMEM((8, C), F32)])


def _halo_ex(arr, S, tm, width):
    per = tm // HALO
    last = S // HALO - 1
    return [
        (arr, (tm, width), lambda i, j: (i, 0)),
        (arr, (HALO, width), lambda i, j: (jnp.maximum(i * per - 1, 0), 0)),
        (arr, (HALO, width), lambda i, j: (jnp.minimum((i + 1) * per, last), 0)),
    ]


def _rg_gates_bwd(xc, bd, prm, lam_f, lam_b, h_f, h_b):
    S = xc.shape[0]
    tm = _tile(S, 256)
    n_tiles = S // tm

    def epi(i, accs, ex, out):
        pre = accs[0][...]
        xv = ex[0][...]
        prm_ref = ex[1]
        lams = (ex[2][...], ex[3][...])
        hprev = (_shift(_ext(ex[4], ex[5], ex[6], i, n_tiles), -1, tm),
                 _shift(_ext(ex[7], ex[8], ex[9], i, n_tiles), 1, tm))
        dxc = jnp.zeros_like(xv)
        rows = []
        dlam_rows = []
        for d in range(2):
            r, ig, sp, a, sq = _rg_gate_terms(pre, xv, prm_ref, d)
            lam = lams[d]
            da = lam * hprev[d]
            di = lam * sq * xv
            dxc = dxc + lam * sq * ig
            dsq = lam * ig * xv
            dlog_a = da * a - dsq * (a * a) / sq
            dpre_r = dlog_a * (-RG_C * sp) * r * (1.0 - r)
            dpre_i = di * ig * (1.0 - ig)
            out[0][:, d * 1024:d * 1024 + RG_W] = dpre_r.astype(BF16)
            out[0][:, d * 1024 + RG_W:(d + 1) * 1024] = dpre_i.astype(BF16)
            rows += [jnp.sum(dpre_r, axis=0, keepdims=True), jnp.sum(dpre_i, axis=0, keepdims=True)]
            dsp = jnp.sum(dlog_a * (-RG_C * r), axis=0, keepdims=True)
            dlam_rows.append(-dsp * _sigmoid(-prm_ref[4 + d:5 + d, :]))
        out[1][...] = dxc + _dot(out[0][...], ex[10][...], 1, 1)
        zero = jnp.zeros((2, RG_W), F32)
        _colsum_into(out[2], i, jnp.concatenate(rows + dlam_rows + [zero], axis=0))

    blk = (tm, RG_W)
    im = lambda i, j: (i, 0)
    extras = ([(xc, blk, im), (prm, (8, RG_W), _row0), (lam_f, blk, im), (lam_b, blk, im)]
              + _halo_ex(h_f, S, tm, RG_W) + _halo_ex(h_b, S, tm, RG_W) + [(bd, (RG_W, 4 * RG_W), _row0)])
    outs = [(jax.ShapeDtypeStruct((S, 4 * RG_W), BF16), (tm, 4 * RG_W), im),
            (jax.ShapeDtypeStruct((S, RG_W), F32), blk, im),
            (jax.ShapeDtypeStruct((8, RG_W), F32), (8, RG_W), _row0)]
    return _fused_mm("rg_gates_bwd", S, 4 * RG_W, RG_W, tm, 4 * RG_W, RG_W, [(xc, "mk"), (bd, "kn")], [(0, 1, 0)],
                     extras, outs, epi)


def _roll_rows(ext, off):
    if off == 0:
        return ext
    return pltpu.roll(ext, (-off) % ext.shape[0], 0)


def _conv_bwd(name, p, colblk, w, grads, mode):
    S = p.shape[0]
    ts = _tile(S, 512)
    n_tiles = S // ts
    C = w.shape[1]
    ng = len(grads)

    def body(*refs):
        p_refs = refs[0:3]
        g_refs = refs[3:3 + 3 * ng]
        w_ref = refs[3 + 3 * ng]
        dx_ref, dw_ref, db_ref = refs[4 + 3 * ng:]
        i = pl.program_id(0)
        ext_p = _ext(*p_refs, i, n_tiles)
        dn = _ext(*g_refs[0:3], i, n_tiles)
        for gi in range(1, ng):
            dn = dn + _ext(*g_refs[3 * gi:3 * gi + 3], i, n_tiles)
        if mode == "bias":
            dc = dn
        else:
            c = None
            for j in range(CONV_W):
                term = w_ref[j:j + 1, :] * _roll_rows(ext_p, j - 2)
                c = term if c is None else c + term
            sig = _sigmoid(c)
            s = c * sig
            if mode in ("q", "k"):
                scale = GDN_DK ** -0.5 if mode == "q" else 1.0
                parts = []
                for h in range(GDN_H):
                    cols = slice(h * GDN_DK, (h + 1) * GDN_DK)
                    sh = s[:, cols]
                    dnh = dn[:, cols]
                    rinv = lax.rsqrt(jnp.sum(sh * sh, axis=-1, keepdims=True) + EPS)
                    parts.append(scale * rinv * (dnh - sh * (rinv * rinv) * jnp.sum(dnh * sh, axis=-1, keepdims=True)))
                ds = jnp.concatenate(parts, axis=-1)
            else:
                ds = dn
            dc = ds * (sig * (1.0 + c * (1.0 - sig)))
        dx = None
        for j in range(CONV_W):
            term = w_ref[j:j + 1, :] * _shift(dc, 2 - j, ts)
            dx = term if dx is None else dx + term
        dx_ref[...] = dx.astype(BF16)
        dc_main = dc[HALO:HALO + ts]
        dw = jnp.concatenate([jnp.sum(dc_main * _shift(ext_p, j - 2, ts), axis=0, keepdims=True)
                              for j in range(CONV_W)], axis=0)
        _colsum_into(dw_ref, i, dw)
        _colsum_into(db_ref, i, jnp.sum(dc_main, axis=0, keepdims=True))

    ins = _halo_ins(p, S, ts, C, colblk)
    for garr in grads:
        ins += _halo_ins(garr, S, ts, C, 0)
    ins += [(w, (CONV_W, C), lambda i: (0, 0))]
    z0 = lambda i: (0, 0)
    outs = [(jax.ShapeDtypeStruct((S, C), BF16), (ts, C), lambda i: (i, 0)),
            (jax.ShapeDtypeStruct((CONV_W, C), F32), (CONV_W, C), z0),
            (jax.ShapeDtypeStruct((1, C), F32), (1, C), z0)]
    return _rows(name, S, ts, ins, outs, body)


def _gdn_scan_bwd(loc, do):
    S = do.shape[0]
    ts = _tile(S, GDN_TS)
    n_tiles = S // ts
    ncb = ts // CHUNK
    nch = S // CHUNK

    def body(*refs):
        ins = (refs[0:6], refs[6:12])
        outs = (refs[12:14], refs[14:16])
        dstate = refs[16]

        @pl.when(pl.program_id(0) == 0)
        def _():
            dstate[...] = jnp.zeros_like(dstate)

        def chunk(cc, carry):
            chains = []
            for d in range(2):
                c = ncb - 1 - cc if d == 0 else cc
                rows = pl.ds(pl.multiple_of(c * CHUNK, CHUNK), CHUNK)
                for h in range(GDN_H):
                    cols = slice(h * GDN_DK, (h + 1) * GDN_DK)
                    chains.append(dict(d=d, h=h, c=c, rows=rows, cols=cols, dsn=dstate[d * GDN_H + h]))
            for ch in chains:
                qd_ref, kd_ref, cd_ref, w_ref, a_ref, do_ref = ins[ch["d"]]
                rows, cols = ch["rows"], ch["cols"]
                dob = do_ref[rows, cols].astype(BF16)
                ch["dvn"] = (_dot(a_ref[ch["c"], ch["h"]], dob, 0, 0)
                             + _dot(kd_ref[rows, cols], ch["dsn"].astype(BF16), 1, 0))
                ch["qdo"] = _dot(qd_ref[rows, cols], dob, 0, 0)
            for ch in chains:
                w_ref = ins[ch["d"]][3]
                ch["wdvn"] = _dot(w_ref[ch["rows"], ch["cols"]], ch["dvn"].astype(BF16), 0, 0)
            for ch in chains:
                dvn_ref, ds_ref = outs[ch["d"]]
                cd_ref = ins[ch["d"]][2]
                col = ch["d"] * GDN_H + ch["h"]
                dvn_ref[ch["rows"], ch["cols"]] = ch["dvn"].astype(BF16)
                ds_ref[ch["c"], ch["h"]] = ch["dsn"].astype(BF16)
                dstate[ch["d"] * GDN_H + ch["h"]] = (ch["qdo"] + cd_ref[ch["c"], col:col + 1, :] * ch["dsn"]
                                                     - ch["wdvn"])
            return carry

        lax.fori_loop(0, ncb, chunk, 0)

    ins, outs = [], []
    for d in range(2):
        tix = _dir_tile(d, n_tiles, True)
        im = lambda i, tix=tix: (tix(i), 0)
        im4 = lambda i, tix=tix: (tix(i), 0, 0, 0)
        _, w, a, _, qd, kd = loc[d]
        ins += [(qd, (ts, GDN_W), im), (kd, (ts, GDN_W), im), (loc[2], (ncb, 8, 128), lambda i, tix=tix: (tix(i), 0, 0)),
                (w, (ts, GDN_W), im), (a, (ncb, GDN_H, CHUNK, CHUNK), im4), (do, (ts, GDN_W), im)]
        outs += [(jax.ShapeDtypeStruct((S, GDN_W), BF16), (ts, GDN_W), im),
                 (jax.ShapeDtypeStruct((nch, GDN_H, GDN_DK, GDN_DK), BF16), (ncb, GDN_H, GDN_DK, GDN_DK), im4)]
    res = _rows("gdn_scan_bwd", S, ts, ins, outs, body, scratch=[pltpu.VMEM((2 * GDN_H, GDN_DK, GDN_DK), F32)])
    return res[0:2], res[2:4]


def _gdn_local_bwd(q, k, v, bg, gcr, do, loc, fwd, adj):
    S = q.shape[0]
    ts = _tile(S, GDN_TS)
    ncb = ts // CHUNK

    def body(q_ref, k_ref, v_ref, bg_ref, gcr_ref, do_ref, *rest):
        per_dir = (rest[0:5], rest[5:10])
        dq_ref, dk_ref, dv_ref, dbg_ref, dbgr_ref = rest[10:15]
        ri, ci = _tri_masks()
        lane = lax.broadcasted_iota(jnp.int32, (CHUNK, 128), 1)
        rowi = lax.broadcasted_iota(jnp.int32, (CHUNK, 1), 0)
        ones8 = jnp.ones((SUBLANES, CHUNK), F32)

        def chunk(c, carry):
            r0 = pl.multiple_of(c * CHUNK, CHUNK)
            rows = pl.ds(r0, CHUNK)
            chains = []
            for h in range(GDN_H):
                cols = slice(h * GDN_DK, (h + 1) * GDN_DK)
                qh, kh, vh = q_ref[rows, cols], k_ref[rows, cols], v_ref[rows, cols]
                dob = do_ref[rows, cols].astype(BF16)
                both = _bdot(jnp.concatenate([qh, kh], axis=0), kh, 1, 1)
                for d in range(2):
                    chains.append(dict(h=h, d=d, cols=cols, qh=qh, kh=kh, vh=vh, dob=dob, qk=both[0:CHUNK],
                                       kk=both[CHUNK:2 * CHUNK], col=d * GDN_H + h))
            for ch in chains:
                m = _gdn_decay(bg_ref, gcr_ref, c, rows, r0, ch["col"], ch["d"] == 1, ri, ci)
                t_ref, s_ref, ds_ref, vn_ref, dvn_ref = per_dir[ch["d"]]
                h, cols = ch["h"], ch["cols"]
                ch["m"] = m
                ch["kb"] = ch["kh"] * m["beta"]
                ch["kbg"] = ch["kb"] * m["eg"]
                ch["t"] = t_ref[c, h]
                stb = s_ref[c, h]
                ch["dsn"] = ds_ref[c, h]
                vnb = vn_ref[rows, cols]
                dvnb = dvn_ref[rows, cols]
                ch["dcd"] = jnp.sum(jnp.sum(stb.astype(F32) * ch["dsn"].astype(F32), axis=1, keepdims=True),
                                    axis=0, keepdims=True)
                ch["dqd"] = _dot(ch["dob"], stb, 1, 1)
                ch["d_a"] = _dot(ch["dob"], vnb, 1, 1)
                ch["dkd"] = _bdot(vnb, ch["dsn"], 1, 1)
                ch["dw"] = -_dot(dvnb, stb, 1, 1)
                ch["dvb"] = _dot(ch["t"], dvnb, 1, 0)
                ch["d_t"] = _bdot(dvnb, ch["vh"] * m["beta"], 1, 1)
            for ch in chains:
                dwb = ch["dw"].astype(BF16)
                ch["d_t"] = ch["d_t"] + _bdot(dwb, ch["kbg"], 1, 1)
                ch["dkbg"] = _dot(ch["t"], dwb, 1, 0)
                ch["nn"] = ch["d_a"] * ch["m"]["dm"]
                ch["nn_q"] = _bdot(ch["nn"], ch["qh"], 0, 0)
                ch["nn_k"] = _bdot(ch["nn"], ch["kh"], 1, 0)
            for ch in chains:
                ch["x"] = _dot(ch["d_t"].astype(BF16), ch["t"], 1, 0)
            for ch in chains:
                d_l = -_dot(ch["t"], ch["x"].astype(BF16), 1, 0)
                ch["d_l"] = jnp.where(ch["m"]["strict"], d_l, 0.0)
                ch["mm"] = ch["d_l"] * ch["m"]["dm"]
            for ch in chains:
                m = ch["m"]
                ch["mm_kh"] = _bdot(ch["mm"], ch["kh"], 1, 0)
                ch["mm_kb"] = _bdot(ch["mm"], ch["kb"], 0, 0)
                l_mat = jnp.where(m["strict"], m["beta"] * ch["kk"] * m["dm"], 0.0)
                ch["e"] = ch["d_l"] * l_mat + ch["nn"] * ch["qk"]
                dbgr_ref[c, ch["col"]:ch["col"] + 1, :] = -_dot(ones8, ch["e"], 1, 0, HI)[0:1, :]
            acc_bg = jnp.zeros((CHUNK, 128), F32)
            acc = {}
            for ch in chains:
                m = ch["m"]
                beta, eg, egl = m["beta"], m["eg"], m["egl"]
                dkb = ch["mm_kh"] + ch["dkbg"] * eg
                dk_d = ch["mm_kb"] + ch["nn_q"] + ch["dkd"] * egl + dkb * beta
                dq_d = ch["nn_k"] + ch["dqd"] * eg
                dv_d = ch["dvb"] * beta
                dkd_kd = ch["dkd"] * (ch["kh"] * egl)
                dgc = (jnp.sum(ch["e"], axis=1, keepdims=True)
                       + jnp.sum(ch["dqd"] * (ch["qh"] * eg) - dkd_kd + ch["dkbg"] * ch["kbg"], axis=1, keepdims=True))
                dgl = jnp.sum(jnp.sum(dkd_kd, axis=1, keepdims=True), axis=0, keepdims=True) + ch["dcd"] * m["cd"]
                dgc = dgc + jnp.where(rowi == (0 if ch["d"] == 1 else CHUNK - 1), dgl, 0.0)
                dbeta = jnp.sum(dkb * ch["kh"] + ch["dvb"] * ch["vh"], axis=1, keepdims=True)
                acc_bg = acc_bg + jnp.where(lane == ch["col"], dbeta, 0.0) + jnp.where(lane == 8 + ch["col"], dgc, 0.0)
                if ch["d"] == 0:
                    acc[ch["h"]] = (dq_d, dk_d, dv_d)
                else:
                    dq0, dk0, dv0 = acc[ch["h"]]
                    dq_ref[rows, ch["cols"]] = dq0 + dq_d
                    dk_ref[rows, ch["cols"]] = dk0 + dk_d
                    dv_ref[rows, ch["cols"]] = dv0 + dv_d
            dbg_ref[rows, :] = acc_bg
            return carry

        lax.fori_loop(0, ncb, chunk, 0)

    im = lambda i: (i, 0)
    im4 = lambda i: (i, 0, 0, 0)
    blk = (ts, GDN_W)
    ins = [(q, blk, im), (k, blk, im), (v, blk, im), (bg, (ts, 128), im), (gcr, (ncb, 8, CHUNK), lambda i: (i, 0, 0)),
           (do, blk, im)]
    for d in range(2):
        ins += [(loc[d][3], (ncb, GDN_H, CHUNK, CHUNK), im4), (fwd[d][2], (ncb, GDN_H, GDN_DK, GDN_DK), im4),
                (adj[d][1], (ncb, GDN_H, GDN_DK, GDN_DK), im4), (fwd[d][1], blk, im), (adj[d][0], blk, im)]
    sds = jax.ShapeDtypeStruct((S, GDN_W), F32)
    outs = [(sds, blk, im), (sds, blk, im), (sds, blk, im), (jax.ShapeDtypeStruct((S, 128), F32), (ts, 128), im),
            (jax.ShapeDtypeStruct((S // CHUNK, 8, CHUNK), F32), (ncb, 8, CHUNK), lambda i: (i, 0, 0))]
    dq, dk, dv, dbg, dbg_rows = _rows("gdn_local_bwd", S, ts, ins, outs, body)
    dgc_cols = dbg_rows.transpose(0, 2, 1).reshape(S, 8)
    return dq, dk, dv, dbg + jnp.pad(dgc_cols, ((0, 0), (8, 112)))


def _gdn_prep_bwd(dbg_all, p, prm):
    S = p.shape[0]
    ts = _tile(S, 512)

    def body(dbg_ref, p_ref, prm_ref, dba_ref, dprm_ref):
        i = pl.program_id(0)
        raw = p_ref[...].astype(F32)
        dbg = dbg_ref[...]
        lane = lax.broadcasted_iota(jnp.int32, (1, 128), 1)
        is_g = (lane >= 8) & (lane < 16)
        ea = jnp.exp(prm_ref[0:1, :])
        arg = raw + prm_ref[1:2, :]
        g = jnp.where(is_g, -ea * _softplus(arg), 0.0)
        beta = _sigmoid(raw)
        dgc = jnp.where(is_g, dbg, 0.0)
        ri, ci = _tri_masks()
        lower = (ri >= ci).astype(F32)
        upper = (ri <= ci).astype(F32)
        dgs = []
        for c in range(ts // CHUNK):
            ch = dgc[c * CHUNK:(c + 1) * CHUNK]
            dgs.append(jnp.where(lane < 12, _dot(upper, ch, 1, 0, HI), _dot(lower, ch, 1, 0, HI)))
        dg = jnp.concatenate(dgs, axis=0)
        dalpha = jnp.where(is_g, dg * (-ea) * _sigmoid(arg), 0.0)
        dba_ref[...] = jnp.where(lane < 8, dbg * beta * (1.0 - beta), dalpha).astype(BF16)
        rows = jnp.concatenate([jnp.sum(dg * g, axis=0, keepdims=True), jnp.sum(dalpha, axis=0, keepdims=True),
                                jnp.zeros((6, 128), F32)], axis=0)
        _colsum_into(dprm_ref, i, rows)

    im = lambda i: (i, 0)
    z0 = lambda i: (0, 0)
    return _rows("gdn_prep_bwd", S, ts,
                 [(dbg_all, (ts, 128), im), (p, (ts, 128), lambda i: (i, COL_BA // 128)), (prm, (8, 128), z0)],
                 [(jax.ShapeDtypeStruct((S, 128), BF16), (ts, 128), im), (jax.ShapeDtypeStruct((8, 128), F32), (8, 128), z0)],
                 body)


def _mm_plain(name, M, N, K, tm, tn, tk, a, am, b, bm, dtype):
    return _fused_mm(name, M, N, K, tm, tn, tk, [(a, am), (b, bm)], [(0, 1, 0)], [],
                     [(jax.ShapeDtypeStruct((M, N), dtype), (tm, tn), _mn)],
                     lambda i, accs, ex, out: out[0].__setitem__(Ellipsis, accs[0][...].astype(dtype)))[0]


def _layer_bwd(x0, W, R, emit_big=None, emit_small=None):
    S = x0.shape[0]
    tm = _tile(S, 512)
    tk_s = _tile(S, 1024)
    G = {}

    def emit(**named):
        if emit_big is None:
            G.update(named)
            return None
        return emit_big(**named)

    def ffn_emit(prefix):
        return lambda **kw: emit(**{f"{prefix}_w_{k}": v for k, v in kw.items()})

    dx2, G["ffn2_norm"] = _ffn_bwd("ffn2b", R["dx3"], R["x2"], W["ffn2_norm"], R["h3"], R["a2"], R["b2"], R["f2"],
                                   W["ffn2_w_gate"], W["ffn2_w_up"], W["ffn2_w_down"], ffn_emit("ffn2"))
    tok = emit(w_out=_mm_plain("dw_out", D_MODEL, D_MODEL, S, D_MODEL, D_MODEL, tk_s, R["y"], "km", dx2, "kn", BF16))
    gn = W["gdn_norm"] if tok is None else W["gdn_norm"] + tok
    p = R["p"]
    dhr, dgate, do, dz, G["gdn_norm"] = _mix_out_bwd(dx2, W["w_out"], R["h_f"], R["h_b"], R["o_f"], R["o_b"], p, gn)
    lam_b, lam_f = _rg_scan_adj("rg_scan_bwd", R["a_b"], dhr, R["a_f"], dhr)
    dpre, dxc, d_rgprm = _rg_gates_bwd(R["xc"], R["bd"], R["rg_prm"], lam_f, lam_b, R["h_f"], R["h_b"])
    d_bd = _mm_plain("rg_dbd", RG_W, 4 * RG_W, S, RG_W, 4 * RG_W, tk_s, R["xc"], "km", dpre, "kn", F32)
    dx_rg, G["rg_conv_w"], G["rg_conv_b"] = _conv_bwd("rg_conv_bwd", p, 0, W["rg_conv_w"], [dxc], "bias")
    blocks = jnp.einsum("nigmj,nm->gnij", d_bd.reshape(RG_BLOCKS, RG_BLOCK, 4, RG_BLOCKS, RG_BLOCK),
                        jnp.eye(RG_BLOCKS, dtype=F32))
    G["rg_gate_a_w"] = jnp.stack([blocks[0], blocks[2]])
    G["rg_gate_x_w"] = jnp.stack([blocks[1], blocks[3]])
    G["rg_gate_a_b"] = jnp.stack([d_rgprm[0], d_rgprm[2]])
    G["rg_gate_x_b"] = jnp.stack([d_rgprm[1], d_rgprm[3]])
    G["rg_lambda"] = d_rgprm[4:6]
    adj = _gdn_scan_bwd(R["gdn_loc"], do)
    dq, dk, dv, dbg = _gdn_local_bwd(R["q"], R["k"], R["v"], R["bg"], R["gcr"], do, R["gdn_loc"], R["gdn_fwd"], adj)
    cw = W["gdn_conv_w"]
    dpq, dwq, _ = _conv_bwd("gdn_conv_q_bwd", p, 2, cw[:, 0:512], [dq], "q")
    dpk, dwk, _ = _conv_bwd("gdn_conv_k_bwd", p, 3, cw[:, 512:1024], [dk], "k")
    dpv, dwv, _ = _conv_bwd("gdn_conv_v_bwd", p, 4, cw[:, 1024:1536], [dv], "v")
    G["gdn_conv_w"] = jnp.concatenate([dwq, dwk, dwv], axis=1)
    dba, d_gprm = _gdn_prep_bwd(dbg, p, R["gdn_prm"])
    G["gdn_a_log"] = d_gprm[0, 8:16].reshape(2, GDN_H)
    G["gdn_dt_bias"] = d_gprm[1, 8:16].reshape(2, GDN_H)
    dp = jnp.concatenate([dx_rg, dgate, dpq, dpk, dpv, dz, dba], axis=1)
    tok = emit(w_in=_mm_plain("dw_in", D_MODEL, D_IN_PAD, S, D_MODEL, 640, tk_s, R["h2"], "km", dp, "kn", BF16))
    g_mix = W["mix_norm"] if tok is None else W["mix_norm"] + tok

    def epi_dx1(i, accs, ex, out):
        dx, dgt = _rmsnorm_bwd_tile(accs[0][...], ex[0][...], ex[1][...])
        out[0][...] = ex[2][...] + dx
        _colsum_into(out[1], i, jnp.sum(dgt, axis=0, keepdims=True))

    dx1, G["mix_norm"] = _fused_mm(
        "mix_dx", S, D_MODEL, D_IN_PAD, tm, D_MODEL, D_IN_PAD, [(dp, "mk"), (W["w_in"], "kn")], [(0, 1, 0)],
        [(R["x1"], (tm, D_MODEL), _mn), (g_mix, (1, D_MODEL), _row0), (dx2, (tm, D_MODEL), _mn)],
        [(jax.ShapeDtypeStruct((S, D_MODEL), F32), (tm, D_MODEL), _mn),
         (jax.ShapeDtypeStruct((1, D_MODEL), F32), (1, D_MODEL), _row0)], epi_dx1)
    G["final_norm"] = R["d_final_norm"]
    if emit_small is not None:
        emit_small(G)
    dx0, G["ffn1_norm"] = _ffn_bwd("ffn1b", dx1, x0, W["ffn1_norm"], R["h1"], R["a1"], R["b1"], R["f1"],
                                   W["ffn1_w_gate"], W["ffn1_w_up"], W["ffn1_w_down"], ffn_emit("ffn1"))
    return dx0, G


def _mesh_pos():
    x, y, c = lax.axis_index("x"), lax.axis_index("y"), lax.axis_index("c")
    return x, y, c, 4 * x + 2 * y + c


def _peer(x, y, c, r):
    px = 1 - x if r & 4 else x
    py = 1 - y if r & 2 else y
    pc = 1 - c if r & 1 else c
    return (px, py, pc), 4 * px + 2 * py + pc


_HBM = pl.BlockSpec(memory_space=pltpu.HBM)
_SEM = pl.BlockSpec(memory_space=pltpu.SEMAPHORE)


def _peer_copies(scatter, srcs, lands, send_sems, recv_sems):
    x, y, c, me = _mesh_pos()
    copies = []
    for a, (src, land) in enumerate(zip(srcs, lands)):
        for r in range(1, N_DEV):
            peer, peer_idx = _peer(x, y, c, r)
            copies.append(pltpu.make_async_remote_copy(
                src_ref=src.at[peer_idx] if scatter else src, dst_ref=land.at[r - 1] if scatter else land.at[me],
                send_sem=send_sems.at[a * 7 + r - 1], recv_sem=recv_sems.at[a * 7 + r - 1],
                device_id=peer, device_id_type=pl.DeviceIdType.MESH))
    return copies


def _exchange_start(name, scatter, arrays):
    slabs = arrays
    n = len(slabs)

    def body(*refs):
        srcs, lands = refs[0:n], refs[n:2 * n]
        send_sems, recv_sems = refs[2 * n], refs[2 * n + 1]
        token = refs[4 * n + 2]
        for cp in _peer_copies(scatter, srcs, lands, send_sems, recv_sems):
            cp.start()
        token[...] = jnp.zeros_like(token)

    land_shapes = [(N_DEV - 1,) + s.shape[1:] if scatter else (N_DEV,) + s.shape for s in slabs]
    n_sems = 7 * n
    out_shape = ([pltpu.SemaphoreType.DMA((n_sems,)), pltpu.SemaphoreType.DMA((n_sems,))]
                 + [pltpu.HBM(s.shape, s.dtype) for s in slabs]
                 + [pltpu.HBM(shp, s.dtype) for shp, s in zip(land_shapes, slabs)]
                 + [jax.ShapeDtypeStruct((8, 128), F32)])
    res = pl.pallas_call(
        body, name=name, out_shape=out_shape, in_specs=[_HBM] * (2 * n),
        out_specs=[_SEM, _SEM] + [_HBM] * (2 * n) + [pl.BlockSpec(memory_space=pltpu.VMEM)],
        input_output_aliases={i: 2 + i for i in range(2 * n)},
        compiler_params=pltpu.CompilerParams(has_side_effects=pltpu.SideEffectType.DATAFLOW_SIDE_EFFECTING),
    )(*[pltpu.with_memory_space_constraint(s, pltpu.HBM) for s in slabs],
      *[pltpu.with_memory_space_constraint(lax.empty(shp, s.dtype), pltpu.HBM) for shp, s in zip(land_shapes, slabs)])
    return dict(n=n, scatter=scatter, sems=res[0:2], srcs=res[2:2 + n], lands=res[2 + n:2 + 2 * n],
                token=res[2 + 2 * n][0, 0])


def _exchange_wait(name, started, after):
    n = started["n"]
    scatter = started["scatter"]

    def body(*refs):
        srcs, lands = refs[0:n], refs[n:2 * n]
        send_sems, recv_sems = refs[2 * n], refs[2 * n + 1]
        for cp in _peer_copies(scatter, srcs, lands, send_sems, recv_sems):
            cp.wait_send()
            cp.wait_recv()

    arrays = list(started["srcs"]) + list(started["lands"])
    res = pl.pallas_call(
        body, name=name, out_shape=[pltpu.HBM(a.shape, a.dtype) for a in arrays],
        in_specs=[_HBM] * (2 * n) + [_SEM, _SEM, pl.BlockSpec(memory_space=pl.ANY)], out_specs=[_HBM] * (2 * n),
        input_output_aliases={i: i for i in range(2 * n)},
        compiler_params=pltpu.CompilerParams(has_side_effects=pltpu.SideEffectType.DATAFLOW_SIDE_EFFECTING),
    )(*arrays, *started["sems"], after)
    return res[0:n], res[n:2 * n]


def _adamw_math(w, g, m, v):
    m2 = ADAM_B1 * m + (1.0 - ADAM_B1) * g
    v2 = ADAM_B2 * v + (1.0 - ADAM_B2) * (g * g)
    m_hat = m2 / (1.0 - ADAM_B1 ** ADAM_STEP)
    v_hat = v2 / (1.0 - ADAM_B2 ** ADAM_STEP)
    delta = -ADAM_LR * (m_hat / (jnp.sqrt(v_hat) + ADAM_EPS) + ADAM_WD * w)
    return delta, m2, v2


def _adamw_slabs(name, src, land, me, w, m, v, tr):
    R, C = w.shape

    def body(me_ref, own_ref, land_ref, w_ref, m_ref, v_ref, g_ref, d_ref, m2_ref, v2_ref):
        g = own_ref[0].astype(F32)
        for s in range(N_DEV - 1):
            g = g + land_ref[s].astype(F32)
        delta, m2, v2 = _adamw_math(w_ref[...], g, m_ref[...], v_ref[...])
        g_ref[...] = g
        d_ref[...] = delta
        m2_ref[...] = m2
        v2_ref[...] = v2

    im = lambda i, me_ref: (i, 0)
    grid_spec = pltpu.PrefetchScalarGridSpec(
        num_scalar_prefetch=1, grid=(R // tr,),
        in_specs=[pl.BlockSpec((1, tr, C), lambda i, me_ref: (me_ref[0], i, 0)),
                  pl.BlockSpec((N_DEV - 1, tr, C), lambda i, me_ref: (0, i, 0)),
                  pl.BlockSpec((tr, C), im), pl.BlockSpec((tr, C), im), pl.BlockSpec((tr, C), im)],
        out_specs=[pl.BlockSpec((tr, C), im)] * 4)
    return pl.pallas_call(body, name=name, grid_spec=grid_spec, out_shape=[jax.ShapeDtypeStruct((R, C), F32)] * 4,
                          compiler_params=_cp(1))(me.reshape(1).astype(jnp.int32), src, land, w, m, v)


def _sum_slots(name, slots):
    _, R, C = slots.shape

    def body(s_ref, o_ref):
        g = s_ref[0]
        for s in range(1, N_DEV):
            g = g + s_ref[s]
        o_ref[...] = g

    return _rows(name, R, R, [(slots, (N_DEV, R, C), lambda i: (0, 0, 0))],
                 [(jax.ShapeDtypeStruct((R, C), F32), (R, C), lambda i: (0, 0))], body)[0]


def _adamw_packed(name, g, w, m, v):
    R, C = g.shape

    def body(g_ref, w_ref, m_ref, v_ref, d_ref, m2_ref, v2_ref):
        delta, m2, v2 = _adamw_math(w_ref[...], g_ref[...], m_ref[...], v_ref[...])
        d_ref[...] = delta
        m2_ref[...] = m2
        v2_ref[...] = v2

    im = lambda i: (0, 0)
    sds = jax.ShapeDtypeStruct((R, C), F32)
    return _rows(name, R, R, [(a, (R, C), im) for a in (g, w, m, v)], [(sds, (R, C), im)] * 3, body)


def _pack(arrays):
    rows = []
    for a in arrays:
        flat = a.reshape(-1).astype(F32)
        pad = (-flat.shape[0]) % 128
        rows.append(jnp.pad(flat, (0, pad)).reshape(-1, 128))
    out = jnp.concatenate(rows, axis=0)
    return jnp.pad(out, ((0, (-out.shape[0]) % 8), (0, 0)))


def _unpack(packed, shapes):
    lead = packed.shape[:-2]
    outs = []
    r = 0
    for shp in shapes:
        n = math.prod(shp)
        nr = -(-n // 128)
        flat = packed[..., r:r + nr, :].reshape(lead + (nr * 128,))[..., :n]
        outs.append(flat.reshape(lead + tuple(shp)))
        r += nr
    return outs


FFN1_BIG = ["ffn1_w_gate", "ffn1_w_up", "ffn1_w_down"]
MIX_BIG = ["w_in", "w_out"]
FFN2_BIG = ["ffn2_w_gate", "ffn2_w_up", "ffn2_w_down"]
BIG = FFN1_BIG + MIX_BIG + FFN2_BIG
COL_SHARDED = {"ffn1_w_gate", "ffn1_w_up", "w_in", "ffn2_w_gate", "ffn2_w_up"}
SMALL_SHARDED = ["rg_conv_w", "rg_gate_a_b", "rg_gate_x_b", "rg_lambda", "gdn_conv_w"]
WEIGHTS = ["ffn1_norm", "ffn1_w_gate", "ffn1_w_up", "ffn1_w_down", "mix_norm", "w_in", "w_out", "rg_conv_w", "rg_conv_b",
           "rg_gate_a_w", "rg_gate_a_b", "rg_gate_x_w", "rg_gate_x_b", "rg_lambda", "gdn_conv_w", "gdn_a_log",
           "gdn_dt_bias", "gdn_norm", "ffn2_norm", "ffn2_w_gate", "ffn2_w_up", "ffn2_w_down", "final_norm"]
SMALL = [n for n in WEIGHTS if n not in BIG]
ROW_VECTORS = {"ffn1_norm", "mix_norm", "ffn2_norm", "gdn_norm", "rg_conv_b", "final_norm"}
ROW_TILE = {"ffn1_w_gate": 256, "ffn1_w_up": 256, "ffn1_w_down": 176, "w_in": 256, "w_out": 64,
            "ffn2_w_gate": 256, "ffn2_w_up": 256, "ffn2_w_down": 176}


def _to_slabs(name, g):
    if name in COL_SHARDED:
        r, ctot = g.shape
        return g.reshape(r, N_DEV, ctot // N_DEV).transpose(1, 0, 2)
    return g.reshape(N_DEV, g.shape[0] // N_DEV, g.shape[1])


def _step(x, target, w, m, v):
    _, _, _, me = _mesh_pos()
    def shard_to_send(n, tok=None):
        s = w[n] if tok is None else w[n] + tok
        return (s.T if n in COL_SHARDED else s).astype(BF16)

    def unshard(n, gth):
        full = gth.reshape(-1, gth.shape[-1])
        return jnp.pad(full, ((0, D_IN_PAD - D_IN), (0, 0))) if n == "w_in" else full

    def landed(started, name, after):
        srcs, lands = _exchange_wait(name, started, after)
        def with_own(src, land):
            slot = lax.broadcasted_iota(jnp.int32, (N_DEV,) + (1,) * src.ndim, 0)
            return jnp.where(slot == me, src[None], land)

        return [with_own(src, land) for src, land in zip(srcs, lands)]

    up_names = ["ffn1_w_gate", "ffn1_w_up"]
    small_shards = [w[n] for n in SMALL_SHARDED]
    st_up = _exchange_start("gather_ffn1_up_start", False, [shard_to_send(n) for n in up_names])
    tok = st_up["token"]
    st_down = _exchange_start("gather_ffn1_down_start", False, [shard_to_send("ffn1_w_down", tok)])
    tok = tok + st_down["token"]
    st_mix = _exchange_start("gather_mix_start", False,
                             [shard_to_send(n, tok) for n in MIX_BIG] + [_pack(small_shards) + tok])
    tok = tok + st_mix["token"]
    st_ffn2 = _exchange_start("gather_ffn2_start", False, [shard_to_send(n, tok) for n in FFN2_BIG])
    W = {n: w[n] for n in SMALL if n not in SMALL_SHARDED}
    W["ffn1_norm"] = w["ffn1_norm"] + (tok + st_ffn2["token"])

    def more(stage, after):
        if stage == "ffn1_up":
            return {n: unshard(n, gth) for n, gth in zip(up_names, landed(st_up, "gather_ffn1_up_wait", after))}
        if stage == "ffn1_down":
            return {"ffn1_w_down": unshard("ffn1_w_down", landed(st_down, "gather_ffn1_down_wait", after)[0])}
        if stage == "ffn2":
            return {n: unshard(n, gth) for n, gth in zip(FFN2_BIG, landed(st_ffn2, "gather_ffn2_wait", after))}
        got = landed(st_mix, "gather_mix_wait", after)
        new = {n: unshard(n, gth) for n, gth in zip(MIX_BIG, got)}
        for n, gth in zip(SMALL_SHARDED, _unpack(got[-1], [s.shape for s in small_shards])):
            new[n] = jnp.moveaxis(gth, 0, -2).reshape(gth.shape[1:-1] + (N_DEV * gth.shape[-1],))
        return new

    R = _layer_fwd(x, target, W, more)
    W = R["W"]
    pending = []

    def emit_big(**named):
        slabs = [_to_slabs(n, g[:, :D_IN] if n == "w_in" else g) for n, g in named.items()]
        started = _exchange_start(f"scatter_start_{len(pending)}", True, slabs)
        pending.append((list(named), started))
        return started["token"]

    small_started = []

    def emit_small(G):
        packed = _pack([G[n] for n in SMALL if n != "ffn1_norm"])
        small_started.append(_exchange_start("gather_small_start", False, [packed]))

    grad_x, G = _layer_bwd(x, W, R, emit_big, emit_small)
    st_late = _exchange_start("gather_ffn1_norm_start", False, [_pack([G["ffn1_norm"]])])
    loss = lax.psum(R["loss"][0, 0], ("x", "y", "c"))
    out = {}

    def finish(i, after):
        names, started = pending[i]
        srcs, lands = _exchange_wait(f"scatter_wait_{i}", started, after)
        for n, src, land in zip(names, srcs, lands):
            out[n] = _adamw_slabs(f"adamw_{n}", src, land, me, w[n], m[n], v[n], ROW_TILE[n])

    n_early = len(pending) - 2
    for i in range(n_early):
        finish(i, grad_x)
    early = [n for n in SMALL if n != "ffn1_norm"]
    srcs, lands = _exchange_wait("gather_small_wait", small_started[0], grad_x)
    slot = lax.broadcasted_iota(jnp.int32, (N_DEV, 1, 1), 0)
    slots = jnp.where(slot == me, srcs[0][None], lands[0])
    reduced = dict(zip(early, _unpack(_sum_slots("sum_small_grads", slots), [G[n].shape for n in early])))

    def adamw_small(name, names):
        g_small = []
        for n in names:
            g = reduced[n]
            if n in SMALL_SHARDED:
                per = g.shape[-1] // N_DEV
                g = lax.dynamic_slice_in_dim(g, me * per, per, axis=g.ndim - 1)
            g_small.append(g.reshape(w[n].shape))
        shapes = [w[n].shape for n in names]
        d_p, m_p, v_p = _adamw_packed(name, _pack(g_small), _pack([w[n] for n in names]),
                                      _pack([m[n] for n in names]), _pack([v[n] for n in names]))
        for n, g, d_, m_, v_ in zip(names, g_small, _unpack(d_p, shapes), _unpack(m_p, shapes), _unpack(v_p, shapes)):
            out[n] = (g, d_, m_, v_)
        return d_p

    done_early = adamw_small("adamw_small", early)
    srcs, lands = _exchange_wait("gather_ffn1_norm_wait", st_late, done_early)
    late = jnp.where(slot == me, srcs[0][None], lands[0])
    reduced["ffn1_norm"] = _unpack(_sum_slots("sum_ffn1_norm_grad", late), [G["ffn1_norm"].shape])[0]
    done = adamw_small("adamw_ffn1_norm", ["ffn1_norm"])
    for i in range(n_early, len(pending)):
        finish(i, done)
    return loss, grad_x, out


def kernel(x, ffn1_norm, ffn1_w_gate, ffn1_w_up, ffn1_w_down, mix_norm, w_in, w_out, rg_conv_w, rg_conv_b, rg_gate_a_w, rg_gate_a_b, rg_gate_x_w, rg_gate_x_b, rg_lambda, gdn_conv_w, gdn_a_log, gdn_dt_bias, gdn_norm, ffn2_norm, ffn2_w_gate, ffn2_w_up, ffn2_w_down, final_norm, loss_target, m_ffn1_norm, m_ffn1_w_gate, m_ffn1_w_up, m_ffn1_w_down, m_mix_norm, m_w_in, m_w_out, m_rg_conv_w, m_rg_conv_b, m_rg_gate_a_w, m_rg_gate_a_b, m_rg_gate_x_w, m_rg_gate_x_b, m_rg_lambda, m_gdn_conv_w, m_gdn_a_log, m_gdn_dt_bias, m_gdn_norm, m_ffn2_norm, m_ffn2_w_gate, m_ffn2_w_up, m_ffn2_w_down, m_final_norm, v_ffn1_norm, v_ffn1_w_gate, v_ffn1_w_up, v_ffn1_w_down, v_mix_norm, v_w_in, v_w_out, v_rg_conv_w, v_rg_conv_b, v_rg_gate_a_w, v_rg_gate_a_b, v_rg_gate_x_w, v_rg_gate_x_b, v_rg_lambda, v_gdn_conv_w, v_gdn_a_log, v_gdn_dt_bias, v_gdn_norm, v_ffn2_norm, v_ffn2_w_gate, v_ffn2_w_up, v_ffn2_w_down, v_final_norm):
    args = dict(locals())
    orig_shapes = {n: args[n].shape for n in WEIGHTS}

    def local(prefix):
        d = {}
        for n in WEIGHTS:
            a = args[prefix + n]
            d[n] = a.reshape(1, -1) if n in ROW_VECTORS else a[0]
        return d

    loss, grad_x, out = _step(x[0], loss_target[0], local(""), local("m_"), local("v_"))
    res = [loss, grad_x[None]]
    for k in range(4):
        res += [out[n][k].reshape(orig_shapes[n]) for n in WEIGHTS]
    return tuple(res)
```

```python
import functools
import math

import jax
import jax.numpy as jnp
from jax import lax
from jax.experimental import pallas as pl
from jax.experimental.pallas import tpu as pltpu

F32, BF16 = jnp.float32, jnp.bfloat16

D_MODEL = 1024
D_FF = 2816
RG_W = 512
RG_BLOCKS = 8
RG_BLOCK = 64
RG_C = 8.0
CONV_W = 4
GDN_H = 4
GDN_DK = 128
CHUNK = 64
EPS = 1e-6
D_IN = 3088
D_IN_PAD = 3200
COL_BA = 3072
N_DEV = 8
HALO = 16
VMEM_LIMIT = 48 * 1024 * 1024
VMEM_CAP = 60 * 1024 * 1024

ADAM_LR = 0.001
ADAM_B1 = 0.9
ADAM_B2 = 0.999
ADAM_EPS = 1e-08
ADAM_WD = 0.01
ADAM_STEP = 10

HI = lax.Precision.HIGHEST


def _cp(n, vmem_limit=None):
    return pltpu.CompilerParams(dimension_semantics=("arbitrary",) * n,
                                vmem_limit_bytes=VMEM_LIMIT if vmem_limit is None else vmem_limit)


def _matmul_vmem_limit(block_bytes, acc_bytes):
    need = 2 * block_bytes + 2 * acc_bytes
    return int(min(VMEM_CAP, max(VMEM_LIMIT, need * 4 // 3)))


def _tile(n, pref):
    return min(n, pref)


def _sigmoid(x):
    return 0.5 * jnp.tanh(0.5 * x) + 0.5


def _softplus(x):
    return jnp.maximum(x, 0.0) + jnp.log(1.0 + jnp.exp(-jnp.abs(x)))


def _dot(a, b, ca, cb, prec=None):
    return lax.dot_general(a, b, (((ca,), (cb,)), ((), ())), preferred_element_type=F32, precision=prec)


def _fused_mm(name, M, N, K, tm, tn, tk, ops, pairs, extras, outs, epilogue):
    nm, nn, nk = M // tm, N // tn, K // tk
    assert nm * tm == M and nn * tn == N and nk * tk == K, (name, M, N, K, tm, tn, tk)
    spec_of = {
        "mk": pl.BlockSpec((tm, tk), lambda i, j, k: (i, k)),
        "km": pl.BlockSpec((tk, tm), lambda i, j, k: (k, i)),
        "kn": pl.BlockSpec((tk, tn), lambda i, j, k: (k, j)),
        "nk": pl.BlockSpec((tn, tk), lambda i, j, k: (j, k)),
    }
    in_specs = [spec_of[m] for _, m in ops]
    in_specs += [pl.BlockSpec(bs, lambda i, j, k, im=im: im(i, j)) for _, bs, im in extras]
    out_specs = [pl.BlockSpec(bs, lambda i, j, k, im=im: im(i, j)) for _, bs, im in outs]
    n_ops, n_ex, n_out = len(ops), len(extras), len(outs)
    n_acc = 1 + max(g for _, _, g in pairs)
    modes = [m for _, m in ops]

    def body(*refs):
        op_refs = refs[:n_ops]
        ex_refs = refs[n_ops:n_ops + n_ex]
        out_refs = refs[n_ops + n_ex:n_ops + n_ex + n_out]
        accs = refs[n_ops + n_ex + n_out:]
        i = pl.program_id(0)
        k = pl.program_id(2)
        def dots():
            vals = [r[...].astype(BF16) for r in op_refs]
            for ia, ib, g in pairs:
                yield g, _dot(vals[ia], vals[ib], 1 if modes[ia] == "mk" else 0, 0 if modes[ib] == "kn" else 1)

        if nk == 1:
            sums = [None] * n_acc
            for g, d in dots():
                sums[g] = d if sums[g] is None else sums[g] + d
            epilogue(i, [_Held(s) for s in sums], ex_refs, out_refs)
            return

        @pl.when(k == 0)
        def _():
            for a in accs:
                a[...] = jnp.zeros_like(a)

        for g, d in dots():
            accs[g][...] += d

        @pl.when(k == nk - 1)
        def _():
            epilogue(i, accs, ex_refs, out_refs)

    op_block = {"mk": tm * tk, "km": tm * tk, "kn": tk * tn, "nk": tk * tn}
    block_bytes = sum(op_block[m] * a.dtype.itemsize for a, m in ops)
    block_bytes += sum(math.prod(bs) * jnp.dtype(a.dtype).itemsize for a, bs, _ in list(extras) + list(outs))
    res = pl.pallas_call(
        body, name=name, grid=(nm, nn, nk), in_specs=in_specs, out_specs=out_specs,
        out_shape=[o for o, _, _ in outs],
        scratch_shapes=[pltpu.VMEM((tm, tn), F32)] * (n_acc if nk > 1 else 0),
        compiler_params=_cp(3, _matmul_vmem_limit(block_bytes, n_acc * tm * tn * 4)),
    )(*[a for a, _ in ops], *[a for a, _, _ in extras])
    return res


class _Held:
    def __init__(self, value):
        self.value = value

    def __getitem__(self, idx):
        return self.value[idx]


def _mn(i, j):
    return (i, j)


def _row0(i, j):
    return (0, 0)


def _rows(name, S, ts, ins, outs, body, scratch=()):
    return pl.pallas_call(
        body, name=name, grid=(S // ts,),
        in_specs=[pl.BlockSpec(bs, im) for _, bs, im in ins],
        out_specs=[pl.BlockSpec(bs, im) for _, bs, im in outs],
        out_shape=[o for o, _, _ in outs],
        scratch_shapes=list(scratch),
        compiler_params=_cp(1),
    )(*[a for a, _, _ in ins])


def _halo_ins(arr, S, ts, width, colblk):
    per = ts // HALO
    last = S // HALO - 1
    return [
        (arr, (ts, width), lambda i: (i, colblk)),
        (arr, (HALO, width), lambda i: (jnp.maximum(i * per - 1, 0), colblk)),
        (arr, (HALO, width), lambda i: (jnp.minimum((i + 1) * per, last), colblk)),
    ]


def _ext(main_ref, prev_ref, next_ref, i, n_tiles):
    prev = jnp.where(i > 0, prev_ref[...].astype(F32), 0.0)
    nxt = jnp.where(i < n_tiles - 1, next_ref[...].astype(F32), 0.0)
    return jnp.concatenate([prev, main_ref[...].astype(F32), nxt], axis=0)


def _shift(ext, off, ts):
    n = ext.shape[0]
    if off == 0:
        return ext[HALO:HALO + ts]
    return pltpu.roll(ext, (-off) % n, 0)[HALO:HALO + ts]


def _rmsnorm_fwd(name, x, g):
    S, D = x.shape
    ts = _tile(S, 512)

    def body(x_ref, g_ref, o_ref):
        xv = x_ref[...]
        r = lax.rsqrt(jnp.mean(xv * xv, axis=-1, keepdims=True) + EPS)
        o_ref[...] = (xv * r * g_ref[...]).astype(BF16)

    return _rows(name, S, ts,
                 [(x, (ts, D), lambda i: (i, 0)), (g, (1, D), lambda i: (0, 0))],
                 [(jax.ShapeDtypeStruct((S, D), BF16), (ts, D), lambda i: (i, 0))], body)[0]


def _rmsnorm_bwd_tile(dh, x, g):
    r = lax.rsqrt(jnp.mean(x * x, axis=-1, keepdims=True) + EPS)
    xhat = x * r
    dxn = dh * g
    dx = r * (dxn - xhat * jnp.mean(dxn * xhat, axis=-1, keepdims=True))
    return dx, dh * xhat


def _ffn_fwd(tag, x, h, wg, wu, wd, extras, outs, finish):
    S = x.shape[0]
    tm = _tile(S, 1024)
    tn = 1408

    def epi_up(i, accs, ex, out):
        a = accs[0][...]
        b = accs[1][...]
        s = _sigmoid(a)
        sa = a * s
        out[0][...] = sa.astype(BF16)
        out[1][...] = (b * (s * (1.0 + a * (1.0 - s)))).astype(BF16)
        out[2][...] = (sa * b).astype(BF16)

    sds = jax.ShapeDtypeStruct((S, D_FF), BF16)
    a, b, f = _fused_mm(f"{tag}_up", S, D_FF, D_MODEL, tm, tn, D_MODEL,
                        [(h, "mk"), (wg, "nk"), (wu, "nk")], [(0, 1, 0), (0, 2, 1)], [],
                        [(sds, (tm, tn), _mn)] * 3, epi_up)

    def epi_down(i, accs, ex, out):
        finish(i, ex[0][...] + 0.5 * accs[0][...], ex[1:], out)

    if callable(wd):
        wd = wd(f)
    res = _fused_mm(f"{tag}_down", S, D_MODEL, D_FF, tm, D_MODEL, 1408,
                    [(f, "mk"), (wd, "kn")], [(0, 1, 0)], [(x, (tm, D_MODEL), _mn)] + extras(tm), outs(tm), epi_down)
    return res, a, b, f


def _rmsnorm_tile(xv, g):
    return (xv * lax.rsqrt(jnp.mean(xv * xv, axis=-1, keepdims=True) + EPS) * g).astype(BF16)


def _conv_taps(ext, w_ref, ts):
    acc = None
    for j in range(CONV_W):
        term = w_ref[j:j + 1, :] * _shift(ext, j - 2, ts)
        acc = term if acc is None else acc + term
    return acc


def _l2norm_heads(s, scale):
    outs = []
    for h in range(GDN_H):
        sh = s[:, h * GDN_DK:(h + 1) * GDN_DK]
        outs.append(sh * (lax.rsqrt(jnp.sum(sh * sh, axis=-1, keepdims=True) + EPS) * scale))
    return jnp.concatenate(outs, axis=-1)


def _conv_fwd(name, p, colblk, w, bias, mode):
    S = p.shape[0]
    ts = _tile(S, 512)
    n_tiles = S // ts
    C = w.shape[1]

    def body(main, prev, nxt, w_ref, b_ref, o_ref):
        i = pl.program_id(0)
        c = _conv_taps(_ext(main, prev, nxt, i, n_tiles), w_ref, ts)
        if mode == "bias":
            o_ref[...] = c + b_ref[...]
        else:
            s = c * _sigmoid(c)
            if mode == "q":
                s = _l2norm_heads(s, GDN_DK ** -0.5)
            elif mode == "k":
                s = _l2norm_heads(s, 1.0)
            o_ref[...] = s

    ins = _halo_ins(p, S, ts, C, colblk) + [(w, (CONV_W, C), lambda i: (0, 0)), (bias, (1, C), lambda i: (0, 0))]
    return _rows(name, S, ts, ins, [(jax.ShapeDtypeStruct((S, C), F32), (ts, C), lambda i: (i, 0))], body)[0]


def _rg_gate_terms(pre, xc, prm_ref, d):
    r = _sigmoid(pre[:, d * 1024:d * 1024 + RG_W] + prm_ref[2 * d:2 * d + 1, :])
    ig = _sigmoid(pre[:, d * 1024 + RG_W:(d + 1) * 1024] + prm_ref[2 * d + 1:2 * d + 2, :])
    sp = _softplus(-prm_ref[4 + d:5 + d, :])
    log_a = -RG_C * r * sp
    a = jnp.exp(log_a)
    t = jnp.tanh(log_a)
    sq = jnp.sqrt(-2.0 * t / (1.0 - t))
    return r, ig, sp, a, sq


def _rg_gates_fwd(xc, bd, prm):
    S = xc.shape[0]
    tm = _tile(S, 256)

    def epi(i, accs, ex, out):
        pre = accs[0][...]
        xv = ex[0][...]
        for d in range(2):
            r, ig, sp, a, sq = _rg_gate_terms(pre, xv, ex[1], d)
            out[2 * d][...] = a
            out[2 * d + 1][...] = sq * ig * xv

    sds = jax.ShapeDtypeStruct((S, RG_W), F32)
    blk = (tm, RG_W)
    im = lambda i, j: (i, 0)
    return _fused_mm("rg_gates_fwd", S, 4 * RG_W, RG_W, tm, 4 * RG_W, RG_W,
                     [(xc, "mk"), (bd, "kn")], [(0, 1, 0)],
                     [(xc, blk, im), (prm, (8, RG_W), _row0)], [(sds, blk, im)] * 4, epi)


SUBLANES = 8


def _scan_rows(a, b, reverse):
    rows = lax.broadcasted_iota(jnp.int32, a.shape, 0)
    s = 1
    while s < SUBLANES:
        shift = SUBLANES - s if reverse else s
        a_sh = pltpu.roll(a, shift, 0)
        b_sh = pltpu.roll(b, shift, 0)
        valid = (rows < SUBLANES - s) if reverse else (rows >= s)
        b = jnp.where(valid, a * b_sh + b, b)
        a = jnp.where(valid, a * a_sh, a)
        s *= 2
    return a, b


def _rg_scan(name, a_f, b_f, a_b, b_b):
    S, C = a_f.shape
    ts = _tile(S, 512)
    n_tiles = S // ts

    def body(af, bf, ab, bb, hf, hb, carry):
        @pl.when(pl.program_id(0) == 0)
        def _():
            carry[...] = jnp.zeros_like(carry)

        n_sub = ts // SUBLANES

        def step(j, c):
            cf, cb = c
            r0 = pl.multiple_of(j * SUBLANES, SUBLANES)
            cum_a, h0 = _scan_rows(af[pl.ds(r0, SUBLANES), :], bf[pl.ds(r0, SUBLANES), :], False)
            h = h0 + cum_a * cf
            hf[pl.ds(r0, SUBLANES), :] = h
            cf = h[SUBLANES - 1:SUBLANES, :]
            r1 = pl.multiple_of((n_sub - 1 - j) * SUBLANES, SUBLANES)
            cum_a, h0 = _scan_rows(ab[pl.ds(r1, SUBLANES), :], bb[pl.ds(r1, SUBLANES), :], True)
            h = h0 + cum_a * cb
            hb[pl.ds(r1, SUBLANES), :] = h
            cb = h[0:1, :]
            return cf, cb

        cf, cb = lax.fori_loop(0, n_sub, step, (carry[0:1, :], carry[1:2, :]), unroll=4)
        carry[0:1, :] = cf
        carry[1:2, :] = cb

    fw = lambda i: (i, 0)
    bw = lambda i: (n_tiles - 1 - i, 0)
    sds = jax.ShapeDtypeStruct((S, C), F32)
    return _rows(name, S, ts,
                 [(a_f, (ts, C), fw), (b_f, (ts, C), fw), (a_b, (ts, C), bw), (b_b, (ts, C), bw)],
                 [(sds, (ts, C), fw), (sds, (ts, C), bw)], body, scratch=[pltpu.VMEM((8, C), F32)])


def _tri_masks():
    ri = lax.broadcasted_iota(jnp.int32, (CHUNK, CHUNK), 0)
    ci = lax.broadcasted_iota(jnp.int32, (CHUNK, CHUNK), 1)
    return ri, ci


def _gdn_prep_fwd(p, prm):
    S = p.shape[0]
    ts = _tile(S, 512)

    def body(p_ref, prm_ref, o_ref):
        raw = p_ref[...].astype(F32)
        lane = lax.broadcasted_iota(jnp.int32, (1, 128), 1)
        g = -jnp.exp(prm_ref[0:1, :]) * _softplus(raw + prm_ref[1:2, :])
        g = jnp.where((lane >= 8) & (lane < 16), g, 0.0)
        beta = _sigmoid(raw)
        ri, ci = _tri_masks()
        lower = (ri >= ci).astype(F32)
        upper = (ri <= ci).astype(F32)
        for c in range(ts // CHUNK):
            rows = slice(c * CHUNK, (c + 1) * CHUNK)
            gch = g[rows]
            gc = jnp.where(lane < 12, _dot(lower, gch, 1, 0, HI), _dot(upper, gch, 1, 0, HI))
            o_ref[rows, :] = jnp.where(lane < 8, beta[rows], gc)

    return _rows("gdn_prep_fwd", S, ts,
                 [(p, (ts, 128), lambda i: (i, COL_BA // 128)), (prm, (8, 128), lambda i: (0, 0))],
                 [(jax.ShapeDtypeStruct((S, 128), F32), (ts, 128), lambda i: (i, 0))], body)[0]


def _bdot(a, b, ca, cb):
    return _dot(a.astype(BF16), b.astype(BF16), ca, cb)


GDN_W = GDN_H * GDN_DK
GDN_TS = 256
LOCAL_CHUNKS = 2

def _gdn_decay(bg_ref, gcr_ref, c, rows, r0, col, rev, ri, ci):
    beta = bg_ref[rows, col:col + 1]
    gc = bg_ref[rows, 8 + col:9 + col]
    last = 0 if rev else CHUNK - 1
    gl = bg_ref[pl.ds(r0 + last, 1), 8 + col:9 + col]
    out = dict(beta=beta, gc=gc, gl=gl, eg=jnp.exp(gc), egl=jnp.exp(gl - gc), cd=jnp.exp(gl))
    if gcr_ref is not None:
        incl = (ri <= ci) if rev else (ri >= ci)
        out["strict"] = (ri < ci) if rev else (ri > ci)
        out["dm"] = jnp.where(incl, jnp.exp(jnp.where(incl, gc - gcr_ref[c, col:col + 1, :], 0.0)), 0.0)
    return out


def _dir_tile(d, n_tiles, flip):
    if (d == 1) != flip:
        return lambda i: n_tiles - 1 - i
    return lambda i: i


def _gdn_local_fwd(q, k, v, bg, gcr):
    S = q.shape[0]
    ts = _tile(S, GDN_TS)
    ncb = ts // CHUNK
    nch = S // CHUNK

    def body(q_ref, k_ref, v_ref, bg_ref, gcr_ref, *out_refs):
        ri, ci = _tri_masks()
        eye = (ri == ci).astype(F32)
        outs = (out_refs[0:6], out_refs[6:12])
        cd_ref = out_refs[12]

        def chunk(cc, carry):
            chains = []
            for c in (LOCAL_CHUNKS * cc + j for j in range(LOCAL_CHUNKS)):
                r0 = pl.multiple_of(c * CHUNK, CHUNK)
                rows = pl.ds(r0, CHUNK)
                for h in range(GDN_H):
                    cols = slice(h * GDN_DK, (h + 1) * GDN_DK)
                    qh, kh, vh = q_ref[rows, cols], k_ref[rows, cols], v_ref[rows, cols]
                    both = _bdot(jnp.concatenate([qh, kh], axis=0), kh, 1, 1)
                    for d in range(2):
                        chains.append(dict(c=c, r0=r0, rows=rows, h=h, d=d, cols=cols, qh=qh, kh=kh, vh=vh,
                                           qk=both[0:CHUNK], kk=both[CHUNK:2 * CHUNK]))
            for ch in chains:
                m = _gdn_decay(bg_ref, gcr_ref, ch["c"], ch["rows"], ch["r0"], ch["d"] * GDN_H + ch["h"], ch["d"] == 1,
                               ri, ci)
                ch["m"] = m
                ch["x"] = -jnp.where(m["strict"], m["beta"] * ch["kk"] * m["dm"], 0.0)
                ch["t"] = eye + ch["x"]
            for ch in chains:
                ch["pw"] = _bdot(ch["x"], ch["x"], 1, 0)
            for level in range(1, 6):
                last_level = level == 5
                for ch in chains:
                    rhs = ch["t"] if last_level else jnp.concatenate([ch["t"], ch["pw"]], axis=1)
                    ch["prod"] = _bdot(ch["pw"], rhs, 1, 0)
                for ch in chains:
                    ch["t"] = ch["t"] + ch["prod"][:, 0:CHUNK]
                    if not last_level:
                        ch["pw"] = ch["prod"][:, CHUNK:2 * CHUNK]
            for ch in chains:
                m = ch["m"]
                rhs = jnp.concatenate([ch["vh"] * m["beta"], ch["kh"] * (m["beta"] * m["eg"])], axis=1)
                ch["uw"] = _bdot(ch["t"], rhs, 1, 0)
            for ch in chains:
                u_ref, w_ref, a_ref, t_ref, qd_ref, kd_ref = outs[ch["d"]]
                m = ch["m"]
                c, rows = ch["c"], ch["rows"]
                col = ch["d"] * GDN_H + ch["h"]
                u_ref[rows, ch["cols"]] = ch["uw"][:, 0:GDN_DK]
                w_ref[rows, ch["cols"]] = ch["uw"][:, GDN_DK:2 * GDN_DK].astype(BF16)
                a_ref[c, ch["h"]] = (ch["qk"] * m["dm"]).astype(BF16)
                t_ref[c, ch["h"]] = _bdot(ch["t"], eye, 0, 0).astype(BF16)
                qd_ref[rows, ch["cols"]] = (ch["qh"] * m["eg"]).astype(BF16)
                kd_ref[rows, ch["cols"]] = (ch["kh"] * m["egl"]).astype(BF16)
                cd_ref[c, col:col + 1, :] = jnp.broadcast_to(m["cd"], (1, 128))
            return carry

        lax.fori_loop(0, ncb // LOCAL_CHUNKS, chunk, 0)

    im = lambda i: (i, 0)
    im4 = lambda i: (i, 0, 0, 0)
    ins = [(q, (ts, GDN_W), im), (k, (ts, GDN_W), im), (v, (ts, GDN_W), im), (bg, (ts, 128), im),
           (gcr, (ncb, 8, CHUNK), lambda i: (i, 0, 0))]
    per_dir = [(jax.ShapeDtypeStruct((S, GDN_W), F32), (ts, GDN_W), im),
               (jax.ShapeDtypeStruct((S, GDN_W), BF16), (ts, GDN_W), im),
               (jax.ShapeDtypeStruct((nch, GDN_H, CHUNK, CHUNK), BF16), (ncb, GDN_H, CHUNK, CHUNK), im4),
               (jax.ShapeDtypeStruct((nch, GDN_H, CHUNK, CHUNK), BF16), (ncb, GDN_H, CHUNK, CHUNK), im4),
               (jax.ShapeDtypeStruct((S, GDN_W), BF16), (ts, GDN_W), im),
               (jax.ShapeDtypeStruct((S, GDN_W), BF16), (ts, GDN_W), im)]
    cd_out = (jax.ShapeDtypeStruct((nch, 8, 128), F32), (ncb, 8, 128), lambda i: (i, 0, 0))
    res = _rows("gdn_local_fwd", S, ts, ins, per_dir * 2 + [cd_out], body)
    return res[0:6], res[6:12], res[12]


def _gdn_scan_fwd(loc):
    S = loc[0][0].shape[0]
    ts = _tile(S, GDN_TS)
    n_tiles = S // ts
    ncb = ts // CHUNK
    nch = S // CHUNK

    def body(*refs):
        ins = (refs[0:6], refs[6:12])
        outs = (refs[12:15], refs[15:18])
        state = refs[18]

        @pl.when(pl.program_id(0) == 0)
        def _():
            state[...] = jnp.zeros_like(state)

        def chunk(cc, carry):
            chains = []
            for d in range(2):
                c = cc if d == 0 else ncb - 1 - cc
                rows = pl.ds(pl.multiple_of(c * CHUNK, CHUNK), CHUNK)
                for h in range(GDN_H):
                    cols = slice(h * GDN_DK, (h + 1) * GDN_DK)
                    chains.append(dict(d=d, h=h, c=c, rows=rows, cols=cols, st=state[d * GDN_H + h]))
            for ch in chains:
                qd_ref, kd_ref, u_ref, w_ref, a_ref, cd_ref = ins[ch["d"]]
                rows, cols = ch["rows"], ch["cols"]
                lhs = jnp.concatenate([w_ref[rows, cols], qd_ref[rows, cols]], axis=0)
                ch["ws_qs"] = _dot(lhs, ch["st"].astype(BF16), 1, 0)
            for ch in chains:
                qd_ref, kd_ref, u_ref, w_ref, a_ref, cd_ref = ins[ch["d"]]
                rows, cols = ch["rows"], ch["cols"]
                vn = u_ref[rows, cols] - ch["ws_qs"][0:CHUNK]
                vnb = vn.astype(BF16)
                ch["vn"] = vn
                ch["avn"] = _dot(a_ref[ch["c"], ch["h"]], vnb, 1, 0)
                ch["kvn"] = _dot(kd_ref[rows, cols], vnb, 0, 0)
            for ch in chains:
                o_ref, vn_ref, s_ref = outs[ch["d"]]
                cd_ref = ins[ch["d"]][5]
                rows, cols = ch["rows"], ch["cols"]
                col = ch["d"] * GDN_H + ch["h"]
                o_ref[rows, cols] = ch["ws_qs"][CHUNK:2 * CHUNK] + ch["avn"]
                vn_ref[rows, cols] = ch["vn"].astype(BF16)
                s_ref[ch["c"], ch["h"]] = ch["st"].astype(BF16)
                state[ch["d"] * GDN_H + ch["h"]] = ch["st"] * cd_ref[ch["c"], col:col + 1, :] + ch["kvn"]
            return carry

        lax.fori_loop(0, ncb, chunk, 0)

    ins, outs = [], []
    for d in range(2):
        tix = _dir_tile(d, n_tiles, False)
        im = lambda i, tix=tix: (tix(i), 0)
        im4 = lambda i, tix=tix: (tix(i), 0, 0, 0)
        u, w, a, _, qd, kd = loc[d]
        ins += [(qd, (ts, GDN_W), im), (kd, (ts, GDN_W), im), (u, (ts, GDN_W), im), (w, (ts, GDN_W), im),
                (a, (ncb, GDN_H, CHUNK, CHUNK), im4), (loc[2], (ncb, 8, 128), lambda i, tix=tix: (tix(i), 0, 0))]
        outs += [(jax.ShapeDtypeStruct((S, GDN_W), F32), (ts, GDN_W), im),
                 (jax.ShapeDtypeStruct((S, GDN_W), BF16), (ts, GDN_W), im),
                 (jax.ShapeDtypeStruct((nch, GDN_H, GDN_DK, GDN_DK), BF16), (ncb, GDN_H, GDN_DK, GDN_DK), im4)]
    res = _rows("gdn_scan_fwd", S, ts, ins, outs, body, scratch=[pltpu.VMEM((2 * GDN_H, GDN_DK, GDN_DK), F32)])
    return res[0:3], res[3:6]


def _gelu(x):
    c = math.sqrt(2.0 / math.pi)
    t = jnp.tanh(c * (x + 0.044715 * x * x * x))
    return 0.5 * x * (1.0 + t), t


def _mix_out_fwd(h_f, h_b, o_f, o_b, p, gn):
    S = h_f.shape[0]
    ts = _tile(S, 512)

    def body(hf, hb, of, ob, gate, z, gn_ref, y_ref):
        ge, _ = _gelu(gate[...].astype(F32))
        y_ref[:, 0:RG_W] = ((hf[...] + hb[...]) * ge).astype(BF16)
        o = of[...] + ob[...]
        zv = z[...].astype(F32)
        sz = zv * _sigmoid(zv)
        for h in range(GDN_H):
            cols = slice(h * GDN_DK, (h + 1) * GDN_DK)
            oh = o[:, cols]
            n = oh * lax.rsqrt(jnp.mean(oh * oh, axis=-1, keepdims=True) + EPS) * gn_ref[...]
            y_ref[:, RG_W + h * GDN_DK:RG_W + (h + 1) * GDN_DK] = (n * sz[:, cols]).astype(BF16)

    blk = (ts, RG_W)
    im = lambda i: (i, 0)
    ins = [(h_f, blk, im), (h_b, blk, im), (o_f, blk, im), (o_b, blk, im),
           (p, blk, lambda i: (i, 1)), (p, blk, lambda i: (i, 5)), (gn, (1, GDN_DK), lambda i: (0, 0))]
    return _rows("mix_out_fwd", S, ts, ins,
                 [(jax.ShapeDtypeStruct((S, D_MODEL), BF16), (ts, D_MODEL), im)], body)[0]


def _block_diag(w):
    n = w.shape[0]
    return jnp.einsum("nij,nm->nimj", w, jnp.eye(n, dtype=w.dtype)).reshape(n * w.shape[1], n * w.shape[2])


def _rg_bd(a_w, x_w):
    return jnp.concatenate([_block_diag(a_w[0]), _block_diag(x_w[0]), _block_diag(a_w[1]), _block_diag(x_w[1])],
                           axis=1).astype(BF16)


def _rg_prm(ba, bx, lam):
    return jnp.concatenate([ba[0:1], bx[0:1], ba[1:2], bx[1:2], lam, jnp.zeros((2, RG_W), F32)], axis=0)


def _gdn_prm(a_log, dt_bias):
    rows = jnp.zeros((8, 128), F32)
    rows = rows.at[0, 8:16].set(a_log.reshape(-1))
    return rows.at[1, 8:16].set(dt_bias.reshape(-1))


def _gc_rows(bg):
    S = bg.shape[0]
    return bg[:, 8:16].reshape(S // CHUNK, CHUNK, 8).transpose(0, 2, 1)


def _layer_fwd(x0, target, W, more=None):
    S = x0.shape[0]
    R = {}
    R["h1"] = _rmsnorm_fwd("rms1", x0, W["ffn1_norm"])
    if more is not None:
        W = {**W, **more("ffn1_up", R["h1"])}
    late_wd = {}

    def ffn1_wd(after):
        late_wd.update(more("ffn1_down", after))
        return late_wd["ffn1_w_down"]

    sd_x = jax.ShapeDtypeStruct((S, D_MODEL), F32)
    sd_h = jax.ShapeDtypeStruct((S, D_MODEL), BF16)

    def norm_after(gain):
        extras = lambda t: [(gain, (1, D_MODEL), _row0)]
        outs = lambda t: [(sd_x, (t, D_MODEL), _mn), (sd_h, (t, D_MODEL), _mn)]

        def finish(i, xo, ex, out):
            out[0][...] = xo
            out[1][...] = _rmsnorm_tile(xo, ex[0][...])

        return extras, outs, finish

    (R["x1"], R["h2"]), R["a1"], R["b1"], R["f1"] = _ffn_fwd(
        "ffn1", x0, R["h1"], W["ffn1_w_gate"], W["ffn1_w_up"], ffn1_wd if more is not None else W["ffn1_w_down"],
        *norm_after(W["mix_norm"]))
    if more is not None:
        W = {**W, **late_wd, **more("mixer", R["x1"])}
    tm = _tile(S, 512)
    tmp = _tile(S, 1024)
    tmp = _tile(S, 512)
    R["p"] = _fused_mm("in_proj", S, D_IN_PAD, D_MODEL, tmp, D_IN_PAD, D_MODEL, [(R["h2"], "mk"), (W["w_in"], "nk")],
                       [(0, 1, 0)], [], [(jax.ShapeDtypeStruct((S, D_IN_PAD), BF16), (tmp, D_IN_PAD), _mn)],
                       lambda i, accs, ex, out: out[0].__setitem__(Ellipsis, accs[0][...].astype(BF16)))[0]
    p = R["p"]
    R["xc"] = _conv_fwd("rg_conv_fwd", p, 0, W["rg_conv_w"], W["rg_conv_b"], "bias")
    R["bd"] = _rg_bd(W["rg_gate_a_w"], W["rg_gate_x_w"])
    R["rg_prm"] = _rg_prm(W["rg_gate_a_b"], W["rg_gate_x_b"], W["rg_lambda"])
    a_f, b_f, a_b, b_b = _rg_gates_fwd(R["xc"], R["bd"], R["rg_prm"])
    R["a_f"], R["a_b"] = a_f, a_b
    R["h_f"], R["h_b"] = _rg_scan("rg_scan_fwd", a_f, b_f, a_b, b_b)
    zero_b = jnp.zeros((1, RG_W), F32)
    cw = W["gdn_conv_w"]
    R["q"] = _conv_fwd("gdn_conv_q", p, 2, cw[:, 0:512], zero_b, "q")
    R["k"] = _conv_fwd("gdn_conv_k", p, 3, cw[:, 512:1024], zero_b, "k")
    R["v"] = _conv_fwd("gdn_conv_v", p, 4, cw[:, 1024:1536], zero_b, "v")
    R["gdn_prm"] = _gdn_prm(W["gdn_a_log"], W["gdn_dt_bias"])
    R["bg"] = _gdn_prep_fwd(p, R["gdn_prm"])
    R["gcr"] = _gc_rows(R["bg"])
    R["gdn_loc"] = _gdn_local_fwd(R["q"], R["k"], R["v"], R["bg"], R["gcr"])
    R["gdn_fwd"] = _gdn_scan_fwd(R["gdn_loc"])
    R["o_f"], R["o_b"] = R["gdn_fwd"][0][0], R["gdn_fwd"][1][0]
    R["y"] = _mix_out_fwd(R["h_f"], R["h_b"], R["o_f"], R["o_b"], p, W["gdn_norm"])
    def epi_out(i, accs, ex, out):
        x2 = ex[0][...] + accs[0][...]
        out[0][...] = x2
        out[1][...] = _rmsnorm_tile(x2, ex[1][...])

    R["x2"], R["h3"] = _fused_mm("out_proj", S, D_MODEL, D_MODEL, tm, D_MODEL, D_MODEL,
                                 [(R["y"], "mk"), (W["w_out"], "kn")], [(0, 1, 0)],
                                 [(R["x1"], (tm, D_MODEL), _mn), (W["ffn2_norm"], (1, D_MODEL), _row0)],
                                 [(sd_x, (tm, D_MODEL), _mn), (sd_h, (tm, D_MODEL), _mn)], epi_out)
    if more is not None:
        W = {**W, **more("ffn2", R["x2"])}

    def loss_finish(i, xo, ex, out):
        gv = ex[1][...]
        r = lax.rsqrt(jnp.mean(xo * xo, axis=-1, keepdims=True) + EPS)
        err = xo * r * gv - ex[0][...]
        dx, dgt = _rmsnorm_bwd_tile(err * (1.0 / D_MODEL), xo, gv)
        out[0][...] = dx
        _colsum_into(out[1], i, jnp.zeros((8, 128), F32) + jnp.sum(err * err) * (0.5 / D_MODEL))
        _colsum_into(out[2], i, jnp.sum(dgt, axis=0, keepdims=True))
        out[3][...] = dx.astype(BF16)

    (R["dx3"], R["loss"], R["d_final_norm"], R["dx3_b"]), R["a2"], R["b2"], R["f2"] = _ffn_fwd(
        "ffn2", R["x2"], R["h3"], W["ffn2_w_gate"], W["ffn2_w_up"], W["ffn2_w_down"],
        lambda t: [(target, (t, D_MODEL), _mn), (W["final_norm"], (1, D_MODEL), _row0)],
        lambda t: [(sd_x, (t, D_MODEL), _mn), (jax.ShapeDtypeStruct((8, 128), F32), (8, 128), _row0),
                   (jax.ShapeDtypeStruct((1, D_MODEL), F32), (1, D_MODEL), _row0), (sd_h, (t, D_MODEL), _mn)],
        loss_finish)
    R["W"] = W
    return R


def _colsum_into(ref, i, val):
    @pl.when(i == 0)
    def _():
        ref[...] = val

    @pl.when(i > 0)
    def _():
        ref[...] += val


def _ffn_bwd(tag, dout, dout_b, x, g, h, a, b, f, wg, wu, wd, emit):
    S = x.shape[0]
    tm = _tile(S, 512)
    tk_s = _tile(S, 1024)
    dwd = _fused_mm(f"{tag}_dw_down", D_FF, D_MODEL, S, 1408, D_MODEL, tk_s, [(f, "km"), (dout_b, "kn")], [(0, 1, 0)], [],
                    [(jax.ShapeDtypeStruct((D_FF, D_MODEL), BF16), (1408, D_MODEL), _mn)],
                    lambda i, accs, ex, out: out[0].__setitem__(Ellipsis, (0.5 * accs[0][...]).astype(BF16)))[0]
    emit(down=dwd)

    def epi_act(i, accs, ex, out):
        df = 0.5 * accs[0][...]
        out[0][...] = (df * ex[1][...].astype(F32)).astype(BF16)
        out[1][...] = (df * ex[0][...].astype(F32)).astype(BF16)

    sds = jax.ShapeDtypeStruct((S, D_FF), BF16)
    da, db = _fused_mm(f"{tag}_dact", S, D_FF, D_MODEL, tm, 1408, D_MODEL, [(dout_b, "mk"), (wd, "nk")], [(0, 1, 0)],
                       [(a, (tm, 1408), _mn), (b, (tm, 1408), _mn)], [(sds, (tm, 1408), _mn)] * 2, epi_act)

    def epi_w2(i, accs, ex, out):
        out[0][...] = accs[0][...].astype(BF16)
        out[1][...] = accs[1][...].astype(BF16)

    sdw = jax.ShapeDtypeStruct((D_MODEL, D_FF), BF16)
    dwg, dwu = _fused_mm(f"{tag}_dw_up", D_MODEL, D_FF, S, D_MODEL, 1408, tk_s,
                         [(h, "km"), (da, "kn"), (db, "kn")], [(0, 1, 0), (0, 2, 1)], [],
                         [(sdw, (D_MODEL, 1408), _mn)] * 2, epi_w2)
    tok = emit(gate=dwg, up=dwu)
    if tok is not None:
        g = g + tok

    def epi_dx(i, accs, ex, out):
        dx, dgt = _rmsnorm_bwd_tile(accs[0][...], ex[0][...], ex[1][...])
        out[0][...] = ex[2][...] + dx
        _colsum_into(out[1], i, jnp.sum(dgt, axis=0, keepdims=True))

    tmx = _tile(S, 1024)
    dx, dg = _fused_mm(f"{tag}_dx", S, D_MODEL, D_FF, tmx, D_MODEL, 1408,
                       [(da, "mk"), (wg, "kn"), (db, "mk"), (wu, "kn")], [(0, 1, 0), (2, 3, 0)],
                       [(x, (tmx, D_MODEL), _mn), (g, (1, D_MODEL), _row0), (dout, (tmx, D_MODEL), _mn)],
                       [(jax.ShapeDtypeStruct((S, D_MODEL), F32), (tmx, D_MODEL), _mn),
                        (jax.ShapeDtypeStruct((1, D_MODEL), F32), (1, D_MODEL), _row0)], epi_dx)
    return dx, dg


def _mix_out_bwd(dx2, w_out, h_f, h_b, o_f, o_b, p, gn):
    S = dx2.shape[0]
    ts = _tile(S, 512)
    c0 = math.sqrt(2.0 / math.pi)

    def epi(i, accs, ex, out):
        hf, hb, of, ob, gate, z, gn_ref = ex
        dhr_ref, dgate_ref, do_ref, dz_ref, dgn_ref = out
        dy_ref = accs[0]
        gv = gate[...].astype(F32)
        ge, t = _gelu(gv)
        dy_rg = dy_ref[:, 0:RG_W]
        dhr_ref[...] = dy_rg * ge
        dgelu = 0.5 * (1.0 + t) + 0.5 * gv * (1.0 - t * t) * c0 * (1.0 + 3.0 * 0.044715 * gv * gv)
        dgate_ref[...] = (dy_rg * (hf[...] + hb[...]) * dgelu).astype(BF16)
        o = of[...] + ob[...]
        zv = z[...].astype(F32)
        sig = _sigmoid(zv)
        gnv = gn_ref[...]
        dgn = jnp.zeros((1, GDN_DK), F32)
        for h in range(GDN_H):
            cols = slice(h * GDN_DK, (h + 1) * GDN_DK)
            oh = o[:, cols]
            r = lax.rsqrt(jnp.mean(oh * oh, axis=-1, keepdims=True) + EPS)
            ohat = oh * r
            dyh = dy_ref[:, RG_W + h * GDN_DK:RG_W + (h + 1) * GDN_DK]
            zh = zv[:, cols]
            sh = sig[:, cols]
            dn = dyh * zh * sh
            dz_ref[:, cols] = (dyh * ohat * gnv * (sh * (1.0 + zh * (1.0 - sh)))).astype(BF16)
            dxn = dn * gnv
            do_ref[:, cols] = r * (dxn - ohat * jnp.mean(dxn * ohat, axis=-1, keepdims=True))
            dgn = dgn + jnp.sum(dn * ohat, axis=0, keepdims=True)
        _colsum_into(dgn_ref, i, dgn)

    blk = (ts, RG_W)
    im = lambda i, j: (i, 0)
    extras = [(h_f, blk, im), (h_b, blk, im), (o_f, blk, im), (o_b, blk, im),
              (p, blk, lambda i, j: (i, 1)), (p, blk, lambda i, j: (i, 5)), (gn, (1, GDN_DK), _row0)]
    outs = [(jax.ShapeDtypeStruct((S, RG_W), F32), blk, im), (jax.ShapeDtypeStruct((S, RG_W), BF16), blk, im),
            (jax.ShapeDtypeStruct((S, RG_W), F32), blk, im), (jax.ShapeDtypeStruct((S, RG_W), BF16), blk, im),
            (jax.ShapeDtypeStruct((1, GDN_DK), F32), (1, GDN_DK), _row0)]
    return _fused_mm("mix_out_bwd", S, D_MODEL, D_MODEL, ts, D_MODEL, D_MODEL, [(dx2, "mk"), (w_out, "nk")], [(0, 1, 0)],
                     extras, outs, epi)


def _rg_scan_adj(name, a_up, b_up, a_dn, b_dn):
    S, C = a_up.shape
    ts = _tile(S, 512)
    n_tiles = S // ts

    def body(au, bu, ad, bd, mu_ref, lam_ref, carry):
        @pl.when(pl.program_id(0) == 0)
        def _():
            carry[...] = jnp.zeros_like(carry)

        n_sub = ts // SUBLANES
        rows = lax.broadcasted_iota(jnp.int32, (SUBLANES, C), 0)

        def half(a_ref, b_ref, out_ref, r0, c_in, reverse):
            a = a_ref[pl.ds(r0, SUBLANES), :]
            b = b_ref[pl.ds(r0, SUBLANES), :]
            cum_a, c0 = _scan_rows(a, a * b, reverse)
            c = c0 + cum_a * c_in
            edge = 0 if not reverse else SUBLANES - 1
            c_prev = jnp.where(rows == edge, c_in, pltpu.roll(c, SUBLANES - 1 if reverse else 1, 0))
            out_ref[pl.ds(r0, SUBLANES), :] = b + c_prev
            return c[0:1, :] if reverse else c[SUBLANES - 1:SUBLANES, :]

        def step(j, c):
            cu, cd = c
            cu = half(au, bu, mu_ref, pl.multiple_of(j * SUBLANES, SUBLANES), cu, False)
            cd = half(ad, bd, lam_ref, pl.multiple_of((n_sub - 1 - j) * SUBLANES, SUBLANES), cd, True)
            return cu, cd

        cu, cd = lax.fori_loop(0, n_sub, step, (carry[0:1, :], carry[1:2, :]), unroll=4)
        carry[0:1, :] = cu
        carry[1:2, :] = cd

    fw = lambda i: (i, 0)
    bw = lambda i: (n_tiles - 1 - i, 0)
    sds = jax.ShapeDtypeStruct((S, C), F32)
    return _rows(name, S, ts,
                 [(a_up, (ts, C), fw), (b_up, (ts, C), fw), (a_dn, (ts, C), bw), (b_dn, (ts, C), bw)],
                 [(sds, (ts, C), fw), (sds, (ts, C), bw)], body, scratch=[pltpu.VMEM((8, C), F32)])


def _halo_ex(arr, S, tm, width):
    per = tm // HALO
    last = S // HALO - 1
    return [
        (arr, (tm, width), lambda i, j: (i, 0)),
        (arr, (HALO, width), lambda i, j: (jnp.maximum(i * per - 1, 0), 0)),
        (arr, (HALO, width), lambda i, j: (jnp.minimum((i + 1) * per, last), 0)),
    ]


def _rg_gates_bwd(xc, bd, prm, lam_f, lam_b, h_f, h_b):
    S = xc.shape[0]
    tm = _tile(S, 256)
    n_tiles = S // tm

    def epi(i, accs, ex, out):
        pre = accs[0][...]
        xv = ex[0][...]
        prm_ref = ex[1]
        lams = (ex[2][...], ex[3][...])
        hprev = (_shift(_ext(ex[4], ex[5], ex[6], i, n_tiles), -1, tm),
                 _shift(_ext(ex[7], ex[8], ex[9], i, n_tiles), 1, tm))
        dxc = jnp.zeros_like(xv)
        rows = []
        dlam_rows = []
        for d in range(2):
            r, ig, sp, a, sq = _rg_gate_terms(pre, xv, prm_ref, d)
            lam = lams[d]
            da = lam * hprev[d]
            di = lam * sq * xv
            dxc = dxc + lam * sq * ig
            dsq = lam * ig * xv
            dlog_a = da * a - dsq * (a * a) / sq
            dpre_r = dlog_a * (-RG_C * sp) * r * (1.0 - r)
            dpre_i = di * ig * (1.0 - ig)
            out[0][:, d * 1024:d * 1024 + RG_W] = dpre_r.astype(BF16)
            out[0][:, d * 1024 + RG_W:(d + 1) * 1024] = dpre_i.astype(BF16)
            rows += [jnp.sum(dpre_r, axis=0, keepdims=True), jnp.sum(dpre_i, axis=0, keepdims=True)]
            dsp = jnp.sum(dlog_a * (-RG_C * r), axis=0, keepdims=True)
            dlam_rows.append(-dsp * _sigmoid(-prm_ref[4 + d:5 + d, :]))
        out[1][...] = dxc + _dot(out[0][...], ex[10][...], 1, 1)
        zero = jnp.zeros((2, RG_W), F32)
        _colsum_into(out[2], i, jnp.concatenate(rows + dlam_rows + [zero], axis=0))

    blk = (tm, RG_W)
    im = lambda i, j: (i, 0)
    extras = ([(xc, blk, im), (prm, (8, RG_W), _row0), (lam_f, blk, im), (lam_b, blk, im)]
              + _halo_ex(h_f, S, tm, RG_W) + _halo_ex(h_b, S, tm, RG_W) + [(bd, (RG_W, 4 * RG_W), _row0)])
    outs = [(jax.ShapeDtypeStruct((S, 4 * RG_W), BF16), (tm, 4 * RG_W), im),
            (jax.ShapeDtypeStruct((S, RG_W), F32), blk, im),
            (jax.ShapeDtypeStruct((8, RG_W), F32), (8, RG_W), _row0)]
    return _fused_mm("rg_gates_bwd", S, 4 * RG_W, RG_W, tm, 4 * RG_W, RG_W, [(xc, "mk"), (bd, "kn")], [(0, 1, 0)],
                     extras, outs, epi)


def _roll_rows(ext, off):
    if off == 0:
        return ext
    return pltpu.roll(ext, (-off) % ext.shape[0], 0)


def _conv_bwd(name, p, colblk, w, grads, mode):
    S = p.shape[0]
    ts = _tile(S, 512)
    n_tiles = S // ts
    C = w.shape[1]
    ng = len(grads)

    def body(*refs):
        p_refs = refs[0:3]
        g_refs = refs[3:3 + 3 * ng]
        w_ref = refs[3 + 3 * ng]
        dx_ref, dw_ref, db_ref = refs[4 + 3 * ng:]
        i = pl.program_id(0)
        ext_p = _ext(*p_refs, i, n_tiles)
        dn = _ext(*g_refs[0:3], i, n_tiles)
        for gi in range(1, ng):
            dn = dn + _ext(*g_refs[3 * gi:3 * gi + 3], i, n_tiles)
        if mode == "bias":
            dc = dn
        else:
            c = None
            for j in range(CONV_W):
                term = w_ref[j:j + 1, :] * _roll_rows(ext_p, j - 2)
                c = term if c is None else c + term
            sig = _sigmoid(c)
            s = c * sig
            if mode in ("q", "k"):
                scale = GDN_DK ** -0.5 if mode == "q" else 1.0
                parts = []
                for h in range(GDN_H):
                    cols = slice(h * GDN_DK, (h + 1) * GDN_DK)
                    sh = s[:, cols]
                    dnh = dn[:, cols]
                    rinv = lax.rsqrt(jnp.sum(sh * sh, axis=-1, keepdims=True) + EPS)
                    parts.append(scale * rinv * (dnh - sh * (rinv * rinv) * jnp.sum(dnh * sh, axis=-1, keepdims=True)))
                ds = jnp.concatenate(parts, axis=-1)
            else:
                ds = dn
            dc = ds * (sig * (1.0 + c * (1.0 - sig)))
        dx = None
        for j in range(CONV_W):
            term = w_ref[j:j + 1, :] * _shift(dc, 2 - j, ts)
            dx = term if dx is None else dx + term
        dx_ref[...] = dx.astype(BF16)
        dc_main = dc[HALO:HALO + ts]
        dw = jnp.concatenate([jnp.sum(dc_main * _shift(ext_p, j - 2, ts), axis=0, keepdims=True)
                              for j in range(CONV_W)], axis=0)
        _colsum_into(dw_ref, i, dw)
        _colsum_into(db_ref, i, jnp.sum(dc_main, axis=0, keepdims=True))

    ins = _halo_ins(p, S, ts, C, colblk)
    for garr in grads:
        ins += _halo_ins(garr, S, ts, C, 0)
    ins += [(w, (CONV_W, C), lambda i: (0, 0))]
    z0 = lambda i: (0, 0)
    outs = [(jax.ShapeDtypeStruct((S, C), BF16), (ts, C), lambda i: (i, 0)),
            (jax.ShapeDtypeStruct((CONV_W, C), F32), (CONV_W, C), z0),
            (jax.ShapeDtypeStruct((1, C), F32), (1, C), z0)]
    return _rows(name, S, ts, ins, outs, body)


def _gdn_scan_bwd(loc, do):
    S = do.shape[0]
    ts = _tile(S, GDN_TS)
    n_tiles = S // ts
    ncb = ts // CHUNK
    nch = S // CHUNK

    def body(*refs):
        ins = (refs[0:6], refs[6:12])
        outs = (refs[12:14], refs[14:16])
        dstate = refs[16]

        @pl.when(pl.program_id(0) == 0)
        def _():
            dstate[...] = jnp.zeros_like(dstate)

        def chunk(cc, carry):
            chains = []
            for d in range(2):
                c = ncb - 1 - cc if d == 0 else cc
                rows = pl.ds(pl.multiple_of(c * CHUNK, CHUNK), CHUNK)
                for h in range(GDN_H):
                    cols = slice(h * GDN_DK, (h + 1) * GDN_DK)
                    chains.append(dict(d=d, h=h, c=c, rows=rows, cols=cols, dsn=dstate[d * GDN_H + h]))
            for ch in chains:
                qd_ref, kd_ref, cd_ref, w_ref, a_ref, do_ref = ins[ch["d"]]
                rows, cols = ch["rows"], ch["cols"]
                dob = do_ref[rows, cols].astype(BF16)
                ch["dvn"] = (_dot(a_ref[ch["c"], ch["h"]], dob, 0, 0)
                             + _dot(kd_ref[rows, cols], ch["dsn"].astype(BF16), 1, 0))
                ch["qdo"] = _dot(qd_ref[rows, cols], dob, 0, 0)
            for ch in chains:
                w_ref = ins[ch["d"]][3]
                ch["wdvn"] = _dot(w_ref[ch["rows"], ch["cols"]], ch["dvn"].astype(BF16), 0, 0)
            for ch in chains:
                dvn_ref, ds_ref = outs[ch["d"]]
                cd_ref = ins[ch["d"]][2]
                col = ch["d"] * GDN_H + ch["h"]
                dvn_ref[ch["rows"], ch["cols"]] = ch["dvn"].astype(BF16)
                ds_ref[ch["c"], ch["h"]] = ch["dsn"].astype(BF16)
                dstate[ch["d"] * GDN_H + ch["h"]] = (ch["qdo"] + cd_ref[ch["c"], col:col + 1, :] * ch["dsn"]
                                                     - ch["wdvn"])
            return carry

        lax.fori_loop(0, ncb, chunk, 0)

    ins, outs = [], []
    for d in range(2):
        tix = _dir_tile(d, n_tiles, True)
        im = lambda i, tix=tix: (tix(i), 0)
        im4 = lambda i, tix=tix: (tix(i), 0, 0, 0)
        _, w, a, _, qd, kd = loc[d]
        ins += [(qd, (ts, GDN_W), im), (kd, (ts, GDN_W), im), (loc[2], (ncb, 8, 128), lambda i, tix=tix: (tix(i), 0, 0)),
                (w, (ts, GDN_W), im), (a, (ncb, GDN_H, CHUNK, CHUNK), im4), (do, (ts, GDN_W), im)]
        outs += [(jax.ShapeDtypeStruct((S, GDN_W), BF16), (ts, GDN_W), im),
                 (jax.ShapeDtypeStruct((nch, GDN_H, GDN_DK, GDN_DK), BF16), (ncb, GDN_H, GDN_DK, GDN_DK), im4)]
    res = _rows("gdn_scan_bwd", S, ts, ins, outs, body, scratch=[pltpu.VMEM((2 * GDN_H, GDN_DK, GDN_DK), F32)])
    return res[0:2], res[2:4]


def _gdn_local_bwd(q, k, v, bg, gcr, do, loc, fwd, adj):
    S = q.shape[0]
    ts = _tile(S, GDN_TS)
    ncb = ts // CHUNK

    def body(q_ref, k_ref, v_ref, bg_ref, gcr_ref, do_ref, *rest):
        per_dir = (rest[0:5], rest[5:10])
        dq_ref, dk_ref, dv_ref, dbg_ref, dbgr_ref = rest[10:15]
        ri, ci = _tri_masks()
        lane = lax.broadcasted_iota(jnp.int32, (CHUNK, 128), 1)
        rowi = lax.broadcasted_iota(jnp.int32, (CHUNK, 1), 0)
        ones8 = jnp.ones((SUBLANES, CHUNK), F32)

        def chunk(c, carry):
            r0 = pl.multiple_of(c * CHUNK, CHUNK)
            rows = pl.ds(r0, CHUNK)
            chains = []
            for h in range(GDN_H):
                cols = slice(h * GDN_DK, (h + 1) * GDN_DK)
                qh, kh, vh = q_ref[rows, cols], k_ref[rows, cols], v_ref[rows, cols]
                dob = do_ref[rows, cols].astype(BF16)
                both = _bdot(jnp.concatenate([qh, kh], axis=0), kh, 1, 1)
                for d in range(2):
                    chains.append(dict(h=h, d=d, cols=cols, qh=qh, kh=kh, vh=vh, dob=dob, qk=both[0:CHUNK],
                                       kk=both[CHUNK:2 * CHUNK], col=d * GDN_H + h))
            for ch in chains:
                m = _gdn_decay(bg_ref, gcr_ref, c, rows, r0, ch["col"], ch["d"] == 1, ri, ci)
                t_ref, s_ref, ds_ref, vn_ref, dvn_ref = per_dir[ch["d"]]
                h, cols = ch["h"], ch["cols"]
                ch["m"] = m
                ch["kb"] = ch["kh"] * m["beta"]
                ch["kbg"] = ch["kb"] * m["eg"]
                ch["t"] = t_ref[c, h]
                stb = s_ref[c, h]
                ch["dsn"] = ds_ref[c, h]
                vnb = vn_ref[rows, cols]
                dvnb = dvn_ref[rows, cols]
                ch["dcd"] = jnp.sum(jnp.sum(stb.astype(F32) * ch["dsn"].astype(F32), axis=1, keepdims=True),
                                    axis=0, keepdims=True)
                ch["dqd"] = _dot(ch["dob"], stb, 1, 1)
                ch["d_a"] = _dot(ch["dob"], vnb, 1, 1)
                ch["dkd"] = _bdot(vnb, ch["dsn"], 1, 1)
                ch["dw"] = -_dot(dvnb, stb, 1, 1)
                ch["dvb"] = _dot(ch["t"], dvnb, 1, 0)
                ch["d_t"] = _bdot(dvnb, ch["vh"] * m["beta"], 1, 1)
            for ch in chains:
                dwb = ch["dw"].astype(BF16)
                ch["d_t"] = ch["d_t"] + _bdot(dwb, ch["kbg"], 1, 1)
                ch["dkbg"] = _dot(ch["t"], dwb, 1, 0)
                ch["nn"] = ch["d_a"] * ch["m"]["dm"]
                ch["nn_q"] = _bdot(ch["nn"], ch["qh"], 0, 0)
                ch["nn_k"] = _bdot(ch["nn"], ch["kh"], 1, 0)
            for ch in chains:
                ch["x"] = _dot(ch["d_t"].astype(BF16), ch["t"], 1, 0)
            for ch in chains:
                d_l = -_dot(ch["t"], ch["x"].astype(BF16), 1, 0)
                ch["d_l"] = jnp.where(ch["m"]["strict"], d_l, 0.0)
                ch["mm"] = ch["d_l"] * ch["m"]["dm"]
            for ch in chains:
                m = ch["m"]
                ch["mm_kh"] = _bdot(ch["mm"], ch["kh"], 1, 0)
                ch["mm_kb"] = _bdot(ch["mm"], ch["kb"], 0, 0)
                l_mat = jnp.where(m["strict"], m["beta"] * ch["kk"] * m["dm"], 0.0)
                ch["e"] = ch["d_l"] * l_mat + ch["nn"] * ch["qk"]
                dbgr_ref[c, ch["col"]:ch["col"] + 1, :] = -_dot(ones8, ch["e"], 1, 0, HI)[0:1, :]
            acc_bg = jnp.zeros((CHUNK, 128), F32)
            acc = {}
            for ch in chains:
                m = ch["m"]
                beta, eg, egl = m["beta"], m["eg"], m["egl"]
                dkb = ch["mm_kh"] + ch["dkbg"] * eg
                dk_d = ch["mm_kb"] + ch["nn_q"] + ch["dkd"] * egl + dkb * beta
                dq_d = ch["nn_k"] + ch["dqd"] * eg
                dv_d = ch["dvb"] * beta
                dkd_kd = ch["dkd"] * (ch["kh"] * egl)
                dgc = (jnp.sum(ch["e"], axis=1, keepdims=True)
                       + jnp.sum(ch["dqd"] * (ch["qh"] * eg) - dkd_kd + ch["dkbg"] * ch["kbg"], axis=1, keepdims=True))
                dgl = jnp.sum(jnp.sum(dkd_kd, axis=1, keepdims=True), axis=0, keepdims=True) + ch["dcd"] * m["cd"]
                dgc = dgc + jnp.where(rowi == (0 if ch["d"] == 1 else CHUNK - 1), dgl, 0.0)
                dbeta = jnp.sum(dkb * ch["kh"] + ch["dvb"] * ch["vh"], axis=1, keepdims=True)
                acc_bg = acc_bg + jnp.where(lane == ch["col"], dbeta, 0.0) + jnp.where(lane == 8 + ch["col"], dgc, 0.0)
                if ch["d"] == 0:
                    acc[ch["h"]] = (dq_d, dk_d, dv_d)
                else:
                    dq0, dk0, dv0 = acc[ch["h"]]
                    dq_ref[rows, ch["cols"]] = dq0 + dq_d
                    dk_ref[rows, ch["cols"]] = dk0 + dk_d
                    dv_ref[rows, ch["cols"]] = dv0 + dv_d
            dbg_ref[rows, :] = acc_bg
            return carry

        lax.fori_loop(0, ncb, chunk, 0)

    im = lambda i: (i, 0)
    im4 = lambda i: (i, 0, 0, 0)
    blk = (ts, GDN_W)
    ins = [(q, blk, im), (k, blk, im), (v, blk, im), (bg, (ts, 128), im), (gcr, (ncb, 8, CHUNK), lambda i: (i, 0, 0)),
           (do, blk, im)]
    for d in range(2):
        ins += [(loc[d][3], (ncb, GDN_H, CHUNK, CHUNK), im4), (fwd[d][2], (ncb, GDN_H, GDN_DK, GDN_DK), im4),
                (adj[d][1], (ncb, GDN_H, GDN_DK, GDN_DK), im4), (fwd[d][1], blk, im), (adj[d][0], blk, im)]
    sds = jax.ShapeDtypeStruct((S, GDN_W), F32)
    outs = [(sds, blk, im), (sds, blk, im), (sds, blk, im), (jax.ShapeDtypeStruct((S, 128), F32), (ts, 128), im),
            (jax.ShapeDtypeStruct((S // CHUNK, 8, CHUNK), F32), (ncb, 8, CHUNK), lambda i: (i, 0, 0))]
    dq, dk, dv, dbg, dbg_rows = _rows("gdn_local_bwd", S, ts, ins, outs, body)
    dgc_cols = dbg_rows.transpose(0, 2, 1).reshape(S, 8)
    return dq, dk, dv, dbg + jnp.pad(dgc_cols, ((0, 0), (8, 112)))


def _gdn_prep_bwd(dbg_all, p, prm):
    S = p.shape[0]
    ts = _tile(S, 512)

    def body(dbg_ref, p_ref, prm_ref, dba_ref, dprm_ref):
        i = pl.program_id(0)
        raw = p_ref[...].astype(F32)
        dbg = dbg_ref[...]
        lane = lax.broadcasted_iota(jnp.int32, (1, 128), 1)
        is_g = (lane >= 8) & (lane < 16)
        ea = jnp.exp(prm_ref[0:1, :])
        arg = raw + prm_ref[1:2, :]
        g = jnp.where(is_g, -ea * _softplus(arg), 0.0)
        beta = _sigmoid(raw)
        dgc = jnp.where(is_g, dbg, 0.0)
        ri, ci = _tri_masks()
        lower = (ri >= ci).astype(F32)
        upper = (ri <= ci).astype(F32)
        dgs = []
        for c in range(ts // CHUNK):
            ch = dgc[c * CHUNK:(c + 1) * CHUNK]
            dgs.append(jnp.where(lane < 12, _dot(upper, ch, 1, 0, HI), _dot(lower, ch, 1, 0, HI)))
        dg = jnp.concatenate(dgs, axis=0)
        dalpha = jnp.where(is_g, dg * (-ea) * _sigmoid(arg), 0.0)
        dba_ref[...] = jnp.where(lane < 8, dbg * beta * (1.0 - beta), dalpha).astype(BF16)
        rows = jnp.concatenate([jnp.sum(dg * g, axis=0, keepdims=True), jnp.sum(dalpha, axis=0, keepdims=True),
                                jnp.zeros((6, 128), F32)], axis=0)
        _colsum_into(dprm_ref, i, rows)

    im = lambda i: (i, 0)
    z0 = lambda i: (0, 0)
    return _rows("gdn_prep_bwd", S, ts,
                 [(dbg_all, (ts, 128), im), (p, (ts, 128), lambda i: (i, COL_BA // 128)), (prm, (8, 128), z0)],
                 [(jax.ShapeDtypeStruct((S, 128), BF16), (ts, 128), im), (jax.ShapeDtypeStruct((8, 128), F32), (8, 128), z0)],
                 body)


def _mm_plain(name, M, N, K, tm, tn, tk, a, am, b, bm, dtype):
    return _fused_mm(name, M, N, K, tm, tn, tk, [(a, am), (b, bm)], [(0, 1, 0)], [],
                     [(jax.ShapeDtypeStruct((M, N), dtype), (tm, tn), _mn)],
                     lambda i, accs, ex, out: out[0].__setitem__(Ellipsis, accs[0][...].astype(dtype)))[0]


def _layer_bwd(x0, W, R, emit_big=None, emit_small=None):
    S = x0.shape[0]
    tm = _tile(S, 512)
    tk_s = _tile(S, 1024)
    G = {}

    def emit(**named):
        if emit_big is None:
            G.update(named)
            return None
        return emit_big(**named)

    def ffn_emit(prefix):
        return lambda **kw: emit(**{f"{prefix}_w_{k}": v for k, v in kw.items()})

    dx2, G["ffn2_norm"] = _ffn_bwd("ffn2b", R["dx3"], R["dx3_b"], R["x2"], W["ffn2_norm"], R["h3"], R["a2"], R["b2"], R["f2"],
                                   W["ffn2_w_gate"], W["ffn2_w_up"], W["ffn2_w_down"], ffn_emit("ffn2"))
    tok = emit(w_out=_mm_plain("dw_out", D_MODEL, D_MODEL, S, D_MODEL, D_MODEL, tk_s, R["y"], "km", dx2, "kn", BF16))
    gn = W["gdn_norm"] if tok is None else W["gdn_norm"] + tok
    p = R["p"]
    dhr, dgate, do, dz, G["gdn_norm"] = _mix_out_bwd(dx2, W["w_out"], R["h_f"], R["h_b"], R["o_f"], R["o_b"], p, gn)
    lam_b, lam_f = _rg_scan_adj("rg_scan_bwd", R["a_b"], dhr, R["a_f"], dhr)
    dpre, dxc, d_rgprm = _rg_gates_bwd(R["xc"], R["bd"], R["rg_prm"], lam_f, lam_b, R["h_f"], R["h_b"])
    d_bd = _mm_plain("rg_dbd", RG_W, 4 * RG_W, S, RG_W, 4 * RG_W, tk_s, R["xc"], "km", dpre, "kn", F32)
    dx_rg, G["rg_conv_w"], G["rg_conv_b"] = _conv_bwd("rg_conv_bwd", p, 0, W["rg_conv_w"], [dxc], "bias")
    blocks = jnp.einsum("nigmj,nm->gnij", d_bd.reshape(RG_BLOCKS, RG_BLOCK, 4, RG_BLOCKS, RG_BLOCK),
                        jnp.eye(RG_BLOCKS, dtype=F32))
    G["rg_gate_a_w"] = jnp.stack([blocks[0], blocks[2]])
    G["rg_gate_x_w"] = jnp.stack([blocks[1], blocks[3]])
    G["rg_gate_a_b"] = jnp.stack([d_rgprm[0], d_rgprm[2]])
    G["rg_gate_x_b"] = jnp.stack([d_rgprm[1], d_rgprm[3]])
    G["rg_lambda"] = d_rgprm[4:6]
    adj = _gdn_scan_bwd(R["gdn_loc"], do)
    dq, dk, dv, dbg = _gdn_local_bwd(R["q"], R["k"], R["v"], R["bg"], R["gcr"], do, R["gdn_loc"], R["gdn_fwd"], adj)
    cw = W["gdn_conv_w"]
    dpq, dwq, _ = _conv_bwd("gdn_conv_q_bwd", p, 2, cw[:, 0:512], [dq], "q")
    dpk, dwk, _ = _conv_bwd("gdn_conv_k_bwd", p, 3, cw[:, 512:1024], [dk], "k")
    dpv, dwv, _ = _conv_bwd("gdn_conv_v_bwd", p, 4, cw[:, 1024:1536], [dv], "v")
    G["gdn_conv_w"] = jnp.concatenate([dwq, dwk, dwv], axis=1)
    dba, d_gprm = _gdn_prep_bwd(dbg, p, R["gdn_prm"])
    G["gdn_a_log"] = d_gprm[0, 8:16].reshape(2, GDN_H)
    G["gdn_dt_bias"] = d_gprm[1, 8:16].reshape(2, GDN_H)
    dp = jnp.concatenate([dx_rg, dgate, dpq, dpk, dpv, dz, dba], axis=1)
    tok = emit(w_in=_mm_plain("dw_in", D_MODEL, D_IN_PAD, S, D_MODEL, 640, tk_s, R["h2"], "km", dp, "kn", BF16))
    g_mix = W["mix_norm"] if tok is None else W["mix_norm"] + tok

    def epi_dx1(i, accs, ex, out):
        dx, dgt = _rmsnorm_bwd_tile(accs[0][...], ex[0][...], ex[1][...])
        dx1_tile = ex[2][...] + dx
        out[0][...] = dx1_tile
        _colsum_into(out[1], i, jnp.sum(dgt, axis=0, keepdims=True))
        out[2][...] = dx1_tile.astype(BF16)

    dx1, G["mix_norm"], dx1_b = _fused_mm(
        "mix_dx", S, D_MODEL, D_IN_PAD, tm, D_MODEL, D_IN_PAD, [(dp, "mk"), (W["w_in"], "kn")], [(0, 1, 0)],
        [(R["x1"], (tm, D_MODEL), _mn), (g_mix, (1, D_MODEL), _row0), (dx2, (tm, D_MODEL), _mn)],
        [(jax.ShapeDtypeStruct((S, D_MODEL), F32), (tm, D_MODEL), _mn),
         (jax.ShapeDtypeStruct((1, D_MODEL), F32), (1, D_MODEL), _row0),
         (jax.ShapeDtypeStruct((S, D_MODEL), BF16), (tm, D_MODEL), _mn)], epi_dx1)
    G["final_norm"] = R["d_final_norm"]
    if emit_small is not None:
        emit_small(G)
    dx0, G["ffn1_norm"] = _ffn_bwd("ffn1b", dx1, dx1_b, x0, W["ffn1_norm"], R["h1"], R["a1"], R["b1"], R["f1"],
                                   W["ffn1_w_gate"], W["ffn1_w_up"], W["ffn1_w_down"], ffn_emit("ffn1"))
    return dx0, G


def _mesh_pos():
    x, y, c = lax.axis_index("x"), lax.axis_index("y"), lax.axis_index("c")
    return x, y, c, 4 * x + 2 * y + c


def _peer(x, y, c, r):
    px = 1 - x if r & 4 else x
    py = 1 - y if r & 2 else y
    pc = 1 - c if r & 1 else c
    return (px, py, pc), 4 * px + 2 * py + pc


_HBM = pl.BlockSpec(memory_space=pltpu.HBM)
_SEM = pl.BlockSpec(memory_space=pltpu.SEMAPHORE)


def _peer_copies(scatter, srcs, lands, send_sems, recv_sems):
    x, y, c, me = _mesh_pos()
    copies = []
    for a, (src, land) in enumerate(zip(srcs, lands)):
        for r in range(1, N_DEV):
            peer, peer_idx = _peer(x, y, c, r)
            copies.append(pltpu.make_async_remote_copy(
                src_ref=src.at[peer_idx] if scatter else src, dst_ref=land.at[r - 1] if scatter else land.at[me],
                send_sem=send_sems.at[a * 7 + r - 1], recv_sem=recv_sems.at[a * 7 + r - 1],
                device_id=peer, device_id_type=pl.DeviceIdType.MESH))
    return copies


def _exchange_start(name, scatter, arrays):
    slabs = arrays
    n = len(slabs)

    def body(*refs):
        srcs, lands = refs[0:n], refs[n:2 * n]
        send_sems, recv_sems = refs[2 * n], refs[2 * n + 1]
        token = refs[4 * n + 2]
        for cp in _peer_copies(scatter, srcs, lands, send_sems, recv_sems):
            cp.start()
        token[...] = jnp.zeros_like(token)

    land_shapes = [(N_DEV - 1,) + s.shape[1:] if scatter else (N_DEV,) + s.shape for s in slabs]
    n_sems = 7 * n
    out_shape = ([pltpu.SemaphoreType.DMA((n_sems,)), pltpu.SemaphoreType.DMA((n_sems,))]
                 + [pltpu.HBM(s.shape, s.dtype) for s in slabs]
                 + [pltpu.HBM(shp, s.dtype) for shp, s in zip(land_shapes, slabs)]
                 + [jax.ShapeDtypeStruct((8, 128), F32)])
    res = pl.pallas_call(
        body, name=name, out_shape=out_shape, in_specs=[_HBM] * (2 * n),
        out_specs=[_SEM, _SEM] + [_HBM] * (2 * n) + [pl.BlockSpec(memory_space=pltpu.VMEM)],
        input_output_aliases={i: 2 + i for i in range(2 * n)},
        compiler_params=pltpu.CompilerParams(has_side_effects=pltpu.SideEffectType.DATAFLOW_SIDE_EFFECTING),
    )(*[pltpu.with_memory_space_constraint(s, pltpu.HBM) for s in slabs],
      *[pltpu.with_memory_space_constraint(lax.empty(shp, s.dtype), pltpu.HBM) for shp, s in zip(land_shapes, slabs)])
    return dict(n=n, scatter=scatter, sems=res[0:2], srcs=res[2:2 + n], lands=res[2 + n:2 + 2 * n],
                token=res[2 + 2 * n][0, 0])


def _exchange_wait(name, started, after):
    n = started["n"]
    scatter = started["scatter"]

    def body(*refs):
        srcs, lands = refs[0:n], refs[n:2 * n]
        send_sems, recv_sems = refs[2 * n], refs[2 * n + 1]
        for cp in _peer_copies(scatter, srcs, lands, send_sems, recv_sems):
            cp.wait_send()
            cp.wait_recv()

    arrays = list(started["srcs"]) + list(started["lands"])
    res = pl.pallas_call(
        body, name=name, out_shape=[pltpu.HBM(a.shape, a.dtype) for a in arrays],
        in_specs=[_HBM] * (2 * n) + [_SEM, _SEM, pl.BlockSpec(memory_space=pl.ANY)], out_specs=[_HBM] * (2 * n),
        input_output_aliases={i: i for i in range(2 * n)},
        compiler_params=pltpu.CompilerParams(has_side_effects=pltpu.SideEffectType.DATAFLOW_SIDE_EFFECTING),
    )(*arrays, *started["sems"], after)
    return res[0:n], res[n:2 * n]


def _adamw_math(w, g, m, v):
    m2 = ADAM_B1 * m + (1.0 - ADAM_B1) * g
    v2 = ADAM_B2 * v + (1.0 - ADAM_B2) * (g * g)
    m_hat = m2 / (1.0 - ADAM_B1 ** ADAM_STEP)
    v_hat = v2 / (1.0 - ADAM_B2 ** ADAM_STEP)
    delta = -ADAM_LR * (m_hat / (jnp.sqrt(v_hat) + ADAM_EPS) + ADAM_WD * w)
    return delta, m2, v2


def _adamw_slabs(name, src, land, me, w, m, v, tr):
    R, C = w.shape

    def body(me_ref, own_ref, land_ref, w_ref, m_ref, v_ref, g_ref, d_ref, m2_ref, v2_ref):
        g = own_ref[0].astype(F32)
        for s in range(N_DEV - 1):
            g = g + land_ref[s].astype(F32)
        delta, m2, v2 = _adamw_math(w_ref[...], g, m_ref[...], v_ref[...])
        g_ref[...] = g
        d_ref[...] = delta
        m2_ref[...] = m2
        v2_ref[...] = v2

    im = lambda i, me_ref: (i, 0)
    grid_spec = pltpu.PrefetchScalarGridSpec(
        num_scalar_prefetch=1, grid=(R // tr,),
        in_specs=[pl.BlockSpec((1, tr, C), lambda i, me_ref: (me_ref[0], i, 0)),
                  pl.BlockSpec((N_DEV - 1, tr, C), lambda i, me_ref: (0, i, 0)),
                  pl.BlockSpec((tr, C), im), pl.BlockSpec((tr, C), im), pl.BlockSpec((tr, C), im)],
        out_specs=[pl.BlockSpec((tr, C), im)] * 4)
    return pl.pallas_call(body, name=name, grid_spec=grid_spec, out_shape=[jax.ShapeDtypeStruct((R, C), F32)] * 4,
                          compiler_params=_cp(1))(me.reshape(1).astype(jnp.int32), src, land, w, m, v)


def _sum_slots(name, slots):
    _, R, C = slots.shape

    def body(s_ref, o_ref):
        g = s_ref[0]
        for s in range(1, N_DEV):
            g = g + s_ref[s]
        o_ref[...] = g

    return _rows(name, R, R, [(slots, (N_DEV, R, C), lambda i: (0, 0, 0))],
                 [(jax.ShapeDtypeStruct((R, C), F32), (R, C), lambda i: (0, 0))], body)[0]


def _adamw_packed(name, g, w, m, v):
    R, C = g.shape

    def body(g_ref, w_ref, m_ref, v_ref, d_ref, m2_ref, v2_ref):
        delta, m2, v2 = _adamw_math(w_ref[...], g_ref[...], m_ref[...], v_ref[...])
        d_ref[...] = delta
        m2_ref[...] = m2
        v2_ref[...] = v2

    im = lambda i: (0, 0)
    sds = jax.ShapeDtypeStruct((R, C), F32)
    return _rows(name, R, R, [(a, (R, C), im) for a in (g, w, m, v)], [(sds, (R, C), im)] * 3, body)


def _pack(arrays):
    rows = []
    for a in arrays:
        flat = a.reshape(-1).astype(F32)
        pad = (-flat.shape[0]) % 128
        rows.append(jnp.pad(flat, (0, pad)).reshape(-1, 128))
    out = jnp.concatenate(rows, axis=0)
    return jnp.pad(out, ((0, (-out.shape[0]) % 8), (0, 0)))


def _unpack(packed, shapes):
    lead = packed.shape[:-2]
    outs = []
    r = 0
    for shp in shapes:
        n = math.prod(shp)
        nr = -(-n // 128)
        flat = packed[..., r:r + nr, :].reshape(lead + (nr * 128,))[..., :n]
        outs.append(flat.reshape(lead + tuple(shp)))
        r += nr
    return outs


FFN1_BIG = ["ffn1_w_gate", "ffn1_w_up", "ffn1_w_down"]
MIX_BIG = ["w_in", "w_out"]
FFN2_BIG = ["ffn2_w_gate", "ffn2_w_up", "ffn2_w_down"]
BIG = FFN1_BIG + MIX_BIG + FFN2_BIG
COL_SHARDED = {"ffn1_w_gate", "ffn1_w_up", "w_in", "ffn2_w_gate", "ffn2_w_up"}
SMALL_SHARDED = ["rg_conv_w", "rg_gate_a_b", "rg_gate_x_b", "rg_lambda", "gdn_conv_w"]
WEIGHTS = ["ffn1_norm", "ffn1_w_gate", "ffn1_w_up", "ffn1_w_down", "mix_norm", "w_in", "w_out", "rg_conv_w", "rg_conv_b",
           "rg_gate_a_w", "rg_gate_a_b", "rg_gate_x_w", "rg_gate_x_b", "rg_lambda", "gdn_conv_w", "gdn_a_log",
           "gdn_dt_bias", "gdn_norm", "ffn2_norm", "ffn2_w_gate", "ffn2_w_up", "ffn2_w_down", "final_norm"]
SMALL = [n for n in WEIGHTS if n not in BIG]
ROW_VECTORS = {"ffn1_norm", "mix_norm", "ffn2_norm", "gdn_norm", "rg_conv_b", "final_norm"}
ROW_TILE = {"ffn1_w_gate": 256, "ffn1_w_up": 256, "ffn1_w_down": 176, "w_in": 256, "w_out": 64,
            "ffn2_w_gate": 256, "ffn2_w_up": 256, "ffn2_w_down": 176}


def _to_slabs(name, g):
    if name in COL_SHARDED:
        r, ctot = g.shape
        return g.reshape(r, N_DEV, ctot // N_DEV).transpose(1, 0, 2)
    return g.reshape(N_DEV, g.shape[0] // N_DEV, g.shape[1])


def _step(x, target, w, m, v):
    _, _, _, me = _mesh_pos()
    def shard_to_send(n, tok=None):
        s = w[n] if tok is None else w[n] + tok
        return (s.T if n in COL_SHARDED else s).astype(BF16)

    def unshard(n, gth):
        full = gth.reshape(-1, gth.shape[-1])
        return jnp.pad(full, ((0, D_IN_PAD - D_IN), (0, 0))) if n == "w_in" else full

    def landed(started, name, after):
        srcs, lands = _exchange_wait(name, started, after)
        def with_own(src, land):
            slot = lax.broadcasted_iota(jnp.int32, (N_DEV,) + (1,) * src.ndim, 0)
            return jnp.where(slot == me, src[None], land)

        return [with_own(src, land) for src, land in zip(srcs, lands)]

    up_names = ["ffn1_w_gate", "ffn1_w_up"]
    small_shards = [w[n] for n in SMALL_SHARDED]
    st_up = _exchange_start("gather_ffn1_up_start", False, [shard_to_send(n) for n in up_names])
    tok = st_up["token"]
    st_down = _exchange_start("gather_ffn1_down_start", False, [shard_to_send("ffn1_w_down", tok)])
    tok = tok + st_down["token"]
    st_mix = _exchange_start("gather_mix_start", False,
                             [shard_to_send(n, tok) for n in MIX_BIG] + [_pack(small_shards) + tok])
    tok = tok + st_mix["token"]
    st_ffn2 = _exchange_start("gather_ffn2_start", False, [shard_to_send(n, tok) for n in FFN2_BIG])
    W = {n: w[n] for n in SMALL if n not in SMALL_SHARDED}
    W["ffn1_norm"] = w["ffn1_norm"] + (tok + st_ffn2["token"])

    def more(stage, after):
        if stage == "ffn1_up":
            return {n: unshard(n, gth) for n, gth in zip(up_names, landed(st_up, "gather_ffn1_up_wait", after))}
        if stage == "ffn1_down":
            return {"ffn1_w_down": unshard("ffn1_w_down", landed(st_down, "gather_ffn1_down_wait", after)[0])}
        if stage == "ffn2":
            return {n: unshard(n, gth) for n, gth in zip(FFN2_BIG, landed(st_ffn2, "gather_ffn2_wait", after))}
        got = landed(st_mix, "gather_mix_wait", after)
        new = {n: unshard(n, gth) for n, gth in zip(MIX_BIG, got)}
        for n, gth in zip(SMALL_SHARDED, _unpack(got[-1], [s.shape for s in small_shards])):
            new[n] = jnp.moveaxis(gth, 0, -2).reshape(gth.shape[1:-1] + (N_DEV * gth.shape[-1],))
        return new

    R = _layer_fwd(x, target, W, more)
    W = R["W"]
    pending = []

    def emit_big(**named):
        slabs = [_to_slabs(n, g[:, :D_IN] if n == "w_in" else g) for n, g in named.items()]
        started = _exchange_start(f"scatter_start_{len(pending)}", True, slabs)
        pending.append((list(named), started))
        return started["token"]

    small_started = []

    def emit_small(G):
        packed = _pack([G[n] for n in SMALL if n != "ffn1_norm"])
        small_started.append(_exchange_start("gather_small_start", False, [packed]))

    grad_x, G = _layer_bwd(x, W, R, emit_big, emit_small)
    st_late = _exchange_start("gather_ffn1_norm_start", False, [_pack([G["ffn1_norm"]])])
    loss = lax.psum(R["loss"][0, 0], ("x", "y", "c"))
    out = {}

    def finish(i, after):
        names, started = pending[i]
        srcs, lands = _exchange_wait(f"scatter_wait_{i}", started, after)
        for n, src, land in zip(names, srcs, lands):
            out[n] = _adamw_slabs(f"adamw_{n}", src, land, me, w[n], m[n], v[n], ROW_TILE[n])

    n_early = len(pending) - 2
    for i in range(n_early):
        finish(i, grad_x)
    early = [n for n in SMALL if n != "ffn1_norm"]
    srcs, lands = _exchange_wait("gather_small_wait", small_started[0], grad_x)
    slot = lax.broadcasted_iota(jnp.int32, (N_DEV, 1, 1), 0)
    slots = jnp.where(slot == me, srcs[0][None], lands[0])
    reduced = dict(zip(early, _unpack(_sum_slots("sum_small_grads", slots), [G[n].shape for n in early])))

    def adamw_small(name, names):
        g_small = []
        for n in names:
            g = reduced[n]
            if n in SMALL_SHARDED:
                per = g.shape[-1] // N_DEV
                g = lax.dynamic_slice_in_dim(g, me * per, per, axis=g.ndim - 1)
            g_small.append(g.reshape(w[n].shape))
        shapes = [w[n].shape for n in names]
        d_p, m_p, v_p = _adamw_packed(name, _pack(g_small), _pack([w[n] for n in names]),
                                      _pack([m[n] for n in names]), _pack([v[n] for n in names]))
        for n, g, d_, m_, v_ in zip(names, g_small, _unpack(d_p, shapes), _unpack(m_p, shapes), _unpack(v_p, shapes)):
            out[n] = (g, d_, m_, v_)
        return d_p

    done_early = adamw_small("adamw_small", early)
    srcs, lands = _exchange_wait("gather_ffn1_norm_wait", st_late, done_early)
    late = jnp.where(slot == me, srcs[0][None], lands[0])
    reduced["ffn1_norm"] = _unpack(_sum_slots("sum_ffn1_norm_grad", late), [G["ffn1_norm"].shape])[0]
    done = adamw_small("adamw_ffn1_norm", ["ffn1_norm"])
    for i in range(n_early, len(pending)):
        finish(i, done)
    return loss, grad_x, out


def kernel(x, ffn1_norm, ffn1_w_gate, ffn1_w_up, ffn1_w_down, mix_norm, w_in, w_out, rg_conv_w, rg_conv_b, rg_gate_a_w, rg_gate_a_b, rg_gate_x_w, rg_gate_x_b, rg_lambda, gdn_conv_w, gdn_a_log, gdn_dt_bias, gdn_norm, ffn2_norm, ffn2_w_gate, ffn2_w_up, ffn2_w_down, final_norm, loss_target, m_ffn1_norm, m_ffn1_w_gate, m_ffn1_w_up, m_ffn1_w_down, m_mix_norm, m_w_in, m_w_out, m_rg_conv_w, m_rg_conv_b, m_rg_gate_a_w, m_rg_gate_a_b, m_rg_gate_x_w, m_rg_gate_x_b, m_rg_lambda, m_gdn_conv_w, m_gdn_a_log, m_gdn_dt_bias, m_gdn_norm, m_ffn2_norm, m_ffn2_w_gate, m_ffn2_w_up, m_ffn2_w_down, m_final_norm, v_ffn1_norm, v_ffn1_w_gate, v_ffn1_w_up, v_ffn1_w_down, v_mix_norm, v_w_in, v_w_out, v_rg_conv_w, v_rg_conv_b, v_rg_gate_a_w, v_rg_gate_a_b, v_rg_gate_x_w, v_rg_gate_x_b, v_rg_lambda, v_gdn_conv_w, v_gdn_a_log, v_gdn_dt_bias, v_gdn_norm, v_ffn2_norm, v_ffn2_w_gate, v_ffn2_w_up, v_ffn2_w_down, v_final_norm):
    args = dict(locals())
    orig_shapes = {n: args[n].shape for n in WEIGHTS}

    def local(prefix):
        d = {}
        for n in WEIGHTS:
            a = args[prefix + n]
            d[n] = a.reshape(1, -1) if n in ROW_VECTORS else a[0]
        return d

    loss, grad_x, out = _step(x[0], loss_target[0], local(""), local("m_"), local("v_"))
    res = [loss, grad_x[None]]
    for k in range(4):
        res += [out[n][k].reshape(orig_shapes[n]) for n in WEIGHTS]
    return tuple(res)
```

```python
import functools
import math

import jax
import jax.numpy as jnp
from jax import lax
from jax.experimental import pallas as pl
from jax.experimental.pallas import tpu as pltpu

F32, BF16 = jnp.float32, jnp.bfloat16

D_MODEL = 1024
D_FF = 2816
RG_W = 512
RG_BLOCKS = 8
RG_BLOCK = 64
RG_C = 8.0
CONV_W = 4
GDN_H = 4
GDN_DK = 128
CHUNK = 64
EPS = 1e-6
D_IN = 3088
D_IN_PAD = 3200
COL_BA = 3072
N_DEV = 8
HALO = 16
VMEM_LIMIT = 48 * 1024 * 1024
VMEM_CAP = 60 * 1024 * 1024

ADAM_LR = 0.001
ADAM_B1 = 0.9
ADAM_B2 = 0.999
ADAM_EPS = 1e-08
ADAM_WD = 0.01
ADAM_STEP = 10

HI = lax.Precision.HIGHEST


def _cp(n, vmem_limit=None):
    return pltpu.CompilerParams(dimension_semantics=("arbitrary",) * n,
                                vmem_limit_bytes=VMEM_LIMIT if vmem_limit is None else vmem_limit)


def _matmul_vmem_limit(block_bytes, acc_bytes):
    need = 2 * block_bytes + 2 * acc_bytes
    return int(min(VMEM_CAP, max(VMEM_LIMIT, need * 4 // 3)))


def _tile(n, pref):
    return min(n, pref)


def _sigmoid(x):
    return 0.5 * jnp.tanh(0.5 * x) + 0.5


def _softplus(x):
    return jnp.maximum(x, 0.0) + jnp.log(1.0 + jnp.exp(-jnp.abs(x)))


def _dot(a, b, ca, cb, prec=None):
    return lax.dot_general(a, b, (((ca,), (cb,)), ((), ())), preferred_element_type=F32, precision=prec)


def _fused_mm(name, M, N, K, tm, tn, tk, ops, pairs, extras, outs, epilogue):
    nm, nn, nk = M // tm, N // tn, K // tk
    assert nm * tm == M and nn * tn == N and nk * tk == K, (name, M, N, K, tm, tn, tk)
    spec_of = {
        "mk": pl.BlockSpec((tm, tk), lambda i, j, k: (i, k)),
        "km": pl.BlockSpec((tk, tm), lambda i, j, k: (k, i)),
        "kn": pl.BlockSpec((tk, tn), lambda i, j, k: (k, j)),
        "nk": pl.BlockSpec((tn, tk), lambda i, j, k: (j, k)),
    }
    in_specs = [spec_of[m] for _, m in ops]
    in_specs += [pl.BlockSpec(bs, lambda i, j, k, im=im: im(i, j)) for _, bs, im in extras]
    out_specs = [pl.BlockSpec(bs, lambda i, j, k, im=im: im(i, j)) for _, bs, im in outs]
    n_ops, n_ex, n_out = len(ops), len(extras), len(outs)
    n_acc = 1 + max(g for _, _, g in pairs)
    modes = [m for _, m in ops]

    def body(*refs):
        op_refs = refs[:n_ops]
        ex_refs = refs[n_ops:n_ops + n_ex]
        out_refs = refs[n_ops + n_ex:n_ops + n_ex + n_out]
        accs = refs[n_ops + n_ex + n_out:]
        i = pl.program_id(0)
        k = pl.program_id(2)
        def dots():
            vals = [r[...].astype(BF16) for r in op_refs]
            for ia, ib, g in pairs:
                yield g, _dot(vals[ia], vals[ib], 1 if modes[ia] == "mk" else 0, 0 if modes[ib] == "kn" else 1)

        if nk == 1:
            sums = [None] * n_acc
            for g, d in dots():
                sums[g] = d if sums[g] is None else sums[g] + d
            epilogue(i, [_Held(s) for s in sums], ex_refs, out_refs)
            return

        @pl.when(k == 0)
        def _():
            for a in accs:
                a[...] = jnp.zeros_like(a)

        for g, d in dots():
            accs[g][...] += d

        @pl.when(k == nk - 1)
        def _():
            epilogue(i, accs, ex_refs, out_refs)

    op_block = {"mk": tm * tk, "km": tm * tk, "kn": tk * tn, "nk": tk * tn}
    block_bytes = sum(op_block[m] * a.dtype.itemsize for a, m in ops)
    block_bytes += sum(math.prod(bs) * jnp.dtype(a.dtype).itemsize for a, bs, _ in list(extras) + list(outs))
    res = pl.pallas_call(
        body, name=name, grid=(nm, nn, nk), in_specs=in_specs, out_specs=out_specs,
        out_shape=[o for o, _, _ in outs],
        scratch_shapes=[pltpu.VMEM((tm, tn), F32)] * (n_acc if nk > 1 else 0),
        compiler_params=_cp(3, _matmul_vmem_limit(block_bytes, n_acc * tm * tn * 4)),
    )(*[a for a, _ in ops], *[a for a, _, _ in extras])
    return res


class _Held:
    def __init__(self, value):
        self.value = value

    def __getitem__(self, idx):
        return self.value[idx]


def _mn(i, j):
    return (i, j)


def _row0(i, j):
    return (0, 0)


def _rows(name, S, ts, ins, outs, body, scratch=()):
    return pl.pallas_call(
        body, name=name, grid=(S // ts,),
        in_specs=[pl.BlockSpec(bs, im) for _, bs, im in ins],
        out_specs=[pl.BlockSpec(bs, im) for _, bs, im in outs],
        out_shape=[o for o, _, _ in outs],
        scratch_shapes=list(scratch),
        compiler_params=_cp(1),
    )(*[a for a, _, _ in ins])


def _halo_ins(arr, S, ts, width, colblk):
    per = ts // HALO
    last = S // HALO - 1
    return [
        (arr, (ts, width), lambda i: (i, colblk)),
        (arr, (HALO, width), lambda i: (jnp.maximum(i * per - 1, 0), colblk)),
        (arr, (HALO, width), lambda i: (jnp.minimum((i + 1) * per, last), colblk)),
    ]


def _ext(main_ref, prev_ref, next_ref, i, n_tiles):
    prev = jnp.where(i > 0, prev_ref[...].astype(F32), 0.0)
    nxt = jnp.where(i < n_tiles - 1, next_ref[...].astype(F32), 0.0)
    return jnp.concatenate([prev, main_ref[...].astype(F32), nxt], axis=0)


def _shift(ext, off, ts):
    n = ext.shape[0]
    if off == 0:
        return ext[HALO:HALO + ts]
    return pltpu.roll(ext, (-off) % n, 0)[HALO:HALO + ts]


def _rmsnorm_fwd(name, x, g):
    S, D = x.shape
    ts = _tile(S, 512)

    def body(x_ref, g_ref, o_ref):
        xv = x_ref[...]
        r = lax.rsqrt(jnp.mean(xv * xv, axis=-1, keepdims=True) + EPS)
        o_ref[...] = (xv * r * g_ref[...]).astype(BF16)

    return _rows(name, S, ts,
                 [(x, (ts, D), lambda i: (i, 0)), (g, (1, D), lambda i: (0, 0))],
                 [(jax.ShapeDtypeStruct((S, D), BF16), (ts, D), lambda i: (i, 0))], body)[0]


def _rmsnorm_bwd_tile(dh, x, g):
    r = lax.rsqrt(jnp.mean(x * x, axis=-1, keepdims=True) + EPS)
    xhat = x * r
    dxn = dh * g
    dx = r * (dxn - xhat * jnp.mean(dxn * xhat, axis=-1, keepdims=True))
    return dx, dh * xhat


def _ffn_fwd(tag, x, h, wg, wu, wd, extras, outs, finish):
    S = x.shape[0]
    tm = _tile(S, 1024)
    tn = 1408

    def epi_up(i, accs, ex, out):
        a = accs[0][...]
        b = accs[1][...]
        s = _sigmoid(a)
        sa = a * s
        out[0][...] = sa.astype(BF16)
        out[1][...] = (b * (s * (1.0 + a * (1.0 - s)))).astype(BF16)
        out[2][...] = (sa * b).astype(BF16)

    sds = jax.ShapeDtypeStruct((S, D_FF), BF16)
    a, b, f = _fused_mm(f"{tag}_up", S, D_FF, D_MODEL, tm, tn, D_MODEL,
                        [(h, "mk"), (wg, "nk"), (wu, "nk")], [(0, 1, 0), (0, 2, 1)], [],
                        [(sds, (tm, tn), _mn)] * 3, epi_up)

    def epi_down(i, accs, ex, out):
        finish(i, ex[0][...] + 0.5 * accs[0][...], ex[1:], out)

    if callable(wd):
        wd = wd(f)
    res = _fused_mm(f"{tag}_down", S, D_MODEL, D_FF, tm, D_MODEL, 1408,
                    [(f, "mk"), (wd, "kn")], [(0, 1, 0)], [(x, (tm, D_MODEL), _mn)] + extras(tm), outs(tm), epi_down)
    return res, a, b, f


def _rmsnorm_tile(xv, g):
    return (xv * lax.rsqrt(jnp.mean(xv * xv, axis=-1, keepdims=True) + EPS) * g).astype(BF16)


def _conv_taps(ext, w_ref, ts):
    acc = None
    for j in range(CONV_W):
        term = w_ref[j:j + 1, :] * _shift(ext, j - 2, ts)
        acc = term if acc is None else acc + term
    return acc


def _l2norm_heads(s, scale):
    outs = []
    for h in range(GDN_H):
        sh = s[:, h * GDN_DK:(h + 1) * GDN_DK]
        outs.append(sh * (lax.rsqrt(jnp.sum(sh * sh, axis=-1, keepdims=True) + EPS) * scale))
    return jnp.concatenate(outs, axis=-1)


def _conv_fwd(name, p, colblk, w, bias, mode):
    S = p.shape[0]
    ts = _tile(S, 512)
    n_tiles = S // ts
    C = w.shape[1]

    def body(main, prev, nxt, w_ref, b_ref, o_ref):
        i = pl.program_id(0)
        c = _conv_taps(_ext(main, prev, nxt, i, n_tiles), w_ref, ts)
        if mode == "bias":
            o_ref[...] = c + b_ref[...]
        else:
            s = c * _sigmoid(c)
            if mode == "q":
                s = _l2norm_heads(s, GDN_DK ** -0.5)
            elif mode == "k":
                s = _l2norm_heads(s, 1.0)
            o_ref[...] = s

    ins = _halo_ins(p, S, ts, C, colblk) + [(w, (CONV_W, C), lambda i: (0, 0)), (bias, (1, C), lambda i: (0, 0))]
    return _rows(name, S, ts, ins, [(jax.ShapeDtypeStruct((S, C), F32), (ts, C), lambda i: (i, 0))], body)[0]


def _rg_gate_terms(pre, xc, prm_ref, d):
    r = _sigmoid(pre[:, d * 1024:d * 1024 + RG_W] + prm_ref[2 * d:2 * d + 1, :])
    ig = _sigmoid(pre[:, d * 1024 + RG_W:(d + 1) * 1024] + prm_ref[2 * d + 1:2 * d + 2, :])
    sp = _softplus(-prm_ref[4 + d:5 + d, :])
    log_a = -RG_C * r * sp
    a = jnp.exp(log_a)
    t = jnp.tanh(log_a)
    sq = jnp.sqrt(-2.0 * t / (1.0 - t))
    return r, ig, sp, a, sq


def _rg_gates_fwd(xc, bd, prm):
    S = xc.shape[0]
    tm = _tile(S, 256)

    def epi(i, accs, ex, out):
        pre = accs[0][...]
        xv = ex[0][...]
        for d in range(2):
            r, ig, sp, a, sq = _rg_gate_terms(pre, xv, ex[1], d)
            out[2 * d][...] = a
            out[2 * d + 1][...] = sq * ig * xv

    sds = jax.ShapeDtypeStruct((S, RG_W), F32)
    blk = (tm, RG_W)
    im = lambda i, j: (i, 0)
    return _fused_mm("rg_gates_fwd", S, 4 * RG_W, RG_W, tm, 4 * RG_W, RG_W,
                     [(xc, "mk"), (bd, "kn")], [(0, 1, 0)],
                     [(xc, blk, im), (prm, (8, RG_W), _row0)], [(sds, blk, im)] * 4, epi)


SUBLANES = 8


def _scan_rows(a, b, reverse):
    rows = lax.broadcasted_iota(jnp.int32, a.shape, 0)
    s = 1
    while s < SUBLANES:
        shift = SUBLANES - s if reverse else s
        a_sh = pltpu.roll(a, shift, 0)
        b_sh = pltpu.roll(b, shift, 0)
        valid = (rows < SUBLANES - s) if reverse else (rows >= s)
        b = jnp.where(valid, a * b_sh + b, b)
        a = jnp.where(valid, a * a_sh, a)
        s *= 2
    return a, b


def _rg_scan(name, a_f, b_f, a_b, b_b):
    S, C = a_f.shape
    ts = _tile(S, 512)
    n_tiles = S // ts

    def body(af, bf, ab, bb, hf, hb, carry):
        @pl.when(pl.program_id(0) == 0)
        def _():
            carry[...] = jnp.zeros_like(carry)

        n_sub = ts // SUBLANES

        def step(j, c):
            cf, cb = c
            r0 = pl.multiple_of(j * SUBLANES, SUBLANES)
            cum_a, h0 = _scan_rows(af[pl.ds(r0, SUBLANES), :], bf[pl.ds(r0, SUBLANES), :], False)
            h = h0 + cum_a * cf
            hf[pl.ds(r0, SUBLANES), :] = h
            cf = h[SUBLANES - 1:SUBLANES, :]
            r1 = pl.multiple_of((n_sub - 1 - j) * SUBLANES, SUBLANES)
            cum_a, h0 = _scan_rows(ab[pl.ds(r1, SUBLANES), :], bb[pl.ds(r1, SUBLANES), :], True)
            h = h0 + cum_a * cb
            hb[pl.ds(r1, SUBLANES), :] = h
            cb = h[0:1, :]
            return cf, cb

        cf, cb = lax.fori_loop(0, n_sub, step, (carry[0:1, :], carry[1:2, :]), unroll=4)
        carry[0:1, :] = cf
        carry[1:2, :] = cb

    fw = lambda i: (i, 0)
    bw = lambda i: (n_tiles - 1 - i, 0)
    sds = jax.ShapeDtypeStruct((S, C), F32)
    return _rows(name, S, ts,
                 [(a_f, (ts, C), fw), (b_f, (ts, C), fw), (a_b, (ts, C), bw), (b_b, (ts, C), bw)],
                 [(sds, (ts, C), fw), (sds, (ts, C), bw)], body, scratch=[pltpu.VMEM((8, C), F32)])


def _tri_masks():
    ri = lax.broadcasted_iota(jnp.int32, (CHUNK, CHUNK), 0)
    ci = lax.broadcasted_iota(jnp.int32, (CHUNK, CHUNK), 1)
    return ri, ci


def _gdn_prep_fwd(p, prm):
    S = p.shape[0]
    ts = _tile(S, 512)

    def body(p_ref, prm_ref, o_ref):
        raw = p_ref[...].astype(F32)
        lane = lax.broadcasted_iota(jnp.int32, (1, 128), 1)
        g = -jnp.exp(prm_ref[0:1, :]) * _softplus(raw + prm_ref[1:2, :])
        g = jnp.where((lane >= 8) & (lane < 16), g, 0.0)
        beta = _sigmoid(raw)
        ri, ci = _tri_masks()
        lower = (ri >= ci).astype(F32)
        upper = (ri <= ci).astype(F32)
        for c in range(ts // CHUNK):
            rows = slice(c * CHUNK, (c + 1) * CHUNK)
            gch = g[rows]
            gc = jnp.where(lane < 12, _dot(lower, gch, 1, 0, HI), _dot(upper, gch, 1, 0, HI))
            o_ref[rows, :] = jnp.where(lane < 8, beta[rows], gc)

    return _rows("gdn_prep_fwd", S, ts,
                 [(p, (ts, 128), lambda i: (i, COL_BA // 128)), (prm, (8, 128), lambda i: (0, 0))],
                 [(jax.ShapeDtypeStruct((S, 128), F32), (ts, 128), lambda i: (i, 0))], body)[0]


def _bdot(a, b, ca, cb):
    return _dot(a.astype(BF16), b.astype(BF16), ca, cb)


GDN_W = GDN_H * GDN_DK
GDN_TS = 256
LOCAL_CHUNKS = 2

def _gdn_decay(bg_ref, gcr_ref, c, rows, r0, col, rev, ri, ci):
    beta = bg_ref[rows, col:col + 1]
    gc = bg_ref[rows, 8 + col:9 + col]
    last = 0 if rev else CHUNK - 1
    gl = bg_ref[pl.ds(r0 + last, 1), 8 + col:9 + col]
    out = dict(beta=beta, gc=gc, gl=gl, eg=jnp.exp(gc), egl=jnp.exp(gl - gc), cd=jnp.exp(gl))
    if gcr_ref is not None:
        incl = (ri <= ci) if rev else (ri >= ci)
        out["strict"] = (ri < ci) if rev else (ri > ci)
        out["dm"] = jnp.where(incl, jnp.exp(jnp.where(incl, gc - gcr_ref[c, col:col + 1, :], 0.0)), 0.0)
    return out


def _dir_tile(d, n_tiles, flip):
    if (d == 1) != flip:
        return lambda i: n_tiles - 1 - i
    return lambda i: i


def _gdn_local_fwd(q, k, v, bg, gcr):
    S = q.shape[0]
    ts = _tile(S, GDN_TS)
    ncb = ts // CHUNK
    nch = S // CHUNK

    def body(q_ref, k_ref, v_ref, bg_ref, gcr_ref, *out_refs):
        ri, ci = _tri_masks()
        eye = (ri == ci).astype(F32)
        outs = (out_refs[0:6], out_refs[6:12])
        cd_ref = out_refs[12]

        def chunk(cc, carry):
            chains = []
            for c in (LOCAL_CHUNKS * cc + j for j in range(LOCAL_CHUNKS)):
                r0 = pl.multiple_of(c * CHUNK, CHUNK)
                rows = pl.ds(r0, CHUNK)
                for h in range(GDN_H):
                    cols = slice(h * GDN_DK, (h + 1) * GDN_DK)
                    qh, kh, vh = q_ref[rows, cols], k_ref[rows, cols], v_ref[rows, cols]
                    both = _bdot(jnp.concatenate([qh, kh], axis=0), kh, 1, 1)
                    for d in range(2):
                        chains.append(dict(c=c, r0=r0, rows=rows, h=h, d=d, cols=cols, qh=qh, kh=kh, vh=vh,
                                           qk=both[0:CHUNK], kk=both[CHUNK:2 * CHUNK]))
            for ch in chains:
                m = _gdn_decay(bg_ref, gcr_ref, ch["c"], ch["rows"], ch["r0"], ch["d"] * GDN_H + ch["h"], ch["d"] == 1,
                               ri, ci)
                ch["m"] = m
                ch["x"] = -jnp.where(m["strict"], m["beta"] * ch["kk"] * m["dm"], 0.0)
                ch["t"] = eye + ch["x"]
            for ch in chains:
                ch["pw"] = _bdot(ch["x"], ch["x"], 1, 0)
            for level in range(1, 6):
                last_level = level == 5
                for ch in chains:
                    rhs = ch["t"] if last_level else jnp.concatenate([ch["t"], ch["pw"]], axis=1)
                    ch["prod"] = _bdot(ch["pw"], rhs, 1, 0)
                for ch in chains:
                    ch["t"] = ch["t"] + ch["prod"][:, 0:CHUNK]
                    if not last_level:
                        ch["pw"] = ch["prod"][:, CHUNK:2 * CHUNK]
            for ch in chains:
                m = ch["m"]
                rhs = jnp.concatenate([ch["vh"] * m["beta"], ch["kh"] * (m["beta"] * m["eg"])], axis=1)
                ch["uw"] = _bdot(ch["t"], rhs, 1, 0)
            for ch in chains:
                u_ref, w_ref, a_ref, t_ref, qd_ref, kd_ref = outs[ch["d"]]
                m = ch["m"]
                c, rows = ch["c"], ch["rows"]
                col = ch["d"] * GDN_H + ch["h"]
                u_ref[rows, ch["cols"]] = ch["uw"][:, 0:GDN_DK]
                w_ref[rows, ch["cols"]] = ch["uw"][:, GDN_DK:2 * GDN_DK].astype(BF16)
                a_ref[c, ch["h"]] = (ch["qk"] * m["dm"]).astype(BF16)
                t_ref[c, ch["h"]] = _bdot(ch["t"], eye, 0, 0).astype(BF16)
                qd_ref[rows, ch["cols"]] = (ch["qh"] * m["eg"]).astype(BF16)
                kd_ref[rows, ch["cols"]] = (ch["kh"] * m["egl"]).astype(BF16)
                cd_ref[c, col:col + 1, :] = jnp.broadcast_to(m["cd"], (1, 128))
            return carry

        lax.fori_loop(0, ncb // LOCAL_CHUNKS, chunk, 0)

    im = lambda i: (i, 0)
    im4 = lambda i: (i, 0, 0, 0)
    ins = [(q, (ts, GDN_W), im), (k, (ts, GDN_W), im), (v, (ts, GDN_W), im), (bg, (ts, 128), im),
           (gcr, (ncb, 8, CHUNK), lambda i: (i, 0, 0))]
    per_dir = [(jax.ShapeDtypeStruct((S, GDN_W), F32), (ts, GDN_W), im),
               (jax.ShapeDtypeStruct((S, GDN_W), BF16), (ts, GDN_W), im),
               (jax.ShapeDtypeStruct((nch, GDN_H, CHUNK, CHUNK), BF16), (ncb, GDN_H, CHUNK, CHUNK), im4),
               (jax.ShapeDtypeStruct((nch, GDN_H, CHUNK, CHUNK), BF16), (ncb, GDN_H, CHUNK, CHUNK), im4),
               (jax.ShapeDtypeStruct((S, GDN_W), BF16), (ts, GDN_W), im),
               (jax.ShapeDtypeStruct((S, GDN_W), BF16), (ts, GDN_W), im)]
    cd_out = (jax.ShapeDtypeStruct((nch, 8, 128), F32), (ncb, 8, 128), lambda i: (i, 0, 0))
    res = _rows("gdn_local_fwd", S, ts, ins, per_dir * 2 + [cd_out], body)
    return res[0:6], res[6:12], res[12]


def _gdn_scan_fwd(loc):
    S = loc[0][0].shape[0]
    ts = _tile(S, GDN_TS)
    n_tiles = S // ts
    ncb = ts // CHUNK
    nch = S // CHUNK

    def body(*refs):
        ins = (refs[0:6], refs[6:12])
        outs = (refs[12:15], refs[15:18])
        state = refs[18]

        @pl.when(pl.program_id(0) == 0)
        def _():
            state[...] = jnp.zeros_like(state)

        def chunk(cc, carry):
            chains = []
            for d in range(2):
                c = cc if d == 0 else ncb - 1 - cc
                rows = pl.ds(pl.multiple_of(c * CHUNK, CHUNK), CHUNK)
                for h in range(GDN_H):
                    cols = slice(h * GDN_DK, (h + 1) * GDN_DK)
                    chains.append(dict(d=d, h=h, c=c, rows=rows, cols=cols, st=state[d * GDN_H + h]))
            for ch in chains:
                qd_ref, kd_ref, u_ref, w_ref, a_ref, cd_ref = ins[ch["d"]]
                rows, cols = ch["rows"], ch["cols"]
                lhs = jnp.concatenate([w_ref[rows, cols], qd_ref[rows, cols]], axis=0)
                ch["ws_qs"] = _dot(lhs, ch["st"].astype(BF16), 1, 0)
            for ch in chains:
                qd_ref, kd_ref, u_ref, w_ref, a_ref, cd_ref = ins[ch["d"]]
                rows, cols = ch["rows"], ch["cols"]
                vn = u_ref[rows, cols] - ch["ws_qs"][0:CHUNK]
                vnb = vn.astype(BF16)
                ch["vn"] = vn
                ch["avn"] = _dot(a_ref[ch["c"], ch["h"]], vnb, 1, 0)
                ch["kvn"] = _dot(kd_ref[rows, cols], vnb, 0, 0)
            for ch in chains:
                o_ref, vn_ref, s_ref = outs[ch["d"]]
                cd_ref = ins[ch["d"]][5]
                rows, cols = ch["rows"], ch["cols"]
                col = ch["d"] * GDN_H + ch["h"]
                o_ref[rows, cols] = ch["ws_qs"][CHUNK:2 * CHUNK] + ch["avn"]
                vn_ref[rows, cols] = ch["vn"].astype(BF16)
                s_ref[ch["c"], ch["h"]] = ch["st"].astype(BF16)
                state[ch["d"] * GDN_H + ch["h"]] = ch["st"] * cd_ref[ch["c"], col:col + 1, :] + ch["kvn"]
            return carry

        lax.fori_loop(0, ncb, chunk, 0)

    ins, outs = [], []
    for d in range(2):
        tix = _dir_tile(d, n_tiles, False)
        im = lambda i, tix=tix: (tix(i), 0)
        im4 = lambda i, tix=tix: (tix(i), 0, 0, 0)
        u, w, a, _, qd, kd = loc[d]
        ins += [(qd, (ts, GDN_W), im), (kd, (ts, GDN_W), im), (u, (ts, GDN_W), im), (w, (ts, GDN_W), im),
                (a, (ncb, GDN_H, CHUNK, CHUNK), im4), (loc[2], (ncb, 8, 128), lambda i, tix=tix: (tix(i), 0, 0))]
        outs += [(jax.ShapeDtypeStruct((S, GDN_W), F32), (ts, GDN_W), im),
                 (jax.ShapeDtypeStruct((S, GDN_W), BF16), (ts, GDN_W), im),
                 (jax.ShapeDtypeStruct((nch, GDN_H, GDN_DK, GDN_DK), BF16), (ncb, GDN_H, GDN_DK, GDN_DK), im4)]
    res = _rows("gdn_scan_fwd", S, ts, ins, outs, body, scratch=[pltpu.VMEM((2 * GDN_H, GDN_DK, GDN_DK), F32)])
    return res[0:3], res[3:6]


def _gelu(x):
    c = math.sqrt(2.0 / math.pi)
    t = jnp.tanh(c * (x + 0.044715 * x * x * x))
    return 0.5 * x * (1.0 + t), t


def _mix_out_fwd(h_f, h_b, o_f, o_b, p, gn):
    S = h_f.shape[0]
    ts = _tile(S, 512)

    def body(hf, hb, of, ob, gate, z, gn_ref, y_ref):
        ge, _ = _gelu(gate[...].astype(F32))
        y_ref[:, 0:RG_W] = ((hf[...] + hb[...]) * ge).astype(BF16)
        o = of[...] + ob[...]
        zv = z[...].astype(F32)
        sz = zv * _sigmoid(zv)
        for h in range(GDN_H):
            cols = slice(h * GDN_DK, (h + 1) * GDN_DK)
            oh = o[:, cols]
            n = oh * lax.rsqrt(jnp.mean(oh * oh, axis=-1, keepdims=True) + EPS) * gn_ref[...]
            y_ref[:, RG_W + h * GDN_DK:RG_W + (h + 1) * GDN_DK] = (n * sz[:, cols]).astype(BF16)

    blk = (ts, RG_W)
    im = lambda i: (i, 0)
    ins = [(h_f, blk, im), (h_b, blk, im), (o_f, blk, im), (o_b, blk, im),
           (p, blk, lambda i: (i, 1)), (p, blk, lambda i: (i, 5)), (gn, (1, GDN_DK), lambda i: (0, 0))]
    return _rows("mix_out_fwd", S, ts, ins,
                 [(jax.ShapeDtypeStruct((S, D_MODEL), BF16), (ts, D_MODEL), im)], body)[0]


def _block_diag(w):
    n = w.shape[0]
    return jnp.einsum("nij,nm->nimj", w, jnp.eye(n, dtype=w.dtype)).reshape(n * w.shape[1], n * w.shape[2])


def _rg_bd(a_w, x_w):
    return jnp.concatenate([_block_diag(a_w[0]), _block_diag(x_w[0]), _block_diag(a_w[1]), _block_diag(x_w[1])],
                           axis=1).astype(BF16)


def _rg_prm(ba, bx, lam):
    return jnp.concatenate([ba[0:1], bx[0:1], ba[1:2], bx[1:2], lam, jnp.zeros((2, RG_W), F32)], axis=0)


def _gdn_prm(a_log, dt_bias):
    rows = jnp.zeros((8, 128), F32)
    rows = rows.at[0, 8:16].set(a_log.reshape(-1))
    return rows.at[1, 8:16].set(dt_bias.reshape(-1))


def _gc_rows(bg):
    S = bg.shape[0]
    return bg[:, 8:16].reshape(S // CHUNK, CHUNK, 8).transpose(0, 2, 1)


def _layer_fwd(x0, target, W, more=None):
    S = x0.shape[0]
    R = {}
    R["h1"] = _rmsnorm_fwd("rms1", x0, W["ffn1_norm"])
    if more is not None:
        W = {**W, **more("ffn1_up", R["h1"])}
    late_wd = {}

    def ffn1_wd(after):
        late_wd.update(more("ffn1_down", after))
        return late_wd["ffn1_w_down"]

    sd_x = jax.ShapeDtypeStruct((S, D_MODEL), F32)
    sd_h = jax.ShapeDtypeStruct((S, D_MODEL), BF16)

    def norm_after(gain):
        extras = lambda t: [(gain, (1, D_MODEL), _row0)]
        outs = lambda t: [(sd_x, (t, D_MODEL), _mn), (sd_h, (t, D_MODEL), _mn)]

        def finish(i, xo, ex, out):
            out[0][...] = xo
            out[1][...] = _rmsnorm_tile(xo, ex[0][...])

        return extras, outs, finish

    (R["x1"], R["h2"]), R["a1"], R["b1"], R["f1"] = _ffn_fwd(
        "ffn1", x0, R["h1"], W["ffn1_w_gate"], W["ffn1_w_up"], ffn1_wd if more is not None else W["ffn1_w_down"],
        *norm_after(W["mix_norm"]))
    if more is not None:
        W = {**W, **late_wd, **more("mixer", R["x1"])}
    tm = _tile(S, 512)
    tmp = _tile(S, 1024)
    tmp = _tile(S, 512)
    R["p"] = _fused_mm("in_proj", S, D_IN_PAD, D_MODEL, tmp, D_IN_PAD, D_MODEL, [(R["h2"], "mk"), (W["w_in"], "nk")],
                       [(0, 1, 0)], [], [(jax.ShapeDtypeStruct((S, D_IN_PAD), BF16), (tmp, D_IN_PAD), _mn)],
                       lambda i, accs, ex, out: out[0].__setitem__(Ellipsis, accs[0][...].astype(BF16)))[0]
    p = R["p"]
    R["xc"] = _conv_fwd("rg_conv_fwd", p, 0, W["rg_conv_w"], W["rg_conv_b"], "bias")
    R["bd"] = _rg_bd(W["rg_gate_a_w"], W["rg_gate_x_w"])
    R["rg_prm"] = _rg_prm(W["rg_gate_a_b"], W["rg_gate_x_b"], W["rg_lambda"])
    a_f, b_f, a_b, b_b = _rg_gates_fwd(R["xc"], R["bd"], R["rg_prm"])
    R["a_f"], R["a_b"] = a_f, a_b
    R["h_f"], R["h_b"] = _rg_scan("rg_scan_fwd", a_f, b_f, a_b, b_b)
    zero_b = jnp.zeros((1, RG_W), F32)
    cw = W["gdn_conv_w"]
    R["q"] = _conv_fwd("gdn_conv_q", p, 2, cw[:, 0:512], zero_b, "q")
    R["k"] = _conv_fwd("gdn_conv_k", p, 3, cw[:, 512:1024], zero_b, "k")
    R["v"] = _conv_fwd("gdn_conv_v", p, 4, cw[:, 1024:1536], zero_b, "v")
    R["gdn_prm"] = _gdn_prm(W["gdn_a_log"], W["gdn_dt_bias"])
    R["bg"] = _gdn_prep_fwd(p, R["gdn_prm"])
    R["gcr"] = _gc_rows(R["bg"])
    R["gdn_loc"] = _gdn_local_fwd(R["q"], R["k"], R["v"], R["bg"], R["gcr"])
    R["gdn_fwd"] = _gdn_scan_fwd(R["gdn_loc"])
    R["o_f"], R["o_b"] = R["gdn_fwd"][0][0], R["gdn_fwd"][1][0]
    R["y"] = _mix_out_fwd(R["h_f"], R["h_b"], R["o_f"], R["o_b"], p, W["gdn_norm"])
    def epi_out(i, accs, ex, out):
        x2 = ex[0][...] + accs[0][...]
        out[0][...] = x2
        out[1][...] = _rmsnorm_tile(x2, ex[1][...])

    R["x2"], R["h3"] = _fused_mm("out_proj", S, D_MODEL, D_MODEL, tm, D_MODEL, D_MODEL,
                                 [(R["y"], "mk"), (W["w_out"], "kn")], [(0, 1, 0)],
                                 [(R["x1"], (tm, D_MODEL), _mn), (W["ffn2_norm"], (1, D_MODEL), _row0)],
                                 [(sd_x, (tm, D_MODEL), _mn), (sd_h, (tm, D_MODEL), _mn)], epi_out)
    if more is not None:
        W = {**W, **more("ffn2", R["x2"])}

    def loss_finish(i, xo, ex, out):
        gv = ex[1][...]
        r = lax.rsqrt(jnp.mean(xo * xo, axis=-1, keepdims=True) + EPS)
        err = xo * r * gv - ex[0][...]
        dx, dgt = _rmsnorm_bwd_tile(err * (1.0 / D_MODEL), xo, gv)
        out[0][...] = dx
        _colsum_into(out[1], i, jnp.zeros((8, 128), F32) + jnp.sum(err * err) * (0.5 / D_MODEL))
        _colsum_into(out[2], i, jnp.sum(dgt, axis=0, keepdims=True))

    (R["dx3"], R["loss"], R["d_final_norm"]), R["a2"], R["b2"], R["f2"] = _ffn_fwd(
        "ffn2", R["x2"], R["h3"], W["ffn2_w_gate"], W["ffn2_w_up"], W["ffn2_w_down"],
        lambda t: [(target, (t, D_MODEL), _mn), (W["final_norm"], (1, D_MODEL), _row0)],
        lambda t: [(sd_x, (t, D_MODEL), _mn), (jax.ShapeDtypeStruct((8, 128), F32), (8, 128), _row0),
                   (jax.ShapeDtypeStruct((1, D_MODEL), F32), (1, D_MODEL), _row0)],
        loss_finish)
    R["W"] = W
    return R


def _colsum_into(ref, i, val):
    @pl.when(i == 0)
    def _():
        ref[...] = val

    @pl.when(i > 0)
    def _():
        ref[...] += val


def _ffn_bwd(tag, dout, x, g, h, a, b, f, wg, wu, wd, emit):
    S = x.shape[0]
    tm = _tile(S, 512)
    tk_s = _tile(S, 1024)
    dwd = _fused_mm(f"{tag}_dw_down", D_FF, D_MODEL, S, 1408, D_MODEL, tk_s, [(f, "km"), (dout, "kn")], [(0, 1, 0)], [],
                    [(jax.ShapeDtypeStruct((D_FF, D_MODEL), BF16), (1408, D_MODEL), _mn)],
                    lambda i, accs, ex, out: out[0].__setitem__(Ellipsis, (0.5 * accs[0][...]).astype(BF16)))[0]
    emit(down=dwd)

    def epi_act(i, accs, ex, out):
        df = 0.5 * accs[0][...]
        out[0][...] = (df * ex[1][...].astype(F32)).astype(BF16)
        out[1][...] = (df * ex[0][...].astype(F32)).astype(BF16)

    sds = jax.ShapeDtypeStruct((S, D_FF), BF16)
    da, db = _fused_mm(f"{tag}_dact", S, D_FF, D_MODEL, tm, 1408, D_MODEL, [(dout, "mk"), (wd, "nk")], [(0, 1, 0)],
                       [(a, (tm, 1408), _mn), (b, (tm, 1408), _mn)], [(sds, (tm, 1408), _mn)] * 2, epi_act)

    def epi_w2(i, accs, ex, out):
        out[0][...] = accs[0][...].astype(BF16)
        out[1][...] = accs[1][...].astype(BF16)

    sdw = jax.ShapeDtypeStruct((D_MODEL, D_FF), BF16)
    dwg, dwu = _fused_mm(f"{tag}_dw_up", D_MODEL, D_FF, S, D_MODEL, 1408, tk_s,
                         [(h, "km"), (da, "kn"), (db, "kn")], [(0, 1, 0), (0, 2, 1)], [],
                         [(sdw, (D_MODEL, 1408), _mn)] * 2, epi_w2)
    tok = emit(gate=dwg, up=dwu)
    if tok is not None:
        g = g + tok

    def epi_dx(i, accs, ex, out):
        dx, dgt = _rmsnorm_bwd_tile(accs[0][...], ex[0][...], ex[1][...])
        out[0][...] = ex[2][...] + dx
        _colsum_into(out[1], i, jnp.sum(dgt, axis=0, keepdims=True))

    tmx = _tile(S, 512)
    dx, dg = _fused_mm(f"{tag}_dx", S, D_MODEL, D_FF, tmx, D_MODEL, D_FF,
                       [(da, "mk"), (wg, "kn"), (db, "mk"), (wu, "kn")], [(0, 1, 0), (2, 3, 0)],
                       [(x, (tmx, D_MODEL), _mn), (g, (1, D_MODEL), _row0), (dout, (tmx, D_MODEL), _mn)],
                       [(jax.ShapeDtypeStruct((S, D_MODEL), F32), (tmx, D_MODEL), _mn),
                        (jax.ShapeDtypeStruct((1, D_MODEL), F32), (1, D_MODEL), _row0)], epi_dx)
    return dx, dg


def _mix_out_bwd(dx2, w_out, h_f, h_b, o_f, o_b, p, gn):
    S = dx2.shape[0]
    ts = _tile(S, 512)
    c0 = math.sqrt(2.0 / math.pi)

    def epi(i, accs, ex, out):
        hf, hb, of, ob, gate, z, gn_ref = ex
        dhr_ref, dgate_ref, do_ref, dz_ref, dgn_ref = out
        dy_ref = accs[0]
        gv = gate[...].astype(F32)
        ge, t = _gelu(gv)
        dy_rg = dy_ref[:, 0:RG_W]
        dhr_ref[...] = dy_rg * ge
        dgelu = 0.5 * (1.0 + t) + 0.5 * gv * (1.0 - t * t) * c0 * (1.0 + 3.0 * 0.044715 * gv * gv)
        dgate_ref[...] = (dy_rg * (hf[...] + hb[...]) * dgelu).astype(BF16)
        o = of[...] + ob[...]
        zv = z[...].astype(F32)
        sig = _sigmoid(zv)
        gnv = gn_ref[...]
        dgn = jnp.zeros((1, GDN_DK), F32)
        for h in range(GDN_H):
            cols = slice(h * GDN_DK, (h + 1) * GDN_DK)
            oh = o[:, cols]
            r = lax.rsqrt(jnp.mean(oh * oh, axis=-1, keepdims=True) + EPS)
            ohat = oh * r
            dyh = dy_ref[:, RG_W + h * GDN_DK:RG_W + (h + 1) * GDN_DK]
            zh = zv[:, cols]
            sh = sig[:, cols]
            dn = dyh * zh * sh
            dz_ref[:, cols] = (dyh * ohat * gnv * (sh * (1.0 + zh * (1.0 - sh)))).astype(BF16)
            dxn = dn * gnv
            do_ref[:, cols] = r * (dxn - ohat * jnp.mean(dxn * ohat, axis=-1, keepdims=True))
            dgn = dgn + jnp.sum(dn * ohat, axis=0, keepdims=True)
        _colsum_into(dgn_ref, i, dgn)

    blk = (ts, RG_W)
    im = lambda i, j: (i, 0)
    extras = [(h_f, blk, im), (h_b, blk, im), (o_f, blk, im), (o_b, blk, im),
              (p, blk, lambda i, j: (i, 1)), (p, blk, lambda i, j: (i, 5)), (gn, (1, GDN_DK), _row0)]
    outs = [(jax.ShapeDtypeStruct((S, RG_W), F32), blk, im), (jax.ShapeDtypeStruct((S, RG_W), BF16), blk, im),
            (jax.ShapeDtypeStruct((S, RG_W), F32), blk, im), (jax.ShapeDtypeStruct((S, RG_W), BF16), blk, im),
            (jax.ShapeDtypeStruct((1, GDN_DK), F32), (1, GDN_DK), _row0)]
    return _fused_mm("mix_out_bwd", S, D_MODEL, D_MODEL, ts, D_MODEL, D_MODEL, [(dx2, "mk"), (w_out, "nk")], [(0, 1, 0)],
                     extras, outs, epi)


def _rg_scan_adj(name, a_up, b_up, a_dn, b_dn):
    S, C = a_up.shape
    ts = _tile(S, 512)
    n_tiles = S // ts

    def body(au, bu, ad, bd, mu_ref, lam_ref, carry):
        @pl.when(pl.program_id(0) == 0)
        def _():
            carry[...] = jnp.zeros_like(carry)

        n_sub = ts // SUBLANES
        rows = lax.broadcasted_iota(jnp.int32, (SUBLANES, C), 0)

        def half(a_ref, b_ref, out_ref, r0, c_in, reverse):
            a = a_ref[pl.ds(r0, SUBLANES), :]
            b = b_ref[pl.ds(r0, SUBLANES), :]
            cum_a, c0 = _scan_rows(a, a * b, reverse)
            c = c0 + cum_a * c_in
            edge = 0 if not reverse else SUBLANES - 1
            c_prev = jnp.where(rows == edge, c_in, pltpu.roll(c, SUBLANES - 1 if reverse else 1, 0))
            out_ref[pl.ds(r0, SUBLANES), :] = b + c_prev
            return c[0:1, :] if reverse else c[SUBLANES - 1:SUBLANES, :]

        def step(j, c):
            cu, cd = c
            cu = half(au, bu, mu_ref, pl.multiple_of(j * SUBLANES, SUBLANES), cu, False)
            cd = half(ad, bd, lam_ref, pl.multiple_of((n_sub - 1 - j) * SUBLANES, SUBLANES), cd, True)
            return cu, cd

        cu, cd = lax.fori_loop(0, n_sub, step, (carry[0:1, :], carry[1:2, :]), unroll=4)
        carry[0:1, :] = cu
        carry[1:2, :] = cd

    fw = lambda i: (i, 0)
    bw = lambda i: (n_tiles - 1 - i, 0)
    sds = jax.ShapeDtypeStruct((S, C), F32)
    return _rows(name, S, ts,
                 [(a_up, (ts, C), fw), (b_up, (ts, C), fw), (a_dn, (ts, C), bw), (b_dn, (ts, C), bw)],
                 [(sds, (ts, C), fw), (sds, (ts, C), bw)], body, scratch=[pltpu.VMEM((8, C), F32)])


def _halo_ex(arr, S, tm, width):
    per = tm // HALO
    last = S // HALO - 1
    return [
        (arr, (tm, width), lambda i, j: (i, 0)),
        (arr, (HALO, width), lambda i, j: (jnp.maximum(i * per - 1, 0), 0)),
        (arr, (HALO, width), lambda i, j: (jnp.minimum((i + 1) * per, last), 0)),
    ]


def _rg_gates_bwd(xc, bd, prm, lam_f, lam_b, h_f, h_b):
    S = xc.shape[0]
    tm = _tile(S, 256)
    n_tiles = S // tm

    def epi(i, accs, ex, out):
        pre = accs[0][...]
        xv = ex[0][...]
        prm_ref = ex[1]
        lams = (ex[2][...], ex[3][...])
        hprev = (_shift(_ext(ex[4], ex[5], ex[6], i, n_tiles), -1, tm),
                 _shift(_ext(ex[7], ex[8], ex[9], i, n_tiles), 1, tm))
        dxc = jnp.zeros_like(xv)
        rows = []
        dlam_rows = []
        for d in range(2):
            r, ig, sp, a, sq = _rg_gate_terms(pre, xv, prm_ref, d)
            lam = lams[d]
            da = lam * hprev[d]
            di = lam * sq * xv
            dxc = dxc + lam * sq * ig
            dsq = lam * ig * xv
            dlog_a = da * a - dsq * (a * a) / sq
            dpre_r = dlog_a * (-RG_C * sp) * r * (1.0 - r)
            dpre_i = di * ig * (1.0 - ig)
            out[0][:, d * 1024:d * 1024 + RG_W] = dpre_r.astype(BF16)
            out[0][:, d * 1024 + RG_W:(d + 1) * 1024] = dpre_i.astype(BF16)
            rows += [jnp.sum(dpre_r, axis=0, keepdims=True), jnp.sum(dpre_i, axis=0, keepdims=True)]
            dsp = jnp.sum(dlog_a * (-RG_C * r), axis=0, keepdims=True)
            dlam_rows.append(-dsp * _sigmoid(-prm_ref[4 + d:5 + d, :]))
        out[1][...] = dxc + _dot(out[0][...], ex[10][...], 1, 1)
        zero = jnp.zeros((2, RG_W), F32)
        _colsum_into(out[2], i, jnp.concatenate(rows + dlam_rows + [zero], axis=0))

    blk = (tm, RG_W)
    im = lambda i, j: (i, 0)
    extras = ([(xc, blk, im), (prm, (8, RG_W), _row0), (lam_f, blk, im), (lam_b, blk, im)]
              + _halo_ex(h_f, S, tm, RG_W) + _halo_ex(h_b, S, tm, RG_W) + [(bd, (RG_W, 4 * RG_W), _row0)])
    outs = [(jax.ShapeDtypeStruct((S, 4 * RG_W), BF16), (tm, 4 * RG_W), im),
            (jax.ShapeDtypeStruct((S, RG_W), F32), blk, im),
            (jax.ShapeDtypeStruct((8, RG_W), F32), (8, RG_W), _row0)]
    return _fused_mm("rg_gates_bwd", S, 4 * RG_W, RG_W, tm, 4 * RG_W, RG_W, [(xc, "mk"), (bd, "kn")], [(0, 1, 0)],
                     extras, outs, epi)


def _roll_rows(ext, off):
    if off == 0:
        return ext
    return pltpu.roll(ext, (-off) % ext.shape[0], 0)


def _conv_bwd(name, p, colblk, w, grads, mode):
    S = p.shape[0]
    ts = _tile(S, 512)
    n_tiles = S // ts
    C = w.shape[1]
    ng = len(grads)

    def body(*refs):
        p_refs = refs[0:3]
        g_refs = refs[3:3 + 3 * ng]
        w_ref = refs[3 + 3 * ng]
        dx_ref, dw_ref, db_ref = refs[4 + 3 * ng:]
        i = pl.program_id(0)
        ext_p = _ext(*p_refs, i, n_tiles)
        dn = _ext(*g_refs[0:3], i, n_tiles)
        for gi in range(1, ng):
            dn = dn + _ext(*g_refs[3 * gi:3 * gi + 3], i, n_tiles)
        if mode == "bias":
            dc = dn
        else:
            c = None
            for j in range(CONV_W):
                term = w_ref[j:j + 1, :] * _roll_rows(ext_p, j - 2)
                c = term if c is None else c + term
            sig = _sigmoid(c)
            s = c * sig
            if mode in ("q", "k"):
                scale = GDN_DK ** -0.5 if mode == "q" else 1.0
                parts = []
                for h in range(GDN_H):
                    cols = slice(h * GDN_DK, (h + 1) * GDN_DK)
                    sh = s[:, cols]
                    dnh = dn[:, cols]
                    rinv = lax.rsqrt(jnp.sum(sh * sh, axis=-1, keepdims=True) + EPS)
                    parts.append(scale * rinv * (dnh - sh * (rinv * rinv) * jnp.sum(dnh * sh, axis=-1, keepdims=True)))
                ds = jnp.concatenate(parts, axis=-1)
            else:
                ds = dn
            dc = ds * (sig * (1.0 + c * (1.0 - sig)))
        dx = None
        for j in range(CONV_W):
            term = w_ref[j:j + 1, :] * _shift(dc, 2 - j, ts)
            dx = term if dx is None else dx + term
        dx_ref[...] = dx.astype(BF16)
        dc_main = dc[HALO:HALO + ts]
        dw = jnp.concatenate([jnp.sum(dc_main * _shift(ext_p, j - 2, ts), axis=0, keepdims=True)
                              for j in range(CONV_W)], axis=0)
        _colsum_into(dw_ref, i, dw)
        _colsum_into(db_ref, i, jnp.sum(dc_main, axis=0, keepdims=True))

    ins = _halo_ins(p, S, ts, C, colblk)
    for garr in grads:
        ins += _halo_ins(garr, S, ts, C, 0)
    ins += [(w, (CONV_W, C), lambda i: (0, 0))]
    z0 = lambda i: (0, 0)
    outs = [(jax.ShapeDtypeStruct((S, C), BF16), (ts, C), lambda i: (i, 0)),
            (jax.ShapeDtypeStruct((CONV_W, C), F32), (CONV_W, C), z0),
            (jax.ShapeDtypeStruct((1, C), F32), (1, C), z0)]
    return _rows(name, S, ts, ins, outs, body)


def _gdn_scan_bwd(loc, do):
    S = do.shape[0]
    ts = _tile(S, GDN_TS)
    n_tiles = S // ts
    ncb = ts // CHUNK
    nch = S // CHUNK

    def body(*refs):
        ins = (refs[0:6], refs[6:12])
        outs = (refs[12:14], refs[14:16])
        dstate = refs[16]

        @pl.when(pl.program_id(0) == 0)
        def _():
            dstate[...] = jnp.zeros_like(dstate)

        def chunk(cc, carry):
            chains = []
            for d in range(2):
                c = ncb - 1 - cc if d == 0 else cc
                rows = pl.ds(pl.multiple_of(c * CHUNK, CHUNK), CHUNK)
                for h in range(GDN_H):
                    cols = slice(h * GDN_DK, (h + 1) * GDN_DK)
                    chains.append(dict(d=d, h=h, c=c, rows=rows, cols=cols, dsn=dstate[d * GDN_H + h]))
            for ch in chains:
                qd_ref, kd_ref, cd_ref, w_ref, a_ref, do_ref = ins[ch["d"]]
                rows, cols = ch["rows"], ch["cols"]
                dob = do_ref[rows, cols].astype(BF16)
                ch["dvn"] = (_dot(a_ref[ch["c"], ch["h"]], dob, 0, 0)
                             + _dot(kd_ref[rows, cols], ch["dsn"].astype(BF16), 1, 0))
                ch["qdo"] = _dot(qd_ref[rows, cols], dob, 0, 0)
            for ch in chains:
                w_ref = ins[ch["d"]][3]
                ch["wdvn"] = _dot(w_ref[ch["rows"], ch["cols"]], ch["dvn"].astype(BF16), 0, 0)
            for ch in chains:
                dvn_ref, ds_ref = outs[ch["d"]]
                cd_ref = ins[ch["d"]][2]
                col = ch["d"] * GDN_H + ch["h"]
                dvn_ref[ch["rows"], ch["cols"]] = ch["dvn"].astype(BF16)
                ds_ref[ch["c"], ch["h"]] = ch["dsn"].astype(BF16)
                dstate[ch["d"] * GDN_H + ch["h"]] = (ch["qdo"] + cd_ref[ch["c"], col:col + 1, :] * ch["dsn"]
                                                     - ch["wdvn"])
            return carry

        lax.fori_loop(0, ncb, chunk, 0)

    ins, outs = [], []
    for d in range(2):
        tix = _dir_tile(d, n_tiles, True)
        im = lambda i, tix=tix: (tix(i), 0)
        im4 = lambda i, tix=tix: (tix(i), 0, 0, 0)
        _, w, a, _, qd, kd = loc[d]
        ins += [(qd, (ts, GDN_W), im), (kd, (ts, GDN_W), im), (loc[2], (ncb, 8, 128), lambda i, tix=tix: (tix(i), 0, 0)),
                (w, (ts, GDN_W), im), (a, (ncb, GDN_H, CHUNK, CHUNK), im4), (do, (ts, GDN_W), im)]
        outs += [(jax.ShapeDtypeStruct((S, GDN_W), BF16), (ts, GDN_W), im),
                 (jax.ShapeDtypeStruct((nch, GDN_H, GDN_DK, GDN_DK), BF16), (ncb, GDN_H, GDN_DK, GDN_DK), im4)]
    res = _rows("gdn_scan_bwd", S, ts, ins, outs, body, scratch=[pltpu.VMEM((2 * GDN_H, GDN_DK, GDN_DK), F32)])
    return res[0:2], res[2:4]


def _gdn_local_bwd(q, k, v, bg, gcr, do, loc, fwd, adj):
    S = q.shape[0]
    ts = _tile(S, GDN_TS)
    ncb = ts // CHUNK

    def body(q_ref, k_ref, v_ref, bg_ref, gcr_ref, do_ref, *rest):
        per_dir = (rest[0:5], rest[5:10])
        dq_ref, dk_ref, dv_ref, dbg_ref, dbgr_ref = rest[10:15]
        ri, ci = _tri_masks()
        lane = lax.broadcasted_iota(jnp.int32, (CHUNK, 128), 1)
        rowi = lax.broadcasted_iota(jnp.int32, (CHUNK, 1), 0)
        ones8 = jnp.ones((SUBLANES, CHUNK), F32)

        def chunk(c, carry):
            r0 = pl.multiple_of(c * CHUNK, CHUNK)
            rows = pl.ds(r0, CHUNK)
            chains = []
            for h in range(GDN_H):
                cols = slice(h * GDN_DK, (h + 1) * GDN_DK)
                qh, kh, vh = q_ref[rows, cols], k_ref[rows, cols], v_ref[rows, cols]
                dob = do_ref[rows, cols].astype(BF16)
                both = _bdot(jnp.concatenate([qh, kh], axis=0), kh, 1, 1)
                for d in range(2):
                    chains.append(dict(h=h, d=d, cols=cols, qh=qh, kh=kh, vh=vh, dob=dob, qk=both[0:CHUNK],
                                       kk=both[CHUNK:2 * CHUNK], col=d * GDN_H + h))
            for ch in chains:
                m = _gdn_decay(bg_ref, gcr_ref, c, rows, r0, ch["col"], ch["d"] == 1, ri, ci)
                t_ref, s_ref, ds_ref, vn_ref, dvn_ref = per_dir[ch["d"]]
                h, cols = ch["h"], ch["cols"]
                ch["m"] = m
                ch["kb"] = ch["kh"] * m["beta"]
                ch["kbg"] = ch["kb"] * m["eg"]
                ch["t"] = t_ref[c, h]
                stb = s_ref[c, h]
                ch["dsn"] = ds_ref[c, h]
                vnb = vn_ref[rows, cols]
                dvnb = dvn_ref[rows, cols]
                ch["dcd"] = jnp.sum(jnp.sum(stb.astype(F32) * ch["dsn"].astype(F32), axis=1, keepdims=True),
                                    axis=0, keepdims=True)
                ch["dqd"] = _dot(ch["dob"], stb, 1, 1)
                ch["d_a"] = _dot(ch["dob"], vnb, 1, 1)
                ch["dkd"] = _bdot(vnb, ch["dsn"], 1, 1)
                ch["dw"] = -_dot(dvnb, stb, 1, 1)
                ch["dvb"] = _dot(ch["t"], dvnb, 1, 0)
                ch["d_t"] = _bdot(dvnb, ch["vh"] * m["beta"], 1, 1)
            for ch in chains:
                dwb = ch["dw"].astype(BF16)
                ch["d_t"] = ch["d_t"] + _bdot(dwb, ch["kbg"], 1, 1)
                ch["dkbg"] = _dot(ch["t"], dwb, 1, 0)
                ch["nn"] = ch["d_a"] * ch["m"]["dm"]
                ch["nn_q"] = _bdot(ch["nn"], ch["qh"], 0, 0)
                ch["nn_k"] = _bdot(ch["nn"], ch["kh"], 1, 0)
            for ch in chains:
                ch["x"] = _dot(ch["d_t"].astype(BF16), ch["t"], 1, 0)
            for ch in chains:
                d_l = -_dot(ch["t"], ch["x"].astype(BF16), 1, 0)
                ch["d_l"] = jnp.where(ch["m"]["strict"], d_l, 0.0)
                ch["mm"] = ch["d_l"] * ch["m"]["dm"]
            for ch in chains:
                m = ch["m"]
                ch["mm_kh"] = _bdot(ch["mm"], ch["kh"], 1, 0)
                ch["mm_kb"] = _bdot(ch["mm"], ch["kb"], 0, 0)
                l_mat = jnp.where(m["strict"], m["beta"] * ch["kk"] * m["dm"], 0.0)
                ch["e"] = ch["d_l"] * l_mat + ch["nn"] * ch["qk"]
                dbgr_ref[c, ch["col"]:ch["col"] + 1, :] = -_dot(ones8, ch["e"], 1, 0, HI)[0:1, :]
            acc_bg = jnp.zeros((CHUNK, 128), F32)
            acc = {}
            for ch in chains:
                m = ch["m"]
                beta, eg, egl = m["beta"], m["eg"], m["egl"]
                dkb = ch["mm_kh"] + ch["dkbg"] * eg
                dk_d = ch["mm_kb"] + ch["nn_q"] + ch["dkd"] * egl + dkb * beta
                dq_d = ch["nn_k"] + ch["dqd"] * eg
                dv_d = ch["dvb"] * beta
                dkd_kd = ch["dkd"] * (ch["kh"] * egl)
                dgc = (jnp.sum(ch["e"], axis=1, keepdims=True)
                       + jnp.sum(ch["dqd"] * (ch["qh"] * eg) - dkd_kd + ch["dkbg"] * ch["kbg"], axis=1, keepdims=True))
                dgl = jnp.sum(jnp.sum(dkd_kd, axis=1, keepdims=True), axis=0, keepdims=True) + ch["dcd"] * m["cd"]
                dgc = dgc + jnp.where(rowi == (0 if ch["d"] == 1 else CHUNK - 1), dgl, 0.0)
                dbeta = jnp.sum(dkb * ch["kh"] + ch["dvb"] * ch["vh"], axis=1, keepdims=True)
                acc_bg = acc_bg + jnp.where(lane == ch["col"], dbeta, 0.0) + jnp.where(lane == 8 + ch["col"], dgc, 0.0)
                if ch["d"] == 0:
                    acc[ch["h"]] = (dq_d, dk_d, dv_d)
                else:
                    dq0, dk0, dv0 = acc[ch["h"]]
                    dq_ref[rows, ch["cols"]] = dq0 + dq_d
                    dk_ref[rows, ch["cols"]] = dk0 + dk_d
                    dv_ref[rows, ch["cols"]] = dv0 + dv_d
            dbg_ref[rows, :] = acc_bg
            return carry

        lax.fori_loop(0, ncb, chunk, 0)

    im = lambda i: (i, 0)
    im4 = lambda i: (i, 0, 0, 0)
    blk = (ts, GDN_W)
    ins = [(q, blk, im), (k, blk, im), (v, blk, im), (bg, (ts, 128), im), (gcr, (ncb, 8, CHUNK), lambda i: (i, 0, 0)),
           (do, blk, im)]
    for d in range(2):
        ins += [(loc[d][3], (ncb, GDN_H, CHUNK, CHUNK), im4), (fwd[d][2], (ncb, GDN_H, GDN_DK, GDN_DK), im4),
                (adj[d][1], (ncb, GDN_H, GDN_DK, GDN_DK), im4), (fwd[d][1], blk, im), (adj[d][0], blk, im)]
    sds = jax.ShapeDtypeStruct((S, GDN_W), F32)
    outs = [(sds, blk, im), (sds, blk, im), (sds, blk, im), (jax.ShapeDtypeStruct((S, 128), F32), (ts, 128), im),
            (jax.ShapeDtypeStruct((S // CHUNK, 8, CHUNK), F32), (ncb, 8, CHUNK), lambda i: (i, 0, 0))]
    dq, dk, dv, dbg, dbg_rows = _rows("gdn_local_bwd", S, ts, ins, outs, body)
    dgc_cols = dbg_rows.transpose(0, 2, 1).reshape(S, 8)
    return dq, dk, dv, dbg + jnp.pad(dgc_cols, ((0, 0), (8, 112)))


def _gdn_prep_bwd(dbg_all, p, prm):
    S = p.shape[0]
    ts = _tile(S, 512)

    def body(dbg_ref, p_ref, prm_ref, dba_ref, dprm_ref):
        i = pl.program_id(0)
        raw = p_ref[...].astype(F32)
        dbg = dbg_ref[...]
        lane = lax.broadcasted_iota(jnp.int32, (1, 128), 1)
        is_g = (lane >= 8) & (lane < 16)
        ea = jnp.exp(prm_ref[0:1, :])
        arg = raw + prm_ref[1:2, :]
        g = jnp.where(is_g, -ea * _softplus(arg), 0.0)
        beta = _sigmoid(raw)
        dgc = jnp.where(is_g, dbg, 0.0)
        ri, ci = _tri_masks()
        lower = (ri >= ci).astype(F32)
        upper = (ri <= ci).astype(F32)
        dgs = []
        for c in range(ts // CHUNK):
            ch = dgc[c * CHUNK:(c + 1) * CHUNK]
            dgs.append(jnp.where(lane < 12, _dot(upper, ch, 1, 0, HI), _dot(lower, ch, 1, 0, HI)))
        dg = jnp.concatenate(dgs, axis=0)
        dalpha = jnp.where(is_g, dg * (-ea) * _sigmoid(arg), 0.0)
        dba_ref[...] = jnp.where(lane < 8, dbg * beta * (1.0 - beta), dalpha).astype(BF16)
        rows = jnp.concatenate([jnp.sum(dg * g, axis=0, keepdims=True), jnp.sum(dalpha, axis=0, keepdims=True),
                                jnp.zeros((6, 128), F32)], axis=0)
        _colsum_into(dprm_ref, i, rows)

    im = lambda i: (i, 0)
    z0 = lambda i: (0, 0)
    return _rows("gdn_prep_bwd", S, ts,
                 [(dbg_all, (ts, 128), im), (p, (ts, 128), lambda i: (i, COL_BA // 128)), (prm, (8, 128), z0)],
                 [(jax.ShapeDtypeStruct((S, 128), BF16), (ts, 128), im), (jax.ShapeDtypeStruct((8, 128), F32), (8, 128), z0)],
                 body)


def _mm_plain(name, M, N, K, tm, tn, tk, a, am, b, bm, dtype):
    return _fused_mm(name, M, N, K, tm, tn, tk, [(a, am), (b, bm)], [(0, 1, 0)], [],
                     [(jax.ShapeDtypeStruct((M, N), dtype), (tm, tn), _mn)],
                     lambda i, accs, ex, out: out[0].__setitem__(Ellipsis, accs[0][...].astype(dtype)))[0]


def _layer_bwd(x0, W, R, emit_big=None, emit_small=None):
    S = x0.shape[0]
    tm = _tile(S, 512)
    tk_s = _tile(S, 1024)
    G = {}

    def emit(**named):
        if emit_big is None:
            G.update(named)
            return None
        return emit_big(**named)

    def ffn_emit(prefix):
        return lambda **kw: emit(**{f"{prefix}_w_{k}": v for k, v in kw.items()})

    dx2, G["ffn2_norm"] = _ffn_bwd("ffn2b", R["dx3"], R["x2"], W["ffn2_norm"], R["h3"], R["a2"], R["b2"], R["f2"],
                                   W["ffn2_w_gate"], W["ffn2_w_up"], W["ffn2_w_down"], ffn_emit("ffn2"))
    tok = emit(w_out=_mm_plain("dw_out", D_MODEL, D_MODEL, S, D_MODEL, D_MODEL, tk_s, R["y"], "km", dx2, "kn", BF16))
    gn = W["gdn_norm"] if tok is None else W["gdn_norm"] + tok
    p = R["p"]
    dhr, dgate, do, dz, G["gdn_norm"] = _mix_out_bwd(dx2, W["w_out"], R["h_f"], R["h_b"], R["o_f"], R["o_b"], p, gn)
    lam_b, lam_f = _rg_scan_adj("rg_scan_bwd", R["a_b"], dhr, R["a_f"], dhr)
    dpre, dxc, d_rgprm = _rg_gates_bwd(R["xc"], R["bd"], R["rg_prm"], lam_f, lam_b, R["h_f"], R["h_b"])
    d_bd = _mm_plain("rg_dbd", RG_W, 4 * RG_W, S, RG_W, 4 * RG_W, tk_s, R["xc"], "km", dpre, "kn", F32)
    dx_rg, G["rg_conv_w"], G["rg_conv_b"] = _conv_bwd("rg_conv_bwd", p, 0, W["rg_conv_w"], [dxc], "bias")
    blocks = jnp.einsum("nigmj,nm->gnij", d_bd.reshape(RG_BLOCKS, RG_BLOCK, 4, RG_BLOCKS, RG_BLOCK),
                        jnp.eye(RG_BLOCKS, dtype=F32))
    G["rg_gate_a_w"] = jnp.stack([blocks[0], blocks[2]])
    G["rg_gate_x_w"] = jnp.stack([blocks[1], blocks[3]])
    G["rg_gate_a_b"] = jnp.stack([d_rgprm[0], d_rgprm[2]])
    G["rg_gate_x_b"] = jnp.stack([d_rgprm[1], d_rgprm[3]])
    G["rg_lambda"] = d_rgprm[4:6]
    adj = _gdn_scan_bwd(R["gdn_loc"], do)
    dq, dk, dv, dbg = _gdn_local_bwd(R["q"], R["k"], R["v"], R["bg"], R["gcr"], do, R["gdn_loc"], R["gdn_fwd"], adj)
    cw = W["gdn_conv_w"]
    dpq, dwq, _ = _conv_bwd("gdn_conv_q_bwd", p, 2, cw[:, 0:512], [dq], "q")
    dpk, dwk, _ = _conv_bwd("gdn_conv_k_bwd", p, 3, cw[:, 512:1024], [dk], "k")
    dpv, dwv, _ = _conv_bwd("gdn_conv_v_bwd", p, 4, cw[:, 1024:1536], [dv], "v")
    G["gdn_conv_w"] = jnp.concatenate([dwq, dwk, dwv], axis=1)
    dba, d_gprm = _gdn_prep_bwd(dbg, p, R["gdn_prm"])
    G["gdn_a_log"] = d_gprm[0, 8:16].reshape(2, GDN_H)
    G["gdn_dt_bias"] = d_gprm[1, 8:16].reshape(2, GDN_H)
    dp = jnp.concatenate([dx_rg, dgate, dpq, dpk, dpv, dz, dba], axis=1)
    tok = emit(w_in=_mm_plain("dw_in", D_MODEL, D_IN_PAD, S, D_MODEL, 640, tk_s, R["h2"], "km", dp, "kn", BF16))
    g_mix = W["mix_norm"] if tok is None else W["mix_norm"] + tok

    def epi_dx1(i, accs, ex, out):
        dx, dgt = _rmsnorm_bwd_tile(accs[0][...], ex[0][...], ex[1][...])
        out[0][...] = ex[2][...] + dx
        _colsum_into(out[1], i, jnp.sum(dgt, axis=0, keepdims=True))

    dx1, G["mix_norm"] = _fused_mm(
        "mix_dx", S, D_MODEL, D_IN_PAD, tm, D_MODEL, D_IN_PAD, [(dp, "mk"), (W["w_in"], "kn")], [(0, 1, 0)],
        [(R["x1"], (tm, D_MODEL), _mn), (g_mix, (1, D_MODEL), _row0), (dx2, (tm, D_MODEL), _mn)],
        [(jax.ShapeDtypeStruct((S, D_MODEL), F32), (tm, D_MODEL), _mn),
         (jax.ShapeDtypeStruct((1, D_MODEL), F32), (1, D_MODEL), _row0)], epi_dx1)
    G["final_norm"] = R["d_final_norm"]
    if emit_small is not None:
        emit_small(G)
    dx0, G["ffn1_norm"] = _ffn_bwd("ffn1b", dx1, x0, W["ffn1_norm"], R["h1"], R["a1"], R["b1"], R["f1"],
                                   W["ffn1_w_gate"], W["ffn1_w_up"], W["ffn1_w_down"], ffn_emit("ffn1"))
    return dx0, G


def _mesh_pos():
    x, y, c = lax.axis_index("x"), lax.axis_index("y"), lax.axis_index("c")
    return x, y, c, 4 * x + 2 * y + c


def _peer(x, y, c, r):
    px = 1 - x if r & 4 else x
    py = 1 - y if r & 2 else y
    pc = 1 - c if r & 1 else c
    return (px, py, pc), 4 * px + 2 * py + pc


_HBM = pl.BlockSpec(memory_space=pltpu.HBM)
_SEM = pl.BlockSpec(memory_space=pltpu.SEMAPHORE)


def _peer_copies(scatter, srcs, lands, send_sems, recv_sems):
    x, y, c, me = _mesh_pos()
    copies = []
    for a, (src, land) in enumerate(zip(srcs, lands)):
        for r in range(1, N_DEV):
            peer, peer_idx = _peer(x, y, c, r)
            copies.append(pltpu.make_async_remote_copy(
                src_ref=src.at[peer_idx] if scatter else src, dst_ref=land.at[r - 1] if scatter else land.at[me],
                send_sem=send_sems.at[a * 7 + r - 1], recv_sem=recv_sems.at[a * 7 + r - 1],
                device_id=peer, device_id_type=pl.DeviceIdType.MESH))
    return copies


def _exchange_start(name, scatter, arrays):
    slabs = arrays
    n = len(slabs)

    def body(*refs):
        srcs, lands = refs[0:n], refs[n:2 * n]
        send_sems, recv_sems = refs[2 * n], refs[2 * n + 1]
        token = refs[4 * n + 2]
        for cp in _peer_copies(scatter, srcs, lands, send_sems, recv_sems):
            cp.start()
        token[...] = jnp.zeros_like(token)

    land_shapes = [(N_DEV - 1,) + s.shape[1:] if scatter else (N_DEV,) + s.shape for s in slabs]
    n_sems = 7 * n
    out_shape = ([pltpu.SemaphoreType.DMA((n_sems,)), pltpu.SemaphoreType.DMA((n_sems,))]
                 + [pltpu.HBM(s.shape, s.dtype) for s in slabs]
                 + [pltpu.HBM(shp, s.dtype) for shp, s in zip(land_shapes, slabs)]
                 + [jax.ShapeDtypeStruct((8, 128), F32)])
    res = pl.pallas_call(
        body, name=name, out_shape=out_shape, in_specs=[_HBM] * (2 * n),
        out_specs=[_SEM, _SEM] + [_HBM] * (2 * n) + [pl.BlockSpec(memory_space=pltpu.VMEM)],
        input_output_aliases={i: 2 + i for i in range(2 * n)},
        compiler_params=pltpu.CompilerParams(has_side_effects=pltpu.SideEffectType.DATAFLOW_SIDE_EFFECTING),
    )(*[pltpu.with_memory_space_constraint(s, pltpu.HBM) for s in slabs],
      *[pltpu.with_memory_space_constraint(lax.empty(shp, s.dtype), pltpu.HBM) for shp, s in zip(land_shapes, slabs)])
    return dict(n=n, scatter=scatter, sems=res[0:2], srcs=res[2:2 + n], lands=res[2 + n:2 + 2 * n],
                token=res[2 + 2 * n][0, 0])


def _exchange_wait(name, started, after):
    n = started["n"]
    scatter = started["scatter"]

    def body(*refs):
        srcs, lands = refs[0:n], refs[n:2 * n]
        send_sems, recv_sems = refs[2 * n], refs[2 * n + 1]
        for cp in _peer_copies(scatter, srcs, lands, send_sems, recv_sems):
            cp.wait_send()
            cp.wait_recv()

    arrays = list(started["srcs"]) + list(started["lands"])
    res = pl.pallas_call(
        body, name=name, out_shape=[pltpu.HBM(a.shape, a.dtype) for a in arrays],
        in_specs=[_HBM] * (2 * n) + [_SEM, _SEM, pl.BlockSpec(memory_space=pl.ANY)], out_specs=[_HBM] * (2 * n),
        input_output_aliases={i: i for i in range(2 * n)},
        compiler_params=pltpu.CompilerParams(has_side_effects=pltpu.SideEffectType.DATAFLOW_SIDE_EFFECTING),
    )(*arrays, *started["sems"], after)
    return res[0:n], res[n:2 * n]


def _adamw_math(w, g, m, v):
    m2 = ADAM_B1 * m + (1.0 - ADAM_B1) * g
    v2 = ADAM_B2 * v + (1.0 - ADAM_B2) * (g * g)
    m_hat = m2 / (1.0 - ADAM_B1 ** ADAM_STEP)
    v_hat = v2 / (1.0 - ADAM_B2 ** ADAM_STEP)
    delta = -ADAM_LR * (m_hat / (jnp.sqrt(v_hat) + ADAM_EPS) + ADAM_WD * w)
    return delta, m2, v2


def _adamw_slabs(name, src, land, me, w, m, v, tr):
    R, C = w.shape

    def body(me_ref, own_ref, land_ref, w_ref, m_ref, v_ref, g_ref, d_ref, m2_ref, v2_ref):
        g = own_ref[0].astype(F32)
        for s in range(N_DEV - 1):
            g = g + land_ref[s].astype(F32)
        delta, m2, v2 = _adamw_math(w_ref[...], g, m_ref[...], v_ref[...])
        g_ref[...] = g
        d_ref[...] = delta
        m2_ref[...] = m2
        v2_ref[...] = v2

    im = lambda i, me_ref: (i, 0)
    grid_spec = pltpu.PrefetchScalarGridSpec(
        num_scalar_prefetch=1, grid=(R // tr,),
        in_specs=[pl.BlockSpec((1, tr, C), lambda i, me_ref: (me_ref[0], i, 0)),
                  pl.BlockSpec((N_DEV - 1, tr, C), lambda i, me_ref: (0, i, 0)),
                  pl.BlockSpec((tr, C), im), pl.BlockSpec((tr, C), im), pl.BlockSpec((tr, C), im)],
        out_specs=[pl.BlockSpec((tr, C), im)] * 4)
    return pl.pallas_call(body, name=name, grid_spec=grid_spec, out_shape=[jax.ShapeDtypeStruct((R, C), F32)] * 4,
                          compiler_params=_cp(1))(me.reshape(1).astype(jnp.int32), src, land, w, m, v)


def _sum_slots(name, slots):
    _, R, C = slots.shape

    def body(s_ref, o_ref):
        g = s_ref[0]
        for s in range(1, N_DEV):
            g = g + s_ref[s]
        o_ref[...] = g

    return _rows(name, R, R, [(slots, (N_DEV, R, C), lambda i: (0, 0, 0))],
                 [(jax.ShapeDtypeStruct((R, C), F32), (R, C), lambda i: (0, 0))], body)[0]


def _adamw_packed(name, g, w, m, v):
    R, C = g.shape

    def body(g_ref, w_ref, m_ref, v_ref, d_ref, m2_ref, v2_ref):
        delta, m2, v2 = _adamw_math(w_ref[...], g_ref[...], m_ref[...], v_ref[...])
        d_ref[...] = delta
        m2_ref[...] = m2
        v2_ref[...] = v2

    im = lambda i: (0, 0)
    sds = jax.ShapeDtypeStruct((R, C), F32)
    return _rows(name, R, R, [(a, (R, C), im) for a in (g, w, m, v)], [(sds, (R, C), im)] * 3, body)


def _pack(arrays):
    rows = []
    for a in arrays:
        flat = a.reshape(-1).astype(F32)
        pad = (-flat.shape[0]) % 128
        rows.append(jnp.pad(flat, (0, pad)).reshape(-1, 128))
    out = jnp.concatenate(rows, axis=0)
    return jnp.pad(out, ((0, (-out.shape[0]) % 8), (0, 0)))


def _unpack(packed, shapes):
    lead = packed.shape[:-2]
    outs = []
    r = 0
    for shp in shapes:
        n = math.prod(shp)
        nr = -(-n // 128)
        flat = packed[..., r:r + nr, :].reshape(lead + (nr * 128,))[..., :n]
        outs.append(flat.reshape(lead + tuple(shp)))
        r += nr
    return outs


FFN1_BIG = ["ffn1_w_gate", "ffn1_w_up", "ffn1_w_down"]
MIX_BIG = ["w_in", "w_out"]
FFN2_BIG = ["ffn2_w_gate", "ffn2_w_up", "ffn2_w_down"]
BIG = FFN1_BIG + MIX_BIG + FFN2_BIG
COL_SHARDED = {"ffn1_w_gate", "ffn1_w_up", "w_in", "ffn2_w_gate", "ffn2_w_up"}
SMALL_SHARDED = ["rg_conv_w", "rg_gate_a_b", "rg_gate_x_b", "rg_lambda", "gdn_conv_w"]
WEIGHTS = ["ffn1_norm", "ffn1_w_gate", "ffn1_w_up", "ffn1_w_down", "mix_norm", "w_in", "w_out", "rg_conv_w", "rg_conv_b",
           "rg_gate_a_w", "rg_gate_a_b", "rg_gate_x_w", "rg_gate_x_b", "rg_lambda", "gdn_conv_w", "gdn_a_log",
           "gdn_dt_bias", "gdn_norm", "ffn2_norm", "ffn2_w_gate", "ffn2_w_up", "ffn2_w_down", "final_norm"]
SMALL = [n for n in WEIGHTS if n not in BIG]
ROW_VECTORS = {"ffn1_norm", "mix_norm", "ffn2_norm", "gdn_norm", "rg_conv_b", "final_norm"}
ROW_TILE = {"ffn1_w_gate": 256, "ffn1_w_up": 256, "ffn1_w_down": 176, "w_in": 256, "w_out": 64,
            "ffn2_w_gate": 256, "ffn2_w_up": 256, "ffn2_w_down": 176}


def _to_slabs(name, g):
    if name in COL_SHARDED:
        r, ctot = g.shape
        return g.reshape(r, N_DEV, ctot // N_DEV).transpose(1, 0, 2)
    return g.reshape(N_DEV, g.shape[0] // N_DEV, g.shape[1])


def _step(x, target, w, m, v):
    _, _, _, me = _mesh_pos()
    def shard_to_send(n, tok=None):
        s = w[n] if tok is None else w[n] + tok
        return (s.T if n in COL_SHARDED else s).astype(BF16)

    def unshard(n, gth):
        full = gth.reshape(-1, gth.shape[-1])
        return jnp.pad(full, ((0, D_IN_PAD - D_IN), (0, 0))) if n == "w_in" else full

    def landed(started, name, after):
        srcs, lands = _exchange_wait(name, started, after)
        def with_own(src, land):
            slot = lax.broadcasted_iota(jnp.int32, (N_DEV,) + (1,) * src.ndim, 0)
            return jnp.where(slot == me, src[None], land)

        return [with_own(src, land) for src, land in zip(srcs, lands)]

    up_names = ["ffn1_w_gate", "ffn1_w_up"]
    small_shards = [w[n] for n in SMALL_SHARDED]
    st_up = _exchange_start("gather_ffn1_up_start", False, [shard_to_send(n) for n in up_names])
    tok = st_up["token"]
    st_down = _exchange_start("gather_ffn1_down_start", False, [shard_to_send("ffn1_w_down", tok)])
    tok = tok + st_down["token"]
    st_mix = _exchange_start("gather_mix_start", False,
                             [shard_to_send(n, tok) for n in MIX_BIG] + [_pack(small_shards) + tok])
    tok = tok + st_mix["token"]
    st_ffn2 = _exchange_start("gather_ffn2_start", False, [shard_to_send(n, tok) for n in FFN2_BIG])
    W = {n: w[n] for n in SMALL if n not in SMALL_SHARDED}
    W["ffn1_norm"] = w["ffn1_norm"] + (tok + st_ffn2["token"])

    def more(stage, after):
        if stage == "ffn1_up":
            return {n: unshard(n, gth) for n, gth in zip(up_names, landed(st_up, "gather_ffn1_up_wait", after))}
        if stage == "ffn1_down":
            return {"ffn1_w_down": unshard("ffn1_w_down", landed(st_down, "gather_ffn1_down_wait", after)[0])}
        if stage == "ffn2":
            return {n: unshard(n, gth) for n, gth in zip(FFN2_BIG, landed(st_ffn2, "gather_ffn2_wait", after))}
        got = landed(st_mix, "gather_mix_wait", after)
        new = {n: unshard(n, gth) for n, gth in zip(MIX_BIG, got)}
        for n, gth in zip(SMALL_SHARDED, _unpack(got[-1], [s.shape for s in small_shards])):
            new[n] = jnp.moveaxis(gth, 0, -2).reshape(gth.shape[1:-1] + (N_DEV * gth.shape[-1],))
        return new

    R = _layer_fwd(x, target, W, more)
    W = R["W"]
    pending = []

    def emit_big(**named):
        slabs = [_to_slabs(n, g[:, :D_IN] if n == "w_in" else g) for n, g in named.items()]
        started = _exchange_start(f"scatter_start_{len(pending)}", True, slabs)
        pending.append((list(named), started))
        return started["token"]

    small_started = []

    def emit_small(G):
        packed = _pack([G[n] for n in SMALL if n != "ffn1_norm"])
        small_started.append(_exchange_start("gather_small_start", False, [packed]))

    grad_x, G = _layer_bwd(x, W, R, emit_big, emit_small)
    st_late = _exchange_start("gather_ffn1_norm_start", False, [_pack([G["ffn1_norm"]])])
    loss = lax.psum(R["loss"][0, 0], ("x", "y", "c"))
    out = {}

    def finish(i, after):
        names, started = pending[i]
        srcs, lands = _exchange_wait(f"scatter_wait_{i}", started, after)
        for n, src, land in zip(names, srcs, lands):
            out[n] = _adamw_slabs(f"adamw_{n}", src, land, me, w[n], m[n], v[n], ROW_TILE[n])

    n_early = len(pending) - 2
    for i in range(n_early):
        finish(i, grad_x)
    early = [n for n in SMALL if n != "ffn1_norm"]
    srcs, lands = _exchange_wait("gather_small_wait", small_started[0], grad_x)
    slot = lax.broadcasted_iota(jnp.int32, (N_DEV, 1, 1), 0)
    slots = jnp.where(slot == me, srcs[0][None], lands[0])
    reduced = dict(zip(early, _unpack(_sum_slots("sum_small_grads", slots), [G[n].shape for n in early])))

    def adamw_small(name, names):
        g_small = []
        for n in names:
            g = reduced[n]
            if n in SMALL_SHARDED:
                per = g.shape[-1] // N_DEV
                g = lax.dynamic_slice_in_dim(g, me * per, per, axis=g.ndim - 1)
            g_small.append(g.reshape(w[n].shape))
        shapes = [w[n].shape for n in names]
        d_p, m_p, v_p = _adamw_packed(name, _pack(g_small), _pack([w[n] for n in names]),
                                      _pack([m[n] for n in names]), _pack([v[n] for n in names]))
        for n, g, d_, m_, v_ in zip(names, g_small, _unpack(d_p, shapes), _unpack(m_p, shapes), _unpack(v_p, shapes)):
            out[n] = (g, d_, m_, v_)
        return d_p

    done_early = adamw_small("adamw_small", early)
    srcs, lands = _exchange_wait("gather_ffn1_norm_wait", st_late, done_early)
    late = jnp.where(slot == me, srcs[0][None], lands[0])
    reduced["ffn1_norm"] = _unpack(_sum_slots("sum_ffn1_norm_grad", late), [G["ffn1_norm"].shape])[0]
    done = adamw_small("adamw_ffn1_norm", ["ffn1_norm"])
    for i in range(n_early, len(pending)):
        finish(i, done)
    return loss, grad_x, out


def kernel(x, ffn1_norm, ffn1_w_gate, ffn1_w_up, ffn1_w_down, mix_norm, w_in, w_out, rg_conv_w, rg_conv_b, rg_gate_a_w, rg_gate_a_b, rg_gate_x_w, rg_gate_x_b, rg_lambda, gdn_conv_w, gdn_a_log, gdn_dt_bias, gdn_norm, ffn2_norm, ffn2_w_gate, ffn2_w_up, ffn2_w_down, final_norm, loss_target, m_ffn1_norm, m_ffn1_w_gate, m_ffn1_w_up, m_ffn1_w_down, m_mix_norm, m_w_in, m_w_out, m_rg_conv_w, m_rg_conv_b, m_rg_gate_a_w, m_rg_gate_a_b, m_rg_gate_x_w, m_rg_gate_x_b, m_rg_lambda, m_gdn_conv_w, m_gdn_a_log, m_gdn_dt_bias, m_gdn_norm, m_ffn2_norm, m_ffn2_w_gate, m_ffn2_w_up, m_ffn2_w_down, m_final_norm, v_ffn1_norm, v_ffn1_w_gate, v_ffn1_w_up, v_ffn1_w_down, v_mix_norm, v_w_in, v_w_out, v_rg_conv_w, v_rg_conv_b, v_rg_gate_a_w, v_rg_gate_a_b, v_rg_gate_x_w, v_rg_gate_x_b, v_rg_lambda, v_gdn_conv_w, v_gdn_a_log, v_gdn_dt_bias, v_gdn_norm, v_ffn2_norm, v_ffn2_w_gate, v_ffn2_w_up, v_ffn2_w_down, v_final_norm):
    args = dict(locals())
    orig_shapes = {n: args[n].shape for n in WEIGHTS}

    def local(prefix):
        d = {}
        for n in WEIGHTS:
            a = args[prefix + n]
            d[n] = a.reshape(1, -1) if n in ROW_VECTORS else a[0]
        return d

    loss, grad_x, out = _step(x[0], loss_target[0], local(""), local("m_"), local("v_"))
    res = [loss, grad_x[None]]
    for k in range(4):
        res += [out[n][k].reshape(orig_shapes[n]) for n in WEIGHTS]
    return tuple(res)
```
